```python
import math
import jax, jax.numpy as jnp
from jax import lax
import numpy as np

D_MODEL = 1024
BATCH = 16
SEQ = 2048
DEPTH = 1

PLE_DIM = 256
RMS_EPS = 1e-6
N_BRANCH = 2
S5_WIDTH = 512
S5_GROUP = 16
S5_GROUPS = S5_WIDTH // S5_GROUP
S5_STATE = 64
SSD_WIDTH = 1536
SSD_HEADDIM = 64
SSD_HEADS = SSD_WIDTH // SSD_HEADDIM
SSD_GROUPS = 4
SSD_HPG = SSD_HEADS // SSD_GROUPS
SSD_STATE = 128
SSD_CONV = 4
SSD_CHUNK = 128
SSD_BC = SSD_GROUPS * SSD_STATE
SSD_CONV_DIM = SSD_WIDTH + 2 * SSD_BC
DT_MIN, DT_MAX = 1e-3, 1e-1
_SIZES = (S5_WIDTH, S5_WIDTH, SSD_WIDTH, SSD_CONV_DIM, SSD_HEADS, N_BRANCH * D_MODEL)
IN_PROJ_DIM = int(sum(_SIZES))
SPLITS = tuple(int(v) for v in np.cumsum(_SIZES)[:-1])

kernel_name = "hybrid_s5_ssd_gated_block"


def rms_norm(x, w):
    xf = x.astype(jnp.float32)
    y = xf * lax.rsqrt(jnp.mean(xf * xf, axis=-1, keepdims=True) + RMS_EPS)
    return (y * w.astype(jnp.float32)).astype(x.dtype)


def s5_mixer(u, a_re, a_im, b_re, b_im, c_re, c_im, d, log_step, w_glu, b_glu):
    f32 = jnp.float32
    bsz, L, _ = u.shape
    uf = u.astype(f32).reshape(bsz, L, S5_GROUPS, S5_GROUP)
    a_re = a_re.astype(f32); a_im = a_im.astype(f32)
    step = jnp.exp(log_step.astype(f32))[:, None]
    mag = jnp.exp(a_re * step)
    lb_re = mag * jnp.cos(a_im * step)
    lb_im = mag * jnp.sin(a_im * step)
    den = a_re * a_re + a_im * a_im
    n_re = lb_re - 1.0
    n_im = lb_im
    f_re = (n_re * a_re + n_im * a_im) / den
    f_im = (n_im * a_re - n_re * a_im) / den
    b_re = b_re.astype(f32); b_im = b_im.astype(f32)
    bb_re = f_re[..., None] * b_re - f_im[..., None] * b_im
    bb_im = f_re[..., None] * b_im + f_im[..., None] * b_re
    bu_re = jnp.einsum('gph,blgh->blgp', bb_re, uf)
    bu_im = jnp.einsum('gph,blgh->blgp', bb_im, uf)
    ar = jnp.broadcast_to(lb_re, (1, L, S5_GROUPS, S5_STATE))
    ai = jnp.broadcast_to(lb_im, (1, L, S5_GROUPS, S5_STATE))

    def combine(e1, e2):
        a1r, a1i, b1r, b1i = e1
        a2r, a2i, b2r, b2i = e2
        return (a2r * a1r - a2i * a1i,
                a2r * a1i + a2i * a1r,
                a2r * b1r - a2i * b1i + b2r,
                a2r * b1i + a2i * b1r + b2i)

    _, _, s_re, s_im = lax.associative_scan(combine, (ar, ai, bu_re, bu_im), axis=1)
    y = (jnp.einsum('ghp,blgp->blgh', c_re.astype(f32), s_re)
         - jnp.einsum('ghp,blgp->blgh', c_im.astype(f32), s_im))
    y = y.reshape(bsz, L, S5_WIDTH) + d.astype(f32) * u.astype(f32)
    y = jax.nn.gelu(y)
    y = y * jax.nn.sigmoid(y @ w_glu.astype(f32) + b_glu.astype(f32))
    return y.astype(u.dtype)


def causal_depthwise_conv(x, w, b):
    C = x.shape[-1]
    y = lax.conv_general_dilated(x, w[:, None, :], window_strides=(1,),
                                 padding=[(SSD_CONV - 1, 0)],
                                 dimension_numbers=('NWC', 'WIO', 'NWC'),
                                 feature_group_count=C)
    return y + b


def ssd_mixer(z, xbc, dt_raw, conv_w, conv_b, dt_bias, a_log, d_skip, norm_w):
    f32 = jnp.float32
    bsz, L, _ = xbc.shape
    nc = L // SSD_CHUNK
    G, J, P, N, CH = SSD_GROUPS, SSD_HPG, SSD_HEADDIM, SSD_STATE, SSD_CHUNK
    xbc = jax.nn.silu(causal_depthwise_conv(xbc.astype(f32), conv_w.astype(f32), conv_b.astype(f32)))
    xs, bm, cm = jnp.split(xbc, [SSD_WIDTH, SSD_WIDTH + SSD_BC], axis=-1)
    dt = jax.nn.softplus(dt_raw.astype(f32) + dt_bias.astype(f32))
    a = -jnp.exp(a_log.astype(f32))
    xs = xs.reshape(bsz, nc, CH, G, J, P)
    bm = bm.reshape(bsz, nc, CH, G, N)
    cm = cm.reshape(bsz, nc, CH, G, N)
    dtc = dt.reshape(bsz, nc, CH, G, J)
    xdt = xs * dtc[..., None]
    da = jnp.transpose((dt * a).reshape(bsz, nc, CH, G, J), (0, 1, 3, 4, 2))
    a_cum = jnp.cumsum(da, axis=-1)
    seg = a_cum[..., :, None] - a_cum[..., None, :]
    causal = jnp.tril(jnp.ones((CH, CH), dtype=bool))
    lmat = jnp.exp(jnp.where(causal, seg, -jnp.inf))
    scores = jnp.einsum('bclgn,bcsgn->bcgls', cm, bm)
    wts = scores[:, :, :, None] * lmat
    y_diag = jnp.einsum('bcgjls,bcsgjp->bclgjp', wts, xdt)
    decay_states = jnp.exp(a_cum[..., -1:] - a_cum)
    states = jnp.einsum('bclgn,bcgjl,bclgjp->bcgjpn', bm, decay_states, xdt)
    chunk_decay = jnp.exp(a_cum[..., -1])

    def step(carry, inp):
        dec, st = inp
        return carry * dec[..., None, None] + st, carry

    init = jnp.zeros((bsz, G, J, P, N), f32)
    _, prev = lax.scan(step, init, (jnp.moveaxis(chunk_decay, 1, 0), jnp.moveaxis(states, 1, 0)))
    prev = jnp.moveaxis(prev, 0, 1)
    y_off = jnp.einsum('bclgn,bcgjpn,bcgjl->bclgjp', cm, prev, jnp.exp(a_cum))
    y = y_diag + y_off + xs * d_skip.astype(f32).reshape(G, J)[:, :, None]
    y = y.reshape(bsz, L, SSD_WIDTH)
    yg = (y * jax.nn.silu(z.astype(f32))).reshape(bsz, L, G, SSD_WIDTH // G)
    yg = yg * lax.rsqrt(jnp.mean(yg * yg, axis=-1, keepdims=True) + RMS_EPS)
    y = yg.reshape(bsz, L, SSD_WIDTH) * norm_w.astype(f32)
    return y.astype(z.dtype)


def _fwd_setup_inputs(seed: int = 0) -> dict:
    key = jax.random.key(seed)
    ks = iter(jax.random.split(key, 40))
    nrm = lambda shape, s: jax.random.normal(next(ks), shape, jnp.float32) * s
    D = D_MODEL
    x = jax.random.normal(next(ks), (BATCH, SEQ, D), jnp.float32)
    p = jax.random.normal(next(ks), (DEPTH, BATCH, SEQ, PLE_DIM), jnp.float32)
    norm_w = 1.0 + nrm((DEPTH, D), 0.02)
    w_in = nrm((DEPTH, D, IN_PROJ_DIM), D ** -0.5)
    n_idx = jnp.arange(S5_STATE, dtype=jnp.float32)
    s5_a_re = -0.5 + nrm((DEPTH, S5_GROUPS, S5_STATE), 0.01)
    s5_a_im = math.pi * n_idx + nrm((DEPTH, S5_GROUPS, S5_STATE), 0.01)
    s5_b_re = nrm((DEPTH, S5_GROUPS, S5_STATE, S5_GROUP), (2 * S5_GROUP) ** -0.5)
    s5_b_im = nrm((DEPTH, S5_GROUPS, S5_STATE, S5_GROUP), (2 * S5_GROUP) ** -0.5)
    s5_c_re = nrm((DEPTH, S5_GROUPS, S5_GROUP, S5_STATE), S5_STATE ** -0.5)
    s5_c_im = nrm((DEPTH, S5_GROUPS, S5_GROUP, S5_STATE), S5_STATE ** -0.5)
    s5_d = nrm((DEPTH, S5_WIDTH), 1.0)
    s5_log_step = jax.random.uniform(next(ks), (DEPTH, S5_GROUPS), jnp.float32,
                                     math.log(DT_MIN), math.log(DT_MAX))
    s5_w_glu = nrm((DEPTH, S5_WIDTH, S5_WIDTH), S5_WIDTH ** -0.5)
    s5_b_glu = nrm((DEPTH, S5_WIDTH), 0.01)
    ssd_conv_w = nrm((DEPTH, SSD_CONV, SSD_CONV_DIM), SSD_CONV ** -0.5)
    ssd_conv_b = nrm((DEPTH, SSD_CONV_DIM), 0.01)
    dt0 = jnp.exp(jax.random.uniform(next(ks), (DEPTH, SSD_HEADS), jnp.float32,
                                     math.log(DT_MIN), math.log(DT_MAX)))
    ssd_dt_bias = dt0 + jnp.log(-jnp.expm1(-dt0))
    ssd_a_log = jnp.log(jax.random.uniform(next(ks), (DEPTH, SSD_HEADS), jnp.float32, 1.0, 16.0))
    ssd_d = 1.0 + nrm((DEPTH, SSD_HEADS), 0.1)
    ssd_norm_w = 1.0 + nrm((DEPTH, SSD_WIDTH), 0.02)
    w_br_s5 = nrm((DEPTH, S5_WIDTH, D), S5_WIDTH ** -0.5)
    w_br_ssd = nrm((DEPTH, SSD_WIDTH, D), SSD_WIDTH ** -0.5)
    w_out = nrm((DEPTH, D, D), D ** -0.5)
    ple_norm_w = 1.0 + nrm((DEPTH, D), 0.02)
    w_ple_gate = nrm((DEPTH, D, D), D ** -0.5)
    w_ple_proj = nrm((DEPTH, PLE_DIM, D), PLE_DIM ** -0.5)
    final_norm_w = 1.0 + nrm((D,), 0.02)
    return {"x": x, "p": p, "norm_w": norm_w, "w_in": w_in,
            "s5_a_re": s5_a_re, "s5_a_im": s5_a_im, "s5_b_re": s5_b_re, "s5_b_im": s5_b_im,
            "s5_c_re": s5_c_re, "s5_c_im": s5_c_im, "s5_d": s5_d, "s5_log_step": s5_log_step,
            "s5_w_glu": s5_w_glu, "s5_b_glu": s5_b_glu,
            "ssd_conv_w": ssd_conv_w, "ssd_conv_b": ssd_conv_b, "ssd_dt_bias": ssd_dt_bias,
            "ssd_a_log": ssd_a_log, "ssd_d": ssd_d, "ssd_norm_w": ssd_norm_w,
            "w_br_s5": w_br_s5, "w_br_ssd": w_br_ssd, "w_out": w_out,
            "ple_norm_w": ple_norm_w, "w_ple_gate": w_ple_gate, "w_ple_proj": w_ple_proj,
            "final_norm_w": final_norm_w}


def _fwd_reference(x, p, norm_w, w_in, s5_a_re, s5_a_im, s5_b_re, s5_b_im, s5_c_re, s5_c_im,
              s5_d, s5_log_step, s5_w_glu, s5_b_glu, ssd_conv_w, ssd_conv_b, ssd_dt_bias,
              ssd_a_log, ssd_d, ssd_norm_w, w_br_s5, w_br_ssd, w_out, ple_norm_w,
              w_ple_gate, w_ple_proj, final_norm_w):
    h = x
    for i in range(DEPTH):
        hn = rms_norm(h, norm_w[i])
        proj = hn @ w_in[i]
        s5_u, s5_z, ssd_z, ssd_xbc, ssd_dt, gate_logits = jnp.split(proj, SPLITS, axis=-1)
        y5 = s5_mixer(s5_u, s5_a_re[i], s5_a_im[i], s5_b_re[i], s5_b_im[i], s5_c_re[i],
                      s5_c_im[i], s5_d[i], s5_log_step[i], s5_w_glu[i], s5_b_glu[i])
        y5 = y5 * jax.nn.silu(s5_z)
        yss = ssd_mixer(ssd_z, ssd_xbc, ssd_dt, ssd_conv_w[i], ssd_conv_b[i], ssd_dt_bias[i],
                        ssd_a_log[i], ssd_d[i], ssd_norm_w[i])
        g5, gss = jnp.split(jax.nn.sigmoid(gate_logits), N_BRANCH, axis=-1)
        merged = g5 * (y5 @ w_br_s5[i]) + gss * (yss @ w_br_ssd[i])
        h = h + merged @ w_out[i]
        ple_gate = jax.nn.sigmoid(rms_norm(h, ple_norm_w[i]) @ w_ple_gate[i])
        h = h + ple_gate * (p[i] @ w_ple_proj[i])
    return rms_norm(h, final_norm_w)


import jax as _jax
import jax.numpy as _jnp

TWIN_FORMAT = 'train_step'
FWD_PARAMS = ['x', 'p', 'norm_w', 'w_in', 's5_a_re', 's5_a_im', 's5_b_re', 's5_b_im', 's5_c_re', 's5_c_im', 's5_d', 's5_log_step', 's5_w_glu', 's5_b_glu', 'ssd_conv_w', 'ssd_conv_b', 'ssd_dt_bias', 'ssd_a_log', 'ssd_d', 'ssd_norm_w', 'w_br_s5', 'w_br_ssd', 'w_out', 'ple_norm_w', 'w_ple_gate', 'w_ple_proj', 'final_norm_w']
TWIN_WEIGHTS = ['norm_w', 'w_in', 's5_a_re', 's5_a_im', 's5_b_re', 's5_b_im', 's5_c_re', 's5_c_im', 's5_d', 's5_log_step', 's5_w_glu', 's5_b_glu', 'ssd_conv_w', 'ssd_conv_b', 'ssd_dt_bias', 'ssd_a_log', 'ssd_d', 'ssd_norm_w', 'w_br_s5', 'w_br_ssd', 'w_out', 'ple_norm_w', 'w_ple_gate', 'w_ple_proj', 'final_norm_w']
TWIN_DIFF_INPUT = 'x'
TWIN_INPUTS = ['x', 'p', 'norm_w', 'w_in', 's5_a_re', 's5_a_im', 's5_b_re', 's5_b_im', 's5_c_re', 's5_c_im', 's5_d', 's5_log_step', 's5_w_glu', 's5_b_glu', 'ssd_conv_w', 'ssd_conv_b', 'ssd_dt_bias', 'ssd_a_log', 'ssd_d', 'ssd_norm_w', 'w_br_s5', 'w_br_ssd', 'w_out', 'ple_norm_w', 'w_ple_gate', 'w_ple_proj', 'final_norm_w', 'loss_target', 'm_norm_w', 'm_w_in', 'm_s5_a_re', 'm_s5_a_im', 'm_s5_b_re', 'm_s5_b_im', 'm_s5_c_re', 'm_s5_c_im', 'm_s5_d', 'm_s5_log_step', 'm_s5_w_glu', 'm_s5_b_glu', 'm_ssd_conv_w', 'm_ssd_conv_b', 'm_ssd_dt_bias', 'm_ssd_a_log', 'm_ssd_d', 'm_ssd_norm_w', 'm_w_br_s5', 'm_w_br_ssd', 'm_w_out', 'm_ple_norm_w', 'm_w_ple_gate', 'm_w_ple_proj', 'm_final_norm_w', 'v_norm_w', 'v_w_in', 'v_s5_a_re', 'v_s5_a_im', 'v_s5_b_re', 'v_s5_b_im', 'v_s5_c_re', 'v_s5_c_im', 'v_s5_d', 'v_s5_log_step', 'v_s5_w_glu', 'v_s5_b_glu', 'v_ssd_conv_w', 'v_ssd_conv_b', 'v_ssd_dt_bias', 'v_ssd_a_log', 'v_ssd_d', 'v_ssd_norm_w', 'v_w_br_s5', 'v_w_br_ssd', 'v_w_out', 'v_ple_norm_w', 'v_w_ple_gate', 'v_w_ple_proj', 'v_final_norm_w']
TWIN_OUTPUTS = ['loss', 'grad_x', 'grad_norm_w', 'grad_w_in', 'grad_s5_a_re', 'grad_s5_a_im', 'grad_s5_b_re', 'grad_s5_b_im', 'grad_s5_c_re', 'grad_s5_c_im', 'grad_s5_d', 'grad_s5_log_step', 'grad_s5_w_glu', 'grad_s5_b_glu', 'grad_ssd_conv_w', 'grad_ssd_conv_b', 'grad_ssd_dt_bias', 'grad_ssd_a_log', 'grad_ssd_d', 'grad_ssd_norm_w', 'grad_w_br_s5', 'grad_w_br_ssd', 'grad_w_out', 'grad_ple_norm_w', 'grad_w_ple_gate', 'grad_w_ple_proj', 'grad_final_norm_w', 'delta_norm_w', 'delta_w_in', 'delta_s5_a_re', 'delta_s5_a_im', 'delta_s5_b_re', 'delta_s5_b_im', 'delta_s5_c_re', 'delta_s5_c_im', 'delta_s5_d', 'delta_s5_log_step', 'delta_s5_w_glu', 'delta_s5_b_glu', 'delta_ssd_conv_w', 'delta_ssd_conv_b', 'delta_ssd_dt_bias', 'delta_ssd_a_log', 'delta_ssd_d', 'delta_ssd_norm_w', 'delta_w_br_s5', 'delta_w_br_ssd', 'delta_w_out', 'delta_ple_norm_w', 'delta_w_ple_gate', 'delta_w_ple_proj', 'delta_final_norm_w', 'new_m_norm_w', 'new_m_w_in', 'new_m_s5_a_re', 'new_m_s5_a_im', 'new_m_s5_b_re', 'new_m_s5_b_im', 'new_m_s5_c_re', 'new_m_s5_c_im', 'new_m_s5_d', 'new_m_s5_log_step', 'new_m_s5_w_glu', 'new_m_s5_b_glu', 'new_m_ssd_conv_w', 'new_m_ssd_conv_b', 'new_m_ssd_dt_bias', 'new_m_ssd_a_log', 'new_m_ssd_d', 'new_m_ssd_norm_w', 'new_m_w_br_s5', 'new_m_w_br_ssd', 'new_m_w_out', 'new_m_ple_norm_w', 'new_m_w_ple_gate', 'new_m_w_ple_proj', 'new_m_final_norm_w', 'new_v_norm_w', 'new_v_w_in', 'new_v_s5_a_re', 'new_v_s5_a_im', 'new_v_s5_b_re', 'new_v_s5_b_im', 'new_v_s5_c_re', 'new_v_s5_c_im', 'new_v_s5_d', 'new_v_s5_log_step', 'new_v_s5_w_glu', 'new_v_s5_b_glu', 'new_v_ssd_conv_w', 'new_v_ssd_conv_b', 'new_v_ssd_dt_bias', 'new_v_ssd_a_log', 'new_v_ssd_d', 'new_v_ssd_norm_w', 'new_v_w_br_s5', 'new_v_w_br_ssd', 'new_v_w_out', 'new_v_ple_norm_w', 'new_v_w_ple_gate', 'new_v_w_ple_proj', 'new_v_final_norm_w']
TWIN_LEAF_KINDS = {'loss': 'loss', 'grad_x': 'grad_x', 'grad_norm_w': 'grad_w', 'grad_w_in': 'grad_w', 'grad_s5_a_re': 'grad_w', 'grad_s5_a_im': 'grad_w', 'grad_s5_b_re': 'grad_w', 'grad_s5_b_im': 'grad_w', 'grad_s5_c_re': 'grad_w', 'grad_s5_c_im': 'grad_w', 'grad_s5_d': 'grad_w', 'grad_s5_log_step': 'grad_w', 'grad_s5_w_glu': 'grad_w', 'grad_s5_b_glu': 'grad_w', 'grad_ssd_conv_w': 'grad_w', 'grad_ssd_conv_b': 'grad_w', 'grad_ssd_dt_bias': 'grad_w', 'grad_ssd_a_log': 'grad_w', 'grad_ssd_d': 'grad_w', 'grad_ssd_norm_w': 'grad_w', 'grad_w_br_s5': 'grad_w', 'grad_w_br_ssd': 'grad_w', 'grad_w_out': 'grad_w', 'grad_ple_norm_w': 'grad_w', 'grad_w_ple_gate': 'grad_w', 'grad_w_ple_proj': 'grad_w', 'grad_final_norm_w': 'grad_w', 'delta_norm_w': 'delta_w', 'delta_w_in': 'delta_w', 'delta_s5_a_re': 'delta_w', 'delta_s5_a_im': 'delta_w', 'delta_s5_b_re': 'delta_w', 'delta_s5_b_im': 'delta_w', 'delta_s5_c_re': 'delta_w', 'delta_s5_c_im': 'delta_w', 'delta_s5_d': 'delta_w', 'delta_s5_log_step': 'delta_w', 'delta_s5_w_glu': 'delta_w', 'delta_s5_b_glu': 'delta_w', 'delta_ssd_conv_w': 'delta_w', 'delta_ssd_conv_b': 'delta_w', 'delta_ssd_dt_bias': 'delta_w', 'delta_ssd_a_log': 'delta_w', 'delta_ssd_d': 'delta_w', 'delta_ssd_norm_w': 'delta_w', 'delta_w_br_s5': 'delta_w', 'delta_w_br_ssd': 'delta_w', 'delta_w_out': 'delta_w', 'delta_ple_norm_w': 'delta_w', 'delta_w_ple_gate': 'delta_w', 'delta_w_ple_proj': 'delta_w', 'delta_final_norm_w': 'delta_w', 'new_m_norm_w': 'new_m', 'new_m_w_in': 'new_m', 'new_m_s5_a_re': 'new_m', 'new_m_s5_a_im': 'new_m', 'new_m_s5_b_re': 'new_m', 'new_m_s5_b_im': 'new_m', 'new_m_s5_c_re': 'new_m', 'new_m_s5_c_im': 'new_m', 'new_m_s5_d': 'new_m', 'new_m_s5_log_step': 'new_m', 'new_m_s5_w_glu': 'new_m', 'new_m_s5_b_glu': 'new_m', 'new_m_ssd_conv_w': 'new_m', 'new_m_ssd_conv_b': 'new_m', 'new_m_ssd_dt_bias': 'new_m', 'new_m_ssd_a_log': 'new_m', 'new_m_ssd_d': 'new_m', 'new_m_ssd_norm_w': 'new_m', 'new_m_w_br_s5': 'new_m', 'new_m_w_br_ssd': 'new_m', 'new_m_w_out': 'new_m', 'new_m_ple_norm_w': 'new_m', 'new_m_w_ple_gate': 'new_m', 'new_m_w_ple_proj': 'new_m', 'new_m_final_norm_w': 'new_m', 'new_v_norm_w': 'new_v', 'new_v_w_in': 'new_v', 'new_v_s5_a_re': 'new_v', 'new_v_s5_a_im': 'new_v', 'new_v_s5_b_re': 'new_v', 'new_v_s5_b_im': 'new_v', 'new_v_s5_c_re': 'new_v', 'new_v_s5_c_im': 'new_v', 'new_v_s5_d': 'new_v', 'new_v_s5_log_step': 'new_v', 'new_v_s5_w_glu': 'new_v', 'new_v_s5_b_glu': 'new_v', 'new_v_ssd_conv_w': 'new_v', 'new_v_ssd_conv_b': 'new_v', 'new_v_ssd_dt_bias': 'new_v', 'new_v_ssd_a_log': 'new_v', 'new_v_ssd_d': 'new_v', 'new_v_ssd_norm_w': 'new_v', 'new_v_w_br_s5': 'new_v', 'new_v_w_br_ssd': 'new_v', 'new_v_w_out': 'new_v', 'new_v_ple_norm_w': 'new_v', 'new_v_w_ple_gate': 'new_v', 'new_v_w_ple_proj': 'new_v', 'new_v_final_norm_w': 'new_v'}


def _forward(args):
    return _fwd_reference(*[args[k] for k in FWD_PARAMS])


def _output_shape():
    out = _jax.eval_shape(lambda: _forward(_fwd_setup_inputs(0)))
    return out.shape, out.dtype

N_MICROBATCH = 1
ADAM_LR = 0.001
ADAM_B1 = 0.9
ADAM_B2 = 0.999
ADAM_EPS = 1e-08
ADAM_WD = 0.01
ADAM_STEP = 10
PER_EXAMPLE_BATCH_AXIS = {'x': 0, 'p': 1, 'loss_target': 0}
SHARED_INPUTS = []
_WEIGHT_DTYPES = {'norm_w': _jnp.float32, 'w_in': _jnp.float32, 's5_a_re': _jnp.float32, 's5_a_im': _jnp.float32, 's5_b_re': _jnp.float32, 's5_b_im': _jnp.float32, 's5_c_re': _jnp.float32, 's5_c_im': _jnp.float32, 's5_d': _jnp.float32, 's5_log_step': _jnp.float32, 's5_w_glu': _jnp.float32, 's5_b_glu': _jnp.float32, 'ssd_conv_w': _jnp.float32, 'ssd_conv_b': _jnp.float32, 'ssd_dt_bias': _jnp.float32, 'ssd_a_log': _jnp.float32, 'ssd_d': _jnp.float32, 'ssd_norm_w': _jnp.float32, 'w_br_s5': _jnp.float32, 'w_br_ssd': _jnp.float32, 'w_out': _jnp.float32, 'ple_norm_w': _jnp.float32, 'w_ple_gate': _jnp.float32, 'w_ple_proj': _jnp.float32, 'final_norm_w': _jnp.float32}
MOMENT_SCALE = {'norm_w': 1.237315e-01, 'w_in': 4.651939e-02, 's5_a_re': 1.813827e-03, 's5_a_im': 1.861615e-03, 's5_b_re': 1.155621e-03, 's5_b_im': 1.134049e-03, 's5_c_re': 1.634008e-03, 's5_c_im': 1.601839e-03, 's5_d': 2.577435e-02, 's5_log_step': 2.725068e+00, 's5_w_glu': 7.481605e-03, 's5_b_glu': 1.227990e-02, 'ssd_conv_w': 5.563463e-02, 'ssd_conv_b': 7.879194e-02, 'ssd_dt_bias': 1.691669e-01, 'ssd_a_log': 1.611441e-01, 'ssd_d': 6.438182e-01, 'ssd_norm_w': 6.900047e-02, 'w_br_s5': 1.689445e-02, 'w_br_ssd': 7.767807e-02, 'w_out': 7.938851e-02, 'ple_norm_w': 2.897481e-02, 'w_ple_gate': 2.986535e-02, 'w_ple_proj': 7.600305e-02, 'final_norm_w': 3.199170e+01}


def _to_microbatches(a, axis):
    t = _jnp.moveaxis(a, axis, 0)
    t = t.reshape((N_MICROBATCH, t.shape[0] // N_MICROBATCH) + t.shape[1:])
    return _jnp.moveaxis(t, 1, axis + 1)


def setup_inputs(seed: int = 0) -> dict:
    inp = _fwd_setup_inputs(seed)
    key = _jax.random.fold_in(_jax.random.key(seed), 7919)
    shape, _ = _output_shape()
    out = dict(inp)
    out["loss_target"] = _jax.random.normal(_jax.random.fold_in(key, 0), shape, _jnp.float32)
    for i, name in enumerate(TWIN_WEIGHTS):
        w = inp[name].astype(_jnp.float32)
        if MOMENT_SCALE is None:
            s = _jnp.sqrt(_jnp.mean(_jnp.square(w)) + 1e-30)
        else:
            s = MOMENT_SCALE[name]
        km, kv = _jax.random.split(_jax.random.fold_in(key, i + 1))
        out[name] = w
        out["m_" + name] = s * _jax.random.normal(km, w.shape, _jnp.float32)
        out["v_" + name] = (s * s) * _jax.random.uniform(kv, w.shape, _jnp.float32, 0.5, 1.5)
    if N_MICROBATCH > 1:
        for name, axis in PER_EXAMPLE_BATCH_AXIS.items():
            out[name] = _to_microbatches(out[name], axis)
    return {'x': out['x'], 'p': out['p'], 'norm_w': out['norm_w'], 'w_in': out['w_in'], 's5_a_re': out['s5_a_re'], 's5_a_im': out['s5_a_im'], 's5_b_re': out['s5_b_re'], 's5_b_im': out['s5_b_im'], 's5_c_re': out['s5_c_re'], 's5_c_im': out['s5_c_im'], 's5_d': out['s5_d'], 's5_log_step': out['s5_log_step'], 's5_w_glu': out['s5_w_glu'], 's5_b_glu': out['s5_b_glu'], 'ssd_conv_w': out['ssd_conv_w'], 'ssd_conv_b': out['ssd_conv_b'], 'ssd_dt_bias': out['ssd_dt_bias'], 'ssd_a_log': out['ssd_a_log'], 'ssd_d': out['ssd_d'], 'ssd_norm_w': out['ssd_norm_w'], 'w_br_s5': out['w_br_s5'], 'w_br_ssd': out['w_br_ssd'], 'w_out': out['w_out'], 'ple_norm_w': out['ple_norm_w'], 'w_ple_gate': out['w_ple_gate'], 'w_ple_proj': out['w_ple_proj'], 'final_norm_w': out['final_norm_w'], 'loss_target': out['loss_target'], 'm_norm_w': out['m_norm_w'], 'm_w_in': out['m_w_in'], 'm_s5_a_re': out['m_s5_a_re'], 'm_s5_a_im': out['m_s5_a_im'], 'm_s5_b_re': out['m_s5_b_re'], 'm_s5_b_im': out['m_s5_b_im'], 'm_s5_c_re': out['m_s5_c_re'], 'm_s5_c_im': out['m_s5_c_im'], 'm_s5_d': out['m_s5_d'], 'm_s5_log_step': out['m_s5_log_step'], 'm_s5_w_glu': out['m_s5_w_glu'], 'm_s5_b_glu': out['m_s5_b_glu'], 'm_ssd_conv_w': out['m_ssd_conv_w'], 'm_ssd_conv_b': out['m_ssd_conv_b'], 'm_ssd_dt_bias': out['m_ssd_dt_bias'], 'm_ssd_a_log': out['m_ssd_a_log'], 'm_ssd_d': out['m_ssd_d'], 'm_ssd_norm_w': out['m_ssd_norm_w'], 'm_w_br_s5': out['m_w_br_s5'], 'm_w_br_ssd': out['m_w_br_ssd'], 'm_w_out': out['m_w_out'], 'm_ple_norm_w': out['m_ple_norm_w'], 'm_w_ple_gate': out['m_w_ple_gate'], 'm_w_ple_proj': out['m_w_ple_proj'], 'm_final_norm_w': out['m_final_norm_w'], 'v_norm_w': out['v_norm_w'], 'v_w_in': out['v_w_in'], 'v_s5_a_re': out['v_s5_a_re'], 'v_s5_a_im': out['v_s5_a_im'], 'v_s5_b_re': out['v_s5_b_re'], 'v_s5_b_im': out['v_s5_b_im'], 'v_s5_c_re': out['v_s5_c_re'], 'v_s5_c_im': out['v_s5_c_im'], 'v_s5_d': out['v_s5_d'], 'v_s5_log_step': out['v_s5_log_step'], 'v_s5_w_glu': out['v_s5_w_glu'], 'v_s5_b_glu': out['v_s5_b_glu'], 'v_ssd_conv_w': out['v_ssd_conv_w'], 'v_ssd_conv_b': out['v_ssd_conv_b'], 'v_ssd_dt_bias': out['v_ssd_dt_bias'], 'v_ssd_a_log': out['v_ssd_a_log'], 'v_ssd_d': out['v_ssd_d'], 'v_ssd_norm_w': out['v_ssd_norm_w'], 'v_w_br_s5': out['v_w_br_s5'], 'v_w_br_ssd': out['v_w_br_ssd'], 'v_w_out': out['v_w_out'], 'v_ple_norm_w': out['v_ple_norm_w'], 'v_w_ple_gate': out['v_w_ple_gate'], 'v_w_ple_proj': out['v_w_ple_proj'], 'v_final_norm_w': out['v_final_norm_w']}


def _loss(weights, diff, rest, loss_target):
    with _jax.named_scope("forward"):
        args = {**rest, TWIN_DIFF_INPUT: diff, **{k: w.astype(_WEIGHT_DTYPES[k]) for k, w in weights.items()}}
        y = _forward(args)
    with _jax.named_scope("loss_head"):
        err = _jnp.square(y.astype(_jnp.float32) - loss_target)
        return 0.5 * _jnp.sum(_jnp.mean(err, axis=-1)) if err.ndim else 0.5 * err


def _adamw(w, g, m, v):
    m = ADAM_B1 * m + (1.0 - ADAM_B1) * g
    v = ADAM_B2 * v + (1.0 - ADAM_B2) * _jnp.square(g)
    m_hat = m / (1.0 - ADAM_B1 ** ADAM_STEP)
    v_hat = v / (1.0 - ADAM_B2 ** ADAM_STEP)
    delta = -ADAM_LR * (m_hat / (_jnp.sqrt(v_hat) + ADAM_EPS) + ADAM_WD * w)
    return delta, m, v


def reference(x, p, norm_w, w_in, s5_a_re, s5_a_im, s5_b_re, s5_b_im, s5_c_re, s5_c_im, s5_d, s5_log_step, s5_w_glu, s5_b_glu, ssd_conv_w, ssd_conv_b, ssd_dt_bias, ssd_a_log, ssd_d, ssd_norm_w, w_br_s5, w_br_ssd, w_out, ple_norm_w, w_ple_gate, w_ple_proj, final_norm_w, loss_target, m_norm_w, m_w_in, m_s5_a_re, m_s5_a_im, m_s5_b_re, m_s5_b_im, m_s5_c_re, m_s5_c_im, m_s5_d, m_s5_log_step, m_s5_w_glu, m_s5_b_glu, m_ssd_conv_w, m_ssd_conv_b, m_ssd_dt_bias, m_ssd_a_log, m_ssd_d, m_ssd_norm_w, m_w_br_s5, m_w_br_ssd, m_w_out, m_ple_norm_w, m_w_ple_gate, m_w_ple_proj, m_final_norm_w, v_norm_w, v_w_in, v_s5_a_re, v_s5_a_im, v_s5_b_re, v_s5_b_im, v_s5_c_re, v_s5_c_im, v_s5_d, v_s5_log_step, v_s5_w_glu, v_s5_b_glu, v_ssd_conv_w, v_ssd_conv_b, v_ssd_dt_bias, v_ssd_a_log, v_ssd_d, v_ssd_norm_w, v_w_br_s5, v_w_br_ssd, v_w_out, v_ple_norm_w, v_w_ple_gate, v_w_ple_proj, v_final_norm_w):
    given = dict(x=x, p=p, norm_w=norm_w, w_in=w_in, s5_a_re=s5_a_re, s5_a_im=s5_a_im, s5_b_re=s5_b_re, s5_b_im=s5_b_im, s5_c_re=s5_c_re, s5_c_im=s5_c_im, s5_d=s5_d, s5_log_step=s5_log_step, s5_w_glu=s5_w_glu, s5_b_glu=s5_b_glu, ssd_conv_w=ssd_conv_w, ssd_conv_b=ssd_conv_b, ssd_dt_bias=ssd_dt_bias, ssd_a_log=ssd_a_log, ssd_d=ssd_d, ssd_norm_w=ssd_norm_w, w_br_s5=w_br_s5, w_br_ssd=w_br_ssd, w_out=w_out, ple_norm_w=ple_norm_w, w_ple_gate=w_ple_gate, w_ple_proj=w_ple_proj, final_norm_w=final_norm_w, loss_target=loss_target, m_norm_w=m_norm_w, m_w_in=m_w_in, m_s5_a_re=m_s5_a_re, m_s5_a_im=m_s5_a_im, m_s5_b_re=m_s5_b_re, m_s5_b_im=m_s5_b_im, m_s5_c_re=m_s5_c_re, m_s5_c_im=m_s5_c_im, m_s5_d=m_s5_d, m_s5_log_step=m_s5_log_step, m_s5_w_glu=m_s5_w_glu, m_s5_b_glu=m_s5_b_glu, m_ssd_conv_w=m_ssd_conv_w, m_ssd_conv_b=m_ssd_conv_b, m_ssd_dt_bias=m_ssd_dt_bias, m_ssd_a_log=m_ssd_a_log, m_ssd_d=m_ssd_d, m_ssd_norm_w=m_ssd_norm_w, m_w_br_s5=m_w_br_s5, m_w_br_ssd=m_w_br_ssd, m_w_out=m_w_out, m_ple_norm_w=m_ple_norm_w, m_w_ple_gate=m_w_ple_gate, m_w_ple_proj=m_w_ple_proj, m_final_norm_w=m_final_norm_w, v_norm_w=v_norm_w, v_w_in=v_w_in, v_s5_a_re=v_s5_a_re, v_s5_a_im=v_s5_a_im, v_s5_b_re=v_s5_b_re, v_s5_b_im=v_s5_b_im, v_s5_c_re=v_s5_c_re, v_s5_c_im=v_s5_c_im, v_s5_d=v_s5_d, v_s5_log_step=v_s5_log_step, v_s5_w_glu=v_s5_w_glu, v_s5_b_glu=v_s5_b_glu, v_ssd_conv_w=v_ssd_conv_w, v_ssd_conv_b=v_ssd_conv_b, v_ssd_dt_bias=v_ssd_dt_bias, v_ssd_a_log=v_ssd_a_log, v_ssd_d=v_ssd_d, v_ssd_norm_w=v_ssd_norm_w, v_w_br_s5=v_w_br_s5, v_w_br_ssd=v_w_br_ssd, v_w_out=v_w_out, v_ple_norm_w=v_ple_norm_w, v_w_ple_gate=v_w_ple_gate, v_w_ple_proj=v_w_ple_proj, v_final_norm_w=v_final_norm_w)
    weights = {n: given[n] for n in TWIN_WEIGHTS}
    shared = {n: given[n] for n in SHARED_INPUTS}
    per_example = {n: given[n] for n in ['x', 'p']}
    grad_fn = _jax.value_and_grad(_loss, argnums=(0, 1))

    def one_microbatch(ex, loss_target):
        ex = dict(ex)
        diff = ex.pop(TWIN_DIFF_INPUT)
        return grad_fn(weights, diff, {**shared, **ex}, loss_target)

    if N_MICROBATCH == 1:
        loss, (grad_w, grad_x) = one_microbatch(per_example, given["loss_target"])
    else:
        def body(carry, xs):
            loss_sum, grad_sum = carry
            l_k, (gw_k, gx_k) = one_microbatch(xs[0], xs[1])
            with _jax.named_scope("update"):
                return (loss_sum + l_k, _jax.tree.map(_jnp.add, grad_sum, gw_k)), gx_k

        init = (_jnp.zeros((), _jnp.float32), _jax.tree.map(_jnp.zeros_like, weights))
        (loss, grad_w), grad_x = _jax.lax.scan(body, init, (per_example, given["loss_target"]))
    with _jax.named_scope("update"):
        delta_w, new_m, new_v = {}, {}, {}
        for n in TWIN_WEIGHTS:
            delta_w[n], new_m[n], new_v[n] = _adamw(weights[n], grad_w[n], given["m_" + n], given["v_" + n])
    return (loss, grad_x, *[grad_w[n] for n in TWIN_WEIGHTS], *[delta_w[n] for n in TWIN_WEIGHTS],
            *[new_m[n] for n in TWIN_WEIGHTS], *[new_v[n] for n in TWIN_WEIGHTS])
```

```python
import functools
import math

import jax
import jax.numpy as jnp
from jax import lax
from jax.experimental import pallas as pl
from jax.experimental.pallas import tpu as pltpu

F32 = jnp.float32
BF16 = jnp.bfloat16
MESH = pl.DeviceIdType.MESH

D_MODEL = 1024
PLE_DIM = 256
RMS_EPS = 1e-6
S5_WIDTH = 512
S5_GROUP = 16
S5_GROUPS = 32
S5_STATE = 64
S5_N = S5_GROUPS * S5_STATE
S5_LB = 512
S5_NJ = S5_N // S5_LB
S5_TB = 256
S5_LOG_TB = 8
SSD_WIDTH = 1536
SSD_HEADDIM = 64
SSD_HEADS = 24
SSD_GROUPS = 4
SSD_HPG = 6
SSD_STATE = 128
SSD_CONV = 4
SSD_CHUNK = 128
SSD_BC = 512
SSD_CONV_DIM = 2560
GROUP_W = SSD_WIDTH // SSD_GROUPS
N_CHIPS = 4
N_DEV = 8

OFF_XBC, OFF_U5, OFF_ZS, OFF_Z5, OFF_G5, OFF_GS, OFF_DT = 0, 2560, 3072, 4608, 5120, 6144, 7168
PROJ_W = 7680
IN_PROJ_DIM = 7192

ADAM_LR, ADAM_B1, ADAM_B2, ADAM_EPS, ADAM_WD, ADAM_STEP = 0.001, 0.9, 0.999, 1e-08, 0.01, 10

VMEM_LIMIT = 56 * 1024 * 1024


def _pick(n, cands):
    for c in cands:
        if n % c == 0:
            return c
    return n


def _cparams(sem):
    return pltpu.CompilerParams(dimension_semantics=sem, vmem_limit_bytes=VMEM_LIMIT)


def _dg(a, b, ca, cb):
    return lax.dot_general(a.astype(BF16), b.astype(BF16), (((ca,), (cb,)), ((), ())), preferred_element_type=F32)


@jax.custom_vjp
def dot_nn(a, b):
    return _dg(a, b, 1, 0)


@jax.custom_vjp
def dot_nt(a, b):
    return _dg(a, b, 1, 1)


@jax.custom_vjp
def dot_tn(a, b):
    return _dg(a, b, 0, 0)


dot_nn.defvjp(lambda a, b: (_dg(a, b, 1, 0), (a, b)), lambda r, g: (_dg(g, r[1], 1, 1), _dg(r[0], g, 0, 0)))
dot_nt.defvjp(lambda a, b: (_dg(a, b, 1, 1), (a, b)), lambda r, g: (_dg(g, r[1], 1, 0), _dg(g, r[0], 0, 0)))
dot_tn.defvjp(lambda a, b: (_dg(a, b, 0, 0), (a, b)), lambda r, g: (_dg(r[1], g, 1, 1), _dg(r[0], g, 1, 0)))


def _matmul(name, a, b, *, ta=False, tb=False, a_win=None, out_dtype=F32):
    a_off, a_w = a_win if a_win is not None else (0, a.shape[1])
    if ta:
        kdim, m = a.shape[0], a_w
    else:
        m, kdim = a.shape[0], a_w
    n = b.shape[0] if tb else b.shape[1]
    assert (b.shape[1] if tb else b.shape[0]) == kdim, (name, a.shape, b.shape)
    tm = _pick(m, (512, 256, 128))
    tn = _pick(n, (512, 256, 128))
    tk = _pick(kdim, (512, 256, 128))
    nk = kdim // tk
    if ta:
        assert a_off % tm == 0
        a_spec = pl.BlockSpec((tk, tm), lambda i, j, k: (k, i + a_off // tm))
    else:
        assert a_off % tk == 0
        a_spec = pl.BlockSpec((tm, tk), lambda i, j, k: (i, k + a_off // tk))
    if tb:
        b_spec = pl.BlockSpec((tn, tk), lambda i, j, k: (j, k))
    else:
        b_spec = pl.BlockSpec((tk, tn), lambda i, j, k: (k, j))
    ca, cb = (0 if ta else 1), (1 if tb else 0)

    def body(a_ref, b_ref, o_ref, acc_ref):
        k = pl.program_id(2)

        @pl.when(k == 0)
        def _():
            acc_ref[...] = jnp.zeros_like(acc_ref)

        acc_ref[...] += _dg(a_ref[...], b_ref[...], ca, cb)

        @pl.when(k == nk - 1)
        def _():
            o_ref[...] = acc_ref[...].astype(o_ref.dtype)

    return pl.pallas_call(
        body, name=name, grid=(m // tm, n // tn, nk),
        in_specs=[a_spec, b_spec],
        out_specs=pl.BlockSpec((tm, tn), lambda i, j, k: (i, j)),
        out_shape=jax.ShapeDtypeStruct((m, n), out_dtype),
        scratch_shapes=[pltpu.VMEM((tm, tn), F32)],
        compiler_params=_cparams(("parallel", "parallel", "arbitrary")),
    )(a, b)


def _rowwise(name, fn, n_rows, tr, row_ins, par_ins, row_outs, acc_outs=()):
    nr, npar, no, na = len(row_ins), len(par_ins), len(row_outs), len(acc_outs)
    in_specs = []
    for arr, off, w in row_ins:
        assert off % w == 0 and arr.shape[0] == n_rows, (name, arr.shape, off, w)
        in_specs.append(pl.BlockSpec((tr, w), functools.partial(lambda i, c: (i, c), c=off // w)))
    for arr, off, w in par_ins:
        assert off % w == 0
        in_specs.append(pl.BlockSpec((arr.shape[0], w), functools.partial(lambda i, c: (0, c), c=off // w)))
    out_specs = [pl.BlockSpec((tr, w), lambda i: (i, 0)) for w, _ in row_outs]
    out_specs += [pl.BlockSpec((r, w), lambda i: (0, 0)) for r, w in acc_outs]
    out_shape = [jax.ShapeDtypeStruct((n_rows, w), dt) for w, dt in row_outs]
    out_shape += [jax.ShapeDtypeStruct((r, w), F32) for r, w in acc_outs]

    def body(*refs):
        rows = [r[...] for r in refs[:nr]]
        pars = [r[...] for r in refs[nr:nr + npar]]
        o_refs = refs[nr + npar:nr + npar + no]
        a_refs = refs[nr + npar + no:]
        outs, accs = fn(rows, pars)
        for o_ref, o in zip(o_refs, outs, strict=True):
            o_ref[...] = o.astype(o_ref.dtype)
        if na:
            @pl.when(pl.program_id(0) == 0)
            def _():
                for a_ref in a_refs:
                    a_ref[...] = jnp.zeros_like(a_ref)

            for a_ref, a in zip(a_refs, accs, strict=True):
                a_ref[...] += jnp.broadcast_to(a, a_ref.shape)

    res = pl.pallas_call(
        body, name=name, grid=(n_rows // tr,),
        in_specs=in_specs, out_specs=out_specs, out_shape=out_shape,
        compiler_params=_cparams(("arbitrary",) if na else ("parallel",)),
    )(*[a for a, _, _ in row_ins], *[a for a, _, _ in par_ins])
    return res


def _rms(x, w):
    return x * lax.rsqrt(jnp.mean(x * x, axis=-1, keepdims=True) + RMS_EPS) * w


def _gated_norm(y, z, w):
    outs = []
    for g in range(SSD_GROUPS):
        sl = slice(g * GROUP_W, (g + 1) * GROUP_W)
        yg = y[:, sl] * jax.nn.silu(z[:, sl])
        outs.append(yg * lax.rsqrt(jnp.mean(yg * yg, axis=-1, keepdims=True) + RMS_EPS) * w[:, sl])
    return jnp.concatenate(outs, axis=-1)


def _s5_out(yc, u, z, t, d, bg):
    ge = jax.nn.gelu(yc + d * u)
    return ge * jax.nn.sigmoid(t + bg) * jax.nn.silu(z)


def _merge(g5, gs, m5, ms):
    return jax.nn.sigmoid(g5) * m5 + jax.nn.sigmoid(gs) * ms


def _head_loss(h1, pgl, pp, fw, tgt):
    h2 = h1 + jax.nn.sigmoid(pgl) * pp
    err = _rms(h2, fw) - tgt
    per_row = 0.5 * jnp.mean(err * err, axis=-1, keepdims=True)
    return jnp.sum(per_row, axis=0, keepdims=True)


def _s5_disc(a_re, a_im, log_step, b_re2, b_im2, expand):
    step = jnp.exp(log_step)
    mag = jnp.exp(a_re * step)
    lb_re = mag * jnp.cos(a_im * step)
    lb_im = mag * jnp.sin(a_im * step)
    den = a_re * a_re + a_im * a_im
    n_re = lb_re - 1.0
    f_re = (n_re * a_re + lb_im * a_im) / den
    f_im = (lb_im * a_re - n_re * a_im) / den
    hi = lax.Precision.HIGHEST
    fr = jnp.dot(expand, f_re, precision=hi, preferred_element_type=F32)
    fi = jnp.dot(expand, f_im, precision=hi, preferred_element_type=F32)
    return lb_re, lb_im, fr * b_re2 - fi * b_im2, fr * b_im2 + fi * b_re2


def _s5_params_fwd(a_re, a_im, log_step, b_re2, b_im2, expand):
    gp = a_re.shape

    def body(ar, ai, ls, br, bi, ex, pr_ref, pi_ref, bbr_ref, bbi_ref):
        lr, li, bbr, bbi = _s5_disc(ar[...], ai[...], ls[...], br[...], bi[...], ex[...])
        bbr_ref[...] = bbr
        bbi_ref[...] = bbi
        for k in range(S5_LOG_TB):
            pr_ref[k] = lr
            pi_ref[k] = li
            lr, li = lr * lr - li * li, 2.0 * lr * li

    return pl.pallas_call(
        body, name="s5_params_fwd",
        out_shape=(jax.ShapeDtypeStruct((S5_LOG_TB,) + gp, F32), jax.ShapeDtypeStruct((S5_LOG_TB,) + gp, F32),
                   jax.ShapeDtypeStruct(b_re2.shape, F32), jax.ShapeDtypeStruct(b_re2.shape, F32)),
    )(a_re, a_im, log_step, b_re2, b_im2, expand)


def _s5_params_bwd(a_re, a_im, log_step, b_re2, b_im2, expand, d_lr, d_li, d_bbr, d_bbi):
    def body(ar, ai, ls, br, bi, ex, glr, gli, gbr, gbi, dar, dai, dls, dbr, dbi):
        _, vjp = jax.vjp(lambda *p: _s5_disc(*p, ex[...]), ar[...], ai[...], ls[...], br[...], bi[...])
        g = vjp((glr[...], gli[...], gbr[...], gbi[...]))
        for ref, val in zip((dar, dai, dls, dbr, dbi), g, strict=True):
            ref[...] = val

    return pl.pallas_call(
        body, name="s5_params_bwd",
        out_shape=tuple(jax.ShapeDtypeStruct(v.shape, F32) for v in (a_re, a_im, log_step, b_re2, b_im2)),
    )(a_re, a_im, log_step, b_re2, b_im2, expand, d_lr, d_li, d_bbr, d_bbi)


def _scan_block(xr, xi, lp, cr, ci, reverse):
    tb = xr.shape[0]
    row = lax.broadcasted_iota(jnp.int32, xr.shape, 0)
    l_r, l_i = lp[0:1, :S5_LB], lp[0:1, S5_LB:]
    edge = (row == tb - 1) if reverse else (row == 0)
    xr = xr + jnp.where(edge, l_r * cr - l_i * ci, 0.0)
    xi = xi + jnp.where(edge, l_r * ci + l_i * cr, 0.0)
    for k in range(S5_LOG_TB):
        sh = 1 << k
        a_r, a_i = lp[k:k + 1, :S5_LB], lp[k:k + 1, S5_LB:]
        if reverse:
            keep = row < tb - sh
            pr = jnp.where(keep, pltpu.roll(xr, tb - sh, 0), 0.0)
            pi = jnp.where(keep, pltpu.roll(xi, tb - sh, 0), 0.0)
        else:
            keep = row >= sh
            pr = jnp.where(keep, pltpu.roll(xr, sh, 0), 0.0)
            pi = jnp.where(keep, pltpu.roll(xi, sh, 0), 0.0)
        xr, xi = xr + a_r * pr - a_i * pi, xi + a_r * pi + a_i * pr
    return xr, xi


def _s5_scan_fwd(bu, lam_pow, n_seq, seq_len):
    n_t = seq_len // S5_TB
    blk = 2 * S5_LB

    def body(bu_ref, lp_ref, s_ref, cr, ci):
        @pl.when(pl.program_id(2) == 0)
        def _():
            cr[...] = jnp.zeros_like(cr)
            ci[...] = jnp.zeros_like(ci)

        xr, xi = _scan_block(bu_ref[:, :S5_LB], bu_ref[:, S5_LB:], lp_ref[...], cr[...], ci[...], False)
        s_ref[:, :S5_LB] = xr
        s_ref[:, S5_LB:] = xi
        cr[...] = xr[S5_TB - 1:S5_TB, :]
        ci[...] = xi[S5_TB - 1:S5_TB, :]

    return pl.pallas_call(
        body, name="s5_scan_fwd", grid=(S5_NJ, n_seq, n_t),
        in_specs=[pl.BlockSpec((S5_TB, blk), lambda j, b, t: (b * n_t + t, j)),
                  pl.BlockSpec((S5_LOG_TB, blk), lambda j, b, t: (0, j))],
        out_specs=pl.BlockSpec((S5_TB, blk), lambda j, b, t: (b * n_t + t, j)),
        out_shape=jax.ShapeDtypeStruct(bu.shape, F32),
        scratch_shapes=[pltpu.VMEM((1, S5_LB), F32), pltpu.VMEM((1, S5_LB), F32)],
        compiler_params=_cparams(("parallel", "parallel", "arbitrary")),
    )(bu, lam_pow)


def _s5_scan_bwd(ds, s, lam_pow_conj, n_seq, seq_len):
    n_t = seq_len // S5_TB
    blk = 2 * S5_LB
    halo_per_blk = S5_TB // 8

    def rows(j, b, t):
        return (b * n_t + (n_t - 1 - t), j)

    def halo(j, b, t):
        return (jnp.maximum((b * n_t + (n_t - 1 - t)) * halo_per_blk - 1, 0), j)

    def body(ds_ref, s_ref, h_ref, lp_ref, g_ref, dl_ref, cr, ci):
        b, t = pl.program_id(1), pl.program_id(2)

        @pl.when(t == 0)
        def _():
            cr[...] = jnp.zeros_like(cr)
            ci[...] = jnp.zeros_like(ci)

        @pl.when((b == 0) & (t == 0))
        def _():
            dl_ref[...] = jnp.zeros_like(dl_ref)

        gr, gi = _scan_block(ds_ref[:, :S5_LB], ds_ref[:, S5_LB:], lp_ref[...], cr[...], ci[...], True)
        g_ref[:, :S5_LB] = gr.astype(g_ref.dtype)
        g_ref[:, S5_LB:] = gi.astype(g_ref.dtype)
        cr[...] = gr[0:1, :]
        ci[...] = gi[0:1, :]
        row = lax.broadcasted_iota(jnp.int32, gr.shape, 0)
        first_blk = t == n_t - 1
        h_r = jnp.where(first_blk, 0.0, h_ref[7:8, :S5_LB])
        h_i = jnp.where(first_blk, 0.0, h_ref[7:8, S5_LB:])
        sp_r = jnp.where(row == 0, h_r, pltpu.roll(s_ref[:, :S5_LB], 1, 0))
        sp_i = jnp.where(row == 0, h_i, pltpu.roll(s_ref[:, S5_LB:], 1, 0))
        dl_ref[:, :S5_LB] += jnp.sum(gr * sp_r + gi * sp_i, axis=0, keepdims=True)
        dl_ref[:, S5_LB:] += jnp.sum(gi * sp_r - gr * sp_i, axis=0, keepdims=True)

    return pl.pallas_call(
        body, name="s5_scan_bwd", grid=(S5_NJ, n_seq, n_t),
        in_specs=[pl.BlockSpec((S5_TB, blk), rows), pl.BlockSpec((S5_TB, blk), rows),
                  pl.BlockSpec((8, blk), halo), pl.BlockSpec((S5_LOG_TB, blk), lambda j, b, t: (0, j))],
        out_specs=[pl.BlockSpec((S5_TB, blk), rows), pl.BlockSpec((1, blk), lambda j, b, t: (0, j))],
        out_shape=[jax.ShapeDtypeStruct(ds.shape, BF16), jax.ShapeDtypeStruct((1, ds.shape[1]), F32)],
        scratch_shapes=[pltpu.VMEM((1, S5_LB), F32), pltpu.VMEM((1, S5_LB), F32)],
        compiler_params=_cparams(("parallel", "arbitrary", "arbitrary")),
    )(ds, s, s, lam_pow_conj)


CONV_TR = 256
CONV_CW = 512


def _shift_down(x, halo, k):
    if k == 0:
        return x
    row8 = lax.broadcasted_iota(jnp.int32, halo.shape, 0)
    rolled = pltpu.roll(x, k, 0)
    top = jnp.where(row8 < k, pltpu.roll(halo, k, 0), rolled[:8])
    if x.shape[0] == 8:
        return top
    return jnp.concatenate([top, rolled[8:]], axis=0)


def _shift_up(x, halo, k):
    if k == 0:
        return x
    n = x.shape[0]
    row8 = lax.broadcasted_iota(jnp.int32, halo.shape, 0)
    rolled = pltpu.roll(x, n - k, 0)
    bot = jnp.where(row8 >= 8 - k, pltpu.roll(halo, 8 - k, 0), rolled[n - 8:])
    if n == 8:
        return bot
    return jnp.concatenate([rolled[:n - 8], bot], axis=0)


def _conv_pre(x, halo, w, b):
    acc = b + w[SSD_CONV - 1:SSD_CONV, :] * x
    for k in range(SSD_CONV - 1):
        acc = acc + w[k:k + 1, :] * _shift_down(x, halo, SSD_CONV - 1 - k)
    return acc


def _conv_specs(seq_len, col_off):
    lt = seq_len // CONV_TR
    cb = col_off // CONV_CW
    cur = pl.BlockSpec((CONV_TR, CONV_CW), lambda j, i: (i, j + cb))
    prev = pl.BlockSpec((8, CONV_CW), lambda j, i: (jnp.maximum(i * (CONV_TR // 8) - 1, 0), j + cb))
    return lt, cur, prev


def _conv_fwd(proj, conv_w, conv_b, n_rows, seq_len):
    lt, cur, prev = _conv_specs(seq_len, OFF_XBC)

    def body(x_ref, h_ref, w_ref, b_ref, o_ref):
        halo = jnp.where(pl.program_id(1) % lt == 0, 0.0, h_ref[...])
        o_ref[...] = jax.nn.silu(_conv_pre(x_ref[...], halo, w_ref[...], b_ref[...]))

    return pl.pallas_call(
        body, name="ssd_conv_fwd", grid=(SSD_CONV_DIM // CONV_CW, n_rows // CONV_TR),
        in_specs=[cur, prev, pl.BlockSpec((SSD_CONV, CONV_CW), lambda j, i: (0, j)),
                  pl.BlockSpec((1, CONV_CW), lambda j, i: (0, j))],
        out_specs=pl.BlockSpec((CONV_TR, CONV_CW), lambda j, i: (i, j)),
        out_shape=jax.ShapeDtypeStruct((n_rows, SSD_CONV_DIM), F32),
        compiler_params=_cparams(("parallel", "parallel")),
    )(proj, proj, conv_w, conv_b)


def _conv_bwd(name, proj, d_act, conv_w, conv_b, n_rows, seq_len, col_off):
    width = d_act.shape[1]
    lt, cur, prev = _conv_specs(seq_len, OFF_XBC + col_off)
    n_blk = n_rows // CONV_TR
    cb = (OFF_XBC + col_off) // CONV_CW
    pb = col_off // CONV_CW
    nxt = pl.BlockSpec((8, CONV_CW), lambda j, i: (jnp.minimum((i + 1) * (CONV_TR // 8), n_rows // 8 - 1), j + cb))
    d_cur = pl.BlockSpec((CONV_TR, CONV_CW), lambda j, i: (i, j))
    d_nxt = pl.BlockSpec((8, CONV_CW), lambda j, i: (jnp.minimum((i + 1) * (CONV_TR // 8), n_rows // 8 - 1), j))

    def dsilu(pre):
        sg = jax.nn.sigmoid(pre)
        return sg * (1.0 + pre * (1.0 - sg))

    def body(x_ref, hp_ref, hn_ref, d_ref, dn_ref, w_ref, b_ref, dx_ref, dw_ref, db_ref):
        i = pl.program_id(1)
        x, w, b = x_ref[...], w_ref[...], b_ref[...]
        halo_p = jnp.where(i % lt == 0, 0.0, hp_ref[...])
        at_end = i % lt == lt - 1
        dpre = d_ref[...] * dsilu(_conv_pre(x, halo_p, w, b))
        pre_n = _conv_pre(hn_ref[...], x[CONV_TR - 8:, :], w, b)
        dpre_n = jnp.where(at_end, 0.0, dn_ref[...] * dsilu(pre_n))
        dx = w[SSD_CONV - 1:SSD_CONV, :] * dpre
        for k in range(SSD_CONV - 1):
            dx = dx + w[k:k + 1, :] * _shift_up(dpre, dpre_n, SSD_CONV - 1 - k)
        dx_ref[...] = dx.astype(dx_ref.dtype)

        @pl.when(i == 0)
        def _():
            dw_ref[...] = jnp.zeros_like(dw_ref)
            db_ref[...] = jnp.zeros_like(db_ref)

        for k in range(SSD_CONV):
            xs = _shift_down(x, halo_p, SSD_CONV - 1 - k)
            dw_ref[k:k + 1, :] += jnp.sum(dpre * xs, axis=0, keepdims=True)
        db_ref[...] += jnp.sum(dpre, axis=0, keepdims=True)

    return pl.pallas_call(
        body, name=name, grid=(width // CONV_CW, n_blk),
        in_specs=[cur, prev, nxt, d_cur, d_nxt,
                  pl.BlockSpec((SSD_CONV, CONV_CW), lambda j, i: (0, j + pb)),
                  pl.BlockSpec((1, CONV_CW), lambda j, i: (0, j + pb))],
        out_specs=[pl.BlockSpec((CONV_TR, CONV_CW), lambda j, i: (i, j)),
                   pl.BlockSpec((SSD_CONV, CONV_CW), lambda j, i: (0, j)),
                   pl.BlockSpec((1, CONV_CW), lambda j, i: (0, j))],
        out_shape=[jax.ShapeDtypeStruct((n_rows, width), BF16),
                   jax.ShapeDtypeStruct((SSD_CONV, width), F32), jax.ShapeDtypeStruct((1, width), F32)],
        compiler_params=_cparams(("parallel", "arbitrary")),
    )(proj, proj, proj, d_act, d_act, conv_w, conv_b)


def _ssd_chunk(xs, bm, cm, dtr, prev, dtb, alog, dsk, tri, causal):
    dt = jax.nn.softplus(dtr + dtb)
    da = dt * (-jnp.exp(alog))
    acum = jnp.dot(tri, da, precision=lax.Precision.HIGHEST, preferred_element_type=F32)
    acum_t = acum.T
    alast = acum[SSD_CHUNK - 1:SSD_CHUNK, :]
    scores = dot_nt(cm, bm)
    ys, new = [], []
    for j in range(SSD_HPG):
        xh = xs[:, j * SSD_HEADDIM:(j + 1) * SSD_HEADDIM]
        xdt = xh * dt[:, j:j + 1]
        ac = acum[:, j:j + 1]
        lmat = jnp.exp(jnp.where(causal, ac - acum_t[j:j + 1, :], -jnp.inf))
        y_diag = dot_nn(scores * lmat, xdt)
        st = dot_tn(xdt * jnp.exp(alast[:, j:j + 1] - ac), bm)
        y_off = dot_nt(cm, prev[j]) * jnp.exp(ac)
        ys.append(y_diag + y_off + xh * dsk[:, j:j + 1])
        new.append(prev[j] * jnp.exp(alast[:, j:j + 1]) + st)
    return jnp.concatenate(ys, axis=-1), new


def _ssd_consts():
    r = lax.broadcasted_iota(jnp.int32, (SSD_CHUNK, SSD_CHUNK), 0)
    c = lax.broadcasted_iota(jnp.int32, (SSD_CHUNK, SSD_CHUNK), 1)
    return (r >= c).astype(F32), r >= c


def _ssd_specs(n_c, reverse):
    def cidx(c):
        return n_c - 1 - c if reverse else c

    xs = pl.BlockSpec((SSD_CHUNK, GROUP_W), lambda g, b, c: (b * n_c + cidx(c), g))
    bm = pl.BlockSpec((SSD_CHUNK, SSD_STATE), lambda g, b, c: (b * n_c + cidx(c), SSD_WIDTH // SSD_STATE + g))
    cm = pl.BlockSpec((SSD_CHUNK, SSD_STATE), lambda g, b, c: (b * n_c + cidx(c), (SSD_WIDTH + SSD_BC) // SSD_STATE + g))
    dt = pl.BlockSpec((SSD_CHUNK, 128), lambda g, b, c: (b * n_c + cidx(c), OFF_DT // 128 + g))
    par = pl.BlockSpec((1, 128), lambda g, b, c: (0, g))
    st = pl.BlockSpec((1, 1, SSD_HPG, SSD_HEADDIM, SSD_STATE), lambda g, b, c: (b, cidx(c), g, 0, 0))
    return xs, bm, cm, dt, par, st


def _ssd_fwd(xbc_act, proj, dtb, alog, dsk, n_seq, seq_len):
    n_c = seq_len // SSD_CHUNK
    xs_s, bm_s, cm_s, dt_s, par_s, st_s = _ssd_specs(n_c, False)

    def body(xs_ref, bm_ref, cm_ref, dt_ref, dtb_ref, al_ref, dk_ref, y_ref, st_ref, state):
        @pl.when(pl.program_id(2) == 0)
        def _():
            state[...] = jnp.zeros_like(state)

        tri, causal = _ssd_consts()
        prev = [state[j] for j in range(SSD_HPG)]
        for j in range(SSD_HPG):
            st_ref[0, 0, j] = prev[j]
        y, new = _ssd_chunk(xs_ref[...], bm_ref[...], cm_ref[...], dt_ref[...], prev,
                            dtb_ref[...], al_ref[...], dk_ref[...], tri, causal)
        y_ref[...] = y
        for j in range(SSD_HPG):
            state[j] = new[j]

    return pl.pallas_call(
        body, name="ssd_fwd", grid=(SSD_GROUPS, n_seq, n_c),
        in_specs=[xs_s, bm_s, cm_s, dt_s, par_s, par_s, par_s],
        out_specs=[pl.BlockSpec((SSD_CHUNK, GROUP_W), lambda g, b, c: (b * n_c + c, g)), st_s],
        out_shape=[jax.ShapeDtypeStruct((n_seq * seq_len, SSD_WIDTH), F32),
                   jax.ShapeDtypeStruct((n_seq, n_c, SSD_HEADS, SSD_HEADDIM, SSD_STATE), F32)],
        scratch_shapes=[pltpu.VMEM((SSD_HPG, SSD_HEADDIM, SSD_STATE), F32)],
        compiler_params=_cparams(("parallel", "parallel", "arbitrary")),
    )(xbc_act, xbc_act, xbc_act, proj, dtb, alog, dsk)


def _ssd_bwd(xbc_act, proj, states, dy, dtb, alog, dsk, n_seq, seq_len):
    n_c = seq_len // SSD_CHUNK
    n_rows = n_seq * seq_len
    xs_s, bm_s, cm_s, dt_s, par_s, st_s = _ssd_specs(n_c, True)

    def rows(w):
        return pl.BlockSpec((SSD_CHUNK, w), lambda g, b, c: (b * n_c + (n_c - 1 - c), g))

    def body(xs_ref, bm_ref, cm_ref, dt_ref, st_ref, dy_ref, dtb_ref, al_ref, dk_ref,
             dxs_ref, dbm_ref, dcm_ref, ddt_ref, ddtb_ref, dal_ref, ddk_ref, dstate):
        b, c = pl.program_id(1), pl.program_id(2)

        @pl.when(c == 0)
        def _():
            dstate[...] = jnp.zeros_like(dstate)

        @pl.when((b == 0) & (c == 0))
        def _():
            ddtb_ref[...] = jnp.zeros_like(ddtb_ref)
            dal_ref[...] = jnp.zeros_like(dal_ref)
            ddk_ref[...] = jnp.zeros_like(ddk_ref)

        tri, causal = _ssd_consts()
        prev = [st_ref[0, 0, j] for j in range(SSD_HPG)]
        _, vjp = jax.vjp(
            lambda xs, bm, cm, dtr, prev, dtb, alog, dsk: _ssd_chunk(xs, bm, cm, dtr, prev, dtb, alog, dsk, tri, causal),
            xs_ref[...], bm_ref[...], cm_ref[...], dt_ref[...], prev, dtb_ref[...], al_ref[...], dk_ref[...])
        dxs, dbm, dcm, ddtr, dprev, ddtb, dal, ddk = vjp((dy_ref[...], [dstate[j] for j in range(SSD_HPG)]))
        dxs_ref[...] = dxs
        dbm_ref[...] = dbm
        dcm_ref[...] = dcm
        ddt_ref[...] = ddtr.astype(ddt_ref.dtype)
        ddtb_ref[...] += ddtb
        dal_ref[...] += dal
        ddk_ref[...] += ddk
        for j in range(SSD_HPG):
            dstate[j] = dprev[j]

    acc = pl.BlockSpec((1, 128), lambda g, b, c: (0, g))
    return pl.pallas_call(
        body, name="ssd_bwd", grid=(SSD_GROUPS, n_seq, n_c),
        in_specs=[xs_s, bm_s, cm_s, dt_s, st_s, rows(GROUP_W), par_s, par_s, par_s],
        out_specs=[rows(GROUP_W), rows(SSD_STATE), rows(SSD_STATE), rows(128), acc, acc, acc],
        out_shape=[jax.ShapeDtypeStruct((n_rows, SSD_WIDTH), F32), jax.ShapeDtypeStruct((n_rows, SSD_BC), F32),
                   jax.ShapeDtypeStruct((n_rows, SSD_BC), F32), jax.ShapeDtypeStruct((n_rows, 4 * 128), BF16),
                   jax.ShapeDtypeStruct((1, 512), F32), jax.ShapeDtypeStruct((1, 512), F32),
                   jax.ShapeDtypeStruct((1, 512), F32)],
        scratch_shapes=[pltpu.VMEM((SSD_HPG, SSD_HEADDIM, SSD_STATE), F32)],
        compiler_params=_cparams(("parallel", "arbitrary", "arbitrary")),
    )(xbc_act, xbc_act, xbc_act, proj, states, dy, dtb, alog, dsk)


def _pad_heads(v):
    return jnp.pad(v.reshape(SSD_GROUPS, SSD_HPG), ((0, 0), (0, 128 - SSD_HPG))).reshape(1, SSD_GROUPS * 128)


def _unpad_heads(v):
    return v.reshape(SSD_GROUPS, 128)[:, :SSD_HPG].reshape(1, SSD_HEADS)


def _state_cols(v):
    re, im = v
    lead = re.shape[:-1]
    re = re.reshape(lead + (S5_NJ, 1, S5_LB))
    im = im.reshape(lead + (S5_NJ, 1, S5_LB))
    return jnp.concatenate([re, im], axis=-2).reshape(lead + (2 * S5_N,))


def _state_uncols(v):
    lead = v.shape[:-1]
    v = v.reshape(lead + (S5_NJ, 2, S5_LB))
    return v[..., 0, :].reshape(lead + (S5_N,)), v[..., 1, :].reshape(lead + (S5_N,))


def _block_diag(w2):
    gh = S5_GROUPS * S5_GROUP
    rg = (jnp.arange(gh) // S5_GROUP)[:, None, None]
    cg = jnp.arange(S5_GROUPS)[None, :, None]
    return jnp.where(rg == cg, w2[:, None, :], 0.0).reshape(gh, S5_N)


def _block_diag_take(wd):
    gh = S5_GROUPS * S5_GROUP
    w4 = wd.reshape(S5_GROUPS, S5_GROUP, S5_GROUPS, S5_STATE)
    idx = jnp.arange(S5_GROUPS)
    return w4[idx, :, idx, :].reshape(gh, S5_STATE)


def _pad_w_in(w_in):
    u5, z5, zs, xbc, dt, gl = (w_in[:, 0:512], w_in[:, 512:1024], w_in[:, 1024:2560], w_in[:, 2560:5120],
                               w_in[:, 5120:5144], w_in[:, 5144:7192])
    dtp = jnp.pad(dt.reshape(-1, SSD_GROUPS, SSD_HPG), ((0, 0), (0, 0), (0, 128 - SSD_HPG))).reshape(-1, 512)
    return jnp.concatenate([xbc, u5, zs, z5, gl, dtp], axis=1)


def _unpad_w_in(wp):
    dt = wp[:, OFF_DT:].reshape(-1, SSD_GROUPS, 128)[:, :, :SSD_HPG].reshape(-1, SSD_HEADS)
    return jnp.concatenate([wp[:, OFF_U5:OFF_U5 + 512], wp[:, OFF_Z5:OFF_Z5 + 512], wp[:, OFF_ZS:OFF_ZS + 1536],
                            wp[:, OFF_XBC:OFF_XBC + 2560], dt, wp[:, OFF_G5:OFF_G5 + 2048]], axis=1)


def _local_step(x, p, tgt, w):
    n_seq, seq_len, _ = x.shape
    n_rows = n_seq * seq_len
    tr = 256
    x2 = x.reshape(n_rows, D_MODEL)
    p2 = p.reshape(n_rows, PLE_DIM)
    t2 = tgt.reshape(n_rows, D_MODEL)
    row = functools.partial(_rowwise, n_rows=n_rows, tr=tr)

    w_pad = _pad_w_in(w["w_in"])
    norm_w = w["norm_w"].reshape(1, D_MODEL)
    ple_norm_w = w["ple_norm_w"].reshape(1, D_MODEL)
    final_w = w["final_norm_w"].reshape(1, D_MODEL)
    s5_d = w["s5_d"].reshape(1, S5_WIDTH)
    b_glu = w["s5_b_glu"].reshape(1, S5_WIDTH)
    conv_w = w["ssd_conv_w"].reshape(SSD_CONV, SSD_CONV_DIM)
    conv_b = w["ssd_conv_b"].reshape(1, SSD_CONV_DIM)
    ssd_norm_w = w["ssd_norm_w"].reshape(1, SSD_WIDTH)
    dtb, alog, dsk = (_pad_heads(w[k].reshape(1, SSD_HEADS)) for k in ("ssd_dt_bias", "ssd_a_log", "ssd_d"))

    gh = S5_GROUPS * S5_GROUP
    a_re = w["s5_a_re"].reshape(S5_GROUPS, S5_STATE)
    a_im = w["s5_a_im"].reshape(S5_GROUPS, S5_STATE)
    log_step = w["s5_log_step"].reshape(S5_GROUPS, 1)
    b_re2 = jnp.transpose(w["s5_b_re"].reshape(S5_GROUPS, S5_STATE, S5_GROUP), (0, 2, 1)).reshape(gh, S5_STATE)
    b_im2 = jnp.transpose(w["s5_b_im"].reshape(S5_GROUPS, S5_STATE, S5_GROUP), (0, 2, 1)).reshape(gh, S5_STATE)
    expand = (jnp.arange(gh)[:, None] // S5_GROUP == jnp.arange(S5_GROUPS)[None, :]).astype(F32)
    pow_re, pow_im, bb_re2, bb_im2 = _s5_params_fwd(a_re, a_im, log_step, b_re2, b_im2, expand)
    lam_pow = _state_cols((pow_re.reshape(S5_LOG_TB, S5_N), pow_im.reshape(S5_LOG_TB, S5_N)))
    lam_pow_conj = _state_cols((pow_re.reshape(S5_LOG_TB, S5_N), -pow_im.reshape(S5_LOG_TB, S5_N)))
    bb_dense = _state_cols((_block_diag(bb_re2), _block_diag(bb_im2))).astype(BF16)
    c_re2 = w["s5_c_re"].reshape(gh, S5_STATE)
    c_im2 = w["s5_c_im"].reshape(gh, S5_STATE)
    c_dense = _state_cols((_block_diag(c_re2), -_block_diag(c_im2))).astype(BF16)

    (hn,) = row("rms_in", lambda r, q: ([_rms(r[0], q[0])], []), row_ins=[(x2, 0, D_MODEL)],
                par_ins=[(norm_w, 0, D_MODEL)], row_outs=[(D_MODEL, BF16)])
    proj = _matmul("mm_proj", hn, w_pad)
    bu = _matmul("mm_s5_bu", proj, bb_dense, a_win=(OFF_U5, S5_WIDTH))
    s = _s5_scan_fwd(bu, lam_pow, n_seq, seq_len)
    yc = _matmul("mm_s5_y", s, c_dense, tb=True)
    (ge,) = row("s5_gelu", lambda r, q: ([jax.nn.gelu(r[0] + q[0] * r[1])], []),
                row_ins=[(yc, 0, S5_WIDTH), (proj, OFF_U5, S5_WIDTH)], par_ins=[(s5_d, 0, S5_WIDTH)],
                row_outs=[(S5_WIDTH, BF16)])
    tg = _matmul("mm_s5_glu", ge, w["s5_w_glu"])
    s5_rows = [(yc, 0, S5_WIDTH), (proj, OFF_U5, S5_WIDTH), (proj, OFF_Z5, S5_WIDTH), (tg, 0, S5_WIDTH)]
    s5_pars = [(s5_d, 0, S5_WIDTH), (b_glu, 0, S5_WIDTH)]
    (y5,) = row("s5_out", lambda r, q: ([_s5_out(*r, *q)], []), row_ins=s5_rows, par_ins=s5_pars,
                row_outs=[(S5_WIDTH, BF16)])

    xbc_act = _conv_fwd(proj, conv_w, conv_b, n_rows, seq_len)
    y_ssd, states = _ssd_fwd(xbc_act, proj, dtb, alog, dsk, n_seq, seq_len)
    gn_rows = [(y_ssd, 0, SSD_WIDTH), (proj, OFF_ZS, SSD_WIDTH)]
    (yss,) = row("ssd_gate", lambda r, q: ([_gated_norm(r[0], r[1], q[0])], []), row_ins=gn_rows,
                 par_ins=[(ssd_norm_w, 0, SSD_WIDTH)], row_outs=[(SSD_WIDTH, BF16)])

    m5 = _matmul("mm_br_s5", y5, w["w_br_s5"])
    ms = _matmul("mm_br_ssd", yss, w["w_br_ssd"])
    mg_rows = [(proj, OFF_G5, D_MODEL), (proj, OFF_GS, D_MODEL), (m5, 0, D_MODEL), (ms, 0, D_MODEL)]
    (merged,) = row("merge", lambda r, q: ([_merge(*r)], []), row_ins=mg_rows, par_ins=[], row_outs=[(D_MODEL, BF16)])
    o = _matmul("mm_out", merged, w["w_out"])

    def resid_fn(r, q):
        h1 = r[0] + r[1]
        return [h1, _rms(h1, q[0])], []

    h1, hp = row("resid_ple_norm", resid_fn, row_ins=[(x2, 0, D_MODEL), (o, 0, D_MODEL)],
                 par_ins=[(ple_norm_w, 0, D_MODEL)], row_outs=[(D_MODEL, F32), (D_MODEL, BF16)])
    pgl = _matmul("mm_ple_gate", hp, w["w_ple_gate"])
    pp = _matmul("mm_ple_proj", p2, w["w_ple_proj"])

    def head_fn(r, q):
        h1_, pgl_, pp_, tgt_ = r
        loss, vjp = jax.vjp(lambda a, b, c, f: _head_loss(a, b, c, f, tgt_), h1_, pgl_, pp_, q[0])
        dh1_, dpgl_, dpp_, dfw_ = vjp(jnp.ones_like(loss))
        return [dh1_, dpgl_, dpp_], [loss, dfw_]

    dh2, dpgl, dpp, loss_acc, d_final_w = row(
        "head", head_fn, row_ins=[(h1, 0, D_MODEL), (pgl, 0, D_MODEL), (pp, 0, D_MODEL), (t2, 0, D_MODEL)],
        par_ins=[(final_w, 0, D_MODEL)], row_outs=[(D_MODEL, F32), (D_MODEL, BF16), (D_MODEL, BF16)],
        acc_outs=[(1, 128), (1, D_MODEL)])
    loss = loss_acc[0, 0]

    g = {}
    g["final_norm_w"] = d_final_w
    g["w_ple_gate"] = _matmul("mm_d_w_ple_gate", hp, dpgl, ta=True)
    g["w_ple_proj"] = _matmul("mm_d_w_ple_proj", p2, dpp, ta=True)
    dhp = _matmul("mm_d_hp", dpgl, w["w_ple_gate"], tb=True)

    def ple_norm_bwd(r, q):
        h1_, dhp_, dh2_ = r
        _, vjp = jax.vjp(_rms, h1_, q[0])
        dh, dw = vjp(dhp_)
        dh = dh + dh2_
        return [dh, dh], [dw]

    dh1, dh1_b, g["ple_norm_w"] = row(
        "ple_norm_bwd", ple_norm_bwd, row_ins=[(h1, 0, D_MODEL), (dhp, 0, D_MODEL), (dh2, 0, D_MODEL)],
        par_ins=[(ple_norm_w, 0, D_MODEL)], row_outs=[(D_MODEL, F32), (D_MODEL, BF16)], acc_outs=[(1, D_MODEL)])
    g["w_out"] = _matmul("mm_d_w_out", merged, dh1_b, ta=True)
    dmerged = _matmul("mm_d_merged", dh1_b, w["w_out"], tb=True)

    def merge_bwd(r, q):
        _, vjp = jax.vjp(_merge, *r[:4])
        return list(vjp(r[4])), []

    dg5, dgs, dm5, dms = row("merge_bwd", merge_bwd, row_ins=mg_rows + [(dmerged, 0, D_MODEL)], par_ins=[],
                             row_outs=[(D_MODEL, BF16)] * 4)
    g["w_br_s5"] = _matmul("mm_d_w_br_s5", y5, dm5, ta=True)
    g["w_br_ssd"] = _matmul("mm_d_w_br_ssd", yss, dms, ta=True)
    dy5 = _matmul("mm_d_y5", dm5, w["w_br_s5"], tb=True)
    dyss = _matmul("mm_d_yss", dms, w["w_br_ssd"], tb=True)

    def s5_out_bwd_a(r, q):
        yc_, u_, z_, t_, dy_ = r
        d_, bg_ = q
        ge_ = jax.nn.gelu(yc_ + d_ * u_)
        _, vjp = jax.vjp(lambda a, z, t, b: a * jax.nn.sigmoid(t + b) * jax.nn.silu(z), ge_, z_, t_, bg_)
        dge, dz, dt_, dbg = vjp(dy_)
        return [dge, dz, dt_], [dbg]

    dge_a, dz5, dtg, g["s5_b_glu"] = row(
        "s5_out_bwd_a", s5_out_bwd_a, row_ins=s5_rows + [(dy5, 0, S5_WIDTH)], par_ins=s5_pars,
        row_outs=[(S5_WIDTH, F32), (S5_WIDTH, BF16), (S5_WIDTH, BF16)], acc_outs=[(1, S5_WIDTH)])
    g["s5_w_glu"] = _matmul("mm_d_w_glu", ge, dtg, ta=True)
    dge_b = _matmul("mm_d_ge", dtg, w["s5_w_glu"], tb=True)

    def s5_out_bwd_b(r, q):
        yc_, u_, da_, db_ = r
        _, vjp = jax.vjp(lambda yc, u, d: jax.nn.gelu(yc + d * u), yc_, u_, q[0])
        dyc_, du_, dd_ = vjp(da_ + db_)
        return [dyc_, du_], [dd_]

    dyc, du5_a, g["s5_d"] = row(
        "s5_out_bwd_b", s5_out_bwd_b,
        row_ins=[(yc, 0, S5_WIDTH), (proj, OFF_U5, S5_WIDTH), (dge_a, 0, S5_WIDTH), (dge_b, 0, S5_WIDTH)],
        par_ins=[(s5_d, 0, S5_WIDTH)], row_outs=[(S5_WIDTH, BF16), (S5_WIDTH, F32)], acc_outs=[(1, S5_WIDTH)])
    d_c_dense = _matmul("mm_d_c", dyc, s, ta=True)
    ds = _matmul("mm_d_s", dyc, c_dense)
    dbu, d_lam = _s5_scan_bwd(ds, s, lam_pow_conj, n_seq, seq_len)
    d_bb_dense = _matmul("mm_d_bb", proj, dbu, ta=True, a_win=(OFF_U5, S5_WIDTH))
    du5_b = _matmul("mm_d_u5", dbu, bb_dense, tb=True)
    (du5,) = row("s5_du", lambda r, q: ([r[0] + r[1]], []), row_ins=[(du5_a, 0, S5_WIDTH), (du5_b, 0, S5_WIDTH)],
                 par_ins=[], row_outs=[(S5_WIDTH, BF16)])

    d_lr, d_li = _state_uncols(d_lam)
    d_bbr, d_bbi = (_block_diag_take(v) for v in _state_uncols(d_bb_dense))
    d_are, d_aim, d_ls, d_br2, d_bi2 = _s5_params_bwd(
        a_re, a_im, log_step, b_re2, b_im2, expand,
        d_lr.reshape(S5_GROUPS, S5_STATE), d_li.reshape(S5_GROUPS, S5_STATE), d_bbr, d_bbi)
    g["s5_a_re"], g["s5_a_im"], g["s5_log_step"] = d_are, d_aim, d_ls
    g["s5_b_re"] = jnp.transpose(d_br2.reshape(S5_GROUPS, S5_GROUP, S5_STATE), (0, 2, 1))
    g["s5_b_im"] = jnp.transpose(d_bi2.reshape(S5_GROUPS, S5_GROUP, S5_STATE), (0, 2, 1))
    d_cr, d_ci = (_block_diag_take(v) for v in _state_uncols(d_c_dense))
    g["s5_c_re"], g["s5_c_im"] = d_cr, -d_ci

    def gate_bwd(r, q):
        _, vjp = jax.vjp(_gated_norm, r[0], r[1], q[0])
        dy_, dz_, dw_ = vjp(r[2])
        return [dy_, dz_], [dw_]

    dy_ssd, dzs, g["ssd_norm_w"] = row(
        "ssd_gate_bwd", gate_bwd, row_ins=gn_rows + [(dyss, 0, SSD_WIDTH)], par_ins=[(ssd_norm_w, 0, SSD_WIDTH)],
        row_outs=[(SSD_WIDTH, F32), (SSD_WIDTH, BF16)], acc_outs=[(1, SSD_WIDTH)])
    dxs, dbm, dcm, ddt, d_dtb, d_alog, d_dsk = _ssd_bwd(xbc_act, proj, states, dy_ssd, dtb, alog, dsk, n_seq, seq_len)
    g["ssd_dt_bias"], g["ssd_a_log"], g["ssd_d"] = _unpad_heads(d_dtb), _unpad_heads(d_alog), _unpad_heads(d_dsk)
    conv_parts = [_conv_bwd("ssd_conv_bwd_x", proj, dxs, conv_w, conv_b, n_rows, seq_len, 0),
                  _conv_bwd("ssd_conv_bwd_b", proj, dbm, conv_w, conv_b, n_rows, seq_len, SSD_WIDTH),
                  _conv_bwd("ssd_conv_bwd_c", proj, dcm, conv_w, conv_b, n_rows, seq_len, SSD_WIDTH + SSD_BC)]
    g["ssd_conv_w"] = jnp.concatenate([c[1] for c in conv_parts], axis=1)
    g["ssd_conv_b"] = jnp.concatenate([c[2] for c in conv_parts], axis=1)

    dproj = jnp.concatenate([c[0] for c in conv_parts] + [du5, dzs, dz5, dg5, dgs, ddt], axis=1)
    g["w_in"] = _unpad_w_in(_matmul("mm_d_w_in", hn, dproj, ta=True))
    dhn = _matmul("mm_d_hn", dproj, w_pad, tb=True)

    def norm_bwd(r, q):
        x_, dhn_, dh1_ = r
        _, vjp = jax.vjp(_rms, x_, q[0])
        dx_, dw_ = vjp(dhn_)
        return [dx_ + dh1_], [dw_]

    dx, g["norm_w"] = row("rms_in_bwd", norm_bwd, row_ins=[(x2, 0, D_MODEL), (dhn, 0, D_MODEL), (dh1, 0, D_MODEL)],
                          par_ins=[(norm_w, 0, D_MODEL)], row_outs=[(D_MODEL, F32)], acc_outs=[(1, D_MODEL)])
    return loss, dx.reshape(x.shape), g


HBM = pl.BlockSpec(memory_space=pltpu.HBM)


def _chip_index(x, y):
    return 2 * x + y


def _gather_chips(shards):
    n = len(shards)

    def body(*refs):
        ins, outs = refs[:n], refs[n:2 * n]
        send_sems, recv_sems, local_sems = refs[2 * n:]
        x, y, c = lax.axis_index("x"), lax.axis_index("y"), lax.axis_index("c")
        me = _chip_index(x, y)
        peers = [(1 - x, y), (x, 1 - y), (1 - x, 1 - y)]
        copies = []
        for t in range(n):
            loc = pltpu.make_async_copy(ins[t], outs[t].at[me], local_sems.at[t])
            loc.start()
            copies.append(loc)
        sends = []
        for t in range(n):
            for k, (px, py) in enumerate(peers):
                cp = pltpu.make_async_remote_copy(
                    src_ref=ins[t], dst_ref=outs[t].at[me], send_sem=send_sems.at[t, k], recv_sem=recv_sems.at[t, k],
                    device_id=(px, py, c), device_id_type=MESH)
                cp.start()
                sends.append(cp)
        for t in range(n):
            for k, (px, py) in enumerate(peers):
                pltpu.make_async_remote_copy(
                    src_ref=ins[t], dst_ref=outs[t].at[_chip_index(px, py)], send_sem=send_sems.at[t, k],
                    recv_sem=recv_sems.at[t, k], device_id=(px, py, c), device_id_type=MESH).wait_recv()
        for cp in sends:
            cp.wait_send()
        for cp in copies:
            cp.wait()

    return pl.pallas_call(
        body, name="gather_weights",
        in_specs=[HBM] * n, out_specs=[HBM] * n,
        out_shape=[jax.ShapeDtypeStruct((N_CHIPS,) + a.shape, a.dtype) for a in shards],
        scratch_shapes=[pltpu.SemaphoreType.DMA((n, 3)), pltpu.SemaphoreType.DMA((n, 3)), pltpu.SemaphoreType.DMA((n,))],
    )(*shards)


def _scatter_chips(slotted, small):
    n = len(slotted)

    def body(*refs):
        ins, sm_in = refs[:n], refs[n]
        outs, sm_out = refs[n + 1:2 * n + 1], refs[2 * n + 1]
        send_sems, recv_sems, local_sems, sm_send, sm_recv = refs[2 * n + 2:]
        x, y, c = lax.axis_index("x"), lax.axis_index("y"), lax.axis_index("c")
        me = _chip_index(x, y)
        dev = 2 * me + c
        peers = [(1 - x, y), (x, 1 - y), (1 - x, 1 - y)]
        local = []
        for t in range(n):
            cp = pltpu.make_async_copy(ins[t].at[me], outs[t].at[me], local_sems.at[t])
            cp.start()
            local.append(cp)
        cp = pltpu.make_async_copy(sm_in, sm_out.at[dev], local_sems.at[n])
        cp.start()
        local.append(cp)
        sends = []
        rel = [(fx, fy, fc) for fx in (0, 1) for fy in (0, 1) for fc in (0, 1)][1:]
        for k, (fx, fy, fc) in enumerate(rel):
            cp = pltpu.make_async_remote_copy(
                src_ref=sm_in, dst_ref=sm_out.at[dev], send_sem=sm_send.at[k], recv_sem=sm_recv.at[k],
                device_id=(x ^ fx, y ^ fy, c ^ fc), device_id_type=MESH)
            cp.start()
            sends.append(cp)
        for t in range(n):
            for k, (px, py) in enumerate(peers):
                cp = pltpu.make_async_remote_copy(
                    src_ref=ins[t].at[_chip_index(px, py)], dst_ref=outs[t].at[me], send_sem=send_sems.at[t, k],
                    recv_sem=recv_sems.at[t, k], device_id=(px, py, c), device_id_type=MESH)
                cp.start()
                sends.append(cp)
        for k, (fx, fy, fc) in enumerate(rel):
            src_dev = 4 * (x ^ fx) + 2 * (y ^ fy) + (c ^ fc)
            pltpu.make_async_remote_copy(
                src_ref=sm_in, dst_ref=sm_out.at[src_dev], send_sem=sm_send.at[k], recv_sem=sm_recv.at[k],
                device_id=(x ^ fx, y ^ fy, c ^ fc), device_id_type=MESH).wait_recv()
        for t in range(n):
            for k, (px, py) in enumerate(peers):
                pltpu.make_async_remote_copy(
                    src_ref=ins[t].at[me], dst_ref=outs[t].at[_chip_index(px, py)], send_sem=send_sems.at[t, k],
                    recv_sem=recv_sems.at[t, k], device_id=(px, py, c), device_id_type=MESH).wait_recv()
        for cp in sends:
            cp.wait_send()
        for cp in local:
            cp.wait()

    return pl.pallas_call(
        body, name="scatter_grads",
        in_specs=[HBM] * (n + 1), out_specs=[HBM] * (n + 1),
        out_shape=[jax.ShapeDtypeStruct(a.shape, a.dtype) for a in slotted]
        + [jax.ShapeDtypeStruct((N_DEV,) + small.shape, small.dtype)],
        scratch_shapes=[pltpu.SemaphoreType.DMA((n, 3)), pltpu.SemaphoreType.DMA((n, 3)),
                        pltpu.SemaphoreType.DMA((n + 1,)), pltpu.SemaphoreType.DMA((7,)), pltpu.SemaphoreType.DMA((7,))],
    )(*slotted, small)


def _swap_sibling(parts):
    n = len(parts)

    def body(*refs):
        ins, outs = refs[:n], refs[n:2 * n]
        send_sems, recv_sems = refs[2 * n:]
        x, y, c = lax.axis_index("x"), lax.axis_index("y"), lax.axis_index("c")
        cps = []
        for t in range(n):
            cp = pltpu.make_async_remote_copy(
                src_ref=ins[t], dst_ref=outs[t], send_sem=send_sems.at[t], recv_sem=recv_sems.at[t],
                device_id=(x, y, 1 - c), device_id_type=MESH)
            cp.start()
            cps.append(cp)
        for cp in cps:
            cp.wait_recv()
        for cp in cps:
            cp.wait_send()

    return pl.pallas_call(
        body, name="swap_sibling",
        in_specs=[HBM] * n, out_specs=[HBM] * n,
        out_shape=[jax.ShapeDtypeStruct(a.shape, a.dtype) for a in parts],
        scratch_shapes=[pltpu.SemaphoreType.DMA((n,)), pltpu.SemaphoreType.DMA((n,))],
    )(*parts)


def _sum_slots(name, a):
    k, r, c = a.shape
    tr = _pick(r, (256, 128, 64, 32, 16, 8))

    def body(a_ref, o_ref):
        acc = a_ref[0]
        for i in range(1, k):
            acc = acc + a_ref[i]
        o_ref[...] = acc

    return pl.pallas_call(
        body, name=name, grid=(r // tr,),
        in_specs=[pl.BlockSpec((k, tr, c), lambda i: (0, i, 0))],
        out_specs=pl.BlockSpec((tr, c), lambda i: (i, 0)),
        out_shape=jax.ShapeDtypeStruct((r, c), a.dtype),
        compiler_params=_cparams(("parallel",)),
    )(a)


def _adamw(name, w, m, v, g_parts):
    r, c = w.shape
    tr = _pick(r, (256, 128, 64, 32, 16, 8))
    ng = len(g_parts)
    c1 = 1.0 - ADAM_B1 ** ADAM_STEP
    c2 = 1.0 - ADAM_B2 ** ADAM_STEP

    def body(*refs):
        w_ref, m_ref, v_ref = refs[:3]
        g_refs = refs[3:3 + ng]
        go_ref, d_ref, mo_ref, vo_ref = refs[3 + ng:]
        g = g_refs[0][...]
        for gr in g_refs[1:]:
            g = g + gr[...]
        m_new = ADAM_B1 * m_ref[...] + (1.0 - ADAM_B1) * g
        v_new = ADAM_B2 * v_ref[...] + (1.0 - ADAM_B2) * (g * g)
        go_ref[...] = g
        mo_ref[...] = m_new
        vo_ref[...] = v_new
        d_ref[...] = -ADAM_LR * ((m_new / c1) / (jnp.sqrt(v_new / c2) + ADAM_EPS) + ADAM_WD * w_ref[...])

    spec = pl.BlockSpec((tr, c), lambda i: (i, 0))
    return pl.pallas_call(
        body, name=name, grid=(r // tr,),
        in_specs=[spec] * (3 + ng), out_specs=[spec] * 4,
        out_shape=[jax.ShapeDtypeStruct((r, c), F32)] * 4,
        compiler_params=_cparams(("parallel",)),
    )(w, m, v, *g_parts)


WEIGHTS = ['norm_w', 'w_in', 's5_a_re', 's5_a_im', 's5_b_re', 's5_b_im', 's5_c_re', 's5_c_im', 's5_d', 's5_log_step',
           's5_w_glu', 's5_b_glu', 'ssd_conv_w', 'ssd_conv_b', 'ssd_dt_bias', 'ssd_a_log', 'ssd_d', 'ssd_norm_w',
           'w_br_s5', 'w_br_ssd', 'w_out', 'ple_norm_w', 'w_ple_gate', 'w_ple_proj', 'final_norm_w']
SHARDED = {'w_in': ((1024, IN_PROJ_DIM), 1), 's5_w_glu': ((512, 512), 0), 'ssd_conv_w': ((SSD_CONV, SSD_CONV_DIM), 1),
           'w_br_s5': ((512, 1024), 1), 'w_br_ssd': ((1536, 1024), 0), 'w_out': ((1024, 1024), 0),
           'w_ple_gate': ((1024, 1024), 0), 'w_ple_proj': ((256, 1024), 1)}
SMALL = [n for n in WEIGHTS if n not in SHARDED]


def _shard2d(name, a):
    (r, c), ax = SHARDED[name]
    return a.reshape((r // N_CHIPS, c) if ax == 0 else (r, c // N_CHIPS))


def _unslot(name, a4):
    (r, c), ax = SHARDED[name]
    if ax == 0:
        return a4.reshape(r, c)
    return jnp.transpose(a4, (1, 0, 2)).reshape(r, c)


def _slot(name, full):
    (r, c), ax = SHARDED[name]
    if ax == 0:
        return full.reshape(N_CHIPS, r // N_CHIPS, c)
    return jnp.transpose(full.reshape(r, N_CHIPS, c // N_CHIPS), (1, 0, 2))


def _pack_small(vals):
    flat = jnp.concatenate([v.reshape(-1).astype(F32) for v in vals])
    rows = -(-flat.shape[0] // (8 * 128)) * 8
    return jnp.pad(flat, (0, rows * 128 - flat.shape[0])).reshape(rows, 128)


def _unpack_small(packed, shapes):
    flat = packed.reshape(-1)
    out, off = [], 0
    for sh in shapes:
        n = math.prod(sh)
        out.append(flat[off:off + n].reshape(sh))
        off += n
    return out


def kernel(x, p, norm_w, w_in, s5_a_re, s5_a_im, s5_b_re, s5_b_im, s5_c_re, s5_c_im, s5_d, s5_log_step, s5_w_glu, s5_b_glu, ssd_conv_w, ssd_conv_b, ssd_dt_bias, ssd_a_log, ssd_d, ssd_norm_w, w_br_s5, w_br_ssd, w_out, ple_norm_w, w_ple_gate, w_ple_proj, final_norm_w, loss_target, m_norm_w, m_w_in, m_s5_a_re, m_s5_a_im, m_s5_b_re, m_s5_b_im, m_s5_c_re, m_s5_c_im, m_s5_d, m_s5_log_step, m_s5_w_glu, m_s5_b_glu, m_ssd_conv_w, m_ssd_conv_b, m_ssd_dt_bias, m_ssd_a_log, m_ssd_d, m_ssd_norm_w, m_w_br_s5, m_w_br_ssd, m_w_out, m_ple_norm_w, m_w_ple_gate, m_w_ple_proj, m_final_norm_w, v_norm_w, v_w_in, v_s5_a_re, v_s5_a_im, v_s5_b_re, v_s5_b_im, v_s5_c_re, v_s5_c_im, v_s5_d, v_s5_log_step, v_s5_w_glu, v_s5_b_glu, v_ssd_conv_w, v_ssd_conv_b, v_ssd_dt_bias, v_ssd_a_log, v_ssd_d, v_ssd_norm_w, v_w_br_s5, v_w_br_ssd, v_w_out, v_ple_norm_w, v_w_ple_gate, v_w_ple_proj, v_final_norm_w):
    args = locals()
    wl = {n: args[n] for n in WEIGHTS}
    ml = {n: args["m_" + n] for n in WEIGHTS}
    vl = {n: args["v_" + n] for n in WEIGHTS}
    big = list(SHARDED)

    gathered = _gather_chips([_shard2d(n, wl[n]).astype(BF16 if n != 'ssd_conv_w' else F32) for n in big])
    full = {n: wl[n] for n in SMALL}
    for n, a4 in zip(big, gathered, strict=True):
        full[n] = _unslot(n, a4)

    loss, grad_x, g = _local_step(x, p[0], loss_target, full)

    small_shapes = [(1, 1)] + [wl[n].shape for n in SMALL]
    small_pack = _pack_small([loss] + [g[n] for n in SMALL])
    recv = _scatter_chips([_slot(n, g[n]) for n in big], small_pack)
    chip_sums = [_sum_slots("sum_chips_" + n, r) for n, r in zip(big, recv[:-1], strict=True)]
    small_sum = _sum_slots("sum_small", recv[-1])
    sib_sums = _swap_sibling(chip_sums)

    out_g, out_d, out_m, out_v = {}, {}, {}, {}
    for n, own, sib in zip(big, chip_sums, sib_sums, strict=True):
        res = _adamw("adamw_" + n, _shard2d(n, wl[n]), _shard2d(n, ml[n]), _shard2d(n, vl[n]), [own, sib])
        out_g[n], out_d[n], out_m[n], out_v[n] = (r.reshape(wl[n].shape) for r in res)
    sm = _unpack_small(small_sum, small_shapes)
    loss_total = sm[0].reshape(())
    small_g = _pack_small(sm[1:])
    res = _adamw("adamw_small", _pack_small([wl[n] for n in SMALL]), _pack_small([ml[n] for n in SMALL]),
                 _pack_small([vl[n] for n in SMALL]), [small_g])
    shapes = [wl[n].shape for n in SMALL]
    for dst, packed in zip((out_g, out_d, out_m, out_v), res, strict=True):
        for n, val in zip(SMALL, _unpack_small(packed, shapes), strict=True):
            dst[n] = val

    return (loss_total, grad_x, *[out_g[n] for n in WEIGHTS], *[out_d[n] for n in WEIGHTS],
            *[out_m[n] for n in WEIGHTS], *[out_v[n] for n in WEIGHTS])
```

```python
import functools
import math

import jax
import jax.numpy as jnp
from jax import lax
from jax.experimental import pallas as pl
from jax.experimental.pallas import tpu as pltpu

F32 = jnp.float32
BF16 = jnp.bfloat16
MESH = pl.DeviceIdType.MESH

D_MODEL = 1024
PLE_DIM = 256
RMS_EPS = 1e-6
S5_WIDTH = 512
S5_GROUP = 16
S5_GROUPS = 32
S5_STATE = 64
S5_N = S5_GROUPS * S5_STATE
S5_LB = 512
S5_NJ = S5_N // S5_LB
S5_TB = 256
S5_LOG_TB = 8
SSD_WIDTH = 1536
SSD_HEADDIM = 64
SSD_HEADS = 24
SSD_GROUPS = 4
SSD_HPG = 6
SSD_STATE = 128
SSD_CONV = 4
SSD_CHUNK = 128
SSD_BC = 512
SSD_CONV_DIM = 2560
GROUP_W = SSD_WIDTH // SSD_GROUPS
N_CHIPS = 4
N_DEV = 8

OFF_XBC, OFF_U5, OFF_ZS, OFF_Z5, OFF_G5, OFF_GS, OFF_DT = 0, 2560, 3072, 4608, 5120, 6144, 7168
PROJ_W = 7680
IN_PROJ_DIM = 7192

ADAM_LR, ADAM_B1, ADAM_B2, ADAM_EPS, ADAM_WD, ADAM_STEP = 0.001, 0.9, 0.999, 1e-08, 0.01, 10

VMEM_LIMIT = 56 * 1024 * 1024


def _pick(n, cands):
    for c in cands:
        if n % c == 0:
            return c
    return n


ROW_BLOCK_BYTES = 8 * 1024 * 1024


def _row_tile(r, bytes_per_row):
    for t in (r, 4096, 2048, 1024, 512, 256, 128, 64, 32, 16, 8):
        if t <= r and r % t == 0 and t * bytes_per_row <= ROW_BLOCK_BYTES:
            return t
    return r


def _cparams(sem):
    return pltpu.CompilerParams(dimension_semantics=sem, vmem_limit_bytes=VMEM_LIMIT)


def _dg(a, b, ca, cb):
    return lax.dot_general(a.astype(BF16), b.astype(BF16), (((ca,), (cb,)), ((), ())), preferred_element_type=F32)


@jax.custom_vjp
def dot_nn(a, b):
    return _dg(a, b, 1, 0)


@jax.custom_vjp
def dot_nt(a, b):
    return _dg(a, b, 1, 1)


@jax.custom_vjp
def dot_tn(a, b):
    return _dg(a, b, 0, 0)


dot_nn.defvjp(lambda a, b: (_dg(a, b, 1, 0), (a, b)), lambda r, g: (_dg(g, r[1], 1, 1), _dg(r[0], g, 0, 0)))
dot_nt.defvjp(lambda a, b: (_dg(a, b, 1, 1), (a, b)), lambda r, g: (_dg(g, r[1], 1, 0), _dg(g, r[0], 0, 0)))
dot_tn.defvjp(lambda a, b: (_dg(a, b, 0, 0), (a, b)), lambda r, g: (_dg(r[1], g, 1, 1), _dg(r[0], g, 1, 0)))


MM_VMEM_BUDGET = 30 * 1024 * 1024


def _mm_tiles(m, n, k, sa, sb, so):
    best, best_key = None, None
    for tm in (1024, 512, 256, 128, 64, 32, 16, 8):
        if m % tm:
            continue
        for tn in (2048, 1536, 1280, 1024, 768, 640, 512, 384, 256, 128):
            if n % tn:
                continue
            for tk in (k, 2048, 1536, 1280, 1024, 768, 512, 256, 128):
                if k % tk or tk > max(k, 128) or (tk == k and k > 2048 and k % 128 == 0):
                    continue
                need = 2 * (tm * tk * sa + tk * tn * sb + tm * tn * so) + (tm * tn * 4 if tk < k else 0)
                if need > MM_VMEM_BUDGET:
                    continue
                key = (tm * tn * tk, tk)
                if best_key is None or key > best_key:
                    best, best_key = (tm, tn, tk), key
    assert best is not None, (m, n, k)
    return best


def _matmul(name, a, b, *, ta=False, tb=False, a_win=None, out_dtype=F32):
    a_off, a_w = a_win if a_win is not None else (0, a.shape[1])
    if ta:
        kdim, m = a.shape[0], a_w
    else:
        m, kdim = a.shape[0], a_w
    n = b.shape[0] if tb else b.shape[1]
    assert (b.shape[1] if tb else b.shape[0]) == kdim, (name, a.shape, b.shape)
    tm, tn, tk = _mm_tiles(m, n, kdim, a.dtype.itemsize, b.dtype.itemsize, jnp.dtype(out_dtype).itemsize)
    nk = kdim // tk
    if ta:
        assert a_off % tm == 0
        a_spec = pl.BlockSpec((tk, tm), lambda i, j, k: (k, i + a_off // tm))
    else:
        assert a_off % tk == 0
        a_spec = pl.BlockSpec((tm, tk), lambda i, j, k: (i, k + a_off // tk))
    if tb:
        b_spec = pl.BlockSpec((tn, tk), lambda i, j, k: (j, k))
    else:
        b_spec = pl.BlockSpec((tk, tn), lambda i, j, k: (k, j))
    ca, cb = (0 if ta else 1), (1 if tb else 0)

    def body(a_ref, b_ref, o_ref, *acc):
        if nk == 1:
            o_ref[...] = _dg(a_ref[...], b_ref[...], ca, cb).astype(o_ref.dtype)
            return
        (acc_ref,) = acc
        k = pl.program_id(2)

        @pl.when(k == 0)
        def _():
            acc_ref[...] = jnp.zeros_like(acc_ref)

        acc_ref[...] += _dg(a_ref[...], b_ref[...], ca, cb)

        @pl.when(k == nk - 1)
        def _():
            o_ref[...] = acc_ref[...].astype(o_ref.dtype)

    return pl.pallas_call(
        body, name=name, grid=(m // tm, n // tn, nk),
        in_specs=[a_spec, b_spec],
        out_specs=pl.BlockSpec((tm, tn), lambda i, j, k: (i, j)),
        out_shape=jax.ShapeDtypeStruct((m, n), out_dtype),
        scratch_shapes=[pltpu.VMEM((tm, tn), F32)] if nk > 1 else [],
        compiler_params=_cparams(("parallel", "parallel", "arbitrary")),
    )(a, b)


BAND = 128


def _band_matmul(name, kind, a, b, *, a_blk0=0, out_dtype=F32):
    n_rows = a.shape[0]
    blk = 2 * S5_LB
    tm = _pick(n_rows, (1024, 512, 256))
    if kind == "nn":
        grid = (n_rows // tm, S5_NJ)
        in_specs = [pl.BlockSpec((tm, BAND), lambda i, j: (i, a_blk0 + j)), pl.BlockSpec((BAND, blk), lambda i, j: (j, 0))]
        out_spec = pl.BlockSpec((tm, blk), lambda i, j: (i, j))
        out_shape = (n_rows, S5_NJ * blk)
        sem = ("parallel", "parallel")

        def body(a_ref, b_ref, o_ref):
            o_ref[...] = _dg(a_ref[...], b_ref[...], 1, 0).astype(o_ref.dtype)
    elif kind == "nt":
        grid = (n_rows // tm, S5_NJ)
        in_specs = [pl.BlockSpec((tm, blk), lambda i, j: (i, j)), pl.BlockSpec((BAND, blk), lambda i, j: (j, 0))]
        out_spec = pl.BlockSpec((tm, BAND), lambda i, j: (i, j))
        out_shape = (n_rows, S5_NJ * BAND)
        sem = ("parallel", "parallel")

        def body(a_ref, b_ref, o_ref):
            o_ref[...] = _dg(a_ref[...], b_ref[...], 1, 1).astype(o_ref.dtype)
    else:
        grid = (S5_NJ, n_rows // tm)
        in_specs = [pl.BlockSpec((tm, BAND), lambda j, k: (k, a_blk0 + j)), pl.BlockSpec((tm, blk), lambda j, k: (k, j))]
        out_spec = pl.BlockSpec((BAND, blk), lambda j, k: (j, 0))
        out_shape = (S5_NJ * BAND, blk)
        sem = ("parallel", "arbitrary")

        def body(a_ref, b_ref, o_ref):
            @pl.when(pl.program_id(1) == 0)
            def _():
                o_ref[...] = jnp.zeros_like(o_ref)

            o_ref[...] += _dg(a_ref[...], b_ref[...], 0, 0)

    return pl.pallas_call(
        body, name=name, grid=grid, in_specs=in_specs, out_specs=out_spec,
        out_shape=jax.ShapeDtypeStruct(out_shape, out_dtype), compiler_params=_cparams(sem),
    )(a, b)


def _rowwise(name, fn, n_rows, tr, row_ins, par_ins, row_outs, acc_outs=()):
    nr, npar, no, na = len(row_ins), len(par_ins), len(row_outs), len(acc_outs)
    in_specs = []
    for arr, off, w in row_ins:
        assert off % w == 0 and arr.shape[0] == n_rows, (name, arr.shape, off, w)
        in_specs.append(pl.BlockSpec((tr, w), functools.partial(lambda i, c: (i, c), c=off // w)))
    for arr, off, w in par_ins:
        assert off % w == 0
        in_specs.append(pl.BlockSpec((arr.shape[0], w), functools.partial(lambda i, c: (0, c), c=off // w)))
    out_specs = [pl.BlockSpec((tr, w), lambda i: (i, 0)) for w, _ in row_outs]
    out_specs += [pl.BlockSpec((r, w), lambda i: (0, 0)) for r, w in acc_outs]
    out_shape = [jax.ShapeDtypeStruct((n_rows, w), dt) for w, dt in row_outs]
    out_shape += [jax.ShapeDtypeStruct((r, w), F32) for r, w in acc_outs]

    def body(*refs):
        rows = [r[...] for r in refs[:nr]]
        pars = [r[...] for r in refs[nr:nr + npar]]
        o_refs = refs[nr + npar:nr + npar + no]
        a_refs = refs[nr + npar + no:]
        outs, accs = fn(rows, pars)
        for o_ref, o in zip(o_refs, outs, strict=True):
            o_ref[...] = o.astype(o_ref.dtype)
        if na:
            @pl.when(pl.program_id(0) == 0)
            def _():
                for a_ref in a_refs:
                    a_ref[...] = jnp.zeros_like(a_ref)

            for a_ref, a in zip(a_refs, accs, strict=True):
                a_ref[...] += jnp.broadcast_to(a, a_ref.shape)

    res = pl.pallas_call(
        body, name=name, grid=(n_rows // tr,),
        in_specs=in_specs, out_specs=out_specs, out_shape=out_shape,
        compiler_params=_cparams(("arbitrary",) if na else ("parallel",)),
    )(*[a for a, _, _ in row_ins], *[a for a, _, _ in par_ins])
    return res


def _rms(x, w):
    return x * lax.rsqrt(jnp.mean(x * x, axis=-1, keepdims=True) + RMS_EPS) * w


def _gated_norm(y, z, w):
    outs = []
    for g in range(SSD_GROUPS):
        sl = slice(g * GROUP_W, (g + 1) * GROUP_W)
        yg = y[:, sl] * jax.nn.silu(z[:, sl])
        outs.append(yg * lax.rsqrt(jnp.mean(yg * yg, axis=-1, keepdims=True) + RMS_EPS) * w[:, sl])
    return jnp.concatenate(outs, axis=-1)


def _s5_out(yc, u, z, t, d, bg):
    ge = jax.nn.gelu(yc + d * u)
    return ge * jax.nn.sigmoid(t + bg) * jax.nn.silu(z)


def _merge(g5, gs, m5, ms):
    return jax.nn.sigmoid(g5) * m5 + jax.nn.sigmoid(gs) * ms


def _head_loss(h1, pgl, pp, fw, tgt):
    h2 = h1 + jax.nn.sigmoid(pgl) * pp
    err = _rms(h2, fw) - tgt
    per_row = 0.5 * jnp.mean(err * err, axis=-1, keepdims=True)
    return jnp.sum(per_row, axis=0, keepdims=True)


def _s5_disc(a_re, a_im, log_step, b_re2, b_im2, expand):
    step = jnp.exp(log_step)
    mag = jnp.exp(a_re * step)
    lb_re = mag * jnp.cos(a_im * step)
    lb_im = mag * jnp.sin(a_im * step)
    den = a_re * a_re + a_im * a_im
    n_re = lb_re - 1.0
    f_re = (n_re * a_re + lb_im * a_im) / den
    f_im = (lb_im * a_re - n_re * a_im) / den
    hi = lax.Precision.HIGHEST
    fr = jnp.dot(expand, f_re, precision=hi, preferred_element_type=F32)
    fi = jnp.dot(expand, f_im, precision=hi, preferred_element_type=F32)
    return lb_re, lb_im, fr * b_re2 - fi * b_im2, fr * b_im2 + fi * b_re2


def _s5_params_fwd(a_re, a_im, log_step, b_re2, b_im2, expand):
    gp = a_re.shape

    def body(ar, ai, ls, br, bi, ex, pr_ref, pi_ref, bbr_ref, bbi_ref):
        lr, li, bbr, bbi = _s5_disc(ar[...], ai[...], ls[...], br[...], bi[...], ex[...])
        bbr_ref[...] = bbr
        bbi_ref[...] = bbi
        for k in range(S5_LOG_TB):
            pr_ref[k] = lr
            pi_ref[k] = li
            lr, li = lr * lr - li * li, 2.0 * lr * li

    return pl.pallas_call(
        body, name="s5_params_fwd",
        out_shape=(jax.ShapeDtypeStruct((S5_LOG_TB,) + gp, F32), jax.ShapeDtypeStruct((S5_LOG_TB,) + gp, F32),
                   jax.ShapeDtypeStruct(b_re2.shape, F32), jax.ShapeDtypeStruct(b_re2.shape, F32)),
    )(a_re, a_im, log_step, b_re2, b_im2, expand)


def _s5_params_bwd(a_re, a_im, log_step, b_re2, b_im2, expand, d_lr, d_li, d_bbr, d_bbi):
    def body(ar, ai, ls, br, bi, ex, glr, gli, gbr, gbi, dar, dai, dls, dbr, dbi):
        _, vjp = jax.vjp(lambda *p: _s5_disc(*p, ex[...]), ar[...], ai[...], ls[...], br[...], bi[...])
        g = vjp((glr[...], gli[...], gbr[...], gbi[...]))
        for ref, val in zip((dar, dai, dls, dbr, dbi), g, strict=True):
            ref[...] = val

    return pl.pallas_call(
        body, name="s5_params_bwd",
        out_shape=tuple(jax.ShapeDtypeStruct(v.shape, F32) for v in (a_re, a_im, log_step, b_re2, b_im2)),
    )(a_re, a_im, log_step, b_re2, b_im2, expand, d_lr, d_li, d_bbr, d_bbi)


def _scan_block(xr, xi, lp, cr, ci, reverse):
    tb = xr.shape[0]
    row = lax.broadcasted_iota(jnp.int32, xr.shape, 0)
    l_r, l_i = lp[0:1, :S5_LB], lp[0:1, S5_LB:]
    edge = (row == tb - 1) if reverse else (row == 0)
    xr = xr + jnp.where(edge, l_r * cr - l_i * ci, 0.0)
    xi = xi + jnp.where(edge, l_r * ci + l_i * cr, 0.0)
    for k in range(S5_LOG_TB):
        sh = 1 << k
        a_r, a_i = lp[k:k + 1, :S5_LB], lp[k:k + 1, S5_LB:]
        if reverse:
            keep = row < tb - sh
            pr = jnp.where(keep, pltpu.roll(xr, tb - sh, 0), 0.0)
            pi = jnp.where(keep, pltpu.roll(xi, tb - sh, 0), 0.0)
        else:
            keep = row >= sh
            pr = jnp.where(keep, pltpu.roll(xr, sh, 0), 0.0)
            pi = jnp.where(keep, pltpu.roll(xi, sh, 0), 0.0)
        xr, xi = xr + a_r * pr - a_i * pi, xi + a_r * pi + a_i * pr
    return xr, xi


def _s5_scan_fwd(bu, lam_pow, n_seq, seq_len):
    n_t = seq_len // S5_TB
    blk = 2 * S5_LB

    def body(bu_ref, lp_ref, s_ref, cr, ci):
        @pl.when(pl.program_id(2) == 0)
        def _():
            cr[...] = jnp.zeros_like(cr)
            ci[...] = jnp.zeros_like(ci)

        xr, xi = _scan_block(bu_ref[:, :S5_LB], bu_ref[:, S5_LB:], lp_ref[...], cr[...], ci[...], False)
        s_ref[:, :S5_LB] = xr
        s_ref[:, S5_LB:] = xi
        cr[...] = xr[S5_TB - 1:S5_TB, :]
        ci[...] = xi[S5_TB - 1:S5_TB, :]

    return pl.pallas_call(
        body, name="s5_scan_fwd", grid=(S5_NJ, n_seq, n_t),
        in_specs=[pl.BlockSpec((S5_TB, blk), lambda j, b, t: (b * n_t + t, j)),
                  pl.BlockSpec((S5_LOG_TB, blk), lambda j, b, t: (0, j))],
        out_specs=pl.BlockSpec((S5_TB, blk), lambda j, b, t: (b * n_t + t, j)),
        out_shape=jax.ShapeDtypeStruct(bu.shape, F32),
        scratch_shapes=[pltpu.VMEM((1, S5_LB), F32), pltpu.VMEM((1, S5_LB), F32)],
        compiler_params=_cparams(("parallel", "parallel", "arbitrary")),
    )(bu, lam_pow)


def _s5_scan_bwd(ds, s, lam_pow_conj, n_seq, seq_len):
    n_t = seq_len // S5_TB
    blk = 2 * S5_LB
    halo_per_blk = S5_TB // 8

    def rows(j, b, t):
        return (b * n_t + (n_t - 1 - t), j)

    def halo(j, b, t):
        return (jnp.maximum((b * n_t + (n_t - 1 - t)) * halo_per_blk - 1, 0), j)

    def body(ds_ref, s_ref, h_ref, lp_ref, g_ref, dl_ref, cr, ci):
        b, t = pl.program_id(1), pl.program_id(2)

        @pl.when(t == 0)
        def _():
            cr[...] = jnp.zeros_like(cr)
            ci[...] = jnp.zeros_like(ci)

        @pl.when((b == 0) & (t == 0))
        def _():
            dl_ref[...] = jnp.zeros_like(dl_ref)

        gr, gi = _scan_block(ds_ref[:, :S5_LB], ds_ref[:, S5_LB:], lp_ref[...], cr[...], ci[...], True)
        g_ref[:, :S5_LB] = gr.astype(g_ref.dtype)
        g_ref[:, S5_LB:] = gi.astype(g_ref.dtype)
        cr[...] = gr[0:1, :]
        ci[...] = gi[0:1, :]
        row = lax.broadcasted_iota(jnp.int32, gr.shape, 0)
        first_blk = t == n_t - 1
        h_r = jnp.where(first_blk, 0.0, h_ref[7:8, :S5_LB])
        h_i = jnp.where(first_blk, 0.0, h_ref[7:8, S5_LB:])
        sp_r = jnp.where(row == 0, h_r, pltpu.roll(s_ref[:, :S5_LB], 1, 0))
        sp_i = jnp.where(row == 0, h_i, pltpu.roll(s_ref[:, S5_LB:], 1, 0))
        dl_ref[:, :S5_LB] += jnp.sum(gr * sp_r + gi * sp_i, axis=0, keepdims=True)
        dl_ref[:, S5_LB:] += jnp.sum(gi * sp_r - gr * sp_i, axis=0, keepdims=True)

    return pl.pallas_call(
        body, name="s5_scan_bwd", grid=(S5_NJ, n_seq, n_t),
        in_specs=[pl.BlockSpec((S5_TB, blk), rows), pl.BlockSpec((S5_TB, blk), rows),
                  pl.BlockSpec((8, blk), halo), pl.BlockSpec((S5_LOG_TB, blk), lambda j, b, t: (0, j))],
        out_specs=[pl.BlockSpec((S5_TB, blk), rows), pl.BlockSpec((1, blk), lambda j, b, t: (0, j))],
        out_shape=[jax.ShapeDtypeStruct(ds.shape, BF16), jax.ShapeDtypeStruct((1, ds.shape[1]), F32)],
        scratch_shapes=[pltpu.VMEM((1, S5_LB), F32), pltpu.VMEM((1, S5_LB), F32)],
        compiler_params=_cparams(("parallel", "arbitrary", "arbitrary")),
    )(ds, s, s, lam_pow_conj)


CONV_TR = 256
CONV_CW = 512


def _shift_down(x, halo, k):
    if k == 0:
        return x
    row8 = lax.broadcasted_iota(jnp.int32, halo.shape, 0)
    rolled = pltpu.roll(x, k, 0)
    top = jnp.where(row8 < k, pltpu.roll(halo, k, 0), rolled[:8])
    if x.shape[0] == 8:
        return top
    return jnp.concatenate([top, rolled[8:]], axis=0)


def _shift_up(x, halo, k):
    if k == 0:
        return x
    n = x.shape[0]
    row8 = lax.broadcasted_iota(jnp.int32, halo.shape, 0)
    rolled = pltpu.roll(x, n - k, 0)
    bot = jnp.where(row8 >= 8 - k, pltpu.roll(halo, 8 - k, 0), rolled[n - 8:])
    if n == 8:
        return bot
    return jnp.concatenate([rolled[:n - 8], bot], axis=0)


def _conv_pre(x, halo, w, b):
    acc = b + w[SSD_CONV - 1:SSD_CONV, :] * x
    for k in range(SSD_CONV - 1):
        acc = acc + w[k:k + 1, :] * _shift_down(x, halo, SSD_CONV - 1 - k)
    return acc


def _conv_specs(seq_len, col_off):
    lt = seq_len // CONV_TR
    cb = col_off // CONV_CW
    cur = pl.BlockSpec((CONV_TR, CONV_CW), lambda j, i: (i, j + cb))
    prev = pl.BlockSpec((8, CONV_CW), lambda j, i: (jnp.maximum(i * (CONV_TR // 8) - 1, 0), j + cb))
    return lt, cur, prev


def _conv_fwd(proj, conv_w, conv_b, n_rows, seq_len):
    lt, cur, prev = _conv_specs(seq_len, OFF_XBC)

    def body(x_ref, h_ref, w_ref, b_ref, o_ref):
        halo = jnp.where(pl.program_id(1) % lt == 0, 0.0, h_ref[...])
        o_ref[...] = jax.nn.silu(_conv_pre(x_ref[...], halo, w_ref[...], b_ref[...]))

    return pl.pallas_call(
        body, name="ssd_conv_fwd", grid=(SSD_CONV_DIM // CONV_CW, n_rows // CONV_TR),
        in_specs=[cur, prev, pl.BlockSpec((SSD_CONV, CONV_CW), lambda j, i: (0, j)),
                  pl.BlockSpec((1, CONV_CW), lambda j, i: (0, j))],
        out_specs=pl.BlockSpec((CONV_TR, CONV_CW), lambda j, i: (i, j)),
        out_shape=jax.ShapeDtypeStruct((n_rows, SSD_CONV_DIM), F32),
        compiler_params=_cparams(("parallel", "parallel")),
    )(proj, proj, conv_w, conv_b)


def _conv_bwd(name, proj, d_act, conv_w, conv_b, n_rows, seq_len, col_off):
    width = d_act.shape[1]
    lt, cur, prev = _conv_specs(seq_len, OFF_XBC + col_off)
    n_blk = n_rows // CONV_TR
    cb = (OFF_XBC + col_off) // CONV_CW
    pb = col_off // CONV_CW
    nxt = pl.BlockSpec((8, CONV_CW), lambda j, i: (jnp.minimum((i + 1) * (CONV_TR // 8), n_rows // 8 - 1), j + cb))
    d_cur = pl.BlockSpec((CONV_TR, CONV_CW), lambda j, i: (i, j))
    d_nxt = pl.BlockSpec((8, CONV_CW), lambda j, i: (jnp.minimum((i + 1) * (CONV_TR // 8), n_rows // 8 - 1), j))

    def dsilu(pre):
        sg = jax.nn.sigmoid(pre)
        return sg * (1.0 + pre * (1.0 - sg))

    def body(x_ref, hp_ref, hn_ref, d_ref, dn_ref, w_ref, b_ref, dx_ref, dw_ref, db_ref):
        i = pl.program_id(1)
        x, w, b = x_ref[...], w_ref[...], b_ref[...]
        halo_p = jnp.where(i % lt == 0, 0.0, hp_ref[...])
        at_end = i % lt == lt - 1
        dpre = d_ref[...] * dsilu(_conv_pre(x, halo_p, w, b))
        pre_n = _conv_pre(hn_ref[...], x[CONV_TR - 8:, :], w, b)
        dpre_n = jnp.where(at_end, 0.0, dn_ref[...] * dsilu(pre_n))
        dx = w[SSD_CONV - 1:SSD_CONV, :] * dpre
        for k in range(SSD_CONV - 1):
            dx = dx + w[k:k + 1, :] * _shift_up(dpre, dpre_n, SSD_CONV - 1 - k)
        dx_ref[...] = dx.astype(dx_ref.dtype)

        @pl.when(i == 0)
        def _():
            dw_ref[...] = jnp.zeros_like(dw_ref)
            db_ref[...] = jnp.zeros_like(db_ref)

        for k in range(SSD_CONV):
            xs = _shift_down(x, halo_p, SSD_CONV - 1 - k)
            dw_ref[k:k + 1, :] += jnp.sum(dpre * xs, axis=0, keepdims=True)
        db_ref[...] += jnp.sum(dpre, axis=0, keepdims=True)

    return pl.pallas_call(
        body, name=name, grid=(width // CONV_CW, n_blk),
        in_specs=[cur, prev, nxt, d_cur, d_nxt,
                  pl.BlockSpec((SSD_CONV, CONV_CW), lambda j, i: (0, j + pb)),
                  pl.BlockSpec((1, CONV_CW), lambda j, i: (0, j + pb))],
        out_specs=[pl.BlockSpec((CONV_TR, CONV_CW), lambda j, i: (i, j)),
                   pl.BlockSpec((SSD_CONV, CONV_CW), lambda j, i: (0, j)),
                   pl.BlockSpec((1, CONV_CW), lambda j, i: (0, j))],
        out_shape=[jax.ShapeDtypeStruct((n_rows, width), BF16),
                   jax.ShapeDtypeStruct((SSD_CONV, width), F32), jax.ShapeDtypeStruct((1, width), F32)],
        compiler_params=_cparams(("parallel", "arbitrary")),
    )(proj, proj, proj, d_act, d_act, conv_w, conv_b)


def _ssd_chunk(xs, bm, cm, dtr, prev, dtb, alog, dsk, tri, causal):
    dt = jax.nn.softplus(dtr + dtb)
    da = dt * (-jnp.exp(alog))
    acum = jnp.dot(tri, da, precision=lax.Precision.HIGHEST, preferred_element_type=F32)
    acum_t = acum.T
    alast = acum[SSD_CHUNK - 1:SSD_CHUNK, :]
    scores = dot_nt(cm, bm)
    ys, new = [], []
    for j in range(SSD_HPG):
        xh = xs[:, j * SSD_HEADDIM:(j + 1) * SSD_HEADDIM]
        xdt = xh * dt[:, j:j + 1]
        ac = acum[:, j:j + 1]
        lmat = jnp.exp(jnp.where(causal, ac - acum_t[j:j + 1, :], -jnp.inf))
        y_diag = dot_nn(scores * lmat, xdt)
        st = dot_tn(xdt * jnp.exp(alast[:, j:j + 1] - ac), bm)
        y_off = dot_nt(cm, prev[j]) * jnp.exp(ac)
        ys.append(y_diag + y_off + xh * dsk[:, j:j + 1])
        new.append(prev[j] * jnp.exp(alast[:, j:j + 1]) + st)
    return jnp.concatenate(ys, axis=-1), new


def _ssd_consts():
    r = lax.broadcasted_iota(jnp.int32, (SSD_CHUNK, SSD_CHUNK), 0)
    c = lax.broadcasted_iota(jnp.int32, (SSD_CHUNK, SSD_CHUNK), 1)
    return (r >= c).astype(F32), r >= c


def _ssd_specs(n_c, reverse):
    def cidx(c):
        return n_c - 1 - c if reverse else c

    xs = pl.BlockSpec((SSD_CHUNK, GROUP_W), lambda g, b, c: (b * n_c + cidx(c), g))
    bm = pl.BlockSpec((SSD_CHUNK, SSD_STATE), lambda g, b, c: (b * n_c + cidx(c), SSD_WIDTH // SSD_STATE + g))
    cm = pl.BlockSpec((SSD_CHUNK, SSD_STATE), lambda g, b, c: (b * n_c + cidx(c), (SSD_WIDTH + SSD_BC) // SSD_STATE + g))
    dt = pl.BlockSpec((SSD_CHUNK, 128), lambda g, b, c: (b * n_c + cidx(c), OFF_DT // 128 + g))
    par = pl.BlockSpec((1, 128), lambda g, b, c: (0, g))
    st = pl.BlockSpec((1, 1, SSD_HPG, SSD_HEADDIM, SSD_STATE), lambda g, b, c: (b, cidx(c), g, 0, 0))
    return xs, bm, cm, dt, par, st


def _ssd_fwd(xbc_act, proj, dtb, alog, dsk, n_seq, seq_len):
    n_c = seq_len // SSD_CHUNK
    xs_s, bm_s, cm_s, dt_s, par_s, st_s = _ssd_specs(n_c, False)

    def body(xs_ref, bm_ref, cm_ref, dt_ref, dtb_ref, al_ref, dk_ref, y_ref, st_ref, state):
        @pl.when(pl.program_id(2) == 0)
        def _():
            state[...] = jnp.zeros_like(state)

        tri, causal = _ssd_consts()
        prev = [state[j] for j in range(SSD_HPG)]
        for j in range(SSD_HPG):
            st_ref[0, 0, j] = prev[j]
        y, new = _ssd_chunk(xs_ref[...], bm_ref[...], cm_ref[...], dt_ref[...], prev,
                            dtb_ref[...], al_ref[...], dk_ref[...], tri, causal)
        y_ref[...] = y
        for j in range(SSD_HPG):
            state[j] = new[j]

    return pl.pallas_call(
        body, name="ssd_fwd", grid=(SSD_GROUPS, n_seq, n_c),
        in_specs=[xs_s, bm_s, cm_s, dt_s, par_s, par_s, par_s],
        out_specs=[pl.BlockSpec((SSD_CHUNK, GROUP_W), lambda g, b, c: (b * n_c + c, g)), st_s],
        out_shape=[jax.ShapeDtypeStruct((n_seq * seq_len, SSD_WIDTH), F32),
                   jax.ShapeDtypeStruct((n_seq, n_c, SSD_HEADS, SSD_HEADDIM, SSD_STATE), F32)],
        scratch_shapes=[pltpu.VMEM((SSD_HPG, SSD_HEADDIM, SSD_STATE), F32)],
        compiler_params=_cparams(("parallel", "parallel", "arbitrary")),
    )(xbc_act, xbc_act, xbc_act, proj, dtb, alog, dsk)


def _ssd_bwd(xbc_act, proj, states, dy, dtb, alog, dsk, n_seq, seq_len):
    n_c = seq_len // SSD_CHUNK
    n_rows = n_seq * seq_len
    xs_s, bm_s, cm_s, dt_s, par_s, st_s = _ssd_specs(n_c, True)

    def rows(w):
        return pl.BlockSpec((SSD_CHUNK, w), lambda g, b, c: (b * n_c + (n_c - 1 - c), g))

    def body(xs_ref, bm_ref, cm_ref, dt_ref, st_ref, dy_ref, dtb_ref, al_ref, dk_ref,
             dxs_ref, dbm_ref, dcm_ref, ddt_ref, ddtb_ref, dal_ref, ddk_ref, dstate):
        b, c = pl.program_id(1), pl.program_id(2)

        @pl.when(c == 0)
        def _():
            dstate[...] = jnp.zeros_like(dstate)

        @pl.when((b == 0) & (c == 0))
        def _():
            ddtb_ref[...] = jnp.zeros_like(ddtb_ref)
            dal_ref[...] = jnp.zeros_like(dal_ref)
            ddk_ref[...] = jnp.zeros_like(ddk_ref)

        tri, causal = _ssd_consts()
        prev = [st_ref[0, 0, j] for j in range(SSD_HPG)]
        _, vjp = jax.vjp(
            lambda xs, bm, cm, dtr, prev, dtb, alog, dsk: _ssd_chunk(xs, bm, cm, dtr, prev, dtb, alog, dsk, tri, causal),
            xs_ref[...], bm_ref[...], cm_ref[...], dt_ref[...], prev, dtb_ref[...], al_ref[...], dk_ref[...])
        dxs, dbm, dcm, ddtr, dprev, ddtb, dal, ddk = vjp((dy_ref[...], [dstate[j] for j in range(SSD_HPG)]))
        dxs_ref[...] = dxs
        dbm_ref[...] = dbm
        dcm_ref[...] = dcm
        ddt_ref[...] = ddtr.astype(ddt_ref.dtype)
        ddtb_ref[...] += ddtb
        dal_ref[...] += dal
        ddk_ref[...] += ddk
        for j in range(SSD_HPG):
            dstate[j] = dprev[j]

    acc = pl.BlockSpec((1, 128), lambda g, b, c: (0, g))
    return pl.pallas_call(
        body, name="ssd_bwd", grid=(SSD_GROUPS, n_seq, n_c),
        in_specs=[xs_s, bm_s, cm_s, dt_s, st_s, rows(GROUP_W), par_s, par_s, par_s],
        out_specs=[rows(GROUP_W), rows(SSD_STATE), rows(SSD_STATE), rows(128), acc, acc, acc],
        out_shape=[jax.ShapeDtypeStruct((n_rows, SSD_WIDTH), F32), jax.ShapeDtypeStruct((n_rows, SSD_BC), F32),
                   jax.ShapeDtypeStruct((n_rows, SSD_BC), F32), jax.ShapeDtypeStruct((n_rows, 4 * 128), BF16),
                   jax.ShapeDtypeStruct((1, 512), F32), jax.ShapeDtypeStruct((1, 512), F32),
                   jax.ShapeDtypeStruct((1, 512), F32)],
        scratch_shapes=[pltpu.VMEM((SSD_HPG, SSD_HEADDIM, SSD_STATE), F32)],
        compiler_params=_cparams(("parallel", "arbitrary", "arbitrary")),
    )(xbc_act, xbc_act, xbc_act, proj, states, dy, dtb, alog, dsk)


def _pad_heads(v):
    return jnp.pad(v.reshape(SSD_GROUPS, SSD_HPG), ((0, 0), (0, 128 - SSD_HPG))).reshape(1, SSD_GROUPS * 128)


def _unpad_heads(v):
    return v.reshape(SSD_GROUPS, 128)[:, :SSD_HPG].reshape(1, SSD_HEADS)


def _state_cols(v):
    re, im = v
    lead = re.shape[:-1]
    re = re.reshape(lead + (S5_NJ, 1, S5_LB))
    im = im.reshape(lead + (S5_NJ, 1, S5_LB))
    return jnp.concatenate([re, im], axis=-2).reshape(lead + (2 * S5_N,))


def _state_uncols(v):
    lead = v.shape[:-1]
    v = v.reshape(lead + (S5_NJ, 2, S5_LB))
    return v[..., 0, :].reshape(lead + (S5_N,)), v[..., 1, :].reshape(lead + (S5_N,))


GROUPS_PER_BAND = BAND // S5_GROUP


def _band(w2_re, w2_im):
    gh = S5_GROUPS * S5_GROUP
    rg = ((jnp.arange(gh) // S5_GROUP) % GROUPS_PER_BAND)[:, None, None]
    cg = jnp.arange(GROUPS_PER_BAND)[None, :, None]
    parts = [jnp.where(rg == cg, v[:, None, :], 0.0).reshape(gh, S5_LB) for v in (w2_re, w2_im)]
    return jnp.concatenate(parts, axis=1)


def _band_take(wb):
    gh = S5_GROUPS * S5_GROUP
    w4 = wb.reshape(gh, 2, GROUPS_PER_BAND, S5_STATE)
    sel = w4[jnp.arange(gh), :, (jnp.arange(gh) // S5_GROUP) % GROUPS_PER_BAND, :]
    return sel[:, 0, :], sel[:, 1, :]


def _pad_w_in(w_in):
    u5, z5, zs, xbc, dt, gl = (w_in[:, 0:512], w_in[:, 512:1024], w_in[:, 1024:2560], w_in[:, 2560:5120],
                               w_in[:, 5120:5144], w_in[:, 5144:7192])
    dtp = jnp.pad(dt.reshape(-1, SSD_GROUPS, SSD_HPG), ((0, 0), (0, 0), (0, 128 - SSD_HPG))).reshape(-1, 512)
    return jnp.concatenate([xbc, u5, zs, z5, gl, dtp], axis=1)


def _unpad_w_in(wp):
    dt = wp[:, OFF_DT:].reshape(-1, SSD_GROUPS, 128)[:, :, :SSD_HPG].reshape(-1, SSD_HEADS)
    return jnp.concatenate([wp[:, OFF_U5:OFF_U5 + 512], wp[:, OFF_Z5:OFF_Z5 + 512], wp[:, OFF_ZS:OFF_ZS + 1536],
                            wp[:, OFF_XBC:OFF_XBC + 2560], dt, wp[:, OFF_G5:OFF_G5 + 2048]], axis=1)


def _local_step(x, p, tgt, w):
    n_seq, seq_len, _ = x.shape
    n_rows = n_seq * seq_len
    tr = 256
    x2 = x.reshape(n_rows, D_MODEL)
    p2 = p.reshape(n_rows, PLE_DIM)
    t2 = tgt.reshape(n_rows, D_MODEL)
    row = functools.partial(_rowwise, n_rows=n_rows, tr=tr)

    w_pad = _pad_w_in(w["w_in"])
    norm_w = w["norm_w"].reshape(1, D_MODEL)
    ple_norm_w = w["ple_norm_w"].reshape(1, D_MODEL)
    final_w = w["final_norm_w"].reshape(1, D_MODEL)
    s5_d = w["s5_d"].reshape(1, S5_WIDTH)
    b_glu = w["s5_b_glu"].reshape(1, S5_WIDTH)
    conv_w = w["ssd_conv_w"].reshape(SSD_CONV, SSD_CONV_DIM)
    conv_b = w["ssd_conv_b"].reshape(1, SSD_CONV_DIM)
    ssd_norm_w = w["ssd_norm_w"].reshape(1, SSD_WIDTH)
    dtb, alog, dsk = (_pad_heads(w[k].reshape(1, SSD_HEADS)) for k in ("ssd_dt_bias", "ssd_a_log", "ssd_d"))

    gh = S5_GROUPS * S5_GROUP
    a_re = w["s5_a_re"].reshape(S5_GROUPS, S5_STATE)
    a_im = w["s5_a_im"].reshape(S5_GROUPS, S5_STATE)
    log_step = w["s5_log_step"].reshape(S5_GROUPS, 1)
    b_re2 = jnp.transpose(w["s5_b_re"].reshape(S5_GROUPS, S5_STATE, S5_GROUP), (0, 2, 1)).reshape(gh, S5_STATE)
    b_im2 = jnp.transpose(w["s5_b_im"].reshape(S5_GROUPS, S5_STATE, S5_GROUP), (0, 2, 1)).reshape(gh, S5_STATE)
    expand = (jnp.arange(gh)[:, None] // S5_GROUP == jnp.arange(S5_GROUPS)[None, :]).astype(F32)
    pow_re, pow_im, bb_re2, bb_im2 = _s5_params_fwd(a_re, a_im, log_step, b_re2, b_im2, expand)
    lam_pow = _state_cols((pow_re.reshape(S5_LOG_TB, S5_N), pow_im.reshape(S5_LOG_TB, S5_N)))
    lam_pow_conj = _state_cols((pow_re.reshape(S5_LOG_TB, S5_N), -pow_im.reshape(S5_LOG_TB, S5_N)))
    bb_band = _band(bb_re2, bb_im2).astype(BF16)
    c_band = _band(w["s5_c_re"].reshape(gh, S5_STATE), -w["s5_c_im"].reshape(gh, S5_STATE)).astype(BF16)

    (hn,) = row("rms_in", lambda r, q: ([_rms(r[0], q[0])], []), row_ins=[(x2, 0, D_MODEL)],
                par_ins=[(norm_w, 0, D_MODEL)], row_outs=[(D_MODEL, BF16)])
    proj = _matmul("mm_proj", hn, w_pad)
    bu = _band_matmul("mm_s5_bu", "nn", proj, bb_band, a_blk0=OFF_U5 // BAND)
    s = _s5_scan_fwd(bu, lam_pow, n_seq, seq_len)
    yc = _band_matmul("mm_s5_y", "nt", s, c_band)
    (ge,) = row("s5_gelu", lambda r, q: ([jax.nn.gelu(r[0] + q[0] * r[1])], []),
                row_ins=[(yc, 0, S5_WIDTH), (proj, OFF_U5, S5_WIDTH)], par_ins=[(s5_d, 0, S5_WIDTH)],
                row_outs=[(S5_WIDTH, BF16)])
    tg = _matmul("mm_s5_glu", ge, w["s5_w_glu"])
    s5_rows = [(yc, 0, S5_WIDTH), (proj, OFF_U5, S5_WIDTH), (proj, OFF_Z5, S5_WIDTH), (tg, 0, S5_WIDTH)]
    s5_pars = [(s5_d, 0, S5_WIDTH), (b_glu, 0, S5_WIDTH)]
    (y5,) = row("s5_out", lambda r, q: ([_s5_out(*r, *q)], []), row_ins=s5_rows, par_ins=s5_pars,
                row_outs=[(S5_WIDTH, BF16)])

    xbc_act = _conv_fwd(proj, conv_w, conv_b, n_rows, seq_len)
    y_ssd, states = _ssd_fwd(xbc_act, proj, dtb, alog, dsk, n_seq, seq_len)
    gn_rows = [(y_ssd, 0, SSD_WIDTH), (proj, OFF_ZS, SSD_WIDTH)]
    (yss,) = row("ssd_gate", lambda r, q: ([_gated_norm(r[0], r[1], q[0])], []), row_ins=gn_rows,
                 par_ins=[(ssd_norm_w, 0, SSD_WIDTH)], row_outs=[(SSD_WIDTH, BF16)])

    m5 = _matmul("mm_br_s5", y5, w["w_br_s5"])
    ms = _matmul("mm_br_ssd", yss, w["w_br_ssd"])
    mg_rows = [(proj, OFF_G5, D_MODEL), (proj, OFF_GS, D_MODEL), (m5, 0, D_MODEL), (ms, 0, D_MODEL)]
    (merged,) = row("merge", lambda r, q: ([_merge(*r)], []), row_ins=mg_rows, par_ins=[], row_outs=[(D_MODEL, BF16)])
    o = _matmul("mm_out", merged, w["w_out"])

    def resid_fn(r, q):
        h1 = r[0] + r[1]
        return [h1, _rms(h1, q[0])], []

    h1, hp = row("resid_ple_norm", resid_fn, row_ins=[(x2, 0, D_MODEL), (o, 0, D_MODEL)],
                 par_ins=[(ple_norm_w, 0, D_MODEL)], row_outs=[(D_MODEL, F32), (D_MODEL, BF16)])
    pgl = _matmul("mm_ple_gate", hp, w["w_ple_gate"])
    pp = _matmul("mm_ple_proj", p2, w["w_ple_proj"])

    def head_fn(r, q):
        h1_, pgl_, pp_, tgt_ = r
        loss, vjp = jax.vjp(lambda a, b, c, f: _head_loss(a, b, c, f, tgt_), h1_, pgl_, pp_, q[0])
        dh1_, dpgl_, dpp_, dfw_ = vjp(jnp.ones_like(loss))
        return [dh1_, dpgl_, dpp_], [loss, dfw_]

    dh2, dpgl, dpp, loss_acc, d_final_w = row(
        "head", head_fn, row_ins=[(h1, 0, D_MODEL), (pgl, 0, D_MODEL), (pp, 0, D_MODEL), (t2, 0, D_MODEL)],
        par_ins=[(final_w, 0, D_MODEL)], row_outs=[(D_MODEL, F32), (D_MODEL, BF16), (D_MODEL, BF16)],
        acc_outs=[(1, 128), (1, D_MODEL)])
    loss = loss_acc[0, 0]

    g = {}
    g["final_norm_w"] = d_final_w
    g["w_ple_gate"] = _matmul("mm_d_w_ple_gate", hp, dpgl, ta=True)
    g["w_ple_proj"] = _matmul("mm_d_w_ple_proj", p2, dpp, ta=True)
    dhp = _matmul("mm_d_hp", dpgl, w["w_ple_gate"], tb=True)

    def ple_norm_bwd(r, q):
        h1_, dhp_, dh2_ = r
        _, vjp = jax.vjp(_rms, h1_, q[0])
        dh, dw = vjp(dhp_)
        dh = dh + dh2_
        return [dh, dh], [dw]

    dh1, dh1_b, g["ple_norm_w"] = row(
        "ple_norm_bwd", ple_norm_bwd, row_ins=[(h1, 0, D_MODEL), (dhp, 0, D_MODEL), (dh2, 0, D_MODEL)],
        par_ins=[(ple_norm_w, 0, D_MODEL)], row_outs=[(D_MODEL, F32), (D_MODEL, BF16)], acc_outs=[(1, D_MODEL)])
    g["w_out"] = _matmul("mm_d_w_out", merged, dh1_b, ta=True)
    dmerged = _matmul("mm_d_merged", dh1_b, w["w_out"], tb=True)

    def merge_bwd(r, q):
        _, vjp = jax.vjp(_merge, *r[:4])
        return list(vjp(r[4])), []

    dg5, dgs, dm5, dms = row("merge_bwd", merge_bwd, row_ins=mg_rows + [(dmerged, 0, D_MODEL)], par_ins=[],
                             row_outs=[(D_MODEL, BF16)] * 4)
    g["w_br_s5"] = _matmul("mm_d_w_br_s5", y5, dm5, ta=True)
    g["w_br_ssd"] = _matmul("mm_d_w_br_ssd", yss, dms, ta=True)
    dy5 = _matmul("mm_d_y5", dm5, w["w_br_s5"], tb=True)
    dyss = _matmul("mm_d_yss", dms, w["w_br_ssd"], tb=True)

    def s5_out_bwd_a(r, q):
        yc_, u_, z_, t_, dy_ = r
        d_, bg_ = q
        ge_ = jax.nn.gelu(yc_ + d_ * u_)
        _, vjp = jax.vjp(lambda a, z, t, b: a * jax.nn.sigmoid(t + b) * jax.nn.silu(z), ge_, z_, t_, bg_)
        dge, dz, dt_, dbg = vjp(dy_)
        return [dge, dz, dt_], [dbg]

    dge_a, dz5, dtg, g["s5_b_glu"] = row(
        "s5_out_bwd_a", s5_out_bwd_a, row_ins=s5_rows + [(dy5, 0, S5_WIDTH)], par_ins=s5_pars,
        row_outs=[(S5_WIDTH, F32), (S5_WIDTH, BF16), (S5_WIDTH, BF16)], acc_outs=[(1, S5_WIDTH)])
    g["s5_w_glu"] = _matmul("mm_d_w_glu", ge, dtg, ta=True)
    dge_b = _matmul("mm_d_ge", dtg, w["s5_w_glu"], tb=True)

    def s5_out_bwd_b(r, q):
        yc_, u_, da_, db_ = r
        _, vjp = jax.vjp(lambda yc, u, d: jax.nn.gelu(yc + d * u), yc_, u_, q[0])
        dyc_, du_, dd_ = vjp(da_ + db_)
        return [dyc_, du_], [dd_]

    dyc, du5_a, g["s5_d"] = row(
        "s5_out_bwd_b", s5_out_bwd_b,
        row_ins=[(yc, 0, S5_WIDTH), (proj, OFF_U5, S5_WIDTH), (dge_a, 0, S5_WIDTH), (dge_b, 0, S5_WIDTH)],
        par_ins=[(s5_d, 0, S5_WIDTH)], row_outs=[(S5_WIDTH, BF16), (S5_WIDTH, F32)], acc_outs=[(1, S5_WIDTH)])
    d_c_band = _band_matmul("mm_d_c", "tn", dyc, s)
    ds = _band_matmul("mm_d_s", "nn", dyc, c_band)
    dbu, d_lam = _s5_scan_bwd(ds, s, lam_pow_conj, n_seq, seq_len)
    d_bb_band = _band_matmul("mm_d_bb", "tn", proj, dbu, a_blk0=OFF_U5 // BAND)
    du5_b = _band_matmul("mm_d_u5", "nt", dbu, bb_band)
    (du5,) = row("s5_du", lambda r, q: ([r[0] + r[1]], []), row_ins=[(du5_a, 0, S5_WIDTH), (du5_b, 0, S5_WIDTH)],
                 par_ins=[], row_outs=[(S5_WIDTH, BF16)])

    d_lr, d_li = _state_uncols(d_lam)
    d_bbr, d_bbi = _band_take(d_bb_band)
    d_are, d_aim, d_ls, d_br2, d_bi2 = _s5_params_bwd(
        a_re, a_im, log_step, b_re2, b_im2, expand,
        d_lr.reshape(S5_GROUPS, S5_STATE), d_li.reshape(S5_GROUPS, S5_STATE), d_bbr, d_bbi)
    g["s5_a_re"], g["s5_a_im"], g["s5_log_step"] = d_are, d_aim, d_ls
    g["s5_b_re"] = jnp.transpose(d_br2.reshape(S5_GROUPS, S5_GROUP, S5_STATE), (0, 2, 1))
    g["s5_b_im"] = jnp.transpose(d_bi2.reshape(S5_GROUPS, S5_GROUP, S5_STATE), (0, 2, 1))
    d_cr, d_ci = _band_take(d_c_band)
    g["s5_c_re"], g["s5_c_im"] = d_cr, -d_ci

    def gate_bwd(r, q):
        _, vjp = jax.vjp(_gated_norm, r[0], r[1], q[0])
        dy_, dz_, dw_ = vjp(r[2])
        return [dy_, dz_], [dw_]

    dy_ssd, dzs, g["ssd_norm_w"] = row(
        "ssd_gate_bwd", gate_bwd, row_ins=gn_rows + [(dyss, 0, SSD_WIDTH)], par_ins=[(ssd_norm_w, 0, SSD_WIDTH)],
        row_outs=[(SSD_WIDTH, F32), (SSD_WIDTH, BF16)], acc_outs=[(1, SSD_WIDTH)])
    dxs, dbm, dcm, ddt, d_dtb, d_alog, d_dsk = _ssd_bwd(xbc_act, proj, states, dy_ssd, dtb, alog, dsk, n_seq, seq_len)
    g["ssd_dt_bias"], g["ssd_a_log"], g["ssd_d"] = _unpad_heads(d_dtb), _unpad_heads(d_alog), _unpad_heads(d_dsk)
    conv_parts = [_conv_bwd("ssd_conv_bwd_x", proj, dxs, conv_w, conv_b, n_rows, seq_len, 0),
                  _conv_bwd("ssd_conv_bwd_b", proj, dbm, conv_w, conv_b, n_rows, seq_len, SSD_WIDTH),
                  _conv_bwd("ssd_conv_bwd_c", proj, dcm, conv_w, conv_b, n_rows, seq_len, SSD_WIDTH + SSD_BC)]
    g["ssd_conv_w"] = jnp.concatenate([c[1] for c in conv_parts], axis=1)
    g["ssd_conv_b"] = jnp.concatenate([c[2] for c in conv_parts], axis=1)

    dproj = jnp.concatenate([c[0] for c in conv_parts] + [du5, dzs, dz5, dg5, dgs, ddt], axis=1)
    g["w_in"] = _unpad_w_in(_matmul("mm_d_w_in", hn, dproj, ta=True))
    dhn = _matmul("mm_d_hn", dproj, w_pad, tb=True)

    def norm_bwd(r, q):
        x_, dhn_, dh1_ = r
        _, vjp = jax.vjp(_rms, x_, q[0])
        dx_, dw_ = vjp(dhn_)
        return [dx_ + dh1_], [dw_]

    dx, g["norm_w"] = row("rms_in_bwd", norm_bwd, row_ins=[(x2, 0, D_MODEL), (dhn, 0, D_MODEL), (dh1, 0, D_MODEL)],
                          par_ins=[(norm_w, 0, D_MODEL)], row_outs=[(D_MODEL, F32)], acc_outs=[(1, D_MODEL)])
    return loss, dx.reshape(x.shape), g


HBM = pl.BlockSpec(memory_space=pltpu.HBM)


def _chip_index(x, y):
    return 2 * x + y


def _gather_chips(shards):
    n = len(shards)

    def body(*refs):
        ins, outs = refs[:n], refs[n:2 * n]
        send_sems, recv_sems, local_sems = refs[2 * n:]
        x, y, c = lax.axis_index("x"), lax.axis_index("y"), lax.axis_index("c")
        me = _chip_index(x, y)
        peers = [(1 - x, y), (x, 1 - y), (1 - x, 1 - y)]
        copies = []
        for t in range(n):
            loc = pltpu.make_async_copy(ins[t], outs[t].at[me], local_sems.at[t])
            loc.start()
            copies.append(loc)
        sends = []
        for t in range(n):
            for k, (px, py) in enumerate(peers):
                cp = pltpu.make_async_remote_copy(
                    src_ref=ins[t], dst_ref=outs[t].at[me], send_sem=send_sems.at[t, k], recv_sem=recv_sems.at[t, k],
                    device_id=(px, py, c), device_id_type=MESH)
                cp.start()
                sends.append(cp)
        for t in range(n):
            for k, (px, py) in enumerate(peers):
                pltpu.make_async_remote_copy(
                    src_ref=ins[t], dst_ref=outs[t].at[_chip_index(px, py)], send_sem=send_sems.at[t, k],
                    recv_sem=recv_sems.at[t, k], device_id=(px, py, c), device_id_type=MESH).wait_recv()
        for cp in sends:
            cp.wait_send()
        for cp in copies:
            cp.wait()

    return pl.pallas_call(
        body, name="gather_weights",
        in_specs=[HBM] * n, out_specs=[HBM] * n,
        out_shape=[jax.ShapeDtypeStruct((N_CHIPS,) + a.shape, a.dtype) for a in shards],
        scratch_shapes=[pltpu.SemaphoreType.DMA((n, 3)), pltpu.SemaphoreType.DMA((n, 3)), pltpu.SemaphoreType.DMA((n,))],
    )(*shards)


def _scatter_chips(slotted, small):
    n = len(slotted)

    def body(*refs):
        ins, sm_in = refs[:n], refs[n]
        outs, sm_out = refs[n + 1:2 * n + 1], refs[2 * n + 1]
        send_sems, recv_sems, local_sems, sm_send, sm_recv = refs[2 * n + 2:]
        x, y, c = lax.axis_index("x"), lax.axis_index("y"), lax.axis_index("c")
        me = _chip_index(x, y)
        dev = 2 * me + c
        peers = [(1 - x, y), (x, 1 - y), (1 - x, 1 - y)]
        local = []
        for t in range(n):
            cp = pltpu.make_async_copy(ins[t].at[me], outs[t].at[me], local_sems.at[t])
            cp.start()
            local.append(cp)
        cp = pltpu.make_async_copy(sm_in, sm_out.at[dev], local_sems.at[n])
        cp.start()
        local.append(cp)
        sends = []
        rel = [(fx, fy, fc) for fx in (0, 1) for fy in (0, 1) for fc in (0, 1)][1:]
        for k, (fx, fy, fc) in enumerate(rel):
            cp = pltpu.make_async_remote_copy(
                src_ref=sm_in, dst_ref=sm_out.at[dev], send_sem=sm_send.at[k], recv_sem=sm_recv.at[k],
                device_id=(x ^ fx, y ^ fy, c ^ fc), device_id_type=MESH)
            cp.start()
            sends.append(cp)
        for t in range(n):
            for k, (px, py) in enumerate(peers):
                cp = pltpu.make_async_remote_copy(
                    src_ref=ins[t].at[_chip_index(px, py)], dst_ref=outs[t].at[me], send_sem=send_sems.at[t, k],
                    recv_sem=recv_sems.at[t, k], device_id=(px, py, c), device_id_type=MESH)
                cp.start()
                sends.append(cp)
        for k, (fx, fy, fc) in enumerate(rel):
            src_dev = 4 * (x ^ fx) + 2 * (y ^ fy) + (c ^ fc)
            pltpu.make_async_remote_copy(
                src_ref=sm_in, dst_ref=sm_out.at[src_dev], send_sem=sm_send.at[k], recv_sem=sm_recv.at[k],
                device_id=(x ^ fx, y ^ fy, c ^ fc), device_id_type=MESH).wait_recv()
        for t in range(n):
            for k, (px, py) in enumerate(peers):
                pltpu.make_async_remote_copy(
                    src_ref=ins[t].at[me], dst_ref=outs[t].at[_chip_index(px, py)], send_sem=send_sems.at[t, k],
                    recv_sem=recv_sems.at[t, k], device_id=(px, py, c), device_id_type=MESH).wait_recv()
        for cp in sends:
            cp.wait_send()
        for cp in local:
            cp.wait()

    return pl.pallas_call(
        body, name="scatter_grads",
        in_specs=[HBM] * (n + 1), out_specs=[HBM] * (n + 1),
        out_shape=[jax.ShapeDtypeStruct(a.shape, a.dtype) for a in slotted]
        + [jax.ShapeDtypeStruct((N_DEV,) + small.shape, small.dtype)],
        scratch_shapes=[pltpu.SemaphoreType.DMA((n, 3)), pltpu.SemaphoreType.DMA((n, 3)),
                        pltpu.SemaphoreType.DMA((n + 1,)), pltpu.SemaphoreType.DMA((7,)), pltpu.SemaphoreType.DMA((7,))],
    )(*slotted, small)


def _swap_sibling(parts):
    n = len(parts)

    def body(*refs):
        ins, outs = refs[:n], refs[n:2 * n]
        send_sems, recv_sems = refs[2 * n:]
        x, y, c = lax.axis_index("x"), lax.axis_index("y"), lax.axis_index("c")
        cps = []
        for t in range(n):
            cp = pltpu.make_async_remote_copy(
                src_ref=ins[t], dst_ref=outs[t], send_sem=send_sems.at[t], recv_sem=recv_sems.at[t],
                device_id=(x, y, 1 - c), device_id_type=MESH)
            cp.start()
            cps.append(cp)
        for cp in cps:
            cp.wait_recv()
        for cp in cps:
            cp.wait_send()

    return pl.pallas_call(
        body, name="swap_sibling",
        in_specs=[HBM] * n, out_specs=[HBM] * n,
        out_shape=[jax.ShapeDtypeStruct(a.shape, a.dtype) for a in parts],
        scratch_shapes=[pltpu.SemaphoreType.DMA((n,)), pltpu.SemaphoreType.DMA((n,))],
    )(*parts)


def _sum_slots(name, a):
    k, r, c = a.shape
    tr = _row_tile(r, (k + 1) * c * 4)

    def body(a_ref, o_ref):
        acc = a_ref[0]
        for i in range(1, k):
            acc = acc + a_ref[i]
        o_ref[...] = acc

    return pl.pallas_call(
        body, name=name, grid=(r // tr,),
        in_specs=[pl.BlockSpec((k, tr, c), lambda i: (0, i, 0))],
        out_specs=pl.BlockSpec((tr, c), lambda i: (i, 0)),
        out_shape=jax.ShapeDtypeStruct((r, c), a.dtype),
        compiler_params=_cparams(("parallel",)),
    )(a)


def _adamw(name, w, m, v, g_parts):
    r, c = w.shape
    ng = len(g_parts)
    tr = _row_tile(r, (7 + ng) * c * 4)
    c1 = 1.0 - ADAM_B1 ** ADAM_STEP
    c2 = 1.0 - ADAM_B2 ** ADAM_STEP

    def body(*refs):
        w_ref, m_ref, v_ref = refs[:3]
        g_refs = refs[3:3 + ng]
        go_ref, d_ref, mo_ref, vo_ref = refs[3 + ng:]
        g = g_refs[0][...]
        for gr in g_refs[1:]:
            g = g + gr[...]
        m_new = ADAM_B1 * m_ref[...] + (1.0 - ADAM_B1) * g
        v_new = ADAM_B2 * v_ref[...] + (1.0 - ADAM_B2) * (g * g)
        go_ref[...] = g
        mo_ref[...] = m_new
        vo_ref[...] = v_new
        d_ref[...] = -ADAM_LR * ((m_new / c1) / (jnp.sqrt(v_new / c2) + ADAM_EPS) + ADAM_WD * w_ref[...])

    spec = pl.BlockSpec((tr, c), lambda i: (i, 0))
    return pl.pallas_call(
        body, name=name, grid=(r // tr,),
        in_specs=[spec] * (3 + ng), out_specs=[spec] * 4,
        out_shape=[jax.ShapeDtypeStruct((r, c), F32)] * 4,
        compiler_params=_cparams(("parallel",)),
    )(w, m, v, *g_parts)


WEIGHTS = ['norm_w', 'w_in', 's5_a_re', 's5_a_im', 's5_b_re', 's5_b_im', 's5_c_re', 's5_c_im', 's5_d', 's5_log_step',
           's5_w_glu', 's5_b_glu', 'ssd_conv_w', 'ssd_conv_b', 'ssd_dt_bias', 'ssd_a_log', 'ssd_d', 'ssd_norm_w',
           'w_br_s5', 'w_br_ssd', 'w_out', 'ple_norm_w', 'w_ple_gate', 'w_ple_proj', 'final_norm_w']
SHARDED = {'w_in': ((1024, IN_PROJ_DIM), 1), 's5_w_glu': ((512, 512), 0), 'ssd_conv_w': ((SSD_CONV, SSD_CONV_DIM), 1),
           'w_br_s5': ((512, 1024), 1), 'w_br_ssd': ((1536, 1024), 0), 'w_out': ((1024, 1024), 0),
           'w_ple_gate': ((1024, 1024), 0), 'w_ple_proj': ((256, 1024), 1)}
SMALL = [n for n in WEIGHTS if n not in SHARDED]


def _shard2d(name, a):
    (r, c), ax = SHARDED[name]
    return a.reshape((r // N_CHIPS, c) if ax == 0 else (r, c // N_CHIPS))


def _unslot(name, a4):
    (r, c), ax = SHARDED[name]
    if ax == 0:
        return a4.reshape(r, c)
    return jnp.transpose(a4, (1, 0, 2)).reshape(r, c)


def _slot(name, full):
    (r, c), ax = SHARDED[name]
    if ax == 0:
        return full.reshape(N_CHIPS, r // N_CHIPS, c)
    return jnp.transpose(full.reshape(r, N_CHIPS, c // N_CHIPS), (1, 0, 2))


def _pack_small(vals):
    flat = jnp.concatenate([v.reshape(-1).astype(F32) for v in vals])
    rows = -(-flat.shape[0] // (256 * 128)) * 256
    return jnp.pad(flat, (0, rows * 128 - flat.shape[0])).reshape(rows, 128)


def _unpack_small(packed, shapes):
    flat = packed.reshape(-1)
    out, off = [], 0
    for sh in shapes:
        n = math.prod(sh)
        out.append(flat[off:off + n].reshape(sh))
        off += n
    return out


def kernel(x, p, norm_w, w_in, s5_a_re, s5_a_im, s5_b_re, s5_b_im, s5_c_re, s5_c_im, s5_d, s5_log_step, s5_w_glu, s5_b_glu, ssd_conv_w, ssd_conv_b, ssd_dt_bias, ssd_a_log, ssd_d, ssd_norm_w, w_br_s5, w_br_ssd, w_out, ple_norm_w, w_ple_gate, w_ple_proj, final_norm_w, loss_target, m_norm_w, m_w_in, m_s5_a_re, m_s5_a_im, m_s5_b_re, m_s5_b_im, m_s5_c_re, m_s5_c_im, m_s5_d, m_s5_log_step, m_s5_w_glu, m_s5_b_glu, m_ssd_conv_w, m_ssd_conv_b, m_ssd_dt_bias, m_ssd_a_log, m_ssd_d, m_ssd_norm_w, m_w_br_s5, m_w_br_ssd, m_w_out, m_ple_norm_w, m_w_ple_gate, m_w_ple_proj, m_final_norm_w, v_norm_w, v_w_in, v_s5_a_re, v_s5_a_im, v_s5_b_re, v_s5_b_im, v_s5_c_re, v_s5_c_im, v_s5_d, v_s5_log_step, v_s5_w_glu, v_s5_b_glu, v_ssd_conv_w, v_ssd_conv_b, v_ssd_dt_bias, v_ssd_a_log, v_ssd_d, v_ssd_norm_w, v_w_br_s5, v_w_br_ssd, v_w_out, v_ple_norm_w, v_w_ple_gate, v_w_ple_proj, v_final_norm_w):
    args = locals()
    wl = {n: args[n] for n in WEIGHTS}
    ml = {n: args["m_" + n] for n in WEIGHTS}
    vl = {n: args["v_" + n] for n in WEIGHTS}
    big = list(SHARDED)

    gathered = _gather_chips([_shard2d(n, wl[n]).astype(BF16 if n != 'ssd_conv_w' else F32) for n in big])
    full = {n: wl[n] for n in SMALL}
    for n, a4 in zip(big, gathered, strict=True):
        full[n] = _unslot(n, a4)

    loss, grad_x, g = _local_step(x, p[0], loss_target, full)

    small_shapes = [(1, 1)] + [wl[n].shape for n in SMALL]
    small_pack = _pack_small([loss] + [g[n] for n in SMALL])
    recv = _scatter_chips([_slot(n, g[n]) for n in big], small_pack)
    chip_sums = [_sum_slots("sum_chips_" + n, r) for n, r in zip(big, recv[:-1], strict=True)]
    small_sum = _sum_slots("sum_small", recv[-1])
    sib_sums = _swap_sibling(chip_sums)

    out_g, out_d, out_m, out_v = {}, {}, {}, {}
    for n, own, sib in zip(big, chip_sums, sib_sums, strict=True):
        res = _adamw("adamw_" + n, _shard2d(n, wl[n]), _shard2d(n, ml[n]), _shard2d(n, vl[n]), [own, sib])
        out_g[n], out_d[n], out_m[n], out_v[n] = (r.reshape(wl[n].shape) for r in res)
    sm = _unpack_small(small_sum, small_shapes)
    loss_total = sm[0].reshape(())
    small_g = _pack_small(sm[1:])
    res = _adamw("adamw_small", _pack_small([wl[n] for n in SMALL]), _pack_small([ml[n] for n in SMALL]),
                 _pack_small([vl[n] for n in SMALL]), [small_g])
    shapes = [wl[n].shape for n in SMALL]
    for dst, packed in zip((out_g, out_d, out_m, out_v), res, strict=True):
        for n, val in zip(SMALL, _unpack_small(packed, shapes), strict=True):
            dst[n] = val

    return (loss_total, grad_x, *[out_g[n] for n in WEIGHTS], *[out_d[n] for n in WEIGHTS],
            *[out_m[n] for n in WEIGHTS], *[out_v[n] for n in WEIGHTS])
```

```python
import functools
import math

import jax
import jax.numpy as jnp
from jax import lax
from jax.experimental import pallas as pl
from jax.experimental.pallas import tpu as pltpu

F32 = jnp.float32
BF16 = jnp.bfloat16
MESH = pl.DeviceIdType.MESH

D_MODEL = 1024
PLE_DIM = 256
RMS_EPS = 1e-6
S5_WIDTH = 512
S5_GROUP = 16
S5_GROUPS = 32
S5_STATE = 64
S5_N = S5_GROUPS * S5_STATE
S5_LB = 512
S5_NJ = S5_N // S5_LB
S5_TB = 256
S5_LOG_TB = 8
SSD_WIDTH = 1536
SSD_HEADDIM = 64
SSD_HEADS = 24
SSD_GROUPS = 4
SSD_HPG = 6
SSD_STATE = 128
SSD_CONV = 4
SSD_CHUNK = 128
SSD_BC = 512
SSD_CONV_DIM = 2560
GROUP_W = SSD_WIDTH // SSD_GROUPS
N_CHIPS = 4
N_DEV = 8

OFF_XBC, OFF_U5, OFF_ZS, OFF_Z5, OFF_G5, OFF_GS, OFF_DT = 0, 2560, 3072, 4608, 5120, 6144, 7168
PROJ_W = 7680
IN_PROJ_DIM = 7192

ADAM_LR, ADAM_B1, ADAM_B2, ADAM_EPS, ADAM_WD, ADAM_STEP = 0.001, 0.9, 0.999, 1e-08, 0.01, 10

VMEM_LIMIT = 56 * 1024 * 1024


def _pick(n, cands):
    for c in cands:
        if n % c == 0:
            return c
    return n


ROW_BLOCK_BYTES = 8 * 1024 * 1024


def _row_tile(r, bytes_per_row):
    for t in (r, 4096, 2048, 1024, 512, 256, 128, 64, 32, 16, 8):
        if t <= r and r % t == 0 and t * bytes_per_row <= ROW_BLOCK_BYTES:
            return t
    return r


def _cparams(sem):
    return pltpu.CompilerParams(dimension_semantics=sem, vmem_limit_bytes=VMEM_LIMIT)


def _dg(a, b, ca, cb):
    return lax.dot_general(a.astype(BF16), b.astype(BF16), (((ca,), (cb,)), ((), ())), preferred_element_type=F32)


@jax.custom_vjp
def dot_nn(a, b):
    return _dg(a, b, 1, 0)


@jax.custom_vjp
def dot_nt(a, b):
    return _dg(a, b, 1, 1)


@jax.custom_vjp
def dot_tn(a, b):
    return _dg(a, b, 0, 0)


dot_nn.defvjp(lambda a, b: (_dg(a, b, 1, 0), (a, b)), lambda r, g: (_dg(g, r[1], 1, 1), _dg(r[0], g, 0, 0)))
dot_nt.defvjp(lambda a, b: (_dg(a, b, 1, 1), (a, b)), lambda r, g: (_dg(g, r[1], 1, 0), _dg(g, r[0], 0, 0)))
dot_tn.defvjp(lambda a, b: (_dg(a, b, 0, 0), (a, b)), lambda r, g: (_dg(r[1], g, 1, 1), _dg(r[0], g, 1, 0)))


MM_VMEM_BUDGET = 30 * 1024 * 1024


def _mm_tiles(m, n, k, sa, sb, so):
    best, best_key = None, None
    for tm in (1024, 512, 256, 128, 64, 32, 16, 8):
        if m % tm:
            continue
        for tn in (2048, 1536, 1280, 1024, 768, 640, 512, 384, 256, 128):
            if n % tn:
                continue
            for tk in (k, 2048, 1536, 1280, 1024, 768, 512, 256, 128):
                if k % tk or tk > max(k, 128) or (tk == k and k > 2048 and k % 128 == 0):
                    continue
                need = 2 * (tm * tk * sa + tk * tn * sb + tm * tn * so) + (tm * tn * 4 if tk < k else 0)
                if need > MM_VMEM_BUDGET:
                    continue
                key = (tm * tn * tk, tk)
                if best_key is None or key > best_key:
                    best, best_key = (tm, tn, tk), key
    assert best is not None, (m, n, k)
    return best


def _matmul(name, a, b, *, ta=False, tb=False, a_win=None, out_dtype=F32):
    a_off, a_w = a_win if a_win is not None else (0, a.shape[1])
    if ta:
        kdim, m = a.shape[0], a_w
    else:
        m, kdim = a.shape[0], a_w
    n = b.shape[0] if tb else b.shape[1]
    assert (b.shape[1] if tb else b.shape[0]) == kdim, (name, a.shape, b.shape)
    tm, tn, tk = _mm_tiles(m, n, kdim, a.dtype.itemsize, b.dtype.itemsize, jnp.dtype(out_dtype).itemsize)
    nk = kdim // tk
    if ta:
        assert a_off % tm == 0
        a_spec = pl.BlockSpec((tk, tm), lambda i, j, k: (k, i + a_off // tm))
    else:
        assert a_off % tk == 0
        a_spec = pl.BlockSpec((tm, tk), lambda i, j, k: (i, k + a_off // tk))
    if tb:
        b_spec = pl.BlockSpec((tn, tk), lambda i, j, k: (j, k))
    else:
        b_spec = pl.BlockSpec((tk, tn), lambda i, j, k: (k, j))
    ca, cb = (0 if ta else 1), (1 if tb else 0)

    def body(a_ref, b_ref, o_ref, *acc):
        if nk == 1:
            o_ref[...] = _dg(a_ref[...], b_ref[...], ca, cb).astype(o_ref.dtype)
            return
        (acc_ref,) = acc
        k = pl.program_id(2)

        @pl.when(k == 0)
        def _():
            acc_ref[...] = jnp.zeros_like(acc_ref)

        acc_ref[...] += _dg(a_ref[...], b_ref[...], ca, cb)

        @pl.when(k == nk - 1)
        def _():
            o_ref[...] = acc_ref[...].astype(o_ref.dtype)

    return pl.pallas_call(
        body, name=name, grid=(m // tm, n // tn, nk),
        in_specs=[a_spec, b_spec],
        out_specs=pl.BlockSpec((tm, tn), lambda i, j, k: (i, j)),
        out_shape=jax.ShapeDtypeStruct((m, n), out_dtype),
        scratch_shapes=[pltpu.VMEM((tm, tn), F32)] if nk > 1 else [],
        compiler_params=_cparams(("parallel", "parallel", "arbitrary")),
    )(a, b)


BAND = 128


def _band_matmul(name, kind, a, b, *, a_blk0=0, out_dtype=F32):
    n_rows = a.shape[0]
    blk = 2 * S5_LB
    tm = _pick(n_rows, (1024, 512, 256))
    if kind == "nn":
        grid = (n_rows // tm, S5_NJ)
        in_specs = [pl.BlockSpec((tm, BAND), lambda i, j: (i, a_blk0 + j)), pl.BlockSpec((BAND, blk), lambda i, j: (j, 0))]
        out_spec = pl.BlockSpec((tm, blk), lambda i, j: (i, j))
        out_shape = (n_rows, S5_NJ * blk)
        sem = ("parallel", "parallel")

        def body(a_ref, b_ref, o_ref):
            o_ref[...] = _dg(a_ref[...], b_ref[...], 1, 0).astype(o_ref.dtype)
    elif kind == "nt":
        grid = (n_rows // tm, S5_NJ)
        in_specs = [pl.BlockSpec((tm, blk), lambda i, j: (i, j)), pl.BlockSpec((BAND, blk), lambda i, j: (j, 0))]
        out_spec = pl.BlockSpec((tm, BAND), lambda i, j: (i, j))
        out_shape = (n_rows, S5_NJ * BAND)
        sem = ("parallel", "parallel")

        def body(a_ref, b_ref, o_ref):
            o_ref[...] = _dg(a_ref[...], b_ref[...], 1, 1).astype(o_ref.dtype)
    else:
        grid = (S5_NJ, n_rows // tm)
        in_specs = [pl.BlockSpec((tm, BAND), lambda j, k: (k, a_blk0 + j)), pl.BlockSpec((tm, blk), lambda j, k: (k, j))]
        out_spec = pl.BlockSpec((BAND, blk), lambda j, k: (j, 0))
        out_shape = (S5_NJ * BAND, blk)
        sem = ("parallel", "arbitrary")

        def body(a_ref, b_ref, o_ref):
            @pl.when(pl.program_id(1) == 0)
            def _():
                o_ref[...] = jnp.zeros_like(o_ref)

            o_ref[...] += _dg(a_ref[...], b_ref[...], 0, 0)

    return pl.pallas_call(
        body, name=name, grid=grid, in_specs=in_specs, out_specs=out_spec,
        out_shape=jax.ShapeDtypeStruct(out_shape, out_dtype), compiler_params=_cparams(sem),
    )(a, b)


def _rowwise(name, fn, n_rows, tr, row_ins, par_ins, row_outs, acc_outs=()):
    nr, npar, no, na = len(row_ins), len(par_ins), len(row_outs), len(acc_outs)
    in_specs = []
    for arr, off, w in row_ins:
        assert off % w == 0 and arr.shape[0] == n_rows, (name, arr.shape, off, w)
        in_specs.append(pl.BlockSpec((tr, w), functools.partial(lambda i, c: (i, c), c=off // w)))
    for arr, off, w in par_ins:
        assert off % w == 0
        in_specs.append(pl.BlockSpec((arr.shape[0], w), functools.partial(lambda i, c: (0, c), c=off // w)))
    out_specs = [pl.BlockSpec((tr, w), lambda i: (i, 0)) for w, _ in row_outs]
    out_specs += [pl.BlockSpec((r, w), lambda i: (0, 0)) for r, w in acc_outs]
    out_shape = [jax.ShapeDtypeStruct((n_rows, w), dt) for w, dt in row_outs]
    out_shape += [jax.ShapeDtypeStruct((r, w), F32) for r, w in acc_outs]

    def body(*refs):
        rows = [r[...] for r in refs[:nr]]
        pars = [r[...] for r in refs[nr:nr + npar]]
        o_refs = refs[nr + npar:nr + npar + no]
        a_refs = refs[nr + npar + no:]
        outs, accs = fn(rows, pars)
        for o_ref, o in zip(o_refs, outs, strict=True):
            o_ref[...] = o.astype(o_ref.dtype)
        if na:
            @pl.when(pl.program_id(0) == 0)
            def _():
                for a_ref in a_refs:
                    a_ref[...] = jnp.zeros_like(a_ref)

            for a_ref, a in zip(a_refs, accs, strict=True):
                a_ref[...] += jnp.broadcast_to(a, a_ref.shape)

    res = pl.pallas_call(
        body, name=name, grid=(n_rows // tr,),
        in_specs=in_specs, out_specs=out_specs, out_shape=out_shape,
        compiler_params=_cparams(("arbitrary",) if na else ("parallel",)),
    )(*[a for a, _, _ in row_ins], *[a for a, _, _ in par_ins])
    return res


def _rms(x, w):
    return x * lax.rsqrt(jnp.mean(x * x, axis=-1, keepdims=True) + RMS_EPS) * w


def _gated_norm(y, z, w):
    outs = []
    for g in range(SSD_GROUPS):
        sl = slice(g * GROUP_W, (g + 1) * GROUP_W)
        yg = y[:, sl] * jax.nn.silu(z[:, sl])
        outs.append(yg * lax.rsqrt(jnp.mean(yg * yg, axis=-1, keepdims=True) + RMS_EPS) * w[:, sl])
    return jnp.concatenate(outs, axis=-1)


def _s5_out(yc, u, z, t, d, bg):
    ge = jax.nn.gelu(yc + d * u)
    return ge * jax.nn.sigmoid(t + bg) * jax.nn.silu(z)


def _merge(g5, gs, m5, ms):
    return jax.nn.sigmoid(g5) * m5 + jax.nn.sigmoid(gs) * ms


def _head_loss(h1, pgl, pp, fw, tgt):
    h2 = h1 + jax.nn.sigmoid(pgl) * pp
    err = _rms(h2, fw) - tgt
    per_row = 0.5 * jnp.mean(err * err, axis=-1, keepdims=True)
    return jnp.sum(per_row, axis=0, keepdims=True)


def _s5_disc(a_re, a_im, log_step, b_re2, b_im2, expand):
    step = jnp.exp(log_step)
    mag = jnp.exp(a_re * step)
    lb_re = mag * jnp.cos(a_im * step)
    lb_im = mag * jnp.sin(a_im * step)
    den = a_re * a_re + a_im * a_im
    n_re = lb_re - 1.0
    f_re = (n_re * a_re + lb_im * a_im) / den
    f_im = (lb_im * a_re - n_re * a_im) / den
    hi = lax.Precision.HIGHEST
    fr = jnp.dot(expand, f_re, precision=hi, preferred_element_type=F32)
    fi = jnp.dot(expand, f_im, precision=hi, preferred_element_type=F32)
    return lb_re, lb_im, fr * b_re2 - fi * b_im2, fr * b_im2 + fi * b_re2


def _s5_params_fwd(a_re, a_im, log_step, b_re2, b_im2, expand):
    gp = a_re.shape

    def body(ar, ai, ls, br, bi, ex, pr_ref, pi_ref, bbr_ref, bbi_ref):
        lr, li, bbr, bbi = _s5_disc(ar[...], ai[...], ls[...], br[...], bi[...], ex[...])
        bbr_ref[...] = bbr
        bbi_ref[...] = bbi
        for k in range(S5_LOG_TB):
            pr_ref[k] = lr
            pi_ref[k] = li
            lr, li = lr * lr - li * li, 2.0 * lr * li

    return pl.pallas_call(
        body, name="s5_params_fwd",
        out_shape=(jax.ShapeDtypeStruct((S5_LOG_TB,) + gp, F32), jax.ShapeDtypeStruct((S5_LOG_TB,) + gp, F32),
                   jax.ShapeDtypeStruct(b_re2.shape, F32), jax.ShapeDtypeStruct(b_re2.shape, F32)),
    )(a_re, a_im, log_step, b_re2, b_im2, expand)


def _s5_params_bwd(a_re, a_im, log_step, b_re2, b_im2, expand, d_lr, d_li, d_bbr, d_bbi):
    def body(ar, ai, ls, br, bi, ex, glr, gli, gbr, gbi, dar, dai, dls, dbr, dbi):
        _, vjp = jax.vjp(lambda *p: _s5_disc(*p, ex[...]), ar[...], ai[...], ls[...], br[...], bi[...])
        g = vjp((glr[...], gli[...], gbr[...], gbi[...]))
        for ref, val in zip((dar, dai, dls, dbr, dbi), g, strict=True):
            ref[...] = val

    return pl.pallas_call(
        body, name="s5_params_bwd",
        out_shape=tuple(jax.ShapeDtypeStruct(v.shape, F32) for v in (a_re, a_im, log_step, b_re2, b_im2)),
    )(a_re, a_im, log_step, b_re2, b_im2, expand, d_lr, d_li, d_bbr, d_bbi)


def _scan_block(xr, xi, lp, cr, ci, reverse):
    tb = xr.shape[0]
    row = lax.broadcasted_iota(jnp.int32, xr.shape, 0)
    l_r, l_i = lp[0:1, :S5_LB], lp[0:1, S5_LB:]
    edge = (row == tb - 1) if reverse else (row == 0)
    xr = xr + jnp.where(edge, l_r * cr - l_i * ci, 0.0)
    xi = xi + jnp.where(edge, l_r * ci + l_i * cr, 0.0)
    for k in range(S5_LOG_TB):
        sh = 1 << k
        a_r, a_i = lp[k:k + 1, :S5_LB], lp[k:k + 1, S5_LB:]
        if reverse:
            keep = row < tb - sh
            pr = jnp.where(keep, pltpu.roll(xr, tb - sh, 0), 0.0)
            pi = jnp.where(keep, pltpu.roll(xi, tb - sh, 0), 0.0)
        else:
            keep = row >= sh
            pr = jnp.where(keep, pltpu.roll(xr, sh, 0), 0.0)
            pi = jnp.where(keep, pltpu.roll(xi, sh, 0), 0.0)
        xr, xi = xr + a_r * pr - a_i * pi, xi + a_r * pi + a_i * pr
    return xr, xi


def _s5_scan_fwd(bu, lam_pow, n_seq, seq_len):
    n_t = seq_len // S5_TB
    blk = 2 * S5_LB

    def body(bu_ref, lp_ref, s_ref, cr, ci):
        @pl.when(pl.program_id(2) == 0)
        def _():
            cr[...] = jnp.zeros_like(cr)
            ci[...] = jnp.zeros_like(ci)

        xr, xi = _scan_block(bu_ref[:, :S5_LB], bu_ref[:, S5_LB:], lp_ref[...], cr[...], ci[...], False)
        s_ref[:, :S5_LB] = xr
        s_ref[:, S5_LB:] = xi
        cr[...] = xr[S5_TB - 1:S5_TB, :]
        ci[...] = xi[S5_TB - 1:S5_TB, :]

    return pl.pallas_call(
        body, name="s5_scan_fwd", grid=(S5_NJ, n_seq, n_t),
        in_specs=[pl.BlockSpec((S5_TB, blk), lambda j, b, t: (b * n_t + t, j)),
                  pl.BlockSpec((S5_LOG_TB, blk), lambda j, b, t: (0, j))],
        out_specs=pl.BlockSpec((S5_TB, blk), lambda j, b, t: (b * n_t + t, j)),
        out_shape=jax.ShapeDtypeStruct(bu.shape, F32),
        scratch_shapes=[pltpu.VMEM((1, S5_LB), F32), pltpu.VMEM((1, S5_LB), F32)],
        compiler_params=_cparams(("parallel", "parallel", "arbitrary")),
    )(bu, lam_pow)


def _s5_scan_bwd(ds, s, lam_pow_conj, n_seq, seq_len):
    n_t = seq_len // S5_TB
    blk = 2 * S5_LB
    halo_per_blk = S5_TB // 8

    def rows(j, b, t):
        return (b * n_t + (n_t - 1 - t), j)

    def halo(j, b, t):
        return (jnp.maximum((b * n_t + (n_t - 1 - t)) * halo_per_blk - 1, 0), j)

    def body(ds_ref, s_ref, h_ref, lp_ref, g_ref, dl_ref, cr, ci):
        b, t = pl.program_id(1), pl.program_id(2)

        @pl.when(t == 0)
        def _():
            cr[...] = jnp.zeros_like(cr)
            ci[...] = jnp.zeros_like(ci)

        @pl.when((b == 0) & (t == 0))
        def _():
            dl_ref[...] = jnp.zeros_like(dl_ref)

        gr, gi = _scan_block(ds_ref[:, :S5_LB], ds_ref[:, S5_LB:], lp_ref[...], cr[...], ci[...], True)
        g_ref[:, :S5_LB] = gr.astype(g_ref.dtype)
        g_ref[:, S5_LB:] = gi.astype(g_ref.dtype)
        cr[...] = gr[0:1, :]
        ci[...] = gi[0:1, :]
        row = lax.broadcasted_iota(jnp.int32, gr.shape, 0)
        first_blk = t == n_t - 1
        h_r = jnp.where(first_blk, 0.0, h_ref[7:8, :S5_LB])
        h_i = jnp.where(first_blk, 0.0, h_ref[7:8, S5_LB:])
        sp_r = jnp.where(row == 0, h_r, pltpu.roll(s_ref[:, :S5_LB], 1, 0))
        sp_i = jnp.where(row == 0, h_i, pltpu.roll(s_ref[:, S5_LB:], 1, 0))
        dl_ref[:, :S5_LB] += jnp.sum(gr * sp_r + gi * sp_i, axis=0, keepdims=True)
        dl_ref[:, S5_LB:] += jnp.sum(gi * sp_r - gr * sp_i, axis=0, keepdims=True)

    return pl.pallas_call(
        body, name="s5_scan_bwd", grid=(S5_NJ, n_seq, n_t),
        in_specs=[pl.BlockSpec((S5_TB, blk), rows), pl.BlockSpec((S5_TB, blk), rows),
                  pl.BlockSpec((8, blk), halo), pl.BlockSpec((S5_LOG_TB, blk), lambda j, b, t: (0, j))],
        out_specs=[pl.BlockSpec((S5_TB, blk), rows), pl.BlockSpec((1, blk), lambda j, b, t: (0, j))],
        out_shape=[jax.ShapeDtypeStruct(ds.shape, BF16), jax.ShapeDtypeStruct((1, ds.shape[1]), F32)],
        scratch_shapes=[pltpu.VMEM((1, S5_LB), F32), pltpu.VMEM((1, S5_LB), F32)],
        compiler_params=_cparams(("parallel", "arbitrary", "arbitrary")),
    )(ds, s, s, lam_pow_conj)


CONV_TR = 256
CONV_CW = 512


def _shift_down(x, halo, k):
    if k == 0:
        return x
    row8 = lax.broadcasted_iota(jnp.int32, halo.shape, 0)
    rolled = pltpu.roll(x, k, 0)
    top = jnp.where(row8 < k, pltpu.roll(halo, k, 0), rolled[:8])
    if x.shape[0] == 8:
        return top
    return jnp.concatenate([top, rolled[8:]], axis=0)


def _shift_up(x, halo, k):
    if k == 0:
        return x
    n = x.shape[0]
    row8 = lax.broadcasted_iota(jnp.int32, halo.shape, 0)
    rolled = pltpu.roll(x, n - k, 0)
    bot = jnp.where(row8 >= 8 - k, pltpu.roll(halo, 8 - k, 0), rolled[n - 8:])
    if n == 8:
        return bot
    return jnp.concatenate([rolled[:n - 8], bot], axis=0)


def _conv_pre(x, halo, w, b):
    acc = b + w[SSD_CONV - 1:SSD_CONV, :] * x
    for k in range(SSD_CONV - 1):
        acc = acc + w[k:k + 1, :] * _shift_down(x, halo, SSD_CONV - 1 - k)
    return acc


def _conv_specs(seq_len, col_off):
    lt = seq_len // CONV_TR
    cb = col_off // CONV_CW
    cur = pl.BlockSpec((CONV_TR, CONV_CW), lambda j, i: (i, j + cb))
    prev = pl.BlockSpec((8, CONV_CW), lambda j, i: (jnp.maximum(i * (CONV_TR // 8) - 1, 0), j + cb))
    return lt, cur, prev


def _conv_fwd(proj, conv_w, conv_b, n_rows, seq_len):
    lt, cur, prev = _conv_specs(seq_len, OFF_XBC)

    def body(x_ref, h_ref, w_ref, b_ref, o_ref):
        halo = jnp.where(pl.program_id(1) % lt == 0, 0.0, h_ref[...])
        o_ref[...] = jax.nn.silu(_conv_pre(x_ref[...], halo, w_ref[...], b_ref[...]))

    return pl.pallas_call(
        body, name="ssd_conv_fwd", grid=(SSD_CONV_DIM // CONV_CW, n_rows // CONV_TR),
        in_specs=[cur, prev, pl.BlockSpec((SSD_CONV, CONV_CW), lambda j, i: (0, j)),
                  pl.BlockSpec((1, CONV_CW), lambda j, i: (0, j))],
        out_specs=pl.BlockSpec((CONV_TR, CONV_CW), lambda j, i: (i, j)),
        out_shape=jax.ShapeDtypeStruct((n_rows, SSD_CONV_DIM), F32),
        compiler_params=_cparams(("parallel", "parallel")),
    )(proj, proj, conv_w, conv_b)


def _conv_bwd(name, proj, d_act, conv_w, conv_b, n_rows, seq_len, col_off):
    width = d_act.shape[1]
    lt, cur, prev = _conv_specs(seq_len, OFF_XBC + col_off)
    n_blk = n_rows // CONV_TR
    cb = (OFF_XBC + col_off) // CONV_CW
    pb = col_off // CONV_CW
    nxt = pl.BlockSpec((8, CONV_CW), lambda j, i: (jnp.minimum((i + 1) * (CONV_TR // 8), n_rows // 8 - 1), j + cb))
    d_cur = pl.BlockSpec((CONV_TR, CONV_CW), lambda j, i: (i, j))
    d_nxt = pl.BlockSpec((8, CONV_CW), lambda j, i: (jnp.minimum((i + 1) * (CONV_TR // 8), n_rows // 8 - 1), j))

    def dsilu(pre):
        sg = jax.nn.sigmoid(pre)
        return sg * (1.0 + pre * (1.0 - sg))

    def body(x_ref, hp_ref, hn_ref, d_ref, dn_ref, w_ref, b_ref, dx_ref, dw_ref, db_ref):
        i = pl.program_id(1)
        x, w, b = x_ref[...], w_ref[...], b_ref[...]
        halo_p = jnp.where(i % lt == 0, 0.0, hp_ref[...])
        at_end = i % lt == lt - 1
        dpre = d_ref[...] * dsilu(_conv_pre(x, halo_p, w, b))
        pre_n = _conv_pre(hn_ref[...], x[CONV_TR - 8:, :], w, b)
        dpre_n = jnp.where(at_end, 0.0, dn_ref[...] * dsilu(pre_n))
        dx = w[SSD_CONV - 1:SSD_CONV, :] * dpre
        for k in range(SSD_CONV - 1):
            dx = dx + w[k:k + 1, :] * _shift_up(dpre, dpre_n, SSD_CONV - 1 - k)
        dx_ref[...] = dx.astype(dx_ref.dtype)

        @pl.when(i == 0)
        def _():
            dw_ref[...] = jnp.zeros_like(dw_ref)
            db_ref[...] = jnp.zeros_like(db_ref)

        for k in range(SSD_CONV):
            xs = _shift_down(x, halo_p, SSD_CONV - 1 - k)
            dw_ref[k:k + 1, :] += jnp.sum(dpre * xs, axis=0, keepdims=True)
        db_ref[...] += jnp.sum(dpre, axis=0, keepdims=True)

    return pl.pallas_call(
        body, name=name, grid=(width // CONV_CW, n_blk),
        in_specs=[cur, prev, nxt, d_cur, d_nxt,
                  pl.BlockSpec((SSD_CONV, CONV_CW), lambda j, i: (0, j + pb)),
                  pl.BlockSpec((1, CONV_CW), lambda j, i: (0, j + pb))],
        out_specs=[pl.BlockSpec((CONV_TR, CONV_CW), lambda j, i: (i, j)),
                   pl.BlockSpec((SSD_CONV, CONV_CW), lambda j, i: (0, j)),
                   pl.BlockSpec((1, CONV_CW), lambda j, i: (0, j))],
        out_shape=[jax.ShapeDtypeStruct((n_rows, width), BF16),
                   jax.ShapeDtypeStruct((SSD_CONV, width), F32), jax.ShapeDtypeStruct((1, width), F32)],
        compiler_params=_cparams(("parallel", "arbitrary")),
    )(proj, proj, proj, d_act, d_act, conv_w, conv_b)


def _ssd_chunk(xs, bm, cm, dtr, prev, dtb, alog, dsk, tri, causal):
    dt = jax.nn.softplus(dtr + dtb)
    da = dt * (-jnp.exp(alog))
    acum = jnp.dot(tri, da, precision=lax.Precision.HIGHEST, preferred_element_type=F32)
    acum_t = acum.T
    alast = acum[SSD_CHUNK - 1:SSD_CHUNK, :]
    scores = dot_nt(cm, bm)
    ys, new = [], []
    for j in range(SSD_HPG):
        xh = xs[:, j * SSD_HEADDIM:(j + 1) * SSD_HEADDIM]
        xdt = xh * dt[:, j:j + 1]
        ac = acum[:, j:j + 1]
        lmat = jnp.exp(jnp.where(causal, ac - acum_t[j:j + 1, :], -jnp.inf))
        y_diag = dot_nn(scores * lmat, xdt)
        st = dot_tn(xdt * jnp.exp(alast[:, j:j + 1] - ac), bm)
        y_off = dot_nt(cm, prev[j]) * jnp.exp(ac)
        ys.append(y_diag + y_off + xh * dsk[:, j:j + 1])
        new.append(prev[j] * jnp.exp(alast[:, j:j + 1]) + st)
    return jnp.concatenate(ys, axis=-1), new


def _ssd_consts():
    r = lax.broadcasted_iota(jnp.int32, (SSD_CHUNK, SSD_CHUNK), 0)
    c = lax.broadcasted_iota(jnp.int32, (SSD_CHUNK, SSD_CHUNK), 1)
    return (r >= c).astype(F32), r >= c


def _ssd_specs(n_c, reverse):
    def cidx(c):
        return n_c - 1 - c if reverse else c

    xs = pl.BlockSpec((SSD_CHUNK, GROUP_W), lambda g, b, c: (b * n_c + cidx(c), g))
    bm = pl.BlockSpec((SSD_CHUNK, SSD_STATE), lambda g, b, c: (b * n_c + cidx(c), SSD_WIDTH // SSD_STATE + g))
    cm = pl.BlockSpec((SSD_CHUNK, SSD_STATE), lambda g, b, c: (b * n_c + cidx(c), (SSD_WIDTH + SSD_BC) // SSD_STATE + g))
    dt = pl.BlockSpec((SSD_CHUNK, 128), lambda g, b, c: (b * n_c + cidx(c), OFF_DT // 128 + g))
    par = pl.BlockSpec((1, 128), lambda g, b, c: (0, g))
    st = pl.BlockSpec((1, 1, SSD_HPG, SSD_HEADDIM, SSD_STATE), lambda g, b, c: (b, cidx(c), g, 0, 0))
    return xs, bm, cm, dt, par, st


def _ssd_fwd(xbc_act, proj, dtb, alog, dsk, n_seq, seq_len):
    n_c = seq_len // SSD_CHUNK
    xs_s, bm_s, cm_s, dt_s, par_s, st_s = _ssd_specs(n_c, False)

    def body(xs_ref, bm_ref, cm_ref, dt_ref, dtb_ref, al_ref, dk_ref, y_ref, st_ref, state):
        @pl.when(pl.program_id(2) == 0)
        def _():
            state[...] = jnp.zeros_like(state)

        tri, causal = _ssd_consts()
        prev = [state[j] for j in range(SSD_HPG)]
        for j in range(SSD_HPG):
            st_ref[0, 0, j] = prev[j]
        y, new = _ssd_chunk(xs_ref[...], bm_ref[...], cm_ref[...], dt_ref[...], prev,
                            dtb_ref[...], al_ref[...], dk_ref[...], tri, causal)
        y_ref[...] = y
        for j in range(SSD_HPG):
            state[j] = new[j]

    return pl.pallas_call(
        body, name="ssd_fwd", grid=(SSD_GROUPS, n_seq, n_c),
        in_specs=[xs_s, bm_s, cm_s, dt_s, par_s, par_s, par_s],
        out_specs=[pl.BlockSpec((SSD_CHUNK, GROUP_W), lambda g, b, c: (b * n_c + c, g)), st_s],
        out_shape=[jax.ShapeDtypeStruct((n_seq * seq_len, SSD_WIDTH), F32),
                   jax.ShapeDtypeStruct((n_seq, n_c, SSD_HEADS, SSD_HEADDIM, SSD_STATE), F32)],
        scratch_shapes=[pltpu.VMEM((SSD_HPG, SSD_HEADDIM, SSD_STATE), F32)],
        compiler_params=_cparams(("parallel", "parallel", "arbitrary")),
    )(xbc_act, xbc_act, xbc_act, proj, dtb, alog, dsk)


def _ssd_bwd(xbc_act, proj, states, dy, dtb, alog, dsk, n_seq, seq_len):
    n_c = seq_len // SSD_CHUNK
    n_rows = n_seq * seq_len
    xs_s, bm_s, cm_s, dt_s, par_s, st_s = _ssd_specs(n_c, True)

    def rows(w):
        return pl.BlockSpec((SSD_CHUNK, w), lambda g, b, c: (b * n_c + (n_c - 1 - c), g))

    def body(xs_ref, bm_ref, cm_ref, dt_ref, st_ref, dy_ref, dtb_ref, al_ref, dk_ref,
             dxs_ref, dbm_ref, dcm_ref, ddt_ref, ddtb_ref, dal_ref, ddk_ref, dstate):
        b, c = pl.program_id(1), pl.program_id(2)

        @pl.when(c == 0)
        def _():
            dstate[...] = jnp.zeros_like(dstate)

        @pl.when((b == 0) & (c == 0))
        def _():
            ddtb_ref[...] = jnp.zeros_like(ddtb_ref)
            dal_ref[...] = jnp.zeros_like(dal_ref)
            ddk_ref[...] = jnp.zeros_like(ddk_ref)

        tri, causal = _ssd_consts()
        prev = [st_ref[0, 0, j] for j in range(SSD_HPG)]
        _, vjp = jax.vjp(
            lambda xs, bm, cm, dtr, prev, dtb, alog, dsk: _ssd_chunk(xs, bm, cm, dtr, prev, dtb, alog, dsk, tri, causal),
            xs_ref[...], bm_ref[...], cm_ref[...], dt_ref[...], prev, dtb_ref[...], al_ref[...], dk_ref[...])
        dxs, dbm, dcm, ddtr, dprev, ddtb, dal, ddk = vjp((dy_ref[...], [dstate[j] for j in range(SSD_HPG)]))
        dxs_ref[...] = dxs
        dbm_ref[...] = dbm
        dcm_ref[...] = dcm
        ddt_ref[...] = ddtr.astype(ddt_ref.dtype)
        ddtb_ref[...] += ddtb
        dal_ref[...] += dal
        ddk_ref[...] += ddk
        for j in range(SSD_HPG):
            dstate[j] = dprev[j]

    acc = pl.BlockSpec((1, 128), lambda g, b, c: (0, g))
    return pl.pallas_call(
        body, name="ssd_bwd", grid=(SSD_GROUPS, n_seq, n_c),
        in_specs=[xs_s, bm_s, cm_s, dt_s, st_s, rows(GROUP_W), par_s, par_s, par_s],
        out_specs=[rows(GROUP_W), rows(SSD_STATE), rows(SSD_STATE), rows(128), acc, acc, acc],
        out_shape=[jax.ShapeDtypeStruct((n_rows, SSD_WIDTH), F32), jax.ShapeDtypeStruct((n_rows, SSD_BC), F32),
                   jax.ShapeDtypeStruct((n_rows, SSD_BC), F32), jax.ShapeDtypeStruct((n_rows, 4 * 128), BF16),
                   jax.ShapeDtypeStruct((1, 512), F32), jax.ShapeDtypeStruct((1, 512), F32),
                   jax.ShapeDtypeStruct((1, 512), F32)],
        scratch_shapes=[pltpu.VMEM((SSD_HPG, SSD_HEADDIM, SSD_STATE), F32)],
        compiler_params=_cparams(("parallel", "arbitrary", "arbitrary")),
    )(xbc_act, xbc_act, xbc_act, proj, states, dy, dtb, alog, dsk)


def _pad_heads(v):
    return jnp.pad(v.reshape(SSD_GROUPS, SSD_HPG), ((0, 0), (0, 128 - SSD_HPG))).reshape(1, SSD_GROUPS * 128)


def _unpad_heads(v):
    return v.reshape(SSD_GROUPS, 128)[:, :SSD_HPG].reshape(1, SSD_HEADS)


def _state_cols(v):
    re, im = v
    lead = re.shape[:-1]
    re = re.reshape(lead + (S5_NJ, 1, S5_LB))
    im = im.reshape(lead + (S5_NJ, 1, S5_LB))
    return jnp.concatenate([re, im], axis=-2).reshape(lead + (2 * S5_N,))


def _state_uncols(v):
    lead = v.shape[:-1]
    v = v.reshape(lead + (S5_NJ, 2, S5_LB))
    return v[..., 0, :].reshape(lead + (S5_N,)), v[..., 1, :].reshape(lead + (S5_N,))


GROUPS_PER_BAND = BAND // S5_GROUP


def _band(w2_re, w2_im):
    gh = S5_GROUPS * S5_GROUP
    rg = ((jnp.arange(gh) // S5_GROUP) % GROUPS_PER_BAND)[:, None, None]
    cg = jnp.arange(GROUPS_PER_BAND)[None, :, None]
    parts = [jnp.where(rg == cg, v[:, None, :], 0.0).reshape(gh, S5_LB) for v in (w2_re, w2_im)]
    return jnp.concatenate(parts, axis=1)


def _band_take(wb):
    gh = S5_GROUPS * S5_GROUP
    w4 = wb.reshape(gh, 2, GROUPS_PER_BAND, S5_STATE)
    sel = w4[jnp.arange(gh), :, (jnp.arange(gh) // S5_GROUP) % GROUPS_PER_BAND, :]
    return sel[:, 0, :], sel[:, 1, :]


def _pad_w_in(w_in):
    u5, z5, zs, xbc, dt, gl = (w_in[:, 0:512], w_in[:, 512:1024], w_in[:, 1024:2560], w_in[:, 2560:5120],
                               w_in[:, 5120:5144], w_in[:, 5144:7192])
    dtp = jnp.pad(dt.reshape(-1, SSD_GROUPS, SSD_HPG), ((0, 0), (0, 0), (0, 128 - SSD_HPG))).reshape(-1, 512)
    return jnp.concatenate([xbc, u5, zs, z5, gl, dtp], axis=1)


def _unpad_w_in(wp):
    dt = wp[:, OFF_DT:].reshape(-1, SSD_GROUPS, 128)[:, :, :SSD_HPG].reshape(-1, SSD_HEADS)
    return jnp.concatenate([wp[:, OFF_U5:OFF_U5 + 512], wp[:, OFF_Z5:OFF_Z5 + 512], wp[:, OFF_ZS:OFF_ZS + 1536],
                            wp[:, OFF_XBC:OFF_XBC + 2560], dt, wp[:, OFF_G5:OFF_G5 + 2048]], axis=1)


def _local_step(x, p, tgt, w):
    n_seq, seq_len, _ = x.shape
    n_rows = n_seq * seq_len
    tr = 256
    x2 = x.reshape(n_rows, D_MODEL)
    p2 = p.reshape(n_rows, PLE_DIM)
    t2 = tgt.reshape(n_rows, D_MODEL)
    row = functools.partial(_rowwise, n_rows=n_rows, tr=tr)

    w_pad = _pad_w_in(w["w_in"])
    norm_w = w["norm_w"].reshape(1, D_MODEL)
    ple_norm_w = w["ple_norm_w"].reshape(1, D_MODEL)
    final_w = w["final_norm_w"].reshape(1, D_MODEL)
    s5_d = w["s5_d"].reshape(1, S5_WIDTH)
    b_glu = w["s5_b_glu"].reshape(1, S5_WIDTH)
    conv_w = w["ssd_conv_w"].reshape(SSD_CONV, SSD_CONV_DIM)
    conv_b = w["ssd_conv_b"].reshape(1, SSD_CONV_DIM)
    ssd_norm_w = w["ssd_norm_w"].reshape(1, SSD_WIDTH)
    dtb, alog, dsk = (_pad_heads(w[k].reshape(1, SSD_HEADS)) for k in ("ssd_dt_bias", "ssd_a_log", "ssd_d"))

    gh = S5_GROUPS * S5_GROUP
    a_re = w["s5_a_re"].reshape(S5_GROUPS, S5_STATE)
    a_im = w["s5_a_im"].reshape(S5_GROUPS, S5_STATE)
    log_step = w["s5_log_step"].reshape(S5_GROUPS, 1)
    b_re2 = jnp.transpose(w["s5_b_re"].reshape(S5_GROUPS, S5_STATE, S5_GROUP), (0, 2, 1)).reshape(gh, S5_STATE)
    b_im2 = jnp.transpose(w["s5_b_im"].reshape(S5_GROUPS, S5_STATE, S5_GROUP), (0, 2, 1)).reshape(gh, S5_STATE)
    expand = (jnp.arange(gh)[:, None] // S5_GROUP == jnp.arange(S5_GROUPS)[None, :]).astype(F32)
    pow_re, pow_im, bb_re2, bb_im2 = _s5_params_fwd(a_re, a_im, log_step, b_re2, b_im2, expand)
    lam_pow = _state_cols((pow_re.reshape(S5_LOG_TB, S5_N), pow_im.reshape(S5_LOG_TB, S5_N)))
    lam_pow_conj = _state_cols((pow_re.reshape(S5_LOG_TB, S5_N), -pow_im.reshape(S5_LOG_TB, S5_N)))
    bb_band = _band(bb_re2, bb_im2).astype(BF16)
    c_band = _band(w["s5_c_re"].reshape(gh, S5_STATE), -w["s5_c_im"].reshape(gh, S5_STATE)).astype(BF16)

    (hn,) = row("rms_in", lambda r, q: ([_rms(r[0], q[0])], []), row_ins=[(x2, 0, D_MODEL)],
                par_ins=[(norm_w, 0, D_MODEL)], row_outs=[(D_MODEL, BF16)])
    proj = _matmul("mm_proj", hn, w_pad)
    bu = _band_matmul("mm_s5_bu", "nn", proj, bb_band, a_blk0=OFF_U5 // BAND)
    s = _s5_scan_fwd(bu, lam_pow, n_seq, seq_len)
    yc = _band_matmul("mm_s5_y", "nt", s, c_band)
    (ge,) = row("s5_gelu", lambda r, q: ([jax.nn.gelu(r[0] + q[0] * r[1])], []),
                row_ins=[(yc, 0, S5_WIDTH), (proj, OFF_U5, S5_WIDTH)], par_ins=[(s5_d, 0, S5_WIDTH)],
                row_outs=[(S5_WIDTH, BF16)])
    tg = _matmul("mm_s5_glu", ge, w["s5_w_glu"])
    s5_rows = [(yc, 0, S5_WIDTH), (proj, OFF_U5, S5_WIDTH), (proj, OFF_Z5, S5_WIDTH), (tg, 0, S5_WIDTH)]
    s5_pars = [(s5_d, 0, S5_WIDTH), (b_glu, 0, S5_WIDTH)]
    (y5,) = row("s5_out", lambda r, q: ([_s5_out(*r, *q)], []), row_ins=s5_rows, par_ins=s5_pars,
                row_outs=[(S5_WIDTH, BF16)])

    xbc_act = _conv_fwd(proj, conv_w, conv_b, n_rows, seq_len)
    y_ssd, states = _ssd_fwd(xbc_act, proj, dtb, alog, dsk, n_seq, seq_len)
    gn_rows = [(y_ssd, 0, SSD_WIDTH), (proj, OFF_ZS, SSD_WIDTH)]
    (yss,) = row("ssd_gate", lambda r, q: ([_gated_norm(r[0], r[1], q[0])], []), row_ins=gn_rows,
                 par_ins=[(ssd_norm_w, 0, SSD_WIDTH)], row_outs=[(SSD_WIDTH, BF16)])

    m5 = _matmul("mm_br_s5", y5, w["w_br_s5"])
    ms = _matmul("mm_br_ssd", yss, w["w_br_ssd"])
    mg_rows = [(proj, OFF_G5, D_MODEL), (proj, OFF_GS, D_MODEL), (m5, 0, D_MODEL), (ms, 0, D_MODEL)]
    (merged,) = row("merge", lambda r, q: ([_merge(*r)], []), row_ins=mg_rows, par_ins=[], row_outs=[(D_MODEL, BF16)])
    o = _matmul("mm_out", merged, w["w_out"])

    def resid_fn(r, q):
        h1 = r[0] + r[1]
        return [h1, _rms(h1, q[0])], []

    h1, hp = row("resid_ple_norm", resid_fn, row_ins=[(x2, 0, D_MODEL), (o, 0, D_MODEL)],
                 par_ins=[(ple_norm_w, 0, D_MODEL)], row_outs=[(D_MODEL, F32), (D_MODEL, BF16)])
    pgl = _matmul("mm_ple_gate", hp, w["w_ple_gate"])
    pp = _matmul("mm_ple_proj", p2, w["w_ple_proj"])

    def head_fn(r, q):
        h1_, pgl_, pp_, tgt_ = r
        loss, vjp = jax.vjp(lambda a, b, c, f: _head_loss(a, b, c, f, tgt_), h1_, pgl_, pp_, q[0])
        dh1_, dpgl_, dpp_, dfw_ = vjp(jnp.ones_like(loss))
        return [dh1_, dpgl_, dpp_], [loss, dfw_]

    dh2, dpgl, dpp, loss_acc, d_final_w = row(
        "head", head_fn, row_ins=[(h1, 0, D_MODEL), (pgl, 0, D_MODEL), (pp, 0, D_MODEL), (t2, 0, D_MODEL)],
        par_ins=[(final_w, 0, D_MODEL)], row_outs=[(D_MODEL, F32), (D_MODEL, BF16), (D_MODEL, BF16)],
        acc_outs=[(1, 128), (1, D_MODEL)])
    loss = loss_acc[0, 0]

    g = {}
    g["final_norm_w"] = d_final_w
    g["w_ple_gate"] = _matmul("mm_d_w_ple_gate", hp, dpgl, ta=True)
    g["w_ple_proj"] = _matmul("mm_d_w_ple_proj", p2, dpp, ta=True)
    dhp = _matmul("mm_d_hp", dpgl, w["w_ple_gate"], tb=True)

    def ple_norm_bwd(r, q):
        h1_, dhp_, dh2_ = r
        _, vjp = jax.vjp(_rms, h1_, q[0])
        dh, dw = vjp(dhp_)
        dh = dh + dh2_
        return [dh, dh], [dw]

    dh1, dh1_b, g["ple_norm_w"] = row(
        "ple_norm_bwd", ple_norm_bwd, row_ins=[(h1, 0, D_MODEL), (dhp, 0, D_MODEL), (dh2, 0, D_MODEL)],
        par_ins=[(ple_norm_w, 0, D_MODEL)], row_outs=[(D_MODEL, F32), (D_MODEL, BF16)], acc_outs=[(1, D_MODEL)])
    g["w_out"] = _matmul("mm_d_w_out", merged, dh1_b, ta=True)
    dmerged = _matmul("mm_d_merged", dh1_b, w["w_out"], tb=True)

    def merge_bwd(r, q):
        _, vjp = jax.vjp(_merge, *r[:4])
        return list(vjp(r[4])), []

    dg5, dgs, dm5, dms = row("merge_bwd", merge_bwd, row_ins=mg_rows + [(dmerged, 0, D_MODEL)], par_ins=[],
                             row_outs=[(D_MODEL, BF16)] * 4)
    g["w_br_s5"] = _matmul("mm_d_w_br_s5", y5, dm5, ta=True)
    g["w_br_ssd"] = _matmul("mm_d_w_br_ssd", yss, dms, ta=True)
    dy5 = _matmul("mm_d_y5", dm5, w["w_br_s5"], tb=True)
    dyss = _matmul("mm_d_yss", dms, w["w_br_ssd"], tb=True)

    def s5_out_bwd_a(r, q):
        yc_, u_, z_, t_, dy_ = r
        d_, bg_ = q
        ge_ = jax.nn.gelu(yc_ + d_ * u_)
        _, vjp = jax.vjp(lambda a, z, t, b: a * jax.nn.sigmoid(t + b) * jax.nn.silu(z), ge_, z_, t_, bg_)
        dge, dz, dt_, dbg = vjp(dy_)
        return [dge, dz, dt_], [dbg]

    dge_a, dz5, dtg, g["s5_b_glu"] = row(
        "s5_out_bwd_a", s5_out_bwd_a, row_ins=s5_rows + [(dy5, 0, S5_WIDTH)], par_ins=s5_pars,
        row_outs=[(S5_WIDTH, F32), (S5_WIDTH, BF16), (S5_WIDTH, BF16)], acc_outs=[(1, S5_WIDTH)])
    g["s5_w_glu"] = _matmul("mm_d_w_glu", ge, dtg, ta=True)
    dge_b = _matmul("mm_d_ge", dtg, w["s5_w_glu"], tb=True)

    def s5_out_bwd_b(r, q):
        yc_, u_, da_, db_ = r
        _, vjp = jax.vjp(lambda yc, u, d: jax.nn.gelu(yc + d * u), yc_, u_, q[0])
        dyc_, du_, dd_ = vjp(da_ + db_)
        return [dyc_, du_], [dd_]

    dyc, du5_a, g["s5_d"] = row(
        "s5_out_bwd_b", s5_out_bwd_b,
        row_ins=[(yc, 0, S5_WIDTH), (proj, OFF_U5, S5_WIDTH), (dge_a, 0, S5_WIDTH), (dge_b, 0, S5_WIDTH)],
        par_ins=[(s5_d, 0, S5_WIDTH)], row_outs=[(S5_WIDTH, BF16), (S5_WIDTH, F32)], acc_outs=[(1, S5_WIDTH)])
    d_c_band = _band_matmul("mm_d_c", "tn", dyc, s)
    ds = _band_matmul("mm_d_s", "nn", dyc, c_band)
    dbu, d_lam = _s5_scan_bwd(ds, s, lam_pow_conj, n_seq, seq_len)
    d_bb_band = _band_matmul("mm_d_bb", "tn", proj, dbu, a_blk0=OFF_U5 // BAND)
    du5_b = _band_matmul("mm_d_u5", "nt", dbu, bb_band)
    (du5,) = row("s5_du", lambda r, q: ([r[0] + r[1]], []), row_ins=[(du5_a, 0, S5_WIDTH), (du5_b, 0, S5_WIDTH)],
                 par_ins=[], row_outs=[(S5_WIDTH, BF16)])

    d_lr, d_li = _state_uncols(d_lam)
    d_bbr, d_bbi = _band_take(d_bb_band)
    d_are, d_aim, d_ls, d_br2, d_bi2 = _s5_params_bwd(
        a_re, a_im, log_step, b_re2, b_im2, expand,
        d_lr.reshape(S5_GROUPS, S5_STATE), d_li.reshape(S5_GROUPS, S5_STATE), d_bbr, d_bbi)
    g["s5_a_re"], g["s5_a_im"], g["s5_log_step"] = d_are, d_aim, d_ls
    g["s5_b_re"] = jnp.transpose(d_br2.reshape(S5_GROUPS, S5_GROUP, S5_STATE), (0, 2, 1))
    g["s5_b_im"] = jnp.transpose(d_bi2.reshape(S5_GROUPS, S5_GROUP, S5_STATE), (0, 2, 1))
    d_cr, d_ci = _band_take(d_c_band)
    g["s5_c_re"], g["s5_c_im"] = d_cr, -d_ci

    def gate_bwd(r, q):
        _, vjp = jax.vjp(_gated_norm, r[0], r[1], q[0])
        dy_, dz_, dw_ = vjp(r[2])
        return [dy_, dz_], [dw_]

    dy_ssd, dzs, g["ssd_norm_w"] = row(
        "ssd_gate_bwd", gate_bwd, row_ins=gn_rows + [(dyss, 0, SSD_WIDTH)], par_ins=[(ssd_norm_w, 0, SSD_WIDTH)],
        row_outs=[(SSD_WIDTH, F32), (SSD_WIDTH, BF16)], acc_outs=[(1, SSD_WIDTH)])
    dxs, dbm, dcm, ddt, d_dtb, d_alog, d_dsk = _ssd_bwd(xbc_act, proj, states, dy_ssd, dtb, alog, dsk, n_seq, seq_len)
    g["ssd_dt_bias"], g["ssd_a_log"], g["ssd_d"] = _unpad_heads(d_dtb), _unpad_heads(d_alog), _unpad_heads(d_dsk)
    conv_parts = [_conv_bwd("ssd_conv_bwd_x", proj, dxs, conv_w, conv_b, n_rows, seq_len, 0),
                  _conv_bwd("ssd_conv_bwd_b", proj, dbm, conv_w, conv_b, n_rows, seq_len, SSD_WIDTH),
                  _conv_bwd("ssd_conv_bwd_c", proj, dcm, conv_w, conv_b, n_rows, seq_len, SSD_WIDTH + SSD_BC)]
    g["ssd_conv_w"] = jnp.concatenate([c[1] for c in conv_parts], axis=1)
    g["ssd_conv_b"] = jnp.concatenate([c[2] for c in conv_parts], axis=1)

    dproj = jnp.concatenate([c[0] for c in conv_parts] + [du5, dzs, dz5, dg5, dgs, ddt], axis=1)
    g["w_in"] = _unpad_w_in(_matmul("mm_d_w_in", hn, dproj, ta=True))
    dhn = _matmul("mm_d_hn", dproj, w_pad, tb=True)

    def norm_bwd(r, q):
        x_, dhn_, dh1_ = r
        _, vjp = jax.vjp(_rms, x_, q[0])
        dx_, dw_ = vjp(dhn_)
        return [dx_ + dh1_], [dw_]

    dx, g["norm_w"] = row("rms_in_bwd", norm_bwd, row_ins=[(x2, 0, D_MODEL), (dhn, 0, D_MODEL), (dh1, 0, D_MODEL)],
                          par_ins=[(norm_w, 0, D_MODEL)], row_outs=[(D_MODEL, F32)], acc_outs=[(1, D_MODEL)])
    return loss, dx.reshape(x.shape), g


HBM = pl.BlockSpec(memory_space=pltpu.HBM)


def _chip_index(x, y):
    return 2 * x + y


def _gather_chips(split, whole):
    ns, nw = len(split), len(whole)
    n = ns + nw

    def body(*refs):
        ins, outs = refs[:n], refs[n:2 * n]
        ici_send, ici_recv, d2d_send, d2d_recv, local_sems = refs[2 * n:]
        x, y, c = lax.axis_index("x"), lax.axis_index("y"), lax.axis_index("c")
        me = _chip_index(x, y)
        sibling = (x, y, 1 - c)
        peers = [(1 - x, y), (x, 1 - y), (1 - x, 1 - y)]

        def half(t, which):
            h = split[t].shape[0] // 2
            return pl.ds(which * h, h)

        copies = []
        for t in range(n):
            loc = pltpu.make_async_copy(ins[t], outs[t].at[me], local_sems.at[t])
            loc.start()
            copies.append(loc)

        def ici(t, k, slot):
            px, py = peers[k]
            if t < ns:
                src, dst = ins[t].at[half(t, c), :], outs[t].at[slot, half(t, c), :]
            else:
                src, dst = ins[t], outs[t].at[slot]
            return pltpu.make_async_remote_copy(src_ref=src, dst_ref=dst, send_sem=ici_send.at[t, k],
                                                recv_sem=ici_recv.at[t, k], device_id=(px, py, c), device_id_type=MESH)

        def d2d(t, k, which):
            rows = outs[t].at[_chip_index(*peers[k]), half(t, which), :]
            return pltpu.make_async_remote_copy(src_ref=rows, dst_ref=rows, send_sem=d2d_send.at[t, k],
                                                recv_sem=d2d_recv.at[t, k], device_id=sibling, device_id_type=MESH)

        sends = []
        for t in range(n):
            for k in range(3):
                cp = ici(t, k, me)
                cp.start()
                sends.append(cp)
        for t in range(n):
            for k in range(3):
                ici(t, k, _chip_index(*peers[k])).wait_recv()
                if t < ns:
                    cp = d2d(t, k, c)
                    cp.start()
                    sends.append(cp)
        for t in range(ns):
            for k in range(3):
                d2d(t, k, 1 - c).wait_recv()
        for cp in sends:
            cp.wait_send()
        for cp in copies:
            cp.wait()

    arrays = list(split) + list(whole)
    return pl.pallas_call(
        body, name="gather_weights",
        in_specs=[HBM] * n, out_specs=[HBM] * n,
        out_shape=[jax.ShapeDtypeStruct((N_CHIPS,) + a.shape, a.dtype) for a in arrays],
        scratch_shapes=[pltpu.SemaphoreType.DMA((n, 3)), pltpu.SemaphoreType.DMA((n, 3)),
                        pltpu.SemaphoreType.DMA((ns, 3)), pltpu.SemaphoreType.DMA((ns, 3)),
                        pltpu.SemaphoreType.DMA((n,))],
    )(*arrays)


def _swap_halves(slotted, small):
    n = len(slotted)

    def body(*refs):
        ins, sm_in = refs[:n], refs[n]
        outs, sm_out = refs[n + 1:2 * n + 1], refs[2 * n + 1]
        send_sems, recv_sems, local_sem, sm_send, sm_recv = refs[2 * n + 2:]
        x, y, c = lax.axis_index("x"), lax.axis_index("y"), lax.axis_index("c")
        dev = 4 * x + 2 * y + c
        local = pltpu.make_async_copy(sm_in, sm_out.at[dev], local_sem)
        local.start()
        sends = []
        for t in range(n):
            h = slotted[t].shape[1] // 2
            cp = pltpu.make_async_remote_copy(
                src_ref=ins[t].at[:, pl.ds((1 - c) * h, h), :], dst_ref=outs[t], send_sem=send_sems.at[t],
                recv_sem=recv_sems.at[t], device_id=(x, y, 1 - c), device_id_type=MESH)
            cp.start()
            sends.append(cp)
        rel = [(fx, fy, fc) for fx in (0, 1) for fy in (0, 1) for fc in (0, 1)][1:]
        for k, (fx, fy, fc) in enumerate(rel):
            cp = pltpu.make_async_remote_copy(
                src_ref=sm_in, dst_ref=sm_out.at[dev], send_sem=sm_send.at[k], recv_sem=sm_recv.at[k],
                device_id=(x ^ fx, y ^ fy, c ^ fc), device_id_type=MESH)
            cp.start()
            sends.append(cp)
        for cp in sends[:n]:
            cp.wait_recv()
        for k, (fx, fy, fc) in enumerate(rel):
            src_dev = 4 * (x ^ fx) + 2 * (y ^ fy) + (c ^ fc)
            pltpu.make_async_remote_copy(
                src_ref=sm_in, dst_ref=sm_out.at[src_dev], send_sem=sm_send.at[k], recv_sem=sm_recv.at[k],
                device_id=(x ^ fx, y ^ fy, c ^ fc), device_id_type=MESH).wait_recv()
        for cp in sends:
            cp.wait_send()
        local.wait()

    return pl.pallas_call(
        body, name="swap_halves",
        in_specs=[HBM] * (n + 1), out_specs=[HBM] * (n + 1),
        out_shape=[jax.ShapeDtypeStruct((a.shape[0], a.shape[1] // 2, a.shape[2]), a.dtype) for a in slotted]
        + [jax.ShapeDtypeStruct((N_DEV,) + small.shape, small.dtype)],
        scratch_shapes=[pltpu.SemaphoreType.DMA((n,)), pltpu.SemaphoreType.DMA((n,)), pltpu.SemaphoreType.DMA,
                        pltpu.SemaphoreType.DMA((7,)), pltpu.SemaphoreType.DMA((7,))],
    )(*slotted, small)


def _scatter_halves(parts):
    n = len(parts)

    def body(*refs):
        ins, outs = refs[:n], refs[n:2 * n]
        send_sems, recv_sems = refs[2 * n:]
        x, y, c = lax.axis_index("x"), lax.axis_index("y"), lax.axis_index("c")
        peers = [(1 - x, y), (x, 1 - y), (1 - x, 1 - y)]
        sends = []
        for t in range(n):
            for k, (px, py) in enumerate(peers):
                cp = pltpu.make_async_remote_copy(
                    src_ref=ins[t].at[_chip_index(px, py)], dst_ref=outs[t].at[k], send_sem=send_sems.at[t, k],
                    recv_sem=recv_sems.at[t, k], device_id=(px, py, c), device_id_type=MESH)
                cp.start()
                sends.append(cp)
        for cp in sends:
            cp.wait_recv()
        for cp in sends:
            cp.wait_send()

    return pl.pallas_call(
        body, name="scatter_halves",
        in_specs=[HBM] * n, out_specs=[HBM] * n,
        out_shape=[jax.ShapeDtypeStruct((3,) + a.shape[1:], a.dtype) for a in parts],
        scratch_shapes=[pltpu.SemaphoreType.DMA((n, 3)), pltpu.SemaphoreType.DMA((n, 3))],
    )(*parts)


def _pair_sum(name, slotted, other, idx):
    _, r, c = slotted.shape
    h = r // 2
    th = _row_tile(h, 13 * c * 4)
    a4 = slotted.reshape(N_CHIPS, 2, h, c)

    def body(idx_ref, a_ref, b_ref, am_ref, bm_ref, p_ref, own_ref):
        p_ref[...] = (a_ref[:, 0] + b_ref[...]).astype(p_ref.dtype)
        own_ref[...] = am_ref[0, 0] + bm_ref[0]

    grid_spec = pltpu.PrefetchScalarGridSpec(
        num_scalar_prefetch=1, grid=(h // th,),
        in_specs=[pl.BlockSpec((N_CHIPS, 1, th, c), lambda i, s: (0, s[0], i, 0)),
                  pl.BlockSpec((N_CHIPS, th, c), lambda i, s: (0, i, 0)),
                  pl.BlockSpec((1, 1, th, c), lambda i, s: (s[1], s[0], i, 0)),
                  pl.BlockSpec((1, th, c), lambda i, s: (s[1], i, 0))],
        out_specs=[pl.BlockSpec((N_CHIPS, th, c), lambda i, s: (0, i, 0)), pl.BlockSpec((th, c), lambda i, s: (i, 0))])
    return pl.pallas_call(
        body, name=name, grid_spec=grid_spec,
        out_shape=[jax.ShapeDtypeStruct((N_CHIPS, h, c), BF16), jax.ShapeDtypeStruct((h, c), F32)],
        compiler_params=_cparams(("parallel",)),
    )(idx, a4, other, a4, other)


def _sum_parts(name, own, recv):
    h, c = own.shape
    th = _row_tile(h, 4 * c * 4)

    def body(o_ref, r_ref, out_ref):
        acc = o_ref[...]
        for k in range(3):
            acc = acc + r_ref[k].astype(F32)
        out_ref[...] = acc

    return pl.pallas_call(
        body, name=name, grid=(h // th,),
        in_specs=[pl.BlockSpec((th, c), lambda i: (i, 0)), pl.BlockSpec((3, th, c), lambda i: (0, i, 0))],
        out_specs=pl.BlockSpec((th, c), lambda i: (i, 0)),
        out_shape=jax.ShapeDtypeStruct((h, c), F32),
        compiler_params=_cparams(("parallel",)),
    )(own, recv)


def _swap_sibling(parts):
    n = len(parts)

    def body(*refs):
        ins, outs = refs[:n], refs[n:2 * n]
        send_sems, recv_sems = refs[2 * n:]
        x, y, c = lax.axis_index("x"), lax.axis_index("y"), lax.axis_index("c")
        cps = []
        for t in range(n):
            cp = pltpu.make_async_remote_copy(
                src_ref=ins[t], dst_ref=outs[t], send_sem=send_sems.at[t], recv_sem=recv_sems.at[t],
                device_id=(x, y, 1 - c), device_id_type=MESH)
            cp.start()
            cps.append(cp)
        for cp in cps:
            cp.wait_recv()
        for cp in cps:
            cp.wait_send()

    return pl.pallas_call(
        body, name="swap_sibling",
        in_specs=[HBM] * n, out_specs=[HBM] * n,
        out_shape=[jax.ShapeDtypeStruct(a.shape, a.dtype) for a in parts],
        scratch_shapes=[pltpu.SemaphoreType.DMA((n,)), pltpu.SemaphoreType.DMA((n,))],
    )(*parts)


def _sum_slots(name, a):
    k, r, c = a.shape
    tr = _row_tile(r, (k + 1) * c * 4)

    def body(a_ref, o_ref):
        acc = a_ref[0]
        for i in range(1, k):
            acc = acc + a_ref[i]
        o_ref[...] = acc

    return pl.pallas_call(
        body, name=name, grid=(r // tr,),
        in_specs=[pl.BlockSpec((k, tr, c), lambda i: (0, i, 0))],
        out_specs=pl.BlockSpec((tr, c), lambda i: (i, 0)),
        out_shape=jax.ShapeDtypeStruct((r, c), a.dtype),
        compiler_params=_cparams(("parallel",)),
    )(a)


def _adamw(name, w, m, v, g_parts):
    r, c = w.shape
    ng = len(g_parts)
    tr = _row_tile(r, (7 + ng) * c * 4)
    c1 = 1.0 - ADAM_B1 ** ADAM_STEP
    c2 = 1.0 - ADAM_B2 ** ADAM_STEP

    def body(*refs):
        w_ref, m_ref, v_ref = refs[:3]
        g_refs = refs[3:3 + ng]
        go_ref, d_ref, mo_ref, vo_ref = refs[3 + ng:]
        g = g_refs[0][...]
        for gr in g_refs[1:]:
            g = g + gr[...]
        m_new = ADAM_B1 * m_ref[...] + (1.0 - ADAM_B1) * g
        v_new = ADAM_B2 * v_ref[...] + (1.0 - ADAM_B2) * (g * g)
        go_ref[...] = g
        mo_ref[...] = m_new
        vo_ref[...] = v_new
        d_ref[...] = -ADAM_LR * ((m_new / c1) / (jnp.sqrt(v_new / c2) + ADAM_EPS) + ADAM_WD * w_ref[...])

    spec = pl.BlockSpec((tr, c), lambda i: (i, 0))
    return pl.pallas_call(
        body, name=name, grid=(r // tr,),
        in_specs=[spec] * (3 + ng), out_specs=[spec] * 4,
        out_shape=[jax.ShapeDtypeStruct((r, c), F32)] * 4,
        compiler_params=_cparams(("parallel",)),
    )(w, m, v, *g_parts)


def _adamw_halves(name, w, m, v, own, other, idx):
    r, c = w.shape
    h = r // 2
    th = _row_tile(h, 9 * c * 4)
    nb = h // th
    c1 = 1.0 - ADAM_B1 ** ADAM_STEP
    c2 = 1.0 - ADAM_B2 ** ADAM_STEP

    def body(idx_ref, w_ref, m_ref, v_ref, own_ref, oth_ref, go_ref, d_ref, mo_ref, vo_ref):
        g = jnp.where(pl.program_id(0) == idx_ref[0], own_ref[...], oth_ref[...])
        m_new = ADAM_B1 * m_ref[...] + (1.0 - ADAM_B1) * g
        v_new = ADAM_B2 * v_ref[...] + (1.0 - ADAM_B2) * (g * g)
        go_ref[...] = g
        mo_ref[...] = m_new
        vo_ref[...] = v_new
        d_ref[...] = -ADAM_LR * ((m_new / c1) / (jnp.sqrt(v_new / c2) + ADAM_EPS) + ADAM_WD * w_ref[...])

    full = pl.BlockSpec((th, c), lambda hh, i, s: (hh * nb + i, 0))
    part = pl.BlockSpec((th, c), lambda hh, i, s: (i, 0))
    grid_spec = pltpu.PrefetchScalarGridSpec(
        num_scalar_prefetch=1, grid=(2, nb), in_specs=[full, full, full, part, part], out_specs=[full] * 4)
    return pl.pallas_call(
        body, name=name, grid_spec=grid_spec, out_shape=[jax.ShapeDtypeStruct((r, c), F32)] * 4,
        compiler_params=_cparams(("parallel", "parallel")),
    )(idx, w, m, v, own, other)


WEIGHTS = ['norm_w', 'w_in', 's5_a_re', 's5_a_im', 's5_b_re', 's5_b_im', 's5_c_re', 's5_c_im', 's5_d', 's5_log_step',
           's5_w_glu', 's5_b_glu', 'ssd_conv_w', 'ssd_conv_b', 'ssd_dt_bias', 'ssd_a_log', 'ssd_d', 'ssd_norm_w',
           'w_br_s5', 'w_br_ssd', 'w_out', 'ple_norm_w', 'w_ple_gate', 'w_ple_proj', 'final_norm_w']
SHARDED = {'w_in': ((1024, IN_PROJ_DIM), 1), 's5_w_glu': ((512, 512), 0), 'ssd_conv_w': ((SSD_CONV, SSD_CONV_DIM), 1),
           'w_br_s5': ((512, 1024), 1), 'w_br_ssd': ((1536, 1024), 0), 'w_out': ((1024, 1024), 0),
           'w_ple_gate': ((1024, 1024), 0), 'w_ple_proj': ((256, 1024), 1)}
SMALL = [n for n in WEIGHTS if n not in SHARDED]


def _shard2d(name, a):
    (r, c), ax = SHARDED[name]
    return a.reshape((r // N_CHIPS, c) if ax == 0 else (r, c // N_CHIPS))


def _unslot(name, a4):
    (r, c), ax = SHARDED[name]
    if ax == 0:
        return a4.reshape(r, c)
    return jnp.transpose(a4, (1, 0, 2)).reshape(r, c)


def _slot(name, full):
    (r, c), ax = SHARDED[name]
    if ax == 0:
        return full.reshape(N_CHIPS, r // N_CHIPS, c)
    return jnp.transpose(full.reshape(r, N_CHIPS, c // N_CHIPS), (1, 0, 2))


def _pack_small(vals):
    flat = jnp.concatenate([v.reshape(-1).astype(F32) for v in vals])
    rows = -(-flat.shape[0] // (256 * 128)) * 256
    return jnp.pad(flat, (0, rows * 128 - flat.shape[0])).reshape(rows, 128)


def _unpack_small(packed, shapes):
    flat = packed.reshape(-1)
    out, off = [], 0
    for sh in shapes:
        n = math.prod(sh)
        out.append(flat[off:off + n].reshape(sh))
        off += n
    return out


def kernel(x, p, norm_w, w_in, s5_a_re, s5_a_im, s5_b_re, s5_b_im, s5_c_re, s5_c_im, s5_d, s5_log_step, s5_w_glu, s5_b_glu, ssd_conv_w, ssd_conv_b, ssd_dt_bias, ssd_a_log, ssd_d, ssd_norm_w, w_br_s5, w_br_ssd, w_out, ple_norm_w, w_ple_gate, w_ple_proj, final_norm_w, loss_target, m_norm_w, m_w_in, m_s5_a_re, m_s5_a_im, m_s5_b_re, m_s5_b_im, m_s5_c_re, m_s5_c_im, m_s5_d, m_s5_log_step, m_s5_w_glu, m_s5_b_glu, m_ssd_conv_w, m_ssd_conv_b, m_ssd_dt_bias, m_ssd_a_log, m_ssd_d, m_ssd_norm_w, m_w_br_s5, m_w_br_ssd, m_w_out, m_ple_norm_w, m_w_ple_gate, m_w_ple_proj, m_final_norm_w, v_norm_w, v_w_in, v_s5_a_re, v_s5_a_im, v_s5_b_re, v_s5_b_im, v_s5_c_re, v_s5_c_im, v_s5_d, v_s5_log_step, v_s5_w_glu, v_s5_b_glu, v_ssd_conv_w, v_ssd_conv_b, v_ssd_dt_bias, v_ssd_a_log, v_ssd_d, v_ssd_norm_w, v_w_br_s5, v_w_br_ssd, v_w_out, v_ple_norm_w, v_w_ple_gate, v_w_ple_proj, v_final_norm_w):
    args = locals()
    wl = {n: args[n] for n in WEIGHTS}
    ml = {n: args["m_" + n] for n in WEIGHTS}
    vl = {n: args["v_" + n] for n in WEIGHTS}
    big = [n for n in SHARDED if n != 'ssd_conv_w']
    chip = _chip_index(lax.axis_index("x"), lax.axis_index("y"))
    idx = jnp.stack([lax.axis_index("c"), chip]).astype(jnp.int32)

    gathered = _gather_chips([_shard2d(n, wl[n]).astype(BF16) for n in big], [_shard2d('ssd_conv_w', wl['ssd_conv_w'])])
    full = {n: wl[n] for n in SMALL}
    for n, a4 in zip(big + ['ssd_conv_w'], gathered, strict=True):
        full[n] = _unslot(n, a4)

    loss, grad_x, g = _local_step(x, p[0], loss_target, full)

    small_names = SMALL + ['ssd_conv_w']
    small_shapes = [(1, 1)] + [g[n].shape for n in small_names]
    small_pack = _pack_small([loss] + [g[n] for n in small_names])
    slotted = [_slot(n, g[n]) for n in big]
    swapped = _swap_halves(slotted, small_pack)
    pair = [_pair_sum("pair_sum_" + n, a, b, idx) for n, a, b in zip(big, slotted, swapped[:-1], strict=True)]
    small_sum = _sum_slots("sum_small", swapped[-1])
    recv = _scatter_halves([pb for pb, _ in pair])
    halves = [_sum_parts("sum_chips_" + n, own, r) for n, (_, own), r in zip(big, pair, recv, strict=True)]
    other_halves = _swap_sibling(halves)

    out_g, out_d, out_m, out_v = {}, {}, {}, {}
    for n, own, oth in zip(big, halves, other_halves, strict=True):
        res = _adamw_halves("adamw_" + n, _shard2d(n, wl[n]), _shard2d(n, ml[n]), _shard2d(n, vl[n]), own, oth, idx)
        out_g[n], out_d[n], out_m[n], out_v[n] = (r.reshape(wl[n].shape) for r in res)
    sm = _unpack_small(small_sum, small_shapes)
    loss_total = sm[0].reshape(())
    n = 'ssd_conv_w'
    conv_g = lax.dynamic_slice(sm[-1], (0, chip * (SSD_CONV_DIM // N_CHIPS)), (SSD_CONV, SSD_CONV_DIM // N_CHIPS))
    res = _adamw("adamw_" + n, _shard2d(n, wl[n]), _shard2d(n, ml[n]), _shard2d(n, vl[n]), [conv_g])
    out_g[n], out_d[n], out_m[n], out_v[n] = (r.reshape(wl[n].shape) for r in res)
    small_g = _pack_small(sm[1:-1])
    res = _adamw("adamw_small", _pack_small([wl[n] for n in SMALL]), _pack_small([ml[n] for n in SMALL]),
                 _pack_small([vl[n] for n in SMALL]), [small_g])
    shapes = [wl[n].shape for n in SMALL]
    for dst, packed in zip((out_g, out_d, out_m, out_v), res, strict=True):
        for n, val in zip(SMALL, _unpack_small(packed, shapes), strict=True):
            dst[n] = val

    return (loss_total, grad_x, *[out_g[n] for n in WEIGHTS], *[out_d[n] for n in WEIGHTS],
            *[out_m[n] for n in WEIGHTS], *[out_v[n] for n in WEIGHTS])
```

```python
import functools
import math

import jax
import jax.numpy as jnp
from jax import lax
from jax.experimental import pallas as pl
from jax.experimental.pallas import tpu as pltpu

F32 = jnp.float32
BF16 = jnp.bfloat16
MESH = pl.DeviceIdType.MESH

D_MODEL = 1024
PLE_DIM = 256
RMS_EPS = 1e-6
S5_WIDTH = 512
S5_GROUP = 16
S5_GROUPS = 32
S5_STATE = 64
S5_N = S5_GROUPS * S5_STATE
S5_LB = 512
S5_NJ = S5_N // S5_LB
S5_TB = 256
S5_LOG_TB = 8
SSD_WIDTH = 1536
SSD_HEADDIM = 64
SSD_HEADS = 24
SSD_GROUPS = 4
SSD_HPG = 6
SSD_STATE = 128
SSD_CONV = 4
SSD_CHUNK = 128
SSD_BC = 512
SSD_CONV_DIM = 2560
GROUP_W = SSD_WIDTH // SSD_GROUPS
N_CHIPS = 4
N_DEV = 8

OFF_XBC, OFF_U5, OFF_ZS, OFF_Z5, OFF_G5, OFF_GS, OFF_DT = 0, 2560, 3072, 4608, 5120, 6144, 7168
PROJ_W = 7680
IN_PROJ_DIM = 7192

ADAM_LR, ADAM_B1, ADAM_B2, ADAM_EPS, ADAM_WD, ADAM_STEP = 0.001, 0.9, 0.999, 1e-08, 0.01, 10

VMEM_LIMIT = 56 * 1024 * 1024


def _pick(n, cands):
    for c in cands:
        if n % c == 0:
            return c
    return n


ROW_BLOCK_BYTES = 8 * 1024 * 1024


def _row_tile(r, bytes_per_row):
    for t in (r, 4096, 2048, 1024, 512, 256, 128, 64, 32, 16, 8):
        if t <= r and r % t == 0 and t * bytes_per_row <= ROW_BLOCK_BYTES:
            return t
    return r


def _cparams(sem):
    return pltpu.CompilerParams(dimension_semantics=sem, vmem_limit_bytes=VMEM_LIMIT)


def _dg(a, b, ca, cb):
    return lax.dot_general(a.astype(BF16), b.astype(BF16), (((ca,), (cb,)), ((), ())), preferred_element_type=F32)


@jax.custom_vjp
def dot_nn(a, b):
    return _dg(a, b, 1, 0)


@jax.custom_vjp
def dot_nt(a, b):
    return _dg(a, b, 1, 1)


@jax.custom_vjp
def dot_tn(a, b):
    return _dg(a, b, 0, 0)


dot_nn.defvjp(lambda a, b: (_dg(a, b, 1, 0), (a, b)), lambda r, g: (_dg(g, r[1], 1, 1), _dg(r[0], g, 0, 0)))
dot_nt.defvjp(lambda a, b: (_dg(a, b, 1, 1), (a, b)), lambda r, g: (_dg(g, r[1], 1, 0), _dg(g, r[0], 0, 0)))
dot_tn.defvjp(lambda a, b: (_dg(a, b, 0, 0), (a, b)), lambda r, g: (_dg(r[1], g, 1, 1), _dg(r[0], g, 1, 0)))


MM_VMEM_BUDGET = 30 * 1024 * 1024


def _mm_tiles(m, n, k, sa, sb, so):
    best, best_key = None, None
    for tm in (1024, 512, 256, 128, 64, 32, 16, 8):
        if m % tm:
            continue
        for tn in (2048, 1536, 1280, 1024, 768, 640, 512, 384, 256, 128):
            if n % tn:
                continue
            for tk in (k, 2048, 1536, 1280, 1024, 768, 512, 256, 128):
                if k % tk or tk > max(k, 128) or (tk == k and k > 2048 and k % 128 == 0):
                    continue
                need = 2 * (tm * tk * sa + tk * tn * sb + tm * tn * so) + (tm * tn * 4 if tk < k else 0)
                if need > MM_VMEM_BUDGET:
                    continue
                key = (tm * tn * tk, tk)
                if best_key is None or key > best_key:
                    best, best_key = (tm, tn, tk), key
    assert best is not None, (m, n, k)
    return best


def _matmul(name, a, b, *, ta=False, tb=False, a_win=None, out_dtype=F32):
    a_off, a_w = a_win if a_win is not None else (0, a.shape[1])
    if ta:
        kdim, m = a.shape[0], a_w
    else:
        m, kdim = a.shape[0], a_w
    n = b.shape[0] if tb else b.shape[1]
    assert (b.shape[1] if tb else b.shape[0]) == kdim, (name, a.shape, b.shape)
    tm, tn, tk = _mm_tiles(m, n, kdim, a.dtype.itemsize, b.dtype.itemsize, jnp.dtype(out_dtype).itemsize)
    nk = kdim // tk
    if ta:
        assert a_off % tm == 0
        a_spec = pl.BlockSpec((tk, tm), lambda i, j, k: (k, i + a_off // tm))
    else:
        assert a_off % tk == 0
        a_spec = pl.BlockSpec((tm, tk), lambda i, j, k: (i, k + a_off // tk))
    if tb:
        b_spec = pl.BlockSpec((tn, tk), lambda i, j, k: (j, k))
    else:
        b_spec = pl.BlockSpec((tk, tn), lambda i, j, k: (k, j))
    ca, cb = (0 if ta else 1), (1 if tb else 0)

    def body(a_ref, b_ref, o_ref, *acc):
        if nk == 1:
            o_ref[...] = _dg(a_ref[...], b_ref[...], ca, cb).astype(o_ref.dtype)
            return
        (acc_ref,) = acc
        k = pl.program_id(2)

        @pl.when(k == 0)
        def _():
            acc_ref[...] = jnp.zeros_like(acc_ref)

        acc_ref[...] += _dg(a_ref[...], b_ref[...], ca, cb)

        @pl.when(k == nk - 1)
        def _():
            o_ref[...] = acc_ref[...].astype(o_ref.dtype)

    return pl.pallas_call(
        body, name=name, grid=(m // tm, n // tn, nk),
        in_specs=[a_spec, b_spec],
        out_specs=pl.BlockSpec((tm, tn), lambda i, j, k: (i, j)),
        out_shape=jax.ShapeDtypeStruct((m, n), out_dtype),
        scratch_shapes=[pltpu.VMEM((tm, tn), F32)] if nk > 1 else [],
        compiler_params=_cparams(("parallel", "parallel", "arbitrary")),
    )(a, b)


BAND = 128


def _band_matmul(name, kind, a, b, *, a_blk0=0, out_dtype=F32):
    n_rows = a.shape[0]
    blk = 2 * S5_LB
    tm = _pick(n_rows, (1024, 512, 256))
    if kind == "nn":
        grid = (n_rows // tm, S5_NJ)
        in_specs = [pl.BlockSpec((tm, BAND), lambda i, j: (i, a_blk0 + j)), pl.BlockSpec((BAND, blk), lambda i, j: (j, 0))]
        out_spec = pl.BlockSpec((tm, blk), lambda i, j: (i, j))
        out_shape = (n_rows, S5_NJ * blk)
        sem = ("parallel", "parallel")

        def body(a_ref, b_ref, o_ref):
            o_ref[...] = _dg(a_ref[...], b_ref[...], 1, 0).astype(o_ref.dtype)
    elif kind == "nt":
        grid = (n_rows // tm, S5_NJ)
        in_specs = [pl.BlockSpec((tm, blk), lambda i, j: (i, j)), pl.BlockSpec((BAND, blk), lambda i, j: (j, 0))]
        out_spec = pl.BlockSpec((tm, BAND), lambda i, j: (i, j))
        out_shape = (n_rows, S5_NJ * BAND)
        sem = ("parallel", "parallel")

        def body(a_ref, b_ref, o_ref):
            o_ref[...] = _dg(a_ref[...], b_ref[...], 1, 1).astype(o_ref.dtype)
    else:
        grid = (S5_NJ, n_rows // tm)
        in_specs = [pl.BlockSpec((tm, BAND), lambda j, k: (k, a_blk0 + j)), pl.BlockSpec((tm, blk), lambda j, k: (k, j))]
        out_spec = pl.BlockSpec((BAND, blk), lambda j, k: (j, 0))
        out_shape = (S5_NJ * BAND, blk)
        sem = ("parallel", "arbitrary")

        def body(a_ref, b_ref, o_ref):
            @pl.when(pl.program_id(1) == 0)
            def _():
                o_ref[...] = jnp.zeros_like(o_ref)

            o_ref[...] += _dg(a_ref[...], b_ref[...], 0, 0)

    return pl.pallas_call(
        body, name=name, grid=grid, in_specs=in_specs, out_specs=out_spec,
        out_shape=jax.ShapeDtypeStruct(out_shape, out_dtype), compiler_params=_cparams(sem),
    )(a, b)


def _rowwise(name, fn, n_rows, tr, row_ins, par_ins, row_outs, acc_outs=()):
    nr, npar, no, na = len(row_ins), len(par_ins), len(row_outs), len(acc_outs)
    in_specs = []
    for arr, off, w in row_ins:
        assert off % w == 0 and arr.shape[0] == n_rows, (name, arr.shape, off, w)
        in_specs.append(pl.BlockSpec((tr, w), functools.partial(lambda i, c: (i, c), c=off // w)))
    for arr, off, w in par_ins:
        assert off % w == 0
        in_specs.append(pl.BlockSpec((arr.shape[0], w), functools.partial(lambda i, c: (0, c), c=off // w)))
    out_specs = [pl.BlockSpec((tr, w), lambda i: (i, 0)) for w, _ in row_outs]
    out_specs += [pl.BlockSpec((r, w), lambda i: (0, 0)) for r, w in acc_outs]
    out_shape = [jax.ShapeDtypeStruct((n_rows, w), dt) for w, dt in row_outs]
    out_shape += [jax.ShapeDtypeStruct((r, w), F32) for r, w in acc_outs]

    def body(*refs):
        rows = [r[...] for r in refs[:nr]]
        pars = [r[...] for r in refs[nr:nr + npar]]
        o_refs = refs[nr + npar:nr + npar + no]
        a_refs = refs[nr + npar + no:]
        outs, accs = fn(rows, pars)
        for o_ref, o in zip(o_refs, outs, strict=True):
            o_ref[...] = o.astype(o_ref.dtype)
        if na:
            @pl.when(pl.program_id(0) == 0)
            def _():
                for a_ref in a_refs:
                    a_ref[...] = jnp.zeros_like(a_ref)

            for a_ref, a in zip(a_refs, accs, strict=True):
                a_ref[...] += jnp.broadcast_to(a, a_ref.shape)

    res = pl.pallas_call(
        body, name=name, grid=(n_rows // tr,),
        in_specs=in_specs, out_specs=out_specs, out_shape=out_shape,
        compiler_params=_cparams(("arbitrary",) if na else ("parallel",)),
    )(*[a for a, _, _ in row_ins], *[a for a, _, _ in par_ins])
    return res


def _rms(x, w):
    return x * lax.rsqrt(jnp.mean(x * x, axis=-1, keepdims=True) + RMS_EPS) * w


def _gated_norm(y, z, w):
    outs = []
    for g in range(SSD_GROUPS):
        sl = slice(g * GROUP_W, (g + 1) * GROUP_W)
        yg = y[:, sl] * jax.nn.silu(z[:, sl])
        outs.append(yg * lax.rsqrt(jnp.mean(yg * yg, axis=-1, keepdims=True) + RMS_EPS) * w[:, sl])
    return jnp.concatenate(outs, axis=-1)


def _s5_out(yc, u, z, t, d, bg):
    ge = jax.nn.gelu(yc + d * u)
    return ge * jax.nn.sigmoid(t + bg) * jax.nn.silu(z)


def _merge(g5, gs, m5, ms):
    return jax.nn.sigmoid(g5) * m5 + jax.nn.sigmoid(gs) * ms


def _head_loss(h1, pgl, pp, fw, tgt):
    h2 = h1 + jax.nn.sigmoid(pgl) * pp
    err = _rms(h2, fw) - tgt
    per_row = 0.5 * jnp.mean(err * err, axis=-1, keepdims=True)
    return jnp.sum(per_row, axis=0, keepdims=True)


def _s5_disc(a_re, a_im, log_step, b_re2, b_im2, expand):
    step = jnp.exp(log_step)
    mag = jnp.exp(a_re * step)
    lb_re = mag * jnp.cos(a_im * step)
    lb_im = mag * jnp.sin(a_im * step)
    den = a_re * a_re + a_im * a_im
    n_re = lb_re - 1.0
    f_re = (n_re * a_re + lb_im * a_im) / den
    f_im = (lb_im * a_re - n_re * a_im) / den
    hi = lax.Precision.HIGHEST
    fr = jnp.dot(expand, f_re, precision=hi, preferred_element_type=F32)
    fi = jnp.dot(expand, f_im, precision=hi, preferred_element_type=F32)
    return lb_re, lb_im, fr * b_re2 - fi * b_im2, fr * b_im2 + fi * b_re2


def _s5_params_fwd(a_re, a_im, log_step, b_re2, b_im2, expand):
    gp = a_re.shape

    def body(ar, ai, ls, br, bi, ex, pr_ref, pi_ref, bbr_ref, bbi_ref):
        lr, li, bbr, bbi = _s5_disc(ar[...], ai[...], ls[...], br[...], bi[...], ex[...])
        bbr_ref[...] = bbr
        bbi_ref[...] = bbi
        for k in range(S5_LOG_TB):
            pr_ref[k] = lr
            pi_ref[k] = li
            lr, li = lr * lr - li * li, 2.0 * lr * li

    return pl.pallas_call(
        body, name="s5_params_fwd",
        out_shape=(jax.ShapeDtypeStruct((S5_LOG_TB,) + gp, F32), jax.ShapeDtypeStruct((S5_LOG_TB,) + gp, F32),
                   jax.ShapeDtypeStruct(b_re2.shape, F32), jax.ShapeDtypeStruct(b_re2.shape, F32)),
    )(a_re, a_im, log_step, b_re2, b_im2, expand)


def _s5_params_bwd(a_re, a_im, log_step, b_re2, b_im2, expand, d_lr, d_li, d_bbr, d_bbi):
    def body(ar, ai, ls, br, bi, ex, glr, gli, gbr, gbi, dar, dai, dls, dbr, dbi):
        _, vjp = jax.vjp(lambda *p: _s5_disc(*p, ex[...]), ar[...], ai[...], ls[...], br[...], bi[...])
        g = vjp((glr[...], gli[...], gbr[...], gbi[...]))
        for ref, val in zip((dar, dai, dls, dbr, dbi), g, strict=True):
            ref[...] = val

    return pl.pallas_call(
        body, name="s5_params_bwd",
        out_shape=tuple(jax.ShapeDtypeStruct(v.shape, F32) for v in (a_re, a_im, log_step, b_re2, b_im2)),
    )(a_re, a_im, log_step, b_re2, b_im2, expand, d_lr, d_li, d_bbr, d_bbi)


def _scan_block(xr, xi, lp, cr, ci, reverse):
    tb = xr.shape[0]
    row = lax.broadcasted_iota(jnp.int32, xr.shape, 0)
    l_r, l_i = lp[0:1, :S5_LB], lp[0:1, S5_LB:]
    edge = (row == tb - 1) if reverse else (row == 0)
    xr = xr + jnp.where(edge, l_r * cr - l_i * ci, 0.0)
    xi = xi + jnp.where(edge, l_r * ci + l_i * cr, 0.0)
    for k in range(S5_LOG_TB):
        sh = 1 << k
        a_r, a_i = lp[k:k + 1, :S5_LB], lp[k:k + 1, S5_LB:]
        if reverse:
            keep = row < tb - sh
            pr = jnp.where(keep, pltpu.roll(xr, tb - sh, 0), 0.0)
            pi = jnp.where(keep, pltpu.roll(xi, tb - sh, 0), 0.0)
        else:
            keep = row >= sh
            pr = jnp.where(keep, pltpu.roll(xr, sh, 0), 0.0)
            pi = jnp.where(keep, pltpu.roll(xi, sh, 0), 0.0)
        xr, xi = xr + a_r * pr - a_i * pi, xi + a_r * pi + a_i * pr
    return xr, xi


def _s5_scan_fwd(bu, lam_pow, n_seq, seq_len):
    n_t = seq_len // S5_TB
    blk = 2 * S5_LB

    def body(bu_ref, lp_ref, s_ref, cr, ci):
        @pl.when(pl.program_id(2) == 0)
        def _():
            cr[...] = jnp.zeros_like(cr)
            ci[...] = jnp.zeros_like(ci)

        xr, xi = _scan_block(bu_ref[:, :S5_LB], bu_ref[:, S5_LB:], lp_ref[...], cr[...], ci[...], False)
        s_ref[:, :S5_LB] = xr
        s_ref[:, S5_LB:] = xi
        cr[...] = xr[S5_TB - 1:S5_TB, :]
        ci[...] = xi[S5_TB - 1:S5_TB, :]

    return pl.pallas_call(
        body, name="s5_scan_fwd", grid=(S5_NJ, n_seq, n_t),
        in_specs=[pl.BlockSpec((S5_TB, blk), lambda j, b, t: (b * n_t + t, j)),
                  pl.BlockSpec((S5_LOG_TB, blk), lambda j, b, t: (0, j))],
        out_specs=pl.BlockSpec((S5_TB, blk), lambda j, b, t: (b * n_t + t, j)),
        out_shape=jax.ShapeDtypeStruct(bu.shape, F32),
        scratch_shapes=[pltpu.VMEM((1, S5_LB), F32), pltpu.VMEM((1, S5_LB), F32)],
        compiler_params=_cparams(("parallel", "parallel", "arbitrary")),
    )(bu, lam_pow)


def _s5_scan_bwd(ds, s, lam_pow_conj, n_seq, seq_len):
    n_t = seq_len // S5_TB
    blk = 2 * S5_LB
    halo_per_blk = S5_TB // 8

    def rows(j, b, t):
        return (b * n_t + (n_t - 1 - t), j)

    def halo(j, b, t):
        return (jnp.maximum((b * n_t + (n_t - 1 - t)) * halo_per_blk - 1, 0), j)

    def body(ds_ref, s_ref, h_ref, lp_ref, g_ref, dl_ref, cr, ci):
        b, t = pl.program_id(1), pl.program_id(2)

        @pl.when(t == 0)
        def _():
            cr[...] = jnp.zeros_like(cr)
            ci[...] = jnp.zeros_like(ci)

        @pl.when((b == 0) & (t == 0))
        def _():
            dl_ref[...] = jnp.zeros_like(dl_ref)

        gr, gi = _scan_block(ds_ref[:, :S5_LB], ds_ref[:, S5_LB:], lp_ref[...], cr[...], ci[...], True)
        g_ref[:, :S5_LB] = gr.astype(g_ref.dtype)
        g_ref[:, S5_LB:] = gi.astype(g_ref.dtype)
        cr[...] = gr[0:1, :]
        ci[...] = gi[0:1, :]
        row = lax.broadcasted_iota(jnp.int32, gr.shape, 0)
        first_blk = t == n_t - 1
        h_r = jnp.where(first_blk, 0.0, h_ref[7:8, :S5_LB])
        h_i = jnp.where(first_blk, 0.0, h_ref[7:8, S5_LB:])
        sp_r = jnp.where(row == 0, h_r, pltpu.roll(s_ref[:, :S5_LB], 1, 0))
        sp_i = jnp.where(row == 0, h_i, pltpu.roll(s_ref[:, S5_LB:], 1, 0))
        dl_ref[:, :S5_LB] += jnp.sum(gr * sp_r + gi * sp_i, axis=0, keepdims=True)
        dl_ref[:, S5_LB:] += jnp.sum(gi * sp_r - gr * sp_i, axis=0, keepdims=True)

    return pl.pallas_call(
        body, name="s5_scan_bwd", grid=(S5_NJ, n_seq, n_t),
        in_specs=[pl.BlockSpec((S5_TB, blk), rows), pl.BlockSpec((S5_TB, blk), rows),
                  pl.BlockSpec((8, blk), halo), pl.BlockSpec((S5_LOG_TB, blk), lambda j, b, t: (0, j))],
        out_specs=[pl.BlockSpec((S5_TB, blk), rows), pl.BlockSpec((1, blk), lambda j, b, t: (0, j))],
        out_shape=[jax.ShapeDtypeStruct(ds.shape, BF16), jax.ShapeDtypeStruct((1, ds.shape[1]), F32)],
        scratch_shapes=[pltpu.VMEM((1, S5_LB), F32), pltpu.VMEM((1, S5_LB), F32)],
        compiler_params=_cparams(("parallel", "arbitrary", "arbitrary")),
    )(ds, s, s, lam_pow_conj)


CONV_TR = 256
CONV_CW = 512


def _shift_down(x, halo, k):
    if k == 0:
        return x
    row8 = lax.broadcasted_iota(jnp.int32, halo.shape, 0)
    rolled = pltpu.roll(x, k, 0)
    top = jnp.where(row8 < k, pltpu.roll(halo, k, 0), rolled[:8])
    if x.shape[0] == 8:
        return top
    return jnp.concatenate([top, rolled[8:]], axis=0)


def _shift_up(x, halo, k):
    if k == 0:
        return x
    n = x.shape[0]
    row8 = lax.broadcasted_iota(jnp.int32, halo.shape, 0)
    rolled = pltpu.roll(x, n - k, 0)
    bot = jnp.where(row8 >= 8 - k, pltpu.roll(halo, 8 - k, 0), rolled[n - 8:])
    if n == 8:
        return bot
    return jnp.concatenate([rolled[:n - 8], bot], axis=0)


def _conv_pre(x, halo, w, b):
    acc = b + w[SSD_CONV - 1:SSD_CONV, :] * x
    for k in range(SSD_CONV - 1):
        acc = acc + w[k:k + 1, :] * _shift_down(x, halo, SSD_CONV - 1 - k)
    return acc


def _conv_specs(seq_len, col_off):
    lt = seq_len // CONV_TR
    cb = col_off // CONV_CW
    cur = pl.BlockSpec((CONV_TR, CONV_CW), lambda j, i: (i, j + cb))
    prev = pl.BlockSpec((8, CONV_CW), lambda j, i: (jnp.maximum(i * (CONV_TR // 8) - 1, 0), j + cb))
    return lt, cur, prev


def _conv_fwd(proj, conv_w, conv_b, n_rows, seq_len):
    lt, cur, prev = _conv_specs(seq_len, OFF_XBC)

    def body(x_ref, h_ref, w_ref, b_ref, o_ref):
        halo = jnp.where(pl.program_id(1) % lt == 0, 0.0, h_ref[...])
        o_ref[...] = jax.nn.silu(_conv_pre(x_ref[...], halo, w_ref[...], b_ref[...]))

    return pl.pallas_call(
        body, name="ssd_conv_fwd", grid=(SSD_CONV_DIM // CONV_CW, n_rows // CONV_TR),
        in_specs=[cur, prev, pl.BlockSpec((SSD_CONV, CONV_CW), lambda j, i: (0, j)),
                  pl.BlockSpec((1, CONV_CW), lambda j, i: (0, j))],
        out_specs=pl.BlockSpec((CONV_TR, CONV_CW), lambda j, i: (i, j)),
        out_shape=jax.ShapeDtypeStruct((n_rows, SSD_CONV_DIM), F32),
        compiler_params=_cparams(("parallel", "parallel")),
    )(proj, proj, conv_w, conv_b)


def _conv_bwd(name, proj, d_act, conv_w, conv_b, n_rows, seq_len, col_off):
    width = d_act.shape[1]
    lt, cur, prev = _conv_specs(seq_len, OFF_XBC + col_off)
    n_blk = n_rows // CONV_TR
    cb = (OFF_XBC + col_off) // CONV_CW
    pb = col_off // CONV_CW
    nxt = pl.BlockSpec((8, CONV_CW), lambda j, i: (jnp.minimum((i + 1) * (CONV_TR // 8), n_rows // 8 - 1), j + cb))
    d_cur = pl.BlockSpec((CONV_TR, CONV_CW), lambda j, i: (i, j))
    d_nxt = pl.BlockSpec((8, CONV_CW), lambda j, i: (jnp.minimum((i + 1) * (CONV_TR // 8), n_rows // 8 - 1), j))

    def dsilu(pre):
        sg = jax.nn.sigmoid(pre)
        return sg * (1.0 + pre * (1.0 - sg))

    def body(x_ref, hp_ref, hn_ref, d_ref, dn_ref, w_ref, b_ref, dx_ref, dw_ref, db_ref):
        i = pl.program_id(1)
        x, w, b = x_ref[...], w_ref[...], b_ref[...]
        halo_p = jnp.where(i % lt == 0, 0.0, hp_ref[...])
        at_end = i % lt == lt - 1
        dpre = d_ref[...] * dsilu(_conv_pre(x, halo_p, w, b))
        pre_n = _conv_pre(hn_ref[...], x[CONV_TR - 8:, :], w, b)
        dpre_n = jnp.where(at_end, 0.0, dn_ref[...] * dsilu(pre_n))
        dx = w[SSD_CONV - 1:SSD_CONV, :] * dpre
        for k in range(SSD_CONV - 1):
            dx = dx + w[k:k + 1, :] * _shift_up(dpre, dpre_n, SSD_CONV - 1 - k)
        dx_ref[...] = dx.astype(dx_ref.dtype)

        @pl.when(i == 0)
        def _():
            dw_ref[...] = jnp.zeros_like(dw_ref)
            db_ref[...] = jnp.zeros_like(db_ref)

        for k in range(SSD_CONV):
            xs = _shift_down(x, halo_p, SSD_CONV - 1 - k)
            dw_ref[k:k + 1, :] += jnp.sum(dpre * xs, axis=0, keepdims=True)
        db_ref[...] += jnp.sum(dpre, axis=0, keepdims=True)

    return pl.pallas_call(
        body, name=name, grid=(width // CONV_CW, n_blk),
        in_specs=[cur, prev, nxt, d_cur, d_nxt,
                  pl.BlockSpec((SSD_CONV, CONV_CW), lambda j, i: (0, j + pb)),
                  pl.BlockSpec((1, CONV_CW), lambda j, i: (0, j + pb))],
        out_specs=[pl.BlockSpec((CONV_TR, CONV_CW), lambda j, i: (i, j)),
                   pl.BlockSpec((SSD_CONV, CONV_CW), lambda j, i: (0, j)),
                   pl.BlockSpec((1, CONV_CW), lambda j, i: (0, j))],
        out_shape=[jax.ShapeDtypeStruct((n_rows, width), BF16),
                   jax.ShapeDtypeStruct((SSD_CONV, width), F32), jax.ShapeDtypeStruct((1, width), F32)],
        compiler_params=_cparams(("parallel", "arbitrary")),
    )(proj, proj, proj, d_act, d_act, conv_w, conv_b)


def _split3(x):
    hi = x.astype(BF16)
    r = x - hi.astype(F32)
    mid = r.astype(BF16)
    return hi, mid, (r - mid.astype(F32)).astype(BF16)


def _sel_dot(a, b, a_is_sel):
    dn = (((1,), (0,)), ((), ()))
    if a_is_sel:
        return sum(lax.dot_general(a, t, dn, preferred_element_type=F32) for t in _split3(b))
    return sum(lax.dot_general(t, b, dn, preferred_element_type=F32) for t in _split3(a))


@jax.custom_vjp
def sel_left(sel, sel_t, x):
    return _sel_dot(sel, x, True)


@jax.custom_vjp
def sel_right(x, sel, sel_t):
    return _sel_dot(x, sel, False)


sel_left.defvjp(lambda s, st, x: (_sel_dot(s, x, True), (s, st)),
                lambda r, g: (jnp.zeros_like(r[0]), jnp.zeros_like(r[1]), _sel_dot(r[1], g, True)))
sel_right.defvjp(lambda x, s, st: (_sel_dot(x, s, False), (s, st)),
                 lambda r, g: (_sel_dot(g, r[1], False), jnp.zeros_like(r[0]), jnp.zeros_like(r[1])))


def _ssd_chunk(xs, bm, cm, dtr, st, dtb, alog, dsk, k):
    dt = jax.nn.softplus(dtr + dtb)
    acum = sel_left(k["tri"], k["tri_t"], dt * (-jnp.exp(alog)))
    dt_e = sel_right(dt, k["spread"], k["spread_t"])
    ac_e = sel_right(acum, k["spread"], k["spread_t"])
    al_e = ac_e[SSD_CHUNK - 1:SSD_CHUNK, :]
    dsk_e = sel_right(jnp.broadcast_to(dsk, (8, 128)), k["spread"], k["spread_t"])[0:1, :]
    xdt = xs * dt_e
    acum_t = acum.T
    scores = dot_nt(cm, bm)
    y = dot_nn(cm, st) * jnp.exp(ac_e) + xs * dsk_e
    for j in range(SSD_HPG):
        lmat = jnp.exp(jnp.where(k["causal"], acum[:, j:j + 1] - acum_t[j:j + 1, :], -jnp.inf))
        y = y + dot_nn(scores * lmat, jnp.where(k["head"] == j, xdt, 0.0))
    new = st * jnp.exp(al_e) + dot_tn(bm, xdt * jnp.exp(al_e - ac_e))
    return y, new


def _ssd_consts():
    r = lax.broadcasted_iota(jnp.int32, (SSD_CHUNK, SSD_CHUNK), 0)
    c = lax.broadcasted_iota(jnp.int32, (SSD_CHUNK, SSD_CHUNK), 1)
    hd = jnp.int32(SSD_HEADDIM)
    sr = lax.broadcasted_iota(jnp.int32, (128, GROUP_W), 0)
    sc = lax.div(lax.broadcasted_iota(jnp.int32, (128, GROUP_W), 1), hd)
    tr = lax.div(lax.broadcasted_iota(jnp.int32, (GROUP_W, 128), 0), hd)
    tc = lax.broadcasted_iota(jnp.int32, (GROUP_W, 128), 1)
    return {"tri": (r >= c).astype(BF16), "tri_t": (c >= r).astype(BF16), "causal": r >= c,
            "spread": (sr == sc).astype(BF16), "spread_t": (tr == tc).astype(BF16),
            "head": lax.div(lax.broadcasted_iota(jnp.int32, (SSD_CHUNK, GROUP_W), 1), hd)}


def _ssd_specs(n_c, reverse):
    def cidx(c):
        return n_c - 1 - c if reverse else c

    xs = pl.BlockSpec((SSD_CHUNK, GROUP_W), lambda g, b, c: (b * n_c + cidx(c), g))
    bm = pl.BlockSpec((SSD_CHUNK, SSD_STATE), lambda g, b, c: (b * n_c + cidx(c), SSD_WIDTH // SSD_STATE + g))
    cm = pl.BlockSpec((SSD_CHUNK, SSD_STATE), lambda g, b, c: (b * n_c + cidx(c), (SSD_WIDTH + SSD_BC) // SSD_STATE + g))
    dt = pl.BlockSpec((SSD_CHUNK, 128), lambda g, b, c: (b * n_c + cidx(c), OFF_DT // 128 + g))
    par = pl.BlockSpec((1, 128), lambda g, b, c: (0, g))
    st = pl.BlockSpec((1, 1, 1, SSD_STATE, GROUP_W), lambda g, b, c: (b, cidx(c), g, 0, 0))
    return xs, bm, cm, dt, par, st


def _ssd_fwd(xbc_act, proj, dtb, alog, dsk, n_seq, seq_len):
    n_c = seq_len // SSD_CHUNK
    xs_s, bm_s, cm_s, dt_s, par_s, st_s = _ssd_specs(n_c, False)

    def body(xs_ref, bm_ref, cm_ref, dt_ref, dtb_ref, al_ref, dk_ref, y_ref, st_ref, state):
        @pl.when(pl.program_id(2) == 0)
        def _():
            state[...] = jnp.zeros_like(state)

        prev = state[...]
        st_ref[0, 0, 0] = prev
        y, new = _ssd_chunk(xs_ref[...], bm_ref[...], cm_ref[...], dt_ref[...], prev,
                            dtb_ref[...], al_ref[...], dk_ref[...], _ssd_consts())
        y_ref[...] = y
        state[...] = new

    return pl.pallas_call(
        body, name="ssd_fwd", grid=(SSD_GROUPS, n_seq, n_c),
        in_specs=[xs_s, bm_s, cm_s, dt_s, par_s, par_s, par_s],
        out_specs=[pl.BlockSpec((SSD_CHUNK, GROUP_W), lambda g, b, c: (b * n_c + c, g)), st_s],
        out_shape=[jax.ShapeDtypeStruct((n_seq * seq_len, SSD_WIDTH), F32),
                   jax.ShapeDtypeStruct((n_seq, n_c, SSD_GROUPS, SSD_STATE, GROUP_W), F32)],
        scratch_shapes=[pltpu.VMEM((SSD_STATE, GROUP_W), F32)],
        compiler_params=_cparams(("parallel", "parallel", "arbitrary")),
    )(xbc_act, xbc_act, xbc_act, proj, dtb, alog, dsk)


def _ssd_bwd(xbc_act, proj, states, dy, dtb, alog, dsk, n_seq, seq_len):
    n_c = seq_len // SSD_CHUNK
    n_rows = n_seq * seq_len
    xs_s, bm_s, cm_s, dt_s, par_s, st_s = _ssd_specs(n_c, True)

    def rows(w):
        return pl.BlockSpec((SSD_CHUNK, w), lambda g, b, c: (b * n_c + (n_c - 1 - c), g))

    def body(xs_ref, bm_ref, cm_ref, dt_ref, st_ref, dy_ref, dtb_ref, al_ref, dk_ref,
             dxs_ref, dbm_ref, dcm_ref, ddt_ref, ddtb_ref, dal_ref, ddk_ref, dstate):
        b, c = pl.program_id(1), pl.program_id(2)

        @pl.when(c == 0)
        def _():
            dstate[...] = jnp.zeros_like(dstate)

        @pl.when((b == 0) & (c == 0))
        def _():
            ddtb_ref[...] = jnp.zeros_like(ddtb_ref)
            dal_ref[...] = jnp.zeros_like(dal_ref)
            ddk_ref[...] = jnp.zeros_like(ddk_ref)

        consts = _ssd_consts()
        _, vjp = jax.vjp(
            lambda xs, bm, cm, dtr, prev, dtb, alog, dsk: _ssd_chunk(xs, bm, cm, dtr, prev, dtb, alog, dsk, consts),
            xs_ref[...], bm_ref[...], cm_ref[...], dt_ref[...], st_ref[0, 0, 0], dtb_ref[...], al_ref[...], dk_ref[...])
        dxs, dbm, dcm, ddtr, dprev, ddtb, dal, ddk = vjp((dy_ref[...], dstate[...]))
        dxs_ref[...] = dxs
        dbm_ref[...] = dbm
        dcm_ref[...] = dcm
        ddt_ref[...] = ddtr.astype(ddt_ref.dtype)
        ddtb_ref[...] += ddtb
        dal_ref[...] += dal
        ddk_ref[...] += ddk
        dstate[...] = dprev

    acc = pl.BlockSpec((1, 128), lambda g, b, c: (0, g))
    return pl.pallas_call(
        body, name="ssd_bwd", grid=(SSD_GROUPS, n_seq, n_c),
        in_specs=[xs_s, bm_s, cm_s, dt_s, st_s, rows(GROUP_W), par_s, par_s, par_s],
        out_specs=[rows(GROUP_W), rows(SSD_STATE), rows(SSD_STATE), rows(128), acc, acc, acc],
        out_shape=[jax.ShapeDtypeStruct((n_rows, SSD_WIDTH), F32), jax.ShapeDtypeStruct((n_rows, SSD_BC), F32),
                   jax.ShapeDtypeStruct((n_rows, SSD_BC), F32), jax.ShapeDtypeStruct((n_rows, 4 * 128), BF16),
                   jax.ShapeDtypeStruct((1, 512), F32), jax.ShapeDtypeStruct((1, 512), F32),
                   jax.ShapeDtypeStruct((1, 512), F32)],
        scratch_shapes=[pltpu.VMEM((SSD_STATE, GROUP_W), F32)],
        compiler_params=_cparams(("parallel", "arbitrary", "arbitrary")),
    )(xbc_act, xbc_act, xbc_act, proj, states, dy, dtb, alog, dsk)


def _pad_heads(v):
    return jnp.pad(v.reshape(SSD_GROUPS, SSD_HPG), ((0, 0), (0, 128 - SSD_HPG))).reshape(1, SSD_GROUPS * 128)


def _unpad_heads(v):
    return v.reshape(SSD_GROUPS, 128)[:, :SSD_HPG].reshape(1, SSD_HEADS)


def _state_cols(v):
    re, im = v
    lead = re.shape[:-1]
    re = re.reshape(lead + (S5_NJ, 1, S5_LB))
    im = im.reshape(lead + (S5_NJ, 1, S5_LB))
    return jnp.concatenate([re, im], axis=-2).reshape(lead + (2 * S5_N,))


def _state_uncols(v):
    lead = v.shape[:-1]
    v = v.reshape(lead + (S5_NJ, 2, S5_LB))
    return v[..., 0, :].reshape(lead + (S5_N,)), v[..., 1, :].reshape(lead + (S5_N,))


GROUPS_PER_BAND = BAND // S5_GROUP


def _band(w2_re, w2_im):
    gh = S5_GROUPS * S5_GROUP
    rg = ((jnp.arange(gh) // S5_GROUP) % GROUPS_PER_BAND)[:, None, None]
    cg = jnp.arange(GROUPS_PER_BAND)[None, :, None]
    parts = [jnp.where(rg == cg, v[:, None, :], 0.0).reshape(gh, S5_LB) for v in (w2_re, w2_im)]
    return jnp.concatenate(parts, axis=1)


def _band_take(wb):
    gh = S5_GROUPS * S5_GROUP
    w4 = wb.reshape(gh, 2, GROUPS_PER_BAND, S5_STATE)
    sel = w4[jnp.arange(gh), :, (jnp.arange(gh) // S5_GROUP) % GROUPS_PER_BAND, :]
    return sel[:, 0, :], sel[:, 1, :]


def _pad_w_in(w_in):
    u5, z5, zs, xbc, dt, gl = (w_in[:, 0:512], w_in[:, 512:1024], w_in[:, 1024:2560], w_in[:, 2560:5120],
                               w_in[:, 5120:5144], w_in[:, 5144:7192])
    dtp = jnp.pad(dt.reshape(-1, SSD_GROUPS, SSD_HPG), ((0, 0), (0, 0), (0, 128 - SSD_HPG))).reshape(-1, 512)
    return jnp.concatenate([xbc, u5, zs, z5, gl, dtp], axis=1)


def _unpad_w_in(wp):
    dt = wp[:, OFF_DT:].reshape(-1, SSD_GROUPS, 128)[:, :, :SSD_HPG].reshape(-1, SSD_HEADS)
    return jnp.concatenate([wp[:, OFF_U5:OFF_U5 + 512], wp[:, OFF_Z5:OFF_Z5 + 512], wp[:, OFF_ZS:OFF_ZS + 1536],
                            wp[:, OFF_XBC:OFF_XBC + 2560], dt, wp[:, OFF_G5:OFF_G5 + 2048]], axis=1)


def _local_step(x, p, tgt, w):
    n_seq, seq_len, _ = x.shape
    n_rows = n_seq * seq_len
    tr = 256
    x2 = x.reshape(n_rows, D_MODEL)
    p2 = p.reshape(n_rows, PLE_DIM)
    t2 = tgt.reshape(n_rows, D_MODEL)
    row = functools.partial(_rowwise, n_rows=n_rows, tr=tr)

    w_pad = _pad_w_in(w["w_in"])
    norm_w = w["norm_w"].reshape(1, D_MODEL)
    ple_norm_w = w["ple_norm_w"].reshape(1, D_MODEL)
    final_w = w["final_norm_w"].reshape(1, D_MODEL)
    s5_d = w["s5_d"].reshape(1, S5_WIDTH)
    b_glu = w["s5_b_glu"].reshape(1, S5_WIDTH)
    conv_w = w["ssd_conv_w"].reshape(SSD_CONV, SSD_CONV_DIM)
    conv_b = w["ssd_conv_b"].reshape(1, SSD_CONV_DIM)
    ssd_norm_w = w["ssd_norm_w"].reshape(1, SSD_WIDTH)
    dtb, alog, dsk = (_pad_heads(w[k].reshape(1, SSD_HEADS)) for k in ("ssd_dt_bias", "ssd_a_log", "ssd_d"))

    gh = S5_GROUPS * S5_GROUP
    a_re = w["s5_a_re"].reshape(S5_GROUPS, S5_STATE)
    a_im = w["s5_a_im"].reshape(S5_GROUPS, S5_STATE)
    log_step = w["s5_log_step"].reshape(S5_GROUPS, 1)
    b_re2 = jnp.transpose(w["s5_b_re"].reshape(S5_GROUPS, S5_STATE, S5_GROUP), (0, 2, 1)).reshape(gh, S5_STATE)
    b_im2 = jnp.transpose(w["s5_b_im"].reshape(S5_GROUPS, S5_STATE, S5_GROUP), (0, 2, 1)).reshape(gh, S5_STATE)
    expand = (jnp.arange(gh)[:, None] // S5_GROUP == jnp.arange(S5_GROUPS)[None, :]).astype(F32)
    pow_re, pow_im, bb_re2, bb_im2 = _s5_params_fwd(a_re, a_im, log_step, b_re2, b_im2, expand)
    lam_pow = _state_cols((pow_re.reshape(S5_LOG_TB, S5_N), pow_im.reshape(S5_LOG_TB, S5_N)))
    lam_pow_conj = _state_cols((pow_re.reshape(S5_LOG_TB, S5_N), -pow_im.reshape(S5_LOG_TB, S5_N)))
    bb_band = _band(bb_re2, bb_im2).astype(BF16)
    c_band = _band(w["s5_c_re"].reshape(gh, S5_STATE), -w["s5_c_im"].reshape(gh, S5_STATE)).astype(BF16)

    (hn,) = row("rms_in", lambda r, q: ([_rms(r[0], q[0])], []), row_ins=[(x2, 0, D_MODEL)],
                par_ins=[(norm_w, 0, D_MODEL)], row_outs=[(D_MODEL, BF16)])
    proj = _matmul("mm_proj", hn, w_pad)
    bu = _band_matmul("mm_s5_bu", "nn", proj, bb_band, a_blk0=OFF_U5 // BAND)
    s = _s5_scan_fwd(bu, lam_pow, n_seq, seq_len)
    yc = _band_matmul("mm_s5_y", "nt", s, c_band)
    (ge,) = row("s5_gelu", lambda r, q: ([jax.nn.gelu(r[0] + q[0] * r[1])], []),
                row_ins=[(yc, 0, S5_WIDTH), (proj, OFF_U5, S5_WIDTH)], par_ins=[(s5_d, 0, S5_WIDTH)],
                row_outs=[(S5_WIDTH, BF16)])
    tg = _matmul("mm_s5_glu", ge, w["s5_w_glu"])
    s5_rows = [(yc, 0, S5_WIDTH), (proj, OFF_U5, S5_WIDTH), (proj, OFF_Z5, S5_WIDTH), (tg, 0, S5_WIDTH)]
    s5_pars = [(s5_d, 0, S5_WIDTH), (b_glu, 0, S5_WIDTH)]
    (y5,) = row("s5_out", lambda r, q: ([_s5_out(*r, *q)], []), row_ins=s5_rows, par_ins=s5_pars,
                row_outs=[(S5_WIDTH, BF16)])

    xbc_act = _conv_fwd(proj, conv_w, conv_b, n_rows, seq_len)
    y_ssd, states = _ssd_fwd(xbc_act, proj, dtb, alog, dsk, n_seq, seq_len)
    gn_rows = [(y_ssd, 0, SSD_WIDTH), (proj, OFF_ZS, SSD_WIDTH)]
    (yss,) = row("ssd_gate", lambda r, q: ([_gated_norm(r[0], r[1], q[0])], []), row_ins=gn_rows,
                 par_ins=[(ssd_norm_w, 0, SSD_WIDTH)], row_outs=[(SSD_WIDTH, BF16)])

    m5 = _matmul("mm_br_s5", y5, w["w_br_s5"])
    ms = _matmul("mm_br_ssd", yss, w["w_br_ssd"])
    mg_rows = [(proj, OFF_G5, D_MODEL), (proj, OFF_GS, D_MODEL), (m5, 0, D_MODEL), (ms, 0, D_MODEL)]
    (merged,) = row("merge", lambda r, q: ([_merge(*r)], []), row_ins=mg_rows, par_ins=[], row_outs=[(D_MODEL, BF16)])
    o = _matmul("mm_out", merged, w["w_out"])

    def resid_fn(r, q):
        h1 = r[0] + r[1]
        return [h1, _rms(h1, q[0])], []

    h1, hp = row("resid_ple_norm", resid_fn, row_ins=[(x2, 0, D_MODEL), (o, 0, D_MODEL)],
                 par_ins=[(ple_norm_w, 0, D_MODEL)], row_outs=[(D_MODEL, F32), (D_MODEL, BF16)])
    pgl = _matmul("mm_ple_gate", hp, w["w_ple_gate"])
    pp = _matmul("mm_ple_proj", p2, w["w_ple_proj"])

    def head_fn(r, q):
        h1_, pgl_, pp_, tgt_ = r
        loss, vjp = jax.vjp(lambda a, b, c, f: _head_loss(a, b, c, f, tgt_), h1_, pgl_, pp_, q[0])
        dh1_, dpgl_, dpp_, dfw_ = vjp(jnp.ones_like(loss))
        return [dh1_, dpgl_, dpp_], [loss, dfw_]

    dh2, dpgl, dpp, loss_acc, d_final_w = row(
        "head", head_fn, row_ins=[(h1, 0, D_MODEL), (pgl, 0, D_MODEL), (pp, 0, D_MODEL), (t2, 0, D_MODEL)],
        par_ins=[(final_w, 0, D_MODEL)], row_outs=[(D_MODEL, F32), (D_MODEL, BF16), (D_MODEL, BF16)],
        acc_outs=[(1, 128), (1, D_MODEL)])
    loss = loss_acc[0, 0]

    g = {}
    g["final_norm_w"] = d_final_w
    g["w_ple_gate"] = _matmul("mm_d_w_ple_gate", hp, dpgl, ta=True)
    g["w_ple_proj"] = _matmul("mm_d_w_ple_proj", p2, dpp, ta=True)
    dhp = _matmul("mm_d_hp", dpgl, w["w_ple_gate"], tb=True)

    def ple_norm_bwd(r, q):
        h1_, dhp_, dh2_ = r
        _, vjp = jax.vjp(_rms, h1_, q[0])
        dh, dw = vjp(dhp_)
        dh = dh + dh2_
        return [dh, dh], [dw]

    dh1, dh1_b, g["ple_norm_w"] = row(
        "ple_norm_bwd", ple_norm_bwd, row_ins=[(h1, 0, D_MODEL), (dhp, 0, D_MODEL), (dh2, 0, D_MODEL)],
        par_ins=[(ple_norm_w, 0, D_MODEL)], row_outs=[(D_MODEL, F32), (D_MODEL, BF16)], acc_outs=[(1, D_MODEL)])
    g["w_out"] = _matmul("mm_d_w_out", merged, dh1_b, ta=True)
    dmerged = _matmul("mm_d_merged", dh1_b, w["w_out"], tb=True)

    def merge_bwd(r, q):
        _, vjp = jax.vjp(_merge, *r[:4])
        return list(vjp(r[4])), []

    dg5, dgs, dm5, dms = row("merge_bwd", merge_bwd, row_ins=mg_rows + [(dmerged, 0, D_MODEL)], par_ins=[],
                             row_outs=[(D_MODEL, BF16)] * 4)
    g["w_br_s5"] = _matmul("mm_d_w_br_s5", y5, dm5, ta=True)
    g["w_br_ssd"] = _matmul("mm_d_w_br_ssd", yss, dms, ta=True)
    dy5 = _matmul("mm_d_y5", dm5, w["w_br_s5"], tb=True)
    dyss = _matmul("mm_d_yss", dms, w["w_br_ssd"], tb=True)

    def s5_out_bwd_a(r, q):
        yc_, u_, z_, t_, dy_ = r
        d_, bg_ = q
        ge_ = jax.nn.gelu(yc_ + d_ * u_)
        _, vjp = jax.vjp(lambda a, z, t, b: a * jax.nn.sigmoid(t + b) * jax.nn.silu(z), ge_, z_, t_, bg_)
        dge, dz, dt_, dbg = vjp(dy_)
        return [dge, dz, dt_], [dbg]

    dge_a, dz5, dtg, g["s5_b_glu"] = row(
        "s5_out_bwd_a", s5_out_bwd_a, row_ins=s5_rows + [(dy5, 0, S5_WIDTH)], par_ins=s5_pars,
        row_outs=[(S5_WIDTH, F32), (S5_WIDTH, BF16), (S5_WIDTH, BF16)], acc_outs=[(1, S5_WIDTH)])
    g["s5_w_glu"] = _matmul("mm_d_w_glu", ge, dtg, ta=True)
    dge_b = _matmul("mm_d_ge", dtg, w["s5_w_glu"], tb=True)

    def s5_out_bwd_b(r, q):
        yc_, u_, da_, db_ = r
        _, vjp = jax.vjp(lambda yc, u, d: jax.nn.gelu(yc + d * u), yc_, u_, q[0])
        dyc_, du_, dd_ = vjp(da_ + db_)
        return [dyc_, du_], [dd_]

    dyc, du5_a, g["s5_d"] = row(
        "s5_out_bwd_b", s5_out_bwd_b,
        row_ins=[(yc, 0, S5_WIDTH), (proj, OFF_U5, S5_WIDTH), (dge_a, 0, S5_WIDTH), (dge_b, 0, S5_WIDTH)],
        par_ins=[(s5_d, 0, S5_WIDTH)], row_outs=[(S5_WIDTH, BF16), (S5_WIDTH, F32)], acc_outs=[(1, S5_WIDTH)])
    d_c_band = _band_matmul("mm_d_c", "tn", dyc, s)
    ds = _band_matmul("mm_d_s", "nn", dyc, c_band)
    dbu, d_lam = _s5_scan_bwd(ds, s, lam_pow_conj, n_seq, seq_len)
    d_bb_band = _band_matmul("mm_d_bb", "tn", proj, dbu, a_blk0=OFF_U5 // BAND)
    du5_b = _band_matmul("mm_d_u5", "nt", dbu, bb_band)
    (du5,) = row("s5_du", lambda r, q: ([r[0] + r[1]], []), row_ins=[(du5_a, 0, S5_WIDTH), (du5_b, 0, S5_WIDTH)],
                 par_ins=[], row_outs=[(S5_WIDTH, BF16)])

    d_lr, d_li = _state_uncols(d_lam)
    d_bbr, d_bbi = _band_take(d_bb_band)
    d_are, d_aim, d_ls, d_br2, d_bi2 = _s5_params_bwd(
        a_re, a_im, log_step, b_re2, b_im2, expand,
        d_lr.reshape(S5_GROUPS, S5_STATE), d_li.reshape(S5_GROUPS, S5_STATE), d_bbr, d_bbi)
    g["s5_a_re"], g["s5_a_im"], g["s5_log_step"] = d_are, d_aim, d_ls
    g["s5_b_re"] = jnp.transpose(d_br2.reshape(S5_GROUPS, S5_GROUP, S5_STATE), (0, 2, 1))
    g["s5_b_im"] = jnp.transpose(d_bi2.reshape(S5_GROUPS, S5_GROUP, S5_STATE), (0, 2, 1))
    d_cr, d_ci = _band_take(d_c_band)
    g["s5_c_re"], g["s5_c_im"] = d_cr, -d_ci

    def gate_bwd(r, q):
        _, vjp = jax.vjp(_gated_norm, r[0], r[1], q[0])
        dy_, dz_, dw_ = vjp(r[2])
        return [dy_, dz_], [dw_]

    dy_ssd, dzs, g["ssd_norm_w"] = row(
        "ssd_gate_bwd", gate_bwd, row_ins=gn_rows + [(dyss, 0, SSD_WIDTH)], par_ins=[(ssd_norm_w, 0, SSD_WIDTH)],
        row_outs=[(SSD_WIDTH, F32), (SSD_WIDTH, BF16)], acc_outs=[(1, SSD_WIDTH)])
    dxs, dbm, dcm, ddt, d_dtb, d_alog, d_dsk = _ssd_bwd(xbc_act, proj, states, dy_ssd, dtb, alog, dsk, n_seq, seq_len)
    g["ssd_dt_bias"], g["ssd_a_log"], g["ssd_d"] = _unpad_heads(d_dtb), _unpad_heads(d_alog), _unpad_heads(d_dsk)
    conv_parts = [_conv_bwd("ssd_conv_bwd_x", proj, dxs, conv_w, conv_b, n_rows, seq_len, 0),
                  _conv_bwd("ssd_conv_bwd_b", proj, dbm, conv_w, conv_b, n_rows, seq_len, SSD_WIDTH),
                  _conv_bwd("ssd_conv_bwd_c", proj, dcm, conv_w, conv_b, n_rows, seq_len, SSD_WIDTH + SSD_BC)]
    g["ssd_conv_w"] = jnp.concatenate([c[1] for c in conv_parts], axis=1)
    g["ssd_conv_b"] = jnp.concatenate([c[2] for c in conv_parts], axis=1)

    dproj = jnp.concatenate([c[0] for c in conv_parts] + [du5, dzs, dz5, dg5, dgs, ddt], axis=1)
    g["w_in"] = _unpad_w_in(_matmul("mm_d_w_in", hn, dproj, ta=True))
    dhn = _matmul("mm_d_hn", dproj, w_pad, tb=True)

    def norm_bwd(r, q):
        x_, dhn_, dh1_ = r
        _, vjp = jax.vjp(_rms, x_, q[0])
        dx_, dw_ = vjp(dhn_)
        return [dx_ + dh1_], [dw_]

    dx, g["norm_w"] = row("rms_in_bwd", norm_bwd, row_ins=[(x2, 0, D_MODEL), (dhn, 0, D_MODEL), (dh1, 0, D_MODEL)],
                          par_ins=[(norm_w, 0, D_MODEL)], row_outs=[(D_MODEL, F32)], acc_outs=[(1, D_MODEL)])
    return loss, dx.reshape(x.shape), g


HBM = pl.BlockSpec(memory_space=pltpu.HBM)


def _chip_index(x, y):
    return 2 * x + y


def _gather_chips(split, whole):
    ns, nw = len(split), len(whole)
    n = ns + nw

    def body(*refs):
        ins, outs = refs[:n], refs[n:2 * n]
        ici_send, ici_recv, d2d_send, d2d_recv, local_sems = refs[2 * n:]
        x, y, c = lax.axis_index("x"), lax.axis_index("y"), lax.axis_index("c")
        me = _chip_index(x, y)
        sibling = (x, y, 1 - c)
        peers = [(1 - x, y), (x, 1 - y), (1 - x, 1 - y)]

        def half(t, which):
            h = split[t].shape[0] // 2
            return pl.ds(which * h, h)

        copies = []
        for t in range(n):
            loc = pltpu.make_async_copy(ins[t], outs[t].at[me], local_sems.at[t])
            loc.start()
            copies.append(loc)

        def ici(t, k, slot):
            px, py = peers[k]
            if t < ns:
                src, dst = ins[t].at[half(t, c), :], outs[t].at[slot, half(t, c), :]
            else:
                src, dst = ins[t], outs[t].at[slot]
            return pltpu.make_async_remote_copy(src_ref=src, dst_ref=dst, send_sem=ici_send.at[t, k],
                                                recv_sem=ici_recv.at[t, k], device_id=(px, py, c), device_id_type=MESH)

        def d2d(t, k, which):
            rows = outs[t].at[_chip_index(*peers[k]), half(t, which), :]
            return pltpu.make_async_remote_copy(src_ref=rows, dst_ref=rows, send_sem=d2d_send.at[t, k],
                                                recv_sem=d2d_recv.at[t, k], device_id=sibling, device_id_type=MESH)

        sends = []
        for t in range(n):
            for k in range(3):
                cp = ici(t, k, me)
                cp.start()
                sends.append(cp)
        for t in range(n):
            for k in range(3):
                ici(t, k, _chip_index(*peers[k])).wait_recv()
                if t < ns:
                    cp = d2d(t, k, c)
                    cp.start()
                    sends.append(cp)
        for t in range(ns):
            for k in range(3):
                d2d(t, k, 1 - c).wait_recv()
        for cp in sends:
            cp.wait_send()
        for cp in copies:
            cp.wait()

    arrays = list(split) + list(whole)
    return pl.pallas_call(
        body, name="gather_weights",
        in_specs=[HBM] * n, out_specs=[HBM] * n,
        out_shape=[jax.ShapeDtypeStruct((N_CHIPS,) + a.shape, a.dtype) for a in arrays],
        scratch_shapes=[pltpu.SemaphoreType.DMA((n, 3)), pltpu.SemaphoreType.DMA((n, 3)),
                        pltpu.SemaphoreType.DMA((ns, 3)), pltpu.SemaphoreType.DMA((ns, 3)),
                        pltpu.SemaphoreType.DMA((n,))],
    )(*arrays)


def _swap_halves(slotted, small):
    n = len(slotted)

    def body(*refs):
        ins, sm_in = refs[:n], refs[n]
        outs, sm_out = refs[n + 1:2 * n + 1], refs[2 * n + 1]
        send_sems, recv_sems, local_sem, sm_send, sm_recv = refs[2 * n + 2:]
        x, y, c = lax.axis_index("x"), lax.axis_index("y"), lax.axis_index("c")
        dev = 4 * x + 2 * y + c
        local = pltpu.make_async_copy(sm_in, sm_out.at[dev], local_sem)
        local.start()
        sends = []
        for t in range(n):
            h = slotted[t].shape[1] // 2
            cp = pltpu.make_async_remote_copy(
                src_ref=ins[t].at[:, pl.ds((1 - c) * h, h), :], dst_ref=outs[t], send_sem=send_sems.at[t],
                recv_sem=recv_sems.at[t], device_id=(x, y, 1 - c), device_id_type=MESH)
            cp.start()
            sends.append(cp)
        rel = [(fx, fy, fc) for fx in (0, 1) for fy in (0, 1) for fc in (0, 1)][1:]
        for k, (fx, fy, fc) in enumerate(rel):
            cp = pltpu.make_async_remote_copy(
                src_ref=sm_in, dst_ref=sm_out.at[dev], send_sem=sm_send.at[k], recv_sem=sm_recv.at[k],
                device_id=(x ^ fx, y ^ fy, c ^ fc), device_id_type=MESH)
            cp.start()
            sends.append(cp)
        for cp in sends[:n]:
            cp.wait_recv()
        for k, (fx, fy, fc) in enumerate(rel):
            src_dev = 4 * (x ^ fx) + 2 * (y ^ fy) + (c ^ fc)
            pltpu.make_async_remote_copy(
                src_ref=sm_in, dst_ref=sm_out.at[src_dev], send_sem=sm_send.at[k], recv_sem=sm_recv.at[k],
                device_id=(x ^ fx, y ^ fy, c ^ fc), device_id_type=MESH).wait_recv()
        for cp in sends:
            cp.wait_send()
        local.wait()

    return pl.pallas_call(
        body, name="swap_halves",
        in_specs=[HBM] * (n + 1), out_specs=[HBM] * (n + 1),
        out_shape=[jax.ShapeDtypeStruct((a.shape[0], a.shape[1] // 2, a.shape[2]), a.dtype) for a in slotted]
        + [jax.ShapeDtypeStruct((N_DEV,) + small.shape, small.dtype)],
        scratch_shapes=[pltpu.SemaphoreType.DMA((n,)), pltpu.SemaphoreType.DMA((n,)), pltpu.SemaphoreType.DMA,
                        pltpu.SemaphoreType.DMA((7,)), pltpu.SemaphoreType.DMA((7,))],
    )(*slotted, small)


def _scatter_halves(parts):
    n = len(parts)

    def body(*refs):
        ins, outs = refs[:n], refs[n:2 * n]
        send_sems, recv_sems = refs[2 * n:]
        x, y, c = lax.axis_index("x"), lax.axis_index("y"), lax.axis_index("c")
        peers = [(1 - x, y), (x, 1 - y), (1 - x, 1 - y)]
        sends = []
        for t in range(n):
            for k, (px, py) in enumerate(peers):
                cp = pltpu.make_async_remote_copy(
                    src_ref=ins[t].at[_chip_index(px, py)], dst_ref=outs[t].at[k], send_sem=send_sems.at[t, k],
                    recv_sem=recv_sems.at[t, k], device_id=(px, py, c), device_id_type=MESH)
                cp.start()
                sends.append(cp)
        for cp in sends:
            cp.wait_recv()
        for cp in sends:
            cp.wait_send()

    return pl.pallas_call(
        body, name="scatter_halves",
        in_specs=[HBM] * n, out_specs=[HBM] * n,
        out_shape=[jax.ShapeDtypeStruct((3,) + a.shape[1:], a.dtype) for a in parts],
        scratch_shapes=[pltpu.SemaphoreType.DMA((n, 3)), pltpu.SemaphoreType.DMA((n, 3))],
    )(*parts)


def _pair_sum(name, slotted, other, idx):
    _, r, c = slotted.shape
    h = r // 2
    th = _row_tile(h, 13 * c * 4)
    a4 = slotted.reshape(N_CHIPS, 2, h, c)

    def body(idx_ref, a_ref, b_ref, am_ref, bm_ref, p_ref, own_ref):
        p_ref[...] = (a_ref[:, 0] + b_ref[...]).astype(p_ref.dtype)
        own_ref[...] = am_ref[0, 0] + bm_ref[0]

    grid_spec = pltpu.PrefetchScalarGridSpec(
        num_scalar_prefetch=1, grid=(h // th,),
        in_specs=[pl.BlockSpec((N_CHIPS, 1, th, c), lambda i, s: (0, s[0], i, 0)),
                  pl.BlockSpec((N_CHIPS, th, c), lambda i, s: (0, i, 0)),
                  pl.BlockSpec((1, 1, th, c), lambda i, s: (s[1], s[0], i, 0)),
                  pl.BlockSpec((1, th, c), lambda i, s: (s[1], i, 0))],
        out_specs=[pl.BlockSpec((N_CHIPS, th, c), lambda i, s: (0, i, 0)), pl.BlockSpec((th, c), lambda i, s: (i, 0))])
    return pl.pallas_call(
        body, name=name, grid_spec=grid_spec,
        out_shape=[jax.ShapeDtypeStruct((N_CHIPS, h, c), BF16), jax.ShapeDtypeStruct((h, c), F32)],
        compiler_params=_cparams(("parallel",)),
    )(idx, a4, other, a4, other)


def _sum_parts(name, own, recv):
    h, c = own.shape
    th = _row_tile(h, 4 * c * 4)

    def body(o_ref, r_ref, out_ref):
        acc = o_ref[...]
        for k in range(3):
            acc = acc + r_ref[k].astype(F32)
        out_ref[...] = acc

    return pl.pallas_call(
        body, name=name, grid=(h // th,),
        in_specs=[pl.BlockSpec((th, c), lambda i: (i, 0)), pl.BlockSpec((3, th, c), lambda i: (0, i, 0))],
        out_specs=pl.BlockSpec((th, c), lambda i: (i, 0)),
        out_shape=jax.ShapeDtypeStruct((h, c), F32),
        compiler_params=_cparams(("parallel",)),
    )(own, recv)


def _swap_sibling(parts):
    n = len(parts)

    def body(*refs):
        ins, outs = refs[:n], refs[n:2 * n]
        send_sems, recv_sems = refs[2 * n:]
        x, y, c = lax.axis_index("x"), lax.axis_index("y"), lax.axis_index("c")
        cps = []
        for t in range(n):
            cp = pltpu.make_async_remote_copy(
                src_ref=ins[t], dst_ref=outs[t], send_sem=send_sems.at[t], recv_sem=recv_sems.at[t],
                device_id=(x, y, 1 - c), device_id_type=MESH)
            cp.start()
            cps.append(cp)
        for cp in cps:
            cp.wait_recv()
        for cp in cps:
            cp.wait_send()

    return pl.pallas_call(
        body, name="swap_sibling",
        in_specs=[HBM] * n, out_specs=[HBM] * n,
        out_shape=[jax.ShapeDtypeStruct(a.shape, a.dtype) for a in parts],
        scratch_shapes=[pltpu.SemaphoreType.DMA((n,)), pltpu.SemaphoreType.DMA((n,))],
    )(*parts)


def _sum_slots(name, a):
    k, r, c = a.shape
    tr = _row_tile(r, (k + 1) * c * 4)

    def body(a_ref, o_ref):
        acc = a_ref[0]
        for i in range(1, k):
            acc = acc + a_ref[i]
        o_ref[...] = acc

    return pl.pallas_call(
        body, name=name, grid=(r // tr,),
        in_specs=[pl.BlockSpec((k, tr, c), lambda i: (0, i, 0))],
        out_specs=pl.BlockSpec((tr, c), lambda i: (i, 0)),
        out_shape=jax.ShapeDtypeStruct((r, c), a.dtype),
        compiler_params=_cparams(("parallel",)),
    )(a)


def _adamw(name, w, m, v, g_parts):
    r, c = w.shape
    ng = len(g_parts)
    tr = _row_tile(r, (7 + ng) * c * 4)
    c1 = 1.0 - ADAM_B1 ** ADAM_STEP
    c2 = 1.0 - ADAM_B2 ** ADAM_STEP

    def body(*refs):
        w_ref, m_ref, v_ref = refs[:3]
        g_refs = refs[3:3 + ng]
        go_ref, d_ref, mo_ref, vo_ref = refs[3 + ng:]
        g = g_refs[0][...]
        for gr in g_refs[1:]:
            g = g + gr[...]
        m_new = ADAM_B1 * m_ref[...] + (1.0 - ADAM_B1) * g
        v_new = ADAM_B2 * v_ref[...] + (1.0 - ADAM_B2) * (g * g)
        go_ref[...] = g
        mo_ref[...] = m_new
        vo_ref[...] = v_new
        d_ref[...] = -ADAM_LR * ((m_new / c1) / (jnp.sqrt(v_new / c2) + ADAM_EPS) + ADAM_WD * w_ref[...])

    spec = pl.BlockSpec((tr, c), lambda i: (i, 0))
    return pl.pallas_call(
        body, name=name, grid=(r // tr,),
        in_specs=[spec] * (3 + ng), out_specs=[spec] * 4,
        out_shape=[jax.ShapeDtypeStruct((r, c), F32)] * 4,
        compiler_params=_cparams(("parallel",)),
    )(w, m, v, *g_parts)


def _adamw_halves(name, w, m, v, own, other, idx):
    r, c = w.shape
    h = r // 2
    th = _row_tile(h, 9 * c * 4)
    nb = h // th
    c1 = 1.0 - ADAM_B1 ** ADAM_STEP
    c2 = 1.0 - ADAM_B2 ** ADAM_STEP

    def body(idx_ref, w_ref, m_ref, v_ref, own_ref, oth_ref, go_ref, d_ref, mo_ref, vo_ref):
        g = jnp.where(pl.program_id(0) == idx_ref[0], own_ref[...], oth_ref[...])
        m_new = ADAM_B1 * m_ref[...] + (1.0 - ADAM_B1) * g
        v_new = ADAM_B2 * v_ref[...] + (1.0 - ADAM_B2) * (g * g)
        go_ref[...] = g
        mo_ref[...] = m_new
        vo_ref[...] = v_new
        d_ref[...] = -ADAM_LR * ((m_new / c1) / (jnp.sqrt(v_new / c2) + ADAM_EPS) + ADAM_WD * w_ref[...])

    full = pl.BlockSpec((th, c), lambda hh, i, s: (hh * nb + i, 0))
    part = pl.BlockSpec((th, c), lambda hh, i, s: (i, 0))
    grid_spec = pltpu.PrefetchScalarGridSpec(
        num_scalar_prefetch=1, grid=(2, nb), in_specs=[full, full, full, part, part], out_specs=[full] * 4)
    return pl.pallas_call(
        body, name=name, grid_spec=grid_spec, out_shape=[jax.ShapeDtypeStruct((r, c), F32)] * 4,
        compiler_params=_cparams(("parallel", "parallel")),
    )(idx, w, m, v, own, other)


WEIGHTS = ['norm_w', 'w_in', 's5_a_re', 's5_a_im', 's5_b_re', 's5_b_im', 's5_c_re', 's5_c_im', 's5_d', 's5_log_step',
           's5_w_glu', 's5_b_glu', 'ssd_conv_w', 'ssd_conv_b', 'ssd_dt_bias', 'ssd_a_log', 'ssd_d', 'ssd_norm_w',
           'w_br_s5', 'w_br_ssd', 'w_out', 'ple_norm_w', 'w_ple_gate', 'w_ple_proj', 'final_norm_w']
SHARDED = {'w_in': ((1024, IN_PROJ_DIM), 1), 's5_w_glu': ((512, 512), 0), 'ssd_conv_w': ((SSD_CONV, SSD_CONV_DIM), 1),
           'w_br_s5': ((512, 1024), 1), 'w_br_ssd': ((1536, 1024), 0), 'w_out': ((1024, 1024), 0),
           'w_ple_gate': ((1024, 1024), 0), 'w_ple_proj': ((256, 1024), 1)}
SMALL = [n for n in WEIGHTS if n not in SHARDED]


def _shard2d(name, a):
    (r, c), ax = SHARDED[name]
    return a.reshape((r // N_CHIPS, c) if ax == 0 else (r, c // N_CHIPS))


def _unslot(name, a4):
    (r, c), ax = SHARDED[name]
    if ax == 0:
        return a4.reshape(r, c)
    return jnp.transpose(a4, (1, 0, 2)).reshape(r, c)


def _slot(name, full):
    (r, c), ax = SHARDED[name]
    if ax == 0:
        return full.reshape(N_CHIPS, r // N_CHIPS, c)
    return jnp.transpose(full.reshape(r, N_CHIPS, c // N_CHIPS), (1, 0, 2))


def _pack_small(vals):
    flat = jnp.concatenate([v.reshape(-1).astype(F32) for v in vals])
    rows = -(-flat.shape[0] // (256 * 128)) * 256
    return jnp.pad(flat, (0, rows * 128 - flat.shape[0])).reshape(rows, 128)


def _unpack_small(packed, shapes):
    flat = packed.reshape(-1)
    out, off = [], 0
    for sh in shapes:
        n = math.prod(sh)
        out.append(flat[off:off + n].reshape(sh))
        off += n
    return out


def kernel(x, p, norm_w, w_in, s5_a_re, s5_a_im, s5_b_re, s5_b_im, s5_c_re, s5_c_im, s5_d, s5_log_step, s5_w_glu, s5_b_glu, ssd_conv_w, ssd_conv_b, ssd_dt_bias, ssd_a_log, ssd_d, ssd_norm_w, w_br_s5, w_br_ssd, w_out, ple_norm_w, w_ple_gate, w_ple_proj, final_norm_w, loss_target, m_norm_w, m_w_in, m_s5_a_re, m_s5_a_im, m_s5_b_re, m_s5_b_im, m_s5_c_re, m_s5_c_im, m_s5_d, m_s5_log_step, m_s5_w_glu, m_s5_b_glu, m_ssd_conv_w, m_ssd_conv_b, m_ssd_dt_bias, m_ssd_a_log, m_ssd_d, m_ssd_norm_w, m_w_br_s5, m_w_br_ssd, m_w_out, m_ple_norm_w, m_w_ple_gate, m_w_ple_proj, m_final_norm_w, v_norm_w, v_w_in, v_s5_a_re, v_s5_a_im, v_s5_b_re, v_s5_b_im, v_s5_c_re, v_s5_c_im, v_s5_d, v_s5_log_step, v_s5_w_glu, v_s5_b_glu, v_ssd_conv_w, v_ssd_conv_b, v_ssd_dt_bias, v_ssd_a_log, v_ssd_d, v_ssd_norm_w, v_w_br_s5, v_w_br_ssd, v_w_out, v_ple_norm_w, v_w_ple_gate, v_w_ple_proj, v_final_norm_w):
    args = locals()
    wl = {n: args[n] for n in WEIGHTS}
    ml = {n: args["m_" + n] for n in WEIGHTS}
    vl = {n: args["v_" + n] for n in WEIGHTS}
    big = [n for n in SHARDED if n != 'ssd_conv_w']
    chip = _chip_index(lax.axis_index("x"), lax.axis_index("y"))
    idx = jnp.stack([lax.axis_index("c"), chip]).astype(jnp.int32)

    gathered = _gather_chips([_shard2d(n, wl[n]).astype(BF16) for n in big], [_shard2d('ssd_conv_w', wl['ssd_conv_w'])])
    full = {n: wl[n] for n in SMALL}
    for n, a4 in zip(big + ['ssd_conv_w'], gathered, strict=True):
        full[n] = _unslot(n, a4)

    loss, grad_x, g = _local_step(x, p[0], loss_target, full)

    small_names = SMALL + ['ssd_conv_w']
    small_shapes = [(1, 1)] + [g[n].shape for n in small_names]
    small_pack = _pack_small([loss] + [g[n] for n in small_names])
    slotted = [_slot(n, g[n]) for n in big]
    swapped = _swap_halves(slotted, small_pack)
    pair = [_pair_sum("pair_sum_" + n, a, b, idx) for n, a, b in zip(big, slotted, swapped[:-1], strict=True)]
    small_sum = _sum_slots("sum_small", swapped[-1])
    recv = _scatter_halves([pb for pb, _ in pair])
    halves = [_sum_parts("sum_chips_" + n, own, r) for n, (_, own), r in zip(big, pair, recv, strict=True)]
    other_halves = _swap_sibling(halves)

    out_g, out_d, out_m, out_v = {}, {}, {}, {}
    for n, own, oth in zip(big, halves, other_halves, strict=True):
        res = _adamw_halves("adamw_" + n, _shard2d(n, wl[n]), _shard2d(n, ml[n]), _shard2d(n, vl[n]), own, oth, idx)
        out_g[n], out_d[n], out_m[n], out_v[n] = (r.reshape(wl[n].shape) for r in res)
    sm = _unpack_small(small_sum, small_shapes)
    loss_total = sm[0].reshape(())
    n = 'ssd_conv_w'
    conv_g = lax.dynamic_slice(sm[-1], (0, chip * (SSD_CONV_DIM // N_CHIPS)), (SSD_CONV, SSD_CONV_DIM // N_CHIPS))
    res = _adamw("adamw_" + n, _shard2d(n, wl[n]), _shard2d(n, ml[n]), _shard2d(n, vl[n]), [conv_g])
    out_g[n], out_d[n], out_m[n], out_v[n] = (r.reshape(wl[n].shape) for r in res)
    small_g = _pack_small(sm[1:-1])
    res = _adamw("adamw_small", _pack_small([wl[n] for n in SMALL]), _pack_small([ml[n] for n in SMALL]),
                 _pack_small([vl[n] for n in SMALL]), [small_g])
    shapes = [wl[n].shape for n in SMALL]
    for dst, packed in zip((out_g, out_d, out_m, out_v), res, strict=True):
        for n, val in zip(SMALL, _unpack_small(packed, shapes), strict=True):
            dst[n] = val

    return (loss_total, grad_x, *[out_g[n] for n in WEIGHTS], *[out_d[n] for n in WEIGHTS],
            *[out_m[n] for n in WEIGHTS], *[out_v[n] for n in WEIGHTS])
```

```python
import functools
import math

import jax
import jax.numpy as jnp
from jax import lax
from jax.experimental import pallas as pl
from jax.experimental.pallas import tpu as pltpu

F32 = jnp.float32
BF16 = jnp.bfloat16
MESH = pl.DeviceIdType.MESH

D_MODEL = 1024
PLE_DIM = 256
RMS_EPS = 1e-6
S5_WIDTH = 512
S5_GROUP = 16
S5_GROUPS = 32
S5_STATE = 64
S5_N = S5_GROUPS * S5_STATE
S5_LB = 512
S5_NJ = S5_N // S5_LB
S5_TB = 256
S5_LOG_TB = 8
SSD_WIDTH = 1536
SSD_HEADDIM = 64
SSD_HEADS = 24
SSD_GROUPS = 4
SSD_HPG = 6
SSD_STATE = 128
SSD_CONV = 4
SSD_CHUNK = 128
SSD_BC = 512
SSD_CONV_DIM = 2560
GROUP_W = SSD_WIDTH // SSD_GROUPS
N_CHIPS = 4
N_DEV = 8

OFF_XBC, OFF_U5, OFF_ZS, OFF_Z5, OFF_G5, OFF_GS, OFF_DT = 0, 2560, 3072, 4608, 5120, 6144, 7168
PROJ_W = 7680
IN_PROJ_DIM = 7192

ADAM_LR, ADAM_B1, ADAM_B2, ADAM_EPS, ADAM_WD, ADAM_STEP = 0.001, 0.9, 0.999, 1e-08, 0.01, 10

VMEM_LIMIT = 56 * 1024 * 1024


def _pick(n, cands):
    for c in cands:
        if n % c == 0:
            return c
    return n


ROW_BLOCK_BYTES = 8 * 1024 * 1024


def _row_tile(r, bytes_per_row):
    for t in (r, 4096, 2048, 1024, 512, 256, 128, 64, 32, 16, 8):
        if t <= r and r % t == 0 and t * bytes_per_row <= ROW_BLOCK_BYTES:
            return t
    return r


def _cparams(sem):
    return pltpu.CompilerParams(dimension_semantics=sem, vmem_limit_bytes=VMEM_LIMIT)


def _dg(a, b, ca, cb):
    return lax.dot_general(a.astype(BF16), b.astype(BF16), (((ca,), (cb,)), ((), ())), preferred_element_type=F32)


@jax.custom_vjp
def dot_nn(a, b):
    return _dg(a, b, 1, 0)


@jax.custom_vjp
def dot_nt(a, b):
    return _dg(a, b, 1, 1)


@jax.custom_vjp
def dot_tn(a, b):
    return _dg(a, b, 0, 0)


dot_nn.defvjp(lambda a, b: (_dg(a, b, 1, 0), (a, b)), lambda r, g: (_dg(g, r[1], 1, 1), _dg(r[0], g, 0, 0)))
dot_nt.defvjp(lambda a, b: (_dg(a, b, 1, 1), (a, b)), lambda r, g: (_dg(g, r[1], 1, 0), _dg(g, r[0], 0, 0)))
dot_tn.defvjp(lambda a, b: (_dg(a, b, 0, 0), (a, b)), lambda r, g: (_dg(r[1], g, 1, 1), _dg(r[0], g, 1, 0)))


MM_VMEM_BUDGET = 30 * 1024 * 1024


def _mm_tiles(m, n, k, sa, sb, so):
    best, best_key = None, None
    for tm in (1024, 512, 256, 128, 64, 32, 16, 8):
        if m % tm:
            continue
        for tn in (2048, 1536, 1280, 1024, 768, 640, 512, 384, 256, 128):
            if n % tn:
                continue
            for tk in (k, 2048, 1536, 1280, 1024, 768, 512, 256, 128):
                if k % tk or tk > max(k, 128) or (tk == k and k > 2048 and k % 128 == 0):
                    continue
                need = 2 * (tm * tk * sa + tk * tn * sb + tm * tn * so) + (tm * tn * 4 if tk < k else 0)
                if need > MM_VMEM_BUDGET:
                    continue
                key = (tm * tn * tk, tk)
                if best_key is None or key > best_key:
                    best, best_key = (tm, tn, tk), key
    assert best is not None, (m, n, k)
    return best


def _matmul(name, a, b, *, ta=False, tb=False, a_win=None, out_dtype=F32):
    a_off, a_w = a_win if a_win is not None else (0, a.shape[1])
    if ta:
        kdim, m = a.shape[0], a_w
    else:
        m, kdim = a.shape[0], a_w
    n = b.shape[0] if tb else b.shape[1]
    assert (b.shape[1] if tb else b.shape[0]) == kdim, (name, a.shape, b.shape)
    tm, tn, tk = _mm_tiles(m, n, kdim, a.dtype.itemsize, b.dtype.itemsize, jnp.dtype(out_dtype).itemsize)
    nk = kdim // tk
    if ta:
        assert a_off % tm == 0
        a_spec = pl.BlockSpec((tk, tm), lambda i, j, k: (k, i + a_off // tm))
    else:
        assert a_off % tk == 0
        a_spec = pl.BlockSpec((tm, tk), lambda i, j, k: (i, k + a_off // tk))
    if tb:
        b_spec = pl.BlockSpec((tn, tk), lambda i, j, k: (j, k))
    else:
        b_spec = pl.BlockSpec((tk, tn), lambda i, j, k: (k, j))
    ca, cb = (0 if ta else 1), (1 if tb else 0)

    def body(a_ref, b_ref, o_ref, *acc):
        if nk == 1:
            o_ref[...] = _dg(a_ref[...], b_ref[...], ca, cb).astype(o_ref.dtype)
            return
        (acc_ref,) = acc
        k = pl.program_id(2)

        @pl.when(k == 0)
        def _():
            acc_ref[...] = jnp.zeros_like(acc_ref)

        acc_ref[...] += _dg(a_ref[...], b_ref[...], ca, cb)

        @pl.when(k == nk - 1)
        def _():
            o_ref[...] = acc_ref[...].astype(o_ref.dtype)

    return pl.pallas_call(
        body, name=name, grid=(m // tm, n // tn, nk),
        in_specs=[a_spec, b_spec],
        out_specs=pl.BlockSpec((tm, tn), lambda i, j, k: (i, j)),
        out_shape=jax.ShapeDtypeStruct((m, n), out_dtype),
        scratch_shapes=[pltpu.VMEM((tm, tn), F32)] if nk > 1 else [],
        compiler_params=_cparams(("parallel", "parallel", "arbitrary")),
    )(a, b)


BAND = 128


def _band_matmul(name, kind, a, b, *, a_blk0=0, out_dtype=F32):
    n_rows = a.shape[0]
    blk = 2 * S5_LB
    tm = _pick(n_rows, (1024, 512, 256))
    if kind == "nn":
        grid = (n_rows // tm, S5_NJ)
        in_specs = [pl.BlockSpec((tm, BAND), lambda i, j: (i, a_blk0 + j)), pl.BlockSpec((BAND, blk), lambda i, j: (j, 0))]
        out_spec = pl.BlockSpec((tm, blk), lambda i, j: (i, j))
        out_shape = (n_rows, S5_NJ * blk)
        sem = ("parallel", "parallel")

        def body(a_ref, b_ref, o_ref):
            o_ref[...] = _dg(a_ref[...], b_ref[...], 1, 0).astype(o_ref.dtype)
    elif kind == "nt":
        grid = (n_rows // tm, S5_NJ)
        in_specs = [pl.BlockSpec((tm, blk), lambda i, j: (i, j)), pl.BlockSpec((BAND, blk), lambda i, j: (j, 0))]
        out_spec = pl.BlockSpec((tm, BAND), lambda i, j: (i, j))
        out_shape = (n_rows, S5_NJ * BAND)
        sem = ("parallel", "parallel")

        def body(a_ref, b_ref, o_ref):
            o_ref[...] = _dg(a_ref[...], b_ref[...], 1, 1).astype(o_ref.dtype)
    else:
        grid = (S5_NJ, n_rows // tm)
        in_specs = [pl.BlockSpec((tm, BAND), lambda j, k: (k, a_blk0 + j)), pl.BlockSpec((tm, blk), lambda j, k: (k, j))]
        out_spec = pl.BlockSpec((BAND, blk), lambda j, k: (j, 0))
        out_shape = (S5_NJ * BAND, blk)
        sem = ("parallel", "arbitrary")

        def body(a_ref, b_ref, o_ref):
            @pl.when(pl.program_id(1) == 0)
            def _():
                o_ref[...] = jnp.zeros_like(o_ref)

            o_ref[...] += _dg(a_ref[...], b_ref[...], 0, 0)

    return pl.pallas_call(
        body, name=name, grid=grid, in_specs=in_specs, out_specs=out_spec,
        out_shape=jax.ShapeDtypeStruct(out_shape, out_dtype), compiler_params=_cparams(sem),
    )(a, b)


def _rowwise(name, fn, n_rows, tr, row_ins, par_ins, row_outs, acc_outs=()):
    nr, npar, no, na = len(row_ins), len(par_ins), len(row_outs), len(acc_outs)
    in_specs = []
    for arr, off, w in row_ins:
        assert off % w == 0 and arr.shape[0] == n_rows, (name, arr.shape, off, w)
        in_specs.append(pl.BlockSpec((tr, w), functools.partial(lambda i, c: (i, c), c=off // w)))
    for arr, off, w in par_ins:
        assert off % w == 0
        in_specs.append(pl.BlockSpec((arr.shape[0], w), functools.partial(lambda i, c: (0, c), c=off // w)))
    out_specs = [pl.BlockSpec((tr, w), lambda i: (i, 0)) for w, _ in row_outs]
    out_specs += [pl.BlockSpec((r, w), lambda i: (0, 0)) for r, w in acc_outs]
    out_shape = [jax.ShapeDtypeStruct((n_rows, w), dt) for w, dt in row_outs]
    out_shape += [jax.ShapeDtypeStruct((r, w), F32) for r, w in acc_outs]

    def body(*refs):
        rows = [r[...] for r in refs[:nr]]
        pars = [r[...] for r in refs[nr:nr + npar]]
        o_refs = refs[nr + npar:nr + npar + no]
        a_refs = refs[nr + npar + no:]
        outs, accs = fn(rows, pars)
        for o_ref, o in zip(o_refs, outs, strict=True):
            o_ref[...] = o.astype(o_ref.dtype)
        if na:
            @pl.when(pl.program_id(0) == 0)
            def _():
                for a_ref in a_refs:
                    a_ref[...] = jnp.zeros_like(a_ref)

            for a_ref, a in zip(a_refs, accs, strict=True):
                a_ref[...] += jnp.broadcast_to(a, a_ref.shape)

    res = pl.pallas_call(
        body, name=name, grid=(n_rows // tr,),
        in_specs=in_specs, out_specs=out_specs, out_shape=out_shape,
        compiler_params=_cparams(("arbitrary",) if na else ("parallel",)),
    )(*[a for a, _, _ in row_ins], *[a for a, _, _ in par_ins])
    return res


def _rms(x, w):
    return x * lax.rsqrt(jnp.mean(x * x, axis=-1, keepdims=True) + RMS_EPS) * w


def _gated_norm(y, z, w):
    outs = []
    for g in range(SSD_GROUPS):
        sl = slice(g * GROUP_W, (g + 1) * GROUP_W)
        yg = y[:, sl] * jax.nn.silu(z[:, sl])
        outs.append(yg * lax.rsqrt(jnp.mean(yg * yg, axis=-1, keepdims=True) + RMS_EPS) * w[:, sl])
    return jnp.concatenate(outs, axis=-1)


def _s5_out(yc, u, z, t, d, bg):
    ge = jax.nn.gelu(yc + d * u)
    return ge * jax.nn.sigmoid(t + bg) * jax.nn.silu(z)


def _merge(g5, gs, m5, ms):
    return jax.nn.sigmoid(g5) * m5 + jax.nn.sigmoid(gs) * ms


def _head_loss(h1, pgl, pp, fw, tgt):
    h2 = h1 + jax.nn.sigmoid(pgl) * pp
    err = _rms(h2, fw) - tgt
    per_row = 0.5 * jnp.mean(err * err, axis=-1, keepdims=True)
    return jnp.sum(per_row, axis=0, keepdims=True)


def _s5_disc(a_re, a_im, log_step, b_re2, b_im2, expand):
    step = jnp.exp(log_step)
    mag = jnp.exp(a_re * step)
    lb_re = mag * jnp.cos(a_im * step)
    lb_im = mag * jnp.sin(a_im * step)
    den = a_re * a_re + a_im * a_im
    n_re = lb_re - 1.0
    f_re = (n_re * a_re + lb_im * a_im) / den
    f_im = (lb_im * a_re - n_re * a_im) / den
    hi = lax.Precision.HIGHEST
    fr = jnp.dot(expand, f_re, precision=hi, preferred_element_type=F32)
    fi = jnp.dot(expand, f_im, precision=hi, preferred_element_type=F32)
    return lb_re, lb_im, fr * b_re2 - fi * b_im2, fr * b_im2 + fi * b_re2


def _s5_params_fwd(a_re, a_im, log_step, b_re2, b_im2, expand):
    gp = a_re.shape

    def body(ar, ai, ls, br, bi, ex, pr_ref, pi_ref, bbr_ref, bbi_ref):
        lr, li, bbr, bbi = _s5_disc(ar[...], ai[...], ls[...], br[...], bi[...], ex[...])
        bbr_ref[...] = bbr
        bbi_ref[...] = bbi
        qr, qi = lr, li
        for k in range(S5_LOG_TB):
            pr_ref[k] = qr
            pi_ref[k] = qi
            qr, qi = qr * lr - qi * li, qr * li + qi * lr

    return pl.pallas_call(
        body, name="s5_params_fwd",
        out_shape=(jax.ShapeDtypeStruct((S5_LOG_TB,) + gp, F32), jax.ShapeDtypeStruct((S5_LOG_TB,) + gp, F32),
                   jax.ShapeDtypeStruct(b_re2.shape, F32), jax.ShapeDtypeStruct(b_re2.shape, F32)),
    )(a_re, a_im, log_step, b_re2, b_im2, expand)


def _s5_params_bwd(a_re, a_im, log_step, b_re2, b_im2, expand, d_lr, d_li, d_bbr, d_bbi):
    def body(ar, ai, ls, br, bi, ex, glr, gli, gbr, gbi, dar, dai, dls, dbr, dbi):
        _, vjp = jax.vjp(lambda *p: _s5_disc(*p, ex[...]), ar[...], ai[...], ls[...], br[...], bi[...])
        g = vjp((glr[...], gli[...], gbr[...], gbi[...]))
        for ref, val in zip((dar, dai, dls, dbr, dbi), g, strict=True):
            ref[...] = val

    return pl.pallas_call(
        body, name="s5_params_bwd",
        out_shape=tuple(jax.ShapeDtypeStruct(v.shape, F32) for v in (a_re, a_im, log_step, b_re2, b_im2)),
    )(a_re, a_im, log_step, b_re2, b_im2, expand, d_lr, d_li, d_bbr, d_bbi)


def _scan_block(x_ref, out_ref, lp, edge_pow, cr, ci, reverse, each=None):
    n_g = x_ref.shape[0] // 8
    sub = lax.broadcasted_iota(jnp.int32, (8, S5_LB), 0)
    steps = []
    for sh in (1, 2, 4):
        keep = (sub < 8 - sh) if reverse else (sub >= sh)
        steps.append((8 - sh if reverse else sh, jnp.where(keep, lp[sh - 1:sh, :S5_LB], 0.0),
                      jnp.where(keep, lp[sh - 1:sh, S5_LB:], 0.0)))
    e_r, e_i = edge_pow[:, :S5_LB], edge_pow[:, S5_LB:]
    for r in (range(n_g - 1, -1, -1) if reverse else range(n_g)):
        rows = slice(8 * r, 8 * r + 8)
        xr, xi = x_ref[rows, :S5_LB], x_ref[rows, S5_LB:]
        for by, a_r, a_i in steps:
            pr, pi = pltpu.roll(xr, by, 0), pltpu.roll(xi, by, 0)
            xr, xi = xr + a_r * pr - a_i * pi, xi + a_r * pi + a_i * pr
        xr = xr + e_r * cr - e_i * ci
        xi = xi + e_r * ci + e_i * cr
        out_ref[rows, :S5_LB] = xr
        out_ref[rows, S5_LB:] = xi
        if each is not None:
            each(r, xr, xi)
        cr, ci = (xr[0:1, :], xi[0:1, :]) if reverse else (xr[7:8, :], xi[7:8, :])
    return cr, ci


def _s5_scan_fwd(bu, lam_pow, n_seq, seq_len):
    n_t = seq_len // S5_TB
    blk = 2 * S5_LB

    def body(bu_ref, lp_ref, s_ref, cr, ci):
        @pl.when(pl.program_id(2) == 0)
        def _():
            cr[...] = jnp.zeros_like(cr)
            ci[...] = jnp.zeros_like(ci)

        lp = lp_ref[...]
        cr[...], ci[...] = _scan_block(bu_ref, s_ref, lp, lp, cr[...], ci[...], False)

    return pl.pallas_call(
        body, name="s5_scan_fwd", grid=(S5_NJ, n_seq, n_t),
        in_specs=[pl.BlockSpec((S5_TB, blk), lambda j, b, t: (b * n_t + t, j)),
                  pl.BlockSpec((S5_LOG_TB, blk), lambda j, b, t: (0, j))],
        out_specs=pl.BlockSpec((S5_TB, blk), lambda j, b, t: (b * n_t + t, j)),
        out_shape=jax.ShapeDtypeStruct(bu.shape, F32),
        scratch_shapes=[pltpu.VMEM((1, S5_LB), F32), pltpu.VMEM((1, S5_LB), F32)],
        compiler_params=_cparams(("parallel", "parallel", "arbitrary")),
    )(bu, lam_pow)


def _s5_scan_bwd(ds, s, lam_pow_conj, lam_pow_conj_rev, n_seq, seq_len):
    n_t = seq_len // S5_TB
    blk = 2 * S5_LB
    halo_per_blk = S5_TB // 8

    def rows(j, b, t):
        return (b * n_t + (n_t - 1 - t), j)

    def halo(j, b, t):
        return (jnp.maximum((b * n_t + (n_t - 1 - t)) * halo_per_blk - 1, 0), j)

    def body(ds_ref, s_ref, h_ref, lp_ref, lpr_ref, g_ref, dl_ref, cr, ci, buf):
        b, t = pl.program_id(1), pl.program_id(2)

        @pl.when(t == 0)
        def _():
            cr[...] = jnp.zeros_like(cr)
            ci[...] = jnp.zeros_like(ci)

        @pl.when((b == 0) & (t == 0))
        def _():
            dl_ref[...] = jnp.zeros_like(dl_ref)

        first_blk = t == n_t - 1
        sub = lax.broadcasted_iota(jnp.int32, (8, S5_LB), 0)
        acc = [jnp.zeros((8, S5_LB), F32), jnp.zeros((8, S5_LB), F32)]

        def each(r, gr, gi):
            rows = slice(8 * r, 8 * r + 8)
            if r == 0:
                before_r = jnp.where(first_blk, 0.0, h_ref[7:8, :S5_LB])
                before_i = jnp.where(first_blk, 0.0, h_ref[7:8, S5_LB:])
            else:
                before_r, before_i = s_ref[8 * r - 1:8 * r, :S5_LB], s_ref[8 * r - 1:8 * r, S5_LB:]
            sp_r = jnp.where(sub == 0, before_r, pltpu.roll(s_ref[rows, :S5_LB], 1, 0))
            sp_i = jnp.where(sub == 0, before_i, pltpu.roll(s_ref[rows, S5_LB:], 1, 0))
            acc[0] = acc[0] + gr * sp_r + gi * sp_i
            acc[1] = acc[1] + gi * sp_r - gr * sp_i

        cr[...], ci[...] = _scan_block(ds_ref, buf, lp_ref[...], lpr_ref[...], cr[...], ci[...], True, each)
        g_ref[...] = buf[...].astype(g_ref.dtype)
        dl_ref[:, :S5_LB] += jnp.sum(acc[0], axis=0, keepdims=True)
        dl_ref[:, S5_LB:] += jnp.sum(acc[1], axis=0, keepdims=True)

    return pl.pallas_call(
        body, name="s5_scan_bwd", grid=(S5_NJ, n_seq, n_t),
        in_specs=[pl.BlockSpec((S5_TB, blk), rows), pl.BlockSpec((S5_TB, blk), rows),
                  pl.BlockSpec((8, blk), halo), pl.BlockSpec((S5_LOG_TB, blk), lambda j, b, t: (0, j)),
                  pl.BlockSpec((S5_LOG_TB, blk), lambda j, b, t: (0, j))],
        out_specs=[pl.BlockSpec((S5_TB, blk), rows), pl.BlockSpec((1, blk), lambda j, b, t: (0, j))],
        out_shape=[jax.ShapeDtypeStruct(ds.shape, BF16), jax.ShapeDtypeStruct((1, ds.shape[1]), F32)],
        scratch_shapes=[pltpu.VMEM((1, S5_LB), F32), pltpu.VMEM((1, S5_LB), F32), pltpu.VMEM((S5_TB, blk), F32)],
        compiler_params=_cparams(("parallel", "arbitrary", "arbitrary")),
    )(ds, s, s, lam_pow_conj, lam_pow_conj_rev)


CONV_TR = 256
CONV_CW = 512


def _shift_down(x, halo, k):
    if k == 0:
        return x
    row8 = lax.broadcasted_iota(jnp.int32, halo.shape, 0)
    rolled = pltpu.roll(x, k, 0)
    top = jnp.where(row8 < k, pltpu.roll(halo, k, 0), rolled[:8])
    if x.shape[0] == 8:
        return top
    return jnp.concatenate([top, rolled[8:]], axis=0)


def _shift_up(x, halo, k):
    if k == 0:
        return x
    n = x.shape[0]
    row8 = lax.broadcasted_iota(jnp.int32, halo.shape, 0)
    rolled = pltpu.roll(x, n - k, 0)
    bot = jnp.where(row8 >= 8 - k, pltpu.roll(halo, 8 - k, 0), rolled[n - 8:])
    if n == 8:
        return bot
    return jnp.concatenate([rolled[:n - 8], bot], axis=0)


def _conv_pre(x, halo, w, b):
    acc = b + w[SSD_CONV - 1:SSD_CONV, :] * x
    for k in range(SSD_CONV - 1):
        acc = acc + w[k:k + 1, :] * _shift_down(x, halo, SSD_CONV - 1 - k)
    return acc


def _conv_specs(seq_len, col_off):
    lt = seq_len // CONV_TR
    cb = col_off // CONV_CW
    cur = pl.BlockSpec((CONV_TR, CONV_CW), lambda j, i: (i, j + cb))
    prev = pl.BlockSpec((8, CONV_CW), lambda j, i: (jnp.maximum(i * (CONV_TR // 8) - 1, 0), j + cb))
    return lt, cur, prev


def _conv_fwd(proj, conv_w, conv_b, n_rows, seq_len):
    lt, cur, prev = _conv_specs(seq_len, OFF_XBC)

    def body(x_ref, h_ref, w_ref, b_ref, o_ref):
        halo = jnp.where(pl.program_id(1) % lt == 0, 0.0, h_ref[...])
        o_ref[...] = jax.nn.silu(_conv_pre(x_ref[...], halo, w_ref[...], b_ref[...]))

    return pl.pallas_call(
        body, name="ssd_conv_fwd", grid=(SSD_CONV_DIM // CONV_CW, n_rows // CONV_TR),
        in_specs=[cur, prev, pl.BlockSpec((SSD_CONV, CONV_CW), lambda j, i: (0, j)),
                  pl.BlockSpec((1, CONV_CW), lambda j, i: (0, j))],
        out_specs=pl.BlockSpec((CONV_TR, CONV_CW), lambda j, i: (i, j)),
        out_shape=jax.ShapeDtypeStruct((n_rows, SSD_CONV_DIM), F32),
        compiler_params=_cparams(("parallel", "parallel")),
    )(proj, proj, conv_w, conv_b)


def _conv_bwd(name, proj, d_act, conv_w, conv_b, n_rows, seq_len, col_off):
    width = d_act.shape[1]
    lt, cur, prev = _conv_specs(seq_len, OFF_XBC + col_off)
    n_blk = n_rows // CONV_TR
    cb = (OFF_XBC + col_off) // CONV_CW
    pb = col_off // CONV_CW
    nxt = pl.BlockSpec((8, CONV_CW), lambda j, i: (jnp.minimum((i + 1) * (CONV_TR // 8), n_rows // 8 - 1), j + cb))
    d_cur = pl.BlockSpec((CONV_TR, CONV_CW), lambda j, i: (i, j))
    d_nxt = pl.BlockSpec((8, CONV_CW), lambda j, i: (jnp.minimum((i + 1) * (CONV_TR // 8), n_rows // 8 - 1), j))

    def dsilu(pre):
        sg = jax.nn.sigmoid(pre)
        return sg * (1.0 + pre * (1.0 - sg))

    def body(x_ref, hp_ref, hn_ref, d_ref, dn_ref, w_ref, b_ref, dx_ref, dw_ref, db_ref):
        i = pl.program_id(1)
        x, w, b = x_ref[...], w_ref[...], b_ref[...]
        halo_p = jnp.where(i % lt == 0, 0.0, hp_ref[...])
        at_end = i % lt == lt - 1
        dpre = d_ref[...] * dsilu(_conv_pre(x, halo_p, w, b))
        pre_n = _conv_pre(hn_ref[...], x[CONV_TR - 8:, :], w, b)
        dpre_n = jnp.where(at_end, 0.0, dn_ref[...] * dsilu(pre_n))
        dx = w[SSD_CONV - 1:SSD_CONV, :] * dpre
        for k in range(SSD_CONV - 1):
            dx = dx + w[k:k + 1, :] * _shift_up(dpre, dpre_n, SSD_CONV - 1 - k)
        dx_ref[...] = dx.astype(dx_ref.dtype)

        @pl.when(i == 0)
        def _():
            dw_ref[...] = jnp.zeros_like(dw_ref)
            db_ref[...] = jnp.zeros_like(db_ref)

        for k in range(SSD_CONV):
            xs = _shift_down(x, halo_p, SSD_CONV - 1 - k)
            dw_ref[k:k + 1, :] += jnp.sum(dpre * xs, axis=0, keepdims=True)
        db_ref[...] += jnp.sum(dpre, axis=0, keepdims=True)

    return pl.pallas_call(
        body, name=name, grid=(width // CONV_CW, n_blk),
        in_specs=[cur, prev, nxt, d_cur, d_nxt,
                  pl.BlockSpec((SSD_CONV, CONV_CW), lambda j, i: (0, j + pb)),
                  pl.BlockSpec((1, CONV_CW), lambda j, i: (0, j + pb))],
        out_specs=[pl.BlockSpec((CONV_TR, CONV_CW), lambda j, i: (i, j)),
                   pl.BlockSpec((SSD_CONV, CONV_CW), lambda j, i: (0, j)),
                   pl.BlockSpec((1, CONV_CW), lambda j, i: (0, j))],
        out_shape=[jax.ShapeDtypeStruct((n_rows, width), BF16),
                   jax.ShapeDtypeStruct((SSD_CONV, width), F32), jax.ShapeDtypeStruct((1, width), F32)],
        compiler_params=_cparams(("parallel", "arbitrary")),
    )(proj, proj, proj, d_act, d_act, conv_w, conv_b)


def _split3(x):
    hi = x.astype(BF16)
    r = x - hi.astype(F32)
    mid = r.astype(BF16)
    return hi, mid, (r - mid.astype(F32)).astype(BF16)


def _sel_dot(a, b, a_is_sel):
    dn = (((1,), (0,)), ((), ()))
    if a_is_sel:
        return sum(lax.dot_general(a, t, dn, preferred_element_type=F32) for t in _split3(b))
    return sum(lax.dot_general(t, b, dn, preferred_element_type=F32) for t in _split3(a))


@jax.custom_vjp
def sel_left(sel, sel_t, x):
    return _sel_dot(sel, x, True)


@jax.custom_vjp
def sel_right(x, sel, sel_t):
    return _sel_dot(x, sel, False)


sel_left.defvjp(lambda s, st, x: (_sel_dot(s, x, True), (s, st)),
                lambda r, g: (jnp.zeros_like(r[0]), jnp.zeros_like(r[1]), _sel_dot(r[1], g, True)))
sel_right.defvjp(lambda x, s, st: (_sel_dot(x, s, False), (s, st)),
                 lambda r, g: (_sel_dot(g, r[1], False), jnp.zeros_like(r[0]), jnp.zeros_like(r[1])))


def _ssd_chunk(xs, bm, cm, dtr, st, dtb, alog, dsk, k):
    dt = jax.nn.softplus(dtr + dtb)
    acum = sel_left(k["tri"], k["tri_t"], dt * (-jnp.exp(alog)))
    dt_e = sel_right(dt, k["spread"], k["spread_t"])
    ac_e = sel_right(acum, k["spread"], k["spread_t"])
    al_e = ac_e[SSD_CHUNK - 1:SSD_CHUNK, :]
    dsk_e = sel_right(jnp.broadcast_to(dsk, (8, 128)), k["spread"], k["spread_t"])[0:1, :]
    xdt = xs * dt_e
    acum_t = acum.T
    scores = dot_nt(cm, bm)
    y = dot_nn(cm, st) * jnp.exp(ac_e) + xs * dsk_e
    for j in range(SSD_HPG):
        lmat = jnp.exp(jnp.where(k["causal"], acum[:, j:j + 1] - acum_t[j:j + 1, :], -jnp.inf))
        y = y + dot_nn(scores * lmat, jnp.where(k["head"] == j, xdt, 0.0))
    new = st * jnp.exp(al_e) + dot_tn(bm, xdt * jnp.exp(al_e - ac_e))
    return y, new


def _ssd_consts():
    r = lax.broadcasted_iota(jnp.int32, (SSD_CHUNK, SSD_CHUNK), 0)
    c = lax.broadcasted_iota(jnp.int32, (SSD_CHUNK, SSD_CHUNK), 1)
    hd = jnp.int32(SSD_HEADDIM)
    sr = lax.broadcasted_iota(jnp.int32, (128, GROUP_W), 0)
    sc = lax.div(lax.broadcasted_iota(jnp.int32, (128, GROUP_W), 1), hd)
    tr = lax.div(lax.broadcasted_iota(jnp.int32, (GROUP_W, 128), 0), hd)
    tc = lax.broadcasted_iota(jnp.int32, (GROUP_W, 128), 1)
    return {"tri": (r >= c).astype(BF16), "tri_t": (c >= r).astype(BF16), "causal": r >= c,
            "spread": (sr == sc).astype(BF16), "spread_t": (tr == tc).astype(BF16),
            "head": lax.div(lax.broadcasted_iota(jnp.int32, (SSD_CHUNK, GROUP_W), 1), hd)}


def _ssd_specs(n_c, reverse):
    def cidx(c):
        return n_c - 1 - c if reverse else c

    xs = pl.BlockSpec((SSD_CHUNK, GROUP_W), lambda g, b, c: (b * n_c + cidx(c), g))
    bm = pl.BlockSpec((SSD_CHUNK, SSD_STATE), lambda g, b, c: (b * n_c + cidx(c), SSD_WIDTH // SSD_STATE + g))
    cm = pl.BlockSpec((SSD_CHUNK, SSD_STATE), lambda g, b, c: (b * n_c + cidx(c), (SSD_WIDTH + SSD_BC) // SSD_STATE + g))
    dt = pl.BlockSpec((SSD_CHUNK, 128), lambda g, b, c: (b * n_c + cidx(c), OFF_DT // 128 + g))
    par = pl.BlockSpec((1, 128), lambda g, b, c: (0, g))
    st = pl.BlockSpec((1, 1, 1, SSD_STATE, GROUP_W), lambda g, b, c: (b, cidx(c), g, 0, 0))
    return xs, bm, cm, dt, par, st


def _ssd_fwd(xbc_act, proj, dtb, alog, dsk, n_seq, seq_len):
    n_c = seq_len // SSD_CHUNK
    xs_s, bm_s, cm_s, dt_s, par_s, st_s = _ssd_specs(n_c, False)

    def body(xs_ref, bm_ref, cm_ref, dt_ref, dtb_ref, al_ref, dk_ref, y_ref, st_ref, state):
        @pl.when(pl.program_id(2) == 0)
        def _():
            state[...] = jnp.zeros_like(state)

        prev = state[...]
        st_ref[0, 0, 0] = prev
        y, new = _ssd_chunk(xs_ref[...], bm_ref[...], cm_ref[...], dt_ref[...], prev,
                            dtb_ref[...], al_ref[...], dk_ref[...], _ssd_consts())
        y_ref[...] = y
        state[...] = new

    return pl.pallas_call(
        body, name="ssd_fwd", grid=(SSD_GROUPS, n_seq, n_c),
        in_specs=[xs_s, bm_s, cm_s, dt_s, par_s, par_s, par_s],
        out_specs=[pl.BlockSpec((SSD_CHUNK, GROUP_W), lambda g, b, c: (b * n_c + c, g)), st_s],
        out_shape=[jax.ShapeDtypeStruct((n_seq * seq_len, SSD_WIDTH), F32),
                   jax.ShapeDtypeStruct((n_seq, n_c, SSD_GROUPS, SSD_STATE, GROUP_W), F32)],
        scratch_shapes=[pltpu.VMEM((SSD_STATE, GROUP_W), F32)],
        compiler_params=_cparams(("parallel", "parallel", "arbitrary")),
    )(xbc_act, xbc_act, xbc_act, proj, dtb, alog, dsk)


def _ssd_bwd(xbc_act, proj, states, dy, dtb, alog, dsk, n_seq, seq_len):
    n_c = seq_len // SSD_CHUNK
    n_rows = n_seq * seq_len
    xs_s, bm_s, cm_s, dt_s, par_s, st_s = _ssd_specs(n_c, True)

    def rows(w):
        return pl.BlockSpec((SSD_CHUNK, w), lambda g, b, c: (b * n_c + (n_c - 1 - c), g))

    def body(xs_ref, bm_ref, cm_ref, dt_ref, st_ref, dy_ref, dtb_ref, al_ref, dk_ref,
             dxs_ref, dbm_ref, dcm_ref, ddt_ref, ddtb_ref, dal_ref, ddk_ref, dstate):
        b, c = pl.program_id(1), pl.program_id(2)

        @pl.when(c == 0)
        def _():
            dstate[...] = jnp.zeros_like(dstate)

        @pl.when((b == 0) & (c == 0))
        def _():
            ddtb_ref[...] = jnp.zeros_like(ddtb_ref)
            dal_ref[...] = jnp.zeros_like(dal_ref)
            ddk_ref[...] = jnp.zeros_like(ddk_ref)

        consts = _ssd_consts()
        _, vjp = jax.vjp(
            lambda xs, bm, cm, dtr, prev, dtb, alog, dsk: _ssd_chunk(xs, bm, cm, dtr, prev, dtb, alog, dsk, consts),
            xs_ref[...], bm_ref[...], cm_ref[...], dt_ref[...], st_ref[0, 0, 0], dtb_ref[...], al_ref[...], dk_ref[...])
        dxs, dbm, dcm, ddtr, dprev, ddtb, dal, ddk = vjp((dy_ref[...], dstate[...]))
        dxs_ref[...] = dxs
        dbm_ref[...] = dbm
        dcm_ref[...] = dcm
        ddt_ref[...] = ddtr.astype(ddt_ref.dtype)
        ddtb_ref[...] += ddtb
        dal_ref[...] += dal
        ddk_ref[...] += ddk
        dstate[...] = dprev

    acc = pl.BlockSpec((1, 128), lambda g, b, c: (0, g))
    return pl.pallas_call(
        body, name="ssd_bwd", grid=(SSD_GROUPS, n_seq, n_c),
        in_specs=[xs_s, bm_s, cm_s, dt_s, st_s, rows(GROUP_W), par_s, par_s, par_s],
        out_specs=[rows(GROUP_W), rows(SSD_STATE), rows(SSD_STATE), rows(128), acc, acc, acc],
        out_shape=[jax.ShapeDtypeStruct((n_rows, SSD_WIDTH), F32), jax.ShapeDtypeStruct((n_rows, SSD_BC), F32),
                   jax.ShapeDtypeStruct((n_rows, SSD_BC), F32), jax.ShapeDtypeStruct((n_rows, 4 * 128), BF16),
                   jax.ShapeDtypeStruct((1, 512), F32), jax.ShapeDtypeStruct((1, 512), F32),
                   jax.ShapeDtypeStruct((1, 512), F32)],
        scratch_shapes=[pltpu.VMEM((SSD_STATE, GROUP_W), F32)],
        compiler_params=_cparams(("parallel", "arbitrary", "arbitrary")),
    )(xbc_act, xbc_act, xbc_act, proj, states, dy, dtb, alog, dsk)


def _pad_heads(v):
    return jnp.pad(v.reshape(SSD_GROUPS, SSD_HPG), ((0, 0), (0, 128 - SSD_HPG))).reshape(1, SSD_GROUPS * 128)


def _unpad_heads(v):
    return v.reshape(SSD_GROUPS, 128)[:, :SSD_HPG].reshape(1, SSD_HEADS)


def _state_cols(v):
    re, im = v
    lead = re.shape[:-1]
    re = re.reshape(lead + (S5_NJ, 1, S5_LB))
    im = im.reshape(lead + (S5_NJ, 1, S5_LB))
    return jnp.concatenate([re, im], axis=-2).reshape(lead + (2 * S5_N,))


def _state_uncols(v):
    lead = v.shape[:-1]
    v = v.reshape(lead + (S5_NJ, 2, S5_LB))
    return v[..., 0, :].reshape(lead + (S5_N,)), v[..., 1, :].reshape(lead + (S5_N,))


GROUPS_PER_BAND = BAND // S5_GROUP


def _band(w2_re, w2_im):
    gh = S5_GROUPS * S5_GROUP
    rg = ((jnp.arange(gh) // S5_GROUP) % GROUPS_PER_BAND)[:, None, None]
    cg = jnp.arange(GROUPS_PER_BAND)[None, :, None]
    parts = [jnp.where(rg == cg, v[:, None, :], 0.0).reshape(gh, S5_LB) for v in (w2_re, w2_im)]
    return jnp.concatenate(parts, axis=1)


def _band_take(wb):
    gh = S5_GROUPS * S5_GROUP
    w4 = wb.reshape(gh, 2, GROUPS_PER_BAND, S5_STATE)
    sel = w4[jnp.arange(gh), :, (jnp.arange(gh) // S5_GROUP) % GROUPS_PER_BAND, :]
    return sel[:, 0, :], sel[:, 1, :]


def _pad_w_in_t(wt):
    u5, z5, zs, xbc, dt, gl = wt[0:512], wt[512:1024], wt[1024:2560], wt[2560:5120], wt[5120:5144], wt[5144:7192]
    dtp = jnp.pad(dt.reshape(SSD_GROUPS, SSD_HPG, -1), ((0, 0), (0, 128 - SSD_HPG), (0, 0))).reshape(512, -1)
    return jnp.concatenate([xbc, u5, zs, z5, gl, dtp], axis=0)


def _unpad_w_in_t(wp):
    dt = wp[OFF_DT:].reshape(SSD_GROUPS, 128, -1)[:, :SSD_HPG].reshape(SSD_HEADS, -1)
    return jnp.concatenate([wp[OFF_U5:OFF_U5 + 512], wp[OFF_Z5:OFF_Z5 + 512], wp[OFF_ZS:OFF_ZS + 1536],
                            wp[OFF_XBC:OFF_XBC + 2560], dt, wp[OFF_G5:OFF_G5 + 2048]], axis=0)


def _local_step(x, p, tgt, w):
    n_seq, seq_len, _ = x.shape
    n_rows = n_seq * seq_len
    tr = 256
    x2 = x.reshape(n_rows, D_MODEL)
    p2 = p.reshape(n_rows, PLE_DIM)
    t2 = tgt.reshape(n_rows, D_MODEL)
    row = functools.partial(_rowwise, n_rows=n_rows, tr=tr)

    w_pad_t = _pad_w_in_t(w["w_in_t"])
    norm_w = w["norm_w"].reshape(1, D_MODEL)
    ple_norm_w = w["ple_norm_w"].reshape(1, D_MODEL)
    final_w = w["final_norm_w"].reshape(1, D_MODEL)
    s5_d = w["s5_d"].reshape(1, S5_WIDTH)
    b_glu = w["s5_b_glu"].reshape(1, S5_WIDTH)
    conv_w = w["ssd_conv_w"].reshape(SSD_CONV, SSD_CONV_DIM)
    conv_b = w["ssd_conv_b"].reshape(1, SSD_CONV_DIM)
    ssd_norm_w = w["ssd_norm_w"].reshape(1, SSD_WIDTH)
    dtb, alog, dsk = (_pad_heads(w[k].reshape(1, SSD_HEADS)) for k in ("ssd_dt_bias", "ssd_a_log", "ssd_d"))

    gh = S5_GROUPS * S5_GROUP
    a_re = w["s5_a_re"].reshape(S5_GROUPS, S5_STATE)
    a_im = w["s5_a_im"].reshape(S5_GROUPS, S5_STATE)
    log_step = w["s5_log_step"].reshape(S5_GROUPS, 1)
    b_re2 = jnp.transpose(w["s5_b_re"].reshape(S5_GROUPS, S5_STATE, S5_GROUP), (0, 2, 1)).reshape(gh, S5_STATE)
    b_im2 = jnp.transpose(w["s5_b_im"].reshape(S5_GROUPS, S5_STATE, S5_GROUP), (0, 2, 1)).reshape(gh, S5_STATE)
    expand = (jnp.arange(gh)[:, None] // S5_GROUP == jnp.arange(S5_GROUPS)[None, :]).astype(F32)
    pow_re, pow_im, bb_re2, bb_im2 = _s5_params_fwd(a_re, a_im, log_step, b_re2, b_im2, expand)
    lam_pow = _state_cols((pow_re.reshape(S5_LOG_TB, S5_N), pow_im.reshape(S5_LOG_TB, S5_N)))
    lam_pow_conj = _state_cols((pow_re.reshape(S5_LOG_TB, S5_N), -pow_im.reshape(S5_LOG_TB, S5_N)))
    bb_band = _band(bb_re2, bb_im2).astype(BF16)
    c_band = _band(w["s5_c_re"].reshape(gh, S5_STATE), -w["s5_c_im"].reshape(gh, S5_STATE)).astype(BF16)

    (hn,) = row("rms_in", lambda r, q: ([_rms(r[0], q[0])], []), row_ins=[(x2, 0, D_MODEL)],
                par_ins=[(norm_w, 0, D_MODEL)], row_outs=[(D_MODEL, BF16)])
    proj = _matmul("mm_proj", hn, w_pad_t, tb=True)
    bu = _band_matmul("mm_s5_bu", "nn", proj, bb_band, a_blk0=OFF_U5 // BAND)
    s = _s5_scan_fwd(bu, lam_pow, n_seq, seq_len)
    yc = _band_matmul("mm_s5_y", "nt", s, c_band)
    (ge,) = row("s5_gelu", lambda r, q: ([jax.nn.gelu(r[0] + q[0] * r[1])], []),
                row_ins=[(yc, 0, S5_WIDTH), (proj, OFF_U5, S5_WIDTH)], par_ins=[(s5_d, 0, S5_WIDTH)],
                row_outs=[(S5_WIDTH, BF16)])
    tg = _matmul("mm_s5_glu", ge, w["s5_w_glu"])
    s5_rows = [(yc, 0, S5_WIDTH), (proj, OFF_U5, S5_WIDTH), (proj, OFF_Z5, S5_WIDTH), (tg, 0, S5_WIDTH)]
    s5_pars = [(s5_d, 0, S5_WIDTH), (b_glu, 0, S5_WIDTH)]
    (y5,) = row("s5_out", lambda r, q: ([_s5_out(*r, *q)], []), row_ins=s5_rows, par_ins=s5_pars,
                row_outs=[(S5_WIDTH, BF16)])

    xbc_act = _conv_fwd(proj, conv_w, conv_b, n_rows, seq_len)
    y_ssd, states = _ssd_fwd(xbc_act, proj, dtb, alog, dsk, n_seq, seq_len)
    gn_rows = [(y_ssd, 0, SSD_WIDTH), (proj, OFF_ZS, SSD_WIDTH)]
    (yss,) = row("ssd_gate", lambda r, q: ([_gated_norm(r[0], r[1], q[0])], []), row_ins=gn_rows,
                 par_ins=[(ssd_norm_w, 0, SSD_WIDTH)], row_outs=[(SSD_WIDTH, BF16)])

    m5 = _matmul("mm_br_s5", y5, w["w_br_s5"])
    ms = _matmul("mm_br_ssd", yss, w["w_br_ssd"])
    mg_rows = [(proj, OFF_G5, D_MODEL), (proj, OFF_GS, D_MODEL), (m5, 0, D_MODEL), (ms, 0, D_MODEL)]
    (merged,) = row("merge", lambda r, q: ([_merge(*r)], []), row_ins=mg_rows, par_ins=[], row_outs=[(D_MODEL, BF16)])
    o = _matmul("mm_out", merged, w["w_out"])

    def resid_fn(r, q):
        h1 = r[0] + r[1]
        return [h1, _rms(h1, q[0])], []

    h1, hp = row("resid_ple_norm", resid_fn, row_ins=[(x2, 0, D_MODEL), (o, 0, D_MODEL)],
                 par_ins=[(ple_norm_w, 0, D_MODEL)], row_outs=[(D_MODEL, F32), (D_MODEL, BF16)])
    pgl = _matmul("mm_ple_gate", hp, w["w_ple_gate"])
    pp = _matmul("mm_ple_proj", p2, w["w_ple_proj"])

    def head_fn(r, q):
        h1_, pgl_, pp_, tgt_ = r
        loss, vjp = jax.vjp(lambda a, b, c, f: _head_loss(a, b, c, f, tgt_), h1_, pgl_, pp_, q[0])
        dh1_, dpgl_, dpp_, dfw_ = vjp(jnp.ones_like(loss))
        return [dh1_, dpgl_, dpp_], [loss, dfw_]

    dh2, dpgl, dpp, loss_acc, d_final_w = row(
        "head", head_fn, row_ins=[(h1, 0, D_MODEL), (pgl, 0, D_MODEL), (pp, 0, D_MODEL), (t2, 0, D_MODEL)],
        par_ins=[(final_w, 0, D_MODEL)], row_outs=[(D_MODEL, F32), (D_MODEL, BF16), (D_MODEL, BF16)],
        acc_outs=[(1, 128), (1, D_MODEL)])
    loss = loss_acc[0, 0]

    g = {}
    g["final_norm_w"] = d_final_w
    g["w_ple_gate"] = _matmul("mm_d_w_ple_gate", hp, dpgl, ta=True)
    g["w_ple_proj"] = _matmul("mm_d_w_ple_proj", p2, dpp, ta=True)
    dhp = _matmul("mm_d_hp", dpgl, w["w_ple_gate"], tb=True)

    def ple_norm_bwd(r, q):
        h1_, dhp_, dh2_ = r
        _, vjp = jax.vjp(_rms, h1_, q[0])
        dh, dw = vjp(dhp_)
        dh = dh + dh2_
        return [dh, dh], [dw]

    dh1, dh1_b, g["ple_norm_w"] = row(
        "ple_norm_bwd", ple_norm_bwd, row_ins=[(h1, 0, D_MODEL), (dhp, 0, D_MODEL), (dh2, 0, D_MODEL)],
        par_ins=[(ple_norm_w, 0, D_MODEL)], row_outs=[(D_MODEL, F32), (D_MODEL, BF16)], acc_outs=[(1, D_MODEL)])
    g["w_out"] = _matmul("mm_d_w_out", merged, dh1_b, ta=True)
    dmerged = _matmul("mm_d_merged", dh1_b, w["w_out"], tb=True)

    def merge_bwd(r, q):
        _, vjp = jax.vjp(_merge, *r[:4])
        return list(vjp(r[4])), []

    dg5, dgs, dm5, dms = row("merge_bwd", merge_bwd, row_ins=mg_rows + [(dmerged, 0, D_MODEL)], par_ins=[],
                             row_outs=[(D_MODEL, BF16)] * 4)
    g["w_br_s5"] = _matmul("mm_d_w_br_s5", y5, dm5, ta=True)
    g["w_br_ssd"] = _matmul("mm_d_w_br_ssd", yss, dms, ta=True)
    dy5 = _matmul("mm_d_y5", dm5, w["w_br_s5"], tb=True)
    dyss = _matmul("mm_d_yss", dms, w["w_br_ssd"], tb=True)

    def s5_out_bwd_a(r, q):
        yc_, u_, z_, t_, dy_ = r
        d_, bg_ = q
        ge_ = jax.nn.gelu(yc_ + d_ * u_)
        _, vjp = jax.vjp(lambda a, z, t, b: a * jax.nn.sigmoid(t + b) * jax.nn.silu(z), ge_, z_, t_, bg_)
        dge, dz, dt_, dbg = vjp(dy_)
        return [dge, dz, dt_], [dbg]

    dge_a, dz5, dtg, g["s5_b_glu"] = row(
        "s5_out_bwd_a", s5_out_bwd_a, row_ins=s5_rows + [(dy5, 0, S5_WIDTH)], par_ins=s5_pars,
        row_outs=[(S5_WIDTH, F32), (S5_WIDTH, BF16), (S5_WIDTH, BF16)], acc_outs=[(1, S5_WIDTH)])
    g["s5_w_glu"] = _matmul("mm_d_w_glu", ge, dtg, ta=True)
    dge_b = _matmul("mm_d_ge", dtg, w["s5_w_glu"], tb=True)

    def s5_out_bwd_b(r, q):
        yc_, u_, da_, db_ = r
        _, vjp = jax.vjp(lambda yc, u, d: jax.nn.gelu(yc + d * u), yc_, u_, q[0])
        dyc_, du_, dd_ = vjp(da_ + db_)
        return [dyc_, du_], [dd_]

    dyc, du5_a, g["s5_d"] = row(
        "s5_out_bwd_b", s5_out_bwd_b,
        row_ins=[(yc, 0, S5_WIDTH), (proj, OFF_U5, S5_WIDTH), (dge_a, 0, S5_WIDTH), (dge_b, 0, S5_WIDTH)],
        par_ins=[(s5_d, 0, S5_WIDTH)], row_outs=[(S5_WIDTH, BF16), (S5_WIDTH, F32)], acc_outs=[(1, S5_WIDTH)])
    d_c_band = _band_matmul("mm_d_c", "tn", dyc, s)
    ds = _band_matmul("mm_d_s", "nn", dyc, c_band)
    dbu, d_lam = _s5_scan_bwd(ds, s, lam_pow_conj, jnp.flip(lam_pow_conj, axis=0), n_seq, seq_len)
    d_bb_band = _band_matmul("mm_d_bb", "tn", proj, dbu, a_blk0=OFF_U5 // BAND)
    du5_b = _band_matmul("mm_d_u5", "nt", dbu, bb_band)
    (du5,) = row("s5_du", lambda r, q: ([r[0] + r[1]], []), row_ins=[(du5_a, 0, S5_WIDTH), (du5_b, 0, S5_WIDTH)],
                 par_ins=[], row_outs=[(S5_WIDTH, BF16)])

    d_lr, d_li = _state_uncols(d_lam)
    d_bbr, d_bbi = _band_take(d_bb_band)
    d_are, d_aim, d_ls, d_br2, d_bi2 = _s5_params_bwd(
        a_re, a_im, log_step, b_re2, b_im2, expand,
        d_lr.reshape(S5_GROUPS, S5_STATE), d_li.reshape(S5_GROUPS, S5_STATE), d_bbr, d_bbi)
    g["s5_a_re"], g["s5_a_im"], g["s5_log_step"] = d_are, d_aim, d_ls
    g["s5_b_re"] = jnp.transpose(d_br2.reshape(S5_GROUPS, S5_GROUP, S5_STATE), (0, 2, 1))
    g["s5_b_im"] = jnp.transpose(d_bi2.reshape(S5_GROUPS, S5_GROUP, S5_STATE), (0, 2, 1))
    d_cr, d_ci = _band_take(d_c_band)
    g["s5_c_re"], g["s5_c_im"] = d_cr, -d_ci

    def gate_bwd(r, q):
        _, vjp = jax.vjp(_gated_norm, r[0], r[1], q[0])
        dy_, dz_, dw_ = vjp(r[2])
        return [dy_, dz_], [dw_]

    dy_ssd, dzs, g["ssd_norm_w"] = row(
        "ssd_gate_bwd", gate_bwd, row_ins=gn_rows + [(dyss, 0, SSD_WIDTH)], par_ins=[(ssd_norm_w, 0, SSD_WIDTH)],
        row_outs=[(SSD_WIDTH, F32), (SSD_WIDTH, BF16)], acc_outs=[(1, SSD_WIDTH)])
    dxs, dbm, dcm, ddt, d_dtb, d_alog, d_dsk = _ssd_bwd(xbc_act, proj, states, dy_ssd, dtb, alog, dsk, n_seq, seq_len)
    g["ssd_dt_bias"], g["ssd_a_log"], g["ssd_d"] = _unpad_heads(d_dtb), _unpad_heads(d_alog), _unpad_heads(d_dsk)
    conv_parts = [_conv_bwd("ssd_conv_bwd_x", proj, dxs, conv_w, conv_b, n_rows, seq_len, 0),
                  _conv_bwd("ssd_conv_bwd_b", proj, dbm, conv_w, conv_b, n_rows, seq_len, SSD_WIDTH),
                  _conv_bwd("ssd_conv_bwd_c", proj, dcm, conv_w, conv_b, n_rows, seq_len, SSD_WIDTH + SSD_BC)]
    g["ssd_conv_w"] = jnp.concatenate([c[1] for c in conv_parts], axis=1)
    g["ssd_conv_b"] = jnp.concatenate([c[2] for c in conv_parts], axis=1)

    dproj = jnp.concatenate([c[0] for c in conv_parts] + [du5, dzs, dz5, dg5, dgs, ddt], axis=1)
    g["w_in_t"] = _unpad_w_in_t(_matmul("mm_d_w_in", dproj, hn, ta=True))
    dhn = _matmul("mm_d_hn", dproj, w_pad_t)

    def norm_bwd(r, q):
        x_, dhn_, dh1_ = r
        _, vjp = jax.vjp(_rms, x_, q[0])
        dx_, dw_ = vjp(dhn_)
        return [dx_ + dh1_], [dw_]

    dx, g["norm_w"] = row("rms_in_bwd", norm_bwd, row_ins=[(x2, 0, D_MODEL), (dhn, 0, D_MODEL), (dh1, 0, D_MODEL)],
                          par_ins=[(norm_w, 0, D_MODEL)], row_outs=[(D_MODEL, F32)], acc_outs=[(1, D_MODEL)])
    return loss, dx.reshape(x.shape), g


HBM = pl.BlockSpec(memory_space=pltpu.HBM)


def _chip_index(x, y):
    return 2 * x + y


def _half(shape2d, axis, which):
    h = shape2d[axis] // 2
    sl = pl.ds(pl.multiple_of(which * h, 128 if axis else 8), h)
    return (slice(None), sl) if axis else (sl, slice(None))


def _gather_chips(split, axes, whole):
    ns, nw = len(split), len(whole)
    n = ns + nw

    def body(*refs):
        ins, outs = refs[:n], refs[n:2 * n]
        ici_send, ici_recv, d2d_send, d2d_recv, local_sems = refs[2 * n:]
        x, y, c = lax.axis_index("x"), lax.axis_index("y"), lax.axis_index("c")
        me = _chip_index(x, y)
        sibling = (x, y, 1 - c)
        peers = [(1 - x, y), (x, 1 - y), (1 - x, 1 - y)]

        def half(t, which):
            return _half(split[t].shape, axes[t], which)

        copies = []
        for t in range(n):
            loc = pltpu.make_async_copy(ins[t], outs[t].at[me], local_sems.at[t])
            loc.start()
            copies.append(loc)

        def ici(t, k, slot):
            px, py = peers[k]
            if t < ns:
                src, dst = ins[t].at[half(t, c)], outs[t].at[(slot,) + half(t, c)]
            else:
                src, dst = ins[t], outs[t].at[slot]
            return pltpu.make_async_remote_copy(src_ref=src, dst_ref=dst, send_sem=ici_send.at[t, k],
                                                recv_sem=ici_recv.at[t, k], device_id=(px, py, c), device_id_type=MESH)

        def d2d(t, k, which):
            rows = outs[t].at[(_chip_index(*peers[k]),) + half(t, which)]
            return pltpu.make_async_remote_copy(src_ref=rows, dst_ref=rows, send_sem=d2d_send.at[t, k],
                                                recv_sem=d2d_recv.at[t, k], device_id=sibling, device_id_type=MESH)

        sends = []
        for t in range(n):
            for k in range(3):
                cp = ici(t, k, me)
                cp.start()
                sends.append(cp)
        for t in range(n):
            for k in range(3):
                ici(t, k, _chip_index(*peers[k])).wait_recv()
                if t < ns:
                    cp = d2d(t, k, c)
                    cp.start()
                    sends.append(cp)
        for t in range(ns):
            for k in range(3):
                d2d(t, k, 1 - c).wait_recv()
        for cp in sends:
            cp.wait_send()
        for cp in copies:
            cp.wait()

    arrays = list(split) + list(whole)
    return pl.pallas_call(
        body, name="gather_weights",
        in_specs=[HBM] * n, out_specs=[HBM] * n,
        out_shape=[jax.ShapeDtypeStruct((N_CHIPS,) + a.shape, a.dtype) for a in arrays],
        scratch_shapes=[pltpu.SemaphoreType.DMA((n, 3)), pltpu.SemaphoreType.DMA((n, 3)),
                        pltpu.SemaphoreType.DMA((ns, 3)), pltpu.SemaphoreType.DMA((ns, 3)),
                        pltpu.SemaphoreType.DMA((n,))],
    )(*arrays)


def _half_shape(shape2d, axis):
    r, c = shape2d
    return (r, c // 2) if axis else (r // 2, c)


def _swap_halves(slotted, axes, small):
    n = len(slotted)

    def body(*refs):
        ins, sm_in = refs[:n], refs[n]
        outs, sm_out = refs[n + 1:2 * n + 1], refs[2 * n + 1]
        send_sems, recv_sems, local_sem, sm_send, sm_recv = refs[2 * n + 2:]
        x, y, c = lax.axis_index("x"), lax.axis_index("y"), lax.axis_index("c")
        dev = 4 * x + 2 * y + c
        local = pltpu.make_async_copy(sm_in, sm_out.at[dev], local_sem)
        local.start()
        sends = []
        for t in range(n):
            other = (slice(None),) + _half(slotted[t].shape[1:], axes[t], 1 - c)
            cp = pltpu.make_async_remote_copy(
                src_ref=ins[t].at[other], dst_ref=outs[t], send_sem=send_sems.at[t],
                recv_sem=recv_sems.at[t], device_id=(x, y, 1 - c), device_id_type=MESH)
            cp.start()
            sends.append(cp)
        rel = [(fx, fy, fc) for fx in (0, 1) for fy in (0, 1) for fc in (0, 1)][1:]
        for k, (fx, fy, fc) in enumerate(rel):
            cp = pltpu.make_async_remote_copy(
                src_ref=sm_in, dst_ref=sm_out.at[dev], send_sem=sm_send.at[k], recv_sem=sm_recv.at[k],
                device_id=(x ^ fx, y ^ fy, c ^ fc), device_id_type=MESH)
            cp.start()
            sends.append(cp)
        for cp in sends[:n]:
            cp.wait_recv()
        for k, (fx, fy, fc) in enumerate(rel):
            src_dev = 4 * (x ^ fx) + 2 * (y ^ fy) + (c ^ fc)
            pltpu.make_async_remote_copy(
                src_ref=sm_in, dst_ref=sm_out.at[src_dev], send_sem=sm_send.at[k], recv_sem=sm_recv.at[k],
                device_id=(x ^ fx, y ^ fy, c ^ fc), device_id_type=MESH).wait_recv()
        for cp in sends:
            cp.wait_send()
        local.wait()

    return pl.pallas_call(
        body, name="swap_halves",
        in_specs=[HBM] * (n + 1), out_specs=[HBM] * (n + 1),
        out_shape=[jax.ShapeDtypeStruct((a.shape[0],) + _half_shape(a.shape[1:], ax), a.dtype)
                   for a, ax in zip(slotted, axes, strict=True)]
        + [jax.ShapeDtypeStruct((N_DEV,) + small.shape, small.dtype)],
        scratch_shapes=[pltpu.SemaphoreType.DMA((n,)), pltpu.SemaphoreType.DMA((n,)), pltpu.SemaphoreType.DMA,
                        pltpu.SemaphoreType.DMA((7,)), pltpu.SemaphoreType.DMA((7,))],
    )(*slotted, small)


def _scatter_halves(parts):
    n = len(parts)

    def body(*refs):
        ins, outs = refs[:n], refs[n:2 * n]
        send_sems, recv_sems = refs[2 * n:]
        x, y, c = lax.axis_index("x"), lax.axis_index("y"), lax.axis_index("c")
        peers = [(1 - x, y), (x, 1 - y), (1 - x, 1 - y)]
        sends = []
        for t in range(n):
            for k, (px, py) in enumerate(peers):
                cp = pltpu.make_async_remote_copy(
                    src_ref=ins[t].at[_chip_index(px, py)], dst_ref=outs[t].at[k], send_sem=send_sems.at[t, k],
                    recv_sem=recv_sems.at[t, k], device_id=(px, py, c), device_id_type=MESH)
                cp.start()
                sends.append(cp)
        for cp in sends:
            cp.wait_recv()
        for cp in sends:
            cp.wait_send()

    return pl.pallas_call(
        body, name="scatter_halves",
        in_specs=[HBM] * n, out_specs=[HBM] * n,
        out_shape=[jax.ShapeDtypeStruct((3,) + a.shape[1:], a.dtype) for a in parts],
        scratch_shapes=[pltpu.SemaphoreType.DMA((n, 3)), pltpu.SemaphoreType.DMA((n, 3))],
    )(*parts)


def _tiling(r, c, f32_per_elem):
    if r % 8 == 0:
        tr = _row_tile(r, f32_per_elem * c * 4)
        return r // tr, (tr, c), lambda i: (i, 0)
    assert c % 128 == 0, (r, c)
    return c // 128, (r, 128), lambda i: (0, i)


def _pair_sum(name, slotted, other, idx, axis):
    _, r, c = slotted.shape
    hr, hc = _half_shape((r, c), axis)
    n, (tr, tc), at = _tiling(hr, hc, 13)
    if axis == 0:
        a = slotted.reshape(N_CHIPS, 2, hr, c)
        a_all = pl.BlockSpec((N_CHIPS, 1, tr, tc), lambda i, s: (0, s[0]) + at(i))
        a_own = pl.BlockSpec((1, 1, tr, tc), lambda i, s: (s[1], s[0]) + at(i))
    else:
        a, per_half = slotted, hc // tc
        a_all = pl.BlockSpec((N_CHIPS, tr, tc), lambda i, s: (0, at(i)[0], s[0] * per_half + at(i)[1]))
        a_own = pl.BlockSpec((1, tr, tc), lambda i, s: (s[1], at(i)[0], s[0] * per_half + at(i)[1]))

    def body(idx_ref, a_ref, b_ref, am_ref, bm_ref, p_ref, own_ref):
        mine, mine_own = (a_ref[:, 0], am_ref[0, 0]) if axis == 0 else (a_ref[...], am_ref[0])
        p_ref[...] = (mine + b_ref[...]).astype(p_ref.dtype)
        own_ref[...] = mine_own + bm_ref[0]

    grid_spec = pltpu.PrefetchScalarGridSpec(
        num_scalar_prefetch=1, grid=(n,),
        in_specs=[a_all, pl.BlockSpec((N_CHIPS, tr, tc), lambda i, s: (0,) + at(i)),
                  a_own, pl.BlockSpec((1, tr, tc), lambda i, s: (s[1],) + at(i))],
        out_specs=[pl.BlockSpec((N_CHIPS, tr, tc), lambda i, s: (0,) + at(i)),
                   pl.BlockSpec((tr, tc), lambda i, s: at(i))])
    return pl.pallas_call(
        body, name=name, grid_spec=grid_spec,
        out_shape=[jax.ShapeDtypeStruct((N_CHIPS, hr, hc), BF16), jax.ShapeDtypeStruct((hr, hc), F32)],
        compiler_params=_cparams(("parallel",)),
    )(idx, a, other, a, other)


def _sum_parts(name, own, recv):
    h, c = own.shape
    n, (tr, tc), at = _tiling(h, c, 4)

    def body(o_ref, r_ref, out_ref):
        acc = o_ref[...]
        for k in range(3):
            acc = acc + r_ref[k].astype(F32)
        out_ref[...] = acc

    return pl.pallas_call(
        body, name=name, grid=(n,),
        in_specs=[pl.BlockSpec((tr, tc), at), pl.BlockSpec((3, tr, tc), lambda i: (0,) + at(i))],
        out_specs=pl.BlockSpec((tr, tc), at),
        out_shape=jax.ShapeDtypeStruct((h, c), F32),
        compiler_params=_cparams(("parallel",)),
    )(own, recv)


def _swap_sibling(parts):
    n = len(parts)

    def body(*refs):
        ins, outs = refs[:n], refs[n:2 * n]
        send_sems, recv_sems = refs[2 * n:]
        x, y, c = lax.axis_index("x"), lax.axis_index("y"), lax.axis_index("c")
        cps = []
        for t in range(n):
            cp = pltpu.make_async_remote_copy(
                src_ref=ins[t], dst_ref=outs[t], send_sem=send_sems.at[t], recv_sem=recv_sems.at[t],
                device_id=(x, y, 1 - c), device_id_type=MESH)
            cp.start()
            cps.append(cp)
        for cp in cps:
            cp.wait_recv()
        for cp in cps:
            cp.wait_send()

    return pl.pallas_call(
        body, name="swap_sibling",
        in_specs=[HBM] * n, out_specs=[HBM] * n,
        out_shape=[jax.ShapeDtypeStruct(a.shape, a.dtype) for a in parts],
        scratch_shapes=[pltpu.SemaphoreType.DMA((n,)), pltpu.SemaphoreType.DMA((n,))],
    )(*parts)


def _sum_slots(name, a):
    k, r, c = a.shape
    tr = _row_tile(r, (k + 1) * c * 4)

    def body(a_ref, o_ref):
        acc = a_ref[0]
        for i in range(1, k):
            acc = acc + a_ref[i]
        o_ref[...] = acc

    return pl.pallas_call(
        body, name=name, grid=(r // tr,),
        in_specs=[pl.BlockSpec((k, tr, c), lambda i: (0, i, 0))],
        out_specs=pl.BlockSpec((tr, c), lambda i: (i, 0)),
        out_shape=jax.ShapeDtypeStruct((r, c), a.dtype),
        compiler_params=_cparams(("parallel",)),
    )(a)


def _adamw(name, w, m, v, g_parts):
    r, c = w.shape
    ng = len(g_parts)
    tr = _row_tile(r, (7 + ng) * c * 4)
    c1 = 1.0 - ADAM_B1 ** ADAM_STEP
    c2 = 1.0 - ADAM_B2 ** ADAM_STEP

    def body(*refs):
        w_ref, m_ref, v_ref = refs[:3]
        g_refs = refs[3:3 + ng]
        go_ref, d_ref, mo_ref, vo_ref = refs[3 + ng:]
        g = g_refs[0][...]
        for gr in g_refs[1:]:
            g = g + gr[...]
        m_new = ADAM_B1 * m_ref[...] + (1.0 - ADAM_B1) * g
        v_new = ADAM_B2 * v_ref[...] + (1.0 - ADAM_B2) * (g * g)
        go_ref[...] = g
        mo_ref[...] = m_new
        vo_ref[...] = v_new
        d_ref[...] = -ADAM_LR * ((m_new / c1) / (jnp.sqrt(v_new / c2) + ADAM_EPS) + ADAM_WD * w_ref[...])

    spec = pl.BlockSpec((tr, c), lambda i: (i, 0))
    return pl.pallas_call(
        body, name=name, grid=(r // tr,),
        in_specs=[spec] * (3 + ng), out_specs=[spec] * 4,
        out_shape=[jax.ShapeDtypeStruct((r, c), F32)] * 4,
        compiler_params=_cparams(("parallel",)),
    )(w, m, v, *g_parts)


def _adamw_halves(name, w, m, v, own, other, idx, axis):
    hr, hc = own.shape
    nb, (tr, tc), at = _tiling(hr, hc, 9)
    c1 = 1.0 - ADAM_B1 ** ADAM_STEP
    c2 = 1.0 - ADAM_B2 ** ADAM_STEP

    def body(idx_ref, w_ref, m_ref, v_ref, own_ref, oth_ref, go_ref, d_ref, mo_ref, vo_ref):
        g = jnp.where(pl.program_id(0) == idx_ref[0], own_ref[...], oth_ref[...])
        m_new = ADAM_B1 * m_ref[...] + (1.0 - ADAM_B1) * g
        v_new = ADAM_B2 * v_ref[...] + (1.0 - ADAM_B2) * (g * g)
        go_ref[...] = g
        mo_ref[...] = m_new
        vo_ref[...] = v_new
        d_ref[...] = -ADAM_LR * ((m_new / c1) / (jnp.sqrt(v_new / c2) + ADAM_EPS) + ADAM_WD * w_ref[...])

    per_half = (hc // tc) if axis else (hr // tr)
    if axis:
        full = pl.BlockSpec((tr, tc), lambda hh, i, s: (at(i)[0], hh * per_half + at(i)[1]))
    else:
        full = pl.BlockSpec((tr, tc), lambda hh, i, s: (hh * per_half + at(i)[0], at(i)[1]))
    part = pl.BlockSpec((tr, tc), lambda hh, i, s: at(i))
    grid_spec = pltpu.PrefetchScalarGridSpec(
        num_scalar_prefetch=1, grid=(2, nb), in_specs=[full, full, full, part, part], out_specs=[full] * 4)
    return pl.pallas_call(
        body, name=name, grid_spec=grid_spec, out_shape=[jax.ShapeDtypeStruct(w.shape, F32)] * 4,
        compiler_params=_cparams(("parallel", "parallel")),
    )(idx, w, m, v, own, other)


WEIGHTS = ['norm_w', 'w_in', 's5_a_re', 's5_a_im', 's5_b_re', 's5_b_im', 's5_c_re', 's5_c_im', 's5_d', 's5_log_step',
           's5_w_glu', 's5_b_glu', 'ssd_conv_w', 'ssd_conv_b', 'ssd_dt_bias', 'ssd_a_log', 'ssd_d', 'ssd_norm_w',
           'w_br_s5', 'w_br_ssd', 'w_out', 'ple_norm_w', 'w_ple_gate', 'w_ple_proj', 'final_norm_w']
SHARDED = {'w_in': ((IN_PROJ_DIM, 1024), 0), 's5_w_glu': ((512, 512), 0), 'ssd_conv_w': ((SSD_CONV, SSD_CONV_DIM), 1),
           'w_br_s5': ((512, 1024), 1), 'w_br_ssd': ((1536, 1024), 0), 'w_out': ((1024, 1024), 0),
           'w_ple_gate': ((1024, 1024), 0), 'w_ple_proj': ((256, 1024), 1)}
TRANSPOSED = ('w_in',)
SMALL = [n for n in WEIGHTS if n not in SHARDED]


def _shard_shape(name):
    (r, c), ax = SHARDED[name]
    return (r // N_CHIPS, c) if ax == 0 else (r, c // N_CHIPS)


def _half_axis(name):
    return 0 if (_shard_shape(name)[0] // 2) % 16 == 0 else 1


def _shard2d(name, a):
    r, c = _shard_shape(name)
    return a.reshape(c, r).T if name in TRANSPOSED else a.reshape(r, c)


def _unshard2d(name, a2, shape):
    return (a2.T if name in TRANSPOSED else a2).reshape(shape)


def _unslot(name, a4):
    (r, c), ax = SHARDED[name]
    if ax == 0:
        return a4.reshape(r, c)
    return jnp.transpose(a4, (1, 0, 2)).reshape(r, c)


def _slot(name, full):
    (r, c), ax = SHARDED[name]
    if ax == 0:
        return full.reshape(N_CHIPS, r // N_CHIPS, c)
    return jnp.transpose(full.reshape(r, N_CHIPS, c // N_CHIPS), (1, 0, 2))


def _pack_small(vals):
    flat = jnp.concatenate([v.reshape(-1).astype(F32) for v in vals])
    rows = -(-flat.shape[0] // (256 * 128)) * 256
    return jnp.pad(flat, (0, rows * 128 - flat.shape[0])).reshape(rows, 128)


def _unpack_small(packed, shapes):
    flat = packed.reshape(-1)
    out, off = [], 0
    for sh in shapes:
        n = math.prod(sh)
        out.append(flat[off:off + n].reshape(sh))
        off += n
    return out


def kernel(x, p, norm_w, w_in, s5_a_re, s5_a_im, s5_b_re, s5_b_im, s5_c_re, s5_c_im, s5_d, s5_log_step, s5_w_glu, s5_b_glu, ssd_conv_w, ssd_conv_b, ssd_dt_bias, ssd_a_log, ssd_d, ssd_norm_w, w_br_s5, w_br_ssd, w_out, ple_norm_w, w_ple_gate, w_ple_proj, final_norm_w, loss_target, m_norm_w, m_w_in, m_s5_a_re, m_s5_a_im, m_s5_b_re, m_s5_b_im, m_s5_c_re, m_s5_c_im, m_s5_d, m_s5_log_step, m_s5_w_glu, m_s5_b_glu, m_ssd_conv_w, m_ssd_conv_b, m_ssd_dt_bias, m_ssd_a_log, m_ssd_d, m_ssd_norm_w, m_w_br_s5, m_w_br_ssd, m_w_out, m_ple_norm_w, m_w_ple_gate, m_w_ple_proj, m_final_norm_w, v_norm_w, v_w_in, v_s5_a_re, v_s5_a_im, v_s5_b_re, v_s5_b_im, v_s5_c_re, v_s5_c_im, v_s5_d, v_s5_log_step, v_s5_w_glu, v_s5_b_glu, v_ssd_conv_w, v_ssd_conv_b, v_ssd_dt_bias, v_ssd_a_log, v_ssd_d, v_ssd_norm_w, v_w_br_s5, v_w_br_ssd, v_w_out, v_ple_norm_w, v_w_ple_gate, v_w_ple_proj, v_final_norm_w):
    args = locals()
    wl = {n: args[n] for n in WEIGHTS}
    ml = {n: args["m_" + n] for n in WEIGHTS}
    vl = {n: args["v_" + n] for n in WEIGHTS}
    big = [n for n in SHARDED if n != 'ssd_conv_w']
    chip = _chip_index(lax.axis_index("x"), lax.axis_index("y"))
    idx = jnp.stack([lax.axis_index("c"), chip]).astype(jnp.int32)

    axes = [_half_axis(n) for n in big]
    gathered = _gather_chips([_shard2d(n, wl[n]).astype(BF16) for n in big], axes,
                             [_shard2d('ssd_conv_w', wl['ssd_conv_w'])])
    full = {n: wl[n] for n in SMALL}
    for n, a4 in zip(big + ['ssd_conv_w'], gathered, strict=True):
        full[n + "_t" if n in TRANSPOSED else n] = _unslot(n, a4)

    loss, grad_x, g = _local_step(x, p[0], loss_target, full)
    for n in TRANSPOSED:
        g[n] = g.pop(n + "_t")

    small_names = SMALL + ['ssd_conv_w']
    small_shapes = [(1, 1)] + [g[n].shape for n in small_names]
    small_pack = _pack_small([loss] + [g[n] for n in small_names])
    slotted = [_slot(n, g[n]) for n in big]
    swapped = _swap_halves(slotted, axes, small_pack)
    pair = [_pair_sum("pair_sum_" + n, a, b, idx, ax)
            for n, a, b, ax in zip(big, slotted, swapped[:-1], axes, strict=True)]
    small_sum = _sum_slots("sum_small", swapped[-1])
    recv = _scatter_halves([pb for pb, _ in pair])
    halves = [_sum_parts("sum_chips_" + n, own, r) for n, (_, own), r in zip(big, pair, recv, strict=True)]
    other_halves = _swap_sibling(halves)

    out_g, out_d, out_m, out_v = {}, {}, {}, {}
    for n, own, oth, ax in zip(big, halves, other_halves, axes, strict=True):
        res = _adamw_halves("adamw_" + n, _shard2d(n, wl[n]), _shard2d(n, ml[n]), _shard2d(n, vl[n]), own, oth, idx, ax)
        out_g[n], out_d[n], out_m[n], out_v[n] = (_unshard2d(n, r, wl[n].shape) for r in res)
    sm = _unpack_small(small_sum, small_shapes)
    loss_total = sm[0].reshape(())
    n = 'ssd_conv_w'
    conv_g = lax.dynamic_slice(sm[-1], (0, chip * (SSD_CONV_DIM // N_CHIPS)), (SSD_CONV, SSD_CONV_DIM // N_CHIPS))
    res = _adamw("adamw_" + n, _shard2d(n, wl[n]), _shard2d(n, ml[n]), _shard2d(n, vl[n]), [conv_g])
    out_g[n], out_d[n], out_m[n], out_v[n] = (r.reshape(wl[n].shape) for r in res)
    small_g = _pack_small(sm[1:-1])
    res = _adamw("adamw_small", _pack_small([wl[n] for n in SMALL]), _pack_small([ml[n] for n in SMALL]),
                 _pack_small([vl[n] for n in SMALL]), [small_g])
    shapes = [wl[n].shape for n in SMALL]
    for dst, packed in zip((out_g, out_d, out_m, out_v), res, strict=True):
        for n, val in zip(SMALL, _unpack_small(packed, shapes), strict=True):
            dst[n] = val

    return (loss_total, grad_x, *[out_g[n] for n in WEIGHTS], *[out_d[n] for n in WEIGHTS],
            *[out_m[n] for n in WEIGHTS], *[out_v[n] for n in WEIGHTS])
```

```python
import functools
import math

import jax
import jax.numpy as jnp
from jax import lax
from jax.experimental import pallas as pl
from jax.experimental.pallas import tpu as pltpu

F32 = jnp.float32
BF16 = jnp.bfloat16
MESH = pl.DeviceIdType.MESH

D_MODEL = 1024
PLE_DIM = 256
RMS_EPS = 1e-6
S5_WIDTH = 512
S5_GROUP = 16
S5_GROUPS = 32
S5_STATE = 64
S5_N = S5_GROUPS * S5_STATE
S5_LB = 512
S5_NJ = S5_N // S5_LB
S5_TB = 256
S5_LOG_TB = 8
SSD_WIDTH = 1536
SSD_HEADDIM = 64
SSD_HEADS = 24
SSD_GROUPS = 4
SSD_HPG = 6
SSD_STATE = 128
SSD_CONV = 4
SSD_CHUNK = 128
SSD_BC = 512
SSD_CONV_DIM = 2560
GROUP_W = SSD_WIDTH // SSD_GROUPS
N_CHIPS = 4
N_DEV = 8

OFF_XBC, OFF_U5, OFF_ZS, OFF_Z5, OFF_G5, OFF_GS, OFF_DT = 0, 2560, 3072, 4608, 5120, 6144, 7168
PROJ_W = 7680
IN_PROJ_DIM = 7192

ADAM_LR, ADAM_B1, ADAM_B2, ADAM_EPS, ADAM_WD, ADAM_STEP = 0.001, 0.9, 0.999, 1e-08, 0.01, 10

VMEM_LIMIT = 56 * 1024 * 1024


def _pick(n, cands):
    for c in cands:
        if n % c == 0:
            return c
    return n


ROW_BLOCK_BYTES = 8 * 1024 * 1024


def _row_tile(r, bytes_per_row):
    for t in (r, 4096, 2048, 1024, 512, 256, 128, 64, 32, 16, 8):
        if t <= r and r % t == 0 and t * bytes_per_row <= ROW_BLOCK_BYTES:
            return t
    return r


def _cparams(sem):
    return pltpu.CompilerParams(dimension_semantics=sem, vmem_limit_bytes=VMEM_LIMIT)


def _dg(a, b, ca, cb):
    return lax.dot_general(a.astype(BF16), b.astype(BF16), (((ca,), (cb,)), ((), ())), preferred_element_type=F32)


@jax.custom_vjp
def dot_nn(a, b):
    return _dg(a, b, 1, 0)


@jax.custom_vjp
def dot_nt(a, b):
    return _dg(a, b, 1, 1)


@jax.custom_vjp
def dot_tn(a, b):
    return _dg(a, b, 0, 0)


dot_nn.defvjp(lambda a, b: (_dg(a, b, 1, 0), (a, b)), lambda r, g: (_dg(g, r[1], 1, 1), _dg(r[0], g, 0, 0)))
dot_nt.defvjp(lambda a, b: (_dg(a, b, 1, 1), (a, b)), lambda r, g: (_dg(g, r[1], 1, 0), _dg(g, r[0], 0, 0)))
dot_tn.defvjp(lambda a, b: (_dg(a, b, 0, 0), (a, b)), lambda r, g: (_dg(r[1], g, 1, 1), _dg(r[0], g, 1, 0)))


MM_VMEM_BUDGET = 30 * 1024 * 1024


def _mm_tiles(m, n, k, sa, sb, so):
    best, best_key = None, None
    for tm in (1024, 512, 256, 128, 64, 32, 16, 8):
        if m % tm:
            continue
        for tn in (2048, 1536, 1280, 1024, 768, 640, 512, 384, 256, 128):
            if n % tn:
                continue
            for tk in (k, 2048, 1536, 1280, 1024, 768, 512, 256, 128):
                if k % tk or tk > max(k, 128) or (tk == k and k > 2048 and k % 128 == 0):
                    continue
                need = 2 * (tm * tk * sa + tk * tn * sb + tm * tn * so) + (tm * tn * 4 if tk < k else 0)
                if need > MM_VMEM_BUDGET:
                    continue
                key = (tm * tn * tk, tk)
                if best_key is None or key > best_key:
                    best, best_key = (tm, tn, tk), key
    assert best is not None, (m, n, k)
    return best


def _matmul(name, a, b, *, ta=False, tb=False, a_win=None, out_dtype=F32):
    a_off, a_w = a_win if a_win is not None else (0, a.shape[1])
    if ta:
        kdim, m = a.shape[0], a_w
    else:
        m, kdim = a.shape[0], a_w
    n = b.shape[0] if tb else b.shape[1]
    assert (b.shape[1] if tb else b.shape[0]) == kdim, (name, a.shape, b.shape)
    tm, tn, tk = _mm_tiles(m, n, kdim, a.dtype.itemsize, b.dtype.itemsize, jnp.dtype(out_dtype).itemsize)
    nk = kdim // tk
    if ta:
        assert a_off % tm == 0
        a_spec = pl.BlockSpec((tk, tm), lambda i, j, k: (k, i + a_off // tm))
    else:
        assert a_off % tk == 0
        a_spec = pl.BlockSpec((tm, tk), lambda i, j, k: (i, k + a_off // tk))
    if tb:
        b_spec = pl.BlockSpec((tn, tk), lambda i, j, k: (j, k))
    else:
        b_spec = pl.BlockSpec((tk, tn), lambda i, j, k: (k, j))
    ca, cb = (0 if ta else 1), (1 if tb else 0)

    def body(a_ref, b_ref, o_ref, *acc):
        if nk == 1:
            o_ref[...] = _dg(a_ref[...], b_ref[...], ca, cb).astype(o_ref.dtype)
            return
        (acc_ref,) = acc
        k = pl.program_id(2)

        @pl.when(k == 0)
        def _():
            acc_ref[...] = jnp.zeros_like(acc_ref)

        acc_ref[...] += _dg(a_ref[...], b_ref[...], ca, cb)

        @pl.when(k == nk - 1)
        def _():
            o_ref[...] = acc_ref[...].astype(o_ref.dtype)

    return pl.pallas_call(
        body, name=name, grid=(m // tm, n // tn, nk),
        in_specs=[a_spec, b_spec],
        out_specs=pl.BlockSpec((tm, tn), lambda i, j, k: (i, j)),
        out_shape=jax.ShapeDtypeStruct((m, n), out_dtype),
        scratch_shapes=[pltpu.VMEM((tm, tn), F32)] if nk > 1 else [],
        compiler_params=_cparams(("parallel", "parallel", "arbitrary")),
    )(a, b)


BAND = 128


def _band_matmul(name, kind, a, b, *, a_blk0=0, out_dtype=F32):
    n_rows = a.shape[0]
    blk = 2 * S5_LB
    tm = _pick(n_rows, (1024, 512, 256))
    if kind == "nn":
        grid = (n_rows // tm, S5_NJ)
        in_specs = [pl.BlockSpec((tm, BAND), lambda i, j: (i, a_blk0 + j)), pl.BlockSpec((BAND, blk), lambda i, j: (j, 0))]
        out_spec = pl.BlockSpec((tm, blk), lambda i, j: (i, j))
        out_shape = (n_rows, S5_NJ * blk)
        sem = ("parallel", "parallel")

        def body(a_ref, b_ref, o_ref):
            o_ref[...] = _dg(a_ref[...], b_ref[...], 1, 0).astype(o_ref.dtype)
    elif kind == "nt":
        grid = (n_rows // tm, S5_NJ)
        in_specs = [pl.BlockSpec((tm, blk), lambda i, j: (i, j)), pl.BlockSpec((BAND, blk), lambda i, j: (j, 0))]
        out_spec = pl.BlockSpec((tm, BAND), lambda i, j: (i, j))
        out_shape = (n_rows, S5_NJ * BAND)
        sem = ("parallel", "parallel")

        def body(a_ref, b_ref, o_ref):
            o_ref[...] = _dg(a_ref[...], b_ref[...], 1, 1).astype(o_ref.dtype)
    else:
        grid = (S5_NJ, n_rows // tm)
        in_specs = [pl.BlockSpec((tm, BAND), lambda j, k: (k, a_blk0 + j)), pl.BlockSpec((tm, blk), lambda j, k: (k, j))]
        out_spec = pl.BlockSpec((BAND, blk), lambda j, k: (j, 0))
        out_shape = (S5_NJ * BAND, blk)
        sem = ("parallel", "arbitrary")

        def body(a_ref, b_ref, o_ref):
            @pl.when(pl.program_id(1) == 0)
            def _():
                o_ref[...] = jnp.zeros_like(o_ref)

            o_ref[...] += _dg(a_ref[...], b_ref[...], 0, 0)

    return pl.pallas_call(
        body, name=name, grid=grid, in_specs=in_specs, out_specs=out_spec,
        out_shape=jax.ShapeDtypeStruct(out_shape, out_dtype), compiler_params=_cparams(sem),
    )(a, b)


def _rowwise(name, fn, n_rows, tr, row_ins, par_ins, row_outs, acc_outs=()):
    nr, npar, no, na = len(row_ins), len(par_ins), len(row_outs), len(acc_outs)
    in_specs = []
    for arr, off, w in row_ins:
        assert off % w == 0 and arr.shape[0] == n_rows, (name, arr.shape, off, w)
        in_specs.append(pl.BlockSpec((tr, w), functools.partial(lambda i, c: (i, c), c=off // w)))
    for arr, off, w in par_ins:
        assert off % w == 0
        in_specs.append(pl.BlockSpec((arr.shape[0], w), functools.partial(lambda i, c: (0, c), c=off // w)))
    out_specs = [pl.BlockSpec((tr, w), lambda i: (i, 0)) for w, _ in row_outs]
    out_specs += [pl.BlockSpec((r, w), lambda i: (0, 0)) for r, w in acc_outs]
    out_shape = [jax.ShapeDtypeStruct((n_rows, w), dt) for w, dt in row_outs]
    out_shape += [jax.ShapeDtypeStruct((r, w), F32) for r, w in acc_outs]

    def body(*refs):
        rows = [r[...] for r in refs[:nr]]
        pars = [r[...] for r in refs[nr:nr + npar]]
        o_refs = refs[nr + npar:nr + npar + no]
        a_refs = refs[nr + npar + no:]
        outs, accs = fn(rows, pars)
        for o_ref, o in zip(o_refs, outs, strict=True):
            o_ref[...] = o.astype(o_ref.dtype)
        if na:
            @pl.when(pl.program_id(0) == 0)
            def _():
                for a_ref in a_refs:
                    a_ref[...] = jnp.zeros_like(a_ref)

            for a_ref, a in zip(a_refs, accs, strict=True):
                a_ref[...] += jnp.broadcast_to(a, a_ref.shape)

    res = pl.pallas_call(
        body, name=name, grid=(n_rows // tr,),
        in_specs=in_specs, out_specs=out_specs, out_shape=out_shape,
        compiler_params=_cparams(("arbitrary",) if na else ("parallel",)),
    )(*[a for a, _, _ in row_ins], *[a for a, _, _ in par_ins])
    return res


def _rms(x, w):
    return x * lax.rsqrt(jnp.mean(x * x, axis=-1, keepdims=True) + RMS_EPS) * w


def _gated_norm(y, z, w):
    outs = []
    for g in range(SSD_GROUPS):
        sl = slice(g * GROUP_W, (g + 1) * GROUP_W)
        yg = y[:, sl] * jax.nn.silu(z[:, sl])
        outs.append(yg * lax.rsqrt(jnp.mean(yg * yg, axis=-1, keepdims=True) + RMS_EPS) * w[:, sl])
    return jnp.concatenate(outs, axis=-1)


def _s5_out(yc, u, z, t, d, bg):
    ge = jax.nn.gelu(yc + d * u)
    return ge * jax.nn.sigmoid(t + bg) * jax.nn.silu(z)


def _merge(g5, gs, m5, ms):
    return jax.nn.sigmoid(g5) * m5 + jax.nn.sigmoid(gs) * ms


def _head_loss(h1, pgl, pp, fw, tgt):
    h2 = h1 + jax.nn.sigmoid(pgl) * pp
    err = _rms(h2, fw) - tgt
    per_row = 0.5 * jnp.mean(err * err, axis=-1, keepdims=True)
    return jnp.sum(per_row, axis=0, keepdims=True)


def _s5_disc(a_re, a_im, log_step, b_re2, b_im2, expand):
    step = jnp.exp(log_step)
    mag = jnp.exp(a_re * step)
    lb_re = mag * jnp.cos(a_im * step)
    lb_im = mag * jnp.sin(a_im * step)
    den = a_re * a_re + a_im * a_im
    n_re = lb_re - 1.0
    f_re = (n_re * a_re + lb_im * a_im) / den
    f_im = (lb_im * a_re - n_re * a_im) / den
    hi = lax.Precision.HIGHEST
    fr = jnp.dot(expand, f_re, precision=hi, preferred_element_type=F32)
    fi = jnp.dot(expand, f_im, precision=hi, preferred_element_type=F32)
    return lb_re, lb_im, fr * b_re2 - fi * b_im2, fr * b_im2 + fi * b_re2


def _s5_params_fwd(a_re, a_im, log_step, b_re2, b_im2, expand):
    gp = a_re.shape

    def body(ar, ai, ls, br, bi, ex, pr_ref, pi_ref, bbr_ref, bbi_ref):
        lr, li, bbr, bbi = _s5_disc(ar[...], ai[...], ls[...], br[...], bi[...], ex[...])
        bbr_ref[...] = bbr
        bbi_ref[...] = bbi
        qr, qi = lr, li
        for k in range(S5_LOG_TB):
            pr_ref[k] = qr
            pi_ref[k] = qi
            qr, qi = qr * lr - qi * li, qr * li + qi * lr

    return pl.pallas_call(
        body, name="s5_params_fwd",
        out_shape=(jax.ShapeDtypeStruct((S5_LOG_TB,) + gp, F32), jax.ShapeDtypeStruct((S5_LOG_TB,) + gp, F32),
                   jax.ShapeDtypeStruct(b_re2.shape, F32), jax.ShapeDtypeStruct(b_re2.shape, F32)),
    )(a_re, a_im, log_step, b_re2, b_im2, expand)


def _s5_params_bwd(a_re, a_im, log_step, b_re2, b_im2, expand, d_lr, d_li, d_bbr, d_bbi):
    def body(ar, ai, ls, br, bi, ex, glr, gli, gbr, gbi, dar, dai, dls, dbr, dbi):
        _, vjp = jax.vjp(lambda *p: _s5_disc(*p, ex[...]), ar[...], ai[...], ls[...], br[...], bi[...])
        g = vjp((glr[...], gli[...], gbr[...], gbi[...]))
        for ref, val in zip((dar, dai, dls, dbr, dbi), g, strict=True):
            ref[...] = val

    return pl.pallas_call(
        body, name="s5_params_bwd",
        out_shape=tuple(jax.ShapeDtypeStruct(v.shape, F32) for v in (a_re, a_im, log_step, b_re2, b_im2)),
    )(a_re, a_im, log_step, b_re2, b_im2, expand, d_lr, d_li, d_bbr, d_bbi)


def _scan_block(x_ref, out_ref, lp, edge_pow, cr, ci, reverse, each=None):
    n_g = x_ref.shape[0] // 8
    sub = lax.broadcasted_iota(jnp.int32, (8, S5_LB), 0)
    steps = []
    for sh in (1, 2, 4):
        keep = (sub < 8 - sh) if reverse else (sub >= sh)
        steps.append((8 - sh if reverse else sh, jnp.where(keep, lp[sh - 1:sh, :S5_LB], 0.0),
                      jnp.where(keep, lp[sh - 1:sh, S5_LB:], 0.0)))
    e_r, e_i = edge_pow[:, :S5_LB], edge_pow[:, S5_LB:]
    for r in (range(n_g - 1, -1, -1) if reverse else range(n_g)):
        rows = slice(8 * r, 8 * r + 8)
        xr, xi = x_ref[rows, :S5_LB], x_ref[rows, S5_LB:]
        for by, a_r, a_i in steps:
            pr, pi = pltpu.roll(xr, by, 0), pltpu.roll(xi, by, 0)
            xr, xi = xr + a_r * pr - a_i * pi, xi + a_r * pi + a_i * pr
        xr = xr + e_r * cr - e_i * ci
        xi = xi + e_r * ci + e_i * cr
        out_ref[rows, :S5_LB] = xr
        out_ref[rows, S5_LB:] = xi
        if each is not None:
            each(r, xr, xi)
        cr, ci = (xr[0:1, :], xi[0:1, :]) if reverse else (xr[7:8, :], xi[7:8, :])
    return cr, ci


def _s5_scan_fwd(bu, lam_pow, n_seq, seq_len):
    n_t = seq_len // S5_TB
    blk = 2 * S5_LB

    def body(bu_ref, lp_ref, s_ref, cr, ci):
        @pl.when(pl.program_id(2) == 0)
        def _():
            cr[...] = jnp.zeros_like(cr)
            ci[...] = jnp.zeros_like(ci)

        lp = lp_ref[...]
        cr[...], ci[...] = _scan_block(bu_ref, s_ref, lp, lp, cr[...], ci[...], False)

    return pl.pallas_call(
        body, name="s5_scan_fwd", grid=(S5_NJ, n_seq, n_t),
        in_specs=[pl.BlockSpec((S5_TB, blk), lambda j, b, t: (b * n_t + t, j)),
                  pl.BlockSpec((S5_LOG_TB, blk), lambda j, b, t: (0, j))],
        out_specs=pl.BlockSpec((S5_TB, blk), lambda j, b, t: (b * n_t + t, j)),
        out_shape=jax.ShapeDtypeStruct(bu.shape, F32),
        scratch_shapes=[pltpu.VMEM((1, S5_LB), F32), pltpu.VMEM((1, S5_LB), F32)],
        compiler_params=_cparams(("parallel", "parallel", "arbitrary")),
    )(bu, lam_pow)


def _s5_scan_bwd(ds, s, lam_pow_conj, n_seq, seq_len):
    n_t = seq_len // S5_TB
    blk = 2 * S5_LB
    halo_per_blk = S5_TB // 8

    def rows(j, b, t):
        return (b * n_t + (n_t - 1 - t), j)

    def halo(j, b, t):
        return (jnp.maximum((b * n_t + (n_t - 1 - t)) * halo_per_blk - 1, 0), j)

    def body(ds_ref, s_ref, h_ref, lp_ref, g_ref, dl_ref, cr, ci, buf):
        b, t = pl.program_id(1), pl.program_id(2)

        @pl.when(t == 0)
        def _():
            cr[...] = jnp.zeros_like(cr)
            ci[...] = jnp.zeros_like(ci)

        @pl.when((b == 0) & (t == 0))
        def _():
            dl_ref[...] = jnp.zeros_like(dl_ref)

        first_blk = t == n_t - 1
        sub = lax.broadcasted_iota(jnp.int32, (8, S5_LB), 0)
        acc = [jnp.zeros((8, S5_LB), F32), jnp.zeros((8, S5_LB), F32)]

        def each(r, gr, gi):
            rows = slice(8 * r, 8 * r + 8)
            if r == 0:
                before_r = jnp.where(first_blk, 0.0, h_ref[7:8, :S5_LB])
                before_i = jnp.where(first_blk, 0.0, h_ref[7:8, S5_LB:])
            else:
                before_r, before_i = s_ref[8 * r - 1:8 * r, :S5_LB], s_ref[8 * r - 1:8 * r, S5_LB:]
            sp_r = jnp.where(sub == 0, before_r, pltpu.roll(s_ref[rows, :S5_LB], 1, 0))
            sp_i = jnp.where(sub == 0, before_i, pltpu.roll(s_ref[rows, S5_LB:], 1, 0))
            acc[0] = acc[0] + gr * sp_r + gi * sp_i
            acc[1] = acc[1] + gi * sp_r - gr * sp_i

        lp = lp_ref[...]
        edge_pow = jnp.concatenate([lp[7 - i:8 - i, :] for i in range(8)], axis=0)
        cr[...], ci[...] = _scan_block(ds_ref, buf, lp, edge_pow, cr[...], ci[...], True, each)
        g_ref[...] = buf[...].astype(g_ref.dtype)
        dl_ref[:, :S5_LB] += jnp.sum(acc[0], axis=0, keepdims=True)
        dl_ref[:, S5_LB:] += jnp.sum(acc[1], axis=0, keepdims=True)

    return pl.pallas_call(
        body, name="s5_scan_bwd", grid=(S5_NJ, n_seq, n_t),
        in_specs=[pl.BlockSpec((S5_TB, blk), rows), pl.BlockSpec((S5_TB, blk), rows),
                  pl.BlockSpec((8, blk), halo), pl.BlockSpec((S5_LOG_TB, blk), lambda j, b, t: (0, j))],
        out_specs=[pl.BlockSpec((S5_TB, blk), rows), pl.BlockSpec((1, blk), lambda j, b, t: (0, j))],
        out_shape=[jax.ShapeDtypeStruct(ds.shape, BF16), jax.ShapeDtypeStruct((1, ds.shape[1]), F32)],
        scratch_shapes=[pltpu.VMEM((1, S5_LB), F32), pltpu.VMEM((1, S5_LB), F32), pltpu.VMEM((S5_TB, blk), F32)],
        compiler_params=_cparams(("parallel", "arbitrary", "arbitrary")),
    )(ds, s, s, lam_pow_conj)


CONV_TR = 256
CONV_CW = 512


def _shift_down(x, halo, k):
    if k == 0:
        return x
    row8 = lax.broadcasted_iota(jnp.int32, halo.shape, 0)
    rolled = pltpu.roll(x, k, 0)
    top = jnp.where(row8 < k, pltpu.roll(halo, k, 0), rolled[:8])
    if x.shape[0] == 8:
        return top
    return jnp.concatenate([top, rolled[8:]], axis=0)


def _shift_up(x, halo, k):
    if k == 0:
        return x
    n = x.shape[0]
    row8 = lax.broadcasted_iota(jnp.int32, halo.shape, 0)
    rolled = pltpu.roll(x, n - k, 0)
    bot = jnp.where(row8 >= 8 - k, pltpu.roll(halo, 8 - k, 0), rolled[n - 8:])
    if n == 8:
        return bot
    return jnp.concatenate([rolled[:n - 8], bot], axis=0)


def _conv_pre(x, halo, w, b):
    acc = b + w[SSD_CONV - 1:SSD_CONV, :] * x
    for k in range(SSD_CONV - 1):
        acc = acc + w[k:k + 1, :] * _shift_down(x, halo, SSD_CONV - 1 - k)
    return acc


def _conv_specs(seq_len, col_off):
    lt = seq_len // CONV_TR
    cb = col_off // CONV_CW
    cur = pl.BlockSpec((CONV_TR, CONV_CW), lambda j, i: (i, j + cb))
    prev = pl.BlockSpec((8, CONV_CW), lambda j, i: (jnp.maximum(i * (CONV_TR // 8) - 1, 0), j + cb))
    return lt, cur, prev


def _conv_fwd(proj, conv_w, conv_b, n_rows, seq_len):
    lt, cur, prev = _conv_specs(seq_len, OFF_XBC)

    def body(x_ref, h_ref, w_ref, b_ref, o_ref):
        halo = jnp.where(pl.program_id(1) % lt == 0, 0.0, h_ref[...])
        o_ref[...] = jax.nn.silu(_conv_pre(x_ref[...], halo, w_ref[...], b_ref[...]))

    return pl.pallas_call(
        body, name="ssd_conv_fwd", grid=(SSD_CONV_DIM // CONV_CW, n_rows // CONV_TR),
        in_specs=[cur, prev, pl.BlockSpec((SSD_CONV, CONV_CW), lambda j, i: (0, j)),
                  pl.BlockSpec((1, CONV_CW), lambda j, i: (0, j))],
        out_specs=pl.BlockSpec((CONV_TR, CONV_CW), lambda j, i: (i, j)),
        out_shape=jax.ShapeDtypeStruct((n_rows, SSD_CONV_DIM), F32),
        compiler_params=_cparams(("parallel", "parallel")),
    )(proj, proj, conv_w, conv_b)


def _conv_bwd(name, proj, d_act, conv_w, conv_b, n_rows, seq_len, col_off):
    width = d_act.shape[1]
    lt, cur, prev = _conv_specs(seq_len, OFF_XBC + col_off)
    n_blk = n_rows // CONV_TR
    cb = (OFF_XBC + col_off) // CONV_CW
    pb = col_off // CONV_CW
    nxt = pl.BlockSpec((8, CONV_CW), lambda j, i: (jnp.minimum((i + 1) * (CONV_TR // 8), n_rows // 8 - 1), j + cb))
    d_cur = pl.BlockSpec((CONV_TR, CONV_CW), lambda j, i: (i, j))
    d_nxt = pl.BlockSpec((8, CONV_CW), lambda j, i: (jnp.minimum((i + 1) * (CONV_TR // 8), n_rows // 8 - 1), j))

    def dsilu(pre):
        sg = jax.nn.sigmoid(pre)
        return sg * (1.0 + pre * (1.0 - sg))

    def body(x_ref, hp_ref, hn_ref, d_ref, dn_ref, w_ref, b_ref, dx_ref, dw_ref, db_ref):
        i = pl.program_id(1)
        x, w, b = x_ref[...], w_ref[...], b_ref[...]
        halo_p = jnp.where(i % lt == 0, 0.0, hp_ref[...])
        at_end = i % lt == lt - 1
        dpre = d_ref[...] * dsilu(_conv_pre(x, halo_p, w, b))
        pre_n = _conv_pre(hn_ref[...], x[CONV_TR - 8:, :], w, b)
        dpre_n = jnp.where(at_end, 0.0, dn_ref[...] * dsilu(pre_n))
        dx = w[SSD_CONV - 1:SSD_CONV, :] * dpre
        for k in range(SSD_CONV - 1):
            dx = dx + w[k:k + 1, :] * _shift_up(dpre, dpre_n, SSD_CONV - 1 - k)
        dx_ref[...] = dx.astype(dx_ref.dtype)

        @pl.when(i == 0)
        def _():
            dw_ref[...] = jnp.zeros_like(dw_ref)
            db_ref[...] = jnp.zeros_like(db_ref)

        for k in range(SSD_CONV):
            xs = _shift_down(x, halo_p, SSD_CONV - 1 - k)
            dw_ref[k:k + 1, :] += jnp.sum(dpre * xs, axis=0, keepdims=True)
        db_ref[...] += jnp.sum(dpre, axis=0, keepdims=True)

    return pl.pallas_call(
        body, name=name, grid=(width // CONV_CW, n_blk),
        in_specs=[cur, prev, nxt, d_cur, d_nxt,
                  pl.BlockSpec((SSD_CONV, CONV_CW), lambda j, i: (0, j + pb)),
                  pl.BlockSpec((1, CONV_CW), lambda j, i: (0, j + pb))],
        out_specs=[pl.BlockSpec((CONV_TR, CONV_CW), lambda j, i: (i, j)),
                   pl.BlockSpec((SSD_CONV, CONV_CW), lambda j, i: (0, j)),
                   pl.BlockSpec((1, CONV_CW), lambda j, i: (0, j))],
        out_shape=[jax.ShapeDtypeStruct((n_rows, width), BF16),
                   jax.ShapeDtypeStruct((SSD_CONV, width), F32), jax.ShapeDtypeStruct((1, width), F32)],
        compiler_params=_cparams(("parallel", "arbitrary")),
    )(proj, proj, proj, d_act, d_act, conv_w, conv_b)


def _split3(x):
    hi = x.astype(BF16)
    r = x - hi.astype(F32)
    mid = r.astype(BF16)
    return hi, mid, (r - mid.astype(F32)).astype(BF16)


def _sel_dot(a, b, a_is_sel):
    dn = (((1,), (0,)), ((), ()))
    if a_is_sel:
        return sum(lax.dot_general(a, t, dn, preferred_element_type=F32) for t in _split3(b))
    return sum(lax.dot_general(t, b, dn, preferred_element_type=F32) for t in _split3(a))


@jax.custom_vjp
def sel_left(sel, sel_t, x):
    return _sel_dot(sel, x, True)


@jax.custom_vjp
def sel_right(x, sel, sel_t):
    return _sel_dot(x, sel, False)


sel_left.defvjp(lambda s, st, x: (_sel_dot(s, x, True), (s, st)),
                lambda r, g: (jnp.zeros_like(r[0]), jnp.zeros_like(r[1]), _sel_dot(r[1], g, True)))
sel_right.defvjp(lambda x, s, st: (_sel_dot(x, s, False), (s, st)),
                 lambda r, g: (_sel_dot(g, r[1], False), jnp.zeros_like(r[0]), jnp.zeros_like(r[1])))


def _ssd_chunk(xs, bm, cm, dtr, st, dtb, alog, dsk, k):
    dt = jax.nn.softplus(dtr + dtb)
    acum = sel_left(k["tri"], k["tri_t"], dt * (-jnp.exp(alog)))
    dt_e = sel_right(dt, k["spread"], k["spread_t"])
    ac_e = sel_right(acum, k["spread"], k["spread_t"])
    al_e = ac_e[SSD_CHUNK - 1:SSD_CHUNK, :]
    dsk_e = sel_right(jnp.broadcast_to(dsk, (8, 128)), k["spread"], k["spread_t"])[0:1, :]
    xdt = xs * dt_e
    acum_t = acum.T
    scores = dot_nt(cm, bm)
    y = dot_nn(cm, st) * jnp.exp(ac_e) + xs * dsk_e
    for j in range(SSD_HPG):
        lmat = jnp.exp(jnp.where(k["causal"], acum[:, j:j + 1] - acum_t[j:j + 1, :], -jnp.inf))
        y = y + dot_nn(scores * lmat, jnp.where(k["head"] == j, xdt, 0.0))
    new = st * jnp.exp(al_e) + dot_tn(bm, xdt * jnp.exp(al_e - ac_e))
    return y, new


def _ssd_consts():
    r = lax.broadcasted_iota(jnp.int32, (SSD_CHUNK, SSD_CHUNK), 0)
    c = lax.broadcasted_iota(jnp.int32, (SSD_CHUNK, SSD_CHUNK), 1)
    hd = jnp.int32(SSD_HEADDIM)
    sr = lax.broadcasted_iota(jnp.int32, (128, GROUP_W), 0)
    sc = lax.div(lax.broadcasted_iota(jnp.int32, (128, GROUP_W), 1), hd)
    tr = lax.div(lax.broadcasted_iota(jnp.int32, (GROUP_W, 128), 0), hd)
    tc = lax.broadcasted_iota(jnp.int32, (GROUP_W, 128), 1)
    return {"tri": (r >= c).astype(BF16), "tri_t": (c >= r).astype(BF16), "causal": r >= c,
            "spread": (sr == sc).astype(BF16), "spread_t": (tr == tc).astype(BF16),
            "head": lax.div(lax.broadcasted_iota(jnp.int32, (SSD_CHUNK, GROUP_W), 1), hd)}


def _ssd_specs(n_c, reverse):
    def cidx(c):
        return n_c - 1 - c if reverse else c

    xs = pl.BlockSpec((SSD_CHUNK, GROUP_W), lambda g, b, c: (b * n_c + cidx(c), g))
    bm = pl.BlockSpec((SSD_CHUNK, SSD_STATE), lambda g, b, c: (b * n_c + cidx(c), SSD_WIDTH // SSD_STATE + g))
    cm = pl.BlockSpec((SSD_CHUNK, SSD_STATE), lambda g, b, c: (b * n_c + cidx(c), (SSD_WIDTH + SSD_BC) // SSD_STATE + g))
    dt = pl.BlockSpec((SSD_CHUNK, 128), lambda g, b, c: (b * n_c + cidx(c), OFF_DT // 128 + g))
    par = pl.BlockSpec((1, 128), lambda g, b, c: (0, g))
    st = pl.BlockSpec((1, 1, 1, SSD_STATE, GROUP_W), lambda g, b, c: (b, cidx(c), g, 0, 0))
    return xs, bm, cm, dt, par, st


def _ssd_fwd(xbc_act, proj, dtb, alog, dsk, n_seq, seq_len):
    n_c = seq_len // SSD_CHUNK
    xs_s, bm_s, cm_s, dt_s, par_s, st_s = _ssd_specs(n_c, False)

    def body(xs_ref, bm_ref, cm_ref, dt_ref, dtb_ref, al_ref, dk_ref, y_ref, st_ref, state):
        @pl.when(pl.program_id(2) == 0)
        def _():
            state[...] = jnp.zeros_like(state)

        prev = state[...]
        st_ref[0, 0, 0] = prev
        y, new = _ssd_chunk(xs_ref[...], bm_ref[...], cm_ref[...], dt_ref[...], prev,
                            dtb_ref[...], al_ref[...], dk_ref[...], _ssd_consts())
        y_ref[...] = y
        state[...] = new

    return pl.pallas_call(
        body, name="ssd_fwd", grid=(SSD_GROUPS, n_seq, n_c),
        in_specs=[xs_s, bm_s, cm_s, dt_s, par_s, par_s, par_s],
        out_specs=[pl.BlockSpec((SSD_CHUNK, GROUP_W), lambda g, b, c: (b * n_c + c, g)), st_s],
        out_shape=[jax.ShapeDtypeStruct((n_seq * seq_len, SSD_WIDTH), F32),
                   jax.ShapeDtypeStruct((n_seq, n_c, SSD_GROUPS, SSD_STATE, GROUP_W), F32)],
        scratch_shapes=[pltpu.VMEM((SSD_STATE, GROUP_W), F32)],
        compiler_params=_cparams(("parallel", "parallel", "arbitrary")),
    )(xbc_act, xbc_act, xbc_act, proj, dtb, alog, dsk)


def _ssd_bwd(xbc_act, proj, states, dy, dtb, alog, dsk, n_seq, seq_len):
    n_c = seq_len // SSD_CHUNK
    n_rows = n_seq * seq_len
    xs_s, bm_s, cm_s, dt_s, par_s, st_s = _ssd_specs(n_c, True)

    def rows(w):
        return pl.BlockSpec((SSD_CHUNK, w), lambda g, b, c: (b * n_c + (n_c - 1 - c), g))

    def body(xs_ref, bm_ref, cm_ref, dt_ref, st_ref, dy_ref, dtb_ref, al_ref, dk_ref,
             dxs_ref, dbm_ref, dcm_ref, ddt_ref, ddtb_ref, dal_ref, ddk_ref, dstate):
        b, c = pl.program_id(1), pl.program_id(2)

        @pl.when(c == 0)
        def _():
            dstate[...] = jnp.zeros_like(dstate)

        @pl.when((b == 0) & (c == 0))
        def _():
            ddtb_ref[...] = jnp.zeros_like(ddtb_ref)
            dal_ref[...] = jnp.zeros_like(dal_ref)
            ddk_ref[...] = jnp.zeros_like(ddk_ref)

        consts = _ssd_consts()
        _, vjp = jax.vjp(
            lambda xs, bm, cm, dtr, prev, dtb, alog, dsk: _ssd_chunk(xs, bm, cm, dtr, prev, dtb, alog, dsk, consts),
            xs_ref[...], bm_ref[...], cm_ref[...], dt_ref[...], st_ref[0, 0, 0], dtb_ref[...], al_ref[...], dk_ref[...])
        dxs, dbm, dcm, ddtr, dprev, ddtb, dal, ddk = vjp((dy_ref[...], dstate[...]))
        dxs_ref[...] = dxs
        dbm_ref[...] = dbm
        dcm_ref[...] = dcm
        ddt_ref[...] = ddtr.astype(ddt_ref.dtype)
        ddtb_ref[...] += ddtb
        dal_ref[...] += dal
        ddk_ref[...] += ddk
        dstate[...] = dprev

    acc = pl.BlockSpec((1, 128), lambda g, b, c: (0, g))
    return pl.pallas_call(
        body, name="ssd_bwd", grid=(SSD_GROUPS, n_seq, n_c),
        in_specs=[xs_s, bm_s, cm_s, dt_s, st_s, rows(GROUP_W), par_s, par_s, par_s],
        out_specs=[rows(GROUP_W), rows(SSD_STATE), rows(SSD_STATE), rows(128), acc, acc, acc],
        out_shape=[jax.ShapeDtypeStruct((n_rows, SSD_WIDTH), F32), jax.ShapeDtypeStruct((n_rows, SSD_BC), F32),
                   jax.ShapeDtypeStruct((n_rows, SSD_BC), F32), jax.ShapeDtypeStruct((n_rows, 4 * 128), BF16),
                   jax.ShapeDtypeStruct((1, 512), F32), jax.ShapeDtypeStruct((1, 512), F32),
                   jax.ShapeDtypeStruct((1, 512), F32)],
        scratch_shapes=[pltpu.VMEM((SSD_STATE, GROUP_W), F32)],
        compiler_params=_cparams(("parallel", "arbitrary", "arbitrary")),
    )(xbc_act, xbc_act, xbc_act, proj, states, dy, dtb, alog, dsk)


def _pad_heads(v):
    return jnp.pad(v.reshape(SSD_GROUPS, SSD_HPG), ((0, 0), (0, 128 - SSD_HPG))).reshape(1, SSD_GROUPS * 128)


def _unpad_heads(v):
    return v.reshape(SSD_GROUPS, 128)[:, :SSD_HPG].reshape(1, SSD_HEADS)


def _state_cols(v):
    re, im = v
    lead = re.shape[:-1]
    re = re.reshape(lead + (S5_NJ, 1, S5_LB))
    im = im.reshape(lead + (S5_NJ, 1, S5_LB))
    return jnp.concatenate([re, im], axis=-2).reshape(lead + (2 * S5_N,))


def _state_uncols(v):
    lead = v.shape[:-1]
    v = v.reshape(lead + (S5_NJ, 2, S5_LB))
    return v[..., 0, :].reshape(lead + (S5_N,)), v[..., 1, :].reshape(lead + (S5_N,))


GROUPS_PER_BAND = BAND // S5_GROUP


def _band(w2_re, w2_im):
    gh = S5_GROUPS * S5_GROUP
    rg = ((jnp.arange(gh) // S5_GROUP) % GROUPS_PER_BAND)[:, None, None]
    cg = jnp.arange(GROUPS_PER_BAND)[None, :, None]
    parts = [jnp.where(rg == cg, v[:, None, :], 0.0).reshape(gh, S5_LB) for v in (w2_re, w2_im)]
    return jnp.concatenate(parts, axis=1)


def _band_take(wb):
    gh = S5_GROUPS * S5_GROUP
    w4 = wb.reshape(gh, 2, GROUPS_PER_BAND, S5_STATE)
    sel = w4[jnp.arange(gh), :, (jnp.arange(gh) // S5_GROUP) % GROUPS_PER_BAND, :]
    return sel[:, 0, :], sel[:, 1, :]


W_IN_SHARD = IN_PROJ_DIM // N_CHIPS
W_IN_SEGS = ((0, 512, OFF_U5), (512, 1024, OFF_Z5), (1024, 2560, OFF_ZS), (2560, 5120, OFF_XBC), (5144, 7192, OFF_G5))
DT_ROWS = (5120, 5144)


def _pad_w_in_t(a4):
    def rows(lo, hi):
        out = []
        for j in range(N_CHIPS):
            s, e = max(lo, j * W_IN_SHARD), min(hi, (j + 1) * W_IN_SHARD)
            if s < e:
                out.append(a4[j, s - j * W_IN_SHARD:e - j * W_IN_SHARD])
        return out

    dt = jnp.concatenate(rows(*DT_ROWS), axis=0)
    dtp = jnp.pad(dt.reshape(SSD_GROUPS, SSD_HPG, -1), ((0, 0), (0, 128 - SSD_HPG), (0, 0))).reshape(512, -1)
    by_off = sorted(W_IN_SEGS, key=lambda seg: seg[2])
    return jnp.concatenate([piece for lo, hi, _ in by_off for piece in rows(lo, hi)] + [dtp], axis=0)


def _unpad_w_in_t(wp):
    dt = wp[OFF_DT:].reshape(SSD_GROUPS, 128, -1)[:, :SSD_HPG].reshape(SSD_HEADS, -1)
    slots = []
    for j in range(N_CHIPS):
        lo, hi = j * W_IN_SHARD, (j + 1) * W_IN_SHARD
        pieces = []
        for a, b, off in sorted(W_IN_SEGS + (DT_ROWS + (None,),)):
            s, e = max(lo, a), min(hi, b)
            if s < e:
                pieces.append(dt[s - a:e - a] if off is None else wp[off + s - a:off + e - a])
        slots.append(jnp.concatenate(pieces, axis=0))
    return jnp.stack(slots, axis=0)


def _local_step(x, p, tgt, w):
    n_seq, seq_len, _ = x.shape
    n_rows = n_seq * seq_len
    tr = 256
    x2 = x.reshape(n_rows, D_MODEL)
    p2 = p.reshape(n_rows, PLE_DIM)
    t2 = tgt.reshape(n_rows, D_MODEL)
    row = functools.partial(_rowwise, n_rows=n_rows, tr=tr)

    w_pad_t = _pad_w_in_t(w["w_in_t"])
    norm_w = w["norm_w"].reshape(1, D_MODEL)
    ple_norm_w = w["ple_norm_w"].reshape(1, D_MODEL)
    final_w = w["final_norm_w"].reshape(1, D_MODEL)
    s5_d = w["s5_d"].reshape(1, S5_WIDTH)
    b_glu = w["s5_b_glu"].reshape(1, S5_WIDTH)
    conv_w = w["ssd_conv_w"].reshape(SSD_CONV, SSD_CONV_DIM)
    conv_b = w["ssd_conv_b"].reshape(1, SSD_CONV_DIM)
    ssd_norm_w = w["ssd_norm_w"].reshape(1, SSD_WIDTH)
    dtb, alog, dsk = (_pad_heads(w[k].reshape(1, SSD_HEADS)) for k in ("ssd_dt_bias", "ssd_a_log", "ssd_d"))

    gh = S5_GROUPS * S5_GROUP
    a_re = w["s5_a_re"].reshape(S5_GROUPS, S5_STATE)
    a_im = w["s5_a_im"].reshape(S5_GROUPS, S5_STATE)
    log_step = w["s5_log_step"].reshape(S5_GROUPS, 1)
    b_re2 = jnp.transpose(w["s5_b_re"].reshape(S5_GROUPS, S5_STATE, S5_GROUP), (0, 2, 1)).reshape(gh, S5_STATE)
    b_im2 = jnp.transpose(w["s5_b_im"].reshape(S5_GROUPS, S5_STATE, S5_GROUP), (0, 2, 1)).reshape(gh, S5_STATE)
    expand = (jnp.arange(gh)[:, None] // S5_GROUP == jnp.arange(S5_GROUPS)[None, :]).astype(F32)
    pow_re, pow_im, bb_re2, bb_im2 = _s5_params_fwd(a_re, a_im, log_step, b_re2, b_im2, expand)
    lam_pow = _state_cols((pow_re.reshape(S5_LOG_TB, S5_N), pow_im.reshape(S5_LOG_TB, S5_N)))
    lam_pow_conj = _state_cols((pow_re.reshape(S5_LOG_TB, S5_N), -pow_im.reshape(S5_LOG_TB, S5_N)))
    bb_band = _band(bb_re2, bb_im2).astype(BF16)
    c_band = _band(w["s5_c_re"].reshape(gh, S5_STATE), -w["s5_c_im"].reshape(gh, S5_STATE)).astype(BF16)

    (hn,) = row("rms_in", lambda r, q: ([_rms(r[0], q[0])], []), row_ins=[(x2, 0, D_MODEL)],
                par_ins=[(norm_w, 0, D_MODEL)], row_outs=[(D_MODEL, BF16)])
    proj = _matmul("mm_proj", hn, w_pad_t, tb=True)
    bu = _band_matmul("mm_s5_bu", "nn", proj, bb_band, a_blk0=OFF_U5 // BAND)
    s = _s5_scan_fwd(bu, lam_pow, n_seq, seq_len)
    yc = _band_matmul("mm_s5_y", "nt", s, c_band)
    (ge,) = row("s5_gelu", lambda r, q: ([jax.nn.gelu(r[0] + q[0] * r[1])], []),
                row_ins=[(yc, 0, S5_WIDTH), (proj, OFF_U5, S5_WIDTH)], par_ins=[(s5_d, 0, S5_WIDTH)],
                row_outs=[(S5_WIDTH, BF16)])
    tg = _matmul("mm_s5_glu", ge, w["s5_w_glu"])
    s5_rows = [(yc, 0, S5_WIDTH), (proj, OFF_U5, S5_WIDTH), (proj, OFF_Z5, S5_WIDTH), (tg, 0, S5_WIDTH)]
    s5_pars = [(s5_d, 0, S5_WIDTH), (b_glu, 0, S5_WIDTH)]
    (y5,) = row("s5_out", lambda r, q: ([_s5_out(*r, *q)], []), row_ins=s5_rows, par_ins=s5_pars,
                row_outs=[(S5_WIDTH, BF16)])

    xbc_act = _conv_fwd(proj, conv_w, conv_b, n_rows, seq_len)
    y_ssd, states = _ssd_fwd(xbc_act, proj, dtb, alog, dsk, n_seq, seq_len)
    gn_rows = [(y_ssd, 0, SSD_WIDTH), (proj, OFF_ZS, SSD_WIDTH)]
    (yss,) = row("ssd_gate", lambda r, q: ([_gated_norm(r[0], r[1], q[0])], []), row_ins=gn_rows,
                 par_ins=[(ssd_norm_w, 0, SSD_WIDTH)], row_outs=[(SSD_WIDTH, BF16)])

    m5 = _matmul("mm_br_s5", y5, w["w_br_s5"])
    ms = _matmul("mm_br_ssd", yss, w["w_br_ssd"])
    mg_rows = [(proj, OFF_G5, D_MODEL), (proj, OFF_GS, D_MODEL), (m5, 0, D_MODEL), (ms, 0, D_MODEL)]
    (merged,) = row("merge", lambda r, q: ([_merge(*r)], []), row_ins=mg_rows, par_ins=[], row_outs=[(D_MODEL, BF16)])
    o = _matmul("mm_out", merged, w["w_out"])

    def resid_fn(r, q):
        h1 = r[0] + r[1]
        return [h1, _rms(h1, q[0])], []

    h1, hp = row("resid_ple_norm", resid_fn, row_ins=[(x2, 0, D_MODEL), (o, 0, D_MODEL)],
                 par_ins=[(ple_norm_w, 0, D_MODEL)], row_outs=[(D_MODEL, F32), (D_MODEL, BF16)])
    pgl = _matmul("mm_ple_gate", hp, w["w_ple_gate"])
    pp = _matmul("mm_ple_proj", p2, w["w_ple_proj"])

    def head_fn(r, q):
        h1_, pgl_, pp_, tgt_ = r
        loss, vjp = jax.vjp(lambda a, b, c, f: _head_loss(a, b, c, f, tgt_), h1_, pgl_, pp_, q[0])
        dh1_, dpgl_, dpp_, dfw_ = vjp(jnp.ones_like(loss))
        return [dh1_, dpgl_, dpp_], [loss, dfw_]

    dh2, dpgl, dpp, loss_acc, d_final_w = row(
        "head", head_fn, row_ins=[(h1, 0, D_MODEL), (pgl, 0, D_MODEL), (pp, 0, D_MODEL), (t2, 0, D_MODEL)],
        par_ins=[(final_w, 0, D_MODEL)], row_outs=[(D_MODEL, F32), (D_MODEL, BF16), (D_MODEL, BF16)],
        acc_outs=[(1, 128), (1, D_MODEL)])
    loss = loss_acc[0, 0]

    g = {}
    g["final_norm_w"] = d_final_w
    g["w_ple_gate"] = _matmul("mm_d_w_ple_gate", hp, dpgl, ta=True)
    g["w_ple_proj"] = _matmul("mm_d_w_ple_proj", p2, dpp, ta=True)
    dhp = _matmul("mm_d_hp", dpgl, w["w_ple_gate"], tb=True)

    def ple_norm_bwd(r, q):
        h1_, dhp_, dh2_ = r
        _, vjp = jax.vjp(_rms, h1_, q[0])
        dh, dw = vjp(dhp_)
        dh = dh + dh2_
        return [dh, dh], [dw]

    dh1, dh1_b, g["ple_norm_w"] = row(
        "ple_norm_bwd", ple_norm_bwd, row_ins=[(h1, 0, D_MODEL), (dhp, 0, D_MODEL), (dh2, 0, D_MODEL)],
        par_ins=[(ple_norm_w, 0, D_MODEL)], row_outs=[(D_MODEL, F32), (D_MODEL, BF16)], acc_outs=[(1, D_MODEL)])
    g["w_out"] = _matmul("mm_d_w_out", merged, dh1_b, ta=True)
    dmerged = _matmul("mm_d_merged", dh1_b, w["w_out"], tb=True)

    def merge_bwd(r, q):
        _, vjp = jax.vjp(_merge, *r[:4])
        return list(vjp(r[4])), []

    dg5, dgs, dm5, dms = row("merge_bwd", merge_bwd, row_ins=mg_rows + [(dmerged, 0, D_MODEL)], par_ins=[],
                             row_outs=[(D_MODEL, BF16)] * 4)
    g["w_br_s5"] = _matmul("mm_d_w_br_s5", y5, dm5, ta=True)
    g["w_br_ssd"] = _matmul("mm_d_w_br_ssd", yss, dms, ta=True)
    dy5 = _matmul("mm_d_y5", dm5, w["w_br_s5"], tb=True)
    dyss = _matmul("mm_d_yss", dms, w["w_br_ssd"], tb=True)

    def s5_out_bwd_a(r, q):
        yc_, u_, z_, t_, dy_ = r
        d_, bg_ = q
        ge_ = jax.nn.gelu(yc_ + d_ * u_)
        _, vjp = jax.vjp(lambda a, z, t, b: a * jax.nn.sigmoid(t + b) * jax.nn.silu(z), ge_, z_, t_, bg_)
        dge, dz, dt_, dbg = vjp(dy_)
        return [dge, dz, dt_], [dbg]

    dge_a, dz5, dtg, g["s5_b_glu"] = row(
        "s5_out_bwd_a", s5_out_bwd_a, row_ins=s5_rows + [(dy5, 0, S5_WIDTH)], par_ins=s5_pars,
        row_outs=[(S5_WIDTH, F32), (S5_WIDTH, BF16), (S5_WIDTH, BF16)], acc_outs=[(1, S5_WIDTH)])
    g["s5_w_glu"] = _matmul("mm_d_w_glu", ge, dtg, ta=True)
    dge_b = _matmul("mm_d_ge", dtg, w["s5_w_glu"], tb=True)

    def s5_out_bwd_b(r, q):
        yc_, u_, da_, db_ = r
        _, vjp = jax.vjp(lambda yc, u, d: jax.nn.gelu(yc + d * u), yc_, u_, q[0])
        dyc_, du_, dd_ = vjp(da_ + db_)
        return [dyc_, du_], [dd_]

    dyc, du5_a, g["s5_d"] = row(
        "s5_out_bwd_b", s5_out_bwd_b,
        row_ins=[(yc, 0, S5_WIDTH), (proj, OFF_U5, S5_WIDTH), (dge_a, 0, S5_WIDTH), (dge_b, 0, S5_WIDTH)],
        par_ins=[(s5_d, 0, S5_WIDTH)], row_outs=[(S5_WIDTH, BF16), (S5_WIDTH, F32)], acc_outs=[(1, S5_WIDTH)])
    d_c_band = _band_matmul("mm_d_c", "tn", dyc, s)
    ds = _band_matmul("mm_d_s", "nn", dyc, c_band)
    dbu, d_lam = _s5_scan_bwd(ds, s, lam_pow_conj, n_seq, seq_len)
    d_bb_band = _band_matmul("mm_d_bb", "tn", proj, dbu, a_blk0=OFF_U5 // BAND)
    du5_b = _band_matmul("mm_d_u5", "nt", dbu, bb_band)
    (du5,) = row("s5_du", lambda r, q: ([r[0] + r[1]], []), row_ins=[(du5_a, 0, S5_WIDTH), (du5_b, 0, S5_WIDTH)],
                 par_ins=[], row_outs=[(S5_WIDTH, BF16)])

    d_lr, d_li = _state_uncols(d_lam)
    d_bbr, d_bbi = _band_take(d_bb_band)
    d_are, d_aim, d_ls, d_br2, d_bi2 = _s5_params_bwd(
        a_re, a_im, log_step, b_re2, b_im2, expand,
        d_lr.reshape(S5_GROUPS, S5_STATE), d_li.reshape(S5_GROUPS, S5_STATE), d_bbr, d_bbi)
    g["s5_a_re"], g["s5_a_im"], g["s5_log_step"] = d_are, d_aim, d_ls
    g["s5_b_re_ghp"], g["s5_b_im_ghp"] = d_br2, d_bi2
    d_cr, d_ci = _band_take(d_c_band)
    g["s5_c_re"], g["s5_c_im"] = d_cr, -d_ci

    def gate_bwd(r, q):
        _, vjp = jax.vjp(_gated_norm, r[0], r[1], q[0])
        dy_, dz_, dw_ = vjp(r[2])
        return [dy_, dz_], [dw_]

    dy_ssd, dzs, g["ssd_norm_w"] = row(
        "ssd_gate_bwd", gate_bwd, row_ins=gn_rows + [(dyss, 0, SSD_WIDTH)], par_ins=[(ssd_norm_w, 0, SSD_WIDTH)],
        row_outs=[(SSD_WIDTH, F32), (SSD_WIDTH, BF16)], acc_outs=[(1, SSD_WIDTH)])
    dxs, dbm, dcm, ddt, d_dtb, d_alog, d_dsk = _ssd_bwd(xbc_act, proj, states, dy_ssd, dtb, alog, dsk, n_seq, seq_len)
    g["ssd_dt_bias"], g["ssd_a_log"], g["ssd_d"] = _unpad_heads(d_dtb), _unpad_heads(d_alog), _unpad_heads(d_dsk)
    conv_parts = [_conv_bwd("ssd_conv_bwd_x", proj, dxs, conv_w, conv_b, n_rows, seq_len, 0),
                  _conv_bwd("ssd_conv_bwd_b", proj, dbm, conv_w, conv_b, n_rows, seq_len, SSD_WIDTH),
                  _conv_bwd("ssd_conv_bwd_c", proj, dcm, conv_w, conv_b, n_rows, seq_len, SSD_WIDTH + SSD_BC)]
    g["ssd_conv_w"] = jnp.concatenate([c[1] for c in conv_parts], axis=1)
    g["ssd_conv_b"] = jnp.concatenate([c[2] for c in conv_parts], axis=1)

    dproj = jnp.concatenate([c[0] for c in conv_parts] + [du5, dzs, dz5, dg5, dgs, ddt], axis=1)
    g["w_in_t"] = _unpad_w_in_t(_matmul("mm_d_w_in", dproj, hn, ta=True))
    dhn = _matmul("mm_d_hn", dproj, w_pad_t)

    def norm_bwd(r, q):
        x_, dhn_, dh1_ = r
        _, vjp = jax.vjp(_rms, x_, q[0])
        dx_, dw_ = vjp(dhn_)
        return [dx_ + dh1_], [dw_]

    dx, g["norm_w"] = row("rms_in_bwd", norm_bwd, row_ins=[(x2, 0, D_MODEL), (dhn, 0, D_MODEL), (dh1, 0, D_MODEL)],
                          par_ins=[(norm_w, 0, D_MODEL)], row_outs=[(D_MODEL, F32)], acc_outs=[(1, D_MODEL)])
    return loss, dx.reshape(x.shape), g


HBM = pl.BlockSpec(memory_space=pltpu.HBM)


def _chip_index(x, y):
    return 2 * x + y


def _half(shape2d, axis, which):
    h = shape2d[axis] // 2
    sl = pl.ds(pl.multiple_of(which * h, 128 if axis else 8), h)
    return (slice(None), sl) if axis else (sl, slice(None))


def _gather_chips(split, axes, whole):
    ns, nw = len(split), len(whole)
    n = ns + nw

    def body(*refs):
        ins, outs = refs[:n], refs[n:2 * n]
        ici_send, ici_recv, d2d_send, d2d_recv, local_sems = refs[2 * n:]
        x, y, c = lax.axis_index("x"), lax.axis_index("y"), lax.axis_index("c")
        me = _chip_index(x, y)
        sibling = (x, y, 1 - c)
        peers = [(1 - x, y), (x, 1 - y), (1 - x, 1 - y)]

        def half(t, which):
            return _half(split[t].shape, axes[t], which)

        copies = []
        for t in range(n):
            loc = pltpu.make_async_copy(ins[t], outs[t].at[me], local_sems.at[t])
            loc.start()
            copies.append(loc)

        def ici(t, k, slot):
            px, py = peers[k]
            if t < ns:
                src, dst = ins[t].at[half(t, c)], outs[t].at[(slot,) + half(t, c)]
            else:
                src, dst = ins[t], outs[t].at[slot]
            return pltpu.make_async_remote_copy(src_ref=src, dst_ref=dst, send_sem=ici_send.at[t, k],
                                                recv_sem=ici_recv.at[t, k], device_id=(px, py, c), device_id_type=MESH)

        def d2d(t, k, which):
            rows = outs[t].at[(_chip_index(*peers[k]),) + half(t, which)]
            return pltpu.make_async_remote_copy(src_ref=rows, dst_ref=rows, send_sem=d2d_send.at[t, k],
                                                recv_sem=d2d_recv.at[t, k], device_id=sibling, device_id_type=MESH)

        sends = []
        for t in range(n):
            for k in range(3):
                cp = ici(t, k, me)
                cp.start()
                sends.append(cp)
        for t in range(n):
            for k in range(3):
                ici(t, k, _chip_index(*peers[k])).wait_recv()
                if t < ns:
                    cp = d2d(t, k, c)
                    cp.start()
                    sends.append(cp)
        for t in range(ns):
            for k in range(3):
                d2d(t, k, 1 - c).wait_recv()
        for cp in sends:
            cp.wait_send()
        for cp in copies:
            cp.wait()

    arrays = list(split) + list(whole)
    return pl.pallas_call(
        body, name="gather_weights",
        in_specs=[HBM] * n, out_specs=[HBM] * n,
        out_shape=[jax.ShapeDtypeStruct((N_CHIPS,) + a.shape, a.dtype) for a in arrays],
        scratch_shapes=[pltpu.SemaphoreType.DMA((n, 3)), pltpu.SemaphoreType.DMA((n, 3)),
                        pltpu.SemaphoreType.DMA((ns, 3)), pltpu.SemaphoreType.DMA((ns, 3)),
                        pltpu.SemaphoreType.DMA((n,))],
    )(*arrays)


def _half_shape(shape2d, axis):
    r, c = shape2d
    return (r, c // 2) if axis else (r // 2, c)


def _swap_halves(slotted, axes, small):
    n = len(slotted)

    def body(*refs):
        ins, sm_in = refs[:n], refs[n]
        outs, sm_out = refs[n + 1:2 * n + 1], refs[2 * n + 1]
        send_sems, recv_sems, local_sem, sm_send, sm_recv = refs[2 * n + 2:]
        x, y, c = lax.axis_index("x"), lax.axis_index("y"), lax.axis_index("c")
        dev = 4 * x + 2 * y + c
        local = pltpu.make_async_copy(sm_in, sm_out.at[dev], local_sem)
        local.start()
        sends = []
        for t in range(n):
            other = (slice(None),) + _half(slotted[t].shape[1:], axes[t], 1 - c)
            cp = pltpu.make_async_remote_copy(
                src_ref=ins[t].at[other], dst_ref=outs[t], send_sem=send_sems.at[t],
                recv_sem=recv_sems.at[t], device_id=(x, y, 1 - c), device_id_type=MESH)
            cp.start()
            sends.append(cp)
        rel = [(fx, fy, fc) for fx in (0, 1) for fy in (0, 1) for fc in (0, 1)][1:]
        for k, (fx, fy, fc) in enumerate(rel):
            cp = pltpu.make_async_remote_copy(
                src_ref=sm_in, dst_ref=sm_out.at[dev], send_sem=sm_send.at[k], recv_sem=sm_recv.at[k],
                device_id=(x ^ fx, y ^ fy, c ^ fc), device_id_type=MESH)
            cp.start()
            sends.append(cp)
        for cp in sends[:n]:
            cp.wait_recv()
        for k, (fx, fy, fc) in enumerate(rel):
            src_dev = 4 * (x ^ fx) + 2 * (y ^ fy) + (c ^ fc)
            pltpu.make_async_remote_copy(
                src_ref=sm_in, dst_ref=sm_out.at[src_dev], send_sem=sm_send.at[k], recv_sem=sm_recv.at[k],
                device_id=(x ^ fx, y ^ fy, c ^ fc), device_id_type=MESH).wait_recv()
        for cp in sends:
            cp.wait_send()
        local.wait()

    return pl.pallas_call(
        body, name="swap_halves",
        in_specs=[HBM] * (n + 1), out_specs=[HBM] * (n + 1),
        out_shape=[jax.ShapeDtypeStruct((a.shape[0],) + _half_shape(a.shape[1:], ax), a.dtype)
                   for a, ax in zip(slotted, axes, strict=True)]
        + [jax.ShapeDtypeStruct((N_DEV,) + small.shape, small.dtype)],
        scratch_shapes=[pltpu.SemaphoreType.DMA((n,)), pltpu.SemaphoreType.DMA((n,)), pltpu.SemaphoreType.DMA,
                        pltpu.SemaphoreType.DMA((7,)), pltpu.SemaphoreType.DMA((7,))],
    )(*slotted, small)


def _scatter_halves(parts):
    n = len(parts)

    def body(*refs):
        ins, outs = refs[:n], refs[n:2 * n]
        send_sems, recv_sems = refs[2 * n:]
        x, y, c = lax.axis_index("x"), lax.axis_index("y"), lax.axis_index("c")
        peers = [(1 - x, y), (x, 1 - y), (1 - x, 1 - y)]
        sends = []
        for t in range(n):
            for k, (px, py) in enumerate(peers):
                cp = pltpu.make_async_remote_copy(
                    src_ref=ins[t].at[_chip_index(px, py)], dst_ref=outs[t].at[k], send_sem=send_sems.at[t, k],
                    recv_sem=recv_sems.at[t, k], device_id=(px, py, c), device_id_type=MESH)
                cp.start()
                sends.append(cp)
        for cp in sends:
            cp.wait_recv()
        for cp in sends:
            cp.wait_send()

    return pl.pallas_call(
        body, name="scatter_halves",
        in_specs=[HBM] * n, out_specs=[HBM] * n,
        out_shape=[jax.ShapeDtypeStruct((3,) + a.shape[1:], a.dtype) for a in parts],
        scratch_shapes=[pltpu.SemaphoreType.DMA((n, 3)), pltpu.SemaphoreType.DMA((n, 3))],
    )(*parts)


def _tiling(r, c, f32_per_elem):
    if r % 8 == 0:
        tr = _row_tile(r, f32_per_elem * c * 4)
        return r // tr, (tr, c), lambda i: (i, 0)
    assert c % 128 == 0, (r, c)
    return c // 128, (r, 128), lambda i: (0, i)


def _pair_sum(name, slotted, other, idx, axis):
    _, r, c = slotted.shape
    hr, hc = _half_shape((r, c), axis)
    n, (tr, tc), at = _tiling(hr, hc, 13)
    if axis == 0:
        a = slotted.reshape(N_CHIPS, 2, hr, c)
        a_all = pl.BlockSpec((N_CHIPS, 1, tr, tc), lambda i, s: (0, s[0]) + at(i))
        a_own = pl.BlockSpec((1, 1, tr, tc), lambda i, s: (s[1], s[0]) + at(i))
    else:
        a, per_half = slotted, hc // tc
        a_all = pl.BlockSpec((N_CHIPS, tr, tc), lambda i, s: (0, at(i)[0], s[0] * per_half + at(i)[1]))
        a_own = pl.BlockSpec((1, tr, tc), lambda i, s: (s[1], at(i)[0], s[0] * per_half + at(i)[1]))

    def body(idx_ref, a_ref, b_ref, am_ref, bm_ref, p_ref, own_ref):
        mine, mine_own = (a_ref[:, 0], am_ref[0, 0]) if axis == 0 else (a_ref[...], am_ref[0])
        p_ref[...] = (mine + b_ref[...]).astype(p_ref.dtype)
        own_ref[...] = mine_own + bm_ref[0]

    grid_spec = pltpu.PrefetchScalarGridSpec(
        num_scalar_prefetch=1, grid=(n,),
        in_specs=[a_all, pl.BlockSpec((N_CHIPS, tr, tc), lambda i, s: (0,) + at(i)),
                  a_own, pl.BlockSpec((1, tr, tc), lambda i, s: (s[1],) + at(i))],
        out_specs=[pl.BlockSpec((N_CHIPS, tr, tc), lambda i, s: (0,) + at(i)),
                   pl.BlockSpec((tr, tc), lambda i, s: at(i))])
    return pl.pallas_call(
        body, name=name, grid_spec=grid_spec,
        out_shape=[jax.ShapeDtypeStruct((N_CHIPS, hr, hc), BF16), jax.ShapeDtypeStruct((hr, hc), F32)],
        compiler_params=_cparams(("parallel",)),
    )(idx, a, other, a, other)


def _sum_parts(name, own, recv):
    h, c = own.shape
    n, (tr, tc), at = _tiling(h, c, 4)

    def body(o_ref, r_ref, out_ref):
        acc = o_ref[...]
        for k in range(3):
            acc = acc + r_ref[k].astype(F32)
        out_ref[...] = acc

    return pl.pallas_call(
        body, name=name, grid=(n,),
        in_specs=[pl.BlockSpec((tr, tc), at), pl.BlockSpec((3, tr, tc), lambda i: (0,) + at(i))],
        out_specs=pl.BlockSpec((tr, tc), at),
        out_shape=jax.ShapeDtypeStruct((h, c), F32),
        compiler_params=_cparams(("parallel",)),
    )(own, recv)


def _swap_sibling(parts):
    n = len(parts)

    def body(*refs):
        ins, outs = refs[:n], refs[n:2 * n]
        send_sems, recv_sems = refs[2 * n:]
        x, y, c = lax.axis_index("x"), lax.axis_index("y"), lax.axis_index("c")
        cps = []
        for t in range(n):
            cp = pltpu.make_async_remote_copy(
                src_ref=ins[t], dst_ref=outs[t], send_sem=send_sems.at[t], recv_sem=recv_sems.at[t],
                device_id=(x, y, 1 - c), device_id_type=MESH)
            cp.start()
            cps.append(cp)
        for cp in cps:
            cp.wait_recv()
        for cp in cps:
            cp.wait_send()

    return pl.pallas_call(
        body, name="swap_sibling",
        in_specs=[HBM] * n, out_specs=[HBM] * n,
        out_shape=[jax.ShapeDtypeStruct(a.shape, a.dtype) for a in parts],
        scratch_shapes=[pltpu.SemaphoreType.DMA((n,)), pltpu.SemaphoreType.DMA((n,))],
    )(*parts)


def _sum_slots(name, a):
    k, r, c = a.shape
    tr = _row_tile(r, (k + 1) * c * 4)

    def body(a_ref, o_ref):
        acc = a_ref[0]
        for i in range(1, k):
            acc = acc + a_ref[i]
        o_ref[...] = acc

    return pl.pallas_call(
        body, name=name, grid=(r // tr,),
        in_specs=[pl.BlockSpec((k, tr, c), lambda i: (0, i, 0))],
        out_specs=pl.BlockSpec((tr, c), lambda i: (i, 0)),
        out_shape=jax.ShapeDtypeStruct((r, c), a.dtype),
        compiler_params=_cparams(("parallel",)),
    )(a)


def _adamw(name, w, m, v, g_parts):
    r, c = w.shape
    ng = len(g_parts)
    tr = _row_tile(r, (7 + ng) * c * 4)
    c1 = 1.0 - ADAM_B1 ** ADAM_STEP
    c2 = 1.0 - ADAM_B2 ** ADAM_STEP

    def body(*refs):
        w_ref, m_ref, v_ref = refs[:3]
        g_refs = refs[3:3 + ng]
        go_ref, d_ref, mo_ref, vo_ref = refs[3 + ng:]
        g = g_refs[0][...]
        for gr in g_refs[1:]:
            g = g + gr[...]
        m_new = ADAM_B1 * m_ref[...] + (1.0 - ADAM_B1) * g
        v_new = ADAM_B2 * v_ref[...] + (1.0 - ADAM_B2) * (g * g)
        go_ref[...] = g
        mo_ref[...] = m_new
        vo_ref[...] = v_new
        d_ref[...] = -ADAM_LR * ((m_new / c1) / (jnp.sqrt(v_new / c2) + ADAM_EPS) + ADAM_WD * w_ref[...])

    spec = pl.BlockSpec((tr, c), lambda i: (i, 0))
    return pl.pallas_call(
        body, name=name, grid=(r // tr,),
        in_specs=[spec] * (3 + ng), out_specs=[spec] * 4,
        out_shape=[jax.ShapeDtypeStruct((r, c), F32)] * 4,
        compiler_params=_cparams(("parallel",)),
    )(w, m, v, *g_parts)


def _adamw_halves(name, w, m, v, own, other, idx, axis):
    hr, hc = own.shape
    nb, (tr, tc), at = _tiling(hr, hc, 9)
    c1 = 1.0 - ADAM_B1 ** ADAM_STEP
    c2 = 1.0 - ADAM_B2 ** ADAM_STEP

    def body(idx_ref, w_ref, m_ref, v_ref, own_ref, oth_ref, go_ref, d_ref, mo_ref, vo_ref):
        g = jnp.where(pl.program_id(0) == idx_ref[0], own_ref[...], oth_ref[...])
        m_new = ADAM_B1 * m_ref[...] + (1.0 - ADAM_B1) * g
        v_new = ADAM_B2 * v_ref[...] + (1.0 - ADAM_B2) * (g * g)
        go_ref[...] = g
        mo_ref[...] = m_new
        vo_ref[...] = v_new
        d_ref[...] = -ADAM_LR * ((m_new / c1) / (jnp.sqrt(v_new / c2) + ADAM_EPS) + ADAM_WD * w_ref[...])

    per_half = (hc // tc) if axis else (hr // tr)
    if axis:
        full = pl.BlockSpec((tr, tc), lambda hh, i, s: (at(i)[0], hh * per_half + at(i)[1]))
    else:
        full = pl.BlockSpec((tr, tc), lambda hh, i, s: (hh * per_half + at(i)[0], at(i)[1]))
    part = pl.BlockSpec((tr, tc), lambda hh, i, s: at(i))
    grid_spec = pltpu.PrefetchScalarGridSpec(
        num_scalar_prefetch=1, grid=(2, nb), in_specs=[full, full, full, part, part], out_specs=[full] * 4)
    return pl.pallas_call(
        body, name=name, grid_spec=grid_spec, out_shape=[jax.ShapeDtypeStruct(w.shape, F32)] * 4,
        compiler_params=_cparams(("parallel", "parallel")),
    )(idx, w, m, v, own, other)


WEIGHTS = ['norm_w', 'w_in', 's5_a_re', 's5_a_im', 's5_b_re', 's5_b_im', 's5_c_re', 's5_c_im', 's5_d', 's5_log_step',
           's5_w_glu', 's5_b_glu', 'ssd_conv_w', 'ssd_conv_b', 'ssd_dt_bias', 'ssd_a_log', 'ssd_d', 'ssd_norm_w',
           'w_br_s5', 'w_br_ssd', 'w_out', 'ple_norm_w', 'w_ple_gate', 'w_ple_proj', 'final_norm_w']
SHARDED = {'w_in': ((IN_PROJ_DIM, 1024), 0), 's5_w_glu': ((512, 512), 0), 'ssd_conv_w': ((SSD_CONV, SSD_CONV_DIM), 1),
           'w_br_s5': ((512, 1024), 1), 'w_br_ssd': ((1536, 1024), 0), 'w_out': ((1024, 1024), 0),
           'w_ple_gate': ((1024, 1024), 0), 'w_ple_proj': ((256, 1024), 1)}
TRANSPOSED = ('w_in',)
SMALL = [n for n in WEIGHTS if n not in SHARDED]


def _shard_shape(name):
    (r, c), ax = SHARDED[name]
    return (r // N_CHIPS, c) if ax == 0 else (r, c // N_CHIPS)


def _half_axis(name):
    return 0 if (_shard_shape(name)[0] // 2) % 16 == 0 else 1


def _shard2d(name, a):
    r, c = _shard_shape(name)
    return a.reshape(c, r).T if name in TRANSPOSED else a.reshape(r, c)


def _unshard2d(name, a2, shape):
    return (a2.T if name in TRANSPOSED else a2).reshape(shape)


def _unslot(name, a4):
    (r, c), ax = SHARDED[name]
    if ax == 0:
        return a4.reshape(r, c)
    return jnp.transpose(a4, (1, 0, 2)).reshape(r, c)


def _slot(name, full):
    (r, c), ax = SHARDED[name]
    if ax == 0:
        return full.reshape(N_CHIPS, r // N_CHIPS, c)
    return jnp.transpose(full.reshape(r, N_CHIPS, c // N_CHIPS), (1, 0, 2))


GHP = ('s5_b_re', 's5_b_im')


def _view_shape(name):
    if name in ('s5_a_re', 's5_a_im'):
        return (S5_GROUPS, S5_STATE)
    if name in GHP + ('s5_c_re', 's5_c_im'):
        return (S5_GROUPS, S5_GROUP, S5_STATE)
    if name == 'ssd_conv_w':
        return (SSD_CONV, SSD_CONV_DIM // N_CHIPS)
    return (1, {'s5_log_step': S5_GROUPS, 'ssd_conv_b': SSD_CONV_DIM, 'ssd_norm_w': SSD_WIDTH, 's5_d': S5_WIDTH,
                's5_b_glu': S5_WIDTH, 'ssd_dt_bias': SSD_HEADS, 'ssd_a_log': SSD_HEADS, 'ssd_d': SSD_HEADS}.get(name, D_MODEL))


def _view(name, a):
    if name in GHP:
        return jnp.swapaxes(a.reshape(S5_GROUPS, S5_STATE, S5_GROUP), 1, 2)
    return a.reshape(_view_shape(name))


def _unview(name, a, shape):
    return (jnp.swapaxes(a, 1, 2) if name in GHP else a).reshape(shape)


def _adamw_small(ws, ms, vs, gs):
    n = len(ws)
    c1 = 1.0 - ADAM_B1 ** ADAM_STEP
    c2 = 1.0 - ADAM_B2 ** ADAM_STEP

    def body(*refs):
        w_r, m_r, v_r, g_r = (refs[k * n:(k + 1) * n] for k in range(4))
        d_o, m_o, v_o = (refs[k * n:(k + 1) * n] for k in range(4, 7))
        for i in range(n):
            g = g_r[i][...]
            m_new = ADAM_B1 * m_r[i][...] + (1.0 - ADAM_B1) * g
            v_new = ADAM_B2 * v_r[i][...] + (1.0 - ADAM_B2) * (g * g)
            m_o[i][...] = m_new
            v_o[i][...] = v_new
            d_o[i][...] = -ADAM_LR * ((m_new / c1) / (jnp.sqrt(v_new / c2) + ADAM_EPS) + ADAM_WD * w_r[i][...])

    return pl.pallas_call(
        body, name="adamw_small", out_shape=[jax.ShapeDtypeStruct(w.shape, F32) for w in ws] * 3,
        compiler_params=pltpu.CompilerParams(vmem_limit_bytes=VMEM_LIMIT),
    )(*ws, *ms, *vs, *gs)


def _pack_small(vals):
    flat = jnp.concatenate([v.reshape(-1).astype(F32) for v in vals])
    rows = -(-flat.shape[0] // (256 * 128)) * 256
    return jnp.pad(flat, (0, rows * 128 - flat.shape[0])).reshape(rows, 128)


def _unpack_small(packed, shapes):
    flat = packed.reshape(-1)
    out, off = [], 0
    for sh in shapes:
        n = math.prod(sh)
        out.append(flat[off:off + n].reshape(sh))
        off += n
    return out


def kernel(x, p, norm_w, w_in, s5_a_re, s5_a_im, s5_b_re, s5_b_im, s5_c_re, s5_c_im, s5_d, s5_log_step, s5_w_glu, s5_b_glu, ssd_conv_w, ssd_conv_b, ssd_dt_bias, ssd_a_log, ssd_d, ssd_norm_w, w_br_s5, w_br_ssd, w_out, ple_norm_w, w_ple_gate, w_ple_proj, final_norm_w, loss_target, m_norm_w, m_w_in, m_s5_a_re, m_s5_a_im, m_s5_b_re, m_s5_b_im, m_s5_c_re, m_s5_c_im, m_s5_d, m_s5_log_step, m_s5_w_glu, m_s5_b_glu, m_ssd_conv_w, m_ssd_conv_b, m_ssd_dt_bias, m_ssd_a_log, m_ssd_d, m_ssd_norm_w, m_w_br_s5, m_w_br_ssd, m_w_out, m_ple_norm_w, m_w_ple_gate, m_w_ple_proj, m_final_norm_w, v_norm_w, v_w_in, v_s5_a_re, v_s5_a_im, v_s5_b_re, v_s5_b_im, v_s5_c_re, v_s5_c_im, v_s5_d, v_s5_log_step, v_s5_w_glu, v_s5_b_glu, v_ssd_conv_w, v_ssd_conv_b, v_ssd_dt_bias, v_ssd_a_log, v_ssd_d, v_ssd_norm_w, v_w_br_s5, v_w_br_ssd, v_w_out, v_ple_norm_w, v_w_ple_gate, v_w_ple_proj, v_final_norm_w):
    args = locals()
    wl = {n: args[n] for n in WEIGHTS}
    ml = {n: args["m_" + n] for n in WEIGHTS}
    vl = {n: args["v_" + n] for n in WEIGHTS}
    big = [n for n in SHARDED if n != 'ssd_conv_w']
    chip = _chip_index(lax.axis_index("x"), lax.axis_index("y"))
    idx = jnp.stack([lax.axis_index("c"), chip]).astype(jnp.int32)

    axes = [_half_axis(n) for n in big]
    gathered = _gather_chips([_shard2d(n, wl[n]).astype(BF16) for n in big], axes,
                             [_shard2d('ssd_conv_w', wl['ssd_conv_w'])])
    full = {n: wl[n] for n in SMALL}
    for n, a4 in zip(big + ['ssd_conv_w'], gathered, strict=True):
        if n in TRANSPOSED:
            full[n + "_t"] = a4
        else:
            full[n] = _unslot(n, a4)

    loss, grad_x, g = _local_step(x, p[0], loss_target, full)
    for n in TRANSPOSED:
        g[n] = g.pop(n + "_t")

    for n in ('s5_b_re', 's5_b_im'):
        g[n] = g.pop(n + "_ghp")
    small_shapes = [(1, 1)] + [_view_shape(n) for n in SMALL] + [(SSD_CONV, SSD_CONV_DIM)]
    small_pack = _pack_small([loss] + [g[n] for n in SMALL] + [g['ssd_conv_w']])
    slotted = [g[n] if n in TRANSPOSED else _slot(n, g[n]) for n in big]
    swapped = _swap_halves(slotted, axes, small_pack)
    pair = [_pair_sum("pair_sum_" + n, a, b, idx, ax)
            for n, a, b, ax in zip(big, slotted, swapped[:-1], axes, strict=True)]
    small_sum = _sum_slots("sum_small", swapped[-1])
    recv = _scatter_halves([pb for pb, _ in pair])
    halves = [_sum_parts("sum_chips_" + n, own, r) for n, (_, own), r in zip(big, pair, recv, strict=True)]
    other_halves = _swap_sibling(halves)

    out_g, out_d, out_m, out_v = {}, {}, {}, {}
    for n, own, oth, ax in zip(big, halves, other_halves, axes, strict=True):
        res = _adamw_halves("adamw_" + n, _shard2d(n, wl[n]), _shard2d(n, ml[n]), _shard2d(n, vl[n]), own, oth, idx, ax)
        out_g[n], out_d[n], out_m[n], out_v[n] = (_unshard2d(n, r, wl[n].shape) for r in res)
    sm = _unpack_small(small_sum, small_shapes)
    loss_total = sm[0].reshape(())
    conv_g = lax.dynamic_slice(sm[-1], (0, chip * (SSD_CONV_DIM // N_CHIPS)), (SSD_CONV, SSD_CONV_DIM // N_CHIPS))
    names = SMALL + ['ssd_conv_w']
    grads = sm[1:-1] + [conv_g]
    res = _adamw_small([_view(n, wl[n]) for n in names], [_view(n, ml[n]) for n in names],
                       [_view(n, vl[n]) for n in names], grads)
    for i, n in enumerate(names):
        out_g[n] = _unview(n, grads[i], wl[n].shape)
        out_d[n], out_m[n], out_v[n] = (_unview(n, res[k * len(names) + i], wl[n].shape) for k in range(3))

    return (loss_total, grad_x, *[out_g[n] for n in WEIGHTS], *[out_d[n] for n in WEIGHTS],
            *[out_m[n] for n in WEIGHTS], *[out_v[n] for n in WEIGHTS])
```

```python
import functools
import math

import jax
import jax.numpy as jnp
from jax import lax
from jax.experimental import pallas as pl
from jax.experimental.pallas import tpu as pltpu

F32 = jnp.float32
BF16 = jnp.bfloat16
MESH = pl.DeviceIdType.MESH

D_MODEL = 1024
PLE_DIM = 256
RMS_EPS = 1e-6
S5_WIDTH = 512
S5_GROUP = 16
S5_GROUPS = 32
S5_STATE = 64
S5_N = S5_GROUPS * S5_STATE
S5_LB = 512
S5_NJ = S5_N // S5_LB
S5_TB = 256
S5_LOG_TB = 8
SSD_WIDTH = 1536
SSD_HEADDIM = 64
SSD_HEADS = 24
SSD_GROUPS = 4
SSD_HPG = 6
SSD_STATE = 128
SSD_CONV = 4
SSD_CHUNK = 128
SSD_BC = 512
SSD_CONV_DIM = 2560
GROUP_W = SSD_WIDTH // SSD_GROUPS
N_CHIPS = 4
N_DEV = 8

OFF_XBC, OFF_U5, OFF_ZS, OFF_Z5, OFF_G5, OFF_GS, OFF_DT = 0, 2560, 3072, 4608, 5120, 6144, 7168
PROJ_W = 7680
IN_PROJ_DIM = 7192

ADAM_LR, ADAM_B1, ADAM_B2, ADAM_EPS, ADAM_WD, ADAM_STEP = 0.001, 0.9, 0.999, 1e-08, 0.01, 10

VMEM_LIMIT = 56 * 1024 * 1024


def _pick(n, cands):
    for c in cands:
        if n % c == 0:
            return c
    return n


ROW_BLOCK_BYTES = 8 * 1024 * 1024


def _row_tile(r, bytes_per_row):
    for t in (r, 4096, 2048, 1024, 512, 256, 128, 64, 32, 16, 8):
        if t <= r and r % t == 0 and t * bytes_per_row <= ROW_BLOCK_BYTES:
            return t
    return r


def _cparams(sem):
    return pltpu.CompilerParams(dimension_semantics=sem, vmem_limit_bytes=VMEM_LIMIT)


def _dg(a, b, ca, cb):
    return lax.dot_general(a.astype(BF16), b.astype(BF16), (((ca,), (cb,)), ((), ())), preferred_element_type=F32)


@jax.custom_vjp
def dot_nn(a, b):
    return _dg(a, b, 1, 0)


@jax.custom_vjp
def dot_nt(a, b):
    return _dg(a, b, 1, 1)


@jax.custom_vjp
def dot_tn(a, b):
    return _dg(a, b, 0, 0)


dot_nn.defvjp(lambda a, b: (_dg(a, b, 1, 0), (a, b)), lambda r, g: (_dg(g, r[1], 1, 1), _dg(r[0], g, 0, 0)))
dot_nt.defvjp(lambda a, b: (_dg(a, b, 1, 1), (a, b)), lambda r, g: (_dg(g, r[1], 1, 0), _dg(g, r[0], 0, 0)))
dot_tn.defvjp(lambda a, b: (_dg(a, b, 0, 0), (a, b)), lambda r, g: (_dg(r[1], g, 1, 1), _dg(r[0], g, 1, 0)))


MM_VMEM_BUDGET = 30 * 1024 * 1024


def _mm_tiles(m, n, k, sa, sb, so):
    best, best_key = None, None
    for tm in (1024, 512, 256, 128, 64, 32, 16, 8):
        if m % tm:
            continue
        for tn in (2048, 1536, 1280, 1024, 768, 640, 512, 384, 256, 128):
            if n % tn:
                continue
            for tk in (k, 2048, 1536, 1280, 1024, 768, 512, 256, 128):
                if k % tk or tk > max(k, 128) or (tk == k and k > 2048 and k % 128 == 0):
                    continue
                need = 2 * (tm * tk * sa + tk * tn * sb + tm * tn * so) + (tm * tn * 4 if tk < k else 0)
                if need > MM_VMEM_BUDGET:
                    continue
                key = (tm * tn * tk, tk)
                if best_key is None or key > best_key:
                    best, best_key = (tm, tn, tk), key
    assert best is not None, (m, n, k)
    return best


def _matmul(name, a, b, *, ta=False, tb=False, a_win=None, out_dtype=F32):
    a_off, a_w = a_win if a_win is not None else (0, a.shape[1])
    if ta:
        kdim, m = a.shape[0], a_w
    else:
        m, kdim = a.shape[0], a_w
    n = b.shape[0] if tb else b.shape[1]
    assert (b.shape[1] if tb else b.shape[0]) == kdim, (name, a.shape, b.shape)
    tm, tn, tk = _mm_tiles(m, n, kdim, a.dtype.itemsize, b.dtype.itemsize, jnp.dtype(out_dtype).itemsize)
    nk = kdim // tk
    if ta:
        assert a_off % tm == 0
        a_spec = pl.BlockSpec((tk, tm), lambda i, j, k: (k, i + a_off // tm))
    else:
        assert a_off % tk == 0
        a_spec = pl.BlockSpec((tm, tk), lambda i, j, k: (i, k + a_off // tk))
    if tb:
        b_spec = pl.BlockSpec((tn, tk), lambda i, j, k: (j, k))
    else:
        b_spec = pl.BlockSpec((tk, tn), lambda i, j, k: (k, j))
    ca, cb = (0 if ta else 1), (1 if tb else 0)

    def body(a_ref, b_ref, o_ref, *acc):
        if nk == 1:
            o_ref[...] = _dg(a_ref[...], b_ref[...], ca, cb).astype(o_ref.dtype)
            return
        (acc_ref,) = acc
        k = pl.program_id(2)

        @pl.when(k == 0)
        def _():
            acc_ref[...] = jnp.zeros_like(acc_ref)

        acc_ref[...] += _dg(a_ref[...], b_ref[...], ca, cb)

        @pl.when(k == nk - 1)
        def _():
            o_ref[...] = acc_ref[...].astype(o_ref.dtype)

    return pl.pallas_call(
        body, name=name, grid=(m // tm, n // tn, nk),
        in_specs=[a_spec, b_spec],
        out_specs=pl.BlockSpec((tm, tn), lambda i, j, k: (i, j)),
        out_shape=jax.ShapeDtypeStruct((m, n), out_dtype),
        scratch_shapes=[pltpu.VMEM((tm, tn), F32)] if nk > 1 else [],
        compiler_params=_cparams(("parallel", "parallel", "arbitrary")),
    )(a, b)


BAND = 128


def _band_matmul(name, kind, a, b, *, a_blk0=0, out_dtype=F32):
    n_rows = a.shape[0]
    blk = 2 * S5_LB
    tm = _pick(n_rows, (1024, 512, 256))
    if kind == "nn":
        grid = (n_rows // tm, S5_NJ)
        in_specs = [pl.BlockSpec((tm, BAND), lambda i, j: (i, a_blk0 + j)), pl.BlockSpec((BAND, blk), lambda i, j: (j, 0))]
        out_spec = pl.BlockSpec((tm, blk), lambda i, j: (i, j))
        out_shape = (n_rows, S5_NJ * blk)
        sem = ("parallel", "parallel")

        def body(a_ref, b_ref, o_ref):
            o_ref[...] = _dg(a_ref[...], b_ref[...], 1, 0).astype(o_ref.dtype)
    elif kind == "nt":
        grid = (n_rows // tm, S5_NJ)
        in_specs = [pl.BlockSpec((tm, blk), lambda i, j: (i, j)), pl.BlockSpec((BAND, blk), lambda i, j: (j, 0))]
        out_spec = pl.BlockSpec((tm, BAND), lambda i, j: (i, j))
        out_shape = (n_rows, S5_NJ * BAND)
        sem = ("parallel", "parallel")

        def body(a_ref, b_ref, o_ref):
            o_ref[...] = _dg(a_ref[...], b_ref[...], 1, 1).astype(o_ref.dtype)
    else:
        grid = (S5_NJ, n_rows // tm)
        in_specs = [pl.BlockSpec((tm, BAND), lambda j, k: (k, a_blk0 + j)), pl.BlockSpec((tm, blk), lambda j, k: (k, j))]
        out_spec = pl.BlockSpec((BAND, blk), lambda j, k: (j, 0))
        out_shape = (S5_NJ * BAND, blk)
        sem = ("parallel", "arbitrary")

        def body(a_ref, b_ref, o_ref):
            @pl.when(pl.program_id(1) == 0)
            def _():
                o_ref[...] = jnp.zeros_like(o_ref)

            o_ref[...] += _dg(a_ref[...], b_ref[...], 0, 0)

    return pl.pallas_call(
        body, name=name, grid=grid, in_specs=in_specs, out_specs=out_spec,
        out_shape=jax.ShapeDtypeStruct(out_shape, out_dtype), compiler_params=_cparams(sem),
    )(a, b)


def _rowwise(name, fn, n_rows, tr, row_ins, par_ins, row_outs, acc_outs=()):
    nr, npar, no, na = len(row_ins), len(par_ins), len(row_outs), len(acc_outs)
    in_specs = []
    for arr, off, w in row_ins:
        assert off % w == 0 and arr.shape[0] == n_rows, (name, arr.shape, off, w)
        in_specs.append(pl.BlockSpec((tr, w), functools.partial(lambda i, c: (i, c), c=off // w)))
    for arr, off, w in par_ins:
        assert off % w == 0
        in_specs.append(pl.BlockSpec((arr.shape[0], w), functools.partial(lambda i, c: (0, c), c=off // w)))
    out_specs = [pl.BlockSpec((tr, w), lambda i: (i, 0)) for w, _ in row_outs]
    out_specs += [pl.BlockSpec((r, w), lambda i: (0, 0)) for r, w in acc_outs]
    out_shape = [jax.ShapeDtypeStruct((n_rows, w), dt) for w, dt in row_outs]
    out_shape += [jax.ShapeDtypeStruct((r, w), F32) for r, w in acc_outs]

    def body(*refs):
        rows = [r[...] for r in refs[:nr]]
        pars = [r[...] for r in refs[nr:nr + npar]]
        o_refs = refs[nr + npar:nr + npar + no]
        a_refs = refs[nr + npar + no:]
        outs, accs = fn(rows, pars)
        for o_ref, o in zip(o_refs, outs, strict=True):
            o_ref[...] = o.astype(o_ref.dtype)
        if na:
            @pl.when(pl.program_id(0) == 0)
            def _():
                for a_ref in a_refs:
                    a_ref[...] = jnp.zeros_like(a_ref)

            for a_ref, a in zip(a_refs, accs, strict=True):
                a_ref[...] += jnp.broadcast_to(a, a_ref.shape)

    res = pl.pallas_call(
        body, name=name, grid=(n_rows // tr,),
        in_specs=in_specs, out_specs=out_specs, out_shape=out_shape,
        compiler_params=_cparams(("arbitrary",) if na else ("parallel",)),
    )(*[a for a, _, _ in row_ins], *[a for a, _, _ in par_ins])
    return res


def _rms(x, w):
    return x * lax.rsqrt(jnp.mean(x * x, axis=-1, keepdims=True) + RMS_EPS) * w


def _gated_norm(y, z, w):
    outs = []
    for g in range(SSD_GROUPS):
        sl = slice(g * GROUP_W, (g + 1) * GROUP_W)
        yg = y[:, sl] * jax.nn.silu(z[:, sl])
        outs.append(yg * lax.rsqrt(jnp.mean(yg * yg, axis=-1, keepdims=True) + RMS_EPS) * w[:, sl])
    return jnp.concatenate(outs, axis=-1)


def _s5_out(yc, u, z, t, d, bg):
    ge = jax.nn.gelu(yc + d * u)
    return ge * jax.nn.sigmoid(t + bg) * jax.nn.silu(z)


def _merge(g5, gs, m5, ms):
    return jax.nn.sigmoid(g5) * m5 + jax.nn.sigmoid(gs) * ms


def _head_loss(h1, pgl, pp, fw, tgt):
    h2 = h1 + jax.nn.sigmoid(pgl) * pp
    err = _rms(h2, fw) - tgt
    per_row = 0.5 * jnp.mean(err * err, axis=-1, keepdims=True)
    return jnp.sum(per_row, axis=0, keepdims=True)


def _s5_disc(a_re, a_im, log_step, b_re2, b_im2, expand):
    step = jnp.exp(log_step)
    mag = jnp.exp(a_re * step)
    lb_re = mag * jnp.cos(a_im * step)
    lb_im = mag * jnp.sin(a_im * step)
    den = a_re * a_re + a_im * a_im
    n_re = lb_re - 1.0
    f_re = (n_re * a_re + lb_im * a_im) / den
    f_im = (lb_im * a_re - n_re * a_im) / den
    hi = lax.Precision.HIGHEST
    fr = jnp.dot(expand, f_re, precision=hi, preferred_element_type=F32)
    fi = jnp.dot(expand, f_im, precision=hi, preferred_element_type=F32)
    return lb_re, lb_im, fr * b_re2 - fi * b_im2, fr * b_im2 + fi * b_re2


def _s5_params_fwd(a_re, a_im, log_step, b_re2, b_im2, expand):
    gp = a_re.shape

    def body(ar, ai, ls, br, bi, ex, pr_ref, pi_ref, bbr_ref, bbi_ref):
        lr, li, bbr, bbi = _s5_disc(ar[...], ai[...], ls[...], br[...], bi[...], ex[...])
        bbr_ref[...] = bbr
        bbi_ref[...] = bbi
        qr, qi = lr, li
        for k in range(S5_LOG_TB):
            pr_ref[k] = qr
            pi_ref[k] = qi
            qr, qi = qr * lr - qi * li, qr * li + qi * lr

    return pl.pallas_call(
        body, name="s5_params_fwd",
        out_shape=(jax.ShapeDtypeStruct((S5_LOG_TB,) + gp, F32), jax.ShapeDtypeStruct((S5_LOG_TB,) + gp, F32),
                   jax.ShapeDtypeStruct(b_re2.shape, F32), jax.ShapeDtypeStruct(b_re2.shape, F32)),
    )(a_re, a_im, log_step, b_re2, b_im2, expand)


def _s5_params_bwd(a_re, a_im, log_step, b_re2, b_im2, expand, d_lr, d_li, d_bbr, d_bbi):
    def body(ar, ai, ls, br, bi, ex, glr, gli, gbr, gbi, dar, dai, dls, dbr, dbi):
        _, vjp = jax.vjp(lambda *p: _s5_disc(*p, ex[...]), ar[...], ai[...], ls[...], br[...], bi[...])
        g = vjp((glr[...], gli[...], gbr[...], gbi[...]))
        for ref, val in zip((dar, dai, dls, dbr, dbi), g, strict=True):
            ref[...] = val

    return pl.pallas_call(
        body, name="s5_params_bwd",
        out_shape=tuple(jax.ShapeDtypeStruct(v.shape, F32) for v in (a_re, a_im, log_step, b_re2, b_im2)),
    )(a_re, a_im, log_step, b_re2, b_im2, expand, d_lr, d_li, d_bbr, d_bbi)


def _scan_block(x_ref, out_ref, lp, edge_pow, cr, ci, reverse, each=None):
    n_g = x_ref.shape[0] // 8
    sub = lax.broadcasted_iota(jnp.int32, (8, S5_LB), 0)
    steps = []
    for sh in (1, 2, 4):
        keep = (sub < 8 - sh) if reverse else (sub >= sh)
        steps.append((8 - sh if reverse else sh, jnp.where(keep, lp[sh - 1:sh, :S5_LB], 0.0),
                      jnp.where(keep, lp[sh - 1:sh, S5_LB:], 0.0)))
    e_r, e_i = edge_pow[:, :S5_LB], edge_pow[:, S5_LB:]
    for r in (range(n_g - 1, -1, -1) if reverse else range(n_g)):
        rows = slice(8 * r, 8 * r + 8)
        xr, xi = x_ref[rows, :S5_LB], x_ref[rows, S5_LB:]
        for by, a_r, a_i in steps:
            pr, pi = pltpu.roll(xr, by, 0), pltpu.roll(xi, by, 0)
            xr, xi = xr + a_r * pr - a_i * pi, xi + a_r * pi + a_i * pr
        xr = xr + e_r * cr - e_i * ci
        xi = xi + e_r * ci + e_i * cr
        out_ref[rows, :S5_LB] = xr
        out_ref[rows, S5_LB:] = xi
        if each is not None:
            each(r, xr, xi)
        cr, ci = (xr[0:1, :], xi[0:1, :]) if reverse else (xr[7:8, :], xi[7:8, :])
    return cr, ci


def _s5_scan_fwd(bu, lam_pow, n_seq, seq_len):
    n_t = seq_len // S5_TB
    blk = 2 * S5_LB

    def body(bu_ref, lp_ref, s_ref, cr, ci):
        @pl.when(pl.program_id(2) == 0)
        def _():
            cr[...] = jnp.zeros_like(cr)
            ci[...] = jnp.zeros_like(ci)

        lp = lp_ref[...]
        cr[...], ci[...] = _scan_block(bu_ref, s_ref, lp, lp, cr[...], ci[...], False)

    return pl.pallas_call(
        body, name="s5_scan_fwd", grid=(S5_NJ, n_seq, n_t),
        in_specs=[pl.BlockSpec((S5_TB, blk), lambda j, b, t: (b * n_t + t, j)),
                  pl.BlockSpec((S5_LOG_TB, blk), lambda j, b, t: (0, j))],
        out_specs=pl.BlockSpec((S5_TB, blk), lambda j, b, t: (b * n_t + t, j)),
        out_shape=jax.ShapeDtypeStruct(bu.shape, F32),
        scratch_shapes=[pltpu.VMEM((1, S5_LB), F32), pltpu.VMEM((1, S5_LB), F32)],
        compiler_params=_cparams(("parallel", "parallel", "arbitrary")),
    )(bu, lam_pow)


def _s5_scan_bwd(ds, s, lam_pow_conj, n_seq, seq_len):
    n_t = seq_len // S5_TB
    blk = 2 * S5_LB
    halo_per_blk = S5_TB // 8

    def rows(j, b, t):
        return (b * n_t + (n_t - 1 - t), j)

    def halo(j, b, t):
        return (jnp.maximum((b * n_t + (n_t - 1 - t)) * halo_per_blk - 1, 0), j)

    def body(ds_ref, s_ref, h_ref, lp_ref, g_ref, dl_ref, cr, ci, buf):
        b, t = pl.program_id(1), pl.program_id(2)

        @pl.when(t == 0)
        def _():
            cr[...] = jnp.zeros_like(cr)
            ci[...] = jnp.zeros_like(ci)

        @pl.when((b == 0) & (t == 0))
        def _():
            dl_ref[...] = jnp.zeros_like(dl_ref)

        first_blk = t == n_t - 1
        sub = lax.broadcasted_iota(jnp.int32, (8, S5_LB), 0)
        acc = [jnp.zeros((8, S5_LB), F32), jnp.zeros((8, S5_LB), F32)]

        def each(r, gr, gi):
            rows = slice(8 * r, 8 * r + 8)
            if r == 0:
                before_r = jnp.where(first_blk, 0.0, h_ref[7:8, :S5_LB])
                before_i = jnp.where(first_blk, 0.0, h_ref[7:8, S5_LB:])
            else:
                before_r, before_i = s_ref[8 * r - 1:8 * r, :S5_LB], s_ref[8 * r - 1:8 * r, S5_LB:]
            sp_r = jnp.where(sub == 0, before_r, pltpu.roll(s_ref[rows, :S5_LB], 1, 0))
            sp_i = jnp.where(sub == 0, before_i, pltpu.roll(s_ref[rows, S5_LB:], 1, 0))
            acc[0] = acc[0] + gr * sp_r + gi * sp_i
            acc[1] = acc[1] + gi * sp_r - gr * sp_i

        lp = lp_ref[...]
        edge_pow = jnp.concatenate([lp[7 - i:8 - i, :] for i in range(8)], axis=0)
        cr[...], ci[...] = _scan_block(ds_ref, buf, lp, edge_pow, cr[...], ci[...], True, each)
        g_ref[...] = buf[...].astype(g_ref.dtype)
        dl_ref[:, :S5_LB] += jnp.sum(acc[0], axis=0, keepdims=True)
        dl_ref[:, S5_LB:] += jnp.sum(acc[1], axis=0, keepdims=True)

    return pl.pallas_call(
        body, name="s5_scan_bwd", grid=(S5_NJ, n_seq, n_t),
        in_specs=[pl.BlockSpec((S5_TB, blk), rows), pl.BlockSpec((S5_TB, blk), rows),
                  pl.BlockSpec((8, blk), halo), pl.BlockSpec((S5_LOG_TB, blk), lambda j, b, t: (0, j))],
        out_specs=[pl.BlockSpec((S5_TB, blk), rows), pl.BlockSpec((1, blk), lambda j, b, t: (0, j))],
        out_shape=[jax.ShapeDtypeStruct(ds.shape, BF16), jax.ShapeDtypeStruct((1, ds.shape[1]), F32)],
        scratch_shapes=[pltpu.VMEM((1, S5_LB), F32), pltpu.VMEM((1, S5_LB), F32), pltpu.VMEM((S5_TB, blk), F32)],
        compiler_params=_cparams(("parallel", "arbitrary", "arbitrary")),
    )(ds, s, s, lam_pow_conj)


CONV_TR = 512
CONV_CW = 512


def _shift_down(x, halo, k):
    if k == 0:
        return x
    row8 = lax.broadcasted_iota(jnp.int32, halo.shape, 0)
    rolled = pltpu.roll(x, k, 0)
    top = jnp.where(row8 < k, pltpu.roll(halo, k, 0), rolled[:8])
    if x.shape[0] == 8:
        return top
    return jnp.concatenate([top, rolled[8:]], axis=0)


def _shift_up(x, halo, k):
    if k == 0:
        return x
    n = x.shape[0]
    row8 = lax.broadcasted_iota(jnp.int32, halo.shape, 0)
    rolled = pltpu.roll(x, n - k, 0)
    bot = jnp.where(row8 >= 8 - k, pltpu.roll(halo, 8 - k, 0), rolled[n - 8:])
    if n == 8:
        return bot
    return jnp.concatenate([rolled[:n - 8], bot], axis=0)


def _conv_pre(x, halo, w, b):
    acc = b + w[SSD_CONV - 1:SSD_CONV, :] * x
    for k in range(SSD_CONV - 1):
        acc = acc + w[k:k + 1, :] * _shift_down(x, halo, SSD_CONV - 1 - k)
    return acc


def _conv_specs(seq_len, col_off):
    lt = seq_len // CONV_TR
    cb = col_off // CONV_CW
    cur = pl.BlockSpec((CONV_TR, CONV_CW), lambda j, i: (i, j + cb))
    prev = pl.BlockSpec((8, CONV_CW), lambda j, i: (jnp.maximum(i * (CONV_TR // 8) - 1, 0), j + cb))
    return lt, cur, prev


def _conv_fwd(proj, conv_w, conv_b, n_rows, seq_len):
    lt, cur, prev = _conv_specs(seq_len, OFF_XBC)

    def body(x_ref, h_ref, w_ref, b_ref, o_ref):
        halo = jnp.where(pl.program_id(1) % lt == 0, 0.0, h_ref[...])
        o_ref[...] = jax.nn.silu(_conv_pre(x_ref[...], halo, w_ref[...], b_ref[...]))

    return pl.pallas_call(
        body, name="ssd_conv_fwd", grid=(SSD_CONV_DIM // CONV_CW, n_rows // CONV_TR),
        in_specs=[cur, prev, pl.BlockSpec((SSD_CONV, CONV_CW), lambda j, i: (0, j)),
                  pl.BlockSpec((1, CONV_CW), lambda j, i: (0, j))],
        out_specs=pl.BlockSpec((CONV_TR, CONV_CW), lambda j, i: (i, j)),
        out_shape=jax.ShapeDtypeStruct((n_rows, SSD_CONV_DIM), F32),
        compiler_params=_cparams(("parallel", "parallel")),
    )(proj, proj, conv_w, conv_b)


def _conv_bwd(name, proj, d_act, conv_w, conv_b, n_rows, seq_len, col_off):
    width = d_act.shape[1]
    lt, cur, prev = _conv_specs(seq_len, OFF_XBC + col_off)
    n_blk = n_rows // CONV_TR
    cb = (OFF_XBC + col_off) // CONV_CW
    pb = col_off // CONV_CW
    nxt = pl.BlockSpec((8, CONV_CW), lambda j, i: (jnp.minimum((i + 1) * (CONV_TR // 8), n_rows // 8 - 1), j + cb))
    d_cur = pl.BlockSpec((CONV_TR, CONV_CW), lambda j, i: (i, j))
    d_nxt = pl.BlockSpec((8, CONV_CW), lambda j, i: (jnp.minimum((i + 1) * (CONV_TR // 8), n_rows // 8 - 1), j))

    def dsilu(pre):
        sg = jax.nn.sigmoid(pre)
        return sg * (1.0 + pre * (1.0 - sg))

    def body(x_ref, hp_ref, hn_ref, d_ref, dn_ref, w_ref, b_ref, dx_ref, dw_ref, db_ref):
        i = pl.program_id(1)
        x, w, b = x_ref[...], w_ref[...], b_ref[...]
        halo_p = jnp.where(i % lt == 0, 0.0, hp_ref[...])
        at_end = i % lt == lt - 1
        dpre = d_ref[...] * dsilu(_conv_pre(x, halo_p, w, b))
        pre_n = _conv_pre(hn_ref[...], x[CONV_TR - 8:, :], w, b)
        dpre_n = jnp.where(at_end, 0.0, dn_ref[...] * dsilu(pre_n))
        dx = w[SSD_CONV - 1:SSD_CONV, :] * dpre
        for k in range(SSD_CONV - 1):
            dx = dx + w[k:k + 1, :] * _shift_up(dpre, dpre_n, SSD_CONV - 1 - k)
        dx_ref[...] = dx.astype(dx_ref.dtype)

        @pl.when(i == 0)
        def _():
            dw_ref[...] = jnp.zeros_like(dw_ref)
            db_ref[...] = jnp.zeros_like(db_ref)

        for k in range(SSD_CONV):
            xs = _shift_down(x, halo_p, SSD_CONV - 1 - k)
            dw_ref[k:k + 1, :] += jnp.sum(dpre * xs, axis=0, keepdims=True)
        db_ref[...] += jnp.sum(dpre, axis=0, keepdims=True)

    return pl.pallas_call(
        body, name=name, grid=(width // CONV_CW, n_blk),
        in_specs=[cur, prev, nxt, d_cur, d_nxt,
                  pl.BlockSpec((SSD_CONV, CONV_CW), lambda j, i: (0, j + pb)),
                  pl.BlockSpec((1, CONV_CW), lambda j, i: (0, j + pb))],
        out_specs=[pl.BlockSpec((CONV_TR, CONV_CW), lambda j, i: (i, j)),
                   pl.BlockSpec((SSD_CONV, CONV_CW), lambda j, i: (0, j)),
                   pl.BlockSpec((1, CONV_CW), lambda j, i: (0, j))],
        out_shape=[jax.ShapeDtypeStruct((n_rows, width), BF16),
                   jax.ShapeDtypeStruct((SSD_CONV, width), F32), jax.ShapeDtypeStruct((1, width), F32)],
        compiler_params=_cparams(("parallel", "arbitrary")),
    )(proj, proj, proj, d_act, d_act, conv_w, conv_b)


def _split3(x):
    hi = x.astype(BF16)
    r = x - hi.astype(F32)
    mid = r.astype(BF16)
    return hi, mid, (r - mid.astype(F32)).astype(BF16)


def _sel_dot(a, b, a_is_sel):
    dn = (((1,), (0,)), ((), ()))
    if a_is_sel:
        return sum(lax.dot_general(a, t, dn, preferred_element_type=F32) for t in _split3(b))
    return sum(lax.dot_general(t, b, dn, preferred_element_type=F32) for t in _split3(a))


@jax.custom_vjp
def sel_left(sel, sel_t, x):
    return _sel_dot(sel, x, True)


@jax.custom_vjp
def sel_right(x, sel, sel_t):
    return _sel_dot(x, sel, False)


sel_left.defvjp(lambda s, st, x: (_sel_dot(s, x, True), (s, st)),
                lambda r, g: (jnp.zeros_like(r[0]), jnp.zeros_like(r[1]), _sel_dot(r[1], g, True)))
sel_right.defvjp(lambda x, s, st: (_sel_dot(x, s, False), (s, st)),
                 lambda r, g: (_sel_dot(g, r[1], False), jnp.zeros_like(r[0]), jnp.zeros_like(r[1])))


def _ssd_chunk(xs, bm, cm, dtr, st, dtb, alog, dsk, k):
    dt = jax.nn.softplus(dtr + dtb)
    acum = sel_left(k["tri"], k["tri_t"], dt * (-jnp.exp(alog)))
    dt_e = sel_right(dt, k["spread"], k["spread_t"])
    ac_e = sel_right(acum, k["spread"], k["spread_t"])
    al_e = ac_e[SSD_CHUNK - 1:SSD_CHUNK, :]
    dsk_e = sel_right(jnp.broadcast_to(dsk, (8, 128)), k["spread"], k["spread_t"])[0:1, :]
    xdt = xs * dt_e
    acum_t = acum.T
    scores = dot_nt(cm, bm)
    y = dot_nn(cm, st) * jnp.exp(ac_e) + xs * dsk_e
    for j in range(SSD_HPG):
        lmat = jnp.exp(jnp.where(k["causal"], acum[:, j:j + 1] - acum_t[j:j + 1, :], -jnp.inf))
        y = y + dot_nn(scores * lmat, jnp.where(k["head"] == j, xdt, 0.0))
    new = st * jnp.exp(al_e) + dot_tn(bm, xdt * jnp.exp(al_e - ac_e))
    return y, new


def _ssd_consts():
    r = lax.broadcasted_iota(jnp.int32, (SSD_CHUNK, SSD_CHUNK), 0)
    c = lax.broadcasted_iota(jnp.int32, (SSD_CHUNK, SSD_CHUNK), 1)
    hd = jnp.int32(SSD_HEADDIM)
    sr = lax.broadcasted_iota(jnp.int32, (128, GROUP_W), 0)
    sc = lax.div(lax.broadcasted_iota(jnp.int32, (128, GROUP_W), 1), hd)
    tr = lax.div(lax.broadcasted_iota(jnp.int32, (GROUP_W, 128), 0), hd)
    tc = lax.broadcasted_iota(jnp.int32, (GROUP_W, 128), 1)
    return {"tri": (r >= c).astype(BF16), "tri_t": (c >= r).astype(BF16), "causal": r >= c,
            "spread": (sr == sc).astype(BF16), "spread_t": (tr == tc).astype(BF16),
            "head": lax.div(lax.broadcasted_iota(jnp.int32, (SSD_CHUNK, GROUP_W), 1), hd)}


def _ssd_specs(n_c, reverse):
    def cidx(c):
        return n_c - 1 - c if reverse else c

    xs = pl.BlockSpec((SSD_CHUNK, GROUP_W), lambda g, b, c: (b * n_c + cidx(c), g))
    bm = pl.BlockSpec((SSD_CHUNK, SSD_STATE), lambda g, b, c: (b * n_c + cidx(c), SSD_WIDTH // SSD_STATE + g))
    cm = pl.BlockSpec((SSD_CHUNK, SSD_STATE), lambda g, b, c: (b * n_c + cidx(c), (SSD_WIDTH + SSD_BC) // SSD_STATE + g))
    dt = pl.BlockSpec((SSD_CHUNK, 128), lambda g, b, c: (b * n_c + cidx(c), OFF_DT // 128 + g))
    par = pl.BlockSpec((1, 128), lambda g, b, c: (0, g))
    st = pl.BlockSpec((1, 1, 1, SSD_STATE, GROUP_W), lambda g, b, c: (b, cidx(c), g, 0, 0))
    return xs, bm, cm, dt, par, st


def _ssd_fwd(xbc_act, proj, dtb, alog, dsk, n_seq, seq_len):
    n_c = seq_len // SSD_CHUNK
    xs_s, bm_s, cm_s, dt_s, par_s, st_s = _ssd_specs(n_c, False)

    def body(xs_ref, bm_ref, cm_ref, dt_ref, dtb_ref, al_ref, dk_ref, y_ref, st_ref, state):
        @pl.when(pl.program_id(2) == 0)
        def _():
            state[...] = jnp.zeros_like(state)

        prev = state[...]
        st_ref[0, 0, 0] = prev
        y, new = _ssd_chunk(xs_ref[...], bm_ref[...], cm_ref[...], dt_ref[...], prev,
                            dtb_ref[...], al_ref[...], dk_ref[...], _ssd_consts())
        y_ref[...] = y
        state[...] = new

    return pl.pallas_call(
        body, name="ssd_fwd", grid=(SSD_GROUPS, n_seq, n_c),
        in_specs=[xs_s, bm_s, cm_s, dt_s, par_s, par_s, par_s],
        out_specs=[pl.BlockSpec((SSD_CHUNK, GROUP_W), lambda g, b, c: (b * n_c + c, g)), st_s],
        out_shape=[jax.ShapeDtypeStruct((n_seq * seq_len, SSD_WIDTH), F32),
                   jax.ShapeDtypeStruct((n_seq, n_c, SSD_GROUPS, SSD_STATE, GROUP_W), F32)],
        scratch_shapes=[pltpu.VMEM((SSD_STATE, GROUP_W), F32)],
        compiler_params=_cparams(("parallel", "parallel", "arbitrary")),
    )(xbc_act, xbc_act, xbc_act, proj, dtb, alog, dsk)


def _ssd_bwd(xbc_act, proj, states, dy, dtb, alog, dsk, n_seq, seq_len):
    n_c = seq_len // SSD_CHUNK
    n_rows = n_seq * seq_len
    xs_s, bm_s, cm_s, dt_s, par_s, st_s = _ssd_specs(n_c, True)

    def rows(w):
        return pl.BlockSpec((SSD_CHUNK, w), lambda g, b, c: (b * n_c + (n_c - 1 - c), g))

    def body(xs_ref, bm_ref, cm_ref, dt_ref, st_ref, dy_ref, dtb_ref, al_ref, dk_ref,
             dxs_ref, dbm_ref, dcm_ref, ddt_ref, ddtb_ref, dal_ref, ddk_ref, dstate):
        b, c = pl.program_id(1), pl.program_id(2)

        @pl.when(c == 0)
        def _():
            dstate[...] = jnp.zeros_like(dstate)

        @pl.when((b == 0) & (c == 0))
        def _():
            ddtb_ref[...] = jnp.zeros_like(ddtb_ref)
            dal_ref[...] = jnp.zeros_like(dal_ref)
            ddk_ref[...] = jnp.zeros_like(ddk_ref)

        consts = _ssd_consts()
        _, vjp = jax.vjp(
            lambda xs, bm, cm, dtr, prev, dtb, alog, dsk: _ssd_chunk(xs, bm, cm, dtr, prev, dtb, alog, dsk, consts),
            xs_ref[...], bm_ref[...], cm_ref[...], dt_ref[...], st_ref[0, 0, 0], dtb_ref[...], al_ref[...], dk_ref[...])
        dxs, dbm, dcm, ddtr, dprev, ddtb, dal, ddk = vjp((dy_ref[...], dstate[...]))
        dxs_ref[...] = dxs
        dbm_ref[...] = dbm
        dcm_ref[...] = dcm
        ddt_ref[...] = ddtr.astype(ddt_ref.dtype)
        ddtb_ref[...] += ddtb
        dal_ref[...] += dal
        ddk_ref[...] += ddk
        dstate[...] = dprev

    acc = pl.BlockSpec((1, 128), lambda g, b, c: (0, g))
    return pl.pallas_call(
        body, name="ssd_bwd", grid=(SSD_GROUPS, n_seq, n_c),
        in_specs=[xs_s, bm_s, cm_s, dt_s, st_s, rows(GROUP_W), par_s, par_s, par_s],
        out_specs=[rows(GROUP_W), rows(SSD_STATE), rows(SSD_STATE), rows(128), acc, acc, acc],
        out_shape=[jax.ShapeDtypeStruct((n_rows, SSD_WIDTH), F32), jax.ShapeDtypeStruct((n_rows, SSD_BC), F32),
                   jax.ShapeDtypeStruct((n_rows, SSD_BC), F32), jax.ShapeDtypeStruct((n_rows, 4 * 128), BF16),
                   jax.ShapeDtypeStruct((1, 512), F32), jax.ShapeDtypeStruct((1, 512), F32),
                   jax.ShapeDtypeStruct((1, 512), F32)],
        scratch_shapes=[pltpu.VMEM((SSD_STATE, GROUP_W), F32)],
        compiler_params=_cparams(("parallel", "arbitrary", "arbitrary")),
    )(xbc_act, xbc_act, xbc_act, proj, states, dy, dtb, alog, dsk)


def _pad_heads(v):
    return jnp.pad(v.reshape(SSD_GROUPS, SSD_HPG), ((0, 0), (0, 128 - SSD_HPG))).reshape(1, SSD_GROUPS * 128)


def _unpad_heads(v):
    return v.reshape(SSD_GROUPS, 128)[:, :SSD_HPG].reshape(1, SSD_HEADS)


def _state_cols(v):
    re, im = v
    lead = re.shape[:-1]
    re = re.reshape(lead + (S5_NJ, 1, S5_LB))
    im = im.reshape(lead + (S5_NJ, 1, S5_LB))
    return jnp.concatenate([re, im], axis=-2).reshape(lead + (2 * S5_N,))


def _state_uncols(v):
    lead = v.shape[:-1]
    v = v.reshape(lead + (S5_NJ, 2, S5_LB))
    return v[..., 0, :].reshape(lead + (S5_N,)), v[..., 1, :].reshape(lead + (S5_N,))


GROUPS_PER_BAND = BAND // S5_GROUP


def _band(w2_re, w2_im):
    gh = S5_GROUPS * S5_GROUP
    rg = ((jnp.arange(gh) // S5_GROUP) % GROUPS_PER_BAND)[:, None, None]
    cg = jnp.arange(GROUPS_PER_BAND)[None, :, None]
    parts = [jnp.where(rg == cg, v[:, None, :], 0.0).reshape(gh, S5_LB) for v in (w2_re, w2_im)]
    return jnp.concatenate(parts, axis=1)


def _band_take(wb):
    gh = S5_GROUPS * S5_GROUP
    w4 = wb.reshape(gh, 2, GROUPS_PER_BAND, S5_STATE)
    sel = w4[jnp.arange(gh), :, (jnp.arange(gh) // S5_GROUP) % GROUPS_PER_BAND, :]
    return sel[:, 0, :], sel[:, 1, :]


W_IN_SHARD = IN_PROJ_DIM // N_CHIPS
W_IN_SEGS = ((0, 512, OFF_U5), (512, 1024, OFF_Z5), (1024, 2560, OFF_ZS), (2560, 5120, OFF_XBC), (5144, 7192, OFF_G5))
DT_ROWS = (5120, 5144)


def _w_in_pieces():
    runs = []
    segs = list(W_IN_SEGS) + [(DT_ROWS[0] + SSD_HPG * g, DT_ROWS[0] + SSD_HPG * (g + 1), OFF_DT + 128 * g)
                              for g in range(SSD_GROUPS)]
    for lo, hi, off in segs:
        for j in range(N_CHIPS):
            s, e = max(lo, j * W_IN_SHARD), min(hi, (j + 1) * W_IN_SHARD)
            if s < e:
                runs.append((j, s - j * W_IN_SHARD, off + s - lo, e - s))
    return runs


RELAYOUT_LANES = 256


def _pad_w_in_t(a4):
    runs = _w_in_pieces()

    def body(a_ref, o_ref):
        o_ref[pl.ds(OFF_DT, PROJ_W - OFF_DT), :] = jnp.zeros((PROJ_W - OFF_DT, RELAYOUT_LANES), o_ref.dtype)
        for j, src, dst, n in runs:
            o_ref[pl.ds(dst, n), :] = a_ref[j, pl.ds(src, n), :]

    return pl.pallas_call(
        body, name="w_in_to_padded", grid=(D_MODEL // RELAYOUT_LANES,),
        in_specs=[pl.BlockSpec((N_CHIPS, W_IN_SHARD, RELAYOUT_LANES), lambda i: (0, 0, i))],
        out_specs=pl.BlockSpec((PROJ_W, RELAYOUT_LANES), lambda i: (0, i)),
        out_shape=jax.ShapeDtypeStruct((PROJ_W, D_MODEL), a4.dtype),
        compiler_params=_cparams(("parallel",)),
    )(a4)


def _unpad_w_in_t(wp):
    runs = _w_in_pieces()

    def body(p_ref, o_ref):
        for j, dst, src, n in runs:
            o_ref[j, pl.ds(dst, n), :] = p_ref[pl.ds(src, n), :]

    return pl.pallas_call(
        body, name="w_in_from_padded", grid=(D_MODEL // RELAYOUT_LANES,),
        in_specs=[pl.BlockSpec((PROJ_W, RELAYOUT_LANES), lambda i: (0, i))],
        out_specs=pl.BlockSpec((N_CHIPS, W_IN_SHARD, RELAYOUT_LANES), lambda i: (0, 0, i)),
        out_shape=jax.ShapeDtypeStruct((N_CHIPS, W_IN_SHARD, D_MODEL), wp.dtype),
        compiler_params=_cparams(("parallel",)),
    )(wp)


def _local_step(x, p, tgt, w):
    n_seq, seq_len, _ = x.shape
    n_rows = n_seq * seq_len
    tr = 512
    x2 = x.reshape(n_rows, D_MODEL)
    p2 = p.reshape(n_rows, PLE_DIM)
    t2 = tgt.reshape(n_rows, D_MODEL)
    row = functools.partial(_rowwise, n_rows=n_rows, tr=tr)

    w_pad_t = _pad_w_in_t(w["w_in_t"])
    norm_w = w["norm_w"].reshape(1, D_MODEL)
    ple_norm_w = w["ple_norm_w"].reshape(1, D_MODEL)
    final_w = w["final_norm_w"].reshape(1, D_MODEL)
    s5_d = w["s5_d"].reshape(1, S5_WIDTH)
    b_glu = w["s5_b_glu"].reshape(1, S5_WIDTH)
    conv_w = w["ssd_conv_w"].reshape(SSD_CONV, SSD_CONV_DIM)
    conv_b = w["ssd_conv_b"].reshape(1, SSD_CONV_DIM)
    ssd_norm_w = w["ssd_norm_w"].reshape(1, SSD_WIDTH)
    dtb, alog, dsk = (_pad_heads(w[k].reshape(1, SSD_HEADS)) for k in ("ssd_dt_bias", "ssd_a_log", "ssd_d"))

    gh = S5_GROUPS * S5_GROUP
    a_re = w["s5_a_re"].reshape(S5_GROUPS, S5_STATE)
    a_im = w["s5_a_im"].reshape(S5_GROUPS, S5_STATE)
    log_step = w["s5_log_step"].reshape(S5_GROUPS, 1)
    b_re2 = jnp.transpose(w["s5_b_re"].reshape(S5_GROUPS, S5_STATE, S5_GROUP), (0, 2, 1)).reshape(gh, S5_STATE)
    b_im2 = jnp.transpose(w["s5_b_im"].reshape(S5_GROUPS, S5_STATE, S5_GROUP), (0, 2, 1)).reshape(gh, S5_STATE)
    expand = (jnp.arange(gh)[:, None] // S5_GROUP == jnp.arange(S5_GROUPS)[None, :]).astype(F32)
    pow_re, pow_im, bb_re2, bb_im2 = _s5_params_fwd(a_re, a_im, log_step, b_re2, b_im2, expand)
    lam_pow = _state_cols((pow_re.reshape(S5_LOG_TB, S5_N), pow_im.reshape(S5_LOG_TB, S5_N)))
    lam_pow_conj = _state_cols((pow_re.reshape(S5_LOG_TB, S5_N), -pow_im.reshape(S5_LOG_TB, S5_N)))
    bb_band = _band(bb_re2, bb_im2).astype(BF16)
    c_band = _band(w["s5_c_re"].reshape(gh, S5_STATE), -w["s5_c_im"].reshape(gh, S5_STATE)).astype(BF16)

    (hn,) = row("rms_in", lambda r, q: ([_rms(r[0], q[0])], []), row_ins=[(x2, 0, D_MODEL)],
                par_ins=[(norm_w, 0, D_MODEL)], row_outs=[(D_MODEL, BF16)])
    proj = _matmul("mm_proj", hn, w_pad_t, tb=True)
    bu = _band_matmul("mm_s5_bu", "nn", proj, bb_band, a_blk0=OFF_U5 // BAND)
    s = _s5_scan_fwd(bu, lam_pow, n_seq, seq_len)
    yc = _band_matmul("mm_s5_y", "nt", s, c_band)
    (ge,) = row("s5_gelu", lambda r, q: ([jax.nn.gelu(r[0] + q[0] * r[1])], []),
                row_ins=[(yc, 0, S5_WIDTH), (proj, OFF_U5, S5_WIDTH)], par_ins=[(s5_d, 0, S5_WIDTH)],
                row_outs=[(S5_WIDTH, BF16)])
    tg = _matmul("mm_s5_glu", ge, w["s5_w_glu"])
    s5_rows = [(yc, 0, S5_WIDTH), (proj, OFF_U5, S5_WIDTH), (proj, OFF_Z5, S5_WIDTH), (tg, 0, S5_WIDTH)]
    s5_pars = [(s5_d, 0, S5_WIDTH), (b_glu, 0, S5_WIDTH)]
    (y5,) = row("s5_out", lambda r, q: ([_s5_out(*r, *q)], []), row_ins=s5_rows, par_ins=s5_pars,
                row_outs=[(S5_WIDTH, BF16)])

    xbc_act = _conv_fwd(proj, conv_w, conv_b, n_rows, seq_len)
    y_ssd, states = _ssd_fwd(xbc_act, proj, dtb, alog, dsk, n_seq, seq_len)
    gn_rows = [(y_ssd, 0, SSD_WIDTH), (proj, OFF_ZS, SSD_WIDTH)]
    (yss,) = row("ssd_gate", lambda r, q: ([_gated_norm(r[0], r[1], q[0])], []), row_ins=gn_rows,
                 par_ins=[(ssd_norm_w, 0, SSD_WIDTH)], row_outs=[(SSD_WIDTH, BF16)])

    m5 = _matmul("mm_br_s5", y5, w["w_br_s5"])
    ms = _matmul("mm_br_ssd", yss, w["w_br_ssd"])
    mg_rows = [(proj, OFF_G5, D_MODEL), (proj, OFF_GS, D_MODEL), (m5, 0, D_MODEL), (ms, 0, D_MODEL)]
    (merged,) = row("merge", lambda r, q: ([_merge(*r)], []), row_ins=mg_rows, par_ins=[], row_outs=[(D_MODEL, BF16)])
    o = _matmul("mm_out", merged, w["w_out"])

    def resid_fn(r, q):
        h1 = r[0] + r[1]
        return [h1, _rms(h1, q[0])], []

    h1, hp = row("resid_ple_norm", resid_fn, row_ins=[(x2, 0, D_MODEL), (o, 0, D_MODEL)],
                 par_ins=[(ple_norm_w, 0, D_MODEL)], row_outs=[(D_MODEL, F32), (D_MODEL, BF16)])
    pgl = _matmul("mm_ple_gate", hp, w["w_ple_gate"])
    pp = _matmul("mm_ple_proj", p2, w["w_ple_proj"])

    def head_fn(r, q):
        h1_, pgl_, pp_, tgt_ = r
        loss, vjp = jax.vjp(lambda a, b, c, f: _head_loss(a, b, c, f, tgt_), h1_, pgl_, pp_, q[0])
        dh1_, dpgl_, dpp_, dfw_ = vjp(jnp.ones_like(loss))
        return [dh1_, dpgl_, dpp_], [loss, dfw_]

    dh2, dpgl, dpp, loss_acc, d_final_w = row(
        "head", head_fn, row_ins=[(h1, 0, D_MODEL), (pgl, 0, D_MODEL), (pp, 0, D_MODEL), (t2, 0, D_MODEL)],
        par_ins=[(final_w, 0, D_MODEL)], row_outs=[(D_MODEL, F32), (D_MODEL, BF16), (D_MODEL, BF16)],
        acc_outs=[(1, 128), (1, D_MODEL)])
    loss = loss_acc[0, 0]

    g = {}
    g["final_norm_w"] = d_final_w
    g["w_ple_gate"] = _matmul("mm_d_w_ple_gate", hp, dpgl, ta=True)
    g["w_ple_proj"] = _matmul("mm_d_w_ple_proj", p2, dpp, ta=True)
    dhp = _matmul("mm_d_hp", dpgl, w["w_ple_gate"], tb=True)

    def ple_norm_bwd(r, q):
        h1_, dhp_, dh2_ = r
        _, vjp = jax.vjp(_rms, h1_, q[0])
        dh, dw = vjp(dhp_)
        dh = dh + dh2_
        return [dh, dh], [dw]

    dh1, dh1_b, g["ple_norm_w"] = row(
        "ple_norm_bwd", ple_norm_bwd, row_ins=[(h1, 0, D_MODEL), (dhp, 0, D_MODEL), (dh2, 0, D_MODEL)],
        par_ins=[(ple_norm_w, 0, D_MODEL)], row_outs=[(D_MODEL, F32), (D_MODEL, BF16)], acc_outs=[(1, D_MODEL)])
    g["w_out"] = _matmul("mm_d_w_out", merged, dh1_b, ta=True)
    dmerged = _matmul("mm_d_merged", dh1_b, w["w_out"], tb=True)

    def merge_bwd(r, q):
        _, vjp = jax.vjp(_merge, *r[:4])
        return list(vjp(r[4])), []

    dg5, dgs, dm5, dms = row("merge_bwd", merge_bwd, row_ins=mg_rows + [(dmerged, 0, D_MODEL)], par_ins=[],
                             row_outs=[(D_MODEL, BF16)] * 4)
    g["w_br_s5"] = _matmul("mm_d_w_br_s5", y5, dm5, ta=True)
    g["w_br_ssd"] = _matmul("mm_d_w_br_ssd", yss, dms, ta=True)
    dy5 = _matmul("mm_d_y5", dm5, w["w_br_s5"], tb=True)
    dyss = _matmul("mm_d_yss", dms, w["w_br_ssd"], tb=True)

    def s5_out_bwd_a(r, q):
        yc_, u_, z_, t_, dy_ = r
        d_, bg_ = q
        ge_ = jax.nn.gelu(yc_ + d_ * u_)
        _, vjp = jax.vjp(lambda a, z, t, b: a * jax.nn.sigmoid(t + b) * jax.nn.silu(z), ge_, z_, t_, bg_)
        dge, dz, dt_, dbg = vjp(dy_)
        return [dge, dz, dt_], [dbg]

    dge_a, dz5, dtg, g["s5_b_glu"] = row(
        "s5_out_bwd_a", s5_out_bwd_a, row_ins=s5_rows + [(dy5, 0, S5_WIDTH)], par_ins=s5_pars,
        row_outs=[(S5_WIDTH, F32), (S5_WIDTH, BF16), (S5_WIDTH, BF16)], acc_outs=[(1, S5_WIDTH)])
    g["s5_w_glu"] = _matmul("mm_d_w_glu", ge, dtg, ta=True)
    dge_b = _matmul("mm_d_ge", dtg, w["s5_w_glu"], tb=True)

    def s5_out_bwd_b(r, q):
        yc_, u_, da_, db_ = r
        _, vjp = jax.vjp(lambda yc, u, d: jax.nn.gelu(yc + d * u), yc_, u_, q[0])
        dyc_, du_, dd_ = vjp(da_ + db_)
        return [dyc_, du_], [dd_]

    dyc, du5_a, g["s5_d"] = row(
        "s5_out_bwd_b", s5_out_bwd_b,
        row_ins=[(yc, 0, S5_WIDTH), (proj, OFF_U5, S5_WIDTH), (dge_a, 0, S5_WIDTH), (dge_b, 0, S5_WIDTH)],
        par_ins=[(s5_d, 0, S5_WIDTH)], row_outs=[(S5_WIDTH, BF16), (S5_WIDTH, F32)], acc_outs=[(1, S5_WIDTH)])
    d_c_band = _band_matmul("mm_d_c", "tn", dyc, s)
    ds = _band_matmul("mm_d_s", "nn", dyc, c_band)
    dbu, d_lam = _s5_scan_bwd(ds, s, lam_pow_conj, n_seq, seq_len)
    d_bb_band = _band_matmul("mm_d_bb", "tn", proj, dbu, a_blk0=OFF_U5 // BAND)
    du5_b = _band_matmul("mm_d_u5", "nt", dbu, bb_band)
    (du5,) = row("s5_du", lambda r, q: ([r[0] + r[1]], []), row_ins=[(du5_a, 0, S5_WIDTH), (du5_b, 0, S5_WIDTH)],
                 par_ins=[], row_outs=[(S5_WIDTH, BF16)])

    d_lr, d_li = _state_uncols(d_lam)
    d_bbr, d_bbi = _band_take(d_bb_band)
    d_are, d_aim, d_ls, d_br2, d_bi2 = _s5_params_bwd(
        a_re, a_im, log_step, b_re2, b_im2, expand,
        d_lr.reshape(S5_GROUPS, S5_STATE), d_li.reshape(S5_GROUPS, S5_STATE), d_bbr, d_bbi)
    g["s5_a_re"], g["s5_a_im"], g["s5_log_step"] = d_are, d_aim, d_ls
    g["s5_b_re_ghp"], g["s5_b_im_ghp"] = d_br2, d_bi2
    d_cr, d_ci = _band_take(d_c_band)
    g["s5_c_re"], g["s5_c_im"] = d_cr, -d_ci

    def gate_bwd(r, q):
        _, vjp = jax.vjp(_gated_norm, r[0], r[1], q[0])
        dy_, dz_, dw_ = vjp(r[2])
        return [dy_, dz_], [dw_]

    dy_ssd, dzs, g["ssd_norm_w"] = row(
        "ssd_gate_bwd", gate_bwd, row_ins=gn_rows + [(dyss, 0, SSD_WIDTH)], par_ins=[(ssd_norm_w, 0, SSD_WIDTH)],
        row_outs=[(SSD_WIDTH, F32), (SSD_WIDTH, BF16)], acc_outs=[(1, SSD_WIDTH)])
    dxs, dbm, dcm, ddt, d_dtb, d_alog, d_dsk = _ssd_bwd(xbc_act, proj, states, dy_ssd, dtb, alog, dsk, n_seq, seq_len)
    g["ssd_dt_bias"], g["ssd_a_log"], g["ssd_d"] = _unpad_heads(d_dtb), _unpad_heads(d_alog), _unpad_heads(d_dsk)
    conv_parts = [_conv_bwd("ssd_conv_bwd_x", proj, dxs, conv_w, conv_b, n_rows, seq_len, 0),
                  _conv_bwd("ssd_conv_bwd_b", proj, dbm, conv_w, conv_b, n_rows, seq_len, SSD_WIDTH),
                  _conv_bwd("ssd_conv_bwd_c", proj, dcm, conv_w, conv_b, n_rows, seq_len, SSD_WIDTH + SSD_BC)]
    g["ssd_conv_w"] = jnp.concatenate([c[1] for c in conv_parts], axis=1)
    g["ssd_conv_b"] = jnp.concatenate([c[2] for c in conv_parts], axis=1)

    dproj = jnp.concatenate([c[0] for c in conv_parts] + [du5, dzs, dz5, dg5, dgs, ddt], axis=1)
    g["w_in_t"] = _unpad_w_in_t(_matmul("mm_d_w_in", dproj, hn, ta=True))
    dhn = _matmul("mm_d_hn", dproj, w_pad_t)

    def norm_bwd(r, q):
        x_, dhn_, dh1_ = r
        _, vjp = jax.vjp(_rms, x_, q[0])
        dx_, dw_ = vjp(dhn_)
        return [dx_ + dh1_], [dw_]

    dx, g["norm_w"] = row("rms_in_bwd", norm_bwd, row_ins=[(x2, 0, D_MODEL), (dhn, 0, D_MODEL), (dh1, 0, D_MODEL)],
                          par_ins=[(norm_w, 0, D_MODEL)], row_outs=[(D_MODEL, F32)], acc_outs=[(1, D_MODEL)])
    return loss, dx.reshape(x.shape), g


HBM = pl.BlockSpec(memory_space=pltpu.HBM)


def _chip_index(x, y):
    return 2 * x + y


def _half(shape2d, axis, which):
    h = shape2d[axis] // 2
    sl = pl.ds(pl.multiple_of(which * h, 128 if axis else 8), h)
    return (slice(None), sl) if axis else (sl, slice(None))


def _gather_chips(split, axes, whole):
    ns, nw = len(split), len(whole)
    n = ns + nw

    def body(*refs):
        ins, outs = refs[:n], refs[n:2 * n]
        ici_send, ici_recv, d2d_send, d2d_recv, local_sems = refs[2 * n:]
        x, y, c = lax.axis_index("x"), lax.axis_index("y"), lax.axis_index("c")
        me = _chip_index(x, y)
        sibling = (x, y, 1 - c)
        peers = [(1 - x, y), (x, 1 - y), (1 - x, 1 - y)]

        def half(t, which):
            return _half(split[t].shape, axes[t], which)

        copies = []
        for t in range(n):
            loc = pltpu.make_async_copy(ins[t], outs[t].at[me], local_sems.at[t])
            loc.start()
            copies.append(loc)

        def ici(t, k, slot):
            px, py = peers[k]
            if t < ns:
                src, dst = ins[t].at[half(t, c)], outs[t].at[(slot,) + half(t, c)]
            else:
                src, dst = ins[t], outs[t].at[slot]
            return pltpu.make_async_remote_copy(src_ref=src, dst_ref=dst, send_sem=ici_send.at[t, k],
                                                recv_sem=ici_recv.at[t, k], device_id=(px, py, c), device_id_type=MESH)

        def d2d(t, k, which):
            rows = outs[t].at[(_chip_index(*peers[k]),) + half(t, which)]
            return pltpu.make_async_remote_copy(src_ref=rows, dst_ref=rows, send_sem=d2d_send.at[t, k],
                                                recv_sem=d2d_recv.at[t, k], device_id=sibling, device_id_type=MESH)

        sends = []
        for t in range(n):
            for k in range(3):
                cp = ici(t, k, me)
                cp.start()
                sends.append(cp)
        for t in range(n):
            for k in range(3):
                ici(t, k, _chip_index(*peers[k])).wait_recv()
                if t < ns:
                    cp = d2d(t, k, c)
                    cp.start()
                    sends.append(cp)
        for t in range(ns):
            for k in range(3):
                d2d(t, k, 1 - c).wait_recv()
        for cp in sends:
            cp.wait_send()
        for cp in copies:
            cp.wait()

    arrays = list(split) + list(whole)
    return pl.pallas_call(
        body, name="gather_weights",
        in_specs=[HBM] * n, out_specs=[HBM] * n,
        out_shape=[jax.ShapeDtypeStruct((N_CHIPS,) + a.shape, a.dtype) for a in arrays],
        scratch_shapes=[pltpu.SemaphoreType.DMA((n, 3)), pltpu.SemaphoreType.DMA((n, 3)),
                        pltpu.SemaphoreType.DMA((ns, 3)), pltpu.SemaphoreType.DMA((ns, 3)),
                        pltpu.SemaphoreType.DMA((n,))],
    )(*arrays)


def _half_shape(shape2d, axis):
    r, c = shape2d
    return (r, c // 2) if axis else (r // 2, c)


def _swap_halves(slotted, axes, small):
    n = len(slotted)

    def body(*refs):
        ins, sm_in = refs[:n], refs[n]
        outs, sm_out = refs[n + 1:2 * n + 1], refs[2 * n + 1]
        send_sems, recv_sems, local_sem, sm_send, sm_recv = refs[2 * n + 2:]
        x, y, c = lax.axis_index("x"), lax.axis_index("y"), lax.axis_index("c")
        dev = 4 * x + 2 * y + c
        local = pltpu.make_async_copy(sm_in, sm_out.at[dev], local_sem)
        local.start()
        sends = []
        for t in range(n):
            other = (slice(None),) + _half(slotted[t].shape[1:], axes[t], 1 - c)
            cp = pltpu.make_async_remote_copy(
                src_ref=ins[t].at[other], dst_ref=outs[t], send_sem=send_sems.at[t],
                recv_sem=recv_sems.at[t], device_id=(x, y, 1 - c), device_id_type=MESH)
            cp.start()
            sends.append(cp)
        rel = [(fx, fy, fc) for fx in (0, 1) for fy in (0, 1) for fc in (0, 1)][1:]
        for k, (fx, fy, fc) in enumerate(rel):
            cp = pltpu.make_async_remote_copy(
                src_ref=sm_in, dst_ref=sm_out.at[dev], send_sem=sm_send.at[k], recv_sem=sm_recv.at[k],
                device_id=(x ^ fx, y ^ fy, c ^ fc), device_id_type=MESH)
            cp.start()
            sends.append(cp)
        for cp in sends[:n]:
            cp.wait_recv()
        for k, (fx, fy, fc) in enumerate(rel):
            src_dev = 4 * (x ^ fx) + 2 * (y ^ fy) + (c ^ fc)
            pltpu.make_async_remote_copy(
                src_ref=sm_in, dst_ref=sm_out.at[src_dev], send_sem=sm_send.at[k], recv_sem=sm_recv.at[k],
                device_id=(x ^ fx, y ^ fy, c ^ fc), device_id_type=MESH).wait_recv()
        for cp in sends:
            cp.wait_send()
        local.wait()

    return pl.pallas_call(
        body, name="swap_halves",
        in_specs=[HBM] * (n + 1), out_specs=[HBM] * (n + 1),
        out_shape=[jax.ShapeDtypeStruct((a.shape[0],) + _half_shape(a.shape[1:], ax), a.dtype)
                   for a, ax in zip(slotted, axes, strict=True)]
        + [jax.ShapeDtypeStruct((N_DEV,) + small.shape, small.dtype)],
        scratch_shapes=[pltpu.SemaphoreType.DMA((n,)), pltpu.SemaphoreType.DMA((n,)), pltpu.SemaphoreType.DMA,
                        pltpu.SemaphoreType.DMA((7,)), pltpu.SemaphoreType.DMA((7,))],
    )(*slotted, small)


def _scatter_halves(parts):
    n = len(parts)

    def body(*refs):
        ins, outs = refs[:n], refs[n:2 * n]
        send_sems, recv_sems = refs[2 * n:]
        x, y, c = lax.axis_index("x"), lax.axis_index("y"), lax.axis_index("c")
        peers = [(1 - x, y), (x, 1 - y), (1 - x, 1 - y)]
        sends = []
        for t in range(n):
            for k, (px, py) in enumerate(peers):
                cp = pltpu.make_async_remote_copy(
                    src_ref=ins[t].at[_chip_index(px, py)], dst_ref=outs[t].at[k], send_sem=send_sems.at[t, k],
                    recv_sem=recv_sems.at[t, k], device_id=(px, py, c), device_id_type=MESH)
                cp.start()
                sends.append(cp)
        for cp in sends:
            cp.wait_recv()
        for cp in sends:
            cp.wait_send()

    return pl.pallas_call(
        body, name="scatter_halves",
        in_specs=[HBM] * n, out_specs=[HBM] * n,
        out_shape=[jax.ShapeDtypeStruct((3,) + a.shape[1:], a.dtype) for a in parts],
        scratch_shapes=[pltpu.SemaphoreType.DMA((n, 3)), pltpu.SemaphoreType.DMA((n, 3))],
    )(*parts)


def _tiling(r, c, f32_per_elem):
    if r % 8 == 0:
        tr = _row_tile(r, f32_per_elem * c * 4)
        return r // tr, (tr, c), lambda i: (i, 0)
    assert c % 128 == 0, (r, c)
    return c // 128, (r, 128), lambda i: (0, i)


def _pair_sum(name, slotted, other, idx, axis):
    _, r, c = slotted.shape
    hr, hc = _half_shape((r, c), axis)
    n, (tr, tc), at = _tiling(hr, hc, 13)
    if axis == 0:
        a = slotted.reshape(N_CHIPS, 2, hr, c)
        a_all = pl.BlockSpec((N_CHIPS, 1, tr, tc), lambda i, s: (0, s[0]) + at(i))
        a_own = pl.BlockSpec((1, 1, tr, tc), lambda i, s: (s[1], s[0]) + at(i))
    else:
        a, per_half = slotted, hc // tc
        a_all = pl.BlockSpec((N_CHIPS, tr, tc), lambda i, s: (0, at(i)[0], s[0] * per_half + at(i)[1]))
        a_own = pl.BlockSpec((1, tr, tc), lambda i, s: (s[1], at(i)[0], s[0] * per_half + at(i)[1]))

    def body(idx_ref, a_ref, b_ref, am_ref, bm_ref, p_ref, own_ref):
        mine, mine_own = (a_ref[:, 0], am_ref[0, 0]) if axis == 0 else (a_ref[...], am_ref[0])
        p_ref[...] = (mine + b_ref[...]).astype(p_ref.dtype)
        own_ref[...] = mine_own + bm_ref[0]

    grid_spec = pltpu.PrefetchScalarGridSpec(
        num_scalar_prefetch=1, grid=(n,),
        in_specs=[a_all, pl.BlockSpec((N_CHIPS, tr, tc), lambda i, s: (0,) + at(i)),
                  a_own, pl.BlockSpec((1, tr, tc), lambda i, s: (s[1],) + at(i))],
        out_specs=[pl.BlockSpec((N_CHIPS, tr, tc), lambda i, s: (0,) + at(i)),
                   pl.BlockSpec((tr, tc), lambda i, s: at(i))])
    return pl.pallas_call(
        body, name=name, grid_spec=grid_spec,
        out_shape=[jax.ShapeDtypeStruct((N_CHIPS, hr, hc), BF16), jax.ShapeDtypeStruct((hr, hc), F32)],
        compiler_params=_cparams(("parallel",)),
    )(idx, a, other, a, other)


def _sum_parts(name, own, recv):
    h, c = own.shape
    n, (tr, tc), at = _tiling(h, c, 4)

    def body(o_ref, r_ref, out_ref):
        acc = o_ref[...]
        for k in range(3):
            acc = acc + r_ref[k].astype(F32)
        out_ref[...] = acc

    return pl.pallas_call(
        body, name=name, grid=(n,),
        in_specs=[pl.BlockSpec((tr, tc), at), pl.BlockSpec((3, tr, tc), lambda i: (0,) + at(i))],
        out_specs=pl.BlockSpec((tr, tc), at),
        out_shape=jax.ShapeDtypeStruct((h, c), F32),
        compiler_params=_cparams(("parallel",)),
    )(own, recv)


def _swap_sibling(parts):
    n = len(parts)

    def body(*refs):
        ins, outs = refs[:n], refs[n:2 * n]
        send_sems, recv_sems = refs[2 * n:]
        x, y, c = lax.axis_index("x"), lax.axis_index("y"), lax.axis_index("c")
        cps = []
        for t in range(n):
            cp = pltpu.make_async_remote_copy(
                src_ref=ins[t], dst_ref=outs[t], send_sem=send_sems.at[t], recv_sem=recv_sems.at[t],
                device_id=(x, y, 1 - c), device_id_type=MESH)
            cp.start()
            cps.append(cp)
        for cp in cps:
            cp.wait_recv()
        for cp in cps:
            cp.wait_send()

    return pl.pallas_call(
        body, name="swap_sibling",
        in_specs=[HBM] * n, out_specs=[HBM] * n,
        out_shape=[jax.ShapeDtypeStruct(a.shape, a.dtype) for a in parts],
        scratch_shapes=[pltpu.SemaphoreType.DMA((n,)), pltpu.SemaphoreType.DMA((n,))],
    )(*parts)


def _sum_slots(name, a):
    k, r, c = a.shape
    tr = _row_tile(r, (k + 1) * c * 4)

    def body(a_ref, o_ref):
        acc = a_ref[0]
        for i in range(1, k):
            acc = acc + a_ref[i]
        o_ref[...] = acc

    return pl.pallas_call(
        body, name=name, grid=(r // tr,),
        in_specs=[pl.BlockSpec((k, tr, c), lambda i: (0, i, 0))],
        out_specs=pl.BlockSpec((tr, c), lambda i: (i, 0)),
        out_shape=jax.ShapeDtypeStruct((r, c), a.dtype),
        compiler_params=_cparams(("parallel",)),
    )(a)


def _adamw(name, w, m, v, g_parts):
    r, c = w.shape
    ng = len(g_parts)
    tr = _row_tile(r, (7 + ng) * c * 4)
    c1 = 1.0 - ADAM_B1 ** ADAM_STEP
    c2 = 1.0 - ADAM_B2 ** ADAM_STEP

    def body(*refs):
        w_ref, m_ref, v_ref = refs[:3]
        g_refs = refs[3:3 + ng]
        go_ref, d_ref, mo_ref, vo_ref = refs[3 + ng:]
        g = g_refs[0][...]
        for gr in g_refs[1:]:
            g = g + gr[...]
        m_new = ADAM_B1 * m_ref[...] + (1.0 - ADAM_B1) * g
        v_new = ADAM_B2 * v_ref[...] + (1.0 - ADAM_B2) * (g * g)
        go_ref[...] = g
        mo_ref[...] = m_new
        vo_ref[...] = v_new
        d_ref[...] = -ADAM_LR * ((m_new / c1) / (jnp.sqrt(v_new / c2) + ADAM_EPS) + ADAM_WD * w_ref[...])

    spec = pl.BlockSpec((tr, c), lambda i: (i, 0))
    return pl.pallas_call(
        body, name=name, grid=(r // tr,),
        in_specs=[spec] * (3 + ng), out_specs=[spec] * 4,
        out_shape=[jax.ShapeDtypeStruct((r, c), F32)] * 4,
        compiler_params=_cparams(("parallel",)),
    )(w, m, v, *g_parts)


def _adamw_halves(name, w, m, v, own, other, idx, axis):
    hr, hc = own.shape
    nb, (tr, tc), at = _tiling(hr, hc, 9)
    c1 = 1.0 - ADAM_B1 ** ADAM_STEP
    c2 = 1.0 - ADAM_B2 ** ADAM_STEP

    def body(idx_ref, w_ref, m_ref, v_ref, own_ref, oth_ref, go_ref, d_ref, mo_ref, vo_ref):
        g = jnp.where(pl.program_id(0) == idx_ref[0], own_ref[...], oth_ref[...])
        m_new = ADAM_B1 * m_ref[...] + (1.0 - ADAM_B1) * g
        v_new = ADAM_B2 * v_ref[...] + (1.0 - ADAM_B2) * (g * g)
        go_ref[...] = g
        mo_ref[...] = m_new
        vo_ref[...] = v_new
        d_ref[...] = -ADAM_LR * ((m_new / c1) / (jnp.sqrt(v_new / c2) + ADAM_EPS) + ADAM_WD * w_ref[...])

    per_half = (hc // tc) if axis else (hr // tr)
    if axis:
        full = pl.BlockSpec((tr, tc), lambda hh, i, s: (at(i)[0], hh * per_half + at(i)[1]))
    else:
        full = pl.BlockSpec((tr, tc), lambda hh, i, s: (hh * per_half + at(i)[0], at(i)[1]))
    part = pl.BlockSpec((tr, tc), lambda hh, i, s: at(i))
    grid_spec = pltpu.PrefetchScalarGridSpec(
        num_scalar_prefetch=1, grid=(2, nb), in_specs=[full, full, full, part, part], out_specs=[full] * 4)
    return pl.pallas_call(
        body, name=name, grid_spec=grid_spec, out_shape=[jax.ShapeDtypeStruct(w.shape, F32)] * 4,
        compiler_params=_cparams(("parallel", "parallel")),
    )(idx, w, m, v, own, other)


WEIGHTS = ['norm_w', 'w_in', 's5_a_re', 's5_a_im', 's5_b_re', 's5_b_im', 's5_c_re', 's5_c_im', 's5_d', 's5_log_step',
           's5_w_glu', 's5_b_glu', 'ssd_conv_w', 'ssd_conv_b', 'ssd_dt_bias', 'ssd_a_log', 'ssd_d', 'ssd_norm_w',
           'w_br_s5', 'w_br_ssd', 'w_out', 'ple_norm_w', 'w_ple_gate', 'w_ple_proj', 'final_norm_w']
SHARDED = {'w_in': ((IN_PROJ_DIM, 1024), 0), 's5_w_glu': ((512, 512), 0), 'ssd_conv_w': ((SSD_CONV, SSD_CONV_DIM), 1),
           'w_br_s5': ((512, 1024), 1), 'w_br_ssd': ((1536, 1024), 0), 'w_out': ((1024, 1024), 0),
           'w_ple_gate': ((1024, 1024), 0), 'w_ple_proj': ((256, 1024), 1)}
TRANSPOSED = ('w_in',)
SMALL = [n for n in WEIGHTS if n not in SHARDED]


def _shard_shape(name):
    (r, c), ax = SHARDED[name]
    return (r // N_CHIPS, c) if ax == 0 else (r, c // N_CHIPS)


def _half_axis(name):
    return 0 if (_shard_shape(name)[0] // 2) % 16 == 0 else 1


def _shard2d(name, a):
    r, c = _shard_shape(name)
    return a.reshape(c, r).T if name in TRANSPOSED else a.reshape(r, c)


def _unshard2d(name, a2, shape):
    return (a2.T if name in TRANSPOSED else a2).reshape(shape)


def _unslot(name, a4):
    (r, c), ax = SHARDED[name]
    if ax == 0:
        return a4.reshape(r, c)
    return jnp.transpose(a4, (1, 0, 2)).reshape(r, c)


def _slot(name, full):
    (r, c), ax = SHARDED[name]
    if ax == 0:
        return full.reshape(N_CHIPS, r // N_CHIPS, c)
    return jnp.transpose(full.reshape(r, N_CHIPS, c // N_CHIPS), (1, 0, 2))


GHP = ('s5_b_re', 's5_b_im')


def _view_shape(name):
    if name in ('s5_a_re', 's5_a_im'):
        return (S5_GROUPS, S5_STATE)
    if name in GHP + ('s5_c_re', 's5_c_im'):
        return (S5_GROUPS, S5_GROUP, S5_STATE)
    if name == 'ssd_conv_w':
        return (SSD_CONV, SSD_CONV_DIM // N_CHIPS)
    return (1, {'s5_log_step': S5_GROUPS, 'ssd_conv_b': SSD_CONV_DIM, 'ssd_norm_w': SSD_WIDTH, 's5_d': S5_WIDTH,
                's5_b_glu': S5_WIDTH, 'ssd_dt_bias': SSD_HEADS, 'ssd_a_log': SSD_HEADS, 'ssd_d': SSD_HEADS}.get(name, D_MODEL))


def _view(name, a):
    if name in GHP:
        return jnp.swapaxes(a.reshape(S5_GROUPS, S5_STATE, S5_GROUP), 1, 2)
    return a.reshape(_view_shape(name))


def _unview(name, a, shape):
    return (jnp.swapaxes(a, 1, 2) if name in GHP else a).reshape(shape)


def _adamw_small(ws, ms, vs, gs):
    n = len(ws)
    c1 = 1.0 - ADAM_B1 ** ADAM_STEP
    c2 = 1.0 - ADAM_B2 ** ADAM_STEP

    def body(*refs):
        w_r, m_r, v_r, g_r = (refs[k * n:(k + 1) * n] for k in range(4))
        d_o, m_o, v_o = (refs[k * n:(k + 1) * n] for k in range(4, 7))
        for i in range(n):
            g = g_r[i][...]
            m_new = ADAM_B1 * m_r[i][...] + (1.0 - ADAM_B1) * g
            v_new = ADAM_B2 * v_r[i][...] + (1.0 - ADAM_B2) * (g * g)
            m_o[i][...] = m_new
            v_o[i][...] = v_new
            d_o[i][...] = -ADAM_LR * ((m_new / c1) / (jnp.sqrt(v_new / c2) + ADAM_EPS) + ADAM_WD * w_r[i][...])

    return pl.pallas_call(
        body, name="adamw_small", out_shape=[jax.ShapeDtypeStruct(w.shape, F32) for w in ws] * 3,
        compiler_params=pltpu.CompilerParams(vmem_limit_bytes=VMEM_LIMIT),
    )(*ws, *ms, *vs, *gs)


def _pack_small(vals):
    flat = jnp.concatenate([v.reshape(-1).astype(F32) for v in vals])
    rows = -(-flat.shape[0] // (256 * 128)) * 256
    return jnp.pad(flat, (0, rows * 128 - flat.shape[0])).reshape(rows, 128)


def _unpack_small(packed, shapes):
    flat = packed.reshape(-1)
    out, off = [], 0
    for sh in shapes:
        n = math.prod(sh)
        out.append(flat[off:off + n].reshape(sh))
        off += n
    return out


def kernel(x, p, norm_w, w_in, s5_a_re, s5_a_im, s5_b_re, s5_b_im, s5_c_re, s5_c_im, s5_d, s5_log_step, s5_w_glu, s5_b_glu, ssd_conv_w, ssd_conv_b, ssd_dt_bias, ssd_a_log, ssd_d, ssd_norm_w, w_br_s5, w_br_ssd, w_out, ple_norm_w, w_ple_gate, w_ple_proj, final_norm_w, loss_target, m_norm_w, m_w_in, m_s5_a_re, m_s5_a_im, m_s5_b_re, m_s5_b_im, m_s5_c_re, m_s5_c_im, m_s5_d, m_s5_log_step, m_s5_w_glu, m_s5_b_glu, m_ssd_conv_w, m_ssd_conv_b, m_ssd_dt_bias, m_ssd_a_log, m_ssd_d, m_ssd_norm_w, m_w_br_s5, m_w_br_ssd, m_w_out, m_ple_norm_w, m_w_ple_gate, m_w_ple_proj, m_final_norm_w, v_norm_w, v_w_in, v_s5_a_re, v_s5_a_im, v_s5_b_re, v_s5_b_im, v_s5_c_re, v_s5_c_im, v_s5_d, v_s5_log_step, v_s5_w_glu, v_s5_b_glu, v_ssd_conv_w, v_ssd_conv_b, v_ssd_dt_bias, v_ssd_a_log, v_ssd_d, v_ssd_norm_w, v_w_br_s5, v_w_br_ssd, v_w_out, v_ple_norm_w, v_w_ple_gate, v_w_ple_proj, v_final_norm_w):
    args = locals()
    wl = {n: args[n] for n in WEIGHTS}
    ml = {n: args["m_" + n] for n in WEIGHTS}
    vl = {n: args["v_" + n] for n in WEIGHTS}
    big = [n for n in SHARDED if n != 'ssd_conv_w']
    chip = _chip_index(lax.axis_index("x"), lax.axis_index("y"))
    idx = jnp.stack([lax.axis_index("c"), chip]).astype(jnp.int32)

    axes = [_half_axis(n) for n in big]
    gathered = _gather_chips([_shard2d(n, wl[n]).astype(BF16) for n in big], axes,
                             [_shard2d('ssd_conv_w', wl['ssd_conv_w'])])
    full = {n: wl[n] for n in SMALL}
    for n, a4 in zip(big + ['ssd_conv_w'], gathered, strict=True):
        if n in TRANSPOSED:
            full[n + "_t"] = a4
        else:
            full[n] = _unslot(n, a4)

    loss, grad_x, g = _local_step(x, p[0], loss_target, full)
    for n in TRANSPOSED:
        g[n] = g.pop(n + "_t")

    for n in ('s5_b_re', 's5_b_im'):
        g[n] = g.pop(n + "_ghp")
    small_shapes = [(1, 1)] + [_view_shape(n) for n in SMALL] + [(SSD_CONV, SSD_CONV_DIM)]
    small_pack = _pack_small([loss] + [g[n] for n in SMALL] + [g['ssd_conv_w']])
    slotted = [g[n] if n in TRANSPOSED else _slot(n, g[n]) for n in big]
    swapped = _swap_halves(slotted, axes, small_pack)
    pair = [_pair_sum("pair_sum_" + n, a, b, idx, ax)
            for n, a, b, ax in zip(big, slotted, swapped[:-1], axes, strict=True)]
    small_sum = _sum_slots("sum_small", swapped[-1])
    recv = _scatter_halves([pb for pb, _ in pair])
    halves = [_sum_parts("sum_chips_" + n, own, r) for n, (_, own), r in zip(big, pair, recv, strict=True)]
    other_halves = _swap_sibling(halves)

    out_g, out_d, out_m, out_v = {}, {}, {}, {}
    for n, own, oth, ax in zip(big, halves, other_halves, axes, strict=True):
        res = _adamw_halves("adamw_" + n, _shard2d(n, wl[n]), _shard2d(n, ml[n]), _shard2d(n, vl[n]), own, oth, idx, ax)
        out_g[n], out_d[n], out_m[n], out_v[n] = (_unshard2d(n, r, wl[n].shape) for r in res)
    sm = _unpack_small(small_sum, small_shapes)
    loss_total = sm[0].reshape(())
    conv_g = lax.dynamic_slice(sm[-1], (0, chip * (SSD_CONV_DIM // N_CHIPS)), (SSD_CONV, SSD_CONV_DIM // N_CHIPS))
    names = SMALL + ['ssd_conv_w']
    grads = sm[1:-1] + [conv_g]
    res = _adamw_small([_view(n, wl[n]) for n in names], [_view(n, ml[n]) for n in names],
                       [_view(n, vl[n]) for n in names], grads)
    for i, n in enumerate(names):
        out_g[n] = _unview(n, grads[i], wl[n].shape)
        out_d[n], out_m[n], out_v[n] = (_unview(n, res[k * len(names) + i], wl[n].shape) for k in range(3))

    return (loss_total, grad_x, *[out_g[n] for n in WEIGHTS], *[out_d[n] for n in WEIGHTS],
            *[out_m[n] for n in WEIGHTS], *[out_v[n] for n in WEIGHTS])
```

```python
import functools
import math

import jax
import jax.numpy as jnp
from jax import lax
from jax.experimental import pallas as pl
from jax.experimental.pallas import tpu as pltpu

F32 = jnp.float32
BF16 = jnp.bfloat16
MESH = pl.DeviceIdType.MESH

D_MODEL = 1024
PLE_DIM = 256
RMS_EPS = 1e-6
S5_WIDTH = 512
S5_GROUP = 16
S5_GROUPS = 32
S5_STATE = 64
S5_N = S5_GROUPS * S5_STATE
S5_LB = 512
S5_NJ = S5_N // S5_LB
S5_TB = 256
S5_LOG_TB = 8
SSD_WIDTH = 1536
SSD_HEADDIM = 64
SSD_HEADS = 24
SSD_GROUPS = 4
SSD_HPG = 6
SSD_STATE = 128
SSD_CONV = 4
SSD_CHUNK = 128
SSD_BC = 512
SSD_CONV_DIM = 2560
GROUP_W = SSD_WIDTH // SSD_GROUPS
N_CHIPS = 4
N_DEV = 8

OFF_XBC, OFF_U5, OFF_ZS, OFF_Z5, OFF_G5, OFF_GS, OFF_DT = 0, 2560, 3072, 4608, 5120, 6144, 7168
PROJ_W = 7680
IN_PROJ_DIM = 7192

ADAM_LR, ADAM_B1, ADAM_B2, ADAM_EPS, ADAM_WD, ADAM_STEP = 0.001, 0.9, 0.999, 1e-08, 0.01, 10

VMEM_LIMIT = 56 * 1024 * 1024


def _pick(n, cands):
    for c in cands:
        if n % c == 0:
            return c
    return n


ROW_BLOCK_BYTES = 8 * 1024 * 1024


def _row_tile(r, bytes_per_row):
    for t in (r, 4096, 2048, 1024, 512, 256, 128, 64, 32, 16, 8):
        if t <= r and r % t == 0 and t * bytes_per_row <= ROW_BLOCK_BYTES:
            return t
    return r


def _cparams(sem):
    return pltpu.CompilerParams(dimension_semantics=sem, vmem_limit_bytes=VMEM_LIMIT)


def _dg(a, b, ca, cb):
    return lax.dot_general(a.astype(BF16), b.astype(BF16), (((ca,), (cb,)), ((), ())), preferred_element_type=F32)


@jax.custom_vjp
def dot_nn(a, b):
    return _dg(a, b, 1, 0)


@jax.custom_vjp
def dot_nt(a, b):
    return _dg(a, b, 1, 1)


@jax.custom_vjp
def dot_tn(a, b):
    return _dg(a, b, 0, 0)


dot_nn.defvjp(lambda a, b: (_dg(a, b, 1, 0), (a, b)), lambda r, g: (_dg(g, r[1], 1, 1), _dg(r[0], g, 0, 0)))
dot_nt.defvjp(lambda a, b: (_dg(a, b, 1, 1), (a, b)), lambda r, g: (_dg(g, r[1], 1, 0), _dg(g, r[0], 0, 0)))
dot_tn.defvjp(lambda a, b: (_dg(a, b, 0, 0), (a, b)), lambda r, g: (_dg(r[1], g, 1, 1), _dg(r[0], g, 1, 0)))


MM_VMEM_BUDGET = 30 * 1024 * 1024


def _mm_tiles(m, n, k, sa, sb, so, tn_only=None):
    best, best_key = None, None
    for tm in (1024, 512, 256, 128, 64, 32, 16, 8):
        if m % tm:
            continue
        for tn in (2048, 1536, 1280, 1024, 768, 640, 512, 384, 256, 128):
            if n % tn or (tn_only is not None and tn not in tn_only):
                continue
            for tk in (k, 2048, 1536, 1280, 1024, 768, 512, 256, 128):
                if k % tk or tk > max(k, 128) or (tk == k and k > 2048 and k % 128 == 0):
                    continue
                need = 2 * (tm * tk * sa + tk * tn * sb + tm * tn * so) + (tm * tn * 4 if tk < k else 0)
                if need > MM_VMEM_BUDGET:
                    continue
                key = (tm * tn * tk, tk)
                if best_key is None or key > best_key:
                    best, best_key = (tm, tn, tk), key
    assert best is not None, (m, n, k)
    return best


def _matmul(name, a, b, *, ta=False, tb=False, a_win=None, out_dtype=F32, epilogue=None, epi_rows=(), epi_pars=(),
            epi_outs=(), full_rows=False):
    a_off, a_w = a_win if a_win is not None else (0, a.shape[1])
    if ta:
        kdim, m = a.shape[0], a_w
    else:
        m, kdim = a.shape[0], a_w
    n = b.shape[0] if tb else b.shape[1]
    assert (b.shape[1] if tb else b.shape[0]) == kdim, (name, a.shape, b.shape)
    out_dtypes = list(epi_outs) if epilogue is not None else [out_dtype]
    so = sum(jnp.dtype(d).itemsize for d in out_dtypes) + sum(r.dtype.itemsize for r, _ in epi_rows)
    tn_ok = [n] if full_rows else [t for t in (1024, 512, 256, 128) if all(off % t == 0 for _, off in epi_rows)]
    tm, tn, tk = _mm_tiles(m, n, kdim, a.dtype.itemsize, b.dtype.itemsize, so, tn_ok if epilogue is not None else None)
    nk = kdim // tk
    n_er, n_ep, n_out = len(epi_rows), len(epi_pars), len(out_dtypes)
    if ta:
        assert a_off % tm == 0
        a_spec = pl.BlockSpec((tk, tm), lambda i, j, k: (k, i + a_off // tm))
    else:
        assert a_off % tk == 0
        a_spec = pl.BlockSpec((tm, tk), lambda i, j, k: (i, k + a_off // tk))
    if tb:
        b_spec = pl.BlockSpec((tn, tk), lambda i, j, k: (j, k))
    else:
        b_spec = pl.BlockSpec((tk, tn), lambda i, j, k: (k, j))
    ca, cb = (0 if ta else 1), (1 if tb else 0)

    def body(a_ref, b_ref, *refs):
        er, ep = refs[:n_er], refs[n_er:n_er + n_ep]
        o_refs, acc = refs[n_er + n_ep:n_er + n_ep + n_out], refs[n_er + n_ep + n_out:]

        def finish(c):
            outs = [c] if epilogue is None else epilogue(c, [r[...] for r in er], [p[...] for p in ep])
            for o_ref, o in zip(o_refs, outs, strict=True):
                o_ref[...] = o.astype(o_ref.dtype)

        if nk == 1:
            finish(_dg(a_ref[...], b_ref[...], ca, cb))
            return
        (acc_ref,) = acc
        k = pl.program_id(2)

        @pl.when(k == 0)
        def _():
            acc_ref[...] = jnp.zeros_like(acc_ref)

        acc_ref[...] += _dg(a_ref[...], b_ref[...], ca, cb)

        @pl.when(k == nk - 1)
        def _():
            finish(acc_ref[...])

    in_specs = [a_spec, b_spec]
    in_specs += [pl.BlockSpec((tm, tn), functools.partial(lambda i, j, k, c: (i, j + c), c=off // tn)) for _, off in epi_rows]
    in_specs += [pl.BlockSpec((p.shape[0], tn), lambda i, j, k: (0, j)) for p in epi_pars]
    res = pl.pallas_call(
        body, name=name, grid=(m // tm, n // tn, nk),
        in_specs=in_specs,
        out_specs=[pl.BlockSpec((tm, tn), lambda i, j, k: (i, j)) for _ in out_dtypes],
        out_shape=[jax.ShapeDtypeStruct((m, n), d) for d in out_dtypes],
        scratch_shapes=[pltpu.VMEM((tm, tn), F32)] if nk > 1 else [],
        compiler_params=_cparams(("parallel", "parallel", "arbitrary")),
    )(a, b, *[r for r, _ in epi_rows], *epi_pars)
    return res if epilogue is not None else res[0]


BAND = 128


def _band_matmul(name, kind, a, b, *, a_blk0=0, out_dtype=F32, epilogue=None, epi_rows=(), epi_pars=(), epi_outs=(),
                 into=None):
    n_rows = a.shape[0]
    blk = 2 * S5_LB
    tm = _pick(n_rows, (1024, 512, 256))
    if epilogue is not None:
        assert kind == "nt"
        n_er, n_ep = len(epi_rows), len(epi_pars)
        in_specs = [pl.BlockSpec((tm, blk), lambda i, j: (i, j)), pl.BlockSpec((BAND, blk), lambda i, j: (j, 0))]
        in_specs += [pl.BlockSpec((tm, BAND), functools.partial(lambda i, j, c: (i, c + j), c=c0)) for _, c0 in epi_rows]
        in_specs += [pl.BlockSpec((1, BAND), lambda i, j: (0, j)) for _ in epi_pars]
        out_specs = [pl.BlockSpec((tm, BAND), lambda i, j: (i, j)) for _ in epi_outs]
        out_shape = [jax.ShapeDtypeStruct((n_rows, S5_NJ * BAND), d) for d in epi_outs]
        extra, aliases = [], {}
        if into is not None:
            buf, c0 = into
            in_specs.append(pl.BlockSpec(memory_space=pl.ANY))
            out_specs[0] = pl.BlockSpec((tm, BAND), functools.partial(lambda i, j, c: (i, c + j), c=c0))
            out_shape[0] = jax.ShapeDtypeStruct(buf.shape, buf.dtype)
            extra, aliases = [buf], {2 + n_er + n_ep: 0}

        def epi_body(a_ref, b_ref, *refs):
            outs = epilogue(_dg(a_ref[...], b_ref[...], 1, 1), [r[...] for r in refs[:n_er]],
                            [p[...] for p in refs[n_er:n_er + n_ep]])
            for o_ref, o in zip(refs[n_er + n_ep + len(extra):], outs, strict=True):
                o_ref[...] = o.astype(o_ref.dtype)

        return pl.pallas_call(
            epi_body, name=name, grid=(n_rows // tm, S5_NJ), in_specs=in_specs, out_specs=out_specs,
            out_shape=out_shape, input_output_aliases=aliases, compiler_params=_cparams(("parallel", "parallel")),
        )(a, b, *[r for r, _ in epi_rows], *epi_pars, *extra)
    if kind == "nn":
        grid = (n_rows // tm, S5_NJ)
        in_specs = [pl.BlockSpec((tm, BAND), lambda i, j: (i, a_blk0 + j)), pl.BlockSpec((BAND, blk), lambda i, j: (j, 0))]
        out_spec = pl.BlockSpec((tm, blk), lambda i, j: (i, j))
        out_shape = (n_rows, S5_NJ * blk)
        sem = ("parallel", "parallel")

        def body(a_ref, b_ref, o_ref):
            o_ref[...] = _dg(a_ref[...], b_ref[...], 1, 0).astype(o_ref.dtype)
    elif kind == "nt":
        grid = (n_rows // tm, S5_NJ)
        in_specs = [pl.BlockSpec((tm, blk), lambda i, j: (i, j)), pl.BlockSpec((BAND, blk), lambda i, j: (j, 0))]
        out_spec = pl.BlockSpec((tm, BAND), lambda i, j: (i, j))
        out_shape = (n_rows, S5_NJ * BAND)
        sem = ("parallel", "parallel")

        def body(a_ref, b_ref, o_ref):
            o_ref[...] = _dg(a_ref[...], b_ref[...], 1, 1).astype(o_ref.dtype)
    else:
        grid = (S5_NJ, n_rows // tm)
        in_specs = [pl.BlockSpec((tm, BAND), lambda j, k: (k, a_blk0 + j)), pl.BlockSpec((tm, blk), lambda j, k: (k, j))]
        out_spec = pl.BlockSpec((BAND, blk), lambda j, k: (j, 0))
        out_shape = (S5_NJ * BAND, blk)
        sem = ("parallel", "arbitrary")

        def body(a_ref, b_ref, o_ref):
            @pl.when(pl.program_id(1) == 0)
            def _():
                o_ref[...] = jnp.zeros_like(o_ref)

            o_ref[...] += _dg(a_ref[...], b_ref[...], 0, 0)

    return pl.pallas_call(
        body, name=name, grid=grid, in_specs=in_specs, out_specs=out_spec,
        out_shape=jax.ShapeDtypeStruct(out_shape, out_dtype), compiler_params=_cparams(sem),
    )(a, b)


def _rowwise(name, fn, n_rows, tr, row_ins, par_ins, row_outs, acc_outs=(), into=None):
    nr, npar, no, na = len(row_ins), len(par_ins), len(row_outs), len(acc_outs)
    in_specs = []
    for arr, off, w in row_ins:
        assert off % w == 0 and arr.shape[0] == n_rows, (name, arr.shape, off, w)
        in_specs.append(pl.BlockSpec((tr, w), functools.partial(lambda i, c: (i, c), c=off // w)))
    for arr, off, w in par_ins:
        assert off % w == 0
        in_specs.append(pl.BlockSpec((arr.shape[0], w), functools.partial(lambda i, c: (0, c), c=off // w)))
    out_specs = [pl.BlockSpec((tr, w), lambda i: (i, 0)) for w, _ in row_outs]
    out_specs += [pl.BlockSpec((r, w), lambda i: (0, 0)) for r, w in acc_outs]
    out_shape = [jax.ShapeDtypeStruct((n_rows, w), dt) for w, dt in row_outs]
    out_shape += [jax.ShapeDtypeStruct((r, w), F32) for r, w in acc_outs]
    extra, aliases = [], {}
    if into is not None:
        buf, off = into
        w0 = row_outs[0][0]
        assert off % w0 == 0 and buf.dtype == row_outs[0][1]
        in_specs.append(pl.BlockSpec(memory_space=pl.ANY))
        out_specs[0] = pl.BlockSpec((tr, w0), functools.partial(lambda i, c: (i, c), c=off // w0))
        out_shape[0] = jax.ShapeDtypeStruct(buf.shape, buf.dtype)
        extra, aliases = [buf], {nr + npar: 0}
    nx = len(extra)

    def body(*refs):
        rows = [r[...] for r in refs[:nr]]
        pars = [r[...] for r in refs[nr:nr + npar]]
        o_refs = refs[nr + npar + nx:nr + npar + nx + no]
        a_refs = refs[nr + npar + nx + no:]
        outs, accs = fn(rows, pars)
        for o_ref, o in zip(o_refs, outs, strict=True):
            o_ref[...] = o.astype(o_ref.dtype)
        if na:
            @pl.when(pl.program_id(0) == 0)
            def _():
                for a_ref in a_refs:
                    a_ref[...] = jnp.zeros_like(a_ref)

            for a_ref, a in zip(a_refs, accs, strict=True):
                a_ref[...] += jnp.broadcast_to(a, a_ref.shape)

    res = pl.pallas_call(
        body, name=name, grid=(n_rows // tr,),
        in_specs=in_specs, out_specs=out_specs, out_shape=out_shape, input_output_aliases=aliases,
        compiler_params=_cparams(("arbitrary",) if na else ("parallel",)),
    )(*[a for a, _, _ in row_ins], *[a for a, _, _ in par_ins], *extra)
    return res


def _rms(x, w):
    return x * lax.rsqrt(jnp.mean(x * x, axis=-1, keepdims=True) + RMS_EPS) * w


def _gated_norm(y, z, w):
    outs = []
    for g in range(SSD_GROUPS):
        sl = slice(g * GROUP_W, (g + 1) * GROUP_W)
        yg = y[:, sl] * jax.nn.silu(z[:, sl])
        outs.append(yg * lax.rsqrt(jnp.mean(yg * yg, axis=-1, keepdims=True) + RMS_EPS) * w[:, sl])
    return jnp.concatenate(outs, axis=-1)


def _s5_out(yc, u, z, t, d, bg):
    ge = jax.nn.gelu(yc + d * u)
    return ge * jax.nn.sigmoid(t + bg) * jax.nn.silu(z)


def _merge(g5, gs, m5, ms):
    return jax.nn.sigmoid(g5) * m5 + jax.nn.sigmoid(gs) * ms


def _head_loss(h1, pgl, pp, fw, tgt):
    h2 = h1 + jax.nn.sigmoid(pgl) * pp
    err = _rms(h2, fw) - tgt
    per_row = 0.5 * jnp.mean(err * err, axis=-1, keepdims=True)
    return jnp.sum(per_row, axis=0, keepdims=True)


def _s5_disc(a_re, a_im, log_step, b_re2, b_im2, expand):
    step = jnp.exp(log_step)
    mag = jnp.exp(a_re * step)
    lb_re = mag * jnp.cos(a_im * step)
    lb_im = mag * jnp.sin(a_im * step)
    den = a_re * a_re + a_im * a_im
    n_re = lb_re - 1.0
    f_re = (n_re * a_re + lb_im * a_im) / den
    f_im = (lb_im * a_re - n_re * a_im) / den
    hi = lax.Precision.HIGHEST
    fr = jnp.dot(expand, f_re, precision=hi, preferred_element_type=F32)
    fi = jnp.dot(expand, f_im, precision=hi, preferred_element_type=F32)
    return lb_re, lb_im, fr * b_re2 - fi * b_im2, fr * b_im2 + fi * b_re2


def _s5_params_fwd(a_re, a_im, log_step, b_re2, b_im2, expand):
    gp = a_re.shape

    def body(ar, ai, ls, br, bi, ex, pr_ref, pi_ref, bbr_ref, bbi_ref):
        lr, li, bbr, bbi = _s5_disc(ar[...], ai[...], ls[...], br[...], bi[...], ex[...])
        bbr_ref[...] = bbr
        bbi_ref[...] = bbi
        qr, qi = lr, li
        for k in range(S5_LOG_TB):
            pr_ref[k] = qr
            pi_ref[k] = qi
            qr, qi = qr * lr - qi * li, qr * li + qi * lr

    return pl.pallas_call(
        body, name="s5_params_fwd",
        out_shape=(jax.ShapeDtypeStruct((S5_LOG_TB,) + gp, F32), jax.ShapeDtypeStruct((S5_LOG_TB,) + gp, F32),
                   jax.ShapeDtypeStruct(b_re2.shape, F32), jax.ShapeDtypeStruct(b_re2.shape, F32)),
    )(a_re, a_im, log_step, b_re2, b_im2, expand)


def _s5_params_bwd(a_re, a_im, log_step, b_re2, b_im2, expand, d_lr, d_li, d_bbr, d_bbi):
    def body(ar, ai, ls, br, bi, ex, glr, gli, gbr, gbi, dar, dai, dls, dbr, dbi):
        _, vjp = jax.vjp(lambda *p: _s5_disc(*p, ex[...]), ar[...], ai[...], ls[...], br[...], bi[...])
        g = vjp((glr[...], gli[...], gbr[...], gbi[...]))
        for ref, val in zip((dar, dai, dls, dbr, dbi), g, strict=True):
            ref[...] = val

    return pl.pallas_call(
        body, name="s5_params_bwd",
        out_shape=tuple(jax.ShapeDtypeStruct(v.shape, F32) for v in (a_re, a_im, log_step, b_re2, b_im2)),
    )(a_re, a_im, log_step, b_re2, b_im2, expand, d_lr, d_li, d_bbr, d_bbi)


def _scan_block(x_ref, out_ref, lp, edge_pow, cr, ci, reverse, each=None):
    n_g = x_ref.shape[0] // 8
    sub = lax.broadcasted_iota(jnp.int32, (8, S5_LB), 0)
    steps = []
    for sh in (1, 2, 4):
        keep = (sub < 8 - sh) if reverse else (sub >= sh)
        steps.append((8 - sh if reverse else sh, jnp.where(keep, lp[sh - 1:sh, :S5_LB], 0.0),
                      jnp.where(keep, lp[sh - 1:sh, S5_LB:], 0.0)))
    e_r, e_i = edge_pow[:, :S5_LB], edge_pow[:, S5_LB:]
    for r in (range(n_g - 1, -1, -1) if reverse else range(n_g)):
        rows = slice(8 * r, 8 * r + 8)
        xr, xi = x_ref[rows, :S5_LB], x_ref[rows, S5_LB:]
        for by, a_r, a_i in steps:
            pr, pi = pltpu.roll(xr, by, 0), pltpu.roll(xi, by, 0)
            xr, xi = xr + a_r * pr - a_i * pi, xi + a_r * pi + a_i * pr
        xr = xr + e_r * cr - e_i * ci
        xi = xi + e_r * ci + e_i * cr
        out_ref[rows, :S5_LB] = xr
        out_ref[rows, S5_LB:] = xi
        if each is not None:
            each(r, xr, xi)
        cr, ci = (xr[0:1, :], xi[0:1, :]) if reverse else (xr[7:8, :], xi[7:8, :])
    return cr, ci


def _s5_scan_fwd(bu, lam_pow, n_seq, seq_len):
    n_t = seq_len // S5_TB
    blk = 2 * S5_LB

    def body(bu_ref, lp_ref, s_ref, cr, ci):
        @pl.when(pl.program_id(2) == 0)
        def _():
            cr[...] = jnp.zeros_like(cr)
            ci[...] = jnp.zeros_like(ci)

        lp = lp_ref[...]
        cr[...], ci[...] = _scan_block(bu_ref, s_ref, lp, lp, cr[...], ci[...], False)

    return pl.pallas_call(
        body, name="s5_scan_fwd", grid=(S5_NJ, n_seq, n_t),
        in_specs=[pl.BlockSpec((S5_TB, blk), lambda j, b, t: (b * n_t + t, j)),
                  pl.BlockSpec((S5_LOG_TB, blk), lambda j, b, t: (0, j))],
        out_specs=pl.BlockSpec((S5_TB, blk), lambda j, b, t: (b * n_t + t, j)),
        out_shape=jax.ShapeDtypeStruct(bu.shape, F32),
        scratch_shapes=[pltpu.VMEM((1, S5_LB), F32), pltpu.VMEM((1, S5_LB), F32)],
        compiler_params=_cparams(("parallel", "parallel", "arbitrary")),
    )(bu, lam_pow)


def _s5_scan_bwd(ds, s, lam_pow_conj, n_seq, seq_len):
    n_t = seq_len // S5_TB
    blk = 2 * S5_LB
    halo_per_blk = S5_TB // 8

    def rows(j, b, t):
        return (b * n_t + (n_t - 1 - t), j)

    def halo(j, b, t):
        return (jnp.maximum((b * n_t + (n_t - 1 - t)) * halo_per_blk - 1, 0), j)

    def body(ds_ref, s_ref, h_ref, lp_ref, g_ref, dl_ref, cr, ci, buf):
        b, t = pl.program_id(1), pl.program_id(2)

        @pl.when(t == 0)
        def _():
            cr[...] = jnp.zeros_like(cr)
            ci[...] = jnp.zeros_like(ci)

        @pl.when((b == 0) & (t == 0))
        def _():
            dl_ref[...] = jnp.zeros_like(dl_ref)

        first_blk = t == n_t - 1
        sub = lax.broadcasted_iota(jnp.int32, (8, S5_LB), 0)
        acc = [jnp.zeros((8, S5_LB), F32), jnp.zeros((8, S5_LB), F32)]

        def each(r, gr, gi):
            rows = slice(8 * r, 8 * r + 8)
            if r == 0:
                before_r = jnp.where(first_blk, 0.0, h_ref[7:8, :S5_LB])
                before_i = jnp.where(first_blk, 0.0, h_ref[7:8, S5_LB:])
            else:
                before_r, before_i = s_ref[8 * r - 1:8 * r, :S5_LB], s_ref[8 * r - 1:8 * r, S5_LB:]
            sp_r = jnp.where(sub == 0, before_r, pltpu.roll(s_ref[rows, :S5_LB], 1, 0))
            sp_i = jnp.where(sub == 0, before_i, pltpu.roll(s_ref[rows, S5_LB:], 1, 0))
            acc[0] = acc[0] + gr * sp_r + gi * sp_i
            acc[1] = acc[1] + gi * sp_r - gr * sp_i

        lp = lp_ref[...]
        edge_pow = jnp.concatenate([lp[7 - i:8 - i, :] for i in range(8)], axis=0)
        cr[...], ci[...] = _scan_block(ds_ref, buf, lp, edge_pow, cr[...], ci[...], True, each)
        g_ref[...] = buf[...].astype(g_ref.dtype)
        dl_ref[:, :S5_LB] += jnp.sum(acc[0], axis=0, keepdims=True)
        dl_ref[:, S5_LB:] += jnp.sum(acc[1], axis=0, keepdims=True)

    return pl.pallas_call(
        body, name="s5_scan_bwd", grid=(S5_NJ, n_seq, n_t),
        in_specs=[pl.BlockSpec((S5_TB, blk), rows), pl.BlockSpec((S5_TB, blk), rows),
                  pl.BlockSpec((8, blk), halo), pl.BlockSpec((S5_LOG_TB, blk), lambda j, b, t: (0, j))],
        out_specs=[pl.BlockSpec((S5_TB, blk), rows), pl.BlockSpec((1, blk), lambda j, b, t: (0, j))],
        out_shape=[jax.ShapeDtypeStruct(ds.shape, BF16), jax.ShapeDtypeStruct((1, ds.shape[1]), F32)],
        scratch_shapes=[pltpu.VMEM((1, S5_LB), F32), pltpu.VMEM((1, S5_LB), F32), pltpu.VMEM((S5_TB, blk), F32)],
        compiler_params=_cparams(("parallel", "arbitrary", "arbitrary")),
    )(ds, s, s, lam_pow_conj)


CONV_TR = 512
CONV_CW = 512


def _shift_down(x, halo, k):
    if k == 0:
        return x
    row8 = lax.broadcasted_iota(jnp.int32, halo.shape, 0)
    rolled = pltpu.roll(x, k, 0)
    top = jnp.where(row8 < k, pltpu.roll(halo, k, 0), rolled[:8])
    if x.shape[0] == 8:
        return top
    return jnp.concatenate([top, rolled[8:]], axis=0)


def _shift_up(x, halo, k):
    if k == 0:
        return x
    n = x.shape[0]
    row8 = lax.broadcasted_iota(jnp.int32, halo.shape, 0)
    rolled = pltpu.roll(x, n - k, 0)
    bot = jnp.where(row8 >= 8 - k, pltpu.roll(halo, 8 - k, 0), rolled[n - 8:])
    if n == 8:
        return bot
    return jnp.concatenate([rolled[:n - 8], bot], axis=0)


def _conv_pre(x, halo, w, b):
    acc = b + w[SSD_CONV - 1:SSD_CONV, :] * x
    for k in range(SSD_CONV - 1):
        acc = acc + w[k:k + 1, :] * _shift_down(x, halo, SSD_CONV - 1 - k)
    return acc


def _conv_specs(seq_len, col_off):
    lt = seq_len // CONV_TR
    cb = col_off // CONV_CW
    cur = pl.BlockSpec((CONV_TR, CONV_CW), lambda j, i: (i, j + cb))
    prev = pl.BlockSpec((8, CONV_CW), lambda j, i: (jnp.maximum(i * (CONV_TR // 8) - 1, 0), j + cb))
    return lt, cur, prev


def _conv_fwd(proj, conv_w, conv_b, n_rows, seq_len):
    lt, cur, prev = _conv_specs(seq_len, OFF_XBC)

    def body(x_ref, h_ref, w_ref, b_ref, o_ref):
        halo = jnp.where(pl.program_id(1) % lt == 0, 0.0, h_ref[...])
        o_ref[...] = jax.nn.silu(_conv_pre(x_ref[...], halo, w_ref[...], b_ref[...]))

    return pl.pallas_call(
        body, name="ssd_conv_fwd", grid=(SSD_CONV_DIM // CONV_CW, n_rows // CONV_TR),
        in_specs=[cur, prev, pl.BlockSpec((SSD_CONV, CONV_CW), lambda j, i: (0, j)),
                  pl.BlockSpec((1, CONV_CW), lambda j, i: (0, j))],
        out_specs=pl.BlockSpec((CONV_TR, CONV_CW), lambda j, i: (i, j)),
        out_shape=jax.ShapeDtypeStruct((n_rows, SSD_CONV_DIM), F32),
        compiler_params=_cparams(("parallel", "parallel")),
    )(proj, proj, conv_w, conv_b)


def _conv_bwd(name, proj, d_act, conv_w, conv_b, n_rows, seq_len, col_off, dproj):
    width = d_act.shape[1]
    lt, cur, prev = _conv_specs(seq_len, OFF_XBC + col_off)
    n_blk = n_rows // CONV_TR
    cb = (OFF_XBC + col_off) // CONV_CW
    pb = col_off // CONV_CW
    nxt = pl.BlockSpec((8, CONV_CW), lambda j, i: (jnp.minimum((i + 1) * (CONV_TR // 8), n_rows // 8 - 1), j + cb))
    d_cur = pl.BlockSpec((CONV_TR, CONV_CW), lambda j, i: (i, j))
    d_nxt = pl.BlockSpec((8, CONV_CW), lambda j, i: (jnp.minimum((i + 1) * (CONV_TR // 8), n_rows // 8 - 1), j))

    def dsilu(pre):
        sg = jax.nn.sigmoid(pre)
        return sg * (1.0 + pre * (1.0 - sg))

    def body(x_ref, hp_ref, hn_ref, d_ref, dn_ref, w_ref, b_ref, _, dx_ref, dw_ref, db_ref):
        i = pl.program_id(1)
        x, w, b = x_ref[...], w_ref[...], b_ref[...]
        halo_p = jnp.where(i % lt == 0, 0.0, hp_ref[...])
        at_end = i % lt == lt - 1
        dpre = d_ref[...] * dsilu(_conv_pre(x, halo_p, w, b))
        pre_n = _conv_pre(hn_ref[...], x[CONV_TR - 8:, :], w, b)
        dpre_n = jnp.where(at_end, 0.0, dn_ref[...] * dsilu(pre_n))
        dx = w[SSD_CONV - 1:SSD_CONV, :] * dpre
        for k in range(SSD_CONV - 1):
            dx = dx + w[k:k + 1, :] * _shift_up(dpre, dpre_n, SSD_CONV - 1 - k)
        dx_ref[...] = dx.astype(dx_ref.dtype)

        @pl.when(i == 0)
        def _():
            dw_ref[...] = jnp.zeros_like(dw_ref)
            db_ref[...] = jnp.zeros_like(db_ref)

        for k in range(SSD_CONV):
            xs = _shift_down(x, halo_p, SSD_CONV - 1 - k)
            dw_ref[k:k + 1, :] += jnp.sum(dpre * xs, axis=0, keepdims=True)
        db_ref[...] += jnp.sum(dpre, axis=0, keepdims=True)

    return pl.pallas_call(
        body, name=name, grid=(width // CONV_CW, n_blk),
        in_specs=[cur, prev, nxt, d_cur, d_nxt,
                  pl.BlockSpec((SSD_CONV, CONV_CW), lambda j, i: (0, j + pb)),
                  pl.BlockSpec((1, CONV_CW), lambda j, i: (0, j + pb)), pl.BlockSpec(memory_space=pl.ANY)],
        out_specs=[pl.BlockSpec((CONV_TR, CONV_CW), lambda j, i: (i, j + cb)),
                   pl.BlockSpec((SSD_CONV, CONV_CW), lambda j, i: (0, j)),
                   pl.BlockSpec((1, CONV_CW), lambda j, i: (0, j))],
        out_shape=[jax.ShapeDtypeStruct(dproj.shape, dproj.dtype),
                   jax.ShapeDtypeStruct((SSD_CONV, width), F32), jax.ShapeDtypeStruct((1, width), F32)],
        input_output_aliases={7: 0},
        compiler_params=_cparams(("parallel", "arbitrary")),
    )(proj, proj, proj, d_act, d_act, conv_w, conv_b, dproj)


def _split3(x):
    hi = x.astype(BF16)
    r = x - hi.astype(F32)
    mid = r.astype(BF16)
    return hi, mid, (r - mid.astype(F32)).astype(BF16)


def _sel_dot(a, b, a_is_sel):
    dn = (((1,), (0,)), ((), ()))
    if a_is_sel:
        return sum(lax.dot_general(a, t, dn, preferred_element_type=F32) for t in _split3(b))
    return sum(lax.dot_general(t, b, dn, preferred_element_type=F32) for t in _split3(a))


@jax.custom_vjp
def sel_left(sel, sel_t, x):
    return _sel_dot(sel, x, True)


@jax.custom_vjp
def sel_right(x, sel, sel_t):
    return _sel_dot(x, sel, False)


sel_left.defvjp(lambda s, st, x: (_sel_dot(s, x, True), (s, st)),
                lambda r, g: (jnp.zeros_like(r[0]), jnp.zeros_like(r[1]), _sel_dot(r[1], g, True)))
sel_right.defvjp(lambda x, s, st: (_sel_dot(x, s, False), (s, st)),
                 lambda r, g: (_sel_dot(g, r[1], False), jnp.zeros_like(r[0]), jnp.zeros_like(r[1])))


def _ssd_chunk(xs, bm, cm, dtr, st, dtb, alog, dsk, k):
    dt = jax.nn.softplus(dtr + dtb)
    acum = sel_left(k["tri"], k["tri_t"], dt * (-jnp.exp(alog)))
    dt_e = sel_right(dt, k["spread"], k["spread_t"])
    ac_e = sel_right(acum, k["spread"], k["spread_t"])
    al_e = ac_e[SSD_CHUNK - 1:SSD_CHUNK, :]
    dsk_e = sel_right(jnp.broadcast_to(dsk, (8, 128)), k["spread"], k["spread_t"])[0:1, :]
    xdt = xs * dt_e
    acum_t = acum.T
    scores = dot_nt(cm, bm)
    y = dot_nn(cm, st) * jnp.exp(ac_e) + xs * dsk_e
    for j in range(SSD_HPG):
        lmat = jnp.exp(jnp.where(k["causal"], acum[:, j:j + 1] - acum_t[j:j + 1, :], -jnp.inf))
        y = y + dot_nn(scores * lmat, jnp.where(k["head"] == j, xdt, 0.0))
    new = st * jnp.exp(al_e) + dot_tn(bm, xdt * jnp.exp(al_e - ac_e))
    return y, new


def _ssd_consts():
    r = lax.broadcasted_iota(jnp.int32, (SSD_CHUNK, SSD_CHUNK), 0)
    c = lax.broadcasted_iota(jnp.int32, (SSD_CHUNK, SSD_CHUNK), 1)
    hd = jnp.int32(SSD_HEADDIM)
    sr = lax.broadcasted_iota(jnp.int32, (128, GROUP_W), 0)
    sc = lax.div(lax.broadcasted_iota(jnp.int32, (128, GROUP_W), 1), hd)
    tr = lax.div(lax.broadcasted_iota(jnp.int32, (GROUP_W, 128), 0), hd)
    tc = lax.broadcasted_iota(jnp.int32, (GROUP_W, 128), 1)
    return {"tri": (r >= c).astype(BF16), "tri_t": (c >= r).astype(BF16), "causal": r >= c,
            "spread": (sr == sc).astype(BF16), "spread_t": (tr == tc).astype(BF16),
            "head": lax.div(lax.broadcasted_iota(jnp.int32, (SSD_CHUNK, GROUP_W), 1), hd)}


def _ssd_specs(n_c, reverse):
    def cidx(c):
        return n_c - 1 - c if reverse else c

    xs = pl.BlockSpec((SSD_CHUNK, GROUP_W), lambda g, b, c: (b * n_c + cidx(c), g))
    bm = pl.BlockSpec((SSD_CHUNK, SSD_STATE), lambda g, b, c: (b * n_c + cidx(c), SSD_WIDTH // SSD_STATE + g))
    cm = pl.BlockSpec((SSD_CHUNK, SSD_STATE), lambda g, b, c: (b * n_c + cidx(c), (SSD_WIDTH + SSD_BC) // SSD_STATE + g))
    dt = pl.BlockSpec((SSD_CHUNK, 128), lambda g, b, c: (b * n_c + cidx(c), OFF_DT // 128 + g))
    par = pl.BlockSpec((1, 128), lambda g, b, c: (0, g))
    st = pl.BlockSpec((1, 1, 1, SSD_STATE, GROUP_W), lambda g, b, c: (b, cidx(c), g, 0, 0))
    return xs, bm, cm, dt, par, st


def _ssd_fwd(xbc_act, proj, dtb, alog, dsk, n_seq, seq_len):
    n_c = seq_len // SSD_CHUNK
    xs_s, bm_s, cm_s, dt_s, par_s, st_s = _ssd_specs(n_c, False)

    def body(xs_ref, bm_ref, cm_ref, dt_ref, dtb_ref, al_ref, dk_ref, y_ref, st_ref, state):
        @pl.when(pl.program_id(2) == 0)
        def _():
            state[...] = jnp.zeros_like(state)

        prev = state[...]
        st_ref[0, 0, 0] = prev
        y, new = _ssd_chunk(xs_ref[...], bm_ref[...], cm_ref[...], dt_ref[...], prev,
                            dtb_ref[...], al_ref[...], dk_ref[...], _ssd_consts())
        y_ref[...] = y
        state[...] = new

    return pl.pallas_call(
        body, name="ssd_fwd", grid=(SSD_GROUPS, n_seq, n_c),
        in_specs=[xs_s, bm_s, cm_s, dt_s, par_s, par_s, par_s],
        out_specs=[pl.BlockSpec((SSD_CHUNK, GROUP_W), lambda g, b, c: (b * n_c + c, g)), st_s],
        out_shape=[jax.ShapeDtypeStruct((n_seq * seq_len, SSD_WIDTH), F32),
                   jax.ShapeDtypeStruct((n_seq, n_c, SSD_GROUPS, SSD_STATE, GROUP_W), F32)],
        scratch_shapes=[pltpu.VMEM((SSD_STATE, GROUP_W), F32)],
        compiler_params=_cparams(("parallel", "parallel", "arbitrary")),
    )(xbc_act, xbc_act, xbc_act, proj, dtb, alog, dsk)


def _ssd_bwd(xbc_act, proj, states, dy, dtb, alog, dsk, n_seq, seq_len, dproj):
    n_c = seq_len // SSD_CHUNK
    n_rows = n_seq * seq_len
    xs_s, bm_s, cm_s, dt_s, par_s, st_s = _ssd_specs(n_c, True)

    def rows(w, first=0):
        return pl.BlockSpec((SSD_CHUNK, w), lambda g, b, c: (b * n_c + (n_c - 1 - c), first + g))

    def body(xs_ref, bm_ref, cm_ref, dt_ref, st_ref, dy_ref, dtb_ref, al_ref, dk_ref, _,
             dxs_ref, dbm_ref, dcm_ref, ddt_ref, ddtb_ref, dal_ref, ddk_ref, dstate):
        b, c = pl.program_id(1), pl.program_id(2)

        @pl.when(c == 0)
        def _():
            dstate[...] = jnp.zeros_like(dstate)

        @pl.when((b == 0) & (c == 0))
        def _():
            ddtb_ref[...] = jnp.zeros_like(ddtb_ref)
            dal_ref[...] = jnp.zeros_like(dal_ref)
            ddk_ref[...] = jnp.zeros_like(ddk_ref)

        consts = _ssd_consts()
        _, vjp = jax.vjp(
            lambda xs, bm, cm, dtr, prev, dtb, alog, dsk: _ssd_chunk(xs, bm, cm, dtr, prev, dtb, alog, dsk, consts),
            xs_ref[...], bm_ref[...], cm_ref[...], dt_ref[...], st_ref[0, 0, 0], dtb_ref[...], al_ref[...], dk_ref[...])
        dxs, dbm, dcm, ddtr, dprev, ddtb, dal, ddk = vjp((dy_ref[...], dstate[...]))
        dxs_ref[...] = dxs
        dbm_ref[...] = dbm
        dcm_ref[...] = dcm
        ddt_ref[...] = ddtr.astype(ddt_ref.dtype)
        ddtb_ref[...] += ddtb
        dal_ref[...] += dal
        ddk_ref[...] += ddk
        dstate[...] = dprev

    acc = pl.BlockSpec((1, 128), lambda g, b, c: (0, g))
    return pl.pallas_call(
        body, name="ssd_bwd", grid=(SSD_GROUPS, n_seq, n_c),
        in_specs=[xs_s, bm_s, cm_s, dt_s, st_s, rows(GROUP_W), par_s, par_s, par_s, pl.BlockSpec(memory_space=pl.ANY)],
        out_specs=[rows(GROUP_W), rows(SSD_STATE), rows(SSD_STATE), rows(128, OFF_DT // 128), acc, acc, acc],
        out_shape=[jax.ShapeDtypeStruct((n_rows, SSD_WIDTH), F32), jax.ShapeDtypeStruct((n_rows, SSD_BC), F32),
                   jax.ShapeDtypeStruct((n_rows, SSD_BC), F32), jax.ShapeDtypeStruct(dproj.shape, dproj.dtype),
                   jax.ShapeDtypeStruct((1, 512), F32), jax.ShapeDtypeStruct((1, 512), F32),
                   jax.ShapeDtypeStruct((1, 512), F32)],
        input_output_aliases={9: 3},
        scratch_shapes=[pltpu.VMEM((SSD_STATE, GROUP_W), F32)],
        compiler_params=_cparams(("parallel", "arbitrary", "arbitrary")),
    )(xbc_act, xbc_act, xbc_act, proj, states, dy, dtb, alog, dsk, dproj)


def _pad_heads(v):
    return jnp.pad(v.reshape(SSD_GROUPS, SSD_HPG), ((0, 0), (0, 128 - SSD_HPG))).reshape(1, SSD_GROUPS * 128)


def _unpad_heads(v):
    return v.reshape(SSD_GROUPS, 128)[:, :SSD_HPG].reshape(1, SSD_HEADS)


def _state_cols(v):
    re, im = v
    lead = re.shape[:-1]
    re = re.reshape(lead + (S5_NJ, 1, S5_LB))
    im = im.reshape(lead + (S5_NJ, 1, S5_LB))
    return jnp.concatenate([re, im], axis=-2).reshape(lead + (2 * S5_N,))


def _state_uncols(v):
    lead = v.shape[:-1]
    v = v.reshape(lead + (S5_NJ, 2, S5_LB))
    return v[..., 0, :].reshape(lead + (S5_N,)), v[..., 1, :].reshape(lead + (S5_N,))


GROUPS_PER_BAND = BAND // S5_GROUP


def _band(w2_re, w2_im):
    gh = S5_GROUPS * S5_GROUP
    rg = ((jnp.arange(gh) // S5_GROUP) % GROUPS_PER_BAND)[:, None, None]
    cg = jnp.arange(GROUPS_PER_BAND)[None, :, None]
    parts = [jnp.where(rg == cg, v[:, None, :], 0.0).reshape(gh, S5_LB) for v in (w2_re, w2_im)]
    return jnp.concatenate(parts, axis=1)


def _band_take(wb):
    gh = S5_GROUPS * S5_GROUP
    w4 = wb.reshape(gh, 2, GROUPS_PER_BAND, S5_STATE)
    sel = w4[jnp.arange(gh), :, (jnp.arange(gh) // S5_GROUP) % GROUPS_PER_BAND, :]
    return sel[:, 0, :], sel[:, 1, :]


W_IN_SHARD = IN_PROJ_DIM // N_CHIPS
W_IN_SEGS = ((0, 512, OFF_U5), (512, 1024, OFF_Z5), (1024, 2560, OFF_ZS), (2560, 5120, OFF_XBC), (5144, 7192, OFF_G5))
DT_ROWS = (5120, 5144)


def _w_in_pieces():
    runs = []
    segs = list(W_IN_SEGS) + [(DT_ROWS[0] + SSD_HPG * g, DT_ROWS[0] + SSD_HPG * (g + 1), OFF_DT + 128 * g)
                              for g in range(SSD_GROUPS)]
    for lo, hi, off in segs:
        for j in range(N_CHIPS):
            s, e = max(lo, j * W_IN_SHARD), min(hi, (j + 1) * W_IN_SHARD)
            if s < e:
                runs.append((j, s - j * W_IN_SHARD, off + s - lo, e - s))
    return runs


RELAYOUT_LANES = 256


def _pad_w_in_t(a4):
    runs = _w_in_pieces()

    def body(a_ref, o_ref):
        o_ref[pl.ds(OFF_DT, PROJ_W - OFF_DT), :] = jnp.zeros((PROJ_W - OFF_DT, RELAYOUT_LANES), o_ref.dtype)
        for j, src, dst, n in runs:
            o_ref[pl.ds(dst, n), :] = a_ref[j, pl.ds(src, n), :]

    return pl.pallas_call(
        body, name="w_in_to_padded", grid=(D_MODEL // RELAYOUT_LANES,),
        in_specs=[pl.BlockSpec((N_CHIPS, W_IN_SHARD, RELAYOUT_LANES), lambda i: (0, 0, i))],
        out_specs=pl.BlockSpec((PROJ_W, RELAYOUT_LANES), lambda i: (0, i)),
        out_shape=jax.ShapeDtypeStruct((PROJ_W, D_MODEL), a4.dtype),
        compiler_params=_cparams(("parallel",)),
    )(a4)


def _unpad_w_in_t(wp):
    runs = _w_in_pieces()

    def body(p_ref, o_ref):
        for j, dst, src, n in runs:
            o_ref[j, pl.ds(dst, n), :] = p_ref[pl.ds(src, n), :]

    return pl.pallas_call(
        body, name="w_in_from_padded", grid=(D_MODEL // RELAYOUT_LANES,),
        in_specs=[pl.BlockSpec((PROJ_W, RELAYOUT_LANES), lambda i: (0, i))],
        out_specs=pl.BlockSpec((N_CHIPS, W_IN_SHARD, RELAYOUT_LANES), lambda i: (0, 0, i)),
        out_shape=jax.ShapeDtypeStruct((N_CHIPS, W_IN_SHARD, D_MODEL), wp.dtype),
        compiler_params=_cparams(("parallel",)),
    )(wp)


def _local_step(x, p, tgt, w):
    n_seq, seq_len, _ = x.shape
    n_rows = n_seq * seq_len
    tr = 512
    x2 = x.reshape(n_rows, D_MODEL)
    p2 = p.reshape(n_rows, PLE_DIM)
    t2 = tgt.reshape(n_rows, D_MODEL)
    row = functools.partial(_rowwise, n_rows=n_rows, tr=tr)

    w_pad_t = _pad_w_in_t(w["w_in_t"])
    norm_w = w["norm_w"].reshape(1, D_MODEL)
    ple_norm_w = w["ple_norm_w"].reshape(1, D_MODEL)
    final_w = w["final_norm_w"].reshape(1, D_MODEL)
    s5_d = w["s5_d"].reshape(1, S5_WIDTH)
    b_glu = w["s5_b_glu"].reshape(1, S5_WIDTH)
    conv_w = w["ssd_conv_w"].reshape(SSD_CONV, SSD_CONV_DIM)
    conv_b = w["ssd_conv_b"].reshape(1, SSD_CONV_DIM)
    ssd_norm_w = w["ssd_norm_w"].reshape(1, SSD_WIDTH)
    dtb, alog, dsk = (_pad_heads(w[k].reshape(1, SSD_HEADS)) for k in ("ssd_dt_bias", "ssd_a_log", "ssd_d"))

    gh = S5_GROUPS * S5_GROUP
    a_re = w["s5_a_re"].reshape(S5_GROUPS, S5_STATE)
    a_im = w["s5_a_im"].reshape(S5_GROUPS, S5_STATE)
    log_step = w["s5_log_step"].reshape(S5_GROUPS, 1)
    b_re2 = jnp.transpose(w["s5_b_re"].reshape(S5_GROUPS, S5_STATE, S5_GROUP), (0, 2, 1)).reshape(gh, S5_STATE)
    b_im2 = jnp.transpose(w["s5_b_im"].reshape(S5_GROUPS, S5_STATE, S5_GROUP), (0, 2, 1)).reshape(gh, S5_STATE)
    expand = (jnp.arange(gh)[:, None] // S5_GROUP == jnp.arange(S5_GROUPS)[None, :]).astype(F32)
    pow_re, pow_im, bb_re2, bb_im2 = _s5_params_fwd(a_re, a_im, log_step, b_re2, b_im2, expand)
    lam_pow = _state_cols((pow_re.reshape(S5_LOG_TB, S5_N), pow_im.reshape(S5_LOG_TB, S5_N)))
    lam_pow_conj = _state_cols((pow_re.reshape(S5_LOG_TB, S5_N), -pow_im.reshape(S5_LOG_TB, S5_N)))
    bb_band = _band(bb_re2, bb_im2).astype(BF16)
    c_band = _band(w["s5_c_re"].reshape(gh, S5_STATE), -w["s5_c_im"].reshape(gh, S5_STATE)).astype(BF16)

    (hn,) = row("rms_in", lambda r, q: ([_rms(r[0], q[0])], []), row_ins=[(x2, 0, D_MODEL)],
                par_ins=[(norm_w, 0, D_MODEL)], row_outs=[(D_MODEL, BF16)])
    proj = _matmul("mm_proj", hn, w_pad_t, tb=True)
    bu = _band_matmul("mm_s5_bu", "nn", proj, bb_band, a_blk0=OFF_U5 // BAND)
    s = _s5_scan_fwd(bu, lam_pow, n_seq, seq_len)
    yc, ge = _band_matmul("mm_s5_y", "nt", s, c_band, epilogue=lambda c, r, q: [c, jax.nn.gelu(c + q[0] * r[0])],
                          epi_rows=[(proj, OFF_U5 // BAND)], epi_pars=[s5_d], epi_outs=[F32, BF16])
    tg, y5 = _matmul("mm_s5_glu", ge, w["s5_w_glu"], epilogue=lambda c, r, q: [c, _s5_out(r[0], r[1], r[2], c, *q)],
                     epi_rows=[(yc, 0), (proj, OFF_U5), (proj, OFF_Z5)], epi_pars=[s5_d, b_glu], epi_outs=[F32, BF16])
    s5_rows = [(yc, 0, S5_WIDTH), (proj, OFF_U5, S5_WIDTH), (proj, OFF_Z5, S5_WIDTH), (tg, 0, S5_WIDTH)]
    s5_pars = [(s5_d, 0, S5_WIDTH), (b_glu, 0, S5_WIDTH)]

    xbc_act = _conv_fwd(proj, conv_w, conv_b, n_rows, seq_len)
    y_ssd, states = _ssd_fwd(xbc_act, proj, dtb, alog, dsk, n_seq, seq_len)
    gn_rows = [(y_ssd, 0, SSD_WIDTH), (proj, OFF_ZS, SSD_WIDTH)]
    (yss,) = row("ssd_gate", lambda r, q: ([_gated_norm(r[0], r[1], q[0])], []), row_ins=gn_rows,
                 par_ins=[(ssd_norm_w, 0, SSD_WIDTH)], row_outs=[(SSD_WIDTH, BF16)])

    m5 = _matmul("mm_br_s5", y5, w["w_br_s5"])
    ms, merged = _matmul("mm_br_ssd", yss, w["w_br_ssd"], epilogue=lambda c, r, q: [c, _merge(r[0], r[1], r[2], c)],
                         epi_rows=[(proj, OFF_G5), (proj, OFF_GS), (m5, 0)], epi_outs=[F32, BF16])
    mg_rows = [(proj, OFF_G5, D_MODEL), (proj, OFF_GS, D_MODEL), (m5, 0, D_MODEL), (ms, 0, D_MODEL)]

    def resid_norm(c, r, q):
        h1_ = r[0] + c
        return [h1_, _rms(h1_, q[0])]

    h1, hp = _matmul("mm_out", merged, w["w_out"], epilogue=resid_norm, epi_rows=[(x2, 0)], epi_pars=[ple_norm_w],
                     epi_outs=[F32, BF16], full_rows=True)
    pgl = _matmul("mm_ple_gate", hp, w["w_ple_gate"])
    pp = _matmul("mm_ple_proj", p2, w["w_ple_proj"])

    def head_fn(r, q):
        h1_, pgl_, pp_, tgt_ = r
        loss, vjp = jax.vjp(lambda a, b, c, f: _head_loss(a, b, c, f, tgt_), h1_, pgl_, pp_, q[0])
        dh1_, dpgl_, dpp_, dfw_ = vjp(jnp.ones_like(loss))
        return [dh1_, dpgl_, dpp_], [loss, dfw_]

    dh2, dpgl, dpp, loss_acc, d_final_w = row(
        "head", head_fn, row_ins=[(h1, 0, D_MODEL), (pgl, 0, D_MODEL), (pp, 0, D_MODEL), (t2, 0, D_MODEL)],
        par_ins=[(final_w, 0, D_MODEL)], row_outs=[(D_MODEL, F32), (D_MODEL, BF16), (D_MODEL, BF16)],
        acc_outs=[(1, 128), (1, D_MODEL)])
    loss = loss_acc[0, 0]

    g = {}
    g["final_norm_w"] = d_final_w
    g["w_ple_gate"] = _matmul("mm_d_w_ple_gate", hp, dpgl, ta=True)
    g["w_ple_proj"] = _matmul("mm_d_w_ple_proj", p2, dpp, ta=True)
    dhp = _matmul("mm_d_hp", dpgl, w["w_ple_gate"], tb=True)

    def ple_norm_bwd(r, q):
        h1_, dhp_, dh2_ = r
        _, vjp = jax.vjp(_rms, h1_, q[0])
        dh, dw = vjp(dhp_)
        dh = dh + dh2_
        return [dh, dh], [dw]

    dh1, dh1_b, g["ple_norm_w"] = row(
        "ple_norm_bwd", ple_norm_bwd, row_ins=[(h1, 0, D_MODEL), (dhp, 0, D_MODEL), (dh2, 0, D_MODEL)],
        par_ins=[(ple_norm_w, 0, D_MODEL)], row_outs=[(D_MODEL, F32), (D_MODEL, BF16)], acc_outs=[(1, D_MODEL)])
    g["w_out"] = _matmul("mm_d_w_out", merged, dh1_b, ta=True)
    dmerged = _matmul("mm_d_merged", dh1_b, w["w_out"], tb=True)

    dproj = lax.empty((n_rows, PROJ_W), BF16)

    def merge_bwd(r, q):
        _, vjp = jax.vjp(lambda gate, br: jax.nn.sigmoid(gate) * br, r[0], r[1])
        return list(vjp(r[2])), []

    dproj, dm5 = row("merge_bwd_s5", merge_bwd, row_ins=[mg_rows[0], mg_rows[2], (dmerged, 0, D_MODEL)], par_ins=[],
                     row_outs=[(D_MODEL, BF16)] * 2, into=(dproj, OFF_G5))
    dproj, dms = row("merge_bwd_ssd", merge_bwd, row_ins=[mg_rows[1], mg_rows[3], (dmerged, 0, D_MODEL)], par_ins=[],
                     row_outs=[(D_MODEL, BF16)] * 2, into=(dproj, OFF_GS))
    g["w_br_s5"] = _matmul("mm_d_w_br_s5", y5, dm5, ta=True)
    g["w_br_ssd"] = _matmul("mm_d_w_br_ssd", yss, dms, ta=True)
    dy5 = _matmul("mm_d_y5", dm5, w["w_br_s5"], tb=True)
    dyss = _matmul("mm_d_yss", dms, w["w_br_ssd"], tb=True)

    def s5_out_bwd_a(r, q):
        yc_, u_, z_, t_, dy_ = r
        d_, bg_ = q
        ge_ = jax.nn.gelu(yc_ + d_ * u_)
        _, vjp = jax.vjp(lambda a, z, t, b: a * jax.nn.sigmoid(t + b) * jax.nn.silu(z), ge_, z_, t_, bg_)
        dge, dz, dt_, dbg = vjp(dy_)
        return [dz, dge, dt_], [dbg]

    dproj, dge_a, dtg, g["s5_b_glu"] = row(
        "s5_out_bwd_a", s5_out_bwd_a, row_ins=s5_rows + [(dy5, 0, S5_WIDTH)], par_ins=s5_pars,
        row_outs=[(S5_WIDTH, BF16), (S5_WIDTH, F32), (S5_WIDTH, BF16)], acc_outs=[(1, S5_WIDTH)],
        into=(dproj, OFF_Z5))
    g["s5_w_glu"] = _matmul("mm_d_w_glu", ge, dtg, ta=True)
    dge_b = _matmul("mm_d_ge", dtg, w["s5_w_glu"], tb=True)

    def s5_out_bwd_b(r, q):
        yc_, u_, da_, db_ = r
        _, vjp = jax.vjp(lambda yc, u, d: jax.nn.gelu(yc + d * u), yc_, u_, q[0])
        dyc_, du_, dd_ = vjp(da_ + db_)
        return [dyc_, du_], [dd_]

    dyc, du5_a, g["s5_d"] = row(
        "s5_out_bwd_b", s5_out_bwd_b,
        row_ins=[(yc, 0, S5_WIDTH), (proj, OFF_U5, S5_WIDTH), (dge_a, 0, S5_WIDTH), (dge_b, 0, S5_WIDTH)],
        par_ins=[(s5_d, 0, S5_WIDTH)], row_outs=[(S5_WIDTH, BF16), (S5_WIDTH, F32)], acc_outs=[(1, S5_WIDTH)])
    d_c_band = _band_matmul("mm_d_c", "tn", dyc, s)
    ds = _band_matmul("mm_d_s", "nn", dyc, c_band)
    dbu, d_lam = _s5_scan_bwd(ds, s, lam_pow_conj, n_seq, seq_len)
    d_bb_band = _band_matmul("mm_d_bb", "tn", proj, dbu, a_blk0=OFF_U5 // BAND)
    (dproj,) = _band_matmul("mm_d_u5", "nt", dbu, bb_band, epilogue=lambda c, r, q: [c + r[0]],
                            epi_rows=[(du5_a, 0)], epi_outs=[BF16], into=(dproj, OFF_U5 // BAND))

    d_lr, d_li = _state_uncols(d_lam)
    d_bbr, d_bbi = _band_take(d_bb_band)
    d_are, d_aim, d_ls, d_br2, d_bi2 = _s5_params_bwd(
        a_re, a_im, log_step, b_re2, b_im2, expand,
        d_lr.reshape(S5_GROUPS, S5_STATE), d_li.reshape(S5_GROUPS, S5_STATE), d_bbr, d_bbi)
    g["s5_a_re"], g["s5_a_im"], g["s5_log_step"] = d_are, d_aim, d_ls
    g["s5_b_re_ghp"], g["s5_b_im_ghp"] = d_br2, d_bi2
    d_cr, d_ci = _band_take(d_c_band)
    g["s5_c_re"], g["s5_c_im"] = d_cr, -d_ci

    def gate_bwd(r, q):
        _, vjp = jax.vjp(_gated_norm, r[0], r[1], q[0])
        dy_, dz_, dw_ = vjp(r[2])
        return [dz_, dy_], [dw_]

    dproj, dy_ssd, g["ssd_norm_w"] = row(
        "ssd_gate_bwd", gate_bwd, row_ins=gn_rows + [(dyss, 0, SSD_WIDTH)], par_ins=[(ssd_norm_w, 0, SSD_WIDTH)],
        row_outs=[(SSD_WIDTH, BF16), (SSD_WIDTH, F32)], acc_outs=[(1, SSD_WIDTH)], into=(dproj, OFF_ZS))
    dxs, dbm, dcm, dproj, d_dtb, d_alog, d_dsk = _ssd_bwd(xbc_act, proj, states, dy_ssd, dtb, alog, dsk, n_seq, seq_len,
                                                       dproj)
    g["ssd_dt_bias"], g["ssd_a_log"], g["ssd_d"] = _unpad_heads(d_dtb), _unpad_heads(d_alog), _unpad_heads(d_dsk)
    conv_dw, conv_db = [], []
    for nm, d_act, off in (("x", dxs, 0), ("b", dbm, SSD_WIDTH), ("c", dcm, SSD_WIDTH + SSD_BC)):
        dproj, dw_, db_ = _conv_bwd("ssd_conv_bwd_" + nm, proj, d_act, conv_w, conv_b, n_rows, seq_len, off, dproj)
        conv_dw.append(dw_)
        conv_db.append(db_)
    g["ssd_conv_w"] = jnp.concatenate(conv_dw, axis=1)
    g["ssd_conv_b"] = jnp.concatenate(conv_db, axis=1)

    g["w_in_t"] = _unpad_w_in_t(_matmul("mm_d_w_in", dproj, hn, ta=True))
    dhn = _matmul("mm_d_hn", dproj, w_pad_t)

    def norm_bwd(r, q):
        x_, dhn_, dh1_ = r
        _, vjp = jax.vjp(_rms, x_, q[0])
        dx_, dw_ = vjp(dhn_)
        return [dx_ + dh1_], [dw_]

    dx, g["norm_w"] = row("rms_in_bwd", norm_bwd, row_ins=[(x2, 0, D_MODEL), (dhn, 0, D_MODEL), (dh1, 0, D_MODEL)],
                          par_ins=[(norm_w, 0, D_MODEL)], row_outs=[(D_MODEL, F32)], acc_outs=[(1, D_MODEL)])
    return loss, dx.reshape(x.shape), g


HBM = pl.BlockSpec(memory_space=pltpu.HBM)


def _chip_index(x, y):
    return 2 * x + y


def _half(shape2d, axis, which):
    h = shape2d[axis] // 2
    sl = pl.ds(pl.multiple_of(which * h, 128 if axis else 8), h)
    return (slice(None), sl) if axis else (sl, slice(None))


def _gather_chips(split, axes, whole):
    ns, nw = len(split), len(whole)
    n = ns + nw

    def body(*refs):
        ins, outs = refs[:n], refs[n:2 * n]
        ici_send, ici_recv, d2d_send, d2d_recv, local_sems = refs[2 * n:]
        x, y, c = lax.axis_index("x"), lax.axis_index("y"), lax.axis_index("c")
        me = _chip_index(x, y)
        sibling = (x, y, 1 - c)
        peers = [(1 - x, y), (x, 1 - y), (1 - x, 1 - y)]

        def half(t, which):
            return _half(split[t].shape, axes[t], which)

        copies = []
        for t in range(n):
            loc = pltpu.make_async_copy(ins[t], outs[t].at[me], local_sems.at[t])
            loc.start()
            copies.append(loc)

        def ici(t, k, slot):
            px, py = peers[k]
            if t < ns:
                src, dst = ins[t].at[half(t, c)], outs[t].at[(slot,) + half(t, c)]
            else:
                src, dst = ins[t], outs[t].at[slot]
            return pltpu.make_async_remote_copy(src_ref=src, dst_ref=dst, send_sem=ici_send.at[t, k],
                                                recv_sem=ici_recv.at[t, k], device_id=(px, py, c), device_id_type=MESH)

        def d2d(t, k, which):
            rows = outs[t].at[(_chip_index(*peers[k]),) + half(t, which)]
            return pltpu.make_async_remote_copy(src_ref=rows, dst_ref=rows, send_sem=d2d_send.at[t, k],
                                                recv_sem=d2d_recv.at[t, k], device_id=sibling, device_id_type=MESH)

        sends = []
        for t in range(n):
            for k in range(3):
                cp = ici(t, k, me)
                cp.start()
                sends.append(cp)
        for t in range(n):
            for k in range(3):
                ici(t, k, _chip_index(*peers[k])).wait_recv()
                if t < ns:
                    cp = d2d(t, k, c)
                    cp.start()
                    sends.append(cp)
        for t in range(ns):
            for k in range(3):
                d2d(t, k, 1 - c).wait_recv()
        for cp in sends:
            cp.wait_send()
        for cp in copies:
            cp.wait()

    arrays = list(split) + list(whole)
    return pl.pallas_call(
        body, name="gather_weights",
        in_specs=[HBM] * n, out_specs=[HBM] * n,
        out_shape=[jax.ShapeDtypeStruct((N_CHIPS,) + a.shape, a.dtype) for a in arrays],
        scratch_shapes=[pltpu.SemaphoreType.DMA((n, 3)), pltpu.SemaphoreType.DMA((n, 3)),
                        pltpu.SemaphoreType.DMA((ns, 3)), pltpu.SemaphoreType.DMA((ns, 3)),
                        pltpu.SemaphoreType.DMA((n,))],
    )(*arrays)


def _half_shape(shape2d, axis):
    r, c = shape2d
    return (r, c // 2) if axis else (r // 2, c)


def _swap_halves(slotted, axes, small):
    n = len(slotted)

    def body(*refs):
        ins, sm_in = refs[:n], refs[n]
        outs, sm_out = refs[n + 1:2 * n + 1], refs[2 * n + 1]
        send_sems, recv_sems, local_sem, sm_send, sm_recv = refs[2 * n + 2:]
        x, y, c = lax.axis_index("x"), lax.axis_index("y"), lax.axis_index("c")
        dev = 4 * x + 2 * y + c
        local = pltpu.make_async_copy(sm_in, sm_out.at[dev], local_sem)
        local.start()
        sends = []
        for t in range(n):
            other = (slice(None),) + _half(slotted[t].shape[1:], axes[t], 1 - c)
            cp = pltpu.make_async_remote_copy(
                src_ref=ins[t].at[other], dst_ref=outs[t], send_sem=send_sems.at[t],
                recv_sem=recv_sems.at[t], device_id=(x, y, 1 - c), device_id_type=MESH)
            cp.start()
            sends.append(cp)
        rel = [(fx, fy, fc) for fx in (0, 1) for fy in (0, 1) for fc in (0, 1)][1:]
        for k, (fx, fy, fc) in enumerate(rel):
            cp = pltpu.make_async_remote_copy(
                src_ref=sm_in, dst_ref=sm_out.at[dev], send_sem=sm_send.at[k], recv_sem=sm_recv.at[k],
                device_id=(x ^ fx, y ^ fy, c ^ fc), device_id_type=MESH)
            cp.start()
            sends.append(cp)
        for cp in sends[:n]:
            cp.wait_recv()
        for k, (fx, fy, fc) in enumerate(rel):
            src_dev = 4 * (x ^ fx) + 2 * (y ^ fy) + (c ^ fc)
            pltpu.make_async_remote_copy(
                src_ref=sm_in, dst_ref=sm_out.at[src_dev], send_sem=sm_send.at[k], recv_sem=sm_recv.at[k],
                device_id=(x ^ fx, y ^ fy, c ^ fc), device_id_type=MESH).wait_recv()
        for cp in sends:
            cp.wait_send()
        local.wait()

    return pl.pallas_call(
        body, name="swap_halves",
        in_specs=[HBM] * (n + 1), out_specs=[HBM] * (n + 1),
        out_shape=[jax.ShapeDtypeStruct((a.shape[0],) + _half_shape(a.shape[1:], ax), a.dtype)
                   for a, ax in zip(slotted, axes, strict=True)]
        + [jax.ShapeDtypeStruct((N_DEV,) + small.shape, small.dtype)],
        scratch_shapes=[pltpu.SemaphoreType.DMA((n,)), pltpu.SemaphoreType.DMA((n,)), pltpu.SemaphoreType.DMA,
                        pltpu.SemaphoreType.DMA((7,)), pltpu.SemaphoreType.DMA((7,))],
    )(*slotted, small)


def _scatter_halves(parts):
    n = len(parts)

    def body(*refs):
        ins, outs = refs[:n], refs[n:2 * n]
        send_sems, recv_sems = refs[2 * n:]
        x, y, c = lax.axis_index("x"), lax.axis_index("y"), lax.axis_index("c")
        peers = [(1 - x, y), (x, 1 - y), (1 - x, 1 - y)]
        sends = []
        for t in range(n):
            for k, (px, py) in enumerate(peers):
                cp = pltpu.make_async_remote_copy(
                    src_ref=ins[t].at[_chip_index(px, py)], dst_ref=outs[t].at[k], send_sem=send_sems.at[t, k],
                    recv_sem=recv_sems.at[t, k], device_id=(px, py, c), device_id_type=MESH)
                cp.start()
                sends.append(cp)
        for cp in sends:
            cp.wait_recv()
        for cp in sends:
            cp.wait_send()

    return pl.pallas_call(
        body, name="scatter_halves",
        in_specs=[HBM] * n, out_specs=[HBM] * n,
        out_shape=[jax.ShapeDtypeStruct((3,) + a.shape[1:], a.dtype) for a in parts],
        scratch_shapes=[pltpu.SemaphoreType.DMA((n, 3)), pltpu.SemaphoreType.DMA((n, 3))],
    )(*parts)


def _tiling(r, c, f32_per_elem):
    if r % 8 == 0:
        tr = _row_tile(r, f32_per_elem * c * 4)
        return r // tr, (tr, c), lambda i: (i, 0)
    assert c % 128 == 0, (r, c)
    return c // 128, (r, 128), lambda i: (0, i)


def _pair_sum(name, slotted, other, idx, axis):
    _, r, c = slotted.shape
    hr, hc = _half_shape((r, c), axis)
    n, (tr, tc), at = _tiling(hr, hc, 13)
    if axis == 0:
        a = slotted.reshape(N_CHIPS, 2, hr, c)
        a_all = pl.BlockSpec((N_CHIPS, 1, tr, tc), lambda i, s: (0, s[0]) + at(i))
        a_own = pl.BlockSpec((1, 1, tr, tc), lambda i, s: (s[1], s[0]) + at(i))
    else:
        a, per_half = slotted, hc // tc
        a_all = pl.BlockSpec((N_CHIPS, tr, tc), lambda i, s: (0, at(i)[0], s[0] * per_half + at(i)[1]))
        a_own = pl.BlockSpec((1, tr, tc), lambda i, s: (s[1], at(i)[0], s[0] * per_half + at(i)[1]))

    def body(idx_ref, a_ref, b_ref, am_ref, bm_ref, p_ref, own_ref):
        mine, mine_own = (a_ref[:, 0], am_ref[0, 0]) if axis == 0 else (a_ref[...], am_ref[0])
        p_ref[...] = (mine + b_ref[...]).astype(p_ref.dtype)
        own_ref[...] = mine_own + bm_ref[0]

    grid_spec = pltpu.PrefetchScalarGridSpec(
        num_scalar_prefetch=1, grid=(n,),
        in_specs=[a_all, pl.BlockSpec((N_CHIPS, tr, tc), lambda i, s: (0,) + at(i)),
                  a_own, pl.BlockSpec((1, tr, tc), lambda i, s: (s[1],) + at(i))],
        out_specs=[pl.BlockSpec((N_CHIPS, tr, tc), lambda i, s: (0,) + at(i)),
                   pl.BlockSpec((tr, tc), lambda i, s: at(i))])
    return pl.pallas_call(
        body, name=name, grid_spec=grid_spec,
        out_shape=[jax.ShapeDtypeStruct((N_CHIPS, hr, hc), BF16), jax.ShapeDtypeStruct((hr, hc), F32)],
        compiler_params=_cparams(("parallel",)),
    )(idx, a, other, a, other)


def _sum_parts(name, own, recv):
    h, c = own.shape
    n, (tr, tc), at = _tiling(h, c, 4)

    def body(o_ref, r_ref, out_ref):
        acc = o_ref[...]
        for k in range(3):
            acc = acc + r_ref[k].astype(F32)
        out_ref[...] = acc

    return pl.pallas_call(
        body, name=name, grid=(n,),
        in_specs=[pl.BlockSpec((tr, tc), at), pl.BlockSpec((3, tr, tc), lambda i: (0,) + at(i))],
        out_specs=pl.BlockSpec((tr, tc), at),
        out_shape=jax.ShapeDtypeStruct((h, c), F32),
        compiler_params=_cparams(("parallel",)),
    )(own, recv)


def _swap_sibling(parts):
    n = len(parts)

    def body(*refs):
        ins, outs = refs[:n], refs[n:2 * n]
        send_sems, recv_sems = refs[2 * n:]
        x, y, c = lax.axis_index("x"), lax.axis_index("y"), lax.axis_index("c")
        cps = []
        for t in range(n):
            cp = pltpu.make_async_remote_copy(
                src_ref=ins[t], dst_ref=outs[t], send_sem=send_sems.at[t], recv_sem=recv_sems.at[t],
                device_id=(x, y, 1 - c), device_id_type=MESH)
            cp.start()
            cps.append(cp)
        for cp in cps:
            cp.wait_recv()
        for cp in cps:
            cp.wait_send()

    return pl.pallas_call(
        body, name="swap_sibling",
        in_specs=[HBM] * n, out_specs=[HBM] * n,
        out_shape=[jax.ShapeDtypeStruct(a.shape, a.dtype) for a in parts],
        scratch_shapes=[pltpu.SemaphoreType.DMA((n,)), pltpu.SemaphoreType.DMA((n,))],
    )(*parts)


def _sum_slots(name, a):
    k, r, c = a.shape
    tr = _row_tile(r, (k + 1) * c * 4)

    def body(a_ref, o_ref):
        acc = a_ref[0]
        for i in range(1, k):
            acc = acc + a_ref[i]
        o_ref[...] = acc

    return pl.pallas_call(
        body, name=name, grid=(r // tr,),
        in_specs=[pl.BlockSpec((k, tr, c), lambda i: (0, i, 0))],
        out_specs=pl.BlockSpec((tr, c), lambda i: (i, 0)),
        out_shape=jax.ShapeDtypeStruct((r, c), a.dtype),
        compiler_params=_cparams(("parallel",)),
    )(a)


def _adamw(name, w, m, v, g_parts):
    r, c = w.shape
    ng = len(g_parts)
    tr = _row_tile(r, (7 + ng) * c * 4)
    c1 = 1.0 - ADAM_B1 ** ADAM_STEP
    c2 = 1.0 - ADAM_B2 ** ADAM_STEP

    def body(*refs):
        w_ref, m_ref, v_ref = refs[:3]
        g_refs = refs[3:3 + ng]
        go_ref, d_ref, mo_ref, vo_ref = refs[3 + ng:]
        g = g_refs[0][...]
        for gr in g_refs[1:]:
            g = g + gr[...]
        m_new = ADAM_B1 * m_ref[...] + (1.0 - ADAM_B1) * g
        v_new = ADAM_B2 * v_ref[...] + (1.0 - ADAM_B2) * (g * g)
        go_ref[...] = g
        mo_ref[...] = m_new
        vo_ref[...] = v_new
        d_ref[...] = -ADAM_LR * ((m_new / c1) / (jnp.sqrt(v_new / c2) + ADAM_EPS) + ADAM_WD * w_ref[...])

    spec = pl.BlockSpec((tr, c), lambda i: (i, 0))
    return pl.pallas_call(
        body, name=name, grid=(r // tr,),
        in_specs=[spec] * (3 + ng), out_specs=[spec] * 4,
        out_shape=[jax.ShapeDtypeStruct((r, c), F32)] * 4,
        compiler_params=_cparams(("parallel",)),
    )(w, m, v, *g_parts)


def _adamw_halves(name, w, m, v, own, other, idx, axis):
    hr, hc = own.shape
    nb, (tr, tc), at = _tiling(hr, hc, 9)
    c1 = 1.0 - ADAM_B1 ** ADAM_STEP
    c2 = 1.0 - ADAM_B2 ** ADAM_STEP

    def body(idx_ref, w_ref, m_ref, v_ref, own_ref, oth_ref, go_ref, d_ref, mo_ref, vo_ref):
        g = jnp.where(pl.program_id(0) == idx_ref[0], own_ref[...], oth_ref[...])
        m_new = ADAM_B1 * m_ref[...] + (1.0 - ADAM_B1) * g
        v_new = ADAM_B2 * v_ref[...] + (1.0 - ADAM_B2) * (g * g)
        go_ref[...] = g
        mo_ref[...] = m_new
        vo_ref[...] = v_new
        d_ref[...] = -ADAM_LR * ((m_new / c1) / (jnp.sqrt(v_new / c2) + ADAM_EPS) + ADAM_WD * w_ref[...])

    per_half = (hc // tc) if axis else (hr // tr)
    if axis:
        full = pl.BlockSpec((tr, tc), lambda hh, i, s: (at(i)[0], hh * per_half + at(i)[1]))
    else:
        full = pl.BlockSpec((tr, tc), lambda hh, i, s: (hh * per_half + at(i)[0], at(i)[1]))
    part = pl.BlockSpec((tr, tc), lambda hh, i, s: at(i))
    grid_spec = pltpu.PrefetchScalarGridSpec(
        num_scalar_prefetch=1, grid=(2, nb), in_specs=[full, full, full, part, part], out_specs=[full] * 4)
    return pl.pallas_call(
        body, name=name, grid_spec=grid_spec, out_shape=[jax.ShapeDtypeStruct(w.shape, F32)] * 4,
        compiler_params=_cparams(("parallel", "parallel")),
    )(idx, w, m, v, own, other)


WEIGHTS = ['norm_w', 'w_in', 's5_a_re', 's5_a_im', 's5_b_re', 's5_b_im', 's5_c_re', 's5_c_im', 's5_d', 's5_log_step',
           's5_w_glu', 's5_b_glu', 'ssd_conv_w', 'ssd_conv_b', 'ssd_dt_bias', 'ssd_a_log', 'ssd_d', 'ssd_norm_w',
           'w_br_s5', 'w_br_ssd', 'w_out', 'ple_norm_w', 'w_ple_gate', 'w_ple_proj', 'final_norm_w']
SHARDED = {'w_in': ((IN_PROJ_DIM, 1024), 0), 's5_w_glu': ((512, 512), 0), 'ssd_conv_w': ((SSD_CONV, SSD_CONV_DIM), 1),
           'w_br_s5': ((512, 1024), 1), 'w_br_ssd': ((1536, 1024), 0), 'w_out': ((1024, 1024), 0),
           'w_ple_gate': ((1024, 1024), 0), 'w_ple_proj': ((256, 1024), 1)}
TRANSPOSED = ('w_in',)
SMALL = [n for n in WEIGHTS if n not in SHARDED]


def _shard_shape(name):
    (r, c), ax = SHARDED[name]
    return (r // N_CHIPS, c) if ax == 0 else (r, c // N_CHIPS)


def _half_axis(name):
    return 0 if (_shard_shape(name)[0] // 2) % 16 == 0 else 1


def _shard2d(name, a):
    r, c = _shard_shape(name)
    return a.reshape(c, r).T if name in TRANSPOSED else a.reshape(r, c)


def _unshard2d(name, a2, shape):
    return (a2.T if name in TRANSPOSED else a2).reshape(shape)


def _unslot(name, a4):
    (r, c), ax = SHARDED[name]
    if ax == 0:
        return a4.reshape(r, c)
    return jnp.transpose(a4, (1, 0, 2)).reshape(r, c)


def _slot(name, full):
    (r, c), ax = SHARDED[name]
    if ax == 0:
        return full.reshape(N_CHIPS, r // N_CHIPS, c)
    return jnp.transpose(full.reshape(r, N_CHIPS, c // N_CHIPS), (1, 0, 2))


GHP = ('s5_b_re', 's5_b_im')


def _view_shape(name):
    if name in ('s5_a_re', 's5_a_im'):
        return (S5_GROUPS, S5_STATE)
    if name in GHP + ('s5_c_re', 's5_c_im'):
        return (S5_GROUPS, S5_GROUP, S5_STATE)
    if name == 'ssd_conv_w':
        return (SSD_CONV, SSD_CONV_DIM // N_CHIPS)
    return (1, {'s5_log_step': S5_GROUPS, 'ssd_conv_b': SSD_CONV_DIM, 'ssd_norm_w': SSD_WIDTH, 's5_d': S5_WIDTH,
                's5_b_glu': S5_WIDTH, 'ssd_dt_bias': SSD_HEADS, 'ssd_a_log': SSD_HEADS, 'ssd_d': SSD_HEADS}.get(name, D_MODEL))


def _view(name, a):
    if name in GHP:
        return jnp.swapaxes(a.reshape(S5_GROUPS, S5_STATE, S5_GROUP), 1, 2)
    return a.reshape(_view_shape(name))


def _unview(name, a, shape):
    return (jnp.swapaxes(a, 1, 2) if name in GHP else a).reshape(shape)


def _adamw_small(ws, ms, vs, gs):
    n = len(ws)
    c1 = 1.0 - ADAM_B1 ** ADAM_STEP
    c2 = 1.0 - ADAM_B2 ** ADAM_STEP

    def body(*refs):
        w_r, m_r, v_r, g_r = (refs[k * n:(k + 1) * n] for k in range(4))
        d_o, m_o, v_o = (refs[k * n:(k + 1) * n] for k in range(4, 7))
        for i in range(n):
            g = g_r[i][...]
            m_new = ADAM_B1 * m_r[i][...] + (1.0 - ADAM_B1) * g
            v_new = ADAM_B2 * v_r[i][...] + (1.0 - ADAM_B2) * (g * g)
            m_o[i][...] = m_new
            v_o[i][...] = v_new
            d_o[i][...] = -ADAM_LR * ((m_new / c1) / (jnp.sqrt(v_new / c2) + ADAM_EPS) + ADAM_WD * w_r[i][...])

    return pl.pallas_call(
        body, name="adamw_small", out_shape=[jax.ShapeDtypeStruct(w.shape, F32) for w in ws] * 3,
        compiler_params=pltpu.CompilerParams(vmem_limit_bytes=VMEM_LIMIT),
    )(*ws, *ms, *vs, *gs)


def _pack_small(vals):
    flat = jnp.concatenate([v.reshape(-1).astype(F32) for v in vals])
    rows = -(-flat.shape[0] // (256 * 128)) * 256
    return jnp.pad(flat, (0, rows * 128 - flat.shape[0])).reshape(rows, 128)


def _unpack_small(packed, shapes):
    flat = packed.reshape(-1)
    out, off = [], 0
    for sh in shapes:
        n = math.prod(sh)
        out.append(flat[off:off + n].reshape(sh))
        off += n
    return out


def kernel(x, p, norm_w, w_in, s5_a_re, s5_a_im, s5_b_re, s5_b_im, s5_c_re, s5_c_im, s5_d, s5_log_step, s5_w_glu, s5_b_glu, ssd_conv_w, ssd_conv_b, ssd_dt_bias, ssd_a_log, ssd_d, ssd_norm_w, w_br_s5, w_br_ssd, w_out, ple_norm_w, w_ple_gate, w_ple_proj, final_norm_w, loss_target, m_norm_w, m_w_in, m_s5_a_re, m_s5_a_im, m_s5_b_re, m_s5_b_im, m_s5_c_re, m_s5_c_im, m_s5_d, m_s5_log_step, m_s5_w_glu, m_s5_b_glu, m_ssd_conv_w, m_ssd_conv_b, m_ssd_dt_bias, m_ssd_a_log, m_ssd_d, m_ssd_norm_w, m_w_br_s5, m_w_br_ssd, m_w_out, m_ple_norm_w, m_w_ple_gate, m_w_ple_proj, m_final_norm_w, v_norm_w, v_w_in, v_s5_a_re, v_s5_a_im, v_s5_b_re, v_s5_b_im, v_s5_c_re, v_s5_c_im, v_s5_d, v_s5_log_step, v_s5_w_glu, v_s5_b_glu, v_ssd_conv_w, v_ssd_conv_b, v_ssd_dt_bias, v_ssd_a_log, v_ssd_d, v_ssd_norm_w, v_w_br_s5, v_w_br_ssd, v_w_out, v_ple_norm_w, v_w_ple_gate, v_w_ple_proj, v_final_norm_w):
    args = locals()
    wl = {n: args[n] for n in WEIGHTS}
    ml = {n: args["m_" + n] for n in WEIGHTS}
    vl = {n: args["v_" + n] for n in WEIGHTS}
    big = [n for n in SHARDED if n != 'ssd_conv_w']
    chip = _chip_index(lax.axis_index("x"), lax.axis_index("y"))
    idx = jnp.stack([lax.axis_index("c"), chip]).astype(jnp.int32)

    axes = [_half_axis(n) for n in big]
    gathered = _gather_chips([_shard2d(n, wl[n]).astype(BF16) for n in big], axes,
                             [_shard2d('ssd_conv_w', wl['ssd_conv_w'])])
    full = {n: wl[n] for n in SMALL}
    for n, a4 in zip(big + ['ssd_conv_w'], gathered, strict=True):
        if n in TRANSPOSED:
            full[n + "_t"] = a4
        else:
            full[n] = _unslot(n, a4)

    loss, grad_x, g = _local_step(x, p[0], loss_target, full)
    for n in TRANSPOSED:
        g[n] = g.pop(n + "_t")

    for n in ('s5_b_re', 's5_b_im'):
        g[n] = g.pop(n + "_ghp")
    small_shapes = [(1, 1)] + [_view_shape(n) for n in SMALL] + [(SSD_CONV, SSD_CONV_DIM)]
    small_pack = _pack_small([loss] + [g[n] for n in SMALL] + [g['ssd_conv_w']])
    slotted = [g[n] if n in TRANSPOSED else _slot(n, g[n]) for n in big]
    swapped = _swap_halves(slotted, axes, small_pack)
    pair = [_pair_sum("pair_sum_" + n, a, b, idx, ax)
            for n, a, b, ax in zip(big, slotted, swapped[:-1], axes, strict=True)]
    small_sum = _sum_slots("sum_small", swapped[-1])
    recv = _scatter_halves([pb for pb, _ in pair])
    halves = [_sum_parts("sum_chips_" + n, own, r) for n, (_, own), r in zip(big, pair, recv, strict=True)]
    other_halves = _swap_sibling(halves)

    out_g, out_d, out_m, out_v = {}, {}, {}, {}
    for n, own, oth, ax in zip(big, halves, other_halves, axes, strict=True):
        res = _adamw_halves("adamw_" + n, _shard2d(n, wl[n]), _shard2d(n, ml[n]), _shard2d(n, vl[n]), own, oth, idx, ax)
        out_g[n], out_d[n], out_m[n], out_v[n] = (_unshard2d(n, r, wl[n].shape) for r in res)
    sm = _unpack_small(small_sum, small_shapes)
    loss_total = sm[0].reshape(())
    conv_g = lax.dynamic_slice(sm[-1], (0, chip * (SSD_CONV_DIM // N_CHIPS)), (SSD_CONV, SSD_CONV_DIM // N_CHIPS))
    names = SMALL + ['ssd_conv_w']
    grads = sm[1:-1] + [conv_g]
    res = _adamw_small([_view(n, wl[n]) for n in names], [_view(n, ml[n]) for n in names],
                       [_view(n, vl[n]) for n in names], grads)
    for i, n in enumerate(names):
        out_g[n] = _unview(n, grads[i], wl[n].shape)
        out_d[n], out_m[n], out_v[n] = (_unview(n, res[k * len(names) + i], wl[n].shape) for k in range(3))

    return (loss_total, grad_x, *[out_g[n] for n in WEIGHTS], *[out_d[n] for n in WEIGHTS],
            *[out_m[n] for n in WEIGHTS], *[out_v[n] for n in WEIGHTS])
```

```python
import functools
import math

import jax
import jax.numpy as jnp
from jax import lax
from jax.experimental import pallas as pl
from jax.experimental.pallas import tpu as pltpu

F32 = jnp.float32
BF16 = jnp.bfloat16
MESH = pl.DeviceIdType.MESH

D_MODEL = 1024
PLE_DIM = 256
RMS_EPS = 1e-6
S5_WIDTH = 512
S5_GROUP = 16
S5_GROUPS = 32
S5_STATE = 64
S5_N = S5_GROUPS * S5_STATE
S5_LB = 512
S5_NJ = S5_N // S5_LB
S5_TB = 256
S5_LOG_TB = 8
SSD_WIDTH = 1536
SSD_HEADDIM = 64
SSD_HEADS = 24
SSD_GROUPS = 4
SSD_HPG = 6
SSD_STATE = 128
SSD_CONV = 4
SSD_CHUNK = 128
SSD_BC = 512
SSD_CONV_DIM = 2560
GROUP_W = SSD_WIDTH // SSD_GROUPS
N_CHIPS = 4
N_DEV = 8

OFF_XBC, OFF_U5, OFF_Z5, OFF_DT, OFF_G5, OFF_GS, OFF_ZS = 0, 2560, 3072, 3584, 4096, 5120, 6144
DT_W = 512
PROJ_W = 7680
IN_PROJ_DIM = 7192

ADAM_LR, ADAM_B1, ADAM_B2, ADAM_EPS, ADAM_WD, ADAM_STEP = 0.001, 0.9, 0.999, 1e-08, 0.01, 10

VMEM_LIMIT = 56 * 1024 * 1024


def _pick(n, cands):
    for c in cands:
        if n % c == 0:
            return c
    return n


ROW_BLOCK_BYTES = 8 * 1024 * 1024


def _row_tile(r, bytes_per_row):
    for t in (r, 4096, 2048, 1024, 512, 256, 128, 64, 32, 16, 8):
        if t <= r and r % t == 0 and t * bytes_per_row <= ROW_BLOCK_BYTES:
            return t
    return r


def _cparams(sem):
    return pltpu.CompilerParams(dimension_semantics=sem, vmem_limit_bytes=VMEM_LIMIT)


def _dg(a, b, ca, cb):
    return lax.dot_general(a.astype(BF16), b.astype(BF16), (((ca,), (cb,)), ((), ())), preferred_element_type=F32)


@jax.custom_vjp
def dot_nn(a, b):
    return _dg(a, b, 1, 0)


@jax.custom_vjp
def dot_nt(a, b):
    return _dg(a, b, 1, 1)


@jax.custom_vjp
def dot_tn(a, b):
    return _dg(a, b, 0, 0)


dot_nn.defvjp(lambda a, b: (_dg(a, b, 1, 0), (a, b)), lambda r, g: (_dg(g, r[1], 1, 1), _dg(r[0], g, 0, 0)))
dot_nt.defvjp(lambda a, b: (_dg(a, b, 1, 1), (a, b)), lambda r, g: (_dg(g, r[1], 1, 0), _dg(g, r[0], 0, 0)))
dot_tn.defvjp(lambda a, b: (_dg(a, b, 0, 0), (a, b)), lambda r, g: (_dg(r[1], g, 1, 1), _dg(r[0], g, 1, 0)))


MM_VMEM_BUDGET = 30 * 1024 * 1024


def _mm_tiles(m, n, k, sa, sb, so, tn_only=None):
    best, best_key = None, None
    for tm in (1024, 512, 256, 128, 64, 32, 16, 8):
        if m % tm:
            continue
        for tn in (2048, 1536, 1280, 1024, 768, 640, 512, 384, 256, 128):
            if n % tn or (tn_only is not None and tn not in tn_only):
                continue
            for tk in (k, 2048, 1536, 1280, 1024, 768, 512, 256, 128):
                if k % tk or tk > max(k, 128) or (tk == k and k > 2048 and k % 128 == 0):
                    continue
                need = 2 * (tm * tk * sa + tk * tn * sb + tm * tn * so) + (tm * tn * 4 if tk < k else 0)
                if need > MM_VMEM_BUDGET:
                    continue
                key = (tm * tn * tk, tk)
                if best_key is None or key > best_key:
                    best, best_key = (tm, tn, tk), key
    assert best is not None, (m, n, k)
    return best


def _matmul(name, a, b, *, ta=False, tb=False, a_win=None, out_dtype=F32, epilogue=None, epi_rows=(), epi_pars=(),
            epi_outs=(), full_rows=False, epi_accs=(), epi_into=None):
    a_off, a_w = a_win if a_win is not None else (0, a.shape[1])
    if ta:
        kdim, m = a.shape[0], a_w
    else:
        m, kdim = a.shape[0], a_w
    n = b.shape[0] if tb else b.shape[1]
    assert (b.shape[1] if tb else b.shape[0]) == kdim, (name, a.shape, b.shape)
    out_dtypes = list(epi_outs) if epilogue is not None else [out_dtype]
    so = sum(jnp.dtype(d).itemsize for d in out_dtypes) + sum(r.dtype.itemsize for r, _ in epi_rows)
    tn_ok = [n] if full_rows else [t for t in (1024, 512, 256, 128) if all(off % t == 0 for _, off in epi_rows)]
    tm, tn, tk = _mm_tiles(m, n, kdim, a.dtype.itemsize, b.dtype.itemsize, so, tn_ok if epilogue is not None else None)
    nk = kdim // tk
    n_er, n_ep, n_out = len(epi_rows), len(epi_pars), len(out_dtypes)
    if ta:
        assert a_off % tm == 0
        a_spec = pl.BlockSpec((tk, tm), lambda i, j, k: (k, i + a_off // tm))
    else:
        assert a_off % tk == 0
        a_spec = pl.BlockSpec((tm, tk), lambda i, j, k: (i, k + a_off // tk))
    if tb:
        b_spec = pl.BlockSpec((tn, tk), lambda i, j, k: (j, k))
    else:
        b_spec = pl.BlockSpec((tk, tn), lambda i, j, k: (k, j))
    ca, cb = (0 if ta else 1), (1 if tb else 0)

    n_acc = len(epi_accs)
    n_x = 1 if epi_into is not None else 0
    assert not (n_acc or n_x) or full_rows

    def body(a_ref, b_ref, *refs):
        er, ep = refs[:n_er], refs[n_er:n_er + n_ep]
        first_out = n_er + n_ep + n_x
        o_refs = refs[first_out:first_out + n_out]
        s_refs = refs[first_out + n_out:first_out + n_out + n_acc]
        acc = refs[first_out + n_out + n_acc:]

        def finish(c):
            outs = [c] if epilogue is None else epilogue(c, [r[...] for r in er], [p[...] for p in ep])
            if n_acc:
                outs, sums = outs

                @pl.when(pl.program_id(0) == 0)
                def _():
                    for s_ref in s_refs:
                        s_ref[...] = jnp.zeros_like(s_ref)

                for s_ref, s in zip(s_refs, sums, strict=True):
                    s_ref[...] += jnp.broadcast_to(s, s_ref.shape)
            for o_ref, o in zip(o_refs, outs, strict=True):
                o_ref[...] = o.astype(o_ref.dtype)

        if nk == 1:
            finish(_dg(a_ref[...], b_ref[...], ca, cb))
            return
        (acc_ref,) = acc
        k = pl.program_id(2)

        @pl.when(k == 0)
        def _():
            acc_ref[...] = jnp.zeros_like(acc_ref)

        acc_ref[...] += _dg(a_ref[...], b_ref[...], ca, cb)

        @pl.when(k == nk - 1)
        def _():
            finish(acc_ref[...])

    in_specs = [a_spec, b_spec]
    in_specs += [pl.BlockSpec((tm, tn), functools.partial(lambda i, j, k, c: (i, j + c), c=off // tn)) for _, off in epi_rows]
    in_specs += [pl.BlockSpec((p.shape[0], tn), lambda i, j, k: (0, j)) for p in epi_pars]
    out_specs = [pl.BlockSpec((tm, tn), lambda i, j, k: (i, j)) for _ in out_dtypes]
    out_shape = [jax.ShapeDtypeStruct((m, n), d) for d in out_dtypes]
    extra, aliases = [], {}
    if epi_into is not None:
        buf, off, width = epi_into
        assert off % width == 0 and buf.dtype == out_dtypes[0]
        in_specs.append(pl.BlockSpec(memory_space=pl.ANY))
        out_specs[0] = pl.BlockSpec((tm, width), functools.partial(lambda i, j, k, c: (i, c), c=off // width))
        out_shape[0] = jax.ShapeDtypeStruct(buf.shape, buf.dtype)
        extra, aliases = [buf], {2 + n_er + n_ep: 0}
    out_specs += [pl.BlockSpec((r, w), lambda i, j, k: (0, 0)) for r, w in epi_accs]
    out_shape += [jax.ShapeDtypeStruct((r, w), F32) for r, w in epi_accs]
    res = pl.pallas_call(
        body, name=name, grid=(m // tm, n // tn, nk),
        in_specs=in_specs, out_specs=out_specs, out_shape=out_shape, input_output_aliases=aliases,
        scratch_shapes=[pltpu.VMEM((tm, tn), F32)] if nk > 1 else [],
        compiler_params=_cparams(("arbitrary",) * 3 if n_acc else ("parallel", "parallel", "arbitrary")),
    )(a, b, *[r for r, _ in epi_rows], *epi_pars, *extra)
    return res if epilogue is not None else res[0]


BAND = 128


def _band_matmul(name, kind, a, b, *, a_blk0=0, out_dtype=F32, epilogue=None, epi_rows=(), epi_pars=(), epi_outs=(),
                 into=None):
    n_rows = a.shape[0]
    blk = 2 * S5_LB
    tm = _pick(n_rows, (1024, 512, 256))
    if epilogue is not None:
        assert kind == "nt"
        n_er, n_ep = len(epi_rows), len(epi_pars)
        in_specs = [pl.BlockSpec((tm, blk), lambda i, j: (i, j)), pl.BlockSpec((BAND, blk), lambda i, j: (j, 0))]
        in_specs += [pl.BlockSpec((tm, BAND), functools.partial(lambda i, j, c: (i, c + j), c=c0)) for _, c0 in epi_rows]
        in_specs += [pl.BlockSpec((1, BAND), lambda i, j: (0, j)) for _ in epi_pars]
        out_specs = [pl.BlockSpec((tm, BAND), lambda i, j: (i, j)) for _ in epi_outs]
        out_shape = [jax.ShapeDtypeStruct((n_rows, S5_NJ * BAND), d) for d in epi_outs]
        extra, aliases = [], {}
        if into is not None:
            buf, c0 = into
            in_specs.append(pl.BlockSpec(memory_space=pl.ANY))
            out_specs[0] = pl.BlockSpec((tm, BAND), functools.partial(lambda i, j, c: (i, c + j), c=c0))
            out_shape[0] = jax.ShapeDtypeStruct(buf.shape, buf.dtype)
            extra, aliases = [buf], {2 + n_er + n_ep: 0}

        def epi_body(a_ref, b_ref, *refs):
            outs = epilogue(_dg(a_ref[...], b_ref[...], 1, 1), [r[...] for r in refs[:n_er]],
                            [p[...] for p in refs[n_er:n_er + n_ep]])
            for o_ref, o in zip(refs[n_er + n_ep + len(extra):], outs, strict=True):
                o_ref[...] = o.astype(o_ref.dtype)

        return pl.pallas_call(
            epi_body, name=name, grid=(n_rows // tm, S5_NJ), in_specs=in_specs, out_specs=out_specs,
            out_shape=out_shape, input_output_aliases=aliases, compiler_params=_cparams(("parallel", "parallel")),
        )(a, b, *[r for r, _ in epi_rows], *epi_pars, *extra)
    if kind == "nn":
        grid = (n_rows // tm, S5_NJ)
        in_specs = [pl.BlockSpec((tm, BAND), lambda i, j: (i, a_blk0 + j)), pl.BlockSpec((BAND, blk), lambda i, j: (j, 0))]
        out_spec = pl.BlockSpec((tm, blk), lambda i, j: (i, j))
        out_shape = (n_rows, S5_NJ * blk)
        sem = ("parallel", "parallel")

        def body(a_ref, b_ref, o_ref):
            o_ref[...] = _dg(a_ref[...], b_ref[...], 1, 0).astype(o_ref.dtype)
    elif kind == "nt":
        grid = (n_rows // tm, S5_NJ)
        in_specs = [pl.BlockSpec((tm, blk), lambda i, j: (i, j)), pl.BlockSpec((BAND, blk), lambda i, j: (j, 0))]
        out_spec = pl.BlockSpec((tm, BAND), lambda i, j: (i, j))
        out_shape = (n_rows, S5_NJ * BAND)
        sem = ("parallel", "parallel")

        def body(a_ref, b_ref, o_ref):
            o_ref[...] = _dg(a_ref[...], b_ref[...], 1, 1).astype(o_ref.dtype)
    else:
        grid = (S5_NJ, n_rows // tm)
        in_specs = [pl.BlockSpec((tm, BAND), lambda j, k: (k, a_blk0 + j)), pl.BlockSpec((tm, blk), lambda j, k: (k, j))]
        out_spec = pl.BlockSpec((BAND, blk), lambda j, k: (j, 0))
        out_shape = (S5_NJ * BAND, blk)
        sem = ("parallel", "arbitrary")

        def body(a_ref, b_ref, o_ref):
            @pl.when(pl.program_id(1) == 0)
            def _():
                o_ref[...] = jnp.zeros_like(o_ref)

            o_ref[...] += _dg(a_ref[...], b_ref[...], 0, 0)

    return pl.pallas_call(
        body, name=name, grid=grid, in_specs=in_specs, out_specs=out_spec,
        out_shape=jax.ShapeDtypeStruct(out_shape, out_dtype), compiler_params=_cparams(sem),
    )(a, b)


def _rowwise(name, fn, n_rows, tr, row_ins, par_ins, row_outs, acc_outs=(), into=None):
    nr, npar, no, na = len(row_ins), len(par_ins), len(row_outs), len(acc_outs)
    in_specs = []
    for arr, off, w in row_ins:
        assert off % w == 0 and arr.shape[0] == n_rows, (name, arr.shape, off, w)
        in_specs.append(pl.BlockSpec((tr, w), functools.partial(lambda i, c: (i, c), c=off // w)))
    for arr, off, w in par_ins:
        assert off % w == 0
        in_specs.append(pl.BlockSpec((arr.shape[0], w), functools.partial(lambda i, c: (0, c), c=off // w)))
    out_specs = [pl.BlockSpec((tr, w), lambda i: (i, 0)) for w, _ in row_outs]
    out_specs += [pl.BlockSpec((r, w), lambda i: (0, 0)) for r, w in acc_outs]
    out_shape = [jax.ShapeDtypeStruct((n_rows, w), dt) for w, dt in row_outs]
    out_shape += [jax.ShapeDtypeStruct((r, w), F32) for r, w in acc_outs]
    extra, aliases = [], {}
    if into is not None:
        buf, off = into
        w0 = row_outs[0][0]
        assert off % w0 == 0 and buf.dtype == row_outs[0][1]
        in_specs.append(pl.BlockSpec(memory_space=pl.ANY))
        out_specs[0] = pl.BlockSpec((tr, w0), functools.partial(lambda i, c: (i, c), c=off // w0))
        out_shape[0] = jax.ShapeDtypeStruct(buf.shape, buf.dtype)
        extra, aliases = [buf], {nr + npar: 0}
    nx = len(extra)

    def body(*refs):
        rows = [r[...] for r in refs[:nr]]
        pars = [r[...] for r in refs[nr:nr + npar]]
        o_refs = refs[nr + npar + nx:nr + npar + nx + no]
        a_refs = refs[nr + npar + nx + no:]
        outs, accs = fn(rows, pars)
        for o_ref, o in zip(o_refs, outs, strict=True):
            o_ref[...] = o.astype(o_ref.dtype)
        if na:
            @pl.when(pl.program_id(0) == 0)
            def _():
                for a_ref in a_refs:
                    a_ref[...] = jnp.zeros_like(a_ref)

            for a_ref, a in zip(a_refs, accs, strict=True):
                a_ref[...] += jnp.broadcast_to(a, a_ref.shape)

    res = pl.pallas_call(
        body, name=name, grid=(n_rows // tr,),
        in_specs=in_specs, out_specs=out_specs, out_shape=out_shape, input_output_aliases=aliases,
        compiler_params=_cparams(("arbitrary",) if na else ("parallel",)),
    )(*[a for a, _, _ in row_ins], *[a for a, _, _ in par_ins], *extra)
    return res


def _rms(x, w):
    return x * lax.rsqrt(jnp.mean(x * x, axis=-1, keepdims=True) + RMS_EPS) * w


def _gated_norm(y, z, w):
    outs = []
    for g in range(SSD_GROUPS):
        sl = slice(g * GROUP_W, (g + 1) * GROUP_W)
        yg = y[:, sl] * jax.nn.silu(z[:, sl])
        outs.append(yg * lax.rsqrt(jnp.mean(yg * yg, axis=-1, keepdims=True) + RMS_EPS) * w[:, sl])
    return jnp.concatenate(outs, axis=-1)


def _s5_out(yc, u, z, t, d, bg):
    ge = jax.nn.gelu(yc + d * u)
    return ge * jax.nn.sigmoid(t + bg) * jax.nn.silu(z)


def _merge(g5, gs, m5, ms):
    return jax.nn.sigmoid(g5) * m5 + jax.nn.sigmoid(gs) * ms


def _head_loss(h1, pgl, pp, fw, tgt):
    h2 = h1 + jax.nn.sigmoid(pgl) * pp
    err = _rms(h2, fw) - tgt
    per_row = 0.5 * jnp.mean(err * err, axis=-1, keepdims=True)
    return jnp.sum(per_row, axis=0, keepdims=True)


def _s5_disc(a_re, a_im, log_step, b_re2, b_im2, expand):
    step = jnp.exp(log_step)
    mag = jnp.exp(a_re * step)
    lb_re = mag * jnp.cos(a_im * step)
    lb_im = mag * jnp.sin(a_im * step)
    den = a_re * a_re + a_im * a_im
    n_re = lb_re - 1.0
    f_re = (n_re * a_re + lb_im * a_im) / den
    f_im = (lb_im * a_re - n_re * a_im) / den
    hi = lax.Precision.HIGHEST
    fr = jnp.dot(expand, f_re, precision=hi, preferred_element_type=F32)
    fi = jnp.dot(expand, f_im, precision=hi, preferred_element_type=F32)
    return lb_re, lb_im, fr * b_re2 - fi * b_im2, fr * b_im2 + fi * b_re2


def _s5_params_fwd(a_re, a_im, log_step, b_re2, b_im2, expand):
    gp = a_re.shape

    def body(ar, ai, ls, br, bi, ex, pr_ref, pi_ref, bbr_ref, bbi_ref):
        lr, li, bbr, bbi = _s5_disc(ar[...], ai[...], ls[...], br[...], bi[...], ex[...])
        bbr_ref[...] = bbr
        bbi_ref[...] = bbi
        qr, qi = lr, li
        for k in range(S5_LOG_TB):
            pr_ref[k] = qr
            pi_ref[k] = qi
            qr, qi = qr * lr - qi * li, qr * li + qi * lr

    return pl.pallas_call(
        body, name="s5_params_fwd",
        out_shape=(jax.ShapeDtypeStruct((S5_LOG_TB,) + gp, F32), jax.ShapeDtypeStruct((S5_LOG_TB,) + gp, F32),
                   jax.ShapeDtypeStruct(b_re2.shape, F32), jax.ShapeDtypeStruct(b_re2.shape, F32)),
    )(a_re, a_im, log_step, b_re2, b_im2, expand)


def _s5_params_bwd(a_re, a_im, log_step, b_re2, b_im2, expand, d_lr, d_li, d_bbr, d_bbi):
    def body(ar, ai, ls, br, bi, ex, glr, gli, gbr, gbi, dar, dai, dls, dbr, dbi):
        _, vjp = jax.vjp(lambda *p: _s5_disc(*p, ex[...]), ar[...], ai[...], ls[...], br[...], bi[...])
        g = vjp((glr[...], gli[...], gbr[...], gbi[...]))
        for ref, val in zip((dar, dai, dls, dbr, dbi), g, strict=True):
            ref[...] = val

    return pl.pallas_call(
        body, name="s5_params_bwd",
        out_shape=tuple(jax.ShapeDtypeStruct(v.shape, F32) for v in (a_re, a_im, log_step, b_re2, b_im2)),
    )(a_re, a_im, log_step, b_re2, b_im2, expand, d_lr, d_li, d_bbr, d_bbi)


def _scan_block(x_ref, out_ref, lp, edge_pow, cr, ci, reverse, each=None):
    n_g = x_ref.shape[0] // 8
    sub = lax.broadcasted_iota(jnp.int32, (8, S5_LB), 0)
    steps = []
    for sh in (1, 2, 4):
        keep = (sub < 8 - sh) if reverse else (sub >= sh)
        steps.append((8 - sh if reverse else sh, jnp.where(keep, lp[sh - 1:sh, :S5_LB], 0.0),
                      jnp.where(keep, lp[sh - 1:sh, S5_LB:], 0.0)))
    e_r, e_i = edge_pow[:, :S5_LB], edge_pow[:, S5_LB:]
    for r in (range(n_g - 1, -1, -1) if reverse else range(n_g)):
        rows = slice(8 * r, 8 * r + 8)
        xr, xi = x_ref[rows, :S5_LB], x_ref[rows, S5_LB:]
        for by, a_r, a_i in steps:
            pr, pi = pltpu.roll(xr, by, 0), pltpu.roll(xi, by, 0)
            xr, xi = xr + a_r * pr - a_i * pi, xi + a_r * pi + a_i * pr
        xr = xr + e_r * cr - e_i * ci
        xi = xi + e_r * ci + e_i * cr
        out_ref[rows, :S5_LB] = xr
        out_ref[rows, S5_LB:] = xi
        if each is not None:
            each(r, xr, xi)
        cr, ci = (xr[0:1, :], xi[0:1, :]) if reverse else (xr[7:8, :], xi[7:8, :])
    return cr, ci


def _s5_scan_fwd(bu, lam_pow, n_seq, seq_len):
    n_t = seq_len // S5_TB
    blk = 2 * S5_LB

    def body(bu_ref, lp_ref, s_ref, cr, ci, buf):
        @pl.when(pl.program_id(2) == 0)
        def _():
            cr[...] = jnp.zeros_like(cr)
            ci[...] = jnp.zeros_like(ci)

        lp = lp_ref[...]
        cr[...], ci[...] = _scan_block(bu_ref, buf, lp, lp, cr[...], ci[...], False)
        s_ref[...] = buf[...].astype(s_ref.dtype)

    return pl.pallas_call(
        body, name="s5_scan_fwd", grid=(S5_NJ, n_seq, n_t),
        in_specs=[pl.BlockSpec((S5_TB, blk), lambda j, b, t: (b * n_t + t, j)),
                  pl.BlockSpec((S5_LOG_TB, blk), lambda j, b, t: (0, j))],
        out_specs=pl.BlockSpec((S5_TB, blk), lambda j, b, t: (b * n_t + t, j)),
        out_shape=jax.ShapeDtypeStruct(bu.shape, BF16),
        scratch_shapes=[pltpu.VMEM((1, S5_LB), F32), pltpu.VMEM((1, S5_LB), F32), pltpu.VMEM((S5_TB, blk), F32)],
        compiler_params=_cparams(("parallel", "parallel", "arbitrary")),
    )(bu, lam_pow)


def _s5_scan_bwd(ds, s, lam_pow_conj, n_seq, seq_len):
    n_t = seq_len // S5_TB
    blk = 2 * S5_LB
    halo_rows = 16
    halo_per_blk = S5_TB // halo_rows

    def rows(j, b, t):
        return (b * n_t + (n_t - 1 - t), j)

    def halo(j, b, t):
        return (jnp.maximum((b * n_t + (n_t - 1 - t)) * halo_per_blk - 1, 0), j)

    def body(ds_ref, sq_ref, hq_ref, lp_ref, g_ref, dl_ref, cr, ci, buf, s_ref):
        b, t = pl.program_id(1), pl.program_id(2)
        s_ref[...] = sq_ref[...].astype(F32)
        h_last = hq_ref[...].astype(F32)[halo_rows - 1:halo_rows, :]

        @pl.when(t == 0)
        def _():
            cr[...] = jnp.zeros_like(cr)
            ci[...] = jnp.zeros_like(ci)

        @pl.when((b == 0) & (t == 0))
        def _():
            dl_ref[...] = jnp.zeros_like(dl_ref)

        first_blk = t == n_t - 1
        sub = lax.broadcasted_iota(jnp.int32, (8, S5_LB), 0)
        acc = [jnp.zeros((8, S5_LB), F32), jnp.zeros((8, S5_LB), F32)]

        def each(r, gr, gi):
            rows = slice(8 * r, 8 * r + 8)
            if r == 0:
                before_r = jnp.where(first_blk, 0.0, h_last[:, :S5_LB])
                before_i = jnp.where(first_blk, 0.0, h_last[:, S5_LB:])
            else:
                before_r, before_i = s_ref[8 * r - 1:8 * r, :S5_LB], s_ref[8 * r - 1:8 * r, S5_LB:]
            sp_r = jnp.where(sub == 0, before_r, pltpu.roll(s_ref[rows, :S5_LB], 1, 0))
            sp_i = jnp.where(sub == 0, before_i, pltpu.roll(s_ref[rows, S5_LB:], 1, 0))
            acc[0] = acc[0] + gr * sp_r + gi * sp_i
            acc[1] = acc[1] + gi * sp_r - gr * sp_i

        lp = lp_ref[...]
        edge_pow = jnp.concatenate([lp[7 - i:8 - i, :] for i in range(8)], axis=0)
        cr[...], ci[...] = _scan_block(ds_ref, buf, lp, edge_pow, cr[...], ci[...], True, each)
        g_ref[...] = buf[...].astype(g_ref.dtype)
        dl_ref[:, :S5_LB] += jnp.sum(acc[0], axis=0, keepdims=True)
        dl_ref[:, S5_LB:] += jnp.sum(acc[1], axis=0, keepdims=True)

    return pl.pallas_call(
        body, name="s5_scan_bwd", grid=(S5_NJ, n_seq, n_t),
        in_specs=[pl.BlockSpec((S5_TB, blk), rows), pl.BlockSpec((S5_TB, blk), rows),
                  pl.BlockSpec((halo_rows, blk), halo), pl.BlockSpec((S5_LOG_TB, blk), lambda j, b, t: (0, j))],
        out_specs=[pl.BlockSpec((S5_TB, blk), rows), pl.BlockSpec((1, blk), lambda j, b, t: (0, j))],
        out_shape=[jax.ShapeDtypeStruct(ds.shape, BF16), jax.ShapeDtypeStruct((1, ds.shape[1]), F32)],
        scratch_shapes=[pltpu.VMEM((1, S5_LB), F32), pltpu.VMEM((1, S5_LB), F32), pltpu.VMEM((S5_TB, blk), F32),
                        pltpu.VMEM((S5_TB, blk), F32)],
        compiler_params=_cparams(("parallel", "arbitrary", "arbitrary")),
    )(ds, s, s, lam_pow_conj)


CONV_TR = 512
CONV_CW = 512


def _shift_down(x, halo, k):
    if k == 0:
        return x
    row8 = lax.broadcasted_iota(jnp.int32, halo.shape, 0)
    rolled = pltpu.roll(x, k, 0)
    top = jnp.where(row8 < k, pltpu.roll(halo, k, 0), rolled[:8])
    if x.shape[0] == 8:
        return top
    return jnp.concatenate([top, rolled[8:]], axis=0)


def _shift_up(x, halo, k):
    if k == 0:
        return x
    n = x.shape[0]
    row8 = lax.broadcasted_iota(jnp.int32, halo.shape, 0)
    rolled = pltpu.roll(x, n - k, 0)
    bot = jnp.where(row8 >= 8 - k, pltpu.roll(halo, 8 - k, 0), rolled[n - 8:])
    if n == 8:
        return bot
    return jnp.concatenate([rolled[:n - 8], bot], axis=0)


def _conv_pre(x, halo, w, b):
    acc = b + w[SSD_CONV - 1:SSD_CONV, :] * x
    for k in range(SSD_CONV - 1):
        acc = acc + w[k:k + 1, :] * _shift_down(x, halo, SSD_CONV - 1 - k)
    return acc


def _conv_specs(seq_len, col_off):
    lt = seq_len // CONV_TR
    cb = col_off // CONV_CW
    cur = pl.BlockSpec((CONV_TR, CONV_CW), lambda j, i: (i, j + cb))
    prev = pl.BlockSpec((8, CONV_CW), lambda j, i: (jnp.maximum(i * (CONV_TR // 8) - 1, 0), j + cb))
    return lt, cur, prev


def _conv_fwd(proj, conv_w, conv_b, n_rows, seq_len):
    lt, cur, prev = _conv_specs(seq_len, OFF_XBC)

    def body(x_ref, h_ref, w_ref, b_ref, o_ref):
        halo = jnp.where(pl.program_id(1) % lt == 0, 0.0, h_ref[...])
        o_ref[...] = jax.nn.silu(_conv_pre(x_ref[...], halo, w_ref[...], b_ref[...]))

    return pl.pallas_call(
        body, name="ssd_conv_fwd", grid=(SSD_CONV_DIM // CONV_CW, n_rows // CONV_TR),
        in_specs=[cur, prev, pl.BlockSpec((SSD_CONV, CONV_CW), lambda j, i: (0, j)),
                  pl.BlockSpec((1, CONV_CW), lambda j, i: (0, j))],
        out_specs=pl.BlockSpec((CONV_TR, CONV_CW), lambda j, i: (i, j)),
        out_shape=jax.ShapeDtypeStruct((n_rows, SSD_CONV_DIM), F32),
        compiler_params=_cparams(("parallel", "parallel")),
    )(proj, proj, conv_w, conv_b)


def _conv_bwd(name, proj, d_act, conv_w, conv_b, n_rows, seq_len, col_off, dproj):
    width = d_act.shape[1]
    lt, cur, prev = _conv_specs(seq_len, OFF_XBC + col_off)
    n_blk = n_rows // CONV_TR
    cb = (OFF_XBC + col_off) // CONV_CW
    pb = col_off // CONV_CW
    nxt = pl.BlockSpec((8, CONV_CW), lambda j, i: (jnp.minimum((i + 1) * (CONV_TR // 8), n_rows // 8 - 1), j + cb))
    d_cur = pl.BlockSpec((CONV_TR, CONV_CW), lambda j, i: (i, j))
    d_nxt = pl.BlockSpec((8, CONV_CW), lambda j, i: (jnp.minimum((i + 1) * (CONV_TR // 8), n_rows // 8 - 1), j))

    def dsilu(pre):
        sg = jax.nn.sigmoid(pre)
        return sg * (1.0 + pre * (1.0 - sg))

    def body(x_ref, hp_ref, hn_ref, d_ref, dn_ref, w_ref, b_ref, _, dx_ref, dw_ref, db_ref):
        i = pl.program_id(1)
        x, w, b = x_ref[...], w_ref[...], b_ref[...]
        halo_p = jnp.where(i % lt == 0, 0.0, hp_ref[...])
        at_end = i % lt == lt - 1
        dpre = d_ref[...] * dsilu(_conv_pre(x, halo_p, w, b))
        pre_n = _conv_pre(hn_ref[...], x[CONV_TR - 8:, :], w, b)
        dpre_n = jnp.where(at_end, 0.0, dn_ref[...] * dsilu(pre_n))
        dx = w[SSD_CONV - 1:SSD_CONV, :] * dpre
        for k in range(SSD_CONV - 1):
            dx = dx + w[k:k + 1, :] * _shift_up(dpre, dpre_n, SSD_CONV - 1 - k)
        dx_ref[...] = dx.astype(dx_ref.dtype)

        @pl.when(i == 0)
        def _():
            dw_ref[...] = jnp.zeros_like(dw_ref)
            db_ref[...] = jnp.zeros_like(db_ref)

        for k in range(SSD_CONV):
            xs = _shift_down(x, halo_p, SSD_CONV - 1 - k)
            dw_ref[k:k + 1, :] += jnp.sum(dpre * xs, axis=0, keepdims=True)
        db_ref[...] += jnp.sum(dpre, axis=0, keepdims=True)

    return pl.pallas_call(
        body, name=name, grid=(width // CONV_CW, n_blk),
        in_specs=[cur, prev, nxt, d_cur, d_nxt,
                  pl.BlockSpec((SSD_CONV, CONV_CW), lambda j, i: (0, j + pb)),
                  pl.BlockSpec((1, CONV_CW), lambda j, i: (0, j + pb)), pl.BlockSpec(memory_space=pl.ANY)],
        out_specs=[pl.BlockSpec((CONV_TR, CONV_CW), lambda j, i: (i, j + cb)),
                   pl.BlockSpec((SSD_CONV, CONV_CW), lambda j, i: (0, j)),
                   pl.BlockSpec((1, CONV_CW), lambda j, i: (0, j))],
        out_shape=[jax.ShapeDtypeStruct(dproj.shape, dproj.dtype),
                   jax.ShapeDtypeStruct((SSD_CONV, width), F32), jax.ShapeDtypeStruct((1, width), F32)],
        input_output_aliases={7: 0},
        compiler_params=_cparams(("parallel", "arbitrary")),
    )(proj, proj, proj, d_act, d_act, conv_w, conv_b, dproj)


def _split3(x):
    hi = x.astype(BF16)
    r = x - hi.astype(F32)
    mid = r.astype(BF16)
    return hi, mid, (r - mid.astype(F32)).astype(BF16)


def _sel_dot(a, b, a_is_sel):
    dn = (((1,), (0,)), ((), ()))
    if a_is_sel:
        return sum(lax.dot_general(a, t, dn, preferred_element_type=F32) for t in _split3(b))
    return sum(lax.dot_general(t, b, dn, preferred_element_type=F32) for t in _split3(a))


@jax.custom_vjp
def sel_left(sel, sel_t, x):
    return _sel_dot(sel, x, True)


@jax.custom_vjp
def sel_right(x, sel, sel_t):
    return _sel_dot(x, sel, False)


sel_left.defvjp(lambda s, st, x: (_sel_dot(s, x, True), (s, st)),
                lambda r, g: (jnp.zeros_like(r[0]), jnp.zeros_like(r[1]), _sel_dot(r[1], g, True)))
sel_right.defvjp(lambda x, s, st: (_sel_dot(x, s, False), (s, st)),
                 lambda r, g: (_sel_dot(g, r[1], False), jnp.zeros_like(r[0]), jnp.zeros_like(r[1])))


def _ssd_chunk(xs, bm, cm, dtr, st, dtb, alog, dsk, k):
    dt = jax.nn.softplus(dtr + dtb)
    acum = sel_left(k["tri"], k["tri_t"], dt * (-jnp.exp(alog)))
    dt_e = sel_right(dt, k["spread"], k["spread_t"])
    ac_e = sel_right(acum, k["spread"], k["spread_t"])
    al_e = ac_e[SSD_CHUNK - 1:SSD_CHUNK, :]
    dsk_e = sel_right(jnp.broadcast_to(dsk, (8, 128)), k["spread"], k["spread_t"])[0:1, :]
    xdt = xs * dt_e
    acum_t = acum.T
    scores = dot_nt(cm, bm)
    y = dot_nn(cm, st) * jnp.exp(ac_e) + xs * dsk_e
    for j in range(SSD_HPG):
        lmat = jnp.exp(jnp.where(k["causal"], acum[:, j:j + 1] - acum_t[j:j + 1, :], -jnp.inf))
        y = y + dot_nn(scores * lmat, jnp.where(k["head"] == j, xdt, 0.0))
    new = st * jnp.exp(al_e) + dot_tn(bm, xdt * jnp.exp(al_e - ac_e))
    return y, new


def _ssd_consts():
    r = lax.broadcasted_iota(jnp.int32, (SSD_CHUNK, SSD_CHUNK), 0)
    c = lax.broadcasted_iota(jnp.int32, (SSD_CHUNK, SSD_CHUNK), 1)
    hd = jnp.int32(SSD_HEADDIM)
    sr = lax.broadcasted_iota(jnp.int32, (128, GROUP_W), 0)
    sc = lax.div(lax.broadcasted_iota(jnp.int32, (128, GROUP_W), 1), hd)
    tr = lax.div(lax.broadcasted_iota(jnp.int32, (GROUP_W, 128), 0), hd)
    tc = lax.broadcasted_iota(jnp.int32, (GROUP_W, 128), 1)
    return {"tri": (r >= c).astype(BF16), "tri_t": (c >= r).astype(BF16), "causal": r >= c,
            "spread": (sr == sc).astype(BF16), "spread_t": (tr == tc).astype(BF16),
            "head": lax.div(lax.broadcasted_iota(jnp.int32, (SSD_CHUNK, GROUP_W), 1), hd)}


def _ssd_specs(n_c, reverse):
    def cidx(c):
        return n_c - 1 - c if reverse else c

    xs = pl.BlockSpec((SSD_CHUNK, GROUP_W), lambda g, b, c: (b * n_c + cidx(c), g))
    bm = pl.BlockSpec((SSD_CHUNK, SSD_STATE), lambda g, b, c: (b * n_c + cidx(c), SSD_WIDTH // SSD_STATE + g))
    cm = pl.BlockSpec((SSD_CHUNK, SSD_STATE), lambda g, b, c: (b * n_c + cidx(c), (SSD_WIDTH + SSD_BC) // SSD_STATE + g))
    dt = pl.BlockSpec((SSD_CHUNK, 128), lambda g, b, c: (b * n_c + cidx(c), OFF_DT // 128 + g))
    par = pl.BlockSpec((1, 128), lambda g, b, c: (0, g))
    st = pl.BlockSpec((1, 1, 1, SSD_STATE, GROUP_W), lambda g, b, c: (b, cidx(c), g, 0, 0))
    return xs, bm, cm, dt, par, st


def _ssd_fwd(xbc_act, proj, dtb, alog, dsk, n_seq, seq_len):
    n_c = seq_len // SSD_CHUNK
    xs_s, bm_s, cm_s, dt_s, par_s, st_s = _ssd_specs(n_c, False)

    def body(xs_ref, bm_ref, cm_ref, dt_ref, dtb_ref, al_ref, dk_ref, y_ref, st_ref, state):
        @pl.when(pl.program_id(2) == 0)
        def _():
            state[...] = jnp.zeros_like(state)

        prev = state[...]
        st_ref[0, 0, 0] = prev
        y, new = _ssd_chunk(xs_ref[...], bm_ref[...], cm_ref[...], dt_ref[...], prev,
                            dtb_ref[...], al_ref[...], dk_ref[...], _ssd_consts())
        y_ref[...] = y
        state[...] = new

    return pl.pallas_call(
        body, name="ssd_fwd", grid=(SSD_GROUPS, n_seq, n_c),
        in_specs=[xs_s, bm_s, cm_s, dt_s, par_s, par_s, par_s],
        out_specs=[pl.BlockSpec((SSD_CHUNK, GROUP_W), lambda g, b, c: (b * n_c + c, g)), st_s],
        out_shape=[jax.ShapeDtypeStruct((n_seq * seq_len, SSD_WIDTH), F32),
                   jax.ShapeDtypeStruct((n_seq, n_c, SSD_GROUPS, SSD_STATE, GROUP_W), F32)],
        scratch_shapes=[pltpu.VMEM((SSD_STATE, GROUP_W), F32)],
        compiler_params=_cparams(("parallel", "parallel", "arbitrary")),
    )(xbc_act, xbc_act, xbc_act, proj, dtb, alog, dsk)


def _ssd_bwd(xbc_act, proj, states, dy, dtb, alog, dsk, n_seq, seq_len, dproj):
    n_c = seq_len // SSD_CHUNK
    n_rows = n_seq * seq_len
    xs_s, bm_s, cm_s, dt_s, par_s, st_s = _ssd_specs(n_c, True)

    def rows(w, first=0):
        return pl.BlockSpec((SSD_CHUNK, w), lambda g, b, c: (b * n_c + (n_c - 1 - c), first + g))

    def body(xs_ref, bm_ref, cm_ref, dt_ref, st_ref, dy_ref, dtb_ref, al_ref, dk_ref, _,
             dxs_ref, dbm_ref, dcm_ref, ddt_ref, ddtb_ref, dal_ref, ddk_ref, dstate):
        b, c = pl.program_id(1), pl.program_id(2)

        @pl.when(c == 0)
        def _():
            dstate[...] = jnp.zeros_like(dstate)

        @pl.when((b == 0) & (c == 0))
        def _():
            ddtb_ref[...] = jnp.zeros_like(ddtb_ref)
            dal_ref[...] = jnp.zeros_like(dal_ref)
            ddk_ref[...] = jnp.zeros_like(ddk_ref)

        consts = _ssd_consts()
        _, vjp = jax.vjp(
            lambda xs, bm, cm, dtr, prev, dtb, alog, dsk: _ssd_chunk(xs, bm, cm, dtr, prev, dtb, alog, dsk, consts),
            xs_ref[...], bm_ref[...], cm_ref[...], dt_ref[...], st_ref[0, 0, 0], dtb_ref[...], al_ref[...], dk_ref[...])
        dxs, dbm, dcm, ddtr, dprev, ddtb, dal, ddk = vjp((dy_ref[...], dstate[...]))
        dxs_ref[...] = dxs
        dbm_ref[...] = dbm
        dcm_ref[...] = dcm
        ddt_ref[...] = ddtr.astype(ddt_ref.dtype)
        ddtb_ref[...] += ddtb
        dal_ref[...] += dal
        ddk_ref[...] += ddk
        dstate[...] = dprev

    acc = pl.BlockSpec((1, 128), lambda g, b, c: (0, g))
    return pl.pallas_call(
        body, name="ssd_bwd", grid=(SSD_GROUPS, n_seq, n_c),
        in_specs=[xs_s, bm_s, cm_s, dt_s, st_s, rows(GROUP_W), par_s, par_s, par_s, pl.BlockSpec(memory_space=pl.ANY)],
        out_specs=[rows(GROUP_W), rows(SSD_STATE), rows(SSD_STATE), rows(128, OFF_DT // 128), acc, acc, acc],
        out_shape=[jax.ShapeDtypeStruct((n_rows, SSD_WIDTH), F32), jax.ShapeDtypeStruct((n_rows, SSD_BC), F32),
                   jax.ShapeDtypeStruct((n_rows, SSD_BC), F32), jax.ShapeDtypeStruct(dproj.shape, dproj.dtype),
                   jax.ShapeDtypeStruct((1, 512), F32), jax.ShapeDtypeStruct((1, 512), F32),
                   jax.ShapeDtypeStruct((1, 512), F32)],
        input_output_aliases={9: 3},
        scratch_shapes=[pltpu.VMEM((SSD_STATE, GROUP_W), F32)],
        compiler_params=_cparams(("parallel", "arbitrary", "arbitrary")),
    )(xbc_act, xbc_act, xbc_act, proj, states, dy, dtb, alog, dsk, dproj)


def _pad_heads(v):
    return jnp.pad(v.reshape(SSD_GROUPS, SSD_HPG), ((0, 0), (0, 128 - SSD_HPG))).reshape(1, SSD_GROUPS * 128)


def _unpad_heads(v):
    return v.reshape(SSD_GROUPS, 128)[:, :SSD_HPG].reshape(1, SSD_HEADS)


def _state_cols(v):
    re, im = v
    lead = re.shape[:-1]
    re = re.reshape(lead + (S5_NJ, 1, S5_LB))
    im = im.reshape(lead + (S5_NJ, 1, S5_LB))
    return jnp.concatenate([re, im], axis=-2).reshape(lead + (2 * S5_N,))


def _state_uncols(v):
    lead = v.shape[:-1]
    v = v.reshape(lead + (S5_NJ, 2, S5_LB))
    return v[..., 0, :].reshape(lead + (S5_N,)), v[..., 1, :].reshape(lead + (S5_N,))


GROUPS_PER_BAND = BAND // S5_GROUP


def _band(w2_re, w2_im):
    gh = S5_GROUPS * S5_GROUP
    rg = ((jnp.arange(gh) // S5_GROUP) % GROUPS_PER_BAND)[:, None, None]
    cg = jnp.arange(GROUPS_PER_BAND)[None, :, None]
    parts = [jnp.where(rg == cg, v[:, None, :], 0.0).reshape(gh, S5_LB) for v in (w2_re, w2_im)]
    return jnp.concatenate(parts, axis=1)


def _band_take(wb):
    gh = S5_GROUPS * S5_GROUP
    w4 = wb.reshape(gh, 2, GROUPS_PER_BAND, S5_STATE)
    sel = w4[jnp.arange(gh), :, (jnp.arange(gh) // S5_GROUP) % GROUPS_PER_BAND, :]
    return sel[:, 0, :], sel[:, 1, :]


W_IN_SHARD = IN_PROJ_DIM // N_CHIPS
W_IN_SEGS = ((0, 512, OFF_U5), (512, 1024, OFF_Z5), (1024, 2560, OFF_ZS), (2560, 5120, OFF_XBC), (5144, 7192, OFF_G5))
DT_ROWS = (5120, 5144)


def _w_in_pieces():
    runs = []
    segs = list(W_IN_SEGS) + [(DT_ROWS[0] + SSD_HPG * g, DT_ROWS[0] + SSD_HPG * (g + 1), OFF_DT + 128 * g)
                              for g in range(SSD_GROUPS)]
    for lo, hi, off in segs:
        for j in range(N_CHIPS):
            s, e = max(lo, j * W_IN_SHARD), min(hi, (j + 1) * W_IN_SHARD)
            if s < e:
                runs.append((j, s - j * W_IN_SHARD, off + s - lo, e - s))
    return runs


RELAYOUT_LANES = 256


def _pad_w_in_t(a4):
    runs = _w_in_pieces()

    def body(a_ref, o_ref):
        o_ref[pl.ds(OFF_DT, DT_W), :] = jnp.zeros((DT_W, RELAYOUT_LANES), o_ref.dtype)
        for j, src, dst, n in runs:
            o_ref[pl.ds(dst, n), :] = a_ref[j, pl.ds(src, n), :]

    return pl.pallas_call(
        body, name="w_in_to_padded", grid=(D_MODEL // RELAYOUT_LANES,),
        in_specs=[pl.BlockSpec((N_CHIPS, W_IN_SHARD, RELAYOUT_LANES), lambda i: (0, 0, i))],
        out_specs=pl.BlockSpec((PROJ_W, RELAYOUT_LANES), lambda i: (0, i)),
        out_shape=jax.ShapeDtypeStruct((PROJ_W, D_MODEL), a4.dtype),
        compiler_params=_cparams(("parallel",)),
    )(a4)


def _unpad_w_in_t(wp):
    runs = _w_in_pieces()

    def body(p_ref, o_ref):
        for j, dst, src, n in runs:
            o_ref[j, pl.ds(dst, n), :] = p_ref[pl.ds(src, n), :]

    return pl.pallas_call(
        body, name="w_in_from_padded", grid=(D_MODEL // RELAYOUT_LANES,),
        in_specs=[pl.BlockSpec((PROJ_W, RELAYOUT_LANES), lambda i: (0, i))],
        out_specs=pl.BlockSpec((N_CHIPS, W_IN_SHARD, RELAYOUT_LANES), lambda i: (0, 0, i)),
        out_shape=jax.ShapeDtypeStruct((N_CHIPS, W_IN_SHARD, D_MODEL), wp.dtype),
        compiler_params=_cparams(("parallel",)),
    )(wp)


def _local_step(x, p, tgt, w):
    n_seq, seq_len, _ = x.shape
    n_rows = n_seq * seq_len
    tr = 512
    x2 = x.reshape(n_rows, D_MODEL)
    p2 = p.reshape(n_rows, PLE_DIM)
    t2 = tgt.reshape(n_rows, D_MODEL)
    row = functools.partial(_rowwise, n_rows=n_rows, tr=tr)

    w_pad_t = _pad_w_in_t(w["w_in_t"])
    norm_w = w["norm_w"].reshape(1, D_MODEL)
    ple_norm_w = w["ple_norm_w"].reshape(1, D_MODEL)
    final_w = w["final_norm_w"].reshape(1, D_MODEL)
    s5_d = w["s5_d"].reshape(1, S5_WIDTH)
    b_glu = w["s5_b_glu"].reshape(1, S5_WIDTH)
    conv_w = w["ssd_conv_w"].reshape(SSD_CONV, SSD_CONV_DIM)
    conv_b = w["ssd_conv_b"].reshape(1, SSD_CONV_DIM)
    ssd_norm_w = w["ssd_norm_w"].reshape(1, SSD_WIDTH)
    dtb, alog, dsk = (_pad_heads(w[k].reshape(1, SSD_HEADS)) for k in ("ssd_dt_bias", "ssd_a_log", "ssd_d"))

    gh = S5_GROUPS * S5_GROUP
    a_re = w["s5_a_re"].reshape(S5_GROUPS, S5_STATE)
    a_im = w["s5_a_im"].reshape(S5_GROUPS, S5_STATE)
    log_step = w["s5_log_step"].reshape(S5_GROUPS, 1)
    b_re2 = jnp.transpose(w["s5_b_re"].reshape(S5_GROUPS, S5_STATE, S5_GROUP), (0, 2, 1)).reshape(gh, S5_STATE)
    b_im2 = jnp.transpose(w["s5_b_im"].reshape(S5_GROUPS, S5_STATE, S5_GROUP), (0, 2, 1)).reshape(gh, S5_STATE)
    expand = (jnp.arange(gh)[:, None] // S5_GROUP == jnp.arange(S5_GROUPS)[None, :]).astype(F32)
    pow_re, pow_im, bb_re2, bb_im2 = _s5_params_fwd(a_re, a_im, log_step, b_re2, b_im2, expand)
    lam_pow = _state_cols((pow_re.reshape(S5_LOG_TB, S5_N), pow_im.reshape(S5_LOG_TB, S5_N)))
    lam_pow_conj = _state_cols((pow_re.reshape(S5_LOG_TB, S5_N), -pow_im.reshape(S5_LOG_TB, S5_N)))
    bb_band = _band(bb_re2, bb_im2).astype(BF16)
    c_band = _band(w["s5_c_re"].reshape(gh, S5_STATE), -w["s5_c_im"].reshape(gh, S5_STATE)).astype(BF16)

    (hn,) = row("rms_in", lambda r, q: ([_rms(r[0], q[0])], []), row_ins=[(x2, 0, D_MODEL)],
                par_ins=[(norm_w, 0, D_MODEL)], row_outs=[(D_MODEL, BF16)])
    proj = _matmul("mm_proj", hn, w_pad_t, tb=True)
    bu = _band_matmul("mm_s5_bu", "nn", proj, bb_band, a_blk0=OFF_U5 // BAND)
    s = _s5_scan_fwd(bu, lam_pow, n_seq, seq_len)
    yc, ge = _band_matmul("mm_s5_y", "nt", s, c_band, epilogue=lambda c, r, q: [c, jax.nn.gelu(c + q[0] * r[0])],
                          epi_rows=[(proj, OFF_U5 // BAND)], epi_pars=[s5_d], epi_outs=[F32, BF16])
    tg, y5 = _matmul("mm_s5_glu", ge, w["s5_w_glu"], epilogue=lambda c, r, q: [c, _s5_out(r[0], r[1], r[2], c, *q)],
                     epi_rows=[(yc, 0), (proj, OFF_U5), (proj, OFF_Z5)], epi_pars=[s5_d, b_glu], epi_outs=[F32, BF16])
    s5_rows = [(yc, 0, S5_WIDTH), (proj, OFF_U5, S5_WIDTH), (proj, OFF_Z5, S5_WIDTH), (tg, 0, S5_WIDTH)]
    s5_pars = [(s5_d, 0, S5_WIDTH), (b_glu, 0, S5_WIDTH)]

    xbc_act = _conv_fwd(proj, conv_w, conv_b, n_rows, seq_len)
    y_ssd, states = _ssd_fwd(xbc_act, proj, dtb, alog, dsk, n_seq, seq_len)
    gn_rows = [(y_ssd, 0, SSD_WIDTH), (proj, OFF_ZS, SSD_WIDTH)]
    (yss,) = row("ssd_gate", lambda r, q: ([_gated_norm(r[0], r[1], q[0])], []), row_ins=gn_rows,
                 par_ins=[(ssd_norm_w, 0, SSD_WIDTH)], row_outs=[(SSD_WIDTH, BF16)])

    m5 = _matmul("mm_br_s5", y5, w["w_br_s5"])
    ms, merged = _matmul("mm_br_ssd", yss, w["w_br_ssd"], epilogue=lambda c, r, q: [c, _merge(r[0], r[1], r[2], c)],
                         epi_rows=[(proj, OFF_G5), (proj, OFF_GS), (m5, 0)], epi_outs=[F32, BF16])
    mg_rows = [(proj, OFF_G5, D_MODEL), (proj, OFF_GS, D_MODEL), (m5, 0, D_MODEL), (ms, 0, D_MODEL)]

    def resid_norm(c, r, q):
        h1_ = r[0] + c
        return [h1_, _rms(h1_, q[0])]

    h1, hp = _matmul("mm_out", merged, w["w_out"], epilogue=resid_norm, epi_rows=[(x2, 0)], epi_pars=[ple_norm_w],
                     epi_outs=[F32, BF16], full_rows=True)
    pp = _matmul("mm_ple_proj", p2, w["w_ple_proj"])

    def head_fn(pgl_, r, q):
        h1_, pp_, tgt_ = r
        loss, vjp = jax.vjp(lambda a, b, c, f: _head_loss(a, b, c, f, tgt_), h1_, pgl_, pp_, q[0])
        dh1_, dpgl_, dpp_, dfw_ = vjp(jnp.ones_like(loss))
        return [dh1_, dpgl_, dpp_], [loss, dfw_]

    dh2, dpgl, dpp, loss_acc, d_final_w = _matmul(
        "mm_ple_gate_head", hp, w["w_ple_gate"], epilogue=head_fn, epi_rows=[(h1, 0), (pp, 0), (t2, 0)],
        epi_pars=[final_w], epi_outs=[F32, BF16, BF16], epi_accs=[(1, 128), (1, D_MODEL)], full_rows=True)
    loss = loss_acc[0, 0]

    g = {}
    g["final_norm_w"] = d_final_w
    g["w_ple_gate"] = _matmul("mm_d_w_ple_gate", hp, dpgl, ta=True)
    g["w_ple_proj"] = _matmul("mm_d_w_ple_proj", p2, dpp, ta=True)

    def ple_norm_bwd(dhp_, r, q):
        h1_, dh2_ = r
        _, vjp = jax.vjp(_rms, h1_, q[0])
        dh, dw = vjp(dhp_)
        dh = dh + dh2_
        return [dh, dh], [dw]

    dh1, dh1_b, g["ple_norm_w"] = _matmul(
        "mm_d_hp", dpgl, w["w_ple_gate"], tb=True, epilogue=ple_norm_bwd, epi_rows=[(h1, 0), (dh2, 0)],
        epi_pars=[ple_norm_w], epi_outs=[F32, BF16], epi_accs=[(1, D_MODEL)], full_rows=True)
    g["w_out"] = _matmul("mm_d_w_out", merged, dh1_b, ta=True)

    dproj = lax.empty((n_rows, PROJ_W), BF16)

    def merge_bwd(dmerged, r, q):
        sg5, sgs = jax.nn.sigmoid(r[0]), jax.nn.sigmoid(r[1])
        d_gates = jnp.concatenate([dmerged * r[2] * sg5 * (1.0 - sg5), dmerged * r[3] * sgs * (1.0 - sgs)], axis=1)
        return [d_gates, dmerged * sg5, dmerged * sgs]

    dproj, dm5, dms = _matmul(
        "mm_d_merged", dh1_b, w["w_out"], tb=True, epilogue=merge_bwd,
        epi_rows=[(proj, OFF_G5), (proj, OFF_GS), (m5, 0), (ms, 0)], epi_outs=[BF16, BF16, BF16], full_rows=True,
        epi_into=(dproj, OFF_G5, 2 * D_MODEL))
    g["w_br_s5"] = _matmul("mm_d_w_br_s5", y5, dm5, ta=True)
    g["w_br_ssd"] = _matmul("mm_d_w_br_ssd", yss, dms, ta=True)
    dy5 = _matmul("mm_d_y5", dm5, w["w_br_s5"], tb=True)
    dyss = _matmul("mm_d_yss", dms, w["w_br_ssd"], tb=True)

    def s5_out_bwd_a(r, q):
        yc_, u_, z_, t_, dy_ = r
        d_, bg_ = q
        ge_ = jax.nn.gelu(yc_ + d_ * u_)
        _, vjp = jax.vjp(lambda a, z, t, b: a * jax.nn.sigmoid(t + b) * jax.nn.silu(z), ge_, z_, t_, bg_)
        dge, dz, dt_, dbg = vjp(dy_)
        return [dz, dge, dt_], [dbg]

    dproj, dge_a, dtg, g["s5_b_glu"] = row(
        "s5_out_bwd_a", s5_out_bwd_a, row_ins=s5_rows + [(dy5, 0, S5_WIDTH)], par_ins=s5_pars,
        row_outs=[(S5_WIDTH, BF16), (S5_WIDTH, F32), (S5_WIDTH, BF16)], acc_outs=[(1, S5_WIDTH)],
        into=(dproj, OFF_Z5))
    g["s5_w_glu"] = _matmul("mm_d_w_glu", ge, dtg, ta=True)
    dge_b = _matmul("mm_d_ge", dtg, w["s5_w_glu"], tb=True)

    def s5_out_bwd_b(r, q):
        yc_, u_, da_, db_ = r
        _, vjp = jax.vjp(lambda yc, u, d: jax.nn.gelu(yc + d * u), yc_, u_, q[0])
        dyc_, du_, dd_ = vjp(da_ + db_)
        return [dyc_, du_], [dd_]

    dyc, du5_a, g["s5_d"] = row(
        "s5_out_bwd_b", s5_out_bwd_b,
        row_ins=[(yc, 0, S5_WIDTH), (proj, OFF_U5, S5_WIDTH), (dge_a, 0, S5_WIDTH), (dge_b, 0, S5_WIDTH)],
        par_ins=[(s5_d, 0, S5_WIDTH)], row_outs=[(S5_WIDTH, BF16), (S5_WIDTH, F32)], acc_outs=[(1, S5_WIDTH)])
    d_c_band = _band_matmul("mm_d_c", "tn", dyc, s)
    ds = _band_matmul("mm_d_s", "nn", dyc, c_band)
    dbu, d_lam = _s5_scan_bwd(ds, s, lam_pow_conj, n_seq, seq_len)
    d_bb_band = _band_matmul("mm_d_bb", "tn", proj, dbu, a_blk0=OFF_U5 // BAND)
    (dproj,) = _band_matmul("mm_d_u5", "nt", dbu, bb_band, epilogue=lambda c, r, q: [c + r[0]],
                            epi_rows=[(du5_a, 0)], epi_outs=[BF16], into=(dproj, OFF_U5 // BAND))

    d_lr, d_li = _state_uncols(d_lam)
    d_bbr, d_bbi = _band_take(d_bb_band)
    d_are, d_aim, d_ls, d_br2, d_bi2 = _s5_params_bwd(
        a_re, a_im, log_step, b_re2, b_im2, expand,
        d_lr.reshape(S5_GROUPS, S5_STATE), d_li.reshape(S5_GROUPS, S5_STATE), d_bbr, d_bbi)
    g["s5_a_re"], g["s5_a_im"], g["s5_log_step"] = d_are, d_aim, d_ls
    g["s5_b_re_ghp"], g["s5_b_im_ghp"] = d_br2, d_bi2
    d_cr, d_ci = _band_take(d_c_band)
    g["s5_c_re"], g["s5_c_im"] = d_cr, -d_ci

    def gate_bwd(r, q):
        _, vjp = jax.vjp(_gated_norm, r[0], r[1], q[0])
        dy_, dz_, dw_ = vjp(r[2])
        return [dz_, dy_], [dw_]

    dproj, dy_ssd, g["ssd_norm_w"] = row(
        "ssd_gate_bwd", gate_bwd, row_ins=gn_rows + [(dyss, 0, SSD_WIDTH)], par_ins=[(ssd_norm_w, 0, SSD_WIDTH)],
        row_outs=[(SSD_WIDTH, BF16), (SSD_WIDTH, F32)], acc_outs=[(1, SSD_WIDTH)], into=(dproj, OFF_ZS))
    dxs, dbm, dcm, dproj, d_dtb, d_alog, d_dsk = _ssd_bwd(xbc_act, proj, states, dy_ssd, dtb, alog, dsk, n_seq, seq_len,
                                                       dproj)
    g["ssd_dt_bias"], g["ssd_a_log"], g["ssd_d"] = _unpad_heads(d_dtb), _unpad_heads(d_alog), _unpad_heads(d_dsk)
    conv_dw, conv_db = [], []
    for nm, d_act, off in (("x", dxs, 0), ("b", dbm, SSD_WIDTH), ("c", dcm, SSD_WIDTH + SSD_BC)):
        dproj, dw_, db_ = _conv_bwd("ssd_conv_bwd_" + nm, proj, d_act, conv_w, conv_b, n_rows, seq_len, off, dproj)
        conv_dw.append(dw_)
        conv_db.append(db_)
    g["ssd_conv_w"] = jnp.concatenate(conv_dw, axis=1)
    g["ssd_conv_b"] = jnp.concatenate(conv_db, axis=1)

    g["w_in_t"] = _unpad_w_in_t(_matmul("mm_d_w_in", dproj, hn, ta=True))
    def norm_bwd(dhn_, r, q):
        x_, dh1_ = r
        _, vjp = jax.vjp(_rms, x_, q[0])
        dx_, dw_ = vjp(dhn_)
        return [dx_ + dh1_], [dw_]

    dx, g["norm_w"] = _matmul("mm_d_hn", dproj, w_pad_t, epilogue=norm_bwd, epi_rows=[(x2, 0), (dh1, 0)],
                              epi_pars=[norm_w], epi_outs=[F32], epi_accs=[(1, D_MODEL)], full_rows=True)
    return loss, dx.reshape(x.shape), g


HBM = pl.BlockSpec(memory_space=pltpu.HBM)


def _chip_index(x, y):
    return 2 * x + y


def _half(shape2d, axis, which):
    h = shape2d[axis] // 2
    sl = pl.ds(pl.multiple_of(which * h, 128 if axis else 8), h)
    return (slice(None), sl) if axis else (sl, slice(None))


def _gather_chips(split, axes, whole):
    ns, nw = len(split), len(whole)
    n = ns + nw

    def body(*refs):
        ins, outs = refs[:n], refs[n:2 * n]
        ici_send, ici_recv, d2d_send, d2d_recv, local_sems = refs[2 * n:]
        x, y, c = lax.axis_index("x"), lax.axis_index("y"), lax.axis_index("c")
        me = _chip_index(x, y)
        sibling = (x, y, 1 - c)
        peers = [(1 - x, y), (x, 1 - y), (1 - x, 1 - y)]

        def half(t, which):
            return _half(split[t].shape, axes[t], which)

        copies = []
        for t in range(n):
            loc = pltpu.make_async_copy(ins[t], outs[t].at[me], local_sems.at[t])
            loc.start()
            copies.append(loc)

        def ici(t, k, slot):
            px, py = peers[k]
            if t < ns:
                src, dst = ins[t].at[half(t, c)], outs[t].at[(slot,) + half(t, c)]
            else:
                src, dst = ins[t], outs[t].at[slot]
            return pltpu.make_async_remote_copy(src_ref=src, dst_ref=dst, send_sem=ici_send.at[t, k],
                                                recv_sem=ici_recv.at[t, k], device_id=(px, py, c), device_id_type=MESH)

        def d2d(t, k, which):
            rows = outs[t].at[(_chip_index(*peers[k]),) + half(t, which)]
            return pltpu.make_async_remote_copy(src_ref=rows, dst_ref=rows, send_sem=d2d_send.at[t, k],
                                                recv_sem=d2d_recv.at[t, k], device_id=sibling, device_id_type=MESH)

        sends = []
        for t in range(n):
            for k in range(3):
                cp = ici(t, k, me)
                cp.start()
                sends.append(cp)
        for t in range(n):
            for k in range(3):
                ici(t, k, _chip_index(*peers[k])).wait_recv()
                if t < ns:
                    cp = d2d(t, k, c)
                    cp.start()
                    sends.append(cp)
        for t in range(ns):
            for k in range(3):
                d2d(t, k, 1 - c).wait_recv()
        for cp in sends:
            cp.wait_send()
        for cp in copies:
            cp.wait()

    arrays = list(split) + list(whole)
    return pl.pallas_call(
        body, name="gather_weights",
        in_specs=[HBM] * n, out_specs=[HBM] * n,
        out_shape=[jax.ShapeDtypeStruct((N_CHIPS,) + a.shape, a.dtype) for a in arrays],
        scratch_shapes=[pltpu.SemaphoreType.DMA((n, 3)), pltpu.SemaphoreType.DMA((n, 3)),
                        pltpu.SemaphoreType.DMA((ns, 3)), pltpu.SemaphoreType.DMA((ns, 3)),
                        pltpu.SemaphoreType.DMA((n,))],
    )(*arrays)


def _half_shape(shape2d, axis):
    r, c = shape2d
    return (r, c // 2) if axis else (r // 2, c)


def _swap_halves(slotted, axes, small):
    n = len(slotted)

    def body(*refs):
        ins, sm_in = refs[:n], refs[n]
        outs, sm_out = refs[n + 1:2 * n + 1], refs[2 * n + 1]
        send_sems, recv_sems, local_sem, sm_send, sm_recv = refs[2 * n + 2:]
        x, y, c = lax.axis_index("x"), lax.axis_index("y"), lax.axis_index("c")
        dev = 4 * x + 2 * y + c
        local = pltpu.make_async_copy(sm_in, sm_out.at[dev], local_sem)
        local.start()
        sends = []
        for t in range(n):
            other = (slice(None),) + _half(slotted[t].shape[1:], axes[t], 1 - c)
            cp = pltpu.make_async_remote_copy(
                src_ref=ins[t].at[other], dst_ref=outs[t], send_sem=send_sems.at[t],
                recv_sem=recv_sems.at[t], device_id=(x, y, 1 - c), device_id_type=MESH)
            cp.start()
            sends.append(cp)
        rel = [(fx, fy, fc) for fx in (0, 1) for fy in (0, 1) for fc in (0, 1)][1:]
        for k, (fx, fy, fc) in enumerate(rel):
            cp = pltpu.make_async_remote_copy(
                src_ref=sm_in, dst_ref=sm_out.at[dev], send_sem=sm_send.at[k], recv_sem=sm_recv.at[k],
                device_id=(x ^ fx, y ^ fy, c ^ fc), device_id_type=MESH)
            cp.start()
            sends.append(cp)
        for cp in sends[:n]:
            cp.wait_recv()
        for k, (fx, fy, fc) in enumerate(rel):
            src_dev = 4 * (x ^ fx) + 2 * (y ^ fy) + (c ^ fc)
            pltpu.make_async_remote_copy(
                src_ref=sm_in, dst_ref=sm_out.at[src_dev], send_sem=sm_send.at[k], recv_sem=sm_recv.at[k],
                device_id=(x ^ fx, y ^ fy, c ^ fc), device_id_type=MESH).wait_recv()
        for cp in sends:
            cp.wait_send()
        local.wait()

    return pl.pallas_call(
        body, name="swap_halves",
        in_specs=[HBM] * (n + 1), out_specs=[HBM] * (n + 1),
        out_shape=[jax.ShapeDtypeStruct((a.shape[0],) + _half_shape(a.shape[1:], ax), a.dtype)
                   for a, ax in zip(slotted, axes, strict=True)]
        + [jax.ShapeDtypeStruct((N_DEV,) + small.shape, small.dtype)],
        scratch_shapes=[pltpu.SemaphoreType.DMA((n,)), pltpu.SemaphoreType.DMA((n,)), pltpu.SemaphoreType.DMA,
                        pltpu.SemaphoreType.DMA((7,)), pltpu.SemaphoreType.DMA((7,))],
    )(*slotted, small)


def _scatter_halves(parts):
    n = len(parts)

    def body(*refs):
        ins, outs = refs[:n], refs[n:2 * n]
        send_sems, recv_sems = refs[2 * n:]
        x, y, c = lax.axis_index("x"), lax.axis_index("y"), lax.axis_index("c")
        peers = [(1 - x, y), (x, 1 - y), (1 - x, 1 - y)]
        sends = []
        for t in range(n):
            for k, (px, py) in enumerate(peers):
                cp = pltpu.make_async_remote_copy(
                    src_ref=ins[t].at[_chip_index(px, py)], dst_ref=outs[t].at[k], send_sem=send_sems.at[t, k],
                    recv_sem=recv_sems.at[t, k], device_id=(px, py, c), device_id_type=MESH)
                cp.start()
                sends.append(cp)
        for cp in sends:
            cp.wait_recv()
        for cp in sends:
            cp.wait_send()

    return pl.pallas_call(
        body, name="scatter_halves",
        in_specs=[HBM] * n, out_specs=[HBM] * n,
        out_shape=[jax.ShapeDtypeStruct((3,) + a.shape[1:], a.dtype) for a in parts],
        scratch_shapes=[pltpu.SemaphoreType.DMA((n, 3)), pltpu.SemaphoreType.DMA((n, 3))],
    )(*parts)


def _tiling(r, c, f32_per_elem):
    if r % 8 == 0:
        tr = _row_tile(r, f32_per_elem * c * 4)
        return r // tr, (tr, c), lambda i: (i, 0)
    assert c % 128 == 0, (r, c)
    return c // 128, (r, 128), lambda i: (0, i)


def _pair_sum(name, slotted, other, idx, axis):
    _, r, c = slotted.shape
    hr, hc = _half_shape((r, c), axis)
    n, (tr, tc), at = _tiling(hr, hc, 13)
    if axis == 0:
        a = slotted.reshape(N_CHIPS, 2, hr, c)
        a_all = pl.BlockSpec((N_CHIPS, 1, tr, tc), lambda i, s: (0, s[0]) + at(i))
        a_own = pl.BlockSpec((1, 1, tr, tc), lambda i, s: (s[1], s[0]) + at(i))
    else:
        a, per_half = slotted, hc // tc
        a_all = pl.BlockSpec((N_CHIPS, tr, tc), lambda i, s: (0, at(i)[0], s[0] * per_half + at(i)[1]))
        a_own = pl.BlockSpec((1, tr, tc), lambda i, s: (s[1], at(i)[0], s[0] * per_half + at(i)[1]))

    def body(idx_ref, a_ref, b_ref, am_ref, bm_ref, p_ref, own_ref):
        mine, mine_own = (a_ref[:, 0], am_ref[0, 0]) if axis == 0 else (a_ref[...], am_ref[0])
        p_ref[...] = (mine + b_ref[...]).astype(p_ref.dtype)
        own_ref[...] = mine_own + bm_ref[0]

    grid_spec = pltpu.PrefetchScalarGridSpec(
        num_scalar_prefetch=1, grid=(n,),
        in_specs=[a_all, pl.BlockSpec((N_CHIPS, tr, tc), lambda i, s: (0,) + at(i)),
                  a_own, pl.BlockSpec((1, tr, tc), lambda i, s: (s[1],) + at(i))],
        out_specs=[pl.BlockSpec((N_CHIPS, tr, tc), lambda i, s: (0,) + at(i)),
                   pl.BlockSpec((tr, tc), lambda i, s: at(i))])
    return pl.pallas_call(
        body, name=name, grid_spec=grid_spec,
        out_shape=[jax.ShapeDtypeStruct((N_CHIPS, hr, hc), BF16), jax.ShapeDtypeStruct((hr, hc), F32)],
        compiler_params=_cparams(("parallel",)),
    )(idx, a, other, a, other)


def _sum_parts(name, own, recv):
    h, c = own.shape
    n, (tr, tc), at = _tiling(h, c, 4)

    def body(o_ref, r_ref, out_ref):
        acc = o_ref[...]
        for k in range(3):
            acc = acc + r_ref[k].astype(F32)
        out_ref[...] = acc

    return pl.pallas_call(
        body, name=name, grid=(n,),
        in_specs=[pl.BlockSpec((tr, tc), at), pl.BlockSpec((3, tr, tc), lambda i: (0,) + at(i))],
        out_specs=pl.BlockSpec((tr, tc), at),
        out_shape=jax.ShapeDtypeStruct((h, c), F32),
        compiler_params=_cparams(("parallel",)),
    )(own, recv)


def _swap_sibling(parts):
    n = len(parts)

    def body(*refs):
        ins, outs = refs[:n], refs[n:2 * n]
        send_sems, recv_sems = refs[2 * n:]
        x, y, c = lax.axis_index("x"), lax.axis_index("y"), lax.axis_index("c")
        cps = []
        for t in range(n):
            cp = pltpu.make_async_remote_copy(
                src_ref=ins[t], dst_ref=outs[t], send_sem=send_sems.at[t], recv_sem=recv_sems.at[t],
                device_id=(x, y, 1 - c), device_id_type=MESH)
            cp.start()
            cps.append(cp)
        for cp in cps:
            cp.wait_recv()
        for cp in cps:
            cp.wait_send()

    return pl.pallas_call(
        body, name="swap_sibling",
        in_specs=[HBM] * n, out_specs=[HBM] * n,
        out_shape=[jax.ShapeDtypeStruct(a.shape, a.dtype) for a in parts],
        scratch_shapes=[pltpu.SemaphoreType.DMA((n,)), pltpu.SemaphoreType.DMA((n,))],
    )(*parts)


def _sum_slots(name, a):
    k, r, c = a.shape
    tr = _row_tile(r, (k + 1) * c * 4)

    def body(a_ref, o_ref):
        acc = a_ref[0]
        for i in range(1, k):
            acc = acc + a_ref[i]
        o_ref[...] = acc

    return pl.pallas_call(
        body, name=name, grid=(r // tr,),
        in_specs=[pl.BlockSpec((k, tr, c), lambda i: (0, i, 0))],
        out_specs=pl.BlockSpec((tr, c), lambda i: (i, 0)),
        out_shape=jax.ShapeDtypeStruct((r, c), a.dtype),
        compiler_params=_cparams(("parallel",)),
    )(a)


def _adamw(name, w, m, v, g_parts):
    r, c = w.shape
    ng = len(g_parts)
    tr = _row_tile(r, (7 + ng) * c * 4)
    c1 = 1.0 - ADAM_B1 ** ADAM_STEP
    c2 = 1.0 - ADAM_B2 ** ADAM_STEP

    def body(*refs):
        w_ref, m_ref, v_ref = refs[:3]
        g_refs = refs[3:3 + ng]
        go_ref, d_ref, mo_ref, vo_ref = refs[3 + ng:]
        g = g_refs[0][...]
        for gr in g_refs[1:]:
            g = g + gr[...]
        m_new = ADAM_B1 * m_ref[...] + (1.0 - ADAM_B1) * g
        v_new = ADAM_B2 * v_ref[...] + (1.0 - ADAM_B2) * (g * g)
        go_ref[...] = g
        mo_ref[...] = m_new
        vo_ref[...] = v_new
        d_ref[...] = -ADAM_LR * ((m_new / c1) / (jnp.sqrt(v_new / c2) + ADAM_EPS) + ADAM_WD * w_ref[...])

    spec = pl.BlockSpec((tr, c), lambda i: (i, 0))
    return pl.pallas_call(
        body, name=name, grid=(r // tr,),
        in_specs=[spec] * (3 + ng), out_specs=[spec] * 4,
        out_shape=[jax.ShapeDtypeStruct((r, c), F32)] * 4,
        compiler_params=_cparams(("parallel",)),
    )(w, m, v, *g_parts)


def _adamw_halves(name, w, m, v, own, other, idx, axis):
    hr, hc = own.shape
    nb, (tr, tc), at = _tiling(hr, hc, 9)
    c1 = 1.0 - ADAM_B1 ** ADAM_STEP
    c2 = 1.0 - ADAM_B2 ** ADAM_STEP

    def body(idx_ref, w_ref, m_ref, v_ref, own_ref, oth_ref, go_ref, d_ref, mo_ref, vo_ref):
        g = jnp.where(pl.program_id(0) == idx_ref[0], own_ref[...], oth_ref[...])
        m_new = ADAM_B1 * m_ref[...] + (1.0 - ADAM_B1) * g
        v_new = ADAM_B2 * v_ref[...] + (1.0 - ADAM_B2) * (g * g)
        go_ref[...] = g
        mo_ref[...] = m_new
        vo_ref[...] = v_new
        d_ref[...] = -ADAM_LR * ((m_new / c1) / (jnp.sqrt(v_new / c2) + ADAM_EPS) + ADAM_WD * w_ref[...])

    per_half = (hc // tc) if axis else (hr // tr)
    if axis:
        full = pl.BlockSpec((tr, tc), lambda hh, i, s: (at(i)[0], hh * per_half + at(i)[1]))
    else:
        full = pl.BlockSpec((tr, tc), lambda hh, i, s: (hh * per_half + at(i)[0], at(i)[1]))
    part = pl.BlockSpec((tr, tc), lambda hh, i, s: at(i))
    grid_spec = pltpu.PrefetchScalarGridSpec(
        num_scalar_prefetch=1, grid=(2, nb), in_specs=[full, full, full, part, part], out_specs=[full] * 4)
    return pl.pallas_call(
        body, name=name, grid_spec=grid_spec, out_shape=[jax.ShapeDtypeStruct(w.shape, F32)] * 4,
        compiler_params=_cparams(("parallel", "parallel")),
    )(idx, w, m, v, own, other)


WEIGHTS = ['norm_w', 'w_in', 's5_a_re', 's5_a_im', 's5_b_re', 's5_b_im', 's5_c_re', 's5_c_im', 's5_d', 's5_log_step',
           's5_w_glu', 's5_b_glu', 'ssd_conv_w', 'ssd_conv_b', 'ssd_dt_bias', 'ssd_a_log', 'ssd_d', 'ssd_norm_w',
           'w_br_s5', 'w_br_ssd', 'w_out', 'ple_norm_w', 'w_ple_gate', 'w_ple_proj', 'final_norm_w']
SHARDED = {'w_in': ((IN_PROJ_DIM, 1024), 0), 's5_w_glu': ((512, 512), 0), 'ssd_conv_w': ((SSD_CONV, SSD_CONV_DIM), 1),
           'w_br_s5': ((512, 1024), 1), 'w_br_ssd': ((1536, 1024), 0), 'w_out': ((1024, 1024), 0),
           'w_ple_gate': ((1024, 1024), 0), 'w_ple_proj': ((256, 1024), 1)}
TRANSPOSED = ('w_in',)
SMALL = [n for n in WEIGHTS if n not in SHARDED]


def _shard_shape(name):
    (r, c), ax = SHARDED[name]
    return (r // N_CHIPS, c) if ax == 0 else (r, c // N_CHIPS)


def _half_axis(name):
    return 0 if (_shard_shape(name)[0] // 2) % 16 == 0 else 1


def _shard2d(name, a):
    r, c = _shard_shape(name)
    return a.reshape(c, r).T if name in TRANSPOSED else a.reshape(r, c)


def _unshard2d(name, a2, shape):
    return (a2.T if name in TRANSPOSED else a2).reshape(shape)


def _unslot(name, a4):
    (r, c), ax = SHARDED[name]
    if ax == 0:
        return a4.reshape(r, c)
    return jnp.transpose(a4, (1, 0, 2)).reshape(r, c)


def _slot(name, full):
    (r, c), ax = SHARDED[name]
    if ax == 0:
        return full.reshape(N_CHIPS, r // N_CHIPS, c)
    return jnp.transpose(full.reshape(r, N_CHIPS, c // N_CHIPS), (1, 0, 2))


GHP = ('s5_b_re', 's5_b_im')


def _view_shape(name):
    if name in ('s5_a_re', 's5_a_im'):
        return (S5_GROUPS, S5_STATE)
    if name in GHP + ('s5_c_re', 's5_c_im'):
        return (S5_GROUPS, S5_GROUP, S5_STATE)
    if name == 'ssd_conv_w':
        return (SSD_CONV, SSD_CONV_DIM // N_CHIPS)
    return (1, {'s5_log_step': S5_GROUPS, 'ssd_conv_b': SSD_CONV_DIM, 'ssd_norm_w': SSD_WIDTH, 's5_d': S5_WIDTH,
                's5_b_glu': S5_WIDTH, 'ssd_dt_bias': SSD_HEADS, 'ssd_a_log': SSD_HEADS, 'ssd_d': SSD_HEADS}.get(name, D_MODEL))


def _view(name, a):
    if name in GHP:
        return jnp.swapaxes(a.reshape(S5_GROUPS, S5_STATE, S5_GROUP), 1, 2)
    return a.reshape(_view_shape(name))


def _unview(name, a, shape):
    return (jnp.swapaxes(a, 1, 2) if name in GHP else a).reshape(shape)


def _adamw_small(ws, ms, vs, gs):
    n = len(ws)
    c1 = 1.0 - ADAM_B1 ** ADAM_STEP
    c2 = 1.0 - ADAM_B2 ** ADAM_STEP

    def body(*refs):
        w_r, m_r, v_r, g_r = (refs[k * n:(k + 1) * n] for k in range(4))
        d_o, m_o, v_o = (refs[k * n:(k + 1) * n] for k in range(4, 7))
        for i in range(n):
            g = g_r[i][...]
            m_new = ADAM_B1 * m_r[i][...] + (1.0 - ADAM_B1) * g
            v_new = ADAM_B2 * v_r[i][...] + (1.0 - ADAM_B2) * (g * g)
            m_o[i][...] = m_new
            v_o[i][...] = v_new
            d_o[i][...] = -ADAM_LR * ((m_new / c1) / (jnp.sqrt(v_new / c2) + ADAM_EPS) + ADAM_WD * w_r[i][...])

    return pl.pallas_call(
        body, name="adamw_small", out_shape=[jax.ShapeDtypeStruct(w.shape, F32) for w in ws] * 3,
        compiler_params=pltpu.CompilerParams(vmem_limit_bytes=VMEM_LIMIT),
    )(*ws, *ms, *vs, *gs)


def _pack_small(vals):
    flat = jnp.concatenate([v.reshape(-1).astype(F32) for v in vals])
    rows = -(-flat.shape[0] // (256 * 128)) * 256
    return jnp.pad(flat, (0, rows * 128 - flat.shape[0])).reshape(rows, 128)


def _unpack_small(packed, shapes):
    flat = packed.reshape(-1)
    out, off = [], 0
    for sh in shapes:
        n = math.prod(sh)
        out.append(flat[off:off + n].reshape(sh))
        off += n
    return out


def kernel(x, p, norm_w, w_in, s5_a_re, s5_a_im, s5_b_re, s5_b_im, s5_c_re, s5_c_im, s5_d, s5_log_step, s5_w_glu, s5_b_glu, ssd_conv_w, ssd_conv_b, ssd_dt_bias, ssd_a_log, ssd_d, ssd_norm_w, w_br_s5, w_br_ssd, w_out, ple_norm_w, w_ple_gate, w_ple_proj, final_norm_w, loss_target, m_norm_w, m_w_in, m_s5_a_re, m_s5_a_im, m_s5_b_re, m_s5_b_im, m_s5_c_re, m_s5_c_im, m_s5_d, m_s5_log_step, m_s5_w_glu, m_s5_b_glu, m_ssd_conv_w, m_ssd_conv_b, m_ssd_dt_bias, m_ssd_a_log, m_ssd_d, m_ssd_norm_w, m_w_br_s5, m_w_br_ssd, m_w_out, m_ple_norm_w, m_w_ple_gate, m_w_ple_proj, m_final_norm_w, v_norm_w, v_w_in, v_s5_a_re, v_s5_a_im, v_s5_b_re, v_s5_b_im, v_s5_c_re, v_s5_c_im, v_s5_d, v_s5_log_step, v_s5_w_glu, v_s5_b_glu, v_ssd_conv_w, v_ssd_conv_b, v_ssd_dt_bias, v_ssd_a_log, v_ssd_d, v_ssd_norm_w, v_w_br_s5, v_w_br_ssd, v_w_out, v_ple_norm_w, v_w_ple_gate, v_w_ple_proj, v_final_norm_w):
    args = locals()
    wl = {n: args[n] for n in WEIGHTS}
    ml = {n: args["m_" + n] for n in WEIGHTS}
    vl = {n: args["v_" + n] for n in WEIGHTS}
    big = [n for n in SHARDED if n != 'ssd_conv_w']
    chip = _chip_index(lax.axis_index("x"), lax.axis_index("y"))
    idx = jnp.stack([lax.axis_index("c"), chip]).astype(jnp.int32)

    axes = [_half_axis(n) for n in big]
    gathered = _gather_chips([_shard2d(n, wl[n]).astype(BF16) for n in big], axes,
                             [_shard2d('ssd_conv_w', wl['ssd_conv_w'])])
    full = {n: wl[n] for n in SMALL}
    for n, a4 in zip(big + ['ssd_conv_w'], gathered, strict=True):
        if n in TRANSPOSED:
            full[n + "_t"] = a4
        else:
            full[n] = _unslot(n, a4)

    loss, grad_x, g = _local_step(x, p[0], loss_target, full)
    for n in TRANSPOSED:
        g[n] = g.pop(n + "_t")

    for n in ('s5_b_re', 's5_b_im'):
        g[n] = g.pop(n + "_ghp")
    small_shapes = [(1, 1)] + [_view_shape(n) for n in SMALL] + [(SSD_CONV, SSD_CONV_DIM)]
    small_pack = _pack_small([loss] + [g[n] for n in SMALL] + [g['ssd_conv_w']])
    slotted = [g[n] if n in TRANSPOSED else _slot(n, g[n]) for n in big]
    swapped = _swap_halves(slotted, axes, small_pack)
    pair = [_pair_sum("pair_sum_" + n, a, b, idx, ax)
            for n, a, b, ax in zip(big, slotted, swapped[:-1], axes, strict=True)]
    small_sum = _sum_slots("sum_small", swapped[-1])
    recv = _scatter_halves([pb for pb, _ in pair])
    halves = [_sum_parts("sum_chips_" + n, own, r) for n, (_, own), r in zip(big, pair, recv, strict=True)]
    other_halves = _swap_sibling(halves)

    out_g, out_d, out_m, out_v = {}, {}, {}, {}
    for n, own, oth, ax in zip(big, halves, other_halves, axes, strict=True):
        res = _adamw_halves("adamw_" + n, _shard2d(n, wl[n]), _shard2d(n, ml[n]), _shard2d(n, vl[n]), own, oth, idx, ax)
        out_g[n], out_d[n], out_m[n], out_v[n] = (_unshard2d(n, r, wl[n].shape) for r in res)
    sm = _unpack_small(small_sum, small_shapes)
    loss_total = sm[0].reshape(())
    conv_g = lax.dynamic_slice(sm[-1], (0, chip * (SSD_CONV_DIM // N_CHIPS)), (SSD_CONV, SSD_CONV_DIM // N_CHIPS))
    names = SMALL + ['ssd_conv_w']
    grads = sm[1:-1] + [conv_g]
    res = _adamw_small([_view(n, wl[n]) for n in names], [_view(n, ml[n]) for n in names],
                       [_view(n, vl[n]) for n in names], grads)
    for i, n in enumerate(names):
        out_g[n] = _unview(n, grads[i], wl[n].shape)
        out_d[n], out_m[n], out_v[n] = (_unview(n, res[k * len(names) + i], wl[n].shape) for k in range(3))

    return (loss_total, grad_x, *[out_g[n] for n in WEIGHTS], *[out_d[n] for n in WEIGHTS],
            *[out_m[n] for n in WEIGHTS], *[out_v[n] for n in WEIGHTS])
```

```python
import functools
import math

import jax
import jax.numpy as jnp
from jax import lax
from jax.experimental import pallas as pl
from jax.experimental.pallas import tpu as pltpu

F32 = jnp.float32
BF16 = jnp.bfloat16
MESH = pl.DeviceIdType.MESH

D_MODEL = 1024
PLE_DIM = 256
RMS_EPS = 1e-6
S5_WIDTH = 512
S5_GROUP = 16
S5_GROUPS = 32
S5_STATE = 64
S5_N = S5_GROUPS * S5_STATE
S5_LB = 512
S5_NJ = S5_N // S5_LB
S5_TB = 256
S5_LOG_TB = 8
SSD_WIDTH = 1536
SSD_HEADDIM = 64
SSD_HEADS = 24
SSD_GROUPS = 4
SSD_HPG = 6
SSD_STATE = 128
SSD_CONV = 4
SSD_CHUNK = 128
SSD_BC = 512
SSD_CONV_DIM = 2560
GROUP_W = SSD_WIDTH // SSD_GROUPS
N_CHIPS = 4
N_DEV = 8

OFF_XBC, OFF_U5, OFF_Z5, OFF_DT, OFF_G5, OFF_GS, OFF_ZS = 0, 2560, 3072, 3584, 4096, 5120, 6144
DT_W = 512
PROJ_W = 7680
IN_PROJ_DIM = 7192

ADAM_LR, ADAM_B1, ADAM_B2, ADAM_EPS, ADAM_WD, ADAM_STEP = 0.001, 0.9, 0.999, 1e-08, 0.01, 10

VMEM_LIMIT = 56 * 1024 * 1024


def _pick(n, cands):
    for c in cands:
        if n % c == 0:
            return c
    return n


ROW_BLOCK_BYTES = 8 * 1024 * 1024


def _row_tile(r, bytes_per_row):
    for t in (r, 4096, 2048, 1024, 512, 256, 128, 64, 32, 16, 8):
        if t <= r and r % t == 0 and t * bytes_per_row <= ROW_BLOCK_BYTES:
            return t
    return r


def _cparams(sem):
    return pltpu.CompilerParams(dimension_semantics=sem, vmem_limit_bytes=VMEM_LIMIT)


def _dg(a, b, ca, cb):
    return lax.dot_general(a.astype(BF16), b.astype(BF16), (((ca,), (cb,)), ((), ())), preferred_element_type=F32)


@jax.custom_vjp
def dot_nn(a, b):
    return _dg(a, b, 1, 0)


@jax.custom_vjp
def dot_nt(a, b):
    return _dg(a, b, 1, 1)


@jax.custom_vjp
def dot_tn(a, b):
    return _dg(a, b, 0, 0)


dot_nn.defvjp(lambda a, b: (_dg(a, b, 1, 0), (a, b)), lambda r, g: (_dg(g, r[1], 1, 1), _dg(r[0], g, 0, 0)))
dot_nt.defvjp(lambda a, b: (_dg(a, b, 1, 1), (a, b)), lambda r, g: (_dg(g, r[1], 1, 0), _dg(g, r[0], 0, 0)))
dot_tn.defvjp(lambda a, b: (_dg(a, b, 0, 0), (a, b)), lambda r, g: (_dg(r[1], g, 1, 1), _dg(r[0], g, 1, 0)))


MM_VMEM_BUDGET = 30 * 1024 * 1024


def _mm_tiles(m, n, k, sa, sb, so, tn_only=None):
    best, best_key = None, None
    for tm in (1024, 512, 256, 128, 64, 32, 16, 8):
        if m % tm:
            continue
        for tn in (2048, 1536, 1280, 1024, 768, 640, 512, 384, 256, 128):
            if n % tn or (tn_only is not None and tn not in tn_only):
                continue
            for tk in (k, 2048, 1536, 1280, 1024, 768, 512, 256, 128):
                if k % tk or tk > max(k, 128) or (tk == k and k > 2048 and k % 128 == 0):
                    continue
                need = 2 * (tm * tk * sa + tk * tn * sb + tm * tn * so) + (tm * tn * 4 if tk < k else 0)
                if need > MM_VMEM_BUDGET:
                    continue
                key = (tm * tn * tk, tk)
                if best_key is None or key > best_key:
                    best, best_key = (tm, tn, tk), key
    assert best is not None, (m, n, k)
    return best


def _matmul(name, a, b, *, ta=False, tb=False, a_win=None, out_dtype=F32, epilogue=None, epi_rows=(), epi_pars=(),
            epi_outs=(), full_rows=False, epi_accs=(), epi_into=None):
    a_off, a_w = a_win if a_win is not None else (0, a.shape[1])
    if ta:
        kdim, m = a.shape[0], a_w
    else:
        m, kdim = a.shape[0], a_w
    n = b.shape[0] if tb else b.shape[1]
    assert (b.shape[1] if tb else b.shape[0]) == kdim, (name, a.shape, b.shape)
    out_dtypes = list(epi_outs) if epilogue is not None else [out_dtype]
    so = sum(jnp.dtype(d).itemsize for d in out_dtypes) + sum(r.dtype.itemsize for r, _ in epi_rows)
    tn_ok = [n] if full_rows else [t for t in (1024, 512, 256, 128) if all(off % t == 0 for _, off in epi_rows)]
    tm, tn, tk = _mm_tiles(m, n, kdim, a.dtype.itemsize, b.dtype.itemsize, so, tn_ok if epilogue is not None else None)
    nk = kdim // tk
    n_er, n_ep, n_out = len(epi_rows), len(epi_pars), len(out_dtypes)
    if ta:
        assert a_off % tm == 0
        a_spec = pl.BlockSpec((tk, tm), lambda i, j, k: (k, i + a_off // tm))
    else:
        assert a_off % tk == 0
        a_spec = pl.BlockSpec((tm, tk), lambda i, j, k: (i, k + a_off // tk))
    if tb:
        b_spec = pl.BlockSpec((tn, tk), lambda i, j, k: (j, k))
    else:
        b_spec = pl.BlockSpec((tk, tn), lambda i, j, k: (k, j))
    ca, cb = (0 if ta else 1), (1 if tb else 0)

    n_acc = len(epi_accs)
    n_x = 1 if epi_into is not None else 0
    assert not (n_acc or n_x) or full_rows

    def body(a_ref, b_ref, *refs):
        er, ep = refs[:n_er], refs[n_er:n_er + n_ep]
        first_out = n_er + n_ep + n_x
        o_refs = refs[first_out:first_out + n_out]
        s_refs = refs[first_out + n_out:first_out + n_out + n_acc]
        acc = refs[first_out + n_out + n_acc:]

        def finish(c):
            outs = [c] if epilogue is None else epilogue(c, [r[...] for r in er], [p[...] for p in ep])
            if n_acc:
                outs, sums = outs

                @pl.when(pl.program_id(0) == 0)
                def _():
                    for s_ref in s_refs:
                        s_ref[...] = jnp.zeros_like(s_ref)

                for s_ref, s in zip(s_refs, sums, strict=True):
                    s_ref[...] += jnp.broadcast_to(s, s_ref.shape)
            for o_ref, o in zip(o_refs, outs, strict=True):
                o_ref[...] = o.astype(o_ref.dtype)

        if nk == 1:
            finish(_dg(a_ref[...], b_ref[...], ca, cb))
            return
        (acc_ref,) = acc
        k = pl.program_id(2)

        @pl.when(k == 0)
        def _():
            acc_ref[...] = jnp.zeros_like(acc_ref)

        acc_ref[...] += _dg(a_ref[...], b_ref[...], ca, cb)

        @pl.when(k == nk - 1)
        def _():
            finish(acc_ref[...])

    in_specs = [a_spec, b_spec]
    in_specs += [pl.BlockSpec((tm, tn), functools.partial(lambda i, j, k, c: (i, j + c), c=off // tn)) for _, off in epi_rows]
    in_specs += [pl.BlockSpec((p.shape[0], tn), lambda i, j, k: (0, j)) for p in epi_pars]
    out_specs = [pl.BlockSpec((tm, tn), lambda i, j, k: (i, j)) for _ in out_dtypes]
    out_shape = [jax.ShapeDtypeStruct((m, n), d) for d in out_dtypes]
    extra, aliases = [], {}
    if epi_into is not None:
        buf, off, width = epi_into
        assert off % width == 0 and buf.dtype == out_dtypes[0]
        in_specs.append(pl.BlockSpec(memory_space=pl.ANY))
        out_specs[0] = pl.BlockSpec((tm, width), functools.partial(lambda i, j, k, c: (i, c), c=off // width))
        out_shape[0] = jax.ShapeDtypeStruct(buf.shape, buf.dtype)
        extra, aliases = [buf], {2 + n_er + n_ep: 0}
    out_specs += [pl.BlockSpec((r, w), lambda i, j, k: (0, 0)) for r, w in epi_accs]
    out_shape += [jax.ShapeDtypeStruct((r, w), F32) for r, w in epi_accs]
    res = pl.pallas_call(
        body, name=name, grid=(m // tm, n // tn, nk),
        in_specs=in_specs, out_specs=out_specs, out_shape=out_shape, input_output_aliases=aliases,
        scratch_shapes=[pltpu.VMEM((tm, tn), F32)] if nk > 1 else [],
        compiler_params=_cparams(("arbitrary",) * 3 if n_acc else ("parallel", "parallel", "arbitrary")),
    )(a, b, *[r for r, _ in epi_rows], *epi_pars, *extra)
    return res if epilogue is not None else res[0]


BAND = 128


def _band_matmul(name, kind, a, b, *, a_blk0=0, out_dtype=F32, epilogue=None, epi_rows=(), epi_pars=(), epi_outs=(),
                 into=None):
    n_rows = a.shape[0]
    blk = 2 * S5_LB
    tm = _pick(n_rows, (1024, 512, 256))
    if epilogue is not None:
        assert kind == "nt"
        n_er, n_ep = len(epi_rows), len(epi_pars)
        in_specs = [pl.BlockSpec((tm, blk), lambda i, j: (i, j)), pl.BlockSpec((BAND, blk), lambda i, j: (j, 0))]
        in_specs += [pl.BlockSpec((tm, BAND), functools.partial(lambda i, j, c: (i, c + j), c=c0)) for _, c0 in epi_rows]
        in_specs += [pl.BlockSpec((1, BAND), lambda i, j: (0, j)) for _ in epi_pars]
        out_specs = [pl.BlockSpec((tm, BAND), lambda i, j: (i, j)) for _ in epi_outs]
        out_shape = [jax.ShapeDtypeStruct((n_rows, S5_NJ * BAND), d) for d in epi_outs]
        extra, aliases = [], {}
        if into is not None:
            buf, c0 = into
            in_specs.append(pl.BlockSpec(memory_space=pl.ANY))
            out_specs[0] = pl.BlockSpec((tm, BAND), functools.partial(lambda i, j, c: (i, c + j), c=c0))
            out_shape[0] = jax.ShapeDtypeStruct(buf.shape, buf.dtype)
            extra, aliases = [buf], {2 + n_er + n_ep: 0}

        def epi_body(a_ref, b_ref, *refs):
            outs = epilogue(_dg(a_ref[...], b_ref[...], 1, 1), [r[...] for r in refs[:n_er]],
                            [p[...] for p in refs[n_er:n_er + n_ep]])
            for o_ref, o in zip(refs[n_er + n_ep + len(extra):], outs, strict=True):
                o_ref[...] = o.astype(o_ref.dtype)

        return pl.pallas_call(
            epi_body, name=name, grid=(n_rows // tm, S5_NJ), in_specs=in_specs, out_specs=out_specs,
            out_shape=out_shape, input_output_aliases=aliases, compiler_params=_cparams(("parallel", "parallel")),
        )(a, b, *[r for r, _ in epi_rows], *epi_pars, *extra)
    if kind == "nn":
        grid = (n_rows // tm, S5_NJ)
        in_specs = [pl.BlockSpec((tm, BAND), lambda i, j: (i, a_blk0 + j)), pl.BlockSpec((BAND, blk), lambda i, j: (j, 0))]
        out_spec = pl.BlockSpec((tm, blk), lambda i, j: (i, j))
        out_shape = (n_rows, S5_NJ * blk)
        sem = ("parallel", "parallel")

        def body(a_ref, b_ref, o_ref):
            o_ref[...] = _dg(a_ref[...], b_ref[...], 1, 0).astype(o_ref.dtype)
    elif kind == "nt":
        grid = (n_rows // tm, S5_NJ)
        in_specs = [pl.BlockSpec((tm, blk), lambda i, j: (i, j)), pl.BlockSpec((BAND, blk), lambda i, j: (j, 0))]
        out_spec = pl.BlockSpec((tm, BAND), lambda i, j: (i, j))
        out_shape = (n_rows, S5_NJ * BAND)
        sem = ("parallel", "parallel")

        def body(a_ref, b_ref, o_ref):
            o_ref[...] = _dg(a_ref[...], b_ref[...], 1, 1).astype(o_ref.dtype)
    else:
        grid = (S5_NJ, n_rows // tm)
        in_specs = [pl.BlockSpec((tm, BAND), lambda j, k: (k, a_blk0 + j)), pl.BlockSpec((tm, blk), lambda j, k: (k, j))]
        out_spec = pl.BlockSpec((BAND, blk), lambda j, k: (j, 0))
        out_shape = (S5_NJ * BAND, blk)
        sem = ("parallel", "arbitrary")

        def body(a_ref, b_ref, o_ref):
            @pl.when(pl.program_id(1) == 0)
            def _():
                o_ref[...] = jnp.zeros_like(o_ref)

            o_ref[...] += _dg(a_ref[...], b_ref[...], 0, 0)

    return pl.pallas_call(
        body, name=name, grid=grid, in_specs=in_specs, out_specs=out_spec,
        out_shape=jax.ShapeDtypeStruct(out_shape, out_dtype), compiler_params=_cparams(sem),
    )(a, b)


def _rowwise(name, fn, n_rows, tr, row_ins, par_ins, row_outs, acc_outs=(), into=None):
    nr, npar, no, na = len(row_ins), len(par_ins), len(row_outs), len(acc_outs)
    in_specs = []
    for arr, off, w in row_ins:
        assert off % w == 0 and arr.shape[0] == n_rows, (name, arr.shape, off, w)
        in_specs.append(pl.BlockSpec((tr, w), functools.partial(lambda i, c: (i, c), c=off // w)))
    for arr, off, w in par_ins:
        assert off % w == 0
        in_specs.append(pl.BlockSpec((arr.shape[0], w), functools.partial(lambda i, c: (0, c), c=off // w)))
    out_specs = [pl.BlockSpec((tr, w), lambda i: (i, 0)) for w, _ in row_outs]
    out_specs += [pl.BlockSpec((r, w), lambda i: (0, 0)) for r, w in acc_outs]
    out_shape = [jax.ShapeDtypeStruct((n_rows, w), dt) for w, dt in row_outs]
    out_shape += [jax.ShapeDtypeStruct((r, w), F32) for r, w in acc_outs]
    extra, aliases = [], {}
    if into is not None:
        buf, off = into
        w0 = row_outs[0][0]
        assert off % w0 == 0 and buf.dtype == row_outs[0][1]
        in_specs.append(pl.BlockSpec(memory_space=pl.ANY))
        out_specs[0] = pl.BlockSpec((tr, w0), functools.partial(lambda i, c: (i, c), c=off // w0))
        out_shape[0] = jax.ShapeDtypeStruct(buf.shape, buf.dtype)
        extra, aliases = [buf], {nr + npar: 0}
    nx = len(extra)

    def body(*refs):
        rows = [r[...] for r in refs[:nr]]
        pars = [r[...] for r in refs[nr:nr + npar]]
        o_refs = refs[nr + npar + nx:nr + npar + nx + no]
        a_refs = refs[nr + npar + nx + no:]
        outs, accs = fn(rows, pars)
        for o_ref, o in zip(o_refs, outs, strict=True):
            o_ref[...] = o.astype(o_ref.dtype)
        if na:
            @pl.when(pl.program_id(0) == 0)
            def _():
                for a_ref in a_refs:
                    a_ref[...] = jnp.zeros_like(a_ref)

            for a_ref, a in zip(a_refs, accs, strict=True):
                a_ref[...] += jnp.broadcast_to(a, a_ref.shape)

    res = pl.pallas_call(
        body, name=name, grid=(n_rows // tr,),
        in_specs=in_specs, out_specs=out_specs, out_shape=out_shape, input_output_aliases=aliases,
        compiler_params=_cparams(("arbitrary",) if na else ("parallel",)),
    )(*[a for a, _, _ in row_ins], *[a for a, _, _ in par_ins], *extra)
    return res


def _rms(x, w):
    return x * lax.rsqrt(jnp.mean(x * x, axis=-1, keepdims=True) + RMS_EPS) * w


def _gated_norm(y, z, w):
    outs = []
    for g in range(SSD_GROUPS):
        sl = slice(g * GROUP_W, (g + 1) * GROUP_W)
        yg = y[:, sl] * jax.nn.silu(z[:, sl])
        outs.append(yg * lax.rsqrt(jnp.mean(yg * yg, axis=-1, keepdims=True) + RMS_EPS) * w[:, sl])
    return jnp.concatenate(outs, axis=-1)


def _s5_out(yc, u, z, t, d, bg):
    ge = jax.nn.gelu(yc + d * u)
    return ge * jax.nn.sigmoid(t + bg) * jax.nn.silu(z)


def _merge(g5, gs, m5, ms):
    return jax.nn.sigmoid(g5) * m5 + jax.nn.sigmoid(gs) * ms


def _head_loss(h1, pgl, pp, fw, tgt):
    h2 = h1 + jax.nn.sigmoid(pgl) * pp
    err = _rms(h2, fw) - tgt
    per_row = 0.5 * jnp.mean(err * err, axis=-1, keepdims=True)
    return jnp.sum(per_row, axis=0, keepdims=True)


def _s5_disc(a_re, a_im, log_step, b_re2, b_im2, expand):
    step = jnp.exp(log_step)
    mag = jnp.exp(a_re * step)
    lb_re = mag * jnp.cos(a_im * step)
    lb_im = mag * jnp.sin(a_im * step)
    den = a_re * a_re + a_im * a_im
    n_re = lb_re - 1.0
    f_re = (n_re * a_re + lb_im * a_im) / den
    f_im = (lb_im * a_re - n_re * a_im) / den
    hi = lax.Precision.HIGHEST
    fr = jnp.dot(expand, f_re, precision=hi, preferred_element_type=F32)
    fi = jnp.dot(expand, f_im, precision=hi, preferred_element_type=F32)
    return lb_re, lb_im, fr * b_re2 - fi * b_im2, fr * b_im2 + fi * b_re2


def _s5_params_fwd(a_re, a_im, log_step, b_re2, b_im2, expand):
    gp = a_re.shape

    def body(ar, ai, ls, br, bi, ex, pr_ref, pi_ref, bbr_ref, bbi_ref):
        lr, li, bbr, bbi = _s5_disc(ar[...], ai[...], ls[...], br[...], bi[...], ex[...])
        bbr_ref[...] = bbr
        bbi_ref[...] = bbi
        qr, qi = lr, li
        for k in range(S5_LOG_TB):
            pr_ref[k] = qr
            pi_ref[k] = qi
            qr, qi = qr * lr - qi * li, qr * li + qi * lr

    return pl.pallas_call(
        body, name="s5_params_fwd",
        out_shape=(jax.ShapeDtypeStruct((S5_LOG_TB,) + gp, F32), jax.ShapeDtypeStruct((S5_LOG_TB,) + gp, F32),
                   jax.ShapeDtypeStruct(b_re2.shape, F32), jax.ShapeDtypeStruct(b_re2.shape, F32)),
    )(a_re, a_im, log_step, b_re2, b_im2, expand)


def _s5_params_bwd(a_re, a_im, log_step, b_re2, b_im2, expand, d_lr, d_li, d_bbr, d_bbi):
    def body(ar, ai, ls, br, bi, ex, glr, gli, gbr, gbi, dar, dai, dls, dbr, dbi):
        _, vjp = jax.vjp(lambda *p: _s5_disc(*p, ex[...]), ar[...], ai[...], ls[...], br[...], bi[...])
        g = vjp((glr[...], gli[...], gbr[...], gbi[...]))
        for ref, val in zip((dar, dai, dls, dbr, dbi), g, strict=True):
            ref[...] = val

    return pl.pallas_call(
        body, name="s5_params_bwd",
        out_shape=tuple(jax.ShapeDtypeStruct(v.shape, F32) for v in (a_re, a_im, log_step, b_re2, b_im2)),
    )(a_re, a_im, log_step, b_re2, b_im2, expand, d_lr, d_li, d_bbr, d_bbi)


def _scan_block(x_ref, out_ref, lp, edge_pow, cr, ci, reverse, each=None):
    n_g = x_ref.shape[0] // 8
    sub = lax.broadcasted_iota(jnp.int32, (8, S5_LB), 0)
    steps = []
    for sh in (1, 2, 4):
        keep = (sub < 8 - sh) if reverse else (sub >= sh)
        steps.append((8 - sh if reverse else sh, jnp.where(keep, lp[sh - 1:sh, :S5_LB], 0.0),
                      jnp.where(keep, lp[sh - 1:sh, S5_LB:], 0.0)))
    e_r, e_i = edge_pow[:, :S5_LB], edge_pow[:, S5_LB:]
    for r in (range(n_g - 1, -1, -1) if reverse else range(n_g)):
        rows = slice(8 * r, 8 * r + 8)
        xr, xi = x_ref[rows, :S5_LB], x_ref[rows, S5_LB:]
        for by, a_r, a_i in steps:
            pr, pi = pltpu.roll(xr, by, 0), pltpu.roll(xi, by, 0)
            xr, xi = xr + a_r * pr - a_i * pi, xi + a_r * pi + a_i * pr
        xr = xr + e_r * cr - e_i * ci
        xi = xi + e_r * ci + e_i * cr
        out_ref[rows, :S5_LB] = xr
        out_ref[rows, S5_LB:] = xi
        if each is not None:
            each(r, xr, xi)
        cr, ci = (xr[0:1, :], xi[0:1, :]) if reverse else (xr[7:8, :], xi[7:8, :])
    return cr, ci


def _s5_scan_fwd(bu, lam_pow, n_seq, seq_len):
    n_t = seq_len // S5_TB
    blk = 2 * S5_LB

    def body(bu_ref, lp_ref, s_ref, cr, ci, buf):
        @pl.when(pl.program_id(2) == 0)
        def _():
            cr[...] = jnp.zeros_like(cr)
            ci[...] = jnp.zeros_like(ci)

        lp = lp_ref[...]
        cr[...], ci[...] = _scan_block(bu_ref, buf, lp, lp, cr[...], ci[...], False)
        s_ref[...] = buf[...].astype(s_ref.dtype)

    return pl.pallas_call(
        body, name="s5_scan_fwd", grid=(S5_NJ, n_seq, n_t),
        in_specs=[pl.BlockSpec((S5_TB, blk), lambda j, b, t: (b * n_t + t, j)),
                  pl.BlockSpec((S5_LOG_TB, blk), lambda j, b, t: (0, j))],
        out_specs=pl.BlockSpec((S5_TB, blk), lambda j, b, t: (b * n_t + t, j)),
        out_shape=jax.ShapeDtypeStruct(bu.shape, BF16),
        scratch_shapes=[pltpu.VMEM((1, S5_LB), F32), pltpu.VMEM((1, S5_LB), F32), pltpu.VMEM((S5_TB, blk), F32)],
        compiler_params=_cparams(("parallel", "parallel", "arbitrary")),
    )(bu, lam_pow)


def _s5_scan_bwd(ds, s, lam_pow_conj, n_seq, seq_len):
    n_t = seq_len // S5_TB
    blk = 2 * S5_LB
    halo_rows = 16
    halo_per_blk = S5_TB // halo_rows

    def rows(j, b, t):
        return (b * n_t + (n_t - 1 - t), j)

    def halo(j, b, t):
        return (jnp.maximum((b * n_t + (n_t - 1 - t)) * halo_per_blk - 1, 0), j)

    def body(ds_ref, sq_ref, hq_ref, lp_ref, g_ref, dl_ref, cr, ci, buf, s_ref):
        b, t = pl.program_id(1), pl.program_id(2)
        s_ref[...] = sq_ref[...].astype(F32)
        h_last = hq_ref[...].astype(F32)[halo_rows - 1:halo_rows, :]

        @pl.when(t == 0)
        def _():
            cr[...] = jnp.zeros_like(cr)
            ci[...] = jnp.zeros_like(ci)

        @pl.when((b == 0) & (t == 0))
        def _():
            dl_ref[...] = jnp.zeros_like(dl_ref)

        first_blk = t == n_t - 1
        sub = lax.broadcasted_iota(jnp.int32, (8, S5_LB), 0)
        acc = [jnp.zeros((8, S5_LB), F32), jnp.zeros((8, S5_LB), F32)]

        def each(r, gr, gi):
            rows = slice(8 * r, 8 * r + 8)
            if r == 0:
                before_r = jnp.where(first_blk, 0.0, h_last[:, :S5_LB])
                before_i = jnp.where(first_blk, 0.0, h_last[:, S5_LB:])
            else:
                before_r, before_i = s_ref[8 * r - 1:8 * r, :S5_LB], s_ref[8 * r - 1:8 * r, S5_LB:]
            sp_r = jnp.where(sub == 0, before_r, pltpu.roll(s_ref[rows, :S5_LB], 1, 0))
            sp_i = jnp.where(sub == 0, before_i, pltpu.roll(s_ref[rows, S5_LB:], 1, 0))
            acc[0] = acc[0] + gr * sp_r + gi * sp_i
            acc[1] = acc[1] + gi * sp_r - gr * sp_i

        lp = lp_ref[...]
        edge_pow = jnp.concatenate([lp[7 - i:8 - i, :] for i in range(8)], axis=0)
        cr[...], ci[...] = _scan_block(ds_ref, buf, lp, edge_pow, cr[...], ci[...], True, each)
        g_ref[...] = buf[...].astype(g_ref.dtype)
        dl_ref[:, :S5_LB] += jnp.sum(acc[0], axis=0, keepdims=True)
        dl_ref[:, S5_LB:] += jnp.sum(acc[1], axis=0, keepdims=True)

    return pl.pallas_call(
        body, name="s5_scan_bwd", grid=(S5_NJ, n_seq, n_t),
        in_specs=[pl.BlockSpec((S5_TB, blk), rows), pl.BlockSpec((S5_TB, blk), rows),
                  pl.BlockSpec((halo_rows, blk), halo), pl.BlockSpec((S5_LOG_TB, blk), lambda j, b, t: (0, j))],
        out_specs=[pl.BlockSpec((S5_TB, blk), rows), pl.BlockSpec((1, blk), lambda j, b, t: (0, j))],
        out_shape=[jax.ShapeDtypeStruct(ds.shape, BF16), jax.ShapeDtypeStruct((1, ds.shape[1]), F32)],
        scratch_shapes=[pltpu.VMEM((1, S5_LB), F32), pltpu.VMEM((1, S5_LB), F32), pltpu.VMEM((S5_TB, blk), F32),
                        pltpu.VMEM((S5_TB, blk), F32)],
        compiler_params=_cparams(("parallel", "arbitrary", "arbitrary")),
    )(ds, s, s, lam_pow_conj)


CONV_TR = 512
CONV_CW = 512


def _shift_down(x, halo, k):
    if k == 0:
        return x
    row8 = lax.broadcasted_iota(jnp.int32, halo.shape, 0)
    rolled = pltpu.roll(x, k, 0)
    top = jnp.where(row8 < k, pltpu.roll(halo, k, 0), rolled[:8])
    if x.shape[0] == 8:
        return top
    return jnp.concatenate([top, rolled[8:]], axis=0)


def _shift_up(x, halo, k):
    if k == 0:
        return x
    n = x.shape[0]
    row8 = lax.broadcasted_iota(jnp.int32, halo.shape, 0)
    rolled = pltpu.roll(x, n - k, 0)
    bot = jnp.where(row8 >= 8 - k, pltpu.roll(halo, 8 - k, 0), rolled[n - 8:])
    if n == 8:
        return bot
    return jnp.concatenate([rolled[:n - 8], bot], axis=0)


def _conv_pre(x, halo, w, b):
    acc = b + w[SSD_CONV - 1:SSD_CONV, :] * x
    for k in range(SSD_CONV - 1):
        acc = acc + w[k:k + 1, :] * _shift_down(x, halo, SSD_CONV - 1 - k)
    return acc


def _conv_specs(seq_len, col_off):
    lt = seq_len // CONV_TR
    cb = col_off // CONV_CW
    cur = pl.BlockSpec((CONV_TR, CONV_CW), lambda j, i: (i, j + cb))
    prev = pl.BlockSpec((8, CONV_CW), lambda j, i: (jnp.maximum(i * (CONV_TR // 8) - 1, 0), j + cb))
    return lt, cur, prev


def _conv_fwd(proj, conv_w, conv_b, n_rows, seq_len):
    lt, cur, prev = _conv_specs(seq_len, OFF_XBC)

    def body(x_ref, h_ref, w_ref, b_ref, o_ref):
        halo = jnp.where(pl.program_id(1) % lt == 0, 0.0, h_ref[...])
        o_ref[...] = jax.nn.silu(_conv_pre(x_ref[...], halo, w_ref[...], b_ref[...]))

    return pl.pallas_call(
        body, name="ssd_conv_fwd", grid=(SSD_CONV_DIM // CONV_CW, n_rows // CONV_TR),
        in_specs=[cur, prev, pl.BlockSpec((SSD_CONV, CONV_CW), lambda j, i: (0, j)),
                  pl.BlockSpec((1, CONV_CW), lambda j, i: (0, j))],
        out_specs=pl.BlockSpec((CONV_TR, CONV_CW), lambda j, i: (i, j)),
        out_shape=jax.ShapeDtypeStruct((n_rows, SSD_CONV_DIM), F32),
        compiler_params=_cparams(("parallel", "parallel")),
    )(proj, proj, conv_w, conv_b)


def _conv_bwd(name, proj, d_act, conv_w, conv_b, n_rows, seq_len, col_off, dproj):
    width = d_act.shape[1]
    lt, cur, prev = _conv_specs(seq_len, OFF_XBC + col_off)
    n_blk = n_rows // CONV_TR
    cb = (OFF_XBC + col_off) // CONV_CW
    pb = col_off // CONV_CW
    nxt = pl.BlockSpec((8, CONV_CW), lambda j, i: (jnp.minimum((i + 1) * (CONV_TR // 8), n_rows // 8 - 1), j + cb))
    d_cur = pl.BlockSpec((CONV_TR, CONV_CW), lambda j, i: (i, j))
    d_nxt = pl.BlockSpec((8, CONV_CW), lambda j, i: (jnp.minimum((i + 1) * (CONV_TR // 8), n_rows // 8 - 1), j))

    def dsilu(pre):
        sg = jax.nn.sigmoid(pre)
        return sg * (1.0 + pre * (1.0 - sg))

    def body(x_ref, hp_ref, hn_ref, d_ref, dn_ref, w_ref, b_ref, _, dx_ref, dw_ref, db_ref):
        i = pl.program_id(1)
        x, w, b = x_ref[...], w_ref[...], b_ref[...]
        halo_p = jnp.where(i % lt == 0, 0.0, hp_ref[...])
        at_end = i % lt == lt - 1
        dpre = d_ref[...] * dsilu(_conv_pre(x, halo_p, w, b))
        pre_n = _conv_pre(hn_ref[...], x[CONV_TR - 8:, :], w, b)
        dpre_n = jnp.where(at_end, 0.0, dn_ref[...] * dsilu(pre_n))
        dx = w[SSD_CONV - 1:SSD_CONV, :] * dpre
        for k in range(SSD_CONV - 1):
            dx = dx + w[k:k + 1, :] * _shift_up(dpre, dpre_n, SSD_CONV - 1 - k)
        dx_ref[...] = dx.astype(dx_ref.dtype)

        @pl.when(i == 0)
        def _():
            dw_ref[...] = jnp.zeros_like(dw_ref)
            db_ref[...] = jnp.zeros_like(db_ref)

        for k in range(SSD_CONV):
            xs = _shift_down(x, halo_p, SSD_CONV - 1 - k)
            dw_ref[k:k + 1, :] += jnp.sum(dpre * xs, axis=0, keepdims=True)
        db_ref[...] += jnp.sum(dpre, axis=0, keepdims=True)

    return pl.pallas_call(
        body, name=name, grid=(width // CONV_CW, n_blk),
        in_specs=[cur, prev, nxt, d_cur, d_nxt,
                  pl.BlockSpec((SSD_CONV, CONV_CW), lambda j, i: (0, j + pb)),
                  pl.BlockSpec((1, CONV_CW), lambda j, i: (0, j + pb)), pl.BlockSpec(memory_space=pl.ANY)],
        out_specs=[pl.BlockSpec((CONV_TR, CONV_CW), lambda j, i: (i, j + cb)),
                   pl.BlockSpec((SSD_CONV, CONV_CW), lambda j, i: (0, j)),
                   pl.BlockSpec((1, CONV_CW), lambda j, i: (0, j))],
        out_shape=[jax.ShapeDtypeStruct(dproj.shape, dproj.dtype),
                   jax.ShapeDtypeStruct((SSD_CONV, width), F32), jax.ShapeDtypeStruct((1, width), F32)],
        input_output_aliases={7: 0},
        compiler_params=_cparams(("parallel", "arbitrary")),
    )(proj, proj, proj, d_act, d_act, conv_w, conv_b, dproj)


def _split3(x):
    hi = x.astype(BF16)
    r = x - hi.astype(F32)
    mid = r.astype(BF16)
    return hi, mid, (r - mid.astype(F32)).astype(BF16)


def _sel_dot(a, b, a_is_sel):
    dn = (((1,), (0,)), ((), ()))
    if a_is_sel:
        return sum(lax.dot_general(a, t, dn, preferred_element_type=F32) for t in _split3(b))
    return sum(lax.dot_general(t, b, dn, preferred_element_type=F32) for t in _split3(a))


@jax.custom_vjp
def sel_left(sel, sel_t, x):
    return _sel_dot(sel, x, True)


@jax.custom_vjp
def sel_right(x, sel, sel_t):
    return _sel_dot(x, sel, False)


sel_left.defvjp(lambda s, st, x: (_sel_dot(s, x, True), (s, st)),
                lambda r, g: (jnp.zeros_like(r[0]), jnp.zeros_like(r[1]), _sel_dot(r[1], g, True)))
sel_right.defvjp(lambda x, s, st: (_sel_dot(x, s, False), (s, st)),
                 lambda r, g: (_sel_dot(g, r[1], False), jnp.zeros_like(r[0]), jnp.zeros_like(r[1])))


def _ssd_chunk(xs, bm, cm, dtr, st, dtb, alog, dsk, k):
    dt = jax.nn.softplus(dtr + dtb)
    acum = sel_left(k["tri"], k["tri_t"], dt * (-jnp.exp(alog)))
    dt_e = sel_right(dt, k["spread"], k["spread_t"])
    ac_e = sel_right(acum, k["spread"], k["spread_t"])
    al_e = ac_e[SSD_CHUNK - 1:SSD_CHUNK, :]
    dsk_e = sel_right(jnp.broadcast_to(dsk, (8, 128)), k["spread"], k["spread_t"])[0:1, :]
    xdt = xs * dt_e
    acum_t = acum.T
    scores = dot_nt(cm, bm)
    y = dot_nn(cm, st) * jnp.exp(ac_e) + xs * dsk_e
    for j in range(SSD_HPG):
        lmat = jnp.exp(jnp.where(k["causal"], acum[:, j:j + 1] - acum_t[j:j + 1, :], -jnp.inf))
        y = y + dot_nn(scores * lmat, jnp.where(k["head"] == j, xdt, 0.0))
    new = st * jnp.exp(al_e) + dot_tn(bm, xdt * jnp.exp(al_e - ac_e))
    return y, new


def _ssd_consts():
    r = lax.broadcasted_iota(jnp.int32, (SSD_CHUNK, SSD_CHUNK), 0)
    c = lax.broadcasted_iota(jnp.int32, (SSD_CHUNK, SSD_CHUNK), 1)
    hd = jnp.int32(SSD_HEADDIM)
    sr = lax.broadcasted_iota(jnp.int32, (128, GROUP_W), 0)
    sc = lax.div(lax.broadcasted_iota(jnp.int32, (128, GROUP_W), 1), hd)
    tr = lax.div(lax.broadcasted_iota(jnp.int32, (GROUP_W, 128), 0), hd)
    tc = lax.broadcasted_iota(jnp.int32, (GROUP_W, 128), 1)
    return {"tri": (r >= c).astype(BF16), "tri_t": (c >= r).astype(BF16), "causal": r >= c,
            "spread": (sr == sc).astype(BF16), "spread_t": (tr == tc).astype(BF16),
            "head": lax.div(lax.broadcasted_iota(jnp.int32, (SSD_CHUNK, GROUP_W), 1), hd)}


def _ssd_specs(n_c, reverse):
    def cidx(c):
        return n_c - 1 - c if reverse else c

    xs = pl.BlockSpec((SSD_CHUNK, GROUP_W), lambda g, b, c: (b * n_c + cidx(c), g))
    bm = pl.BlockSpec((SSD_CHUNK, SSD_STATE), lambda g, b, c: (b * n_c + cidx(c), SSD_WIDTH // SSD_STATE + g))
    cm = pl.BlockSpec((SSD_CHUNK, SSD_STATE), lambda g, b, c: (b * n_c + cidx(c), (SSD_WIDTH + SSD_BC) // SSD_STATE + g))
    dt = pl.BlockSpec((SSD_CHUNK, 128), lambda g, b, c: (b * n_c + cidx(c), OFF_DT // 128 + g))
    par = pl.BlockSpec((1, 128), lambda g, b, c: (0, g))
    st = pl.BlockSpec((1, 1, 1, SSD_STATE, GROUP_W), lambda g, b, c: (b, cidx(c), g, 0, 0))
    return xs, bm, cm, dt, par, st


def _ssd_fwd(xbc_act, proj, dtb, alog, dsk, n_seq, seq_len):
    n_c = seq_len // SSD_CHUNK
    xs_s, bm_s, cm_s, dt_s, par_s, st_s = _ssd_specs(n_c, False)

    def body(xs_ref, bm_ref, cm_ref, dt_ref, dtb_ref, al_ref, dk_ref, y_ref, st_ref, state):
        @pl.when(pl.program_id(2) == 0)
        def _():
            state[...] = jnp.zeros_like(state)

        prev = state[...]
        st_ref[0, 0, 0] = prev
        y, new = _ssd_chunk(xs_ref[...], bm_ref[...], cm_ref[...], dt_ref[...], prev,
                            dtb_ref[...], al_ref[...], dk_ref[...], _ssd_consts())
        y_ref[...] = y
        state[...] = new

    return pl.pallas_call(
        body, name="ssd_fwd", grid=(SSD_GROUPS, n_seq, n_c),
        in_specs=[xs_s, bm_s, cm_s, dt_s, par_s, par_s, par_s],
        out_specs=[pl.BlockSpec((SSD_CHUNK, GROUP_W), lambda g, b, c: (b * n_c + c, g)), st_s],
        out_shape=[jax.ShapeDtypeStruct((n_seq * seq_len, SSD_WIDTH), F32),
                   jax.ShapeDtypeStruct((n_seq, n_c, SSD_GROUPS, SSD_STATE, GROUP_W), F32)],
        scratch_shapes=[pltpu.VMEM((SSD_STATE, GROUP_W), F32)],
        compiler_params=_cparams(("parallel", "parallel", "arbitrary")),
    )(xbc_act, xbc_act, xbc_act, proj, dtb, alog, dsk)


def _ssd_bwd(xbc_act, proj, states, dy, dtb, alog, dsk, n_seq, seq_len, dproj):
    n_c = seq_len // SSD_CHUNK
    n_rows = n_seq * seq_len
    xs_s, bm_s, cm_s, dt_s, par_s, st_s = _ssd_specs(n_c, True)

    def rows(w, first=0):
        return pl.BlockSpec((SSD_CHUNK, w), lambda g, b, c: (b * n_c + (n_c - 1 - c), first + g))

    def body(xs_ref, bm_ref, cm_ref, dt_ref, st_ref, dy_ref, dtb_ref, al_ref, dk_ref, _,
             dxs_ref, dbm_ref, dcm_ref, ddt_ref, ddtb_ref, dal_ref, ddk_ref, dstate):
        b, c = pl.program_id(1), pl.program_id(2)

        @pl.when(c == 0)
        def _():
            dstate[...] = jnp.zeros_like(dstate)

        @pl.when((b == 0) & (c == 0))
        def _():
            ddtb_ref[...] = jnp.zeros_like(ddtb_ref)
            dal_ref[...] = jnp.zeros_like(dal_ref)
            ddk_ref[...] = jnp.zeros_like(ddk_ref)

        consts = _ssd_consts()
        _, vjp = jax.vjp(
            lambda xs, bm, cm, dtr, prev, dtb, alog, dsk: _ssd_chunk(xs, bm, cm, dtr, prev, dtb, alog, dsk, consts),
            xs_ref[...], bm_ref[...], cm_ref[...], dt_ref[...], st_ref[0, 0, 0], dtb_ref[...], al_ref[...], dk_ref[...])
        dxs, dbm, dcm, ddtr, dprev, ddtb, dal, ddk = vjp((dy_ref[...], dstate[...]))
        dxs_ref[...] = dxs
        dbm_ref[...] = dbm
        dcm_ref[...] = dcm
        ddt_ref[...] = ddtr.astype(ddt_ref.dtype)
        ddtb_ref[...] += ddtb
        dal_ref[...] += dal
        ddk_ref[...] += ddk
        dstate[...] = dprev

    acc = pl.BlockSpec((1, 128), lambda g, b, c: (0, g))
    return pl.pallas_call(
        body, name="ssd_bwd", grid=(SSD_GROUPS, n_seq, n_c),
        in_specs=[xs_s, bm_s, cm_s, dt_s, st_s, rows(GROUP_W), par_s, par_s, par_s, pl.BlockSpec(memory_space=pl.ANY)],
        out_specs=[rows(GROUP_W), rows(SSD_STATE), rows(SSD_STATE), rows(128, OFF_DT // 128), acc, acc, acc],
        out_shape=[jax.ShapeDtypeStruct((n_rows, SSD_WIDTH), F32), jax.ShapeDtypeStruct((n_rows, SSD_BC), F32),
                   jax.ShapeDtypeStruct((n_rows, SSD_BC), F32), jax.ShapeDtypeStruct(dproj.shape, dproj.dtype),
                   jax.ShapeDtypeStruct((1, 512), F32), jax.ShapeDtypeStruct((1, 512), F32),
                   jax.ShapeDtypeStruct((1, 512), F32)],
        input_output_aliases={9: 3},
        scratch_shapes=[pltpu.VMEM((SSD_STATE, GROUP_W), F32)],
        compiler_params=_cparams(("parallel", "arbitrary", "arbitrary")),
    )(xbc_act, xbc_act, xbc_act, proj, states, dy, dtb, alog, dsk, dproj)


def _pad_heads(v):
    return jnp.pad(v.reshape(SSD_GROUPS, SSD_HPG), ((0, 0), (0, 128 - SSD_HPG))).reshape(1, SSD_GROUPS * 128)


def _unpad_heads(v):
    return v.reshape(SSD_GROUPS, 128)[:, :SSD_HPG].reshape(1, SSD_HEADS)


def _state_cols(v):
    re, im = v
    lead = re.shape[:-1]
    re = re.reshape(lead + (S5_NJ, 1, S5_LB))
    im = im.reshape(lead + (S5_NJ, 1, S5_LB))
    return jnp.concatenate([re, im], axis=-2).reshape(lead + (2 * S5_N,))


def _state_uncols(v):
    lead = v.shape[:-1]
    v = v.reshape(lead + (S5_NJ, 2, S5_LB))
    return v[..., 0, :].reshape(lead + (S5_N,)), v[..., 1, :].reshape(lead + (S5_N,))


GROUPS_PER_BAND = BAND // S5_GROUP


def _band(w2_re, w2_im):
    gh = S5_GROUPS * S5_GROUP
    rg = ((jnp.arange(gh) // S5_GROUP) % GROUPS_PER_BAND)[:, None, None]
    cg = jnp.arange(GROUPS_PER_BAND)[None, :, None]
    parts = [jnp.where(rg == cg, v[:, None, :], 0.0).reshape(gh, S5_LB) for v in (w2_re, w2_im)]
    return jnp.concatenate(parts, axis=1)


def _band_take(wb):
    gh = S5_GROUPS * S5_GROUP
    w4 = wb.reshape(gh, 2, GROUPS_PER_BAND, S5_STATE)
    sel = w4[jnp.arange(gh), :, (jnp.arange(gh) // S5_GROUP) % GROUPS_PER_BAND, :]
    return sel[:, 0, :], sel[:, 1, :]


W_IN_SHARD = IN_PROJ_DIM // N_CHIPS
W_IN_SEGS = ((0, 512, OFF_U5), (512, 1024, OFF_Z5), (1024, 2560, OFF_ZS), (2560, 5120, OFF_XBC), (5144, 7192, OFF_G5))
DT_ROWS = (5120, 5144)


def _w_in_pieces():
    runs = []
    segs = list(W_IN_SEGS) + [(DT_ROWS[0] + SSD_HPG * g, DT_ROWS[0] + SSD_HPG * (g + 1), OFF_DT + 128 * g)
                              for g in range(SSD_GROUPS)]
    for lo, hi, off in segs:
        for j in range(N_CHIPS):
            s, e = max(lo, j * W_IN_SHARD), min(hi, (j + 1) * W_IN_SHARD)
            if s < e:
                runs.append((j, s - j * W_IN_SHARD, off + s - lo, e - s))
    return runs


RELAYOUT_LANES = 256


def _pad_w_in_t(a4):
    runs = _w_in_pieces()

    def body(a_ref, o_ref):
        o_ref[pl.ds(OFF_DT, DT_W), :] = jnp.zeros((DT_W, RELAYOUT_LANES), o_ref.dtype)
        for j, src, dst, n in runs:
            o_ref[pl.ds(dst, n), :] = a_ref[j, pl.ds(src, n), :]

    return pl.pallas_call(
        body, name="w_in_to_padded", grid=(D_MODEL // RELAYOUT_LANES,),
        in_specs=[pl.BlockSpec((N_CHIPS, W_IN_SHARD, RELAYOUT_LANES), lambda i: (0, 0, i))],
        out_specs=pl.BlockSpec((PROJ_W, RELAYOUT_LANES), lambda i: (0, i)),
        out_shape=jax.ShapeDtypeStruct((PROJ_W, D_MODEL), a4.dtype),
        compiler_params=_cparams(("parallel",)),
    )(a4)


def _unpad_w_in_t(wp):
    runs = _w_in_pieces()

    def body(p_ref, o_ref):
        for j, dst, src, n in runs:
            o_ref[j, pl.ds(dst, n), :] = p_ref[pl.ds(src, n), :]

    return pl.pallas_call(
        body, name="w_in_from_padded", grid=(D_MODEL // RELAYOUT_LANES,),
        in_specs=[pl.BlockSpec((PROJ_W, RELAYOUT_LANES), lambda i: (0, i))],
        out_specs=pl.BlockSpec((N_CHIPS, W_IN_SHARD, RELAYOUT_LANES), lambda i: (0, 0, i)),
        out_shape=jax.ShapeDtypeStruct((N_CHIPS, W_IN_SHARD, D_MODEL), wp.dtype),
        compiler_params=_cparams(("parallel",)),
    )(wp)


def _local_step(x, p, tgt, w):
    n_seq, seq_len, _ = x.shape
    n_rows = n_seq * seq_len
    tr = 512
    x2 = x.reshape(n_rows, D_MODEL)
    p2 = p.reshape(n_rows, PLE_DIM)
    t2 = tgt.reshape(n_rows, D_MODEL)
    row = functools.partial(_rowwise, n_rows=n_rows, tr=tr)

    w_pad_t = _pad_w_in_t(w["w_in_t"])
    norm_w = w["norm_w"].reshape(1, D_MODEL)
    ple_norm_w = w["ple_norm_w"].reshape(1, D_MODEL)
    final_w = w["final_norm_w"].reshape(1, D_MODEL)
    s5_d = w["s5_d"].reshape(1, S5_WIDTH)
    b_glu = w["s5_b_glu"].reshape(1, S5_WIDTH)
    conv_w = w["ssd_conv_w"].reshape(SSD_CONV, SSD_CONV_DIM)
    conv_b = w["ssd_conv_b"].reshape(1, SSD_CONV_DIM)
    ssd_norm_w = w["ssd_norm_w"].reshape(1, SSD_WIDTH)
    dtb, alog, dsk = (_pad_heads(w[k].reshape(1, SSD_HEADS)) for k in ("ssd_dt_bias", "ssd_a_log", "ssd_d"))

    gh = S5_GROUPS * S5_GROUP
    a_re = w["s5_a_re"].reshape(S5_GROUPS, S5_STATE)
    a_im = w["s5_a_im"].reshape(S5_GROUPS, S5_STATE)
    log_step = w["s5_log_step"].reshape(S5_GROUPS, 1)
    b_re2 = jnp.transpose(w["s5_b_re"].reshape(S5_GROUPS, S5_STATE, S5_GROUP), (0, 2, 1)).reshape(gh, S5_STATE)
    b_im2 = jnp.transpose(w["s5_b_im"].reshape(S5_GROUPS, S5_STATE, S5_GROUP), (0, 2, 1)).reshape(gh, S5_STATE)
    expand = (jnp.arange(gh)[:, None] // S5_GROUP == jnp.arange(S5_GROUPS)[None, :]).astype(F32)
    pow_re, pow_im, bb_re2, bb_im2 = _s5_params_fwd(a_re, a_im, log_step, b_re2, b_im2, expand)
    lam_pow = _state_cols((pow_re.reshape(S5_LOG_TB, S5_N), pow_im.reshape(S5_LOG_TB, S5_N)))
    lam_pow_conj = _state_cols((pow_re.reshape(S5_LOG_TB, S5_N), -pow_im.reshape(S5_LOG_TB, S5_N)))
    bb_band = _band(bb_re2, bb_im2).astype(BF16)
    c_band = _band(w["s5_c_re"].reshape(gh, S5_STATE), -w["s5_c_im"].reshape(gh, S5_STATE)).astype(BF16)

    late = w.get("_late")
    (hn,) = row("rms_in", lambda r, q: ([_rms(r[0], q[0])], []), row_ins=[(x2, 0, D_MODEL)],
                par_ins=[(norm_w, 0, D_MODEL)] + ([(late[0], 0, 128)] if late else []), row_outs=[(D_MODEL, BF16)])
    proj = _matmul("mm_proj", hn, w_pad_t, tb=True)
    bu = _band_matmul("mm_s5_bu", "nn", proj, bb_band, a_blk0=OFF_U5 // BAND)
    s = _s5_scan_fwd(bu, lam_pow, n_seq, seq_len)
    yc, ge = _band_matmul("mm_s5_y", "nt", s, c_band, epilogue=lambda c, r, q: [c, jax.nn.gelu(c + q[0] * r[0])],
                          epi_rows=[(proj, OFF_U5 // BAND)], epi_pars=[s5_d], epi_outs=[F32, BF16])
    if late:
        w = {**w, **late[1](ge)}
    tg, y5 = _matmul("mm_s5_glu", ge, w["s5_w_glu"], epilogue=lambda c, r, q: [c, _s5_out(r[0], r[1], r[2], c, *q)],
                     epi_rows=[(yc, 0), (proj, OFF_U5), (proj, OFF_Z5)], epi_pars=[s5_d, b_glu], epi_outs=[F32, BF16])
    s5_rows = [(yc, 0, S5_WIDTH), (proj, OFF_U5, S5_WIDTH), (proj, OFF_Z5, S5_WIDTH), (tg, 0, S5_WIDTH)]
    s5_pars = [(s5_d, 0, S5_WIDTH), (b_glu, 0, S5_WIDTH)]

    xbc_act = _conv_fwd(proj, conv_w, conv_b, n_rows, seq_len)
    y_ssd, states = _ssd_fwd(xbc_act, proj, dtb, alog, dsk, n_seq, seq_len)
    gn_rows = [(y_ssd, 0, SSD_WIDTH), (proj, OFF_ZS, SSD_WIDTH)]
    (yss,) = row("ssd_gate", lambda r, q: ([_gated_norm(r[0], r[1], q[0])], []), row_ins=gn_rows,
                 par_ins=[(ssd_norm_w, 0, SSD_WIDTH)], row_outs=[(SSD_WIDTH, BF16)])

    m5 = _matmul("mm_br_s5", y5, w["w_br_s5"])
    ms, merged = _matmul("mm_br_ssd", yss, w["w_br_ssd"], epilogue=lambda c, r, q: [c, _merge(r[0], r[1], r[2], c)],
                         epi_rows=[(proj, OFF_G5), (proj, OFF_GS), (m5, 0)], epi_outs=[F32, BF16])
    mg_rows = [(proj, OFF_G5, D_MODEL), (proj, OFF_GS, D_MODEL), (m5, 0, D_MODEL), (ms, 0, D_MODEL)]

    def resid_norm(c, r, q):
        h1_ = r[0] + c
        return [h1_, _rms(h1_, q[0])]

    h1, hp = _matmul("mm_out", merged, w["w_out"], epilogue=resid_norm, epi_rows=[(x2, 0)], epi_pars=[ple_norm_w],
                     epi_outs=[F32, BF16], full_rows=True)
    pp = _matmul("mm_ple_proj", p2, w["w_ple_proj"])

    def head_fn(pgl_, r, q):
        h1_, pp_, tgt_ = r
        loss, vjp = jax.vjp(lambda a, b, c, f: _head_loss(a, b, c, f, tgt_), h1_, pgl_, pp_, q[0])
        dh1_, dpgl_, dpp_, dfw_ = vjp(jnp.ones_like(loss))
        return [dh1_, dpgl_, dpp_], [loss, dfw_]

    dh2, dpgl, dpp, loss_acc, d_final_w = _matmul(
        "mm_ple_gate_head", hp, w["w_ple_gate"], epilogue=head_fn, epi_rows=[(h1, 0), (pp, 0), (t2, 0)],
        epi_pars=[final_w], epi_outs=[F32, BF16, BF16], epi_accs=[(1, 128), (1, D_MODEL)], full_rows=True)
    loss = loss_acc[0, 0]

    g = {}
    g["final_norm_w"] = d_final_w
    g["w_ple_gate"] = _matmul("mm_d_w_ple_gate", hp, dpgl, ta=True)
    g["w_ple_proj"] = _matmul("mm_d_w_ple_proj", p2, dpp, ta=True)

    def ple_norm_bwd(dhp_, r, q):
        h1_, dh2_ = r
        _, vjp = jax.vjp(_rms, h1_, q[0])
        dh, dw = vjp(dhp_)
        dh = dh + dh2_
        return [dh, dh], [dw]

    dh1, dh1_b, g["ple_norm_w"] = _matmul(
        "mm_d_hp", dpgl, w["w_ple_gate"], tb=True, epilogue=ple_norm_bwd, epi_rows=[(h1, 0), (dh2, 0)],
        epi_pars=[ple_norm_w], epi_outs=[F32, BF16], epi_accs=[(1, D_MODEL)], full_rows=True)
    g["w_out"] = _matmul("mm_d_w_out", merged, dh1_b, ta=True)

    dproj = lax.empty((n_rows, PROJ_W), BF16)

    def merge_bwd(dmerged, r, q):
        sg5, sgs = jax.nn.sigmoid(r[0]), jax.nn.sigmoid(r[1])
        d_gates = jnp.concatenate([dmerged * r[2] * sg5 * (1.0 - sg5), dmerged * r[3] * sgs * (1.0 - sgs)], axis=1)
        return [d_gates, dmerged * sg5, dmerged * sgs]

    dproj, dm5, dms = _matmul(
        "mm_d_merged", dh1_b, w["w_out"], tb=True, epilogue=merge_bwd,
        epi_rows=[(proj, OFF_G5), (proj, OFF_GS), (m5, 0), (ms, 0)], epi_outs=[BF16, BF16, BF16], full_rows=True,
        epi_into=(dproj, OFF_G5, 2 * D_MODEL))
    g["w_br_s5"] = _matmul("mm_d_w_br_s5", y5, dm5, ta=True)
    g["w_br_ssd"] = _matmul("mm_d_w_br_ssd", yss, dms, ta=True)
    dy5 = _matmul("mm_d_y5", dm5, w["w_br_s5"], tb=True)
    dyss = _matmul("mm_d_yss", dms, w["w_br_ssd"], tb=True)

    def s5_out_bwd_a(r, q):
        yc_, u_, z_, t_, dy_ = r
        d_, bg_ = q
        ge_ = jax.nn.gelu(yc_ + d_ * u_)
        _, vjp = jax.vjp(lambda a, z, t, b: a * jax.nn.sigmoid(t + b) * jax.nn.silu(z), ge_, z_, t_, bg_)
        dge, dz, dt_, dbg = vjp(dy_)
        return [dz, dge, dt_], [dbg]

    dproj, dge_a, dtg, g["s5_b_glu"] = row(
        "s5_out_bwd_a", s5_out_bwd_a, row_ins=s5_rows + [(dy5, 0, S5_WIDTH)], par_ins=s5_pars,
        row_outs=[(S5_WIDTH, BF16), (S5_WIDTH, F32), (S5_WIDTH, BF16)], acc_outs=[(1, S5_WIDTH)],
        into=(dproj, OFF_Z5))
    g["s5_w_glu"] = _matmul("mm_d_w_glu", ge, dtg, ta=True)
    dge_b = _matmul("mm_d_ge", dtg, w["s5_w_glu"], tb=True)

    def s5_out_bwd_b(r, q):
        yc_, u_, da_, db_ = r
        _, vjp = jax.vjp(lambda yc, u, d: jax.nn.gelu(yc + d * u), yc_, u_, q[0])
        dyc_, du_, dd_ = vjp(da_ + db_)
        return [dyc_, du_], [dd_]

    dyc, du5_a, g["s5_d"] = row(
        "s5_out_bwd_b", s5_out_bwd_b,
        row_ins=[(yc, 0, S5_WIDTH), (proj, OFF_U5, S5_WIDTH), (dge_a, 0, S5_WIDTH), (dge_b, 0, S5_WIDTH)],
        par_ins=[(s5_d, 0, S5_WIDTH)], row_outs=[(S5_WIDTH, BF16), (S5_WIDTH, F32)], acc_outs=[(1, S5_WIDTH)])
    d_c_band = _band_matmul("mm_d_c", "tn", dyc, s)
    ds = _band_matmul("mm_d_s", "nn", dyc, c_band)
    dbu, d_lam = _s5_scan_bwd(ds, s, lam_pow_conj, n_seq, seq_len)
    d_bb_band = _band_matmul("mm_d_bb", "tn", proj, dbu, a_blk0=OFF_U5 // BAND)
    (dproj,) = _band_matmul("mm_d_u5", "nt", dbu, bb_band, epilogue=lambda c, r, q: [c + r[0]],
                            epi_rows=[(du5_a, 0)], epi_outs=[BF16], into=(dproj, OFF_U5 // BAND))

    d_lr, d_li = _state_uncols(d_lam)
    d_bbr, d_bbi = _band_take(d_bb_band)
    d_are, d_aim, d_ls, d_br2, d_bi2 = _s5_params_bwd(
        a_re, a_im, log_step, b_re2, b_im2, expand,
        d_lr.reshape(S5_GROUPS, S5_STATE), d_li.reshape(S5_GROUPS, S5_STATE), d_bbr, d_bbi)
    g["s5_a_re"], g["s5_a_im"], g["s5_log_step"] = d_are, d_aim, d_ls
    g["s5_b_re_ghp"], g["s5_b_im_ghp"] = d_br2, d_bi2
    d_cr, d_ci = _band_take(d_c_band)
    g["s5_c_re"], g["s5_c_im"] = d_cr, -d_ci

    def gate_bwd(r, q):
        _, vjp = jax.vjp(_gated_norm, r[0], r[1], q[0])
        dy_, dz_, dw_ = vjp(r[2])
        return [dz_, dy_], [dw_]

    dproj, dy_ssd, g["ssd_norm_w"] = row(
        "ssd_gate_bwd", gate_bwd, row_ins=gn_rows + [(dyss, 0, SSD_WIDTH)], par_ins=[(ssd_norm_w, 0, SSD_WIDTH)],
        row_outs=[(SSD_WIDTH, BF16), (SSD_WIDTH, F32)], acc_outs=[(1, SSD_WIDTH)], into=(dproj, OFF_ZS))
    dxs, dbm, dcm, dproj, d_dtb, d_alog, d_dsk = _ssd_bwd(xbc_act, proj, states, dy_ssd, dtb, alog, dsk, n_seq, seq_len,
                                                       dproj)
    g["ssd_dt_bias"], g["ssd_a_log"], g["ssd_d"] = _unpad_heads(d_dtb), _unpad_heads(d_alog), _unpad_heads(d_dsk)
    conv_dw, conv_db = [], []
    for nm, d_act, off in (("x", dxs, 0), ("b", dbm, SSD_WIDTH), ("c", dcm, SSD_WIDTH + SSD_BC)):
        dproj, dw_, db_ = _conv_bwd("ssd_conv_bwd_" + nm, proj, d_act, conv_w, conv_b, n_rows, seq_len, off, dproj)
        conv_dw.append(dw_)
        conv_db.append(db_)
    g["ssd_conv_w"] = jnp.concatenate(conv_dw, axis=1)
    g["ssd_conv_b"] = jnp.concatenate(conv_db, axis=1)

    g["w_in_t"] = _unpad_w_in_t(_matmul("mm_d_w_in", dproj, hn, ta=True))
    def norm_bwd(dhn_, r, q):
        x_, dh1_ = r
        _, vjp = jax.vjp(_rms, x_, q[0])
        dx_, dw_ = vjp(dhn_)
        return [dx_ + dh1_], [dw_]

    dx, g["norm_w"] = _matmul("mm_d_hn", dproj, w_pad_t, epilogue=norm_bwd, epi_rows=[(x2, 0), (dh1, 0)],
                              epi_pars=[norm_w], epi_outs=[F32], epi_accs=[(1, D_MODEL)], full_rows=True)
    return loss, dx.reshape(x.shape), g


HBM = pl.BlockSpec(memory_space=pltpu.HBM)


def _chip_index(x, y):
    return 2 * x + y


def _half(shape2d, axis, which):
    h = shape2d[axis] // 2
    sl = pl.ds(pl.multiple_of(which * h, 128 if axis else 8), h)
    return (slice(None), sl) if axis else (sl, slice(None))


def _gather_chips(split, axes, whole):
    ns, nw = len(split), len(whole)
    n = ns + nw

    def body(*refs):
        ins, outs = refs[:n], refs[n:2 * n]
        ici_send, ici_recv, d2d_send, d2d_recv, local_sems = refs[2 * n:]
        x, y, c = lax.axis_index("x"), lax.axis_index("y"), lax.axis_index("c")
        me = _chip_index(x, y)
        sibling = (x, y, 1 - c)
        peers = [(1 - x, y), (x, 1 - y), (1 - x, 1 - y)]

        def half(t, which):
            return _half(split[t].shape, axes[t], which)

        copies = []
        for t in range(n):
            loc = pltpu.make_async_copy(ins[t], outs[t].at[me], local_sems.at[t])
            loc.start()
            copies.append(loc)

        def ici(t, k, slot):
            px, py = peers[k]
            if t < ns:
                src, dst = ins[t].at[half(t, c)], outs[t].at[(slot,) + half(t, c)]
            else:
                src, dst = ins[t], outs[t].at[slot]
            return pltpu.make_async_remote_copy(src_ref=src, dst_ref=dst, send_sem=ici_send.at[t, k],
                                                recv_sem=ici_recv.at[t, k], device_id=(px, py, c), device_id_type=MESH)

        def d2d(t, k, which):
            rows = outs[t].at[(_chip_index(*peers[k]),) + half(t, which)]
            return pltpu.make_async_remote_copy(src_ref=rows, dst_ref=rows, send_sem=d2d_send.at[t, k],
                                                recv_sem=d2d_recv.at[t, k], device_id=sibling, device_id_type=MESH)

        sends = []
        for t in range(n):
            for k in range(3):
                cp = ici(t, k, me)
                cp.start()
                sends.append(cp)
        for t in range(n):
            for k in range(3):
                ici(t, k, _chip_index(*peers[k])).wait_recv()
                if t < ns:
                    cp = d2d(t, k, c)
                    cp.start()
                    sends.append(cp)
        for t in range(ns):
            for k in range(3):
                d2d(t, k, 1 - c).wait_recv()
        for cp in sends:
            cp.wait_send()
        for cp in copies:
            cp.wait()

    arrays = list(split) + list(whole)
    return pl.pallas_call(
        body, name="gather_weights",
        in_specs=[HBM] * n, out_specs=[HBM] * n,
        out_shape=[jax.ShapeDtypeStruct((N_CHIPS,) + a.shape, a.dtype) for a in arrays],
        scratch_shapes=[pltpu.SemaphoreType.DMA((n, 3)), pltpu.SemaphoreType.DMA((n, 3)),
                        pltpu.SemaphoreType.DMA((ns, 3)), pltpu.SemaphoreType.DMA((ns, 3)),
                        pltpu.SemaphoreType.DMA((n,))],
    )(*arrays)


SEM = pl.BlockSpec(memory_space=pltpu.SEMAPHORE)
DATAFLOW = pltpu.SideEffectType.DATAFLOW_SIDE_EFFECTING


def _gather_start(shards, after):
    n = len(shards)

    def body(*refs):
        ins, lands = refs[:n], refs[n:2 * n]
        send_sems, recv_sems = refs[2 * n + 1], refs[2 * n + 2]
        token = refs[-1]
        x, y, c = lax.axis_index("x"), lax.axis_index("y"), lax.axis_index("c")
        me = _chip_index(x, y)
        for t in range(n):
            for k, (px, py) in enumerate([(1 - x, y), (x, 1 - y), (1 - x, 1 - y)]):
                pltpu.make_async_remote_copy(
                    src_ref=ins[t], dst_ref=lands[t].at[me], send_sem=send_sems.at[3 * t + k],
                    recv_sem=recv_sems.at[3 * t + k],
                    device_id=(px, py, c), device_id_type=MESH).start()
        token[...] = jnp.zeros_like(token)

    zones = [lax.empty((N_CHIPS,) + a.shape, a.dtype) for a in shards]
    res = pl.pallas_call(
        body, name="gather_rest_start",
        out_shape=(pltpu.SemaphoreType.DMA((3 * n,)), pltpu.SemaphoreType.DMA((3 * n,)),
                   *[pltpu.HBM(a.shape, a.dtype) for a in shards], *[pltpu.HBM(z.shape, z.dtype) for z in zones],
                   jax.ShapeDtypeStruct((8, 128), F32)),
        in_specs=[HBM] * (2 * n) + [pl.BlockSpec(memory_space=pl.ANY)],
        out_specs=(SEM, SEM, *[HBM] * (2 * n), pl.BlockSpec(memory_space=pltpu.VMEM)),
        input_output_aliases={t: 2 + t for t in range(2 * n)},
        compiler_params=pltpu.CompilerParams(has_side_effects=DATAFLOW),
    )(*[pltpu.with_memory_space_constraint(a, pltpu.HBM) for a in shards],
      *[pltpu.with_memory_space_constraint(z, pltpu.HBM) for z in zones], after)
    return res[0], res[1], list(res[2:2 + n]), list(res[2 + n:2 + 2 * n]), res[-1]


def _gather_wait(send_sems, recv_sems, thru, lands, after):
    n = len(thru)

    def body(*refs):
        ins, zones = refs[:n], refs[n:2 * n]
        s_sems, r_sems = refs[2 * n], refs[2 * n + 1]
        x, y, c = lax.axis_index("x"), lax.axis_index("y"), lax.axis_index("c")
        for t in range(n):
            for k, (px, py) in enumerate([(1 - x, y), (x, 1 - y), (1 - x, 1 - y)]):
                cp = pltpu.make_async_remote_copy(
                    src_ref=ins[t], dst_ref=zones[t].at[_chip_index(px, py)], send_sem=s_sems.at[3 * t + k],
                    recv_sem=r_sems.at[3 * t + k], device_id=(px, py, c), device_id_type=MESH)
                cp.wait_send()
                cp.wait_recv()

    res = pl.pallas_call(
        body, name="gather_rest_wait",
        out_shape=(*[pltpu.HBM(a.shape, a.dtype) for a in thru], *[pltpu.HBM(z.shape, z.dtype) for z in lands]),
        in_specs=[HBM] * (2 * n) + [SEM, SEM, pl.BlockSpec(memory_space=pl.ANY)], out_specs=[HBM] * (2 * n),
        input_output_aliases={t: t for t in range(2 * n)},
        compiler_params=pltpu.CompilerParams(has_side_effects=DATAFLOW),
    )(*thru, *lands, send_sems, recv_sems, after)
    return list(res[:n]), list(res[n:])


def _half_shape(shape2d, axis):
    r, c = shape2d
    return (r, c // 2) if axis else (r // 2, c)


def _swap_halves(slotted, axes, small):
    n = len(slotted)

    def body(*refs):
        ins, sm_in = refs[:n], refs[n]
        outs, sm_out = refs[n + 1:2 * n + 1], refs[2 * n + 1]
        send_sems, recv_sems, local_sem, sm_send, sm_recv = refs[2 * n + 2:]
        x, y, c = lax.axis_index("x"), lax.axis_index("y"), lax.axis_index("c")
        dev = 4 * x + 2 * y + c
        local = pltpu.make_async_copy(sm_in, sm_out.at[dev], local_sem)
        local.start()
        sends = []
        for t in range(n):
            other = (slice(None),) + _half(slotted[t].shape[1:], axes[t], 1 - c)
            cp = pltpu.make_async_remote_copy(
                src_ref=ins[t].at[other], dst_ref=outs[t], send_sem=send_sems.at[t],
                recv_sem=recv_sems.at[t], device_id=(x, y, 1 - c), device_id_type=MESH)
            cp.start()
            sends.append(cp)
        rel = [(fx, fy, fc) for fx in (0, 1) for fy in (0, 1) for fc in (0, 1)][1:]
        for k, (fx, fy, fc) in enumerate(rel):
            cp = pltpu.make_async_remote_copy(
                src_ref=sm_in, dst_ref=sm_out.at[dev], send_sem=sm_send.at[k], recv_sem=sm_recv.at[k],
                device_id=(x ^ fx, y ^ fy, c ^ fc), device_id_type=MESH)
            cp.start()
            sends.append(cp)
        for cp in sends[:n]:
            cp.wait_recv()
        for k, (fx, fy, fc) in enumerate(rel):
            src_dev = 4 * (x ^ fx) + 2 * (y ^ fy) + (c ^ fc)
            pltpu.make_async_remote_copy(
                src_ref=sm_in, dst_ref=sm_out.at[src_dev], send_sem=sm_send.at[k], recv_sem=sm_recv.at[k],
                device_id=(x ^ fx, y ^ fy, c ^ fc), device_id_type=MESH).wait_recv()
        for cp in sends:
            cp.wait_send()
        local.wait()

    return pl.pallas_call(
        body, name="swap_halves",
        in_specs=[HBM] * (n + 1), out_specs=[HBM] * (n + 1),
        out_shape=[jax.ShapeDtypeStruct((a.shape[0],) + _half_shape(a.shape[1:], ax), a.dtype)
                   for a, ax in zip(slotted, axes, strict=True)]
        + [jax.ShapeDtypeStruct((N_DEV,) + small.shape, small.dtype)],
        scratch_shapes=[pltpu.SemaphoreType.DMA((n,)), pltpu.SemaphoreType.DMA((n,)), pltpu.SemaphoreType.DMA,
                        pltpu.SemaphoreType.DMA((7,)), pltpu.SemaphoreType.DMA((7,))],
    )(*slotted, small)


def _scatter_halves(parts):
    n = len(parts)

    def body(*refs):
        ins, outs = refs[:n], refs[n:2 * n]
        send_sems, recv_sems = refs[2 * n:]
        x, y, c = lax.axis_index("x"), lax.axis_index("y"), lax.axis_index("c")
        peers = [(1 - x, y), (x, 1 - y), (1 - x, 1 - y)]
        sends = []
        for t in range(n):
            for k, (px, py) in enumerate(peers):
                cp = pltpu.make_async_remote_copy(
                    src_ref=ins[t].at[_chip_index(px, py)], dst_ref=outs[t].at[k], send_sem=send_sems.at[t, k],
                    recv_sem=recv_sems.at[t, k], device_id=(px, py, c), device_id_type=MESH)
                cp.start()
                sends.append(cp)
        for cp in sends:
            cp.wait_recv()
        for cp in sends:
            cp.wait_send()

    return pl.pallas_call(
        body, name="scatter_halves",
        in_specs=[HBM] * n, out_specs=[HBM] * n,
        out_shape=[jax.ShapeDtypeStruct((3,) + a.shape[1:], a.dtype) for a in parts],
        scratch_shapes=[pltpu.SemaphoreType.DMA((n, 3)), pltpu.SemaphoreType.DMA((n, 3))],
    )(*parts)


def _tiling(r, c, f32_per_elem):
    if r % 8 == 0:
        tr = _row_tile(r, f32_per_elem * c * 4)
        return r // tr, (tr, c), lambda i: (i, 0)
    assert c % 128 == 0, (r, c)
    return c // 128, (r, 128), lambda i: (0, i)


def _pair_sum(name, slotted, other, idx, axis):
    _, r, c = slotted.shape
    hr, hc = _half_shape((r, c), axis)
    n, (tr, tc), at = _tiling(hr, hc, 13)
    if axis == 0:
        a = slotted.reshape(N_CHIPS, 2, hr, c)
        a_all = pl.BlockSpec((N_CHIPS, 1, tr, tc), lambda i, s: (0, s[0]) + at(i))
        a_own = pl.BlockSpec((1, 1, tr, tc), lambda i, s: (s[1], s[0]) + at(i))
    else:
        a, per_half = slotted, hc // tc
        a_all = pl.BlockSpec((N_CHIPS, tr, tc), lambda i, s: (0, at(i)[0], s[0] * per_half + at(i)[1]))
        a_own = pl.BlockSpec((1, tr, tc), lambda i, s: (s[1], at(i)[0], s[0] * per_half + at(i)[1]))

    def body(idx_ref, a_ref, b_ref, am_ref, bm_ref, p_ref, own_ref):
        mine, mine_own = (a_ref[:, 0], am_ref[0, 0]) if axis == 0 else (a_ref[...], am_ref[0])
        p_ref[...] = (mine + b_ref[...]).astype(p_ref.dtype)
        own_ref[...] = mine_own + bm_ref[0]

    grid_spec = pltpu.PrefetchScalarGridSpec(
        num_scalar_prefetch=1, grid=(n,),
        in_specs=[a_all, pl.BlockSpec((N_CHIPS, tr, tc), lambda i, s: (0,) + at(i)),
                  a_own, pl.BlockSpec((1, tr, tc), lambda i, s: (s[1],) + at(i))],
        out_specs=[pl.BlockSpec((N_CHIPS, tr, tc), lambda i, s: (0,) + at(i)),
                   pl.BlockSpec((tr, tc), lambda i, s: at(i))])
    return pl.pallas_call(
        body, name=name, grid_spec=grid_spec,
        out_shape=[jax.ShapeDtypeStruct((N_CHIPS, hr, hc), BF16), jax.ShapeDtypeStruct((hr, hc), F32)],
        compiler_params=_cparams(("parallel",)),
    )(idx, a, other, a, other)


def _sum_parts(name, own, recv):
    h, c = own.shape
    n, (tr, tc), at = _tiling(h, c, 4)

    def body(o_ref, r_ref, out_ref):
        acc = o_ref[...]
        for k in range(3):
            acc = acc + r_ref[k].astype(F32)
        out_ref[...] = acc

    return pl.pallas_call(
        body, name=name, grid=(n,),
        in_specs=[pl.BlockSpec((tr, tc), at), pl.BlockSpec((3, tr, tc), lambda i: (0,) + at(i))],
        out_specs=pl.BlockSpec((tr, tc), at),
        out_shape=jax.ShapeDtypeStruct((h, c), F32),
        compiler_params=_cparams(("parallel",)),
    )(own, recv)


def _swap_sibling(parts):
    n = len(parts)

    def body(*refs):
        ins, outs = refs[:n], refs[n:2 * n]
        send_sems, recv_sems = refs[2 * n:]
        x, y, c = lax.axis_index("x"), lax.axis_index("y"), lax.axis_index("c")
        cps = []
        for t in range(n):
            cp = pltpu.make_async_remote_copy(
                src_ref=ins[t], dst_ref=outs[t], send_sem=send_sems.at[t], recv_sem=recv_sems.at[t],
                device_id=(x, y, 1 - c), device_id_type=MESH)
            cp.start()
            cps.append(cp)
        for cp in cps:
            cp.wait_recv()
        for cp in cps:
            cp.wait_send()

    return pl.pallas_call(
        body, name="swap_sibling",
        in_specs=[HBM] * n, out_specs=[HBM] * n,
        out_shape=[jax.ShapeDtypeStruct(a.shape, a.dtype) for a in parts],
        scratch_shapes=[pltpu.SemaphoreType.DMA((n,)), pltpu.SemaphoreType.DMA((n,))],
    )(*parts)


def _sum_slots(name, a):
    k, r, c = a.shape
    tr = _row_tile(r, (k + 1) * c * 4)

    def body(a_ref, o_ref):
        acc = a_ref[0]
        for i in range(1, k):
            acc = acc + a_ref[i]
        o_ref[...] = acc

    return pl.pallas_call(
        body, name=name, grid=(r // tr,),
        in_specs=[pl.BlockSpec((k, tr, c), lambda i: (0, i, 0))],
        out_specs=pl.BlockSpec((tr, c), lambda i: (i, 0)),
        out_shape=jax.ShapeDtypeStruct((r, c), a.dtype),
        compiler_params=_cparams(("parallel",)),
    )(a)


def _adamw(name, w, m, v, g_parts):
    r, c = w.shape
    ng = len(g_parts)
    tr = _row_tile(r, (7 + ng) * c * 4)
    c1 = 1.0 - ADAM_B1 ** ADAM_STEP
    c2 = 1.0 - ADAM_B2 ** ADAM_STEP

    def body(*refs):
        w_ref, m_ref, v_ref = refs[:3]
        g_refs = refs[3:3 + ng]
        go_ref, d_ref, mo_ref, vo_ref = refs[3 + ng:]
        g = g_refs[0][...]
        for gr in g_refs[1:]:
            g = g + gr[...]
        m_new = ADAM_B1 * m_ref[...] + (1.0 - ADAM_B1) * g
        v_new = ADAM_B2 * v_ref[...] + (1.0 - ADAM_B2) * (g * g)
        go_ref[...] = g
        mo_ref[...] = m_new
        vo_ref[...] = v_new
        d_ref[...] = -ADAM_LR * ((m_new / c1) / (jnp.sqrt(v_new / c2) + ADAM_EPS) + ADAM_WD * w_ref[...])

    spec = pl.BlockSpec((tr, c), lambda i: (i, 0))
    return pl.pallas_call(
        body, name=name, grid=(r // tr,),
        in_specs=[spec] * (3 + ng), out_specs=[spec] * 4,
        out_shape=[jax.ShapeDtypeStruct((r, c), F32)] * 4,
        compiler_params=_cparams(("parallel",)),
    )(w, m, v, *g_parts)


def _adamw_halves(name, w, m, v, own, other, idx, axis):
    hr, hc = own.shape
    nb, (tr, tc), at = _tiling(hr, hc, 9)
    c1 = 1.0 - ADAM_B1 ** ADAM_STEP
    c2 = 1.0 - ADAM_B2 ** ADAM_STEP

    def body(idx_ref, w_ref, m_ref, v_ref, own_ref, oth_ref, go_ref, d_ref, mo_ref, vo_ref):
        g = jnp.where(pl.program_id(0) == idx_ref[0], own_ref[...], oth_ref[...])
        m_new = ADAM_B1 * m_ref[...] + (1.0 - ADAM_B1) * g
        v_new = ADAM_B2 * v_ref[...] + (1.0 - ADAM_B2) * (g * g)
        go_ref[...] = g
        mo_ref[...] = m_new
        vo_ref[...] = v_new
        d_ref[...] = -ADAM_LR * ((m_new / c1) / (jnp.sqrt(v_new / c2) + ADAM_EPS) + ADAM_WD * w_ref[...])

    per_half = (hc // tc) if axis else (hr // tr)
    if axis:
        full = pl.BlockSpec((tr, tc), lambda hh, i, s: (at(i)[0], hh * per_half + at(i)[1]))
    else:
        full = pl.BlockSpec((tr, tc), lambda hh, i, s: (hh * per_half + at(i)[0], at(i)[1]))
    part = pl.BlockSpec((tr, tc), lambda hh, i, s: at(i))
    grid_spec = pltpu.PrefetchScalarGridSpec(
        num_scalar_prefetch=1, grid=(2, nb), in_specs=[full, full, full, part, part], out_specs=[full] * 4)
    return pl.pallas_call(
        body, name=name, grid_spec=grid_spec, out_shape=[jax.ShapeDtypeStruct(w.shape, F32)] * 4,
        compiler_params=_cparams(("parallel", "parallel")),
    )(idx, w, m, v, own, other)


WEIGHTS = ['norm_w', 'w_in', 's5_a_re', 's5_a_im', 's5_b_re', 's5_b_im', 's5_c_re', 's5_c_im', 's5_d', 's5_log_step',
           's5_w_glu', 's5_b_glu', 'ssd_conv_w', 'ssd_conv_b', 'ssd_dt_bias', 'ssd_a_log', 'ssd_d', 'ssd_norm_w',
           'w_br_s5', 'w_br_ssd', 'w_out', 'ple_norm_w', 'w_ple_gate', 'w_ple_proj', 'final_norm_w']
SHARDED = {'w_in': ((IN_PROJ_DIM, 1024), 0), 's5_w_glu': ((512, 512), 0), 'ssd_conv_w': ((SSD_CONV, SSD_CONV_DIM), 1),
           'w_br_s5': ((512, 1024), 1), 'w_br_ssd': ((1536, 1024), 0), 'w_out': ((1024, 1024), 0),
           'w_ple_gate': ((1024, 1024), 0), 'w_ple_proj': ((256, 1024), 1)}
TRANSPOSED = ('w_in',)
SMALL = [n for n in WEIGHTS if n not in SHARDED]


def _shard_shape(name):
    (r, c), ax = SHARDED[name]
    return (r // N_CHIPS, c) if ax == 0 else (r, c // N_CHIPS)


def _half_axis(name):
    return 0 if (_shard_shape(name)[0] // 2) % 16 == 0 else 1


def _shard2d(name, a):
    r, c = _shard_shape(name)
    return a.reshape(c, r).T if name in TRANSPOSED else a.reshape(r, c)


def _unshard2d(name, a2, shape):
    return (a2.T if name in TRANSPOSED else a2).reshape(shape)


def _unslot(name, a4):
    (r, c), ax = SHARDED[name]
    if ax == 0:
        return a4.reshape(r, c)
    return jnp.transpose(a4, (1, 0, 2)).reshape(r, c)


def _slot(name, full):
    (r, c), ax = SHARDED[name]
    if ax == 0:
        return full.reshape(N_CHIPS, r // N_CHIPS, c)
    return jnp.transpose(full.reshape(r, N_CHIPS, c // N_CHIPS), (1, 0, 2))


GHP = ('s5_b_re', 's5_b_im')


def _view_shape(name):
    if name in ('s5_a_re', 's5_a_im'):
        return (S5_GROUPS, S5_STATE)
    if name in GHP + ('s5_c_re', 's5_c_im'):
        return (S5_GROUPS, S5_GROUP, S5_STATE)
    if name == 'ssd_conv_w':
        return (SSD_CONV, SSD_CONV_DIM // N_CHIPS)
    return (1, {'s5_log_step': S5_GROUPS, 'ssd_conv_b': SSD_CONV_DIM, 'ssd_norm_w': SSD_WIDTH, 's5_d': S5_WIDTH,
                's5_b_glu': S5_WIDTH, 'ssd_dt_bias': SSD_HEADS, 'ssd_a_log': SSD_HEADS, 'ssd_d': SSD_HEADS}.get(name, D_MODEL))


def _view(name, a):
    if name in GHP:
        return jnp.swapaxes(a.reshape(S5_GROUPS, S5_STATE, S5_GROUP), 1, 2)
    return a.reshape(_view_shape(name))


def _unview(name, a, shape):
    return (jnp.swapaxes(a, 1, 2) if name in GHP else a).reshape(shape)


def _adamw_small(ws, ms, vs, gs):
    n = len(ws)
    c1 = 1.0 - ADAM_B1 ** ADAM_STEP
    c2 = 1.0 - ADAM_B2 ** ADAM_STEP

    def body(*refs):
        w_r, m_r, v_r, g_r = (refs[k * n:(k + 1) * n] for k in range(4))
        d_o, m_o, v_o = (refs[k * n:(k + 1) * n] for k in range(4, 7))
        for i in range(n):
            g = g_r[i][...]
            m_new = ADAM_B1 * m_r[i][...] + (1.0 - ADAM_B1) * g
            v_new = ADAM_B2 * v_r[i][...] + (1.0 - ADAM_B2) * (g * g)
            m_o[i][...] = m_new
            v_o[i][...] = v_new
            d_o[i][...] = -ADAM_LR * ((m_new / c1) / (jnp.sqrt(v_new / c2) + ADAM_EPS) + ADAM_WD * w_r[i][...])

    return pl.pallas_call(
        body, name="adamw_small", out_shape=[jax.ShapeDtypeStruct(w.shape, F32) for w in ws] * 3,
        compiler_params=pltpu.CompilerParams(vmem_limit_bytes=VMEM_LIMIT),
    )(*ws, *ms, *vs, *gs)


def _pack_small(vals):
    flat = jnp.concatenate([v.reshape(-1).astype(F32) for v in vals])
    rows = -(-flat.shape[0] // (256 * 128)) * 256
    return jnp.pad(flat, (0, rows * 128 - flat.shape[0])).reshape(rows, 128)


def _unpack_small(packed, shapes):
    flat = packed.reshape(-1)
    out, off = [], 0
    for sh in shapes:
        n = math.prod(sh)
        out.append(flat[off:off + n].reshape(sh))
        off += n
    return out


def kernel(x, p, norm_w, w_in, s5_a_re, s5_a_im, s5_b_re, s5_b_im, s5_c_re, s5_c_im, s5_d, s5_log_step, s5_w_glu, s5_b_glu, ssd_conv_w, ssd_conv_b, ssd_dt_bias, ssd_a_log, ssd_d, ssd_norm_w, w_br_s5, w_br_ssd, w_out, ple_norm_w, w_ple_gate, w_ple_proj, final_norm_w, loss_target, m_norm_w, m_w_in, m_s5_a_re, m_s5_a_im, m_s5_b_re, m_s5_b_im, m_s5_c_re, m_s5_c_im, m_s5_d, m_s5_log_step, m_s5_w_glu, m_s5_b_glu, m_ssd_conv_w, m_ssd_conv_b, m_ssd_dt_bias, m_ssd_a_log, m_ssd_d, m_ssd_norm_w, m_w_br_s5, m_w_br_ssd, m_w_out, m_ple_norm_w, m_w_ple_gate, m_w_ple_proj, m_final_norm_w, v_norm_w, v_w_in, v_s5_a_re, v_s5_a_im, v_s5_b_re, v_s5_b_im, v_s5_c_re, v_s5_c_im, v_s5_d, v_s5_log_step, v_s5_w_glu, v_s5_b_glu, v_ssd_conv_w, v_ssd_conv_b, v_ssd_dt_bias, v_ssd_a_log, v_ssd_d, v_ssd_norm_w, v_w_br_s5, v_w_br_ssd, v_w_out, v_ple_norm_w, v_w_ple_gate, v_w_ple_proj, v_final_norm_w):
    args = locals()
    wl = {n: args[n] for n in WEIGHTS}
    ml = {n: args["m_" + n] for n in WEIGHTS}
    vl = {n: args["v_" + n] for n in WEIGHTS}
    big = [n for n in SHARDED if n != 'ssd_conv_w']
    chip = _chip_index(lax.axis_index("x"), lax.axis_index("y"))
    idx = jnp.stack([lax.axis_index("c"), chip]).astype(jnp.int32)

    axes = [_half_axis(n) for n in big]
    first = ['w_in']
    rest = [n for n in big if n not in first]
    bf_shards = {n: _shard2d(n, wl[n]).astype(BF16) for n in big}
    w_in_t, conv_w4 = _gather_chips([bf_shards[n] for n in first], [_half_axis(n) for n in first],
                                    [_shard2d('ssd_conv_w', wl['ssd_conv_w'])])
    full = {n: wl[n] for n in SMALL}
    full["w_in_t"] = w_in_t
    full['ssd_conv_w'] = _unslot('ssd_conv_w', conv_w4)
    send_sems, recv_sems, thru, lands, token = _gather_start([bf_shards[n] for n in rest], w_in_t)

    def fetch_rest(after):
        own, zones = _gather_wait(send_sems, recv_sems, thru, lands, after)
        return {n: _unslot(n, lax.dynamic_update_slice(z, o[None], (chip, 0, 0)))
                for n, o, z in zip(rest, own, zones, strict=True)}

    full["_late"] = (token, fetch_rest)
    loss, grad_x, g = _local_step(x, p[0], loss_target, full)
    for n in TRANSPOSED:
        g[n] = g.pop(n + "_t")

    for n in ('s5_b_re', 's5_b_im'):
        g[n] = g.pop(n + "_ghp")
    small_shapes = [(1, 1)] + [_view_shape(n) for n in SMALL] + [(SSD_CONV, SSD_CONV_DIM)]
    small_pack = _pack_small([loss] + [g[n] for n in SMALL] + [g['ssd_conv_w']])
    slotted = [g[n] if n in TRANSPOSED else _slot(n, g[n]) for n in big]
    swapped = _swap_halves(slotted, axes, small_pack)
    pair = [_pair_sum("pair_sum_" + n, a, b, idx, ax)
            for n, a, b, ax in zip(big, slotted, swapped[:-1], axes, strict=True)]
    small_sum = _sum_slots("sum_small", swapped[-1])
    recv = _scatter_halves([pb for pb, _ in pair])
    halves = [_sum_parts("sum_chips_" + n, own, r) for n, (_, own), r in zip(big, pair, recv, strict=True)]
    other_halves = _swap_sibling(halves)

    out_g, out_d, out_m, out_v = {}, {}, {}, {}
    for n, own, oth, ax in zip(big, halves, other_halves, axes, strict=True):
        res = _adamw_halves("adamw_" + n, _shard2d(n, wl[n]), _shard2d(n, ml[n]), _shard2d(n, vl[n]), own, oth, idx, ax)
        out_g[n], out_d[n], out_m[n], out_v[n] = (_unshard2d(n, r, wl[n].shape) for r in res)
    sm = _unpack_small(small_sum, small_shapes)
    loss_total = sm[0].reshape(())
    conv_g = lax.dynamic_slice(sm[-1], (0, chip * (SSD_CONV_DIM // N_CHIPS)), (SSD_CONV, SSD_CONV_DIM // N_CHIPS))
    names = SMALL + ['ssd_conv_w']
    grads = sm[1:-1] + [conv_g]
    res = _adamw_small([_view(n, wl[n]) for n in names], [_view(n, ml[n]) for n in names],
                       [_view(n, vl[n]) for n in names], grads)
    for i, n in enumerate(names):
        out_g[n] = _unview(n, grads[i], wl[n].shape)
        out_d[n], out_m[n], out_v[n] = (_unview(n, res[k * len(names) + i], wl[n].shape) for k in range(3))

    return (loss_total, grad_x, *[out_g[n] for n in WEIGHTS], *[out_d[n] for n in WEIGHTS],
            *[out_m[n] for n in WEIGHTS], *[out_v[n] for n in WEIGHTS])
```

```python
import functools
import math

import jax
import jax.numpy as jnp
from jax import lax
from jax.experimental import pallas as pl
from jax.experimental.pallas import tpu as pltpu

F32 = jnp.float32
BF16 = jnp.bfloat16
MESH = pl.DeviceIdType.MESH

D_MODEL = 1024
PLE_DIM = 256
RMS_EPS = 1e-6
S5_WIDTH = 512
S5_GROUP = 16
S5_GROUPS = 32
S5_STATE = 64
S5_N = S5_GROUPS * S5_STATE
S5_LB = 512
S5_NJ = S5_N // S5_LB
S5_TB = 256
S5_LOG_TB = 8
SSD_WIDTH = 1536
SSD_HEADDIM = 64
SSD_HEADS = 24
SSD_GROUPS = 4
SSD_HPG = 6
SSD_STATE = 128
SSD_CONV = 4
SSD_CHUNK = 128
SSD_BC = 512
SSD_CONV_DIM = 2560
GROUP_W = SSD_WIDTH // SSD_GROUPS
N_CHIPS = 4
N_DEV = 8

OFF_XBC, OFF_U5, OFF_Z5, OFF_DT, OFF_G5, OFF_GS, OFF_ZS = 0, 2560, 3072, 3584, 4096, 5120, 6144
DT_W = 512
PROJ_W = 7680
IN_PROJ_DIM = 7192

ADAM_LR, ADAM_B1, ADAM_B2, ADAM_EPS, ADAM_WD, ADAM_STEP = 0.001, 0.9, 0.999, 1e-08, 0.01, 10

VMEM_LIMIT = 56 * 1024 * 1024


def _pick(n, cands):
    for c in cands:
        if n % c == 0:
            return c
    return n


ROW_BLOCK_BYTES = 8 * 1024 * 1024


def _row_tile(r, bytes_per_row):
    for t in (r, 4096, 2048, 1024, 512, 256, 128, 64, 32, 16, 8):
        if t <= r and r % t == 0 and t * bytes_per_row <= ROW_BLOCK_BYTES:
            return t
    return r


def _cparams(sem):
    return pltpu.CompilerParams(dimension_semantics=sem, vmem_limit_bytes=VMEM_LIMIT)


def _dg(a, b, ca, cb):
    return lax.dot_general(a.astype(BF16), b.astype(BF16), (((ca,), (cb,)), ((), ())), preferred_element_type=F32)


@jax.custom_vjp
def dot_nn(a, b):
    return _dg(a, b, 1, 0)


@jax.custom_vjp
def dot_nt(a, b):
    return _dg(a, b, 1, 1)


@jax.custom_vjp
def dot_tn(a, b):
    return _dg(a, b, 0, 0)


dot_nn.defvjp(lambda a, b: (_dg(a, b, 1, 0), (a, b)), lambda r, g: (_dg(g, r[1], 1, 1), _dg(r[0], g, 0, 0)))
dot_nt.defvjp(lambda a, b: (_dg(a, b, 1, 1), (a, b)), lambda r, g: (_dg(g, r[1], 1, 0), _dg(g, r[0], 0, 0)))
dot_tn.defvjp(lambda a, b: (_dg(a, b, 0, 0), (a, b)), lambda r, g: (_dg(r[1], g, 1, 1), _dg(r[0], g, 1, 0)))


MM_VMEM_BUDGET = 30 * 1024 * 1024


def _mm_tiles(m, n, k, sa, sb, so, tn_only=None):
    best, best_key = None, None
    for tm in (1024, 512, 256, 128, 64, 32, 16, 8):
        if m % tm:
            continue
        for tn in (2048, 1536, 1280, 1024, 768, 640, 512, 384, 256, 128):
            if n % tn or (tn_only is not None and tn not in tn_only):
                continue
            for tk in (k, 2048, 1536, 1280, 1024, 768, 512, 256, 128):
                if k % tk or tk > max(k, 128) or (tk == k and k > 2048 and k % 128 == 0):
                    continue
                need = 2 * (tm * tk * sa + tk * tn * sb + tm * tn * so) + (tm * tn * 4 if tk < k else 0)
                if need > MM_VMEM_BUDGET:
                    continue
                key = (tm * tn * tk, tk)
                if best_key is None or key > best_key:
                    best, best_key = (tm, tn, tk), key
    assert best is not None, (m, n, k)
    return best


def _matmul(name, a, b, *, ta=False, tb=False, a_win=None, out_dtype=F32, epilogue=None, epi_rows=(), epi_pars=(),
            epi_outs=(), full_rows=False, epi_accs=(), epi_into=None):
    a_off, a_w = a_win if a_win is not None else (0, a.shape[1])
    if ta:
        kdim, m = a.shape[0], a_w
    else:
        m, kdim = a.shape[0], a_w
    n = b.shape[0] if tb else b.shape[1]
    assert (b.shape[1] if tb else b.shape[0]) == kdim, (name, a.shape, b.shape)
    out_dtypes = list(epi_outs) if epilogue is not None else [out_dtype]
    so = sum(jnp.dtype(d).itemsize for d in out_dtypes) + sum(r.dtype.itemsize for r, _ in epi_rows)
    tn_ok = [n] if full_rows else [t for t in (1024, 512, 256, 128) if all(off % t == 0 for _, off in epi_rows)]
    tm, tn, tk = _mm_tiles(m, n, kdim, a.dtype.itemsize, b.dtype.itemsize, so, tn_ok if epilogue is not None else None)
    nk = kdim // tk
    n_er, n_ep, n_out = len(epi_rows), len(epi_pars), len(out_dtypes)
    if ta:
        assert a_off % tm == 0
        a_spec = pl.BlockSpec((tk, tm), lambda i, j, k: (k, i + a_off // tm))
    else:
        assert a_off % tk == 0
        a_spec = pl.BlockSpec((tm, tk), lambda i, j, k: (i, k + a_off // tk))
    if tb:
        b_spec = pl.BlockSpec((tn, tk), lambda i, j, k: (j, k))
    else:
        b_spec = pl.BlockSpec((tk, tn), lambda i, j, k: (k, j))
    ca, cb = (0 if ta else 1), (1 if tb else 0)

    n_acc = len(epi_accs)
    n_x = 1 if epi_into is not None else 0
    assert not (n_acc or n_x) or full_rows

    def body(a_ref, b_ref, *refs):
        er, ep = refs[:n_er], refs[n_er:n_er + n_ep]
        first_out = n_er + n_ep + n_x
        o_refs = refs[first_out:first_out + n_out]
        s_refs = refs[first_out + n_out:first_out + n_out + n_acc]
        acc = refs[first_out + n_out + n_acc:]

        def finish(c):
            outs = [c] if epilogue is None else epilogue(c, [r[...] for r in er], [p[...] for p in ep])
            if n_acc:
                outs, sums = outs

                @pl.when(pl.program_id(0) == 0)
                def _():
                    for s_ref in s_refs:
                        s_ref[...] = jnp.zeros_like(s_ref)

                for s_ref, s in zip(s_refs, sums, strict=True):
                    s_ref[...] += jnp.broadcast_to(s, s_ref.shape)
            for o_ref, o in zip(o_refs, outs, strict=True):
                o_ref[...] = o.astype(o_ref.dtype)

        if nk == 1:
            finish(_dg(a_ref[...], b_ref[...], ca, cb))
            return
        (acc_ref,) = acc
        k = pl.program_id(2)

        @pl.when(k == 0)
        def _():
            acc_ref[...] = jnp.zeros_like(acc_ref)

        acc_ref[...] += _dg(a_ref[...], b_ref[...], ca, cb)

        @pl.when(k == nk - 1)
        def _():
            finish(acc_ref[...])

    in_specs = [a_spec, b_spec]
    in_specs += [pl.BlockSpec((tm, tn), functools.partial(lambda i, j, k, c: (i, j + c), c=off // tn)) for _, off in epi_rows]
    in_specs += [pl.BlockSpec((p.shape[0], tn), lambda i, j, k: (0, j)) for p in epi_pars]
    out_specs = [pl.BlockSpec((tm, tn), lambda i, j, k: (i, j)) for _ in out_dtypes]
    out_shape = [jax.ShapeDtypeStruct((m, n), d) for d in out_dtypes]
    extra, aliases = [], {}
    if epi_into is not None:
        buf, off, width = epi_into
        assert off % width == 0 and buf.dtype == out_dtypes[0]
        in_specs.append(pl.BlockSpec(memory_space=pl.ANY))
        out_specs[0] = pl.BlockSpec((tm, width), functools.partial(lambda i, j, k, c: (i, c), c=off // width))
        out_shape[0] = jax.ShapeDtypeStruct(buf.shape, buf.dtype)
        extra, aliases = [buf], {2 + n_er + n_ep: 0}
    out_specs += [pl.BlockSpec((r, w), lambda i, j, k: (0, 0)) for r, w in epi_accs]
    out_shape += [jax.ShapeDtypeStruct((r, w), F32) for r, w in epi_accs]
    res = pl.pallas_call(
        body, name=name, grid=(m // tm, n // tn, nk),
        in_specs=in_specs, out_specs=out_specs, out_shape=out_shape, input_output_aliases=aliases,
        scratch_shapes=[pltpu.VMEM((tm, tn), F32)] if nk > 1 else [],
        compiler_params=_cparams(("arbitrary",) * 3 if n_acc else ("parallel", "parallel", "arbitrary")),
    )(a, b, *[r for r, _ in epi_rows], *epi_pars, *extra)
    return res if epilogue is not None else res[0]


BAND = 128


def _band_matmul(name, kind, a, b, *, a_blk0=0, out_dtype=F32, epilogue=None, epi_rows=(), epi_pars=(), epi_outs=(),
                 into=None):
    n_rows = a.shape[0]
    blk = 2 * S5_LB
    tm = _pick(n_rows, (1024, 512, 256))
    if epilogue is not None:
        assert kind == "nt"
        n_er, n_ep = len(epi_rows), len(epi_pars)
        in_specs = [pl.BlockSpec((tm, blk), lambda i, j: (i, j)), pl.BlockSpec((BAND, blk), lambda i, j: (j, 0))]
        in_specs += [pl.BlockSpec((tm, BAND), functools.partial(lambda i, j, c: (i, c + j), c=c0)) for _, c0 in epi_rows]
        in_specs += [pl.BlockSpec((1, BAND), lambda i, j: (0, j)) for _ in epi_pars]
        out_specs = [pl.BlockSpec((tm, BAND), lambda i, j: (i, j)) for _ in epi_outs]
        out_shape = [jax.ShapeDtypeStruct((n_rows, S5_NJ * BAND), d) for d in epi_outs]
        extra, aliases = [], {}
        if into is not None:
            buf, c0 = into
            in_specs.append(pl.BlockSpec(memory_space=pl.ANY))
            out_specs[0] = pl.BlockSpec((tm, BAND), functools.partial(lambda i, j, c: (i, c + j), c=c0))
            out_shape[0] = jax.ShapeDtypeStruct(buf.shape, buf.dtype)
            extra, aliases = [buf], {2 + n_er + n_ep: 0}

        def epi_body(a_ref, b_ref, *refs):
            outs = epilogue(_dg(a_ref[...], b_ref[...], 1, 1), [r[...] for r in refs[:n_er]],
                            [p[...] for p in refs[n_er:n_er + n_ep]])
            for o_ref, o in zip(refs[n_er + n_ep + len(extra):], outs, strict=True):
                o_ref[...] = o.astype(o_ref.dtype)

        return pl.pallas_call(
            epi_body, name=name, grid=(n_rows // tm, S5_NJ), in_specs=in_specs, out_specs=out_specs,
            out_shape=out_shape, input_output_aliases=aliases, compiler_params=_cparams(("parallel", "parallel")),
        )(a, b, *[r for r, _ in epi_rows], *epi_pars, *extra)
    if kind == "nn":
        grid = (n_rows // tm, S5_NJ)
        in_specs = [pl.BlockSpec((tm, BAND), lambda i, j: (i, a_blk0 + j)), pl.BlockSpec((BAND, blk), lambda i, j: (j, 0))]
        out_spec = pl.BlockSpec((tm, blk), lambda i, j: (i, j))
        out_shape = (n_rows, S5_NJ * blk)
        sem = ("parallel", "parallel")

        def body(a_ref, b_ref, o_ref):
            o_ref[...] = _dg(a_ref[...], b_ref[...], 1, 0).astype(o_ref.dtype)
    elif kind == "nt":
        grid = (n_rows // tm, S5_NJ)
        in_specs = [pl.BlockSpec((tm, blk), lambda i, j: (i, j)), pl.BlockSpec((BAND, blk), lambda i, j: (j, 0))]
        out_spec = pl.BlockSpec((tm, BAND), lambda i, j: (i, j))
        out_shape = (n_rows, S5_NJ * BAND)
        sem = ("parallel", "parallel")

        def body(a_ref, b_ref, o_ref):
            o_ref[...] = _dg(a_ref[...], b_ref[...], 1, 1).astype(o_ref.dtype)
    else:
        grid = (S5_NJ, n_rows // tm)
        in_specs = [pl.BlockSpec((tm, BAND), lambda j, k: (k, a_blk0 + j)), pl.BlockSpec((tm, blk), lambda j, k: (k, j))]
        out_spec = pl.BlockSpec((BAND, blk), lambda j, k: (j, 0))
        out_shape = (S5_NJ * BAND, blk)
        sem = ("parallel", "arbitrary")

        def body(a_ref, b_ref, o_ref):
            @pl.when(pl.program_id(1) == 0)
            def _():
                o_ref[...] = jnp.zeros_like(o_ref)

            o_ref[...] += _dg(a_ref[...], b_ref[...], 0, 0)

    return pl.pallas_call(
        body, name=name, grid=grid, in_specs=in_specs, out_specs=out_spec,
        out_shape=jax.ShapeDtypeStruct(out_shape, out_dtype), compiler_params=_cparams(sem),
    )(a, b)


def _rowwise(name, fn, n_rows, tr, row_ins, par_ins, row_outs, acc_outs=(), into=None):
    nr, npar, no, na = len(row_ins), len(par_ins), len(row_outs), len(acc_outs)
    in_specs = []
    for arr, off, w in row_ins:
        assert off % w == 0 and arr.shape[0] == n_rows, (name, arr.shape, off, w)
        in_specs.append(pl.BlockSpec((tr, w), functools.partial(lambda i, c: (i, c), c=off // w)))
    for arr, off, w in par_ins:
        assert off % w == 0
        in_specs.append(pl.BlockSpec((arr.shape[0], w), functools.partial(lambda i, c: (0, c), c=off // w)))
    out_specs = [pl.BlockSpec((tr, w), lambda i: (i, 0)) for w, _ in row_outs]
    out_specs += [pl.BlockSpec((r, w), lambda i: (0, 0)) for r, w in acc_outs]
    out_shape = [jax.ShapeDtypeStruct((n_rows, w), dt) for w, dt in row_outs]
    out_shape += [jax.ShapeDtypeStruct((r, w), F32) for r, w in acc_outs]
    extra, aliases = [], {}
    if into is not None:
        buf, off = into
        w0 = row_outs[0][0]
        assert off % w0 == 0 and buf.dtype == row_outs[0][1]
        in_specs.append(pl.BlockSpec(memory_space=pl.ANY))
        out_specs[0] = pl.BlockSpec((tr, w0), functools.partial(lambda i, c: (i, c), c=off // w0))
        out_shape[0] = jax.ShapeDtypeStruct(buf.shape, buf.dtype)
        extra, aliases = [buf], {nr + npar: 0}
    nx = len(extra)

    def body(*refs):
        rows = [r[...] for r in refs[:nr]]
        pars = [r[...] for r in refs[nr:nr + npar]]
        o_refs = refs[nr + npar + nx:nr + npar + nx + no]
        a_refs = refs[nr + npar + nx + no:]
        outs, accs = fn(rows, pars)
        for o_ref, o in zip(o_refs, outs, strict=True):
            o_ref[...] = o.astype(o_ref.dtype)
        if na:
            @pl.when(pl.program_id(0) == 0)
            def _():
                for a_ref in a_refs:
                    a_ref[...] = jnp.zeros_like(a_ref)

            for a_ref, a in zip(a_refs, accs, strict=True):
                a_ref[...] += jnp.broadcast_to(a, a_ref.shape)

    res = pl.pallas_call(
        body, name=name, grid=(n_rows // tr,),
        in_specs=in_specs, out_specs=out_specs, out_shape=out_shape, input_output_aliases=aliases,
        compiler_params=_cparams(("arbitrary",) if na else ("parallel",)),
    )(*[a for a, _, _ in row_ins], *[a for a, _, _ in par_ins], *extra)
    return res


def _rms(x, w):
    return x * lax.rsqrt(jnp.mean(x * x, axis=-1, keepdims=True) + RMS_EPS) * w


def _gated_norm(y, z, w):
    outs = []
    for g in range(SSD_GROUPS):
        sl = slice(g * GROUP_W, (g + 1) * GROUP_W)
        yg = y[:, sl] * jax.nn.silu(z[:, sl])
        outs.append(yg * lax.rsqrt(jnp.mean(yg * yg, axis=-1, keepdims=True) + RMS_EPS) * w[:, sl])
    return jnp.concatenate(outs, axis=-1)


def _s5_out(yc, u, z, t, d, bg):
    ge = jax.nn.gelu(yc + d * u)
    return ge * jax.nn.sigmoid(t + bg) * jax.nn.silu(z)


def _merge(g5, gs, m5, ms):
    return jax.nn.sigmoid(g5) * m5 + jax.nn.sigmoid(gs) * ms


def _head_loss(h1, pgl, pp, fw, tgt):
    h2 = h1 + jax.nn.sigmoid(pgl) * pp
    err = _rms(h2, fw) - tgt
    per_row = 0.5 * jnp.mean(err * err, axis=-1, keepdims=True)
    return jnp.sum(per_row, axis=0, keepdims=True)


def _s5_disc(a_re, a_im, log_step, b_re2, b_im2, expand):
    step = jnp.exp(log_step)
    mag = jnp.exp(a_re * step)
    lb_re = mag * jnp.cos(a_im * step)
    lb_im = mag * jnp.sin(a_im * step)
    den = a_re * a_re + a_im * a_im
    n_re = lb_re - 1.0
    f_re = (n_re * a_re + lb_im * a_im) / den
    f_im = (lb_im * a_re - n_re * a_im) / den
    hi = lax.Precision.HIGHEST
    fr = jnp.dot(expand, f_re, precision=hi, preferred_element_type=F32)
    fi = jnp.dot(expand, f_im, precision=hi, preferred_element_type=F32)
    return lb_re, lb_im, fr * b_re2 - fi * b_im2, fr * b_im2 + fi * b_re2


def _s5_params_fwd(a_re, a_im, log_step, b_re2, b_im2, expand):
    gp = a_re.shape

    def body(ar, ai, ls, br, bi, ex, pr_ref, pi_ref, bbr_ref, bbi_ref):
        lr, li, bbr, bbi = _s5_disc(ar[...], ai[...], ls[...], br[...], bi[...], ex[...])
        bbr_ref[...] = bbr
        bbi_ref[...] = bbi
        qr, qi = lr, li
        for k in range(S5_LOG_TB):
            pr_ref[k] = qr
            pi_ref[k] = qi
            qr, qi = qr * lr - qi * li, qr * li + qi * lr

    return pl.pallas_call(
        body, name="s5_params_fwd",
        out_shape=(jax.ShapeDtypeStruct((S5_LOG_TB,) + gp, F32), jax.ShapeDtypeStruct((S5_LOG_TB,) + gp, F32),
                   jax.ShapeDtypeStruct(b_re2.shape, F32), jax.ShapeDtypeStruct(b_re2.shape, F32)),
    )(a_re, a_im, log_step, b_re2, b_im2, expand)


def _s5_params_bwd(a_re, a_im, log_step, b_re2, b_im2, expand, d_lr, d_li, d_bbr, d_bbi):
    def body(ar, ai, ls, br, bi, ex, glr, gli, gbr, gbi, dar, dai, dls, dbr, dbi):
        _, vjp = jax.vjp(lambda *p: _s5_disc(*p, ex[...]), ar[...], ai[...], ls[...], br[...], bi[...])
        g = vjp((glr[...], gli[...], gbr[...], gbi[...]))
        for ref, val in zip((dar, dai, dls, dbr, dbi), g, strict=True):
            ref[...] = val

    return pl.pallas_call(
        body, name="s5_params_bwd",
        out_shape=tuple(jax.ShapeDtypeStruct(v.shape, F32) for v in (a_re, a_im, log_step, b_re2, b_im2)),
    )(a_re, a_im, log_step, b_re2, b_im2, expand, d_lr, d_li, d_bbr, d_bbi)


def _scan_block(x_ref, out_ref, lp, edge_pow, cr, ci, reverse, each=None):
    n_g = x_ref.shape[0] // 8
    sub = lax.broadcasted_iota(jnp.int32, (8, S5_LB), 0)
    steps = []
    for sh in (1, 2, 4):
        keep = (sub < 8 - sh) if reverse else (sub >= sh)
        steps.append((8 - sh if reverse else sh, jnp.where(keep, lp[sh - 1:sh, :S5_LB], 0.0),
                      jnp.where(keep, lp[sh - 1:sh, S5_LB:], 0.0)))
    e_r, e_i = edge_pow[:, :S5_LB], edge_pow[:, S5_LB:]
    for r in (range(n_g - 1, -1, -1) if reverse else range(n_g)):
        rows = slice(8 * r, 8 * r + 8)
        xr, xi = x_ref[rows, :S5_LB], x_ref[rows, S5_LB:]
        for by, a_r, a_i in steps:
            pr, pi = pltpu.roll(xr, by, 0), pltpu.roll(xi, by, 0)
            xr, xi = xr + a_r * pr - a_i * pi, xi + a_r * pi + a_i * pr
        xr = xr + e_r * cr - e_i * ci
        xi = xi + e_r * ci + e_i * cr
        out_ref[rows, :S5_LB] = xr
        out_ref[rows, S5_LB:] = xi
        if each is not None:
            each(r, xr, xi)
        cr, ci = (xr[0:1, :], xi[0:1, :]) if reverse else (xr[7:8, :], xi[7:8, :])
    return cr, ci


def _s5_scan_fwd(bu, lam_pow, n_seq, seq_len):
    n_t = seq_len // S5_TB
    blk = 2 * S5_LB

    def body(bu_ref, lp_ref, s_ref, cr, ci, buf):
        @pl.when(pl.program_id(2) == 0)
        def _():
            cr[...] = jnp.zeros_like(cr)
            ci[...] = jnp.zeros_like(ci)

        lp = lp_ref[...]
        cr[...], ci[...] = _scan_block(bu_ref, buf, lp, lp, cr[...], ci[...], False)
        s_ref[...] = buf[...].astype(s_ref.dtype)

    return pl.pallas_call(
        body, name="s5_scan_fwd", grid=(S5_NJ, n_seq, n_t),
        in_specs=[pl.BlockSpec((S5_TB, blk), lambda j, b, t: (b * n_t + t, j)),
                  pl.BlockSpec((S5_LOG_TB, blk), lambda j, b, t: (0, j))],
        out_specs=pl.BlockSpec((S5_TB, blk), lambda j, b, t: (b * n_t + t, j)),
        out_shape=jax.ShapeDtypeStruct(bu.shape, BF16),
        scratch_shapes=[pltpu.VMEM((1, S5_LB), F32), pltpu.VMEM((1, S5_LB), F32), pltpu.VMEM((S5_TB, blk), F32)],
        compiler_params=_cparams(("parallel", "parallel", "arbitrary")),
    )(bu, lam_pow)


def _s5_scan_bwd(ds, s, lam_pow_conj, n_seq, seq_len):
    n_t = seq_len // S5_TB
    blk = 2 * S5_LB
    halo_rows = 16
    halo_per_blk = S5_TB // halo_rows

    def rows(j, b, t):
        return (b * n_t + (n_t - 1 - t), j)

    def halo(j, b, t):
        return (jnp.maximum((b * n_t + (n_t - 1 - t)) * halo_per_blk - 1, 0), j)

    def body(ds_ref, sq_ref, hq_ref, lp_ref, g_ref, dl_ref, cr, ci, buf, s_ref):
        b, t = pl.program_id(1), pl.program_id(2)
        s_ref[...] = sq_ref[...].astype(F32)
        h_last = hq_ref[...].astype(F32)[halo_rows - 1:halo_rows, :]

        @pl.when(t == 0)
        def _():
            cr[...] = jnp.zeros_like(cr)
            ci[...] = jnp.zeros_like(ci)

        @pl.when((b == 0) & (t == 0))
        def _():
            dl_ref[...] = jnp.zeros_like(dl_ref)

        first_blk = t == n_t - 1
        sub = lax.broadcasted_iota(jnp.int32, (8, S5_LB), 0)
        acc = [jnp.zeros((8, S5_LB), F32), jnp.zeros((8, S5_LB), F32)]

        def each(r, gr, gi):
            rows = slice(8 * r, 8 * r + 8)
            if r == 0:
                before_r = jnp.where(first_blk, 0.0, h_last[:, :S5_LB])
                before_i = jnp.where(first_blk, 0.0, h_last[:, S5_LB:])
            else:
                before_r, before_i = s_ref[8 * r - 1:8 * r, :S5_LB], s_ref[8 * r - 1:8 * r, S5_LB:]
            sp_r = jnp.where(sub == 0, before_r, pltpu.roll(s_ref[rows, :S5_LB], 1, 0))
            sp_i = jnp.where(sub == 0, before_i, pltpu.roll(s_ref[rows, S5_LB:], 1, 0))
            acc[0] = acc[0] + gr * sp_r + gi * sp_i
            acc[1] = acc[1] + gi * sp_r - gr * sp_i

        lp = lp_ref[...]
        edge_pow = jnp.concatenate([lp[7 - i:8 - i, :] for i in range(8)], axis=0)
        cr[...], ci[...] = _scan_block(ds_ref, buf, lp, edge_pow, cr[...], ci[...], True, each)
        g_ref[...] = buf[...].astype(g_ref.dtype)
        dl_ref[:, :S5_LB] += jnp.sum(acc[0], axis=0, keepdims=True)
        dl_ref[:, S5_LB:] += jnp.sum(acc[1], axis=0, keepdims=True)

    return pl.pallas_call(
        body, name="s5_scan_bwd", grid=(S5_NJ, n_seq, n_t),
        in_specs=[pl.BlockSpec((S5_TB, blk), rows), pl.BlockSpec((S5_TB, blk), rows),
                  pl.BlockSpec((halo_rows, blk), halo), pl.BlockSpec((S5_LOG_TB, blk), lambda j, b, t: (0, j))],
        out_specs=[pl.BlockSpec((S5_TB, blk), rows), pl.BlockSpec((1, blk), lambda j, b, t: (0, j))],
        out_shape=[jax.ShapeDtypeStruct(ds.shape, BF16), jax.ShapeDtypeStruct((1, ds.shape[1]), F32)],
        scratch_shapes=[pltpu.VMEM((1, S5_LB), F32), pltpu.VMEM((1, S5_LB), F32), pltpu.VMEM((S5_TB, blk), F32),
                        pltpu.VMEM((S5_TB, blk), F32)],
        compiler_params=_cparams(("parallel", "arbitrary", "arbitrary")),
    )(ds, s, s, lam_pow_conj)


CONV_TR = 512
CONV_CW = 512


def _shift_down(x, halo, k):
    if k == 0:
        return x
    row8 = lax.broadcasted_iota(jnp.int32, halo.shape, 0)
    rolled = pltpu.roll(x, k, 0)
    top = jnp.where(row8 < k, pltpu.roll(halo, k, 0), rolled[:8])
    if x.shape[0] == 8:
        return top
    return jnp.concatenate([top, rolled[8:]], axis=0)


def _shift_up(x, halo, k):
    if k == 0:
        return x
    n = x.shape[0]
    row8 = lax.broadcasted_iota(jnp.int32, halo.shape, 0)
    rolled = pltpu.roll(x, n - k, 0)
    bot = jnp.where(row8 >= 8 - k, pltpu.roll(halo, 8 - k, 0), rolled[n - 8:])
    if n == 8:
        return bot
    return jnp.concatenate([rolled[:n - 8], bot], axis=0)


def _conv_pre(x, halo, w, b):
    acc = b + w[SSD_CONV - 1:SSD_CONV, :] * x
    for k in range(SSD_CONV - 1):
        acc = acc + w[k:k + 1, :] * _shift_down(x, halo, SSD_CONV - 1 - k)
    return acc


def _conv_specs(seq_len, col_off):
    lt = seq_len // CONV_TR
    cb = col_off // CONV_CW
    cur = pl.BlockSpec((CONV_TR, CONV_CW), lambda j, i: (i, j + cb))
    prev = pl.BlockSpec((8, CONV_CW), lambda j, i: (jnp.maximum(i * (CONV_TR // 8) - 1, 0), j + cb))
    return lt, cur, prev


def _conv_fwd(proj, conv_w, conv_b, n_rows, seq_len):
    lt, cur, prev = _conv_specs(seq_len, OFF_XBC)

    def body(x_ref, h_ref, w_ref, b_ref, o_ref):
        halo = jnp.where(pl.program_id(1) % lt == 0, 0.0, h_ref[...])
        o_ref[...] = jax.nn.silu(_conv_pre(x_ref[...], halo, w_ref[...], b_ref[...]))

    return pl.pallas_call(
        body, name="ssd_conv_fwd", grid=(SSD_CONV_DIM // CONV_CW, n_rows // CONV_TR),
        in_specs=[cur, prev, pl.BlockSpec((SSD_CONV, CONV_CW), lambda j, i: (0, j)),
                  pl.BlockSpec((1, CONV_CW), lambda j, i: (0, j))],
        out_specs=pl.BlockSpec((CONV_TR, CONV_CW), lambda j, i: (i, j)),
        out_shape=jax.ShapeDtypeStruct((n_rows, SSD_CONV_DIM), F32),
        compiler_params=_cparams(("parallel", "parallel")),
    )(proj, proj, conv_w, conv_b)


def _conv_bwd(name, proj, d_act, conv_w, conv_b, n_rows, seq_len, col_off, dproj):
    width = d_act.shape[1]
    lt, cur, prev = _conv_specs(seq_len, OFF_XBC + col_off)
    n_blk = n_rows // CONV_TR
    cb = (OFF_XBC + col_off) // CONV_CW
    pb = col_off // CONV_CW
    nxt = pl.BlockSpec((8, CONV_CW), lambda j, i: (jnp.minimum((i + 1) * (CONV_TR // 8), n_rows // 8 - 1), j + cb))
    d_cur = pl.BlockSpec((CONV_TR, CONV_CW), lambda j, i: (i, j))
    d_nxt = pl.BlockSpec((8, CONV_CW), lambda j, i: (jnp.minimum((i + 1) * (CONV_TR // 8), n_rows // 8 - 1), j))

    def dsilu(pre):
        sg = jax.nn.sigmoid(pre)
        return sg * (1.0 + pre * (1.0 - sg))

    def body(x_ref, hp_ref, hn_ref, d_ref, dn_ref, w_ref, b_ref, _, dx_ref, dw_ref, db_ref):
        i = pl.program_id(1)
        x, w, b = x_ref[...], w_ref[...], b_ref[...]
        halo_p = jnp.where(i % lt == 0, 0.0, hp_ref[...])
        at_end = i % lt == lt - 1
        dpre = d_ref[...] * dsilu(_conv_pre(x, halo_p, w, b))
        pre_n = _conv_pre(hn_ref[...], x[CONV_TR - 8:, :], w, b)
        dpre_n = jnp.where(at_end, 0.0, dn_ref[...] * dsilu(pre_n))
        dx = w[SSD_CONV - 1:SSD_CONV, :] * dpre
        for k in range(SSD_CONV - 1):
            dx = dx + w[k:k + 1, :] * _shift_up(dpre, dpre_n, SSD_CONV - 1 - k)
        dx_ref[...] = dx.astype(dx_ref.dtype)

        @pl.when(i == 0)
        def _():
            dw_ref[...] = jnp.zeros_like(dw_ref)
            db_ref[...] = jnp.zeros_like(db_ref)

        for k in range(SSD_CONV):
            xs = _shift_down(x, halo_p, SSD_CONV - 1 - k)
            dw_ref[k:k + 1, :] += jnp.sum(dpre * xs, axis=0, keepdims=True)
        db_ref[...] += jnp.sum(dpre, axis=0, keepdims=True)

    return pl.pallas_call(
        body, name=name, grid=(width // CONV_CW, n_blk),
        in_specs=[cur, prev, nxt, d_cur, d_nxt,
                  pl.BlockSpec((SSD_CONV, CONV_CW), lambda j, i: (0, j + pb)),
                  pl.BlockSpec((1, CONV_CW), lambda j, i: (0, j + pb)), pl.BlockSpec(memory_space=pl.ANY)],
        out_specs=[pl.BlockSpec((CONV_TR, CONV_CW), lambda j, i: (i, j + cb)),
                   pl.BlockSpec((SSD_CONV, CONV_CW), lambda j, i: (0, j)),
                   pl.BlockSpec((1, CONV_CW), lambda j, i: (0, j))],
        out_shape=[jax.ShapeDtypeStruct(dproj.shape, dproj.dtype),
                   jax.ShapeDtypeStruct((SSD_CONV, width), F32), jax.ShapeDtypeStruct((1, width), F32)],
        input_output_aliases={7: 0},
        compiler_params=_cparams(("parallel", "arbitrary")),
    )(proj, proj, proj, d_act, d_act, conv_w, conv_b, dproj)


def _split3(x):
    hi = x.astype(BF16)
    r = x - hi.astype(F32)
    mid = r.astype(BF16)
    return hi, mid, (r - mid.astype(F32)).astype(BF16)


def _sel_dot(a, b, a_is_sel):
    dn = (((1,), (0,)), ((), ()))
    if a_is_sel:
        return sum(lax.dot_general(a, t, dn, preferred_element_type=F32) for t in _split3(b))
    return sum(lax.dot_general(t, b, dn, preferred_element_type=F32) for t in _split3(a))


@jax.custom_vjp
def sel_left(sel, sel_t, x):
    return _sel_dot(sel, x, True)


@jax.custom_vjp
def sel_right(x, sel, sel_t):
    return _sel_dot(x, sel, False)


sel_left.defvjp(lambda s, st, x: (_sel_dot(s, x, True), (s, st)),
                lambda r, g: (jnp.zeros_like(r[0]), jnp.zeros_like(r[1]), _sel_dot(r[1], g, True)))
sel_right.defvjp(lambda x, s, st: (_sel_dot(x, s, False), (s, st)),
                 lambda r, g: (_sel_dot(g, r[1], False), jnp.zeros_like(r[0]), jnp.zeros_like(r[1])))


def _ssd_chunk(xs, bm, cm, dtr, st, dtb, alog, dsk, k):
    dt = jax.nn.softplus(dtr + dtb)
    acum = sel_left(k["tri"], k["tri_t"], dt * (-jnp.exp(alog)))
    dt_e = sel_right(dt, k["spread"], k["spread_t"])
    ac_e = sel_right(acum, k["spread"], k["spread_t"])
    al_e = ac_e[SSD_CHUNK - 1:SSD_CHUNK, :]
    dsk_e = sel_right(jnp.broadcast_to(dsk, (8, 128)), k["spread"], k["spread_t"])[0:1, :]
    xdt = xs * dt_e
    acum_t = acum.T
    scores = dot_nt(cm, bm)
    y = dot_nn(cm, st) * jnp.exp(ac_e) + xs * dsk_e
    for j in range(SSD_HPG):
        lmat = jnp.exp(jnp.where(k["causal"], acum[:, j:j + 1] - acum_t[j:j + 1, :], -jnp.inf))
        y = y + dot_nn(scores * lmat, jnp.where(k["head"] == j, xdt, 0.0))
    new = st * jnp.exp(al_e) + dot_tn(bm, xdt * jnp.exp(al_e - ac_e))
    return y, new


def _ssd_consts():
    r = lax.broadcasted_iota(jnp.int32, (SSD_CHUNK, SSD_CHUNK), 0)
    c = lax.broadcasted_iota(jnp.int32, (SSD_CHUNK, SSD_CHUNK), 1)
    hd = jnp.int32(SSD_HEADDIM)
    sr = lax.broadcasted_iota(jnp.int32, (128, GROUP_W), 0)
    sc = lax.div(lax.broadcasted_iota(jnp.int32, (128, GROUP_W), 1), hd)
    tr = lax.div(lax.broadcasted_iota(jnp.int32, (GROUP_W, 128), 0), hd)
    tc = lax.broadcasted_iota(jnp.int32, (GROUP_W, 128), 1)
    return {"tri": (r >= c).astype(BF16), "tri_t": (c >= r).astype(BF16), "causal": r >= c,
            "spread": (sr == sc).astype(BF16), "spread_t": (tr == tc).astype(BF16),
            "head": lax.div(lax.broadcasted_iota(jnp.int32, (SSD_CHUNK, GROUP_W), 1), hd)}


def _ssd_specs(n_c, reverse):
    def cidx(c):
        return n_c - 1 - c if reverse else c

    xs = pl.BlockSpec((SSD_CHUNK, GROUP_W), lambda g, b, c: (b * n_c + cidx(c), g))
    bm = pl.BlockSpec((SSD_CHUNK, SSD_STATE), lambda g, b, c: (b * n_c + cidx(c), SSD_WIDTH // SSD_STATE + g))
    cm = pl.BlockSpec((SSD_CHUNK, SSD_STATE), lambda g, b, c: (b * n_c + cidx(c), (SSD_WIDTH + SSD_BC) // SSD_STATE + g))
    dt = pl.BlockSpec((SSD_CHUNK, 128), lambda g, b, c: (b * n_c + cidx(c), OFF_DT // 128 + g))
    par = pl.BlockSpec((1, 128), lambda g, b, c: (0, g))
    st = pl.BlockSpec((1, 1, 1, SSD_STATE, GROUP_W), lambda g, b, c: (b, cidx(c), g, 0, 0))
    return xs, bm, cm, dt, par, st


def _ssd_fwd(xbc_act, proj, dtb, alog, dsk, n_seq, seq_len):
    n_c = seq_len // SSD_CHUNK
    xs_s, bm_s, cm_s, dt_s, par_s, st_s = _ssd_specs(n_c, False)

    def body(xs_ref, bm_ref, cm_ref, dt_ref, dtb_ref, al_ref, dk_ref, y_ref, st_ref, state):
        @pl.when(pl.program_id(2) == 0)
        def _():
            state[...] = jnp.zeros_like(state)

        prev = state[...]
        st_ref[0, 0, 0] = prev
        y, new = _ssd_chunk(xs_ref[...], bm_ref[...], cm_ref[...], dt_ref[...], prev,
                            dtb_ref[...], al_ref[...], dk_ref[...], _ssd_consts())
        y_ref[...] = y
        state[...] = new

    return pl.pallas_call(
        body, name="ssd_fwd", grid=(SSD_GROUPS, n_seq, n_c),
        in_specs=[xs_s, bm_s, cm_s, dt_s, par_s, par_s, par_s],
        out_specs=[pl.BlockSpec((SSD_CHUNK, GROUP_W), lambda g, b, c: (b * n_c + c, g)), st_s],
        out_shape=[jax.ShapeDtypeStruct((n_seq * seq_len, SSD_WIDTH), F32),
                   jax.ShapeDtypeStruct((n_seq, n_c, SSD_GROUPS, SSD_STATE, GROUP_W), F32)],
        scratch_shapes=[pltpu.VMEM((SSD_STATE, GROUP_W), F32)],
        compiler_params=_cparams(("parallel", "parallel", "arbitrary")),
    )(xbc_act, xbc_act, xbc_act, proj, dtb, alog, dsk)


def _ssd_bwd(xbc_act, proj, states, dy, dtb, alog, dsk, n_seq, seq_len, dproj):
    n_c = seq_len // SSD_CHUNK
    n_rows = n_seq * seq_len
    xs_s, bm_s, cm_s, dt_s, par_s, st_s = _ssd_specs(n_c, True)

    def rows(w, first=0):
        return pl.BlockSpec((SSD_CHUNK, w), lambda g, b, c: (b * n_c + (n_c - 1 - c), first + g))

    def body(xs_ref, bm_ref, cm_ref, dt_ref, st_ref, dy_ref, dtb_ref, al_ref, dk_ref, _,
             dxs_ref, dbm_ref, dcm_ref, ddt_ref, ddtb_ref, dal_ref, ddk_ref, dstate):
        b, c = pl.program_id(1), pl.program_id(2)

        @pl.when(c == 0)
        def _():
            dstate[...] = jnp.zeros_like(dstate)

        @pl.when((b == 0) & (c == 0))
        def _():
            ddtb_ref[...] = jnp.zeros_like(ddtb_ref)
            dal_ref[...] = jnp.zeros_like(dal_ref)
            ddk_ref[...] = jnp.zeros_like(ddk_ref)

        consts = _ssd_consts()
        _, vjp = jax.vjp(
            lambda xs, bm, cm, dtr, prev, dtb, alog, dsk: _ssd_chunk(xs, bm, cm, dtr, prev, dtb, alog, dsk, consts),
            xs_ref[...], bm_ref[...], cm_ref[...], dt_ref[...], st_ref[0, 0, 0], dtb_ref[...], al_ref[...], dk_ref[...])
        dxs, dbm, dcm, ddtr, dprev, ddtb, dal, ddk = vjp((dy_ref[...], dstate[...]))
        dxs_ref[...] = dxs
        dbm_ref[...] = dbm
        dcm_ref[...] = dcm
        ddt_ref[...] = ddtr.astype(ddt_ref.dtype)
        ddtb_ref[...] += ddtb
        dal_ref[...] += dal
        ddk_ref[...] += ddk
        dstate[...] = dprev

    acc = pl.BlockSpec((1, 128), lambda g, b, c: (0, g))
    return pl.pallas_call(
        body, name="ssd_bwd", grid=(SSD_GROUPS, n_seq, n_c),
        in_specs=[xs_s, bm_s, cm_s, dt_s, st_s, rows(GROUP_W), par_s, par_s, par_s, pl.BlockSpec(memory_space=pl.ANY)],
        out_specs=[rows(GROUP_W), rows(SSD_STATE), rows(SSD_STATE), rows(128, OFF_DT // 128), acc, acc, acc],
        out_shape=[jax.ShapeDtypeStruct((n_rows, SSD_WIDTH), F32), jax.ShapeDtypeStruct((n_rows, SSD_BC), F32),
                   jax.ShapeDtypeStruct((n_rows, SSD_BC), F32), jax.ShapeDtypeStruct(dproj.shape, dproj.dtype),
                   jax.ShapeDtypeStruct((1, 512), F32), jax.ShapeDtypeStruct((1, 512), F32),
                   jax.ShapeDtypeStruct((1, 512), F32)],
        input_output_aliases={9: 3},
        scratch_shapes=[pltpu.VMEM((SSD_STATE, GROUP_W), F32)],
        compiler_params=_cparams(("parallel", "arbitrary", "arbitrary")),
    )(xbc_act, xbc_act, xbc_act, proj, states, dy, dtb, alog, dsk, dproj)


def _pad_heads(v):
    return jnp.pad(v.reshape(SSD_GROUPS, SSD_HPG), ((0, 0), (0, 128 - SSD_HPG))).reshape(1, SSD_GROUPS * 128)


def _unpad_heads(v):
    return v.reshape(SSD_GROUPS, 128)[:, :SSD_HPG].reshape(1, SSD_HEADS)


def _state_cols(v):
    re, im = v
    lead = re.shape[:-1]
    re = re.reshape(lead + (S5_NJ, 1, S5_LB))
    im = im.reshape(lead + (S5_NJ, 1, S5_LB))
    return jnp.concatenate([re, im], axis=-2).reshape(lead + (2 * S5_N,))


def _state_uncols(v):
    lead = v.shape[:-1]
    v = v.reshape(lead + (S5_NJ, 2, S5_LB))
    return v[..., 0, :].reshape(lead + (S5_N,)), v[..., 1, :].reshape(lead + (S5_N,))


GROUPS_PER_BAND = BAND // S5_GROUP


def _band(w2_re, w2_im):
    gh = S5_GROUPS * S5_GROUP
    rg = ((jnp.arange(gh) // S5_GROUP) % GROUPS_PER_BAND)[:, None, None]
    cg = jnp.arange(GROUPS_PER_BAND)[None, :, None]
    parts = [jnp.where(rg == cg, v[:, None, :], 0.0).reshape(gh, S5_LB) for v in (w2_re, w2_im)]
    return jnp.concatenate(parts, axis=1)


def _band_take(wb):
    gh = S5_GROUPS * S5_GROUP
    w4 = wb.reshape(gh, 2, GROUPS_PER_BAND, S5_STATE)
    sel = w4[jnp.arange(gh), :, (jnp.arange(gh) // S5_GROUP) % GROUPS_PER_BAND, :]
    return sel[:, 0, :], sel[:, 1, :]


W_IN_SHARD = IN_PROJ_DIM // N_CHIPS
W_IN_SEGS = ((0, 512, OFF_U5), (512, 1024, OFF_Z5), (1024, 2560, OFF_ZS), (2560, 5120, OFF_XBC), (5144, 7192, OFF_G5))
DT_ROWS = (5120, 5144)


def _w_in_pieces():
    runs = []
    segs = list(W_IN_SEGS) + [(DT_ROWS[0] + SSD_HPG * g, DT_ROWS[0] + SSD_HPG * (g + 1), OFF_DT + 128 * g)
                              for g in range(SSD_GROUPS)]
    for lo, hi, off in segs:
        for j in range(N_CHIPS):
            s, e = max(lo, j * W_IN_SHARD), min(hi, (j + 1) * W_IN_SHARD)
            if s < e:
                runs.append((j, s - j * W_IN_SHARD, off + s - lo, e - s))
    return runs


RELAYOUT_LANES = 256


def _pad_w_in_t(a4):
    runs = _w_in_pieces()

    def body(a_ref, o_ref):
        o_ref[pl.ds(OFF_DT, DT_W), :] = jnp.zeros((DT_W, RELAYOUT_LANES), o_ref.dtype)
        for j, src, dst, n in runs:
            o_ref[pl.ds(dst, n), :] = a_ref[j, pl.ds(src, n), :]

    return pl.pallas_call(
        body, name="w_in_to_padded", grid=(D_MODEL // RELAYOUT_LANES,),
        in_specs=[pl.BlockSpec((N_CHIPS, W_IN_SHARD, RELAYOUT_LANES), lambda i: (0, 0, i))],
        out_specs=pl.BlockSpec((PROJ_W, RELAYOUT_LANES), lambda i: (0, i)),
        out_shape=jax.ShapeDtypeStruct((PROJ_W, D_MODEL), a4.dtype),
        compiler_params=_cparams(("parallel",)),
    )(a4)


def _unpad_w_in_t(wp):
    runs = _w_in_pieces()

    def body(p_ref, o_ref):
        for j, dst, src, n in runs:
            o_ref[j, pl.ds(dst, n), :] = p_ref[pl.ds(src, n), :]

    return pl.pallas_call(
        body, name="w_in_from_padded", grid=(D_MODEL // RELAYOUT_LANES,),
        in_specs=[pl.BlockSpec((PROJ_W, RELAYOUT_LANES), lambda i: (0, i))],
        out_specs=pl.BlockSpec((N_CHIPS, W_IN_SHARD, RELAYOUT_LANES), lambda i: (0, 0, i)),
        out_shape=jax.ShapeDtypeStruct((N_CHIPS, W_IN_SHARD, D_MODEL), wp.dtype),
        compiler_params=_cparams(("parallel",)),
    )(wp)


def _local_step(x, p, tgt, w):
    n_seq, seq_len, _ = x.shape
    n_rows = n_seq * seq_len
    tr = 512
    x2 = x.reshape(n_rows, D_MODEL)
    p2 = p.reshape(n_rows, PLE_DIM)
    t2 = tgt.reshape(n_rows, D_MODEL)
    row = functools.partial(_rowwise, n_rows=n_rows, tr=tr)

    w_pad_t = _pad_w_in_t(w["w_in_t"])
    norm_w = w["norm_w"].reshape(1, D_MODEL)
    ple_norm_w = w["ple_norm_w"].reshape(1, D_MODEL)
    final_w = w["final_norm_w"].reshape(1, D_MODEL)
    s5_d = w["s5_d"].reshape(1, S5_WIDTH)
    b_glu = w["s5_b_glu"].reshape(1, S5_WIDTH)
    conv_w = w["ssd_conv_w"].reshape(SSD_CONV, SSD_CONV_DIM)
    conv_b = w["ssd_conv_b"].reshape(1, SSD_CONV_DIM)
    ssd_norm_w = w["ssd_norm_w"].reshape(1, SSD_WIDTH)
    dtb, alog, dsk = (_pad_heads(w[k].reshape(1, SSD_HEADS)) for k in ("ssd_dt_bias", "ssd_a_log", "ssd_d"))

    gh = S5_GROUPS * S5_GROUP
    a_re = w["s5_a_re"].reshape(S5_GROUPS, S5_STATE)
    a_im = w["s5_a_im"].reshape(S5_GROUPS, S5_STATE)
    log_step = w["s5_log_step"].reshape(S5_GROUPS, 1)
    b_re2 = jnp.transpose(w["s5_b_re"].reshape(S5_GROUPS, S5_STATE, S5_GROUP), (0, 2, 1)).reshape(gh, S5_STATE)
    b_im2 = jnp.transpose(w["s5_b_im"].reshape(S5_GROUPS, S5_STATE, S5_GROUP), (0, 2, 1)).reshape(gh, S5_STATE)
    expand = (jnp.arange(gh)[:, None] // S5_GROUP == jnp.arange(S5_GROUPS)[None, :]).astype(F32)
    pow_re, pow_im, bb_re2, bb_im2 = _s5_params_fwd(a_re, a_im, log_step, b_re2, b_im2, expand)
    lam_pow = _state_cols((pow_re.reshape(S5_LOG_TB, S5_N), pow_im.reshape(S5_LOG_TB, S5_N)))
    lam_pow_conj = _state_cols((pow_re.reshape(S5_LOG_TB, S5_N), -pow_im.reshape(S5_LOG_TB, S5_N)))
    bb_band = _band(bb_re2, bb_im2).astype(BF16)
    c_band = _band(w["s5_c_re"].reshape(gh, S5_STATE), -w["s5_c_im"].reshape(gh, S5_STATE)).astype(BF16)

    late = w.get("_late")
    (hn,) = row("rms_in", lambda r, q: ([_rms(r[0], q[0])], []), row_ins=[(x2, 0, D_MODEL)],
                par_ins=[(norm_w, 0, D_MODEL)] + ([(late[0], 0, 128)] if late else []), row_outs=[(D_MODEL, BF16)])
    proj = _matmul("mm_proj", hn, w_pad_t, tb=True)
    bu = _band_matmul("mm_s5_bu", "nn", proj, bb_band, a_blk0=OFF_U5 // BAND)
    s = _s5_scan_fwd(bu, lam_pow, n_seq, seq_len)
    yc, ge = _band_matmul("mm_s5_y", "nt", s, c_band, epilogue=lambda c, r, q: [c, jax.nn.gelu(c + q[0] * r[0])],
                          epi_rows=[(proj, OFF_U5 // BAND)], epi_pars=[s5_d], epi_outs=[F32, BF16])
    if late:
        w = {**w, **late[1](ge)}
    tg, y5 = _matmul("mm_s5_glu", ge, w["s5_w_glu"], epilogue=lambda c, r, q: [c, _s5_out(r[0], r[1], r[2], c, *q)],
                     epi_rows=[(yc, 0), (proj, OFF_U5), (proj, OFF_Z5)], epi_pars=[s5_d, b_glu], epi_outs=[F32, BF16])
    s5_rows = [(yc, 0, S5_WIDTH), (proj, OFF_U5, S5_WIDTH), (proj, OFF_Z5, S5_WIDTH), (tg, 0, S5_WIDTH)]
    s5_pars = [(s5_d, 0, S5_WIDTH), (b_glu, 0, S5_WIDTH)]

    xbc_act = _conv_fwd(proj, conv_w, conv_b, n_rows, seq_len)
    y_ssd, states = _ssd_fwd(xbc_act, proj, dtb, alog, dsk, n_seq, seq_len)
    gn_rows = [(y_ssd, 0, SSD_WIDTH), (proj, OFF_ZS, SSD_WIDTH)]
    (yss,) = row("ssd_gate", lambda r, q: ([_gated_norm(r[0], r[1], q[0])], []), row_ins=gn_rows,
                 par_ins=[(ssd_norm_w, 0, SSD_WIDTH)], row_outs=[(SSD_WIDTH, BF16)])

    m5 = _matmul("mm_br_s5", y5, w["w_br_s5"])
    ms, merged = _matmul("mm_br_ssd", yss, w["w_br_ssd"], epilogue=lambda c, r, q: [c, _merge(r[0], r[1], r[2], c)],
                         epi_rows=[(proj, OFF_G5), (proj, OFF_GS), (m5, 0)], epi_outs=[F32, BF16])
    mg_rows = [(proj, OFF_G5, D_MODEL), (proj, OFF_GS, D_MODEL), (m5, 0, D_MODEL), (ms, 0, D_MODEL)]

    def resid_norm(c, r, q):
        h1_ = r[0] + c
        return [h1_, _rms(h1_, q[0])]

    h1, hp = _matmul("mm_out", merged, w["w_out"], epilogue=resid_norm, epi_rows=[(x2, 0)], epi_pars=[ple_norm_w],
                     epi_outs=[F32, BF16], full_rows=True)
    pp = _matmul("mm_ple_proj", p2, w["w_ple_proj"])

    def head_fn(pgl_, r, q):
        h1_, pp_, tgt_ = r
        loss, vjp = jax.vjp(lambda a, b, c, f: _head_loss(a, b, c, f, tgt_), h1_, pgl_, pp_, q[0])
        dh1_, dpgl_, dpp_, dfw_ = vjp(jnp.ones_like(loss))
        return [dh1_, dpgl_, dpp_], [loss, dfw_]

    dh2, dpgl, dpp, loss_acc, d_final_w = _matmul(
        "mm_ple_gate_head", hp, w["w_ple_gate"], epilogue=head_fn, epi_rows=[(h1, 0), (pp, 0), (t2, 0)],
        epi_pars=[final_w], epi_outs=[F32, BF16, BF16], epi_accs=[(1, 128), (1, D_MODEL)], full_rows=True)
    loss = loss_acc[0, 0]

    g = {}
    g["final_norm_w"] = d_final_w
    g["w_ple_gate"] = _matmul("mm_d_w_ple_gate", hp, dpgl, ta=True)
    g["w_ple_proj"] = _matmul("mm_d_w_ple_proj", p2, dpp, ta=True)

    def ple_norm_bwd(dhp_, r, q):
        h1_, dh2_ = r
        _, vjp = jax.vjp(_rms, h1_, q[0])
        dh, dw = vjp(dhp_)
        dh = dh + dh2_
        return [dh, dh], [dw]

    dh1, dh1_b, g["ple_norm_w"] = _matmul(
        "mm_d_hp", dpgl, w["w_ple_gate"], tb=True, epilogue=ple_norm_bwd, epi_rows=[(h1, 0), (dh2, 0)],
        epi_pars=[ple_norm_w], epi_outs=[F32, BF16], epi_accs=[(1, D_MODEL)], full_rows=True)
    g["w_out"] = _matmul("mm_d_w_out", merged, dh1_b, ta=True)

    dproj = lax.empty((n_rows, PROJ_W), BF16)

    def merge_bwd(dmerged, r, q):
        sg5, sgs = jax.nn.sigmoid(r[0]), jax.nn.sigmoid(r[1])
        d_gates = jnp.concatenate([dmerged * r[2] * sg5 * (1.0 - sg5), dmerged * r[3] * sgs * (1.0 - sgs)], axis=1)
        return [d_gates, dmerged * sg5, dmerged * sgs]

    dproj, dm5, dms = _matmul(
        "mm_d_merged", dh1_b, w["w_out"], tb=True, epilogue=merge_bwd,
        epi_rows=[(proj, OFF_G5), (proj, OFF_GS), (m5, 0), (ms, 0)], epi_outs=[BF16, BF16, BF16], full_rows=True,
        epi_into=(dproj, OFF_G5, 2 * D_MODEL))
    g["w_br_s5"] = _matmul("mm_d_w_br_s5", y5, dm5, ta=True)
    g["w_br_ssd"] = _matmul("mm_d_w_br_ssd", yss, dms, ta=True)
    dy5 = _matmul("mm_d_y5", dm5, w["w_br_s5"], tb=True)
    dyss = _matmul("mm_d_yss", dms, w["w_br_ssd"], tb=True)

    def s5_out_bwd_a(r, q):
        yc_, u_, z_, t_, dy_ = r
        d_, bg_ = q
        ge_ = jax.nn.gelu(yc_ + d_ * u_)
        _, vjp = jax.vjp(lambda a, z, t, b: a * jax.nn.sigmoid(t + b) * jax.nn.silu(z), ge_, z_, t_, bg_)
        dge, dz, dt_, dbg = vjp(dy_)
        return [dz, dge, dt_], [dbg]

    dproj, dge_a, dtg, g["s5_b_glu"] = row(
        "s5_out_bwd_a", s5_out_bwd_a, row_ins=s5_rows + [(dy5, 0, S5_WIDTH)], par_ins=s5_pars,
        row_outs=[(S5_WIDTH, BF16), (S5_WIDTH, F32), (S5_WIDTH, BF16)], acc_outs=[(1, S5_WIDTH)],
        into=(dproj, OFF_Z5))
    g["s5_w_glu"] = _matmul("mm_d_w_glu", ge, dtg, ta=True)
    early = w["_early"](g) if "_early" in w else None
    dge_b = _matmul("mm_d_ge", dtg, w["s5_w_glu"], tb=True)

    def s5_out_bwd_b(r, q):
        yc_, u_, da_, db_ = r
        _, vjp = jax.vjp(lambda yc, u, d: jax.nn.gelu(yc + d * u), yc_, u_, q[0])
        dyc_, du_, dd_ = vjp(da_ + db_)
        return [dyc_, du_], [dd_]

    dyc, du5_a, g["s5_d"] = row(
        "s5_out_bwd_b", s5_out_bwd_b,
        row_ins=[(yc, 0, S5_WIDTH), (proj, OFF_U5, S5_WIDTH), (dge_a, 0, S5_WIDTH), (dge_b, 0, S5_WIDTH)],
        par_ins=[(s5_d, 0, S5_WIDTH)] + ([(early, 0, 128)] if early is not None else []),
        row_outs=[(S5_WIDTH, BF16), (S5_WIDTH, F32)], acc_outs=[(1, S5_WIDTH)])
    d_c_band = _band_matmul("mm_d_c", "tn", dyc, s)
    ds = _band_matmul("mm_d_s", "nn", dyc, c_band)
    dbu, d_lam = _s5_scan_bwd(ds, s, lam_pow_conj, n_seq, seq_len)
    d_bb_band = _band_matmul("mm_d_bb", "tn", proj, dbu, a_blk0=OFF_U5 // BAND)
    (dproj,) = _band_matmul("mm_d_u5", "nt", dbu, bb_band, epilogue=lambda c, r, q: [c + r[0]],
                            epi_rows=[(du5_a, 0)], epi_outs=[BF16], into=(dproj, OFF_U5 // BAND))

    d_lr, d_li = _state_uncols(d_lam)
    d_bbr, d_bbi = _band_take(d_bb_band)
    d_are, d_aim, d_ls, d_br2, d_bi2 = _s5_params_bwd(
        a_re, a_im, log_step, b_re2, b_im2, expand,
        d_lr.reshape(S5_GROUPS, S5_STATE), d_li.reshape(S5_GROUPS, S5_STATE), d_bbr, d_bbi)
    g["s5_a_re"], g["s5_a_im"], g["s5_log_step"] = d_are, d_aim, d_ls
    g["s5_b_re_ghp"], g["s5_b_im_ghp"] = d_br2, d_bi2
    d_cr, d_ci = _band_take(d_c_band)
    g["s5_c_re"], g["s5_c_im"] = d_cr, -d_ci

    def gate_bwd(r, q):
        _, vjp = jax.vjp(_gated_norm, r[0], r[1], q[0])
        dy_, dz_, dw_ = vjp(r[2])
        return [dz_, dy_], [dw_]

    dproj, dy_ssd, g["ssd_norm_w"] = row(
        "ssd_gate_bwd", gate_bwd, row_ins=gn_rows + [(dyss, 0, SSD_WIDTH)], par_ins=[(ssd_norm_w, 0, SSD_WIDTH)],
        row_outs=[(SSD_WIDTH, BF16), (SSD_WIDTH, F32)], acc_outs=[(1, SSD_WIDTH)], into=(dproj, OFF_ZS))
    dxs, dbm, dcm, dproj, d_dtb, d_alog, d_dsk = _ssd_bwd(xbc_act, proj, states, dy_ssd, dtb, alog, dsk, n_seq, seq_len,
                                                       dproj)
    g["ssd_dt_bias"], g["ssd_a_log"], g["ssd_d"] = _unpad_heads(d_dtb), _unpad_heads(d_alog), _unpad_heads(d_dsk)
    conv_dw, conv_db = [], []
    for nm, d_act, off in (("x", dxs, 0), ("b", dbm, SSD_WIDTH), ("c", dcm, SSD_WIDTH + SSD_BC)):
        dproj, dw_, db_ = _conv_bwd("ssd_conv_bwd_" + nm, proj, d_act, conv_w, conv_b, n_rows, seq_len, off, dproj)
        conv_dw.append(dw_)
        conv_db.append(db_)
    g["ssd_conv_w"] = jnp.concatenate(conv_dw, axis=1)
    g["ssd_conv_b"] = jnp.concatenate(conv_db, axis=1)

    g["w_in_t"] = _unpad_w_in_t(_matmul("mm_d_w_in", dproj, hn, ta=True))
    def norm_bwd(dhn_, r, q):
        x_, dh1_ = r
        _, vjp = jax.vjp(_rms, x_, q[0])
        dx_, dw_ = vjp(dhn_)
        return [dx_ + dh1_], [dw_]

    dx, g["norm_w"] = _matmul("mm_d_hn", dproj, w_pad_t, epilogue=norm_bwd, epi_rows=[(x2, 0), (dh1, 0)],
                              epi_pars=[norm_w], epi_outs=[F32], epi_accs=[(1, D_MODEL)], full_rows=True)
    return loss, dx.reshape(x.shape), g


HBM = pl.BlockSpec(memory_space=pltpu.HBM)


def _chip_index(x, y):
    return 2 * x + y


def _half(shape2d, axis, which):
    h = shape2d[axis] // 2
    sl = pl.ds(pl.multiple_of(which * h, 128 if axis else 8), h)
    return (slice(None), sl) if axis else (sl, slice(None))


def _gather_chips(split, axes, whole):
    ns, nw = len(split), len(whole)
    n = ns + nw

    def body(*refs):
        ins, outs = refs[:n], refs[n:2 * n]
        ici_send, ici_recv, d2d_send, d2d_recv, local_sems = refs[2 * n:]
        x, y, c = lax.axis_index("x"), lax.axis_index("y"), lax.axis_index("c")
        me = _chip_index(x, y)
        sibling = (x, y, 1 - c)
        peers = [(1 - x, y), (x, 1 - y), (1 - x, 1 - y)]

        def half(t, which):
            return _half(split[t].shape, axes[t], which)

        copies = []
        for t in range(n):
            loc = pltpu.make_async_copy(ins[t], outs[t].at[me], local_sems.at[t])
            loc.start()
            copies.append(loc)

        def ici(t, k, slot):
            px, py = peers[k]
            if t < ns:
                src, dst = ins[t].at[half(t, c)], outs[t].at[(slot,) + half(t, c)]
            else:
                src, dst = ins[t], outs[t].at[slot]
            return pltpu.make_async_remote_copy(src_ref=src, dst_ref=dst, send_sem=ici_send.at[t, k],
                                                recv_sem=ici_recv.at[t, k], device_id=(px, py, c), device_id_type=MESH)

        def d2d(t, k, which):
            rows = outs[t].at[(_chip_index(*peers[k]),) + half(t, which)]
            return pltpu.make_async_remote_copy(src_ref=rows, dst_ref=rows, send_sem=d2d_send.at[t, k],
                                                recv_sem=d2d_recv.at[t, k], device_id=sibling, device_id_type=MESH)

        sends = []
        for t in range(n):
            for k in range(3):
                cp = ici(t, k, me)
                cp.start()
                sends.append(cp)
        for t in range(n):
            for k in range(3):
                ici(t, k, _chip_index(*peers[k])).wait_recv()
                if t < ns:
                    cp = d2d(t, k, c)
                    cp.start()
                    sends.append(cp)
        for t in range(ns):
            for k in range(3):
                d2d(t, k, 1 - c).wait_recv()
        for cp in sends:
            cp.wait_send()
        for cp in copies:
            cp.wait()

    arrays = list(split) + list(whole)
    return pl.pallas_call(
        body, name="gather_weights",
        in_specs=[HBM] * n, out_specs=[HBM] * n,
        out_shape=[jax.ShapeDtypeStruct((N_CHIPS,) + a.shape, a.dtype) for a in arrays],
        scratch_shapes=[pltpu.SemaphoreType.DMA((n, 3)), pltpu.SemaphoreType.DMA((n, 3)),
                        pltpu.SemaphoreType.DMA((ns, 3)), pltpu.SemaphoreType.DMA((ns, 3)),
                        pltpu.SemaphoreType.DMA((n,))],
    )(*arrays)


SEM = pl.BlockSpec(memory_space=pltpu.SEMAPHORE)
DATAFLOW = pltpu.SideEffectType.DATAFLOW_SIDE_EFFECTING


def _gather_start(shards, after):
    n = len(shards)

    def body(*refs):
        ins, lands = refs[:n], refs[n:2 * n]
        send_sems, recv_sems = refs[2 * n + 1], refs[2 * n + 2]
        token = refs[-1]
        x, y, c = lax.axis_index("x"), lax.axis_index("y"), lax.axis_index("c")
        me = _chip_index(x, y)
        for t in range(n):
            for k, (px, py) in enumerate([(1 - x, y), (x, 1 - y), (1 - x, 1 - y)]):
                pltpu.make_async_remote_copy(
                    src_ref=ins[t], dst_ref=lands[t].at[me], send_sem=send_sems.at[3 * t + k],
                    recv_sem=recv_sems.at[3 * t + k],
                    device_id=(px, py, c), device_id_type=MESH).start()
        token[...] = jnp.zeros_like(token)

    zones = [lax.empty((N_CHIPS,) + a.shape, a.dtype) for a in shards]
    res = pl.pallas_call(
        body, name="gather_rest_start",
        out_shape=(pltpu.SemaphoreType.DMA((3 * n,)), pltpu.SemaphoreType.DMA((3 * n,)),
                   *[pltpu.HBM(a.shape, a.dtype) for a in shards], *[pltpu.HBM(z.shape, z.dtype) for z in zones],
                   jax.ShapeDtypeStruct((8, 128), F32)),
        in_specs=[HBM] * (2 * n) + [pl.BlockSpec(memory_space=pl.ANY)],
        out_specs=(SEM, SEM, *[HBM] * (2 * n), pl.BlockSpec(memory_space=pltpu.VMEM)),
        input_output_aliases={t: 2 + t for t in range(2 * n)},
        compiler_params=pltpu.CompilerParams(has_side_effects=DATAFLOW),
    )(*[pltpu.with_memory_space_constraint(a, pltpu.HBM) for a in shards],
      *[pltpu.with_memory_space_constraint(z, pltpu.HBM) for z in zones], after)
    return res[0], res[1], list(res[2:2 + n]), list(res[2 + n:2 + 2 * n]), res[-1]


def _gather_wait(send_sems, recv_sems, thru, lands, after):
    n = len(thru)

    def body(*refs):
        ins, zones = refs[:n], refs[n:2 * n]
        s_sems, r_sems = refs[2 * n], refs[2 * n + 1]
        x, y, c = lax.axis_index("x"), lax.axis_index("y"), lax.axis_index("c")
        for t in range(n):
            for k, (px, py) in enumerate([(1 - x, y), (x, 1 - y), (1 - x, 1 - y)]):
                cp = pltpu.make_async_remote_copy(
                    src_ref=ins[t], dst_ref=zones[t].at[_chip_index(px, py)], send_sem=s_sems.at[3 * t + k],
                    recv_sem=r_sems.at[3 * t + k], device_id=(px, py, c), device_id_type=MESH)
                cp.wait_send()
                cp.wait_recv()

    res = pl.pallas_call(
        body, name="gather_rest_wait",
        out_shape=(*[pltpu.HBM(a.shape, a.dtype) for a in thru], *[pltpu.HBM(z.shape, z.dtype) for z in lands]),
        in_specs=[HBM] * (2 * n) + [SEM, SEM, pl.BlockSpec(memory_space=pl.ANY)], out_specs=[HBM] * (2 * n),
        input_output_aliases={t: t for t in range(2 * n)},
        compiler_params=pltpu.CompilerParams(has_side_effects=DATAFLOW),
    )(*thru, *lands, send_sems, recv_sems, after)
    return list(res[:n]), list(res[n:])


def _scatter_start(slotted, after):
    n = len(slotted)

    def body(*refs):
        ins, lands = refs[:n], refs[n:2 * n]
        send_sems, recv_sems = refs[2 * n + 1], refs[2 * n + 2]
        token = refs[-1]
        x, y, c = lax.axis_index("x"), lax.axis_index("y"), lax.axis_index("c")
        for t in range(n):
            for k, (px, py) in enumerate([(1 - x, y), (x, 1 - y), (1 - x, 1 - y)]):
                pltpu.make_async_remote_copy(
                    src_ref=ins[t].at[_chip_index(px, py)], dst_ref=lands[t].at[k], send_sem=send_sems.at[3 * t + k],
                    recv_sem=recv_sems.at[3 * t + k], device_id=(px, py, c), device_id_type=MESH).start()
        token[...] = jnp.zeros_like(token)

    zones = [lax.empty((3,) + a.shape[1:], a.dtype) for a in slotted]
    res = pl.pallas_call(
        body, name="scatter_rest_start",
        out_shape=(pltpu.SemaphoreType.DMA((3 * n,)), pltpu.SemaphoreType.DMA((3 * n,)),
                   *[pltpu.HBM(a.shape, a.dtype) for a in slotted], *[pltpu.HBM(z.shape, z.dtype) for z in zones],
                   jax.ShapeDtypeStruct((8, 128), F32)),
        in_specs=[HBM] * (2 * n) + [pl.BlockSpec(memory_space=pl.ANY)],
        out_specs=(SEM, SEM, *[HBM] * (2 * n), pl.BlockSpec(memory_space=pltpu.VMEM)),
        input_output_aliases={t: 2 + t for t in range(2 * n)},
        compiler_params=pltpu.CompilerParams(has_side_effects=DATAFLOW),
    )(*[pltpu.with_memory_space_constraint(a, pltpu.HBM) for a in slotted],
      *[pltpu.with_memory_space_constraint(z, pltpu.HBM) for z in zones], after)
    return res[0], res[1], list(res[2:2 + n]), list(res[2 + n:2 + 2 * n]), res[-1]


def _scatter_wait(send_sems, recv_sems, thru, lands, after):
    n = len(thru)

    def body(*refs):
        ins, zones = refs[:n], refs[n:2 * n]
        s_sems, r_sems = refs[2 * n], refs[2 * n + 1]
        x, y, c = lax.axis_index("x"), lax.axis_index("y"), lax.axis_index("c")
        for t in range(n):
            for k, (px, py) in enumerate([(1 - x, y), (x, 1 - y), (1 - x, 1 - y)]):
                cp = pltpu.make_async_remote_copy(
                    src_ref=ins[t].at[_chip_index(px, py)], dst_ref=zones[t].at[k], send_sem=s_sems.at[3 * t + k],
                    recv_sem=r_sems.at[3 * t + k], device_id=(px, py, c), device_id_type=MESH)
                cp.wait_send()
                cp.wait_recv()

    res = pl.pallas_call(
        body, name="scatter_rest_wait",
        out_shape=(*[pltpu.HBM(a.shape, a.dtype) for a in thru], *[pltpu.HBM(z.shape, z.dtype) for z in lands]),
        in_specs=[HBM] * (2 * n) + [SEM, SEM, pl.BlockSpec(memory_space=pl.ANY)], out_specs=[HBM] * (2 * n),
        input_output_aliases={t: t for t in range(2 * n)},
        compiler_params=pltpu.CompilerParams(has_side_effects=DATAFLOW),
    )(*thru, *lands, send_sems, recv_sems, after)
    return list(res[:n]), list(res[n:])


def _sum_chips(name, slotted, recv, idx):
    _, r, c = slotted.shape
    tr = _row_tile(r, 5 * c * 4)

    def body(idx_ref, own_ref, r_ref, o_ref):
        acc = own_ref[0]
        for k in range(3):
            acc = acc + r_ref[k]
        o_ref[...] = acc

    grid_spec = pltpu.PrefetchScalarGridSpec(
        num_scalar_prefetch=1, grid=(r // tr,),
        in_specs=[pl.BlockSpec((1, tr, c), lambda i, s: (s[1], i, 0)), pl.BlockSpec((3, tr, c), lambda i, s: (0, i, 0))],
        out_specs=pl.BlockSpec((tr, c), lambda i, s: (i, 0)))
    return pl.pallas_call(
        body, name=name, grid_spec=grid_spec, out_shape=jax.ShapeDtypeStruct((r, c), F32),
        compiler_params=_cparams(("parallel",)),
    )(idx, slotted, recv)


def _half_shape(shape2d, axis):
    r, c = shape2d
    return (r, c // 2) if axis else (r // 2, c)


def _swap_halves(slotted, axes, small):
    n = len(slotted)

    def body(*refs):
        ins, sm_in = refs[:n], refs[n]
        outs, sm_out = refs[n + 1:2 * n + 1], refs[2 * n + 1]
        send_sems, recv_sems, local_sem, sm_send, sm_recv = refs[2 * n + 2:]
        x, y, c = lax.axis_index("x"), lax.axis_index("y"), lax.axis_index("c")
        dev = 4 * x + 2 * y + c
        local = pltpu.make_async_copy(sm_in, sm_out.at[dev], local_sem)
        local.start()
        sends = []
        for t in range(n):
            other = (slice(None),) + _half(slotted[t].shape[1:], axes[t], 1 - c)
            cp = pltpu.make_async_remote_copy(
                src_ref=ins[t].at[other], dst_ref=outs[t], send_sem=send_sems.at[t],
                recv_sem=recv_sems.at[t], device_id=(x, y, 1 - c), device_id_type=MESH)
            cp.start()
            sends.append(cp)
        rel = [(fx, fy, fc) for fx in (0, 1) for fy in (0, 1) for fc in (0, 1)][1:]
        for k, (fx, fy, fc) in enumerate(rel):
            cp = pltpu.make_async_remote_copy(
                src_ref=sm_in, dst_ref=sm_out.at[dev], send_sem=sm_send.at[k], recv_sem=sm_recv.at[k],
                device_id=(x ^ fx, y ^ fy, c ^ fc), device_id_type=MESH)
            cp.start()
            sends.append(cp)
        for cp in sends[:n]:
            cp.wait_recv()
        for k, (fx, fy, fc) in enumerate(rel):
            src_dev = 4 * (x ^ fx) + 2 * (y ^ fy) + (c ^ fc)
            pltpu.make_async_remote_copy(
                src_ref=sm_in, dst_ref=sm_out.at[src_dev], send_sem=sm_send.at[k], recv_sem=sm_recv.at[k],
                device_id=(x ^ fx, y ^ fy, c ^ fc), device_id_type=MESH).wait_recv()
        for cp in sends:
            cp.wait_send()
        local.wait()

    return pl.pallas_call(
        body, name="swap_halves",
        in_specs=[HBM] * (n + 1), out_specs=[HBM] * (n + 1),
        out_shape=[jax.ShapeDtypeStruct((a.shape[0],) + _half_shape(a.shape[1:], ax), a.dtype)
                   for a, ax in zip(slotted, axes, strict=True)]
        + [jax.ShapeDtypeStruct((N_DEV,) + small.shape, small.dtype)],
        scratch_shapes=[pltpu.SemaphoreType.DMA((n,)), pltpu.SemaphoreType.DMA((n,)), pltpu.SemaphoreType.DMA,
                        pltpu.SemaphoreType.DMA((7,)), pltpu.SemaphoreType.DMA((7,))],
    )(*slotted, small)


def _scatter_halves(parts):
    n = len(parts)

    def body(*refs):
        ins, outs = refs[:n], refs[n:2 * n]
        send_sems, recv_sems = refs[2 * n:]
        x, y, c = lax.axis_index("x"), lax.axis_index("y"), lax.axis_index("c")
        peers = [(1 - x, y), (x, 1 - y), (1 - x, 1 - y)]
        sends = []
        for t in range(n):
            for k, (px, py) in enumerate(peers):
                cp = pltpu.make_async_remote_copy(
                    src_ref=ins[t].at[_chip_index(px, py)], dst_ref=outs[t].at[k], send_sem=send_sems.at[t, k],
                    recv_sem=recv_sems.at[t, k], device_id=(px, py, c), device_id_type=MESH)
                cp.start()
                sends.append(cp)
        for cp in sends:
            cp.wait_recv()
        for cp in sends:
            cp.wait_send()

    return pl.pallas_call(
        body, name="scatter_halves",
        in_specs=[HBM] * n, out_specs=[HBM] * n,
        out_shape=[jax.ShapeDtypeStruct((3,) + a.shape[1:], a.dtype) for a in parts],
        scratch_shapes=[pltpu.SemaphoreType.DMA((n, 3)), pltpu.SemaphoreType.DMA((n, 3))],
    )(*parts)


def _tiling(r, c, f32_per_elem):
    if r % 8 == 0:
        tr = _row_tile(r, f32_per_elem * c * 4)
        return r // tr, (tr, c), lambda i: (i, 0)
    assert c % 128 == 0, (r, c)
    return c // 128, (r, 128), lambda i: (0, i)


def _pair_sum(name, slotted, other, idx, axis):
    _, r, c = slotted.shape
    hr, hc = _half_shape((r, c), axis)
    n, (tr, tc), at = _tiling(hr, hc, 13)
    if axis == 0:
        a = slotted.reshape(N_CHIPS, 2, hr, c)
        a_all = pl.BlockSpec((N_CHIPS, 1, tr, tc), lambda i, s: (0, s[0]) + at(i))
        a_own = pl.BlockSpec((1, 1, tr, tc), lambda i, s: (s[1], s[0]) + at(i))
    else:
        a, per_half = slotted, hc // tc
        a_all = pl.BlockSpec((N_CHIPS, tr, tc), lambda i, s: (0, at(i)[0], s[0] * per_half + at(i)[1]))
        a_own = pl.BlockSpec((1, tr, tc), lambda i, s: (s[1], at(i)[0], s[0] * per_half + at(i)[1]))

    def body(idx_ref, a_ref, b_ref, am_ref, bm_ref, p_ref, own_ref):
        mine, mine_own = (a_ref[:, 0], am_ref[0, 0]) if axis == 0 else (a_ref[...], am_ref[0])
        p_ref[...] = (mine + b_ref[...]).astype(p_ref.dtype)
        own_ref[...] = mine_own + bm_ref[0]

    grid_spec = pltpu.PrefetchScalarGridSpec(
        num_scalar_prefetch=1, grid=(n,),
        in_specs=[a_all, pl.BlockSpec((N_CHIPS, tr, tc), lambda i, s: (0,) + at(i)),
                  a_own, pl.BlockSpec((1, tr, tc), lambda i, s: (s[1],) + at(i))],
        out_specs=[pl.BlockSpec((N_CHIPS, tr, tc), lambda i, s: (0,) + at(i)),
                   pl.BlockSpec((tr, tc), lambda i, s: at(i))])
    return pl.pallas_call(
        body, name=name, grid_spec=grid_spec,
        out_shape=[jax.ShapeDtypeStruct((N_CHIPS, hr, hc), BF16), jax.ShapeDtypeStruct((hr, hc), F32)],
        compiler_params=_cparams(("parallel",)),
    )(idx, a, other, a, other)


def _sum_parts(name, own, recv):
    h, c = own.shape
    n, (tr, tc), at = _tiling(h, c, 4)

    def body(o_ref, r_ref, out_ref):
        acc = o_ref[...]
        for k in range(3):
            acc = acc + r_ref[k].astype(F32)
        out_ref[...] = acc

    return pl.pallas_call(
        body, name=name, grid=(n,),
        in_specs=[pl.BlockSpec((tr, tc), at), pl.BlockSpec((3, tr, tc), lambda i: (0,) + at(i))],
        out_specs=pl.BlockSpec((tr, tc), at),
        out_shape=jax.ShapeDtypeStruct((h, c), F32),
        compiler_params=_cparams(("parallel",)),
    )(own, recv)


def _swap_sibling(parts):
    n = len(parts)

    def body(*refs):
        ins, outs = refs[:n], refs[n:2 * n]
        send_sems, recv_sems = refs[2 * n:]
        x, y, c = lax.axis_index("x"), lax.axis_index("y"), lax.axis_index("c")
        cps = []
        for t in range(n):
            cp = pltpu.make_async_remote_copy(
                src_ref=ins[t], dst_ref=outs[t], send_sem=send_sems.at[t], recv_sem=recv_sems.at[t],
                device_id=(x, y, 1 - c), device_id_type=MESH)
            cp.start()
            cps.append(cp)
        for cp in cps:
            cp.wait_recv()
        for cp in cps:
            cp.wait_send()

    return pl.pallas_call(
        body, name="swap_sibling",
        in_specs=[HBM] * n, out_specs=[HBM] * n,
        out_shape=[jax.ShapeDtypeStruct(a.shape, a.dtype) for a in parts],
        scratch_shapes=[pltpu.SemaphoreType.DMA((n,)), pltpu.SemaphoreType.DMA((n,))],
    )(*parts)


def _sum_slots(name, a):
    k, r, c = a.shape
    tr = _row_tile(r, (k + 1) * c * 4)

    def body(a_ref, o_ref):
        acc = a_ref[0]
        for i in range(1, k):
            acc = acc + a_ref[i]
        o_ref[...] = acc

    return pl.pallas_call(
        body, name=name, grid=(r // tr,),
        in_specs=[pl.BlockSpec((k, tr, c), lambda i: (0, i, 0))],
        out_specs=pl.BlockSpec((tr, c), lambda i: (i, 0)),
        out_shape=jax.ShapeDtypeStruct((r, c), a.dtype),
        compiler_params=_cparams(("parallel",)),
    )(a)


def _adamw(name, w, m, v, g_parts):
    r, c = w.shape
    ng = len(g_parts)
    tr = _row_tile(r, (7 + ng) * c * 4)
    c1 = 1.0 - ADAM_B1 ** ADAM_STEP
    c2 = 1.0 - ADAM_B2 ** ADAM_STEP

    def body(*refs):
        w_ref, m_ref, v_ref = refs[:3]
        g_refs = refs[3:3 + ng]
        go_ref, d_ref, mo_ref, vo_ref = refs[3 + ng:]
        g = g_refs[0][...]
        for gr in g_refs[1:]:
            g = g + gr[...]
        m_new = ADAM_B1 * m_ref[...] + (1.0 - ADAM_B1) * g
        v_new = ADAM_B2 * v_ref[...] + (1.0 - ADAM_B2) * (g * g)
        go_ref[...] = g
        mo_ref[...] = m_new
        vo_ref[...] = v_new
        d_ref[...] = -ADAM_LR * ((m_new / c1) / (jnp.sqrt(v_new / c2) + ADAM_EPS) + ADAM_WD * w_ref[...])

    spec = pl.BlockSpec((tr, c), lambda i: (i, 0))
    return pl.pallas_call(
        body, name=name, grid=(r // tr,),
        in_specs=[spec] * (3 + ng), out_specs=[spec] * 4,
        out_shape=[jax.ShapeDtypeStruct((r, c), F32)] * 4,
        compiler_params=_cparams(("parallel",)),
    )(w, m, v, *g_parts)


def _adamw_halves(name, w, m, v, own, other, idx, axis):
    hr, hc = own.shape
    nb, (tr, tc), at = _tiling(hr, hc, 9)
    c1 = 1.0 - ADAM_B1 ** ADAM_STEP
    c2 = 1.0 - ADAM_B2 ** ADAM_STEP

    def body(idx_ref, w_ref, m_ref, v_ref, own_ref, oth_ref, go_ref, d_ref, mo_ref, vo_ref):
        g = jnp.where(pl.program_id(0) == idx_ref[0], own_ref[...], oth_ref[...])
        m_new = ADAM_B1 * m_ref[...] + (1.0 - ADAM_B1) * g
        v_new = ADAM_B2 * v_ref[...] + (1.0 - ADAM_B2) * (g * g)
        go_ref[...] = g
        mo_ref[...] = m_new
        vo_ref[...] = v_new
        d_ref[...] = -ADAM_LR * ((m_new / c1) / (jnp.sqrt(v_new / c2) + ADAM_EPS) + ADAM_WD * w_ref[...])

    per_half = (hc // tc) if axis else (hr // tr)
    if axis:
        full = pl.BlockSpec((tr, tc), lambda hh, i, s: (at(i)[0], hh * per_half + at(i)[1]))
    else:
        full = pl.BlockSpec((tr, tc), lambda hh, i, s: (hh * per_half + at(i)[0], at(i)[1]))
    part = pl.BlockSpec((tr, tc), lambda hh, i, s: at(i))
    grid_spec = pltpu.PrefetchScalarGridSpec(
        num_scalar_prefetch=1, grid=(2, nb), in_specs=[full, full, full, part, part], out_specs=[full] * 4)
    return pl.pallas_call(
        body, name=name, grid_spec=grid_spec, out_shape=[jax.ShapeDtypeStruct(w.shape, F32)] * 4,
        compiler_params=_cparams(("parallel", "parallel")),
    )(idx, w, m, v, own, other)


WEIGHTS = ['norm_w', 'w_in', 's5_a_re', 's5_a_im', 's5_b_re', 's5_b_im', 's5_c_re', 's5_c_im', 's5_d', 's5_log_step',
           's5_w_glu', 's5_b_glu', 'ssd_conv_w', 'ssd_conv_b', 'ssd_dt_bias', 'ssd_a_log', 'ssd_d', 'ssd_norm_w',
           'w_br_s5', 'w_br_ssd', 'w_out', 'ple_norm_w', 'w_ple_gate', 'w_ple_proj', 'final_norm_w']
SHARDED = {'w_in': ((IN_PROJ_DIM, 1024), 0), 's5_w_glu': ((512, 512), 0), 'ssd_conv_w': ((SSD_CONV, SSD_CONV_DIM), 1),
           'w_br_s5': ((512, 1024), 1), 'w_br_ssd': ((1536, 1024), 0), 'w_out': ((1024, 1024), 0),
           'w_ple_gate': ((1024, 1024), 0), 'w_ple_proj': ((256, 1024), 1)}
TRANSPOSED = ('w_in',)
SMALL = [n for n in WEIGHTS if n not in SHARDED]


def _shard_shape(name):
    (r, c), ax = SHARDED[name]
    return (r // N_CHIPS, c) if ax == 0 else (r, c // N_CHIPS)


def _half_axis(name):
    return 0 if (_shard_shape(name)[0] // 2) % 16 == 0 else 1


def _shard2d(name, a):
    r, c = _shard_shape(name)
    return a.reshape(c, r).T if name in TRANSPOSED else a.reshape(r, c)


def _unshard2d(name, a2, shape):
    return (a2.T if name in TRANSPOSED else a2).reshape(shape)


def _unslot(name, a4):
    (r, c), ax = SHARDED[name]
    if ax == 0:
        return a4.reshape(r, c)
    return jnp.transpose(a4, (1, 0, 2)).reshape(r, c)


def _slot(name, full):
    (r, c), ax = SHARDED[name]
    if ax == 0:
        return full.reshape(N_CHIPS, r // N_CHIPS, c)
    return jnp.transpose(full.reshape(r, N_CHIPS, c // N_CHIPS), (1, 0, 2))


GHP = ('s5_b_re', 's5_b_im')


def _view_shape(name):
    if name in ('s5_a_re', 's5_a_im'):
        return (S5_GROUPS, S5_STATE)
    if name in GHP + ('s5_c_re', 's5_c_im'):
        return (S5_GROUPS, S5_GROUP, S5_STATE)
    if name == 'ssd_conv_w':
        return (SSD_CONV, SSD_CONV_DIM // N_CHIPS)
    return (1, {'s5_log_step': S5_GROUPS, 'ssd_conv_b': SSD_CONV_DIM, 'ssd_norm_w': SSD_WIDTH, 's5_d': S5_WIDTH,
                's5_b_glu': S5_WIDTH, 'ssd_dt_bias': SSD_HEADS, 'ssd_a_log': SSD_HEADS, 'ssd_d': SSD_HEADS}.get(name, D_MODEL))


def _view(name, a):
    if name in GHP:
        return jnp.swapaxes(a.reshape(S5_GROUPS, S5_STATE, S5_GROUP), 1, 2)
    return a.reshape(_view_shape(name))


def _unview(name, a, shape):
    return (jnp.swapaxes(a, 1, 2) if name in GHP else a).reshape(shape)


def _adamw_small(ws, ms, vs, gs):
    n = len(ws)
    c1 = 1.0 - ADAM_B1 ** ADAM_STEP
    c2 = 1.0 - ADAM_B2 ** ADAM_STEP

    def body(*refs):
        w_r, m_r, v_r, g_r = (refs[k * n:(k + 1) * n] for k in range(4))
        d_o, m_o, v_o = (refs[k * n:(k + 1) * n] for k in range(4, 7))
        for i in range(n):
            g = g_r[i][...]
            m_new = ADAM_B1 * m_r[i][...] + (1.0 - ADAM_B1) * g
            v_new = ADAM_B2 * v_r[i][...] + (1.0 - ADAM_B2) * (g * g)
            m_o[i][...] = m_new
            v_o[i][...] = v_new
            d_o[i][...] = -ADAM_LR * ((m_new / c1) / (jnp.sqrt(v_new / c2) + ADAM_EPS) + ADAM_WD * w_r[i][...])

    return pl.pallas_call(
        body, name="adamw_small", out_shape=[jax.ShapeDtypeStruct(w.shape, F32) for w in ws] * 3,
        compiler_params=pltpu.CompilerParams(vmem_limit_bytes=VMEM_LIMIT),
    )(*ws, *ms, *vs, *gs)


def _pack_small(vals):
    flat = jnp.concatenate([v.reshape(-1).astype(F32) for v in vals])
    rows = -(-flat.shape[0] // (256 * 128)) * 256
    return jnp.pad(flat, (0, rows * 128 - flat.shape[0])).reshape(rows, 128)


def _unpack_small(packed, shapes):
    flat = packed.reshape(-1)
    out, off = [], 0
    for sh in shapes:
        n = math.prod(sh)
        out.append(flat[off:off + n].reshape(sh))
        off += n
    return out


def kernel(x, p, norm_w, w_in, s5_a_re, s5_a_im, s5_b_re, s5_b_im, s5_c_re, s5_c_im, s5_d, s5_log_step, s5_w_glu, s5_b_glu, ssd_conv_w, ssd_conv_b, ssd_dt_bias, ssd_a_log, ssd_d, ssd_norm_w, w_br_s5, w_br_ssd, w_out, ple_norm_w, w_ple_gate, w_ple_proj, final_norm_w, loss_target, m_norm_w, m_w_in, m_s5_a_re, m_s5_a_im, m_s5_b_re, m_s5_b_im, m_s5_c_re, m_s5_c_im, m_s5_d, m_s5_log_step, m_s5_w_glu, m_s5_b_glu, m_ssd_conv_w, m_ssd_conv_b, m_ssd_dt_bias, m_ssd_a_log, m_ssd_d, m_ssd_norm_w, m_w_br_s5, m_w_br_ssd, m_w_out, m_ple_norm_w, m_w_ple_gate, m_w_ple_proj, m_final_norm_w, v_norm_w, v_w_in, v_s5_a_re, v_s5_a_im, v_s5_b_re, v_s5_b_im, v_s5_c_re, v_s5_c_im, v_s5_d, v_s5_log_step, v_s5_w_glu, v_s5_b_glu, v_ssd_conv_w, v_ssd_conv_b, v_ssd_dt_bias, v_ssd_a_log, v_ssd_d, v_ssd_norm_w, v_w_br_s5, v_w_br_ssd, v_w_out, v_ple_norm_w, v_w_ple_gate, v_w_ple_proj, v_final_norm_w):
    args = locals()
    wl = {n: args[n] for n in WEIGHTS}
    ml = {n: args["m_" + n] for n in WEIGHTS}
    vl = {n: args["v_" + n] for n in WEIGHTS}
    big = [n for n in SHARDED if n != 'ssd_conv_w']
    chip = _chip_index(lax.axis_index("x"), lax.axis_index("y"))
    idx = jnp.stack([lax.axis_index("c"), chip]).astype(jnp.int32)

    axes = [_half_axis(n) for n in big]
    first = ['w_in']
    rest = [n for n in big if n not in first]
    bf_shards = {n: _shard2d(n, wl[n]).astype(BF16) for n in big}
    w_in_t, conv_w4 = _gather_chips([bf_shards[n] for n in first], [_half_axis(n) for n in first],
                                    [_shard2d('ssd_conv_w', wl['ssd_conv_w'])])
    full = {n: wl[n] for n in SMALL}
    full["w_in_t"] = w_in_t
    full['ssd_conv_w'] = _unslot('ssd_conv_w', conv_w4)
    send_sems, recv_sems, thru, lands, token = _gather_start([bf_shards[n] for n in rest], w_in_t)

    def fetch_rest(after):
        own, zones = _gather_wait(send_sems, recv_sems, thru, lands, after)
        return {n: _unslot(n, lax.dynamic_update_slice(z, o[None], (chip, 0, 0)))
                for n, o, z in zip(rest, own, zones, strict=True)}

    full["_late"] = (token, fetch_rest)

    in_flight = []

    def send_rest(g_now):
        in_flight.extend(_scatter_start([_slot(n, g_now[n]) for n in rest], g_now['s5_w_glu']))
        return in_flight[4]

    full["_early"] = send_rest
    loss, grad_x, g = _local_step(x, p[0], loss_target, full)
    for n in TRANSPOSED:
        g[n] = g.pop(n + "_t")

    for n in ('s5_b_re', 's5_b_im'):
        g[n] = g.pop(n + "_ghp")
    small_shapes = [(1, 1)] + [_view_shape(n) for n in SMALL] + [(SSD_CONV, SSD_CONV_DIM)]
    small_pack = _pack_small([loss] + [g[n] for n in SMALL] + [g['ssd_conv_w']])
    first_axes = [_half_axis(n) for n in first]
    slotted = [g[n] for n in first]
    swapped = _swap_halves(slotted, first_axes, small_pack)
    pair = [_pair_sum("pair_sum_" + n, a, b, idx, ax)
            for n, a, b, ax in zip(first, slotted, swapped[:-1], first_axes, strict=True)]
    small_sum = _sum_slots("sum_small", swapped[-1])
    recv = _scatter_halves([pb for pb, _ in pair])
    halves = [_sum_parts("sum_chips_" + n, own, r) for n, (_, own), r in zip(first, pair, recv, strict=True)]
    own_slots, zones = _scatter_wait(*in_flight[:4], g['w_in'])
    chip_sums = [_sum_chips("sum_chips_" + n, a, z, idx) for n, a, z in zip(rest, own_slots, zones, strict=True)]
    swapped2 = _swap_sibling(halves + chip_sums)
    other_halves, sib_sums = swapped2[:len(first)], swapped2[len(first):]

    out_g, out_d, out_m, out_v = {}, {}, {}, {}
    for n, own, oth, ax in zip(first, halves, other_halves, first_axes, strict=True):
        res = _adamw_halves("adamw_" + n, _shard2d(n, wl[n]), _shard2d(n, ml[n]), _shard2d(n, vl[n]), own, oth, idx, ax)
        out_g[n], out_d[n], out_m[n], out_v[n] = (_unshard2d(n, r, wl[n].shape) for r in res)
    for n, own, sib in zip(rest, chip_sums, sib_sums, strict=True):
        res = _adamw("adamw_" + n, _shard2d(n, wl[n]), _shard2d(n, ml[n]), _shard2d(n, vl[n]), [own, sib])
        out_g[n], out_d[n], out_m[n], out_v[n] = (_unshard2d(n, r, wl[n].shape) for r in res)
    sm = _unpack_small(small_sum, small_shapes)
    loss_total = sm[0].reshape(())
    conv_g = lax.dynamic_slice(sm[-1], (0, chip * (SSD_CONV_DIM // N_CHIPS)), (SSD_CONV, SSD_CONV_DIM // N_CHIPS))
    names = SMALL + ['ssd_conv_w']
    grads = sm[1:-1] + [conv_g]
    res = _adamw_small([_view(n, wl[n]) for n in names], [_view(n, ml[n]) for n in names],
                       [_view(n, vl[n]) for n in names], grads)
    for i, n in enumerate(names):
        out_g[n] = _unview(n, grads[i], wl[n].shape)
        out_d[n], out_m[n], out_v[n] = (_unview(n, res[k * len(names) + i], wl[n].shape) for k in range(3))

    return (loss_total, grad_x, *[out_g[n] for n in WEIGHTS], *[out_d[n] for n in WEIGHTS],
            *[out_m[n] for n in WEIGHTS], *[out_v[n] for n in WEIGHTS])
```

```python
import functools
import math

import jax
import jax.numpy as jnp
from jax import lax
from jax.experimental import pallas as pl
from jax.experimental.pallas import tpu as pltpu

F32 = jnp.float32
BF16 = jnp.bfloat16
MESH = pl.DeviceIdType.MESH

D_MODEL = 1024
PLE_DIM = 256
RMS_EPS = 1e-6
S5_WIDTH = 512
S5_GROUP = 16
S5_GROUPS = 32
S5_STATE = 64
S5_N = S5_GROUPS * S5_STATE
S5_LB = 512
S5_NJ = S5_N // S5_LB
S5_TB = 256
S5_LOG_TB = 8
SSD_WIDTH = 1536
SSD_HEADDIM = 64
SSD_HEADS = 24
SSD_GROUPS = 4
SSD_HPG = 6
SSD_STATE = 128
SSD_CONV = 4
SSD_CHUNK = 128
SSD_BC = 512
SSD_CONV_DIM = 2560
GROUP_W = SSD_WIDTH // SSD_GROUPS
N_CHIPS = 4
N_DEV = 8

OFF_XBC, OFF_U5, OFF_Z5, OFF_DT, OFF_G5, OFF_GS, OFF_ZS = 0, 2560, 3072, 3584, 4096, 5120, 6144
DT_W = 512
PROJ_W = 7680
IN_PROJ_DIM = 7192

ADAM_LR, ADAM_B1, ADAM_B2, ADAM_EPS, ADAM_WD, ADAM_STEP = 0.001, 0.9, 0.999, 1e-08, 0.01, 10

VMEM_LIMIT = 56 * 1024 * 1024


def _pick(n, cands):
    for c in cands:
        if n % c == 0:
            return c
    return n


ROW_BLOCK_BYTES = 8 * 1024 * 1024


def _row_tile(r, bytes_per_row):
    for t in (r, 4096, 2048, 1024, 512, 256, 128, 64, 32, 16, 8):
        if t <= r and r % t == 0 and t * bytes_per_row <= ROW_BLOCK_BYTES:
            return t
    return r


def _cparams(sem):
    return pltpu.CompilerParams(dimension_semantics=sem, vmem_limit_bytes=VMEM_LIMIT)


def _dg(a, b, ca, cb):
    return lax.dot_general(a.astype(BF16), b.astype(BF16), (((ca,), (cb,)), ((), ())), preferred_element_type=F32)


@jax.custom_vjp
def dot_nn(a, b):
    return _dg(a, b, 1, 0)


@jax.custom_vjp
def dot_nt(a, b):
    return _dg(a, b, 1, 1)


@jax.custom_vjp
def dot_tn(a, b):
    return _dg(a, b, 0, 0)


dot_nn.defvjp(lambda a, b: (_dg(a, b, 1, 0), (a, b)), lambda r, g: (_dg(g, r[1], 1, 1), _dg(r[0], g, 0, 0)))
dot_nt.defvjp(lambda a, b: (_dg(a, b, 1, 1), (a, b)), lambda r, g: (_dg(g, r[1], 1, 0), _dg(g, r[0], 0, 0)))
dot_tn.defvjp(lambda a, b: (_dg(a, b, 0, 0), (a, b)), lambda r, g: (_dg(r[1], g, 1, 1), _dg(r[0], g, 1, 0)))


MM_VMEM_BUDGET = 30 * 1024 * 1024


def _mm_tiles(m, n, k, sa, sb, so, tn_only=None):
    best, best_key = None, None
    for tm in (1024, 512, 256, 128, 64, 32, 16, 8):
        if m % tm:
            continue
        for tn in (2048, 1536, 1280, 1024, 768, 640, 512, 384, 256, 128):
            if n % tn or (tn_only is not None and tn not in tn_only):
                continue
            for tk in (k, 2048, 1536, 1280, 1024, 768, 512, 256, 128):
                if k % tk or tk > max(k, 128) or (tk == k and k > 2048 and k % 128 == 0):
                    continue
                need = 2 * (tm * tk * sa + tk * tn * sb + tm * tn * so) + (tm * tn * 4 if tk < k else 0)
                if need > MM_VMEM_BUDGET:
                    continue
                key = (tm * tn * tk, tk)
                if best_key is None or key > best_key:
                    best, best_key = (tm, tn, tk), key
    assert best is not None, (m, n, k)
    return best


def _matmul(name, a, b, *, ta=False, tb=False, a_win=None, out_dtype=F32, epilogue=None, epi_rows=(), epi_pars=(),
            epi_outs=(), full_rows=False, epi_accs=(), epi_into=None):
    a_off, a_w = a_win if a_win is not None else (0, a.shape[1])
    if ta:
        kdim, m = a.shape[0], a_w
    else:
        m, kdim = a.shape[0], a_w
    n = b.shape[0] if tb else b.shape[1]
    assert (b.shape[1] if tb else b.shape[0]) == kdim, (name, a.shape, b.shape)
    out_dtypes = list(epi_outs) if epilogue is not None else [out_dtype]
    so = sum(jnp.dtype(d).itemsize for d in out_dtypes) + sum(r.dtype.itemsize for r, _ in epi_rows)
    tn_ok = [n] if full_rows else [t for t in (1024, 512, 256, 128) if all(off % t == 0 for _, off in epi_rows)]
    tm, tn, tk = _mm_tiles(m, n, kdim, a.dtype.itemsize, b.dtype.itemsize, so, tn_ok if epilogue is not None else None)
    nk = kdim // tk
    n_er, n_ep, n_out = len(epi_rows), len(epi_pars), len(out_dtypes)
    if ta:
        assert a_off % tm == 0
        a_spec = pl.BlockSpec((tk, tm), lambda i, j, k: (k, i + a_off // tm))
    else:
        assert a_off % tk == 0
        a_spec = pl.BlockSpec((tm, tk), lambda i, j, k: (i, k + a_off // tk))
    if tb:
        b_spec = pl.BlockSpec((tn, tk), lambda i, j, k: (j, k))
    else:
        b_spec = pl.BlockSpec((tk, tn), lambda i, j, k: (k, j))
    ca, cb = (0 if ta else 1), (1 if tb else 0)

    n_acc = len(epi_accs)
    n_x = 1 if epi_into is not None else 0
    assert not (n_acc or n_x) or full_rows

    def body(a_ref, b_ref, *refs):
        er, ep = refs[:n_er], refs[n_er:n_er + n_ep]
        first_out = n_er + n_ep + n_x
        o_refs = refs[first_out:first_out + n_out]
        s_refs = refs[first_out + n_out:first_out + n_out + n_acc]
        acc = refs[first_out + n_out + n_acc:]

        def finish(c):
            outs = [c] if epilogue is None else epilogue(c, [r[...] for r in er], [p[...] for p in ep])
            if n_acc:
                outs, sums = outs

                @pl.when(pl.program_id(0) == 0)
                def _():
                    for s_ref in s_refs:
                        s_ref[...] = jnp.zeros_like(s_ref)

                for s_ref, s in zip(s_refs, sums, strict=True):
                    s_ref[...] += jnp.broadcast_to(s, s_ref.shape)
            for o_ref, o in zip(o_refs, outs, strict=True):
                o_ref[...] = o.astype(o_ref.dtype)

        if nk == 1:
            finish(_dg(a_ref[...], b_ref[...], ca, cb))
            return
        (acc_ref,) = acc
        k = pl.program_id(2)

        @pl.when(k == 0)
        def _():
            acc_ref[...] = jnp.zeros_like(acc_ref)

        acc_ref[...] += _dg(a_ref[...], b_ref[...], ca, cb)

        @pl.when(k == nk - 1)
        def _():
            finish(acc_ref[...])

    in_specs = [a_spec, b_spec]
    in_specs += [pl.BlockSpec((tm, tn), functools.partial(lambda i, j, k, c: (i, j + c), c=off // tn)) for _, off in epi_rows]
    in_specs += [pl.BlockSpec((p.shape[0], tn), lambda i, j, k: (0, j)) for p in epi_pars]
    out_specs = [pl.BlockSpec((tm, tn), lambda i, j, k: (i, j)) for _ in out_dtypes]
    out_shape = [jax.ShapeDtypeStruct((m, n), d) for d in out_dtypes]
    extra, aliases = [], {}
    if epi_into is not None:
        buf, off, width = epi_into
        assert off % width == 0 and buf.dtype == out_dtypes[0]
        in_specs.append(pl.BlockSpec(memory_space=pl.ANY))
        out_specs[0] = pl.BlockSpec((tm, width), functools.partial(lambda i, j, k, c: (i, c), c=off // width))
        out_shape[0] = jax.ShapeDtypeStruct(buf.shape, buf.dtype)
        extra, aliases = [buf], {2 + n_er + n_ep: 0}
    out_specs += [pl.BlockSpec((r, w), lambda i, j, k: (0, 0)) for r, w in epi_accs]
    out_shape += [jax.ShapeDtypeStruct((r, w), F32) for r, w in epi_accs]
    res = pl.pallas_call(
        body, name=name, grid=(m // tm, n // tn, nk),
        in_specs=in_specs, out_specs=out_specs, out_shape=out_shape, input_output_aliases=aliases,
        scratch_shapes=[pltpu.VMEM((tm, tn), F32)] if nk > 1 else [],
        compiler_params=_cparams(("arbitrary",) * 3 if n_acc else ("parallel", "parallel", "arbitrary")),
    )(a, b, *[r for r, _ in epi_rows], *epi_pars, *extra)
    return res if epilogue is not None else res[0]


BAND = 128


def _band_matmul(name, kind, a, b, *, a_blk0=0, out_dtype=F32, epilogue=None, epi_rows=(), epi_pars=(), epi_outs=(),
                 into=None):
    n_rows = a.shape[0]
    blk = 2 * S5_LB
    tm = _pick(n_rows, (1024, 512, 256))
    if epilogue is not None:
        assert kind == "nt"
        n_er, n_ep = len(epi_rows), len(epi_pars)
        in_specs = [pl.BlockSpec((tm, blk), lambda i, j: (i, j)), pl.BlockSpec((BAND, blk), lambda i, j: (j, 0))]
        in_specs += [pl.BlockSpec((tm, BAND), functools.partial(lambda i, j, c: (i, c + j), c=c0)) for _, c0 in epi_rows]
        in_specs += [pl.BlockSpec((1, BAND), lambda i, j: (0, j)) for _ in epi_pars]
        out_specs = [pl.BlockSpec((tm, BAND), lambda i, j: (i, j)) for _ in epi_outs]
        out_shape = [jax.ShapeDtypeStruct((n_rows, S5_NJ * BAND), d) for d in epi_outs]
        extra, aliases = [], {}
        if into is not None:
            buf, c0 = into
            in_specs.append(pl.BlockSpec(memory_space=pl.ANY))
            out_specs[0] = pl.BlockSpec((tm, BAND), functools.partial(lambda i, j, c: (i, c + j), c=c0))
            out_shape[0] = jax.ShapeDtypeStruct(buf.shape, buf.dtype)
            extra, aliases = [buf], {2 + n_er + n_ep: 0}

        def epi_body(a_ref, b_ref, *refs):
            outs = epilogue(_dg(a_ref[...], b_ref[...], 1, 1), [r[...] for r in refs[:n_er]],
                            [p[...] for p in refs[n_er:n_er + n_ep]])
            for o_ref, o in zip(refs[n_er + n_ep + len(extra):], outs, strict=True):
                o_ref[...] = o.astype(o_ref.dtype)

        return pl.pallas_call(
            epi_body, name=name, grid=(n_rows // tm, S5_NJ), in_specs=in_specs, out_specs=out_specs,
            out_shape=out_shape, input_output_aliases=aliases, compiler_params=_cparams(("parallel", "parallel")),
        )(a, b, *[r for r, _ in epi_rows], *epi_pars, *extra)
    if kind == "nn":
        grid = (n_rows // tm, S5_NJ)
        in_specs = [pl.BlockSpec((tm, BAND), lambda i, j: (i, a_blk0 + j)), pl.BlockSpec((BAND, blk), lambda i, j: (j, 0))]
        out_spec = pl.BlockSpec((tm, blk), lambda i, j: (i, j))
        out_shape = (n_rows, S5_NJ * blk)
        sem = ("parallel", "parallel")

        def body(a_ref, b_ref, o_ref):
            o_ref[...] = _dg(a_ref[...], b_ref[...], 1, 0).astype(o_ref.dtype)
    elif kind == "nt":
        grid = (n_rows // tm, S5_NJ)
        in_specs = [pl.BlockSpec((tm, blk), lambda i, j: (i, j)), pl.BlockSpec((BAND, blk), lambda i, j: (j, 0))]
        out_spec = pl.BlockSpec((tm, BAND), lambda i, j: (i, j))
        out_shape = (n_rows, S5_NJ * BAND)
        sem = ("parallel", "parallel")

        def body(a_ref, b_ref, o_ref):
            o_ref[...] = _dg(a_ref[...], b_ref[...], 1, 1).astype(o_ref.dtype)
    else:
        grid = (S5_NJ, n_rows // tm)
        in_specs = [pl.BlockSpec((tm, BAND), lambda j, k: (k, a_blk0 + j)), pl.BlockSpec((tm, blk), lambda j, k: (k, j))]
        out_spec = pl.BlockSpec((BAND, blk), lambda j, k: (j, 0))
        out_shape = (S5_NJ * BAND, blk)
        sem = ("parallel", "arbitrary")

        def body(a_ref, b_ref, o_ref):
            @pl.when(pl.program_id(1) == 0)
            def _():
                o_ref[...] = jnp.zeros_like(o_ref)

            o_ref[...] += _dg(a_ref[...], b_ref[...], 0, 0)

    return pl.pallas_call(
        body, name=name, grid=grid, in_specs=in_specs, out_specs=out_spec,
        out_shape=jax.ShapeDtypeStruct(out_shape, out_dtype), compiler_params=_cparams(sem),
    )(a, b)


def _rowwise(name, fn, n_rows, tr, row_ins, par_ins, row_outs, acc_outs=(), into=None):
    nr, npar, no, na = len(row_ins), len(par_ins), len(row_outs), len(acc_outs)
    in_specs = []
    for arr, off, w in row_ins:
        assert off % w == 0 and arr.shape[0] == n_rows, (name, arr.shape, off, w)
        in_specs.append(pl.BlockSpec((tr, w), functools.partial(lambda i, c: (i, c), c=off // w)))
    for arr, off, w in par_ins:
        assert off % w == 0
        in_specs.append(pl.BlockSpec((arr.shape[0], w), functools.partial(lambda i, c: (0, c), c=off // w)))
    out_specs = [pl.BlockSpec((tr, w), lambda i: (i, 0)) for w, _ in row_outs]
    out_specs += [pl.BlockSpec((r, w), lambda i: (0, 0)) for r, w in acc_outs]
    out_shape = [jax.ShapeDtypeStruct((n_rows, w), dt) for w, dt in row_outs]
    out_shape += [jax.ShapeDtypeStruct((r, w), F32) for r, w in acc_outs]
    extra, aliases = [], {}
    if into is not None:
        buf, off = into
        w0 = row_outs[0][0]
        assert off % w0 == 0 and buf.dtype == row_outs[0][1]
        in_specs.append(pl.BlockSpec(memory_space=pl.ANY))
        out_specs[0] = pl.BlockSpec((tr, w0), functools.partial(lambda i, c: (i, c), c=off // w0))
        out_shape[0] = jax.ShapeDtypeStruct(buf.shape, buf.dtype)
        extra, aliases = [buf], {nr + npar: 0}
    nx = len(extra)

    def body(*refs):
        rows = [r[...] for r in refs[:nr]]
        pars = [r[...] for r in refs[nr:nr + npar]]
        o_refs = refs[nr + npar + nx:nr + npar + nx + no]
        a_refs = refs[nr + npar + nx + no:]
        outs, accs = fn(rows, pars)
        for o_ref, o in zip(o_refs, outs, strict=True):
            o_ref[...] = o.astype(o_ref.dtype)
        if na:
            @pl.when(pl.program_id(0) == 0)
            def _():
                for a_ref in a_refs:
                    a_ref[...] = jnp.zeros_like(a_ref)

            for a_ref, a in zip(a_refs, accs, strict=True):
                a_ref[...] += jnp.broadcast_to(a, a_ref.shape)

    res = pl.pallas_call(
        body, name=name, grid=(n_rows // tr,),
        in_specs=in_specs, out_specs=out_specs, out_shape=out_shape, input_output_aliases=aliases,
        compiler_params=_cparams(("arbitrary",) if na else ("parallel",)),
    )(*[a for a, _, _ in row_ins], *[a for a, _, _ in par_ins], *extra)
    return res


def _rms(x, w):
    return x * lax.rsqrt(jnp.mean(x * x, axis=-1, keepdims=True) + RMS_EPS) * w


def _gated_norm(y, z, w):
    outs = []
    for g in range(SSD_GROUPS):
        sl = slice(g * GROUP_W, (g + 1) * GROUP_W)
        yg = y[:, sl] * jax.nn.silu(z[:, sl])
        outs.append(yg * lax.rsqrt(jnp.mean(yg * yg, axis=-1, keepdims=True) + RMS_EPS) * w[:, sl])
    return jnp.concatenate(outs, axis=-1)


def _s5_out(yc, u, z, t, d, bg):
    ge = jax.nn.gelu(yc + d * u)
    return ge * jax.nn.sigmoid(t + bg) * jax.nn.silu(z)


def _merge(g5, gs, m5, ms):
    return jax.nn.sigmoid(g5) * m5 + jax.nn.sigmoid(gs) * ms


def _head_loss(h1, pgl, pp, fw, tgt):
    h2 = h1 + jax.nn.sigmoid(pgl) * pp
    err = _rms(h2, fw) - tgt
    per_row = 0.5 * jnp.mean(err * err, axis=-1, keepdims=True)
    return jnp.sum(per_row, axis=0, keepdims=True)


def _s5_disc(a_re, a_im, log_step, b_re2, b_im2, expand):
    step = jnp.exp(log_step)
    mag = jnp.exp(a_re * step)
    lb_re = mag * jnp.cos(a_im * step)
    lb_im = mag * jnp.sin(a_im * step)
    den = a_re * a_re + a_im * a_im
    n_re = lb_re - 1.0
    f_re = (n_re * a_re + lb_im * a_im) / den
    f_im = (lb_im * a_re - n_re * a_im) / den
    hi = lax.Precision.HIGHEST
    fr = jnp.dot(expand, f_re, precision=hi, preferred_element_type=F32)
    fi = jnp.dot(expand, f_im, precision=hi, preferred_element_type=F32)
    return lb_re, lb_im, fr * b_re2 - fi * b_im2, fr * b_im2 + fi * b_re2


def _s5_params_fwd(a_re, a_im, log_step, b_re2, b_im2, expand):
    gp = a_re.shape

    def body(ar, ai, ls, br, bi, ex, pr_ref, pi_ref, bbr_ref, bbi_ref):
        lr, li, bbr, bbi = _s5_disc(ar[...], ai[...], ls[...], br[...], bi[...], ex[...])
        bbr_ref[...] = bbr
        bbi_ref[...] = bbi
        qr, qi = lr, li
        for k in range(S5_LOG_TB):
            pr_ref[k] = qr
            pi_ref[k] = qi
            qr, qi = qr * lr - qi * li, qr * li + qi * lr

    return pl.pallas_call(
        body, name="s5_params_fwd",
        out_shape=(jax.ShapeDtypeStruct((S5_LOG_TB,) + gp, F32), jax.ShapeDtypeStruct((S5_LOG_TB,) + gp, F32),
                   jax.ShapeDtypeStruct(b_re2.shape, F32), jax.ShapeDtypeStruct(b_re2.shape, F32)),
    )(a_re, a_im, log_step, b_re2, b_im2, expand)


def _s5_params_bwd(a_re, a_im, log_step, b_re2, b_im2, expand, d_lr, d_li, d_bbr, d_bbi):
    def body(ar, ai, ls, br, bi, ex, glr, gli, gbr, gbi, dar, dai, dls, dbr, dbi):
        _, vjp = jax.vjp(lambda *p: _s5_disc(*p, ex[...]), ar[...], ai[...], ls[...], br[...], bi[...])
        g = vjp((glr[...], gli[...], gbr[...], gbi[...]))
        for ref, val in zip((dar, dai, dls, dbr, dbi), g, strict=True):
            ref[...] = val

    return pl.pallas_call(
        body, name="s5_params_bwd",
        out_shape=tuple(jax.ShapeDtypeStruct(v.shape, F32) for v in (a_re, a_im, log_step, b_re2, b_im2)),
    )(a_re, a_im, log_step, b_re2, b_im2, expand, d_lr, d_li, d_bbr, d_bbi)


def _scan_block(x_ref, out_ref, lp, edge_pow, cr, ci, reverse, each=None):
    n_g = x_ref.shape[0] // 8
    sub = lax.broadcasted_iota(jnp.int32, (8, S5_LB), 0)
    steps = []
    for sh in (1, 2, 4):
        keep = (sub < 8 - sh) if reverse else (sub >= sh)
        steps.append((8 - sh if reverse else sh, jnp.where(keep, lp[sh - 1:sh, :S5_LB], 0.0),
                      jnp.where(keep, lp[sh - 1:sh, S5_LB:], 0.0)))
    e_r, e_i = edge_pow[:, :S5_LB], edge_pow[:, S5_LB:]
    for r in (range(n_g - 1, -1, -1) if reverse else range(n_g)):
        rows = slice(8 * r, 8 * r + 8)
        xr, xi = x_ref[rows, :S5_LB], x_ref[rows, S5_LB:]
        for by, a_r, a_i in steps:
            pr, pi = pltpu.roll(xr, by, 0), pltpu.roll(xi, by, 0)
            xr, xi = xr + a_r * pr - a_i * pi, xi + a_r * pi + a_i * pr
        xr = xr + e_r * cr - e_i * ci
        xi = xi + e_r * ci + e_i * cr
        out_ref[rows, :S5_LB] = xr
        out_ref[rows, S5_LB:] = xi
        if each is not None:
            each(r, xr, xi)
        cr, ci = (xr[0:1, :], xi[0:1, :]) if reverse else (xr[7:8, :], xi[7:8, :])
    return cr, ci


def _s5_scan_fwd(bu, lam_pow, n_seq, seq_len):
    n_t = seq_len // S5_TB
    blk = 2 * S5_LB

    def body(bu_ref, lp_ref, s_ref, cr, ci, buf):
        @pl.when(pl.program_id(2) == 0)
        def _():
            cr[...] = jnp.zeros_like(cr)
            ci[...] = jnp.zeros_like(ci)

        lp = lp_ref[...]
        cr[...], ci[...] = _scan_block(bu_ref, buf, lp, lp, cr[...], ci[...], False)
        s_ref[...] = buf[...].astype(s_ref.dtype)

    return pl.pallas_call(
        body, name="s5_scan_fwd", grid=(S5_NJ, n_seq, n_t),
        in_specs=[pl.BlockSpec((S5_TB, blk), lambda j, b, t: (b * n_t + t, j)),
                  pl.BlockSpec((S5_LOG_TB, blk), lambda j, b, t: (0, j))],
        out_specs=pl.BlockSpec((S5_TB, blk), lambda j, b, t: (b * n_t + t, j)),
        out_shape=jax.ShapeDtypeStruct(bu.shape, BF16),
        scratch_shapes=[pltpu.VMEM((1, S5_LB), F32), pltpu.VMEM((1, S5_LB), F32), pltpu.VMEM((S5_TB, blk), F32)],
        compiler_params=_cparams(("parallel", "parallel", "arbitrary")),
    )(bu, lam_pow)


def _s5_scan_bwd(ds, s, lam_pow_conj, n_seq, seq_len):
    n_t = seq_len // S5_TB
    blk = 2 * S5_LB
    halo_rows = 16
    halo_per_blk = S5_TB // halo_rows

    def rows(j, b, t):
        return (b * n_t + (n_t - 1 - t), j)

    def halo(j, b, t):
        return (jnp.maximum((b * n_t + (n_t - 1 - t)) * halo_per_blk - 1, 0), j)

    def body(ds_ref, sq_ref, hq_ref, lp_ref, g_ref, dl_ref, cr, ci, buf, s_ref):
        b, t = pl.program_id(1), pl.program_id(2)
        s_ref[...] = sq_ref[...].astype(F32)
        h_last = hq_ref[...].astype(F32)[halo_rows - 1:halo_rows, :]

        @pl.when(t == 0)
        def _():
            cr[...] = jnp.zeros_like(cr)
            ci[...] = jnp.zeros_like(ci)

        @pl.when((b == 0) & (t == 0))
        def _():
            dl_ref[...] = jnp.zeros_like(dl_ref)

        first_blk = t == n_t - 1
        sub = lax.broadcasted_iota(jnp.int32, (8, S5_LB), 0)
        acc = [jnp.zeros((8, S5_LB), F32), jnp.zeros((8, S5_LB), F32)]

        def each(r, gr, gi):
            rows = slice(8 * r, 8 * r + 8)
            if r == 0:
                before_r = jnp.where(first_blk, 0.0, h_last[:, :S5_LB])
                before_i = jnp.where(first_blk, 0.0, h_last[:, S5_LB:])
            else:
                before_r, before_i = s_ref[8 * r - 1:8 * r, :S5_LB], s_ref[8 * r - 1:8 * r, S5_LB:]
            sp_r = jnp.where(sub == 0, before_r, pltpu.roll(s_ref[rows, :S5_LB], 1, 0))
            sp_i = jnp.where(sub == 0, before_i, pltpu.roll(s_ref[rows, S5_LB:], 1, 0))
            acc[0] = acc[0] + gr * sp_r + gi * sp_i
            acc[1] = acc[1] + gi * sp_r - gr * sp_i

        lp = lp_ref[...]
        edge_pow = jnp.concatenate([lp[7 - i:8 - i, :] for i in range(8)], axis=0)
        cr[...], ci[...] = _scan_block(ds_ref, buf, lp, edge_pow, cr[...], ci[...], True, each)
        g_ref[...] = buf[...].astype(g_ref.dtype)
        dl_ref[:, :S5_LB] += jnp.sum(acc[0], axis=0, keepdims=True)
        dl_ref[:, S5_LB:] += jnp.sum(acc[1], axis=0, keepdims=True)

    return pl.pallas_call(
        body, name="s5_scan_bwd", grid=(S5_NJ, n_seq, n_t),
        in_specs=[pl.BlockSpec((S5_TB, blk), rows), pl.BlockSpec((S5_TB, blk), rows),
                  pl.BlockSpec((halo_rows, blk), halo), pl.BlockSpec((S5_LOG_TB, blk), lambda j, b, t: (0, j))],
        out_specs=[pl.BlockSpec((S5_TB, blk), rows), pl.BlockSpec((1, blk), lambda j, b, t: (0, j))],
        out_shape=[jax.ShapeDtypeStruct(ds.shape, BF16), jax.ShapeDtypeStruct((1, ds.shape[1]), F32)],
        scratch_shapes=[pltpu.VMEM((1, S5_LB), F32), pltpu.VMEM((1, S5_LB), F32), pltpu.VMEM((S5_TB, blk), F32),
                        pltpu.VMEM((S5_TB, blk), F32)],
        compiler_params=_cparams(("parallel", "arbitrary", "arbitrary")),
    )(ds, s, s, lam_pow_conj)


CONV_TR = 512
CONV_CW = 512


def _shift_down(x, halo, k):
    if k == 0:
        return x
    row8 = lax.broadcasted_iota(jnp.int32, halo.shape, 0)
    rolled = pltpu.roll(x, k, 0)
    top = jnp.where(row8 < k, pltpu.roll(halo, k, 0), rolled[:8])
    if x.shape[0] == 8:
        return top
    return jnp.concatenate([top, rolled[8:]], axis=0)


def _shift_up(x, halo, k):
    if k == 0:
        return x
    n = x.shape[0]
    row8 = lax.broadcasted_iota(jnp.int32, halo.shape, 0)
    rolled = pltpu.roll(x, n - k, 0)
    bot = jnp.where(row8 >= 8 - k, pltpu.roll(halo, 8 - k, 0), rolled[n - 8:])
    if n == 8:
        return bot
    return jnp.concatenate([rolled[:n - 8], bot], axis=0)


def _conv_pre(x, halo, w, b):
    acc = b + w[SSD_CONV - 1:SSD_CONV, :] * x
    for k in range(SSD_CONV - 1):
        acc = acc + w[k:k + 1, :] * _shift_down(x, halo, SSD_CONV - 1 - k)
    return acc


def _conv_specs(seq_len, col_off):
    lt = seq_len // CONV_TR
    cb = col_off // CONV_CW
    cur = pl.BlockSpec((CONV_TR, CONV_CW), lambda j, i: (i, j + cb))
    prev = pl.BlockSpec((8, CONV_CW), lambda j, i: (jnp.maximum(i * (CONV_TR // 8) - 1, 0), j + cb))
    return lt, cur, prev


def _conv_fwd(proj, conv_w, conv_b, n_rows, seq_len):
    lt, cur, prev = _conv_specs(seq_len, OFF_XBC)

    def body(x_ref, h_ref, w_ref, b_ref, o_ref):
        halo = jnp.where(pl.program_id(1) % lt == 0, 0.0, h_ref[...])
        o_ref[...] = jax.nn.silu(_conv_pre(x_ref[...], halo, w_ref[...], b_ref[...]))

    return pl.pallas_call(
        body, name="ssd_conv_fwd", grid=(SSD_CONV_DIM // CONV_CW, n_rows // CONV_TR),
        in_specs=[cur, prev, pl.BlockSpec((SSD_CONV, CONV_CW), lambda j, i: (0, j)),
                  pl.BlockSpec((1, CONV_CW), lambda j, i: (0, j))],
        out_specs=pl.BlockSpec((CONV_TR, CONV_CW), lambda j, i: (i, j)),
        out_shape=jax.ShapeDtypeStruct((n_rows, SSD_CONV_DIM), F32),
        compiler_params=_cparams(("parallel", "parallel")),
    )(proj, proj, conv_w, conv_b)


def _conv_bwd(name, proj, d_act, conv_w, conv_b, n_rows, seq_len, col_off, dproj):
    width = d_act.shape[1]
    lt, cur, prev = _conv_specs(seq_len, OFF_XBC + col_off)
    n_blk = n_rows // CONV_TR
    cb = (OFF_XBC + col_off) // CONV_CW
    pb = col_off // CONV_CW
    nxt = pl.BlockSpec((8, CONV_CW), lambda j, i: (jnp.minimum((i + 1) * (CONV_TR // 8), n_rows // 8 - 1), j + cb))
    d_cur = pl.BlockSpec((CONV_TR, CONV_CW), lambda j, i: (i, j))
    d_nxt = pl.BlockSpec((8, CONV_CW), lambda j, i: (jnp.minimum((i + 1) * (CONV_TR // 8), n_rows // 8 - 1), j))

    def dsilu(pre):
        sg = jax.nn.sigmoid(pre)
        return sg * (1.0 + pre * (1.0 - sg))

    def body(x_ref, hp_ref, hn_ref, d_ref, dn_ref, w_ref, b_ref, _, dx_ref, dw_ref, db_ref):
        i = pl.program_id(1)
        x, w, b = x_ref[...], w_ref[...], b_ref[...]
        halo_p = jnp.where(i % lt == 0, 0.0, hp_ref[...])
        at_end = i % lt == lt - 1
        dpre = d_ref[...] * dsilu(_conv_pre(x, halo_p, w, b))
        pre_n = _conv_pre(hn_ref[...], x[CONV_TR - 8:, :], w, b)
        dpre_n = jnp.where(at_end, 0.0, dn_ref[...] * dsilu(pre_n))
        dx = w[SSD_CONV - 1:SSD_CONV, :] * dpre
        for k in range(SSD_CONV - 1):
            dx = dx + w[k:k + 1, :] * _shift_up(dpre, dpre_n, SSD_CONV - 1 - k)
        dx_ref[...] = dx.astype(dx_ref.dtype)

        @pl.when(i == 0)
        def _():
            dw_ref[...] = jnp.zeros_like(dw_ref)
            db_ref[...] = jnp.zeros_like(db_ref)

        for k in range(SSD_CONV):
            xs = _shift_down(x, halo_p, SSD_CONV - 1 - k)
            dw_ref[k:k + 1, :] += jnp.sum(dpre * xs, axis=0, keepdims=True)
        db_ref[...] += jnp.sum(dpre, axis=0, keepdims=True)

    return pl.pallas_call(
        body, name=name, grid=(width // CONV_CW, n_blk),
        in_specs=[cur, prev, nxt, d_cur, d_nxt,
                  pl.BlockSpec((SSD_CONV, CONV_CW), lambda j, i: (0, j + pb)),
                  pl.BlockSpec((1, CONV_CW), lambda j, i: (0, j + pb)), pl.BlockSpec(memory_space=pl.ANY)],
        out_specs=[pl.BlockSpec((CONV_TR, CONV_CW), lambda j, i: (i, j + cb)),
                   pl.BlockSpec((SSD_CONV, CONV_CW), lambda j, i: (0, j)),
                   pl.BlockSpec((1, CONV_CW), lambda j, i: (0, j))],
        out_shape=[jax.ShapeDtypeStruct(dproj.shape, dproj.dtype),
                   jax.ShapeDtypeStruct((SSD_CONV, width), F32), jax.ShapeDtypeStruct((1, width), F32)],
        input_output_aliases={7: 0},
        compiler_params=_cparams(("parallel", "arbitrary")),
    )(proj, proj, proj, d_act, d_act, conv_w, conv_b, dproj)


def _split3(x):
    hi = x.astype(BF16)
    r = x - hi.astype(F32)
    mid = r.astype(BF16)
    return hi, mid, (r - mid.astype(F32)).astype(BF16)


def _sel_dot(a, b, a_is_sel):
    dn = (((1,), (0,)), ((), ()))
    if a_is_sel:
        return sum(lax.dot_general(a, t, dn, preferred_element_type=F32) for t in _split3(b))
    return sum(lax.dot_general(t, b, dn, preferred_element_type=F32) for t in _split3(a))


@jax.custom_vjp
def sel_left(sel, sel_t, x):
    return _sel_dot(sel, x, True)


@jax.custom_vjp
def sel_right(x, sel, sel_t):
    return _sel_dot(x, sel, False)


sel_left.defvjp(lambda s, st, x: (_sel_dot(s, x, True), (s, st)),
                lambda r, g: (jnp.zeros_like(r[0]), jnp.zeros_like(r[1]), _sel_dot(r[1], g, True)))
sel_right.defvjp(lambda x, s, st: (_sel_dot(x, s, False), (s, st)),
                 lambda r, g: (_sel_dot(g, r[1], False), jnp.zeros_like(r[0]), jnp.zeros_like(r[1])))


def _ssd_chunk(xs, bm, cm, dtr, st, dtb, alog, dsk, k):
    dt = jax.nn.softplus(dtr + dtb)
    acum = sel_left(k["tri"], k["tri_t"], dt * (-jnp.exp(alog)))
    dt_e = sel_right(dt, k["spread"], k["spread_t"])
    ac_e = sel_right(acum, k["spread"], k["spread_t"])
    al_e = ac_e[SSD_CHUNK - 1:SSD_CHUNK, :]
    dsk_e = sel_right(jnp.broadcast_to(dsk, (8, 128)), k["spread"], k["spread_t"])[0:1, :]
    xdt = xs * dt_e
    acum_t = acum.T
    scores = dot_nt(cm, bm)
    y = dot_nn(cm, st) * jnp.exp(ac_e) + xs * dsk_e
    for j in range(SSD_HPG):
        lmat = jnp.exp(jnp.where(k["causal"], acum[:, j:j + 1] - acum_t[j:j + 1, :], -jnp.inf))
        y = y + dot_nn(scores * lmat, jnp.where(k["head"] == j, xdt, 0.0))
    new = st * jnp.exp(al_e) + dot_tn(bm, xdt * jnp.exp(al_e - ac_e))
    return y, new


def _ssd_consts():
    r = lax.broadcasted_iota(jnp.int32, (SSD_CHUNK, SSD_CHUNK), 0)
    c = lax.broadcasted_iota(jnp.int32, (SSD_CHUNK, SSD_CHUNK), 1)
    hd = jnp.int32(SSD_HEADDIM)
    sr = lax.broadcasted_iota(jnp.int32, (128, GROUP_W), 0)
    sc = lax.div(lax.broadcasted_iota(jnp.int32, (128, GROUP_W), 1), hd)
    tr = lax.div(lax.broadcasted_iota(jnp.int32, (GROUP_W, 128), 0), hd)
    tc = lax.broadcasted_iota(jnp.int32, (GROUP_W, 128), 1)
    return {"tri": (r >= c).astype(BF16), "tri_t": (c >= r).astype(BF16), "causal": r >= c,
            "spread": (sr == sc).astype(BF16), "spread_t": (tr == tc).astype(BF16),
            "head": lax.div(lax.broadcasted_iota(jnp.int32, (SSD_CHUNK, GROUP_W), 1), hd)}


def _ssd_specs(n_c, reverse):
    def cidx(c):
        return n_c - 1 - c if reverse else c

    xs = pl.BlockSpec((SSD_CHUNK, GROUP_W), lambda g, b, c: (b * n_c + cidx(c), g))
    bm = pl.BlockSpec((SSD_CHUNK, SSD_STATE), lambda g, b, c: (b * n_c + cidx(c), SSD_WIDTH // SSD_STATE + g))
    cm = pl.BlockSpec((SSD_CHUNK, SSD_STATE), lambda g, b, c: (b * n_c + cidx(c), (SSD_WIDTH + SSD_BC) // SSD_STATE + g))
    dt = pl.BlockSpec((SSD_CHUNK, 128), lambda g, b, c: (b * n_c + cidx(c), OFF_DT // 128 + g))
    par = pl.BlockSpec((1, 128), lambda g, b, c: (0, g))
    st = pl.BlockSpec((1, 1, 1, SSD_STATE, GROUP_W), lambda g, b, c: (b, cidx(c), g, 0, 0))
    return xs, bm, cm, dt, par, st


def _ssd_fwd(xbc_act, proj, dtb, alog, dsk, n_seq, seq_len):
    n_c = seq_len // SSD_CHUNK
    xs_s, bm_s, cm_s, dt_s, par_s, st_s = _ssd_specs(n_c, False)

    def body(xs_ref, bm_ref, cm_ref, dt_ref, dtb_ref, al_ref, dk_ref, y_ref, st_ref, state):
        @pl.when(pl.program_id(2) == 0)
        def _():
            state[...] = jnp.zeros_like(state)

        prev = state[...]
        st_ref[0, 0, 0] = prev
        y, new = _ssd_chunk(xs_ref[...], bm_ref[...], cm_ref[...], dt_ref[...], prev,
                            dtb_ref[...], al_ref[...], dk_ref[...], _ssd_consts())
        y_ref[...] = y
        state[...] = new

    return pl.pallas_call(
        body, name="ssd_fwd", grid=(SSD_GROUPS, n_seq, n_c),
        in_specs=[xs_s, bm_s, cm_s, dt_s, par_s, par_s, par_s],
        out_specs=[pl.BlockSpec((SSD_CHUNK, GROUP_W), lambda g, b, c: (b * n_c + c, g)), st_s],
        out_shape=[jax.ShapeDtypeStruct((n_seq * seq_len, SSD_WIDTH), F32),
                   jax.ShapeDtypeStruct((n_seq, n_c, SSD_GROUPS, SSD_STATE, GROUP_W), F32)],
        scratch_shapes=[pltpu.VMEM((SSD_STATE, GROUP_W), F32)],
        compiler_params=_cparams(("parallel", "parallel", "arbitrary")),
    )(xbc_act, xbc_act, xbc_act, proj, dtb, alog, dsk)


def _ssd_bwd(xbc_act, proj, states, dy, dtb, alog, dsk, n_seq, seq_len, dproj):
    n_c = seq_len // SSD_CHUNK
    n_rows = n_seq * seq_len
    xs_s, bm_s, cm_s, dt_s, par_s, st_s = _ssd_specs(n_c, True)

    def rows(w, first=0):
        return pl.BlockSpec((SSD_CHUNK, w), lambda g, b, c: (b * n_c + (n_c - 1 - c), first + g))

    def body(xs_ref, bm_ref, cm_ref, dt_ref, st_ref, dy_ref, dtb_ref, al_ref, dk_ref, _,
             dxs_ref, dbm_ref, dcm_ref, ddt_ref, ddtb_ref, dal_ref, ddk_ref, dstate):
        b, c = pl.program_id(1), pl.program_id(2)

        @pl.when(c == 0)
        def _():
            dstate[...] = jnp.zeros_like(dstate)

        @pl.when((b == 0) & (c == 0))
        def _():
            ddtb_ref[...] = jnp.zeros_like(ddtb_ref)
            dal_ref[...] = jnp.zeros_like(dal_ref)
            ddk_ref[...] = jnp.zeros_like(ddk_ref)

        consts = _ssd_consts()
        _, vjp = jax.vjp(
            lambda xs, bm, cm, dtr, prev, dtb, alog, dsk: _ssd_chunk(xs, bm, cm, dtr, prev, dtb, alog, dsk, consts),
            xs_ref[...], bm_ref[...], cm_ref[...], dt_ref[...], st_ref[0, 0, 0], dtb_ref[...], al_ref[...], dk_ref[...])
        dxs, dbm, dcm, ddtr, dprev, ddtb, dal, ddk = vjp((dy_ref[...], dstate[...]))
        dxs_ref[...] = dxs
        dbm_ref[...] = dbm
        dcm_ref[...] = dcm
        ddt_ref[...] = ddtr.astype(ddt_ref.dtype)
        ddtb_ref[...] += ddtb
        dal_ref[...] += dal
        ddk_ref[...] += ddk
        dstate[...] = dprev

    acc = pl.BlockSpec((1, 128), lambda g, b, c: (0, g))
    return pl.pallas_call(
        body, name="ssd_bwd", grid=(SSD_GROUPS, n_seq, n_c),
        in_specs=[xs_s, bm_s, cm_s, dt_s, st_s, rows(GROUP_W), par_s, par_s, par_s, pl.BlockSpec(memory_space=pl.ANY)],
        out_specs=[rows(GROUP_W), rows(SSD_STATE), rows(SSD_STATE), rows(128, OFF_DT // 128), acc, acc, acc],
        out_shape=[jax.ShapeDtypeStruct((n_rows, SSD_WIDTH), F32), jax.ShapeDtypeStruct((n_rows, SSD_BC), F32),
                   jax.ShapeDtypeStruct((n_rows, SSD_BC), F32), jax.ShapeDtypeStruct(dproj.shape, dproj.dtype),
                   jax.ShapeDtypeStruct((1, 512), F32), jax.ShapeDtypeStruct((1, 512), F32),
                   jax.ShapeDtypeStruct((1, 512), F32)],
        input_output_aliases={9: 3},
        scratch_shapes=[pltpu.VMEM((SSD_STATE, GROUP_W), F32)],
        compiler_params=_cparams(("parallel", "arbitrary", "arbitrary")),
    )(xbc_act, xbc_act, xbc_act, proj, states, dy, dtb, alog, dsk, dproj)


def _pad_heads(v):
    return jnp.pad(v.reshape(SSD_GROUPS, SSD_HPG), ((0, 0), (0, 128 - SSD_HPG))).reshape(1, SSD_GROUPS * 128)


def _unpad_heads(v):
    return v.reshape(SSD_GROUPS, 128)[:, :SSD_HPG].reshape(1, SSD_HEADS)


def _state_cols(v):
    re, im = v
    lead = re.shape[:-1]
    re = re.reshape(lead + (S5_NJ, 1, S5_LB))
    im = im.reshape(lead + (S5_NJ, 1, S5_LB))
    return jnp.concatenate([re, im], axis=-2).reshape(lead + (2 * S5_N,))


def _state_uncols(v):
    lead = v.shape[:-1]
    v = v.reshape(lead + (S5_NJ, 2, S5_LB))
    return v[..., 0, :].reshape(lead + (S5_N,)), v[..., 1, :].reshape(lead + (S5_N,))


GROUPS_PER_BAND = BAND // S5_GROUP


def _band(w2_re, w2_im):
    gh = S5_GROUPS * S5_GROUP
    rg = ((jnp.arange(gh) // S5_GROUP) % GROUPS_PER_BAND)[:, None, None]
    cg = jnp.arange(GROUPS_PER_BAND)[None, :, None]
    parts = [jnp.where(rg == cg, v[:, None, :], 0.0).reshape(gh, S5_LB) for v in (w2_re, w2_im)]
    return jnp.concatenate(parts, axis=1)


def _band_take(wb):
    gh = S5_GROUPS * S5_GROUP
    w4 = wb.reshape(gh, 2, GROUPS_PER_BAND, S5_STATE)
    sel = w4[jnp.arange(gh), :, (jnp.arange(gh) // S5_GROUP) % GROUPS_PER_BAND, :]
    return sel[:, 0, :], sel[:, 1, :]


W_IN_SHARD = IN_PROJ_DIM // N_CHIPS
W_IN_SEGS = ((0, 512, OFF_U5), (512, 1024, OFF_Z5), (1024, 2560, OFF_ZS), (2560, 5120, OFF_XBC), (5144, 7192, OFF_G5))
DT_ROWS = (5120, 5144)


def _w_in_pieces():
    runs = []
    segs = list(W_IN_SEGS) + [(DT_ROWS[0] + SSD_HPG * g, DT_ROWS[0] + SSD_HPG * (g + 1), OFF_DT + 128 * g)
                              for g in range(SSD_GROUPS)]
    for lo, hi, off in segs:
        for j in range(N_CHIPS):
            s, e = max(lo, j * W_IN_SHARD), min(hi, (j + 1) * W_IN_SHARD)
            if s < e:
                runs.append((j, s - j * W_IN_SHARD, off + s - lo, e - s))
    return runs


RELAYOUT_LANES = 256


def _pad_w_in_t(a4):
    runs = _w_in_pieces()

    def body(a_ref, o_ref):
        o_ref[pl.ds(OFF_DT, DT_W), :] = jnp.zeros((DT_W, RELAYOUT_LANES), o_ref.dtype)
        for j, src, dst, n in runs:
            o_ref[pl.ds(dst, n), :] = a_ref[j, pl.ds(src, n), :]

    return pl.pallas_call(
        body, name="w_in_to_padded", grid=(D_MODEL // RELAYOUT_LANES,),
        in_specs=[pl.BlockSpec((N_CHIPS, W_IN_SHARD, RELAYOUT_LANES), lambda i: (0, 0, i))],
        out_specs=pl.BlockSpec((PROJ_W, RELAYOUT_LANES), lambda i: (0, i)),
        out_shape=jax.ShapeDtypeStruct((PROJ_W, D_MODEL), a4.dtype),
        compiler_params=_cparams(("parallel",)),
    )(a4)


def _unpad_w_in_t(wp):
    runs = _w_in_pieces()

    def body(p_ref, o_ref):
        for j, dst, src, n in runs:
            o_ref[j, pl.ds(dst, n), :] = p_ref[pl.ds(src, n), :]

    return pl.pallas_call(
        body, name="w_in_from_padded", grid=(D_MODEL // RELAYOUT_LANES,),
        in_specs=[pl.BlockSpec((PROJ_W, RELAYOUT_LANES), lambda i: (0, i))],
        out_specs=pl.BlockSpec((N_CHIPS, W_IN_SHARD, RELAYOUT_LANES), lambda i: (0, 0, i)),
        out_shape=jax.ShapeDtypeStruct((N_CHIPS, W_IN_SHARD, D_MODEL), wp.dtype),
        compiler_params=_cparams(("parallel",)),
    )(wp)


def _local_step(x, p, tgt, w):
    n_seq, seq_len, _ = x.shape
    n_rows = n_seq * seq_len
    tr = 512
    x2 = x.reshape(n_rows, D_MODEL)
    p2 = p.reshape(n_rows, PLE_DIM)
    t2 = tgt.reshape(n_rows, D_MODEL)
    row = functools.partial(_rowwise, n_rows=n_rows, tr=tr)

    w_pad_t = _pad_w_in_t(w["w_in_t"])
    norm_w = w["norm_w"].reshape(1, D_MODEL)
    ple_norm_w = w["ple_norm_w"].reshape(1, D_MODEL)
    final_w = w["final_norm_w"].reshape(1, D_MODEL)
    s5_d = w["s5_d"].reshape(1, S5_WIDTH)
    b_glu = w["s5_b_glu"].reshape(1, S5_WIDTH)
    conv_w = w["ssd_conv_w"].reshape(SSD_CONV, SSD_CONV_DIM)
    conv_b = w["ssd_conv_b"].reshape(1, SSD_CONV_DIM)
    ssd_norm_w = w["ssd_norm_w"].reshape(1, SSD_WIDTH)
    dtb, alog, dsk = (_pad_heads(w[k].reshape(1, SSD_HEADS)) for k in ("ssd_dt_bias", "ssd_a_log", "ssd_d"))

    gh = S5_GROUPS * S5_GROUP
    a_re = w["s5_a_re"].reshape(S5_GROUPS, S5_STATE)
    a_im = w["s5_a_im"].reshape(S5_GROUPS, S5_STATE)
    log_step = w["s5_log_step"].reshape(S5_GROUPS, 1)
    b_re2 = jnp.transpose(w["s5_b_re"].reshape(S5_GROUPS, S5_STATE, S5_GROUP), (0, 2, 1)).reshape(gh, S5_STATE)
    b_im2 = jnp.transpose(w["s5_b_im"].reshape(S5_GROUPS, S5_STATE, S5_GROUP), (0, 2, 1)).reshape(gh, S5_STATE)
    expand = (jnp.arange(gh)[:, None] // S5_GROUP == jnp.arange(S5_GROUPS)[None, :]).astype(F32)
    pow_re, pow_im, bb_re2, bb_im2 = _s5_params_fwd(a_re, a_im, log_step, b_re2, b_im2, expand)
    lam_pow = _state_cols((pow_re.reshape(S5_LOG_TB, S5_N), pow_im.reshape(S5_LOG_TB, S5_N)))
    lam_pow_conj = _state_cols((pow_re.reshape(S5_LOG_TB, S5_N), -pow_im.reshape(S5_LOG_TB, S5_N)))
    bb_band = _band(bb_re2, bb_im2).astype(BF16)
    c_band = _band(w["s5_c_re"].reshape(gh, S5_STATE), -w["s5_c_im"].reshape(gh, S5_STATE)).astype(BF16)

    late = w.get("_late")
    (hn,) = row("rms_in", lambda r, q: ([_rms(r[0], q[0])], []), row_ins=[(x2, 0, D_MODEL)],
                par_ins=[(norm_w, 0, D_MODEL)] + ([(late[0], 0, 128)] if late else []), row_outs=[(D_MODEL, BF16)])
    proj = _matmul("mm_proj", hn, w_pad_t, tb=True)
    bu = _band_matmul("mm_s5_bu", "nn", proj, bb_band, a_blk0=OFF_U5 // BAND)
    s = _s5_scan_fwd(bu, lam_pow, n_seq, seq_len)
    yc, ge = _band_matmul("mm_s5_y", "nt", s, c_band, epilogue=lambda c, r, q: [c, jax.nn.gelu(c + q[0] * r[0])],
                          epi_rows=[(proj, OFF_U5 // BAND)], epi_pars=[s5_d], epi_outs=[F32, BF16])
    if late:
        w = {**w, **late[1](ge)}
    tg, y5 = _matmul("mm_s5_glu", ge, w["s5_w_glu"], epilogue=lambda c, r, q: [c, _s5_out(r[0], r[1], r[2], c, *q)],
                     epi_rows=[(yc, 0), (proj, OFF_U5), (proj, OFF_Z5)], epi_pars=[s5_d, b_glu], epi_outs=[F32, BF16])
    s5_rows = [(yc, 0, S5_WIDTH), (proj, OFF_U5, S5_WIDTH), (proj, OFF_Z5, S5_WIDTH), (tg, 0, S5_WIDTH)]
    s5_pars = [(s5_d, 0, S5_WIDTH), (b_glu, 0, S5_WIDTH)]

    xbc_act = _conv_fwd(proj, conv_w, conv_b, n_rows, seq_len)
    y_ssd, states = _ssd_fwd(xbc_act, proj, dtb, alog, dsk, n_seq, seq_len)
    gn_rows = [(y_ssd, 0, SSD_WIDTH), (proj, OFF_ZS, SSD_WIDTH)]
    (yss,) = row("ssd_gate", lambda r, q: ([_gated_norm(r[0], r[1], q[0])], []), row_ins=gn_rows,
                 par_ins=[(ssd_norm_w, 0, SSD_WIDTH)], row_outs=[(SSD_WIDTH, BF16)])

    m5 = _matmul("mm_br_s5", y5, w["w_br_s5"])
    ms, merged = _matmul("mm_br_ssd", yss, w["w_br_ssd"], epilogue=lambda c, r, q: [c, _merge(r[0], r[1], r[2], c)],
                         epi_rows=[(proj, OFF_G5), (proj, OFF_GS), (m5, 0)], epi_outs=[F32, BF16])
    mg_rows = [(proj, OFF_G5, D_MODEL), (proj, OFF_GS, D_MODEL), (m5, 0, D_MODEL), (ms, 0, D_MODEL)]

    def resid_norm(c, r, q):
        h1_ = r[0] + c
        return [h1_, _rms(h1_, q[0])]

    h1, hp = _matmul("mm_out", merged, w["w_out"], epilogue=resid_norm, epi_rows=[(x2, 0)], epi_pars=[ple_norm_w],
                     epi_outs=[F32, BF16], full_rows=True)
    pp = _matmul("mm_ple_proj", p2, w["w_ple_proj"])

    def head_fn(pgl_, r, q):
        h1_, pp_, tgt_ = r
        loss, vjp = jax.vjp(lambda a, b, c, f: _head_loss(a, b, c, f, tgt_), h1_, pgl_, pp_, q[0])
        dh1_, dpgl_, dpp_, dfw_ = vjp(jnp.ones_like(loss))
        return [dh1_, dpgl_, dpp_], [loss, dfw_]

    dh2, dpgl, dpp, loss_acc, d_final_w = _matmul(
        "mm_ple_gate_head", hp, w["w_ple_gate"], epilogue=head_fn, epi_rows=[(h1, 0), (pp, 0), (t2, 0)],
        epi_pars=[final_w], epi_outs=[F32, BF16, BF16], epi_accs=[(1, 128), (1, D_MODEL)], full_rows=True)
    loss = loss_acc[0, 0]

    g = {}
    g["final_norm_w"] = d_final_w
    g["w_ple_gate"] = _matmul("mm_d_w_ple_gate", hp, dpgl, ta=True)
    g["w_ple_proj"] = _matmul("mm_d_w_ple_proj", p2, dpp, ta=True)

    def ple_norm_bwd(dhp_, r, q):
        h1_, dh2_ = r
        _, vjp = jax.vjp(_rms, h1_, q[0])
        dh, dw = vjp(dhp_)
        dh = dh + dh2_
        return [dh, dh], [dw]

    dh1, dh1_b, g["ple_norm_w"] = _matmul(
        "mm_d_hp", dpgl, w["w_ple_gate"], tb=True, epilogue=ple_norm_bwd, epi_rows=[(h1, 0), (dh2, 0)],
        epi_pars=[ple_norm_w], epi_outs=[F32, BF16], epi_accs=[(1, D_MODEL)], full_rows=True)
    g["w_out"] = _matmul("mm_d_w_out", merged, dh1_b, ta=True)

    dproj = lax.empty((n_rows, PROJ_W), BF16)

    def merge_bwd(dmerged, r, q):
        sg5, sgs = jax.nn.sigmoid(r[0]), jax.nn.sigmoid(r[1])
        d_gates = jnp.concatenate([dmerged * r[2] * sg5 * (1.0 - sg5), dmerged * r[3] * sgs * (1.0 - sgs)], axis=1)
        return [d_gates, dmerged * sg5, dmerged * sgs]

    dproj, dm5, dms = _matmul(
        "mm_d_merged", dh1_b, w["w_out"], tb=True, epilogue=merge_bwd,
        epi_rows=[(proj, OFF_G5), (proj, OFF_GS), (m5, 0), (ms, 0)], epi_outs=[BF16, BF16, BF16], full_rows=True,
        epi_into=(dproj, OFF_G5, 2 * D_MODEL))
    g["w_br_s5"] = _matmul("mm_d_w_br_s5", y5, dm5, ta=True)
    g["w_br_ssd"] = _matmul("mm_d_w_br_ssd", yss, dms, ta=True)
    dy5 = _matmul("mm_d_y5", dm5, w["w_br_s5"], tb=True)
    dyss = _matmul("mm_d_yss", dms, w["w_br_ssd"], tb=True)

    def s5_out_bwd_a(r, q):
        yc_, u_, z_, t_, dy_ = r
        d_, bg_ = q
        ge_ = jax.nn.gelu(yc_ + d_ * u_)
        _, vjp = jax.vjp(lambda a, z, t, b: a * jax.nn.sigmoid(t + b) * jax.nn.silu(z), ge_, z_, t_, bg_)
        dge, dz, dt_, dbg = vjp(dy_)
        return [dz, dge, dt_], [dbg]

    dproj, dge_a, dtg, g["s5_b_glu"] = row(
        "s5_out_bwd_a", s5_out_bwd_a, row_ins=s5_rows + [(dy5, 0, S5_WIDTH)], par_ins=s5_pars,
        row_outs=[(S5_WIDTH, BF16), (S5_WIDTH, F32), (S5_WIDTH, BF16)], acc_outs=[(1, S5_WIDTH)],
        into=(dproj, OFF_Z5))
    g["s5_w_glu"] = _matmul("mm_d_w_glu", ge, dtg, ta=True)
    early = w["_early"](g) if "_early" in w else None
    dge_b = _matmul("mm_d_ge", dtg, w["s5_w_glu"], tb=True)

    def s5_out_bwd_b(r, q):
        yc_, u_, da_, db_ = r
        _, vjp = jax.vjp(lambda yc, u, d: jax.nn.gelu(yc + d * u), yc_, u_, q[0])
        dyc_, du_, dd_ = vjp(da_ + db_)
        return [dyc_, du_], [dd_]

    dyc, du5_a, g["s5_d"] = row(
        "s5_out_bwd_b", s5_out_bwd_b,
        row_ins=[(yc, 0, S5_WIDTH), (proj, OFF_U5, S5_WIDTH), (dge_a, 0, S5_WIDTH), (dge_b, 0, S5_WIDTH)],
        par_ins=[(s5_d, 0, S5_WIDTH)] + ([(early, 0, 128)] if early is not None else []),
        row_outs=[(S5_WIDTH, BF16), (S5_WIDTH, F32)], acc_outs=[(1, S5_WIDTH)])
    d_c_band = _band_matmul("mm_d_c", "tn", dyc, s)
    ds = _band_matmul("mm_d_s", "nn", dyc, c_band)
    dbu, d_lam = _s5_scan_bwd(ds, s, lam_pow_conj, n_seq, seq_len)
    d_bb_band = _band_matmul("mm_d_bb", "tn", proj, dbu, a_blk0=OFF_U5 // BAND)
    (dproj,) = _band_matmul("mm_d_u5", "nt", dbu, bb_band, epilogue=lambda c, r, q: [c + r[0]],
                            epi_rows=[(du5_a, 0)], epi_outs=[BF16], into=(dproj, OFF_U5 // BAND))

    d_lr, d_li = _state_uncols(d_lam)
    d_bbr, d_bbi = _band_take(d_bb_band)
    d_are, d_aim, d_ls, d_br2, d_bi2 = _s5_params_bwd(
        a_re, a_im, log_step, b_re2, b_im2, expand,
        d_lr.reshape(S5_GROUPS, S5_STATE), d_li.reshape(S5_GROUPS, S5_STATE), d_bbr, d_bbi)
    g["s5_a_re"], g["s5_a_im"], g["s5_log_step"] = d_are, d_aim, d_ls
    g["s5_b_re_ghp"], g["s5_b_im_ghp"] = d_br2, d_bi2
    d_cr, d_ci = _band_take(d_c_band)
    g["s5_c_re"], g["s5_c_im"] = d_cr, -d_ci

    def gate_bwd(r, q):
        _, vjp = jax.vjp(_gated_norm, r[0], r[1], q[0])
        dy_, dz_, dw_ = vjp(r[2])
        return [dz_, dy_], [dw_]

    dproj, dy_ssd, g["ssd_norm_w"] = row(
        "ssd_gate_bwd", gate_bwd, row_ins=gn_rows + [(dyss, 0, SSD_WIDTH)], par_ins=[(ssd_norm_w, 0, SSD_WIDTH)],
        row_outs=[(SSD_WIDTH, BF16), (SSD_WIDTH, F32)], acc_outs=[(1, SSD_WIDTH)], into=(dproj, OFF_ZS))
    dxs, dbm, dcm, dproj, d_dtb, d_alog, d_dsk = _ssd_bwd(xbc_act, proj, states, dy_ssd, dtb, alog, dsk, n_seq, seq_len,
                                                       dproj)
    g["ssd_dt_bias"], g["ssd_a_log"], g["ssd_d"] = _unpad_heads(d_dtb), _unpad_heads(d_alog), _unpad_heads(d_dsk)
    conv_dw, conv_db = [], []
    for nm, d_act, off in (("x", dxs, 0), ("b", dbm, SSD_WIDTH), ("c", dcm, SSD_WIDTH + SSD_BC)):
        dproj, dw_, db_ = _conv_bwd("ssd_conv_bwd_" + nm, proj, d_act, conv_w, conv_b, n_rows, seq_len, off, dproj)
        conv_dw.append(dw_)
        conv_db.append(db_)
    g["ssd_conv_w"] = jnp.concatenate(conv_dw, axis=1)
    g["ssd_conv_b"] = jnp.concatenate(conv_db, axis=1)

    g["w_in_t"] = _unpad_w_in_t(_matmul("mm_d_w_in", dproj, hn, ta=True))
    def norm_bwd(dhn_, r, q):
        x_, dh1_ = r
        _, vjp = jax.vjp(_rms, x_, q[0])
        dx_, dw_ = vjp(dhn_)
        return [dx_ + dh1_], [dw_]

    dx, g["norm_w"] = _matmul("mm_d_hn", dproj, w_pad_t, epilogue=norm_bwd, epi_rows=[(x2, 0), (dh1, 0)],
                              epi_pars=[norm_w], epi_outs=[F32], epi_accs=[(1, D_MODEL)], full_rows=True)
    return loss, dx.reshape(x.shape), g


HBM = pl.BlockSpec(memory_space=pltpu.HBM)


def _chip_index(x, y):
    return 2 * x + y


def _half(shape2d, axis, which):
    h = shape2d[axis] // 2
    sl = pl.ds(pl.multiple_of(which * h, 128 if axis else 8), h)
    return (slice(None), sl) if axis else (sl, slice(None))


def _gather_chips(split, axes, whole):
    ns, nw = len(split), len(whole)
    n = ns + nw

    def body(*refs):
        ins, outs = refs[:n], refs[n:2 * n]
        ici_send, ici_recv, d2d_send, d2d_recv, local_sems = refs[2 * n:]
        x, y, c = lax.axis_index("x"), lax.axis_index("y"), lax.axis_index("c")
        me = _chip_index(x, y)
        sibling = (x, y, 1 - c)
        peers = [(1 - x, y), (x, 1 - y), (1 - x, 1 - y)]

        def half(t, which):
            return _half(split[t].shape, axes[t], which)

        copies = []
        for t in range(n):
            loc = pltpu.make_async_copy(ins[t], outs[t].at[me], local_sems.at[t])
            loc.start()
            copies.append(loc)

        def ici(t, k, slot):
            px, py = peers[k]
            if t < ns:
                src, dst = ins[t].at[half(t, c)], outs[t].at[(slot,) + half(t, c)]
            else:
                src, dst = ins[t], outs[t].at[slot]
            return pltpu.make_async_remote_copy(src_ref=src, dst_ref=dst, send_sem=ici_send.at[t, k],
                                                recv_sem=ici_recv.at[t, k], device_id=(px, py, c), device_id_type=MESH)

        def d2d(t, k, which):
            rows = outs[t].at[(_chip_index(*peers[k]),) + half(t, which)]
            return pltpu.make_async_remote_copy(src_ref=rows, dst_ref=rows, send_sem=d2d_send.at[t, k],
                                                recv_sem=d2d_recv.at[t, k], device_id=sibling, device_id_type=MESH)

        sends = []
        for t in range(n):
            for k in range(3):
                cp = ici(t, k, me)
                cp.start()
                sends.append(cp)
        for t in range(n):
            for k in range(3):
                ici(t, k, _chip_index(*peers[k])).wait_recv()
                if t < ns:
                    cp = d2d(t, k, c)
                    cp.start()
                    sends.append(cp)
        for t in range(ns):
            for k in range(3):
                d2d(t, k, 1 - c).wait_recv()
        for cp in sends:
            cp.wait_send()
        for cp in copies:
            cp.wait()

    arrays = list(split) + list(whole)
    return pl.pallas_call(
        body, name="gather_weights",
        in_specs=[HBM] * n, out_specs=[HBM] * n,
        out_shape=[jax.ShapeDtypeStruct((N_CHIPS,) + a.shape, a.dtype) for a in arrays],
        scratch_shapes=[pltpu.SemaphoreType.DMA((n, 3)), pltpu.SemaphoreType.DMA((n, 3)),
                        pltpu.SemaphoreType.DMA((ns, 3)), pltpu.SemaphoreType.DMA((ns, 3)),
                        pltpu.SemaphoreType.DMA((n,))],
    )(*arrays)


SEM = pl.BlockSpec(memory_space=pltpu.SEMAPHORE)
DATAFLOW = pltpu.SideEffectType.DATAFLOW_SIDE_EFFECTING


def _gather_start(shards, after):
    n = len(shards)

    def body(*refs):
        ins, lands = refs[:n], refs[n:2 * n]
        send_sems, recv_sems = refs[2 * n + 1], refs[2 * n + 2]
        token = refs[-1]
        x, y, c = lax.axis_index("x"), lax.axis_index("y"), lax.axis_index("c")
        me = _chip_index(x, y)
        for t in range(n):
            for k, (px, py) in enumerate([(1 - x, y), (x, 1 - y), (1 - x, 1 - y)]):
                pltpu.make_async_remote_copy(
                    src_ref=ins[t], dst_ref=lands[t].at[me], send_sem=send_sems.at[3 * t + k],
                    recv_sem=recv_sems.at[3 * t + k],
                    device_id=(px, py, c), device_id_type=MESH).start()
        token[...] = jnp.zeros_like(token)

    zones = [lax.empty((N_CHIPS,) + a.shape, a.dtype) for a in shards]
    res = pl.pallas_call(
        body, name="gather_rest_start",
        out_shape=(pltpu.SemaphoreType.DMA((3 * n,)), pltpu.SemaphoreType.DMA((3 * n,)),
                   *[pltpu.HBM(a.shape, a.dtype) for a in shards], *[pltpu.HBM(z.shape, z.dtype) for z in zones],
                   jax.ShapeDtypeStruct((8, 128), F32)),
        in_specs=[HBM] * (2 * n) + [pl.BlockSpec(memory_space=pl.ANY)],
        out_specs=(SEM, SEM, *[HBM] * (2 * n), pl.BlockSpec(memory_space=pltpu.VMEM)),
        input_output_aliases={t: 2 + t for t in range(2 * n)},
        compiler_params=pltpu.CompilerParams(has_side_effects=DATAFLOW),
    )(*[pltpu.with_memory_space_constraint(a, pltpu.HBM) for a in shards],
      *[pltpu.with_memory_space_constraint(z, pltpu.HBM) for z in zones], after)
    return res[0], res[1], list(res[2:2 + n]), list(res[2 + n:2 + 2 * n]), res[-1]


def _gather_wait(send_sems, recv_sems, thru, lands, after):
    n = len(thru)

    def body(*refs):
        ins, zones = refs[:n], refs[n:2 * n]
        s_sems, r_sems = refs[2 * n], refs[2 * n + 1]
        x, y, c = lax.axis_index("x"), lax.axis_index("y"), lax.axis_index("c")
        for t in range(n):
            for k, (px, py) in enumerate([(1 - x, y), (x, 1 - y), (1 - x, 1 - y)]):
                cp = pltpu.make_async_remote_copy(
                    src_ref=ins[t], dst_ref=zones[t].at[_chip_index(px, py)], send_sem=s_sems.at[3 * t + k],
                    recv_sem=r_sems.at[3 * t + k], device_id=(px, py, c), device_id_type=MESH)
                cp.wait_send()
                cp.wait_recv()

    res = pl.pallas_call(
        body, name="gather_rest_wait",
        out_shape=(*[pltpu.HBM(a.shape, a.dtype) for a in thru], *[pltpu.HBM(z.shape, z.dtype) for z in lands]),
        in_specs=[HBM] * (2 * n) + [SEM, SEM, pl.BlockSpec(memory_space=pl.ANY)], out_specs=[HBM] * (2 * n),
        input_output_aliases={t: t for t in range(2 * n)},
        compiler_params=pltpu.CompilerParams(has_side_effects=DATAFLOW),
    )(*thru, *lands, send_sems, recv_sems, after)
    return list(res[:n]), list(res[n:])


def _scatter_start(slotted, after):
    n = len(slotted)

    def body(*refs):
        ins, lands = refs[:n], refs[n:2 * n]
        send_sems, recv_sems = refs[2 * n + 1], refs[2 * n + 2]
        token = refs[-1]
        x, y, c = lax.axis_index("x"), lax.axis_index("y"), lax.axis_index("c")
        for t in range(n):
            for k, (px, py) in enumerate([(1 - x, y), (x, 1 - y), (1 - x, 1 - y)]):
                pltpu.make_async_remote_copy(
                    src_ref=ins[t].at[_chip_index(px, py)], dst_ref=lands[t].at[k], send_sem=send_sems.at[3 * t + k],
                    recv_sem=recv_sems.at[3 * t + k], device_id=(px, py, c), device_id_type=MESH).start()
        token[...] = jnp.zeros_like(token)

    zones = [lax.empty((3,) + a.shape[1:], a.dtype) for a in slotted]
    res = pl.pallas_call(
        body, name="scatter_rest_start",
        out_shape=(pltpu.SemaphoreType.DMA((3 * n,)), pltpu.SemaphoreType.DMA((3 * n,)),
                   *[pltpu.HBM(a.shape, a.dtype) for a in slotted], *[pltpu.HBM(z.shape, z.dtype) for z in zones],
                   jax.ShapeDtypeStruct((8, 128), F32)),
        in_specs=[HBM] * (2 * n) + [pl.BlockSpec(memory_space=pl.ANY)],
        out_specs=(SEM, SEM, *[HBM] * (2 * n), pl.BlockSpec(memory_space=pltpu.VMEM)),
        input_output_aliases={t: 2 + t for t in range(2 * n)},
        compiler_params=pltpu.CompilerParams(has_side_effects=DATAFLOW),
    )(*[pltpu.with_memory_space_constraint(a, pltpu.HBM) for a in slotted],
      *[pltpu.with_memory_space_constraint(z, pltpu.HBM) for z in zones], after)
    return res[0], res[1], list(res[2:2 + n]), list(res[2 + n:2 + 2 * n]), res[-1]


def _scatter_wait(send_sems, recv_sems, thru, lands, after):
    n = len(thru)

    def body(*refs):
        ins, zones = refs[:n], refs[n:2 * n]
        s_sems, r_sems = refs[2 * n], refs[2 * n + 1]
        x, y, c = lax.axis_index("x"), lax.axis_index("y"), lax.axis_index("c")
        for t in range(n):
            for k, (px, py) in enumerate([(1 - x, y), (x, 1 - y), (1 - x, 1 - y)]):
                cp = pltpu.make_async_remote_copy(
                    src_ref=ins[t].at[_chip_index(px, py)], dst_ref=zones[t].at[k], send_sem=s_sems.at[3 * t + k],
                    recv_sem=r_sems.at[3 * t + k], device_id=(px, py, c), device_id_type=MESH)
                cp.wait_send()
                cp.wait_recv()

    res = pl.pallas_call(
        body, name="scatter_rest_wait",
        out_shape=(*[pltpu.HBM(a.shape, a.dtype) for a in thru], *[pltpu.HBM(z.shape, z.dtype) for z in lands]),
        in_specs=[HBM] * (2 * n) + [SEM, SEM, pl.BlockSpec(memory_space=pl.ANY)], out_specs=[HBM] * (2 * n),
        input_output_aliases={t: t for t in range(2 * n)},
        compiler_params=pltpu.CompilerParams(has_side_effects=DATAFLOW),
    )(*thru, *lands, send_sems, recv_sems, after)
    return list(res[:n]), list(res[n:])


def _sum_chips(name, slotted, recv, idx):
    _, r, c = slotted.shape
    tr = _row_tile(r, 5 * c * 4)

    def body(idx_ref, own_ref, r_ref, o_ref):
        acc = own_ref[0]
        for k in range(3):
            acc = acc + r_ref[k]
        o_ref[...] = acc

    grid_spec = pltpu.PrefetchScalarGridSpec(
        num_scalar_prefetch=1, grid=(r // tr,),
        in_specs=[pl.BlockSpec((1, tr, c), lambda i, s: (s[1], i, 0)), pl.BlockSpec((3, tr, c), lambda i, s: (0, i, 0))],
        out_specs=pl.BlockSpec((tr, c), lambda i, s: (i, 0)))
    return pl.pallas_call(
        body, name=name, grid_spec=grid_spec, out_shape=jax.ShapeDtypeStruct((r, c), F32),
        compiler_params=_cparams(("parallel",)),
    )(idx, slotted, recv)


def _half_shape(shape2d, axis):
    r, c = shape2d
    return (r, c // 2) if axis else (r // 2, c)


def _swap_halves(slotted, axes, small):
    n = len(slotted)

    def body(*refs):
        ins, sm_in = refs[:n], refs[n]
        outs, sm_out = refs[n + 1:2 * n + 1], refs[2 * n + 1]
        send_sems, recv_sems, local_sem = refs[2 * n + 2:]
        x, y, c = lax.axis_index("x"), lax.axis_index("y"), lax.axis_index("c")
        local = pltpu.make_async_copy(sm_in, sm_out.at[c], local_sem)
        local.start()
        sends = []
        for t in range(n):
            other = (slice(None),) + _half(slotted[t].shape[1:], axes[t], 1 - c)
            cp = pltpu.make_async_remote_copy(
                src_ref=ins[t].at[other], dst_ref=outs[t], send_sem=send_sems.at[t],
                recv_sem=recv_sems.at[t], device_id=(x, y, 1 - c), device_id_type=MESH)
            cp.start()
            sends.append(cp)
        cp = pltpu.make_async_remote_copy(
            src_ref=sm_in, dst_ref=sm_out.at[c], send_sem=send_sems.at[n], recv_sem=recv_sems.at[n],
            device_id=(x, y, 1 - c), device_id_type=MESH)
        cp.start()
        sends.append(cp)
        for cp in sends[:n]:
            cp.wait_recv()
        pltpu.make_async_remote_copy(
            src_ref=sm_in, dst_ref=sm_out.at[1 - c], send_sem=send_sems.at[n], recv_sem=recv_sems.at[n],
            device_id=(x, y, 1 - c), device_id_type=MESH).wait_recv()
        for cp in sends:
            cp.wait_send()
        local.wait()

    return pl.pallas_call(
        body, name="swap_halves",
        in_specs=[HBM] * (n + 1), out_specs=[HBM] * (n + 1),
        out_shape=[jax.ShapeDtypeStruct((a.shape[0],) + _half_shape(a.shape[1:], ax), a.dtype)
                   for a, ax in zip(slotted, axes, strict=True)]
        + [jax.ShapeDtypeStruct((2,) + small.shape, small.dtype)],
        scratch_shapes=[pltpu.SemaphoreType.DMA((n + 1,)), pltpu.SemaphoreType.DMA((n + 1,)), pltpu.SemaphoreType.DMA],
    )(*slotted, small)


def _scatter_halves(parts, small):
    n = len(parts)
    half_rows = small.shape[0] // 2

    def body(*refs):
        ins, sm_in = refs[:n], refs[n]
        outs, sm_out = refs[n + 1:2 * n + 1], refs[2 * n + 1]
        send_sems, recv_sems, sm_send, sm_recv, local_sem = refs[2 * n + 2:]
        x, y, c = lax.axis_index("x"), lax.axis_index("y"), lax.axis_index("c")
        me = _chip_index(x, y)
        peers = [(1 - x, y), (x, 1 - y), (1 - x, 1 - y)]
        mine = sm_in.at[pl.ds(pl.multiple_of(c * half_rows, 8), half_rows), :]
        local = pltpu.make_async_copy(mine, sm_out.at[me], local_sem)
        local.start()
        sends = []
        for k, (px, py) in enumerate(peers):
            cp = pltpu.make_async_remote_copy(src_ref=mine, dst_ref=sm_out.at[me], send_sem=sm_send.at[k],
                                              recv_sem=sm_recv.at[k], device_id=(px, py, c), device_id_type=MESH)
            cp.start()
            sends.append(cp)
        for t in range(n):
            for k, (px, py) in enumerate(peers):
                cp = pltpu.make_async_remote_copy(
                    src_ref=ins[t].at[_chip_index(px, py)], dst_ref=outs[t].at[k], send_sem=send_sems.at[t, k],
                    recv_sem=recv_sems.at[t, k], device_id=(px, py, c), device_id_type=MESH)
                cp.start()
                sends.append(cp)
        for k, (px, py) in enumerate(peers):
            pltpu.make_async_remote_copy(src_ref=mine, dst_ref=sm_out.at[_chip_index(px, py)], send_sem=sm_send.at[k],
                                         recv_sem=sm_recv.at[k], device_id=(px, py, c), device_id_type=MESH).wait_recv()
        for cp in sends[3:]:
            cp.wait_recv()
        for cp in sends:
            cp.wait_send()
        local.wait()

    return pl.pallas_call(
        body, name="scatter_halves",
        in_specs=[HBM] * (n + 1), out_specs=[HBM] * (n + 1),
        out_shape=[jax.ShapeDtypeStruct((3,) + a.shape[1:], a.dtype) for a in parts]
        + [jax.ShapeDtypeStruct((N_CHIPS, half_rows, small.shape[1]), small.dtype)],
        scratch_shapes=[pltpu.SemaphoreType.DMA((n, 3)), pltpu.SemaphoreType.DMA((n, 3)),
                        pltpu.SemaphoreType.DMA((3,)), pltpu.SemaphoreType.DMA((3,)), pltpu.SemaphoreType.DMA],
    )(*parts, small)


def _tiling(r, c, f32_per_elem):
    if r % 8 == 0:
        tr = _row_tile(r, f32_per_elem * c * 4)
        return r // tr, (tr, c), lambda i: (i, 0)
    assert c % 128 == 0, (r, c)
    return c // 128, (r, 128), lambda i: (0, i)


def _pair_sum(name, slotted, other, idx, axis):
    _, r, c = slotted.shape
    hr, hc = _half_shape((r, c), axis)
    n, (tr, tc), at = _tiling(hr, hc, 13)
    if axis == 0:
        a = slotted.reshape(N_CHIPS, 2, hr, c)
        a_all = pl.BlockSpec((N_CHIPS, 1, tr, tc), lambda i, s: (0, s[0]) + at(i))
        a_own = pl.BlockSpec((1, 1, tr, tc), lambda i, s: (s[1], s[0]) + at(i))
    else:
        a, per_half = slotted, hc // tc
        a_all = pl.BlockSpec((N_CHIPS, tr, tc), lambda i, s: (0, at(i)[0], s[0] * per_half + at(i)[1]))
        a_own = pl.BlockSpec((1, tr, tc), lambda i, s: (s[1], at(i)[0], s[0] * per_half + at(i)[1]))

    def body(idx_ref, a_ref, b_ref, am_ref, bm_ref, p_ref, own_ref):
        mine, mine_own = (a_ref[:, 0], am_ref[0, 0]) if axis == 0 else (a_ref[...], am_ref[0])
        p_ref[...] = (mine + b_ref[...]).astype(p_ref.dtype)
        own_ref[...] = mine_own + bm_ref[0]

    grid_spec = pltpu.PrefetchScalarGridSpec(
        num_scalar_prefetch=1, grid=(n,),
        in_specs=[a_all, pl.BlockSpec((N_CHIPS, tr, tc), lambda i, s: (0,) + at(i)),
                  a_own, pl.BlockSpec((1, tr, tc), lambda i, s: (s[1],) + at(i))],
        out_specs=[pl.BlockSpec((N_CHIPS, tr, tc), lambda i, s: (0,) + at(i)),
                   pl.BlockSpec((tr, tc), lambda i, s: at(i))])
    return pl.pallas_call(
        body, name=name, grid_spec=grid_spec,
        out_shape=[jax.ShapeDtypeStruct((N_CHIPS, hr, hc), BF16), jax.ShapeDtypeStruct((hr, hc), F32)],
        compiler_params=_cparams(("parallel",)),
    )(idx, a, other, a, other)


def _sum_parts(name, own, recv):
    h, c = own.shape
    n, (tr, tc), at = _tiling(h, c, 4)

    def body(o_ref, r_ref, out_ref):
        acc = o_ref[...]
        for k in range(3):
            acc = acc + r_ref[k].astype(F32)
        out_ref[...] = acc

    return pl.pallas_call(
        body, name=name, grid=(n,),
        in_specs=[pl.BlockSpec((tr, tc), at), pl.BlockSpec((3, tr, tc), lambda i: (0,) + at(i))],
        out_specs=pl.BlockSpec((tr, tc), at),
        out_shape=jax.ShapeDtypeStruct((h, c), F32),
        compiler_params=_cparams(("parallel",)),
    )(own, recv)


def _swap_sibling(parts):
    n = len(parts)

    def body(*refs):
        ins, outs = refs[:n], refs[n:2 * n]
        send_sems, recv_sems = refs[2 * n:]
        x, y, c = lax.axis_index("x"), lax.axis_index("y"), lax.axis_index("c")
        cps = []
        for t in range(n):
            cp = pltpu.make_async_remote_copy(
                src_ref=ins[t], dst_ref=outs[t], send_sem=send_sems.at[t], recv_sem=recv_sems.at[t],
                device_id=(x, y, 1 - c), device_id_type=MESH)
            cp.start()
            cps.append(cp)
        for cp in cps:
            cp.wait_recv()
        for cp in cps:
            cp.wait_send()

    return pl.pallas_call(
        body, name="swap_sibling",
        in_specs=[HBM] * n, out_specs=[HBM] * n,
        out_shape=[jax.ShapeDtypeStruct(a.shape, a.dtype) for a in parts],
        scratch_shapes=[pltpu.SemaphoreType.DMA((n,)), pltpu.SemaphoreType.DMA((n,))],
    )(*parts)


def _sum_slots(name, a):
    k, r, c = a.shape
    tr = _row_tile(r, (k + 1) * c * 4)

    def body(a_ref, o_ref):
        acc = a_ref[0]
        for i in range(1, k):
            acc = acc + a_ref[i]
        o_ref[...] = acc

    return pl.pallas_call(
        body, name=name, grid=(r // tr,),
        in_specs=[pl.BlockSpec((k, tr, c), lambda i: (0, i, 0))],
        out_specs=pl.BlockSpec((tr, c), lambda i: (i, 0)),
        out_shape=jax.ShapeDtypeStruct((r, c), a.dtype),
        compiler_params=_cparams(("parallel",)),
    )(a)


def _adamw(name, w, m, v, g_parts):
    r, c = w.shape
    ng = len(g_parts)
    tr = _row_tile(r, (7 + ng) * c * 4)
    c1 = 1.0 - ADAM_B1 ** ADAM_STEP
    c2 = 1.0 - ADAM_B2 ** ADAM_STEP

    def body(*refs):
        w_ref, m_ref, v_ref = refs[:3]
        g_refs = refs[3:3 + ng]
        go_ref, d_ref, mo_ref, vo_ref = refs[3 + ng:]
        g = g_refs[0][...]
        for gr in g_refs[1:]:
            g = g + gr[...]
        m_new = ADAM_B1 * m_ref[...] + (1.0 - ADAM_B1) * g
        v_new = ADAM_B2 * v_ref[...] + (1.0 - ADAM_B2) * (g * g)
        go_ref[...] = g
        mo_ref[...] = m_new
        vo_ref[...] = v_new
        d_ref[...] = -ADAM_LR * ((m_new / c1) / (jnp.sqrt(v_new / c2) + ADAM_EPS) + ADAM_WD * w_ref[...])

    spec = pl.BlockSpec((tr, c), lambda i: (i, 0))
    return pl.pallas_call(
        body, name=name, grid=(r // tr,),
        in_specs=[spec] * (3 + ng), out_specs=[spec] * 4,
        out_shape=[jax.ShapeDtypeStruct((r, c), F32)] * 4,
        compiler_params=_cparams(("parallel",)),
    )(w, m, v, *g_parts)


def _adamw_halves(name, w, m, v, own, other, idx, axis):
    hr, hc = own.shape
    nb, (tr, tc), at = _tiling(hr, hc, 9)
    c1 = 1.0 - ADAM_B1 ** ADAM_STEP
    c2 = 1.0 - ADAM_B2 ** ADAM_STEP

    def body(idx_ref, w_ref, m_ref, v_ref, own_ref, oth_ref, go_ref, d_ref, mo_ref, vo_ref):
        g = jnp.where(pl.program_id(0) == idx_ref[0], own_ref[...], oth_ref[...])
        m_new = ADAM_B1 * m_ref[...] + (1.0 - ADAM_B1) * g
        v_new = ADAM_B2 * v_ref[...] + (1.0 - ADAM_B2) * (g * g)
        go_ref[...] = g
        mo_ref[...] = m_new
        vo_ref[...] = v_new
        d_ref[...] = -ADAM_LR * ((m_new / c1) / (jnp.sqrt(v_new / c2) + ADAM_EPS) + ADAM_WD * w_ref[...])

    per_half = (hc // tc) if axis else (hr // tr)
    if axis:
        full = pl.BlockSpec((tr, tc), lambda hh, i, s: (at(i)[0], hh * per_half + at(i)[1]))
    else:
        full = pl.BlockSpec((tr, tc), lambda hh, i, s: (hh * per_half + at(i)[0], at(i)[1]))
    part = pl.BlockSpec((tr, tc), lambda hh, i, s: at(i))
    grid_spec = pltpu.PrefetchScalarGridSpec(
        num_scalar_prefetch=1, grid=(2, nb), in_specs=[full, full, full, part, part], out_specs=[full] * 4)
    return pl.pallas_call(
        body, name=name, grid_spec=grid_spec, out_shape=[jax.ShapeDtypeStruct(w.shape, F32)] * 4,
        compiler_params=_cparams(("parallel", "parallel")),
    )(idx, w, m, v, own, other)


WEIGHTS = ['norm_w', 'w_in', 's5_a_re', 's5_a_im', 's5_b_re', 's5_b_im', 's5_c_re', 's5_c_im', 's5_d', 's5_log_step',
           's5_w_glu', 's5_b_glu', 'ssd_conv_w', 'ssd_conv_b', 'ssd_dt_bias', 'ssd_a_log', 'ssd_d', 'ssd_norm_w',
           'w_br_s5', 'w_br_ssd', 'w_out', 'ple_norm_w', 'w_ple_gate', 'w_ple_proj', 'final_norm_w']
SHARDED = {'w_in': ((IN_PROJ_DIM, 1024), 0), 's5_w_glu': ((512, 512), 0), 'ssd_conv_w': ((SSD_CONV, SSD_CONV_DIM), 1),
           'w_br_s5': ((512, 1024), 1), 'w_br_ssd': ((1536, 1024), 0), 'w_out': ((1024, 1024), 0),
           'w_ple_gate': ((1024, 1024), 0), 'w_ple_proj': ((256, 1024), 1)}
TRANSPOSED = ('w_in',)
SMALL = [n for n in WEIGHTS if n not in SHARDED]


def _shard_shape(name):
    (r, c), ax = SHARDED[name]
    return (r // N_CHIPS, c) if ax == 0 else (r, c // N_CHIPS)


def _half_axis(name):
    return 0 if (_shard_shape(name)[0] // 2) % 16 == 0 else 1


def _shard2d(name, a):
    r, c = _shard_shape(name)
    return a.reshape(c, r).T if name in TRANSPOSED else a.reshape(r, c)


def _unshard2d(name, a2, shape):
    return (a2.T if name in TRANSPOSED else a2).reshape(shape)


def _unslot(name, a4):
    (r, c), ax = SHARDED[name]
    if ax == 0:
        return a4.reshape(r, c)
    return jnp.transpose(a4, (1, 0, 2)).reshape(r, c)


def _slot(name, full):
    (r, c), ax = SHARDED[name]
    if ax == 0:
        return full.reshape(N_CHIPS, r // N_CHIPS, c)
    return jnp.transpose(full.reshape(r, N_CHIPS, c // N_CHIPS), (1, 0, 2))


GHP = ('s5_b_re', 's5_b_im')


def _view_shape(name):
    if name in ('s5_a_re', 's5_a_im'):
        return (S5_GROUPS, S5_STATE)
    if name in GHP + ('s5_c_re', 's5_c_im'):
        return (S5_GROUPS, S5_GROUP, S5_STATE)
    if name == 'ssd_conv_w':
        return (SSD_CONV, SSD_CONV_DIM // N_CHIPS)
    return (1, {'s5_log_step': S5_GROUPS, 'ssd_conv_b': SSD_CONV_DIM, 'ssd_norm_w': SSD_WIDTH, 's5_d': S5_WIDTH,
                's5_b_glu': S5_WIDTH, 'ssd_dt_bias': SSD_HEADS, 'ssd_a_log': SSD_HEADS, 'ssd_d': SSD_HEADS}.get(name, D_MODEL))


def _view(name, a):
    if name in GHP:
        return jnp.swapaxes(a.reshape(S5_GROUPS, S5_STATE, S5_GROUP), 1, 2)
    return a.reshape(_view_shape(name))


def _unview(name, a, shape):
    return (jnp.swapaxes(a, 1, 2) if name in GHP else a).reshape(shape)


def _adamw_small(ws, ms, vs, gs):
    n = len(ws)
    c1 = 1.0 - ADAM_B1 ** ADAM_STEP
    c2 = 1.0 - ADAM_B2 ** ADAM_STEP

    def body(*refs):
        w_r, m_r, v_r, g_r = (refs[k * n:(k + 1) * n] for k in range(4))
        d_o, m_o, v_o = (refs[k * n:(k + 1) * n] for k in range(4, 7))
        for i in range(n):
            g = g_r[i][...]
            m_new = ADAM_B1 * m_r[i][...] + (1.0 - ADAM_B1) * g
            v_new = ADAM_B2 * v_r[i][...] + (1.0 - ADAM_B2) * (g * g)
            m_o[i][...] = m_new
            v_o[i][...] = v_new
            d_o[i][...] = -ADAM_LR * ((m_new / c1) / (jnp.sqrt(v_new / c2) + ADAM_EPS) + ADAM_WD * w_r[i][...])

    return pl.pallas_call(
        body, name="adamw_small", out_shape=[jax.ShapeDtypeStruct(w.shape, F32) for w in ws] * 3,
        compiler_params=pltpu.CompilerParams(vmem_limit_bytes=VMEM_LIMIT),
    )(*ws, *ms, *vs, *gs)


def _pack_small(vals):
    flat = jnp.concatenate([v.reshape(-1).astype(F32) for v in vals])
    rows = -(-flat.shape[0] // (256 * 128)) * 256
    return jnp.pad(flat, (0, rows * 128 - flat.shape[0])).reshape(rows, 128)


def _unpack_small(packed, shapes):
    flat = packed.reshape(-1)
    out, off = [], 0
    for sh in shapes:
        n = math.prod(sh)
        out.append(flat[off:off + n].reshape(sh))
        off += n
    return out


def kernel(x, p, norm_w, w_in, s5_a_re, s5_a_im, s5_b_re, s5_b_im, s5_c_re, s5_c_im, s5_d, s5_log_step, s5_w_glu, s5_b_glu, ssd_conv_w, ssd_conv_b, ssd_dt_bias, ssd_a_log, ssd_d, ssd_norm_w, w_br_s5, w_br_ssd, w_out, ple_norm_w, w_ple_gate, w_ple_proj, final_norm_w, loss_target, m_norm_w, m_w_in, m_s5_a_re, m_s5_a_im, m_s5_b_re, m_s5_b_im, m_s5_c_re, m_s5_c_im, m_s5_d, m_s5_log_step, m_s5_w_glu, m_s5_b_glu, m_ssd_conv_w, m_ssd_conv_b, m_ssd_dt_bias, m_ssd_a_log, m_ssd_d, m_ssd_norm_w, m_w_br_s5, m_w_br_ssd, m_w_out, m_ple_norm_w, m_w_ple_gate, m_w_ple_proj, m_final_norm_w, v_norm_w, v_w_in, v_s5_a_re, v_s5_a_im, v_s5_b_re, v_s5_b_im, v_s5_c_re, v_s5_c_im, v_s5_d, v_s5_log_step, v_s5_w_glu, v_s5_b_glu, v_ssd_conv_w, v_ssd_conv_b, v_ssd_dt_bias, v_ssd_a_log, v_ssd_d, v_ssd_norm_w, v_w_br_s5, v_w_br_ssd, v_w_out, v_ple_norm_w, v_w_ple_gate, v_w_ple_proj, v_final_norm_w):
    args = locals()
    wl = {n: args[n] for n in WEIGHTS}
    ml = {n: args["m_" + n] for n in WEIGHTS}
    vl = {n: args["v_" + n] for n in WEIGHTS}
    big = [n for n in SHARDED if n != 'ssd_conv_w']
    chip = _chip_index(lax.axis_index("x"), lax.axis_index("y"))
    idx = jnp.stack([lax.axis_index("c"), chip]).astype(jnp.int32)

    axes = [_half_axis(n) for n in big]
    first = ['w_in']
    rest = [n for n in big if n not in first]
    bf_shards = {n: _shard2d(n, wl[n]).astype(BF16) for n in big}
    w_in_t, conv_w4 = _gather_chips([bf_shards[n] for n in first], [_half_axis(n) for n in first],
                                    [_shard2d('ssd_conv_w', wl['ssd_conv_w'])])
    full = {n: wl[n] for n in SMALL}
    full["w_in_t"] = w_in_t
    full['ssd_conv_w'] = _unslot('ssd_conv_w', conv_w4)
    send_sems, recv_sems, thru, lands, token = _gather_start([bf_shards[n] for n in rest], w_in_t)

    def fetch_rest(after):
        own, zones = _gather_wait(send_sems, recv_sems, thru, lands, after)
        return {n: _unslot(n, lax.dynamic_update_slice(z, o[None], (chip, 0, 0)))
                for n, o, z in zip(rest, own, zones, strict=True)}

    full["_late"] = (token, fetch_rest)

    in_flight = []

    def send_rest(g_now):
        in_flight.extend(_scatter_start([_slot(n, g_now[n]) for n in rest], g_now['s5_w_glu']))
        return in_flight[4]

    full["_early"] = send_rest
    loss, grad_x, g = _local_step(x, p[0], loss_target, full)
    for n in TRANSPOSED:
        g[n] = g.pop(n + "_t")

    for n in ('s5_b_re', 's5_b_im'):
        g[n] = g.pop(n + "_ghp")
    small_shapes = [(1, 1)] + [_view_shape(n) for n in SMALL] + [(SSD_CONV, SSD_CONV_DIM)]
    small_pack = _pack_small([loss] + [g[n] for n in SMALL] + [g['ssd_conv_w']])
    first_axes = [_half_axis(n) for n in first]
    slotted = [g[n] for n in first]
    swapped = _swap_halves(slotted, first_axes, small_pack)
    pair = [_pair_sum("pair_sum_" + n, a, b, idx, ax)
            for n, a, b, ax in zip(first, slotted, swapped[:-1], first_axes, strict=True)]
    small_chip = _sum_slots("sum_small_pair", swapped[-1])
    recv = _scatter_halves([pb for pb, _ in pair], small_chip)
    small_half = _sum_slots("sum_small_chips", recv[-1])
    halves = [_sum_parts("sum_chips_" + n, own, r) for n, (_, own), r in zip(first, pair, recv[:-1], strict=True)]
    own_slots, zones = _scatter_wait(*in_flight[:4], g['w_in'])
    chip_sums = [_sum_chips("sum_chips_" + n, a, z, idx) for n, a, z in zip(rest, own_slots, zones, strict=True)]
    swapped2 = _swap_sibling(halves + chip_sums + [small_half])
    other_halves, sib_sums = swapped2[:len(first)], swapped2[len(first):-1]
    c_is_0 = lax.axis_index("c") == 0
    small_sum = jnp.concatenate([jnp.where(c_is_0, small_half, swapped2[-1]),
                                 jnp.where(c_is_0, swapped2[-1], small_half)], axis=0)

    out_g, out_d, out_m, out_v = {}, {}, {}, {}
    for n, own, oth, ax in zip(first, halves, other_halves, first_axes, strict=True):
        res = _adamw_halves("adamw_" + n, _shard2d(n, wl[n]), _shard2d(n, ml[n]), _shard2d(n, vl[n]), own, oth, idx, ax)
        out_g[n], out_d[n], out_m[n], out_v[n] = (_unshard2d(n, r, wl[n].shape) for r in res)
    for n, own, sib in zip(rest, chip_sums, sib_sums, strict=True):
        res = _adamw("adamw_" + n, _shard2d(n, wl[n]), _shard2d(n, ml[n]), _shard2d(n, vl[n]), [own, sib])
        out_g[n], out_d[n], out_m[n], out_v[n] = (_unshard2d(n, r, wl[n].shape) for r in res)
    sm = _unpack_small(small_sum, small_shapes)
    loss_total = sm[0].reshape(())
    conv_g = lax.dynamic_slice(sm[-1], (0, chip * (SSD_CONV_DIM // N_CHIPS)), (SSD_CONV, SSD_CONV_DIM // N_CHIPS))
    names = SMALL + ['ssd_conv_w']
    grads = sm[1:-1] + [conv_g]
    res = _adamw_small([_view(n, wl[n]) for n in names], [_view(n, ml[n]) for n in names],
                       [_view(n, vl[n]) for n in names], grads)
    for i, n in enumerate(names):
        out_g[n] = _unview(n, grads[i], wl[n].shape)
        out_d[n], out_m[n], out_v[n] = (_unview(n, res[k * len(names) + i], wl[n].shape) for k in range(3))

    return (loss_total, grad_x, *[out_g[n] for n in WEIGHTS], *[out_d[n] for n in WEIGHTS],
            *[out_m[n] for n in WEIGHTS], *[out_v[n] for n in WEIGHTS])
```

```python
import functools
import math

import jax
import jax.numpy as jnp
from jax import lax
from jax.experimental import pallas as pl
from jax.experimental.pallas import tpu as pltpu

F32 = jnp.float32
BF16 = jnp.bfloat16
MESH = pl.DeviceIdType.MESH

D_MODEL = 1024
PLE_DIM = 256
RMS_EPS = 1e-6
S5_WIDTH = 512
S5_GROUP = 16
S5_GROUPS = 32
S5_STATE = 64
S5_N = S5_GROUPS * S5_STATE
S5_LB = 512
S5_NJ = S5_N // S5_LB
S5_TB = 256
S5_LOG_TB = 8
SSD_WIDTH = 1536
SSD_HEADDIM = 64
SSD_HEADS = 24
SSD_GROUPS = 4
SSD_HPG = 6
SSD_STATE = 128
SSD_CONV = 4
SSD_CHUNK = 128
SSD_BC = 512
SSD_CONV_DIM = 2560
GROUP_W = SSD_WIDTH // SSD_GROUPS
N_CHIPS = 4
N_DEV = 8

OFF_XBC, OFF_U5, OFF_Z5, OFF_DT, OFF_G5, OFF_GS, OFF_ZS = 0, 2560, 3072, 3584, 4096, 5120, 6144
DT_W = 512
PROJ_W = 7680
IN_PROJ_DIM = 7192

ADAM_LR, ADAM_B1, ADAM_B2, ADAM_EPS, ADAM_WD, ADAM_STEP = 0.001, 0.9, 0.999, 1e-08, 0.01, 10

VMEM_LIMIT = 56 * 1024 * 1024


def _pick(n, cands):
    for c in cands:
        if n % c == 0:
            return c
    return n


ROW_BLOCK_BYTES = 8 * 1024 * 1024


def _row_tile(r, bytes_per_row):
    for t in (r, 4096, 2048, 1024, 512, 256, 128, 64, 32, 16, 8):
        if t <= r and r % t == 0 and t * bytes_per_row <= ROW_BLOCK_BYTES:
            return t
    return r


def _cparams(sem):
    return pltpu.CompilerParams(dimension_semantics=sem, vmem_limit_bytes=VMEM_LIMIT)


def _dg(a, b, ca, cb):
    return lax.dot_general(a.astype(BF16), b.astype(BF16), (((ca,), (cb,)), ((), ())), preferred_element_type=F32)


@jax.custom_vjp
def dot_nn(a, b):
    return _dg(a, b, 1, 0)


@jax.custom_vjp
def dot_nt(a, b):
    return _dg(a, b, 1, 1)


@jax.custom_vjp
def dot_tn(a, b):
    return _dg(a, b, 0, 0)


dot_nn.defvjp(lambda a, b: (_dg(a, b, 1, 0), (a, b)), lambda r, g: (_dg(g, r[1], 1, 1), _dg(r[0], g, 0, 0)))
dot_nt.defvjp(lambda a, b: (_dg(a, b, 1, 1), (a, b)), lambda r, g: (_dg(g, r[1], 1, 0), _dg(g, r[0], 0, 0)))
dot_tn.defvjp(lambda a, b: (_dg(a, b, 0, 0), (a, b)), lambda r, g: (_dg(r[1], g, 1, 1), _dg(r[0], g, 1, 0)))


MM_VMEM_BUDGET = 30 * 1024 * 1024


def _mm_tiles(m, n, k, sa, sb, so, tn_only=None):
    best, best_key = None, None
    for tm in (1024, 512, 256, 128, 64, 32, 16, 8):
        if m % tm:
            continue
        for tn in (2048, 1536, 1280, 1024, 768, 640, 512, 384, 256, 128):
            if n % tn or (tn_only is not None and tn not in tn_only):
                continue
            for tk in (k, 2048, 1536, 1280, 1024, 768, 512, 256, 128):
                if k % tk or tk > max(k, 128) or (tk == k and k > 2048 and k % 128 == 0):
                    continue
                need = 2 * (tm * tk * sa + tk * tn * sb + tm * tn * so) + (tm * tn * 4 if tk < k else 0)
                if need > MM_VMEM_BUDGET:
                    continue
                key = (tm * tn * tk, tk)
                if best_key is None or key > best_key:
                    best, best_key = (tm, tn, tk), key
    assert best is not None, (m, n, k)
    return best


def _matmul(name, a, b, *, ta=False, tb=False, a_win=None, out_dtype=F32, epilogue=None, epi_rows=(), epi_pars=(),
            epi_outs=(), full_rows=False, epi_accs=(), epi_into=None):
    a_off, a_w = a_win if a_win is not None else (0, a.shape[1])
    if ta:
        kdim, m = a.shape[0], a_w
    else:
        m, kdim = a.shape[0], a_w
    n = b.shape[0] if tb else b.shape[1]
    assert (b.shape[1] if tb else b.shape[0]) == kdim, (name, a.shape, b.shape)
    out_dtypes = list(epi_outs) if epilogue is not None else [out_dtype]
    so = sum(jnp.dtype(d).itemsize for d in out_dtypes) + sum(r.dtype.itemsize for r, _ in epi_rows)
    tn_ok = [n] if full_rows else [t for t in (1024, 512, 256, 128) if all(off % t == 0 for _, off in epi_rows)]
    tm, tn, tk = _mm_tiles(m, n, kdim, a.dtype.itemsize, b.dtype.itemsize, so, tn_ok if epilogue is not None else None)
    nk = kdim // tk
    n_er, n_ep, n_out = len(epi_rows), len(epi_pars), len(out_dtypes)
    if ta:
        assert a_off % tm == 0
        a_spec = pl.BlockSpec((tk, tm), lambda i, j, k: (k, i + a_off // tm))
    else:
        assert a_off % tk == 0
        a_spec = pl.BlockSpec((tm, tk), lambda i, j, k: (i, k + a_off // tk))
    if tb:
        b_spec = pl.BlockSpec((tn, tk), lambda i, j, k: (j, k))
    else:
        b_spec = pl.BlockSpec((tk, tn), lambda i, j, k: (k, j))
    ca, cb = (0 if ta else 1), (1 if tb else 0)

    n_acc = len(epi_accs)
    n_x = 1 if epi_into is not None else 0
    assert not (n_acc or n_x) or full_rows

    def body(a_ref, b_ref, *refs):
        er, ep = refs[:n_er], refs[n_er:n_er + n_ep]
        first_out = n_er + n_ep + n_x
        o_refs = refs[first_out:first_out + n_out]
        s_refs = refs[first_out + n_out:first_out + n_out + n_acc]
        acc = refs[first_out + n_out + n_acc:]

        def finish(c):
            outs = [c] if epilogue is None else epilogue(c, [r[...] for r in er], [p[...] for p in ep])
            if n_acc:
                outs, sums = outs

                @pl.when(pl.program_id(0) == 0)
                def _():
                    for s_ref in s_refs:
                        s_ref[...] = jnp.zeros_like(s_ref)

                for s_ref, s in zip(s_refs, sums, strict=True):
                    s_ref[...] += jnp.broadcast_to(s, s_ref.shape)
            for o_ref, o in zip(o_refs, outs, strict=True):
                o_ref[...] = o.astype(o_ref.dtype)

        if nk == 1:
            finish(_dg(a_ref[...], b_ref[...], ca, cb))
            return
        (acc_ref,) = acc
        k = pl.program_id(2)

        @pl.when(k == 0)
        def _():
            acc_ref[...] = jnp.zeros_like(acc_ref)

        acc_ref[...] += _dg(a_ref[...], b_ref[...], ca, cb)

        @pl.when(k == nk - 1)
        def _():
            finish(acc_ref[...])

    in_specs = [a_spec, b_spec]
    in_specs += [pl.BlockSpec((tm, tn), functools.partial(lambda i, j, k, c: (i, j + c), c=off // tn)) for _, off in epi_rows]
    in_specs += [pl.BlockSpec((p.shape[0], tn), lambda i, j, k: (0, j)) for p in epi_pars]
    out_specs = [pl.BlockSpec((tm, tn), lambda i, j, k: (i, j)) for _ in out_dtypes]
    out_shape = [jax.ShapeDtypeStruct((m, n), d) for d in out_dtypes]
    extra, aliases = [], {}
    if epi_into is not None:
        buf, off, width = epi_into
        assert off % width == 0 and buf.dtype == out_dtypes[0]
        in_specs.append(pl.BlockSpec(memory_space=pl.ANY))
        out_specs[0] = pl.BlockSpec((tm, width), functools.partial(lambda i, j, k, c: (i, c), c=off // width))
        out_shape[0] = jax.ShapeDtypeStruct(buf.shape, buf.dtype)
        extra, aliases = [buf], {2 + n_er + n_ep: 0}
    out_specs += [pl.BlockSpec((r, w), lambda i, j, k: (0, 0)) for r, w in epi_accs]
    out_shape += [jax.ShapeDtypeStruct((r, w), F32) for r, w in epi_accs]
    res = pl.pallas_call(
        body, name=name, grid=(m // tm, n // tn, nk),
        in_specs=in_specs, out_specs=out_specs, out_shape=out_shape, input_output_aliases=aliases,
        scratch_shapes=[pltpu.VMEM((tm, tn), F32)] if nk > 1 else [],
        compiler_params=_cparams(("arbitrary",) * 3 if n_acc else ("parallel", "parallel", "arbitrary")),
    )(a, b, *[r for r, _ in epi_rows], *epi_pars, *extra)
    return res if epilogue is not None else res[0]


BAND = 128


def _band_matmul(name, kind, a, b, *, a_blk0=0, out_dtype=F32, epilogue=None, epi_rows=(), epi_pars=(), epi_outs=(),
                 into=None):
    n_rows = a.shape[0]
    blk = 2 * S5_LB
    tm = _pick(n_rows, (1024, 512, 256))
    if epilogue is not None:
        assert kind == "nt"
        n_er, n_ep = len(epi_rows), len(epi_pars)
        in_specs = [pl.BlockSpec((tm, blk), lambda i, j: (i, j)), pl.BlockSpec((BAND, blk), lambda i, j: (j, 0))]
        in_specs += [pl.BlockSpec((tm, BAND), functools.partial(lambda i, j, c: (i, c + j), c=c0)) for _, c0 in epi_rows]
        in_specs += [pl.BlockSpec((1, BAND), lambda i, j: (0, j)) for _ in epi_pars]
        out_specs = [pl.BlockSpec((tm, BAND), lambda i, j: (i, j)) for _ in epi_outs]
        out_shape = [jax.ShapeDtypeStruct((n_rows, S5_NJ * BAND), d) for d in epi_outs]
        extra, aliases = [], {}
        if into is not None:
            buf, c0 = into
            in_specs.append(pl.BlockSpec(memory_space=pl.ANY))
            out_specs[0] = pl.BlockSpec((tm, BAND), functools.partial(lambda i, j, c: (i, c + j), c=c0))
            out_shape[0] = jax.ShapeDtypeStruct(buf.shape, buf.dtype)
            extra, aliases = [buf], {2 + n_er + n_ep: 0}

        def epi_body(a_ref, b_ref, *refs):
            outs = epilogue(_dg(a_ref[...], b_ref[...], 1, 1), [r[...] for r in refs[:n_er]],
                            [p[...] for p in refs[n_er:n_er + n_ep]])
            for o_ref, o in zip(refs[n_er + n_ep + len(extra):], outs, strict=True):
                o_ref[...] = o.astype(o_ref.dtype)

        return pl.pallas_call(
            epi_body, name=name, grid=(n_rows // tm, S5_NJ), in_specs=in_specs, out_specs=out_specs,
            out_shape=out_shape, input_output_aliases=aliases, compiler_params=_cparams(("parallel", "parallel")),
        )(a, b, *[r for r, _ in epi_rows], *epi_pars, *extra)
    if kind == "nn":
        grid = (n_rows // tm, S5_NJ)
        in_specs = [pl.BlockSpec((tm, BAND), lambda i, j: (i, a_blk0 + j)), pl.BlockSpec((BAND, blk), lambda i, j: (j, 0))]
        out_spec = pl.BlockSpec((tm, blk), lambda i, j: (i, j))
        out_shape = (n_rows, S5_NJ * blk)
        sem = ("parallel", "parallel")

        def body(a_ref, b_ref, o_ref):
            o_ref[...] = _dg(a_ref[...], b_ref[...], 1, 0).astype(o_ref.dtype)
    elif kind == "nt":
        grid = (n_rows // tm, S5_NJ)
        in_specs = [pl.BlockSpec((tm, blk), lambda i, j: (i, j)), pl.BlockSpec((BAND, blk), lambda i, j: (j, 0))]
        out_spec = pl.BlockSpec((tm, BAND), lambda i, j: (i, j))
        out_shape = (n_rows, S5_NJ * BAND)
        sem = ("parallel", "parallel")

        def body(a_ref, b_ref, o_ref):
            o_ref[...] = _dg(a_ref[...], b_ref[...], 1, 1).astype(o_ref.dtype)
    else:
        grid = (S5_NJ, n_rows // tm)
        in_specs = [pl.BlockSpec((tm, BAND), lambda j, k: (k, a_blk0 + j)), pl.BlockSpec((tm, blk), lambda j, k: (k, j))]
        out_spec = pl.BlockSpec((BAND, blk), lambda j, k: (j, 0))
        out_shape = (S5_NJ * BAND, blk)
        sem = ("parallel", "arbitrary")

        def body(a_ref, b_ref, o_ref):
            @pl.when(pl.program_id(1) == 0)
            def _():
                o_ref[...] = jnp.zeros_like(o_ref)

            o_ref[...] += _dg(a_ref[...], b_ref[...], 0, 0)

    return pl.pallas_call(
        body, name=name, grid=grid, in_specs=in_specs, out_specs=out_spec,
        out_shape=jax.ShapeDtypeStruct(out_shape, out_dtype), compiler_params=_cparams(sem),
    )(a, b)


def _rowwise(name, fn, n_rows, tr, row_ins, par_ins, row_outs, acc_outs=(), into=None):
    nr, npar, no, na = len(row_ins), len(par_ins), len(row_outs), len(acc_outs)
    in_specs = []
    for arr, off, w in row_ins:
        assert off % w == 0 and arr.shape[0] == n_rows, (name, arr.shape, off, w)
        in_specs.append(pl.BlockSpec((tr, w), functools.partial(lambda i, c: (i, c), c=off // w)))
    for arr, off, w in par_ins:
        assert off % w == 0
        in_specs.append(pl.BlockSpec((arr.shape[0], w), functools.partial(lambda i, c: (0, c), c=off // w)))
    out_specs = [pl.BlockSpec((tr, w), lambda i: (i, 0)) for w, _ in row_outs]
    out_specs += [pl.BlockSpec((r, w), lambda i: (0, 0)) for r, w in acc_outs]
    out_shape = [jax.ShapeDtypeStruct((n_rows, w), dt) for w, dt in row_outs]
    out_shape += [jax.ShapeDtypeStruct((r, w), F32) for r, w in acc_outs]
    extra, aliases = [], {}
    if into is not None:
        buf, off = into
        w0 = row_outs[0][0]
        assert off % w0 == 0 and buf.dtype == row_outs[0][1]
        in_specs.append(pl.BlockSpec(memory_space=pl.ANY))
        out_specs[0] = pl.BlockSpec((tr, w0), functools.partial(lambda i, c: (i, c), c=off // w0))
        out_shape[0] = jax.ShapeDtypeStruct(buf.shape, buf.dtype)
        extra, aliases = [buf], {nr + npar: 0}
    nx = len(extra)

    def body(*refs):
        rows = [r[...] for r in refs[:nr]]
        pars = [r[...] for r in refs[nr:nr + npar]]
        o_refs = refs[nr + npar + nx:nr + npar + nx + no]
        a_refs = refs[nr + npar + nx + no:]
        outs, accs = fn(rows, pars)
        for o_ref, o in zip(o_refs, outs, strict=True):
            o_ref[...] = o.astype(o_ref.dtype)
        if na:
            @pl.when(pl.program_id(0) == 0)
            def _():
                for a_ref in a_refs:
                    a_ref[...] = jnp.zeros_like(a_ref)

            for a_ref, a in zip(a_refs, accs, strict=True):
                a_ref[...] += jnp.broadcast_to(a, a_ref.shape)

    res = pl.pallas_call(
        body, name=name, grid=(n_rows // tr,),
        in_specs=in_specs, out_specs=out_specs, out_shape=out_shape, input_output_aliases=aliases,
        compiler_params=_cparams(("arbitrary",) if na else ("parallel",)),
    )(*[a for a, _, _ in row_ins], *[a for a, _, _ in par_ins], *extra)
    return res


def _rms(x, w):
    return x * lax.rsqrt(jnp.mean(x * x, axis=-1, keepdims=True) + RMS_EPS) * w


def _gated_norm(y, z, w):
    outs = []
    for g in range(SSD_GROUPS):
        sl = slice(g * GROUP_W, (g + 1) * GROUP_W)
        yg = y[:, sl] * jax.nn.silu(z[:, sl])
        outs.append(yg * lax.rsqrt(jnp.mean(yg * yg, axis=-1, keepdims=True) + RMS_EPS) * w[:, sl])
    return jnp.concatenate(outs, axis=-1)


def _s5_out(yc, u, z, t, d, bg):
    ge = jax.nn.gelu(yc + d * u)
    return ge * jax.nn.sigmoid(t + bg) * jax.nn.silu(z)


def _merge(g5, gs, m5, ms):
    return jax.nn.sigmoid(g5) * m5 + jax.nn.sigmoid(gs) * ms


def _head_loss(h1, pgl, pp, fw, tgt):
    h2 = h1 + jax.nn.sigmoid(pgl) * pp
    err = _rms(h2, fw) - tgt
    per_row = 0.5 * jnp.mean(err * err, axis=-1, keepdims=True)
    return jnp.sum(per_row, axis=0, keepdims=True)


def _s5_disc(a_re, a_im, log_step, b_re2, b_im2, expand):
    step = jnp.exp(log_step)
    mag = jnp.exp(a_re * step)
    lb_re = mag * jnp.cos(a_im * step)
    lb_im = mag * jnp.sin(a_im * step)
    den = a_re * a_re + a_im * a_im
    n_re = lb_re - 1.0
    f_re = (n_re * a_re + lb_im * a_im) / den
    f_im = (lb_im * a_re - n_re * a_im) / den
    hi = lax.Precision.HIGHEST
    fr = jnp.dot(expand, f_re, precision=hi, preferred_element_type=F32)
    fi = jnp.dot(expand, f_im, precision=hi, preferred_element_type=F32)
    return lb_re, lb_im, fr * b_re2 - fi * b_im2, fr * b_im2 + fi * b_re2


def _s5_params_fwd(a_re, a_im, log_step, b_re2, b_im2, expand):
    gp = a_re.shape

    def body(ar, ai, ls, br, bi, ex, pr_ref, pi_ref, bbr_ref, bbi_ref):
        lr, li, bbr, bbi = _s5_disc(ar[...], ai[...], ls[...], br[...], bi[...], ex[...])
        bbr_ref[...] = bbr
        bbi_ref[...] = bbi
        qr, qi = lr, li
        for k in range(S5_LOG_TB):
            pr_ref[k] = qr
            pi_ref[k] = qi
            qr, qi = qr * lr - qi * li, qr * li + qi * lr

    return pl.pallas_call(
        body, name="s5_params_fwd",
        out_shape=(jax.ShapeDtypeStruct((S5_LOG_TB,) + gp, F32), jax.ShapeDtypeStruct((S5_LOG_TB,) + gp, F32),
                   jax.ShapeDtypeStruct(b_re2.shape, F32), jax.ShapeDtypeStruct(b_re2.shape, F32)),
    )(a_re, a_im, log_step, b_re2, b_im2, expand)


def _s5_params_bwd(a_re, a_im, log_step, b_re2, b_im2, expand, d_lr, d_li, d_bbr, d_bbi):
    def body(ar, ai, ls, br, bi, ex, glr, gli, gbr, gbi, dar, dai, dls, dbr, dbi):
        _, vjp = jax.vjp(lambda *p: _s5_disc(*p, ex[...]), ar[...], ai[...], ls[...], br[...], bi[...])
        g = vjp((glr[...], gli[...], gbr[...], gbi[...]))
        for ref, val in zip((dar, dai, dls, dbr, dbi), g, strict=True):
            ref[...] = val

    return pl.pallas_call(
        body, name="s5_params_bwd",
        out_shape=tuple(jax.ShapeDtypeStruct(v.shape, F32) for v in (a_re, a_im, log_step, b_re2, b_im2)),
    )(a_re, a_im, log_step, b_re2, b_im2, expand, d_lr, d_li, d_bbr, d_bbi)


def _scan_block(x_ref, out_ref, lp, edge_pow, cr, ci, reverse, each=None):
    n_g = x_ref.shape[0] // 8
    sub = lax.broadcasted_iota(jnp.int32, (8, S5_LB), 0)
    steps = []
    for sh in (1, 2, 4):
        keep = (sub < 8 - sh) if reverse else (sub >= sh)
        steps.append((8 - sh if reverse else sh, jnp.where(keep, lp[sh - 1:sh, :S5_LB], 0.0),
                      jnp.where(keep, lp[sh - 1:sh, S5_LB:], 0.0)))
    e_r, e_i = edge_pow[:, :S5_LB], edge_pow[:, S5_LB:]
    for r in (range(n_g - 1, -1, -1) if reverse else range(n_g)):
        rows = slice(8 * r, 8 * r + 8)
        xr, xi = x_ref[rows, :S5_LB], x_ref[rows, S5_LB:]
        for by, a_r, a_i in steps:
            pr, pi = pltpu.roll(xr, by, 0), pltpu.roll(xi, by, 0)
            xr, xi = xr + a_r * pr - a_i * pi, xi + a_r * pi + a_i * pr
        xr = xr + e_r * cr - e_i * ci
        xi = xi + e_r * ci + e_i * cr
        out_ref[rows, :S5_LB] = xr
        out_ref[rows, S5_LB:] = xi
        if each is not None:
            each(r, xr, xi)
        cr, ci = (xr[0:1, :], xi[0:1, :]) if reverse else (xr[7:8, :], xi[7:8, :])
    return cr, ci


def _s5_fwd(proj, bb_band, c_band, lam_pow, s5_d, n_seq, seq_len):
    n_t = seq_len // S5_TB
    blk = 2 * S5_LB
    n_rows = n_seq * seq_len
    u_blk0 = OFF_U5 // BAND

    def body(u_ref, bb_ref, cb_ref, lp_ref, d_ref, s_ref, y_ref, ge_ref, cr, ci, buf):
        @pl.when(pl.program_id(2) == 0)
        def _():
            cr[...] = jnp.zeros_like(cr)
            ci[...] = jnp.zeros_like(ci)

        u = u_ref[...]
        buf[...] = _dg(u, bb_ref[...], 1, 0)
        lp = lp_ref[...]
        cr[...], ci[...] = _scan_block(buf, buf, lp, lp, cr[...], ci[...], False)
        s = buf[...].astype(s_ref.dtype)
        s_ref[...] = s
        y = _dg(s, cb_ref[...], 1, 1)
        y_ref[...] = y
        ge_ref[...] = jax.nn.gelu(y + d_ref[...] * u).astype(ge_ref.dtype)

    def rows(j, b, t):
        return (b * n_t + t, j)

    return pl.pallas_call(
        body, name="s5_fwd", grid=(S5_NJ, n_seq, n_t),
        in_specs=[pl.BlockSpec((S5_TB, BAND), lambda j, b, t: (b * n_t + t, u_blk0 + j)),
                  pl.BlockSpec((BAND, blk), lambda j, b, t: (j, 0)), pl.BlockSpec((BAND, blk), lambda j, b, t: (j, 0)),
                  pl.BlockSpec((S5_LOG_TB, blk), lambda j, b, t: (0, j)), pl.BlockSpec((1, BAND), lambda j, b, t: (0, j))],
        out_specs=[pl.BlockSpec((S5_TB, blk), rows), pl.BlockSpec((S5_TB, BAND), rows), pl.BlockSpec((S5_TB, BAND), rows)],
        out_shape=[jax.ShapeDtypeStruct((n_rows, S5_NJ * blk), BF16), jax.ShapeDtypeStruct((n_rows, S5_WIDTH), F32),
                   jax.ShapeDtypeStruct((n_rows, S5_WIDTH), BF16)],
        scratch_shapes=[pltpu.VMEM((1, S5_LB), F32), pltpu.VMEM((1, S5_LB), F32), pltpu.VMEM((S5_TB, blk), F32)],
        compiler_params=_cparams(("parallel", "parallel", "arbitrary")),
    )(proj, bb_band, c_band, lam_pow, s5_d)


def _s5_bwd(dyc, s, proj, du5_a, bb_band, c_band, lam_pow_conj, dproj, n_seq, seq_len):
    n_t = seq_len // S5_TB
    blk = 2 * S5_LB
    halo_rows = 16
    halo_per_blk = S5_TB // halo_rows
    u_blk0 = OFF_U5 // BAND

    def rows(j, b, t):
        return (b * n_t + (n_t - 1 - t), j)

    def u_rows(j, b, t):
        return (b * n_t + (n_t - 1 - t), u_blk0 + j)

    def halo(j, b, t):
        return (jnp.maximum((b * n_t + (n_t - 1 - t)) * halo_per_blk - 1, 0), j)

    def body(dy_ref, sq_ref, hq_ref, u_ref, dua_ref, bb_ref, cb_ref, lp_ref, _, du_ref, dl_ref, dbb_ref, dcb_ref,
             cr, ci, buf, s_ref):
        b, t = pl.program_id(1), pl.program_id(2)
        sq = sq_ref[...]
        s_ref[...] = sq.astype(F32)
        h_last = hq_ref[...].astype(F32)[halo_rows - 1:halo_rows, :]
        dy = dy_ref[...]
        buf[...] = _dg(dy, cb_ref[...], 1, 0)

        @pl.when(t == 0)
        def _():
            cr[...] = jnp.zeros_like(cr)
            ci[...] = jnp.zeros_like(ci)

        @pl.when((b == 0) & (t == 0))
        def _():
            dl_ref[...] = jnp.zeros_like(dl_ref)
            dbb_ref[...] = jnp.zeros_like(dbb_ref)
            dcb_ref[...] = jnp.zeros_like(dcb_ref)

        dcb_ref[...] += _dg(dy, sq, 0, 0)

        first_blk = t == n_t - 1
        sub = lax.broadcasted_iota(jnp.int32, (8, S5_LB), 0)
        acc = [jnp.zeros((8, S5_LB), F32), jnp.zeros((8, S5_LB), F32)]

        def each(r, gr, gi):
            rows = slice(8 * r, 8 * r + 8)
            if r == 0:
                before_r = jnp.where(first_blk, 0.0, h_last[:, :S5_LB])
                before_i = jnp.where(first_blk, 0.0, h_last[:, S5_LB:])
            else:
                before_r, before_i = s_ref[8 * r - 1:8 * r, :S5_LB], s_ref[8 * r - 1:8 * r, S5_LB:]
            sp_r = jnp.where(sub == 0, before_r, pltpu.roll(s_ref[rows, :S5_LB], 1, 0))
            sp_i = jnp.where(sub == 0, before_i, pltpu.roll(s_ref[rows, S5_LB:], 1, 0))
            acc[0] = acc[0] + gr * sp_r + gi * sp_i
            acc[1] = acc[1] + gi * sp_r - gr * sp_i

        lp = lp_ref[...]
        edge_pow = jnp.concatenate([lp[7 - i:8 - i, :] for i in range(8)], axis=0)
        cr[...], ci[...] = _scan_block(buf, buf, lp, edge_pow, cr[...], ci[...], True, each)
        g = buf[...].astype(BF16)
        du_ref[...] = (dua_ref[...] + _dg(g, bb_ref[...], 1, 1)).astype(du_ref.dtype)
        dbb_ref[...] += _dg(u_ref[...], g, 0, 0)
        dl_ref[:, :S5_LB] += jnp.sum(acc[0], axis=0, keepdims=True)
        dl_ref[:, S5_LB:] += jnp.sum(acc[1], axis=0, keepdims=True)

    band = pl.BlockSpec((BAND, blk), lambda j, b, t: (j, 0))
    return pl.pallas_call(
        body, name="s5_bwd", grid=(S5_NJ, n_seq, n_t),
        in_specs=[pl.BlockSpec((S5_TB, BAND), rows), pl.BlockSpec((S5_TB, blk), rows),
                  pl.BlockSpec((halo_rows, blk), halo), pl.BlockSpec((S5_TB, BAND), u_rows),
                  pl.BlockSpec((S5_TB, BAND), rows), band, band,
                  pl.BlockSpec((S5_LOG_TB, blk), lambda j, b, t: (0, j)), pl.BlockSpec(memory_space=pl.ANY)],
        out_specs=[pl.BlockSpec((S5_TB, BAND), u_rows), pl.BlockSpec((1, blk), lambda j, b, t: (0, j)), band, band],
        out_shape=[jax.ShapeDtypeStruct(dproj.shape, dproj.dtype), jax.ShapeDtypeStruct((1, S5_NJ * blk), F32),
                   jax.ShapeDtypeStruct(bb_band.shape, F32), jax.ShapeDtypeStruct(c_band.shape, F32)],
        input_output_aliases={8: 0},
        scratch_shapes=[pltpu.VMEM((1, S5_LB), F32), pltpu.VMEM((1, S5_LB), F32), pltpu.VMEM((S5_TB, blk), F32),
                        pltpu.VMEM((S5_TB, blk), F32)],
        compiler_params=_cparams(("parallel", "arbitrary", "arbitrary")),
    )(dyc, s, s, proj, du5_a, bb_band, c_band, lam_pow_conj, dproj)


CONV_TR = 512
CONV_CW = 512


def _shift_down(x, halo, k):
    if k == 0:
        return x
    row8 = lax.broadcasted_iota(jnp.int32, halo.shape, 0)
    rolled = pltpu.roll(x, k, 0)
    top = jnp.where(row8 < k, pltpu.roll(halo, k, 0), rolled[:8])
    if x.shape[0] == 8:
        return top
    return jnp.concatenate([top, rolled[8:]], axis=0)


def _shift_up(x, halo, k):
    if k == 0:
        return x
    n = x.shape[0]
    row8 = lax.broadcasted_iota(jnp.int32, halo.shape, 0)
    rolled = pltpu.roll(x, n - k, 0)
    bot = jnp.where(row8 >= 8 - k, pltpu.roll(halo, 8 - k, 0), rolled[n - 8:])
    if n == 8:
        return bot
    return jnp.concatenate([rolled[:n - 8], bot], axis=0)


def _conv_pre(x, halo, w, b):
    acc = b + w[SSD_CONV - 1:SSD_CONV, :] * x
    for k in range(SSD_CONV - 1):
        acc = acc + w[k:k + 1, :] * _shift_down(x, halo, SSD_CONV - 1 - k)
    return acc


def _conv_specs(seq_len, col_off):
    lt = seq_len // CONV_TR
    cb = col_off // CONV_CW
    cur = pl.BlockSpec((CONV_TR, CONV_CW), lambda j, i: (i, j + cb))
    prev = pl.BlockSpec((8, CONV_CW), lambda j, i: (jnp.maximum(i * (CONV_TR // 8) - 1, 0), j + cb))
    return lt, cur, prev


def _conv_fwd(proj, conv_w, conv_b, n_rows, seq_len):
    lt, cur, prev = _conv_specs(seq_len, OFF_XBC)

    def body(x_ref, h_ref, w_ref, b_ref, o_ref):
        halo = jnp.where(pl.program_id(1) % lt == 0, 0.0, h_ref[...])
        o_ref[...] = jax.nn.silu(_conv_pre(x_ref[...], halo, w_ref[...], b_ref[...]))

    return pl.pallas_call(
        body, name="ssd_conv_fwd", grid=(SSD_CONV_DIM // CONV_CW, n_rows // CONV_TR),
        in_specs=[cur, prev, pl.BlockSpec((SSD_CONV, CONV_CW), lambda j, i: (0, j)),
                  pl.BlockSpec((1, CONV_CW), lambda j, i: (0, j))],
        out_specs=pl.BlockSpec((CONV_TR, CONV_CW), lambda j, i: (i, j)),
        out_shape=jax.ShapeDtypeStruct((n_rows, SSD_CONV_DIM), F32),
        compiler_params=_cparams(("parallel", "parallel")),
    )(proj, proj, conv_w, conv_b)


def _conv_bwd(name, proj, d_act, conv_w, conv_b, n_rows, seq_len, col_off, dproj):
    width = d_act.shape[1]
    lt, cur, prev = _conv_specs(seq_len, OFF_XBC + col_off)
    n_blk = n_rows // CONV_TR
    cb = (OFF_XBC + col_off) // CONV_CW
    pb = col_off // CONV_CW
    nxt = pl.BlockSpec((8, CONV_CW), lambda j, i: (jnp.minimum((i + 1) * (CONV_TR // 8), n_rows // 8 - 1), j + cb))
    d_cur = pl.BlockSpec((CONV_TR, CONV_CW), lambda j, i: (i, j))
    d_nxt = pl.BlockSpec((8, CONV_CW), lambda j, i: (jnp.minimum((i + 1) * (CONV_TR // 8), n_rows // 8 - 1), j))

    def dsilu(pre):
        sg = jax.nn.sigmoid(pre)
        return sg * (1.0 + pre * (1.0 - sg))

    def body(x_ref, hp_ref, hn_ref, d_ref, dn_ref, w_ref, b_ref, _, dx_ref, dw_ref, db_ref):
        i = pl.program_id(1)
        x, w, b = x_ref[...], w_ref[...], b_ref[...]
        halo_p = jnp.where(i % lt == 0, 0.0, hp_ref[...])
        at_end = i % lt == lt - 1
        dpre = d_ref[...] * dsilu(_conv_pre(x, halo_p, w, b))
        pre_n = _conv_pre(hn_ref[...], x[CONV_TR - 8:, :], w, b)
        dpre_n = jnp.where(at_end, 0.0, dn_ref[...] * dsilu(pre_n))
        dx = w[SSD_CONV - 1:SSD_CONV, :] * dpre
        for k in range(SSD_CONV - 1):
            dx = dx + w[k:k + 1, :] * _shift_up(dpre, dpre_n, SSD_CONV - 1 - k)
        dx_ref[...] = dx.astype(dx_ref.dtype)

        @pl.when(i == 0)
        def _():
            dw_ref[...] = jnp.zeros_like(dw_ref)
            db_ref[...] = jnp.zeros_like(db_ref)

        for k in range(SSD_CONV):
            xs = _shift_down(x, halo_p, SSD_CONV - 1 - k)
            dw_ref[k:k + 1, :] += jnp.sum(dpre * xs, axis=0, keepdims=True)
        db_ref[...] += jnp.sum(dpre, axis=0, keepdims=True)

    return pl.pallas_call(
        body, name=name, grid=(width // CONV_CW, n_blk),
        in_specs=[cur, prev, nxt, d_cur, d_nxt,
                  pl.BlockSpec((SSD_CONV, CONV_CW), lambda j, i: (0, j + pb)),
                  pl.BlockSpec((1, CONV_CW), lambda j, i: (0, j + pb)), pl.BlockSpec(memory_space=pl.ANY)],
        out_specs=[pl.BlockSpec((CONV_TR, CONV_CW), lambda j, i: (i, j + cb)),
                   pl.BlockSpec((SSD_CONV, CONV_CW), lambda j, i: (0, j)),
                   pl.BlockSpec((1, CONV_CW), lambda j, i: (0, j))],
        out_shape=[jax.ShapeDtypeStruct(dproj.shape, dproj.dtype),
                   jax.ShapeDtypeStruct((SSD_CONV, width), F32), jax.ShapeDtypeStruct((1, width), F32)],
        input_output_aliases={7: 0},
        compiler_params=_cparams(("parallel", "arbitrary")),
    )(proj, proj, proj, d_act, d_act, conv_w, conv_b, dproj)


def _split3(x):
    hi = x.astype(BF16)
    r = x - hi.astype(F32)
    mid = r.astype(BF16)
    return hi, mid, (r - mid.astype(F32)).astype(BF16)


def _sel_dot(a, b, a_is_sel):
    dn = (((1,), (0,)), ((), ()))
    if a_is_sel:
        return sum(lax.dot_general(a, t, dn, preferred_element_type=F32) for t in _split3(b))
    return sum(lax.dot_general(t, b, dn, preferred_element_type=F32) for t in _split3(a))


@jax.custom_vjp
def sel_left(sel, sel_t, x):
    return _sel_dot(sel, x, True)


@jax.custom_vjp
def sel_right(x, sel, sel_t):
    return _sel_dot(x, sel, False)


sel_left.defvjp(lambda s, st, x: (_sel_dot(s, x, True), (s, st)),
                lambda r, g: (jnp.zeros_like(r[0]), jnp.zeros_like(r[1]), _sel_dot(r[1], g, True)))
sel_right.defvjp(lambda x, s, st: (_sel_dot(x, s, False), (s, st)),
                 lambda r, g: (_sel_dot(g, r[1], False), jnp.zeros_like(r[0]), jnp.zeros_like(r[1])))


def _ssd_chunk(xs, bm, cm, dtr, st, dtb, alog, dsk, k):
    dt = jax.nn.softplus(dtr + dtb)
    acum = sel_left(k["tri"], k["tri_t"], dt * (-jnp.exp(alog)))
    dt_e = sel_right(dt, k["spread"], k["spread_t"])
    ac_e = sel_right(acum, k["spread"], k["spread_t"])
    al_e = ac_e[SSD_CHUNK - 1:SSD_CHUNK, :]
    dsk_e = sel_right(jnp.broadcast_to(dsk, (8, 128)), k["spread"], k["spread_t"])[0:1, :]
    xdt = xs * dt_e
    acum_t = acum.T
    scores = dot_nt(cm, bm)
    y = dot_nn(cm, st) * jnp.exp(ac_e) + xs * dsk_e
    for j in range(SSD_HPG):
        lmat = jnp.exp(jnp.where(k["causal"], acum[:, j:j + 1] - acum_t[j:j + 1, :], -jnp.inf))
        y = y + dot_nn(scores * lmat, jnp.where(k["head"] == j, xdt, 0.0))
    new = st * jnp.exp(al_e) + dot_tn(bm, xdt * jnp.exp(al_e - ac_e))
    return y, new


def _ssd_consts():
    r = lax.broadcasted_iota(jnp.int32, (SSD_CHUNK, SSD_CHUNK), 0)
    c = lax.broadcasted_iota(jnp.int32, (SSD_CHUNK, SSD_CHUNK), 1)
    hd = jnp.int32(SSD_HEADDIM)
    sr = lax.broadcasted_iota(jnp.int32, (128, GROUP_W), 0)
    sc = lax.div(lax.broadcasted_iota(jnp.int32, (128, GROUP_W), 1), hd)
    tr = lax.div(lax.broadcasted_iota(jnp.int32, (GROUP_W, 128), 0), hd)
    tc = lax.broadcasted_iota(jnp.int32, (GROUP_W, 128), 1)
    return {"tri": (r >= c).astype(BF16), "tri_t": (c >= r).astype(BF16), "causal": r >= c,
            "spread": (sr == sc).astype(BF16), "spread_t": (tr == tc).astype(BF16),
            "head": lax.div(lax.broadcasted_iota(jnp.int32, (SSD_CHUNK, GROUP_W), 1), hd)}


def _ssd_specs(n_c, reverse):
    def cidx(c):
        return n_c - 1 - c if reverse else c

    xs = pl.BlockSpec((SSD_CHUNK, GROUP_W), lambda g, b, c: (b * n_c + cidx(c), g))
    bm = pl.BlockSpec((SSD_CHUNK, SSD_STATE), lambda g, b, c: (b * n_c + cidx(c), SSD_WIDTH // SSD_STATE + g))
    cm = pl.BlockSpec((SSD_CHUNK, SSD_STATE), lambda g, b, c: (b * n_c + cidx(c), (SSD_WIDTH + SSD_BC) // SSD_STATE + g))
    dt = pl.BlockSpec((SSD_CHUNK, 128), lambda g, b, c: (b * n_c + cidx(c), OFF_DT // 128 + g))
    par = pl.BlockSpec((1, 128), lambda g, b, c: (0, g))
    st = pl.BlockSpec((1, 1, 1, SSD_STATE, GROUP_W), lambda g, b, c: (b, cidx(c), g, 0, 0))
    return xs, bm, cm, dt, par, st


def _ssd_fwd(xbc_act, proj, dtb, alog, dsk, n_seq, seq_len):
    n_c = seq_len // SSD_CHUNK
    xs_s, bm_s, cm_s, dt_s, par_s, st_s = _ssd_specs(n_c, False)

    def body(xs_ref, bm_ref, cm_ref, dt_ref, dtb_ref, al_ref, dk_ref, y_ref, st_ref, state):
        @pl.when(pl.program_id(2) == 0)
        def _():
            state[...] = jnp.zeros_like(state)

        prev = state[...]
        st_ref[0, 0, 0] = prev
        y, new = _ssd_chunk(xs_ref[...], bm_ref[...], cm_ref[...], dt_ref[...], prev,
                            dtb_ref[...], al_ref[...], dk_ref[...], _ssd_consts())
        y_ref[...] = y
        state[...] = new

    return pl.pallas_call(
        body, name="ssd_fwd", grid=(SSD_GROUPS, n_seq, n_c),
        in_specs=[xs_s, bm_s, cm_s, dt_s, par_s, par_s, par_s],
        out_specs=[pl.BlockSpec((SSD_CHUNK, GROUP_W), lambda g, b, c: (b * n_c + c, g)), st_s],
        out_shape=[jax.ShapeDtypeStruct((n_seq * seq_len, SSD_WIDTH), F32),
                   jax.ShapeDtypeStruct((n_seq, n_c, SSD_GROUPS, SSD_STATE, GROUP_W), F32)],
        scratch_shapes=[pltpu.VMEM((SSD_STATE, GROUP_W), F32)],
        compiler_params=_cparams(("parallel", "parallel", "arbitrary")),
    )(xbc_act, xbc_act, xbc_act, proj, dtb, alog, dsk)


def _ssd_bwd(xbc_act, proj, states, dy, dtb, alog, dsk, n_seq, seq_len, dproj):
    n_c = seq_len // SSD_CHUNK
    n_rows = n_seq * seq_len
    xs_s, bm_s, cm_s, dt_s, par_s, st_s = _ssd_specs(n_c, True)

    def rows(w, first=0):
        return pl.BlockSpec((SSD_CHUNK, w), lambda g, b, c: (b * n_c + (n_c - 1 - c), first + g))

    def body(xs_ref, bm_ref, cm_ref, dt_ref, st_ref, dy_ref, dtb_ref, al_ref, dk_ref, _,
             dxs_ref, dbm_ref, dcm_ref, ddt_ref, ddtb_ref, dal_ref, ddk_ref, dstate):
        b, c = pl.program_id(1), pl.program_id(2)

        @pl.when(c == 0)
        def _():
            dstate[...] = jnp.zeros_like(dstate)

        @pl.when((b == 0) & (c == 0))
        def _():
            ddtb_ref[...] = jnp.zeros_like(ddtb_ref)
            dal_ref[...] = jnp.zeros_like(dal_ref)
            ddk_ref[...] = jnp.zeros_like(ddk_ref)

        consts = _ssd_consts()
        _, vjp = jax.vjp(
            lambda xs, bm, cm, dtr, prev, dtb, alog, dsk: _ssd_chunk(xs, bm, cm, dtr, prev, dtb, alog, dsk, consts),
            xs_ref[...], bm_ref[...], cm_ref[...], dt_ref[...], st_ref[0, 0, 0], dtb_ref[...], al_ref[...], dk_ref[...])
        dxs, dbm, dcm, ddtr, dprev, ddtb, dal, ddk = vjp((dy_ref[...], dstate[...]))
        dxs_ref[...] = dxs
        dbm_ref[...] = dbm
        dcm_ref[...] = dcm
        ddt_ref[...] = ddtr.astype(ddt_ref.dtype)
        ddtb_ref[...] += ddtb
        dal_ref[...] += dal
        ddk_ref[...] += ddk
        dstate[...] = dprev

    acc = pl.BlockSpec((1, 128), lambda g, b, c: (0, g))
    return pl.pallas_call(
        body, name="ssd_bwd", grid=(SSD_GROUPS, n_seq, n_c),
        in_specs=[xs_s, bm_s, cm_s, dt_s, st_s, rows(GROUP_W), par_s, par_s, par_s, pl.BlockSpec(memory_space=pl.ANY)],
        out_specs=[rows(GROUP_W), rows(SSD_STATE), rows(SSD_STATE), rows(128, OFF_DT // 128), acc, acc, acc],
        out_shape=[jax.ShapeDtypeStruct((n_rows, SSD_WIDTH), F32), jax.ShapeDtypeStruct((n_rows, SSD_BC), F32),
                   jax.ShapeDtypeStruct((n_rows, SSD_BC), F32), jax.ShapeDtypeStruct(dproj.shape, dproj.dtype),
                   jax.ShapeDtypeStruct((1, 512), F32), jax.ShapeDtypeStruct((1, 512), F32),
                   jax.ShapeDtypeStruct((1, 512), F32)],
        input_output_aliases={9: 3},
        scratch_shapes=[pltpu.VMEM((SSD_STATE, GROUP_W), F32)],
        compiler_params=_cparams(("parallel", "arbitrary", "arbitrary")),
    )(xbc_act, xbc_act, xbc_act, proj, states, dy, dtb, alog, dsk, dproj)


def _pad_heads(v):
    return jnp.pad(v.reshape(SSD_GROUPS, SSD_HPG), ((0, 0), (0, 128 - SSD_HPG))).reshape(1, SSD_GROUPS * 128)


def _unpad_heads(v):
    return v.reshape(SSD_GROUPS, 128)[:, :SSD_HPG].reshape(1, SSD_HEADS)


def _state_cols(v):
    re, im = v
    lead = re.shape[:-1]
    re = re.reshape(lead + (S5_NJ, 1, S5_LB))
    im = im.reshape(lead + (S5_NJ, 1, S5_LB))
    return jnp.concatenate([re, im], axis=-2).reshape(lead + (2 * S5_N,))


def _state_uncols(v):
    lead = v.shape[:-1]
    v = v.reshape(lead + (S5_NJ, 2, S5_LB))
    return v[..., 0, :].reshape(lead + (S5_N,)), v[..., 1, :].reshape(lead + (S5_N,))


GROUPS_PER_BAND = BAND // S5_GROUP


def _band(w2_re, w2_im):
    gh = S5_GROUPS * S5_GROUP
    rg = ((jnp.arange(gh) // S5_GROUP) % GROUPS_PER_BAND)[:, None, None]
    cg = jnp.arange(GROUPS_PER_BAND)[None, :, None]
    parts = [jnp.where(rg == cg, v[:, None, :], 0.0).reshape(gh, S5_LB) for v in (w2_re, w2_im)]
    return jnp.concatenate(parts, axis=1)


def _band_take(wb):
    gh = S5_GROUPS * S5_GROUP
    w4 = wb.reshape(gh, 2, GROUPS_PER_BAND, S5_STATE)
    sel = w4[jnp.arange(gh), :, (jnp.arange(gh) // S5_GROUP) % GROUPS_PER_BAND, :]
    return sel[:, 0, :], sel[:, 1, :]


W_IN_SHARD = IN_PROJ_DIM // N_CHIPS
W_IN_SEGS = ((0, 512, OFF_U5), (512, 1024, OFF_Z5), (1024, 2560, OFF_ZS), (2560, 5120, OFF_XBC), (5144, 7192, OFF_G5))
DT_ROWS = (5120, 5144)


def _w_in_pieces():
    runs = []
    segs = list(W_IN_SEGS) + [(DT_ROWS[0] + SSD_HPG * g, DT_ROWS[0] + SSD_HPG * (g + 1), OFF_DT + 128 * g)
                              for g in range(SSD_GROUPS)]
    for lo, hi, off in segs:
        for j in range(N_CHIPS):
            s, e = max(lo, j * W_IN_SHARD), min(hi, (j + 1) * W_IN_SHARD)
            if s < e:
                runs.append((j, s - j * W_IN_SHARD, off + s - lo, e - s))
    return runs


RELAYOUT_LANES = 256


def _pad_w_in_t(a4):
    runs = _w_in_pieces()

    def body(a_ref, o_ref):
        o_ref[pl.ds(OFF_DT, DT_W), :] = jnp.zeros((DT_W, RELAYOUT_LANES), o_ref.dtype)
        for j, src, dst, n in runs:
            o_ref[pl.ds(dst, n), :] = a_ref[j, pl.ds(src, n), :]

    return pl.pallas_call(
        body, name="w_in_to_padded", grid=(D_MODEL // RELAYOUT_LANES,),
        in_specs=[pl.BlockSpec((N_CHIPS, W_IN_SHARD, RELAYOUT_LANES), lambda i: (0, 0, i))],
        out_specs=pl.BlockSpec((PROJ_W, RELAYOUT_LANES), lambda i: (0, i)),
        out_shape=jax.ShapeDtypeStruct((PROJ_W, D_MODEL), a4.dtype),
        compiler_params=_cparams(("parallel",)),
    )(a4)


def _unpad_w_in_t(wp):
    runs = _w_in_pieces()

    def body(p_ref, o_ref):
        for j, dst, src, n in runs:
            o_ref[j, pl.ds(dst, n), :] = p_ref[pl.ds(src, n), :]

    return pl.pallas_call(
        body, name="w_in_from_padded", grid=(D_MODEL // RELAYOUT_LANES,),
        in_specs=[pl.BlockSpec((PROJ_W, RELAYOUT_LANES), lambda i: (0, i))],
        out_specs=pl.BlockSpec((N_CHIPS, W_IN_SHARD, RELAYOUT_LANES), lambda i: (0, 0, i)),
        out_shape=jax.ShapeDtypeStruct((N_CHIPS, W_IN_SHARD, D_MODEL), wp.dtype),
        compiler_params=_cparams(("parallel",)),
    )(wp)


def _local_step(x, p, tgt, w):
    n_seq, seq_len, _ = x.shape
    n_rows = n_seq * seq_len
    tr = 512
    x2 = x.reshape(n_rows, D_MODEL)
    p2 = p.reshape(n_rows, PLE_DIM)
    t2 = tgt.reshape(n_rows, D_MODEL)
    row = functools.partial(_rowwise, n_rows=n_rows, tr=tr)

    w_pad_t = _pad_w_in_t(w["w_in_t"])
    norm_w = w["norm_w"].reshape(1, D_MODEL)
    ple_norm_w = w["ple_norm_w"].reshape(1, D_MODEL)
    final_w = w["final_norm_w"].reshape(1, D_MODEL)
    s5_d = w["s5_d"].reshape(1, S5_WIDTH)
    b_glu = w["s5_b_glu"].reshape(1, S5_WIDTH)
    conv_w = w["ssd_conv_w"].reshape(SSD_CONV, SSD_CONV_DIM)
    conv_b = w["ssd_conv_b"].reshape(1, SSD_CONV_DIM)
    ssd_norm_w = w["ssd_norm_w"].reshape(1, SSD_WIDTH)
    dtb, alog, dsk = (_pad_heads(w[k].reshape(1, SSD_HEADS)) for k in ("ssd_dt_bias", "ssd_a_log", "ssd_d"))

    gh = S5_GROUPS * S5_GROUP
    a_re = w["s5_a_re"].reshape(S5_GROUPS, S5_STATE)
    a_im = w["s5_a_im"].reshape(S5_GROUPS, S5_STATE)
    log_step = w["s5_log_step"].reshape(S5_GROUPS, 1)
    b_re2 = jnp.transpose(w["s5_b_re"].reshape(S5_GROUPS, S5_STATE, S5_GROUP), (0, 2, 1)).reshape(gh, S5_STATE)
    b_im2 = jnp.transpose(w["s5_b_im"].reshape(S5_GROUPS, S5_STATE, S5_GROUP), (0, 2, 1)).reshape(gh, S5_STATE)
    expand = (jnp.arange(gh)[:, None] // S5_GROUP == jnp.arange(S5_GROUPS)[None, :]).astype(F32)
    pow_re, pow_im, bb_re2, bb_im2 = _s5_params_fwd(a_re, a_im, log_step, b_re2, b_im2, expand)
    lam_pow = _state_cols((pow_re.reshape(S5_LOG_TB, S5_N), pow_im.reshape(S5_LOG_TB, S5_N)))
    lam_pow_conj = _state_cols((pow_re.reshape(S5_LOG_TB, S5_N), -pow_im.reshape(S5_LOG_TB, S5_N)))
    bb_band = _band(bb_re2, bb_im2).astype(BF16)
    c_band = _band(w["s5_c_re"].reshape(gh, S5_STATE), -w["s5_c_im"].reshape(gh, S5_STATE)).astype(BF16)

    late = w.get("_late")
    (hn,) = row("rms_in", lambda r, q: ([_rms(r[0], q[0])], []), row_ins=[(x2, 0, D_MODEL)],
                par_ins=[(norm_w, 0, D_MODEL)] + ([(late[0], 0, 128)] if late else []), row_outs=[(D_MODEL, BF16)])
    proj = _matmul("mm_proj", hn, w_pad_t, tb=True)
    s, yc, ge = _s5_fwd(proj, bb_band, c_band, lam_pow, s5_d, n_seq, seq_len)
    if late:
        w = {**w, **late[1](ge)}
    tg, y5 = _matmul("mm_s5_glu", ge, w["s5_w_glu"], epilogue=lambda c, r, q: [c, _s5_out(r[0], r[1], r[2], c, *q)],
                     epi_rows=[(yc, 0), (proj, OFF_U5), (proj, OFF_Z5)], epi_pars=[s5_d, b_glu], epi_outs=[F32, BF16])
    s5_rows = [(yc, 0, S5_WIDTH), (proj, OFF_U5, S5_WIDTH), (proj, OFF_Z5, S5_WIDTH), (tg, 0, S5_WIDTH)]
    s5_pars = [(s5_d, 0, S5_WIDTH), (b_glu, 0, S5_WIDTH)]

    xbc_act = _conv_fwd(proj, conv_w, conv_b, n_rows, seq_len)
    y_ssd, states = _ssd_fwd(xbc_act, proj, dtb, alog, dsk, n_seq, seq_len)
    gn_rows = [(y_ssd, 0, SSD_WIDTH), (proj, OFF_ZS, SSD_WIDTH)]
    (yss,) = row("ssd_gate", lambda r, q: ([_gated_norm(r[0], r[1], q[0])], []), row_ins=gn_rows,
                 par_ins=[(ssd_norm_w, 0, SSD_WIDTH)], row_outs=[(SSD_WIDTH, BF16)])

    m5 = _matmul("mm_br_s5", y5, w["w_br_s5"])
    ms, merged = _matmul("mm_br_ssd", yss, w["w_br_ssd"], epilogue=lambda c, r, q: [c, _merge(r[0], r[1], r[2], c)],
                         epi_rows=[(proj, OFF_G5), (proj, OFF_GS), (m5, 0)], epi_outs=[F32, BF16])
    mg_rows = [(proj, OFF_G5, D_MODEL), (proj, OFF_GS, D_MODEL), (m5, 0, D_MODEL), (ms, 0, D_MODEL)]

    def resid_norm(c, r, q):
        h1_ = r[0] + c
        return [h1_, _rms(h1_, q[0])]

    h1, hp = _matmul("mm_out", merged, w["w_out"], epilogue=resid_norm, epi_rows=[(x2, 0)], epi_pars=[ple_norm_w],
                     epi_outs=[F32, BF16], full_rows=True)
    pp = _matmul("mm_ple_proj", p2, w["w_ple_proj"])

    def head_fn(pgl_, r, q):
        h1_, pp_, tgt_ = r
        loss, vjp = jax.vjp(lambda a, b, c, f: _head_loss(a, b, c, f, tgt_), h1_, pgl_, pp_, q[0])
        dh1_, dpgl_, dpp_, dfw_ = vjp(jnp.ones_like(loss))
        return [dh1_, dpgl_, dpp_], [loss, dfw_]

    dh2, dpgl, dpp, loss_acc, d_final_w = _matmul(
        "mm_ple_gate_head", hp, w["w_ple_gate"], epilogue=head_fn, epi_rows=[(h1, 0), (pp, 0), (t2, 0)],
        epi_pars=[final_w], epi_outs=[F32, BF16, BF16], epi_accs=[(1, 128), (1, D_MODEL)], full_rows=True)
    loss = loss_acc[0, 0]

    g = {}
    g["final_norm_w"] = d_final_w
    g["w_ple_gate"] = _matmul("mm_d_w_ple_gate", hp, dpgl, ta=True)
    g["w_ple_proj"] = _matmul("mm_d_w_ple_proj", p2, dpp, ta=True)

    def ple_norm_bwd(dhp_, r, q):
        h1_, dh2_ = r
        _, vjp = jax.vjp(_rms, h1_, q[0])
        dh, dw = vjp(dhp_)
        dh = dh + dh2_
        return [dh, dh], [dw]

    dh1, dh1_b, g["ple_norm_w"] = _matmul(
        "mm_d_hp", dpgl, w["w_ple_gate"], tb=True, epilogue=ple_norm_bwd, epi_rows=[(h1, 0), (dh2, 0)],
        epi_pars=[ple_norm_w], epi_outs=[F32, BF16], epi_accs=[(1, D_MODEL)], full_rows=True)
    g["w_out"] = _matmul("mm_d_w_out", merged, dh1_b, ta=True)

    dproj = lax.empty((n_rows, PROJ_W), BF16)

    def merge_bwd(dmerged, r, q):
        sg5, sgs = jax.nn.sigmoid(r[0]), jax.nn.sigmoid(r[1])
        d_gates = jnp.concatenate([dmerged * r[2] * sg5 * (1.0 - sg5), dmerged * r[3] * sgs * (1.0 - sgs)], axis=1)
        return [d_gates, dmerged * sg5, dmerged * sgs]

    dproj, dm5, dms = _matmul(
        "mm_d_merged", dh1_b, w["w_out"], tb=True, epilogue=merge_bwd,
        epi_rows=[(proj, OFF_G5), (proj, OFF_GS), (m5, 0), (ms, 0)], epi_outs=[BF16, BF16, BF16], full_rows=True,
        epi_into=(dproj, OFF_G5, 2 * D_MODEL))
    g["w_br_s5"] = _matmul("mm_d_w_br_s5", y5, dm5, ta=True)
    g["w_br_ssd"] = _matmul("mm_d_w_br_ssd", yss, dms, ta=True)
    dy5 = _matmul("mm_d_y5", dm5, w["w_br_s5"], tb=True)
    dyss = _matmul("mm_d_yss", dms, w["w_br_ssd"], tb=True)

    def s5_out_bwd_a(r, q):
        yc_, u_, z_, t_, dy_ = r
        d_, bg_ = q
        ge_ = jax.nn.gelu(yc_ + d_ * u_)
        _, vjp = jax.vjp(lambda a, z, t, b: a * jax.nn.sigmoid(t + b) * jax.nn.silu(z), ge_, z_, t_, bg_)
        dge, dz, dt_, dbg = vjp(dy_)
        return [dz, dge, dt_], [dbg]

    dproj, dge_a, dtg, g["s5_b_glu"] = row(
        "s5_out_bwd_a", s5_out_bwd_a, row_ins=s5_rows + [(dy5, 0, S5_WIDTH)], par_ins=s5_pars,
        row_outs=[(S5_WIDTH, BF16), (S5_WIDTH, F32), (S5_WIDTH, BF16)], acc_outs=[(1, S5_WIDTH)],
        into=(dproj, OFF_Z5))
    g["s5_w_glu"] = _matmul("mm_d_w_glu", ge, dtg, ta=True)
    early = w["_early"](g) if "_early" in w else None
    dge_b = _matmul("mm_d_ge", dtg, w["s5_w_glu"], tb=True)

    def s5_out_bwd_b(r, q):
        yc_, u_, da_, db_ = r
        _, vjp = jax.vjp(lambda yc, u, d: jax.nn.gelu(yc + d * u), yc_, u_, q[0])
        dyc_, du_, dd_ = vjp(da_ + db_)
        return [dyc_, du_], [dd_]

    dyc, du5_a, g["s5_d"] = row(
        "s5_out_bwd_b", s5_out_bwd_b,
        row_ins=[(yc, 0, S5_WIDTH), (proj, OFF_U5, S5_WIDTH), (dge_a, 0, S5_WIDTH), (dge_b, 0, S5_WIDTH)],
        par_ins=[(s5_d, 0, S5_WIDTH)] + ([(early, 0, 128)] if early is not None else []),
        row_outs=[(S5_WIDTH, BF16), (S5_WIDTH, F32)], acc_outs=[(1, S5_WIDTH)])
    dproj, d_lam, d_bb_band, d_c_band = _s5_bwd(dyc, s, proj, du5_a, bb_band, c_band, lam_pow_conj, dproj,
                                                n_seq, seq_len)

    d_lr, d_li = _state_uncols(d_lam)
    d_bbr, d_bbi = _band_take(d_bb_band)
    d_are, d_aim, d_ls, d_br2, d_bi2 = _s5_params_bwd(
        a_re, a_im, log_step, b_re2, b_im2, expand,
        d_lr.reshape(S5_GROUPS, S5_STATE), d_li.reshape(S5_GROUPS, S5_STATE), d_bbr, d_bbi)
    g["s5_a_re"], g["s5_a_im"], g["s5_log_step"] = d_are, d_aim, d_ls
    g["s5_b_re_ghp"], g["s5_b_im_ghp"] = d_br2, d_bi2
    d_cr, d_ci = _band_take(d_c_band)
    g["s5_c_re"], g["s5_c_im"] = d_cr, -d_ci

    def gate_bwd(r, q):
        _, vjp = jax.vjp(_gated_norm, r[0], r[1], q[0])
        dy_, dz_, dw_ = vjp(r[2])
        return [dz_, dy_], [dw_]

    dproj, dy_ssd, g["ssd_norm_w"] = row(
        "ssd_gate_bwd", gate_bwd, row_ins=gn_rows + [(dyss, 0, SSD_WIDTH)], par_ins=[(ssd_norm_w, 0, SSD_WIDTH)],
        row_outs=[(SSD_WIDTH, BF16), (SSD_WIDTH, F32)], acc_outs=[(1, SSD_WIDTH)], into=(dproj, OFF_ZS))
    dxs, dbm, dcm, dproj, d_dtb, d_alog, d_dsk = _ssd_bwd(xbc_act, proj, states, dy_ssd, dtb, alog, dsk, n_seq, seq_len,
                                                       dproj)
    g["ssd_dt_bias"], g["ssd_a_log"], g["ssd_d"] = _unpad_heads(d_dtb), _unpad_heads(d_alog), _unpad_heads(d_dsk)
    conv_dw, conv_db = [], []
    for nm, d_act, off in (("x", dxs, 0), ("b", dbm, SSD_WIDTH), ("c", dcm, SSD_WIDTH + SSD_BC)):
        dproj, dw_, db_ = _conv_bwd("ssd_conv_bwd_" + nm, proj, d_act, conv_w, conv_b, n_rows, seq_len, off, dproj)
        conv_dw.append(dw_)
        conv_db.append(db_)
    g["ssd_conv_w"] = jnp.concatenate(conv_dw, axis=1)
    g["ssd_conv_b"] = jnp.concatenate(conv_db, axis=1)

    g["w_in_t"] = _unpad_w_in_t(_matmul("mm_d_w_in", dproj, hn, ta=True))
    def norm_bwd(dhn_, r, q):
        x_, dh1_ = r
        _, vjp = jax.vjp(_rms, x_, q[0])
        dx_, dw_ = vjp(dhn_)
        return [dx_ + dh1_], [dw_]

    dx, g["norm_w"] = _matmul("mm_d_hn", dproj, w_pad_t, epilogue=norm_bwd, epi_rows=[(x2, 0), (dh1, 0)],
                              epi_pars=[norm_w], epi_outs=[F32], epi_accs=[(1, D_MODEL)], full_rows=True)
    return loss, dx.reshape(x.shape), g


HBM = pl.BlockSpec(memory_space=pltpu.HBM)


def _chip_index(x, y):
    return 2 * x + y


def _half(shape2d, axis, which):
    h = shape2d[axis] // 2
    sl = pl.ds(pl.multiple_of(which * h, 128 if axis else 8), h)
    return (slice(None), sl) if axis else (sl, slice(None))


def _gather_chips(split, axes, whole):
    ns, nw = len(split), len(whole)
    n = ns + nw

    def body(*refs):
        ins, outs = refs[:n], refs[n:2 * n]
        ici_send, ici_recv, d2d_send, d2d_recv, local_sems = refs[2 * n:]
        x, y, c = lax.axis_index("x"), lax.axis_index("y"), lax.axis_index("c")
        me = _chip_index(x, y)
        sibling = (x, y, 1 - c)
        peers = [(1 - x, y), (x, 1 - y), (1 - x, 1 - y)]

        def half(t, which):
            return _half(split[t].shape, axes[t], which)

        copies = []
        for t in range(n):
            loc = pltpu.make_async_copy(ins[t], outs[t].at[me], local_sems.at[t])
            loc.start()
            copies.append(loc)

        def ici(t, k, slot):
            px, py = peers[k]
            if t < ns:
                src, dst = ins[t].at[half(t, c)], outs[t].at[(slot,) + half(t, c)]
            else:
                src, dst = ins[t], outs[t].at[slot]
            return pltpu.make_async_remote_copy(src_ref=src, dst_ref=dst, send_sem=ici_send.at[t, k],
                                                recv_sem=ici_recv.at[t, k], device_id=(px, py, c), device_id_type=MESH)

        def d2d(t, k, which):
            rows = outs[t].at[(_chip_index(*peers[k]),) + half(t, which)]
            return pltpu.make_async_remote_copy(src_ref=rows, dst_ref=rows, send_sem=d2d_send.at[t, k],
                                                recv_sem=d2d_recv.at[t, k], device_id=sibling, device_id_type=MESH)

        sends = []
        for t in range(n):
            for k in range(3):
                cp = ici(t, k, me)
                cp.start()
                sends.append(cp)
        for t in range(n):
            for k in range(3):
                ici(t, k, _chip_index(*peers[k])).wait_recv()
                if t < ns:
                    cp = d2d(t, k, c)
                    cp.start()
                    sends.append(cp)
        for t in range(ns):
            for k in range(3):
                d2d(t, k, 1 - c).wait_recv()
        for cp in sends:
            cp.wait_send()
        for cp in copies:
            cp.wait()

    arrays = list(split) + list(whole)
    return pl.pallas_call(
        body, name="gather_weights",
        in_specs=[HBM] * n, out_specs=[HBM] * n,
        out_shape=[jax.ShapeDtypeStruct((N_CHIPS,) + a.shape, a.dtype) for a in arrays],
        scratch_shapes=[pltpu.SemaphoreType.DMA((n, 3)), pltpu.SemaphoreType.DMA((n, 3)),
                        pltpu.SemaphoreType.DMA((ns, 3)), pltpu.SemaphoreType.DMA((ns, 3)),
                        pltpu.SemaphoreType.DMA((n,))],
    )(*arrays)


SEM = pl.BlockSpec(memory_space=pltpu.SEMAPHORE)
DATAFLOW = pltpu.SideEffectType.DATAFLOW_SIDE_EFFECTING


def _gather_start(shards, after):
    n = len(shards)

    def body(*refs):
        ins, lands = refs[:n], refs[n:2 * n]
        send_sems, recv_sems = refs[2 * n + 1], refs[2 * n + 2]
        token = refs[-1]
        x, y, c = lax.axis_index("x"), lax.axis_index("y"), lax.axis_index("c")
        me = _chip_index(x, y)
        for t in range(n):
            for k, (px, py) in enumerate([(1 - x, y), (x, 1 - y), (1 - x, 1 - y)]):
                pltpu.make_async_remote_copy(
                    src_ref=ins[t], dst_ref=lands[t].at[me], send_sem=send_sems.at[3 * t + k],
                    recv_sem=recv_sems.at[3 * t + k],
                    device_id=(px, py, c), device_id_type=MESH).start()
        token[...] = jnp.zeros_like(token)

    zones = [lax.empty((N_CHIPS,) + a.shape, a.dtype) for a in shards]
    res = pl.pallas_call(
        body, name="gather_rest_start",
        out_shape=(pltpu.SemaphoreType.DMA((3 * n,)), pltpu.SemaphoreType.DMA((3 * n,)),
                   *[pltpu.HBM(a.shape, a.dtype) for a in shards], *[pltpu.HBM(z.shape, z.dtype) for z in zones],
                   jax.ShapeDtypeStruct((8, 128), F32)),
        in_specs=[HBM] * (2 * n) + [pl.BlockSpec(memory_space=pl.ANY)],
        out_specs=(SEM, SEM, *[HBM] * (2 * n), pl.BlockSpec(memory_space=pltpu.VMEM)),
        input_output_aliases={t: 2 + t for t in range(2 * n)},
        compiler_params=pltpu.CompilerParams(has_side_effects=DATAFLOW),
    )(*[pltpu.with_memory_space_constraint(a, pltpu.HBM) for a in shards],
      *[pltpu.with_memory_space_constraint(z, pltpu.HBM) for z in zones], after)
    return res[0], res[1], list(res[2:2 + n]), list(res[2 + n:2 + 2 * n]), res[-1]


def _gather_wait(send_sems, recv_sems, thru, lands, after):
    n = len(thru)

    def body(*refs):
        ins, zones = refs[:n], refs[n:2 * n]
        s_sems, r_sems = refs[2 * n], refs[2 * n + 1]
        x, y, c = lax.axis_index("x"), lax.axis_index("y"), lax.axis_index("c")
        for t in range(n):
            for k, (px, py) in enumerate([(1 - x, y), (x, 1 - y), (1 - x, 1 - y)]):
                cp = pltpu.make_async_remote_copy(
                    src_ref=ins[t], dst_ref=zones[t].at[_chip_index(px, py)], send_sem=s_sems.at[3 * t + k],
                    recv_sem=r_sems.at[3 * t + k], device_id=(px, py, c), device_id_type=MESH)
                cp.wait_send()
                cp.wait_recv()

    res = pl.pallas_call(
        body, name="gather_rest_wait",
        out_shape=(*[pltpu.HBM(a.shape, a.dtype) for a in thru], *[pltpu.HBM(z.shape, z.dtype) for z in lands]),
        in_specs=[HBM] * (2 * n) + [SEM, SEM, pl.BlockSpec(memory_space=pl.ANY)], out_specs=[HBM] * (2 * n),
        input_output_aliases={t: t for t in range(2 * n)},
        compiler_params=pltpu.CompilerParams(has_side_effects=DATAFLOW),
    )(*thru, *lands, send_sems, recv_sems, after)
    return list(res[:n]), list(res[n:])


def _scatter_start(slotted, after):
    n = len(slotted)

    def body(*refs):
        ins, lands = refs[:n], refs[n:2 * n]
        send_sems, recv_sems = refs[2 * n + 1], refs[2 * n + 2]
        token = refs[-1]
        x, y, c = lax.axis_index("x"), lax.axis_index("y"), lax.axis_index("c")
        for t in range(n):
            for k, (px, py) in enumerate([(1 - x, y), (x, 1 - y), (1 - x, 1 - y)]):
                pltpu.make_async_remote_copy(
                    src_ref=ins[t].at[_chip_index(px, py)], dst_ref=lands[t].at[k], send_sem=send_sems.at[3 * t + k],
                    recv_sem=recv_sems.at[3 * t + k], device_id=(px, py, c), device_id_type=MESH).start()
        token[...] = jnp.zeros_like(token)

    zones = [lax.empty((3,) + a.shape[1:], a.dtype) for a in slotted]
    res = pl.pallas_call(
        body, name="scatter_rest_start",
        out_shape=(pltpu.SemaphoreType.DMA((3 * n,)), pltpu.SemaphoreType.DMA((3 * n,)),
                   *[pltpu.HBM(a.shape, a.dtype) for a in slotted], *[pltpu.HBM(z.shape, z.dtype) for z in zones],
                   jax.ShapeDtypeStruct((8, 128), F32)),
        in_specs=[HBM] * (2 * n) + [pl.BlockSpec(memory_space=pl.ANY)],
        out_specs=(SEM, SEM, *[HBM] * (2 * n), pl.BlockSpec(memory_space=pltpu.VMEM)),
        input_output_aliases={t: 2 + t for t in range(2 * n)},
        compiler_params=pltpu.CompilerParams(has_side_effects=DATAFLOW),
    )(*[pltpu.with_memory_space_constraint(a, pltpu.HBM) for a in slotted],
      *[pltpu.with_memory_space_constraint(z, pltpu.HBM) for z in zones], after)
    return res[0], res[1], list(res[2:2 + n]), list(res[2 + n:2 + 2 * n]), res[-1]


def _scatter_wait(send_sems, recv_sems, thru, lands, after):
    n = len(thru)

    def body(*refs):
        ins, zones = refs[:n], refs[n:2 * n]
        s_sems, r_sems = refs[2 * n], refs[2 * n + 1]
        x, y, c = lax.axis_index("x"), lax.axis_index("y"), lax.axis_index("c")
        for t in range(n):
            for k, (px, py) in enumerate([(1 - x, y), (x, 1 - y), (1 - x, 1 - y)]):
                cp = pltpu.make_async_remote_copy(
                    src_ref=ins[t].at[_chip_index(px, py)], dst_ref=zones[t].at[k], send_sem=s_sems.at[3 * t + k],
                    recv_sem=r_sems.at[3 * t + k], device_id=(px, py, c), device_id_type=MESH)
                cp.wait_send()
                cp.wait_recv()

    res = pl.pallas_call(
        body, name="scatter_rest_wait",
        out_shape=(*[pltpu.HBM(a.shape, a.dtype) for a in thru], *[pltpu.HBM(z.shape, z.dtype) for z in lands]),
        in_specs=[HBM] * (2 * n) + [SEM, SEM, pl.BlockSpec(memory_space=pl.ANY)], out_specs=[HBM] * (2 * n),
        input_output_aliases={t: t for t in range(2 * n)},
        compiler_params=pltpu.CompilerParams(has_side_effects=DATAFLOW),
    )(*thru, *lands, send_sems, recv_sems, after)
    return list(res[:n]), list(res[n:])


def _sum_chips(name, slotted, recv, idx):
    _, r, c = slotted.shape
    tr = _row_tile(r, 5 * c * 4)

    def body(idx_ref, own_ref, r_ref, o_ref):
        acc = own_ref[0]
        for k in range(3):
            acc = acc + r_ref[k]
        o_ref[...] = acc

    grid_spec = pltpu.PrefetchScalarGridSpec(
        num_scalar_prefetch=1, grid=(r // tr,),
        in_specs=[pl.BlockSpec((1, tr, c), lambda i, s: (s[1], i, 0)), pl.BlockSpec((3, tr, c), lambda i, s: (0, i, 0))],
        out_specs=pl.BlockSpec((tr, c), lambda i, s: (i, 0)))
    return pl.pallas_call(
        body, name=name, grid_spec=grid_spec, out_shape=jax.ShapeDtypeStruct((r, c), F32),
        compiler_params=_cparams(("parallel",)),
    )(idx, slotted, recv)


def _half_shape(shape2d, axis):
    r, c = shape2d
    return (r, c // 2) if axis else (r // 2, c)


def _swap_halves(slotted, axes, small):
    n = len(slotted)

    def body(*refs):
        ins, sm_in = refs[:n], refs[n]
        outs, sm_out = refs[n + 1:2 * n + 1], refs[2 * n + 1]
        send_sems, recv_sems, local_sem = refs[2 * n + 2:]
        x, y, c = lax.axis_index("x"), lax.axis_index("y"), lax.axis_index("c")
        local = pltpu.make_async_copy(sm_in, sm_out.at[c], local_sem)
        local.start()
        sends = []
        for t in range(n):
            other = (slice(None),) + _half(slotted[t].shape[1:], axes[t], 1 - c)
            cp = pltpu.make_async_remote_copy(
                src_ref=ins[t].at[other], dst_ref=outs[t], send_sem=send_sems.at[t],
                recv_sem=recv_sems.at[t], device_id=(x, y, 1 - c), device_id_type=MESH)
            cp.start()
            sends.append(cp)
        cp = pltpu.make_async_remote_copy(
            src_ref=sm_in, dst_ref=sm_out.at[c], send_sem=send_sems.at[n], recv_sem=recv_sems.at[n],
            device_id=(x, y, 1 - c), device_id_type=MESH)
        cp.start()
        sends.append(cp)
        for cp in sends[:n]:
            cp.wait_recv()
        pltpu.make_async_remote_copy(
            src_ref=sm_in, dst_ref=sm_out.at[1 - c], send_sem=send_sems.at[n], recv_sem=recv_sems.at[n],
            device_id=(x, y, 1 - c), device_id_type=MESH).wait_recv()
        for cp in sends:
            cp.wait_send()
        local.wait()

    return pl.pallas_call(
        body, name="swap_halves",
        in_specs=[HBM] * (n + 1), out_specs=[HBM] * (n + 1),
        out_shape=[jax.ShapeDtypeStruct((a.shape[0],) + _half_shape(a.shape[1:], ax), a.dtype)
                   for a, ax in zip(slotted, axes, strict=True)]
        + [jax.ShapeDtypeStruct((2,) + small.shape, small.dtype)],
        scratch_shapes=[pltpu.SemaphoreType.DMA((n + 1,)), pltpu.SemaphoreType.DMA((n + 1,)), pltpu.SemaphoreType.DMA],
    )(*slotted, small)


def _scatter_halves(parts, small):
    n = len(parts)
    half_rows = small.shape[0] // 2

    def body(*refs):
        ins, sm_in = refs[:n], refs[n]
        outs, sm_out = refs[n + 1:2 * n + 1], refs[2 * n + 1]
        send_sems, recv_sems, sm_send, sm_recv, local_sem = refs[2 * n + 2:]
        x, y, c = lax.axis_index("x"), lax.axis_index("y"), lax.axis_index("c")
        me = _chip_index(x, y)
        peers = [(1 - x, y), (x, 1 - y), (1 - x, 1 - y)]
        mine = sm_in.at[pl.ds(pl.multiple_of(c * half_rows, 8), half_rows), :]
        local = pltpu.make_async_copy(mine, sm_out.at[me], local_sem)
        local.start()
        sends = []
        for k, (px, py) in enumerate(peers):
            cp = pltpu.make_async_remote_copy(src_ref=mine, dst_ref=sm_out.at[me], send_sem=sm_send.at[k],
                                              recv_sem=sm_recv.at[k], device_id=(px, py, c), device_id_type=MESH)
            cp.start()
            sends.append(cp)
        for t in range(n):
            for k, (px, py) in enumerate(peers):
                cp = pltpu.make_async_remote_copy(
                    src_ref=ins[t].at[_chip_index(px, py)], dst_ref=outs[t].at[k], send_sem=send_sems.at[t, k],
                    recv_sem=recv_sems.at[t, k], device_id=(px, py, c), device_id_type=MESH)
                cp.start()
                sends.append(cp)
        for k, (px, py) in enumerate(peers):
            pltpu.make_async_remote_copy(src_ref=mine, dst_ref=sm_out.at[_chip_index(px, py)], send_sem=sm_send.at[k],
                                         recv_sem=sm_recv.at[k], device_id=(px, py, c), device_id_type=MESH).wait_recv()
        for cp in sends[3:]:
            cp.wait_recv()
        for cp in sends:
            cp.wait_send()
        local.wait()

    return pl.pallas_call(
        body, name="scatter_halves",
        in_specs=[HBM] * (n + 1), out_specs=[HBM] * (n + 1),
        out_shape=[jax.ShapeDtypeStruct((3,) + a.shape[1:], a.dtype) for a in parts]
        + [jax.ShapeDtypeStruct((N_CHIPS, half_rows, small.shape[1]), small.dtype)],
        scratch_shapes=[pltpu.SemaphoreType.DMA((n, 3)), pltpu.SemaphoreType.DMA((n, 3)),
                        pltpu.SemaphoreType.DMA((3,)), pltpu.SemaphoreType.DMA((3,)), pltpu.SemaphoreType.DMA],
    )(*parts, small)


def _tiling(r, c, f32_per_elem):
    if r % 8 == 0:
        tr = _row_tile(r, f32_per_elem * c * 4)
        return r // tr, (tr, c), lambda i: (i, 0)
    assert c % 128 == 0, (r, c)
    return c // 128, (r, 128), lambda i: (0, i)


def _pair_sum(name, slotted, other, idx, axis):
    _, r, c = slotted.shape
    hr, hc = _half_shape((r, c), axis)
    n, (tr, tc), at = _tiling(hr, hc, 13)
    if axis == 0:
        a = slotted.reshape(N_CHIPS, 2, hr, c)
        a_all = pl.BlockSpec((N_CHIPS, 1, tr, tc), lambda i, s: (0, s[0]) + at(i))
        a_own = pl.BlockSpec((1, 1, tr, tc), lambda i, s: (s[1], s[0]) + at(i))
    else:
        a, per_half = slotted, hc // tc
        a_all = pl.BlockSpec((N_CHIPS, tr, tc), lambda i, s: (0, at(i)[0], s[0] * per_half + at(i)[1]))
        a_own = pl.BlockSpec((1, tr, tc), lambda i, s: (s[1], at(i)[0], s[0] * per_half + at(i)[1]))

    def body(idx_ref, a_ref, b_ref, am_ref, bm_ref, p_ref, own_ref):
        mine, mine_own = (a_ref[:, 0], am_ref[0, 0]) if axis == 0 else (a_ref[...], am_ref[0])
        p_ref[...] = (mine + b_ref[...]).astype(p_ref.dtype)
        own_ref[...] = mine_own + bm_ref[0]

    grid_spec = pltpu.PrefetchScalarGridSpec(
        num_scalar_prefetch=1, grid=(n,),
        in_specs=[a_all, pl.BlockSpec((N_CHIPS, tr, tc), lambda i, s: (0,) + at(i)),
                  a_own, pl.BlockSpec((1, tr, tc), lambda i, s: (s[1],) + at(i))],
        out_specs=[pl.BlockSpec((N_CHIPS, tr, tc), lambda i, s: (0,) + at(i)),
                   pl.BlockSpec((tr, tc), lambda i, s: at(i))])
    return pl.pallas_call(
        body, name=name, grid_spec=grid_spec,
        out_shape=[jax.ShapeDtypeStruct((N_CHIPS, hr, hc), BF16), jax.ShapeDtypeStruct((hr, hc), F32)],
        compiler_params=_cparams(("parallel",)),
    )(idx, a, other, a, other)


def _sum_parts(name, own, recv):
    h, c = own.shape
    n, (tr, tc), at = _tiling(h, c, 4)

    def body(o_ref, r_ref, out_ref):
        acc = o_ref[...]
        for k in range(3):
            acc = acc + r_ref[k].astype(F32)
        out_ref[...] = acc

    return pl.pallas_call(
        body, name=name, grid=(n,),
        in_specs=[pl.BlockSpec((tr, tc), at), pl.BlockSpec((3, tr, tc), lambda i: (0,) + at(i))],
        out_specs=pl.BlockSpec((tr, tc), at),
        out_shape=jax.ShapeDtypeStruct((h, c), F32),
        compiler_params=_cparams(("parallel",)),
    )(own, recv)


def _swap_sibling(parts):
    n = len(parts)

    def body(*refs):
        ins, outs = refs[:n], refs[n:2 * n]
        send_sems, recv_sems = refs[2 * n:]
        x, y, c = lax.axis_index("x"), lax.axis_index("y"), lax.axis_index("c")
        cps = []
        for t in range(n):
            cp = pltpu.make_async_remote_copy(
                src_ref=ins[t], dst_ref=outs[t], send_sem=send_sems.at[t], recv_sem=recv_sems.at[t],
                device_id=(x, y, 1 - c), device_id_type=MESH)
            cp.start()
            cps.append(cp)
        for cp in cps:
            cp.wait_recv()
        for cp in cps:
            cp.wait_send()

    return pl.pallas_call(
        body, name="swap_sibling",
        in_specs=[HBM] * n, out_specs=[HBM] * n,
        out_shape=[jax.ShapeDtypeStruct(a.shape, a.dtype) for a in parts],
        scratch_shapes=[pltpu.SemaphoreType.DMA((n,)), pltpu.SemaphoreType.DMA((n,))],
    )(*parts)


def _sum_slots(name, a):
    k, r, c = a.shape
    tr = _row_tile(r, (k + 1) * c * 4)

    def body(a_ref, o_ref):
        acc = a_ref[0]
        for i in range(1, k):
            acc = acc + a_ref[i]
        o_ref[...] = acc

    return pl.pallas_call(
        body, name=name, grid=(r // tr,),
        in_specs=[pl.BlockSpec((k, tr, c), lambda i: (0, i, 0))],
        out_specs=pl.BlockSpec((tr, c), lambda i: (i, 0)),
        out_shape=jax.ShapeDtypeStruct((r, c), a.dtype),
        compiler_params=_cparams(("parallel",)),
    )(a)


def _adamw(name, w, m, v, g_parts):
    r, c = w.shape
    ng = len(g_parts)
    tr = _row_tile(r, (7 + ng) * c * 4)
    c1 = 1.0 - ADAM_B1 ** ADAM_STEP
    c2 = 1.0 - ADAM_B2 ** ADAM_STEP

    def body(*refs):
        w_ref, m_ref, v_ref = refs[:3]
        g_refs = refs[3:3 + ng]
        go_ref, d_ref, mo_ref, vo_ref = refs[3 + ng:]
        g = g_refs[0][...]
        for gr in g_refs[1:]:
            g = g + gr[...]
        m_new = ADAM_B1 * m_ref[...] + (1.0 - ADAM_B1) * g
        v_new = ADAM_B2 * v_ref[...] + (1.0 - ADAM_B2) * (g * g)
        go_ref[...] = g
        mo_ref[...] = m_new
        vo_ref[...] = v_new
        d_ref[...] = -ADAM_LR * ((m_new / c1) / (jnp.sqrt(v_new / c2) + ADAM_EPS) + ADAM_WD * w_ref[...])

    spec = pl.BlockSpec((tr, c), lambda i: (i, 0))
    return pl.pallas_call(
        body, name=name, grid=(r // tr,),
        in_specs=[spec] * (3 + ng), out_specs=[spec] * 4,
        out_shape=[jax.ShapeDtypeStruct((r, c), F32)] * 4,
        compiler_params=_cparams(("parallel",)),
    )(w, m, v, *g_parts)


def _adamw_halves(name, w, m, v, own, other, idx, axis):
    hr, hc = own.shape
    nb, (tr, tc), at = _tiling(hr, hc, 9)
    c1 = 1.0 - ADAM_B1 ** ADAM_STEP
    c2 = 1.0 - ADAM_B2 ** ADAM_STEP

    def body(idx_ref, w_ref, m_ref, v_ref, own_ref, oth_ref, go_ref, d_ref, mo_ref, vo_ref):
        g = jnp.where(pl.program_id(0) == idx_ref[0], own_ref[...], oth_ref[...])
        m_new = ADAM_B1 * m_ref[...] + (1.0 - ADAM_B1) * g
        v_new = ADAM_B2 * v_ref[...] + (1.0 - ADAM_B2) * (g * g)
        go_ref[...] = g
        mo_ref[...] = m_new
        vo_ref[...] = v_new
        d_ref[...] = -ADAM_LR * ((m_new / c1) / (jnp.sqrt(v_new / c2) + ADAM_EPS) + ADAM_WD * w_ref[...])

    per_half = (hc // tc) if axis else (hr // tr)
    if axis:
        full = pl.BlockSpec((tr, tc), lambda hh, i, s: (at(i)[0], hh * per_half + at(i)[1]))
    else:
        full = pl.BlockSpec((tr, tc), lambda hh, i, s: (hh * per_half + at(i)[0], at(i)[1]))
    part = pl.BlockSpec((tr, tc), lambda hh, i, s: at(i))
    grid_spec = pltpu.PrefetchScalarGridSpec(
        num_scalar_prefetch=1, grid=(2, nb), in_specs=[full, full, full, part, part], out_specs=[full] * 4)
    return pl.pallas_call(
        body, name=name, grid_spec=grid_spec, out_shape=[jax.ShapeDtypeStruct(w.shape, F32)] * 4,
        compiler_params=_cparams(("parallel", "parallel")),
    )(idx, w, m, v, own, other)


WEIGHTS = ['norm_w', 'w_in', 's5_a_re', 's5_a_im', 's5_b_re', 's5_b_im', 's5_c_re', 's5_c_im', 's5_d', 's5_log_step',
           's5_w_glu', 's5_b_glu', 'ssd_conv_w', 'ssd_conv_b', 'ssd_dt_bias', 'ssd_a_log', 'ssd_d', 'ssd_norm_w',
           'w_br_s5', 'w_br_ssd', 'w_out', 'ple_norm_w', 'w_ple_gate', 'w_ple_proj', 'final_norm_w']
SHARDED = {'w_in': ((IN_PROJ_DIM, 1024), 0), 's5_w_glu': ((512, 512), 0), 'ssd_conv_w': ((SSD_CONV, SSD_CONV_DIM), 1),
           'w_br_s5': ((512, 1024), 1), 'w_br_ssd': ((1536, 1024), 0), 'w_out': ((1024, 1024), 0),
           'w_ple_gate': ((1024, 1024), 0), 'w_ple_proj': ((256, 1024), 1)}
TRANSPOSED = ('w_in',)
SMALL = [n for n in WEIGHTS if n not in SHARDED]


def _shard_shape(name):
    (r, c), ax = SHARDED[name]
    return (r // N_CHIPS, c) if ax == 0 else (r, c // N_CHIPS)


def _half_axis(name):
    return 0 if (_shard_shape(name)[0] // 2) % 16 == 0 else 1


def _shard2d(name, a):
    r, c = _shard_shape(name)
    return a.reshape(c, r).T if name in TRANSPOSED else a.reshape(r, c)


def _unshard2d(name, a2, shape):
    return (a2.T if name in TRANSPOSED else a2).reshape(shape)


def _unslot(name, a4):
    (r, c), ax = SHARDED[name]
    if ax == 0:
        return a4.reshape(r, c)
    return jnp.transpose(a4, (1, 0, 2)).reshape(r, c)


def _slot(name, full):
    (r, c), ax = SHARDED[name]
    if ax == 0:
        return full.reshape(N_CHIPS, r // N_CHIPS, c)
    return jnp.transpose(full.reshape(r, N_CHIPS, c // N_CHIPS), (1, 0, 2))


GHP = ('s5_b_re', 's5_b_im')


def _view_shape(name):
    if name in ('s5_a_re', 's5_a_im'):
        return (S5_GROUPS, S5_STATE)
    if name in GHP + ('s5_c_re', 's5_c_im'):
        return (S5_GROUPS, S5_GROUP, S5_STATE)
    if name == 'ssd_conv_w':
        return (SSD_CONV, SSD_CONV_DIM // N_CHIPS)
    return (1, {'s5_log_step': S5_GROUPS, 'ssd_conv_b': SSD_CONV_DIM, 'ssd_norm_w': SSD_WIDTH, 's5_d': S5_WIDTH,
                's5_b_glu': S5_WIDTH, 'ssd_dt_bias': SSD_HEADS, 'ssd_a_log': SSD_HEADS, 'ssd_d': SSD_HEADS}.get(name, D_MODEL))


def _view(name, a):
    if name in GHP:
        return jnp.swapaxes(a.reshape(S5_GROUPS, S5_STATE, S5_GROUP), 1, 2)
    return a.reshape(_view_shape(name))


def _unview(name, a, shape):
    return (jnp.swapaxes(a, 1, 2) if name in GHP else a).reshape(shape)


def _adamw_small(ws, ms, vs, gs):
    n = len(ws)
    c1 = 1.0 - ADAM_B1 ** ADAM_STEP
    c2 = 1.0 - ADAM_B2 ** ADAM_STEP

    def body(*refs):
        w_r, m_r, v_r, g_r = (refs[k * n:(k + 1) * n] for k in range(4))
        d_o, m_o, v_o = (refs[k * n:(k + 1) * n] for k in range(4, 7))
        for i in range(n):
            g = g_r[i][...]
            m_new = ADAM_B1 * m_r[i][...] + (1.0 - ADAM_B1) * g
            v_new = ADAM_B2 * v_r[i][...] + (1.0 - ADAM_B2) * (g * g)
            m_o[i][...] = m_new
            v_o[i][...] = v_new
            d_o[i][...] = -ADAM_LR * ((m_new / c1) / (jnp.sqrt(v_new / c2) + ADAM_EPS) + ADAM_WD * w_r[i][...])

    return pl.pallas_call(
        body, name="adamw_small", out_shape=[jax.ShapeDtypeStruct(w.shape, F32) for w in ws] * 3,
        compiler_params=pltpu.CompilerParams(vmem_limit_bytes=VMEM_LIMIT),
    )(*ws, *ms, *vs, *gs)


def _pack_small(vals):
    flat = jnp.concatenate([v.reshape(-1).astype(F32) for v in vals])
    rows = -(-flat.shape[0] // (256 * 128)) * 256
    return jnp.pad(flat, (0, rows * 128 - flat.shape[0])).reshape(rows, 128)


def _unpack_small(packed, shapes):
    flat = packed.reshape(-1)
    out, off = [], 0
    for sh in shapes:
        n = math.prod(sh)
        out.append(flat[off:off + n].reshape(sh))
        off += n
    return out


def kernel(x, p, norm_w, w_in, s5_a_re, s5_a_im, s5_b_re, s5_b_im, s5_c_re, s5_c_im, s5_d, s5_log_step, s5_w_glu, s5_b_glu, ssd_conv_w, ssd_conv_b, ssd_dt_bias, ssd_a_log, ssd_d, ssd_norm_w, w_br_s5, w_br_ssd, w_out, ple_norm_w, w_ple_gate, w_ple_proj, final_norm_w, loss_target, m_norm_w, m_w_in, m_s5_a_re, m_s5_a_im, m_s5_b_re, m_s5_b_im, m_s5_c_re, m_s5_c_im, m_s5_d, m_s5_log_step, m_s5_w_glu, m_s5_b_glu, m_ssd_conv_w, m_ssd_conv_b, m_ssd_dt_bias, m_ssd_a_log, m_ssd_d, m_ssd_norm_w, m_w_br_s5, m_w_br_ssd, m_w_out, m_ple_norm_w, m_w_ple_gate, m_w_ple_proj, m_final_norm_w, v_norm_w, v_w_in, v_s5_a_re, v_s5_a_im, v_s5_b_re, v_s5_b_im, v_s5_c_re, v_s5_c_im, v_s5_d, v_s5_log_step, v_s5_w_glu, v_s5_b_glu, v_ssd_conv_w, v_ssd_conv_b, v_ssd_dt_bias, v_ssd_a_log, v_ssd_d, v_ssd_norm_w, v_w_br_s5, v_w_br_ssd, v_w_out, v_ple_norm_w, v_w_ple_gate, v_w_ple_proj, v_final_norm_w):
    args = locals()
    wl = {n: args[n] for n in WEIGHTS}
    ml = {n: args["m_" + n] for n in WEIGHTS}
    vl = {n: args["v_" + n] for n in WEIGHTS}
    big = [n for n in SHARDED if n != 'ssd_conv_w']
    chip = _chip_index(lax.axis_index("x"), lax.axis_index("y"))
    idx = jnp.stack([lax.axis_index("c"), chip]).astype(jnp.int32)

    axes = [_half_axis(n) for n in big]
    first = ['w_in']
    rest = [n for n in big if n not in first]
    bf_shards = {n: _shard2d(n, wl[n]).astype(BF16) for n in big}
    w_in_t, conv_w4 = _gather_chips([bf_shards[n] for n in first], [_half_axis(n) for n in first],
                                    [_shard2d('ssd_conv_w', wl['ssd_conv_w'])])
    full = {n: wl[n] for n in SMALL}
    full["w_in_t"] = w_in_t
    full['ssd_conv_w'] = _unslot('ssd_conv_w', conv_w4)
    send_sems, recv_sems, thru, lands, token = _gather_start([bf_shards[n] for n in rest], w_in_t)

    def fetch_rest(after):
        own, zones = _gather_wait(send_sems, recv_sems, thru, lands, after)
        return {n: _unslot(n, lax.dynamic_update_slice(z, o[None], (chip, 0, 0)))
                for n, o, z in zip(rest, own, zones, strict=True)}

    full["_late"] = (token, fetch_rest)

    in_flight = []

    def send_rest(g_now):
        in_flight.extend(_scatter_start([_slot(n, g_now[n]) for n in rest], g_now['s5_w_glu']))
        return in_flight[4]

    full["_early"] = send_rest
    loss, grad_x, g = _local_step(x, p[0], loss_target, full)
    for n in TRANSPOSED:
        g[n] = g.pop(n + "_t")

    for n in ('s5_b_re', 's5_b_im'):
        g[n] = g.pop(n + "_ghp")
    small_shapes = [(1, 1)] + [_view_shape(n) for n in SMALL] + [(SSD_CONV, SSD_CONV_DIM)]
    small_pack = _pack_small([loss] + [g[n] for n in SMALL] + [g['ssd_conv_w']])
    first_axes = [_half_axis(n) for n in first]
    slotted = [g[n] for n in first]
    swapped = _swap_halves(slotted, first_axes, small_pack)
    pair = [_pair_sum("pair_sum_" + n, a, b, idx, ax)
            for n, a, b, ax in zip(first, slotted, swapped[:-1], first_axes, strict=True)]
    small_chip = _sum_slots("sum_small_pair", swapped[-1])
    recv = _scatter_halves([pb for pb, _ in pair], small_chip)
    small_half = _sum_slots("sum_small_chips", recv[-1])
    halves = [_sum_parts("sum_chips_" + n, own, r) for n, (_, own), r in zip(first, pair, recv[:-1], strict=True)]
    own_slots, zones = _scatter_wait(*in_flight[:4], g['w_in'])
    chip_sums = [_sum_chips("sum_chips_" + n, a, z, idx) for n, a, z in zip(rest, own_slots, zones, strict=True)]
    swapped2 = _swap_sibling(halves + chip_sums + [small_half])
    other_halves, sib_sums = swapped2[:len(first)], swapped2[len(first):-1]
    c_is_0 = lax.axis_index("c") == 0
    small_sum = jnp.concatenate([jnp.where(c_is_0, small_half, swapped2[-1]),
                                 jnp.where(c_is_0, swapped2[-1], small_half)], axis=0)

    out_g, out_d, out_m, out_v = {}, {}, {}, {}
    for n, own, oth, ax in zip(first, halves, other_halves, first_axes, strict=True):
        res = _adamw_halves("adamw_" + n, _shard2d(n, wl[n]), _shard2d(n, ml[n]), _shard2d(n, vl[n]), own, oth, idx, ax)
        out_g[n], out_d[n], out_m[n], out_v[n] = (_unshard2d(n, r, wl[n].shape) for r in res)
    for n, own, sib in zip(rest, chip_sums, sib_sums, strict=True):
        res = _adamw("adamw_" + n, _shard2d(n, wl[n]), _shard2d(n, ml[n]), _shard2d(n, vl[n]), [own, sib])
        out_g[n], out_d[n], out_m[n], out_v[n] = (_unshard2d(n, r, wl[n].shape) for r in res)
    sm = _unpack_small(small_sum, small_shapes)
    loss_total = sm[0].reshape(())
    conv_g = lax.dynamic_slice(sm[-1], (0, chip * (SSD_CONV_DIM // N_CHIPS)), (SSD_CONV, SSD_CONV_DIM // N_CHIPS))
    names = SMALL + ['ssd_conv_w']
    grads = sm[1:-1] + [conv_g]
    res = _adamw_small([_view(n, wl[n]) for n in names], [_view(n, ml[n]) for n in names],
                       [_view(n, vl[n]) for n in names], grads)
    for i, n in enumerate(names):
        out_g[n] = _unview(n, grads[i], wl[n].shape)
        out_d[n], out_m[n], out_v[n] = (_unview(n, res[k * len(names) + i], wl[n].shape) for k in range(3))

    return (loss_total, grad_x, *[out_g[n] for n in WEIGHTS], *[out_d[n] for n in WEIGHTS],
            *[out_m[n] for n in WEIGHTS], *[out_v[n] for n in WEIGHTS])
```

```python
import functools
import math

import jax
import jax.numpy as jnp
from jax import lax
from jax.experimental import pallas as pl
from jax.experimental.pallas import tpu as pltpu

F32 = jnp.float32
BF16 = jnp.bfloat16
MESH = pl.DeviceIdType.MESH

D_MODEL = 1024
PLE_DIM = 256
RMS_EPS = 1e-6
S5_WIDTH = 512
S5_GROUP = 16
S5_GROUPS = 32
S5_STATE = 64
S5_N = S5_GROUPS * S5_STATE
S5_LB = 512
S5_NJ = S5_N // S5_LB
S5_TB = 256
S5_LOG_TB = 8
SSD_WIDTH = 1536
SSD_HEADDIM = 64
SSD_HEADS = 24
SSD_GROUPS = 4
SSD_HPG = 6
SSD_STATE = 128
SSD_CONV = 4
SSD_CHUNK = 128
SSD_BC = 512
SSD_CONV_DIM = 2560
GROUP_W = SSD_WIDTH // SSD_GROUPS
N_CHIPS = 4
N_DEV = 8

OFF_XBC, OFF_U5, OFF_Z5, OFF_DT, OFF_G5, OFF_GS, OFF_ZS = 0, 2560, 3072, 3584, 4096, 5120, 6144
DT_W = 512
PROJ_W = 7680
IN_PROJ_DIM = 7192

ADAM_LR, ADAM_B1, ADAM_B2, ADAM_EPS, ADAM_WD, ADAM_STEP = 0.001, 0.9, 0.999, 1e-08, 0.01, 10

VMEM_LIMIT = 56 * 1024 * 1024


def _pick(n, cands):
    for c in cands:
        if n % c == 0:
            return c
    return n


ROW_BLOCK_BYTES = 8 * 1024 * 1024


def _row_tile(r, bytes_per_row):
    for t in (r, 4096, 2048, 1024, 512, 256, 128, 64, 32, 16, 8):
        if t <= r and r % t == 0 and t * bytes_per_row <= ROW_BLOCK_BYTES:
            return t
    return r


def _cparams(sem):
    return pltpu.CompilerParams(dimension_semantics=sem, vmem_limit_bytes=VMEM_LIMIT)


def _dg(a, b, ca, cb):
    return lax.dot_general(a.astype(BF16), b.astype(BF16), (((ca,), (cb,)), ((), ())), preferred_element_type=F32)


@jax.custom_vjp
def dot_nn(a, b):
    return _dg(a, b, 1, 0)


@jax.custom_vjp
def dot_nt(a, b):
    return _dg(a, b, 1, 1)


@jax.custom_vjp
def dot_tn(a, b):
    return _dg(a, b, 0, 0)


dot_nn.defvjp(lambda a, b: (_dg(a, b, 1, 0), (a, b)), lambda r, g: (_dg(g, r[1], 1, 1), _dg(r[0], g, 0, 0)))
dot_nt.defvjp(lambda a, b: (_dg(a, b, 1, 1), (a, b)), lambda r, g: (_dg(g, r[1], 1, 0), _dg(g, r[0], 0, 0)))
dot_tn.defvjp(lambda a, b: (_dg(a, b, 0, 0), (a, b)), lambda r, g: (_dg(r[1], g, 1, 1), _dg(r[0], g, 1, 0)))


MM_VMEM_BUDGET = 30 * 1024 * 1024


def _mm_tiles(m, n, k, sa, sb, so, tn_only=None):
    best, best_key = None, None
    for tm in (1024, 512, 256, 128, 64, 32, 16, 8):
        if m % tm:
            continue
        for tn in (2048, 1536, 1280, 1024, 768, 640, 512, 384, 256, 128):
            if n % tn or (tn_only is not None and tn not in tn_only):
                continue
            for tk in (k, 2048, 1536, 1280, 1024, 768, 512, 256, 128):
                if k % tk or tk > max(k, 128) or (tk == k and k > 2048 and k % 128 == 0):
                    continue
                need = 2 * (tm * tk * sa + tk * tn * sb + tm * tn * so) + (tm * tn * 4 if tk < k else 0)
                if need > MM_VMEM_BUDGET:
                    continue
                key = (tm * tn * tk, tk)
                if best_key is None or key > best_key:
                    best, best_key = (tm, tn, tk), key
    assert best is not None, (m, n, k)
    return best


def _matmul(name, a, b, *, ta=False, tb=False, a_win=None, out_dtype=F32, epilogue=None, epi_rows=(), epi_pars=(),
            epi_outs=(), full_rows=False, epi_accs=(), epi_into=None):
    a_off, a_w = a_win if a_win is not None else (0, a.shape[1])
    if ta:
        kdim, m = a.shape[0], a_w
    else:
        m, kdim = a.shape[0], a_w
    n = b.shape[0] if tb else b.shape[1]
    assert (b.shape[1] if tb else b.shape[0]) == kdim, (name, a.shape, b.shape)
    out_dtypes = list(epi_outs) if epilogue is not None else [out_dtype]
    so = sum(jnp.dtype(d).itemsize for d in out_dtypes) + sum(r.dtype.itemsize for r, _ in epi_rows)
    tn_ok = [n] if full_rows else [t for t in (1024, 512, 256, 128) if all(off % t == 0 for _, off in epi_rows)]
    tm, tn, tk = _mm_tiles(m, n, kdim, a.dtype.itemsize, b.dtype.itemsize, so, tn_ok if epilogue is not None else None)
    nk = kdim // tk
    n_er, n_ep, n_out = len(epi_rows), len(epi_pars), len(out_dtypes)
    if ta:
        assert a_off % tm == 0
        a_spec = pl.BlockSpec((tk, tm), lambda i, j, k: (k, i + a_off // tm))
    else:
        assert a_off % tk == 0
        a_spec = pl.BlockSpec((tm, tk), lambda i, j, k: (i, k + a_off // tk))
    if tb:
        b_spec = pl.BlockSpec((tn, tk), lambda i, j, k: (j, k))
    else:
        b_spec = pl.BlockSpec((tk, tn), lambda i, j, k: (k, j))
    ca, cb = (0 if ta else 1), (1 if tb else 0)

    n_acc = len(epi_accs)
    n_x = 1 if epi_into is not None else 0
    assert not (n_acc or n_x) or full_rows

    def body(a_ref, b_ref, *refs):
        er, ep = refs[:n_er], refs[n_er:n_er + n_ep]
        first_out = n_er + n_ep + n_x
        o_refs = refs[first_out:first_out + n_out]
        s_refs = refs[first_out + n_out:first_out + n_out + n_acc]
        acc = refs[first_out + n_out + n_acc:]

        def finish(c):
            outs = [c] if epilogue is None else epilogue(c, [r[...] for r in er], [p[...] for p in ep])
            if n_acc:
                outs, sums = outs

                @pl.when(pl.program_id(0) == 0)
                def _():
                    for s_ref in s_refs:
                        s_ref[...] = jnp.zeros_like(s_ref)

                for s_ref, s in zip(s_refs, sums, strict=True):
                    s_ref[...] += jnp.broadcast_to(s, s_ref.shape)
            for o_ref, o in zip(o_refs, outs, strict=True):
                o_ref[...] = o.astype(o_ref.dtype)

        if nk == 1:
            finish(_dg(a_ref[...], b_ref[...], ca, cb))
            return
        (acc_ref,) = acc
        k = pl.program_id(2)

        @pl.when(k == 0)
        def _():
            acc_ref[...] = jnp.zeros_like(acc_ref)

        acc_ref[...] += _dg(a_ref[...], b_ref[...], ca, cb)

        @pl.when(k == nk - 1)
        def _():
            finish(acc_ref[...])

    in_specs = [a_spec, b_spec]
    in_specs += [pl.BlockSpec((tm, tn), functools.partial(lambda i, j, k, c: (i, j + c), c=off // tn)) for _, off in epi_rows]
    in_specs += [pl.BlockSpec((p.shape[0], tn), lambda i, j, k: (0, j)) for p in epi_pars]
    out_specs = [pl.BlockSpec((tm, tn), lambda i, j, k: (i, j)) for _ in out_dtypes]
    out_shape = [jax.ShapeDtypeStruct((m, n), d) for d in out_dtypes]
    extra, aliases = [], {}
    if epi_into is not None:
        buf, off, width = epi_into
        assert off % width == 0 and buf.dtype == out_dtypes[0]
        in_specs.append(pl.BlockSpec(memory_space=pl.ANY))
        out_specs[0] = pl.BlockSpec((tm, width), functools.partial(lambda i, j, k, c: (i, c), c=off // width))
        out_shape[0] = jax.ShapeDtypeStruct(buf.shape, buf.dtype)
        extra, aliases = [buf], {2 + n_er + n_ep: 0}
    out_specs += [pl.BlockSpec((r, w), lambda i, j, k: (0, 0)) for r, w in epi_accs]
    out_shape += [jax.ShapeDtypeStruct((r, w), F32) for r, w in epi_accs]
    res = pl.pallas_call(
        body, name=name, grid=(m // tm, n // tn, nk),
        in_specs=in_specs, out_specs=out_specs, out_shape=out_shape, input_output_aliases=aliases,
        scratch_shapes=[pltpu.VMEM((tm, tn), F32)] if nk > 1 else [],
        compiler_params=_cparams(("arbitrary",) * 3 if n_acc else ("parallel", "parallel", "arbitrary")),
    )(a, b, *[r for r, _ in epi_rows], *epi_pars, *extra)
    return res if epilogue is not None else res[0]


BAND = 128


def _band_matmul(name, kind, a, b, *, a_blk0=0, out_dtype=F32, epilogue=None, epi_rows=(), epi_pars=(), epi_outs=(),
                 into=None):
    n_rows = a.shape[0]
    blk = 2 * S5_LB
    tm = _pick(n_rows, (1024, 512, 256))
    if epilogue is not None:
        assert kind == "nt"
        n_er, n_ep = len(epi_rows), len(epi_pars)
        in_specs = [pl.BlockSpec((tm, blk), lambda i, j: (i, j)), pl.BlockSpec((BAND, blk), lambda i, j: (j, 0))]
        in_specs += [pl.BlockSpec((tm, BAND), functools.partial(lambda i, j, c: (i, c + j), c=c0)) for _, c0 in epi_rows]
        in_specs += [pl.BlockSpec((1, BAND), lambda i, j: (0, j)) for _ in epi_pars]
        out_specs = [pl.BlockSpec((tm, BAND), lambda i, j: (i, j)) for _ in epi_outs]
        out_shape = [jax.ShapeDtypeStruct((n_rows, S5_NJ * BAND), d) for d in epi_outs]
        extra, aliases = [], {}
        if into is not None:
            buf, c0 = into
            in_specs.append(pl.BlockSpec(memory_space=pl.ANY))
            out_specs[0] = pl.BlockSpec((tm, BAND), functools.partial(lambda i, j, c: (i, c + j), c=c0))
            out_shape[0] = jax.ShapeDtypeStruct(buf.shape, buf.dtype)
            extra, aliases = [buf], {2 + n_er + n_ep: 0}

        def epi_body(a_ref, b_ref, *refs):
            outs = epilogue(_dg(a_ref[...], b_ref[...], 1, 1), [r[...] for r in refs[:n_er]],
                            [p[...] for p in refs[n_er:n_er + n_ep]])
            for o_ref, o in zip(refs[n_er + n_ep + len(extra):], outs, strict=True):
                o_ref[...] = o.astype(o_ref.dtype)

        return pl.pallas_call(
            epi_body, name=name, grid=(n_rows // tm, S5_NJ), in_specs=in_specs, out_specs=out_specs,
            out_shape=out_shape, input_output_aliases=aliases, compiler_params=_cparams(("parallel", "parallel")),
        )(a, b, *[r for r, _ in epi_rows], *epi_pars, *extra)
    if kind == "nn":
        grid = (n_rows // tm, S5_NJ)
        in_specs = [pl.BlockSpec((tm, BAND), lambda i, j: (i, a_blk0 + j)), pl.BlockSpec((BAND, blk), lambda i, j: (j, 0))]
        out_spec = pl.BlockSpec((tm, blk), lambda i, j: (i, j))
        out_shape = (n_rows, S5_NJ * blk)
        sem = ("parallel", "parallel")

        def body(a_ref, b_ref, o_ref):
            o_ref[...] = _dg(a_ref[...], b_ref[...], 1, 0).astype(o_ref.dtype)
    elif kind == "nt":
        grid = (n_rows // tm, S5_NJ)
        in_specs = [pl.BlockSpec((tm, blk), lambda i, j: (i, j)), pl.BlockSpec((BAND, blk), lambda i, j: (j, 0))]
        out_spec = pl.BlockSpec((tm, BAND), lambda i, j: (i, j))
        out_shape = (n_rows, S5_NJ * BAND)
        sem = ("parallel", "parallel")

        def body(a_ref, b_ref, o_ref):
            o_ref[...] = _dg(a_ref[...], b_ref[...], 1, 1).astype(o_ref.dtype)
    else:
        grid = (S5_NJ, n_rows // tm)
        in_specs = [pl.BlockSpec((tm, BAND), lambda j, k: (k, a_blk0 + j)), pl.BlockSpec((tm, blk), lambda j, k: (k, j))]
        out_spec = pl.BlockSpec((BAND, blk), lambda j, k: (j, 0))
        out_shape = (S5_NJ * BAND, blk)
        sem = ("parallel", "arbitrary")

        def body(a_ref, b_ref, o_ref):
            @pl.when(pl.program_id(1) == 0)
            def _():
                o_ref[...] = jnp.zeros_like(o_ref)

            o_ref[...] += _dg(a_ref[...], b_ref[...], 0, 0)

    return pl.pallas_call(
        body, name=name, grid=grid, in_specs=in_specs, out_specs=out_spec,
        out_shape=jax.ShapeDtypeStruct(out_shape, out_dtype), compiler_params=_cparams(sem),
    )(a, b)


def _rowwise(name, fn, n_rows, tr, row_ins, par_ins, row_outs, acc_outs=(), into=None):
    nr, npar, no, na = len(row_ins), len(par_ins), len(row_outs), len(acc_outs)
    in_specs = []
    for arr, off, w in row_ins:
        assert off % w == 0 and arr.shape[0] == n_rows, (name, arr.shape, off, w)
        in_specs.append(pl.BlockSpec((tr, w), functools.partial(lambda i, c: (i, c), c=off // w)))
    for arr, off, w in par_ins:
        assert off % w == 0
        in_specs.append(pl.BlockSpec((arr.shape[0], w), functools.partial(lambda i, c: (0, c), c=off // w)))
    out_specs = [pl.BlockSpec((tr, w), lambda i: (i, 0)) for w, _ in row_outs]
    out_specs += [pl.BlockSpec((r, w), lambda i: (0, 0)) for r, w in acc_outs]
    out_shape = [jax.ShapeDtypeStruct((n_rows, w), dt) for w, dt in row_outs]
    out_shape += [jax.ShapeDtypeStruct((r, w), F32) for r, w in acc_outs]
    extra, aliases = [], {}
    if into is not None:
        buf, off = into
        w0 = row_outs[0][0]
        assert off % w0 == 0 and buf.dtype == row_outs[0][1]
        in_specs.append(pl.BlockSpec(memory_space=pl.ANY))
        out_specs[0] = pl.BlockSpec((tr, w0), functools.partial(lambda i, c: (i, c), c=off // w0))
        out_shape[0] = jax.ShapeDtypeStruct(buf.shape, buf.dtype)
        extra, aliases = [buf], {nr + npar: 0}
    nx = len(extra)

    def body(*refs):
        rows = [r[...] for r in refs[:nr]]
        pars = [r[...] for r in refs[nr:nr + npar]]
        o_refs = refs[nr + npar + nx:nr + npar + nx + no]
        a_refs = refs[nr + npar + nx + no:]
        outs, accs = fn(rows, pars)
        for o_ref, o in zip(o_refs, outs, strict=True):
            o_ref[...] = o.astype(o_ref.dtype)
        if na:
            @pl.when(pl.program_id(0) == 0)
            def _():
                for a_ref in a_refs:
                    a_ref[...] = jnp.zeros_like(a_ref)

            for a_ref, a in zip(a_refs, accs, strict=True):
                a_ref[...] += jnp.broadcast_to(a, a_ref.shape)

    res = pl.pallas_call(
        body, name=name, grid=(n_rows // tr,),
        in_specs=in_specs, out_specs=out_specs, out_shape=out_shape, input_output_aliases=aliases,
        compiler_params=_cparams(("arbitrary",) if na else ("parallel",)),
    )(*[a for a, _, _ in row_ins], *[a for a, _, _ in par_ins], *extra)
    return res


def _rms(x, w):
    return x * lax.rsqrt(jnp.mean(x * x, axis=-1, keepdims=True) + RMS_EPS) * w


def _gated_norm(y, z, w):
    outs = []
    for g in range(SSD_GROUPS):
        sl = slice(g * GROUP_W, (g + 1) * GROUP_W)
        yg = y[:, sl] * jax.nn.silu(z[:, sl])
        outs.append(yg * lax.rsqrt(jnp.mean(yg * yg, axis=-1, keepdims=True) + RMS_EPS) * w[:, sl])
    return jnp.concatenate(outs, axis=-1)


def _s5_out(yc, u, z, t, d, bg):
    ge = jax.nn.gelu(yc + d * u)
    return ge * jax.nn.sigmoid(t + bg) * jax.nn.silu(z)


def _merge(g5, gs, m5, ms):
    return jax.nn.sigmoid(g5) * m5 + jax.nn.sigmoid(gs) * ms


def _head_loss(h1, pgl, pp, fw, tgt):
    h2 = h1 + jax.nn.sigmoid(pgl) * pp
    err = _rms(h2, fw) - tgt
    per_row = 0.5 * jnp.mean(err * err, axis=-1, keepdims=True)
    return jnp.sum(per_row, axis=0, keepdims=True)


def _s5_disc(a_re, a_im, log_step, b_re2, b_im2, expand):
    step = jnp.exp(log_step)
    mag = jnp.exp(a_re * step)
    lb_re = mag * jnp.cos(a_im * step)
    lb_im = mag * jnp.sin(a_im * step)
    den = a_re * a_re + a_im * a_im
    n_re = lb_re - 1.0
    f_re = (n_re * a_re + lb_im * a_im) / den
    f_im = (lb_im * a_re - n_re * a_im) / den
    hi = lax.Precision.HIGHEST
    fr = jnp.dot(expand, f_re, precision=hi, preferred_element_type=F32)
    fi = jnp.dot(expand, f_im, precision=hi, preferred_element_type=F32)
    return lb_re, lb_im, fr * b_re2 - fi * b_im2, fr * b_im2 + fi * b_re2


def _s5_params_fwd(a_re, a_im, log_step, b_re2, b_im2, expand):
    gp = a_re.shape

    def body(ar, ai, ls, br, bi, ex, pr_ref, pi_ref, bbr_ref, bbi_ref):
        lr, li, bbr, bbi = _s5_disc(ar[...], ai[...], ls[...], br[...], bi[...], ex[...])
        bbr_ref[...] = bbr
        bbi_ref[...] = bbi
        qr, qi = lr, li
        for k in range(S5_LOG_TB):
            pr_ref[k] = qr
            pi_ref[k] = qi
            qr, qi = qr * lr - qi * li, qr * li + qi * lr

    return pl.pallas_call(
        body, name="s5_params_fwd",
        out_shape=(jax.ShapeDtypeStruct((S5_LOG_TB,) + gp, F32), jax.ShapeDtypeStruct((S5_LOG_TB,) + gp, F32),
                   jax.ShapeDtypeStruct(b_re2.shape, F32), jax.ShapeDtypeStruct(b_re2.shape, F32)),
    )(a_re, a_im, log_step, b_re2, b_im2, expand)


def _s5_params_bwd(a_re, a_im, log_step, b_re2, b_im2, expand, d_lr, d_li, d_bbr, d_bbi):
    def body(ar, ai, ls, br, bi, ex, glr, gli, gbr, gbi, dar, dai, dls, dbr, dbi):
        _, vjp = jax.vjp(lambda *p: _s5_disc(*p, ex[...]), ar[...], ai[...], ls[...], br[...], bi[...])
        g = vjp((glr[...], gli[...], gbr[...], gbi[...]))
        for ref, val in zip((dar, dai, dls, dbr, dbi), g, strict=True):
            ref[...] = val

    return pl.pallas_call(
        body, name="s5_params_bwd",
        out_shape=tuple(jax.ShapeDtypeStruct(v.shape, F32) for v in (a_re, a_im, log_step, b_re2, b_im2)),
    )(a_re, a_im, log_step, b_re2, b_im2, expand, d_lr, d_li, d_bbr, d_bbi)


def _scan_block(x_ref, out_ref, lp, edge_pow, cr, ci, reverse, each=None):
    n_g = x_ref.shape[0] // 8
    sub = lax.broadcasted_iota(jnp.int32, (8, S5_LB), 0)
    steps = []
    for sh in (1, 2, 4):
        keep = (sub < 8 - sh) if reverse else (sub >= sh)
        steps.append((8 - sh if reverse else sh, jnp.where(keep, lp[sh - 1:sh, :S5_LB], 0.0),
                      jnp.where(keep, lp[sh - 1:sh, S5_LB:], 0.0)))
    e_r, e_i = edge_pow[:, :S5_LB], edge_pow[:, S5_LB:]
    for r in (range(n_g - 1, -1, -1) if reverse else range(n_g)):
        rows = slice(8 * r, 8 * r + 8)
        xr, xi = x_ref[rows, :S5_LB], x_ref[rows, S5_LB:]
        for by, a_r, a_i in steps:
            pr, pi = pltpu.roll(xr, by, 0), pltpu.roll(xi, by, 0)
            xr, xi = xr + a_r * pr - a_i * pi, xi + a_r * pi + a_i * pr
        xr = xr + e_r * cr - e_i * ci
        xi = xi + e_r * ci + e_i * cr
        out_ref[rows, :S5_LB] = xr
        out_ref[rows, S5_LB:] = xi
        if each is not None:
            each(r, xr, xi)
        cr, ci = (xr[0:1, :], xi[0:1, :]) if reverse else (xr[7:8, :], xi[7:8, :])
    return cr, ci


def _s5_fwd(proj, bb_band, c_band, lam_pow, s5_d, n_seq, seq_len):
    n_t = seq_len // S5_TB
    blk = 2 * S5_LB
    n_rows = n_seq * seq_len
    u_blk0 = OFF_U5 // BAND

    def body(u_ref, bb_ref, cb_ref, lp_ref, d_ref, s_ref, y_ref, ge_ref, cr, ci, buf):
        @pl.when(pl.program_id(2) == 0)
        def _():
            cr[...] = jnp.zeros_like(cr)
            ci[...] = jnp.zeros_like(ci)

        u = u_ref[...]
        buf[...] = _dg(u, bb_ref[...], 1, 0)
        lp = lp_ref[...]
        cr[...], ci[...] = _scan_block(buf, buf, lp, lp, cr[...], ci[...], False)
        s = buf[...].astype(s_ref.dtype)
        s_ref[...] = s
        y = _dg(s, cb_ref[...], 1, 1)
        y_ref[...] = y
        ge_ref[...] = jax.nn.gelu(y + d_ref[...] * u).astype(ge_ref.dtype)

    def rows(j, b, t):
        return (b * n_t + t, j)

    return pl.pallas_call(
        body, name="s5_fwd", grid=(S5_NJ, n_seq, n_t),
        in_specs=[pl.BlockSpec((S5_TB, BAND), lambda j, b, t: (b * n_t + t, u_blk0 + j)),
                  pl.BlockSpec((BAND, blk), lambda j, b, t: (j, 0)), pl.BlockSpec((BAND, blk), lambda j, b, t: (j, 0)),
                  pl.BlockSpec((S5_LOG_TB, blk), lambda j, b, t: (0, j)), pl.BlockSpec((1, BAND), lambda j, b, t: (0, j))],
        out_specs=[pl.BlockSpec((S5_TB, blk), rows), pl.BlockSpec((S5_TB, BAND), rows), pl.BlockSpec((S5_TB, BAND), rows)],
        out_shape=[jax.ShapeDtypeStruct((n_rows, S5_NJ * blk), BF16), jax.ShapeDtypeStruct((n_rows, S5_WIDTH), F32),
                   jax.ShapeDtypeStruct((n_rows, S5_WIDTH), BF16)],
        scratch_shapes=[pltpu.VMEM((1, S5_LB), F32), pltpu.VMEM((1, S5_LB), F32), pltpu.VMEM((S5_TB, blk), F32)],
        compiler_params=_cparams(("parallel", "parallel", "arbitrary")),
    )(proj, bb_band, c_band, lam_pow, s5_d)


def _s5_bwd(dyc, s, proj, du5_a, bb_band, c_band, lam_pow_conj, dproj, n_seq, seq_len):
    n_t = seq_len // S5_TB
    blk = 2 * S5_LB
    halo_rows = 16
    halo_per_blk = S5_TB // halo_rows
    u_blk0 = OFF_U5 // BAND

    def rows(j, b, t):
        return (b * n_t + (n_t - 1 - t), j)

    def u_rows(j, b, t):
        return (b * n_t + (n_t - 1 - t), u_blk0 + j)

    def halo(j, b, t):
        return (jnp.maximum((b * n_t + (n_t - 1 - t)) * halo_per_blk - 1, 0), j)

    def body(dy_ref, sq_ref, hq_ref, u_ref, dua_ref, bb_ref, cb_ref, lp_ref, _, du_ref, dl_ref, dbb_ref, dcb_ref,
             cr, ci, buf, s_ref):
        b, t = pl.program_id(1), pl.program_id(2)
        sq = sq_ref[...]
        s_ref[...] = sq.astype(F32)
        h_last = hq_ref[...].astype(F32)[halo_rows - 1:halo_rows, :]
        dy = dy_ref[...]
        buf[...] = _dg(dy, cb_ref[...], 1, 0)

        @pl.when(t == 0)
        def _():
            cr[...] = jnp.zeros_like(cr)
            ci[...] = jnp.zeros_like(ci)

        @pl.when((b == 0) & (t == 0))
        def _():
            dl_ref[...] = jnp.zeros_like(dl_ref)
            dbb_ref[...] = jnp.zeros_like(dbb_ref)
            dcb_ref[...] = jnp.zeros_like(dcb_ref)

        dcb_ref[...] += _dg(dy, sq, 0, 0)

        first_blk = t == n_t - 1
        sub = lax.broadcasted_iota(jnp.int32, (8, S5_LB), 0)
        acc = [jnp.zeros((8, S5_LB), F32), jnp.zeros((8, S5_LB), F32)]

        def each(r, gr, gi):
            rows = slice(8 * r, 8 * r + 8)
            if r == 0:
                before_r = jnp.where(first_blk, 0.0, h_last[:, :S5_LB])
                before_i = jnp.where(first_blk, 0.0, h_last[:, S5_LB:])
            else:
                before_r, before_i = s_ref[8 * r - 1:8 * r, :S5_LB], s_ref[8 * r - 1:8 * r, S5_LB:]
            sp_r = jnp.where(sub == 0, before_r, pltpu.roll(s_ref[rows, :S5_LB], 1, 0))
            sp_i = jnp.where(sub == 0, before_i, pltpu.roll(s_ref[rows, S5_LB:], 1, 0))
            acc[0] = acc[0] + gr * sp_r + gi * sp_i
            acc[1] = acc[1] + gi * sp_r - gr * sp_i

        lp = lp_ref[...]
        edge_pow = jnp.concatenate([lp[7 - i:8 - i, :] for i in range(8)], axis=0)
        cr[...], ci[...] = _scan_block(buf, buf, lp, edge_pow, cr[...], ci[...], True, each)
        g = buf[...].astype(BF16)
        du_ref[...] = (dua_ref[...] + _dg(g, bb_ref[...], 1, 1)).astype(du_ref.dtype)
        dbb_ref[...] += _dg(u_ref[...], g, 0, 0)
        dl_ref[:, :S5_LB] += jnp.sum(acc[0], axis=0, keepdims=True)
        dl_ref[:, S5_LB:] += jnp.sum(acc[1], axis=0, keepdims=True)

    band = pl.BlockSpec((BAND, blk), lambda j, b, t: (j, 0))
    return pl.pallas_call(
        body, name="s5_bwd", grid=(S5_NJ, n_seq, n_t),
        in_specs=[pl.BlockSpec((S5_TB, BAND), rows), pl.BlockSpec((S5_TB, blk), rows),
                  pl.BlockSpec((halo_rows, blk), halo), pl.BlockSpec((S5_TB, BAND), u_rows),
                  pl.BlockSpec((S5_TB, BAND), rows), band, band,
                  pl.BlockSpec((S5_LOG_TB, blk), lambda j, b, t: (0, j)), pl.BlockSpec(memory_space=pl.ANY)],
        out_specs=[pl.BlockSpec((S5_TB, BAND), u_rows), pl.BlockSpec((1, blk), lambda j, b, t: (0, j)), band, band],
        out_shape=[jax.ShapeDtypeStruct(dproj.shape, dproj.dtype), jax.ShapeDtypeStruct((1, S5_NJ * blk), F32),
                   jax.ShapeDtypeStruct(bb_band.shape, F32), jax.ShapeDtypeStruct(c_band.shape, F32)],
        input_output_aliases={8: 0},
        scratch_shapes=[pltpu.VMEM((1, S5_LB), F32), pltpu.VMEM((1, S5_LB), F32), pltpu.VMEM((S5_TB, blk), F32),
                        pltpu.VMEM((S5_TB, blk), F32)],
        compiler_params=_cparams(("parallel", "arbitrary", "arbitrary")),
    )(dyc, s, s, proj, du5_a, bb_band, c_band, lam_pow_conj, dproj)


CONV_TR = 512
CONV_CW = 512


def _shift_down(x, halo, k):
    if k == 0:
        return x
    row8 = lax.broadcasted_iota(jnp.int32, halo.shape, 0)
    rolled = pltpu.roll(x, k, 0)
    top = jnp.where(row8 < k, pltpu.roll(halo, k, 0), rolled[:8])
    if x.shape[0] == 8:
        return top
    return jnp.concatenate([top, rolled[8:]], axis=0)


def _shift_up(x, halo, k):
    if k == 0:
        return x
    n = x.shape[0]
    row8 = lax.broadcasted_iota(jnp.int32, halo.shape, 0)
    rolled = pltpu.roll(x, n - k, 0)
    bot = jnp.where(row8 >= 8 - k, pltpu.roll(halo, 8 - k, 0), rolled[n - 8:])
    if n == 8:
        return bot
    return jnp.concatenate([rolled[:n - 8], bot], axis=0)


def _conv_pre(x, halo, w, b):
    acc = b + w[SSD_CONV - 1:SSD_CONV, :] * x
    for k in range(SSD_CONV - 1):
        acc = acc + w[k:k + 1, :] * _shift_down(x, halo, SSD_CONV - 1 - k)
    return acc


def _conv_specs(seq_len, col_off):
    lt = seq_len // CONV_TR
    cb = col_off // CONV_CW
    cur = pl.BlockSpec((CONV_TR, CONV_CW), lambda j, i: (i, j + cb))
    prev = pl.BlockSpec((8, CONV_CW), lambda j, i: (jnp.maximum(i * (CONV_TR // 8) - 1, 0), j + cb))
    return lt, cur, prev


def _conv_fwd(proj, conv_w, conv_b, n_rows, seq_len):
    lt, cur, prev = _conv_specs(seq_len, OFF_XBC)

    def body(x_ref, h_ref, w_ref, b_ref, o_ref):
        halo = jnp.where(pl.program_id(1) % lt == 0, 0.0, h_ref[...])
        o_ref[...] = jax.nn.silu(_conv_pre(x_ref[...], halo, w_ref[...], b_ref[...]))

    return pl.pallas_call(
        body, name="ssd_conv_fwd", grid=(SSD_CONV_DIM // CONV_CW, n_rows // CONV_TR),
        in_specs=[cur, prev, pl.BlockSpec((SSD_CONV, CONV_CW), lambda j, i: (0, j)),
                  pl.BlockSpec((1, CONV_CW), lambda j, i: (0, j))],
        out_specs=pl.BlockSpec((CONV_TR, CONV_CW), lambda j, i: (i, j)),
        out_shape=jax.ShapeDtypeStruct((n_rows, SSD_CONV_DIM), F32),
        compiler_params=_cparams(("parallel", "parallel")),
    )(proj, proj, conv_w, conv_b)


def _conv_bwd(name, proj, d_act, conv_w, conv_b, n_rows, seq_len, col_off, dproj):
    width = d_act.shape[1]
    lt, cur, prev = _conv_specs(seq_len, OFF_XBC + col_off)
    n_blk = n_rows // CONV_TR
    cb = (OFF_XBC + col_off) // CONV_CW
    pb = col_off // CONV_CW
    nxt = pl.BlockSpec((8, CONV_CW), lambda j, i: (jnp.minimum((i + 1) * (CONV_TR // 8), n_rows // 8 - 1), j + cb))
    d_cur = pl.BlockSpec((CONV_TR, CONV_CW), lambda j, i: (i, j))
    d_nxt = pl.BlockSpec((8, CONV_CW), lambda j, i: (jnp.minimum((i + 1) * (CONV_TR // 8), n_rows // 8 - 1), j))

    def dsilu(pre):
        sg = jax.nn.sigmoid(pre)
        return sg * (1.0 + pre * (1.0 - sg))

    def body(x_ref, hp_ref, hn_ref, d_ref, dn_ref, w_ref, b_ref, _, dx_ref, dw_ref, db_ref):
        i = pl.program_id(1)
        x, w, b = x_ref[...], w_ref[...], b_ref[...]
        halo_p = jnp.where(i % lt == 0, 0.0, hp_ref[...])
        at_end = i % lt == lt - 1
        dpre = d_ref[...] * dsilu(_conv_pre(x, halo_p, w, b))
        pre_n = _conv_pre(hn_ref[...], x[CONV_TR - 8:, :], w, b)
        dpre_n = jnp.where(at_end, 0.0, dn_ref[...] * dsilu(pre_n))
        dx = w[SSD_CONV - 1:SSD_CONV, :] * dpre
        for k in range(SSD_CONV - 1):
            dx = dx + w[k:k + 1, :] * _shift_up(dpre, dpre_n, SSD_CONV - 1 - k)
        dx_ref[...] = dx.astype(dx_ref.dtype)

        @pl.when(i == 0)
        def _():
            dw_ref[...] = jnp.zeros_like(dw_ref)
            db_ref[...] = jnp.zeros_like(db_ref)

        for k in range(SSD_CONV):
            xs = _shift_down(x, halo_p, SSD_CONV - 1 - k)
            dw_ref[k:k + 1, :] += jnp.sum(dpre * xs, axis=0, keepdims=True)
        db_ref[...] += jnp.sum(dpre, axis=0, keepdims=True)

    return pl.pallas_call(
        body, name=name, grid=(width // CONV_CW, n_blk),
        in_specs=[cur, prev, nxt, d_cur, d_nxt,
                  pl.BlockSpec((SSD_CONV, CONV_CW), lambda j, i: (0, j + pb)),
                  pl.BlockSpec((1, CONV_CW), lambda j, i: (0, j + pb)), pl.BlockSpec(memory_space=pl.ANY)],
        out_specs=[pl.BlockSpec((CONV_TR, CONV_CW), lambda j, i: (i, j + cb)),
                   pl.BlockSpec((SSD_CONV, CONV_CW), lambda j, i: (0, j)),
                   pl.BlockSpec((1, CONV_CW), lambda j, i: (0, j))],
        out_shape=[jax.ShapeDtypeStruct(dproj.shape, dproj.dtype),
                   jax.ShapeDtypeStruct((SSD_CONV, width), F32), jax.ShapeDtypeStruct((1, width), F32)],
        input_output_aliases={7: 0},
        compiler_params=_cparams(("parallel", "arbitrary")),
    )(proj, proj, proj, d_act, d_act, conv_w, conv_b, dproj)


def _split3(x):
    hi = x.astype(BF16)
    r = x - hi.astype(F32)
    mid = r.astype(BF16)
    return hi, mid, (r - mid.astype(F32)).astype(BF16)


def _sel_dot(a, b, a_is_sel):
    dn = (((1,), (0,)), ((), ()))
    if a_is_sel:
        return sum(lax.dot_general(a, t, dn, preferred_element_type=F32) for t in _split3(b))
    return sum(lax.dot_general(t, b, dn, preferred_element_type=F32) for t in _split3(a))


@jax.custom_vjp
def sel_left(sel, sel_t, x):
    return _sel_dot(sel, x, True)


@jax.custom_vjp
def sel_right(x, sel, sel_t):
    return _sel_dot(x, sel, False)


sel_left.defvjp(lambda s, st, x: (_sel_dot(s, x, True), (s, st)),
                lambda r, g: (jnp.zeros_like(r[0]), jnp.zeros_like(r[1]), _sel_dot(r[1], g, True)))
sel_right.defvjp(lambda x, s, st: (_sel_dot(x, s, False), (s, st)),
                 lambda r, g: (_sel_dot(g, r[1], False), jnp.zeros_like(r[0]), jnp.zeros_like(r[1])))


def _ssd_chunk(xs, bm, cm, dtr, st, dtb, alog, dsk, k):
    dt = jax.nn.softplus(dtr + dtb)
    acum = sel_left(k["tri"], k["tri_t"], dt * (-jnp.exp(alog)))
    dt_e = sel_right(dt, k["spread"], k["spread_t"])
    ac_e = sel_right(acum, k["spread"], k["spread_t"])
    al_e = ac_e[SSD_CHUNK - 1:SSD_CHUNK, :]
    dsk_e = sel_right(jnp.broadcast_to(dsk, (8, 128)), k["spread"], k["spread_t"])[0:1, :]
    xdt = xs * dt_e
    acum_t = acum.T
    scores = dot_nt(cm, bm)
    y = dot_nn(cm, st) * jnp.exp(ac_e) + xs * dsk_e
    for j in range(SSD_HPG):
        lmat = jnp.exp(jnp.where(k["causal"], acum[:, j:j + 1] - acum_t[j:j + 1, :], -jnp.inf))
        y = y + dot_nn(scores * lmat, jnp.where(k["head"] == j, xdt, 0.0))
    new = st * jnp.exp(al_e) + dot_tn(bm, xdt * jnp.exp(al_e - ac_e))
    return y, new


def _ssd_consts():
    r = lax.broadcasted_iota(jnp.int32, (SSD_CHUNK, SSD_CHUNK), 0)
    c = lax.broadcasted_iota(jnp.int32, (SSD_CHUNK, SSD_CHUNK), 1)
    hd = jnp.int32(SSD_HEADDIM)
    sr = lax.broadcasted_iota(jnp.int32, (128, GROUP_W), 0)
    sc = lax.div(lax.broadcasted_iota(jnp.int32, (128, GROUP_W), 1), hd)
    tr = lax.div(lax.broadcasted_iota(jnp.int32, (GROUP_W, 128), 0), hd)
    tc = lax.broadcasted_iota(jnp.int32, (GROUP_W, 128), 1)
    return {"tri": (r >= c).astype(BF16), "tri_t": (c >= r).astype(BF16), "causal": r >= c,
            "spread": (sr == sc).astype(BF16), "spread_t": (tr == tc).astype(BF16),
            "head": lax.div(lax.broadcasted_iota(jnp.int32, (SSD_CHUNK, GROUP_W), 1), hd)}


def _ssd_specs(n_c, reverse):
    def cidx(c):
        return n_c - 1 - c if reverse else c

    xs = pl.BlockSpec((SSD_CHUNK, GROUP_W), lambda g, b, c: (b * n_c + cidx(c), g))
    bm = pl.BlockSpec((SSD_CHUNK, SSD_STATE), lambda g, b, c: (b * n_c + cidx(c), SSD_WIDTH // SSD_STATE + g))
    cm = pl.BlockSpec((SSD_CHUNK, SSD_STATE), lambda g, b, c: (b * n_c + cidx(c), (SSD_WIDTH + SSD_BC) // SSD_STATE + g))
    dt = pl.BlockSpec((SSD_CHUNK, 128), lambda g, b, c: (b * n_c + cidx(c), OFF_DT // 128 + g))
    par = pl.BlockSpec((1, 128), lambda g, b, c: (0, g))
    st = pl.BlockSpec((1, 1, 1, SSD_STATE, GROUP_W), lambda g, b, c: (b, cidx(c), g, 0, 0))
    return xs, bm, cm, dt, par, st


def _ssd_chunk_gated(xs, bm, cm, dtr, st, dtb, alog, dsk, z, nw, k):
    y, new = _ssd_chunk(xs, bm, cm, dtr, st, dtb, alog, dsk, k)
    yg = y * jax.nn.silu(z)
    return yg * lax.rsqrt(jnp.mean(yg * yg, axis=-1, keepdims=True) + RMS_EPS) * nw, new


def _ssd_gate_specs(n_c, reverse):
    def cidx(c):
        return n_c - 1 - c if reverse else c

    z = pl.BlockSpec((SSD_CHUNK, GROUP_W), lambda g, b, c: (b * n_c + cidx(c), OFF_ZS // GROUP_W + g))
    nw = pl.BlockSpec((1, GROUP_W), lambda g, b, c: (0, g))
    return z, nw


def _ssd_fwd(xbc_act, proj, dtb, alog, dsk, norm_w, n_seq, seq_len):
    n_c = seq_len // SSD_CHUNK
    xs_s, bm_s, cm_s, dt_s, par_s, st_s = _ssd_specs(n_c, False)
    z_s, nw_s = _ssd_gate_specs(n_c, False)

    def body(xs_ref, bm_ref, cm_ref, dt_ref, dtb_ref, al_ref, dk_ref, z_ref, nw_ref, y_ref, st_ref, state):
        @pl.when(pl.program_id(2) == 0)
        def _():
            state[...] = jnp.zeros_like(state)

        prev = state[...]
        st_ref[0, 0, 0] = prev
        y, new = _ssd_chunk_gated(xs_ref[...], bm_ref[...], cm_ref[...], dt_ref[...], prev,
                                  dtb_ref[...], al_ref[...], dk_ref[...], z_ref[...], nw_ref[...], _ssd_consts())
        y_ref[...] = y.astype(y_ref.dtype)
        state[...] = new

    return pl.pallas_call(
        body, name="ssd_fwd", grid=(SSD_GROUPS, n_seq, n_c),
        in_specs=[xs_s, bm_s, cm_s, dt_s, par_s, par_s, par_s, z_s, nw_s],
        out_specs=[pl.BlockSpec((SSD_CHUNK, GROUP_W), lambda g, b, c: (b * n_c + c, g)), st_s],
        out_shape=[jax.ShapeDtypeStruct((n_seq * seq_len, SSD_WIDTH), BF16),
                   jax.ShapeDtypeStruct((n_seq, n_c, SSD_GROUPS, SSD_STATE, GROUP_W), F32)],
        scratch_shapes=[pltpu.VMEM((SSD_STATE, GROUP_W), F32)],
        compiler_params=_cparams(("parallel", "parallel", "arbitrary")),
    )(xbc_act, xbc_act, xbc_act, proj, dtb, alog, dsk, proj, norm_w)


def _ssd_bwd(xbc_act, proj, states, dy, dtb, alog, dsk, norm_w, n_seq, seq_len, dproj):
    n_c = seq_len // SSD_CHUNK
    n_rows = n_seq * seq_len
    xs_s, bm_s, cm_s, dt_s, par_s, st_s = _ssd_specs(n_c, True)
    z_s, nw_s = _ssd_gate_specs(n_c, True)

    def rows(w, first=0):
        return pl.BlockSpec((SSD_CHUNK, w), lambda g, b, c: (b * n_c + (n_c - 1 - c), first + g))

    def body(xs_ref, bm_ref, cm_ref, dt_ref, st_ref, dy_ref, dtb_ref, al_ref, dk_ref, z_ref, nw_ref, _,
             dxs_ref, dbm_ref, dcm_ref, dz_ref, ddt_ref, ddtb_ref, dal_ref, ddk_ref, dnw_ref, dstate):
        b, c = pl.program_id(1), pl.program_id(2)

        @pl.when(c == 0)
        def _():
            dstate[...] = jnp.zeros_like(dstate)

        @pl.when((b == 0) & (c == 0))
        def _():
            for ref in (ddtb_ref, dal_ref, ddk_ref, dnw_ref):
                ref[...] = jnp.zeros_like(ref)

        consts = _ssd_consts()
        _, vjp = jax.vjp(
            lambda *a: _ssd_chunk_gated(*a, consts),
            xs_ref[...], bm_ref[...], cm_ref[...], dt_ref[...], st_ref[0, 0, 0], dtb_ref[...], al_ref[...], dk_ref[...],
            z_ref[...], nw_ref[...])
        dxs, dbm, dcm, ddtr, dprev, ddtb, dal, ddk, dz, dnw = vjp((dy_ref[...], dstate[...]))
        dxs_ref[...] = dxs
        dbm_ref[...] = dbm
        dcm_ref[...] = dcm
        dz_ref[...] = dz.astype(dz_ref.dtype)
        ddt_ref[...] = ddtr.astype(ddt_ref.dtype)
        ddtb_ref[...] += ddtb
        dal_ref[...] += dal
        ddk_ref[...] += ddk
        dnw_ref[...] += dnw
        dstate[...] = dprev

    acc = pl.BlockSpec((1, 128), lambda g, b, c: (0, g))
    return pl.pallas_call(
        body, name="ssd_bwd", grid=(SSD_GROUPS, n_seq, n_c),
        in_specs=[xs_s, bm_s, cm_s, dt_s, st_s, rows(GROUP_W), par_s, par_s, par_s, z_s, nw_s,
                  pl.BlockSpec(memory_space=pl.ANY)],
        out_specs=[rows(GROUP_W), rows(SSD_STATE), rows(SSD_STATE), rows(GROUP_W, OFF_ZS // GROUP_W), rows(128),
                   acc, acc, acc, pl.BlockSpec((1, GROUP_W), lambda g, b, c: (0, g))],
        out_shape=[jax.ShapeDtypeStruct((n_rows, SSD_WIDTH), F32), jax.ShapeDtypeStruct((n_rows, SSD_BC), F32),
                   jax.ShapeDtypeStruct((n_rows, SSD_BC), F32), jax.ShapeDtypeStruct(dproj.shape, dproj.dtype),
                   jax.ShapeDtypeStruct((n_rows, DT_W), BF16),
                   jax.ShapeDtypeStruct((1, 512), F32), jax.ShapeDtypeStruct((1, 512), F32),
                   jax.ShapeDtypeStruct((1, 512), F32), jax.ShapeDtypeStruct((1, SSD_WIDTH), F32)],
        input_output_aliases={11: 3},
        scratch_shapes=[pltpu.VMEM((SSD_STATE, GROUP_W), F32)],
        compiler_params=_cparams(("parallel", "arbitrary", "arbitrary")),
    )(xbc_act, xbc_act, xbc_act, proj, states, dy, dtb, alog, dsk, proj, norm_w, dproj)


def _pad_heads(v):
    return jnp.pad(v.reshape(SSD_GROUPS, SSD_HPG), ((0, 0), (0, 128 - SSD_HPG))).reshape(1, SSD_GROUPS * 128)


def _unpad_heads(v):
    return v.reshape(SSD_GROUPS, 128)[:, :SSD_HPG].reshape(1, SSD_HEADS)


def _state_cols(v):
    re, im = v
    lead = re.shape[:-1]
    re = re.reshape(lead + (S5_NJ, 1, S5_LB))
    im = im.reshape(lead + (S5_NJ, 1, S5_LB))
    return jnp.concatenate([re, im], axis=-2).reshape(lead + (2 * S5_N,))


def _state_uncols(v):
    lead = v.shape[:-1]
    v = v.reshape(lead + (S5_NJ, 2, S5_LB))
    return v[..., 0, :].reshape(lead + (S5_N,)), v[..., 1, :].reshape(lead + (S5_N,))


GROUPS_PER_BAND = BAND // S5_GROUP


def _band(w2_re, w2_im):
    gh = S5_GROUPS * S5_GROUP
    rg = ((jnp.arange(gh) // S5_GROUP) % GROUPS_PER_BAND)[:, None, None]
    cg = jnp.arange(GROUPS_PER_BAND)[None, :, None]
    parts = [jnp.where(rg == cg, v[:, None, :], 0.0).reshape(gh, S5_LB) for v in (w2_re, w2_im)]
    return jnp.concatenate(parts, axis=1)


def _band_take(wb):
    gh = S5_GROUPS * S5_GROUP
    w4 = wb.reshape(gh, 2, GROUPS_PER_BAND, S5_STATE)
    sel = w4[jnp.arange(gh), :, (jnp.arange(gh) // S5_GROUP) % GROUPS_PER_BAND, :]
    return sel[:, 0, :], sel[:, 1, :]


W_IN_SHARD = IN_PROJ_DIM // N_CHIPS
W_IN_SEGS = ((0, 512, OFF_U5), (512, 1024, OFF_Z5), (1024, 2560, OFF_ZS), (2560, 5120, OFF_XBC), (5144, 7192, OFF_G5))
DT_ROWS = (5120, 5144)


def _w_in_pieces():
    runs = []
    segs = list(W_IN_SEGS) + [(DT_ROWS[0] + SSD_HPG * g, DT_ROWS[0] + SSD_HPG * (g + 1), OFF_DT + 128 * g)
                              for g in range(SSD_GROUPS)]
    for lo, hi, off in segs:
        for j in range(N_CHIPS):
            s, e = max(lo, j * W_IN_SHARD), min(hi, (j + 1) * W_IN_SHARD)
            if s < e:
                runs.append((j, s - j * W_IN_SHARD, off + s - lo, e - s))
    return runs


RELAYOUT_LANES = 256


def _pad_w_in_t(a4):
    runs = _w_in_pieces()

    def body(a_ref, o_ref):
        o_ref[pl.ds(OFF_DT, DT_W), :] = jnp.zeros((DT_W, RELAYOUT_LANES), o_ref.dtype)
        for j, src, dst, n in runs:
            o_ref[pl.ds(dst, n), :] = a_ref[j, pl.ds(src, n), :]

    return pl.pallas_call(
        body, name="w_in_to_padded", grid=(D_MODEL // RELAYOUT_LANES,),
        in_specs=[pl.BlockSpec((N_CHIPS, W_IN_SHARD, RELAYOUT_LANES), lambda i: (0, 0, i))],
        out_specs=pl.BlockSpec((PROJ_W, RELAYOUT_LANES), lambda i: (0, i)),
        out_shape=jax.ShapeDtypeStruct((PROJ_W, D_MODEL), a4.dtype),
        compiler_params=_cparams(("parallel",)),
    )(a4)


def _unpad_w_in_t(wp):
    runs = _w_in_pieces()

    def body(p_ref, o_ref):
        for j, dst, src, n in runs:
            o_ref[j, pl.ds(dst, n), :] = p_ref[pl.ds(src, n), :]

    return pl.pallas_call(
        body, name="w_in_from_padded", grid=(D_MODEL // RELAYOUT_LANES,),
        in_specs=[pl.BlockSpec((PROJ_W, RELAYOUT_LANES), lambda i: (0, i))],
        out_specs=pl.BlockSpec((N_CHIPS, W_IN_SHARD, RELAYOUT_LANES), lambda i: (0, 0, i)),
        out_shape=jax.ShapeDtypeStruct((N_CHIPS, W_IN_SHARD, D_MODEL), wp.dtype),
        compiler_params=_cparams(("parallel",)),
    )(wp)


def _local_step(x, p, tgt, w):
    n_seq, seq_len, _ = x.shape
    n_rows = n_seq * seq_len
    tr = 512
    x2 = x.reshape(n_rows, D_MODEL)
    p2 = p.reshape(n_rows, PLE_DIM)
    t2 = tgt.reshape(n_rows, D_MODEL)
    row = functools.partial(_rowwise, n_rows=n_rows, tr=tr)

    w_pad_t = _pad_w_in_t(w["w_in_t"])
    norm_w = w["norm_w"].reshape(1, D_MODEL)
    ple_norm_w = w["ple_norm_w"].reshape(1, D_MODEL)
    final_w = w["final_norm_w"].reshape(1, D_MODEL)
    s5_d = w["s5_d"].reshape(1, S5_WIDTH)
    b_glu = w["s5_b_glu"].reshape(1, S5_WIDTH)
    conv_w = w["ssd_conv_w"].reshape(SSD_CONV, SSD_CONV_DIM)
    conv_b = w["ssd_conv_b"].reshape(1, SSD_CONV_DIM)
    ssd_norm_w = w["ssd_norm_w"].reshape(1, SSD_WIDTH)
    dtb, alog, dsk = (_pad_heads(w[k].reshape(1, SSD_HEADS)) for k in ("ssd_dt_bias", "ssd_a_log", "ssd_d"))

    gh = S5_GROUPS * S5_GROUP
    a_re = w["s5_a_re"].reshape(S5_GROUPS, S5_STATE)
    a_im = w["s5_a_im"].reshape(S5_GROUPS, S5_STATE)
    log_step = w["s5_log_step"].reshape(S5_GROUPS, 1)
    b_re2 = jnp.transpose(w["s5_b_re"].reshape(S5_GROUPS, S5_STATE, S5_GROUP), (0, 2, 1)).reshape(gh, S5_STATE)
    b_im2 = jnp.transpose(w["s5_b_im"].reshape(S5_GROUPS, S5_STATE, S5_GROUP), (0, 2, 1)).reshape(gh, S5_STATE)
    expand = (jnp.arange(gh)[:, None] // S5_GROUP == jnp.arange(S5_GROUPS)[None, :]).astype(F32)
    pow_re, pow_im, bb_re2, bb_im2 = _s5_params_fwd(a_re, a_im, log_step, b_re2, b_im2, expand)
    lam_pow = _state_cols((pow_re.reshape(S5_LOG_TB, S5_N), pow_im.reshape(S5_LOG_TB, S5_N)))
    lam_pow_conj = _state_cols((pow_re.reshape(S5_LOG_TB, S5_N), -pow_im.reshape(S5_LOG_TB, S5_N)))
    bb_band = _band(bb_re2, bb_im2).astype(BF16)
    c_band = _band(w["s5_c_re"].reshape(gh, S5_STATE), -w["s5_c_im"].reshape(gh, S5_STATE)).astype(BF16)

    late = w.get("_late")
    (hn,) = row("rms_in", lambda r, q: ([_rms(r[0], q[0])], []), row_ins=[(x2, 0, D_MODEL)],
                par_ins=[(norm_w, 0, D_MODEL)] + ([(late[0], 0, 128)] if late else []), row_outs=[(D_MODEL, BF16)])
    proj = _matmul("mm_proj", hn, w_pad_t, tb=True)
    s, yc, ge = _s5_fwd(proj, bb_band, c_band, lam_pow, s5_d, n_seq, seq_len)
    if late:
        w = {**w, **late[1](ge)}
    tg, y5 = _matmul("mm_s5_glu", ge, w["s5_w_glu"], epilogue=lambda c, r, q: [c, _s5_out(r[0], r[1], r[2], c, *q)],
                     epi_rows=[(yc, 0), (proj, OFF_U5), (proj, OFF_Z5)], epi_pars=[s5_d, b_glu], epi_outs=[F32, BF16])
    s5_rows = [(yc, 0, S5_WIDTH), (proj, OFF_U5, S5_WIDTH), (proj, OFF_Z5, S5_WIDTH), (tg, 0, S5_WIDTH)]
    s5_pars = [(s5_d, 0, S5_WIDTH), (b_glu, 0, S5_WIDTH)]

    xbc_act = _conv_fwd(proj, conv_w, conv_b, n_rows, seq_len)
    yss, states = _ssd_fwd(xbc_act, proj, dtb, alog, dsk, ssd_norm_w, n_seq, seq_len)

    m5 = _matmul("mm_br_s5", y5, w["w_br_s5"])
    ms, merged = _matmul("mm_br_ssd", yss, w["w_br_ssd"], epilogue=lambda c, r, q: [c, _merge(r[0], r[1], r[2], c)],
                         epi_rows=[(proj, OFF_G5), (proj, OFF_GS), (m5, 0)], epi_outs=[F32, BF16])
    mg_rows = [(proj, OFF_G5, D_MODEL), (proj, OFF_GS, D_MODEL), (m5, 0, D_MODEL), (ms, 0, D_MODEL)]

    def resid_norm(c, r, q):
        h1_ = r[0] + c
        return [h1_, _rms(h1_, q[0])]

    h1, hp = _matmul("mm_out", merged, w["w_out"], epilogue=resid_norm, epi_rows=[(x2, 0)], epi_pars=[ple_norm_w],
                     epi_outs=[F32, BF16], full_rows=True)
    pp = _matmul("mm_ple_proj", p2, w["w_ple_proj"])

    def head_fn(pgl_, r, q):
        h1_, pp_, tgt_ = r
        loss, vjp = jax.vjp(lambda a, b, c, f: _head_loss(a, b, c, f, tgt_), h1_, pgl_, pp_, q[0])
        dh1_, dpgl_, dpp_, dfw_ = vjp(jnp.ones_like(loss))
        return [dh1_, dpgl_, dpp_], [loss, dfw_]

    dh2, dpgl, dpp, loss_acc, d_final_w = _matmul(
        "mm_ple_gate_head", hp, w["w_ple_gate"], epilogue=head_fn, epi_rows=[(h1, 0), (pp, 0), (t2, 0)],
        epi_pars=[final_w], epi_outs=[F32, BF16, BF16], epi_accs=[(1, 128), (1, D_MODEL)], full_rows=True)
    loss = loss_acc[0, 0]

    g = {}
    g["final_norm_w"] = d_final_w
    g["w_ple_gate"] = _matmul("mm_d_w_ple_gate", hp, dpgl, ta=True)
    g["w_ple_proj"] = _matmul("mm_d_w_ple_proj", p2, dpp, ta=True)

    def ple_norm_bwd(dhp_, r, q):
        h1_, dh2_ = r
        _, vjp = jax.vjp(_rms, h1_, q[0])
        dh, dw = vjp(dhp_)
        dh = dh + dh2_
        return [dh, dh], [dw]

    dh1, dh1_b, g["ple_norm_w"] = _matmul(
        "mm_d_hp", dpgl, w["w_ple_gate"], tb=True, epilogue=ple_norm_bwd, epi_rows=[(h1, 0), (dh2, 0)],
        epi_pars=[ple_norm_w], epi_outs=[F32, BF16], epi_accs=[(1, D_MODEL)], full_rows=True)
    g["w_out"] = _matmul("mm_d_w_out", merged, dh1_b, ta=True)

    dproj = lax.empty((n_rows, PROJ_W), BF16)

    def merge_bwd(dmerged, r, q):
        sg5, sgs = jax.nn.sigmoid(r[0]), jax.nn.sigmoid(r[1])
        d_gates = jnp.concatenate([dmerged * r[2] * sg5 * (1.0 - sg5), dmerged * r[3] * sgs * (1.0 - sgs)], axis=1)
        return [d_gates, dmerged * sg5, dmerged * sgs]

    dproj, dm5, dms = _matmul(
        "mm_d_merged", dh1_b, w["w_out"], tb=True, epilogue=merge_bwd,
        epi_rows=[(proj, OFF_G5), (proj, OFF_GS), (m5, 0), (ms, 0)], epi_outs=[BF16, BF16, BF16], full_rows=True,
        epi_into=(dproj, OFF_G5, 2 * D_MODEL))
    g["w_br_s5"] = _matmul("mm_d_w_br_s5", y5, dm5, ta=True)
    g["w_br_ssd"] = _matmul("mm_d_w_br_ssd", yss, dms, ta=True)
    dy5 = _matmul("mm_d_y5", dm5, w["w_br_s5"], tb=True)
    dyss = _matmul("mm_d_yss", dms, w["w_br_ssd"], tb=True)

    def s5_out_bwd_a(r, q):
        yc_, u_, z_, t_, dy_ = r
        d_, bg_ = q
        ge_ = jax.nn.gelu(yc_ + d_ * u_)
        _, vjp = jax.vjp(lambda a, z, t, b: a * jax.nn.sigmoid(t + b) * jax.nn.silu(z), ge_, z_, t_, bg_)
        dge, dz, dt_, dbg = vjp(dy_)
        return [dz, dge, dt_], [dbg]

    dproj, dge_a, dtg, g["s5_b_glu"] = row(
        "s5_out_bwd_a", s5_out_bwd_a, row_ins=s5_rows + [(dy5, 0, S5_WIDTH)], par_ins=s5_pars,
        row_outs=[(S5_WIDTH, BF16), (S5_WIDTH, F32), (S5_WIDTH, BF16)], acc_outs=[(1, S5_WIDTH)],
        into=(dproj, OFF_Z5))
    g["s5_w_glu"] = _matmul("mm_d_w_glu", ge, dtg, ta=True)
    early = w["_early"](g) if "_early" in w else None
    dge_b = _matmul("mm_d_ge", dtg, w["s5_w_glu"], tb=True)

    def s5_out_bwd_b(r, q):
        yc_, u_, da_, db_ = r
        _, vjp = jax.vjp(lambda yc, u, d: jax.nn.gelu(yc + d * u), yc_, u_, q[0])
        dyc_, du_, dd_ = vjp(da_ + db_)
        return [dyc_, du_], [dd_]

    dyc, du5_a, g["s5_d"] = row(
        "s5_out_bwd_b", s5_out_bwd_b,
        row_ins=[(yc, 0, S5_WIDTH), (proj, OFF_U5, S5_WIDTH), (dge_a, 0, S5_WIDTH), (dge_b, 0, S5_WIDTH)],
        par_ins=[(s5_d, 0, S5_WIDTH)] + ([(early, 0, 128)] if early is not None else []),
        row_outs=[(S5_WIDTH, BF16), (S5_WIDTH, F32)], acc_outs=[(1, S5_WIDTH)])
    dproj, d_lam, d_bb_band, d_c_band = _s5_bwd(dyc, s, proj, du5_a, bb_band, c_band, lam_pow_conj, dproj,
                                                n_seq, seq_len)

    d_lr, d_li = _state_uncols(d_lam)
    d_bbr, d_bbi = _band_take(d_bb_band)
    d_are, d_aim, d_ls, d_br2, d_bi2 = _s5_params_bwd(
        a_re, a_im, log_step, b_re2, b_im2, expand,
        d_lr.reshape(S5_GROUPS, S5_STATE), d_li.reshape(S5_GROUPS, S5_STATE), d_bbr, d_bbi)
    g["s5_a_re"], g["s5_a_im"], g["s5_log_step"] = d_are, d_aim, d_ls
    g["s5_b_re_ghp"], g["s5_b_im_ghp"] = d_br2, d_bi2
    d_cr, d_ci = _band_take(d_c_band)
    g["s5_c_re"], g["s5_c_im"] = d_cr, -d_ci

    dxs, dbm, dcm, dproj, ddt, d_dtb, d_alog, d_dsk, g["ssd_norm_w"] = _ssd_bwd(
        xbc_act, proj, states, dyss, dtb, alog, dsk, ssd_norm_w, n_seq, seq_len, dproj)
    (dproj,) = row("ssd_ddt", lambda r, q: ([r[0]], []), row_ins=[(ddt, 0, DT_W)], par_ins=[],
                   row_outs=[(DT_W, BF16)], into=(dproj, OFF_DT))
    g["ssd_dt_bias"], g["ssd_a_log"], g["ssd_d"] = _unpad_heads(d_dtb), _unpad_heads(d_alog), _unpad_heads(d_dsk)
    conv_dw, conv_db = [], []
    for nm, d_act, off in (("x", dxs, 0), ("b", dbm, SSD_WIDTH), ("c", dcm, SSD_WIDTH + SSD_BC)):
        dproj, dw_, db_ = _conv_bwd("ssd_conv_bwd_" + nm, proj, d_act, conv_w, conv_b, n_rows, seq_len, off, dproj)
        conv_dw.append(dw_)
        conv_db.append(db_)
    g["ssd_conv_w"] = jnp.concatenate(conv_dw, axis=1)
    g["ssd_conv_b"] = jnp.concatenate(conv_db, axis=1)

    g["w_in_t"] = _unpad_w_in_t(_matmul("mm_d_w_in", dproj, hn, ta=True))
    def norm_bwd(dhn_, r, q):
        x_, dh1_ = r
        _, vjp = jax.vjp(_rms, x_, q[0])
        dx_, dw_ = vjp(dhn_)
        return [dx_ + dh1_], [dw_]

    dx, g["norm_w"] = _matmul("mm_d_hn", dproj, w_pad_t, epilogue=norm_bwd, epi_rows=[(x2, 0), (dh1, 0)],
                              epi_pars=[norm_w], epi_outs=[F32], epi_accs=[(1, D_MODEL)], full_rows=True)
    return loss, dx.reshape(x.shape), g


HBM = pl.BlockSpec(memory_space=pltpu.HBM)


def _chip_index(x, y):
    return 2 * x + y


def _half(shape2d, axis, which):
    h = shape2d[axis] // 2
    sl = pl.ds(pl.multiple_of(which * h, 128 if axis else 8), h)
    return (slice(None), sl) if axis else (sl, slice(None))


def _gather_chips(split, axes, whole):
    ns, nw = len(split), len(whole)
    n = ns + nw

    def body(*refs):
        ins, outs = refs[:n], refs[n:2 * n]
        ici_send, ici_recv, d2d_send, d2d_recv, local_sems = refs[2 * n:]
        x, y, c = lax.axis_index("x"), lax.axis_index("y"), lax.axis_index("c")
        me = _chip_index(x, y)
        sibling = (x, y, 1 - c)
        peers = [(1 - x, y), (x, 1 - y), (1 - x, 1 - y)]

        def half(t, which):
            return _half(split[t].shape, axes[t], which)

        copies = []
        for t in range(n):
            loc = pltpu.make_async_copy(ins[t], outs[t].at[me], local_sems.at[t])
            loc.start()
            copies.append(loc)

        def ici(t, k, slot):
            px, py = peers[k]
            if t < ns:
                src, dst = ins[t].at[half(t, c)], outs[t].at[(slot,) + half(t, c)]
            else:
                src, dst = ins[t], outs[t].at[slot]
            return pltpu.make_async_remote_copy(src_ref=src, dst_ref=dst, send_sem=ici_send.at[t, k],
                                                recv_sem=ici_recv.at[t, k], device_id=(px, py, c), device_id_type=MESH)

        def d2d(t, k, which):
            rows = outs[t].at[(_chip_index(*peers[k]),) + half(t, which)]
            return pltpu.make_async_remote_copy(src_ref=rows, dst_ref=rows, send_sem=d2d_send.at[t, k],
                                                recv_sem=d2d_recv.at[t, k], device_id=sibling, device_id_type=MESH)

        sends = []
        for t in range(n):
            for k in range(3):
                cp = ici(t, k, me)
                cp.start()
                sends.append(cp)
        for t in range(n):
            for k in range(3):
                ici(t, k, _chip_index(*peers[k])).wait_recv()
                if t < ns:
                    cp = d2d(t, k, c)
                    cp.start()
                    sends.append(cp)
        for t in range(ns):
            for k in range(3):
                d2d(t, k, 1 - c).wait_recv()
        for cp in sends:
            cp.wait_send()
        for cp in copies:
            cp.wait()

    arrays = list(split) + list(whole)
    return pl.pallas_call(
        body, name="gather_weights",
        in_specs=[HBM] * n, out_specs=[HBM] * n,
        out_shape=[jax.ShapeDtypeStruct((N_CHIPS,) + a.shape, a.dtype) for a in arrays],
        scratch_shapes=[pltpu.SemaphoreType.DMA((n, 3)), pltpu.SemaphoreType.DMA((n, 3)),
                        pltpu.SemaphoreType.DMA((ns, 3)), pltpu.SemaphoreType.DMA((ns, 3)),
                        pltpu.SemaphoreType.DMA((n,))],
    )(*arrays)


SEM = pl.BlockSpec(memory_space=pltpu.SEMAPHORE)
DATAFLOW = pltpu.SideEffectType.DATAFLOW_SIDE_EFFECTING


def _gather_start(shards, after):
    n = len(shards)

    def body(*refs):
        ins, lands = refs[:n], refs[n:2 * n]
        send_sems, recv_sems = refs[2 * n + 1], refs[2 * n + 2]
        token = refs[-1]
        x, y, c = lax.axis_index("x"), lax.axis_index("y"), lax.axis_index("c")
        me = _chip_index(x, y)
        for t in range(n):
            for k, (px, py) in enumerate([(1 - x, y), (x, 1 - y), (1 - x, 1 - y)]):
                pltpu.make_async_remote_copy(
                    src_ref=ins[t], dst_ref=lands[t].at[me], send_sem=send_sems.at[3 * t + k],
                    recv_sem=recv_sems.at[3 * t + k],
                    device_id=(px, py, c), device_id_type=MESH).start()
        token[...] = jnp.zeros_like(token)

    zones = [lax.empty((N_CHIPS,) + a.shape, a.dtype) for a in shards]
    res = pl.pallas_call(
        body, name="gather_rest_start",
        out_shape=(pltpu.SemaphoreType.DMA((3 * n,)), pltpu.SemaphoreType.DMA((3 * n,)),
                   *[pltpu.HBM(a.shape, a.dtype) for a in shards], *[pltpu.HBM(z.shape, z.dtype) for z in zones],
                   jax.ShapeDtypeStruct((8, 128), F32)),
        in_specs=[HBM] * (2 * n) + [pl.BlockSpec(memory_space=pl.ANY)],
        out_specs=(SEM, SEM, *[HBM] * (2 * n), pl.BlockSpec(memory_space=pltpu.VMEM)),
        input_output_aliases={t: 2 + t for t in range(2 * n)},
        compiler_params=pltpu.CompilerParams(has_side_effects=DATAFLOW),
    )(*[pltpu.with_memory_space_constraint(a, pltpu.HBM) for a in shards],
      *[pltpu.with_memory_space_constraint(z, pltpu.HBM) for z in zones], after)
    return res[0], res[1], list(res[2:2 + n]), list(res[2 + n:2 + 2 * n]), res[-1]


def _gather_wait(send_sems, recv_sems, thru, lands, after):
    n = len(thru)

    def body(*refs):
        ins, zones = refs[:n], refs[n:2 * n]
        s_sems, r_sems = refs[2 * n], refs[2 * n + 1]
        x, y, c = lax.axis_index("x"), lax.axis_index("y"), lax.axis_index("c")
        for t in range(n):
            for k, (px, py) in enumerate([(1 - x, y), (x, 1 - y), (1 - x, 1 - y)]):
                cp = pltpu.make_async_remote_copy(
                    src_ref=ins[t], dst_ref=zones[t].at[_chip_index(px, py)], send_sem=s_sems.at[3 * t + k],
                    recv_sem=r_sems.at[3 * t + k], device_id=(px, py, c), device_id_type=MESH)
                cp.wait_send()
                cp.wait_recv()

    res = pl.pallas_call(
        body, name="gather_rest_wait",
        out_shape=(*[pltpu.HBM(a.shape, a.dtype) for a in thru], *[pltpu.HBM(z.shape, z.dtype) for z in lands]),
        in_specs=[HBM] * (2 * n) + [SEM, SEM, pl.BlockSpec(memory_space=pl.ANY)], out_specs=[HBM] * (2 * n),
        input_output_aliases={t: t for t in range(2 * n)},
        compiler_params=pltpu.CompilerParams(has_side_effects=DATAFLOW),
    )(*thru, *lands, send_sems, recv_sems, after)
    return list(res[:n]), list(res[n:])


def _scatter_start(slotted, after):
    n = len(slotted)

    def body(*refs):
        ins, lands = refs[:n], refs[n:2 * n]
        send_sems, recv_sems = refs[2 * n + 1], refs[2 * n + 2]
        token = refs[-1]
        x, y, c = lax.axis_index("x"), lax.axis_index("y"), lax.axis_index("c")
        for t in range(n):
            for k, (px, py) in enumerate([(1 - x, y), (x, 1 - y), (1 - x, 1 - y)]):
                pltpu.make_async_remote_copy(
                    src_ref=ins[t].at[_chip_index(px, py)], dst_ref=lands[t].at[k], send_sem=send_sems.at[3 * t + k],
                    recv_sem=recv_sems.at[3 * t + k], device_id=(px, py, c), device_id_type=MESH).start()
        token[...] = jnp.zeros_like(token)

    zones = [lax.empty((3,) + a.shape[1:], a.dtype) for a in slotted]
    res = pl.pallas_call(
        body, name="scatter_rest_start",
        out_shape=(pltpu.SemaphoreType.DMA((3 * n,)), pltpu.SemaphoreType.DMA((3 * n,)),
                   *[pltpu.HBM(a.shape, a.dtype) for a in slotted], *[pltpu.HBM(z.shape, z.dtype) for z in zones],
                   jax.ShapeDtypeStruct((8, 128), F32)),
        in_specs=[HBM] * (2 * n) + [pl.BlockSpec(memory_space=pl.ANY)],
        out_specs=(SEM, SEM, *[HBM] * (2 * n), pl.BlockSpec(memory_space=pltpu.VMEM)),
        input_output_aliases={t: 2 + t for t in range(2 * n)},
        compiler_params=pltpu.CompilerParams(has_side_effects=DATAFLOW),
    )(*[pltpu.with_memory_space_constraint(a, pltpu.HBM) for a in slotted],
      *[pltpu.with_memory_space_constraint(z, pltpu.HBM) for z in zones], after)
    return res[0], res[1], list(res[2:2 + n]), list(res[2 + n:2 + 2 * n]), res[-1]


def _scatter_wait(send_sems, recv_sems, thru, lands, after):
    n = len(thru)

    def body(*refs):
        ins, zones = refs[:n], refs[n:2 * n]
        s_sems, r_sems = refs[2 * n], refs[2 * n + 1]
        x, y, c = lax.axis_index("x"), lax.axis_index("y"), lax.axis_index("c")
        for t in range(n):
            for k, (px, py) in enumerate([(1 - x, y), (x, 1 - y), (1 - x, 1 - y)]):
                cp = pltpu.make_async_remote_copy(
                    src_ref=ins[t].at[_chip_index(px, py)], dst_ref=zones[t].at[k], send_sem=s_sems.at[3 * t + k],
                    recv_sem=r_sems.at[3 * t + k], device_id=(px, py, c), device_id_type=MESH)
                cp.wait_send()
                cp.wait_recv()

    res = pl.pallas_call(
        body, name="scatter_rest_wait",
        out_shape=(*[pltpu.HBM(a.shape, a.dtype) for a in thru], *[pltpu.HBM(z.shape, z.dtype) for z in lands]),
        in_specs=[HBM] * (2 * n) + [SEM, SEM, pl.BlockSpec(memory_space=pl.ANY)], out_specs=[HBM] * (2 * n),
        input_output_aliases={t: t for t in range(2 * n)},
        compiler_params=pltpu.CompilerParams(has_side_effects=DATAFLOW),
    )(*thru, *lands, send_sems, recv_sems, after)
    return list(res[:n]), list(res[n:])


def _sum_chips(name, slotted, recv, idx):
    _, r, c = slotted.shape
    tr = _row_tile(r, 5 * c * 4)

    def body(idx_ref, own_ref, r_ref, o_ref):
        acc = own_ref[0]
        for k in range(3):
            acc = acc + r_ref[k]
        o_ref[...] = acc

    grid_spec = pltpu.PrefetchScalarGridSpec(
        num_scalar_prefetch=1, grid=(r // tr,),
        in_specs=[pl.BlockSpec((1, tr, c), lambda i, s: (s[1], i, 0)), pl.BlockSpec((3, tr, c), lambda i, s: (0, i, 0))],
        out_specs=pl.BlockSpec((tr, c), lambda i, s: (i, 0)))
    return pl.pallas_call(
        body, name=name, grid_spec=grid_spec, out_shape=jax.ShapeDtypeStruct((r, c), F32),
        compiler_params=_cparams(("parallel",)),
    )(idx, slotted, recv)


def _half_shape(shape2d, axis):
    r, c = shape2d
    return (r, c // 2) if axis else (r // 2, c)


def _swap_halves(slotted, axes, small):
    n = len(slotted)

    def body(*refs):
        ins, sm_in = refs[:n], refs[n]
        outs, sm_out = refs[n + 1:2 * n + 1], refs[2 * n + 1]
        send_sems, recv_sems, local_sem = refs[2 * n + 2:]
        x, y, c = lax.axis_index("x"), lax.axis_index("y"), lax.axis_index("c")
        local = pltpu.make_async_copy(sm_in, sm_out.at[c], local_sem)
        local.start()
        sends = []
        for t in range(n):
            other = (slice(None),) + _half(slotted[t].shape[1:], axes[t], 1 - c)
            cp = pltpu.make_async_remote_copy(
                src_ref=ins[t].at[other], dst_ref=outs[t], send_sem=send_sems.at[t],
                recv_sem=recv_sems.at[t], device_id=(x, y, 1 - c), device_id_type=MESH)
            cp.start()
            sends.append(cp)
        cp = pltpu.make_async_remote_copy(
            src_ref=sm_in, dst_ref=sm_out.at[c], send_sem=send_sems.at[n], recv_sem=recv_sems.at[n],
            device_id=(x, y, 1 - c), device_id_type=MESH)
        cp.start()
        sends.append(cp)
        for cp in sends[:n]:
            cp.wait_recv()
        pltpu.make_async_remote_copy(
            src_ref=sm_in, dst_ref=sm_out.at[1 - c], send_sem=send_sems.at[n], recv_sem=recv_sems.at[n],
            device_id=(x, y, 1 - c), device_id_type=MESH).wait_recv()
        for cp in sends:
            cp.wait_send()
        local.wait()

    return pl.pallas_call(
        body, name="swap_halves",
        in_specs=[HBM] * (n + 1), out_specs=[HBM] * (n + 1),
        out_shape=[jax.ShapeDtypeStruct((a.shape[0],) + _half_shape(a.shape[1:], ax), a.dtype)
                   for a, ax in zip(slotted, axes, strict=True)]
        + [jax.ShapeDtypeStruct((2,) + small.shape, small.dtype)],
        scratch_shapes=[pltpu.SemaphoreType.DMA((n + 1,)), pltpu.SemaphoreType.DMA((n + 1,)), pltpu.SemaphoreType.DMA],
    )(*slotted, small)


def _scatter_halves(parts, small):
    n = len(parts)
    half_rows = small.shape[0] // 2

    def body(*refs):
        ins, sm_in = refs[:n], refs[n]
        outs, sm_out = refs[n + 1:2 * n + 1], refs[2 * n + 1]
        send_sems, recv_sems, sm_send, sm_recv, local_sem = refs[2 * n + 2:]
        x, y, c = lax.axis_index("x"), lax.axis_index("y"), lax.axis_index("c")
        me = _chip_index(x, y)
        peers = [(1 - x, y), (x, 1 - y), (1 - x, 1 - y)]
        mine = sm_in.at[pl.ds(pl.multiple_of(c * half_rows, 8), half_rows), :]
        local = pltpu.make_async_copy(mine, sm_out.at[me], local_sem)
        local.start()
        sends = []
        for k, (px, py) in enumerate(peers):
            cp = pltpu.make_async_remote_copy(src_ref=mine, dst_ref=sm_out.at[me], send_sem=sm_send.at[k],
                                              recv_sem=sm_recv.at[k], device_id=(px, py, c), device_id_type=MESH)
            cp.start()
            sends.append(cp)
        for t in range(n):
            for k, (px, py) in enumerate(peers):
                cp = pltpu.make_async_remote_copy(
                    src_ref=ins[t].at[_chip_index(px, py)], dst_ref=outs[t].at[k], send_sem=send_sems.at[t, k],
                    recv_sem=recv_sems.at[t, k], device_id=(px, py, c), device_id_type=MESH)
                cp.start()
                sends.append(cp)
        for k, (px, py) in enumerate(peers):
            pltpu.make_async_remote_copy(src_ref=mine, dst_ref=sm_out.at[_chip_index(px, py)], send_sem=sm_send.at[k],
                                         recv_sem=sm_recv.at[k], device_id=(px, py, c), device_id_type=MESH).wait_recv()
        for cp in sends[3:]:
            cp.wait_recv()
        for cp in sends:
            cp.wait_send()
        local.wait()

    return pl.pallas_call(
        body, name="scatter_halves",
        in_specs=[HBM] * (n + 1), out_specs=[HBM] * (n + 1),
        out_shape=[jax.ShapeDtypeStruct((3,) + a.shape[1:], a.dtype) for a in parts]
        + [jax.ShapeDtypeStruct((N_CHIPS, half_rows, small.shape[1]), small.dtype)],
        scratch_shapes=[pltpu.SemaphoreType.DMA((n, 3)), pltpu.SemaphoreType.DMA((n, 3)),
                        pltpu.SemaphoreType.DMA((3,)), pltpu.SemaphoreType.DMA((3,)), pltpu.SemaphoreType.DMA],
    )(*parts, small)


def _tiling(r, c, f32_per_elem):
    if r % 8 == 0:
        tr = _row_tile(r, f32_per_elem * c * 4)
        return r // tr, (tr, c), lambda i: (i, 0)
    assert c % 128 == 0, (r, c)
    return c // 128, (r, 128), lambda i: (0, i)


def _pair_sum(name, slotted, other, idx, axis):
    _, r, c = slotted.shape
    hr, hc = _half_shape((r, c), axis)
    n, (tr, tc), at = _tiling(hr, hc, 13)
    if axis == 0:
        a = slotted.reshape(N_CHIPS, 2, hr, c)
        a_all = pl.BlockSpec((N_CHIPS, 1, tr, tc), lambda i, s: (0, s[0]) + at(i))
        a_own = pl.BlockSpec((1, 1, tr, tc), lambda i, s: (s[1], s[0]) + at(i))
    else:
        a, per_half = slotted, hc // tc
        a_all = pl.BlockSpec((N_CHIPS, tr, tc), lambda i, s: (0, at(i)[0], s[0] * per_half + at(i)[1]))
        a_own = pl.BlockSpec((1, tr, tc), lambda i, s: (s[1], at(i)[0], s[0] * per_half + at(i)[1]))

    def body(idx_ref, a_ref, b_ref, am_ref, bm_ref, p_ref, own_ref):
        mine, mine_own = (a_ref[:, 0], am_ref[0, 0]) if axis == 0 else (a_ref[...], am_ref[0])
        p_ref[...] = (mine + b_ref[...]).astype(p_ref.dtype)
        own_ref[...] = mine_own + bm_ref[0]

    grid_spec = pltpu.PrefetchScalarGridSpec(
        num_scalar_prefetch=1, grid=(n,),
        in_specs=[a_all, pl.BlockSpec((N_CHIPS, tr, tc), lambda i, s: (0,) + at(i)),
                  a_own, pl.BlockSpec((1, tr, tc), lambda i, s: (s[1],) + at(i))],
        out_specs=[pl.BlockSpec((N_CHIPS, tr, tc), lambda i, s: (0,) + at(i)),
                   pl.BlockSpec((tr, tc), lambda i, s: at(i))])
    return pl.pallas_call(
        body, name=name, grid_spec=grid_spec,
        out_shape=[jax.ShapeDtypeStruct((N_CHIPS, hr, hc), BF16), jax.ShapeDtypeStruct((hr, hc), F32)],
        compiler_params=_cparams(("parallel",)),
    )(idx, a, other, a, other)


def _sum_parts(name, own, recv):
    h, c = own.shape
    n, (tr, tc), at = _tiling(h, c, 4)

    def body(o_ref, r_ref, out_ref):
        acc = o_ref[...]
        for k in range(3):
            acc = acc + r_ref[k].astype(F32)
        out_ref[...] = acc

    return pl.pallas_call(
        body, name=name, grid=(n,),
        in_specs=[pl.BlockSpec((tr, tc), at), pl.BlockSpec((3, tr, tc), lambda i: (0,) + at(i))],
        out_specs=pl.BlockSpec((tr, tc), at),
        out_shape=jax.ShapeDtypeStruct((h, c), F32),
        compiler_params=_cparams(("parallel",)),
    )(own, recv)


def _swap_sibling(parts):
    n = len(parts)

    def body(*refs):
        ins, outs = refs[:n], refs[n:2 * n]
        send_sems, recv_sems = refs[2 * n:]
        x, y, c = lax.axis_index("x"), lax.axis_index("y"), lax.axis_index("c")
        cps = []
        for t in range(n):
            cp = pltpu.make_async_remote_copy(
                src_ref=ins[t], dst_ref=outs[t], send_sem=send_sems.at[t], recv_sem=recv_sems.at[t],
                device_id=(x, y, 1 - c), device_id_type=MESH)
            cp.start()
            cps.append(cp)
        for cp in cps:
            cp.wait_recv()
        for cp in cps:
            cp.wait_send()

    return pl.pallas_call(
        body, name="swap_sibling",
        in_specs=[HBM] * n, out_specs=[HBM] * n,
        out_shape=[jax.ShapeDtypeStruct(a.shape, a.dtype) for a in parts],
        scratch_shapes=[pltpu.SemaphoreType.DMA((n,)), pltpu.SemaphoreType.DMA((n,))],
    )(*parts)


def _sum_slots(name, a):
    k, r, c = a.shape
    tr = _row_tile(r, (k + 1) * c * 4)

    def body(a_ref, o_ref):
        acc = a_ref[0]
        for i in range(1, k):
            acc = acc + a_ref[i]
        o_ref[...] = acc

    return pl.pallas_call(
        body, name=name, grid=(r // tr,),
        in_specs=[pl.BlockSpec((k, tr, c), lambda i: (0, i, 0))],
        out_specs=pl.BlockSpec((tr, c), lambda i: (i, 0)),
        out_shape=jax.ShapeDtypeStruct((r, c), a.dtype),
        compiler_params=_cparams(("parallel",)),
    )(a)


def _adamw(name, w, m, v, g_parts):
    r, c = w.shape
    ng = len(g_parts)
    tr = _row_tile(r, (7 + ng) * c * 4)
    c1 = 1.0 - ADAM_B1 ** ADAM_STEP
    c2 = 1.0 - ADAM_B2 ** ADAM_STEP

    def body(*refs):
        w_ref, m_ref, v_ref = refs[:3]
        g_refs = refs[3:3 + ng]
        go_ref, d_ref, mo_ref, vo_ref = refs[3 + ng:]
        g = g_refs[0][...]
        for gr in g_refs[1:]:
            g = g + gr[...]
        m_new = ADAM_B1 * m_ref[...] + (1.0 - ADAM_B1) * g
        v_new = ADAM_B2 * v_ref[...] + (1.0 - ADAM_B2) * (g * g)
        go_ref[...] = g
        mo_ref[...] = m_new
        vo_ref[...] = v_new
        d_ref[...] = -ADAM_LR * ((m_new / c1) / (jnp.sqrt(v_new / c2) + ADAM_EPS) + ADAM_WD * w_ref[...])

    spec = pl.BlockSpec((tr, c), lambda i: (i, 0))
    return pl.pallas_call(
        body, name=name, grid=(r // tr,),
        in_specs=[spec] * (3 + ng), out_specs=[spec] * 4,
        out_shape=[jax.ShapeDtypeStruct((r, c), F32)] * 4,
        compiler_params=_cparams(("parallel",)),
    )(w, m, v, *g_parts)


def _adamw_halves(name, w, m, v, own, other, idx, axis):
    hr, hc = own.shape
    nb, (tr, tc), at = _tiling(hr, hc, 9)
    c1 = 1.0 - ADAM_B1 ** ADAM_STEP
    c2 = 1.0 - ADAM_B2 ** ADAM_STEP

    def body(idx_ref, w_ref, m_ref, v_ref, own_ref, oth_ref, go_ref, d_ref, mo_ref, vo_ref):
        g = jnp.where(pl.program_id(0) == idx_ref[0], own_ref[...], oth_ref[...])
        m_new = ADAM_B1 * m_ref[...] + (1.0 - ADAM_B1) * g
        v_new = ADAM_B2 * v_ref[...] + (1.0 - ADAM_B2) * (g * g)
        go_ref[...] = g
        mo_ref[...] = m_new
        vo_ref[...] = v_new
        d_ref[...] = -ADAM_LR * ((m_new / c1) / (jnp.sqrt(v_new / c2) + ADAM_EPS) + ADAM_WD * w_ref[...])

    per_half = (hc // tc) if axis else (hr // tr)
    if axis:
        full = pl.BlockSpec((tr, tc), lambda hh, i, s: (at(i)[0], hh * per_half + at(i)[1]))
    else:
        full = pl.BlockSpec((tr, tc), lambda hh, i, s: (hh * per_half + at(i)[0], at(i)[1]))
    part = pl.BlockSpec((tr, tc), lambda hh, i, s: at(i))
    grid_spec = pltpu.PrefetchScalarGridSpec(
        num_scalar_prefetch=1, grid=(2, nb), in_specs=[full, full, full, part, part], out_specs=[full] * 4)
    return pl.pallas_call(
        body, name=name, grid_spec=grid_spec, out_shape=[jax.ShapeDtypeStruct(w.shape, F32)] * 4,
        compiler_params=_cparams(("parallel", "parallel")),
    )(idx, w, m, v, own, other)


WEIGHTS = ['norm_w', 'w_in', 's5_a_re', 's5_a_im', 's5_b_re', 's5_b_im', 's5_c_re', 's5_c_im', 's5_d', 's5_log_step',
           's5_w_glu', 's5_b_glu', 'ssd_conv_w', 'ssd_conv_b', 'ssd_dt_bias', 'ssd_a_log', 'ssd_d', 'ssd_norm_w',
           'w_br_s5', 'w_br_ssd', 'w_out', 'ple_norm_w', 'w_ple_gate', 'w_ple_proj', 'final_norm_w']
SHARDED = {'w_in': ((IN_PROJ_DIM, 1024), 0), 's5_w_glu': ((512, 512), 0), 'ssd_conv_w': ((SSD_CONV, SSD_CONV_DIM), 1),
           'w_br_s5': ((512, 1024), 1), 'w_br_ssd': ((1536, 1024), 0), 'w_out': ((1024, 1024), 0),
           'w_ple_gate': ((1024, 1024), 0), 'w_ple_proj': ((256, 1024), 1)}
TRANSPOSED = ('w_in',)
SMALL = [n for n in WEIGHTS if n not in SHARDED]


def _shard_shape(name):
    (r, c), ax = SHARDED[name]
    return (r // N_CHIPS, c) if ax == 0 else (r, c // N_CHIPS)


def _half_axis(name):
    return 0 if (_shard_shape(name)[0] // 2) % 16 == 0 else 1


def _shard2d(name, a):
    r, c = _shard_shape(name)
    return a.reshape(c, r).T if name in TRANSPOSED else a.reshape(r, c)


def _unshard2d(name, a2, shape):
    return (a2.T if name in TRANSPOSED else a2).reshape(shape)


def _unslot(name, a4):
    (r, c), ax = SHARDED[name]
    if ax == 0:
        return a4.reshape(r, c)
    return jnp.transpose(a4, (1, 0, 2)).reshape(r, c)


def _slot(name, full):
    (r, c), ax = SHARDED[name]
    if ax == 0:
        return full.reshape(N_CHIPS, r // N_CHIPS, c)
    return jnp.transpose(full.reshape(r, N_CHIPS, c // N_CHIPS), (1, 0, 2))


GHP = ('s5_b_re', 's5_b_im')


def _view_shape(name):
    if name in ('s5_a_re', 's5_a_im'):
        return (S5_GROUPS, S5_STATE)
    if name in GHP + ('s5_c_re', 's5_c_im'):
        return (S5_GROUPS, S5_GROUP, S5_STATE)
    if name == 'ssd_conv_w':
        return (SSD_CONV, SSD_CONV_DIM // N_CHIPS)
    return (1, {'s5_log_step': S5_GROUPS, 'ssd_conv_b': SSD_CONV_DIM, 'ssd_norm_w': SSD_WIDTH, 's5_d': S5_WIDTH,
                's5_b_glu': S5_WIDTH, 'ssd_dt_bias': SSD_HEADS, 'ssd_a_log': SSD_HEADS, 'ssd_d': SSD_HEADS}.get(name, D_MODEL))


def _view(name, a):
    if name in GHP:
        return jnp.swapaxes(a.reshape(S5_GROUPS, S5_STATE, S5_GROUP), 1, 2)
    return a.reshape(_view_shape(name))


def _unview(name, a, shape):
    return (jnp.swapaxes(a, 1, 2) if name in GHP else a).reshape(shape)


def _adamw_small(ws, ms, vs, gs):
    n = len(ws)
    c1 = 1.0 - ADAM_B1 ** ADAM_STEP
    c2 = 1.0 - ADAM_B2 ** ADAM_STEP

    def body(*refs):
        w_r, m_r, v_r, g_r = (refs[k * n:(k + 1) * n] for k in range(4))
        d_o, m_o, v_o = (refs[k * n:(k + 1) * n] for k in range(4, 7))
        for i in range(n):
            g = g_r[i][...]
            m_new = ADAM_B1 * m_r[i][...] + (1.0 - ADAM_B1) * g
            v_new = ADAM_B2 * v_r[i][...] + (1.0 - ADAM_B2) * (g * g)
            m_o[i][...] = m_new
            v_o[i][...] = v_new
            d_o[i][...] = -ADAM_LR * ((m_new / c1) / (jnp.sqrt(v_new / c2) + ADAM_EPS) + ADAM_WD * w_r[i][...])

    return pl.pallas_call(
        body, name="adamw_small", out_shape=[jax.ShapeDtypeStruct(w.shape, F32) for w in ws] * 3,
        compiler_params=pltpu.CompilerParams(vmem_limit_bytes=VMEM_LIMIT),
    )(*ws, *ms, *vs, *gs)


def _pack_small(vals):
    flat = jnp.concatenate([v.reshape(-1).astype(F32) for v in vals])
    rows = -(-flat.shape[0] // (256 * 128)) * 256
    return jnp.pad(flat, (0, rows * 128 - flat.shape[0])).reshape(rows, 128)


def _unpack_small(packed, shapes):
    flat = packed.reshape(-1)
    out, off = [], 0
    for sh in shapes:
        n = math.prod(sh)
        out.append(flat[off:off + n].reshape(sh))
        off += n
    return out


def kernel(x, p, norm_w, w_in, s5_a_re, s5_a_im, s5_b_re, s5_b_im, s5_c_re, s5_c_im, s5_d, s5_log_step, s5_w_glu, s5_b_glu, ssd_conv_w, ssd_conv_b, ssd_dt_bias, ssd_a_log, ssd_d, ssd_norm_w, w_br_s5, w_br_ssd, w_out, ple_norm_w, w_ple_gate, w_ple_proj, final_norm_w, loss_target, m_norm_w, m_w_in, m_s5_a_re, m_s5_a_im, m_s5_b_re, m_s5_b_im, m_s5_c_re, m_s5_c_im, m_s5_d, m_s5_log_step, m_s5_w_glu, m_s5_b_glu, m_ssd_conv_w, m_ssd_conv_b, m_ssd_dt_bias, m_ssd_a_log, m_ssd_d, m_ssd_norm_w, m_w_br_s5, m_w_br_ssd, m_w_out, m_ple_norm_w, m_w_ple_gate, m_w_ple_proj, m_final_norm_w, v_norm_w, v_w_in, v_s5_a_re, v_s5_a_im, v_s5_b_re, v_s5_b_im, v_s5_c_re, v_s5_c_im, v_s5_d, v_s5_log_step, v_s5_w_glu, v_s5_b_glu, v_ssd_conv_w, v_ssd_conv_b, v_ssd_dt_bias, v_ssd_a_log, v_ssd_d, v_ssd_norm_w, v_w_br_s5, v_w_br_ssd, v_w_out, v_ple_norm_w, v_w_ple_gate, v_w_ple_proj, v_final_norm_w):
    args = locals()
    wl = {n: args[n] for n in WEIGHTS}
    ml = {n: args["m_" + n] for n in WEIGHTS}
    vl = {n: args["v_" + n] for n in WEIGHTS}
    big = [n for n in SHARDED if n != 'ssd_conv_w']
    chip = _chip_index(lax.axis_index("x"), lax.axis_index("y"))
    idx = jnp.stack([lax.axis_index("c"), chip]).astype(jnp.int32)

    axes = [_half_axis(n) for n in big]
    first = ['w_in']
    rest = [n for n in big if n not in first]
    bf_shards = {n: _shard2d(n, wl[n]).astype(BF16) for n in big}
    w_in_t, conv_w4 = _gather_chips([bf_shards[n] for n in first], [_half_axis(n) for n in first],
                                    [_shard2d('ssd_conv_w', wl['ssd_conv_w'])])
    full = {n: wl[n] for n in SMALL}
    full["w_in_t"] = w_in_t
    full['ssd_conv_w'] = _unslot('ssd_conv_w', conv_w4)
    send_sems, recv_sems, thru, lands, token = _gather_start([bf_shards[n] for n in rest], w_in_t)

    def fetch_rest(after):
        own, zones = _gather_wait(send_sems, recv_sems, thru, lands, after)
        return {n: _unslot(n, lax.dynamic_update_slice(z, o[None], (chip, 0, 0)))
                for n, o, z in zip(rest, own, zones, strict=True)}

    full["_late"] = (token, fetch_rest)

    in_flight = []

    def send_rest(g_now):
        in_flight.extend(_scatter_start([_slot(n, g_now[n]) for n in rest], g_now['s5_w_glu']))
        return in_flight[4]

    full["_early"] = send_rest
    loss, grad_x, g = _local_step(x, p[0], loss_target, full)
    for n in TRANSPOSED:
        g[n] = g.pop(n + "_t")

    for n in ('s5_b_re', 's5_b_im'):
        g[n] = g.pop(n + "_ghp")
    small_shapes = [(1, 1)] + [_view_shape(n) for n in SMALL] + [(SSD_CONV, SSD_CONV_DIM)]
    small_pack = _pack_small([loss] + [g[n] for n in SMALL] + [g['ssd_conv_w']])
    first_axes = [_half_axis(n) for n in first]
    slotted = [g[n] for n in first]
    swapped = _swap_halves(slotted, first_axes, small_pack)
    pair = [_pair_sum("pair_sum_" + n, a, b, idx, ax)
            for n, a, b, ax in zip(first, slotted, swapped[:-1], first_axes, strict=True)]
    small_chip = _sum_slots("sum_small_pair", swapped[-1])
    recv = _scatter_halves([pb for pb, _ in pair], small_chip)
    small_half = _sum_slots("sum_small_chips", recv[-1])
    halves = [_sum_parts("sum_chips_" + n, own, r) for n, (_, own), r in zip(first, pair, recv[:-1], strict=True)]
    own_slots, zones = _scatter_wait(*in_flight[:4], g['w_in'])
    chip_sums = [_sum_chips("sum_chips_" + n, a, z, idx) for n, a, z in zip(rest, own_slots, zones, strict=True)]
    swapped2 = _swap_sibling(halves + chip_sums + [small_half])
    other_halves, sib_sums = swapped2[:len(first)], swapped2[len(first):-1]
    c_is_0 = lax.axis_index("c") == 0
    small_sum = jnp.concatenate([jnp.where(c_is_0, small_half, swapped2[-1]),
                                 jnp.where(c_is_0, swapped2[-1], small_half)], axis=0)

    out_g, out_d, out_m, out_v = {}, {}, {}, {}
    for n, own, oth, ax in zip(first, halves, other_halves, first_axes, strict=True):
        res = _adamw_halves("adamw_" + n, _shard2d(n, wl[n]), _shard2d(n, ml[n]), _shard2d(n, vl[n]), own, oth, idx, ax)
        out_g[n], out_d[n], out_m[n], out_v[n] = (_unshard2d(n, r, wl[n].shape) for r in res)
    for n, own, sib in zip(rest, chip_sums, sib_sums, strict=True):
        res = _adamw("adamw_" + n, _shard2d(n, wl[n]), _shard2d(n, ml[n]), _shard2d(n, vl[n]), [own, sib])
        out_g[n], out_d[n], out_m[n], out_v[n] = (_unshard2d(n, r, wl[n].shape) for r in res)
    sm = _unpack_small(small_sum, small_shapes)
    loss_total = sm[0].reshape(())
    conv_g = lax.dynamic_slice(sm[-1], (0, chip * (SSD_CONV_DIM // N_CHIPS)), (SSD_CONV, SSD_CONV_DIM // N_CHIPS))
    names = SMALL + ['ssd_conv_w']
    grads = sm[1:-1] + [conv_g]
    res = _adamw_small([_view(n, wl[n]) for n in names], [_view(n, ml[n]) for n in names],
                       [_view(n, vl[n]) for n in names], grads)
    for i, n in enumerate(names):
        out_g[n] = _unview(n, grads[i], wl[n].shape)
        out_d[n], out_m[n], out_v[n] = (_unview(n, res[k * len(names) + i], wl[n].shape) for k in range(3))

    return (loss_total, grad_x, *[out_g[n] for n in WEIGHTS], *[out_d[n] for n in WEIGHTS],
            *[out_m[n] for n in WEIGHTS], *[out_v[n] for n in WEIGHTS])
```

```python
import functools
import math

import jax
import jax.numpy as jnp
from jax import lax
from jax.experimental import pallas as pl
from jax.experimental.pallas import tpu as pltpu

F32 = jnp.float32
BF16 = jnp.bfloat16
MESH = pl.DeviceIdType.MESH

D_MODEL = 1024
PLE_DIM = 256
RMS_EPS = 1e-6
S5_WIDTH = 512
S5_GROUP = 16
S5_GROUPS = 32
S5_STATE = 64
S5_N = S5_GROUPS * S5_STATE
S5_LB = 512
S5_NJ = S5_N // S5_LB
S5_TB = 256
S5_LOG_TB = 8
SSD_WIDTH = 1536
SSD_HEADDIM = 64
SSD_HEADS = 24
SSD_GROUPS = 4
SSD_HPG = 6
SSD_STATE = 128
SSD_CONV = 4
SSD_CHUNK = 128
SSD_BC = 512
SSD_CONV_DIM = 2560
GROUP_W = SSD_WIDTH // SSD_GROUPS
N_CHIPS = 4
N_DEV = 8

OFF_XBC, OFF_U5, OFF_Z5, OFF_DT, OFF_G5, OFF_GS, OFF_ZS = 0, 2560, 3072, 3584, 4096, 5120, 6144
DT_W = 512
PROJ_W = 7680
IN_PROJ_DIM = 7192

ADAM_LR, ADAM_B1, ADAM_B2, ADAM_EPS, ADAM_WD, ADAM_STEP = 0.001, 0.9, 0.999, 1e-08, 0.01, 10

VMEM_LIMIT = 56 * 1024 * 1024


def _pick(n, cands):
    for c in cands:
        if n % c == 0:
            return c
    return n


ROW_BLOCK_BYTES = 8 * 1024 * 1024


def _row_tile(r, bytes_per_row):
    for t in (r, 4096, 2048, 1024, 512, 256, 128, 64, 32, 16, 8):
        if t <= r and r % t == 0 and t * bytes_per_row <= ROW_BLOCK_BYTES:
            return t
    return r


def _cparams(sem):
    return pltpu.CompilerParams(dimension_semantics=sem, vmem_limit_bytes=VMEM_LIMIT)


def _dg(a, b, ca, cb):
    return lax.dot_general(a.astype(BF16), b.astype(BF16), (((ca,), (cb,)), ((), ())), preferred_element_type=F32)


@jax.custom_vjp
def dot_nn(a, b):
    return _dg(a, b, 1, 0)


@jax.custom_vjp
def dot_nt(a, b):
    return _dg(a, b, 1, 1)


@jax.custom_vjp
def dot_tn(a, b):
    return _dg(a, b, 0, 0)


dot_nn.defvjp(lambda a, b: (_dg(a, b, 1, 0), (a, b)), lambda r, g: (_dg(g, r[1], 1, 1), _dg(r[0], g, 0, 0)))
dot_nt.defvjp(lambda a, b: (_dg(a, b, 1, 1), (a, b)), lambda r, g: (_dg(g, r[1], 1, 0), _dg(g, r[0], 0, 0)))
dot_tn.defvjp(lambda a, b: (_dg(a, b, 0, 0), (a, b)), lambda r, g: (_dg(r[1], g, 1, 1), _dg(r[0], g, 1, 0)))


MM_VMEM_BUDGET = 30 * 1024 * 1024


def _mm_tiles(m, n, k, sa, sb, so, tn_only=None):
    best, best_key = None, None
    for tm in (1024, 512, 256, 128, 64, 32, 16, 8):
        if m % tm:
            continue
        for tn in (2048, 1536, 1280, 1024, 768, 640, 512, 384, 256, 128):
            if n % tn or (tn_only is not None and tn not in tn_only):
                continue
            for tk in (k, 2048, 1536, 1280, 1024, 768, 512, 256, 128):
                if k % tk or tk > max(k, 128) or (tk == k and k > 2048 and k % 128 == 0):
                    continue
                need = 2 * (tm * tk * sa + tk * tn * sb + tm * tn * so) + (tm * tn * 4 if tk < k else 0)
                if need > MM_VMEM_BUDGET:
                    continue
                key = (tm * tn * tk, tk)
                if best_key is None or key > best_key:
                    best, best_key = (tm, tn, tk), key
    assert best is not None, (m, n, k)
    return best


def _matmul(name, a, b, *, ta=False, tb=False, a_win=None, out_dtype=F32, epilogue=None, epi_rows=(), epi_pars=(),
            epi_outs=(), full_rows=False, epi_accs=(), epi_into=None):
    a_off, a_w = a_win if a_win is not None else (0, a.shape[1])
    if ta:
        kdim, m = a.shape[0], a_w
    else:
        m, kdim = a.shape[0], a_w
    n = b.shape[0] if tb else b.shape[1]
    assert (b.shape[1] if tb else b.shape[0]) == kdim, (name, a.shape, b.shape)
    out_dtypes = list(epi_outs) if epilogue is not None else [out_dtype]
    so = sum(jnp.dtype(d).itemsize for d in out_dtypes) + sum(r.dtype.itemsize for r, _ in epi_rows)
    tn_ok = [n] if full_rows else [t for t in (1024, 512, 256, 128) if all(off % t == 0 for _, off in epi_rows)]
    tm, tn, tk = _mm_tiles(m, n, kdim, a.dtype.itemsize, b.dtype.itemsize, so, tn_ok if epilogue is not None else None)
    nk = kdim // tk
    n_er, n_ep, n_out = len(epi_rows), len(epi_pars), len(out_dtypes)
    if ta:
        assert a_off % tm == 0
        a_spec = pl.BlockSpec((tk, tm), lambda i, j, k: (k, i + a_off // tm))
    else:
        assert a_off % tk == 0
        a_spec = pl.BlockSpec((tm, tk), lambda i, j, k: (i, k + a_off // tk))
    if tb:
        b_spec = pl.BlockSpec((tn, tk), lambda i, j, k: (j, k))
    else:
        b_spec = pl.BlockSpec((tk, tn), lambda i, j, k: (k, j))
    ca, cb = (0 if ta else 1), (1 if tb else 0)

    n_acc = len(epi_accs)
    n_x = 1 if epi_into is not None else 0
    assert not (n_acc or n_x) or full_rows

    def body(a_ref, b_ref, *refs):
        er, ep = refs[:n_er], refs[n_er:n_er + n_ep]
        first_out = n_er + n_ep + n_x
        o_refs = refs[first_out:first_out + n_out]
        s_refs = refs[first_out + n_out:first_out + n_out + n_acc]
        acc = refs[first_out + n_out + n_acc:]

        def finish(c):
            outs = [c] if epilogue is None else epilogue(c, [r[...] for r in er], [p[...] for p in ep])
            if n_acc:
                outs, sums = outs

                @pl.when(pl.program_id(0) == 0)
                def _():
                    for s_ref in s_refs:
                        s_ref[...] = jnp.zeros_like(s_ref)

                for s_ref, s in zip(s_refs, sums, strict=True):
                    s_ref[...] += jnp.broadcast_to(s, s_ref.shape)
            for o_ref, o in zip(o_refs, outs, strict=True):
                o_ref[...] = o.astype(o_ref.dtype)

        if nk == 1:
            finish(_dg(a_ref[...], b_ref[...], ca, cb))
            return
        (acc_ref,) = acc
        k = pl.program_id(2)

        @pl.when(k == 0)
        def _():
            acc_ref[...] = jnp.zeros_like(acc_ref)

        acc_ref[...] += _dg(a_ref[...], b_ref[...], ca, cb)

        @pl.when(k == nk - 1)
        def _():
            finish(acc_ref[...])

    in_specs = [a_spec, b_spec]
    in_specs += [pl.BlockSpec((tm, tn), functools.partial(lambda i, j, k, c: (i, j + c), c=off // tn)) for _, off in epi_rows]
    in_specs += [pl.BlockSpec((p.shape[0], tn), lambda i, j, k: (0, j)) for p in epi_pars]
    out_specs = [pl.BlockSpec((tm, tn), lambda i, j, k: (i, j)) for _ in out_dtypes]
    out_shape = [jax.ShapeDtypeStruct((m, n), d) for d in out_dtypes]
    extra, aliases = [], {}
    if epi_into is not None:
        buf, off, width = epi_into
        assert off % width == 0 and buf.dtype == out_dtypes[0]
        in_specs.append(pl.BlockSpec(memory_space=pl.ANY))
        out_specs[0] = pl.BlockSpec((tm, width), functools.partial(lambda i, j, k, c: (i, c), c=off // width))
        out_shape[0] = jax.ShapeDtypeStruct(buf.shape, buf.dtype)
        extra, aliases = [buf], {2 + n_er + n_ep: 0}
    out_specs += [pl.BlockSpec((r, w), lambda i, j, k: (0, 0)) for r, w in epi_accs]
    out_shape += [jax.ShapeDtypeStruct((r, w), F32) for r, w in epi_accs]
    res = pl.pallas_call(
        body, name=name, grid=(m // tm, n // tn, nk),
        in_specs=in_specs, out_specs=out_specs, out_shape=out_shape, input_output_aliases=aliases,
        scratch_shapes=[pltpu.VMEM((tm, tn), F32)] if nk > 1 else [],
        compiler_params=_cparams(("arbitrary",) * 3 if n_acc else ("parallel", "parallel", "arbitrary")),
    )(a, b, *[r for r, _ in epi_rows], *epi_pars, *extra)
    return res if epilogue is not None else res[0]


BAND = 128


def _band_matmul(name, kind, a, b, *, a_blk0=0, out_dtype=F32, epilogue=None, epi_rows=(), epi_pars=(), epi_outs=(),
                 into=None):
    n_rows = a.shape[0]
    blk = 2 * S5_LB
    tm = _pick(n_rows, (1024, 512, 256))
    if epilogue is not None:
        assert kind == "nt"
        n_er, n_ep = len(epi_rows), len(epi_pars)
        in_specs = [pl.BlockSpec((tm, blk), lambda i, j: (i, j)), pl.BlockSpec((BAND, blk), lambda i, j: (j, 0))]
        in_specs += [pl.BlockSpec((tm, BAND), functools.partial(lambda i, j, c: (i, c + j), c=c0)) for _, c0 in epi_rows]
        in_specs += [pl.BlockSpec((1, BAND), lambda i, j: (0, j)) for _ in epi_pars]
        out_specs = [pl.BlockSpec((tm, BAND), lambda i, j: (i, j)) for _ in epi_outs]
        out_shape = [jax.ShapeDtypeStruct((n_rows, S5_NJ * BAND), d) for d in epi_outs]
        extra, aliases = [], {}
        if into is not None:
            buf, c0 = into
            in_specs.append(pl.BlockSpec(memory_space=pl.ANY))
            out_specs[0] = pl.BlockSpec((tm, BAND), functools.partial(lambda i, j, c: (i, c + j), c=c0))
            out_shape[0] = jax.ShapeDtypeStruct(buf.shape, buf.dtype)
            extra, aliases = [buf], {2 + n_er + n_ep: 0}

        def epi_body(a_ref, b_ref, *refs):
            outs = epilogue(_dg(a_ref[...], b_ref[...], 1, 1), [r[...] for r in refs[:n_er]],
                            [p[...] for p in refs[n_er:n_er + n_ep]])
            for o_ref, o in zip(refs[n_er + n_ep + len(extra):], outs, strict=True):
                o_ref[...] = o.astype(o_ref.dtype)

        return pl.pallas_call(
            epi_body, name=name, grid=(n_rows // tm, S5_NJ), in_specs=in_specs, out_specs=out_specs,
            out_shape=out_shape, input_output_aliases=aliases, compiler_params=_cparams(("parallel", "parallel")),
        )(a, b, *[r for r, _ in epi_rows], *epi_pars, *extra)
    if kind == "nn":
        grid = (n_rows // tm, S5_NJ)
        in_specs = [pl.BlockSpec((tm, BAND), lambda i, j: (i, a_blk0 + j)), pl.BlockSpec((BAND, blk), lambda i, j: (j, 0))]
        out_spec = pl.BlockSpec((tm, blk), lambda i, j: (i, j))
        out_shape = (n_rows, S5_NJ * blk)
        sem = ("parallel", "parallel")

        def body(a_ref, b_ref, o_ref):
            o_ref[...] = _dg(a_ref[...], b_ref[...], 1, 0).astype(o_ref.dtype)
    elif kind == "nt":
        grid = (n_rows // tm, S5_NJ)
        in_specs = [pl.BlockSpec((tm, blk), lambda i, j: (i, j)), pl.BlockSpec((BAND, blk), lambda i, j: (j, 0))]
        out_spec = pl.BlockSpec((tm, BAND), lambda i, j: (i, j))
        out_shape = (n_rows, S5_NJ * BAND)
        sem = ("parallel", "parallel")

        def body(a_ref, b_ref, o_ref):
            o_ref[...] = _dg(a_ref[...], b_ref[...], 1, 1).astype(o_ref.dtype)
    else:
        grid = (S5_NJ, n_rows // tm)
        in_specs = [pl.BlockSpec((tm, BAND), lambda j, k: (k, a_blk0 + j)), pl.BlockSpec((tm, blk), lambda j, k: (k, j))]
        out_spec = pl.BlockSpec((BAND, blk), lambda j, k: (j, 0))
        out_shape = (S5_NJ * BAND, blk)
        sem = ("parallel", "arbitrary")

        def body(a_ref, b_ref, o_ref):
            @pl.when(pl.program_id(1) == 0)
            def _():
                o_ref[...] = jnp.zeros_like(o_ref)

            o_ref[...] += _dg(a_ref[...], b_ref[...], 0, 0)

    return pl.pallas_call(
        body, name=name, grid=grid, in_specs=in_specs, out_specs=out_spec,
        out_shape=jax.ShapeDtypeStruct(out_shape, out_dtype), compiler_params=_cparams(sem),
    )(a, b)


def _rowwise(name, fn, n_rows, tr, row_ins, par_ins, row_outs, acc_outs=(), into=None):
    nr, npar, no, na = len(row_ins), len(par_ins), len(row_outs), len(acc_outs)
    in_specs = []
    for arr, off, w in row_ins:
        assert off % w == 0 and arr.shape[0] == n_rows, (name, arr.shape, off, w)
        in_specs.append(pl.BlockSpec((tr, w), functools.partial(lambda i, c: (i, c), c=off // w)))
    for arr, off, w in par_ins:
        assert off % w == 0
        in_specs.append(pl.BlockSpec((arr.shape[0], w), functools.partial(lambda i, c: (0, c), c=off // w)))
    out_specs = [pl.BlockSpec((tr, w), lambda i: (i, 0)) for w, _ in row_outs]
    out_specs += [pl.BlockSpec((r, w), lambda i: (0, 0)) for r, w in acc_outs]
    out_shape = [jax.ShapeDtypeStruct((n_rows, w), dt) for w, dt in row_outs]
    out_shape += [jax.ShapeDtypeStruct((r, w), F32) for r, w in acc_outs]
    extra, aliases = [], {}
    if into is not None:
        buf, off = into
        w0 = row_outs[0][0]
        assert off % w0 == 0 and buf.dtype == row_outs[0][1]
        in_specs.append(pl.BlockSpec(memory_space=pl.ANY))
        out_specs[0] = pl.BlockSpec((tr, w0), functools.partial(lambda i, c: (i, c), c=off // w0))
        out_shape[0] = jax.ShapeDtypeStruct(buf.shape, buf.dtype)
        extra, aliases = [buf], {nr + npar: 0}
    nx = len(extra)

    def body(*refs):
        rows = [r[...] for r in refs[:nr]]
        pars = [r[...] for r in refs[nr:nr + npar]]
        o_refs = refs[nr + npar + nx:nr + npar + nx + no]
        a_refs = refs[nr + npar + nx + no:]
        outs, accs = fn(rows, pars)
        for o_ref, o in zip(o_refs, outs, strict=True):
            o_ref[...] = o.astype(o_ref.dtype)
        if na:
            @pl.when(pl.program_id(0) == 0)
            def _():
                for a_ref in a_refs:
                    a_ref[...] = jnp.zeros_like(a_ref)

            for a_ref, a in zip(a_refs, accs, strict=True):
                a_ref[...] += jnp.broadcast_to(a, a_ref.shape)

    res = pl.pallas_call(
        body, name=name, grid=(n_rows // tr,),
        in_specs=in_specs, out_specs=out_specs, out_shape=out_shape, input_output_aliases=aliases,
        compiler_params=_cparams(("arbitrary",) if na else ("parallel",)),
    )(*[a for a, _, _ in row_ins], *[a for a, _, _ in par_ins], *extra)
    return res


def _rms(x, w):
    return x * lax.rsqrt(jnp.mean(x * x, axis=-1, keepdims=True) + RMS_EPS) * w


def _gated_norm(y, z, w):
    outs = []
    for g in range(SSD_GROUPS):
        sl = slice(g * GROUP_W, (g + 1) * GROUP_W)
        yg = y[:, sl] * jax.nn.silu(z[:, sl])
        outs.append(yg * lax.rsqrt(jnp.mean(yg * yg, axis=-1, keepdims=True) + RMS_EPS) * w[:, sl])
    return jnp.concatenate(outs, axis=-1)


def _s5_out(yc, u, z, t, d, bg):
    ge = jax.nn.gelu(yc + d * u)
    return ge * jax.nn.sigmoid(t + bg) * jax.nn.silu(z)


def _merge(g5, gs, m5, ms):
    return jax.nn.sigmoid(g5) * m5 + jax.nn.sigmoid(gs) * ms


def _head_loss(h1, pgl, pp, fw, tgt):
    h2 = h1 + jax.nn.sigmoid(pgl) * pp
    err = _rms(h2, fw) - tgt
    per_row = 0.5 * jnp.mean(err * err, axis=-1, keepdims=True)
    return jnp.sum(per_row, axis=0, keepdims=True)


def _s5_disc(a_re, a_im, log_step, b_re2, b_im2, expand):
    step = jnp.exp(log_step)
    mag = jnp.exp(a_re * step)
    lb_re = mag * jnp.cos(a_im * step)
    lb_im = mag * jnp.sin(a_im * step)
    den = a_re * a_re + a_im * a_im
    n_re = lb_re - 1.0
    f_re = (n_re * a_re + lb_im * a_im) / den
    f_im = (lb_im * a_re - n_re * a_im) / den
    hi = lax.Precision.HIGHEST
    fr = jnp.dot(expand, f_re, precision=hi, preferred_element_type=F32)
    fi = jnp.dot(expand, f_im, precision=hi, preferred_element_type=F32)
    return lb_re, lb_im, fr * b_re2 - fi * b_im2, fr * b_im2 + fi * b_re2


def _s5_params_fwd(a_re, a_im, log_step, b_re2, b_im2, expand):
    gp = a_re.shape

    def body(ar, ai, ls, br, bi, ex, pr_ref, pi_ref, bbr_ref, bbi_ref):
        lr, li, bbr, bbi = _s5_disc(ar[...], ai[...], ls[...], br[...], bi[...], ex[...])
        bbr_ref[...] = bbr
        bbi_ref[...] = bbi
        qr, qi = lr, li
        for k in range(S5_LOG_TB):
            pr_ref[k] = qr
            pi_ref[k] = qi
            qr, qi = qr * lr - qi * li, qr * li + qi * lr

    return pl.pallas_call(
        body, name="s5_params_fwd",
        out_shape=(jax.ShapeDtypeStruct((S5_LOG_TB,) + gp, F32), jax.ShapeDtypeStruct((S5_LOG_TB,) + gp, F32),
                   jax.ShapeDtypeStruct(b_re2.shape, F32), jax.ShapeDtypeStruct(b_re2.shape, F32)),
    )(a_re, a_im, log_step, b_re2, b_im2, expand)


def _s5_params_bwd(a_re, a_im, log_step, b_re2, b_im2, expand, d_lr, d_li, d_bbr, d_bbi):
    def body(ar, ai, ls, br, bi, ex, glr, gli, gbr, gbi, dar, dai, dls, dbr, dbi):
        _, vjp = jax.vjp(lambda *p: _s5_disc(*p, ex[...]), ar[...], ai[...], ls[...], br[...], bi[...])
        g = vjp((glr[...], gli[...], gbr[...], gbi[...]))
        for ref, val in zip((dar, dai, dls, dbr, dbi), g, strict=True):
            ref[...] = val

    return pl.pallas_call(
        body, name="s5_params_bwd",
        out_shape=tuple(jax.ShapeDtypeStruct(v.shape, F32) for v in (a_re, a_im, log_step, b_re2, b_im2)),
    )(a_re, a_im, log_step, b_re2, b_im2, expand, d_lr, d_li, d_bbr, d_bbi)


def _scan_block(x_ref, out_ref, lp, edge_pow, cr, ci, reverse, each=None):
    n_g = x_ref.shape[0] // 8
    sub = lax.broadcasted_iota(jnp.int32, (8, S5_LB), 0)
    steps = []
    for sh in (1, 2, 4):
        keep = (sub < 8 - sh) if reverse else (sub >= sh)
        steps.append((8 - sh if reverse else sh, jnp.where(keep, lp[sh - 1:sh, :S5_LB], 0.0),
                      jnp.where(keep, lp[sh - 1:sh, S5_LB:], 0.0)))
    e_r, e_i = edge_pow[:, :S5_LB], edge_pow[:, S5_LB:]
    for r in (range(n_g - 1, -1, -1) if reverse else range(n_g)):
        rows = slice(8 * r, 8 * r + 8)
        xr, xi = x_ref[rows, :S5_LB], x_ref[rows, S5_LB:]
        for by, a_r, a_i in steps:
            pr, pi = pltpu.roll(xr, by, 0), pltpu.roll(xi, by, 0)
            xr, xi = xr + a_r * pr - a_i * pi, xi + a_r * pi + a_i * pr
        xr = xr + e_r * cr - e_i * ci
        xi = xi + e_r * ci + e_i * cr
        out_ref[rows, :S5_LB] = xr
        out_ref[rows, S5_LB:] = xi
        if each is not None:
            each(r, xr, xi)
        cr, ci = (xr[0:1, :], xi[0:1, :]) if reverse else (xr[7:8, :], xi[7:8, :])
    return cr, ci


def _s5_fwd(proj, bb_band, c_band, lam_pow, s5_d, n_seq, seq_len):
    n_t = seq_len // S5_TB
    blk = 2 * S5_LB
    n_rows = n_seq * seq_len
    u_blk0 = OFF_U5 // BAND

    def body(u_ref, bb_ref, cb_ref, lp_ref, d_ref, s_ref, y_ref, ge_ref, cr, ci, buf):
        @pl.when(pl.program_id(2) == 0)
        def _():
            cr[...] = jnp.zeros_like(cr)
            ci[...] = jnp.zeros_like(ci)

        u = u_ref[...]
        buf[...] = _dg(u, bb_ref[...], 1, 0)
        lp = lp_ref[...]
        cr[...], ci[...] = _scan_block(buf, buf, lp, lp, cr[...], ci[...], False)
        s = buf[...].astype(s_ref.dtype)
        s_ref[...] = s
        y = _dg(s, cb_ref[...], 1, 1)
        y_ref[...] = y
        ge_ref[...] = jax.nn.gelu(y + d_ref[...] * u).astype(ge_ref.dtype)

    def rows(j, b, t):
        return (b * n_t + t, j)

    return pl.pallas_call(
        body, name="s5_fwd", grid=(S5_NJ, n_seq, n_t),
        in_specs=[pl.BlockSpec((S5_TB, BAND), lambda j, b, t: (b * n_t + t, u_blk0 + j)),
                  pl.BlockSpec((BAND, blk), lambda j, b, t: (j, 0)), pl.BlockSpec((BAND, blk), lambda j, b, t: (j, 0)),
                  pl.BlockSpec((S5_LOG_TB, blk), lambda j, b, t: (0, j)), pl.BlockSpec((1, BAND), lambda j, b, t: (0, j))],
        out_specs=[pl.BlockSpec((S5_TB, blk), rows), pl.BlockSpec((S5_TB, BAND), rows), pl.BlockSpec((S5_TB, BAND), rows)],
        out_shape=[jax.ShapeDtypeStruct((n_rows, S5_NJ * blk), BF16), jax.ShapeDtypeStruct((n_rows, S5_WIDTH), F32),
                   jax.ShapeDtypeStruct((n_rows, S5_WIDTH), BF16)],
        scratch_shapes=[pltpu.VMEM((1, S5_LB), F32), pltpu.VMEM((1, S5_LB), F32), pltpu.VMEM((S5_TB, blk), F32)],
        compiler_params=_cparams(("parallel", "parallel", "arbitrary")),
    )(proj, bb_band, c_band, lam_pow, s5_d)


def _s5_bwd(dyc, s, proj, du5_a, bb_band, c_band, lam_pow_conj, dproj, n_seq, seq_len):
    n_t = seq_len // S5_TB
    blk = 2 * S5_LB
    halo_rows = 16
    halo_per_blk = S5_TB // halo_rows
    u_blk0 = OFF_U5 // BAND

    def rows(j, b, t):
        return (b * n_t + (n_t - 1 - t), j)

    def u_rows(j, b, t):
        return (b * n_t + (n_t - 1 - t), u_blk0 + j)

    def halo(j, b, t):
        return (jnp.maximum((b * n_t + (n_t - 1 - t)) * halo_per_blk - 1, 0), j)

    def body(dy_ref, sq_ref, hq_ref, u_ref, dua_ref, bb_ref, cb_ref, lp_ref, _, du_ref, dl_ref, dbb_ref, dcb_ref,
             cr, ci, buf, s_ref):
        b, t = pl.program_id(1), pl.program_id(2)
        sq = sq_ref[...]
        s_ref[...] = sq.astype(F32)
        h_last = hq_ref[...].astype(F32)[halo_rows - 1:halo_rows, :]
        dy = dy_ref[...]
        buf[...] = _dg(dy, cb_ref[...], 1, 0)

        @pl.when(t == 0)
        def _():
            cr[...] = jnp.zeros_like(cr)
            ci[...] = jnp.zeros_like(ci)

        @pl.when((b == 0) & (t == 0))
        def _():
            dl_ref[...] = jnp.zeros_like(dl_ref)
            dbb_ref[...] = jnp.zeros_like(dbb_ref)
            dcb_ref[...] = jnp.zeros_like(dcb_ref)

        dcb_ref[...] += _dg(dy, sq, 0, 0)

        first_blk = t == n_t - 1
        sub = lax.broadcasted_iota(jnp.int32, (8, S5_LB), 0)
        acc = [jnp.zeros((8, S5_LB), F32), jnp.zeros((8, S5_LB), F32)]

        def each(r, gr, gi):
            rows = slice(8 * r, 8 * r + 8)
            if r == 0:
                before_r = jnp.where(first_blk, 0.0, h_last[:, :S5_LB])
                before_i = jnp.where(first_blk, 0.0, h_last[:, S5_LB:])
            else:
                before_r, before_i = s_ref[8 * r - 1:8 * r, :S5_LB], s_ref[8 * r - 1:8 * r, S5_LB:]
            sp_r = jnp.where(sub == 0, before_r, pltpu.roll(s_ref[rows, :S5_LB], 1, 0))
            sp_i = jnp.where(sub == 0, before_i, pltpu.roll(s_ref[rows, S5_LB:], 1, 0))
            acc[0] = acc[0] + gr * sp_r + gi * sp_i
            acc[1] = acc[1] + gi * sp_r - gr * sp_i

        lp = lp_ref[...]
        edge_pow = jnp.concatenate([lp[7 - i:8 - i, :] for i in range(8)], axis=0)
        cr[...], ci[...] = _scan_block(buf, buf, lp, edge_pow, cr[...], ci[...], True, each)
        g = buf[...].astype(BF16)
        du_ref[...] = (dua_ref[...] + _dg(g, bb_ref[...], 1, 1)).astype(du_ref.dtype)
        dbb_ref[...] += _dg(u_ref[...], g, 0, 0)
        dl_ref[:, :S5_LB] += jnp.sum(acc[0], axis=0, keepdims=True)
        dl_ref[:, S5_LB:] += jnp.sum(acc[1], axis=0, keepdims=True)

    band = pl.BlockSpec((BAND, blk), lambda j, b, t: (j, 0))
    return pl.pallas_call(
        body, name="s5_bwd", grid=(S5_NJ, n_seq, n_t),
        in_specs=[pl.BlockSpec((S5_TB, BAND), rows), pl.BlockSpec((S5_TB, blk), rows),
                  pl.BlockSpec((halo_rows, blk), halo), pl.BlockSpec((S5_TB, BAND), u_rows),
                  pl.BlockSpec((S5_TB, BAND), rows), band, band,
                  pl.BlockSpec((S5_LOG_TB, blk), lambda j, b, t: (0, j)), pl.BlockSpec(memory_space=pl.ANY)],
        out_specs=[pl.BlockSpec((S5_TB, BAND), u_rows), pl.BlockSpec((1, blk), lambda j, b, t: (0, j)), band, band],
        out_shape=[jax.ShapeDtypeStruct(dproj.shape, dproj.dtype), jax.ShapeDtypeStruct((1, S5_NJ * blk), F32),
                   jax.ShapeDtypeStruct(bb_band.shape, F32), jax.ShapeDtypeStruct(c_band.shape, F32)],
        input_output_aliases={8: 0},
        scratch_shapes=[pltpu.VMEM((1, S5_LB), F32), pltpu.VMEM((1, S5_LB), F32), pltpu.VMEM((S5_TB, blk), F32),
                        pltpu.VMEM((S5_TB, blk), F32)],
        compiler_params=_cparams(("parallel", "arbitrary", "arbitrary")),
    )(dyc, s, s, proj, du5_a, bb_band, c_band, lam_pow_conj, dproj)


CONV_TR = 512
CONV_CW = 512


def _shift_down(x, halo, k):
    if k == 0:
        return x
    row8 = lax.broadcasted_iota(jnp.int32, halo.shape, 0)
    rolled = pltpu.roll(x, k, 0)
    top = jnp.where(row8 < k, pltpu.roll(halo, k, 0), rolled[:8])
    if x.shape[0] == 8:
        return top
    return jnp.concatenate([top, rolled[8:]], axis=0)


def _shift_up(x, halo, k):
    if k == 0:
        return x
    n = x.shape[0]
    row8 = lax.broadcasted_iota(jnp.int32, halo.shape, 0)
    rolled = pltpu.roll(x, n - k, 0)
    bot = jnp.where(row8 >= 8 - k, pltpu.roll(halo, 8 - k, 0), rolled[n - 8:])
    if n == 8:
        return bot
    return jnp.concatenate([rolled[:n - 8], bot], axis=0)


def _conv_pre(x, halo, w, b):
    acc = b + w[SSD_CONV - 1:SSD_CONV, :] * x
    for k in range(SSD_CONV - 1):
        acc = acc + w[k:k + 1, :] * _shift_down(x, halo, SSD_CONV - 1 - k)
    return acc


def _conv_specs(seq_len, col_off):
    lt = seq_len // CONV_TR
    cb = col_off // CONV_CW
    cur = pl.BlockSpec((CONV_TR, CONV_CW), lambda j, i: (i, j + cb))
    prev = pl.BlockSpec((8, CONV_CW), lambda j, i: (jnp.maximum(i * (CONV_TR // 8) - 1, 0), j + cb))
    return lt, cur, prev


def _conv_fwd(proj, conv_w, conv_b, n_rows, seq_len):
    lt, cur, prev = _conv_specs(seq_len, OFF_XBC)

    def body(x_ref, h_ref, w_ref, b_ref, o_ref):
        halo = jnp.where(pl.program_id(1) % lt == 0, 0.0, h_ref[...])
        o_ref[...] = jax.nn.silu(_conv_pre(x_ref[...], halo, w_ref[...], b_ref[...]))

    return pl.pallas_call(
        body, name="ssd_conv_fwd", grid=(SSD_CONV_DIM // CONV_CW, n_rows // CONV_TR),
        in_specs=[cur, prev, pl.BlockSpec((SSD_CONV, CONV_CW), lambda j, i: (0, j)),
                  pl.BlockSpec((1, CONV_CW), lambda j, i: (0, j))],
        out_specs=pl.BlockSpec((CONV_TR, CONV_CW), lambda j, i: (i, j)),
        out_shape=jax.ShapeDtypeStruct((n_rows, SSD_CONV_DIM), F32),
        compiler_params=_cparams(("parallel", "parallel")),
    )(proj, proj, conv_w, conv_b)


def _conv_bwd(name, proj, d_act, conv_w, conv_b, n_rows, seq_len, col_off, dproj):
    width = d_act.shape[1]
    lt, cur, prev = _conv_specs(seq_len, OFF_XBC + col_off)
    n_blk = n_rows // CONV_TR
    cb = (OFF_XBC + col_off) // CONV_CW
    pb = col_off // CONV_CW
    nxt = pl.BlockSpec((8, CONV_CW), lambda j, i: (jnp.minimum((i + 1) * (CONV_TR // 8), n_rows // 8 - 1), j + cb))
    d_cur = pl.BlockSpec((CONV_TR, CONV_CW), lambda j, i: (i, j))
    d_nxt = pl.BlockSpec((8, CONV_CW), lambda j, i: (jnp.minimum((i + 1) * (CONV_TR // 8), n_rows // 8 - 1), j))

    def dsilu(pre):
        sg = jax.nn.sigmoid(pre)
        return sg * (1.0 + pre * (1.0 - sg))

    def body(x_ref, hp_ref, hn_ref, d_ref, dn_ref, w_ref, b_ref, _, dx_ref, dw_ref, db_ref):
        i = pl.program_id(1)
        x, w, b = x_ref[...], w_ref[...], b_ref[...]
        halo_p = jnp.where(i % lt == 0, 0.0, hp_ref[...])
        at_end = i % lt == lt - 1
        dpre = d_ref[...] * dsilu(_conv_pre(x, halo_p, w, b))
        pre_n = _conv_pre(hn_ref[...], x[CONV_TR - 8:, :], w, b)
        dpre_n = jnp.where(at_end, 0.0, dn_ref[...] * dsilu(pre_n))
        dx = w[SSD_CONV - 1:SSD_CONV, :] * dpre
        for k in range(SSD_CONV - 1):
            dx = dx + w[k:k + 1, :] * _shift_up(dpre, dpre_n, SSD_CONV - 1 - k)
        dx_ref[...] = dx.astype(dx_ref.dtype)

        @pl.when(i == 0)
        def _():
            dw_ref[...] = jnp.zeros_like(dw_ref)
            db_ref[...] = jnp.zeros_like(db_ref)

        for k in range(SSD_CONV):
            xs = _shift_down(x, halo_p, SSD_CONV - 1 - k)
            dw_ref[k:k + 1, :] += jnp.sum(dpre * xs, axis=0, keepdims=True)
        db_ref[...] += jnp.sum(dpre, axis=0, keepdims=True)

    return pl.pallas_call(
        body, name=name, grid=(width // CONV_CW, n_blk),
        in_specs=[cur, prev, nxt, d_cur, d_nxt,
                  pl.BlockSpec((SSD_CONV, CONV_CW), lambda j, i: (0, j + pb)),
                  pl.BlockSpec((1, CONV_CW), lambda j, i: (0, j + pb)), pl.BlockSpec(memory_space=pl.ANY)],
        out_specs=[pl.BlockSpec((CONV_TR, CONV_CW), lambda j, i: (i, j + cb)),
                   pl.BlockSpec((SSD_CONV, CONV_CW), lambda j, i: (0, j)),
                   pl.BlockSpec((1, CONV_CW), lambda j, i: (0, j))],
        out_shape=[jax.ShapeDtypeStruct(dproj.shape, dproj.dtype),
                   jax.ShapeDtypeStruct((SSD_CONV, width), F32), jax.ShapeDtypeStruct((1, width), F32)],
        input_output_aliases={7: 0},
        compiler_params=_cparams(("parallel", "arbitrary")),
    )(proj, proj, proj, d_act, d_act, conv_w, conv_b, dproj)


def _split3(x):
    hi = x.astype(BF16)
    r = x - hi.astype(F32)
    mid = r.astype(BF16)
    return hi, mid, (r - mid.astype(F32)).astype(BF16)


def _sel_dot(a, b, a_is_sel):
    dn = (((1,), (0,)), ((), ()))
    if a_is_sel:
        return sum(lax.dot_general(a, t, dn, preferred_element_type=F32) for t in _split3(b))
    return sum(lax.dot_general(t, b, dn, preferred_element_type=F32) for t in _split3(a))


@jax.custom_vjp
def sel_left(sel, sel_t, x):
    return _sel_dot(sel, x, True)


@jax.custom_vjp
def sel_right(x, sel, sel_t):
    return _sel_dot(x, sel, False)


sel_left.defvjp(lambda s, st, x: (_sel_dot(s, x, True), (s, st)),
                lambda r, g: (jnp.zeros_like(r[0]), jnp.zeros_like(r[1]), _sel_dot(r[1], g, True)))
sel_right.defvjp(lambda x, s, st: (_sel_dot(x, s, False), (s, st)),
                 lambda r, g: (_sel_dot(g, r[1], False), jnp.zeros_like(r[0]), jnp.zeros_like(r[1])))


def _ssd_chunk(xs, bm, cm, dtr, st, dtb, alog, dsk, k):
    dt = jax.nn.softplus(dtr + dtb)
    acum = sel_left(k["tri"], k["tri_t"], dt * (-jnp.exp(alog)))
    dt_e = sel_right(dt, k["spread"], k["spread_t"])
    ac_e = sel_right(acum, k["spread"], k["spread_t"])
    al_e = ac_e[SSD_CHUNK - 1:SSD_CHUNK, :]
    dsk_e = sel_right(jnp.broadcast_to(dsk, (8, 128)), k["spread"], k["spread_t"])[0:1, :]
    xdt = xs * dt_e
    acum_t = acum.T
    scores = dot_nt(cm, bm)
    y = dot_nn(cm, st) * jnp.exp(ac_e) + xs * dsk_e
    for j in range(SSD_HPG):
        lmat = jnp.exp(jnp.where(k["causal"], acum[:, j:j + 1] - acum_t[j:j + 1, :], -jnp.inf))
        y = y + dot_nn(scores * lmat, jnp.where(k["head"] == j, xdt, 0.0))
    new = st * jnp.exp(al_e) + dot_tn(bm, xdt * jnp.exp(al_e - ac_e))
    return y, new


def _ssd_consts():
    r = lax.broadcasted_iota(jnp.int32, (SSD_CHUNK, SSD_CHUNK), 0)
    c = lax.broadcasted_iota(jnp.int32, (SSD_CHUNK, SSD_CHUNK), 1)
    hd = jnp.int32(SSD_HEADDIM)
    sr = lax.broadcasted_iota(jnp.int32, (128, GROUP_W), 0)
    sc = lax.div(lax.broadcasted_iota(jnp.int32, (128, GROUP_W), 1), hd)
    tr = lax.div(lax.broadcasted_iota(jnp.int32, (GROUP_W, 128), 0), hd)
    tc = lax.broadcasted_iota(jnp.int32, (GROUP_W, 128), 1)
    return {"tri": (r >= c).astype(BF16), "tri_t": (c >= r).astype(BF16), "causal": r >= c,
            "spread": (sr == sc).astype(BF16), "spread_t": (tr == tc).astype(BF16),
            "head": lax.div(lax.broadcasted_iota(jnp.int32, (SSD_CHUNK, GROUP_W), 1), hd)}


def _ssd_specs(n_c, reverse):
    def cidx(c):
        return n_c - 1 - c if reverse else c

    xs = pl.BlockSpec((SSD_CHUNK, GROUP_W), lambda g, b, c: (b * n_c + cidx(c), g))
    bm = pl.BlockSpec((SSD_CHUNK, SSD_STATE), lambda g, b, c: (b * n_c + cidx(c), SSD_WIDTH // SSD_STATE + g))
    cm = pl.BlockSpec((SSD_CHUNK, SSD_STATE), lambda g, b, c: (b * n_c + cidx(c), (SSD_WIDTH + SSD_BC) // SSD_STATE + g))
    dt = pl.BlockSpec((SSD_CHUNK, 128), lambda g, b, c: (b * n_c + cidx(c), OFF_DT // 128 + g))
    par = pl.BlockSpec((1, 128), lambda g, b, c: (0, g))
    st = pl.BlockSpec((1, 1, 1, SSD_STATE, GROUP_W), lambda g, b, c: (b, cidx(c), g, 0, 0))
    return xs, bm, cm, dt, par, st


def _ssd_chunk_gated(xs, bm, cm, dtr, st, dtb, alog, dsk, z, nw, k):
    y, new = _ssd_chunk(xs, bm, cm, dtr, st, dtb, alog, dsk, k)
    yg = y * jax.nn.silu(z)
    return yg * lax.rsqrt(jnp.mean(yg * yg, axis=-1, keepdims=True) + RMS_EPS) * nw, new


def _ssd_gate_specs(n_c, reverse):
    def cidx(c):
        return n_c - 1 - c if reverse else c

    z = pl.BlockSpec((SSD_CHUNK, GROUP_W), lambda g, b, c: (b * n_c + cidx(c), OFF_ZS // GROUP_W + g))
    nw = pl.BlockSpec((1, GROUP_W), lambda g, b, c: (0, g))
    return z, nw


def _ssd_fwd(xbc_act, proj, dtb, alog, dsk, norm_w, n_seq, seq_len):
    n_c = seq_len // SSD_CHUNK
    xs_s, bm_s, cm_s, dt_s, par_s, st_s = _ssd_specs(n_c, False)
    z_s, nw_s = _ssd_gate_specs(n_c, False)

    def body(xs_ref, bm_ref, cm_ref, dt_ref, dtb_ref, al_ref, dk_ref, z_ref, nw_ref, y_ref, st_ref, state):
        @pl.when(pl.program_id(2) == 0)
        def _():
            state[...] = jnp.zeros_like(state)

        prev = state[...]
        st_ref[0, 0, 0] = prev
        y, new = _ssd_chunk_gated(xs_ref[...], bm_ref[...], cm_ref[...], dt_ref[...], prev,
                                  dtb_ref[...], al_ref[...], dk_ref[...], z_ref[...], nw_ref[...], _ssd_consts())
        y_ref[...] = y.astype(y_ref.dtype)
        state[...] = new

    return pl.pallas_call(
        body, name="ssd_fwd", grid=(SSD_GROUPS, n_seq, n_c),
        in_specs=[xs_s, bm_s, cm_s, dt_s, par_s, par_s, par_s, z_s, nw_s],
        out_specs=[pl.BlockSpec((SSD_CHUNK, GROUP_W), lambda g, b, c: (b * n_c + c, g)), st_s],
        out_shape=[jax.ShapeDtypeStruct((n_seq * seq_len, SSD_WIDTH), BF16),
                   jax.ShapeDtypeStruct((n_seq, n_c, SSD_GROUPS, SSD_STATE, GROUP_W), F32)],
        scratch_shapes=[pltpu.VMEM((SSD_STATE, GROUP_W), F32)],
        compiler_params=_cparams(("parallel", "parallel", "arbitrary")),
    )(xbc_act, xbc_act, xbc_act, proj, dtb, alog, dsk, proj, norm_w)


def _ssd_bwd(xbc_act, proj, states, dy, dtb, alog, dsk, norm_w, n_seq, seq_len, dproj):
    n_c = seq_len // SSD_CHUNK
    n_rows = n_seq * seq_len
    xs_s, bm_s, cm_s, dt_s, par_s, st_s = _ssd_specs(n_c, True)
    z_s, nw_s = _ssd_gate_specs(n_c, True)

    def rows(w, first=0):
        return pl.BlockSpec((SSD_CHUNK, w), lambda g, b, c: (b * n_c + (n_c - 1 - c), first + g))

    def body(xs_ref, bm_ref, cm_ref, dt_ref, st_ref, dy_ref, dtb_ref, al_ref, dk_ref, z_ref, nw_ref, _,
             dxs_ref, dbm_ref, dcm_ref, dz_ref, ddt_ref, ddtb_ref, dal_ref, ddk_ref, dnw_ref, dstate):
        b, c = pl.program_id(1), pl.program_id(2)

        @pl.when(c == 0)
        def _():
            dstate[...] = jnp.zeros_like(dstate)

        @pl.when((b == 0) & (c == 0))
        def _():
            for ref in (ddtb_ref, dal_ref, ddk_ref, dnw_ref):
                ref[...] = jnp.zeros_like(ref)

        consts = _ssd_consts()
        _, vjp = jax.vjp(
            lambda *a: _ssd_chunk_gated(*a, consts),
            xs_ref[...], bm_ref[...], cm_ref[...], dt_ref[...], st_ref[0, 0, 0], dtb_ref[...], al_ref[...], dk_ref[...],
            z_ref[...], nw_ref[...])
        dxs, dbm, dcm, ddtr, dprev, ddtb, dal, ddk, dz, dnw = vjp((dy_ref[...], dstate[...]))
        dxs_ref[...] = dxs
        dbm_ref[...] = dbm
        dcm_ref[...] = dcm
        dz_ref[...] = dz.astype(dz_ref.dtype)
        ddt_ref[...] = ddtr.astype(ddt_ref.dtype)
        ddtb_ref[...] += ddtb
        dal_ref[...] += dal
        ddk_ref[...] += ddk
        dnw_ref[...] += dnw
        dstate[...] = dprev

    acc = pl.BlockSpec((1, 128), lambda g, b, c: (0, g))
    return pl.pallas_call(
        body, name="ssd_bwd", grid=(SSD_GROUPS, n_seq, n_c),
        in_specs=[xs_s, bm_s, cm_s, dt_s, st_s, rows(GROUP_W), par_s, par_s, par_s, z_s, nw_s,
                  pl.BlockSpec(memory_space=pl.ANY)],
        out_specs=[rows(GROUP_W), rows(SSD_STATE), rows(SSD_STATE), rows(GROUP_W, OFF_ZS // GROUP_W), rows(128),
                   acc, acc, acc, pl.BlockSpec((1, GROUP_W), lambda g, b, c: (0, g))],
        out_shape=[jax.ShapeDtypeStruct((n_rows, SSD_WIDTH), F32), jax.ShapeDtypeStruct((n_rows, SSD_BC), F32),
                   jax.ShapeDtypeStruct((n_rows, SSD_BC), F32), jax.ShapeDtypeStruct(dproj.shape, dproj.dtype),
                   jax.ShapeDtypeStruct((n_rows, DT_W), BF16),
                   jax.ShapeDtypeStruct((1, 512), F32), jax.ShapeDtypeStruct((1, 512), F32),
                   jax.ShapeDtypeStruct((1, 512), F32), jax.ShapeDtypeStruct((1, SSD_WIDTH), F32)],
        input_output_aliases={11: 3},
        scratch_shapes=[pltpu.VMEM((SSD_STATE, GROUP_W), F32)],
        compiler_params=_cparams(("parallel", "arbitrary", "arbitrary")),
    )(xbc_act, xbc_act, xbc_act, proj, states, dy, dtb, alog, dsk, proj, norm_w, dproj)


def _pad_heads(v):
    return jnp.pad(v.reshape(SSD_GROUPS, SSD_HPG), ((0, 0), (0, 128 - SSD_HPG))).reshape(1, SSD_GROUPS * 128)


def _unpad_heads(v):
    return v.reshape(SSD_GROUPS, 128)[:, :SSD_HPG].reshape(1, SSD_HEADS)


def _state_cols(v):
    re, im = v
    lead = re.shape[:-1]
    re = re.reshape(lead + (S5_NJ, 1, S5_LB))
    im = im.reshape(lead + (S5_NJ, 1, S5_LB))
    return jnp.concatenate([re, im], axis=-2).reshape(lead + (2 * S5_N,))


def _state_uncols(v):
    lead = v.shape[:-1]
    v = v.reshape(lead + (S5_NJ, 2, S5_LB))
    return v[..., 0, :].reshape(lead + (S5_N,)), v[..., 1, :].reshape(lead + (S5_N,))


GROUPS_PER_BAND = BAND // S5_GROUP


def _band(w2_re, w2_im):
    gh = S5_GROUPS * S5_GROUP
    rg = ((jnp.arange(gh) // S5_GROUP) % GROUPS_PER_BAND)[:, None, None]
    cg = jnp.arange(GROUPS_PER_BAND)[None, :, None]
    parts = [jnp.where(rg == cg, v[:, None, :], 0.0).reshape(gh, S5_LB) for v in (w2_re, w2_im)]
    return jnp.concatenate(parts, axis=1)


def _band_take(wb):
    gh = S5_GROUPS * S5_GROUP
    w4 = wb.reshape(gh, 2, GROUPS_PER_BAND, S5_STATE)
    sel = w4[jnp.arange(gh), :, (jnp.arange(gh) // S5_GROUP) % GROUPS_PER_BAND, :]
    return sel[:, 0, :], sel[:, 1, :]


W_IN_SHARD = IN_PROJ_DIM // N_CHIPS
W_IN_SEGS = ((0, 512, OFF_U5), (512, 1024, OFF_Z5), (1024, 2560, OFF_ZS), (2560, 5120, OFF_XBC), (5144, 7192, OFF_G5))
DT_ROWS = (5120, 5144)


def _w_in_pieces():
    runs = []
    segs = list(W_IN_SEGS) + [(DT_ROWS[0] + SSD_HPG * g, DT_ROWS[0] + SSD_HPG * (g + 1), OFF_DT + 128 * g)
                              for g in range(SSD_GROUPS)]
    for lo, hi, off in segs:
        for j in range(N_CHIPS):
            s, e = max(lo, j * W_IN_SHARD), min(hi, (j + 1) * W_IN_SHARD)
            if s < e:
                runs.append((j, s - j * W_IN_SHARD, off + s - lo, e - s))
    return runs


RELAYOUT_LANES = 256


def _pad_w_in_t(a4):
    runs = _w_in_pieces()

    def body(a_ref, o_ref):
        o_ref[pl.ds(OFF_DT, DT_W), :] = jnp.zeros((DT_W, RELAYOUT_LANES), o_ref.dtype)
        for j, src, dst, n in runs:
            o_ref[pl.ds(dst, n), :] = a_ref[j, pl.ds(src, n), :]

    return pl.pallas_call(
        body, name="w_in_to_padded", grid=(D_MODEL // RELAYOUT_LANES,),
        in_specs=[pl.BlockSpec((N_CHIPS, W_IN_SHARD, RELAYOUT_LANES), lambda i: (0, 0, i))],
        out_specs=pl.BlockSpec((PROJ_W, RELAYOUT_LANES), lambda i: (0, i)),
        out_shape=jax.ShapeDtypeStruct((PROJ_W, D_MODEL), a4.dtype),
        compiler_params=_cparams(("parallel",)),
    )(a4)


def _unpad_w_in_t(wp):
    runs = _w_in_pieces()

    def body(p_ref, o_ref):
        for j, dst, src, n in runs:
            o_ref[j, pl.ds(dst, n), :] = p_ref[pl.ds(src, n), :]

    return pl.pallas_call(
        body, name="w_in_from_padded", grid=(D_MODEL // RELAYOUT_LANES,),
        in_specs=[pl.BlockSpec((PROJ_W, RELAYOUT_LANES), lambda i: (0, i))],
        out_specs=pl.BlockSpec((N_CHIPS, W_IN_SHARD, RELAYOUT_LANES), lambda i: (0, 0, i)),
        out_shape=jax.ShapeDtypeStruct((N_CHIPS, W_IN_SHARD, D_MODEL), wp.dtype),
        compiler_params=_cparams(("parallel",)),
    )(wp)


def _local_step(x, p, tgt, w):
    n_seq, seq_len, _ = x.shape
    n_rows = n_seq * seq_len
    tr = 512
    x2 = x.reshape(n_rows, D_MODEL)
    p2 = p.reshape(n_rows, PLE_DIM)
    t2 = tgt.reshape(n_rows, D_MODEL)
    row = functools.partial(_rowwise, n_rows=n_rows, tr=tr)

    w_pad_t = _pad_w_in_t(w["w_in_t"])
    norm_w = w["norm_w"].reshape(1, D_MODEL)
    ple_norm_w = w["ple_norm_w"].reshape(1, D_MODEL)
    final_w = w["final_norm_w"].reshape(1, D_MODEL)
    s5_d = w["s5_d"].reshape(1, S5_WIDTH)
    b_glu = w["s5_b_glu"].reshape(1, S5_WIDTH)
    conv_w = w["ssd_conv_w"].reshape(SSD_CONV, SSD_CONV_DIM)
    conv_b = w["ssd_conv_b"].reshape(1, SSD_CONV_DIM)
    ssd_norm_w = w["ssd_norm_w"].reshape(1, SSD_WIDTH)
    dtb, alog, dsk = (_pad_heads(w[k].reshape(1, SSD_HEADS)) for k in ("ssd_dt_bias", "ssd_a_log", "ssd_d"))

    gh = S5_GROUPS * S5_GROUP
    a_re = w["s5_a_re"].reshape(S5_GROUPS, S5_STATE)
    a_im = w["s5_a_im"].reshape(S5_GROUPS, S5_STATE)
    log_step = w["s5_log_step"].reshape(S5_GROUPS, 1)
    b_re2 = jnp.transpose(w["s5_b_re"].reshape(S5_GROUPS, S5_STATE, S5_GROUP), (0, 2, 1)).reshape(gh, S5_STATE)
    b_im2 = jnp.transpose(w["s5_b_im"].reshape(S5_GROUPS, S5_STATE, S5_GROUP), (0, 2, 1)).reshape(gh, S5_STATE)
    expand = (jnp.arange(gh)[:, None] // S5_GROUP == jnp.arange(S5_GROUPS)[None, :]).astype(F32)
    pow_re, pow_im, bb_re2, bb_im2 = _s5_params_fwd(a_re, a_im, log_step, b_re2, b_im2, expand)
    lam_pow = _state_cols((pow_re.reshape(S5_LOG_TB, S5_N), pow_im.reshape(S5_LOG_TB, S5_N)))
    lam_pow_conj = _state_cols((pow_re.reshape(S5_LOG_TB, S5_N), -pow_im.reshape(S5_LOG_TB, S5_N)))
    bb_band = _band(bb_re2, bb_im2).astype(BF16)
    c_band = _band(w["s5_c_re"].reshape(gh, S5_STATE), -w["s5_c_im"].reshape(gh, S5_STATE)).astype(BF16)

    late = w.get("_late")
    (hn,) = row("rms_in", lambda r, q: ([_rms(r[0], q[0])], []), row_ins=[(x2, 0, D_MODEL)],
                par_ins=[(norm_w, 0, D_MODEL)] + ([(late[0], 0, 128)] if late else []), row_outs=[(D_MODEL, BF16)])
    proj = _matmul("mm_proj", hn, w_pad_t, tb=True)
    s, yc, ge = _s5_fwd(proj, bb_band, c_band, lam_pow, s5_d, n_seq, seq_len)
    if late:
        w = {**w, **late[1](ge)}
    tg, y5 = _matmul("mm_s5_glu", ge, w["s5_w_glu"], epilogue=lambda c, r, q: [c, _s5_out(r[0], r[1], r[2], c, *q)],
                     epi_rows=[(yc, 0), (proj, OFF_U5), (proj, OFF_Z5)], epi_pars=[s5_d, b_glu], epi_outs=[F32, BF16])
    s5_rows = [(yc, 0, S5_WIDTH), (proj, OFF_U5, S5_WIDTH), (proj, OFF_Z5, S5_WIDTH), (tg, 0, S5_WIDTH)]
    s5_pars = [(s5_d, 0, S5_WIDTH), (b_glu, 0, S5_WIDTH)]

    xbc_act = _conv_fwd(proj, conv_w, conv_b, n_rows, seq_len)
    yss, states = _ssd_fwd(xbc_act, proj, dtb, alog, dsk, ssd_norm_w, n_seq, seq_len)

    m5 = _matmul("mm_br_s5", y5, w["w_br_s5"])
    ms, merged = _matmul("mm_br_ssd", yss, w["w_br_ssd"], epilogue=lambda c, r, q: [c, _merge(r[0], r[1], r[2], c)],
                         epi_rows=[(proj, OFF_G5), (proj, OFF_GS), (m5, 0)], epi_outs=[F32, BF16])
    mg_rows = [(proj, OFF_G5, D_MODEL), (proj, OFF_GS, D_MODEL), (m5, 0, D_MODEL), (ms, 0, D_MODEL)]

    def resid_norm(c, r, q):
        h1_ = r[0] + c
        return [h1_, _rms(h1_, q[0])]

    h1, hp = _matmul("mm_out", merged, w["w_out"], epilogue=resid_norm, epi_rows=[(x2, 0)], epi_pars=[ple_norm_w],
                     epi_outs=[F32, BF16], full_rows=True)
    pp = _matmul("mm_ple_proj", p2, w["w_ple_proj"])

    def head_fn(pgl_, r, q):
        h1_, pp_, tgt_ = r
        loss, vjp = jax.vjp(lambda a, b, c, f: _head_loss(a, b, c, f, tgt_), h1_, pgl_, pp_, q[0])
        dh1_, dpgl_, dpp_, dfw_ = vjp(jnp.ones_like(loss))
        return [dh1_, dpgl_, dpp_], [loss, dfw_]

    dh2, dpgl, dpp, loss_acc, d_final_w = _matmul(
        "mm_ple_gate_head", hp, w["w_ple_gate"], epilogue=head_fn, epi_rows=[(h1, 0), (pp, 0), (t2, 0)],
        epi_pars=[final_w], epi_outs=[F32, BF16, BF16], epi_accs=[(1, 128), (1, D_MODEL)], full_rows=True)
    loss = loss_acc[0, 0]

    g = {}
    g["final_norm_w"] = d_final_w
    g["w_ple_gate"] = _matmul("mm_d_w_ple_gate", hp, dpgl, ta=True)
    g["w_ple_proj"] = _matmul("mm_d_w_ple_proj", p2, dpp, ta=True)

    def ple_norm_bwd(dhp_, r, q):
        h1_, dh2_ = r
        _, vjp = jax.vjp(_rms, h1_, q[0])
        dh, dw = vjp(dhp_)
        dh = dh + dh2_
        return [dh, dh], [dw]

    dh1, dh1_b, g["ple_norm_w"] = _matmul(
        "mm_d_hp", dpgl, w["w_ple_gate"], tb=True, epilogue=ple_norm_bwd, epi_rows=[(h1, 0), (dh2, 0)],
        epi_pars=[ple_norm_w], epi_outs=[F32, BF16], epi_accs=[(1, D_MODEL)], full_rows=True)
    g["w_out"] = _matmul("mm_d_w_out", merged, dh1_b, ta=True)

    dproj = lax.empty((n_rows, PROJ_W), BF16)

    def merge_bwd(dmerged, r, q):
        sg5, sgs = jax.nn.sigmoid(r[0]), jax.nn.sigmoid(r[1])
        d_gates = jnp.concatenate([dmerged * r[2] * sg5 * (1.0 - sg5), dmerged * r[3] * sgs * (1.0 - sgs)], axis=1)
        return [d_gates, dmerged * sg5, dmerged * sgs]

    dproj, dm5, dms = _matmul(
        "mm_d_merged", dh1_b, w["w_out"], tb=True, epilogue=merge_bwd,
        epi_rows=[(proj, OFF_G5), (proj, OFF_GS), (m5, 0), (ms, 0)], epi_outs=[BF16, BF16, BF16], full_rows=True,
        epi_into=(dproj, OFF_G5, 2 * D_MODEL))
    g["w_br_s5"] = _matmul("mm_d_w_br_s5", y5, dm5, ta=True)
    g["w_br_ssd"] = _matmul("mm_d_w_br_ssd", yss, dms, ta=True)
    dy5 = _matmul("mm_d_y5", dm5, w["w_br_s5"], tb=True)
    dyss = _matmul("mm_d_yss", dms, w["w_br_ssd"], tb=True)

    def s5_out_bwd_a(r, q):
        yc_, u_, z_, t_, dy_ = r
        d_, bg_ = q
        ge_ = jax.nn.gelu(yc_ + d_ * u_)
        _, vjp = jax.vjp(lambda a, z, t, b: a * jax.nn.sigmoid(t + b) * jax.nn.silu(z), ge_, z_, t_, bg_)
        dge, dz, dt_, dbg = vjp(dy_)
        return [dz, dge, dt_], [dbg]

    dproj, dge_a, dtg, g["s5_b_glu"] = row(
        "s5_out_bwd_a", s5_out_bwd_a, row_ins=s5_rows + [(dy5, 0, S5_WIDTH)], par_ins=s5_pars,
        row_outs=[(S5_WIDTH, BF16), (S5_WIDTH, F32), (S5_WIDTH, BF16)], acc_outs=[(1, S5_WIDTH)],
        into=(dproj, OFF_Z5))
    g["s5_w_glu"] = _matmul("mm_d_w_glu", ge, dtg, ta=True)
    early = w["_early"](g) if "_early" in w else None
    dge_b = _matmul("mm_d_ge", dtg, w["s5_w_glu"], tb=True)

    def s5_out_bwd_b(r, q):
        yc_, u_, da_, db_ = r
        _, vjp = jax.vjp(lambda yc, u, d: jax.nn.gelu(yc + d * u), yc_, u_, q[0])
        dyc_, du_, dd_ = vjp(da_ + db_)
        return [dyc_, du_], [dd_]

    dyc, du5_a, g["s5_d"] = row(
        "s5_out_bwd_b", s5_out_bwd_b,
        row_ins=[(yc, 0, S5_WIDTH), (proj, OFF_U5, S5_WIDTH), (dge_a, 0, S5_WIDTH), (dge_b, 0, S5_WIDTH)],
        par_ins=[(s5_d, 0, S5_WIDTH)] + ([(early, 0, 128)] if early is not None else []),
        row_outs=[(S5_WIDTH, BF16), (S5_WIDTH, F32)], acc_outs=[(1, S5_WIDTH)])
    dproj, d_lam, d_bb_band, d_c_band = _s5_bwd(dyc, s, proj, du5_a, bb_band, c_band, lam_pow_conj, dproj,
                                                n_seq, seq_len)

    d_lr, d_li = _state_uncols(d_lam)
    d_bbr, d_bbi = _band_take(d_bb_band)
    d_are, d_aim, d_ls, d_br2, d_bi2 = _s5_params_bwd(
        a_re, a_im, log_step, b_re2, b_im2, expand,
        d_lr.reshape(S5_GROUPS, S5_STATE), d_li.reshape(S5_GROUPS, S5_STATE), d_bbr, d_bbi)
    g["s5_a_re"], g["s5_a_im"], g["s5_log_step"] = d_are, d_aim, d_ls
    g["s5_b_re_ghp"], g["s5_b_im_ghp"] = d_br2, d_bi2
    d_cr, d_ci = _band_take(d_c_band)
    g["s5_c_re"], g["s5_c_im"] = d_cr, -d_ci

    dxs, dbm, dcm, dproj, ddt, d_dtb, d_alog, d_dsk, g["ssd_norm_w"] = _ssd_bwd(
        xbc_act, proj, states, dyss, dtb, alog, dsk, ssd_norm_w, n_seq, seq_len, dproj)
    (dproj,) = row("ssd_ddt", lambda r, q: ([r[0]], []), row_ins=[(ddt, 0, DT_W)], par_ins=[],
                   row_outs=[(DT_W, BF16)], into=(dproj, OFF_DT))
    g["ssd_dt_bias"], g["ssd_a_log"], g["ssd_d"] = _unpad_heads(d_dtb), _unpad_heads(d_alog), _unpad_heads(d_dsk)
    conv_dw, conv_db = [], []
    for nm, d_act, off in (("x", dxs, 0), ("b", dbm, SSD_WIDTH), ("c", dcm, SSD_WIDTH + SSD_BC)):
        dproj, dw_, db_ = _conv_bwd("ssd_conv_bwd_" + nm, proj, d_act, conv_w, conv_b, n_rows, seq_len, off, dproj)
        conv_dw.append(dw_)
        conv_db.append(db_)
    g["ssd_conv_w"] = jnp.concatenate(conv_dw, axis=1)
    g["ssd_conv_b"] = jnp.concatenate(conv_db, axis=1)

    g["w_in_t"] = _unpad_w_in_t(_matmul("mm_d_w_in", dproj, hn, ta=True))
    def norm_bwd(dhn_, r, q):
        x_, dh1_ = r
        _, vjp = jax.vjp(_rms, x_, q[0])
        dx_, dw_ = vjp(dhn_)
        return [dx_ + dh1_], [dw_]

    dx, g["norm_w"] = _matmul("mm_d_hn", dproj, w_pad_t, epilogue=norm_bwd, epi_rows=[(x2, 0), (dh1, 0)],
                              epi_pars=[norm_w], epi_outs=[F32], epi_accs=[(1, D_MODEL)], full_rows=True)
    return loss, dx.reshape(x.shape), g


HBM = pl.BlockSpec(memory_space=pltpu.HBM)


def _chip_index(x, y):
    return 2 * x + y


def _half(shape2d, axis, which):
    h = shape2d[axis] // 2
    sl = pl.ds(pl.multiple_of(which * h, 128 if axis else 8), h)
    return (slice(None), sl) if axis else (sl, slice(None))


def _gather_chips(split, axes, whole):
    ns, nw = len(split), len(whole)
    n = ns + nw

    def body(*refs):
        ins, outs = refs[:n], refs[n:2 * n]
        ici_send, ici_recv, d2d_send, d2d_recv, local_sems = refs[2 * n:]
        x, y, c = lax.axis_index("x"), lax.axis_index("y"), lax.axis_index("c")
        me = _chip_index(x, y)
        sibling = (x, y, 1 - c)
        peers = [(1 - x, y), (x, 1 - y), (1 - x, 1 - y)]

        def half(t, which):
            return _half(split[t].shape, axes[t], which)

        copies = []
        for t in range(n):
            loc = pltpu.make_async_copy(ins[t], outs[t].at[me], local_sems.at[t])
            loc.start()
            copies.append(loc)

        def ici(t, k, slot):
            px, py = peers[k]
            if t < ns:
                src, dst = ins[t].at[half(t, c)], outs[t].at[(slot,) + half(t, c)]
            else:
                src, dst = ins[t], outs[t].at[slot]
            return pltpu.make_async_remote_copy(src_ref=src, dst_ref=dst, send_sem=ici_send.at[t, k],
                                                recv_sem=ici_recv.at[t, k], device_id=(px, py, c), device_id_type=MESH)

        def d2d(t, k, which):
            rows = outs[t].at[(_chip_index(*peers[k]),) + half(t, which)]
            return pltpu.make_async_remote_copy(src_ref=rows, dst_ref=rows, send_sem=d2d_send.at[t, k],
                                                recv_sem=d2d_recv.at[t, k], device_id=sibling, device_id_type=MESH)

        sends = []
        for t in range(n):
            for k in range(3):
                cp = ici(t, k, me)
                cp.start()
                sends.append(cp)
        for t in range(n):
            for k in range(3):
                ici(t, k, _chip_index(*peers[k])).wait_recv()
                if t < ns:
                    cp = d2d(t, k, c)
                    cp.start()
                    sends.append(cp)
        for t in range(ns):
            for k in range(3):
                d2d(t, k, 1 - c).wait_recv()
        for cp in sends:
            cp.wait_send()
        for cp in copies:
            cp.wait()

    arrays = list(split) + list(whole)
    return pl.pallas_call(
        body, name="gather_weights",
        in_specs=[HBM] * n, out_specs=[HBM] * n,
        out_shape=[jax.ShapeDtypeStruct((N_CHIPS,) + a.shape, a.dtype) for a in arrays],
        scratch_shapes=[pltpu.SemaphoreType.DMA((n, 3)), pltpu.SemaphoreType.DMA((n, 3)),
                        pltpu.SemaphoreType.DMA((ns, 3)), pltpu.SemaphoreType.DMA((ns, 3)),
                        pltpu.SemaphoreType.DMA((n,))],
    )(*arrays)


SEM = pl.BlockSpec(memory_space=pltpu.SEMAPHORE)
DATAFLOW = pltpu.SideEffectType.DATAFLOW_SIDE_EFFECTING


def _gather_start(shards, after):
    n = len(shards)

    def body(*refs):
        ins, lands = refs[:n], refs[n:2 * n]
        send_sems, recv_sems = refs[2 * n + 1], refs[2 * n + 2]
        token = refs[-1]
        x, y, c = lax.axis_index("x"), lax.axis_index("y"), lax.axis_index("c")
        me = _chip_index(x, y)
        for t in range(n):
            for k, (px, py) in enumerate([(1 - x, y), (x, 1 - y), (1 - x, 1 - y)]):
                pltpu.make_async_remote_copy(
                    src_ref=ins[t], dst_ref=lands[t].at[me], send_sem=send_sems.at[3 * t + k],
                    recv_sem=recv_sems.at[3 * t + k],
                    device_id=(px, py, c), device_id_type=MESH).start()
        token[...] = jnp.zeros_like(token)

    zones = [lax.empty((N_CHIPS,) + a.shape, a.dtype) for a in shards]
    res = pl.pallas_call(
        body, name="gather_rest_start",
        out_shape=(pltpu.SemaphoreType.DMA((3 * n,)), pltpu.SemaphoreType.DMA((3 * n,)),
                   *[pltpu.HBM(a.shape, a.dtype) for a in shards], *[pltpu.HBM(z.shape, z.dtype) for z in zones],
                   jax.ShapeDtypeStruct((8, 128), F32)),
        in_specs=[HBM] * (2 * n) + [pl.BlockSpec(memory_space=pl.ANY)],
        out_specs=(SEM, SEM, *[HBM] * (2 * n), pl.BlockSpec(memory_space=pltpu.VMEM)),
        input_output_aliases={t: 2 + t for t in range(2 * n)},
        compiler_params=pltpu.CompilerParams(has_side_effects=DATAFLOW),
    )(*[pltpu.with_memory_space_constraint(a, pltpu.HBM) for a in shards],
      *[pltpu.with_memory_space_constraint(z, pltpu.HBM) for z in zones], after)
    return res[0], res[1], list(res[2:2 + n]), list(res[2 + n:2 + 2 * n]), res[-1]


def _gather_wait(send_sems, recv_sems, thru, lands, after):
    n = len(thru)

    def body(*refs):
        ins, zones = refs[:n], refs[n:2 * n]
        s_sems, r_sems = refs[2 * n], refs[2 * n + 1]
        x, y, c = lax.axis_index("x"), lax.axis_index("y"), lax.axis_index("c")
        for t in range(n):
            for k, (px, py) in enumerate([(1 - x, y), (x, 1 - y), (1 - x, 1 - y)]):
                cp = pltpu.make_async_remote_copy(
                    src_ref=ins[t], dst_ref=zones[t].at[_chip_index(px, py)], send_sem=s_sems.at[3 * t + k],
                    recv_sem=r_sems.at[3 * t + k], device_id=(px, py, c), device_id_type=MESH)
                cp.wait_send()
                cp.wait_recv()

    res = pl.pallas_call(
        body, name="gather_rest_wait",
        out_shape=(*[pltpu.HBM(a.shape, a.dtype) for a in thru], *[pltpu.HBM(z.shape, z.dtype) for z in lands]),
        in_specs=[HBM] * (2 * n) + [SEM, SEM, pl.BlockSpec(memory_space=pl.ANY)], out_specs=[HBM] * (2 * n),
        input_output_aliases={t: t for t in range(2 * n)},
        compiler_params=pltpu.CompilerParams(has_side_effects=DATAFLOW),
    )(*thru, *lands, send_sems, recv_sems, after)
    return list(res[:n]), list(res[n:])


def _scatter_ends(t, k, n_slotted, ins, zones, x, y, me):
    px, py = [(1 - x, y), (x, 1 - y), (1 - x, 1 - y)][k]
    if t < n_slotted:
        return ins[t].at[_chip_index(px, py)], zones[t].at[k], (px, py)
    return ins[t], (zones[t].at[me], zones[t].at[_chip_index(px, py)]), (px, py)


def _scatter_start(name, slotted, whole, after):
    n_s = len(slotted)
    arrays = list(slotted) + list(whole)
    n = len(arrays)

    def body(*refs):
        ins, lands = refs[:n], refs[n:2 * n]
        send_sems, recv_sems = refs[2 * n + 1], refs[2 * n + 2]
        token = refs[-1]
        x, y, c = lax.axis_index("x"), lax.axis_index("y"), lax.axis_index("c")
        for t in range(n):
            for k in range(3):
                src, dst, (px, py) = _scatter_ends(t, k, n_s, ins, lands, x, y, _chip_index(x, y))
                pltpu.make_async_remote_copy(
                    src_ref=src, dst_ref=dst if t < n_s else dst[0], send_sem=send_sems.at[3 * t + k],
                    recv_sem=recv_sems.at[3 * t + k], device_id=(px, py, c), device_id_type=MESH).start()
        token[...] = jnp.zeros_like(token)

    zones = [lax.empty((3,) + a.shape[1:], a.dtype) for a in slotted]
    zones += [lax.empty((N_CHIPS,) + a.shape, a.dtype) for a in whole]
    slotted = arrays
    res = pl.pallas_call(
        body, name=name,
        out_shape=(pltpu.SemaphoreType.DMA((3 * n,)), pltpu.SemaphoreType.DMA((3 * n,)),
                   *[pltpu.HBM(a.shape, a.dtype) for a in slotted], *[pltpu.HBM(z.shape, z.dtype) for z in zones],
                   jax.ShapeDtypeStruct((8, 128), F32)),
        in_specs=[HBM] * (2 * n) + [pl.BlockSpec(memory_space=pl.ANY)],
        out_specs=(SEM, SEM, *[HBM] * (2 * n), pl.BlockSpec(memory_space=pltpu.VMEM)),
        input_output_aliases={t: 2 + t for t in range(2 * n)},
        compiler_params=pltpu.CompilerParams(has_side_effects=DATAFLOW),
    )(*[pltpu.with_memory_space_constraint(a, pltpu.HBM) for a in slotted],
      *[pltpu.with_memory_space_constraint(z, pltpu.HBM) for z in zones], after)
    return res[0], res[1], list(res[2:2 + n]), list(res[2 + n:2 + 2 * n]), res[-1]


def _scatter_wait(name, n_slotted, send_sems, recv_sems, thru, lands, after):
    n = len(thru)

    def body(*refs):
        ins, zones = refs[:n], refs[n:2 * n]
        s_sems, r_sems = refs[2 * n], refs[2 * n + 1]
        x, y, c = lax.axis_index("x"), lax.axis_index("y"), lax.axis_index("c")
        for t in range(n):
            for k in range(3):
                src, dst, (px, py) = _scatter_ends(t, k, n_slotted, ins, zones, x, y, _chip_index(x, y))
                cp = pltpu.make_async_remote_copy(
                    src_ref=src, dst_ref=dst if t < n_slotted else dst[1], send_sem=s_sems.at[3 * t + k],
                    recv_sem=r_sems.at[3 * t + k], device_id=(px, py, c), device_id_type=MESH)
                cp.wait_send()
                cp.wait_recv()

    res = pl.pallas_call(
        body, name=name,
        out_shape=(*[pltpu.HBM(a.shape, a.dtype) for a in thru], *[pltpu.HBM(z.shape, z.dtype) for z in lands]),
        in_specs=[HBM] * (2 * n) + [SEM, SEM, pl.BlockSpec(memory_space=pl.ANY)], out_specs=[HBM] * (2 * n),
        input_output_aliases={t: t for t in range(2 * n)},
        compiler_params=pltpu.CompilerParams(has_side_effects=DATAFLOW),
    )(*thru, *lands, send_sems, recv_sems, after)
    return list(res[:n]), list(res[n:])


def _sum_chips(name, slotted, recv, idx, dep):
    _, r, c = slotted.shape
    tr = _row_tile(r, 5 * c * 4)

    def body(idx_ref, own_ref, r_ref, _, o_ref):
        acc = own_ref[0]
        for k in range(3):
            acc = acc + r_ref[k]
        o_ref[...] = acc

    grid_spec = pltpu.PrefetchScalarGridSpec(
        num_scalar_prefetch=1, grid=(r // tr,),
        in_specs=[pl.BlockSpec((1, tr, c), lambda i, s: (s[1], i, 0)), pl.BlockSpec((3, tr, c), lambda i, s: (0, i, 0)),
                  pl.BlockSpec(memory_space=pl.ANY)],
        out_specs=pl.BlockSpec((tr, c), lambda i, s: (i, 0)))
    return pl.pallas_call(
        body, name=name, grid_spec=grid_spec, out_shape=jax.ShapeDtypeStruct((r, c), F32),
        compiler_params=_cparams(("parallel",)),
    )(idx, slotted, recv, dep)


def _half_shape(shape2d, axis):
    r, c = shape2d
    return (r, c // 2) if axis else (r // 2, c)


def _swap_halves(slotted, axes, small):
    n = len(slotted)

    def body(*refs):
        ins, sm_in = refs[:n], refs[n]
        outs, sm_out = refs[n + 1:2 * n + 1], refs[2 * n + 1]
        send_sems, recv_sems, local_sem = refs[2 * n + 2:]
        x, y, c = lax.axis_index("x"), lax.axis_index("y"), lax.axis_index("c")
        local = pltpu.make_async_copy(sm_in, sm_out.at[c], local_sem)
        local.start()
        sends = []
        for t in range(n):
            other = (slice(None),) + _half(slotted[t].shape[1:], axes[t], 1 - c)
            cp = pltpu.make_async_remote_copy(
                src_ref=ins[t].at[other], dst_ref=outs[t], send_sem=send_sems.at[t],
                recv_sem=recv_sems.at[t], device_id=(x, y, 1 - c), device_id_type=MESH)
            cp.start()
            sends.append(cp)
        cp = pltpu.make_async_remote_copy(
            src_ref=sm_in, dst_ref=sm_out.at[c], send_sem=send_sems.at[n], recv_sem=recv_sems.at[n],
            device_id=(x, y, 1 - c), device_id_type=MESH)
        cp.start()
        sends.append(cp)
        for cp in sends[:n]:
            cp.wait_recv()
        pltpu.make_async_remote_copy(
            src_ref=sm_in, dst_ref=sm_out.at[1 - c], send_sem=send_sems.at[n], recv_sem=recv_sems.at[n],
            device_id=(x, y, 1 - c), device_id_type=MESH).wait_recv()
        for cp in sends:
            cp.wait_send()
        local.wait()

    return pl.pallas_call(
        body, name="swap_halves",
        in_specs=[HBM] * (n + 1), out_specs=[HBM] * (n + 1),
        out_shape=[jax.ShapeDtypeStruct((a.shape[0],) + _half_shape(a.shape[1:], ax), a.dtype)
                   for a, ax in zip(slotted, axes, strict=True)]
        + [jax.ShapeDtypeStruct((2,) + small.shape, small.dtype)],
        scratch_shapes=[pltpu.SemaphoreType.DMA((n + 1,)), pltpu.SemaphoreType.DMA((n + 1,)), pltpu.SemaphoreType.DMA],
    )(*slotted, small)


def _scatter_halves(parts, small):
    n = len(parts)
    half_rows = small.shape[0] // 2

    def body(*refs):
        ins, sm_in = refs[:n], refs[n]
        outs, sm_out = refs[n + 1:2 * n + 1], refs[2 * n + 1]
        send_sems, recv_sems, sm_send, sm_recv, local_sem = refs[2 * n + 2:]
        x, y, c = lax.axis_index("x"), lax.axis_index("y"), lax.axis_index("c")
        me = _chip_index(x, y)
        peers = [(1 - x, y), (x, 1 - y), (1 - x, 1 - y)]
        mine = sm_in.at[pl.ds(pl.multiple_of(c * half_rows, 8), half_rows), :]
        local = pltpu.make_async_copy(mine, sm_out.at[me], local_sem)
        local.start()
        sends = []
        for k, (px, py) in enumerate(peers):
            cp = pltpu.make_async_remote_copy(src_ref=mine, dst_ref=sm_out.at[me], send_sem=sm_send.at[k],
                                              recv_sem=sm_recv.at[k], device_id=(px, py, c), device_id_type=MESH)
            cp.start()
            sends.append(cp)
        for t in range(n):
            for k, (px, py) in enumerate(peers):
                cp = pltpu.make_async_remote_copy(
                    src_ref=ins[t].at[_chip_index(px, py)], dst_ref=outs[t].at[k], send_sem=send_sems.at[t, k],
                    recv_sem=recv_sems.at[t, k], device_id=(px, py, c), device_id_type=MESH)
                cp.start()
                sends.append(cp)
        for k, (px, py) in enumerate(peers):
            pltpu.make_async_remote_copy(src_ref=mine, dst_ref=sm_out.at[_chip_index(px, py)], send_sem=sm_send.at[k],
                                         recv_sem=sm_recv.at[k], device_id=(px, py, c), device_id_type=MESH).wait_recv()
        for cp in sends[3:]:
            cp.wait_recv()
        for cp in sends:
            cp.wait_send()
        local.wait()

    return pl.pallas_call(
        body, name="scatter_halves",
        in_specs=[HBM] * (n + 1), out_specs=[HBM] * (n + 1),
        out_shape=[jax.ShapeDtypeStruct((3,) + a.shape[1:], a.dtype) for a in parts]
        + [jax.ShapeDtypeStruct((N_CHIPS, half_rows, small.shape[1]), small.dtype)],
        scratch_shapes=[pltpu.SemaphoreType.DMA((n, 3)), pltpu.SemaphoreType.DMA((n, 3)),
                        pltpu.SemaphoreType.DMA((3,)), pltpu.SemaphoreType.DMA((3,)), pltpu.SemaphoreType.DMA],
    )(*parts, small)


def _tiling(r, c, f32_per_elem):
    if r % 8 == 0:
        tr = _row_tile(r, f32_per_elem * c * 4)
        return r // tr, (tr, c), lambda i: (i, 0)
    assert c % 128 == 0, (r, c)
    return c // 128, (r, 128), lambda i: (0, i)


def _pair_sum(name, slotted, other, idx, axis):
    _, r, c = slotted.shape
    hr, hc = _half_shape((r, c), axis)
    n, (tr, tc), at = _tiling(hr, hc, 13)
    if axis == 0:
        a = slotted.reshape(N_CHIPS, 2, hr, c)
        a_all = pl.BlockSpec((N_CHIPS, 1, tr, tc), lambda i, s: (0, s[0]) + at(i))
        a_own = pl.BlockSpec((1, 1, tr, tc), lambda i, s: (s[1], s[0]) + at(i))
    else:
        a, per_half = slotted, hc // tc
        a_all = pl.BlockSpec((N_CHIPS, tr, tc), lambda i, s: (0, at(i)[0], s[0] * per_half + at(i)[1]))
        a_own = pl.BlockSpec((1, tr, tc), lambda i, s: (s[1], at(i)[0], s[0] * per_half + at(i)[1]))

    def body(idx_ref, a_ref, b_ref, am_ref, bm_ref, p_ref, own_ref):
        mine, mine_own = (a_ref[:, 0], am_ref[0, 0]) if axis == 0 else (a_ref[...], am_ref[0])
        p_ref[...] = (mine + b_ref[...]).astype(p_ref.dtype)
        own_ref[...] = mine_own + bm_ref[0]

    grid_spec = pltpu.PrefetchScalarGridSpec(
        num_scalar_prefetch=1, grid=(n,),
        in_specs=[a_all, pl.BlockSpec((N_CHIPS, tr, tc), lambda i, s: (0,) + at(i)),
                  a_own, pl.BlockSpec((1, tr, tc), lambda i, s: (s[1],) + at(i))],
        out_specs=[pl.BlockSpec((N_CHIPS, tr, tc), lambda i, s: (0,) + at(i)),
                   pl.BlockSpec((tr, tc), lambda i, s: at(i))])
    return pl.pallas_call(
        body, name=name, grid_spec=grid_spec,
        out_shape=[jax.ShapeDtypeStruct((N_CHIPS, hr, hc), BF16), jax.ShapeDtypeStruct((hr, hc), F32)],
        compiler_params=_cparams(("parallel",)),
    )(idx, a, other, a, other)


def _sum_parts(name, own, recv):
    h, c = own.shape
    n, (tr, tc), at = _tiling(h, c, 4)

    def body(o_ref, r_ref, out_ref):
        acc = o_ref[...]
        for k in range(3):
            acc = acc + r_ref[k].astype(F32)
        out_ref[...] = acc

    return pl.pallas_call(
        body, name=name, grid=(n,),
        in_specs=[pl.BlockSpec((tr, tc), at), pl.BlockSpec((3, tr, tc), lambda i: (0,) + at(i))],
        out_specs=pl.BlockSpec((tr, tc), at),
        out_shape=jax.ShapeDtypeStruct((h, c), F32),
        compiler_params=_cparams(("parallel",)),
    )(own, recv)


def _swap_sibling(name, parts):
    n = len(parts)

    def body(*refs):
        ins, outs = refs[:n], refs[n:2 * n]
        send_sems, recv_sems = refs[2 * n:]
        x, y, c = lax.axis_index("x"), lax.axis_index("y"), lax.axis_index("c")
        cps = []
        for t in range(n):
            cp = pltpu.make_async_remote_copy(
                src_ref=ins[t], dst_ref=outs[t], send_sem=send_sems.at[t], recv_sem=recv_sems.at[t],
                device_id=(x, y, 1 - c), device_id_type=MESH)
            cp.start()
            cps.append(cp)
        for cp in cps:
            cp.wait_recv()
        for cp in cps:
            cp.wait_send()

    return pl.pallas_call(
        body, name=name,
        in_specs=[HBM] * n, out_specs=[HBM] * n,
        out_shape=[jax.ShapeDtypeStruct(a.shape, a.dtype) for a in parts],
        scratch_shapes=[pltpu.SemaphoreType.DMA((n,)), pltpu.SemaphoreType.DMA((n,))],
    )(*parts)


def _sum_slots(name, a):
    k, r, c = a.shape
    tr = _row_tile(r, (k + 1) * c * 4)

    def body(a_ref, o_ref):
        acc = a_ref[0]
        for i in range(1, k):
            acc = acc + a_ref[i]
        o_ref[...] = acc

    return pl.pallas_call(
        body, name=name, grid=(r // tr,),
        in_specs=[pl.BlockSpec((k, tr, c), lambda i: (0, i, 0))],
        out_specs=pl.BlockSpec((tr, c), lambda i: (i, 0)),
        out_shape=jax.ShapeDtypeStruct((r, c), a.dtype),
        compiler_params=_cparams(("parallel",)),
    )(a)


def _adamw(name, w, m, v, g_parts):
    r, c = w.shape
    ng = len(g_parts)
    tr = _row_tile(r, (7 + ng) * c * 4)
    c1 = 1.0 - ADAM_B1 ** ADAM_STEP
    c2 = 1.0 - ADAM_B2 ** ADAM_STEP

    def body(*refs):
        w_ref, m_ref, v_ref = refs[:3]
        g_refs = refs[3:3 + ng]
        go_ref, d_ref, mo_ref, vo_ref = refs[3 + ng:]
        g = g_refs[0][...]
        for gr in g_refs[1:]:
            g = g + gr[...]
        m_new = ADAM_B1 * m_ref[...] + (1.0 - ADAM_B1) * g
        v_new = ADAM_B2 * v_ref[...] + (1.0 - ADAM_B2) * (g * g)
        go_ref[...] = g
        mo_ref[...] = m_new
        vo_ref[...] = v_new
        d_ref[...] = -ADAM_LR * ((m_new / c1) / (jnp.sqrt(v_new / c2) + ADAM_EPS) + ADAM_WD * w_ref[...])

    spec = pl.BlockSpec((tr, c), lambda i: (i, 0))
    return pl.pallas_call(
        body, name=name, grid=(r // tr,),
        in_specs=[spec] * (3 + ng), out_specs=[spec] * 4,
        out_shape=[jax.ShapeDtypeStruct((r, c), F32)] * 4,
        compiler_params=_cparams(("parallel",)),
    )(w, m, v, *g_parts)


def _adamw_halves(name, w, m, v, own, other, idx, axis):
    hr, hc = own.shape
    nb, (tr, tc), at = _tiling(hr, hc, 9)
    c1 = 1.0 - ADAM_B1 ** ADAM_STEP
    c2 = 1.0 - ADAM_B2 ** ADAM_STEP

    def body(idx_ref, w_ref, m_ref, v_ref, own_ref, oth_ref, go_ref, d_ref, mo_ref, vo_ref):
        g = jnp.where(pl.program_id(0) == idx_ref[0], own_ref[...], oth_ref[...])
        m_new = ADAM_B1 * m_ref[...] + (1.0 - ADAM_B1) * g
        v_new = ADAM_B2 * v_ref[...] + (1.0 - ADAM_B2) * (g * g)
        go_ref[...] = g
        mo_ref[...] = m_new
        vo_ref[...] = v_new
        d_ref[...] = -ADAM_LR * ((m_new / c1) / (jnp.sqrt(v_new / c2) + ADAM_EPS) + ADAM_WD * w_ref[...])

    per_half = (hc // tc) if axis else (hr // tr)
    if axis:
        full = pl.BlockSpec((tr, tc), lambda hh, i, s: (at(i)[0], hh * per_half + at(i)[1]))
    else:
        full = pl.BlockSpec((tr, tc), lambda hh, i, s: (hh * per_half + at(i)[0], at(i)[1]))
    part = pl.BlockSpec((tr, tc), lambda hh, i, s: at(i))
    grid_spec = pltpu.PrefetchScalarGridSpec(
        num_scalar_prefetch=1, grid=(2, nb), in_specs=[full, full, full, part, part], out_specs=[full] * 4)
    return pl.pallas_call(
        body, name=name, grid_spec=grid_spec, out_shape=[jax.ShapeDtypeStruct(w.shape, F32)] * 4,
        compiler_params=_cparams(("parallel", "parallel")),
    )(idx, w, m, v, own, other)


WEIGHTS = ['norm_w', 'w_in', 's5_a_re', 's5_a_im', 's5_b_re', 's5_b_im', 's5_c_re', 's5_c_im', 's5_d', 's5_log_step',
           's5_w_glu', 's5_b_glu', 'ssd_conv_w', 'ssd_conv_b', 'ssd_dt_bias', 'ssd_a_log', 'ssd_d', 'ssd_norm_w',
           'w_br_s5', 'w_br_ssd', 'w_out', 'ple_norm_w', 'w_ple_gate', 'w_ple_proj', 'final_norm_w']
SHARDED = {'w_in': ((IN_PROJ_DIM, 1024), 0), 's5_w_glu': ((512, 512), 0), 'ssd_conv_w': ((SSD_CONV, SSD_CONV_DIM), 1),
           'w_br_s5': ((512, 1024), 1), 'w_br_ssd': ((1536, 1024), 0), 'w_out': ((1024, 1024), 0),
           'w_ple_gate': ((1024, 1024), 0), 'w_ple_proj': ((256, 1024), 1)}
TRANSPOSED = ('w_in',)
SMALL = [n for n in WEIGHTS if n not in SHARDED]


def _shard_shape(name):
    (r, c), ax = SHARDED[name]
    return (r // N_CHIPS, c) if ax == 0 else (r, c // N_CHIPS)


def _half_axis(name):
    return 0 if (_shard_shape(name)[0] // 2) % 16 == 0 else 1


def _shard2d(name, a):
    r, c = _shard_shape(name)
    return a.reshape(c, r).T if name in TRANSPOSED else a.reshape(r, c)


def _unshard2d(name, a2, shape):
    return (a2.T if name in TRANSPOSED else a2).reshape(shape)


def _unslot(name, a4):
    (r, c), ax = SHARDED[name]
    if ax == 0:
        return a4.reshape(r, c)
    return jnp.transpose(a4, (1, 0, 2)).reshape(r, c)


def _slot(name, full):
    (r, c), ax = SHARDED[name]
    if ax == 0:
        return full.reshape(N_CHIPS, r // N_CHIPS, c)
    return jnp.transpose(full.reshape(r, N_CHIPS, c // N_CHIPS), (1, 0, 2))


GHP = ('s5_b_re', 's5_b_im')


def _view_shape(name):
    if name in ('s5_a_re', 's5_a_im'):
        return (S5_GROUPS, S5_STATE)
    if name in GHP + ('s5_c_re', 's5_c_im'):
        return (S5_GROUPS, S5_GROUP, S5_STATE)
    if name == 'ssd_conv_w':
        return (SSD_CONV, SSD_CONV_DIM // N_CHIPS)
    return (1, {'s5_log_step': S5_GROUPS, 'ssd_conv_b': SSD_CONV_DIM, 'ssd_norm_w': SSD_WIDTH, 's5_d': S5_WIDTH,
                's5_b_glu': S5_WIDTH, 'ssd_dt_bias': SSD_HEADS, 'ssd_a_log': SSD_HEADS, 'ssd_d': SSD_HEADS}.get(name, D_MODEL))


def _view(name, a):
    if name in GHP:
        return jnp.swapaxes(a.reshape(S5_GROUPS, S5_STATE, S5_GROUP), 1, 2)
    return a.reshape(_view_shape(name))


def _unview(name, a, shape):
    return (jnp.swapaxes(a, 1, 2) if name in GHP else a).reshape(shape)


def _adamw_small(ws, ms, vs, gs):
    n = len(ws)
    c1 = 1.0 - ADAM_B1 ** ADAM_STEP
    c2 = 1.0 - ADAM_B2 ** ADAM_STEP

    def body(*refs):
        w_r, m_r, v_r, g_r = (refs[k * n:(k + 1) * n] for k in range(4))
        d_o, m_o, v_o = (refs[k * n:(k + 1) * n] for k in range(4, 7))
        for i in range(n):
            g = g_r[i][...]
            m_new = ADAM_B1 * m_r[i][...] + (1.0 - ADAM_B1) * g
            v_new = ADAM_B2 * v_r[i][...] + (1.0 - ADAM_B2) * (g * g)
            m_o[i][...] = m_new
            v_o[i][...] = v_new
            d_o[i][...] = -ADAM_LR * ((m_new / c1) / (jnp.sqrt(v_new / c2) + ADAM_EPS) + ADAM_WD * w_r[i][...])

    return pl.pallas_call(
        body, name="adamw_small", out_shape=[jax.ShapeDtypeStruct(w.shape, F32) for w in ws] * 3,
        compiler_params=pltpu.CompilerParams(vmem_limit_bytes=VMEM_LIMIT),
    )(*ws, *ms, *vs, *gs)


def _pack_small(vals):
    flat = jnp.concatenate([v.reshape(-1).astype(F32) for v in vals])
    rows = -(-flat.shape[0] // (256 * 128)) * 256
    return jnp.pad(flat, (0, rows * 128 - flat.shape[0])).reshape(rows, 128)


def _unpack_small(packed, shapes):
    flat = packed.reshape(-1)
    out, off = [], 0
    for sh in shapes:
        n = math.prod(sh)
        out.append(flat[off:off + n].reshape(sh))
        off += n
    return out


def kernel(x, p, norm_w, w_in, s5_a_re, s5_a_im, s5_b_re, s5_b_im, s5_c_re, s5_c_im, s5_d, s5_log_step, s5_w_glu, s5_b_glu, ssd_conv_w, ssd_conv_b, ssd_dt_bias, ssd_a_log, ssd_d, ssd_norm_w, w_br_s5, w_br_ssd, w_out, ple_norm_w, w_ple_gate, w_ple_proj, final_norm_w, loss_target, m_norm_w, m_w_in, m_s5_a_re, m_s5_a_im, m_s5_b_re, m_s5_b_im, m_s5_c_re, m_s5_c_im, m_s5_d, m_s5_log_step, m_s5_w_glu, m_s5_b_glu, m_ssd_conv_w, m_ssd_conv_b, m_ssd_dt_bias, m_ssd_a_log, m_ssd_d, m_ssd_norm_w, m_w_br_s5, m_w_br_ssd, m_w_out, m_ple_norm_w, m_w_ple_gate, m_w_ple_proj, m_final_norm_w, v_norm_w, v_w_in, v_s5_a_re, v_s5_a_im, v_s5_b_re, v_s5_b_im, v_s5_c_re, v_s5_c_im, v_s5_d, v_s5_log_step, v_s5_w_glu, v_s5_b_glu, v_ssd_conv_w, v_ssd_conv_b, v_ssd_dt_bias, v_ssd_a_log, v_ssd_d, v_ssd_norm_w, v_w_br_s5, v_w_br_ssd, v_w_out, v_ple_norm_w, v_w_ple_gate, v_w_ple_proj, v_final_norm_w):
    args = locals()
    wl = {n: args[n] for n in WEIGHTS}
    ml = {n: args["m_" + n] for n in WEIGHTS}
    vl = {n: args["v_" + n] for n in WEIGHTS}
    big = [n for n in SHARDED if n != 'ssd_conv_w']
    chip = _chip_index(lax.axis_index("x"), lax.axis_index("y"))
    idx = jnp.stack([lax.axis_index("c"), chip]).astype(jnp.int32)

    axes = [_half_axis(n) for n in big]
    first = ['w_in']
    rest = [n for n in big if n not in first]
    bf_shards = {n: _shard2d(n, wl[n]).astype(BF16) for n in big}
    w_in_t, conv_w4 = _gather_chips([bf_shards[n] for n in first], [_half_axis(n) for n in first],
                                    [_shard2d('ssd_conv_w', wl['ssd_conv_w'])])
    full = {n: wl[n] for n in SMALL}
    full["w_in_t"] = w_in_t
    full['ssd_conv_w'] = _unslot('ssd_conv_w', conv_w4)
    send_sems, recv_sems, thru, lands, token = _gather_start([bf_shards[n] for n in rest], w_in_t)

    def fetch_rest(after):
        own, zones = _gather_wait(send_sems, recv_sems, thru, lands, after)
        return {n: _unslot(n, lax.dynamic_update_slice(z, o[None], (chip, 0, 0)))
                for n, o, z in zip(rest, own, zones, strict=True)}

    full["_late"] = (token, fetch_rest)

    in_flight = []

    def send_rest(g_now):
        in_flight.extend(_scatter_start("scatter_rest_start", [_slot(n, g_now[n]) for n in rest], [], g_now['s5_w_glu']))
        return in_flight[4]

    full["_early"] = send_rest
    loss, grad_x, g = _local_step(x, p[0], loss_target, full)
    for n in TRANSPOSED:
        g[n] = g.pop(n + "_t")

    for n in ('s5_b_re', 's5_b_im'):
        g[n] = g.pop(n + "_ghp")
    small_shapes = [(1, 1)] + [_view_shape(n) for n in SMALL] + [(SSD_CONV, SSD_CONV_DIM)]
    small_pack = _pack_small([loss] + [g[n] for n in SMALL] + [g['ssd_conv_w']])
    first_axes = [_half_axis(n) for n in first]
    slotted = [g[n] for n in first]
    swapped = _swap_halves(slotted, first_axes, small_pack)
    pair = [_pair_sum("pair_sum_" + n, a, b, idx, ax)
            for n, a, b, ax in zip(first, slotted, swapped[:-1], first_axes, strict=True)]
    small_chip = _sum_slots("sum_small_pair", swapped[-1])
    half_rows = small_chip.shape[0] // 2
    my_half = lax.dynamic_slice(small_chip, (lax.axis_index("c") * half_rows, 0), (half_rows, small_chip.shape[1]))
    s_sems, r_sems, thru, lands, tok = _scatter_start("scatter_first_start", [pb for pb, _ in pair], [my_half], small_chip)

    out_g, out_d, out_m, out_v = {}, {}, {}, {}
    own_slots, zones = _scatter_wait("scatter_rest_wait", len(rest), *in_flight[:4], g['w_in'])
    chip_sums = [_sum_chips("sum_chips_" + n, a, z, idx, tok) for n, a, z in zip(rest, own_slots, zones, strict=True)]
    sib_sums = _swap_sibling("swap_sibling_rest", chip_sums)
    for n, own, sib in zip(rest, chip_sums, sib_sums, strict=True):
        res = _adamw("adamw_" + n, _shard2d(n, wl[n]), _shard2d(n, ml[n]), _shard2d(n, vl[n]), [own, sib])
        out_g[n], out_d[n], out_m[n], out_v[n] = (_unshard2d(n, r, wl[n].shape) for r in res)

    sent, got = _scatter_wait("scatter_first_wait", len(first), s_sems, r_sems, thru, lands, out_d[rest[-1]])
    small_zone = lax.dynamic_update_slice(got[-1], sent[-1][None], (chip, 0, 0))
    small_half = _sum_slots("sum_small_chips", small_zone)
    halves = [_sum_parts("sum_chips_" + n, own, r) for n, (_, own), r in zip(first, pair, got[:-1], strict=True)]
    swapped2 = _swap_sibling("swap_sibling_first", halves + [small_half])
    other_halves = swapped2[:-1]
    c_is_0 = lax.axis_index("c") == 0
    small_sum = jnp.concatenate([jnp.where(c_is_0, small_half, swapped2[-1]),
                                 jnp.where(c_is_0, swapped2[-1], small_half)], axis=0)
    for n, own, oth, ax in zip(first, halves, other_halves, first_axes, strict=True):
        res = _adamw_halves("adamw_" + n, _shard2d(n, wl[n]), _shard2d(n, ml[n]), _shard2d(n, vl[n]), own, oth, idx, ax)
        out_g[n], out_d[n], out_m[n], out_v[n] = (_unshard2d(n, r, wl[n].shape) for r in res)
    sm = _unpack_small(small_sum, small_shapes)
    loss_total = sm[0].reshape(())
    conv_g = lax.dynamic_slice(sm[-1], (0, chip * (SSD_CONV_DIM // N_CHIPS)), (SSD_CONV, SSD_CONV_DIM // N_CHIPS))
    names = SMALL + ['ssd_conv_w']
    grads = sm[1:-1] + [conv_g]
    res = _adamw_small([_view(n, wl[n]) for n in names], [_view(n, ml[n]) for n in names],
                       [_view(n, vl[n]) for n in names], grads)
    for i, n in enumerate(names):
        out_g[n] = _unview(n, grads[i], wl[n].shape)
        out_d[n], out_m[n], out_v[n] = (_unview(n, res[k * len(names) + i], wl[n].shape) for k in range(3))

    return (loss_total, grad_x, *[out_g[n] for n in WEIGHTS], *[out_d[n] for n in WEIGHTS],
            *[out_m[n] for n in WEIGHTS], *[out_v[n] for n in WEIGHTS])
```

```python
import functools
import math

import jax
import jax.numpy as jnp
from jax import lax
from jax.experimental import pallas as pl
from jax.experimental.pallas import tpu as pltpu

F32 = jnp.float32
BF16 = jnp.bfloat16
MESH = pl.DeviceIdType.MESH

D_MODEL = 1024
PLE_DIM = 256
RMS_EPS = 1e-6
S5_WIDTH = 512
S5_GROUP = 16
S5_GROUPS = 32
S5_STATE = 64
S5_N = S5_GROUPS * S5_STATE
S5_LB = 512
S5_NJ = S5_N // S5_LB
S5_TB = 256
S5_LOG_TB = 8
SSD_WIDTH = 1536
SSD_HEADDIM = 64
SSD_HEADS = 24
SSD_GROUPS = 4
SSD_HPG = 6
SSD_STATE = 128
SSD_CONV = 4
SSD_CHUNK = 128
SSD_BC = 512
SSD_CONV_DIM = 2560
GROUP_W = SSD_WIDTH // SSD_GROUPS
N_CHIPS = 4
N_DEV = 8

OFF_XBC, OFF_U5, OFF_Z5, OFF_DT, OFF_G5, OFF_GS, OFF_ZS = 0, 2560, 3072, 3584, 4096, 5120, 6144
DT_W = 512
PROJ_W = 7680
IN_PROJ_DIM = 7192

ADAM_LR, ADAM_B1, ADAM_B2, ADAM_EPS, ADAM_WD, ADAM_STEP = 0.001, 0.9, 0.999, 1e-08, 0.01, 10

VMEM_LIMIT = 56 * 1024 * 1024


def _pick(n, cands):
    for c in cands:
        if n % c == 0:
            return c
    return n


ROW_BLOCK_BYTES = 8 * 1024 * 1024


def _row_tile(r, bytes_per_row):
    for t in (r, 4096, 2048, 1024, 512, 256, 128, 64, 32, 16, 8):
        if t <= r and r % t == 0 and t * bytes_per_row <= ROW_BLOCK_BYTES:
            return t
    return r


def _cparams(sem):
    return pltpu.CompilerParams(dimension_semantics=sem, vmem_limit_bytes=VMEM_LIMIT)


def _dg(a, b, ca, cb):
    return lax.dot_general(a.astype(BF16), b.astype(BF16), (((ca,), (cb,)), ((), ())), preferred_element_type=F32)


@jax.custom_vjp
def dot_nn(a, b):
    return _dg(a, b, 1, 0)


@jax.custom_vjp
def dot_nt(a, b):
    return _dg(a, b, 1, 1)


@jax.custom_vjp
def dot_tn(a, b):
    return _dg(a, b, 0, 0)


dot_nn.defvjp(lambda a, b: (_dg(a, b, 1, 0), (a, b)), lambda r, g: (_dg(g, r[1], 1, 1), _dg(r[0], g, 0, 0)))
dot_nt.defvjp(lambda a, b: (_dg(a, b, 1, 1), (a, b)), lambda r, g: (_dg(g, r[1], 1, 0), _dg(g, r[0], 0, 0)))
dot_tn.defvjp(lambda a, b: (_dg(a, b, 0, 0), (a, b)), lambda r, g: (_dg(r[1], g, 1, 1), _dg(r[0], g, 1, 0)))


MM_VMEM_BUDGET = 30 * 1024 * 1024


def _mm_tiles(m, n, k, sa, sb, so, tn_only=None):
    best, best_key = None, None
    for tm in (1024, 512, 256, 128, 64, 32, 16, 8):
        if m % tm:
            continue
        for tn in (2048, 1536, 1280, 1024, 768, 640, 512, 384, 256, 128):
            if n % tn or (tn_only is not None and tn not in tn_only):
                continue
            for tk in (k, 2048, 1536, 1280, 1024, 768, 512, 256, 128):
                if k % tk or tk > max(k, 128) or (tk == k and k > 2048 and k % 128 == 0):
                    continue
                need = 2 * (tm * tk * sa + tk * tn * sb + tm * tn * so) + (tm * tn * 4 if tk < k else 0)
                if need > MM_VMEM_BUDGET:
                    continue
                key = (tm * tn * tk, tk)
                if best_key is None or key > best_key:
                    best, best_key = (tm, tn, tk), key
    assert best is not None, (m, n, k)
    return best


def _matmul(name, a, b, *, ta=False, tb=False, a_win=None, out_dtype=F32, epilogue=None, epi_rows=(), epi_pars=(),
            epi_outs=(), full_rows=False, epi_accs=(), epi_into=None, dep=None):
    a_off, a_w = a_win if a_win is not None else (0, a.shape[1])
    if ta:
        kdim, m = a.shape[0], a_w
    else:
        m, kdim = a.shape[0], a_w
    n = b.shape[0] if tb else b.shape[1]
    assert (b.shape[1] if tb else b.shape[0]) == kdim, (name, a.shape, b.shape)
    out_dtypes = list(epi_outs) if epilogue is not None else [out_dtype]
    so = sum(jnp.dtype(d).itemsize for d in out_dtypes) + sum(r.dtype.itemsize for r, _ in epi_rows)
    tn_ok = [n] if full_rows else [t for t in (1024, 512, 256, 128) if all(off % t == 0 for _, off in epi_rows)]
    tm, tn, tk = _mm_tiles(m, n, kdim, a.dtype.itemsize, b.dtype.itemsize, so, tn_ok if epilogue is not None else None)
    nk = kdim // tk
    n_er, n_ep, n_out = len(epi_rows), len(epi_pars), len(out_dtypes)
    if ta:
        assert a_off % tm == 0
        a_spec = pl.BlockSpec((tk, tm), lambda i, j, k: (k, i + a_off // tm))
    else:
        assert a_off % tk == 0
        a_spec = pl.BlockSpec((tm, tk), lambda i, j, k: (i, k + a_off // tk))
    if tb:
        b_spec = pl.BlockSpec((tn, tk), lambda i, j, k: (j, k))
    else:
        b_spec = pl.BlockSpec((tk, tn), lambda i, j, k: (k, j))
    ca, cb = (0 if ta else 1), (1 if tb else 0)

    n_acc = len(epi_accs)
    n_x = (1 if epi_into is not None else 0) + (1 if dep is not None else 0)
    assert not (n_acc or epi_into is not None) or full_rows

    def body(a_ref, b_ref, *refs):
        er, ep = refs[:n_er], refs[n_er:n_er + n_ep]
        first_out = n_er + n_ep + n_x
        o_refs = refs[first_out:first_out + n_out]
        s_refs = refs[first_out + n_out:first_out + n_out + n_acc]
        acc = refs[first_out + n_out + n_acc:]

        def finish(c):
            outs = [c] if epilogue is None else epilogue(c, [r[...] for r in er], [p[...] for p in ep])
            if n_acc:
                outs, sums = outs

                @pl.when(pl.program_id(0) == 0)
                def _():
                    for s_ref in s_refs:
                        s_ref[...] = jnp.zeros_like(s_ref)

                for s_ref, s in zip(s_refs, sums, strict=True):
                    s_ref[...] += jnp.broadcast_to(s, s_ref.shape)
            for o_ref, o in zip(o_refs, outs, strict=True):
                o_ref[...] = o.astype(o_ref.dtype)

        if nk == 1:
            finish(_dg(a_ref[...], b_ref[...], ca, cb))
            return
        (acc_ref,) = acc
        k = pl.program_id(2)

        @pl.when(k == 0)
        def _():
            acc_ref[...] = jnp.zeros_like(acc_ref)

        acc_ref[...] += _dg(a_ref[...], b_ref[...], ca, cb)

        @pl.when(k == nk - 1)
        def _():
            finish(acc_ref[...])

    in_specs = [a_spec, b_spec]
    in_specs += [pl.BlockSpec((tm, tn), functools.partial(lambda i, j, k, c: (i, j + c), c=off // tn)) for _, off in epi_rows]
    in_specs += [pl.BlockSpec((p.shape[0], tn), lambda i, j, k: (0, j)) for p in epi_pars]
    out_specs = [pl.BlockSpec((tm, tn), lambda i, j, k: (i, j)) for _ in out_dtypes]
    out_shape = [jax.ShapeDtypeStruct((m, n), d) for d in out_dtypes]
    extra, aliases = [], {}
    if epi_into is not None:
        buf, off, width = epi_into
        assert off % width == 0 and buf.dtype == out_dtypes[0]
        in_specs.append(pl.BlockSpec(memory_space=pl.ANY))
        out_specs[0] = pl.BlockSpec((tm, width), functools.partial(lambda i, j, k, c: (i, c), c=off // width))
        out_shape[0] = jax.ShapeDtypeStruct(buf.shape, buf.dtype)
        extra, aliases = [buf], {2 + n_er + n_ep: 0}
    if dep is not None:
        in_specs.append(pl.BlockSpec(memory_space=pl.ANY))
        extra = extra + [dep]
    out_specs += [pl.BlockSpec((r, w), lambda i, j, k: (0, 0)) for r, w in epi_accs]
    out_shape += [jax.ShapeDtypeStruct((r, w), F32) for r, w in epi_accs]
    res = pl.pallas_call(
        body, name=name, grid=(m // tm, n // tn, nk),
        in_specs=in_specs, out_specs=out_specs, out_shape=out_shape, input_output_aliases=aliases,
        scratch_shapes=[pltpu.VMEM((tm, tn), F32)] if nk > 1 else [],
        compiler_params=_cparams(("arbitrary",) * 3 if n_acc else ("parallel", "parallel", "arbitrary")),
    )(a, b, *[r for r, _ in epi_rows], *epi_pars, *extra)
    return res if epilogue is not None else res[0]


BAND = 128


def _band_matmul(name, kind, a, b, *, a_blk0=0, out_dtype=F32, epilogue=None, epi_rows=(), epi_pars=(), epi_outs=(),
                 into=None):
    n_rows = a.shape[0]
    blk = 2 * S5_LB
    tm = _pick(n_rows, (1024, 512, 256))
    if epilogue is not None:
        assert kind == "nt"
        n_er, n_ep = len(epi_rows), len(epi_pars)
        in_specs = [pl.BlockSpec((tm, blk), lambda i, j: (i, j)), pl.BlockSpec((BAND, blk), lambda i, j: (j, 0))]
        in_specs += [pl.BlockSpec((tm, BAND), functools.partial(lambda i, j, c: (i, c + j), c=c0)) for _, c0 in epi_rows]
        in_specs += [pl.BlockSpec((1, BAND), lambda i, j: (0, j)) for _ in epi_pars]
        out_specs = [pl.BlockSpec((tm, BAND), lambda i, j: (i, j)) for _ in epi_outs]
        out_shape = [jax.ShapeDtypeStruct((n_rows, S5_NJ * BAND), d) for d in epi_outs]
        extra, aliases = [], {}
        if into is not None:
            buf, c0 = into
            in_specs.append(pl.BlockSpec(memory_space=pl.ANY))
            out_specs[0] = pl.BlockSpec((tm, BAND), functools.partial(lambda i, j, c: (i, c + j), c=c0))
            out_shape[0] = jax.ShapeDtypeStruct(buf.shape, buf.dtype)
            extra, aliases = [buf], {2 + n_er + n_ep: 0}

        def epi_body(a_ref, b_ref, *refs):
            outs = epilogue(_dg(a_ref[...], b_ref[...], 1, 1), [r[...] for r in refs[:n_er]],
                            [p[...] for p in refs[n_er:n_er + n_ep]])
            for o_ref, o in zip(refs[n_er + n_ep + len(extra):], outs, strict=True):
                o_ref[...] = o.astype(o_ref.dtype)

        return pl.pallas_call(
            epi_body, name=name, grid=(n_rows // tm, S5_NJ), in_specs=in_specs, out_specs=out_specs,
            out_shape=out_shape, input_output_aliases=aliases, compiler_params=_cparams(("parallel", "parallel")),
        )(a, b, *[r for r, _ in epi_rows], *epi_pars, *extra)
    if kind == "nn":
        grid = (n_rows // tm, S5_NJ)
        in_specs = [pl.BlockSpec((tm, BAND), lambda i, j: (i, a_blk0 + j)), pl.BlockSpec((BAND, blk), lambda i, j: (j, 0))]
        out_spec = pl.BlockSpec((tm, blk), lambda i, j: (i, j))
        out_shape = (n_rows, S5_NJ * blk)
        sem = ("parallel", "parallel")

        def body(a_ref, b_ref, o_ref):
            o_ref[...] = _dg(a_ref[...], b_ref[...], 1, 0).astype(o_ref.dtype)
    elif kind == "nt":
        grid = (n_rows // tm, S5_NJ)
        in_specs = [pl.BlockSpec((tm, blk), lambda i, j: (i, j)), pl.BlockSpec((BAND, blk), lambda i, j: (j, 0))]
        out_spec = pl.BlockSpec((tm, BAND), lambda i, j: (i, j))
        out_shape = (n_rows, S5_NJ * BAND)
        sem = ("parallel", "parallel")

        def body(a_ref, b_ref, o_ref):
            o_ref[...] = _dg(a_ref[...], b_ref[...], 1, 1).astype(o_ref.dtype)
    else:
        grid = (S5_NJ, n_rows // tm)
        in_specs = [pl.BlockSpec((tm, BAND), lambda j, k: (k, a_blk0 + j)), pl.BlockSpec((tm, blk), lambda j, k: (k, j))]
        out_spec = pl.BlockSpec((BAND, blk), lambda j, k: (j, 0))
        out_shape = (S5_NJ * BAND, blk)
        sem = ("parallel", "arbitrary")

        def body(a_ref, b_ref, o_ref):
            @pl.when(pl.program_id(1) == 0)
            def _():
                o_ref[...] = jnp.zeros_like(o_ref)

            o_ref[...] += _dg(a_ref[...], b_ref[...], 0, 0)

    return pl.pallas_call(
        body, name=name, grid=grid, in_specs=in_specs, out_specs=out_spec,
        out_shape=jax.ShapeDtypeStruct(out_shape, out_dtype), compiler_params=_cparams(sem),
    )(a, b)


def _rowwise(name, fn, n_rows, tr, row_ins, par_ins, row_outs, acc_outs=(), into=None):
    nr, npar, no, na = len(row_ins), len(par_ins), len(row_outs), len(acc_outs)
    in_specs = []
    for arr, off, w in row_ins:
        assert off % w == 0 and arr.shape[0] == n_rows, (name, arr.shape, off, w)
        in_specs.append(pl.BlockSpec((tr, w), functools.partial(lambda i, c: (i, c), c=off // w)))
    for arr, off, w in par_ins:
        assert off % w == 0
        in_specs.append(pl.BlockSpec((arr.shape[0], w), functools.partial(lambda i, c: (0, c), c=off // w)))
    out_specs = [pl.BlockSpec((tr, w), lambda i: (i, 0)) for w, _ in row_outs]
    out_specs += [pl.BlockSpec((r, w), lambda i: (0, 0)) for r, w in acc_outs]
    out_shape = [jax.ShapeDtypeStruct((n_rows, w), dt) for w, dt in row_outs]
    out_shape += [jax.ShapeDtypeStruct((r, w), F32) for r, w in acc_outs]
    extra, aliases = [], {}
    if into is not None:
        buf, off = into
        w0 = row_outs[0][0]
        assert off % w0 == 0 and buf.dtype == row_outs[0][1]
        in_specs.append(pl.BlockSpec(memory_space=pl.ANY))
        out_specs[0] = pl.BlockSpec((tr, w0), functools.partial(lambda i, c: (i, c), c=off // w0))
        out_shape[0] = jax.ShapeDtypeStruct(buf.shape, buf.dtype)
        extra, aliases = [buf], {nr + npar: 0}
    nx = len(extra)

    def body(*refs):
        rows = [r[...] for r in refs[:nr]]
        pars = [r[...] for r in refs[nr:nr + npar]]
        o_refs = refs[nr + npar + nx:nr + npar + nx + no]
        a_refs = refs[nr + npar + nx + no:]
        outs, accs = fn(rows, pars)
        for o_ref, o in zip(o_refs, outs, strict=True):
            o_ref[...] = o.astype(o_ref.dtype)
        if na:
            @pl.when(pl.program_id(0) == 0)
            def _():
                for a_ref in a_refs:
                    a_ref[...] = jnp.zeros_like(a_ref)

            for a_ref, a in zip(a_refs, accs, strict=True):
                a_ref[...] += jnp.broadcast_to(a, a_ref.shape)

    res = pl.pallas_call(
        body, name=name, grid=(n_rows // tr,),
        in_specs=in_specs, out_specs=out_specs, out_shape=out_shape, input_output_aliases=aliases,
        compiler_params=_cparams(("arbitrary",) if na else ("parallel",)),
    )(*[a for a, _, _ in row_ins], *[a for a, _, _ in par_ins], *extra)
    return res


def _rms(x, w):
    return x * lax.rsqrt(jnp.mean(x * x, axis=-1, keepdims=True) + RMS_EPS) * w


def _gated_norm(y, z, w):
    outs = []
    for g in range(SSD_GROUPS):
        sl = slice(g * GROUP_W, (g + 1) * GROUP_W)
        yg = y[:, sl] * jax.nn.silu(z[:, sl])
        outs.append(yg * lax.rsqrt(jnp.mean(yg * yg, axis=-1, keepdims=True) + RMS_EPS) * w[:, sl])
    return jnp.concatenate(outs, axis=-1)


def _s5_out(yc, u, z, t, d, bg):
    ge = jax.nn.gelu(yc + d * u)
    return ge * jax.nn.sigmoid(t + bg) * jax.nn.silu(z)


def _merge(g5, gs, m5, ms):
    return jax.nn.sigmoid(g5) * m5 + jax.nn.sigmoid(gs) * ms


def _head_loss(h1, pgl, pp, fw, tgt):
    h2 = h1 + jax.nn.sigmoid(pgl) * pp
    err = _rms(h2, fw) - tgt
    per_row = 0.5 * jnp.mean(err * err, axis=-1, keepdims=True)
    return jnp.sum(per_row, axis=0, keepdims=True)


def _s5_disc(a_re, a_im, log_step, b_re2, b_im2, expand):
    step = jnp.exp(log_step)
    mag = jnp.exp(a_re * step)
    lb_re = mag * jnp.cos(a_im * step)
    lb_im = mag * jnp.sin(a_im * step)
    den = a_re * a_re + a_im * a_im
    n_re = lb_re - 1.0
    f_re = (n_re * a_re + lb_im * a_im) / den
    f_im = (lb_im * a_re - n_re * a_im) / den
    hi = lax.Precision.HIGHEST
    fr = jnp.dot(expand, f_re, precision=hi, preferred_element_type=F32)
    fi = jnp.dot(expand, f_im, precision=hi, preferred_element_type=F32)
    return lb_re, lb_im, fr * b_re2 - fi * b_im2, fr * b_im2 + fi * b_re2


def _s5_params_fwd(a_re, a_im, log_step, b_re2, b_im2, expand):
    gp = a_re.shape

    def body(ar, ai, ls, br, bi, ex, pr_ref, pi_ref, bbr_ref, bbi_ref):
        lr, li, bbr, bbi = _s5_disc(ar[...], ai[...], ls[...], br[...], bi[...], ex[...])
        bbr_ref[...] = bbr
        bbi_ref[...] = bbi
        qr, qi = lr, li
        for k in range(S5_LOG_TB):
            pr_ref[k] = qr
            pi_ref[k] = qi
            qr, qi = qr * lr - qi * li, qr * li + qi * lr

    return pl.pallas_call(
        body, name="s5_params_fwd",
        out_shape=(jax.ShapeDtypeStruct((S5_LOG_TB,) + gp, F32), jax.ShapeDtypeStruct((S5_LOG_TB,) + gp, F32),
                   jax.ShapeDtypeStruct(b_re2.shape, F32), jax.ShapeDtypeStruct(b_re2.shape, F32)),
    )(a_re, a_im, log_step, b_re2, b_im2, expand)


def _s5_params_bwd(a_re, a_im, log_step, b_re2, b_im2, expand, d_lr, d_li, d_bbr, d_bbi):
    def body(ar, ai, ls, br, bi, ex, glr, gli, gbr, gbi, dar, dai, dls, dbr, dbi):
        _, vjp = jax.vjp(lambda *p: _s5_disc(*p, ex[...]), ar[...], ai[...], ls[...], br[...], bi[...])
        g = vjp((glr[...], gli[...], gbr[...], gbi[...]))
        for ref, val in zip((dar, dai, dls, dbr, dbi), g, strict=True):
            ref[...] = val

    return pl.pallas_call(
        body, name="s5_params_bwd",
        out_shape=tuple(jax.ShapeDtypeStruct(v.shape, F32) for v in (a_re, a_im, log_step, b_re2, b_im2)),
    )(a_re, a_im, log_step, b_re2, b_im2, expand, d_lr, d_li, d_bbr, d_bbi)


def _scan_block(x_ref, out_ref, lp, edge_pow, cr, ci, reverse, each=None):
    n_g = x_ref.shape[0] // 8
    sub = lax.broadcasted_iota(jnp.int32, (8, S5_LB), 0)
    steps = []
    for sh in (1, 2, 4):
        keep = (sub < 8 - sh) if reverse else (sub >= sh)
        steps.append((8 - sh if reverse else sh, jnp.where(keep, lp[sh - 1:sh, :S5_LB], 0.0),
                      jnp.where(keep, lp[sh - 1:sh, S5_LB:], 0.0)))
    e_r, e_i = edge_pow[:, :S5_LB], edge_pow[:, S5_LB:]
    for r in (range(n_g - 1, -1, -1) if reverse else range(n_g)):
        rows = slice(8 * r, 8 * r + 8)
        xr, xi = x_ref[rows, :S5_LB], x_ref[rows, S5_LB:]
        for by, a_r, a_i in steps:
            pr, pi = pltpu.roll(xr, by, 0), pltpu.roll(xi, by, 0)
            xr, xi = xr + a_r * pr - a_i * pi, xi + a_r * pi + a_i * pr
        xr = xr + e_r * cr - e_i * ci
        xi = xi + e_r * ci + e_i * cr
        out_ref[rows, :S5_LB] = xr
        out_ref[rows, S5_LB:] = xi
        if each is not None:
            each(r, xr, xi)
        cr, ci = (xr[0:1, :], xi[0:1, :]) if reverse else (xr[7:8, :], xi[7:8, :])
    return cr, ci


def _s5_fwd(proj, bb_band, c_band, lam_pow, s5_d, n_seq, seq_len):
    n_t = seq_len // S5_TB
    blk = 2 * S5_LB
    n_rows = n_seq * seq_len
    u_blk0 = OFF_U5 // BAND

    def body(u_ref, bb_ref, cb_ref, lp_ref, d_ref, s_ref, y_ref, ge_ref, cr, ci, buf):
        @pl.when(pl.program_id(2) == 0)
        def _():
            cr[...] = jnp.zeros_like(cr)
            ci[...] = jnp.zeros_like(ci)

        u = u_ref[...]
        buf[...] = _dg(u, bb_ref[...], 1, 0)
        lp = lp_ref[...]
        cr[...], ci[...] = _scan_block(buf, buf, lp, lp, cr[...], ci[...], False)
        s = buf[...].astype(s_ref.dtype)
        s_ref[...] = s
        y = _dg(s, cb_ref[...], 1, 1)
        y_ref[...] = y
        ge_ref[...] = jax.nn.gelu(y + d_ref[...] * u).astype(ge_ref.dtype)

    def rows(j, b, t):
        return (b * n_t + t, j)

    return pl.pallas_call(
        body, name="s5_fwd", grid=(S5_NJ, n_seq, n_t),
        in_specs=[pl.BlockSpec((S5_TB, BAND), lambda j, b, t: (b * n_t + t, u_blk0 + j)),
                  pl.BlockSpec((BAND, blk), lambda j, b, t: (j, 0)), pl.BlockSpec((BAND, blk), lambda j, b, t: (j, 0)),
                  pl.BlockSpec((S5_LOG_TB, blk), lambda j, b, t: (0, j)), pl.BlockSpec((1, BAND), lambda j, b, t: (0, j))],
        out_specs=[pl.BlockSpec((S5_TB, blk), rows), pl.BlockSpec((S5_TB, BAND), rows), pl.BlockSpec((S5_TB, BAND), rows)],
        out_shape=[jax.ShapeDtypeStruct((n_rows, S5_NJ * blk), BF16), jax.ShapeDtypeStruct((n_rows, S5_WIDTH), F32),
                   jax.ShapeDtypeStruct((n_rows, S5_WIDTH), BF16)],
        scratch_shapes=[pltpu.VMEM((1, S5_LB), F32), pltpu.VMEM((1, S5_LB), F32), pltpu.VMEM((S5_TB, blk), F32)],
        compiler_params=_cparams(("parallel", "parallel", "arbitrary")),
    )(proj, bb_band, c_band, lam_pow, s5_d)


def _s5_bwd(dyc, s, proj, du5_a, bb_band, c_band, lam_pow_conj, dproj, n_seq, seq_len):
    n_t = seq_len // S5_TB
    blk = 2 * S5_LB
    halo_rows = 16
    halo_per_blk = S5_TB // halo_rows
    u_blk0 = OFF_U5 // BAND

    def rows(j, b, t):
        return (b * n_t + (n_t - 1 - t), j)

    def u_rows(j, b, t):
        return (b * n_t + (n_t - 1 - t), u_blk0 + j)

    def halo(j, b, t):
        return (jnp.maximum((b * n_t + (n_t - 1 - t)) * halo_per_blk - 1, 0), j)

    def body(dy_ref, sq_ref, hq_ref, u_ref, dua_ref, bb_ref, cb_ref, lp_ref, _, du_ref, dl_ref, dbb_ref, dcb_ref,
             cr, ci, buf, s_ref):
        b, t = pl.program_id(1), pl.program_id(2)
        sq = sq_ref[...]
        s_ref[...] = sq.astype(F32)
        h_last = hq_ref[...].astype(F32)[halo_rows - 1:halo_rows, :]
        dy = dy_ref[...]
        buf[...] = _dg(dy, cb_ref[...], 1, 0)

        @pl.when(t == 0)
        def _():
            cr[...] = jnp.zeros_like(cr)
            ci[...] = jnp.zeros_like(ci)

        @pl.when((b == 0) & (t == 0))
        def _():
            dl_ref[...] = jnp.zeros_like(dl_ref)
            dbb_ref[...] = jnp.zeros_like(dbb_ref)
            dcb_ref[...] = jnp.zeros_like(dcb_ref)

        dcb_ref[...] += _dg(dy, sq, 0, 0)

        first_blk = t == n_t - 1
        sub = lax.broadcasted_iota(jnp.int32, (8, S5_LB), 0)
        acc = [jnp.zeros((8, S5_LB), F32), jnp.zeros((8, S5_LB), F32)]

        def each(r, gr, gi):
            rows = slice(8 * r, 8 * r + 8)
            if r == 0:
                before_r = jnp.where(first_blk, 0.0, h_last[:, :S5_LB])
                before_i = jnp.where(first_blk, 0.0, h_last[:, S5_LB:])
            else:
                before_r, before_i = s_ref[8 * r - 1:8 * r, :S5_LB], s_ref[8 * r - 1:8 * r, S5_LB:]
            sp_r = jnp.where(sub == 0, before_r, pltpu.roll(s_ref[rows, :S5_LB], 1, 0))
            sp_i = jnp.where(sub == 0, before_i, pltpu.roll(s_ref[rows, S5_LB:], 1, 0))
            acc[0] = acc[0] + gr * sp_r + gi * sp_i
            acc[1] = acc[1] + gi * sp_r - gr * sp_i

        lp = lp_ref[...]
        edge_pow = jnp.concatenate([lp[7 - i:8 - i, :] for i in range(8)], axis=0)
        cr[...], ci[...] = _scan_block(buf, buf, lp, edge_pow, cr[...], ci[...], True, each)
        g = buf[...].astype(BF16)
        du_ref[...] = (dua_ref[...] + _dg(g, bb_ref[...], 1, 1)).astype(du_ref.dtype)
        dbb_ref[...] += _dg(u_ref[...], g, 0, 0)
        dl_ref[:, :S5_LB] += jnp.sum(acc[0], axis=0, keepdims=True)
        dl_ref[:, S5_LB:] += jnp.sum(acc[1], axis=0, keepdims=True)

    band = pl.BlockSpec((BAND, blk), lambda j, b, t: (j, 0))
    return pl.pallas_call(
        body, name="s5_bwd", grid=(S5_NJ, n_seq, n_t),
        in_specs=[pl.BlockSpec((S5_TB, BAND), rows), pl.BlockSpec((S5_TB, blk), rows),
                  pl.BlockSpec((halo_rows, blk), halo), pl.BlockSpec((S5_TB, BAND), u_rows),
                  pl.BlockSpec((S5_TB, BAND), rows), band, band,
                  pl.BlockSpec((S5_LOG_TB, blk), lambda j, b, t: (0, j)), pl.BlockSpec(memory_space=pl.ANY)],
        out_specs=[pl.BlockSpec((S5_TB, BAND), u_rows), pl.BlockSpec((1, blk), lambda j, b, t: (0, j)), band, band],
        out_shape=[jax.ShapeDtypeStruct(dproj.shape, dproj.dtype), jax.ShapeDtypeStruct((1, S5_NJ * blk), F32),
                   jax.ShapeDtypeStruct(bb_band.shape, F32), jax.ShapeDtypeStruct(c_band.shape, F32)],
        input_output_aliases={8: 0},
        scratch_shapes=[pltpu.VMEM((1, S5_LB), F32), pltpu.VMEM((1, S5_LB), F32), pltpu.VMEM((S5_TB, blk), F32),
                        pltpu.VMEM((S5_TB, blk), F32)],
        compiler_params=_cparams(("parallel", "arbitrary", "arbitrary")),
    )(dyc, s, s, proj, du5_a, bb_band, c_band, lam_pow_conj, dproj)


CONV_TR = 512
CONV_CW = 512


def _shift_down(x, halo, k):
    if k == 0:
        return x
    row8 = lax.broadcasted_iota(jnp.int32, halo.shape, 0)
    rolled = pltpu.roll(x, k, 0)
    top = jnp.where(row8 < k, pltpu.roll(halo, k, 0), rolled[:8])
    if x.shape[0] == 8:
        return top
    return jnp.concatenate([top, rolled[8:]], axis=0)


def _shift_up(x, halo, k):
    if k == 0:
        return x
    n = x.shape[0]
    row8 = lax.broadcasted_iota(jnp.int32, halo.shape, 0)
    rolled = pltpu.roll(x, n - k, 0)
    bot = jnp.where(row8 >= 8 - k, pltpu.roll(halo, 8 - k, 0), rolled[n - 8:])
    if n == 8:
        return bot
    return jnp.concatenate([rolled[:n - 8], bot], axis=0)


def _conv_pre(x, halo, w, b):
    acc = b + w[SSD_CONV - 1:SSD_CONV, :] * x
    for k in range(SSD_CONV - 1):
        acc = acc + w[k:k + 1, :] * _shift_down(x, halo, SSD_CONV - 1 - k)
    return acc


def _conv_specs(seq_len, col_off):
    lt = seq_len // CONV_TR
    cb = col_off // CONV_CW
    cur = pl.BlockSpec((CONV_TR, CONV_CW), lambda j, i: (i, j + cb))
    prev = pl.BlockSpec((8, CONV_CW), lambda j, i: (jnp.maximum(i * (CONV_TR // 8) - 1, 0), j + cb))
    return lt, cur, prev


def _conv_fwd(proj, conv_w, conv_b, n_rows, seq_len):
    lt, cur, prev = _conv_specs(seq_len, OFF_XBC)

    def body(x_ref, h_ref, w_ref, b_ref, o_ref):
        halo = jnp.where(pl.program_id(1) % lt == 0, 0.0, h_ref[...])
        o_ref[...] = jax.nn.silu(_conv_pre(x_ref[...], halo, w_ref[...], b_ref[...]))

    return pl.pallas_call(
        body, name="ssd_conv_fwd", grid=(SSD_CONV_DIM // CONV_CW, n_rows // CONV_TR),
        in_specs=[cur, prev, pl.BlockSpec((SSD_CONV, CONV_CW), lambda j, i: (0, j)),
                  pl.BlockSpec((1, CONV_CW), lambda j, i: (0, j))],
        out_specs=pl.BlockSpec((CONV_TR, CONV_CW), lambda j, i: (i, j)),
        out_shape=jax.ShapeDtypeStruct((n_rows, SSD_CONV_DIM), F32),
        compiler_params=_cparams(("parallel", "parallel")),
    )(proj, proj, conv_w, conv_b)


def _conv_bwd(name, proj, d_act, conv_w, conv_b, n_rows, seq_len, col_off, dproj):
    width = d_act.shape[1]
    lt, cur, prev = _conv_specs(seq_len, OFF_XBC + col_off)
    n_blk = n_rows // CONV_TR
    cb = (OFF_XBC + col_off) // CONV_CW
    pb = col_off // CONV_CW
    nxt = pl.BlockSpec((8, CONV_CW), lambda j, i: (jnp.minimum((i + 1) * (CONV_TR // 8), n_rows // 8 - 1), j + cb))
    d_cur = pl.BlockSpec((CONV_TR, CONV_CW), lambda j, i: (i, j))
    d_nxt = pl.BlockSpec((8, CONV_CW), lambda j, i: (jnp.minimum((i + 1) * (CONV_TR // 8), n_rows // 8 - 1), j))

    def dsilu(pre):
        sg = jax.nn.sigmoid(pre)
        return sg * (1.0 + pre * (1.0 - sg))

    def body(x_ref, hp_ref, hn_ref, d_ref, dn_ref, w_ref, b_ref, _, dx_ref, dw_ref, db_ref):
        i = pl.program_id(1)
        x, w, b = x_ref[...], w_ref[...], b_ref[...]
        halo_p = jnp.where(i % lt == 0, 0.0, hp_ref[...])
        at_end = i % lt == lt - 1
        dpre = d_ref[...] * dsilu(_conv_pre(x, halo_p, w, b))
        pre_n = _conv_pre(hn_ref[...], x[CONV_TR - 8:, :], w, b)
        dpre_n = jnp.where(at_end, 0.0, dn_ref[...] * dsilu(pre_n))
        dx = w[SSD_CONV - 1:SSD_CONV, :] * dpre
        for k in range(SSD_CONV - 1):
            dx = dx + w[k:k + 1, :] * _shift_up(dpre, dpre_n, SSD_CONV - 1 - k)
        dx_ref[...] = dx.astype(dx_ref.dtype)

        @pl.when(i == 0)
        def _():
            dw_ref[...] = jnp.zeros_like(dw_ref)
            db_ref[...] = jnp.zeros_like(db_ref)

        for k in range(SSD_CONV):
            xs = _shift_down(x, halo_p, SSD_CONV - 1 - k)
            dw_ref[k:k + 1, :] += jnp.sum(dpre * xs, axis=0, keepdims=True)
        db_ref[...] += jnp.sum(dpre, axis=0, keepdims=True)

    return pl.pallas_call(
        body, name=name, grid=(width // CONV_CW, n_blk),
        in_specs=[cur, prev, nxt, d_cur, d_nxt,
                  pl.BlockSpec((SSD_CONV, CONV_CW), lambda j, i: (0, j + pb)),
                  pl.BlockSpec((1, CONV_CW), lambda j, i: (0, j + pb)), pl.BlockSpec(memory_space=pl.ANY)],
        out_specs=[pl.BlockSpec((CONV_TR, CONV_CW), lambda j, i: (i, j + cb)),
                   pl.BlockSpec((SSD_CONV, CONV_CW), lambda j, i: (0, j)),
                   pl.BlockSpec((1, CONV_CW), lambda j, i: (0, j))],
        out_shape=[jax.ShapeDtypeStruct(dproj.shape, dproj.dtype),
                   jax.ShapeDtypeStruct((SSD_CONV, width), F32), jax.ShapeDtypeStruct((1, width), F32)],
        input_output_aliases={7: 0},
        compiler_params=_cparams(("parallel", "arbitrary")),
    )(proj, proj, proj, d_act, d_act, conv_w, conv_b, dproj)


def _split3(x):
    hi = x.astype(BF16)
    r = x - hi.astype(F32)
    mid = r.astype(BF16)
    return hi, mid, (r - mid.astype(F32)).astype(BF16)


def _sel_dot(a, b, a_is_sel):
    dn = (((1,), (0,)), ((), ()))
    if a_is_sel:
        return sum(lax.dot_general(a, t, dn, preferred_element_type=F32) for t in _split3(b))
    return sum(lax.dot_general(t, b, dn, preferred_element_type=F32) for t in _split3(a))


@jax.custom_vjp
def sel_left(sel, sel_t, x):
    return _sel_dot(sel, x, True)


@jax.custom_vjp
def sel_right(x, sel, sel_t):
    return _sel_dot(x, sel, False)


sel_left.defvjp(lambda s, st, x: (_sel_dot(s, x, True), (s, st)),
                lambda r, g: (jnp.zeros_like(r[0]), jnp.zeros_like(r[1]), _sel_dot(r[1], g, True)))
sel_right.defvjp(lambda x, s, st: (_sel_dot(x, s, False), (s, st)),
                 lambda r, g: (_sel_dot(g, r[1], False), jnp.zeros_like(r[0]), jnp.zeros_like(r[1])))


def _ssd_chunk(xs, bm, cm, dtr, st, dtb, alog, dsk, k):
    dt = jax.nn.softplus(dtr + dtb)
    acum = sel_left(k["tri"], k["tri_t"], dt * (-jnp.exp(alog)))
    dt_e = sel_right(dt, k["spread"], k["spread_t"])
    ac_e = sel_right(acum, k["spread"], k["spread_t"])
    al_e = ac_e[SSD_CHUNK - 1:SSD_CHUNK, :]
    dsk_e = sel_right(jnp.broadcast_to(dsk, (8, 128)), k["spread"], k["spread_t"])[0:1, :]
    xdt = xs * dt_e
    acum_t = acum.T
    scores = dot_nt(cm, bm)
    y = dot_nn(cm, st) * jnp.exp(ac_e) + xs * dsk_e
    for j in range(SSD_HPG):
        lmat = jnp.exp(jnp.where(k["causal"], acum[:, j:j + 1] - acum_t[j:j + 1, :], -jnp.inf))
        y = y + dot_nn(scores * lmat, jnp.where(k["head"] == j, xdt, 0.0))
    new = st * jnp.exp(al_e) + dot_tn(bm, xdt * jnp.exp(al_e - ac_e))
    return y, new


def _ssd_consts():
    r = lax.broadcasted_iota(jnp.int32, (SSD_CHUNK, SSD_CHUNK), 0)
    c = lax.broadcasted_iota(jnp.int32, (SSD_CHUNK, SSD_CHUNK), 1)
    hd = jnp.int32(SSD_HEADDIM)
    sr = lax.broadcasted_iota(jnp.int32, (128, GROUP_W), 0)
    sc = lax.div(lax.broadcasted_iota(jnp.int32, (128, GROUP_W), 1), hd)
    tr = lax.div(lax.broadcasted_iota(jnp.int32, (GROUP_W, 128), 0), hd)
    tc = lax.broadcasted_iota(jnp.int32, (GROUP_W, 128), 1)
    return {"tri": (r >= c).astype(BF16), "tri_t": (c >= r).astype(BF16), "causal": r >= c,
            "spread": (sr == sc).astype(BF16), "spread_t": (tr == tc).astype(BF16),
            "head": lax.div(lax.broadcasted_iota(jnp.int32, (SSD_CHUNK, GROUP_W), 1), hd)}


def _ssd_specs(n_c, reverse):
    def cidx(c):
        return n_c - 1 - c if reverse else c

    xs = pl.BlockSpec((SSD_CHUNK, GROUP_W), lambda g, b, c: (b * n_c + cidx(c), g))
    bm = pl.BlockSpec((SSD_CHUNK, SSD_STATE), lambda g, b, c: (b * n_c + cidx(c), SSD_WIDTH // SSD_STATE + g))
    cm = pl.BlockSpec((SSD_CHUNK, SSD_STATE), lambda g, b, c: (b * n_c + cidx(c), (SSD_WIDTH + SSD_BC) // SSD_STATE + g))
    dt = pl.BlockSpec((SSD_CHUNK, 128), lambda g, b, c: (b * n_c + cidx(c), OFF_DT // 128 + g))
    par = pl.BlockSpec((1, 128), lambda g, b, c: (0, g))
    st = pl.BlockSpec((1, 1, 1, SSD_STATE, GROUP_W), lambda g, b, c: (b, cidx(c), g, 0, 0))
    return xs, bm, cm, dt, par, st


def _ssd_chunk_gated(xs, bm, cm, dtr, st, dtb, alog, dsk, z, nw, k):
    y, new = _ssd_chunk(xs, bm, cm, dtr, st, dtb, alog, dsk, k)
    yg = y * jax.nn.silu(z)
    return yg * lax.rsqrt(jnp.mean(yg * yg, axis=-1, keepdims=True) + RMS_EPS) * nw, new


def _ssd_gate_specs(n_c, reverse):
    def cidx(c):
        return n_c - 1 - c if reverse else c

    z = pl.BlockSpec((SSD_CHUNK, GROUP_W), lambda g, b, c: (b * n_c + cidx(c), OFF_ZS // GROUP_W + g))
    nw = pl.BlockSpec((1, GROUP_W), lambda g, b, c: (0, g))
    return z, nw


def _ssd_fwd(xbc_act, proj, dtb, alog, dsk, norm_w, n_seq, seq_len):
    n_c = seq_len // SSD_CHUNK
    xs_s, bm_s, cm_s, dt_s, par_s, st_s = _ssd_specs(n_c, False)
    z_s, nw_s = _ssd_gate_specs(n_c, False)

    def body(xs_ref, bm_ref, cm_ref, dt_ref, dtb_ref, al_ref, dk_ref, z_ref, nw_ref, y_ref, st_ref, state):
        @pl.when(pl.program_id(2) == 0)
        def _():
            state[...] = jnp.zeros_like(state)

        prev = state[...]
        st_ref[0, 0, 0] = prev
        y, new = _ssd_chunk_gated(xs_ref[...], bm_ref[...], cm_ref[...], dt_ref[...], prev,
                                  dtb_ref[...], al_ref[...], dk_ref[...], z_ref[...], nw_ref[...], _ssd_consts())
        y_ref[...] = y.astype(y_ref.dtype)
        state[...] = new

    return pl.pallas_call(
        body, name="ssd_fwd", grid=(SSD_GROUPS, n_seq, n_c),
        in_specs=[xs_s, bm_s, cm_s, dt_s, par_s, par_s, par_s, z_s, nw_s],
        out_specs=[pl.BlockSpec((SSD_CHUNK, GROUP_W), lambda g, b, c: (b * n_c + c, g)), st_s],
        out_shape=[jax.ShapeDtypeStruct((n_seq * seq_len, SSD_WIDTH), BF16),
                   jax.ShapeDtypeStruct((n_seq, n_c, SSD_GROUPS, SSD_STATE, GROUP_W), F32)],
        scratch_shapes=[pltpu.VMEM((SSD_STATE, GROUP_W), F32)],
        compiler_params=_cparams(("parallel", "parallel", "arbitrary")),
    )(xbc_act, xbc_act, xbc_act, proj, dtb, alog, dsk, proj, norm_w)


def _ssd_bwd(xbc_act, proj, states, dy, dtb, alog, dsk, norm_w, n_seq, seq_len, dproj):
    n_c = seq_len // SSD_CHUNK
    n_rows = n_seq * seq_len
    xs_s, bm_s, cm_s, dt_s, par_s, st_s = _ssd_specs(n_c, True)
    z_s, nw_s = _ssd_gate_specs(n_c, True)

    def rows(w, first=0):
        return pl.BlockSpec((SSD_CHUNK, w), lambda g, b, c: (b * n_c + (n_c - 1 - c), first + g))

    def body(xs_ref, bm_ref, cm_ref, dt_ref, st_ref, dy_ref, dtb_ref, al_ref, dk_ref, z_ref, nw_ref, _,
             dxs_ref, dbm_ref, dcm_ref, dz_ref, ddt_ref, ddtb_ref, dal_ref, ddk_ref, dnw_ref, dstate):
        b, c = pl.program_id(1), pl.program_id(2)

        @pl.when(c == 0)
        def _():
            dstate[...] = jnp.zeros_like(dstate)

        @pl.when((b == 0) & (c == 0))
        def _():
            for ref in (ddtb_ref, dal_ref, ddk_ref, dnw_ref):
                ref[...] = jnp.zeros_like(ref)

        consts = _ssd_consts()
        _, vjp = jax.vjp(
            lambda *a: _ssd_chunk_gated(*a, consts),
            xs_ref[...], bm_ref[...], cm_ref[...], dt_ref[...], st_ref[0, 0, 0], dtb_ref[...], al_ref[...], dk_ref[...],
            z_ref[...], nw_ref[...])
        dxs, dbm, dcm, ddtr, dprev, ddtb, dal, ddk, dz, dnw = vjp((dy_ref[...], dstate[...]))
        dxs_ref[...] = dxs
        dbm_ref[...] = dbm
        dcm_ref[...] = dcm
        dz_ref[...] = dz.astype(dz_ref.dtype)
        ddt_ref[...] = ddtr.astype(ddt_ref.dtype)
        ddtb_ref[...] += ddtb
        dal_ref[...] += dal
        ddk_ref[...] += ddk
        dnw_ref[...] += dnw
        dstate[...] = dprev

    acc = pl.BlockSpec((1, 128), lambda g, b, c: (0, g))
    return pl.pallas_call(
        body, name="ssd_bwd", grid=(SSD_GROUPS, n_seq, n_c),
        in_specs=[xs_s, bm_s, cm_s, dt_s, st_s, rows(GROUP_W), par_s, par_s, par_s, z_s, nw_s,
                  pl.BlockSpec(memory_space=pl.ANY)],
        out_specs=[rows(GROUP_W), rows(SSD_STATE), rows(SSD_STATE), rows(GROUP_W, OFF_ZS // GROUP_W), rows(128),
                   acc, acc, acc, pl.BlockSpec((1, GROUP_W), lambda g, b, c: (0, g))],
        out_shape=[jax.ShapeDtypeStruct((n_rows, SSD_WIDTH), F32), jax.ShapeDtypeStruct((n_rows, SSD_BC), F32),
                   jax.ShapeDtypeStruct((n_rows, SSD_BC), F32), jax.ShapeDtypeStruct(dproj.shape, dproj.dtype),
                   jax.ShapeDtypeStruct((n_rows, DT_W), BF16),
                   jax.ShapeDtypeStruct((1, 512), F32), jax.ShapeDtypeStruct((1, 512), F32),
                   jax.ShapeDtypeStruct((1, 512), F32), jax.ShapeDtypeStruct((1, SSD_WIDTH), F32)],
        input_output_aliases={11: 3},
        scratch_shapes=[pltpu.VMEM((SSD_STATE, GROUP_W), F32)],
        compiler_params=_cparams(("parallel", "arbitrary", "arbitrary")),
    )(xbc_act, xbc_act, xbc_act, proj, states, dy, dtb, alog, dsk, proj, norm_w, dproj)


def _pad_heads(v):
    return jnp.pad(v.reshape(SSD_GROUPS, SSD_HPG), ((0, 0), (0, 128 - SSD_HPG))).reshape(1, SSD_GROUPS * 128)


def _unpad_heads(v):
    return v.reshape(SSD_GROUPS, 128)[:, :SSD_HPG].reshape(1, SSD_HEADS)


def _state_cols(v):
    re, im = v
    lead = re.shape[:-1]
    re = re.reshape(lead + (S5_NJ, 1, S5_LB))
    im = im.reshape(lead + (S5_NJ, 1, S5_LB))
    return jnp.concatenate([re, im], axis=-2).reshape(lead + (2 * S5_N,))


def _state_uncols(v):
    lead = v.shape[:-1]
    v = v.reshape(lead + (S5_NJ, 2, S5_LB))
    return v[..., 0, :].reshape(lead + (S5_N,)), v[..., 1, :].reshape(lead + (S5_N,))


GROUPS_PER_BAND = BAND // S5_GROUP


def _band(w2_re, w2_im):
    gh = S5_GROUPS * S5_GROUP
    rg = ((jnp.arange(gh) // S5_GROUP) % GROUPS_PER_BAND)[:, None, None]
    cg = jnp.arange(GROUPS_PER_BAND)[None, :, None]
    parts = [jnp.where(rg == cg, v[:, None, :], 0.0).reshape(gh, S5_LB) for v in (w2_re, w2_im)]
    return jnp.concatenate(parts, axis=1)


def _band_take(wb):
    gh = S5_GROUPS * S5_GROUP
    w4 = wb.reshape(gh, 2, GROUPS_PER_BAND, S5_STATE)
    sel = w4[jnp.arange(gh), :, (jnp.arange(gh) // S5_GROUP) % GROUPS_PER_BAND, :]
    return sel[:, 0, :], sel[:, 1, :]


W_IN_SHARD = IN_PROJ_DIM // N_CHIPS
W_IN_SEGS = ((0, 512, OFF_U5), (512, 1024, OFF_Z5), (1024, 2560, OFF_ZS), (2560, 5120, OFF_XBC), (5144, 7192, OFF_G5))
DT_ROWS = (5120, 5144)


def _w_in_pieces():
    runs = []
    segs = list(W_IN_SEGS) + [(DT_ROWS[0] + SSD_HPG * g, DT_ROWS[0] + SSD_HPG * (g + 1), OFF_DT + 128 * g)
                              for g in range(SSD_GROUPS)]
    for lo, hi, off in segs:
        for j in range(N_CHIPS):
            s, e = max(lo, j * W_IN_SHARD), min(hi, (j + 1) * W_IN_SHARD)
            if s < e:
                runs.append((j, s - j * W_IN_SHARD, off + s - lo, e - s))
    return runs


RELAYOUT_LANES = 256


def _pad_w_in_t(a4):
    runs = _w_in_pieces()

    def body(a_ref, o_ref):
        o_ref[pl.ds(OFF_DT, DT_W), :] = jnp.zeros((DT_W, RELAYOUT_LANES), o_ref.dtype)
        for j, src, dst, n in runs:
            o_ref[pl.ds(dst, n), :] = a_ref[j, pl.ds(src, n), :]

    return pl.pallas_call(
        body, name="w_in_to_padded", grid=(D_MODEL // RELAYOUT_LANES,),
        in_specs=[pl.BlockSpec((N_CHIPS, W_IN_SHARD, RELAYOUT_LANES), lambda i: (0, 0, i))],
        out_specs=pl.BlockSpec((PROJ_W, RELAYOUT_LANES), lambda i: (0, i)),
        out_shape=jax.ShapeDtypeStruct((PROJ_W, D_MODEL), a4.dtype),
        compiler_params=_cparams(("parallel",)),
    )(a4)


def _unpad_w_in_t(wp):
    runs = _w_in_pieces()

    def body(p_ref, o_ref):
        for j, dst, src, n in runs:
            o_ref[j, pl.ds(dst, n), :] = p_ref[pl.ds(src, n), :]

    return pl.pallas_call(
        body, name="w_in_from_padded", grid=(D_MODEL // RELAYOUT_LANES,),
        in_specs=[pl.BlockSpec((PROJ_W, RELAYOUT_LANES), lambda i: (0, i))],
        out_specs=pl.BlockSpec((N_CHIPS, W_IN_SHARD, RELAYOUT_LANES), lambda i: (0, 0, i)),
        out_shape=jax.ShapeDtypeStruct((N_CHIPS, W_IN_SHARD, D_MODEL), wp.dtype),
        compiler_params=_cparams(("parallel",)),
    )(wp)


def _local_step(x, p, tgt, w):
    n_seq, seq_len, _ = x.shape
    n_rows = n_seq * seq_len
    tr = 512
    x2 = x.reshape(n_rows, D_MODEL)
    p2 = p.reshape(n_rows, PLE_DIM)
    t2 = tgt.reshape(n_rows, D_MODEL)
    row = functools.partial(_rowwise, n_rows=n_rows, tr=tr)

    w_pad_t = _pad_w_in_t(w["w_in_t"])
    norm_w = w["norm_w"].reshape(1, D_MODEL)
    ple_norm_w = w["ple_norm_w"].reshape(1, D_MODEL)
    final_w = w["final_norm_w"].reshape(1, D_MODEL)
    s5_d = w["s5_d"].reshape(1, S5_WIDTH)
    b_glu = w["s5_b_glu"].reshape(1, S5_WIDTH)
    conv_w = w["ssd_conv_w"].reshape(SSD_CONV, SSD_CONV_DIM)
    conv_b = w["ssd_conv_b"].reshape(1, SSD_CONV_DIM)
    ssd_norm_w = w["ssd_norm_w"].reshape(1, SSD_WIDTH)
    dtb, alog, dsk = (_pad_heads(w[k].reshape(1, SSD_HEADS)) for k in ("ssd_dt_bias", "ssd_a_log", "ssd_d"))

    gh = S5_GROUPS * S5_GROUP
    a_re = w["s5_a_re"].reshape(S5_GROUPS, S5_STATE)
    a_im = w["s5_a_im"].reshape(S5_GROUPS, S5_STATE)
    log_step = w["s5_log_step"].reshape(S5_GROUPS, 1)
    b_re2 = jnp.transpose(w["s5_b_re"].reshape(S5_GROUPS, S5_STATE, S5_GROUP), (0, 2, 1)).reshape(gh, S5_STATE)
    b_im2 = jnp.transpose(w["s5_b_im"].reshape(S5_GROUPS, S5_STATE, S5_GROUP), (0, 2, 1)).reshape(gh, S5_STATE)
    expand = (jnp.arange(gh)[:, None] // S5_GROUP == jnp.arange(S5_GROUPS)[None, :]).astype(F32)
    pow_re, pow_im, bb_re2, bb_im2 = _s5_params_fwd(a_re, a_im, log_step, b_re2, b_im2, expand)
    lam_pow = _state_cols((pow_re.reshape(S5_LOG_TB, S5_N), pow_im.reshape(S5_LOG_TB, S5_N)))
    lam_pow_conj = _state_cols((pow_re.reshape(S5_LOG_TB, S5_N), -pow_im.reshape(S5_LOG_TB, S5_N)))
    bb_band = _band(bb_re2, bb_im2).astype(BF16)
    c_band = _band(w["s5_c_re"].reshape(gh, S5_STATE), -w["s5_c_im"].reshape(gh, S5_STATE)).astype(BF16)

    late = w.get("_late")
    (hn,) = row("rms_in", lambda r, q: ([_rms(r[0], q[0])], []), row_ins=[(x2, 0, D_MODEL)],
                par_ins=[(norm_w, 0, D_MODEL)] + ([(late[0], 0, 128)] if late else []), row_outs=[(D_MODEL, BF16)])
    proj = _matmul("mm_proj", hn, w_pad_t, tb=True)
    s, yc, ge = _s5_fwd(proj, bb_band, c_band, lam_pow, s5_d, n_seq, seq_len)
    if late:
        w = {**w, **late[1](ge)}
    tg, y5 = _matmul("mm_s5_glu", ge, w["s5_w_glu"], epilogue=lambda c, r, q: [c, _s5_out(r[0], r[1], r[2], c, *q)],
                     epi_rows=[(yc, 0), (proj, OFF_U5), (proj, OFF_Z5)], epi_pars=[s5_d, b_glu], epi_outs=[F32, BF16])
    s5_rows = [(yc, 0, S5_WIDTH), (proj, OFF_U5, S5_WIDTH), (proj, OFF_Z5, S5_WIDTH), (tg, 0, S5_WIDTH)]
    s5_pars = [(s5_d, 0, S5_WIDTH), (b_glu, 0, S5_WIDTH)]

    xbc_act = _conv_fwd(proj, conv_w, conv_b, n_rows, seq_len)
    yss, states = _ssd_fwd(xbc_act, proj, dtb, alog, dsk, ssd_norm_w, n_seq, seq_len)

    m5 = _matmul("mm_br_s5", y5, w["w_br_s5"])
    ms, merged = _matmul("mm_br_ssd", yss, w["w_br_ssd"], epilogue=lambda c, r, q: [c, _merge(r[0], r[1], r[2], c)],
                         epi_rows=[(proj, OFF_G5), (proj, OFF_GS), (m5, 0)], epi_outs=[F32, BF16])
    mg_rows = [(proj, OFF_G5, D_MODEL), (proj, OFF_GS, D_MODEL), (m5, 0, D_MODEL), (ms, 0, D_MODEL)]

    def resid_norm(c, r, q):
        h1_ = r[0] + c
        return [h1_, _rms(h1_, q[0])]

    h1, hp = _matmul("mm_out", merged, w["w_out"], epilogue=resid_norm, epi_rows=[(x2, 0)], epi_pars=[ple_norm_w],
                     epi_outs=[F32, BF16], full_rows=True)
    pp = _matmul("mm_ple_proj", p2, w["w_ple_proj"])

    def head_fn(pgl_, r, q):
        h1_, pp_, tgt_ = r
        loss, vjp = jax.vjp(lambda a, b, c, f: _head_loss(a, b, c, f, tgt_), h1_, pgl_, pp_, q[0])
        dh1_, dpgl_, dpp_, dfw_ = vjp(jnp.ones_like(loss))
        return [dh1_, dpgl_, dpp_], [loss, dfw_]

    dh2, dpgl, dpp, loss_acc, d_final_w = _matmul(
        "mm_ple_gate_head", hp, w["w_ple_gate"], epilogue=head_fn, epi_rows=[(h1, 0), (pp, 0), (t2, 0)],
        epi_pars=[final_w], epi_outs=[F32, BF16, BF16], epi_accs=[(1, 128), (1, D_MODEL)], full_rows=True)
    loss = loss_acc[0, 0]

    g = {}
    g["final_norm_w"] = d_final_w
    g["w_ple_gate"] = _matmul("mm_d_w_ple_gate", hp, dpgl, ta=True)
    g["w_ple_proj"] = _matmul("mm_d_w_ple_proj", p2, dpp, ta=True)

    def ple_norm_bwd(dhp_, r, q):
        h1_, dh2_ = r
        _, vjp = jax.vjp(_rms, h1_, q[0])
        dh, dw = vjp(dhp_)
        dh = dh + dh2_
        return [dh, dh], [dw]

    dh1, dh1_b, g["ple_norm_w"] = _matmul(
        "mm_d_hp", dpgl, w["w_ple_gate"], tb=True, epilogue=ple_norm_bwd, epi_rows=[(h1, 0), (dh2, 0)],
        epi_pars=[ple_norm_w], epi_outs=[F32, BF16], epi_accs=[(1, D_MODEL)], full_rows=True)
    g["w_out"] = _matmul("mm_d_w_out", merged, dh1_b, ta=True)

    dproj = lax.empty((n_rows, PROJ_W), BF16)

    def merge_bwd(dmerged, r, q):
        sg5, sgs = jax.nn.sigmoid(r[0]), jax.nn.sigmoid(r[1])
        d_gates = jnp.concatenate([dmerged * r[2] * sg5 * (1.0 - sg5), dmerged * r[3] * sgs * (1.0 - sgs)], axis=1)
        return [d_gates, dmerged * sg5, dmerged * sgs]

    dproj, dm5, dms = _matmul(
        "mm_d_merged", dh1_b, w["w_out"], tb=True, epilogue=merge_bwd,
        epi_rows=[(proj, OFF_G5), (proj, OFF_GS), (m5, 0), (ms, 0)], epi_outs=[BF16, BF16, BF16], full_rows=True,
        epi_into=(dproj, OFF_G5, 2 * D_MODEL))
    g["w_br_s5"] = _matmul("mm_d_w_br_s5", y5, dm5, ta=True)
    g["w_br_ssd"] = _matmul("mm_d_w_br_ssd", yss, dms, ta=True)
    dy5 = _matmul("mm_d_y5", dm5, w["w_br_s5"], tb=True)
    dyss = _matmul("mm_d_yss", dms, w["w_br_ssd"], tb=True)

    def s5_out_bwd_a(r, q):
        yc_, u_, z_, t_, dy_ = r
        d_, bg_ = q
        ge_ = jax.nn.gelu(yc_ + d_ * u_)
        _, vjp = jax.vjp(lambda a, z, t, b: a * jax.nn.sigmoid(t + b) * jax.nn.silu(z), ge_, z_, t_, bg_)
        dge, dz, dt_, dbg = vjp(dy_)
        return [dz, dge, dt_], [dbg]

    dproj, dge_a, dtg, g["s5_b_glu"] = row(
        "s5_out_bwd_a", s5_out_bwd_a, row_ins=s5_rows + [(dy5, 0, S5_WIDTH)], par_ins=s5_pars,
        row_outs=[(S5_WIDTH, BF16), (S5_WIDTH, F32), (S5_WIDTH, BF16)], acc_outs=[(1, S5_WIDTH)],
        into=(dproj, OFF_Z5))
    g["s5_w_glu"] = _matmul("mm_d_w_glu", ge, dtg, ta=True)
    early = w["_early"](g) if "_early" in w else None
    dge_b = _matmul("mm_d_ge", dtg, w["s5_w_glu"], tb=True)

    def s5_out_bwd_b(r, q):
        yc_, u_, da_, db_ = r
        _, vjp = jax.vjp(lambda yc, u, d: jax.nn.gelu(yc + d * u), yc_, u_, q[0])
        dyc_, du_, dd_ = vjp(da_ + db_)
        return [dyc_, du_], [dd_]

    dyc, du5_a, g["s5_d"] = row(
        "s5_out_bwd_b", s5_out_bwd_b,
        row_ins=[(yc, 0, S5_WIDTH), (proj, OFF_U5, S5_WIDTH), (dge_a, 0, S5_WIDTH), (dge_b, 0, S5_WIDTH)],
        par_ins=[(s5_d, 0, S5_WIDTH)] + ([(early, 0, 128)] if early is not None else []),
        row_outs=[(S5_WIDTH, BF16), (S5_WIDTH, F32)], acc_outs=[(1, S5_WIDTH)])
    dproj, d_lam, d_bb_band, d_c_band = _s5_bwd(dyc, s, proj, du5_a, bb_band, c_band, lam_pow_conj, dproj,
                                                n_seq, seq_len)

    d_lr, d_li = _state_uncols(d_lam)
    d_bbr, d_bbi = _band_take(d_bb_band)
    d_are, d_aim, d_ls, d_br2, d_bi2 = _s5_params_bwd(
        a_re, a_im, log_step, b_re2, b_im2, expand,
        d_lr.reshape(S5_GROUPS, S5_STATE), d_li.reshape(S5_GROUPS, S5_STATE), d_bbr, d_bbi)
    g["s5_a_re"], g["s5_a_im"], g["s5_log_step"] = d_are, d_aim, d_ls
    g["s5_b_re_ghp"], g["s5_b_im_ghp"] = d_br2, d_bi2
    d_cr, d_ci = _band_take(d_c_band)
    g["s5_c_re"], g["s5_c_im"] = d_cr, -d_ci

    dxs, dbm, dcm, dproj, ddt, d_dtb, d_alog, d_dsk, g["ssd_norm_w"] = _ssd_bwd(
        xbc_act, proj, states, dyss, dtb, alog, dsk, ssd_norm_w, n_seq, seq_len, dproj)
    (dproj,) = row("ssd_ddt", lambda r, q: ([r[0]], []), row_ins=[(ddt, 0, DT_W)], par_ins=[],
                   row_outs=[(DT_W, BF16)], into=(dproj, OFF_DT))
    g["ssd_dt_bias"], g["ssd_a_log"], g["ssd_d"] = _unpad_heads(d_dtb), _unpad_heads(d_alog), _unpad_heads(d_dsk)
    conv_dw, conv_db = [], []
    for nm, d_act, off in (("x", dxs, 0), ("b", dbm, SSD_WIDTH), ("c", dcm, SSD_WIDTH + SSD_BC)):
        dproj, dw_, db_ = _conv_bwd("ssd_conv_bwd_" + nm, proj, d_act, conv_w, conv_b, n_rows, seq_len, off, dproj)
        conv_dw.append(dw_)
        conv_db.append(db_)
    g["ssd_conv_w"] = jnp.concatenate(conv_dw, axis=1)
    g["ssd_conv_b"] = jnp.concatenate(conv_db, axis=1)

    g["w_in_t"] = _unpad_w_in_t(_matmul("mm_d_w_in", dproj, hn, ta=True))
    def norm_bwd(dhn_, r, q):
        x_, dh1_ = r
        _, vjp = jax.vjp(_rms, x_, q[0])
        dx_, dw_ = vjp(dhn_)
        return [dx_ + dh1_], [dw_]

    sent = w["_w_in_ready"](g, loss) if "_w_in_ready" in w else None
    dx, g["norm_w"] = _matmul("mm_d_hn", dproj, w_pad_t, epilogue=norm_bwd, epi_rows=[(x2, 0), (dh1, 0)],
                              epi_pars=[norm_w], epi_outs=[F32], epi_accs=[(1, D_MODEL)], full_rows=True, dep=sent)
    return loss, dx.reshape(x.shape), g


HBM = pl.BlockSpec(memory_space=pltpu.HBM)


def _chip_index(x, y):
    return 2 * x + y


def _half(shape2d, axis, which):
    h = shape2d[axis] // 2
    sl = pl.ds(pl.multiple_of(which * h, 128 if axis else 8), h)
    return (slice(None), sl) if axis else (sl, slice(None))


def _gather_chips(split, axes, whole):
    ns, nw = len(split), len(whole)
    n = ns + nw

    def body(*refs):
        ins, outs = refs[:n], refs[n:2 * n]
        ici_send, ici_recv, d2d_send, d2d_recv, local_sems = refs[2 * n:]
        x, y, c = lax.axis_index("x"), lax.axis_index("y"), lax.axis_index("c")
        me = _chip_index(x, y)
        sibling = (x, y, 1 - c)
        peers = [(1 - x, y), (x, 1 - y), (1 - x, 1 - y)]

        def half(t, which):
            return _half(split[t].shape, axes[t], which)

        copies = []
        for t in range(n):
            loc = pltpu.make_async_copy(ins[t], outs[t].at[me], local_sems.at[t])
            loc.start()
            copies.append(loc)

        def ici(t, k, slot):
            px, py = peers[k]
            if t < ns:
                src, dst = ins[t].at[half(t, c)], outs[t].at[(slot,) + half(t, c)]
            else:
                src, dst = ins[t], outs[t].at[slot]
            return pltpu.make_async_remote_copy(src_ref=src, dst_ref=dst, send_sem=ici_send.at[t, k],
                                                recv_sem=ici_recv.at[t, k], device_id=(px, py, c), device_id_type=MESH)

        def d2d(t, k, which):
            rows = outs[t].at[(_chip_index(*peers[k]),) + half(t, which)]
            return pltpu.make_async_remote_copy(src_ref=rows, dst_ref=rows, send_sem=d2d_send.at[t, k],
                                                recv_sem=d2d_recv.at[t, k], device_id=sibling, device_id_type=MESH)

        sends = []
        for t in range(n):
            for k in range(3):
                cp = ici(t, k, me)
                cp.start()
                sends.append(cp)
        for t in range(n):
            for k in range(3):
                ici(t, k, _chip_index(*peers[k])).wait_recv()
                if t < ns:
                    cp = d2d(t, k, c)
                    cp.start()
                    sends.append(cp)
        for t in range(ns):
            for k in range(3):
                d2d(t, k, 1 - c).wait_recv()
        for cp in sends:
            cp.wait_send()
        for cp in copies:
            cp.wait()

    arrays = list(split) + list(whole)
    return pl.pallas_call(
        body, name="gather_weights",
        in_specs=[HBM] * n, out_specs=[HBM] * n,
        out_shape=[jax.ShapeDtypeStruct((N_CHIPS,) + a.shape, a.dtype) for a in arrays],
        scratch_shapes=[pltpu.SemaphoreType.DMA((n, 3)), pltpu.SemaphoreType.DMA((n, 3)),
                        pltpu.SemaphoreType.DMA((ns, 3)), pltpu.SemaphoreType.DMA((ns, 3)),
                        pltpu.SemaphoreType.DMA((n,))],
    )(*arrays)


SEM = pl.BlockSpec(memory_space=pltpu.SEMAPHORE)
DATAFLOW = pltpu.SideEffectType.DATAFLOW_SIDE_EFFECTING


def _gather_start(shards, after):
    n = len(shards)

    def body(*refs):
        ins, lands = refs[:n], refs[n:2 * n]
        send_sems, recv_sems = refs[2 * n + 1], refs[2 * n + 2]
        token = refs[-1]
        x, y, c = lax.axis_index("x"), lax.axis_index("y"), lax.axis_index("c")
        me = _chip_index(x, y)
        for t in range(n):
            for k, (px, py) in enumerate([(1 - x, y), (x, 1 - y), (1 - x, 1 - y)]):
                pltpu.make_async_remote_copy(
                    src_ref=ins[t], dst_ref=lands[t].at[me], send_sem=send_sems.at[3 * t + k],
                    recv_sem=recv_sems.at[3 * t + k],
                    device_id=(px, py, c), device_id_type=MESH).start()
        token[...] = jnp.zeros_like(token)

    zones = [lax.empty((N_CHIPS,) + a.shape, a.dtype) for a in shards]
    res = pl.pallas_call(
        body, name="gather_rest_start",
        out_shape=(pltpu.SemaphoreType.DMA((3 * n,)), pltpu.SemaphoreType.DMA((3 * n,)),
                   *[pltpu.HBM(a.shape, a.dtype) for a in shards], *[pltpu.HBM(z.shape, z.dtype) for z in zones],
                   jax.ShapeDtypeStruct((8, 128), F32)),
        in_specs=[HBM] * (2 * n) + [pl.BlockSpec(memory_space=pl.ANY)],
        out_specs=(SEM, SEM, *[HBM] * (2 * n), pl.BlockSpec(memory_space=pltpu.VMEM)),
        input_output_aliases={t: 2 + t for t in range(2 * n)},
        compiler_params=pltpu.CompilerParams(has_side_effects=DATAFLOW),
    )(*[pltpu.with_memory_space_constraint(a, pltpu.HBM) for a in shards],
      *[pltpu.with_memory_space_constraint(z, pltpu.HBM) for z in zones], after)
    return res[0], res[1], list(res[2:2 + n]), list(res[2 + n:2 + 2 * n]), res[-1]


def _gather_wait(send_sems, recv_sems, thru, lands, after):
    n = len(thru)

    def body(*refs):
        ins, zones = refs[:n], refs[n:2 * n]
        s_sems, r_sems = refs[2 * n], refs[2 * n + 1]
        x, y, c = lax.axis_index("x"), lax.axis_index("y"), lax.axis_index("c")
        for t in range(n):
            for k, (px, py) in enumerate([(1 - x, y), (x, 1 - y), (1 - x, 1 - y)]):
                cp = pltpu.make_async_remote_copy(
                    src_ref=ins[t], dst_ref=zones[t].at[_chip_index(px, py)], send_sem=s_sems.at[3 * t + k],
                    recv_sem=r_sems.at[3 * t + k], device_id=(px, py, c), device_id_type=MESH)
                cp.wait_send()
                cp.wait_recv()

    res = pl.pallas_call(
        body, name="gather_rest_wait",
        out_shape=(*[pltpu.HBM(a.shape, a.dtype) for a in thru], *[pltpu.HBM(z.shape, z.dtype) for z in lands]),
        in_specs=[HBM] * (2 * n) + [SEM, SEM, pl.BlockSpec(memory_space=pl.ANY)], out_specs=[HBM] * (2 * n),
        input_output_aliases={t: t for t in range(2 * n)},
        compiler_params=pltpu.CompilerParams(has_side_effects=DATAFLOW),
    )(*thru, *lands, send_sems, recv_sems, after)
    return list(res[:n]), list(res[n:])


def _scatter_ends(t, k, n_slotted, ins, zones, x, y, me):
    px, py = [(1 - x, y), (x, 1 - y), (1 - x, 1 - y)][k]
    if t < n_slotted:
        return ins[t].at[_chip_index(px, py)], zones[t].at[k], (px, py)
    return ins[t], (zones[t].at[me], zones[t].at[_chip_index(px, py)]), (px, py)


def _scatter_start(name, slotted, whole, after):
    n_s = len(slotted)
    arrays = list(slotted) + list(whole)
    n = len(arrays)

    def body(*refs):
        ins, lands = refs[:n], refs[n:2 * n]
        send_sems, recv_sems = refs[2 * n + 1], refs[2 * n + 2]
        token = refs[-1]
        x, y, c = lax.axis_index("x"), lax.axis_index("y"), lax.axis_index("c")
        for t in range(n):
            for k in range(3):
                src, dst, (px, py) = _scatter_ends(t, k, n_s, ins, lands, x, y, _chip_index(x, y))
                pltpu.make_async_remote_copy(
                    src_ref=src, dst_ref=dst if t < n_s else dst[0], send_sem=send_sems.at[3 * t + k],
                    recv_sem=recv_sems.at[3 * t + k], device_id=(px, py, c), device_id_type=MESH).start()
        token[...] = jnp.zeros_like(token)

    zones = [lax.empty((3,) + a.shape[1:], a.dtype) for a in slotted]
    zones += [lax.empty((N_CHIPS,) + a.shape, a.dtype) for a in whole]
    slotted = arrays
    res = pl.pallas_call(
        body, name=name,
        out_shape=(pltpu.SemaphoreType.DMA((3 * n,)), pltpu.SemaphoreType.DMA((3 * n,)),
                   *[pltpu.HBM(a.shape, a.dtype) for a in slotted], *[pltpu.HBM(z.shape, z.dtype) for z in zones],
                   jax.ShapeDtypeStruct((8, 128), F32)),
        in_specs=[HBM] * (2 * n) + [pl.BlockSpec(memory_space=pl.ANY)],
        out_specs=(SEM, SEM, *[HBM] * (2 * n), pl.BlockSpec(memory_space=pltpu.VMEM)),
        input_output_aliases={t: 2 + t for t in range(2 * n)},
        compiler_params=pltpu.CompilerParams(has_side_effects=DATAFLOW),
    )(*[pltpu.with_memory_space_constraint(a, pltpu.HBM) for a in slotted],
      *[pltpu.with_memory_space_constraint(z, pltpu.HBM) for z in zones], after)
    return res[0], res[1], list(res[2:2 + n]), list(res[2 + n:2 + 2 * n]), res[-1]


def _scatter_wait(name, n_slotted, send_sems, recv_sems, thru, lands, after):
    n = len(thru)

    def body(*refs):
        ins, zones = refs[:n], refs[n:2 * n]
        s_sems, r_sems = refs[2 * n], refs[2 * n + 1]
        x, y, c = lax.axis_index("x"), lax.axis_index("y"), lax.axis_index("c")
        for t in range(n):
            for k in range(3):
                src, dst, (px, py) = _scatter_ends(t, k, n_slotted, ins, zones, x, y, _chip_index(x, y))
                cp = pltpu.make_async_remote_copy(
                    src_ref=src, dst_ref=dst if t < n_slotted else dst[1], send_sem=s_sems.at[3 * t + k],
                    recv_sem=r_sems.at[3 * t + k], device_id=(px, py, c), device_id_type=MESH)
                cp.wait_send()
                cp.wait_recv()

    res = pl.pallas_call(
        body, name=name,
        out_shape=(*[pltpu.HBM(a.shape, a.dtype) for a in thru], *[pltpu.HBM(z.shape, z.dtype) for z in lands]),
        in_specs=[HBM] * (2 * n) + [SEM, SEM, pl.BlockSpec(memory_space=pl.ANY)], out_specs=[HBM] * (2 * n),
        input_output_aliases={t: t for t in range(2 * n)},
        compiler_params=pltpu.CompilerParams(has_side_effects=DATAFLOW),
    )(*thru, *lands, send_sems, recv_sems, after)
    return list(res[:n]), list(res[n:])


def _sum_chips(name, slotted, recv, idx, dep):
    _, r, c = slotted.shape
    tr = _row_tile(r, 5 * c * 4)

    def body(idx_ref, own_ref, r_ref, _, o_ref):
        acc = own_ref[0]
        for k in range(3):
            acc = acc + r_ref[k]
        o_ref[...] = acc

    grid_spec = pltpu.PrefetchScalarGridSpec(
        num_scalar_prefetch=1, grid=(r // tr,),
        in_specs=[pl.BlockSpec((1, tr, c), lambda i, s: (s[1], i, 0)), pl.BlockSpec((3, tr, c), lambda i, s: (0, i, 0)),
                  pl.BlockSpec(memory_space=pl.ANY)],
        out_specs=pl.BlockSpec((tr, c), lambda i, s: (i, 0)))
    return pl.pallas_call(
        body, name=name, grid_spec=grid_spec, out_shape=jax.ShapeDtypeStruct((r, c), F32),
        compiler_params=_cparams(("parallel",)),
    )(idx, slotted, recv, dep)


def _half_shape(shape2d, axis):
    r, c = shape2d
    return (r, c // 2) if axis else (r // 2, c)


def _swap_halves(slotted, axes, small):
    n = len(slotted)

    def body(*refs):
        ins, sm_in = refs[:n], refs[n]
        outs, sm_out = refs[n + 1:2 * n + 1], refs[2 * n + 1]
        send_sems, recv_sems, local_sem = refs[2 * n + 2:]
        x, y, c = lax.axis_index("x"), lax.axis_index("y"), lax.axis_index("c")
        local = pltpu.make_async_copy(sm_in, sm_out.at[c], local_sem)
        local.start()
        sends = []
        for t in range(n):
            other = (slice(None),) + _half(slotted[t].shape[1:], axes[t], 1 - c)
            cp = pltpu.make_async_remote_copy(
                src_ref=ins[t].at[other], dst_ref=outs[t], send_sem=send_sems.at[t],
                recv_sem=recv_sems.at[t], device_id=(x, y, 1 - c), device_id_type=MESH)
            cp.start()
            sends.append(cp)
        cp = pltpu.make_async_remote_copy(
            src_ref=sm_in, dst_ref=sm_out.at[c], send_sem=send_sems.at[n], recv_sem=recv_sems.at[n],
            device_id=(x, y, 1 - c), device_id_type=MESH)
        cp.start()
        sends.append(cp)
        for cp in sends[:n]:
            cp.wait_recv()
        pltpu.make_async_remote_copy(
            src_ref=sm_in, dst_ref=sm_out.at[1 - c], send_sem=send_sems.at[n], recv_sem=recv_sems.at[n],
            device_id=(x, y, 1 - c), device_id_type=MESH).wait_recv()
        for cp in sends:
            cp.wait_send()
        local.wait()

    return pl.pallas_call(
        body, name="swap_halves",
        in_specs=[HBM] * (n + 1), out_specs=[HBM] * (n + 1),
        out_shape=[jax.ShapeDtypeStruct((a.shape[0],) + _half_shape(a.shape[1:], ax), a.dtype)
                   for a, ax in zip(slotted, axes, strict=True)]
        + [jax.ShapeDtypeStruct((2,) + small.shape, small.dtype)],
        scratch_shapes=[pltpu.SemaphoreType.DMA((n + 1,)), pltpu.SemaphoreType.DMA((n + 1,)), pltpu.SemaphoreType.DMA],
    )(*slotted, small)


def _scatter_halves(parts, small):
    n = len(parts)
    half_rows = small.shape[0] // 2

    def body(*refs):
        ins, sm_in = refs[:n], refs[n]
        outs, sm_out = refs[n + 1:2 * n + 1], refs[2 * n + 1]
        send_sems, recv_sems, sm_send, sm_recv, local_sem = refs[2 * n + 2:]
        x, y, c = lax.axis_index("x"), lax.axis_index("y"), lax.axis_index("c")
        me = _chip_index(x, y)
        peers = [(1 - x, y), (x, 1 - y), (1 - x, 1 - y)]
        mine = sm_in.at[pl.ds(pl.multiple_of(c * half_rows, 8), half_rows), :]
        local = pltpu.make_async_copy(mine, sm_out.at[me], local_sem)
        local.start()
        sends = []
        for k, (px, py) in enumerate(peers):
            cp = pltpu.make_async_remote_copy(src_ref=mine, dst_ref=sm_out.at[me], send_sem=sm_send.at[k],
                                              recv_sem=sm_recv.at[k], device_id=(px, py, c), device_id_type=MESH)
            cp.start()
            sends.append(cp)
        for t in range(n):
            for k, (px, py) in enumerate(peers):
                cp = pltpu.make_async_remote_copy(
                    src_ref=ins[t].at[_chip_index(px, py)], dst_ref=outs[t].at[k], send_sem=send_sems.at[t, k],
                    recv_sem=recv_sems.at[t, k], device_id=(px, py, c), device_id_type=MESH)
                cp.start()
                sends.append(cp)
        for k, (px, py) in enumerate(peers):
            pltpu.make_async_remote_copy(src_ref=mine, dst_ref=sm_out.at[_chip_index(px, py)], send_sem=sm_send.at[k],
                                         recv_sem=sm_recv.at[k], device_id=(px, py, c), device_id_type=MESH).wait_recv()
        for cp in sends[3:]:
            cp.wait_recv()
        for cp in sends:
            cp.wait_send()
        local.wait()

    return pl.pallas_call(
        body, name="scatter_halves",
        in_specs=[HBM] * (n + 1), out_specs=[HBM] * (n + 1),
        out_shape=[jax.ShapeDtypeStruct((3,) + a.shape[1:], a.dtype) for a in parts]
        + [jax.ShapeDtypeStruct((N_CHIPS, half_rows, small.shape[1]), small.dtype)],
        scratch_shapes=[pltpu.SemaphoreType.DMA((n, 3)), pltpu.SemaphoreType.DMA((n, 3)),
                        pltpu.SemaphoreType.DMA((3,)), pltpu.SemaphoreType.DMA((3,)), pltpu.SemaphoreType.DMA],
    )(*parts, small)


def _tiling(r, c, f32_per_elem):
    if r % 8 == 0:
        tr = _row_tile(r, f32_per_elem * c * 4)
        return r // tr, (tr, c), lambda i: (i, 0)
    assert c % 128 == 0, (r, c)
    return c // 128, (r, 128), lambda i: (0, i)


def _pair_sum(name, slotted, other, idx, axis):
    _, r, c = slotted.shape
    hr, hc = _half_shape((r, c), axis)
    n, (tr, tc), at = _tiling(hr, hc, 13)
    if axis == 0:
        a = slotted.reshape(N_CHIPS, 2, hr, c)
        a_all = pl.BlockSpec((N_CHIPS, 1, tr, tc), lambda i, s: (0, s[0]) + at(i))
        a_own = pl.BlockSpec((1, 1, tr, tc), lambda i, s: (s[1], s[0]) + at(i))
    else:
        a, per_half = slotted, hc // tc
        a_all = pl.BlockSpec((N_CHIPS, tr, tc), lambda i, s: (0, at(i)[0], s[0] * per_half + at(i)[1]))
        a_own = pl.BlockSpec((1, tr, tc), lambda i, s: (s[1], at(i)[0], s[0] * per_half + at(i)[1]))

    def body(idx_ref, a_ref, b_ref, am_ref, bm_ref, p_ref, own_ref):
        mine, mine_own = (a_ref[:, 0], am_ref[0, 0]) if axis == 0 else (a_ref[...], am_ref[0])
        p_ref[...] = (mine + b_ref[...]).astype(p_ref.dtype)
        own_ref[...] = mine_own + bm_ref[0]

    grid_spec = pltpu.PrefetchScalarGridSpec(
        num_scalar_prefetch=1, grid=(n,),
        in_specs=[a_all, pl.BlockSpec((N_CHIPS, tr, tc), lambda i, s: (0,) + at(i)),
                  a_own, pl.BlockSpec((1, tr, tc), lambda i, s: (s[1],) + at(i))],
        out_specs=[pl.BlockSpec((N_CHIPS, tr, tc), lambda i, s: (0,) + at(i)),
                   pl.BlockSpec((tr, tc), lambda i, s: at(i))])
    return pl.pallas_call(
        body, name=name, grid_spec=grid_spec,
        out_shape=[jax.ShapeDtypeStruct((N_CHIPS, hr, hc), BF16), jax.ShapeDtypeStruct((hr, hc), F32)],
        compiler_params=_cparams(("parallel",)),
    )(idx, a, other, a, other)


def _sum_parts(name, own, recv):
    h, c = own.shape
    n, (tr, tc), at = _tiling(h, c, 4)

    def body(o_ref, r_ref, out_ref):
        acc = o_ref[...]
        for k in range(3):
            acc = acc + r_ref[k].astype(F32)
        out_ref[...] = acc

    return pl.pallas_call(
        body, name=name, grid=(n,),
        in_specs=[pl.BlockSpec((tr, tc), at), pl.BlockSpec((3, tr, tc), lambda i: (0,) + at(i))],
        out_specs=pl.BlockSpec((tr, tc), at),
        out_shape=jax.ShapeDtypeStruct((h, c), F32),
        compiler_params=_cparams(("parallel",)),
    )(own, recv)


def _all_to_all(name, small):
    def body(sm_in, sm_out, send_sems, recv_sems, local_sem):
        x, y, c = lax.axis_index("x"), lax.axis_index("y"), lax.axis_index("c")
        dev = 4 * x + 2 * y + c
        local = pltpu.make_async_copy(sm_in, sm_out.at[dev], local_sem)
        local.start()
        rel = [(fx, fy, fc) for fx in (0, 1) for fy in (0, 1) for fc in (0, 1)][1:]
        sends = []
        for k, (fx, fy, fc) in enumerate(rel):
            cp = pltpu.make_async_remote_copy(
                src_ref=sm_in, dst_ref=sm_out.at[dev], send_sem=send_sems.at[k], recv_sem=recv_sems.at[k],
                device_id=(x ^ fx, y ^ fy, c ^ fc), device_id_type=MESH)
            cp.start()
            sends.append(cp)
        for k, (fx, fy, fc) in enumerate(rel):
            src_dev = 4 * (x ^ fx) + 2 * (y ^ fy) + (c ^ fc)
            pltpu.make_async_remote_copy(
                src_ref=sm_in, dst_ref=sm_out.at[src_dev], send_sem=send_sems.at[k], recv_sem=recv_sems.at[k],
                device_id=(x ^ fx, y ^ fy, c ^ fc), device_id_type=MESH).wait_recv()
        for cp in sends:
            cp.wait_send()
        local.wait()

    return pl.pallas_call(
        body, name=name, in_specs=[HBM], out_specs=HBM,
        out_shape=jax.ShapeDtypeStruct((N_DEV,) + small.shape, small.dtype),
        scratch_shapes=[pltpu.SemaphoreType.DMA((7,)), pltpu.SemaphoreType.DMA((7,)), pltpu.SemaphoreType.DMA],
    )(small)


def _swap_sibling(name, parts):
    n = len(parts)

    def body(*refs):
        ins, outs = refs[:n], refs[n:2 * n]
        send_sems, recv_sems = refs[2 * n:]
        x, y, c = lax.axis_index("x"), lax.axis_index("y"), lax.axis_index("c")
        cps = []
        for t in range(n):
            cp = pltpu.make_async_remote_copy(
                src_ref=ins[t], dst_ref=outs[t], send_sem=send_sems.at[t], recv_sem=recv_sems.at[t],
                device_id=(x, y, 1 - c), device_id_type=MESH)
            cp.start()
            cps.append(cp)
        for cp in cps:
            cp.wait_recv()
        for cp in cps:
            cp.wait_send()

    return pl.pallas_call(
        body, name=name,
        in_specs=[HBM] * n, out_specs=[HBM] * n,
        out_shape=[jax.ShapeDtypeStruct(a.shape, a.dtype) for a in parts],
        scratch_shapes=[pltpu.SemaphoreType.DMA((n,)), pltpu.SemaphoreType.DMA((n,))],
    )(*parts)


def _sum_slots(name, a):
    k, r, c = a.shape
    tr = _row_tile(r, (k + 1) * c * 4)

    def body(a_ref, o_ref):
        acc = a_ref[0]
        for i in range(1, k):
            acc = acc + a_ref[i]
        o_ref[...] = acc

    return pl.pallas_call(
        body, name=name, grid=(r // tr,),
        in_specs=[pl.BlockSpec((k, tr, c), lambda i: (0, i, 0))],
        out_specs=pl.BlockSpec((tr, c), lambda i: (i, 0)),
        out_shape=jax.ShapeDtypeStruct((r, c), a.dtype),
        compiler_params=_cparams(("parallel",)),
    )(a)


def _adamw(name, w, m, v, g_parts):
    r, c = w.shape
    ng = len(g_parts)
    tr = _row_tile(r, (7 + ng) * c * 4)
    c1 = 1.0 - ADAM_B1 ** ADAM_STEP
    c2 = 1.0 - ADAM_B2 ** ADAM_STEP

    def body(*refs):
        w_ref, m_ref, v_ref = refs[:3]
        g_refs = refs[3:3 + ng]
        go_ref, d_ref, mo_ref, vo_ref = refs[3 + ng:]
        g = g_refs[0][...]
        for gr in g_refs[1:]:
            g = g + gr[...]
        m_new = ADAM_B1 * m_ref[...] + (1.0 - ADAM_B1) * g
        v_new = ADAM_B2 * v_ref[...] + (1.0 - ADAM_B2) * (g * g)
        go_ref[...] = g
        mo_ref[...] = m_new
        vo_ref[...] = v_new
        d_ref[...] = -ADAM_LR * ((m_new / c1) / (jnp.sqrt(v_new / c2) + ADAM_EPS) + ADAM_WD * w_ref[...])

    spec = pl.BlockSpec((tr, c), lambda i: (i, 0))
    return pl.pallas_call(
        body, name=name, grid=(r // tr,),
        in_specs=[spec] * (3 + ng), out_specs=[spec] * 4,
        out_shape=[jax.ShapeDtypeStruct((r, c), F32)] * 4,
        compiler_params=_cparams(("parallel",)),
    )(w, m, v, *g_parts)


def _adamw_halves(name, w, m, v, own, other, idx, axis):
    hr, hc = own.shape
    nb, (tr, tc), at = _tiling(hr, hc, 9)
    c1 = 1.0 - ADAM_B1 ** ADAM_STEP
    c2 = 1.0 - ADAM_B2 ** ADAM_STEP

    def body(idx_ref, w_ref, m_ref, v_ref, own_ref, oth_ref, go_ref, d_ref, mo_ref, vo_ref):
        g = jnp.where(pl.program_id(0) == idx_ref[0], own_ref[...], oth_ref[...])
        m_new = ADAM_B1 * m_ref[...] + (1.0 - ADAM_B1) * g
        v_new = ADAM_B2 * v_ref[...] + (1.0 - ADAM_B2) * (g * g)
        go_ref[...] = g
        mo_ref[...] = m_new
        vo_ref[...] = v_new
        d_ref[...] = -ADAM_LR * ((m_new / c1) / (jnp.sqrt(v_new / c2) + ADAM_EPS) + ADAM_WD * w_ref[...])

    per_half = (hc // tc) if axis else (hr // tr)
    if axis:
        full = pl.BlockSpec((tr, tc), lambda hh, i, s: (at(i)[0], hh * per_half + at(i)[1]))
    else:
        full = pl.BlockSpec((tr, tc), lambda hh, i, s: (hh * per_half + at(i)[0], at(i)[1]))
    part = pl.BlockSpec((tr, tc), lambda hh, i, s: at(i))
    grid_spec = pltpu.PrefetchScalarGridSpec(
        num_scalar_prefetch=1, grid=(2, nb), in_specs=[full, full, full, part, part], out_specs=[full] * 4)
    return pl.pallas_call(
        body, name=name, grid_spec=grid_spec, out_shape=[jax.ShapeDtypeStruct(w.shape, F32)] * 4,
        compiler_params=_cparams(("parallel", "parallel")),
    )(idx, w, m, v, own, other)


WEIGHTS = ['norm_w', 'w_in', 's5_a_re', 's5_a_im', 's5_b_re', 's5_b_im', 's5_c_re', 's5_c_im', 's5_d', 's5_log_step',
           's5_w_glu', 's5_b_glu', 'ssd_conv_w', 'ssd_conv_b', 'ssd_dt_bias', 'ssd_a_log', 'ssd_d', 'ssd_norm_w',
           'w_br_s5', 'w_br_ssd', 'w_out', 'ple_norm_w', 'w_ple_gate', 'w_ple_proj', 'final_norm_w']
SHARDED = {'w_in': ((IN_PROJ_DIM, 1024), 0), 's5_w_glu': ((512, 512), 0), 'ssd_conv_w': ((SSD_CONV, SSD_CONV_DIM), 1),
           'w_br_s5': ((512, 1024), 1), 'w_br_ssd': ((1536, 1024), 0), 'w_out': ((1024, 1024), 0),
           'w_ple_gate': ((1024, 1024), 0), 'w_ple_proj': ((256, 1024), 1)}
TRANSPOSED = ('w_in',)
SMALL = [n for n in WEIGHTS if n not in SHARDED]


def _shard_shape(name):
    (r, c), ax = SHARDED[name]
    return (r // N_CHIPS, c) if ax == 0 else (r, c // N_CHIPS)


def _half_axis(name):
    return 0 if (_shard_shape(name)[0] // 2) % 16 == 0 else 1


def _shard2d(name, a):
    r, c = _shard_shape(name)
    return a.reshape(c, r).T if name in TRANSPOSED else a.reshape(r, c)


def _unshard2d(name, a2, shape):
    return (a2.T if name in TRANSPOSED else a2).reshape(shape)


def _unslot(name, a4):
    (r, c), ax = SHARDED[name]
    if ax == 0:
        return a4.reshape(r, c)
    return jnp.transpose(a4, (1, 0, 2)).reshape(r, c)


def _slot(name, full):
    (r, c), ax = SHARDED[name]
    if ax == 0:
        return full.reshape(N_CHIPS, r // N_CHIPS, c)
    return jnp.transpose(full.reshape(r, N_CHIPS, c // N_CHIPS), (1, 0, 2))


GHP = ('s5_b_re', 's5_b_im')


def _view_shape(name):
    if name in ('s5_a_re', 's5_a_im'):
        return (S5_GROUPS, S5_STATE)
    if name in GHP + ('s5_c_re', 's5_c_im'):
        return (S5_GROUPS, S5_GROUP, S5_STATE)
    if name == 'ssd_conv_w':
        return (SSD_CONV, SSD_CONV_DIM // N_CHIPS)
    return (1, {'s5_log_step': S5_GROUPS, 'ssd_conv_b': SSD_CONV_DIM, 'ssd_norm_w': SSD_WIDTH, 's5_d': S5_WIDTH,
                's5_b_glu': S5_WIDTH, 'ssd_dt_bias': SSD_HEADS, 'ssd_a_log': SSD_HEADS, 'ssd_d': SSD_HEADS}.get(name, D_MODEL))


def _view(name, a):
    if name in GHP:
        return jnp.swapaxes(a.reshape(S5_GROUPS, S5_STATE, S5_GROUP), 1, 2)
    return a.reshape(_view_shape(name))


def _unview(name, a, shape):
    return (jnp.swapaxes(a, 1, 2) if name in GHP else a).reshape(shape)


def _adamw_small(ws, ms, vs, gs):
    n = len(ws)
    c1 = 1.0 - ADAM_B1 ** ADAM_STEP
    c2 = 1.0 - ADAM_B2 ** ADAM_STEP

    def body(*refs):
        w_r, m_r, v_r, g_r = (refs[k * n:(k + 1) * n] for k in range(4))
        d_o, m_o, v_o = (refs[k * n:(k + 1) * n] for k in range(4, 7))
        for i in range(n):
            g = g_r[i][...]
            m_new = ADAM_B1 * m_r[i][...] + (1.0 - ADAM_B1) * g
            v_new = ADAM_B2 * v_r[i][...] + (1.0 - ADAM_B2) * (g * g)
            m_o[i][...] = m_new
            v_o[i][...] = v_new
            d_o[i][...] = -ADAM_LR * ((m_new / c1) / (jnp.sqrt(v_new / c2) + ADAM_EPS) + ADAM_WD * w_r[i][...])

    return pl.pallas_call(
        body, name="adamw_small", out_shape=[jax.ShapeDtypeStruct(w.shape, F32) for w in ws] * 3,
        compiler_params=pltpu.CompilerParams(vmem_limit_bytes=VMEM_LIMIT),
    )(*ws, *ms, *vs, *gs)


def _pack_small(vals):
    flat = jnp.concatenate([v.reshape(-1).astype(F32) for v in vals])
    rows = -(-flat.shape[0] // (256 * 128)) * 256
    return jnp.pad(flat, (0, rows * 128 - flat.shape[0])).reshape(rows, 128)


def _unpack_small(packed, shapes):
    flat = packed.reshape(-1)
    out, off = [], 0
    for sh in shapes:
        n = math.prod(sh)
        out.append(flat[off:off + n].reshape(sh))
        off += n
    return out


def kernel(x, p, norm_w, w_in, s5_a_re, s5_a_im, s5_b_re, s5_b_im, s5_c_re, s5_c_im, s5_d, s5_log_step, s5_w_glu, s5_b_glu, ssd_conv_w, ssd_conv_b, ssd_dt_bias, ssd_a_log, ssd_d, ssd_norm_w, w_br_s5, w_br_ssd, w_out, ple_norm_w, w_ple_gate, w_ple_proj, final_norm_w, loss_target, m_norm_w, m_w_in, m_s5_a_re, m_s5_a_im, m_s5_b_re, m_s5_b_im, m_s5_c_re, m_s5_c_im, m_s5_d, m_s5_log_step, m_s5_w_glu, m_s5_b_glu, m_ssd_conv_w, m_ssd_conv_b, m_ssd_dt_bias, m_ssd_a_log, m_ssd_d, m_ssd_norm_w, m_w_br_s5, m_w_br_ssd, m_w_out, m_ple_norm_w, m_w_ple_gate, m_w_ple_proj, m_final_norm_w, v_norm_w, v_w_in, v_s5_a_re, v_s5_a_im, v_s5_b_re, v_s5_b_im, v_s5_c_re, v_s5_c_im, v_s5_d, v_s5_log_step, v_s5_w_glu, v_s5_b_glu, v_ssd_conv_w, v_ssd_conv_b, v_ssd_dt_bias, v_ssd_a_log, v_ssd_d, v_ssd_norm_w, v_w_br_s5, v_w_br_ssd, v_w_out, v_ple_norm_w, v_w_ple_gate, v_w_ple_proj, v_final_norm_w):
    args = locals()
    wl = {n: args[n] for n in WEIGHTS}
    ml = {n: args["m_" + n] for n in WEIGHTS}
    vl = {n: args["v_" + n] for n in WEIGHTS}
    big = [n for n in SHARDED if n != 'ssd_conv_w']
    chip = _chip_index(lax.axis_index("x"), lax.axis_index("y"))
    idx = jnp.stack([lax.axis_index("c"), chip]).astype(jnp.int32)

    axes = [_half_axis(n) for n in big]
    first = ['w_in']
    rest = [n for n in big if n not in first]
    bf_shards = {n: _shard2d(n, wl[n]).astype(BF16) for n in big}
    w_in_t, conv_w4 = _gather_chips([bf_shards[n] for n in first], [_half_axis(n) for n in first],
                                    [_shard2d('ssd_conv_w', wl['ssd_conv_w'])])
    full = {n: wl[n] for n in SMALL}
    full["w_in_t"] = w_in_t
    full['ssd_conv_w'] = _unslot('ssd_conv_w', conv_w4)
    send_sems, recv_sems, thru, lands, token = _gather_start([bf_shards[n] for n in rest], w_in_t)

    def fetch_rest(after):
        own, zones = _gather_wait(send_sems, recv_sems, thru, lands, after)
        return {n: _unslot(n, lax.dynamic_update_slice(z, o[None], (chip, 0, 0)))
                for n, o, z in zip(rest, own, zones, strict=True)}

    full["_late"] = (token, fetch_rest)

    in_flight = []

    def send_rest(g_now):
        in_flight.extend(_scatter_start("scatter_rest_start", [_slot(n, g_now[n]) for n in rest], [], g_now['s5_w_glu']))
        return in_flight[4]

    full["_early"] = send_rest

    packed = [n for n in SMALL if n != 'norm_w']
    first_axes = [_half_axis(n) for n in first]
    st = {}

    def send_first(g_now, loss_now):
        small_pack = _pack_small([loss_now] + [g_now[n + "_ghp" if n in GHP else n] for n in packed]
                                 + [g_now['ssd_conv_w']])
        swapped = _swap_halves([g_now["w_in_t"]], first_axes, small_pack)
        st["pair"] = [_pair_sum("pair_sum_w_in", g_now["w_in_t"], swapped[0], idx, first_axes[0])]
        small_chip = _sum_slots("sum_small_pair", swapped[-1])
        half_rows = small_chip.shape[0] // 2
        my_half = lax.dynamic_slice(small_chip, (lax.axis_index("c") * half_rows, 0), (half_rows, small_chip.shape[1]))
        st["flight"] = _scatter_start("scatter_first_start", [pb for pb, _ in st["pair"]], [my_half], small_chip)
        return st["flight"][4]

    full["_w_in_ready"] = send_first
    loss, grad_x, g = _local_step(x, p[0], loss_target, full)
    pair = st["pair"]
    s_sems, r_sems, thru, lands, tok = st["flight"]
    small_shapes = [(1, 1)] + [_view_shape(n) for n in packed] + [(SSD_CONV, SSD_CONV_DIM)]
    norm_all = _all_to_all("norm_w_all_to_all", g['norm_w'].reshape(8, 128))
    norm_g = _sum_slots("sum_norm_w", norm_all).reshape(1, D_MODEL)

    out_g, out_d, out_m, out_v = {}, {}, {}, {}
    own_slots, zones = _scatter_wait("scatter_rest_wait", len(rest), *in_flight[:4], norm_g)
    chip_sums = [_sum_chips("sum_chips_" + n, a, z, idx, tok) for n, a, z in zip(rest, own_slots, zones, strict=True)]
    sib_sums = _swap_sibling("swap_sibling_rest", chip_sums)
    for n, own, sib in zip(rest, chip_sums, sib_sums, strict=True):
        res = _adamw("adamw_" + n, _shard2d(n, wl[n]), _shard2d(n, ml[n]), _shard2d(n, vl[n]), [own, sib])
        out_g[n], out_d[n], out_m[n], out_v[n] = (_unshard2d(n, r, wl[n].shape) for r in res)

    sent, got = _scatter_wait("scatter_first_wait", len(first), s_sems, r_sems, thru, lands, out_d[rest[-1]])
    small_zone = lax.dynamic_update_slice(got[-1], sent[-1][None], (chip, 0, 0))
    small_half = _sum_slots("sum_small_chips", small_zone)
    halves = [_sum_parts("sum_chips_" + n, own, r) for n, (_, own), r in zip(first, pair, got[:-1], strict=True)]
    swapped2 = _swap_sibling("swap_sibling_first", halves + [small_half])
    other_halves = swapped2[:-1]
    c_is_0 = lax.axis_index("c") == 0
    small_sum = jnp.concatenate([jnp.where(c_is_0, small_half, swapped2[-1]),
                                 jnp.where(c_is_0, swapped2[-1], small_half)], axis=0)
    for n, own, oth, ax in zip(first, halves, other_halves, first_axes, strict=True):
        res = _adamw_halves("adamw_" + n, _shard2d(n, wl[n]), _shard2d(n, ml[n]), _shard2d(n, vl[n]), own, oth, idx, ax)
        out_g[n], out_d[n], out_m[n], out_v[n] = (_unshard2d(n, r, wl[n].shape) for r in res)
    sm = _unpack_small(small_sum, small_shapes)
    loss_total = sm[0].reshape(())
    conv_g = lax.dynamic_slice(sm[-1], (0, chip * (SSD_CONV_DIM // N_CHIPS)), (SSD_CONV, SSD_CONV_DIM // N_CHIPS))
    names = SMALL + ['ssd_conv_w']
    by_name = {**dict(zip(packed, sm[1:-1], strict=True)), 'norm_w': norm_g, 'ssd_conv_w': conv_g}
    grads = [by_name[n] for n in names]
    res = _adamw_small([_view(n, wl[n]) for n in names], [_view(n, ml[n]) for n in names],
                       [_view(n, vl[n]) for n in names], grads)
    for i, n in enumerate(names):
        out_g[n] = _unview(n, grads[i], wl[n].shape)
        out_d[n], out_m[n], out_v[n] = (_unview(n, res[k * len(names) + i], wl[n].shape) for k in range(3))

    return (loss_total, grad_x, *[out_g[n] for n in WEIGHTS], *[out_d[n] for n in WEIGHTS],
            *[out_m[n] for n in WEIGHTS], *[out_v[n] for n in WEIGHTS])
```

```python
import functools
import math

import jax
import jax.numpy as jnp
from jax import lax
from jax.experimental import pallas as pl
from jax.experimental.pallas import tpu as pltpu

F32 = jnp.float32
BF16 = jnp.bfloat16
MESH = pl.DeviceIdType.MESH

D_MODEL = 1024
PLE_DIM = 256
RMS_EPS = 1e-6
S5_WIDTH = 512
S5_GROUP = 16
S5_GROUPS = 32
S5_STATE = 64
S5_N = S5_GROUPS * S5_STATE
S5_LB = 512
S5_NJ = S5_N // S5_LB
S5_TB = 256
S5_LOG_TB = 8
SSD_WIDTH = 1536
SSD_HEADDIM = 64
SSD_HEADS = 24
SSD_GROUPS = 4
SSD_HPG = 6
SSD_STATE = 128
SSD_CONV = 4
SSD_CHUNK = 128
SSD_BC = 512
SSD_CONV_DIM = 2560
GROUP_W = SSD_WIDTH // SSD_GROUPS
N_CHIPS = 4
N_DEV = 8

OFF_XBC, OFF_U5, OFF_Z5, OFF_DT, OFF_G5, OFF_GS, OFF_ZS = 0, 2560, 3072, 3584, 4096, 5120, 6144
DT_W = 512
PROJ_W = 7680
IN_PROJ_DIM = 7192

ADAM_LR, ADAM_B1, ADAM_B2, ADAM_EPS, ADAM_WD, ADAM_STEP = 0.001, 0.9, 0.999, 1e-08, 0.01, 10

VMEM_LIMIT = 56 * 1024 * 1024


ROW_BLOCK_BYTES = 8 * 1024 * 1024


def _row_tile(r, bytes_per_row):
    for t in (r, 4096, 2048, 1024, 512, 256, 128, 64, 32, 16, 8):
        if t <= r and r % t == 0 and t * bytes_per_row <= ROW_BLOCK_BYTES:
            return t
    return r


def _cparams(sem):
    return pltpu.CompilerParams(dimension_semantics=sem, vmem_limit_bytes=VMEM_LIMIT)


def _dg(a, b, ca, cb):
    return lax.dot_general(a.astype(BF16), b.astype(BF16), (((ca,), (cb,)), ((), ())), preferred_element_type=F32)


@jax.custom_vjp
def dot_nn(a, b):
    return _dg(a, b, 1, 0)


@jax.custom_vjp
def dot_nt(a, b):
    return _dg(a, b, 1, 1)


@jax.custom_vjp
def dot_tn(a, b):
    return _dg(a, b, 0, 0)


dot_nn.defvjp(lambda a, b: (_dg(a, b, 1, 0), (a, b)), lambda r, g: (_dg(g, r[1], 1, 1), _dg(r[0], g, 0, 0)))
dot_nt.defvjp(lambda a, b: (_dg(a, b, 1, 1), (a, b)), lambda r, g: (_dg(g, r[1], 1, 0), _dg(g, r[0], 0, 0)))
dot_tn.defvjp(lambda a, b: (_dg(a, b, 0, 0), (a, b)), lambda r, g: (_dg(r[1], g, 1, 1), _dg(r[0], g, 1, 0)))


MM_VMEM_BUDGET = 30 * 1024 * 1024


def _mm_tiles(m, n, k, sa, sb, so, tn_only=None):
    best, best_key = None, None
    for tm in (1024, 512, 256, 128, 64, 32, 16, 8):
        if m % tm:
            continue
        for tn in (2048, 1536, 1280, 1024, 768, 640, 512, 384, 256, 128):
            if n % tn or (tn_only is not None and tn not in tn_only):
                continue
            for tk in (k, 2048, 1536, 1280, 1024, 768, 512, 256, 128):
                if k % tk or tk > max(k, 128) or (tk == k and k > 2048 and k % 128 == 0):
                    continue
                need = 2 * (tm * tk * sa + tk * tn * sb + tm * tn * so) + (tm * tn * 4 if tk < k else 0)
                if need > MM_VMEM_BUDGET:
                    continue
                key = (tm * tn * tk, tk)
                if best_key is None or key > best_key:
                    best, best_key = (tm, tn, tk), key
    assert best is not None, (m, n, k)
    return best


def _matmul(name, a, b, *, ta=False, tb=False, a_win=None, out_dtype=F32, epilogue=None, epi_rows=(), epi_pars=(),
            epi_outs=(), full_rows=False, epi_accs=(), epi_into=None, dep=None):
    a_off, a_w = a_win if a_win is not None else (0, a.shape[1])
    if ta:
        kdim, m = a.shape[0], a_w
    else:
        m, kdim = a.shape[0], a_w
    n = b.shape[0] if tb else b.shape[1]
    assert (b.shape[1] if tb else b.shape[0]) == kdim, (name, a.shape, b.shape)
    out_dtypes = list(epi_outs) if epilogue is not None else [out_dtype]
    so = sum(jnp.dtype(d).itemsize for d in out_dtypes) + sum(r.dtype.itemsize for r, _ in epi_rows)
    tn_ok = [n] if full_rows else [t for t in (1024, 512, 256, 128) if all(off % t == 0 for _, off in epi_rows)]
    tm, tn, tk = _mm_tiles(m, n, kdim, a.dtype.itemsize, b.dtype.itemsize, so, tn_ok if epilogue is not None else None)
    nk = kdim // tk
    n_er, n_ep, n_out = len(epi_rows), len(epi_pars), len(out_dtypes)
    if ta:
        assert a_off % tm == 0
        a_spec = pl.BlockSpec((tk, tm), lambda i, j, k: (k, i + a_off // tm))
    else:
        assert a_off % tk == 0
        a_spec = pl.BlockSpec((tm, tk), lambda i, j, k: (i, k + a_off // tk))
    if tb:
        b_spec = pl.BlockSpec((tn, tk), lambda i, j, k: (j, k))
    else:
        b_spec = pl.BlockSpec((tk, tn), lambda i, j, k: (k, j))
    ca, cb = (0 if ta else 1), (1 if tb else 0)

    n_acc = len(epi_accs)
    n_x = (1 if epi_into is not None else 0) + (1 if dep is not None else 0)
    assert not (n_acc or epi_into is not None) or full_rows

    def body(a_ref, b_ref, *refs):
        er, ep = refs[:n_er], refs[n_er:n_er + n_ep]
        first_out = n_er + n_ep + n_x
        o_refs = refs[first_out:first_out + n_out]
        s_refs = refs[first_out + n_out:first_out + n_out + n_acc]
        acc = refs[first_out + n_out + n_acc:]

        def finish(c):
            outs = [c] if epilogue is None else epilogue(c, [r[...] for r in er], [p[...] for p in ep])
            if n_acc:
                outs, sums = outs

                @pl.when(pl.program_id(0) == 0)
                def _():
                    for s_ref in s_refs:
                        s_ref[...] = jnp.zeros_like(s_ref)

                for s_ref, s in zip(s_refs, sums, strict=True):
                    s_ref[...] += jnp.broadcast_to(s, s_ref.shape)
            for o_ref, o in zip(o_refs, outs, strict=True):
                o_ref[...] = o.astype(o_ref.dtype)

        if nk == 1:
            finish(_dg(a_ref[...], b_ref[...], ca, cb))
            return
        (acc_ref,) = acc
        k = pl.program_id(2)

        @pl.when(k == 0)
        def _():
            acc_ref[...] = jnp.zeros_like(acc_ref)

        acc_ref[...] += _dg(a_ref[...], b_ref[...], ca, cb)

        @pl.when(k == nk - 1)
        def _():
            finish(acc_ref[...])

    in_specs = [a_spec, b_spec]
    in_specs += [pl.BlockSpec((tm, tn), functools.partial(lambda i, j, k, c: (i, j + c), c=off // tn)) for _, off in epi_rows]
    in_specs += [pl.BlockSpec((p.shape[0], tn), lambda i, j, k: (0, j)) for p in epi_pars]
    out_specs = [pl.BlockSpec((tm, tn), lambda i, j, k: (i, j)) for _ in out_dtypes]
    out_shape = [jax.ShapeDtypeStruct((m, n), d) for d in out_dtypes]
    extra, aliases = [], {}
    if epi_into is not None:
        buf, off, width = epi_into
        assert off % width == 0 and buf.dtype == out_dtypes[0]
        in_specs.append(pl.BlockSpec(memory_space=pl.ANY))
        out_specs[0] = pl.BlockSpec((tm, width), functools.partial(lambda i, j, k, c: (i, c), c=off // width))
        out_shape[0] = jax.ShapeDtypeStruct(buf.shape, buf.dtype)
        extra, aliases = [buf], {2 + n_er + n_ep: 0}
    if dep is not None:
        in_specs.append(pl.BlockSpec(memory_space=pl.ANY))
        extra = extra + [dep]
    out_specs += [pl.BlockSpec((r, w), lambda i, j, k: (0, 0)) for r, w in epi_accs]
    out_shape += [jax.ShapeDtypeStruct((r, w), F32) for r, w in epi_accs]
    res = pl.pallas_call(
        body, name=name, grid=(m // tm, n // tn, nk),
        in_specs=in_specs, out_specs=out_specs, out_shape=out_shape, input_output_aliases=aliases,
        scratch_shapes=[pltpu.VMEM((tm, tn), F32)] if nk > 1 else [],
        compiler_params=_cparams(("arbitrary",) * 3 if n_acc else ("parallel", "parallel", "arbitrary")),
    )(a, b, *[r for r, _ in epi_rows], *epi_pars, *extra)
    return res if epilogue is not None else res[0]


BAND = 128


def _rowwise(name, fn, n_rows, tr, row_ins, par_ins, row_outs, acc_outs=(), into=None):
    nr, npar, no, na = len(row_ins), len(par_ins), len(row_outs), len(acc_outs)
    in_specs = []
    for arr, off, w in row_ins:
        assert off % w == 0 and arr.shape[0] == n_rows, (name, arr.shape, off, w)
        in_specs.append(pl.BlockSpec((tr, w), functools.partial(lambda i, c: (i, c), c=off // w)))
    for arr, off, w in par_ins:
        assert off % w == 0
        in_specs.append(pl.BlockSpec((arr.shape[0], w), functools.partial(lambda i, c: (0, c), c=off // w)))
    out_specs = [pl.BlockSpec((tr, w), lambda i: (i, 0)) for w, _ in row_outs]
    out_specs += [pl.BlockSpec((r, w), lambda i: (0, 0)) for r, w in acc_outs]
    out_shape = [jax.ShapeDtypeStruct((n_rows, w), dt) for w, dt in row_outs]
    out_shape += [jax.ShapeDtypeStruct((r, w), F32) for r, w in acc_outs]
    extra, aliases = [], {}
    if into is not None:
        buf, off = into
        w0 = row_outs[0][0]
        assert off % w0 == 0 and buf.dtype == row_outs[0][1]
        in_specs.append(pl.BlockSpec(memory_space=pl.ANY))
        out_specs[0] = pl.BlockSpec((tr, w0), functools.partial(lambda i, c: (i, c), c=off // w0))
        out_shape[0] = jax.ShapeDtypeStruct(buf.shape, buf.dtype)
        extra, aliases = [buf], {nr + npar: 0}
    nx = len(extra)

    def body(*refs):
        rows = [r[...] for r in refs[:nr]]
        pars = [r[...] for r in refs[nr:nr + npar]]
        o_refs = refs[nr + npar + nx:nr + npar + nx + no]
        a_refs = refs[nr + npar + nx + no:]
        outs, accs = fn(rows, pars)
        for o_ref, o in zip(o_refs, outs, strict=True):
            o_ref[...] = o.astype(o_ref.dtype)
        if na:
            @pl.when(pl.program_id(0) == 0)
            def _():
                for a_ref in a_refs:
                    a_ref[...] = jnp.zeros_like(a_ref)

            for a_ref, a in zip(a_refs, accs, strict=True):
                a_ref[...] += jnp.broadcast_to(a, a_ref.shape)

    res = pl.pallas_call(
        body, name=name, grid=(n_rows // tr,),
        in_specs=in_specs, out_specs=out_specs, out_shape=out_shape, input_output_aliases=aliases,
        compiler_params=_cparams(("arbitrary",) if na else ("parallel",)),
    )(*[a for a, _, _ in row_ins], *[a for a, _, _ in par_ins], *extra)
    return res


def _rms(x, w):
    return x * lax.rsqrt(jnp.mean(x * x, axis=-1, keepdims=True) + RMS_EPS) * w


def _s5_out(yc, u, z, t, d, bg):
    ge = jax.nn.gelu(yc + d * u)
    return ge * jax.nn.sigmoid(t + bg) * jax.nn.silu(z)


def _merge(g5, gs, m5, ms):
    return jax.nn.sigmoid(g5) * m5 + jax.nn.sigmoid(gs) * ms


def _head_loss(h1, pgl, pp, fw, tgt):
    h2 = h1 + jax.nn.sigmoid(pgl) * pp
    err = _rms(h2, fw) - tgt
    per_row = 0.5 * jnp.mean(err * err, axis=-1, keepdims=True)
    return jnp.sum(per_row, axis=0, keepdims=True)


def _s5_disc(a_re, a_im, log_step, b_re2, b_im2, expand):
    step = jnp.exp(log_step)
    mag = jnp.exp(a_re * step)
    lb_re = mag * jnp.cos(a_im * step)
    lb_im = mag * jnp.sin(a_im * step)
    den = a_re * a_re + a_im * a_im
    n_re = lb_re - 1.0
    f_re = (n_re * a_re + lb_im * a_im) / den
    f_im = (lb_im * a_re - n_re * a_im) / den
    hi = lax.Precision.HIGHEST
    fr = jnp.dot(expand, f_re, precision=hi, preferred_element_type=F32)
    fi = jnp.dot(expand, f_im, precision=hi, preferred_element_type=F32)
    return lb_re, lb_im, fr * b_re2 - fi * b_im2, fr * b_im2 + fi * b_re2


def _s5_params_fwd(a_re, a_im, log_step, b_re2, b_im2, expand):
    gp = a_re.shape

    def body(ar, ai, ls, br, bi, ex, pr_ref, pi_ref, bbr_ref, bbi_ref):
        lr, li, bbr, bbi = _s5_disc(ar[...], ai[...], ls[...], br[...], bi[...], ex[...])
        bbr_ref[...] = bbr
        bbi_ref[...] = bbi
        qr, qi = lr, li
        for k in range(S5_LOG_TB):
            pr_ref[k] = qr
            pi_ref[k] = qi
            qr, qi = qr * lr - qi * li, qr * li + qi * lr

    return pl.pallas_call(
        body, name="s5_params_fwd",
        out_shape=(jax.ShapeDtypeStruct((S5_LOG_TB,) + gp, F32), jax.ShapeDtypeStruct((S5_LOG_TB,) + gp, F32),
                   jax.ShapeDtypeStruct(b_re2.shape, F32), jax.ShapeDtypeStruct(b_re2.shape, F32)),
    )(a_re, a_im, log_step, b_re2, b_im2, expand)


def _s5_params_bwd(a_re, a_im, log_step, b_re2, b_im2, expand, d_lr, d_li, d_bbr, d_bbi):
    def body(ar, ai, ls, br, bi, ex, glr, gli, gbr, gbi, dar, dai, dls, dbr, dbi):
        _, vjp = jax.vjp(lambda *p: _s5_disc(*p, ex[...]), ar[...], ai[...], ls[...], br[...], bi[...])
        g = vjp((glr[...], gli[...], gbr[...], gbi[...]))
        for ref, val in zip((dar, dai, dls, dbr, dbi), g, strict=True):
            ref[...] = val

    return pl.pallas_call(
        body, name="s5_params_bwd",
        out_shape=tuple(jax.ShapeDtypeStruct(v.shape, F32) for v in (a_re, a_im, log_step, b_re2, b_im2)),
    )(a_re, a_im, log_step, b_re2, b_im2, expand, d_lr, d_li, d_bbr, d_bbi)


def _scan_block(x_ref, out_ref, lp, edge_pow, cr, ci, reverse, each=None):
    n_g = x_ref.shape[0] // 8
    sub = lax.broadcasted_iota(jnp.int32, (8, S5_LB), 0)
    steps = []
    for sh in (1, 2, 4):
        keep = (sub < 8 - sh) if reverse else (sub >= sh)
        steps.append((8 - sh if reverse else sh, jnp.where(keep, lp[sh - 1:sh, :S5_LB], 0.0),
                      jnp.where(keep, lp[sh - 1:sh, S5_LB:], 0.0)))
    e_r, e_i = edge_pow[:, :S5_LB], edge_pow[:, S5_LB:]
    for r in (range(n_g - 1, -1, -1) if reverse else range(n_g)):
        rows = slice(8 * r, 8 * r + 8)
        xr, xi = x_ref[rows, :S5_LB], x_ref[rows, S5_LB:]
        for by, a_r, a_i in steps:
            pr, pi = pltpu.roll(xr, by, 0), pltpu.roll(xi, by, 0)
            xr, xi = xr + a_r * pr - a_i * pi, xi + a_r * pi + a_i * pr
        xr = xr + e_r * cr - e_i * ci
        xi = xi + e_r * ci + e_i * cr
        out_ref[rows, :S5_LB] = xr
        out_ref[rows, S5_LB:] = xi
        if each is not None:
            each(r, xr, xi)
        cr, ci = (xr[0:1, :], xi[0:1, :]) if reverse else (xr[7:8, :], xi[7:8, :])
    return cr, ci


def _s5_fwd(proj, bb_band, c_band, lam_pow, s5_d, n_seq, seq_len):
    n_t = seq_len // S5_TB
    blk = 2 * S5_LB
    n_rows = n_seq * seq_len
    u_blk0 = OFF_U5 // BAND

    def body(u_ref, bb_ref, cb_ref, lp_ref, d_ref, s_ref, y_ref, ge_ref, cr, ci, buf):
        @pl.when(pl.program_id(2) == 0)
        def _():
            cr[...] = jnp.zeros_like(cr)
            ci[...] = jnp.zeros_like(ci)

        u = u_ref[...]
        buf[...] = _dg(u, bb_ref[...], 1, 0)
        lp = lp_ref[...]
        cr[...], ci[...] = _scan_block(buf, buf, lp, lp, cr[...], ci[...], False)
        s = buf[...].astype(s_ref.dtype)
        s_ref[...] = s
        y = _dg(s, cb_ref[...], 1, 1)
        y_ref[...] = y
        ge_ref[...] = jax.nn.gelu(y + d_ref[...] * u).astype(ge_ref.dtype)

    def rows(j, b, t):
        return (b * n_t + t, j)

    return pl.pallas_call(
        body, name="s5_fwd", grid=(S5_NJ, n_seq, n_t),
        in_specs=[pl.BlockSpec((S5_TB, BAND), lambda j, b, t: (b * n_t + t, u_blk0 + j)),
                  pl.BlockSpec((BAND, blk), lambda j, b, t: (j, 0)), pl.BlockSpec((BAND, blk), lambda j, b, t: (j, 0)),
                  pl.BlockSpec((S5_LOG_TB, blk), lambda j, b, t: (0, j)), pl.BlockSpec((1, BAND), lambda j, b, t: (0, j))],
        out_specs=[pl.BlockSpec((S5_TB, blk), rows), pl.BlockSpec((S5_TB, BAND), rows), pl.BlockSpec((S5_TB, BAND), rows)],
        out_shape=[jax.ShapeDtypeStruct((n_rows, S5_NJ * blk), BF16), jax.ShapeDtypeStruct((n_rows, S5_WIDTH), F32),
                   jax.ShapeDtypeStruct((n_rows, S5_WIDTH), BF16)],
        scratch_shapes=[pltpu.VMEM((1, S5_LB), F32), pltpu.VMEM((1, S5_LB), F32), pltpu.VMEM((S5_TB, blk), F32)],
        compiler_params=_cparams(("parallel", "parallel", "arbitrary")),
    )(proj, bb_band, c_band, lam_pow, s5_d)


def _s5_bwd(dyc, s, proj, du5_a, bb_band, c_band, lam_pow_conj, dproj, n_seq, seq_len):
    n_t = seq_len // S5_TB
    blk = 2 * S5_LB
    halo_rows = 16
    halo_per_blk = S5_TB // halo_rows
    u_blk0 = OFF_U5 // BAND

    def rows(j, b, t):
        return (b * n_t + (n_t - 1 - t), j)

    def u_rows(j, b, t):
        return (b * n_t + (n_t - 1 - t), u_blk0 + j)

    def halo(j, b, t):
        return (jnp.maximum((b * n_t + (n_t - 1 - t)) * halo_per_blk - 1, 0), j)

    def body(dy_ref, sq_ref, hq_ref, u_ref, dua_ref, bb_ref, cb_ref, lp_ref, _, du_ref, dl_ref, dbb_ref, dcb_ref,
             cr, ci, buf, s_ref):
        b, t = pl.program_id(1), pl.program_id(2)
        sq = sq_ref[...]
        s_ref[...] = sq.astype(F32)
        h_last = hq_ref[...].astype(F32)[halo_rows - 1:halo_rows, :]
        dy = dy_ref[...]
        buf[...] = _dg(dy, cb_ref[...], 1, 0)

        @pl.when(t == 0)
        def _():
            cr[...] = jnp.zeros_like(cr)
            ci[...] = jnp.zeros_like(ci)

        @pl.when((b == 0) & (t == 0))
        def _():
            dl_ref[...] = jnp.zeros_like(dl_ref)
            dbb_ref[...] = jnp.zeros_like(dbb_ref)
            dcb_ref[...] = jnp.zeros_like(dcb_ref)

        dcb_ref[...] += _dg(dy, sq, 0, 0)

        first_blk = t == n_t - 1
        sub = lax.broadcasted_iota(jnp.int32, (8, S5_LB), 0)
        acc = [jnp.zeros((8, S5_LB), F32), jnp.zeros((8, S5_LB), F32)]

        def each(r, gr, gi):
            rows = slice(8 * r, 8 * r + 8)
            if r == 0:
                before_r = jnp.where(first_blk, 0.0, h_last[:, :S5_LB])
                before_i = jnp.where(first_blk, 0.0, h_last[:, S5_LB:])
            else:
                before_r, before_i = s_ref[8 * r - 1:8 * r, :S5_LB], s_ref[8 * r - 1:8 * r, S5_LB:]
            sp_r = jnp.where(sub == 0, before_r, pltpu.roll(s_ref[rows, :S5_LB], 1, 0))
            sp_i = jnp.where(sub == 0, before_i, pltpu.roll(s_ref[rows, S5_LB:], 1, 0))
            acc[0] = acc[0] + gr * sp_r + gi * sp_i
            acc[1] = acc[1] + gi * sp_r - gr * sp_i

        lp = lp_ref[...]
        edge_pow = jnp.concatenate([lp[7 - i:8 - i, :] for i in range(8)], axis=0)
        cr[...], ci[...] = _scan_block(buf, buf, lp, edge_pow, cr[...], ci[...], True, each)
        g = buf[...].astype(BF16)
        du_ref[...] = (dua_ref[...] + _dg(g, bb_ref[...], 1, 1)).astype(du_ref.dtype)
        dbb_ref[...] += _dg(u_ref[...], g, 0, 0)
        dl_ref[:, :S5_LB] += jnp.sum(acc[0], axis=0, keepdims=True)
        dl_ref[:, S5_LB:] += jnp.sum(acc[1], axis=0, keepdims=True)

    band = pl.BlockSpec((BAND, blk), lambda j, b, t: (j, 0))
    return pl.pallas_call(
        body, name="s5_bwd", grid=(S5_NJ, n_seq, n_t),
        in_specs=[pl.BlockSpec((S5_TB, BAND), rows), pl.BlockSpec((S5_TB, blk), rows),
                  pl.BlockSpec((halo_rows, blk), halo), pl.BlockSpec((S5_TB, BAND), u_rows),
                  pl.BlockSpec((S5_TB, BAND), rows), band, band,
                  pl.BlockSpec((S5_LOG_TB, blk), lambda j, b, t: (0, j)), pl.BlockSpec(memory_space=pl.ANY)],
        out_specs=[pl.BlockSpec((S5_TB, BAND), u_rows), pl.BlockSpec((1, blk), lambda j, b, t: (0, j)), band, band],
        out_shape=[jax.ShapeDtypeStruct(dproj.shape, dproj.dtype), jax.ShapeDtypeStruct((1, S5_NJ * blk), F32),
                   jax.ShapeDtypeStruct(bb_band.shape, F32), jax.ShapeDtypeStruct(c_band.shape, F32)],
        input_output_aliases={8: 0},
        scratch_shapes=[pltpu.VMEM((1, S5_LB), F32), pltpu.VMEM((1, S5_LB), F32), pltpu.VMEM((S5_TB, blk), F32),
                        pltpu.VMEM((S5_TB, blk), F32)],
        compiler_params=_cparams(("parallel", "arbitrary", "arbitrary")),
    )(dyc, s, s, proj, du5_a, bb_band, c_band, lam_pow_conj, dproj)


CONV_TR = 512
CONV_CW = 512


def _shift_down(x, halo, k):
    if k == 0:
        return x
    row8 = lax.broadcasted_iota(jnp.int32, halo.shape, 0)
    rolled = pltpu.roll(x, k, 0)
    top = jnp.where(row8 < k, pltpu.roll(halo, k, 0), rolled[:8])
    if x.shape[0] == 8:
        return top
    return jnp.concatenate([top, rolled[8:]], axis=0)


def _shift_up(x, halo, k):
    if k == 0:
        return x
    n = x.shape[0]
    row8 = lax.broadcasted_iota(jnp.int32, halo.shape, 0)
    rolled = pltpu.roll(x, n - k, 0)
    bot = jnp.where(row8 >= 8 - k, pltpu.roll(halo, 8 - k, 0), rolled[n - 8:])
    if n == 8:
        return bot
    return jnp.concatenate([rolled[:n - 8], bot], axis=0)


def _conv_pre(x, halo, w, b):
    acc = b + w[SSD_CONV - 1:SSD_CONV, :] * x
    for k in range(SSD_CONV - 1):
        acc = acc + w[k:k + 1, :] * _shift_down(x, halo, SSD_CONV - 1 - k)
    return acc


def _conv_specs(seq_len, col_off):
    lt = seq_len // CONV_TR
    cb = col_off // CONV_CW
    cur = pl.BlockSpec((CONV_TR, CONV_CW), lambda j, i: (i, j + cb))
    prev = pl.BlockSpec((8, CONV_CW), lambda j, i: (jnp.maximum(i * (CONV_TR // 8) - 1, 0), j + cb))
    return lt, cur, prev


def _conv_fwd(proj, conv_w, conv_b, n_rows, seq_len):
    lt, cur, prev = _conv_specs(seq_len, OFF_XBC)

    def body(x_ref, h_ref, w_ref, b_ref, o_ref):
        halo = jnp.where(pl.program_id(1) % lt == 0, 0.0, h_ref[...])
        o_ref[...] = jax.nn.silu(_conv_pre(x_ref[...], halo, w_ref[...], b_ref[...]))

    return pl.pallas_call(
        body, name="ssd_conv_fwd", grid=(SSD_CONV_DIM // CONV_CW, n_rows // CONV_TR),
        in_specs=[cur, prev, pl.BlockSpec((SSD_CONV, CONV_CW), lambda j, i: (0, j)),
                  pl.BlockSpec((1, CONV_CW), lambda j, i: (0, j))],
        out_specs=pl.BlockSpec((CONV_TR, CONV_CW), lambda j, i: (i, j)),
        out_shape=jax.ShapeDtypeStruct((n_rows, SSD_CONV_DIM), F32),
        compiler_params=_cparams(("parallel", "parallel")),
    )(proj, proj, conv_w, conv_b)


def _conv_bwd(name, proj, d_act, conv_w, conv_b, n_rows, seq_len, col_off, dproj):
    width = d_act.shape[1]
    lt, cur, prev = _conv_specs(seq_len, OFF_XBC + col_off)
    n_blk = n_rows // CONV_TR
    cb = (OFF_XBC + col_off) // CONV_CW
    pb = col_off // CONV_CW
    nxt = pl.BlockSpec((8, CONV_CW), lambda j, i: (jnp.minimum((i + 1) * (CONV_TR // 8), n_rows // 8 - 1), j + cb))
    d_cur = pl.BlockSpec((CONV_TR, CONV_CW), lambda j, i: (i, j))
    d_nxt = pl.BlockSpec((8, CONV_CW), lambda j, i: (jnp.minimum((i + 1) * (CONV_TR // 8), n_rows // 8 - 1), j))

    def dsilu(pre):
        sg = jax.nn.sigmoid(pre)
        return sg * (1.0 + pre * (1.0 - sg))

    def body(x_ref, hp_ref, hn_ref, d_ref, dn_ref, w_ref, b_ref, _, dx_ref, dw_ref, db_ref):
        i = pl.program_id(1)
        x, w, b = x_ref[...], w_ref[...], b_ref[...]
        halo_p = jnp.where(i % lt == 0, 0.0, hp_ref[...])
        at_end = i % lt == lt - 1
        dpre = d_ref[...] * dsilu(_conv_pre(x, halo_p, w, b))
        pre_n = _conv_pre(hn_ref[...], x[CONV_TR - 8:, :], w, b)
        dpre_n = jnp.where(at_end, 0.0, dn_ref[...] * dsilu(pre_n))
        dx = w[SSD_CONV - 1:SSD_CONV, :] * dpre
        for k in range(SSD_CONV - 1):
            dx = dx + w[k:k + 1, :] * _shift_up(dpre, dpre_n, SSD_CONV - 1 - k)
        dx_ref[...] = dx.astype(dx_ref.dtype)

        @pl.when(i == 0)
        def _():
            dw_ref[...] = jnp.zeros_like(dw_ref)
            db_ref[...] = jnp.zeros_like(db_ref)

        for k in range(SSD_CONV):
            xs = _shift_down(x, halo_p, SSD_CONV - 1 - k)
            dw_ref[k:k + 1, :] += jnp.sum(dpre * xs, axis=0, keepdims=True)
        db_ref[...] += jnp.sum(dpre, axis=0, keepdims=True)

    return pl.pallas_call(
        body, name=name, grid=(width // CONV_CW, n_blk),
        in_specs=[cur, prev, nxt, d_cur, d_nxt,
                  pl.BlockSpec((SSD_CONV, CONV_CW), lambda j, i: (0, j + pb)),
                  pl.BlockSpec((1, CONV_CW), lambda j, i: (0, j + pb)), pl.BlockSpec(memory_space=pl.ANY)],
        out_specs=[pl.BlockSpec((CONV_TR, CONV_CW), lambda j, i: (i, j + cb)),
                   pl.BlockSpec((SSD_CONV, CONV_CW), lambda j, i: (0, j)),
                   pl.BlockSpec((1, CONV_CW), lambda j, i: (0, j))],
        out_shape=[jax.ShapeDtypeStruct(dproj.shape, dproj.dtype),
                   jax.ShapeDtypeStruct((SSD_CONV, width), F32), jax.ShapeDtypeStruct((1, width), F32)],
        input_output_aliases={7: 0},
        compiler_params=_cparams(("parallel", "arbitrary")),
    )(proj, proj, proj, d_act, d_act, conv_w, conv_b, dproj)


def _split3(x):
    hi = x.astype(BF16)
    r = x - hi.astype(F32)
    mid = r.astype(BF16)
    return hi, mid, (r - mid.astype(F32)).astype(BF16)


def _sel_dot(a, b, a_is_sel):
    dn = (((1,), (0,)), ((), ()))
    if a_is_sel:
        return sum(lax.dot_general(a, t, dn, preferred_element_type=F32) for t in _split3(b))
    return sum(lax.dot_general(t, b, dn, preferred_element_type=F32) for t in _split3(a))


@jax.custom_vjp
def sel_left(sel, sel_t, x):
    return _sel_dot(sel, x, True)


@jax.custom_vjp
def sel_right(x, sel, sel_t):
    return _sel_dot(x, sel, False)


sel_left.defvjp(lambda s, st, x: (_sel_dot(s, x, True), (s, st)),
                lambda r, g: (jnp.zeros_like(r[0]), jnp.zeros_like(r[1]), _sel_dot(r[1], g, True)))
sel_right.defvjp(lambda x, s, st: (_sel_dot(x, s, False), (s, st)),
                 lambda r, g: (_sel_dot(g, r[1], False), jnp.zeros_like(r[0]), jnp.zeros_like(r[1])))


def _ssd_chunk(xs, bm, cm, dtr, st, dtb, alog, dsk, k):
    dt = jax.nn.softplus(dtr + dtb)
    acum = sel_left(k["tri"], k["tri_t"], dt * (-jnp.exp(alog)))
    dt_e = sel_right(dt, k["spread"], k["spread_t"])
    ac_e = sel_right(acum, k["spread"], k["spread_t"])
    al_e = ac_e[SSD_CHUNK - 1:SSD_CHUNK, :]
    dsk_e = sel_right(jnp.broadcast_to(dsk, (8, 128)), k["spread"], k["spread_t"])[0:1, :]
    xdt = xs * dt_e
    acum_t = acum.T
    scores = dot_nt(cm, bm)
    y = dot_nn(cm, st) * jnp.exp(ac_e) + xs * dsk_e
    for j in range(SSD_HPG):
        lmat = jnp.exp(jnp.where(k["causal"], acum[:, j:j + 1] - acum_t[j:j + 1, :], -jnp.inf))
        y = y + dot_nn(scores * lmat, jnp.where(k["head"] == j, xdt, 0.0))
    new = st * jnp.exp(al_e) + dot_tn(bm, xdt * jnp.exp(al_e - ac_e))
    return y, new


def _ssd_consts():
    r = lax.broadcasted_iota(jnp.int32, (SSD_CHUNK, SSD_CHUNK), 0)
    c = lax.broadcasted_iota(jnp.int32, (SSD_CHUNK, SSD_CHUNK), 1)
    hd = jnp.int32(SSD_HEADDIM)
    sr = lax.broadcasted_iota(jnp.int32, (128, GROUP_W), 0)
    sc = lax.div(lax.broadcasted_iota(jnp.int32, (128, GROUP_W), 1), hd)
    tr = lax.div(lax.broadcasted_iota(jnp.int32, (GROUP_W, 128), 0), hd)
    tc = lax.broadcasted_iota(jnp.int32, (GROUP_W, 128), 1)
    return {"tri": (r >= c).astype(BF16), "tri_t": (c >= r).astype(BF16), "causal": r >= c,
            "spread": (sr == sc).astype(BF16), "spread_t": (tr == tc).astype(BF16),
            "head": lax.div(lax.broadcasted_iota(jnp.int32, (SSD_CHUNK, GROUP_W), 1), hd)}


def _ssd_specs(n_c, reverse):
    def cidx(c):
        return n_c - 1 - c if reverse else c

    xs = pl.BlockSpec((SSD_CHUNK, GROUP_W), lambda g, b, c: (b * n_c + cidx(c), g))
    bm = pl.BlockSpec((SSD_CHUNK, SSD_STATE), lambda g, b, c: (b * n_c + cidx(c), SSD_WIDTH // SSD_STATE + g))
    cm = pl.BlockSpec((SSD_CHUNK, SSD_STATE), lambda g, b, c: (b * n_c + cidx(c), (SSD_WIDTH + SSD_BC) // SSD_STATE + g))
    dt = pl.BlockSpec((SSD_CHUNK, 128), lambda g, b, c: (b * n_c + cidx(c), OFF_DT // 128 + g))
    par = pl.BlockSpec((1, 128), lambda g, b, c: (0, g))
    st = pl.BlockSpec((1, 1, 1, SSD_STATE, GROUP_W), lambda g, b, c: (b, cidx(c), g, 0, 0))
    return xs, bm, cm, dt, par, st


def _ssd_chunk_gated(xs, bm, cm, dtr, st, dtb, alog, dsk, z, nw, k):
    y, new = _ssd_chunk(xs, bm, cm, dtr, st, dtb, alog, dsk, k)
    yg = y * jax.nn.silu(z)
    return yg * lax.rsqrt(jnp.mean(yg * yg, axis=-1, keepdims=True) + RMS_EPS) * nw, new


def _ssd_gate_specs(n_c, reverse):
    def cidx(c):
        return n_c - 1 - c if reverse else c

    z = pl.BlockSpec((SSD_CHUNK, GROUP_W), lambda g, b, c: (b * n_c + cidx(c), OFF_ZS // GROUP_W + g))
    nw = pl.BlockSpec((1, GROUP_W), lambda g, b, c: (0, g))
    return z, nw


def _ssd_fwd(xbc_act, proj, dtb, alog, dsk, norm_w, n_seq, seq_len):
    n_c = seq_len // SSD_CHUNK
    xs_s, bm_s, cm_s, dt_s, par_s, st_s = _ssd_specs(n_c, False)
    z_s, nw_s = _ssd_gate_specs(n_c, False)

    def body(xs_ref, bm_ref, cm_ref, dt_ref, dtb_ref, al_ref, dk_ref, z_ref, nw_ref, y_ref, st_ref, state):
        @pl.when(pl.program_id(2) == 0)
        def _():
            state[...] = jnp.zeros_like(state)

        prev = state[...]
        st_ref[0, 0, 0] = prev
        y, new = _ssd_chunk_gated(xs_ref[...], bm_ref[...], cm_ref[...], dt_ref[...], prev,
                                  dtb_ref[...], al_ref[...], dk_ref[...], z_ref[...], nw_ref[...], _ssd_consts())
        y_ref[...] = y.astype(y_ref.dtype)
        state[...] = new

    return pl.pallas_call(
        body, name="ssd_fwd", grid=(SSD_GROUPS, n_seq, n_c),
        in_specs=[xs_s, bm_s, cm_s, dt_s, par_s, par_s, par_s, z_s, nw_s],
        out_specs=[pl.BlockSpec((SSD_CHUNK, GROUP_W), lambda g, b, c: (b * n_c + c, g)), st_s],
        out_shape=[jax.ShapeDtypeStruct((n_seq * seq_len, SSD_WIDTH), BF16),
                   jax.ShapeDtypeStruct((n_seq, n_c, SSD_GROUPS, SSD_STATE, GROUP_W), F32)],
        scratch_shapes=[pltpu.VMEM((SSD_STATE, GROUP_W), F32)],
        compiler_params=_cparams(("parallel", "parallel", "arbitrary")),
    )(xbc_act, xbc_act, xbc_act, proj, dtb, alog, dsk, proj, norm_w)


def _ssd_bwd(xbc_act, proj, states, dy, dtb, alog, dsk, norm_w, n_seq, seq_len, dproj):
    n_c = seq_len // SSD_CHUNK
    n_rows = n_seq * seq_len
    xs_s, bm_s, cm_s, dt_s, par_s, st_s = _ssd_specs(n_c, True)
    z_s, nw_s = _ssd_gate_specs(n_c, True)

    def rows(w, first=0):
        return pl.BlockSpec((SSD_CHUNK, w), lambda g, b, c: (b * n_c + (n_c - 1 - c), first + g))

    def body(xs_ref, bm_ref, cm_ref, dt_ref, st_ref, dy_ref, dtb_ref, al_ref, dk_ref, z_ref, nw_ref, _,
             dxs_ref, dbm_ref, dcm_ref, dz_ref, ddt_ref, ddtb_ref, dal_ref, ddk_ref, dnw_ref, dstate):
        b, c = pl.program_id(1), pl.program_id(2)

        @pl.when(c == 0)
        def _():
            dstate[...] = jnp.zeros_like(dstate)

        @pl.when((b == 0) & (c == 0))
        def _():
            for ref in (ddtb_ref, dal_ref, ddk_ref, dnw_ref):
                ref[...] = jnp.zeros_like(ref)

        consts = _ssd_consts()
        _, vjp = jax.vjp(
            lambda *a: _ssd_chunk_gated(*a, consts),
            xs_ref[...], bm_ref[...], cm_ref[...], dt_ref[...], st_ref[0, 0, 0], dtb_ref[...], al_ref[...], dk_ref[...],
            z_ref[...], nw_ref[...])
        dxs, dbm, dcm, ddtr, dprev, ddtb, dal, ddk, dz, dnw = vjp((dy_ref[...], dstate[...]))
        dxs_ref[...] = dxs
        dbm_ref[...] = dbm
        dcm_ref[...] = dcm
        dz_ref[...] = dz.astype(dz_ref.dtype)
        ddt_ref[...] = ddtr.astype(ddt_ref.dtype)
        ddtb_ref[...] += ddtb
        dal_ref[...] += dal
        ddk_ref[...] += ddk
        dnw_ref[...] += dnw
        dstate[...] = dprev

    acc = pl.BlockSpec((1, 128), lambda g, b, c: (0, g))
    return pl.pallas_call(
        body, name="ssd_bwd", grid=(SSD_GROUPS, n_seq, n_c),
        in_specs=[xs_s, bm_s, cm_s, dt_s, st_s, rows(GROUP_W), par_s, par_s, par_s, z_s, nw_s,
                  pl.BlockSpec(memory_space=pl.ANY)],
        out_specs=[rows(GROUP_W), rows(SSD_STATE), rows(SSD_STATE), rows(GROUP_W, OFF_ZS // GROUP_W), rows(128),
                   acc, acc, acc, pl.BlockSpec((1, GROUP_W), lambda g, b, c: (0, g))],
        out_shape=[jax.ShapeDtypeStruct((n_rows, SSD_WIDTH), F32), jax.ShapeDtypeStruct((n_rows, SSD_BC), F32),
                   jax.ShapeDtypeStruct((n_rows, SSD_BC), F32), jax.ShapeDtypeStruct(dproj.shape, dproj.dtype),
                   jax.ShapeDtypeStruct((n_rows, DT_W), BF16),
                   jax.ShapeDtypeStruct((1, 512), F32), jax.ShapeDtypeStruct((1, 512), F32),
                   jax.ShapeDtypeStruct((1, 512), F32), jax.ShapeDtypeStruct((1, SSD_WIDTH), F32)],
        input_output_aliases={11: 3},
        scratch_shapes=[pltpu.VMEM((SSD_STATE, GROUP_W), F32)],
        compiler_params=_cparams(("parallel", "arbitrary", "arbitrary")),
    )(xbc_act, xbc_act, xbc_act, proj, states, dy, dtb, alog, dsk, proj, norm_w, dproj)


def _pad_heads(v):
    return jnp.pad(v.reshape(SSD_GROUPS, SSD_HPG), ((0, 0), (0, 128 - SSD_HPG))).reshape(1, SSD_GROUPS * 128)


def _unpad_heads(v):
    return v.reshape(SSD_GROUPS, 128)[:, :SSD_HPG].reshape(1, SSD_HEADS)


def _state_cols(v):
    re, im = v
    lead = re.shape[:-1]
    re = re.reshape(lead + (S5_NJ, 1, S5_LB))
    im = im.reshape(lead + (S5_NJ, 1, S5_LB))
    return jnp.concatenate([re, im], axis=-2).reshape(lead + (2 * S5_N,))


def _state_uncols(v):
    lead = v.shape[:-1]
    v = v.reshape(lead + (S5_NJ, 2, S5_LB))
    return v[..., 0, :].reshape(lead + (S5_N,)), v[..., 1, :].reshape(lead + (S5_N,))


GROUPS_PER_BAND = BAND // S5_GROUP


def _band(w2_re, w2_im):
    gh = S5_GROUPS * S5_GROUP
    rg = ((jnp.arange(gh) // S5_GROUP) % GROUPS_PER_BAND)[:, None, None]
    cg = jnp.arange(GROUPS_PER_BAND)[None, :, None]
    parts = [jnp.where(rg == cg, v[:, None, :], 0.0).reshape(gh, S5_LB) for v in (w2_re, w2_im)]
    return jnp.concatenate(parts, axis=1)


def _band_take(wb):
    gh = S5_GROUPS * S5_GROUP
    w4 = wb.reshape(gh, 2, GROUPS_PER_BAND, S5_STATE)
    sel = w4[jnp.arange(gh), :, (jnp.arange(gh) // S5_GROUP) % GROUPS_PER_BAND, :]
    return sel[:, 0, :], sel[:, 1, :]


W_IN_SHARD = IN_PROJ_DIM // N_CHIPS
W_IN_SEGS = ((0, 512, OFF_U5), (512, 1024, OFF_Z5), (1024, 2560, OFF_ZS), (2560, 5120, OFF_XBC), (5144, 7192, OFF_G5))
DT_ROWS = (5120, 5144)


def _w_in_pieces():
    runs = []
    segs = list(W_IN_SEGS) + [(DT_ROWS[0] + SSD_HPG * g, DT_ROWS[0] + SSD_HPG * (g + 1), OFF_DT + 128 * g)
                              for g in range(SSD_GROUPS)]
    for lo, hi, off in segs:
        for j in range(N_CHIPS):
            s, e = max(lo, j * W_IN_SHARD), min(hi, (j + 1) * W_IN_SHARD)
            if s < e:
                runs.append((j, s - j * W_IN_SHARD, off + s - lo, e - s))
    return runs


RELAYOUT_LANES = 256


def _pad_w_in_t(a4):
    runs = _w_in_pieces()

    def body(a_ref, o_ref):
        o_ref[pl.ds(OFF_DT, DT_W), :] = jnp.zeros((DT_W, RELAYOUT_LANES), o_ref.dtype)
        for j, src, dst, n in runs:
            o_ref[pl.ds(dst, n), :] = a_ref[j, pl.ds(src, n), :]

    return pl.pallas_call(
        body, name="w_in_to_padded", grid=(D_MODEL // RELAYOUT_LANES,),
        in_specs=[pl.BlockSpec((N_CHIPS, W_IN_SHARD, RELAYOUT_LANES), lambda i: (0, 0, i))],
        out_specs=pl.BlockSpec((PROJ_W, RELAYOUT_LANES), lambda i: (0, i)),
        out_shape=jax.ShapeDtypeStruct((PROJ_W, D_MODEL), a4.dtype),
        compiler_params=_cparams(("parallel",)),
    )(a4)


def _unpad_w_in_t(wp):
    runs = _w_in_pieces()

    def body(p_ref, o_ref):
        for j, dst, src, n in runs:
            o_ref[j, pl.ds(dst, n), :] = p_ref[pl.ds(src, n), :]

    return pl.pallas_call(
        body, name="w_in_from_padded", grid=(D_MODEL // RELAYOUT_LANES,),
        in_specs=[pl.BlockSpec((PROJ_W, RELAYOUT_LANES), lambda i: (0, i))],
        out_specs=pl.BlockSpec((N_CHIPS, W_IN_SHARD, RELAYOUT_LANES), lambda i: (0, 0, i)),
        out_shape=jax.ShapeDtypeStruct((N_CHIPS, W_IN_SHARD, D_MODEL), wp.dtype),
        compiler_params=_cparams(("parallel",)),
    )(wp)


def _local_step(x, p, tgt, w):
    n_seq, seq_len, _ = x.shape
    n_rows = n_seq * seq_len
    tr = 512
    x2 = x.reshape(n_rows, D_MODEL)
    p2 = p.reshape(n_rows, PLE_DIM)
    t2 = tgt.reshape(n_rows, D_MODEL)
    row = functools.partial(_rowwise, n_rows=n_rows, tr=tr)

    w_pad_t = _pad_w_in_t(w["w_in_t"])
    norm_w = w["norm_w"].reshape(1, D_MODEL)
    ple_norm_w = w["ple_norm_w"].reshape(1, D_MODEL)
    final_w = w["final_norm_w"].reshape(1, D_MODEL)
    s5_d = w["s5_d"].reshape(1, S5_WIDTH)
    b_glu = w["s5_b_glu"].reshape(1, S5_WIDTH)
    conv_w = w["ssd_conv_w"].reshape(SSD_CONV, SSD_CONV_DIM)
    conv_b = w["ssd_conv_b"].reshape(1, SSD_CONV_DIM)
    ssd_norm_w = w["ssd_norm_w"].reshape(1, SSD_WIDTH)
    dtb, alog, dsk = (_pad_heads(w[k].reshape(1, SSD_HEADS)) for k in ("ssd_dt_bias", "ssd_a_log", "ssd_d"))

    gh = S5_GROUPS * S5_GROUP
    a_re = w["s5_a_re"].reshape(S5_GROUPS, S5_STATE)
    a_im = w["s5_a_im"].reshape(S5_GROUPS, S5_STATE)
    log_step = w["s5_log_step"].reshape(S5_GROUPS, 1)
    b_re2 = jnp.transpose(w["s5_b_re"].reshape(S5_GROUPS, S5_STATE, S5_GROUP), (0, 2, 1)).reshape(gh, S5_STATE)
    b_im2 = jnp.transpose(w["s5_b_im"].reshape(S5_GROUPS, S5_STATE, S5_GROUP), (0, 2, 1)).reshape(gh, S5_STATE)
    expand = (jnp.arange(gh)[:, None] // S5_GROUP == jnp.arange(S5_GROUPS)[None, :]).astype(F32)
    pow_re, pow_im, bb_re2, bb_im2 = _s5_params_fwd(a_re, a_im, log_step, b_re2, b_im2, expand)
    lam_pow = _state_cols((pow_re.reshape(S5_LOG_TB, S5_N), pow_im.reshape(S5_LOG_TB, S5_N)))
    lam_pow_conj = _state_cols((pow_re.reshape(S5_LOG_TB, S5_N), -pow_im.reshape(S5_LOG_TB, S5_N)))
    bb_band = _band(bb_re2, bb_im2).astype(BF16)
    c_band = _band(w["s5_c_re"].reshape(gh, S5_STATE), -w["s5_c_im"].reshape(gh, S5_STATE)).astype(BF16)

    late = w.get("_late")
    (hn,) = row("rms_in", lambda r, q: ([_rms(r[0], q[0])], []), row_ins=[(x2, 0, D_MODEL)],
                par_ins=[(norm_w, 0, D_MODEL)] + ([(late[0], 0, 128)] if late else []), row_outs=[(D_MODEL, BF16)])
    proj = _matmul("mm_proj", hn, w_pad_t, tb=True)
    s, yc, ge = _s5_fwd(proj, bb_band, c_band, lam_pow, s5_d, n_seq, seq_len)
    if late:
        w = {**w, **late[1](ge)}
    tg, y5 = _matmul("mm_s5_glu", ge, w["s5_w_glu"], epilogue=lambda c, r, q: [c, _s5_out(r[0], r[1], r[2], c, *q)],
                     epi_rows=[(yc, 0), (proj, OFF_U5), (proj, OFF_Z5)], epi_pars=[s5_d, b_glu], epi_outs=[F32, BF16])

    xbc_act = _conv_fwd(proj, conv_w, conv_b, n_rows, seq_len)
    yss, states = _ssd_fwd(xbc_act, proj, dtb, alog, dsk, ssd_norm_w, n_seq, seq_len)

    m5 = _matmul("mm_br_s5", y5, w["w_br_s5"])
    ms, merged = _matmul("mm_br_ssd", yss, w["w_br_ssd"], epilogue=lambda c, r, q: [c, _merge(r[0], r[1], r[2], c)],
                         epi_rows=[(proj, OFF_G5), (proj, OFF_GS), (m5, 0)], epi_outs=[F32, BF16])

    def resid_norm(c, r, q):
        h1_ = r[0] + c
        return [h1_, _rms(h1_, q[0])]

    h1, hp = _matmul("mm_out", merged, w["w_out"], epilogue=resid_norm, epi_rows=[(x2, 0)], epi_pars=[ple_norm_w],
                     epi_outs=[F32, BF16], full_rows=True)
    pp = _matmul("mm_ple_proj", p2, w["w_ple_proj"])

    def head_fn(pgl_, r, q):
        h1_, pp_, tgt_ = r
        loss, vjp = jax.vjp(lambda a, b, c, f: _head_loss(a, b, c, f, tgt_), h1_, pgl_, pp_, q[0])
        dh1_, dpgl_, dpp_, dfw_ = vjp(jnp.ones_like(loss))
        return [dh1_, dpgl_, dpp_], [loss, dfw_]

    dh2, dpgl, dpp, loss_acc, d_final_w = _matmul(
        "mm_ple_gate_head", hp, w["w_ple_gate"], epilogue=head_fn, epi_rows=[(h1, 0), (pp, 0), (t2, 0)],
        epi_pars=[final_w], epi_outs=[F32, BF16, BF16], epi_accs=[(1, 128), (1, D_MODEL)], full_rows=True)
    loss = loss_acc[0, 0]

    g = {}
    g["final_norm_w"] = d_final_w
    g["w_ple_gate"] = _matmul("mm_d_w_ple_gate", hp, dpgl, ta=True)
    g["w_ple_proj"] = _matmul("mm_d_w_ple_proj", p2, dpp, ta=True)

    def ple_norm_bwd(dhp_, r, q):
        h1_, dh2_ = r
        _, vjp = jax.vjp(_rms, h1_, q[0])
        dh, dw = vjp(dhp_)
        dh = dh + dh2_
        return [dh, dh], [dw]

    dh1, dh1_b, g["ple_norm_w"] = _matmul(
        "mm_d_hp", dpgl, w["w_ple_gate"], tb=True, epilogue=ple_norm_bwd, epi_rows=[(h1, 0), (dh2, 0)],
        epi_pars=[ple_norm_w], epi_outs=[F32, BF16], epi_accs=[(1, D_MODEL)], full_rows=True)
    g["w_out"] = _matmul("mm_d_w_out", merged, dh1_b, ta=True)

    dproj = lax.empty((n_rows, PROJ_W), BF16)

    def merge_bwd(dmerged, r, q):
        sg5, sgs = jax.nn.sigmoid(r[0]), jax.nn.sigmoid(r[1])
        d_gates = jnp.concatenate([dmerged * r[2] * sg5 * (1.0 - sg5), dmerged * r[3] * sgs * (1.0 - sgs)], axis=1)
        return [d_gates, dmerged * sg5, dmerged * sgs]

    dproj, dm5, dms = _matmul(
        "mm_d_merged", dh1_b, w["w_out"], tb=True, epilogue=merge_bwd,
        epi_rows=[(proj, OFF_G5), (proj, OFF_GS), (m5, 0), (ms, 0)], epi_outs=[BF16, BF16, BF16], full_rows=True,
        epi_into=(dproj, OFF_G5, 2 * D_MODEL))
    g["w_br_s5"] = _matmul("mm_d_w_br_s5", y5, dm5, ta=True)
    g["w_br_ssd"] = _matmul("mm_d_w_br_ssd", yss, dms, ta=True)
    dyss = _matmul("mm_d_yss", dms, w["w_br_ssd"], tb=True)

    def s5_out_bwd_a(dy_, r, q):
        yc_, u_, z_, t_ = r
        d_, bg_ = q
        ge_ = jax.nn.gelu(yc_ + d_ * u_)
        _, vjp = jax.vjp(lambda a, z, t, b: a * jax.nn.sigmoid(t + b) * jax.nn.silu(z), ge_, z_, t_, bg_)
        dge, dz, dt_, dbg = vjp(dy_)
        return [dz, dge, dt_], [dbg]

    dproj, dge_a, dtg, g["s5_b_glu"] = _matmul(
        "mm_d_y5", dm5, w["w_br_s5"], tb=True, epilogue=s5_out_bwd_a,
        epi_rows=[(yc, 0), (proj, OFF_U5), (proj, OFF_Z5), (tg, 0)], epi_pars=[s5_d, b_glu],
        epi_outs=[BF16, F32, BF16], epi_accs=[(1, S5_WIDTH)], full_rows=True, epi_into=(dproj, OFF_Z5, S5_WIDTH))
    g["s5_w_glu"] = _matmul("mm_d_w_glu", ge, dtg, ta=True)
    early = w["_early"](g) if "_early" in w else None

    def s5_out_bwd_b(dge_b, r, q):
        yc_, u_, da_ = r
        _, vjp = jax.vjp(lambda yc, u, d: jax.nn.gelu(yc + d * u), yc_, u_, q[0])
        dyc_, du_, dd_ = vjp(da_ + dge_b)
        return [dyc_, du_], [dd_]

    dyc, du5_a, g["s5_d"] = _matmul(
        "mm_d_ge", dtg, w["s5_w_glu"], tb=True, epilogue=s5_out_bwd_b,
        epi_rows=[(yc, 0), (proj, OFF_U5), (dge_a, 0)], epi_pars=[s5_d], epi_outs=[BF16, F32],
        epi_accs=[(1, S5_WIDTH)], full_rows=True, dep=early)
    dproj, d_lam, d_bb_band, d_c_band = _s5_bwd(dyc, s, proj, du5_a, bb_band, c_band, lam_pow_conj, dproj,
                                                n_seq, seq_len)

    d_lr, d_li = _state_uncols(d_lam)
    d_bbr, d_bbi = _band_take(d_bb_band)
    d_are, d_aim, d_ls, d_br2, d_bi2 = _s5_params_bwd(
        a_re, a_im, log_step, b_re2, b_im2, expand,
        d_lr.reshape(S5_GROUPS, S5_STATE), d_li.reshape(S5_GROUPS, S5_STATE), d_bbr, d_bbi)
    g["s5_a_re"], g["s5_a_im"], g["s5_log_step"] = d_are, d_aim, d_ls
    g["s5_b_re_ghp"], g["s5_b_im_ghp"] = d_br2, d_bi2
    d_cr, d_ci = _band_take(d_c_band)
    g["s5_c_re"], g["s5_c_im"] = d_cr, -d_ci

    dxs, dbm, dcm, dproj, ddt, d_dtb, d_alog, d_dsk, g["ssd_norm_w"] = _ssd_bwd(
        xbc_act, proj, states, dyss, dtb, alog, dsk, ssd_norm_w, n_seq, seq_len, dproj)
    (dproj,) = row("ssd_ddt", lambda r, q: ([r[0]], []), row_ins=[(ddt, 0, DT_W)], par_ins=[],
                   row_outs=[(DT_W, BF16)], into=(dproj, OFF_DT))
    g["ssd_dt_bias"], g["ssd_a_log"], g["ssd_d"] = _unpad_heads(d_dtb), _unpad_heads(d_alog), _unpad_heads(d_dsk)
    conv_dw, conv_db = [], []
    for nm, d_act, off in (("x", dxs, 0), ("b", dbm, SSD_WIDTH), ("c", dcm, SSD_WIDTH + SSD_BC)):
        dproj, dw_, db_ = _conv_bwd("ssd_conv_bwd_" + nm, proj, d_act, conv_w, conv_b, n_rows, seq_len, off, dproj)
        conv_dw.append(dw_)
        conv_db.append(db_)
    g["ssd_conv_w"] = jnp.concatenate(conv_dw, axis=1)
    g["ssd_conv_b"] = jnp.concatenate(conv_db, axis=1)

    g["w_in_t"] = _unpad_w_in_t(_matmul("mm_d_w_in", dproj, hn, ta=True))
    def norm_bwd(dhn_, r, q):
        x_, dh1_ = r
        _, vjp = jax.vjp(_rms, x_, q[0])
        dx_, dw_ = vjp(dhn_)
        return [dx_ + dh1_], [dw_]

    sent = w["_w_in_ready"](g, loss) if "_w_in_ready" in w else None
    dx, g["norm_w"] = _matmul("mm_d_hn", dproj, w_pad_t, epilogue=norm_bwd, epi_rows=[(x2, 0), (dh1, 0)],
                              epi_pars=[norm_w], epi_outs=[F32], epi_accs=[(1, D_MODEL)], full_rows=True, dep=sent)
    return loss, dx.reshape(x.shape), g


HBM = pl.BlockSpec(memory_space=pltpu.HBM)


def _chip_index(x, y):
    return 2 * x + y


def _half(shape2d, axis, which):
    h = shape2d[axis] // 2
    sl = pl.ds(pl.multiple_of(which * h, 128 if axis else 8), h)
    return (slice(None), sl) if axis else (sl, slice(None))


def _gather_chips(split, axes, whole):
    ns, nw = len(split), len(whole)
    n = ns + nw

    def body(*refs):
        ins, outs = refs[:n], refs[n:2 * n]
        ici_send, ici_recv, d2d_send, d2d_recv, local_sems = refs[2 * n:]
        x, y, c = lax.axis_index("x"), lax.axis_index("y"), lax.axis_index("c")
        me = _chip_index(x, y)
        sibling = (x, y, 1 - c)
        peers = [(1 - x, y), (x, 1 - y), (1 - x, 1 - y)]

        def half(t, which):
            return _half(split[t].shape, axes[t], which)

        copies = []
        for t in range(n):
            loc = pltpu.make_async_copy(ins[t], outs[t].at[me], local_sems.at[t])
            loc.start()
            copies.append(loc)

        def ici(t, k, slot):
            px, py = peers[k]
            if t < ns:
                src, dst = ins[t].at[half(t, c)], outs[t].at[(slot,) + half(t, c)]
            else:
                src, dst = ins[t], outs[t].at[slot]
            return pltpu.make_async_remote_copy(src_ref=src, dst_ref=dst, send_sem=ici_send.at[t, k],
                                                recv_sem=ici_recv.at[t, k], device_id=(px, py, c), device_id_type=MESH)

        def d2d(t, k, which):
            rows = outs[t].at[(_chip_index(*peers[k]),) + half(t, which)]
            return pltpu.make_async_remote_copy(src_ref=rows, dst_ref=rows, send_sem=d2d_send.at[t, k],
                                                recv_sem=d2d_recv.at[t, k], device_id=sibling, device_id_type=MESH)

        sends = []
        for t in range(n):
            for k in range(3):
                cp = ici(t, k, me)
                cp.start()
                sends.append(cp)
        for t in range(n):
            for k in range(3):
                ici(t, k, _chip_index(*peers[k])).wait_recv()
                if t < ns:
                    cp = d2d(t, k, c)
                    cp.start()
                    sends.append(cp)
        for t in range(ns):
            for k in range(3):
                d2d(t, k, 1 - c).wait_recv()
        for cp in sends:
            cp.wait_send()
        for cp in copies:
            cp.wait()

    arrays = list(split) + list(whole)
    return pl.pallas_call(
        body, name="gather_weights",
        in_specs=[HBM] * n, out_specs=[HBM] * n,
        out_shape=[jax.ShapeDtypeStruct((N_CHIPS,) + a.shape, a.dtype) for a in arrays],
        scratch_shapes=[pltpu.SemaphoreType.DMA((n, 3)), pltpu.SemaphoreType.DMA((n, 3)),
                        pltpu.SemaphoreType.DMA((ns, 3)), pltpu.SemaphoreType.DMA((ns, 3)),
                        pltpu.SemaphoreType.DMA((n,))],
    )(*arrays)


SEM = pl.BlockSpec(memory_space=pltpu.SEMAPHORE)
DATAFLOW = pltpu.SideEffectType.DATAFLOW_SIDE_EFFECTING


def _gather_start(shards, after):
    n = len(shards)

    def body(*refs):
        ins, lands = refs[:n], refs[n:2 * n]
        send_sems, recv_sems = refs[2 * n + 1], refs[2 * n + 2]
        token = refs[-1]
        x, y, c = lax.axis_index("x"), lax.axis_index("y"), lax.axis_index("c")
        me = _chip_index(x, y)
        for t in range(n):
            for k, (px, py) in enumerate([(1 - x, y), (x, 1 - y), (1 - x, 1 - y)]):
                pltpu.make_async_remote_copy(
                    src_ref=ins[t], dst_ref=lands[t].at[me], send_sem=send_sems.at[3 * t + k],
                    recv_sem=recv_sems.at[3 * t + k],
                    device_id=(px, py, c), device_id_type=MESH).start()
        token[...] = jnp.zeros_like(token)

    zones = [lax.empty((N_CHIPS,) + a.shape, a.dtype) for a in shards]
    res = pl.pallas_call(
        body, name="gather_rest_start",
        out_shape=(pltpu.SemaphoreType.DMA((3 * n,)), pltpu.SemaphoreType.DMA((3 * n,)),
                   *[pltpu.HBM(a.shape, a.dtype) for a in shards], *[pltpu.HBM(z.shape, z.dtype) for z in zones],
                   jax.ShapeDtypeStruct((8, 128), F32)),
        in_specs=[HBM] * (2 * n) + [pl.BlockSpec(memory_space=pl.ANY)],
        out_specs=(SEM, SEM, *[HBM] * (2 * n), pl.BlockSpec(memory_space=pltpu.VMEM)),
        input_output_aliases={t: 2 + t for t in range(2 * n)},
        compiler_params=pltpu.CompilerParams(has_side_effects=DATAFLOW),
    )(*[pltpu.with_memory_space_constraint(a, pltpu.HBM) for a in shards],
      *[pltpu.with_memory_space_constraint(z, pltpu.HBM) for z in zones], after)
    return res[0], res[1], list(res[2:2 + n]), list(res[2 + n:2 + 2 * n]), res[-1]


def _gather_wait(send_sems, recv_sems, thru, lands, after):
    n = len(thru)

    def body(*refs):
        ins, zones = refs[:n], refs[n:2 * n]
        s_sems, r_sems = refs[2 * n], refs[2 * n + 1]
        x, y, c = lax.axis_index("x"), lax.axis_index("y"), lax.axis_index("c")
        for t in range(n):
            for k, (px, py) in enumerate([(1 - x, y), (x, 1 - y), (1 - x, 1 - y)]):
                cp = pltpu.make_async_remote_copy(
                    src_ref=ins[t], dst_ref=zones[t].at[_chip_index(px, py)], send_sem=s_sems.at[3 * t + k],
                    recv_sem=r_sems.at[3 * t + k], device_id=(px, py, c), device_id_type=MESH)
                cp.wait_send()
                cp.wait_recv()

    res = pl.pallas_call(
        body, name="gather_rest_wait",
        out_shape=(*[pltpu.HBM(a.shape, a.dtype) for a in thru], *[pltpu.HBM(z.shape, z.dtype) for z in lands]),
        in_specs=[HBM] * (2 * n) + [SEM, SEM, pl.BlockSpec(memory_space=pl.ANY)], out_specs=[HBM] * (2 * n),
        input_output_aliases={t: t for t in range(2 * n)},
        compiler_params=pltpu.CompilerParams(has_side_effects=DATAFLOW),
    )(*thru, *lands, send_sems, recv_sems, after)
    return list(res[:n]), list(res[n:])


def _scatter_ends(t, k, n_slotted, ins, zones, x, y, me):
    px, py = [(1 - x, y), (x, 1 - y), (1 - x, 1 - y)][k]
    if t < n_slotted:
        return ins[t].at[_chip_index(px, py)], zones[t].at[k], (px, py)
    return ins[t], (zones[t].at[me], zones[t].at[_chip_index(px, py)]), (px, py)


def _scatter_start(name, slotted, whole, after):
    n_s = len(slotted)
    arrays = list(slotted) + list(whole)
    n = len(arrays)

    def body(*refs):
        ins, lands = refs[:n], refs[n:2 * n]
        send_sems, recv_sems = refs[2 * n + 1], refs[2 * n + 2]
        token = refs[-1]
        x, y, c = lax.axis_index("x"), lax.axis_index("y"), lax.axis_index("c")
        for t in range(n):
            for k in range(3):
                src, dst, (px, py) = _scatter_ends(t, k, n_s, ins, lands, x, y, _chip_index(x, y))
                pltpu.make_async_remote_copy(
                    src_ref=src, dst_ref=dst if t < n_s else dst[0], send_sem=send_sems.at[3 * t + k],
                    recv_sem=recv_sems.at[3 * t + k], device_id=(px, py, c), device_id_type=MESH).start()
        token[...] = jnp.zeros_like(token)

    zones = [lax.empty((3,) + a.shape[1:], a.dtype) for a in slotted]
    zones += [lax.empty((N_CHIPS,) + a.shape, a.dtype) for a in whole]
    slotted = arrays
    res = pl.pallas_call(
        body, name=name,
        out_shape=(pltpu.SemaphoreType.DMA((3 * n,)), pltpu.SemaphoreType.DMA((3 * n,)),
                   *[pltpu.HBM(a.shape, a.dtype) for a in slotted], *[pltpu.HBM(z.shape, z.dtype) for z in zones],
                   jax.ShapeDtypeStruct((8, 128), F32)),
        in_specs=[HBM] * (2 * n) + [pl.BlockSpec(memory_space=pl.ANY)],
        out_specs=(SEM, SEM, *[HBM] * (2 * n), pl.BlockSpec(memory_space=pltpu.VMEM)),
        input_output_aliases={t: 2 + t for t in range(2 * n)},
        compiler_params=pltpu.CompilerParams(has_side_effects=DATAFLOW),
    )(*[pltpu.with_memory_space_constraint(a, pltpu.HBM) for a in slotted],
      *[pltpu.with_memory_space_constraint(z, pltpu.HBM) for z in zones], after)
    return res[0], res[1], list(res[2:2 + n]), list(res[2 + n:2 + 2 * n]), res[-1]


def _scatter_wait(name, n_slotted, send_sems, recv_sems, thru, lands, after):
    n = len(thru)

    def body(*refs):
        ins, zones = refs[:n], refs[n:2 * n]
        s_sems, r_sems = refs[2 * n], refs[2 * n + 1]
        x, y, c = lax.axis_index("x"), lax.axis_index("y"), lax.axis_index("c")
        for t in range(n):
            for k in range(3):
                src, dst, (px, py) = _scatter_ends(t, k, n_slotted, ins, zones, x, y, _chip_index(x, y))
                cp = pltpu.make_async_remote_copy(
                    src_ref=src, dst_ref=dst if t < n_slotted else dst[1], send_sem=s_sems.at[3 * t + k],
                    recv_sem=r_sems.at[3 * t + k], device_id=(px, py, c), device_id_type=MESH)
                cp.wait_send()
                cp.wait_recv()

    res = pl.pallas_call(
        body, name=name,
        out_shape=(*[pltpu.HBM(a.shape, a.dtype) for a in thru], *[pltpu.HBM(z.shape, z.dtype) for z in lands]),
        in_specs=[HBM] * (2 * n) + [SEM, SEM, pl.BlockSpec(memory_space=pl.ANY)], out_specs=[HBM] * (2 * n),
        input_output_aliases={t: t for t in range(2 * n)},
        compiler_params=pltpu.CompilerParams(has_side_effects=DATAFLOW),
    )(*thru, *lands, send_sems, recv_sems, after)
    return list(res[:n]), list(res[n:])


def _sum_chips(name, slotted, recv, idx, dep):
    _, r, c = slotted.shape
    tr = _row_tile(r, 5 * c * 4)

    def body(idx_ref, own_ref, r_ref, _, o_ref):
        acc = own_ref[0]
        for k in range(3):
            acc = acc + r_ref[k]
        o_ref[...] = acc

    grid_spec = pltpu.PrefetchScalarGridSpec(
        num_scalar_prefetch=1, grid=(r // tr,),
        in_specs=[pl.BlockSpec((1, tr, c), lambda i, s: (s[1], i, 0)), pl.BlockSpec((3, tr, c), lambda i, s: (0, i, 0)),
                  pl.BlockSpec(memory_space=pl.ANY)],
        out_specs=pl.BlockSpec((tr, c), lambda i, s: (i, 0)))
    return pl.pallas_call(
        body, name=name, grid_spec=grid_spec, out_shape=jax.ShapeDtypeStruct((r, c), F32),
        compiler_params=_cparams(("parallel",)),
    )(idx, slotted, recv, dep)


def _half_shape(shape2d, axis):
    r, c = shape2d
    return (r, c // 2) if axis else (r // 2, c)


def _swap_halves(slotted, axes, small):
    n = len(slotted)

    def body(*refs):
        ins, sm_in = refs[:n], refs[n]
        outs, sm_out = refs[n + 1:2 * n + 1], refs[2 * n + 1]
        send_sems, recv_sems, local_sem = refs[2 * n + 2:]
        x, y, c = lax.axis_index("x"), lax.axis_index("y"), lax.axis_index("c")
        local = pltpu.make_async_copy(sm_in, sm_out.at[c], local_sem)
        local.start()
        sends = []
        for t in range(n):
            other = (slice(None),) + _half(slotted[t].shape[1:], axes[t], 1 - c)
            cp = pltpu.make_async_remote_copy(
                src_ref=ins[t].at[other], dst_ref=outs[t], send_sem=send_sems.at[t],
                recv_sem=recv_sems.at[t], device_id=(x, y, 1 - c), device_id_type=MESH)
            cp.start()
            sends.append(cp)
        cp = pltpu.make_async_remote_copy(
            src_ref=sm_in, dst_ref=sm_out.at[c], send_sem=send_sems.at[n], recv_sem=recv_sems.at[n],
            device_id=(x, y, 1 - c), device_id_type=MESH)
        cp.start()
        sends.append(cp)
        for cp in sends[:n]:
            cp.wait_recv()
        pltpu.make_async_remote_copy(
            src_ref=sm_in, dst_ref=sm_out.at[1 - c], send_sem=send_sems.at[n], recv_sem=recv_sems.at[n],
            device_id=(x, y, 1 - c), device_id_type=MESH).wait_recv()
        for cp in sends:
            cp.wait_send()
        local.wait()

    return pl.pallas_call(
        body, name="swap_halves",
        in_specs=[HBM] * (n + 1), out_specs=[HBM] * (n + 1),
        out_shape=[jax.ShapeDtypeStruct((a.shape[0],) + _half_shape(a.shape[1:], ax), a.dtype)
                   for a, ax in zip(slotted, axes, strict=True)]
        + [jax.ShapeDtypeStruct((2,) + small.shape, small.dtype)],
        scratch_shapes=[pltpu.SemaphoreType.DMA((n + 1,)), pltpu.SemaphoreType.DMA((n + 1,)), pltpu.SemaphoreType.DMA],
    )(*slotted, small)


def _tiling(r, c, f32_per_elem):
    if r % 8 == 0:
        tr = _row_tile(r, f32_per_elem * c * 4)
        return r // tr, (tr, c), lambda i: (i, 0)
    assert c % 128 == 0, (r, c)
    return c // 128, (r, 128), lambda i: (0, i)


def _pair_sum(name, slotted, other, idx, axis):
    _, r, c = slotted.shape
    hr, hc = _half_shape((r, c), axis)
    n, (tr, tc), at = _tiling(hr, hc, 13)
    if axis == 0:
        a = slotted.reshape(N_CHIPS, 2, hr, c)
        a_all = pl.BlockSpec((N_CHIPS, 1, tr, tc), lambda i, s: (0, s[0]) + at(i))
        a_own = pl.BlockSpec((1, 1, tr, tc), lambda i, s: (s[1], s[0]) + at(i))
    else:
        a, per_half = slotted, hc // tc
        a_all = pl.BlockSpec((N_CHIPS, tr, tc), lambda i, s: (0, at(i)[0], s[0] * per_half + at(i)[1]))
        a_own = pl.BlockSpec((1, tr, tc), lambda i, s: (s[1], at(i)[0], s[0] * per_half + at(i)[1]))

    def body(idx_ref, a_ref, b_ref, am_ref, bm_ref, p_ref, own_ref):
        mine, mine_own = (a_ref[:, 0], am_ref[0, 0]) if axis == 0 else (a_ref[...], am_ref[0])
        p_ref[...] = (mine + b_ref[...]).astype(p_ref.dtype)
        own_ref[...] = mine_own + bm_ref[0]

    grid_spec = pltpu.PrefetchScalarGridSpec(
        num_scalar_prefetch=1, grid=(n,),
        in_specs=[a_all, pl.BlockSpec((N_CHIPS, tr, tc), lambda i, s: (0,) + at(i)),
                  a_own, pl.BlockSpec((1, tr, tc), lambda i, s: (s[1],) + at(i))],
        out_specs=[pl.BlockSpec((N_CHIPS, tr, tc), lambda i, s: (0,) + at(i)),
                   pl.BlockSpec((tr, tc), lambda i, s: at(i))])
    return pl.pallas_call(
        body, name=name, grid_spec=grid_spec,
        out_shape=[jax.ShapeDtypeStruct((N_CHIPS, hr, hc), BF16), jax.ShapeDtypeStruct((hr, hc), F32)],
        compiler_params=_cparams(("parallel",)),
    )(idx, a, other, a, other)


def _sum_parts(name, own, recv):
    h, c = own.shape
    n, (tr, tc), at = _tiling(h, c, 4)

    def body(o_ref, r_ref, out_ref):
        acc = o_ref[...]
        for k in range(3):
            acc = acc + r_ref[k].astype(F32)
        out_ref[...] = acc

    return pl.pallas_call(
        body, name=name, grid=(n,),
        in_specs=[pl.BlockSpec((tr, tc), at), pl.BlockSpec((3, tr, tc), lambda i: (0,) + at(i))],
        out_specs=pl.BlockSpec((tr, tc), at),
        out_shape=jax.ShapeDtypeStruct((h, c), F32),
        compiler_params=_cparams(("parallel",)),
    )(own, recv)


def _all_to_all(name, small):
    def body(sm_in, sm_out, send_sems, recv_sems, local_sem):
        x, y, c = lax.axis_index("x"), lax.axis_index("y"), lax.axis_index("c")
        dev = 4 * x + 2 * y + c
        local = pltpu.make_async_copy(sm_in, sm_out.at[dev], local_sem)
        local.start()
        rel = [(fx, fy, fc) for fx in (0, 1) for fy in (0, 1) for fc in (0, 1)][1:]
        sends = []
        for k, (fx, fy, fc) in enumerate(rel):
            cp = pltpu.make_async_remote_copy(
                src_ref=sm_in, dst_ref=sm_out.at[dev], send_sem=send_sems.at[k], recv_sem=recv_sems.at[k],
                device_id=(x ^ fx, y ^ fy, c ^ fc), device_id_type=MESH)
            cp.start()
            sends.append(cp)
        for k, (fx, fy, fc) in enumerate(rel):
            src_dev = 4 * (x ^ fx) + 2 * (y ^ fy) + (c ^ fc)
            pltpu.make_async_remote_copy(
                src_ref=sm_in, dst_ref=sm_out.at[src_dev], send_sem=send_sems.at[k], recv_sem=recv_sems.at[k],
                device_id=(x ^ fx, y ^ fy, c ^ fc), device_id_type=MESH).wait_recv()
        for cp in sends:
            cp.wait_send()
        local.wait()

    return pl.pallas_call(
        body, name=name, in_specs=[HBM], out_specs=HBM,
        out_shape=jax.ShapeDtypeStruct((N_DEV,) + small.shape, small.dtype),
        scratch_shapes=[pltpu.SemaphoreType.DMA((7,)), pltpu.SemaphoreType.DMA((7,)), pltpu.SemaphoreType.DMA],
    )(small)


def _swap_sibling(name, parts):
    n = len(parts)

    def body(*refs):
        ins, outs = refs[:n], refs[n:2 * n]
        send_sems, recv_sems = refs[2 * n:]
        x, y, c = lax.axis_index("x"), lax.axis_index("y"), lax.axis_index("c")
        cps = []
        for t in range(n):
            cp = pltpu.make_async_remote_copy(
                src_ref=ins[t], dst_ref=outs[t], send_sem=send_sems.at[t], recv_sem=recv_sems.at[t],
                device_id=(x, y, 1 - c), device_id_type=MESH)
            cp.start()
            cps.append(cp)
        for cp in cps:
            cp.wait_recv()
        for cp in cps:
            cp.wait_send()

    return pl.pallas_call(
        body, name=name,
        in_specs=[HBM] * n, out_specs=[HBM] * n,
        out_shape=[jax.ShapeDtypeStruct(a.shape, a.dtype) for a in parts],
        scratch_shapes=[pltpu.SemaphoreType.DMA((n,)), pltpu.SemaphoreType.DMA((n,))],
    )(*parts)


def _sum_slots(name, a):
    k, r, c = a.shape
    tr = _row_tile(r, (k + 1) * c * 4)

    def body(a_ref, o_ref):
        acc = a_ref[0]
        for i in range(1, k):
            acc = acc + a_ref[i]
        o_ref[...] = acc

    return pl.pallas_call(
        body, name=name, grid=(r // tr,),
        in_specs=[pl.BlockSpec((k, tr, c), lambda i: (0, i, 0))],
        out_specs=pl.BlockSpec((tr, c), lambda i: (i, 0)),
        out_shape=jax.ShapeDtypeStruct((r, c), a.dtype),
        compiler_params=_cparams(("parallel",)),
    )(a)


def _adamw(name, w, m, v, g_parts):
    r, c = w.shape
    ng = len(g_parts)
    tr = _row_tile(r, (7 + ng) * c * 4)
    c1 = 1.0 - ADAM_B1 ** ADAM_STEP
    c2 = 1.0 - ADAM_B2 ** ADAM_STEP

    def body(*refs):
        w_ref, m_ref, v_ref = refs[:3]
        g_refs = refs[3:3 + ng]
        go_ref, d_ref, mo_ref, vo_ref = refs[3 + ng:]
        g = g_refs[0][...]
        for gr in g_refs[1:]:
            g = g + gr[...]
        m_new = ADAM_B1 * m_ref[...] + (1.0 - ADAM_B1) * g
        v_new = ADAM_B2 * v_ref[...] + (1.0 - ADAM_B2) * (g * g)
        go_ref[...] = g
        mo_ref[...] = m_new
        vo_ref[...] = v_new
        d_ref[...] = -ADAM_LR * ((m_new / c1) / (jnp.sqrt(v_new / c2) + ADAM_EPS) + ADAM_WD * w_ref[...])

    spec = pl.BlockSpec((tr, c), lambda i: (i, 0))
    return pl.pallas_call(
        body, name=name, grid=(r // tr,),
        in_specs=[spec] * (3 + ng), out_specs=[spec] * 4,
        out_shape=[jax.ShapeDtypeStruct((r, c), F32)] * 4,
        compiler_params=_cparams(("parallel",)),
    )(w, m, v, *g_parts)


def _adamw_halves(name, w, m, v, own, other, idx, axis):
    hr, hc = own.shape
    nb, (tr, tc), at = _tiling(hr, hc, 9)
    c1 = 1.0 - ADAM_B1 ** ADAM_STEP
    c2 = 1.0 - ADAM_B2 ** ADAM_STEP

    def body(idx_ref, w_ref, m_ref, v_ref, own_ref, oth_ref, go_ref, d_ref, mo_ref, vo_ref):
        g = jnp.where(pl.program_id(0) == idx_ref[0], own_ref[...], oth_ref[...])
        m_new = ADAM_B1 * m_ref[...] + (1.0 - ADAM_B1) * g
        v_new = ADAM_B2 * v_ref[...] + (1.0 - ADAM_B2) * (g * g)
        go_ref[...] = g
        mo_ref[...] = m_new
        vo_ref[...] = v_new
        d_ref[...] = -ADAM_LR * ((m_new / c1) / (jnp.sqrt(v_new / c2) + ADAM_EPS) + ADAM_WD * w_ref[...])

    per_half = (hc // tc) if axis else (hr // tr)
    if axis:
        full = pl.BlockSpec((tr, tc), lambda hh, i, s: (at(i)[0], hh * per_half + at(i)[1]))
    else:
        full = pl.BlockSpec((tr, tc), lambda hh, i, s: (hh * per_half + at(i)[0], at(i)[1]))
    part = pl.BlockSpec((tr, tc), lambda hh, i, s: at(i))
    grid_spec = pltpu.PrefetchScalarGridSpec(
        num_scalar_prefetch=1, grid=(2, nb), in_specs=[full, full, full, part, part], out_specs=[full] * 4)
    return pl.pallas_call(
        body, name=name, grid_spec=grid_spec, out_shape=[jax.ShapeDtypeStruct(w.shape, F32)] * 4,
        compiler_params=_cparams(("parallel", "parallel")),
    )(idx, w, m, v, own, other)


WEIGHTS = ['norm_w', 'w_in', 's5_a_re', 's5_a_im', 's5_b_re', 's5_b_im', 's5_c_re', 's5_c_im', 's5_d', 's5_log_step',
           's5_w_glu', 's5_b_glu', 'ssd_conv_w', 'ssd_conv_b', 'ssd_dt_bias', 'ssd_a_log', 'ssd_d', 'ssd_norm_w',
           'w_br_s5', 'w_br_ssd', 'w_out', 'ple_norm_w', 'w_ple_gate', 'w_ple_proj', 'final_norm_w']
SHARDED = {'w_in': ((IN_PROJ_DIM, 1024), 0), 's5_w_glu': ((512, 512), 0), 'ssd_conv_w': ((SSD_CONV, SSD_CONV_DIM), 1),
           'w_br_s5': ((512, 1024), 1), 'w_br_ssd': ((1536, 1024), 0), 'w_out': ((1024, 1024), 0),
           'w_ple_gate': ((1024, 1024), 0), 'w_ple_proj': ((256, 1024), 1)}
TRANSPOSED = ('w_in',)
SMALL = [n for n in WEIGHTS if n not in SHARDED]


def _shard_shape(name):
    (r, c), ax = SHARDED[name]
    return (r // N_CHIPS, c) if ax == 0 else (r, c // N_CHIPS)


def _half_axis(name):
    return 0 if (_shard_shape(name)[0] // 2) % 16 == 0 else 1


def _shard2d(name, a):
    r, c = _shard_shape(name)
    return a.reshape(c, r).T if name in TRANSPOSED else a.reshape(r, c)


def _unshard2d(name, a2, shape):
    return (a2.T if name in TRANSPOSED else a2).reshape(shape)


def _unslot(name, a4):
    (r, c), ax = SHARDED[name]
    if ax == 0:
        return a4.reshape(r, c)
    return jnp.transpose(a4, (1, 0, 2)).reshape(r, c)


def _slot(name, full):
    (r, c), ax = SHARDED[name]
    if ax == 0:
        return full.reshape(N_CHIPS, r // N_CHIPS, c)
    return jnp.transpose(full.reshape(r, N_CHIPS, c // N_CHIPS), (1, 0, 2))


GHP = ('s5_b_re', 's5_b_im')


def _view_shape(name):
    if name in ('s5_a_re', 's5_a_im'):
        return (S5_GROUPS, S5_STATE)
    if name in GHP + ('s5_c_re', 's5_c_im'):
        return (S5_GROUPS, S5_GROUP, S5_STATE)
    if name == 'ssd_conv_w':
        return (SSD_CONV, SSD_CONV_DIM // N_CHIPS)
    return (1, {'s5_log_step': S5_GROUPS, 'ssd_conv_b': SSD_CONV_DIM, 'ssd_norm_w': SSD_WIDTH, 's5_d': S5_WIDTH,
                's5_b_glu': S5_WIDTH, 'ssd_dt_bias': SSD_HEADS, 'ssd_a_log': SSD_HEADS, 'ssd_d': SSD_HEADS}.get(name, D_MODEL))


def _view(name, a):
    if name in GHP:
        return jnp.swapaxes(a.reshape(S5_GROUPS, S5_STATE, S5_GROUP), 1, 2)
    return a.reshape(_view_shape(name))


def _unview(name, a, shape):
    return (jnp.swapaxes(a, 1, 2) if name in GHP else a).reshape(shape)


def _adamw_small(ws, ms, vs, gs):
    n = len(ws)
    c1 = 1.0 - ADAM_B1 ** ADAM_STEP
    c2 = 1.0 - ADAM_B2 ** ADAM_STEP

    def body(*refs):
        w_r, m_r, v_r, g_r = (refs[k * n:(k + 1) * n] for k in range(4))
        d_o, m_o, v_o = (refs[k * n:(k + 1) * n] for k in range(4, 7))
        for i in range(n):
            g = g_r[i][...]
            m_new = ADAM_B1 * m_r[i][...] + (1.0 - ADAM_B1) * g
            v_new = ADAM_B2 * v_r[i][...] + (1.0 - ADAM_B2) * (g * g)
            m_o[i][...] = m_new
            v_o[i][...] = v_new
            d_o[i][...] = -ADAM_LR * ((m_new / c1) / (jnp.sqrt(v_new / c2) + ADAM_EPS) + ADAM_WD * w_r[i][...])

    return pl.pallas_call(
        body, name="adamw_small", out_shape=[jax.ShapeDtypeStruct(w.shape, F32) for w in ws] * 3,
        compiler_params=pltpu.CompilerParams(vmem_limit_bytes=VMEM_LIMIT),
    )(*ws, *ms, *vs, *gs)


def _pack_small(vals):
    flat = jnp.concatenate([v.reshape(-1).astype(F32) for v in vals])
    rows = -(-flat.shape[0] // (256 * 128)) * 256
    return jnp.pad(flat, (0, rows * 128 - flat.shape[0])).reshape(rows, 128)


def _unpack_small(packed, shapes):
    flat = packed.reshape(-1)
    out, off = [], 0
    for sh in shapes:
        n = math.prod(sh)
        out.append(flat[off:off + n].reshape(sh))
        off += n
    return out


def kernel(x, p, norm_w, w_in, s5_a_re, s5_a_im, s5_b_re, s5_b_im, s5_c_re, s5_c_im, s5_d, s5_log_step, s5_w_glu, s5_b_glu, ssd_conv_w, ssd_conv_b, ssd_dt_bias, ssd_a_log, ssd_d, ssd_norm_w, w_br_s5, w_br_ssd, w_out, ple_norm_w, w_ple_gate, w_ple_proj, final_norm_w, loss_target, m_norm_w, m_w_in, m_s5_a_re, m_s5_a_im, m_s5_b_re, m_s5_b_im, m_s5_c_re, m_s5_c_im, m_s5_d, m_s5_log_step, m_s5_w_glu, m_s5_b_glu, m_ssd_conv_w, m_ssd_conv_b, m_ssd_dt_bias, m_ssd_a_log, m_ssd_d, m_ssd_norm_w, m_w_br_s5, m_w_br_ssd, m_w_out, m_ple_norm_w, m_w_ple_gate, m_w_ple_proj, m_final_norm_w, v_norm_w, v_w_in, v_s5_a_re, v_s5_a_im, v_s5_b_re, v_s5_b_im, v_s5_c_re, v_s5_c_im, v_s5_d, v_s5_log_step, v_s5_w_glu, v_s5_b_glu, v_ssd_conv_w, v_ssd_conv_b, v_ssd_dt_bias, v_ssd_a_log, v_ssd_d, v_ssd_norm_w, v_w_br_s5, v_w_br_ssd, v_w_out, v_ple_norm_w, v_w_ple_gate, v_w_ple_proj, v_final_norm_w):
    args = locals()
    wl = {n: args[n] for n in WEIGHTS}
    ml = {n: args["m_" + n] for n in WEIGHTS}
    vl = {n: args["v_" + n] for n in WEIGHTS}
    big = [n for n in SHARDED if n != 'ssd_conv_w']
    chip = _chip_index(lax.axis_index("x"), lax.axis_index("y"))
    idx = jnp.stack([lax.axis_index("c"), chip]).astype(jnp.int32)

    axes = [_half_axis(n) for n in big]
    first = ['w_in']
    rest = [n for n in big if n not in first]
    bf_shards = {n: _shard2d(n, wl[n]).astype(BF16) for n in big}
    w_in_t, conv_w4 = _gather_chips([bf_shards[n] for n in first], [_half_axis(n) for n in first],
                                    [_shard2d('ssd_conv_w', wl['ssd_conv_w'])])
    full = {n: wl[n] for n in SMALL}
    full["w_in_t"] = w_in_t
    full['ssd_conv_w'] = _unslot('ssd_conv_w', conv_w4)
    send_sems, recv_sems, thru, lands, token = _gather_start([bf_shards[n] for n in rest], w_in_t)

    def fetch_rest(after):
        own, zones = _gather_wait(send_sems, recv_sems, thru, lands, after)
        return {n: _unslot(n, lax.dynamic_update_slice(z, o[None], (chip, 0, 0)))
                for n, o, z in zip(rest, own, zones, strict=True)}

    full["_late"] = (token, fetch_rest)

    in_flight = []

    def send_rest(g_now):
        in_flight.extend(_scatter_start("scatter_rest_start", [_slot(n, g_now[n]) for n in rest], [], g_now['s5_w_glu']))
        return in_flight[4]

    full["_early"] = send_rest

    packed = [n for n in SMALL if n != 'norm_w']
    first_axes = [_half_axis(n) for n in first]
    st = {}

    def send_first(g_now, loss_now):
        small_pack = _pack_small([loss_now] + [g_now[n + "_ghp" if n in GHP else n] for n in packed]
                                 + [g_now['ssd_conv_w']])
        swapped = _swap_halves([g_now["w_in_t"]], first_axes, small_pack)
        st["pair"] = [_pair_sum("pair_sum_w_in", g_now["w_in_t"], swapped[0], idx, first_axes[0])]
        small_chip = _sum_slots("sum_small_pair", swapped[-1])
        half_rows = small_chip.shape[0] // 2
        my_half = lax.dynamic_slice(small_chip, (lax.axis_index("c") * half_rows, 0), (half_rows, small_chip.shape[1]))
        st["flight"] = _scatter_start("scatter_first_start", [pb for pb, _ in st["pair"]], [my_half], small_chip)
        return st["flight"][4]

    full["_w_in_ready"] = send_first
    loss, grad_x, g = _local_step(x, p[0], loss_target, full)
    pair = st["pair"]
    s_sems, r_sems, thru, lands, tok = st["flight"]
    small_shapes = [(1, 1)] + [_view_shape(n) for n in packed] + [(SSD_CONV, SSD_CONV_DIM)]
    norm_all = _all_to_all("norm_w_all_to_all", g['norm_w'].reshape(8, 128))
    norm_g = _sum_slots("sum_norm_w", norm_all).reshape(1, D_MODEL)

    out_g, out_d, out_m, out_v = {}, {}, {}, {}
    own_slots, zones = _scatter_wait("scatter_rest_wait", len(rest), *in_flight[:4], norm_g)
    chip_sums = [_sum_chips("sum_chips_" + n, a, z, idx, tok) for n, a, z in zip(rest, own_slots, zones, strict=True)]
    sib_sums = _swap_sibling("swap_sibling_rest", chip_sums)
    for n, own, sib in zip(rest, chip_sums, sib_sums, strict=True):
        res = _adamw("adamw_" + n, _shard2d(n, wl[n]), _shard2d(n, ml[n]), _shard2d(n, vl[n]), [own, sib])
        out_g[n], out_d[n], out_m[n], out_v[n] = (_unshard2d(n, r, wl[n].shape) for r in res)

    sent, got = _scatter_wait("scatter_first_wait", len(first), s_sems, r_sems, thru, lands, out_d[rest[-1]])
    small_zone = lax.dynamic_update_slice(got[-1], sent[-1][None], (chip, 0, 0))
    small_half = _sum_slots("sum_small_chips", small_zone)
    halves = [_sum_parts("sum_chips_" + n, own, r) for n, (_, own), r in zip(first, pair, got[:-1], strict=True)]
    swapped2 = _swap_sibling("swap_sibling_first", halves + [small_half])
    other_halves = swapped2[:-1]
    c_is_0 = lax.axis_index("c") == 0
    small_sum = jnp.concatenate([jnp.where(c_is_0, small_half, swapped2[-1]),
                                 jnp.where(c_is_0, swapped2[-1], small_half)], axis=0)
    for n, own, oth, ax in zip(first, halves, other_halves, first_axes, strict=True):
        res = _adamw_halves("adamw_" + n, _shard2d(n, wl[n]), _shard2d(n, ml[n]), _shard2d(n, vl[n]), own, oth, idx, ax)
        out_g[n], out_d[n], out_m[n], out_v[n] = (_unshard2d(n, r, wl[n].shape) for r in res)
    sm = _unpack_small(small_sum, small_shapes)
    loss_total = sm[0].reshape(())
    conv_g = lax.dynamic_slice(sm[-1], (0, chip * (SSD_CONV_DIM // N_CHIPS)), (SSD_CONV, SSD_CONV_DIM // N_CHIPS))
    names = SMALL + ['ssd_conv_w']
    by_name = {**dict(zip(packed, sm[1:-1], strict=True)), 'norm_w': norm_g, 'ssd_conv_w': conv_g}
    grads = [by_name[n] for n in names]
    res = _adamw_small([_view(n, wl[n]) for n in names], [_view(n, ml[n]) for n in names],
                       [_view(n, vl[n]) for n in names], grads)
    for i, n in enumerate(names):
        out_g[n] = _unview(n, grads[i], wl[n].shape)
        out_d[n], out_m[n], out_v[n] = (_unview(n, res[k * len(names) + i], wl[n].shape) for k in range(3))

    return (loss_total, grad_x, *[out_g[n] for n in WEIGHTS], *[out_d[n] for n in WEIGHTS],
            *[out_m[n] for n in WEIGHTS], *[out_v[n] for n in WEIGHTS])
```

```python
import functools
import math

import jax
import jax.numpy as jnp
from jax import lax
from jax.experimental import pallas as pl
from jax.experimental.pallas import tpu as pltpu

F32 = jnp.float32
BF16 = jnp.bfloat16
MESH = pl.DeviceIdType.MESH

D_MODEL = 1024
PLE_DIM = 256
RMS_EPS = 1e-6
S5_WIDTH = 512
S5_GROUP = 16
S5_GROUPS = 32
S5_STATE = 64
S5_N = S5_GROUPS * S5_STATE
S5_LB = 512
S5_NJ = S5_N // S5_LB
S5_TB = 256
S5_LOG_TB = 8
SSD_WIDTH = 1536
SSD_HEADDIM = 64
SSD_HEADS = 24
SSD_GROUPS = 4
SSD_HPG = 6
SSD_STATE = 128
SSD_CONV = 4
SSD_CHUNK = 128
SSD_BC = 512
SSD_CONV_DIM = 2560
GROUP_W = SSD_WIDTH // SSD_GROUPS
N_CHIPS = 4
N_DEV = 8

OFF_XBC, OFF_U5, OFF_Z5, OFF_DT, OFF_G5, OFF_GS, OFF_ZS = 0, 2560, 3072, 3584, 4096, 5120, 6144
DT_W = 512
PROJ_W = 7680
IN_PROJ_DIM = 7192

ADAM_LR, ADAM_B1, ADAM_B2, ADAM_EPS, ADAM_WD, ADAM_STEP = 0.001, 0.9, 0.999, 1e-08, 0.01, 10

VMEM_LIMIT = 56 * 1024 * 1024


ROW_BLOCK_BYTES = 8 * 1024 * 1024


def _row_tile(r, bytes_per_row):
    for t in (r, 4096, 2048, 1024, 512, 256, 128, 64, 32, 16, 8):
        if t <= r and r % t == 0 and t * bytes_per_row <= ROW_BLOCK_BYTES:
            return t
    return r


def _cparams(sem):
    return pltpu.CompilerParams(dimension_semantics=sem, vmem_limit_bytes=VMEM_LIMIT)


def _dg(a, b, ca, cb):
    return lax.dot_general(a.astype(BF16), b.astype(BF16), (((ca,), (cb,)), ((), ())), preferred_element_type=F32)


@jax.custom_vjp
def dot_nn(a, b):
    return _dg(a, b, 1, 0)


@jax.custom_vjp
def dot_nt(a, b):
    return _dg(a, b, 1, 1)


@jax.custom_vjp
def dot_tn(a, b):
    return _dg(a, b, 0, 0)


dot_nn.defvjp(lambda a, b: (_dg(a, b, 1, 0), (a, b)), lambda r, g: (_dg(g, r[1], 1, 1), _dg(r[0], g, 0, 0)))
dot_nt.defvjp(lambda a, b: (_dg(a, b, 1, 1), (a, b)), lambda r, g: (_dg(g, r[1], 1, 0), _dg(g, r[0], 0, 0)))
dot_tn.defvjp(lambda a, b: (_dg(a, b, 0, 0), (a, b)), lambda r, g: (_dg(r[1], g, 1, 1), _dg(r[0], g, 1, 0)))


MM_VMEM_BUDGET = 40 * 1024 * 1024


def _mm_tiles(m, n, k, sa, sb, so, tn_only=None):
    best, best_key = None, None
    for tm in (1024, 512, 256, 128, 64, 32, 16, 8):
        if m % tm:
            continue
        for tn in (2048, 1536, 1280, 1024, 768, 640, 512, 384, 256, 128):
            if n % tn or (tn_only is not None and tn not in tn_only):
                continue
            for tk in (k, 2048, 1536, 1280, 1024, 768, 512, 256, 128):
                if k % tk or tk > max(k, 128):
                    continue
                need = 2 * (tm * tk * sa + tk * tn * sb + tm * tn * so) + (tm * tn * 4 if tk < k else 0)
                if need > MM_VMEM_BUDGET:
                    continue
                key = (tm * tn * tk, tk)
                if best_key is None or key > best_key:
                    best, best_key = (tm, tn, tk), key
    assert best is not None, (m, n, k)
    return best


def _matmul(name, a, b, *, ta=False, tb=False, a_win=None, out_dtype=F32, epilogue=None, epi_rows=(), epi_pars=(),
            epi_outs=(), full_rows=False, epi_accs=(), epi_into=None, dep=None):
    a_off, a_w = a_win if a_win is not None else (0, a.shape[1])
    if ta:
        kdim, m = a.shape[0], a_w
    else:
        m, kdim = a.shape[0], a_w
    n = b.shape[0] if tb else b.shape[1]
    assert (b.shape[1] if tb else b.shape[0]) == kdim, (name, a.shape, b.shape)
    out_dtypes = list(epi_outs) if epilogue is not None else [out_dtype]
    so = sum(jnp.dtype(d).itemsize for d in out_dtypes) + sum(r.dtype.itemsize for r, _ in epi_rows)
    tn_ok = [n] if full_rows else [t for t in (1024, 512, 256, 128) if all(off % t == 0 for _, off in epi_rows)]
    tm, tn, tk = _mm_tiles(m, n, kdim, a.dtype.itemsize, b.dtype.itemsize, so, tn_ok if epilogue is not None else None)
    nk = kdim // tk
    n_er, n_ep, n_out = len(epi_rows), len(epi_pars), len(out_dtypes)
    if ta:
        assert a_off % tm == 0
        a_spec = pl.BlockSpec((tk, tm), lambda i, j, k: (k, i + a_off // tm))
    else:
        assert a_off % tk == 0
        a_spec = pl.BlockSpec((tm, tk), lambda i, j, k: (i, k + a_off // tk))
    if tb:
        b_spec = pl.BlockSpec((tn, tk), lambda i, j, k: (j, k))
    else:
        b_spec = pl.BlockSpec((tk, tn), lambda i, j, k: (k, j))
    ca, cb = (0 if ta else 1), (1 if tb else 0)

    n_acc = len(epi_accs)
    n_x = (1 if epi_into is not None else 0) + (1 if dep is not None else 0)
    assert not (n_acc or epi_into is not None) or full_rows

    def body(a_ref, b_ref, *refs):
        er, ep = refs[:n_er], refs[n_er:n_er + n_ep]
        first_out = n_er + n_ep + n_x
        o_refs = refs[first_out:first_out + n_out]
        s_refs = refs[first_out + n_out:first_out + n_out + n_acc]
        acc = refs[first_out + n_out + n_acc:]

        def finish(c):
            outs = [c] if epilogue is None else epilogue(c, [r[...] for r in er], [p[...] for p in ep])
            if n_acc:
                outs, sums = outs

                @pl.when(pl.program_id(0) == 0)
                def _():
                    for s_ref in s_refs:
                        s_ref[...] = jnp.zeros_like(s_ref)

                for s_ref, s in zip(s_refs, sums, strict=True):
                    s_ref[...] += jnp.broadcast_to(s, s_ref.shape)
            for o_ref, o in zip(o_refs, outs, strict=True):
                o_ref[...] = o.astype(o_ref.dtype)

        if nk == 1:
            finish(_dg(a_ref[...], b_ref[...], ca, cb))
            return
        (acc_ref,) = acc
        k = pl.program_id(2)

        @pl.when(k == 0)
        def _():
            acc_ref[...] = jnp.zeros_like(acc_ref)

        acc_ref[...] += _dg(a_ref[...], b_ref[...], ca, cb)

        @pl.when(k == nk - 1)
        def _():
            finish(acc_ref[...])

    in_specs = [a_spec, b_spec]
    in_specs += [pl.BlockSpec((tm, tn), functools.partial(lambda i, j, k, c: (i, j + c), c=off // tn)) for _, off in epi_rows]
    in_specs += [pl.BlockSpec((p.shape[0], tn), lambda i, j, k: (0, j)) for p in epi_pars]
    out_specs = [pl.BlockSpec((tm, tn), lambda i, j, k: (i, j)) for _ in out_dtypes]
    out_shape = [jax.ShapeDtypeStruct((m, n), d) for d in out_dtypes]
    extra, aliases = [], {}
    if epi_into is not None:
        buf, off, width = epi_into
        assert off % width == 0 and buf.dtype == out_dtypes[0]
        in_specs.append(pl.BlockSpec(memory_space=pl.ANY))
        out_specs[0] = pl.BlockSpec((tm, width), functools.partial(lambda i, j, k, c: (i, c), c=off // width))
        out_shape[0] = jax.ShapeDtypeStruct(buf.shape, buf.dtype)
        extra, aliases = [buf], {2 + n_er + n_ep: 0}
    if dep is not None:
        in_specs.append(pl.BlockSpec(memory_space=pl.ANY))
        extra = extra + [dep]
    out_specs += [pl.BlockSpec((r, w), lambda i, j, k: (0, 0)) for r, w in epi_accs]
    out_shape += [jax.ShapeDtypeStruct((r, w), F32) for r, w in epi_accs]
    res = pl.pallas_call(
        body, name=name, grid=(m // tm, n // tn, nk),
        in_specs=in_specs, out_specs=out_specs, out_shape=out_shape, input_output_aliases=aliases,
        scratch_shapes=[pltpu.VMEM((tm, tn), F32)] if nk > 1 else [],
        compiler_params=_cparams(("arbitrary",) * 3 if n_acc else ("parallel", "parallel", "arbitrary")),
    )(a, b, *[r for r, _ in epi_rows], *epi_pars, *extra)
    return res if epilogue is not None else res[0]


BAND = 128


def _rowwise(name, fn, n_rows, tr, row_ins, par_ins, row_outs, acc_outs=(), into=None):
    nr, npar, no, na = len(row_ins), len(par_ins), len(row_outs), len(acc_outs)
    in_specs = []
    for arr, off, w in row_ins:
        assert off % w == 0 and arr.shape[0] == n_rows, (name, arr.shape, off, w)
        in_specs.append(pl.BlockSpec((tr, w), functools.partial(lambda i, c: (i, c), c=off // w)))
    for arr, off, w in par_ins:
        assert off % w == 0
        in_specs.append(pl.BlockSpec((arr.shape[0], w), functools.partial(lambda i, c: (0, c), c=off // w)))
    out_specs = [pl.BlockSpec((tr, w), lambda i: (i, 0)) for w, _ in row_outs]
    out_specs += [pl.BlockSpec((r, w), lambda i: (0, 0)) for r, w in acc_outs]
    out_shape = [jax.ShapeDtypeStruct((n_rows, w), dt) for w, dt in row_outs]
    out_shape += [jax.ShapeDtypeStruct((r, w), F32) for r, w in acc_outs]
    extra, aliases = [], {}
    if into is not None:
        buf, off = into
        w0 = row_outs[0][0]
        assert off % w0 == 0 and buf.dtype == row_outs[0][1]
        in_specs.append(pl.BlockSpec(memory_space=pl.ANY))
        out_specs[0] = pl.BlockSpec((tr, w0), functools.partial(lambda i, c: (i, c), c=off // w0))
        out_shape[0] = jax.ShapeDtypeStruct(buf.shape, buf.dtype)
        extra, aliases = [buf], {nr + npar: 0}
    nx = len(extra)

    def body(*refs):
        rows = [r[...] for r in refs[:nr]]
        pars = [r[...] for r in refs[nr:nr + npar]]
        o_refs = refs[nr + npar + nx:nr + npar + nx + no]
        a_refs = refs[nr + npar + nx + no:]
        outs, accs = fn(rows, pars)
        for o_ref, o in zip(o_refs, outs, strict=True):
            o_ref[...] = o.astype(o_ref.dtype)
        if na:
            @pl.when(pl.program_id(0) == 0)
            def _():
                for a_ref in a_refs:
                    a_ref[...] = jnp.zeros_like(a_ref)

            for a_ref, a in zip(a_refs, accs, strict=True):
                a_ref[...] += jnp.broadcast_to(a, a_ref.shape)

    res = pl.pallas_call(
        body, name=name, grid=(n_rows // tr,),
        in_specs=in_specs, out_specs=out_specs, out_shape=out_shape, input_output_aliases=aliases,
        compiler_params=_cparams(("arbitrary",) if na else ("parallel",)),
    )(*[a for a, _, _ in row_ins], *[a for a, _, _ in par_ins], *extra)
    return res


def _rms(x, w):
    return x * lax.rsqrt(jnp.mean(x * x, axis=-1, keepdims=True) + RMS_EPS) * w


def _s5_out(yc, u, z, t, d, bg):
    ge = jax.nn.gelu(yc + d * u)
    return ge * jax.nn.sigmoid(t + bg) * jax.nn.silu(z)


def _merge(g5, gs, m5, ms):
    return jax.nn.sigmoid(g5) * m5 + jax.nn.sigmoid(gs) * ms


def _head_loss(h1, pgl, pp, fw, tgt):
    h2 = h1 + jax.nn.sigmoid(pgl) * pp
    err = _rms(h2, fw) - tgt
    per_row = 0.5 * jnp.mean(err * err, axis=-1, keepdims=True)
    return jnp.sum(per_row, axis=0, keepdims=True)


def _s5_disc(a_re, a_im, log_step, b_re2, b_im2, expand):
    step = jnp.exp(log_step)
    mag = jnp.exp(a_re * step)
    lb_re = mag * jnp.cos(a_im * step)
    lb_im = mag * jnp.sin(a_im * step)
    den = a_re * a_re + a_im * a_im
    n_re = lb_re - 1.0
    f_re = (n_re * a_re + lb_im * a_im) / den
    f_im = (lb_im * a_re - n_re * a_im) / den
    hi = lax.Precision.HIGHEST
    fr = jnp.dot(expand, f_re, precision=hi, preferred_element_type=F32)
    fi = jnp.dot(expand, f_im, precision=hi, preferred_element_type=F32)
    return lb_re, lb_im, fr * b_re2 - fi * b_im2, fr * b_im2 + fi * b_re2


def _s5_params_fwd(a_re, a_im, log_step, b_re2, b_im2, expand):
    gp = a_re.shape

    def body(ar, ai, ls, br, bi, ex, pr_ref, pi_ref, bbr_ref, bbi_ref):
        lr, li, bbr, bbi = _s5_disc(ar[...], ai[...], ls[...], br[...], bi[...], ex[...])
        bbr_ref[...] = bbr
        bbi_ref[...] = bbi
        qr, qi = lr, li
        for k in range(S5_LOG_TB):
            pr_ref[k] = qr
            pi_ref[k] = qi
            qr, qi = qr * lr - qi * li, qr * li + qi * lr

    return pl.pallas_call(
        body, name="s5_params_fwd",
        out_shape=(jax.ShapeDtypeStruct((S5_LOG_TB,) + gp, F32), jax.ShapeDtypeStruct((S5_LOG_TB,) + gp, F32),
                   jax.ShapeDtypeStruct(b_re2.shape, F32), jax.ShapeDtypeStruct(b_re2.shape, F32)),
    )(a_re, a_im, log_step, b_re2, b_im2, expand)


def _s5_params_bwd(a_re, a_im, log_step, b_re2, b_im2, expand, d_lr, d_li, d_bbr, d_bbi):
    def body(ar, ai, ls, br, bi, ex, glr, gli, gbr, gbi, dar, dai, dls, dbr, dbi):
        _, vjp = jax.vjp(lambda *p: _s5_disc(*p, ex[...]), ar[...], ai[...], ls[...], br[...], bi[...])
        g = vjp((glr[...], gli[...], gbr[...], gbi[...]))
        for ref, val in zip((dar, dai, dls, dbr, dbi), g, strict=True):
            ref[...] = val

    return pl.pallas_call(
        body, name="s5_params_bwd",
        out_shape=tuple(jax.ShapeDtypeStruct(v.shape, F32) for v in (a_re, a_im, log_step, b_re2, b_im2)),
    )(a_re, a_im, log_step, b_re2, b_im2, expand, d_lr, d_li, d_bbr, d_bbi)


def _scan_block(x_ref, out_ref, lp, edge_pow, cr, ci, reverse, each=None):
    n_g = x_ref.shape[0] // 8
    sub = lax.broadcasted_iota(jnp.int32, (8, S5_LB), 0)
    steps = []
    for sh in (1, 2, 4):
        keep = (sub < 8 - sh) if reverse else (sub >= sh)
        steps.append((8 - sh if reverse else sh, jnp.where(keep, lp[sh - 1:sh, :S5_LB], 0.0),
                      jnp.where(keep, lp[sh - 1:sh, S5_LB:], 0.0)))
    e_r, e_i = edge_pow[:, :S5_LB], edge_pow[:, S5_LB:]
    for r in (range(n_g - 1, -1, -1) if reverse else range(n_g)):
        rows = slice(8 * r, 8 * r + 8)
        xr, xi = x_ref[rows, :S5_LB], x_ref[rows, S5_LB:]
        for by, a_r, a_i in steps:
            pr, pi = pltpu.roll(xr, by, 0), pltpu.roll(xi, by, 0)
            xr, xi = xr + a_r * pr - a_i * pi, xi + a_r * pi + a_i * pr
        xr = xr + e_r * cr - e_i * ci
        xi = xi + e_r * ci + e_i * cr
        out_ref[rows, :S5_LB] = xr
        out_ref[rows, S5_LB:] = xi
        if each is not None:
            each(r, xr, xi)
        cr, ci = (xr[0:1, :], xi[0:1, :]) if reverse else (xr[7:8, :], xi[7:8, :])
    return cr, ci


def _s5_fwd(proj, bb_band, c_band, lam_pow, s5_d, n_seq, seq_len):
    n_t = seq_len // S5_TB
    blk = 2 * S5_LB
    n_rows = n_seq * seq_len
    u_blk0 = OFF_U5 // BAND

    def body(u_ref, bb_ref, cb_ref, lp_ref, d_ref, s_ref, y_ref, ge_ref, cr, ci, buf):
        @pl.when(pl.program_id(2) == 0)
        def _():
            cr[...] = jnp.zeros_like(cr)
            ci[...] = jnp.zeros_like(ci)

        u = u_ref[...]
        buf[...] = _dg(u, bb_ref[...], 1, 0)
        lp = lp_ref[...]
        cr[...], ci[...] = _scan_block(buf, buf, lp, lp, cr[...], ci[...], False)
        s = buf[...].astype(s_ref.dtype)
        s_ref[...] = s
        y = _dg(s, cb_ref[...], 1, 1)
        y_ref[...] = y
        ge_ref[...] = jax.nn.gelu(y + d_ref[...] * u).astype(ge_ref.dtype)

    def rows(j, b, t):
        return (b * n_t + t, j)

    return pl.pallas_call(
        body, name="s5_fwd", grid=(S5_NJ, n_seq, n_t),
        in_specs=[pl.BlockSpec((S5_TB, BAND), lambda j, b, t: (b * n_t + t, u_blk0 + j)),
                  pl.BlockSpec((BAND, blk), lambda j, b, t: (j, 0)), pl.BlockSpec((BAND, blk), lambda j, b, t: (j, 0)),
                  pl.BlockSpec((S5_LOG_TB, blk), lambda j, b, t: (0, j)), pl.BlockSpec((1, BAND), lambda j, b, t: (0, j))],
        out_specs=[pl.BlockSpec((S5_TB, blk), rows), pl.BlockSpec((S5_TB, BAND), rows), pl.BlockSpec((S5_TB, BAND), rows)],
        out_shape=[jax.ShapeDtypeStruct((n_rows, S5_NJ * blk), BF16), jax.ShapeDtypeStruct((n_rows, S5_WIDTH), F32),
                   jax.ShapeDtypeStruct((n_rows, S5_WIDTH), BF16)],
        scratch_shapes=[pltpu.VMEM((1, S5_LB), F32), pltpu.VMEM((1, S5_LB), F32), pltpu.VMEM((S5_TB, blk), F32)],
        compiler_params=_cparams(("parallel", "parallel", "arbitrary")),
    )(proj, bb_band, c_band, lam_pow, s5_d)


def _s5_bwd(dyc, s, proj, du5_a, bb_band, c_band, lam_pow_conj, dproj, n_seq, seq_len):
    n_t = seq_len // S5_TB
    blk = 2 * S5_LB
    halo_rows = 16
    halo_per_blk = S5_TB // halo_rows
    u_blk0 = OFF_U5 // BAND

    def rows(j, b, t):
        return (b * n_t + (n_t - 1 - t), j)

    def u_rows(j, b, t):
        return (b * n_t + (n_t - 1 - t), u_blk0 + j)

    def halo(j, b, t):
        return (jnp.maximum((b * n_t + (n_t - 1 - t)) * halo_per_blk - 1, 0), j)

    def body(dy_ref, sq_ref, hq_ref, u_ref, dua_ref, bb_ref, cb_ref, lp_ref, _, du_ref, dl_ref, dbb_ref, dcb_ref,
             cr, ci, buf, s_ref):
        b, t = pl.program_id(1), pl.program_id(2)
        sq = sq_ref[...]
        s_ref[...] = sq.astype(F32)
        h_last = hq_ref[...].astype(F32)[halo_rows - 1:halo_rows, :]
        dy = dy_ref[...]
        buf[...] = _dg(dy, cb_ref[...], 1, 0)

        @pl.when(t == 0)
        def _():
            cr[...] = jnp.zeros_like(cr)
            ci[...] = jnp.zeros_like(ci)

        @pl.when((b == 0) & (t == 0))
        def _():
            dl_ref[...] = jnp.zeros_like(dl_ref)
            dbb_ref[...] = jnp.zeros_like(dbb_ref)
            dcb_ref[...] = jnp.zeros_like(dcb_ref)

        dcb_ref[...] += _dg(dy, sq, 0, 0)

        first_blk = t == n_t - 1
        sub = lax.broadcasted_iota(jnp.int32, (8, S5_LB), 0)
        acc = [jnp.zeros((8, S5_LB), F32), jnp.zeros((8, S5_LB), F32)]

        def each(r, gr, gi):
            rows = slice(8 * r, 8 * r + 8)
            if r == 0:
                before_r = jnp.where(first_blk, 0.0, h_last[:, :S5_LB])
                before_i = jnp.where(first_blk, 0.0, h_last[:, S5_LB:])
            else:
                before_r, before_i = s_ref[8 * r - 1:8 * r, :S5_LB], s_ref[8 * r - 1:8 * r, S5_LB:]
            sp_r = jnp.where(sub == 0, before_r, pltpu.roll(s_ref[rows, :S5_LB], 1, 0))
            sp_i = jnp.where(sub == 0, before_i, pltpu.roll(s_ref[rows, S5_LB:], 1, 0))
            acc[0] = acc[0] + gr * sp_r + gi * sp_i
            acc[1] = acc[1] + gi * sp_r - gr * sp_i

        lp = lp_ref[...]
        edge_pow = jnp.concatenate([lp[7 - i:8 - i, :] for i in range(8)], axis=0)
        cr[...], ci[...] = _scan_block(buf, buf, lp, edge_pow, cr[...], ci[...], True, each)
        g = buf[...].astype(BF16)
        du_ref[...] = (dua_ref[...] + _dg(g, bb_ref[...], 1, 1)).astype(du_ref.dtype)
        dbb_ref[...] += _dg(u_ref[...], g, 0, 0)
        dl_ref[:, :S5_LB] += jnp.sum(acc[0], axis=0, keepdims=True)
        dl_ref[:, S5_LB:] += jnp.sum(acc[1], axis=0, keepdims=True)

    band = pl.BlockSpec((BAND, blk), lambda j, b, t: (j, 0))
    return pl.pallas_call(
        body, name="s5_bwd", grid=(S5_NJ, n_seq, n_t),
        in_specs=[pl.BlockSpec((S5_TB, BAND), rows), pl.BlockSpec((S5_TB, blk), rows),
                  pl.BlockSpec((halo_rows, blk), halo), pl.BlockSpec((S5_TB, BAND), u_rows),
                  pl.BlockSpec((S5_TB, BAND), rows), band, band,
                  pl.BlockSpec((S5_LOG_TB, blk), lambda j, b, t: (0, j)), pl.BlockSpec(memory_space=pl.ANY)],
        out_specs=[pl.BlockSpec((S5_TB, BAND), u_rows), pl.BlockSpec((1, blk), lambda j, b, t: (0, j)), band, band],
        out_shape=[jax.ShapeDtypeStruct(dproj.shape, dproj.dtype), jax.ShapeDtypeStruct((1, S5_NJ * blk), F32),
                   jax.ShapeDtypeStruct(bb_band.shape, F32), jax.ShapeDtypeStruct(c_band.shape, F32)],
        input_output_aliases={8: 0},
        scratch_shapes=[pltpu.VMEM((1, S5_LB), F32), pltpu.VMEM((1, S5_LB), F32), pltpu.VMEM((S5_TB, blk), F32),
                        pltpu.VMEM((S5_TB, blk), F32)],
        compiler_params=_cparams(("parallel", "arbitrary", "arbitrary")),
    )(dyc, s, s, proj, du5_a, bb_band, c_band, lam_pow_conj, dproj)


CONV_TR = 512
CONV_CW = 512


def _shift_down(x, halo, k):
    if k == 0:
        return x
    row8 = lax.broadcasted_iota(jnp.int32, halo.shape, 0)
    rolled = pltpu.roll(x, k, 0)
    top = jnp.where(row8 < k, pltpu.roll(halo, k, 0), rolled[:8])
    if x.shape[0] == 8:
        return top
    return jnp.concatenate([top, rolled[8:]], axis=0)


def _shift_up(x, halo, k):
    if k == 0:
        return x
    n = x.shape[0]
    row8 = lax.broadcasted_iota(jnp.int32, halo.shape, 0)
    rolled = pltpu.roll(x, n - k, 0)
    bot = jnp.where(row8 >= 8 - k, pltpu.roll(halo, 8 - k, 0), rolled[n - 8:])
    if n == 8:
        return bot
    return jnp.concatenate([rolled[:n - 8], bot], axis=0)


def _conv_pre(x, halo, w, b):
    acc = b + w[SSD_CONV - 1:SSD_CONV, :] * x
    for k in range(SSD_CONV - 1):
        acc = acc + w[k:k + 1, :] * _shift_down(x, halo, SSD_CONV - 1 - k)
    return acc


def _conv_specs(seq_len, col_off):
    lt = seq_len // CONV_TR
    cb = col_off // CONV_CW
    cur = pl.BlockSpec((CONV_TR, CONV_CW), lambda j, i: (i, j + cb))
    prev = pl.BlockSpec((8, CONV_CW), lambda j, i: (jnp.maximum(i * (CONV_TR // 8) - 1, 0), j + cb))
    return lt, cur, prev


def _conv_fwd(proj, conv_w, conv_b, n_rows, seq_len):
    lt, cur, prev = _conv_specs(seq_len, OFF_XBC)

    def body(x_ref, h_ref, w_ref, b_ref, o_ref):
        halo = jnp.where(pl.program_id(1) % lt == 0, 0.0, h_ref[...])
        o_ref[...] = jax.nn.silu(_conv_pre(x_ref[...], halo, w_ref[...], b_ref[...]))

    return pl.pallas_call(
        body, name="ssd_conv_fwd", grid=(SSD_CONV_DIM // CONV_CW, n_rows // CONV_TR),
        in_specs=[cur, prev, pl.BlockSpec((SSD_CONV, CONV_CW), lambda j, i: (0, j)),
                  pl.BlockSpec((1, CONV_CW), lambda j, i: (0, j))],
        out_specs=pl.BlockSpec((CONV_TR, CONV_CW), lambda j, i: (i, j)),
        out_shape=jax.ShapeDtypeStruct((n_rows, SSD_CONV_DIM), F32),
        compiler_params=_cparams(("parallel", "parallel")),
    )(proj, proj, conv_w, conv_b)


def _conv_bwd(name, proj, d_act, conv_w, conv_b, n_rows, seq_len, col_off, dproj):
    width = d_act.shape[1]
    lt, cur, prev = _conv_specs(seq_len, OFF_XBC + col_off)
    n_blk = n_rows // CONV_TR
    cb = (OFF_XBC + col_off) // CONV_CW
    pb = col_off // CONV_CW
    nxt = pl.BlockSpec((8, CONV_CW), lambda j, i: (jnp.minimum((i + 1) * (CONV_TR // 8), n_rows // 8 - 1), j + cb))
    d_cur = pl.BlockSpec((CONV_TR, CONV_CW), lambda j, i: (i, j))
    d_nxt = pl.BlockSpec((8, CONV_CW), lambda j, i: (jnp.minimum((i + 1) * (CONV_TR // 8), n_rows // 8 - 1), j))

    def dsilu(pre):
        sg = jax.nn.sigmoid(pre)
        return sg * (1.0 + pre * (1.0 - sg))

    def body(x_ref, hp_ref, hn_ref, d_ref, dn_ref, w_ref, b_ref, _, dx_ref, dw_ref, db_ref):
        i = pl.program_id(1)
        x, w, b = x_ref[...], w_ref[...], b_ref[...]
        halo_p = jnp.where(i % lt == 0, 0.0, hp_ref[...])
        at_end = i % lt == lt - 1
        dpre = d_ref[...] * dsilu(_conv_pre(x, halo_p, w, b))
        pre_n = _conv_pre(hn_ref[...], x[CONV_TR - 8:, :], w, b)
        dpre_n = jnp.where(at_end, 0.0, dn_ref[...] * dsilu(pre_n))
        dx = w[SSD_CONV - 1:SSD_CONV, :] * dpre
        for k in range(SSD_CONV - 1):
            dx = dx + w[k:k + 1, :] * _shift_up(dpre, dpre_n, SSD_CONV - 1 - k)
        dx_ref[...] = dx.astype(dx_ref.dtype)

        @pl.when(i == 0)
        def _():
            dw_ref[...] = jnp.zeros_like(dw_ref)
            db_ref[...] = jnp.zeros_like(db_ref)

        for k in range(SSD_CONV):
            xs = _shift_down(x, halo_p, SSD_CONV - 1 - k)
            dw_ref[k:k + 1, :] += jnp.sum(dpre * xs, axis=0, keepdims=True)
        db_ref[...] += jnp.sum(dpre, axis=0, keepdims=True)

    return pl.pallas_call(
        body, name=name, grid=(width // CONV_CW, n_blk),
        in_specs=[cur, prev, nxt, d_cur, d_nxt,
                  pl.BlockSpec((SSD_CONV, CONV_CW), lambda j, i: (0, j + pb)),
                  pl.BlockSpec((1, CONV_CW), lambda j, i: (0, j + pb)), pl.BlockSpec(memory_space=pl.ANY)],
        out_specs=[pl.BlockSpec((CONV_TR, CONV_CW), lambda j, i: (i, j + cb)),
                   pl.BlockSpec((SSD_CONV, CONV_CW), lambda j, i: (0, j)),
                   pl.BlockSpec((1, CONV_CW), lambda j, i: (0, j))],
        out_shape=[jax.ShapeDtypeStruct(dproj.shape, dproj.dtype),
                   jax.ShapeDtypeStruct((SSD_CONV, width), F32), jax.ShapeDtypeStruct((1, width), F32)],
        input_output_aliases={7: 0},
        compiler_params=_cparams(("parallel", "arbitrary")),
    )(proj, proj, proj, d_act, d_act, conv_w, conv_b, dproj)


def _split3(x):
    hi = x.astype(BF16)
    r = x - hi.astype(F32)
    mid = r.astype(BF16)
    return hi, mid, (r - mid.astype(F32)).astype(BF16)


def _sel_dot(a, b, a_is_sel):
    dn = (((1,), (0,)), ((), ()))
    if a_is_sel:
        return sum(lax.dot_general(a, t, dn, preferred_element_type=F32) for t in _split3(b))
    return sum(lax.dot_general(t, b, dn, preferred_element_type=F32) for t in _split3(a))


@jax.custom_vjp
def sel_left(sel, sel_t, x):
    return _sel_dot(sel, x, True)


@jax.custom_vjp
def sel_right(x, sel, sel_t):
    return _sel_dot(x, sel, False)


sel_left.defvjp(lambda s, st, x: (_sel_dot(s, x, True), (s, st)),
                lambda r, g: (jnp.zeros_like(r[0]), jnp.zeros_like(r[1]), _sel_dot(r[1], g, True)))
sel_right.defvjp(lambda x, s, st: (_sel_dot(x, s, False), (s, st)),
                 lambda r, g: (_sel_dot(g, r[1], False), jnp.zeros_like(r[0]), jnp.zeros_like(r[1])))


def _ssd_chunk(xs, bm, cm, dtr, st, dtb, alog, dsk, k):
    dt = jax.nn.softplus(dtr + dtb)
    acum = sel_left(k["tri"], k["tri_t"], dt * (-jnp.exp(alog)))
    dt_e = sel_right(dt, k["spread"], k["spread_t"])
    ac_e = sel_right(acum, k["spread"], k["spread_t"])
    al_e = ac_e[SSD_CHUNK - 1:SSD_CHUNK, :]
    dsk_e = sel_right(jnp.broadcast_to(dsk, (8, 128)), k["spread"], k["spread_t"])[0:1, :]
    xdt = xs * dt_e
    acum_t = acum.T
    scores = dot_nt(cm, bm)
    y = dot_nn(cm, st) * jnp.exp(ac_e) + xs * dsk_e
    for j in range(SSD_HPG):
        lmat = jnp.exp(jnp.where(k["causal"], acum[:, j:j + 1] - acum_t[j:j + 1, :], -jnp.inf))
        y = y + dot_nn(scores * lmat, jnp.where(k["head"] == j, xdt, 0.0))
    new = st * jnp.exp(al_e) + dot_tn(bm, xdt * jnp.exp(al_e - ac_e))
    return y, new


def _ssd_consts():
    r = lax.broadcasted_iota(jnp.int32, (SSD_CHUNK, SSD_CHUNK), 0)
    c = lax.broadcasted_iota(jnp.int32, (SSD_CHUNK, SSD_CHUNK), 1)
    hd = jnp.int32(SSD_HEADDIM)
    sr = lax.broadcasted_iota(jnp.int32, (128, GROUP_W), 0)
    sc = lax.div(lax.broadcasted_iota(jnp.int32, (128, GROUP_W), 1), hd)
    tr = lax.div(lax.broadcasted_iota(jnp.int32, (GROUP_W, 128), 0), hd)
    tc = lax.broadcasted_iota(jnp.int32, (GROUP_W, 128), 1)
    return {"tri": (r >= c).astype(BF16), "tri_t": (c >= r).astype(BF16), "causal": r >= c,
            "spread": (sr == sc).astype(BF16), "spread_t": (tr == tc).astype(BF16),
            "head": lax.div(lax.broadcasted_iota(jnp.int32, (SSD_CHUNK, GROUP_W), 1), hd)}


def _ssd_specs(n_c, reverse):
    def cidx(c):
        return n_c - 1 - c if reverse else c

    xs = pl.BlockSpec((SSD_CHUNK, GROUP_W), lambda g, b, c: (b * n_c + cidx(c), g))
    bm = pl.BlockSpec((SSD_CHUNK, SSD_STATE), lambda g, b, c: (b * n_c + cidx(c), SSD_WIDTH // SSD_STATE + g))
    cm = pl.BlockSpec((SSD_CHUNK, SSD_STATE), lambda g, b, c: (b * n_c + cidx(c), (SSD_WIDTH + SSD_BC) // SSD_STATE + g))
    dt = pl.BlockSpec((SSD_CHUNK, 128), lambda g, b, c: (b * n_c + cidx(c), OFF_DT // 128 + g))
    par = pl.BlockSpec((1, 128), lambda g, b, c: (0, g))
    st = pl.BlockSpec((1, 1, 1, SSD_STATE, GROUP_W), lambda g, b, c: (b, cidx(c), g, 0, 0))
    return xs, bm, cm, dt, par, st


def _ssd_chunk_gated(xs, bm, cm, dtr, st, dtb, alog, dsk, z, nw, k):
    y, new = _ssd_chunk(xs, bm, cm, dtr, st, dtb, alog, dsk, k)
    yg = y * jax.nn.silu(z)
    return yg * lax.rsqrt(jnp.mean(yg * yg, axis=-1, keepdims=True) + RMS_EPS) * nw, new


def _ssd_gate_specs(n_c, reverse):
    def cidx(c):
        return n_c - 1 - c if reverse else c

    z = pl.BlockSpec((SSD_CHUNK, GROUP_W), lambda g, b, c: (b * n_c + cidx(c), OFF_ZS // GROUP_W + g))
    nw = pl.BlockSpec((1, GROUP_W), lambda g, b, c: (0, g))
    return z, nw


def _ssd_fwd(xbc_act, proj, dtb, alog, dsk, norm_w, n_seq, seq_len):
    n_c = seq_len // SSD_CHUNK
    xs_s, bm_s, cm_s, dt_s, par_s, st_s = _ssd_specs(n_c, False)
    z_s, nw_s = _ssd_gate_specs(n_c, False)

    def body(xs_ref, bm_ref, cm_ref, dt_ref, dtb_ref, al_ref, dk_ref, z_ref, nw_ref, y_ref, st_ref, state):
        @pl.when(pl.program_id(2) == 0)
        def _():
            state[...] = jnp.zeros_like(state)

        prev = state[...]
        st_ref[0, 0, 0] = prev
        y, new = _ssd_chunk_gated(xs_ref[...], bm_ref[...], cm_ref[...], dt_ref[...], prev,
                                  dtb_ref[...], al_ref[...], dk_ref[...], z_ref[...], nw_ref[...], _ssd_consts())
        y_ref[...] = y.astype(y_ref.dtype)
        state[...] = new

    return pl.pallas_call(
        body, name="ssd_fwd", grid=(SSD_GROUPS, n_seq, n_c),
        in_specs=[xs_s, bm_s, cm_s, dt_s, par_s, par_s, par_s, z_s, nw_s],
        out_specs=[pl.BlockSpec((SSD_CHUNK, GROUP_W), lambda g, b, c: (b * n_c + c, g)), st_s],
        out_shape=[jax.ShapeDtypeStruct((n_seq * seq_len, SSD_WIDTH), BF16),
                   jax.ShapeDtypeStruct((n_seq, n_c, SSD_GROUPS, SSD_STATE, GROUP_W), F32)],
        scratch_shapes=[pltpu.VMEM((SSD_STATE, GROUP_W), F32)],
        compiler_params=_cparams(("parallel", "parallel", "arbitrary")),
    )(xbc_act, xbc_act, xbc_act, proj, dtb, alog, dsk, proj, norm_w)


def _ssd_bwd(xbc_act, proj, states, dy, dtb, alog, dsk, norm_w, n_seq, seq_len, dproj):
    n_c = seq_len // SSD_CHUNK
    n_rows = n_seq * seq_len
    xs_s, bm_s, cm_s, dt_s, par_s, st_s = _ssd_specs(n_c, True)
    z_s, nw_s = _ssd_gate_specs(n_c, True)

    def rows(w, first=0):
        return pl.BlockSpec((SSD_CHUNK, w), lambda g, b, c: (b * n_c + (n_c - 1 - c), first + g))

    def body(xs_ref, bm_ref, cm_ref, dt_ref, st_ref, dy_ref, dtb_ref, al_ref, dk_ref, z_ref, nw_ref, _,
             dxs_ref, dbm_ref, dcm_ref, dz_ref, ddt_ref, ddtb_ref, dal_ref, ddk_ref, dnw_ref, dstate):
        b, c = pl.program_id(1), pl.program_id(2)

        @pl.when(c == 0)
        def _():
            dstate[...] = jnp.zeros_like(dstate)

        @pl.when((b == 0) & (c == 0))
        def _():
            for ref in (ddtb_ref, dal_ref, ddk_ref, dnw_ref):
                ref[...] = jnp.zeros_like(ref)

        consts = _ssd_consts()
        _, vjp = jax.vjp(
            lambda *a: _ssd_chunk_gated(*a, consts),
            xs_ref[...], bm_ref[...], cm_ref[...], dt_ref[...], st_ref[0, 0, 0], dtb_ref[...], al_ref[...], dk_ref[...],
            z_ref[...], nw_ref[...])
        dxs, dbm, dcm, ddtr, dprev, ddtb, dal, ddk, dz, dnw = vjp((dy_ref[...], dstate[...]))
        dxs_ref[...] = dxs
        dbm_ref[...] = dbm
        dcm_ref[...] = dcm
        dz_ref[...] = dz.astype(dz_ref.dtype)
        ddt_ref[...] = ddtr.astype(ddt_ref.dtype)
        ddtb_ref[...] += ddtb
        dal_ref[...] += dal
        ddk_ref[...] += ddk
        dnw_ref[...] += dnw
        dstate[...] = dprev

    acc = pl.BlockSpec((1, 128), lambda g, b, c: (0, g))
    return pl.pallas_call(
        body, name="ssd_bwd", grid=(SSD_GROUPS, n_seq, n_c),
        in_specs=[xs_s, bm_s, cm_s, dt_s, st_s, rows(GROUP_W), par_s, par_s, par_s, z_s, nw_s,
                  pl.BlockSpec(memory_space=pl.ANY)],
        out_specs=[rows(GROUP_W), rows(SSD_STATE), rows(SSD_STATE), rows(GROUP_W, OFF_ZS // GROUP_W), rows(128),
                   acc, acc, acc, pl.BlockSpec((1, GROUP_W), lambda g, b, c: (0, g))],
        out_shape=[jax.ShapeDtypeStruct((n_rows, SSD_WIDTH), F32), jax.ShapeDtypeStruct((n_rows, SSD_BC), F32),
                   jax.ShapeDtypeStruct((n_rows, SSD_BC), F32), jax.ShapeDtypeStruct(dproj.shape, dproj.dtype),
                   jax.ShapeDtypeStruct((n_rows, DT_W), BF16),
                   jax.ShapeDtypeStruct((1, 512), F32), jax.ShapeDtypeStruct((1, 512), F32),
                   jax.ShapeDtypeStruct((1, 512), F32), jax.ShapeDtypeStruct((1, SSD_WIDTH), F32)],
        input_output_aliases={11: 3},
        scratch_shapes=[pltpu.VMEM((SSD_STATE, GROUP_W), F32)],
        compiler_params=_cparams(("parallel", "arbitrary", "arbitrary")),
    )(xbc_act, xbc_act, xbc_act, proj, states, dy, dtb, alog, dsk, proj, norm_w, dproj)


def _pad_heads(v):
    return jnp.pad(v.reshape(SSD_GROUPS, SSD_HPG), ((0, 0), (0, 128 - SSD_HPG))).reshape(1, SSD_GROUPS * 128)


def _unpad_heads(v):
    return v.reshape(SSD_GROUPS, 128)[:, :SSD_HPG].reshape(1, SSD_HEADS)


def _state_cols(v):
    re, im = v
    lead = re.shape[:-1]
    re = re.reshape(lead + (S5_NJ, 1, S5_LB))
    im = im.reshape(lead + (S5_NJ, 1, S5_LB))
    return jnp.concatenate([re, im], axis=-2).reshape(lead + (2 * S5_N,))


def _state_uncols(v):
    lead = v.shape[:-1]
    v = v.reshape(lead + (S5_NJ, 2, S5_LB))
    return v[..., 0, :].reshape(lead + (S5_N,)), v[..., 1, :].reshape(lead + (S5_N,))


GROUPS_PER_BAND = BAND // S5_GROUP


def _band(w2_re, w2_im):
    gh = S5_GROUPS * S5_GROUP
    rg = ((jnp.arange(gh) // S5_GROUP) % GROUPS_PER_BAND)[:, None, None]
    cg = jnp.arange(GROUPS_PER_BAND)[None, :, None]
    parts = [jnp.where(rg == cg, v[:, None, :], 0.0).reshape(gh, S5_LB) for v in (w2_re, w2_im)]
    return jnp.concatenate(parts, axis=1)


def _band_take(wb):
    gh = S5_GROUPS * S5_GROUP
    w4 = wb.reshape(gh, 2, GROUPS_PER_BAND, S5_STATE)
    sel = w4[jnp.arange(gh), :, (jnp.arange(gh) // S5_GROUP) % GROUPS_PER_BAND, :]
    return sel[:, 0, :], sel[:, 1, :]


W_IN_SHARD = IN_PROJ_DIM // N_CHIPS
W_IN_SEGS = ((0, 512, OFF_U5), (512, 1024, OFF_Z5), (1024, 2560, OFF_ZS), (2560, 5120, OFF_XBC), (5144, 7192, OFF_G5))
DT_ROWS = (5120, 5144)


def _w_in_pieces():
    runs = []
    segs = list(W_IN_SEGS) + [(DT_ROWS[0] + SSD_HPG * g, DT_ROWS[0] + SSD_HPG * (g + 1), OFF_DT + 128 * g)
                              for g in range(SSD_GROUPS)]
    for lo, hi, off in segs:
        for j in range(N_CHIPS):
            s, e = max(lo, j * W_IN_SHARD), min(hi, (j + 1) * W_IN_SHARD)
            if s < e:
                runs.append((j, s - j * W_IN_SHARD, off + s - lo, e - s))
    return runs


RELAYOUT_LANES = 256


def _pad_w_in_t(a4):
    runs = _w_in_pieces()

    def body(a_ref, o_ref):
        o_ref[pl.ds(OFF_DT, DT_W), :] = jnp.zeros((DT_W, RELAYOUT_LANES), o_ref.dtype)
        for j, src, dst, n in runs:
            o_ref[pl.ds(dst, n), :] = a_ref[j, pl.ds(src, n), :]

    return pl.pallas_call(
        body, name="w_in_to_padded", grid=(D_MODEL // RELAYOUT_LANES,),
        in_specs=[pl.BlockSpec((N_CHIPS, W_IN_SHARD, RELAYOUT_LANES), lambda i: (0, 0, i))],
        out_specs=pl.BlockSpec((PROJ_W, RELAYOUT_LANES), lambda i: (0, i)),
        out_shape=jax.ShapeDtypeStruct((PROJ_W, D_MODEL), a4.dtype),
        compiler_params=_cparams(("parallel",)),
    )(a4)


def _unpad_w_in_t(wp):
    runs = _w_in_pieces()

    def body(p_ref, o_ref):
        for j, dst, src, n in runs:
            o_ref[j, pl.ds(dst, n), :] = p_ref[pl.ds(src, n), :]

    return pl.pallas_call(
        body, name="w_in_from_padded", grid=(D_MODEL // RELAYOUT_LANES,),
        in_specs=[pl.BlockSpec((PROJ_W, RELAYOUT_LANES), lambda i: (0, i))],
        out_specs=pl.BlockSpec((N_CHIPS, W_IN_SHARD, RELAYOUT_LANES), lambda i: (0, 0, i)),
        out_shape=jax.ShapeDtypeStruct((N_CHIPS, W_IN_SHARD, D_MODEL), wp.dtype),
        compiler_params=_cparams(("parallel",)),
    )(wp)


def _local_step(x, p, tgt, w):
    n_seq, seq_len, _ = x.shape
    n_rows = n_seq * seq_len
    tr = 512
    x2 = x.reshape(n_rows, D_MODEL)
    p2 = p.reshape(n_rows, PLE_DIM)
    t2 = tgt.reshape(n_rows, D_MODEL)
    row = functools.partial(_rowwise, n_rows=n_rows, tr=tr)

    w_pad_t = _pad_w_in_t(w["w_in_t"])
    norm_w = w["norm_w"].reshape(1, D_MODEL)
    ple_norm_w = w["ple_norm_w"].reshape(1, D_MODEL)
    final_w = w["final_norm_w"].reshape(1, D_MODEL)
    s5_d = w["s5_d"].reshape(1, S5_WIDTH)
    b_glu = w["s5_b_glu"].reshape(1, S5_WIDTH)
    conv_w = w["ssd_conv_w"].reshape(SSD_CONV, SSD_CONV_DIM)
    conv_b = w["ssd_conv_b"].reshape(1, SSD_CONV_DIM)
    ssd_norm_w = w["ssd_norm_w"].reshape(1, SSD_WIDTH)
    dtb, alog, dsk = (_pad_heads(w[k].reshape(1, SSD_HEADS)) for k in ("ssd_dt_bias", "ssd_a_log", "ssd_d"))

    gh = S5_GROUPS * S5_GROUP
    a_re = w["s5_a_re"].reshape(S5_GROUPS, S5_STATE)
    a_im = w["s5_a_im"].reshape(S5_GROUPS, S5_STATE)
    log_step = w["s5_log_step"].reshape(S5_GROUPS, 1)
    b_re2 = jnp.transpose(w["s5_b_re"].reshape(S5_GROUPS, S5_STATE, S5_GROUP), (0, 2, 1)).reshape(gh, S5_STATE)
    b_im2 = jnp.transpose(w["s5_b_im"].reshape(S5_GROUPS, S5_STATE, S5_GROUP), (0, 2, 1)).reshape(gh, S5_STATE)
    expand = (jnp.arange(gh)[:, None] // S5_GROUP == jnp.arange(S5_GROUPS)[None, :]).astype(F32)
    pow_re, pow_im, bb_re2, bb_im2 = _s5_params_fwd(a_re, a_im, log_step, b_re2, b_im2, expand)
    lam_pow = _state_cols((pow_re.reshape(S5_LOG_TB, S5_N), pow_im.reshape(S5_LOG_TB, S5_N)))
    lam_pow_conj = _state_cols((pow_re.reshape(S5_LOG_TB, S5_N), -pow_im.reshape(S5_LOG_TB, S5_N)))
    bb_band = _band(bb_re2, bb_im2).astype(BF16)
    c_band = _band(w["s5_c_re"].reshape(gh, S5_STATE), -w["s5_c_im"].reshape(gh, S5_STATE)).astype(BF16)

    late = w.get("_late")
    (hn,) = row("rms_in", lambda r, q: ([_rms(r[0], q[0])], []), row_ins=[(x2, 0, D_MODEL)],
                par_ins=[(norm_w, 0, D_MODEL)] + ([(late[0], 0, 128)] if late else []), row_outs=[(D_MODEL, BF16)])
    proj = _matmul("mm_proj", hn, w_pad_t, tb=True)
    s, yc, ge = _s5_fwd(proj, bb_band, c_band, lam_pow, s5_d, n_seq, seq_len)
    if late:
        w = {**w, **late[1](ge)}
    tg, y5 = _matmul("mm_s5_glu", ge, w["s5_w_glu"], epilogue=lambda c, r, q: [c, _s5_out(r[0], r[1], r[2], c, *q)],
                     epi_rows=[(yc, 0), (proj, OFF_U5), (proj, OFF_Z5)], epi_pars=[s5_d, b_glu], epi_outs=[F32, BF16])

    xbc_act = _conv_fwd(proj, conv_w, conv_b, n_rows, seq_len)
    yss, states = _ssd_fwd(xbc_act, proj, dtb, alog, dsk, ssd_norm_w, n_seq, seq_len)

    m5 = _matmul("mm_br_s5", y5, w["w_br_s5"])
    ms, merged = _matmul("mm_br_ssd", yss, w["w_br_ssd"], epilogue=lambda c, r, q: [c, _merge(r[0], r[1], r[2], c)],
                         epi_rows=[(proj, OFF_G5), (proj, OFF_GS), (m5, 0)], epi_outs=[F32, BF16])

    def resid_norm(c, r, q):
        h1_ = r[0] + c
        return [h1_, _rms(h1_, q[0])]

    h1, hp = _matmul("mm_out", merged, w["w_out"], epilogue=resid_norm, epi_rows=[(x2, 0)], epi_pars=[ple_norm_w],
                     epi_outs=[F32, BF16], full_rows=True)
    pp = _matmul("mm_ple_proj", p2, w["w_ple_proj"])

    def head_fn(pgl_, r, q):
        h1_, pp_, tgt_ = r
        loss, vjp = jax.vjp(lambda a, b, c, f: _head_loss(a, b, c, f, tgt_), h1_, pgl_, pp_, q[0])
        dh1_, dpgl_, dpp_, dfw_ = vjp(jnp.ones_like(loss))
        return [dh1_, dpgl_, dpp_], [loss, dfw_]

    dh2, dpgl, dpp, loss_acc, d_final_w = _matmul(
        "mm_ple_gate_head", hp, w["w_ple_gate"], epilogue=head_fn, epi_rows=[(h1, 0), (pp, 0), (t2, 0)],
        epi_pars=[final_w], epi_outs=[F32, BF16, BF16], epi_accs=[(1, 128), (1, D_MODEL)], full_rows=True)
    loss = loss_acc[0, 0]

    g = {}
    g["final_norm_w"] = d_final_w
    g["w_ple_gate"] = _matmul("mm_d_w_ple_gate", hp, dpgl, ta=True)
    g["w_ple_proj"] = _matmul("mm_d_w_ple_proj", p2, dpp, ta=True)

    def ple_norm_bwd(dhp_, r, q):
        h1_, dh2_ = r
        _, vjp = jax.vjp(_rms, h1_, q[0])
        dh, dw = vjp(dhp_)
        dh = dh + dh2_
        return [dh, dh], [dw]

    dh1, dh1_b, g["ple_norm_w"] = _matmul(
        "mm_d_hp", dpgl, w["w_ple_gate"], tb=True, epilogue=ple_norm_bwd, epi_rows=[(h1, 0), (dh2, 0)],
        epi_pars=[ple_norm_w], epi_outs=[F32, BF16], epi_accs=[(1, D_MODEL)], full_rows=True)
    g["w_out"] = _matmul("mm_d_w_out", merged, dh1_b, ta=True)

    dproj = lax.empty((n_rows, PROJ_W), BF16)

    def merge_bwd(dmerged, r, q):
        sg5, sgs = jax.nn.sigmoid(r[0]), jax.nn.sigmoid(r[1])
        d_gates = jnp.concatenate([dmerged * r[2] * sg5 * (1.0 - sg5), dmerged * r[3] * sgs * (1.0 - sgs)], axis=1)
        return [d_gates, dmerged * sg5, dmerged * sgs]

    dproj, dm5, dms = _matmul(
        "mm_d_merged", dh1_b, w["w_out"], tb=True, epilogue=merge_bwd,
        epi_rows=[(proj, OFF_G5), (proj, OFF_GS), (m5, 0), (ms, 0)], epi_outs=[BF16, BF16, BF16], full_rows=True,
        epi_into=(dproj, OFF_G5, 2 * D_MODEL))
    g["w_br_s5"] = _matmul("mm_d_w_br_s5", y5, dm5, ta=True)
    g["w_br_ssd"] = _matmul("mm_d_w_br_ssd", yss, dms, ta=True)
    dyss = _matmul("mm_d_yss", dms, w["w_br_ssd"], tb=True)

    def s5_out_bwd_a(dy_, r, q):
        yc_, u_, z_, t_ = r
        d_, bg_ = q
        ge_ = jax.nn.gelu(yc_ + d_ * u_)
        _, vjp = jax.vjp(lambda a, z, t, b: a * jax.nn.sigmoid(t + b) * jax.nn.silu(z), ge_, z_, t_, bg_)
        dge, dz, dt_, dbg = vjp(dy_)
        return [dz, dge, dt_], [dbg]

    dproj, dge_a, dtg, g["s5_b_glu"] = _matmul(
        "mm_d_y5", dm5, w["w_br_s5"], tb=True, epilogue=s5_out_bwd_a,
        epi_rows=[(yc, 0), (proj, OFF_U5), (proj, OFF_Z5), (tg, 0)], epi_pars=[s5_d, b_glu],
        epi_outs=[BF16, F32, BF16], epi_accs=[(1, S5_WIDTH)], full_rows=True, epi_into=(dproj, OFF_Z5, S5_WIDTH))
    g["s5_w_glu"] = _matmul("mm_d_w_glu", ge, dtg, ta=True)
    early = w["_early"](g) if "_early" in w else None

    def s5_out_bwd_b(dge_b, r, q):
        yc_, u_, da_ = r
        _, vjp = jax.vjp(lambda yc, u, d: jax.nn.gelu(yc + d * u), yc_, u_, q[0])
        dyc_, du_, dd_ = vjp(da_ + dge_b)
        return [dyc_, du_], [dd_]

    dyc, du5_a, g["s5_d"] = _matmul(
        "mm_d_ge", dtg, w["s5_w_glu"], tb=True, epilogue=s5_out_bwd_b,
        epi_rows=[(yc, 0), (proj, OFF_U5), (dge_a, 0)], epi_pars=[s5_d], epi_outs=[BF16, F32],
        epi_accs=[(1, S5_WIDTH)], full_rows=True, dep=early)
    dproj, d_lam, d_bb_band, d_c_band = _s5_bwd(dyc, s, proj, du5_a, bb_band, c_band, lam_pow_conj, dproj,
                                                n_seq, seq_len)

    d_lr, d_li = _state_uncols(d_lam)
    d_bbr, d_bbi = _band_take(d_bb_band)
    d_are, d_aim, d_ls, d_br2, d_bi2 = _s5_params_bwd(
        a_re, a_im, log_step, b_re2, b_im2, expand,
        d_lr.reshape(S5_GROUPS, S5_STATE), d_li.reshape(S5_GROUPS, S5_STATE), d_bbr, d_bbi)
    g["s5_a_re"], g["s5_a_im"], g["s5_log_step"] = d_are, d_aim, d_ls
    g["s5_b_re_ghp"], g["s5_b_im_ghp"] = d_br2, d_bi2
    d_cr, d_ci = _band_take(d_c_band)
    g["s5_c_re"], g["s5_c_im"] = d_cr, -d_ci

    dxs, dbm, dcm, dproj, ddt, d_dtb, d_alog, d_dsk, g["ssd_norm_w"] = _ssd_bwd(
        xbc_act, proj, states, dyss, dtb, alog, dsk, ssd_norm_w, n_seq, seq_len, dproj)
    (dproj,) = row("ssd_ddt", lambda r, q: ([r[0]], []), row_ins=[(ddt, 0, DT_W)], par_ins=[],
                   row_outs=[(DT_W, BF16)], into=(dproj, OFF_DT))
    g["ssd_dt_bias"], g["ssd_a_log"], g["ssd_d"] = _unpad_heads(d_dtb), _unpad_heads(d_alog), _unpad_heads(d_dsk)
    conv_dw, conv_db = [], []
    for nm, d_act, off in (("x", dxs, 0), ("b", dbm, SSD_WIDTH), ("c", dcm, SSD_WIDTH + SSD_BC)):
        dproj, dw_, db_ = _conv_bwd("ssd_conv_bwd_" + nm, proj, d_act, conv_w, conv_b, n_rows, seq_len, off, dproj)
        conv_dw.append(dw_)
        conv_db.append(db_)
    g["ssd_conv_w"] = jnp.concatenate(conv_dw, axis=1)
    g["ssd_conv_b"] = jnp.concatenate(conv_db, axis=1)

    g["w_in_t"] = _unpad_w_in_t(_matmul("mm_d_w_in", dproj, hn, ta=True))
    def norm_bwd(dhn_, r, q):
        x_, dh1_ = r
        _, vjp = jax.vjp(_rms, x_, q[0])
        dx_, dw_ = vjp(dhn_)
        return [dx_ + dh1_], [dw_]

    sent = w["_w_in_ready"](g, loss) if "_w_in_ready" in w else None
    dx, g["norm_w"] = _matmul("mm_d_hn", dproj, w_pad_t, epilogue=norm_bwd, epi_rows=[(x2, 0), (dh1, 0)],
                              epi_pars=[norm_w], epi_outs=[F32], epi_accs=[(1, D_MODEL)], full_rows=True, dep=sent)
    return loss, dx.reshape(x.shape), g


HBM = pl.BlockSpec(memory_space=pltpu.HBM)


def _chip_index(x, y):
    return 2 * x + y


def _half(shape2d, axis, which):
    h = shape2d[axis] // 2
    sl = pl.ds(pl.multiple_of(which * h, 128 if axis else 8), h)
    return (slice(None), sl) if axis else (sl, slice(None))


def _gather_chips(split, axes, whole):
    ns, nw = len(split), len(whole)
    n = ns + nw

    def body(*refs):
        ins, outs = refs[:n], refs[n:2 * n]
        ici_send, ici_recv, d2d_send, d2d_recv, local_sems = refs[2 * n:]
        x, y, c = lax.axis_index("x"), lax.axis_index("y"), lax.axis_index("c")
        me = _chip_index(x, y)
        sibling = (x, y, 1 - c)
        peers = [(1 - x, y), (x, 1 - y), (1 - x, 1 - y)]

        def half(t, which):
            return _half(split[t].shape, axes[t], which)

        copies = []
        for t in range(n):
            loc = pltpu.make_async_copy(ins[t], outs[t].at[me], local_sems.at[t])
            loc.start()
            copies.append(loc)

        def ici(t, k, slot):
            px, py = peers[k]
            if t < ns:
                src, dst = ins[t].at[half(t, c)], outs[t].at[(slot,) + half(t, c)]
            else:
                src, dst = ins[t], outs[t].at[slot]
            return pltpu.make_async_remote_copy(src_ref=src, dst_ref=dst, send_sem=ici_send.at[t, k],
                                                recv_sem=ici_recv.at[t, k], device_id=(px, py, c), device_id_type=MESH)

        def d2d(t, k, which):
            rows = outs[t].at[(_chip_index(*peers[k]),) + half(t, which)]
            return pltpu.make_async_remote_copy(src_ref=rows, dst_ref=rows, send_sem=d2d_send.at[t, k],
                                                recv_sem=d2d_recv.at[t, k], device_id=sibling, device_id_type=MESH)

        sends = []
        for t in range(n):
            for k in range(3):
                cp = ici(t, k, me)
                cp.start()
                sends.append(cp)
        for t in range(n):
            for k in range(3):
                ici(t, k, _chip_index(*peers[k])).wait_recv()
                if t < ns:
                    cp = d2d(t, k, c)
                    cp.start()
                    sends.append(cp)
        for t in range(ns):
            for k in range(3):
                d2d(t, k, 1 - c).wait_recv()
        for cp in sends:
            cp.wait_send()
        for cp in copies:
            cp.wait()

    arrays = list(split) + list(whole)
    return pl.pallas_call(
        body, name="gather_weights",
        in_specs=[HBM] * n, out_specs=[HBM] * n,
        out_shape=[jax.ShapeDtypeStruct((N_CHIPS,) + a.shape, a.dtype) for a in arrays],
        scratch_shapes=[pltpu.SemaphoreType.DMA((n, 3)), pltpu.SemaphoreType.DMA((n, 3)),
                        pltpu.SemaphoreType.DMA((ns, 3)), pltpu.SemaphoreType.DMA((ns, 3)),
                        pltpu.SemaphoreType.DMA((n,))],
    )(*arrays)


SEM = pl.BlockSpec(memory_space=pltpu.SEMAPHORE)
DATAFLOW = pltpu.SideEffectType.DATAFLOW_SIDE_EFFECTING


def _gather_start(shards, after):
    n = len(shards)

    def body(*refs):
        ins, lands = refs[:n], refs[n:2 * n]
        send_sems, recv_sems = refs[2 * n + 1], refs[2 * n + 2]
        token = refs[-1]
        x, y, c = lax.axis_index("x"), lax.axis_index("y"), lax.axis_index("c")
        me = _chip_index(x, y)
        for t in range(n):
            for k, (px, py) in enumerate([(1 - x, y), (x, 1 - y), (1 - x, 1 - y)]):
                pltpu.make_async_remote_copy(
                    src_ref=ins[t], dst_ref=lands[t].at[me], send_sem=send_sems.at[3 * t + k],
                    recv_sem=recv_sems.at[3 * t + k],
                    device_id=(px, py, c), device_id_type=MESH).start()
        token[...] = jnp.zeros_like(token)

    zones = [lax.empty((N_CHIPS,) + a.shape, a.dtype) for a in shards]
    res = pl.pallas_call(
        body, name="gather_rest_start",
        out_shape=(pltpu.SemaphoreType.DMA((3 * n,)), pltpu.SemaphoreType.DMA((3 * n,)),
                   *[pltpu.HBM(a.shape, a.dtype) for a in shards], *[pltpu.HBM(z.shape, z.dtype) for z in zones],
                   jax.ShapeDtypeStruct((8, 128), F32)),
        in_specs=[HBM] * (2 * n) + [pl.BlockSpec(memory_space=pl.ANY)],
        out_specs=(SEM, SEM, *[HBM] * (2 * n), pl.BlockSpec(memory_space=pltpu.VMEM)),
        input_output_aliases={t: 2 + t for t in range(2 * n)},
        compiler_params=pltpu.CompilerParams(has_side_effects=DATAFLOW),
    )(*[pltpu.with_memory_space_constraint(a, pltpu.HBM) for a in shards],
      *[pltpu.with_memory_space_constraint(z, pltpu.HBM) for z in zones], after)
    return res[0], res[1], list(res[2:2 + n]), list(res[2 + n:2 + 2 * n]), res[-1]


def _gather_wait(send_sems, recv_sems, thru, lands, after):
    n = len(thru)

    def body(*refs):
        ins, zones = refs[:n], refs[n:2 * n]
        s_sems, r_sems = refs[2 * n], refs[2 * n + 1]
        x, y, c = lax.axis_index("x"), lax.axis_index("y"), lax.axis_index("c")
        for t in range(n):
            for k, (px, py) in enumerate([(1 - x, y), (x, 1 - y), (1 - x, 1 - y)]):
                cp = pltpu.make_async_remote_copy(
                    src_ref=ins[t], dst_ref=zones[t].at[_chip_index(px, py)], send_sem=s_sems.at[3 * t + k],
                    recv_sem=r_sems.at[3 * t + k], device_id=(px, py, c), device_id_type=MESH)
                cp.wait_send()
                cp.wait_recv()

    res = pl.pallas_call(
        body, name="gather_rest_wait",
        out_shape=(*[pltpu.HBM(a.shape, a.dtype) for a in thru], *[pltpu.HBM(z.shape, z.dtype) for z in lands]),
        in_specs=[HBM] * (2 * n) + [SEM, SEM, pl.BlockSpec(memory_space=pl.ANY)], out_specs=[HBM] * (2 * n),
        input_output_aliases={t: t for t in range(2 * n)},
        compiler_params=pltpu.CompilerParams(has_side_effects=DATAFLOW),
    )(*thru, *lands, send_sems, recv_sems, after)
    return list(res[:n]), list(res[n:])


def _scatter_ends(t, k, n_slotted, ins, zones, x, y, me):
    px, py = [(1 - x, y), (x, 1 - y), (1 - x, 1 - y)][k]
    if t < n_slotted:
        return ins[t].at[_chip_index(px, py)], zones[t].at[k], (px, py)
    return ins[t], (zones[t].at[me], zones[t].at[_chip_index(px, py)]), (px, py)


def _scatter_start(name, slotted, whole, after):
    n_s = len(slotted)
    arrays = list(slotted) + list(whole)
    n = len(arrays)

    def body(*refs):
        ins, lands = refs[:n], refs[n:2 * n]
        send_sems, recv_sems = refs[2 * n + 1], refs[2 * n + 2]
        token = refs[-1]
        x, y, c = lax.axis_index("x"), lax.axis_index("y"), lax.axis_index("c")
        for t in range(n):
            for k in range(3):
                src, dst, (px, py) = _scatter_ends(t, k, n_s, ins, lands, x, y, _chip_index(x, y))
                pltpu.make_async_remote_copy(
                    src_ref=src, dst_ref=dst if t < n_s else dst[0], send_sem=send_sems.at[3 * t + k],
                    recv_sem=recv_sems.at[3 * t + k], device_id=(px, py, c), device_id_type=MESH).start()
        token[...] = jnp.zeros_like(token)

    zones = [lax.empty((3,) + a.shape[1:], a.dtype) for a in slotted]
    zones += [lax.empty((N_CHIPS,) + a.shape, a.dtype) for a in whole]
    slotted = arrays
    res = pl.pallas_call(
        body, name=name,
        out_shape=(pltpu.SemaphoreType.DMA((3 * n,)), pltpu.SemaphoreType.DMA((3 * n,)),
                   *[pltpu.HBM(a.shape, a.dtype) for a in slotted], *[pltpu.HBM(z.shape, z.dtype) for z in zones],
                   jax.ShapeDtypeStruct((8, 128), F32)),
        in_specs=[HBM] * (2 * n) + [pl.BlockSpec(memory_space=pl.ANY)],
        out_specs=(SEM, SEM, *[HBM] * (2 * n), pl.BlockSpec(memory_space=pltpu.VMEM)),
        input_output_aliases={t: 2 + t for t in range(2 * n)},
        compiler_params=pltpu.CompilerParams(has_side_effects=DATAFLOW),
    )(*[pltpu.with_memory_space_constraint(a, pltpu.HBM) for a in slotted],
      *[pltpu.with_memory_space_constraint(z, pltpu.HBM) for z in zones], after)
    return res[0], res[1], list(res[2:2 + n]), list(res[2 + n:2 + 2 * n]), res[-1]


def _scatter_wait(name, n_slotted, send_sems, recv_sems, thru, lands, after):
    n = len(thru)

    def body(*refs):
        ins, zones = refs[:n], refs[n:2 * n]
        s_sems, r_sems = refs[2 * n], refs[2 * n + 1]
        x, y, c = lax.axis_index("x"), lax.axis_index("y"), lax.axis_index("c")
        for t in range(n):
            for k in range(3):
                src, dst, (px, py) = _scatter_ends(t, k, n_slotted, ins, zones, x, y, _chip_index(x, y))
                cp = pltpu.make_async_remote_copy(
                    src_ref=src, dst_ref=dst if t < n_slotted else dst[1], send_sem=s_sems.at[3 * t + k],
                    recv_sem=r_sems.at[3 * t + k], device_id=(px, py, c), device_id_type=MESH)
                cp.wait_send()
                cp.wait_recv()

    res = pl.pallas_call(
        body, name=name,
        out_shape=(*[pltpu.HBM(a.shape, a.dtype) for a in thru], *[pltpu.HBM(z.shape, z.dtype) for z in lands]),
        in_specs=[HBM] * (2 * n) + [SEM, SEM, pl.BlockSpec(memory_space=pl.ANY)], out_specs=[HBM] * (2 * n),
        input_output_aliases={t: t for t in range(2 * n)},
        compiler_params=pltpu.CompilerParams(has_side_effects=DATAFLOW),
    )(*thru, *lands, send_sems, recv_sems, after)
    return list(res[:n]), list(res[n:])


def _sum_chips(name, slotted, recv, idx, dep):
    _, r, c = slotted.shape
    tr = _row_tile(r, 5 * c * 4)

    def body(idx_ref, own_ref, r_ref, _, o_ref):
        acc = own_ref[0]
        for k in range(3):
            acc = acc + r_ref[k]
        o_ref[...] = acc

    grid_spec = pltpu.PrefetchScalarGridSpec(
        num_scalar_prefetch=1, grid=(r // tr,),
        in_specs=[pl.BlockSpec((1, tr, c), lambda i, s: (s[1], i, 0)), pl.BlockSpec((3, tr, c), lambda i, s: (0, i, 0)),
                  pl.BlockSpec(memory_space=pl.ANY)],
        out_specs=pl.BlockSpec((tr, c), lambda i, s: (i, 0)))
    return pl.pallas_call(
        body, name=name, grid_spec=grid_spec, out_shape=jax.ShapeDtypeStruct((r, c), F32),
        compiler_params=_cparams(("parallel",)),
    )(idx, slotted, recv, dep)


def _half_shape(shape2d, axis):
    r, c = shape2d
    return (r, c // 2) if axis else (r // 2, c)


def _swap_halves(slotted, axes, small):
    n = len(slotted)

    def body(*refs):
        ins, sm_in = refs[:n], refs[n]
        outs, sm_out = refs[n + 1:2 * n + 1], refs[2 * n + 1]
        send_sems, recv_sems, local_sem = refs[2 * n + 2:]
        x, y, c = lax.axis_index("x"), lax.axis_index("y"), lax.axis_index("c")
        local = pltpu.make_async_copy(sm_in, sm_out.at[c], local_sem)
        local.start()
        sends = []
        for t in range(n):
            other = (slice(None),) + _half(slotted[t].shape[1:], axes[t], 1 - c)
            cp = pltpu.make_async_remote_copy(
                src_ref=ins[t].at[other], dst_ref=outs[t], send_sem=send_sems.at[t],
                recv_sem=recv_sems.at[t], device_id=(x, y, 1 - c), device_id_type=MESH)
            cp.start()
            sends.append(cp)
        cp = pltpu.make_async_remote_copy(
            src_ref=sm_in, dst_ref=sm_out.at[c], send_sem=send_sems.at[n], recv_sem=recv_sems.at[n],
            device_id=(x, y, 1 - c), device_id_type=MESH)
        cp.start()
        sends.append(cp)
        for cp in sends[:n]:
            cp.wait_recv()
        pltpu.make_async_remote_copy(
            src_ref=sm_in, dst_ref=sm_out.at[1 - c], send_sem=send_sems.at[n], recv_sem=recv_sems.at[n],
            device_id=(x, y, 1 - c), device_id_type=MESH).wait_recv()
        for cp in sends:
            cp.wait_send()
        local.wait()

    return pl.pallas_call(
        body, name="swap_halves",
        in_specs=[HBM] * (n + 1), out_specs=[HBM] * (n + 1),
        out_shape=[jax.ShapeDtypeStruct((a.shape[0],) + _half_shape(a.shape[1:], ax), a.dtype)
                   for a, ax in zip(slotted, axes, strict=True)]
        + [jax.ShapeDtypeStruct((2,) + small.shape, small.dtype)],
        scratch_shapes=[pltpu.SemaphoreType.DMA((n + 1,)), pltpu.SemaphoreType.DMA((n + 1,)), pltpu.SemaphoreType.DMA],
    )(*slotted, small)


def _tiling(r, c, f32_per_elem):
    if r % 8 == 0:
        tr = _row_tile(r, f32_per_elem * c * 4)
        return r // tr, (tr, c), lambda i: (i, 0)
    assert c % 128 == 0, (r, c)
    return c // 128, (r, 128), lambda i: (0, i)


def _pair_sum(name, slotted, other, idx, axis):
    _, r, c = slotted.shape
    hr, hc = _half_shape((r, c), axis)
    n, (tr, tc), at = _tiling(hr, hc, 13)
    if axis == 0:
        a = slotted.reshape(N_CHIPS, 2, hr, c)
        a_all = pl.BlockSpec((N_CHIPS, 1, tr, tc), lambda i, s: (0, s[0]) + at(i))
        a_own = pl.BlockSpec((1, 1, tr, tc), lambda i, s: (s[1], s[0]) + at(i))
    else:
        a, per_half = slotted, hc // tc
        a_all = pl.BlockSpec((N_CHIPS, tr, tc), lambda i, s: (0, at(i)[0], s[0] * per_half + at(i)[1]))
        a_own = pl.BlockSpec((1, tr, tc), lambda i, s: (s[1], at(i)[0], s[0] * per_half + at(i)[1]))

    def body(idx_ref, a_ref, b_ref, am_ref, bm_ref, p_ref, own_ref):
        mine, mine_own = (a_ref[:, 0], am_ref[0, 0]) if axis == 0 else (a_ref[...], am_ref[0])
        p_ref[...] = (mine + b_ref[...]).astype(p_ref.dtype)
        own_ref[...] = mine_own + bm_ref[0]

    grid_spec = pltpu.PrefetchScalarGridSpec(
        num_scalar_prefetch=1, grid=(n,),
        in_specs=[a_all, pl.BlockSpec((N_CHIPS, tr, tc), lambda i, s: (0,) + at(i)),
                  a_own, pl.BlockSpec((1, tr, tc), lambda i, s: (s[1],) + at(i))],
        out_specs=[pl.BlockSpec((N_CHIPS, tr, tc), lambda i, s: (0,) + at(i)),
                   pl.BlockSpec((tr, tc), lambda i, s: at(i))])
    return pl.pallas_call(
        body, name=name, grid_spec=grid_spec,
        out_shape=[jax.ShapeDtypeStruct((N_CHIPS, hr, hc), BF16), jax.ShapeDtypeStruct((hr, hc), F32)],
        compiler_params=_cparams(("parallel",)),
    )(idx, a, other, a, other)


def _sum_parts(name, own, recv):
    h, c = own.shape
    n, (tr, tc), at = _tiling(h, c, 4)

    def body(o_ref, r_ref, out_ref):
        acc = o_ref[...]
        for k in range(3):
            acc = acc + r_ref[k].astype(F32)
        out_ref[...] = acc

    return pl.pallas_call(
        body, name=name, grid=(n,),
        in_specs=[pl.BlockSpec((tr, tc), at), pl.BlockSpec((3, tr, tc), lambda i: (0,) + at(i))],
        out_specs=pl.BlockSpec((tr, tc), at),
        out_shape=jax.ShapeDtypeStruct((h, c), F32),
        compiler_params=_cparams(("parallel",)),
    )(own, recv)


def _all_to_all(name, small):
    def body(sm_in, sm_out, send_sems, recv_sems, local_sem):
        x, y, c = lax.axis_index("x"), lax.axis_index("y"), lax.axis_index("c")
        dev = 4 * x + 2 * y + c
        local = pltpu.make_async_copy(sm_in, sm_out.at[dev], local_sem)
        local.start()
        rel = [(fx, fy, fc) for fx in (0, 1) for fy in (0, 1) for fc in (0, 1)][1:]
        sends = []
        for k, (fx, fy, fc) in enumerate(rel):
            cp = pltpu.make_async_remote_copy(
                src_ref=sm_in, dst_ref=sm_out.at[dev], send_sem=send_sems.at[k], recv_sem=recv_sems.at[k],
                device_id=(x ^ fx, y ^ fy, c ^ fc), device_id_type=MESH)
            cp.start()
            sends.append(cp)
        for k, (fx, fy, fc) in enumerate(rel):
            src_dev = 4 * (x ^ fx) + 2 * (y ^ fy) + (c ^ fc)
            pltpu.make_async_remote_copy(
                src_ref=sm_in, dst_ref=sm_out.at[src_dev], send_sem=send_sems.at[k], recv_sem=recv_sems.at[k],
                device_id=(x ^ fx, y ^ fy, c ^ fc), device_id_type=MESH).wait_recv()
        for cp in sends:
            cp.wait_send()
        local.wait()

    return pl.pallas_call(
        body, name=name, in_specs=[HBM], out_specs=HBM,
        out_shape=jax.ShapeDtypeStruct((N_DEV,) + small.shape, small.dtype),
        scratch_shapes=[pltpu.SemaphoreType.DMA((7,)), pltpu.SemaphoreType.DMA((7,)), pltpu.SemaphoreType.DMA],
    )(small)


def _swap_sibling(name, parts):
    n = len(parts)

    def body(*refs):
        ins, outs = refs[:n], refs[n:2 * n]
        send_sems, recv_sems = refs[2 * n:]
        x, y, c = lax.axis_index("x"), lax.axis_index("y"), lax.axis_index("c")
        cps = []
        for t in range(n):
            cp = pltpu.make_async_remote_copy(
                src_ref=ins[t], dst_ref=outs[t], send_sem=send_sems.at[t], recv_sem=recv_sems.at[t],
                device_id=(x, y, 1 - c), device_id_type=MESH)
            cp.start()
            cps.append(cp)
        for cp in cps:
            cp.wait_recv()
        for cp in cps:
            cp.wait_send()

    return pl.pallas_call(
        body, name=name,
        in_specs=[HBM] * n, out_specs=[HBM] * n,
        out_shape=[jax.ShapeDtypeStruct(a.shape, a.dtype) for a in parts],
        scratch_shapes=[pltpu.SemaphoreType.DMA((n,)), pltpu.SemaphoreType.DMA((n,))],
    )(*parts)


def _sum_slots(name, a):
    k, r, c = a.shape
    tr = _row_tile(r, (k + 1) * c * 4)

    def body(a_ref, o_ref):
        acc = a_ref[0]
        for i in range(1, k):
            acc = acc + a_ref[i]
        o_ref[...] = acc

    return pl.pallas_call(
        body, name=name, grid=(r // tr,),
        in_specs=[pl.BlockSpec((k, tr, c), lambda i: (0, i, 0))],
        out_specs=pl.BlockSpec((tr, c), lambda i: (i, 0)),
        out_shape=jax.ShapeDtypeStruct((r, c), a.dtype),
        compiler_params=_cparams(("parallel",)),
    )(a)


def _adamw(name, w, m, v, g_parts):
    r, c = w.shape
    ng = len(g_parts)
    tr = _row_tile(r, (7 + ng) * c * 4)
    c1 = 1.0 - ADAM_B1 ** ADAM_STEP
    c2 = 1.0 - ADAM_B2 ** ADAM_STEP

    def body(*refs):
        w_ref, m_ref, v_ref = refs[:3]
        g_refs = refs[3:3 + ng]
        go_ref, d_ref, mo_ref, vo_ref = refs[3 + ng:]
        g = g_refs[0][...]
        for gr in g_refs[1:]:
            g = g + gr[...]
        m_new = ADAM_B1 * m_ref[...] + (1.0 - ADAM_B1) * g
        v_new = ADAM_B2 * v_ref[...] + (1.0 - ADAM_B2) * (g * g)
        go_ref[...] = g
        mo_ref[...] = m_new
        vo_ref[...] = v_new
        d_ref[...] = -ADAM_LR * ((m_new / c1) / (jnp.sqrt(v_new / c2) + ADAM_EPS) + ADAM_WD * w_ref[...])

    spec = pl.BlockSpec((tr, c), lambda i: (i, 0))
    return pl.pallas_call(
        body, name=name, grid=(r // tr,),
        in_specs=[spec] * (3 + ng), out_specs=[spec] * 4,
        out_shape=[jax.ShapeDtypeStruct((r, c), F32)] * 4,
        compiler_params=_cparams(("parallel",)),
    )(w, m, v, *g_parts)


def _adamw_halves(name, w, m, v, own, other, idx, axis):
    hr, hc = own.shape
    nb, (tr, tc), at = _tiling(hr, hc, 9)
    c1 = 1.0 - ADAM_B1 ** ADAM_STEP
    c2 = 1.0 - ADAM_B2 ** ADAM_STEP

    def body(idx_ref, w_ref, m_ref, v_ref, own_ref, oth_ref, go_ref, d_ref, mo_ref, vo_ref):
        g = jnp.where(pl.program_id(0) == idx_ref[0], own_ref[...], oth_ref[...])
        m_new = ADAM_B1 * m_ref[...] + (1.0 - ADAM_B1) * g
        v_new = ADAM_B2 * v_ref[...] + (1.0 - ADAM_B2) * (g * g)
        go_ref[...] = g
        mo_ref[...] = m_new
        vo_ref[...] = v_new
        d_ref[...] = -ADAM_LR * ((m_new / c1) / (jnp.sqrt(v_new / c2) + ADAM_EPS) + ADAM_WD * w_ref[...])

    per_half = (hc // tc) if axis else (hr // tr)
    if axis:
        full = pl.BlockSpec((tr, tc), lambda hh, i, s: (at(i)[0], hh * per_half + at(i)[1]))
    else:
        full = pl.BlockSpec((tr, tc), lambda hh, i, s: (hh * per_half + at(i)[0], at(i)[1]))
    part = pl.BlockSpec((tr, tc), lambda hh, i, s: at(i))
    grid_spec = pltpu.PrefetchScalarGridSpec(
        num_scalar_prefetch=1, grid=(2, nb), in_specs=[full, full, full, part, part], out_specs=[full] * 4)
    return pl.pallas_call(
        body, name=name, grid_spec=grid_spec, out_shape=[jax.ShapeDtypeStruct(w.shape, F32)] * 4,
        compiler_params=_cparams(("parallel", "parallel")),
    )(idx, w, m, v, own, other)


WEIGHTS = ['norm_w', 'w_in', 's5_a_re', 's5_a_im', 's5_b_re', 's5_b_im', 's5_c_re', 's5_c_im', 's5_d', 's5_log_step',
           's5_w_glu', 's5_b_glu', 'ssd_conv_w', 'ssd_conv_b', 'ssd_dt_bias', 'ssd_a_log', 'ssd_d', 'ssd_norm_w',
           'w_br_s5', 'w_br_ssd', 'w_out', 'ple_norm_w', 'w_ple_gate', 'w_ple_proj', 'final_norm_w']
SHARDED = {'w_in': ((IN_PROJ_DIM, 1024), 0), 's5_w_glu': ((512, 512), 0), 'ssd_conv_w': ((SSD_CONV, SSD_CONV_DIM), 1),
           'w_br_s5': ((512, 1024), 1), 'w_br_ssd': ((1536, 1024), 0), 'w_out': ((1024, 1024), 0),
           'w_ple_gate': ((1024, 1024), 0), 'w_ple_proj': ((256, 1024), 1)}
TRANSPOSED = ('w_in',)
SMALL = [n for n in WEIGHTS if n not in SHARDED]


def _shard_shape(name):
    (r, c), ax = SHARDED[name]
    return (r // N_CHIPS, c) if ax == 0 else (r, c // N_CHIPS)


def _half_axis(name):
    return 0 if (_shard_shape(name)[0] // 2) % 16 == 0 else 1


def _shard2d(name, a):
    r, c = _shard_shape(name)
    return a.reshape(c, r).T if name in TRANSPOSED else a.reshape(r, c)


def _unshard2d(name, a2, shape):
    return (a2.T if name in TRANSPOSED else a2).reshape(shape)


def _unslot(name, a4):
    (r, c), ax = SHARDED[name]
    if ax == 0:
        return a4.reshape(r, c)
    return jnp.transpose(a4, (1, 0, 2)).reshape(r, c)


def _slot(name, full):
    (r, c), ax = SHARDED[name]
    if ax == 0:
        return full.reshape(N_CHIPS, r // N_CHIPS, c)
    return jnp.transpose(full.reshape(r, N_CHIPS, c // N_CHIPS), (1, 0, 2))


GHP = ('s5_b_re', 's5_b_im')


def _view_shape(name):
    if name in ('s5_a_re', 's5_a_im'):
        return (S5_GROUPS, S5_STATE)
    if name in GHP + ('s5_c_re', 's5_c_im'):
        return (S5_GROUPS, S5_GROUP, S5_STATE)
    if name == 'ssd_conv_w':
        return (SSD_CONV, SSD_CONV_DIM // N_CHIPS)
    return (1, {'s5_log_step': S5_GROUPS, 'ssd_conv_b': SSD_CONV_DIM, 'ssd_norm_w': SSD_WIDTH, 's5_d': S5_WIDTH,
                's5_b_glu': S5_WIDTH, 'ssd_dt_bias': SSD_HEADS, 'ssd_a_log': SSD_HEADS, 'ssd_d': SSD_HEADS}.get(name, D_MODEL))


def _view(name, a):
    if name in GHP:
        return jnp.swapaxes(a.reshape(S5_GROUPS, S5_STATE, S5_GROUP), 1, 2)
    return a.reshape(_view_shape(name))


def _unview(name, a, shape):
    return (jnp.swapaxes(a, 1, 2) if name in GHP else a).reshape(shape)


def _adamw_small(ws, ms, vs, gs):
    n = len(ws)
    c1 = 1.0 - ADAM_B1 ** ADAM_STEP
    c2 = 1.0 - ADAM_B2 ** ADAM_STEP

    def body(*refs):
        w_r, m_r, v_r, g_r = (refs[k * n:(k + 1) * n] for k in range(4))
        d_o, m_o, v_o = (refs[k * n:(k + 1) * n] for k in range(4, 7))
        for i in range(n):
            g = g_r[i][...]
            m_new = ADAM_B1 * m_r[i][...] + (1.0 - ADAM_B1) * g
            v_new = ADAM_B2 * v_r[i][...] + (1.0 - ADAM_B2) * (g * g)
            m_o[i][...] = m_new
            v_o[i][...] = v_new
            d_o[i][...] = -ADAM_LR * ((m_new / c1) / (jnp.sqrt(v_new / c2) + ADAM_EPS) + ADAM_WD * w_r[i][...])

    return pl.pallas_call(
        body, name="adamw_small", out_shape=[jax.ShapeDtypeStruct(w.shape, F32) for w in ws] * 3,
        compiler_params=pltpu.CompilerParams(vmem_limit_bytes=VMEM_LIMIT),
    )(*ws, *ms, *vs, *gs)


def _pack_small(vals):
    flat = jnp.concatenate([v.reshape(-1).astype(F32) for v in vals])
    rows = -(-flat.shape[0] // (256 * 128)) * 256
    return jnp.pad(flat, (0, rows * 128 - flat.shape[0])).reshape(rows, 128)


def _unpack_small(packed, shapes):
    flat = packed.reshape(-1)
    out, off = [], 0
    for sh in shapes:
        n = math.prod(sh)
        out.append(flat[off:off + n].reshape(sh))
        off += n
    return out


def kernel(x, p, norm_w, w_in, s5_a_re, s5_a_im, s5_b_re, s5_b_im, s5_c_re, s5_c_im, s5_d, s5_log_step, s5_w_glu, s5_b_glu, ssd_conv_w, ssd_conv_b, ssd_dt_bias, ssd_a_log, ssd_d, ssd_norm_w, w_br_s5, w_br_ssd, w_out, ple_norm_w, w_ple_gate, w_ple_proj, final_norm_w, loss_target, m_norm_w, m_w_in, m_s5_a_re, m_s5_a_im, m_s5_b_re, m_s5_b_im, m_s5_c_re, m_s5_c_im, m_s5_d, m_s5_log_step, m_s5_w_glu, m_s5_b_glu, m_ssd_conv_w, m_ssd_conv_b, m_ssd_dt_bias, m_ssd_a_log, m_ssd_d, m_ssd_norm_w, m_w_br_s5, m_w_br_ssd, m_w_out, m_ple_norm_w, m_w_ple_gate, m_w_ple_proj, m_final_norm_w, v_norm_w, v_w_in, v_s5_a_re, v_s5_a_im, v_s5_b_re, v_s5_b_im, v_s5_c_re, v_s5_c_im, v_s5_d, v_s5_log_step, v_s5_w_glu, v_s5_b_glu, v_ssd_conv_w, v_ssd_conv_b, v_ssd_dt_bias, v_ssd_a_log, v_ssd_d, v_ssd_norm_w, v_w_br_s5, v_w_br_ssd, v_w_out, v_ple_norm_w, v_w_ple_gate, v_w_ple_proj, v_final_norm_w):
    args = locals()
    wl = {n: args[n] for n in WEIGHTS}
    ml = {n: args["m_" + n] for n in WEIGHTS}
    vl = {n: args["v_" + n] for n in WEIGHTS}
    big = [n for n in SHARDED if n != 'ssd_conv_w']
    chip = _chip_index(lax.axis_index("x"), lax.axis_index("y"))
    idx = jnp.stack([lax.axis_index("c"), chip]).astype(jnp.int32)

    axes = [_half_axis(n) for n in big]
    first = ['w_in']
    rest = [n for n in big if n not in first]
    bf_shards = {n: _shard2d(n, wl[n]).astype(BF16) for n in big}
    w_in_t, conv_w4 = _gather_chips([bf_shards[n] for n in first], [_half_axis(n) for n in first],
                                    [_shard2d('ssd_conv_w', wl['ssd_conv_w'])])
    full = {n: wl[n] for n in SMALL}
    full["w_in_t"] = w_in_t
    full['ssd_conv_w'] = _unslot('ssd_conv_w', conv_w4)
    send_sems, recv_sems, thru, lands, token = _gather_start([bf_shards[n] for n in rest], w_in_t)

    def fetch_rest(after):
        own, zones = _gather_wait(send_sems, recv_sems, thru, lands, after)
        return {n: _unslot(n, lax.dynamic_update_slice(z, o[None], (chip, 0, 0)))
                for n, o, z in zip(rest, own, zones, strict=True)}

    full["_late"] = (token, fetch_rest)

    in_flight = []

    def send_rest(g_now):
        in_flight.extend(_scatter_start("scatter_rest_start", [_slot(n, g_now[n]) for n in rest], [], g_now['s5_w_glu']))
        return in_flight[4]

    full["_early"] = send_rest

    packed = [n for n in SMALL if n != 'norm_w']
    first_axes = [_half_axis(n) for n in first]
    st = {}

    def send_first(g_now, loss_now):
        small_pack = _pack_small([loss_now] + [g_now[n + "_ghp" if n in GHP else n] for n in packed]
                                 + [g_now['ssd_conv_w']])
        swapped = _swap_halves([g_now["w_in_t"]], first_axes, small_pack)
        st["pair"] = [_pair_sum("pair_sum_w_in", g_now["w_in_t"], swapped[0], idx, first_axes[0])]
        small_chip = _sum_slots("sum_small_pair", swapped[-1])
        half_rows = small_chip.shape[0] // 2
        my_half = lax.dynamic_slice(small_chip, (lax.axis_index("c") * half_rows, 0), (half_rows, small_chip.shape[1]))
        st["flight"] = _scatter_start("scatter_first_start", [pb for pb, _ in st["pair"]], [my_half], small_chip)
        return st["flight"][4]

    full["_w_in_ready"] = send_first
    loss, grad_x, g = _local_step(x, p[0], loss_target, full)
    pair = st["pair"]
    s_sems, r_sems, thru, lands, tok = st["flight"]
    small_shapes = [(1, 1)] + [_view_shape(n) for n in packed] + [(SSD_CONV, SSD_CONV_DIM)]
    norm_all = _all_to_all("norm_w_all_to_all", g['norm_w'].reshape(8, 128))
    norm_g = _sum_slots("sum_norm_w", norm_all).reshape(1, D_MODEL)

    out_g, out_d, out_m, out_v = {}, {}, {}, {}
    own_slots, zones = _scatter_wait("scatter_rest_wait", len(rest), *in_flight[:4], norm_g)
    chip_sums = [_sum_chips("sum_chips_" + n, a, z, idx, tok) for n, a, z in zip(rest, own_slots, zones, strict=True)]
    sib_sums = _swap_sibling("swap_sibling_rest", chip_sums)
    for n, own, sib in zip(rest, chip_sums, sib_sums, strict=True):
        res = _adamw("adamw_" + n, _shard2d(n, wl[n]), _shard2d(n, ml[n]), _shard2d(n, vl[n]), [own, sib])
        out_g[n], out_d[n], out_m[n], out_v[n] = (_unshard2d(n, r, wl[n].shape) for r in res)

    sent, got = _scatter_wait("scatter_first_wait", len(first), s_sems, r_sems, thru, lands, out_d[rest[-1]])
    small_zone = lax.dynamic_update_slice(got[-1], sent[-1][None], (chip, 0, 0))
    small_half = _sum_slots("sum_small_chips", small_zone)
    halves = [_sum_parts("sum_chips_" + n, own, r) for n, (_, own), r in zip(first, pair, got[:-1], strict=True)]
    swapped2 = _swap_sibling("swap_sibling_first", halves + [small_half])
    other_halves = swapped2[:-1]
    c_is_0 = lax.axis_index("c") == 0
    small_sum = jnp.concatenate([jnp.where(c_is_0, small_half, swapped2[-1]),
                                 jnp.where(c_is_0, swapped2[-1], small_half)], axis=0)
    for n, own, oth, ax in zip(first, halves, other_halves, first_axes, strict=True):
        res = _adamw_halves("adamw_" + n, _shard2d(n, wl[n]), _shard2d(n, ml[n]), _shard2d(n, vl[n]), own, oth, idx, ax)
        out_g[n], out_d[n], out_m[n], out_v[n] = (_unshard2d(n, r, wl[n].shape) for r in res)
    sm = _unpack_small(small_sum, small_shapes)
    loss_total = sm[0].reshape(())
    conv_g = lax.dynamic_slice(sm[-1], (0, chip * (SSD_CONV_DIM // N_CHIPS)), (SSD_CONV, SSD_CONV_DIM // N_CHIPS))
    names = SMALL + ['ssd_conv_w']
    by_name = {**dict(zip(packed, sm[1:-1], strict=True)), 'norm_w': norm_g, 'ssd_conv_w': conv_g}
    grads = [by_name[n] for n in names]
    res = _adamw_small([_view(n, wl[n]) for n in names], [_view(n, ml[n]) for n in names],
                       [_view(n, vl[n]) for n in names], grads)
    for i, n in enumerate(names):
        out_g[n] = _unview(n, grads[i], wl[n].shape)
        out_d[n], out_m[n], out_v[n] = (_unview(n, res[k * len(names) + i], wl[n].shape) for k in range(3))

    return (loss_total, grad_x, *[out_g[n] for n in WEIGHTS], *[out_d[n] for n in WEIGHTS],
            *[out_m[n] for n in WEIGHTS], *[out_v[n] for n in WEIGHTS])
```

```python
import functools
import math

import jax
import jax.numpy as jnp
from jax import lax
from jax.experimental import pallas as pl
from jax.experimental.pallas import tpu as pltpu

F32 = jnp.float32
BF16 = jnp.bfloat16
MESH = pl.DeviceIdType.MESH

D_MODEL = 1024
PLE_DIM = 256
RMS_EPS = 1e-6
S5_WIDTH = 512
S5_GROUP = 16
S5_GROUPS = 32
S5_STATE = 64
S5_N = S5_GROUPS * S5_STATE
S5_LB = 512
S5_NJ = S5_N // S5_LB
S5_TB = 256
S5_LOG_TB = 8
SSD_WIDTH = 1536
SSD_HEADDIM = 64
SSD_HEADS = 24
SSD_GROUPS = 4
SSD_HPG = 6
SSD_STATE = 128
SSD_CONV = 4
SSD_CHUNK = 128
SSD_BC = 512
SSD_CONV_DIM = 2560
GROUP_W = SSD_WIDTH // SSD_GROUPS
N_CHIPS = 4
N_DEV = 8

OFF_XBC, OFF_U5, OFF_Z5, OFF_DT, OFF_G5, OFF_GS, OFF_ZS = 0, 2560, 3072, 3584, 4096, 5120, 6144
DT_W = 512
PROJ_W = 7680
IN_PROJ_DIM = 7192

ADAM_LR, ADAM_B1, ADAM_B2, ADAM_EPS, ADAM_WD, ADAM_STEP = 0.001, 0.9, 0.999, 1e-08, 0.01, 10

VMEM_LIMIT = 56 * 1024 * 1024


ROW_BLOCK_BYTES = 8 * 1024 * 1024


def _row_tile(r, bytes_per_row):
    for t in (r, 4096, 2048, 1024, 512, 256, 128, 64, 32, 16, 8):
        if t <= r and r % t == 0 and t * bytes_per_row <= ROW_BLOCK_BYTES:
            return t
    return r


def _cparams(sem):
    return pltpu.CompilerParams(dimension_semantics=sem, vmem_limit_bytes=VMEM_LIMIT)


def _dg(a, b, ca, cb):
    return lax.dot_general(a.astype(BF16), b.astype(BF16), (((ca,), (cb,)), ((), ())), preferred_element_type=F32)


@jax.custom_vjp
def dot_nn(a, b):
    return _dg(a, b, 1, 0)


@jax.custom_vjp
def dot_nt(a, b):
    return _dg(a, b, 1, 1)


@jax.custom_vjp
def dot_tn(a, b):
    return _dg(a, b, 0, 0)


dot_nn.defvjp(lambda a, b: (_dg(a, b, 1, 0), (a, b)), lambda r, g: (_dg(g, r[1], 1, 1), _dg(r[0], g, 0, 0)))
dot_nt.defvjp(lambda a, b: (_dg(a, b, 1, 1), (a, b)), lambda r, g: (_dg(g, r[1], 1, 0), _dg(g, r[0], 0, 0)))
dot_tn.defvjp(lambda a, b: (_dg(a, b, 0, 0), (a, b)), lambda r, g: (_dg(r[1], g, 1, 1), _dg(r[0], g, 1, 0)))


MM_VMEM_BUDGET = 40 * 1024 * 1024


MM_STEP_BYTES = 600 * 1024
MM_MIN_TILE = 512


def _mm_tiles(m, n, k, sa, sb, so, tn_only=None):
    best, best_cost = None, None
    for tm in (4096, 2048, 1024, 512, 256, 128, 64, 32, 16, 8):
        if m % tm or tm < min(m, MM_MIN_TILE):
            continue
        for tn in (2048, 1536, 1280, 1024, 768, 640, 512, 384, 256, 128):
            if n % tn or (tn_only is not None and tn not in tn_only) or tn < min(n, MM_MIN_TILE):
                continue
            for tk in (k, 2048, 1536, 1280, 1024, 768, 512, 256, 128):
                if k % tk or tk > max(k, 128):
                    continue
                need = 2 * (tm * tk * sa + tk * tn * sb + tm * tn * so) + (tm * tn * 4 if tk < k else 0)
                if need > MM_VMEM_BUDGET:
                    continue
                steps = (m // tm) * (n // tn) * (k // tk)
                a_reads = 1 if tk == k else n // tn
                b_reads = 1 if (tk == k and tn == n) else m // tm
                cost = m * k * sa * a_reads + k * n * sb * b_reads + m * n * so + steps * MM_STEP_BYTES
                if best_cost is None or cost < best_cost:
                    best, best_cost = (tm, tn, tk), cost
    assert best is not None, (m, n, k)
    return best


def _matmul(name, a, b, *, ta=False, tb=False, a_win=None, out_dtype=F32, epilogue=None, epi_rows=(), epi_pars=(),
            epi_outs=(), full_rows=False, epi_accs=(), epi_into=None, dep=None):
    a_off, a_w = a_win if a_win is not None else (0, a.shape[1])
    if ta:
        kdim, m = a.shape[0], a_w
    else:
        m, kdim = a.shape[0], a_w
    n = b.shape[0] if tb else b.shape[1]
    assert (b.shape[1] if tb else b.shape[0]) == kdim, (name, a.shape, b.shape)
    out_dtypes = list(epi_outs) if epilogue is not None else [out_dtype]
    so = sum(jnp.dtype(d).itemsize for d in out_dtypes) + sum(r.dtype.itemsize for r, _ in epi_rows)
    tn_ok = [n] if full_rows else [t for t in (1024, 512, 256, 128) if all(off % t == 0 for _, off in epi_rows)]
    tm, tn, tk = _mm_tiles(m, n, kdim, a.dtype.itemsize, b.dtype.itemsize, so, tn_ok if epilogue is not None else None)
    nk = kdim // tk
    n_er, n_ep, n_out = len(epi_rows), len(epi_pars), len(out_dtypes)
    if ta:
        assert a_off % tm == 0
        a_spec = pl.BlockSpec((tk, tm), lambda i, j, k: (k, i + a_off // tm))
    else:
        assert a_off % tk == 0
        a_spec = pl.BlockSpec((tm, tk), lambda i, j, k: (i, k + a_off // tk))
    if tb:
        b_spec = pl.BlockSpec((tn, tk), lambda i, j, k: (j, k))
    else:
        b_spec = pl.BlockSpec((tk, tn), lambda i, j, k: (k, j))
    ca, cb = (0 if ta else 1), (1 if tb else 0)

    n_acc = len(epi_accs)
    n_x = (1 if epi_into is not None else 0) + (1 if dep is not None else 0)
    assert not (n_acc or epi_into is not None) or full_rows

    def body(a_ref, b_ref, *refs):
        er, ep = refs[:n_er], refs[n_er:n_er + n_ep]
        first_out = n_er + n_ep + n_x
        o_refs = refs[first_out:first_out + n_out]
        s_refs = refs[first_out + n_out:first_out + n_out + n_acc]
        acc = refs[first_out + n_out + n_acc:]

        def finish(c):
            outs = [c] if epilogue is None else epilogue(c, [r[...] for r in er], [p[...] for p in ep])
            if n_acc:
                outs, sums = outs

                @pl.when(pl.program_id(0) == 0)
                def _():
                    for s_ref in s_refs:
                        s_ref[...] = jnp.zeros_like(s_ref)

                for s_ref, s in zip(s_refs, sums, strict=True):
                    s_ref[...] += jnp.broadcast_to(s, s_ref.shape)
            for o_ref, o in zip(o_refs, outs, strict=True):
                o_ref[...] = o.astype(o_ref.dtype)

        if nk == 1:
            finish(_dg(a_ref[...], b_ref[...], ca, cb))
            return
        (acc_ref,) = acc
        k = pl.program_id(2)

        @pl.when(k == 0)
        def _():
            acc_ref[...] = jnp.zeros_like(acc_ref)

        acc_ref[...] += _dg(a_ref[...], b_ref[...], ca, cb)

        @pl.when(k == nk - 1)
        def _():
            finish(acc_ref[...])

    in_specs = [a_spec, b_spec]
    in_specs += [pl.BlockSpec((tm, tn), functools.partial(lambda i, j, k, c: (i, j + c), c=off // tn)) for _, off in epi_rows]
    in_specs += [pl.BlockSpec((p.shape[0], tn), lambda i, j, k: (0, j)) for p in epi_pars]
    out_specs = [pl.BlockSpec((tm, tn), lambda i, j, k: (i, j)) for _ in out_dtypes]
    out_shape = [jax.ShapeDtypeStruct((m, n), d) for d in out_dtypes]
    extra, aliases = [], {}
    if epi_into is not None:
        buf, off, width = epi_into
        assert off % width == 0 and buf.dtype == out_dtypes[0]
        in_specs.append(pl.BlockSpec(memory_space=pl.ANY))
        out_specs[0] = pl.BlockSpec((tm, width), functools.partial(lambda i, j, k, c: (i, c), c=off // width))
        out_shape[0] = jax.ShapeDtypeStruct(buf.shape, buf.dtype)
        extra, aliases = [buf], {2 + n_er + n_ep: 0}
    if dep is not None:
        in_specs.append(pl.BlockSpec(memory_space=pl.ANY))
        extra = extra + [dep]
    out_specs += [pl.BlockSpec((r, w), lambda i, j, k: (0, 0)) for r, w in epi_accs]
    out_shape += [jax.ShapeDtypeStruct((r, w), F32) for r, w in epi_accs]
    res = pl.pallas_call(
        body, name=name, grid=(m // tm, n // tn, nk),
        in_specs=in_specs, out_specs=out_specs, out_shape=out_shape, input_output_aliases=aliases,
        scratch_shapes=[pltpu.VMEM((tm, tn), F32)] if nk > 1 else [],
        compiler_params=_cparams(("arbitrary",) * 3 if n_acc else ("parallel", "parallel", "arbitrary")),
    )(a, b, *[r for r, _ in epi_rows], *epi_pars, *extra)
    return res if epilogue is not None else res[0]


BAND = 128


def _rowwise(name, fn, n_rows, tr, row_ins, par_ins, row_outs, acc_outs=(), into=None):
    nr, npar, no, na = len(row_ins), len(par_ins), len(row_outs), len(acc_outs)
    in_specs = []
    for arr, off, w in row_ins:
        assert off % w == 0 and arr.shape[0] == n_rows, (name, arr.shape, off, w)
        in_specs.append(pl.BlockSpec((tr, w), functools.partial(lambda i, c: (i, c), c=off // w)))
    for arr, off, w in par_ins:
        assert off % w == 0
        in_specs.append(pl.BlockSpec((arr.shape[0], w), functools.partial(lambda i, c: (0, c), c=off // w)))
    out_specs = [pl.BlockSpec((tr, w), lambda i: (i, 0)) for w, _ in row_outs]
    out_specs += [pl.BlockSpec((r, w), lambda i: (0, 0)) for r, w in acc_outs]
    out_shape = [jax.ShapeDtypeStruct((n_rows, w), dt) for w, dt in row_outs]
    out_shape += [jax.ShapeDtypeStruct((r, w), F32) for r, w in acc_outs]
    extra, aliases = [], {}
    if into is not None:
        buf, off = into
        w0 = row_outs[0][0]
        assert off % w0 == 0 and buf.dtype == row_outs[0][1]
        in_specs.append(pl.BlockSpec(memory_space=pl.ANY))
        out_specs[0] = pl.BlockSpec((tr, w0), functools.partial(lambda i, c: (i, c), c=off // w0))
        out_shape[0] = jax.ShapeDtypeStruct(buf.shape, buf.dtype)
        extra, aliases = [buf], {nr + npar: 0}
    nx = len(extra)

    def body(*refs):
        rows = [r[...] for r in refs[:nr]]
        pars = [r[...] for r in refs[nr:nr + npar]]
        o_refs = refs[nr + npar + nx:nr + npar + nx + no]
        a_refs = refs[nr + npar + nx + no:]
        outs, accs = fn(rows, pars)
        for o_ref, o in zip(o_refs, outs, strict=True):
            o_ref[...] = o.astype(o_ref.dtype)
        if na:
            @pl.when(pl.program_id(0) == 0)
            def _():
                for a_ref in a_refs:
                    a_ref[...] = jnp.zeros_like(a_ref)

            for a_ref, a in zip(a_refs, accs, strict=True):
                a_ref[...] += jnp.broadcast_to(a, a_ref.shape)

    res = pl.pallas_call(
        body, name=name, grid=(n_rows // tr,),
        in_specs=in_specs, out_specs=out_specs, out_shape=out_shape, input_output_aliases=aliases,
        compiler_params=_cparams(("arbitrary",) if na else ("parallel",)),
    )(*[a for a, _, _ in row_ins], *[a for a, _, _ in par_ins], *extra)
    return res


def _rms(x, w):
    return x * lax.rsqrt(jnp.mean(x * x, axis=-1, keepdims=True) + RMS_EPS) * w


def _s5_out(yc, u, z, t, d, bg):
    ge = jax.nn.gelu(yc + d * u)
    return ge * jax.nn.sigmoid(t + bg) * jax.nn.silu(z)


def _merge(g5, gs, m5, ms):
    return jax.nn.sigmoid(g5) * m5 + jax.nn.sigmoid(gs) * ms


def _head_loss(h1, pgl, pp, fw, tgt):
    h2 = h1 + jax.nn.sigmoid(pgl) * pp
    err = _rms(h2, fw) - tgt
    per_row = 0.5 * jnp.mean(err * err, axis=-1, keepdims=True)
    return jnp.sum(per_row, axis=0, keepdims=True)


def _s5_disc(a_re, a_im, log_step, b_re2, b_im2, expand):
    step = jnp.exp(log_step)
    mag = jnp.exp(a_re * step)
    lb_re = mag * jnp.cos(a_im * step)
    lb_im = mag * jnp.sin(a_im * step)
    den = a_re * a_re + a_im * a_im
    n_re = lb_re - 1.0
    f_re = (n_re * a_re + lb_im * a_im) / den
    f_im = (lb_im * a_re - n_re * a_im) / den
    hi = lax.Precision.HIGHEST
    fr = jnp.dot(expand, f_re, precision=hi, preferred_element_type=F32)
    fi = jnp.dot(expand, f_im, precision=hi, preferred_element_type=F32)
    return lb_re, lb_im, fr * b_re2 - fi * b_im2, fr * b_im2 + fi * b_re2


def _s5_params_fwd(a_re, a_im, log_step, b_re2, b_im2, expand):
    gp = a_re.shape

    def body(ar, ai, ls, br, bi, ex, pr_ref, pi_ref, bbr_ref, bbi_ref):
        lr, li, bbr, bbi = _s5_disc(ar[...], ai[...], ls[...], br[...], bi[...], ex[...])
        bbr_ref[...] = bbr
        bbi_ref[...] = bbi
        qr, qi = lr, li
        for k in range(S5_LOG_TB):
            pr_ref[k] = qr
            pi_ref[k] = qi
            qr, qi = qr * lr - qi * li, qr * li + qi * lr

    return pl.pallas_call(
        body, name="s5_params_fwd",
        out_shape=(jax.ShapeDtypeStruct((S5_LOG_TB,) + gp, F32), jax.ShapeDtypeStruct((S5_LOG_TB,) + gp, F32),
                   jax.ShapeDtypeStruct(b_re2.shape, F32), jax.ShapeDtypeStruct(b_re2.shape, F32)),
    )(a_re, a_im, log_step, b_re2, b_im2, expand)


def _s5_params_bwd(a_re, a_im, log_step, b_re2, b_im2, expand, d_lr, d_li, d_bbr, d_bbi):
    def body(ar, ai, ls, br, bi, ex, glr, gli, gbr, gbi, dar, dai, dls, dbr, dbi):
        _, vjp = jax.vjp(lambda *p: _s5_disc(*p, ex[...]), ar[...], ai[...], ls[...], br[...], bi[...])
        g = vjp((glr[...], gli[...], gbr[...], gbi[...]))
        for ref, val in zip((dar, dai, dls, dbr, dbi), g, strict=True):
            ref[...] = val

    return pl.pallas_call(
        body, name="s5_params_bwd",
        out_shape=tuple(jax.ShapeDtypeStruct(v.shape, F32) for v in (a_re, a_im, log_step, b_re2, b_im2)),
    )(a_re, a_im, log_step, b_re2, b_im2, expand, d_lr, d_li, d_bbr, d_bbi)


def _scan_block(x_ref, out_ref, lp, edge_pow, cr, ci, reverse, each=None):
    n_g = x_ref.shape[0] // 8
    sub = lax.broadcasted_iota(jnp.int32, (8, S5_LB), 0)
    steps = []
    for sh in (1, 2, 4):
        keep = (sub < 8 - sh) if reverse else (sub >= sh)
        steps.append((8 - sh if reverse else sh, jnp.where(keep, lp[sh - 1:sh, :S5_LB], 0.0),
                      jnp.where(keep, lp[sh - 1:sh, S5_LB:], 0.0)))
    e_r, e_i = edge_pow[:, :S5_LB], edge_pow[:, S5_LB:]
    for r in (range(n_g - 1, -1, -1) if reverse else range(n_g)):
        rows = slice(8 * r, 8 * r + 8)
        xr, xi = x_ref[rows, :S5_LB], x_ref[rows, S5_LB:]
        for by, a_r, a_i in steps:
            pr, pi = pltpu.roll(xr, by, 0), pltpu.roll(xi, by, 0)
            xr, xi = xr + a_r * pr - a_i * pi, xi + a_r * pi + a_i * pr
        xr = xr + e_r * cr - e_i * ci
        xi = xi + e_r * ci + e_i * cr
        out_ref[rows, :S5_LB] = xr
        out_ref[rows, S5_LB:] = xi
        if each is not None:
            each(r, xr, xi)
        cr, ci = (xr[0:1, :], xi[0:1, :]) if reverse else (xr[7:8, :], xi[7:8, :])
    return cr, ci


def _s5_fwd(proj, bb_band, c_band, lam_pow, s5_d, n_seq, seq_len):
    n_t = seq_len // S5_TB
    blk = 2 * S5_LB
    n_rows = n_seq * seq_len
    u_blk0 = OFF_U5 // BAND

    def body(u_ref, bb_ref, cb_ref, lp_ref, d_ref, s_ref, y_ref, ge_ref, cr, ci, buf):
        @pl.when(pl.program_id(2) == 0)
        def _():
            cr[...] = jnp.zeros_like(cr)
            ci[...] = jnp.zeros_like(ci)

        u = u_ref[...]
        buf[...] = _dg(u, bb_ref[...], 1, 0)
        lp = lp_ref[...]
        cr[...], ci[...] = _scan_block(buf, buf, lp, lp, cr[...], ci[...], False)
        s = buf[...].astype(s_ref.dtype)
        s_ref[...] = s
        y = _dg(s, cb_ref[...], 1, 1)
        y_ref[...] = y
        ge_ref[...] = jax.nn.gelu(y + d_ref[...] * u).astype(ge_ref.dtype)

    def rows(j, b, t):
        return (b * n_t + t, j)

    return pl.pallas_call(
        body, name="s5_fwd", grid=(S5_NJ, n_seq, n_t),
        in_specs=[pl.BlockSpec((S5_TB, BAND), lambda j, b, t: (b * n_t + t, u_blk0 + j)),
                  pl.BlockSpec((BAND, blk), lambda j, b, t: (j, 0)), pl.BlockSpec((BAND, blk), lambda j, b, t: (j, 0)),
                  pl.BlockSpec((S5_LOG_TB, blk), lambda j, b, t: (0, j)), pl.BlockSpec((1, BAND), lambda j, b, t: (0, j))],
        out_specs=[pl.BlockSpec((S5_TB, blk), rows), pl.BlockSpec((S5_TB, BAND), rows), pl.BlockSpec((S5_TB, BAND), rows)],
        out_shape=[jax.ShapeDtypeStruct((n_rows, S5_NJ * blk), BF16), jax.ShapeDtypeStruct((n_rows, S5_WIDTH), F32),
                   jax.ShapeDtypeStruct((n_rows, S5_WIDTH), BF16)],
        scratch_shapes=[pltpu.VMEM((1, S5_LB), F32), pltpu.VMEM((1, S5_LB), F32), pltpu.VMEM((S5_TB, blk), F32)],
        compiler_params=_cparams(("parallel", "parallel", "arbitrary")),
    )(proj, bb_band, c_band, lam_pow, s5_d)


def _s5_bwd(dyc, s, proj, du5_a, bb_band, c_band, lam_pow_conj, dproj, n_seq, seq_len):
    n_t = seq_len // S5_TB
    blk = 2 * S5_LB
    halo_rows = 16
    halo_per_blk = S5_TB // halo_rows
    u_blk0 = OFF_U5 // BAND

    def rows(j, b, t):
        return (b * n_t + (n_t - 1 - t), j)

    def u_rows(j, b, t):
        return (b * n_t + (n_t - 1 - t), u_blk0 + j)

    def halo(j, b, t):
        return (jnp.maximum((b * n_t + (n_t - 1 - t)) * halo_per_blk - 1, 0), j)

    def body(dy_ref, sq_ref, hq_ref, u_ref, dua_ref, bb_ref, cb_ref, lp_ref, _, du_ref, dl_ref, dbb_ref, dcb_ref,
             cr, ci, buf, s_ref):
        b, t = pl.program_id(1), pl.program_id(2)
        sq = sq_ref[...]
        s_ref[...] = sq.astype(F32)
        h_last = hq_ref[...].astype(F32)[halo_rows - 1:halo_rows, :]
        dy = dy_ref[...]
        buf[...] = _dg(dy, cb_ref[...], 1, 0)

        @pl.when(t == 0)
        def _():
            cr[...] = jnp.zeros_like(cr)
            ci[...] = jnp.zeros_like(ci)

        @pl.when((b == 0) & (t == 0))
        def _():
            dl_ref[...] = jnp.zeros_like(dl_ref)
            dbb_ref[...] = jnp.zeros_like(dbb_ref)
            dcb_ref[...] = jnp.zeros_like(dcb_ref)

        dcb_ref[...] += _dg(dy, sq, 0, 0)

        first_blk = t == n_t - 1
        sub = lax.broadcasted_iota(jnp.int32, (8, S5_LB), 0)
        acc = [jnp.zeros((8, S5_LB), F32), jnp.zeros((8, S5_LB), F32)]

        def each(r, gr, gi):
            rows = slice(8 * r, 8 * r + 8)
            if r == 0:
                before_r = jnp.where(first_blk, 0.0, h_last[:, :S5_LB])
                before_i = jnp.where(first_blk, 0.0, h_last[:, S5_LB:])
            else:
                before_r, before_i = s_ref[8 * r - 1:8 * r, :S5_LB], s_ref[8 * r - 1:8 * r, S5_LB:]
            sp_r = jnp.where(sub == 0, before_r, pltpu.roll(s_ref[rows, :S5_LB], 1, 0))
            sp_i = jnp.where(sub == 0, before_i, pltpu.roll(s_ref[rows, S5_LB:], 1, 0))
            acc[0] = acc[0] + gr * sp_r + gi * sp_i
            acc[1] = acc[1] + gi * sp_r - gr * sp_i

        lp = lp_ref[...]
        edge_pow = jnp.concatenate([lp[7 - i:8 - i, :] for i in range(8)], axis=0)
        cr[...], ci[...] = _scan_block(buf, buf, lp, edge_pow, cr[...], ci[...], True, each)
        g = buf[...].astype(BF16)
        du_ref[...] = (dua_ref[...] + _dg(g, bb_ref[...], 1, 1)).astype(du_ref.dtype)
        dbb_ref[...] += _dg(u_ref[...], g, 0, 0)
        dl_ref[:, :S5_LB] += jnp.sum(acc[0], axis=0, keepdims=True)
        dl_ref[:, S5_LB:] += jnp.sum(acc[1], axis=0, keepdims=True)

    band = pl.BlockSpec((BAND, blk), lambda j, b, t: (j, 0))
    return pl.pallas_call(
        body, name="s5_bwd", grid=(S5_NJ, n_seq, n_t),
        in_specs=[pl.BlockSpec((S5_TB, BAND), rows), pl.BlockSpec((S5_TB, blk), rows),
                  pl.BlockSpec((halo_rows, blk), halo), pl.BlockSpec((S5_TB, BAND), u_rows),
                  pl.BlockSpec((S5_TB, BAND), rows), band, band,
                  pl.BlockSpec((S5_LOG_TB, blk), lambda j, b, t: (0, j)), pl.BlockSpec(memory_space=pl.ANY)],
        out_specs=[pl.BlockSpec((S5_TB, BAND), u_rows), pl.BlockSpec((1, blk), lambda j, b, t: (0, j)), band, band],
        out_shape=[jax.ShapeDtypeStruct(dproj.shape, dproj.dtype), jax.ShapeDtypeStruct((1, S5_NJ * blk), F32),
                   jax.ShapeDtypeStruct(bb_band.shape, F32), jax.ShapeDtypeStruct(c_band.shape, F32)],
        input_output_aliases={8: 0},
        scratch_shapes=[pltpu.VMEM((1, S5_LB), F32), pltpu.VMEM((1, S5_LB), F32), pltpu.VMEM((S5_TB, blk), F32),
                        pltpu.VMEM((S5_TB, blk), F32)],
        compiler_params=_cparams(("parallel", "arbitrary", "arbitrary")),
    )(dyc, s, s, proj, du5_a, bb_band, c_band, lam_pow_conj, dproj)


CONV_TR = 512
CONV_CW = 512


def _shift_down(x, halo, k):
    if k == 0:
        return x
    row8 = lax.broadcasted_iota(jnp.int32, halo.shape, 0)
    rolled = pltpu.roll(x, k, 0)
    top = jnp.where(row8 < k, pltpu.roll(halo, k, 0), rolled[:8])
    if x.shape[0] == 8:
        return top
    return jnp.concatenate([top, rolled[8:]], axis=0)


def _shift_up(x, halo, k):
    if k == 0:
        return x
    n = x.shape[0]
    row8 = lax.broadcasted_iota(jnp.int32, halo.shape, 0)
    rolled = pltpu.roll(x, n - k, 0)
    bot = jnp.where(row8 >= 8 - k, pltpu.roll(halo, 8 - k, 0), rolled[n - 8:])
    if n == 8:
        return bot
    return jnp.concatenate([rolled[:n - 8], bot], axis=0)


def _conv_pre(x, halo, w, b):
    acc = b + w[SSD_CONV - 1:SSD_CONV, :] * x
    for k in range(SSD_CONV - 1):
        acc = acc + w[k:k + 1, :] * _shift_down(x, halo, SSD_CONV - 1 - k)
    return acc


def _conv_specs(seq_len, col_off):
    lt = seq_len // CONV_TR
    cb = col_off // CONV_CW
    cur = pl.BlockSpec((CONV_TR, CONV_CW), lambda j, i: (i, j + cb))
    prev = pl.BlockSpec((8, CONV_CW), lambda j, i: (jnp.maximum(i * (CONV_TR // 8) - 1, 0), j + cb))
    return lt, cur, prev


def _conv_fwd(proj, conv_w, conv_b, n_rows, seq_len):
    lt, cur, prev = _conv_specs(seq_len, OFF_XBC)

    def body(x_ref, h_ref, w_ref, b_ref, o_ref):
        halo = jnp.where(pl.program_id(1) % lt == 0, 0.0, h_ref[...])
        o_ref[...] = jax.nn.silu(_conv_pre(x_ref[...], halo, w_ref[...], b_ref[...]))

    return pl.pallas_call(
        body, name="ssd_conv_fwd", grid=(SSD_CONV_DIM // CONV_CW, n_rows // CONV_TR),
        in_specs=[cur, prev, pl.BlockSpec((SSD_CONV, CONV_CW), lambda j, i: (0, j)),
                  pl.BlockSpec((1, CONV_CW), lambda j, i: (0, j))],
        out_specs=pl.BlockSpec((CONV_TR, CONV_CW), lambda j, i: (i, j)),
        out_shape=jax.ShapeDtypeStruct((n_rows, SSD_CONV_DIM), F32),
        compiler_params=_cparams(("parallel", "parallel")),
    )(proj, proj, conv_w, conv_b)


def _conv_bwd(name, proj, d_act, conv_w, conv_b, n_rows, seq_len, col_off, dproj):
    width = d_act.shape[1]
    lt, cur, prev = _conv_specs(seq_len, OFF_XBC + col_off)
    n_blk = n_rows // CONV_TR
    cb = (OFF_XBC + col_off) // CONV_CW
    pb = col_off // CONV_CW
    nxt = pl.BlockSpec((8, CONV_CW), lambda j, i: (jnp.minimum((i + 1) * (CONV_TR // 8), n_rows // 8 - 1), j + cb))
    d_cur = pl.BlockSpec((CONV_TR, CONV_CW), lambda j, i: (i, j))
    d_nxt = pl.BlockSpec((8, CONV_CW), lambda j, i: (jnp.minimum((i + 1) * (CONV_TR // 8), n_rows // 8 - 1), j))

    def dsilu(pre):
        sg = jax.nn.sigmoid(pre)
        return sg * (1.0 + pre * (1.0 - sg))

    def body(x_ref, hp_ref, hn_ref, d_ref, dn_ref, w_ref, b_ref, _, dx_ref, dw_ref, db_ref):
        i = pl.program_id(1)
        x, w, b = x_ref[...], w_ref[...], b_ref[...]
        halo_p = jnp.where(i % lt == 0, 0.0, hp_ref[...])
        at_end = i % lt == lt - 1
        dpre = d_ref[...] * dsilu(_conv_pre(x, halo_p, w, b))
        pre_n = _conv_pre(hn_ref[...], x[CONV_TR - 8:, :], w, b)
        dpre_n = jnp.where(at_end, 0.0, dn_ref[...] * dsilu(pre_n))
        dx = w[SSD_CONV - 1:SSD_CONV, :] * dpre
        for k in range(SSD_CONV - 1):
            dx = dx + w[k:k + 1, :] * _shift_up(dpre, dpre_n, SSD_CONV - 1 - k)
        dx_ref[...] = dx.astype(dx_ref.dtype)

        @pl.when(i == 0)
        def _():
            dw_ref[...] = jnp.zeros_like(dw_ref)
            db_ref[...] = jnp.zeros_like(db_ref)

        for k in range(SSD_CONV):
            xs = _shift_down(x, halo_p, SSD_CONV - 1 - k)
            dw_ref[k:k + 1, :] += jnp.sum(dpre * xs, axis=0, keepdims=True)
        db_ref[...] += jnp.sum(dpre, axis=0, keepdims=True)

    return pl.pallas_call(
        body, name=name, grid=(width // CONV_CW, n_blk),
        in_specs=[cur, prev, nxt, d_cur, d_nxt,
                  pl.BlockSpec((SSD_CONV, CONV_CW), lambda j, i: (0, j + pb)),
                  pl.BlockSpec((1, CONV_CW), lambda j, i: (0, j + pb)), pl.BlockSpec(memory_space=pl.ANY)],
        out_specs=[pl.BlockSpec((CONV_TR, CONV_CW), lambda j, i: (i, j + cb)),
                   pl.BlockSpec((SSD_CONV, CONV_CW), lambda j, i: (0, j)),
                   pl.BlockSpec((1, CONV_CW), lambda j, i: (0, j))],
        out_shape=[jax.ShapeDtypeStruct(dproj.shape, dproj.dtype),
                   jax.ShapeDtypeStruct((SSD_CONV, width), F32), jax.ShapeDtypeStruct((1, width), F32)],
        input_output_aliases={7: 0},
        compiler_params=_cparams(("parallel", "arbitrary")),
    )(proj, proj, proj, d_act, d_act, conv_w, conv_b, dproj)


def _split3(x):
    hi = x.astype(BF16)
    r = x - hi.astype(F32)
    mid = r.astype(BF16)
    return hi, mid, (r - mid.astype(F32)).astype(BF16)


def _sel_dot(a, b, a_is_sel):
    dn = (((1,), (0,)), ((), ()))
    if a_is_sel:
        return sum(lax.dot_general(a, t, dn, preferred_element_type=F32) for t in _split3(b))
    return sum(lax.dot_general(t, b, dn, preferred_element_type=F32) for t in _split3(a))


@jax.custom_vjp
def sel_left(sel, sel_t, x):
    return _sel_dot(sel, x, True)


@jax.custom_vjp
def sel_right(x, sel, sel_t):
    return _sel_dot(x, sel, False)


sel_left.defvjp(lambda s, st, x: (_sel_dot(s, x, True), (s, st)),
                lambda r, g: (jnp.zeros_like(r[0]), jnp.zeros_like(r[1]), _sel_dot(r[1], g, True)))
sel_right.defvjp(lambda x, s, st: (_sel_dot(x, s, False), (s, st)),
                 lambda r, g: (_sel_dot(g, r[1], False), jnp.zeros_like(r[0]), jnp.zeros_like(r[1])))


def _ssd_chunk(xs, bm, cm, dtr, st, dtb, alog, dsk, k):
    dt = jax.nn.softplus(dtr + dtb)
    acum = sel_left(k["tri"], k["tri_t"], dt * (-jnp.exp(alog)))
    dt_e = sel_right(dt, k["spread"], k["spread_t"])
    ac_e = sel_right(acum, k["spread"], k["spread_t"])
    al_e = ac_e[SSD_CHUNK - 1:SSD_CHUNK, :]
    dsk_e = sel_right(jnp.broadcast_to(dsk, (8, 128)), k["spread"], k["spread_t"])[0:1, :]
    xdt = xs * dt_e
    acum_t = acum.T
    scores = dot_nt(cm, bm)
    y = dot_nn(cm, st) * jnp.exp(ac_e) + xs * dsk_e
    for j in range(SSD_HPG):
        lmat = jnp.exp(jnp.where(k["causal"], acum[:, j:j + 1] - acum_t[j:j + 1, :], -jnp.inf))
        y = y + dot_nn(scores * lmat, jnp.where(k["head"] == j, xdt, 0.0))
    new = st * jnp.exp(al_e) + dot_tn(bm, xdt * jnp.exp(al_e - ac_e))
    return y, new


def _ssd_consts():
    r = lax.broadcasted_iota(jnp.int32, (SSD_CHUNK, SSD_CHUNK), 0)
    c = lax.broadcasted_iota(jnp.int32, (SSD_CHUNK, SSD_CHUNK), 1)
    hd = jnp.int32(SSD_HEADDIM)
    sr = lax.broadcasted_iota(jnp.int32, (128, GROUP_W), 0)
    sc = lax.div(lax.broadcasted_iota(jnp.int32, (128, GROUP_W), 1), hd)
    tr = lax.div(lax.broadcasted_iota(jnp.int32, (GROUP_W, 128), 0), hd)
    tc = lax.broadcasted_iota(jnp.int32, (GROUP_W, 128), 1)
    return {"tri": (r >= c).astype(BF16), "tri_t": (c >= r).astype(BF16), "causal": r >= c,
            "spread": (sr == sc).astype(BF16), "spread_t": (tr == tc).astype(BF16),
            "head": lax.div(lax.broadcasted_iota(jnp.int32, (SSD_CHUNK, GROUP_W), 1), hd)}


def _ssd_specs(n_c, reverse):
    def cidx(c):
        return n_c - 1 - c if reverse else c

    xs = pl.BlockSpec((SSD_CHUNK, GROUP_W), lambda g, b, c: (b * n_c + cidx(c), g))
    bm = pl.BlockSpec((SSD_CHUNK, SSD_STATE), lambda g, b, c: (b * n_c + cidx(c), SSD_WIDTH // SSD_STATE + g))
    cm = pl.BlockSpec((SSD_CHUNK, SSD_STATE), lambda g, b, c: (b * n_c + cidx(c), (SSD_WIDTH + SSD_BC) // SSD_STATE + g))
    dt = pl.BlockSpec((SSD_CHUNK, 128), lambda g, b, c: (b * n_c + cidx(c), OFF_DT // 128 + g))
    par = pl.BlockSpec((1, 128), lambda g, b, c: (0, g))
    st = pl.BlockSpec((1, 1, 1, SSD_STATE, GROUP_W), lambda g, b, c: (b, cidx(c), g, 0, 0))
    return xs, bm, cm, dt, par, st


def _ssd_chunk_gated(xs, bm, cm, dtr, st, dtb, alog, dsk, z, nw, k):
    y, new = _ssd_chunk(xs, bm, cm, dtr, st, dtb, alog, dsk, k)
    yg = y * jax.nn.silu(z)
    return yg * lax.rsqrt(jnp.mean(yg * yg, axis=-1, keepdims=True) + RMS_EPS) * nw, new


def _ssd_gate_specs(n_c, reverse):
    def cidx(c):
        return n_c - 1 - c if reverse else c

    z = pl.BlockSpec((SSD_CHUNK, GROUP_W), lambda g, b, c: (b * n_c + cidx(c), OFF_ZS // GROUP_W + g))
    nw = pl.BlockSpec((1, GROUP_W), lambda g, b, c: (0, g))
    return z, nw


def _ssd_fwd(xbc_act, proj, dtb, alog, dsk, norm_w, n_seq, seq_len):
    n_c = seq_len // SSD_CHUNK
    xs_s, bm_s, cm_s, dt_s, par_s, st_s = _ssd_specs(n_c, False)
    z_s, nw_s = _ssd_gate_specs(n_c, False)

    def body(xs_ref, bm_ref, cm_ref, dt_ref, dtb_ref, al_ref, dk_ref, z_ref, nw_ref, y_ref, st_ref, state):
        @pl.when(pl.program_id(2) == 0)
        def _():
            state[...] = jnp.zeros_like(state)

        prev = state[...]
        st_ref[0, 0, 0] = prev
        y, new = _ssd_chunk_gated(xs_ref[...], bm_ref[...], cm_ref[...], dt_ref[...], prev,
                                  dtb_ref[...], al_ref[...], dk_ref[...], z_ref[...], nw_ref[...], _ssd_consts())
        y_ref[...] = y.astype(y_ref.dtype)
        state[...] = new

    return pl.pallas_call(
        body, name="ssd_fwd", grid=(SSD_GROUPS, n_seq, n_c),
        in_specs=[xs_s, bm_s, cm_s, dt_s, par_s, par_s, par_s, z_s, nw_s],
        out_specs=[pl.BlockSpec((SSD_CHUNK, GROUP_W), lambda g, b, c: (b * n_c + c, g)), st_s],
        out_shape=[jax.ShapeDtypeStruct((n_seq * seq_len, SSD_WIDTH), BF16),
                   jax.ShapeDtypeStruct((n_seq, n_c, SSD_GROUPS, SSD_STATE, GROUP_W), F32)],
        scratch_shapes=[pltpu.VMEM((SSD_STATE, GROUP_W), F32)],
        compiler_params=_cparams(("parallel", "parallel", "arbitrary")),
    )(xbc_act, xbc_act, xbc_act, proj, dtb, alog, dsk, proj, norm_w)


def _ssd_bwd(xbc_act, proj, states, dy, dtb, alog, dsk, norm_w, n_seq, seq_len, dproj):
    n_c = seq_len // SSD_CHUNK
    n_rows = n_seq * seq_len
    xs_s, bm_s, cm_s, dt_s, par_s, st_s = _ssd_specs(n_c, True)
    z_s, nw_s = _ssd_gate_specs(n_c, True)

    def rows(w, first=0):
        return pl.BlockSpec((SSD_CHUNK, w), lambda g, b, c: (b * n_c + (n_c - 1 - c), first + g))

    def body(xs_ref, bm_ref, cm_ref, dt_ref, st_ref, dy_ref, dtb_ref, al_ref, dk_ref, z_ref, nw_ref, _,
             dxs_ref, dbm_ref, dcm_ref, dz_ref, ddt_ref, ddtb_ref, dal_ref, ddk_ref, dnw_ref, dstate):
        b, c = pl.program_id(1), pl.program_id(2)

        @pl.when(c == 0)
        def _():
            dstate[...] = jnp.zeros_like(dstate)

        @pl.when((b == 0) & (c == 0))
        def _():
            for ref in (ddtb_ref, dal_ref, ddk_ref, dnw_ref):
                ref[...] = jnp.zeros_like(ref)

        consts = _ssd_consts()
        _, vjp = jax.vjp(
            lambda *a: _ssd_chunk_gated(*a, consts),
            xs_ref[...], bm_ref[...], cm_ref[...], dt_ref[...], st_ref[0, 0, 0], dtb_ref[...], al_ref[...], dk_ref[...],
            z_ref[...], nw_ref[...])
        dxs, dbm, dcm, ddtr, dprev, ddtb, dal, ddk, dz, dnw = vjp((dy_ref[...], dstate[...]))
        dxs_ref[...] = dxs
        dbm_ref[...] = dbm
        dcm_ref[...] = dcm
        dz_ref[...] = dz.astype(dz_ref.dtype)
        ddt_ref[...] = ddtr.astype(ddt_ref.dtype)
        ddtb_ref[...] += ddtb
        dal_ref[...] += dal
        ddk_ref[...] += ddk
        dnw_ref[...] += dnw
        dstate[...] = dprev

    acc = pl.BlockSpec((1, 128), lambda g, b, c: (0, g))
    return pl.pallas_call(
        body, name="ssd_bwd", grid=(SSD_GROUPS, n_seq, n_c),
        in_specs=[xs_s, bm_s, cm_s, dt_s, st_s, rows(GROUP_W), par_s, par_s, par_s, z_s, nw_s,
                  pl.BlockSpec(memory_space=pl.ANY)],
        out_specs=[rows(GROUP_W), rows(SSD_STATE), rows(SSD_STATE), rows(GROUP_W, OFF_ZS // GROUP_W), rows(128),
                   acc, acc, acc, pl.BlockSpec((1, GROUP_W), lambda g, b, c: (0, g))],
        out_shape=[jax.ShapeDtypeStruct((n_rows, SSD_WIDTH), F32), jax.ShapeDtypeStruct((n_rows, SSD_BC), F32),
                   jax.ShapeDtypeStruct((n_rows, SSD_BC), F32), jax.ShapeDtypeStruct(dproj.shape, dproj.dtype),
                   jax.ShapeDtypeStruct((n_rows, DT_W), BF16),
                   jax.ShapeDtypeStruct((1, 512), F32), jax.ShapeDtypeStruct((1, 512), F32),
                   jax.ShapeDtypeStruct((1, 512), F32), jax.ShapeDtypeStruct((1, SSD_WIDTH), F32)],
        input_output_aliases={11: 3},
        scratch_shapes=[pltpu.VMEM((SSD_STATE, GROUP_W), F32)],
        compiler_params=_cparams(("parallel", "arbitrary", "arbitrary")),
    )(xbc_act, xbc_act, xbc_act, proj, states, dy, dtb, alog, dsk, proj, norm_w, dproj)


def _pad_heads(v):
    return jnp.pad(v.reshape(SSD_GROUPS, SSD_HPG), ((0, 0), (0, 128 - SSD_HPG))).reshape(1, SSD_GROUPS * 128)


def _unpad_heads(v):
    return v.reshape(SSD_GROUPS, 128)[:, :SSD_HPG].reshape(1, SSD_HEADS)


def _state_cols(v):
    re, im = v
    lead = re.shape[:-1]
    re = re.reshape(lead + (S5_NJ, 1, S5_LB))
    im = im.reshape(lead + (S5_NJ, 1, S5_LB))
    return jnp.concatenate([re, im], axis=-2).reshape(lead + (2 * S5_N,))


def _state_uncols(v):
    lead = v.shape[:-1]
    v = v.reshape(lead + (S5_NJ, 2, S5_LB))
    return v[..., 0, :].reshape(lead + (S5_N,)), v[..., 1, :].reshape(lead + (S5_N,))


GROUPS_PER_BAND = BAND // S5_GROUP


def _band(w2_re, w2_im):
    gh = S5_GROUPS * S5_GROUP
    rg = ((jnp.arange(gh) // S5_GROUP) % GROUPS_PER_BAND)[:, None, None]
    cg = jnp.arange(GROUPS_PER_BAND)[None, :, None]
    parts = [jnp.where(rg == cg, v[:, None, :], 0.0).reshape(gh, S5_LB) for v in (w2_re, w2_im)]
    return jnp.concatenate(parts, axis=1)


def _band_take(wb):
    gh = S5_GROUPS * S5_GROUP
    w4 = wb.reshape(gh, 2, GROUPS_PER_BAND, S5_STATE)
    sel = w4[jnp.arange(gh), :, (jnp.arange(gh) // S5_GROUP) % GROUPS_PER_BAND, :]
    return sel[:, 0, :], sel[:, 1, :]


W_IN_SHARD = IN_PROJ_DIM // N_CHIPS
W_IN_SEGS = ((0, 512, OFF_U5), (512, 1024, OFF_Z5), (1024, 2560, OFF_ZS), (2560, 5120, OFF_XBC), (5144, 7192, OFF_G5))
DT_ROWS = (5120, 5144)


def _w_in_pieces():
    runs = []
    segs = list(W_IN_SEGS) + [(DT_ROWS[0] + SSD_HPG * g, DT_ROWS[0] + SSD_HPG * (g + 1), OFF_DT + 128 * g)
                              for g in range(SSD_GROUPS)]
    for lo, hi, off in segs:
        for j in range(N_CHIPS):
            s, e = max(lo, j * W_IN_SHARD), min(hi, (j + 1) * W_IN_SHARD)
            if s < e:
                runs.append((j, s - j * W_IN_SHARD, off + s - lo, e - s))
    return runs


RELAYOUT_LANES = 256


def _pad_w_in_t(a4):
    runs = _w_in_pieces()

    def body(a_ref, o_ref):
        o_ref[pl.ds(OFF_DT, DT_W), :] = jnp.zeros((DT_W, RELAYOUT_LANES), o_ref.dtype)
        for j, src, dst, n in runs:
            o_ref[pl.ds(dst, n), :] = a_ref[j, pl.ds(src, n), :]

    return pl.pallas_call(
        body, name="w_in_to_padded", grid=(D_MODEL // RELAYOUT_LANES,),
        in_specs=[pl.BlockSpec((N_CHIPS, W_IN_SHARD, RELAYOUT_LANES), lambda i: (0, 0, i))],
        out_specs=pl.BlockSpec((PROJ_W, RELAYOUT_LANES), lambda i: (0, i)),
        out_shape=jax.ShapeDtypeStruct((PROJ_W, D_MODEL), a4.dtype),
        compiler_params=_cparams(("parallel",)),
    )(a4)


def _unpad_w_in_t(wp):
    runs = _w_in_pieces()

    def body(p_ref, o_ref):
        for j, dst, src, n in runs:
            o_ref[j, pl.ds(dst, n), :] = p_ref[pl.ds(src, n), :]

    return pl.pallas_call(
        body, name="w_in_from_padded", grid=(D_MODEL // RELAYOUT_LANES,),
        in_specs=[pl.BlockSpec((PROJ_W, RELAYOUT_LANES), lambda i: (0, i))],
        out_specs=pl.BlockSpec((N_CHIPS, W_IN_SHARD, RELAYOUT_LANES), lambda i: (0, 0, i)),
        out_shape=jax.ShapeDtypeStruct((N_CHIPS, W_IN_SHARD, D_MODEL), wp.dtype),
        compiler_params=_cparams(("parallel",)),
    )(wp)


def _local_step(x, p, tgt, w):
    n_seq, seq_len, _ = x.shape
    n_rows = n_seq * seq_len
    tr = 512
    x2 = x.reshape(n_rows, D_MODEL)
    p2 = p.reshape(n_rows, PLE_DIM)
    t2 = tgt.reshape(n_rows, D_MODEL)
    row = functools.partial(_rowwise, n_rows=n_rows, tr=tr)

    w_pad_t = _pad_w_in_t(w["w_in_t"])
    norm_w = w["norm_w"].reshape(1, D_MODEL)
    ple_norm_w = w["ple_norm_w"].reshape(1, D_MODEL)
    final_w = w["final_norm_w"].reshape(1, D_MODEL)
    s5_d = w["s5_d"].reshape(1, S5_WIDTH)
    b_glu = w["s5_b_glu"].reshape(1, S5_WIDTH)
    conv_w = w["ssd_conv_w"].reshape(SSD_CONV, SSD_CONV_DIM)
    conv_b = w["ssd_conv_b"].reshape(1, SSD_CONV_DIM)
    ssd_norm_w = w["ssd_norm_w"].reshape(1, SSD_WIDTH)
    dtb, alog, dsk = (_pad_heads(w[k].reshape(1, SSD_HEADS)) for k in ("ssd_dt_bias", "ssd_a_log", "ssd_d"))

    gh = S5_GROUPS * S5_GROUP
    a_re = w["s5_a_re"].reshape(S5_GROUPS, S5_STATE)
    a_im = w["s5_a_im"].reshape(S5_GROUPS, S5_STATE)
    log_step = w["s5_log_step"].reshape(S5_GROUPS, 1)
    b_re2 = jnp.transpose(w["s5_b_re"].reshape(S5_GROUPS, S5_STATE, S5_GROUP), (0, 2, 1)).reshape(gh, S5_STATE)
    b_im2 = jnp.transpose(w["s5_b_im"].reshape(S5_GROUPS, S5_STATE, S5_GROUP), (0, 2, 1)).reshape(gh, S5_STATE)
    expand = (jnp.arange(gh)[:, None] // S5_GROUP == jnp.arange(S5_GROUPS)[None, :]).astype(F32)
    pow_re, pow_im, bb_re2, bb_im2 = _s5_params_fwd(a_re, a_im, log_step, b_re2, b_im2, expand)
    lam_pow = _state_cols((pow_re.reshape(S5_LOG_TB, S5_N), pow_im.reshape(S5_LOG_TB, S5_N)))
    lam_pow_conj = _state_cols((pow_re.reshape(S5_LOG_TB, S5_N), -pow_im.reshape(S5_LOG_TB, S5_N)))
    bb_band = _band(bb_re2, bb_im2).astype(BF16)
    c_band = _band(w["s5_c_re"].reshape(gh, S5_STATE), -w["s5_c_im"].reshape(gh, S5_STATE)).astype(BF16)

    late = w.get("_late")
    (hn,) = row("rms_in", lambda r, q: ([_rms(r[0], q[0])], []), row_ins=[(x2, 0, D_MODEL)],
                par_ins=[(norm_w, 0, D_MODEL)] + ([(late[0], 0, 128)] if late else []), row_outs=[(D_MODEL, BF16)])
    proj = _matmul("mm_proj", hn, w_pad_t, tb=True)
    s, yc, ge = _s5_fwd(proj, bb_band, c_band, lam_pow, s5_d, n_seq, seq_len)
    if late:
        w = {**w, **late[1](ge)}
    tg, y5 = _matmul("mm_s5_glu", ge, w["s5_w_glu"], epilogue=lambda c, r, q: [c, _s5_out(r[0], r[1], r[2], c, *q)],
                     epi_rows=[(yc, 0), (proj, OFF_U5), (proj, OFF_Z5)], epi_pars=[s5_d, b_glu], epi_outs=[F32, BF16])

    xbc_act = _conv_fwd(proj, conv_w, conv_b, n_rows, seq_len)
    yss, states = _ssd_fwd(xbc_act, proj, dtb, alog, dsk, ssd_norm_w, n_seq, seq_len)

    m5 = _matmul("mm_br_s5", y5, w["w_br_s5"])
    ms, merged = _matmul("mm_br_ssd", yss, w["w_br_ssd"], epilogue=lambda c, r, q: [c, _merge(r[0], r[1], r[2], c)],
                         epi_rows=[(proj, OFF_G5), (proj, OFF_GS), (m5, 0)], epi_outs=[F32, BF16])

    def resid_norm(c, r, q):
        h1_ = r[0] + c
        return [h1_, _rms(h1_, q[0])]

    h1, hp = _matmul("mm_out", merged, w["w_out"], epilogue=resid_norm, epi_rows=[(x2, 0)], epi_pars=[ple_norm_w],
                     epi_outs=[F32, BF16], full_rows=True)
    pp = _matmul("mm_ple_proj", p2, w["w_ple_proj"])

    def head_fn(pgl_, r, q):
        h1_, pp_, tgt_ = r
        loss, vjp = jax.vjp(lambda a, b, c, f: _head_loss(a, b, c, f, tgt_), h1_, pgl_, pp_, q[0])
        dh1_, dpgl_, dpp_, dfw_ = vjp(jnp.ones_like(loss))
        return [dh1_, dpgl_, dpp_], [loss, dfw_]

    dh2, dpgl, dpp, loss_acc, d_final_w = _matmul(
        "mm_ple_gate_head", hp, w["w_ple_gate"], epilogue=head_fn, epi_rows=[(h1, 0), (pp, 0), (t2, 0)],
        epi_pars=[final_w], epi_outs=[F32, BF16, BF16], epi_accs=[(1, 128), (1, D_MODEL)], full_rows=True)
    loss = loss_acc[0, 0]

    g = {}
    g["final_norm_w"] = d_final_w
    g["w_ple_gate"] = _matmul("mm_d_w_ple_gate", hp, dpgl, ta=True)
    g["w_ple_proj"] = _matmul("mm_d_w_ple_proj", p2, dpp, ta=True)

    def ple_norm_bwd(dhp_, r, q):
        h1_, dh2_ = r
        _, vjp = jax.vjp(_rms, h1_, q[0])
        dh, dw = vjp(dhp_)
        dh = dh + dh2_
        return [dh, dh], [dw]

    dh1, dh1_b, g["ple_norm_w"] = _matmul(
        "mm_d_hp", dpgl, w["w_ple_gate"], tb=True, epilogue=ple_norm_bwd, epi_rows=[(h1, 0), (dh2, 0)],
        epi_pars=[ple_norm_w], epi_outs=[F32, BF16], epi_accs=[(1, D_MODEL)], full_rows=True)
    g["w_out"] = _matmul("mm_d_w_out", merged, dh1_b, ta=True)

    dproj = lax.empty((n_rows, PROJ_W), BF16)

    def merge_bwd(dmerged, r, q):
        sg5, sgs = jax.nn.sigmoid(r[0]), jax.nn.sigmoid(r[1])
        d_gates = jnp.concatenate([dmerged * r[2] * sg5 * (1.0 - sg5), dmerged * r[3] * sgs * (1.0 - sgs)], axis=1)
        return [d_gates, dmerged * sg5, dmerged * sgs]

    dproj, dm5, dms = _matmul(
        "mm_d_merged", dh1_b, w["w_out"], tb=True, epilogue=merge_bwd,
        epi_rows=[(proj, OFF_G5), (proj, OFF_GS), (m5, 0), (ms, 0)], epi_outs=[BF16, BF16, BF16], full_rows=True,
        epi_into=(dproj, OFF_G5, 2 * D_MODEL))
    g["w_br_s5"] = _matmul("mm_d_w_br_s5", y5, dm5, ta=True)
    g["w_br_ssd"] = _matmul("mm_d_w_br_ssd", yss, dms, ta=True)
    dyss = _matmul("mm_d_yss", dms, w["w_br_ssd"], tb=True)

    def s5_out_bwd_a(dy_, r, q):
        yc_, u_, z_, t_ = r
        d_, bg_ = q
        ge_ = jax.nn.gelu(yc_ + d_ * u_)
        _, vjp = jax.vjp(lambda a, z, t, b: a * jax.nn.sigmoid(t + b) * jax.nn.silu(z), ge_, z_, t_, bg_)
        dge, dz, dt_, dbg = vjp(dy_)
        return [dz, dge, dt_], [dbg]

    dproj, dge_a, dtg, g["s5_b_glu"] = _matmul(
        "mm_d_y5", dm5, w["w_br_s5"], tb=True, epilogue=s5_out_bwd_a,
        epi_rows=[(yc, 0), (proj, OFF_U5), (proj, OFF_Z5), (tg, 0)], epi_pars=[s5_d, b_glu],
        epi_outs=[BF16, F32, BF16], epi_accs=[(1, S5_WIDTH)], full_rows=True, epi_into=(dproj, OFF_Z5, S5_WIDTH))
    g["s5_w_glu"] = _matmul("mm_d_w_glu", ge, dtg, ta=True)
    early = w["_early"](g) if "_early" in w else None

    def s5_out_bwd_b(dge_b, r, q):
        yc_, u_, da_ = r
        _, vjp = jax.vjp(lambda yc, u, d: jax.nn.gelu(yc + d * u), yc_, u_, q[0])
        dyc_, du_, dd_ = vjp(da_ + dge_b)
        return [dyc_, du_], [dd_]

    dyc, du5_a, g["s5_d"] = _matmul(
        "mm_d_ge", dtg, w["s5_w_glu"], tb=True, epilogue=s5_out_bwd_b,
        epi_rows=[(yc, 0), (proj, OFF_U5), (dge_a, 0)], epi_pars=[s5_d], epi_outs=[BF16, F32],
        epi_accs=[(1, S5_WIDTH)], full_rows=True, dep=early)
    dproj, d_lam, d_bb_band, d_c_band = _s5_bwd(dyc, s, proj, du5_a, bb_band, c_band, lam_pow_conj, dproj,
                                                n_seq, seq_len)

    d_lr, d_li = _state_uncols(d_lam)
    d_bbr, d_bbi = _band_take(d_bb_band)
    d_are, d_aim, d_ls, d_br2, d_bi2 = _s5_params_bwd(
        a_re, a_im, log_step, b_re2, b_im2, expand,
        d_lr.reshape(S5_GROUPS, S5_STATE), d_li.reshape(S5_GROUPS, S5_STATE), d_bbr, d_bbi)
    g["s5_a_re"], g["s5_a_im"], g["s5_log_step"] = d_are, d_aim, d_ls
    g["s5_b_re_ghp"], g["s5_b_im_ghp"] = d_br2, d_bi2
    d_cr, d_ci = _band_take(d_c_band)
    g["s5_c_re"], g["s5_c_im"] = d_cr, -d_ci

    dxs, dbm, dcm, dproj, ddt, d_dtb, d_alog, d_dsk, g["ssd_norm_w"] = _ssd_bwd(
        xbc_act, proj, states, dyss, dtb, alog, dsk, ssd_norm_w, n_seq, seq_len, dproj)
    (dproj,) = row("ssd_ddt", lambda r, q: ([r[0]], []), row_ins=[(ddt, 0, DT_W)], par_ins=[],
                   row_outs=[(DT_W, BF16)], into=(dproj, OFF_DT))
    g["ssd_dt_bias"], g["ssd_a_log"], g["ssd_d"] = _unpad_heads(d_dtb), _unpad_heads(d_alog), _unpad_heads(d_dsk)
    conv_dw, conv_db = [], []
    for nm, d_act, off in (("x", dxs, 0), ("b", dbm, SSD_WIDTH), ("c", dcm, SSD_WIDTH + SSD_BC)):
        dproj, dw_, db_ = _conv_bwd("ssd_conv_bwd_" + nm, proj, d_act, conv_w, conv_b, n_rows, seq_len, off, dproj)
        conv_dw.append(dw_)
        conv_db.append(db_)
    g["ssd_conv_w"] = jnp.concatenate(conv_dw, axis=1)
    g["ssd_conv_b"] = jnp.concatenate(conv_db, axis=1)

    g["w_in_t"] = _unpad_w_in_t(_matmul("mm_d_w_in", dproj, hn, ta=True))
    def norm_bwd(dhn_, r, q):
        x_, dh1_ = r
        _, vjp = jax.vjp(_rms, x_, q[0])
        dx_, dw_ = vjp(dhn_)
        return [dx_ + dh1_], [dw_]

    sent = w["_w_in_ready"](g, loss) if "_w_in_ready" in w else None
    dx, g["norm_w"] = _matmul("mm_d_hn", dproj, w_pad_t, epilogue=norm_bwd, epi_rows=[(x2, 0), (dh1, 0)],
                              epi_pars=[norm_w], epi_outs=[F32], epi_accs=[(1, D_MODEL)], full_rows=True, dep=sent)
    return loss, dx.reshape(x.shape), g


HBM = pl.BlockSpec(memory_space=pltpu.HBM)


def _chip_index(x, y):
    return 2 * x + y


def _half(shape2d, axis, which):
    h = shape2d[axis] // 2
    sl = pl.ds(pl.multiple_of(which * h, 128 if axis else 8), h)
    return (slice(None), sl) if axis else (sl, slice(None))


def _gather_chips(split, axes, whole):
    ns, nw = len(split), len(whole)
    n = ns + nw

    def body(*refs):
        ins, outs = refs[:n], refs[n:2 * n]
        ici_send, ici_recv, d2d_send, d2d_recv, local_sems = refs[2 * n:]
        x, y, c = lax.axis_index("x"), lax.axis_index("y"), lax.axis_index("c")
        me = _chip_index(x, y)
        sibling = (x, y, 1 - c)
        peers = [(1 - x, y), (x, 1 - y), (1 - x, 1 - y)]

        def half(t, which):
            return _half(split[t].shape, axes[t], which)

        copies = []
        for t in range(n):
            loc = pltpu.make_async_copy(ins[t], outs[t].at[me], local_sems.at[t])
            loc.start()
            copies.append(loc)

        def ici(t, k, slot):
            px, py = peers[k]
            if t < ns:
                src, dst = ins[t].at[half(t, c)], outs[t].at[(slot,) + half(t, c)]
            else:
                src, dst = ins[t], outs[t].at[slot]
            return pltpu.make_async_remote_copy(src_ref=src, dst_ref=dst, send_sem=ici_send.at[t, k],
                                                recv_sem=ici_recv.at[t, k], device_id=(px, py, c), device_id_type=MESH)

        def d2d(t, k, which):
            rows = outs[t].at[(_chip_index(*peers[k]),) + half(t, which)]
            return pltpu.make_async_remote_copy(src_ref=rows, dst_ref=rows, send_sem=d2d_send.at[t, k],
                                                recv_sem=d2d_recv.at[t, k], device_id=sibling, device_id_type=MESH)

        sends = []
        for t in range(n):
            for k in range(3):
                cp = ici(t, k, me)
                cp.start()
                sends.append(cp)
        for t in range(n):
            for k in range(3):
                ici(t, k, _chip_index(*peers[k])).wait_recv()
                if t < ns:
                    cp = d2d(t, k, c)
                    cp.start()
                    sends.append(cp)
        for t in range(ns):
            for k in range(3):
                d2d(t, k, 1 - c).wait_recv()
        for cp in sends:
            cp.wait_send()
        for cp in copies:
            cp.wait()

    arrays = list(split) + list(whole)
    return pl.pallas_call(
        body, name="gather_weights",
        in_specs=[HBM] * n, out_specs=[HBM] * n,
        out_shape=[jax.ShapeDtypeStruct((N_CHIPS,) + a.shape, a.dtype) for a in arrays],
        scratch_shapes=[pltpu.SemaphoreType.DMA((n, 3)), pltpu.SemaphoreType.DMA((n, 3)),
                        pltpu.SemaphoreType.DMA((ns, 3)), pltpu.SemaphoreType.DMA((ns, 3)),
                        pltpu.SemaphoreType.DMA((n,))],
    )(*arrays)


SEM = pl.BlockSpec(memory_space=pltpu.SEMAPHORE)
DATAFLOW = pltpu.SideEffectType.DATAFLOW_SIDE_EFFECTING


def _gather_start(shards, after):
    n = len(shards)

    def body(*refs):
        ins, lands = refs[:n], refs[n:2 * n]
        send_sems, recv_sems = refs[2 * n + 1], refs[2 * n + 2]
        token = refs[-1]
        x, y, c = lax.axis_index("x"), lax.axis_index("y"), lax.axis_index("c")
        me = _chip_index(x, y)
        for t in range(n):
            for k, (px, py) in enumerate([(1 - x, y), (x, 1 - y), (1 - x, 1 - y)]):
                pltpu.make_async_remote_copy(
                    src_ref=ins[t], dst_ref=lands[t].at[me], send_sem=send_sems.at[3 * t + k],
                    recv_sem=recv_sems.at[3 * t + k],
                    device_id=(px, py, c), device_id_type=MESH).start()
        token[...] = jnp.zeros_like(token)

    zones = [lax.empty((N_CHIPS,) + a.shape, a.dtype) for a in shards]
    res = pl.pallas_call(
        body, name="gather_rest_start",
        out_shape=(pltpu.SemaphoreType.DMA((3 * n,)), pltpu.SemaphoreType.DMA((3 * n,)),
                   *[pltpu.HBM(a.shape, a.dtype) for a in shards], *[pltpu.HBM(z.shape, z.dtype) for z in zones],
                   jax.ShapeDtypeStruct((8, 128), F32)),
        in_specs=[HBM] * (2 * n) + [pl.BlockSpec(memory_space=pl.ANY)],
        out_specs=(SEM, SEM, *[HBM] * (2 * n), pl.BlockSpec(memory_space=pltpu.VMEM)),
        input_output_aliases={t: 2 + t for t in range(2 * n)},
        compiler_params=pltpu.CompilerParams(has_side_effects=DATAFLOW),
    )(*[pltpu.with_memory_space_constraint(a, pltpu.HBM) for a in shards],
      *[pltpu.with_memory_space_constraint(z, pltpu.HBM) for z in zones], after)
    return res[0], res[1], list(res[2:2 + n]), list(res[2 + n:2 + 2 * n]), res[-1]


def _gather_wait(send_sems, recv_sems, thru, lands, after):
    n = len(thru)

    def body(*refs):
        ins, zones = refs[:n], refs[n:2 * n]
        s_sems, r_sems = refs[2 * n], refs[2 * n + 1]
        x, y, c = lax.axis_index("x"), lax.axis_index("y"), lax.axis_index("c")
        for t in range(n):
            for k, (px, py) in enumerate([(1 - x, y), (x, 1 - y), (1 - x, 1 - y)]):
                cp = pltpu.make_async_remote_copy(
                    src_ref=ins[t], dst_ref=zones[t].at[_chip_index(px, py)], send_sem=s_sems.at[3 * t + k],
                    recv_sem=r_sems.at[3 * t + k], device_id=(px, py, c), device_id_type=MESH)
                cp.wait_send()
                cp.wait_recv()

    res = pl.pallas_call(
        body, name="gather_rest_wait",
        out_shape=(*[pltpu.HBM(a.shape, a.dtype) for a in thru], *[pltpu.HBM(z.shape, z.dtype) for z in lands]),
        in_specs=[HBM] * (2 * n) + [SEM, SEM, pl.BlockSpec(memory_space=pl.ANY)], out_specs=[HBM] * (2 * n),
        input_output_aliases={t: t for t in range(2 * n)},
        compiler_params=pltpu.CompilerParams(has_side_effects=DATAFLOW),
    )(*thru, *lands, send_sems, recv_sems, after)
    return list(res[:n]), list(res[n:])


def _scatter_ends(t, k, n_slotted, ins, zones, x, y, me):
    px, py = [(1 - x, y), (x, 1 - y), (1 - x, 1 - y)][k]
    if t < n_slotted:
        return ins[t].at[_chip_index(px, py)], zones[t].at[k], (px, py)
    return ins[t], (zones[t].at[me], zones[t].at[_chip_index(px, py)]), (px, py)


def _scatter_start(name, slotted, whole, after):
    n_s = len(slotted)
    arrays = list(slotted) + list(whole)
    n = len(arrays)

    def body(*refs):
        ins, lands = refs[:n], refs[n:2 * n]
        send_sems, recv_sems = refs[2 * n + 1], refs[2 * n + 2]
        token = refs[-1]
        x, y, c = lax.axis_index("x"), lax.axis_index("y"), lax.axis_index("c")
        for t in range(n):
            for k in range(3):
                src, dst, (px, py) = _scatter_ends(t, k, n_s, ins, lands, x, y, _chip_index(x, y))
                pltpu.make_async_remote_copy(
                    src_ref=src, dst_ref=dst if t < n_s else dst[0], send_sem=send_sems.at[3 * t + k],
                    recv_sem=recv_sems.at[3 * t + k], device_id=(px, py, c), device_id_type=MESH).start()
        token[...] = jnp.zeros_like(token)

    zones = [lax.empty((3,) + a.shape[1:], a.dtype) for a in slotted]
    zones += [lax.empty((N_CHIPS,) + a.shape, a.dtype) for a in whole]
    slotted = arrays
    res = pl.pallas_call(
        body, name=name,
        out_shape=(pltpu.SemaphoreType.DMA((3 * n,)), pltpu.SemaphoreType.DMA((3 * n,)),
                   *[pltpu.HBM(a.shape, a.dtype) for a in slotted], *[pltpu.HBM(z.shape, z.dtype) for z in zones],
                   jax.ShapeDtypeStruct((8, 128), F32)),
        in_specs=[HBM] * (2 * n) + [pl.BlockSpec(memory_space=pl.ANY)],
        out_specs=(SEM, SEM, *[HBM] * (2 * n), pl.BlockSpec(memory_space=pltpu.VMEM)),
        input_output_aliases={t: 2 + t for t in range(2 * n)},
        compiler_params=pltpu.CompilerParams(has_side_effects=DATAFLOW),
    )(*[pltpu.with_memory_space_constraint(a, pltpu.HBM) for a in slotted],
      *[pltpu.with_memory_space_constraint(z, pltpu.HBM) for z in zones], after)
    return res[0], res[1], list(res[2:2 + n]), list(res[2 + n:2 + 2 * n]), res[-1]


def _scatter_wait(name, n_slotted, send_sems, recv_sems, thru, lands, after):
    n = len(thru)

    def body(*refs):
        ins, zones = refs[:n], refs[n:2 * n]
        s_sems, r_sems = refs[2 * n], refs[2 * n + 1]
        x, y, c = lax.axis_index("x"), lax.axis_index("y"), lax.axis_index("c")
        for t in range(n):
            for k in range(3):
                src, dst, (px, py) = _scatter_ends(t, k, n_slotted, ins, zones, x, y, _chip_index(x, y))
                cp = pltpu.make_async_remote_copy(
                    src_ref=src, dst_ref=dst if t < n_slotted else dst[1], send_sem=s_sems.at[3 * t + k],
                    recv_sem=r_sems.at[3 * t + k], device_id=(px, py, c), device_id_type=MESH)
                cp.wait_send()
                cp.wait_recv()

    res = pl.pallas_call(
        body, name=name,
        out_shape=(*[pltpu.HBM(a.shape, a.dtype) for a in thru], *[pltpu.HBM(z.shape, z.dtype) for z in lands]),
        in_specs=[HBM] * (2 * n) + [SEM, SEM, pl.BlockSpec(memory_space=pl.ANY)], out_specs=[HBM] * (2 * n),
        input_output_aliases={t: t for t in range(2 * n)},
        compiler_params=pltpu.CompilerParams(has_side_effects=DATAFLOW),
    )(*thru, *lands, send_sems, recv_sems, after)
    return list(res[:n]), list(res[n:])


def _sum_chips(name, slotted, recv, idx, dep):
    _, r, c = slotted.shape
    tr = _row_tile(r, 5 * c * 4)

    def body(idx_ref, own_ref, r_ref, _, o_ref):
        acc = own_ref[0]
        for k in range(3):
            acc = acc + r_ref[k]
        o_ref[...] = acc

    grid_spec = pltpu.PrefetchScalarGridSpec(
        num_scalar_prefetch=1, grid=(r // tr,),
        in_specs=[pl.BlockSpec((1, tr, c), lambda i, s: (s[1], i, 0)), pl.BlockSpec((3, tr, c), lambda i, s: (0, i, 0)),
                  pl.BlockSpec(memory_space=pl.ANY)],
        out_specs=pl.BlockSpec((tr, c), lambda i, s: (i, 0)))
    return pl.pallas_call(
        body, name=name, grid_spec=grid_spec, out_shape=jax.ShapeDtypeStruct((r, c), F32),
        compiler_params=_cparams(("parallel",)),
    )(idx, slotted, recv, dep)


def _half_shape(shape2d, axis):
    r, c = shape2d
    return (r, c // 2) if axis else (r // 2, c)


def _swap_halves(slotted, axes, small):
    n = len(slotted)

    def body(*refs):
        ins, sm_in = refs[:n], refs[n]
        outs, sm_out = refs[n + 1:2 * n + 1], refs[2 * n + 1]
        send_sems, recv_sems, local_sem = refs[2 * n + 2:]
        x, y, c = lax.axis_index("x"), lax.axis_index("y"), lax.axis_index("c")
        local = pltpu.make_async_copy(sm_in, sm_out.at[c], local_sem)
        local.start()
        sends = []
        for t in range(n):
            other = (slice(None),) + _half(slotted[t].shape[1:], axes[t], 1 - c)
            cp = pltpu.make_async_remote_copy(
                src_ref=ins[t].at[other], dst_ref=outs[t], send_sem=send_sems.at[t],
                recv_sem=recv_sems.at[t], device_id=(x, y, 1 - c), device_id_type=MESH)
            cp.start()
            sends.append(cp)
        cp = pltpu.make_async_remote_copy(
            src_ref=sm_in, dst_ref=sm_out.at[c], send_sem=send_sems.at[n], recv_sem=recv_sems.at[n],
            device_id=(x, y, 1 - c), device_id_type=MESH)
        cp.start()
        sends.append(cp)
        for cp in sends[:n]:
            cp.wait_recv()
        pltpu.make_async_remote_copy(
            src_ref=sm_in, dst_ref=sm_out.at[1 - c], send_sem=send_sems.at[n], recv_sem=recv_sems.at[n],
            device_id=(x, y, 1 - c), device_id_type=MESH).wait_recv()
        for cp in sends:
            cp.wait_send()
        local.wait()

    return pl.pallas_call(
        body, name="swap_halves",
        in_specs=[HBM] * (n + 1), out_specs=[HBM] * (n + 1),
        out_shape=[jax.ShapeDtypeStruct((a.shape[0],) + _half_shape(a.shape[1:], ax), a.dtype)
                   for a, ax in zip(slotted, axes, strict=True)]
        + [jax.ShapeDtypeStruct((2,) + small.shape, small.dtype)],
        scratch_shapes=[pltpu.SemaphoreType.DMA((n + 1,)), pltpu.SemaphoreType.DMA((n + 1,)), pltpu.SemaphoreType.DMA],
    )(*slotted, small)


def _tiling(r, c, f32_per_elem):
    if r % 8 == 0:
        tr = _row_tile(r, f32_per_elem * c * 4)
        return r // tr, (tr, c), lambda i: (i, 0)
    assert c % 128 == 0, (r, c)
    return c // 128, (r, 128), lambda i: (0, i)


def _pair_sum(name, slotted, other, idx, axis):
    _, r, c = slotted.shape
    hr, hc = _half_shape((r, c), axis)
    n, (tr, tc), at = _tiling(hr, hc, 13)
    if axis == 0:
        a = slotted.reshape(N_CHIPS, 2, hr, c)
        a_all = pl.BlockSpec((N_CHIPS, 1, tr, tc), lambda i, s: (0, s[0]) + at(i))
        a_own = pl.BlockSpec((1, 1, tr, tc), lambda i, s: (s[1], s[0]) + at(i))
    else:
        a, per_half = slotted, hc // tc
        a_all = pl.BlockSpec((N_CHIPS, tr, tc), lambda i, s: (0, at(i)[0], s[0] * per_half + at(i)[1]))
        a_own = pl.BlockSpec((1, tr, tc), lambda i, s: (s[1], at(i)[0], s[0] * per_half + at(i)[1]))

    def body(idx_ref, a_ref, b_ref, am_ref, bm_ref, p_ref, own_ref):
        mine, mine_own = (a_ref[:, 0], am_ref[0, 0]) if axis == 0 else (a_ref[...], am_ref[0])
        p_ref[...] = (mine + b_ref[...]).astype(p_ref.dtype)
        own_ref[...] = mine_own + bm_ref[0]

    grid_spec = pltpu.PrefetchScalarGridSpec(
        num_scalar_prefetch=1, grid=(n,),
        in_specs=[a_all, pl.BlockSpec((N_CHIPS, tr, tc), lambda i, s: (0,) + at(i)),
                  a_own, pl.BlockSpec((1, tr, tc), lambda i, s: (s[1],) + at(i))],
        out_specs=[pl.BlockSpec((N_CHIPS, tr, tc), lambda i, s: (0,) + at(i)),
                   pl.BlockSpec((tr, tc), lambda i, s: at(i))])
    return pl.pallas_call(
        body, name=name, grid_spec=grid_spec,
        out_shape=[jax.ShapeDtypeStruct((N_CHIPS, hr, hc), BF16), jax.ShapeDtypeStruct((hr, hc), F32)],
        compiler_params=_cparams(("parallel",)),
    )(idx, a, other, a, other)


def _sum_parts(name, own, recv):
    h, c = own.shape
    n, (tr, tc), at = _tiling(h, c, 4)

    def body(o_ref, r_ref, out_ref):
        acc = o_ref[...]
        for k in range(3):
            acc = acc + r_ref[k].astype(F32)
        out_ref[...] = acc

    return pl.pallas_call(
        body, name=name, grid=(n,),
        in_specs=[pl.BlockSpec((tr, tc), at), pl.BlockSpec((3, tr, tc), lambda i: (0,) + at(i))],
        out_specs=pl.BlockSpec((tr, tc), at),
        out_shape=jax.ShapeDtypeStruct((h, c), F32),
        compiler_params=_cparams(("parallel",)),
    )(own, recv)


def _all_to_all(name, small):
    def body(sm_in, sm_out, send_sems, recv_sems, local_sem):
        x, y, c = lax.axis_index("x"), lax.axis_index("y"), lax.axis_index("c")
        dev = 4 * x + 2 * y + c
        local = pltpu.make_async_copy(sm_in, sm_out.at[dev], local_sem)
        local.start()
        rel = [(fx, fy, fc) for fx in (0, 1) for fy in (0, 1) for fc in (0, 1)][1:]
        sends = []
        for k, (fx, fy, fc) in enumerate(rel):
            cp = pltpu.make_async_remote_copy(
                src_ref=sm_in, dst_ref=sm_out.at[dev], send_sem=send_sems.at[k], recv_sem=recv_sems.at[k],
                device_id=(x ^ fx, y ^ fy, c ^ fc), device_id_type=MESH)
            cp.start()
            sends.append(cp)
        for k, (fx, fy, fc) in enumerate(rel):
            src_dev = 4 * (x ^ fx) + 2 * (y ^ fy) + (c ^ fc)
            pltpu.make_async_remote_copy(
                src_ref=sm_in, dst_ref=sm_out.at[src_dev], send_sem=send_sems.at[k], recv_sem=recv_sems.at[k],
                device_id=(x ^ fx, y ^ fy, c ^ fc), device_id_type=MESH).wait_recv()
        for cp in sends:
            cp.wait_send()
        local.wait()

    return pl.pallas_call(
        body, name=name, in_specs=[HBM], out_specs=HBM,
        out_shape=jax.ShapeDtypeStruct((N_DEV,) + small.shape, small.dtype),
        scratch_shapes=[pltpu.SemaphoreType.DMA((7,)), pltpu.SemaphoreType.DMA((7,)), pltpu.SemaphoreType.DMA],
    )(small)


def _swap_sibling(name, parts):
    n = len(parts)

    def body(*refs):
        ins, outs = refs[:n], refs[n:2 * n]
        send_sems, recv_sems = refs[2 * n:]
        x, y, c = lax.axis_index("x"), lax.axis_index("y"), lax.axis_index("c")
        cps = []
        for t in range(n):
            cp = pltpu.make_async_remote_copy(
                src_ref=ins[t], dst_ref=outs[t], send_sem=send_sems.at[t], recv_sem=recv_sems.at[t],
                device_id=(x, y, 1 - c), device_id_type=MESH)
            cp.start()
            cps.append(cp)
        for cp in cps:
            cp.wait_recv()
        for cp in cps:
            cp.wait_send()

    return pl.pallas_call(
        body, name=name,
        in_specs=[HBM] * n, out_specs=[HBM] * n,
        out_shape=[jax.ShapeDtypeStruct(a.shape, a.dtype) for a in parts],
        scratch_shapes=[pltpu.SemaphoreType.DMA((n,)), pltpu.SemaphoreType.DMA((n,))],
    )(*parts)


def _sum_slots(name, a):
    k, r, c = a.shape
    tr = _row_tile(r, (k + 1) * c * 4)

    def body(a_ref, o_ref):
        acc = a_ref[0]
        for i in range(1, k):
            acc = acc + a_ref[i]
        o_ref[...] = acc

    return pl.pallas_call(
        body, name=name, grid=(r // tr,),
        in_specs=[pl.BlockSpec((k, tr, c), lambda i: (0, i, 0))],
        out_specs=pl.BlockSpec((tr, c), lambda i: (i, 0)),
        out_shape=jax.ShapeDtypeStruct((r, c), a.dtype),
        compiler_params=_cparams(("parallel",)),
    )(a)


def _adamw(name, w, m, v, g_parts):
    r, c = w.shape
    ng = len(g_parts)
    tr = _row_tile(r, (7 + ng) * c * 4)
    c1 = 1.0 - ADAM_B1 ** ADAM_STEP
    c2 = 1.0 - ADAM_B2 ** ADAM_STEP

    def body(*refs):
        w_ref, m_ref, v_ref = refs[:3]
        g_refs = refs[3:3 + ng]
        go_ref, d_ref, mo_ref, vo_ref = refs[3 + ng:]
        g = g_refs[0][...]
        for gr in g_refs[1:]:
            g = g + gr[...]
        m_new = ADAM_B1 * m_ref[...] + (1.0 - ADAM_B1) * g
        v_new = ADAM_B2 * v_ref[...] + (1.0 - ADAM_B2) * (g * g)
        go_ref[...] = g
        mo_ref[...] = m_new
        vo_ref[...] = v_new
        d_ref[...] = -ADAM_LR * ((m_new / c1) / (jnp.sqrt(v_new / c2) + ADAM_EPS) + ADAM_WD * w_ref[...])

    spec = pl.BlockSpec((tr, c), lambda i: (i, 0))
    return pl.pallas_call(
        body, name=name, grid=(r // tr,),
        in_specs=[spec] * (3 + ng), out_specs=[spec] * 4,
        out_shape=[jax.ShapeDtypeStruct((r, c), F32)] * 4,
        compiler_params=_cparams(("parallel",)),
    )(w, m, v, *g_parts)


def _adamw_halves(name, w, m, v, own, other, idx, axis):
    hr, hc = own.shape
    nb, (tr, tc), at = _tiling(hr, hc, 9)
    c1 = 1.0 - ADAM_B1 ** ADAM_STEP
    c2 = 1.0 - ADAM_B2 ** ADAM_STEP

    def body(idx_ref, w_ref, m_ref, v_ref, own_ref, oth_ref, go_ref, d_ref, mo_ref, vo_ref):
        g = jnp.where(pl.program_id(0) == idx_ref[0], own_ref[...], oth_ref[...])
        m_new = ADAM_B1 * m_ref[...] + (1.0 - ADAM_B1) * g
        v_new = ADAM_B2 * v_ref[...] + (1.0 - ADAM_B2) * (g * g)
        go_ref[...] = g
        mo_ref[...] = m_new
        vo_ref[...] = v_new
        d_ref[...] = -ADAM_LR * ((m_new / c1) / (jnp.sqrt(v_new / c2) + ADAM_EPS) + ADAM_WD * w_ref[...])

    per_half = (hc // tc) if axis else (hr // tr)
    if axis:
        full = pl.BlockSpec((tr, tc), lambda hh, i, s: (at(i)[0], hh * per_half + at(i)[1]))
    else:
        full = pl.BlockSpec((tr, tc), lambda hh, i, s: (hh * per_half + at(i)[0], at(i)[1]))
    part = pl.BlockSpec((tr, tc), lambda hh, i, s: at(i))
    grid_spec = pltpu.PrefetchScalarGridSpec(
        num_scalar_prefetch=1, grid=(2, nb), in_specs=[full, full, full, part, part], out_specs=[full] * 4)
    return pl.pallas_call(
        body, name=name, grid_spec=grid_spec, out_shape=[jax.ShapeDtypeStruct(w.shape, F32)] * 4,
        compiler_params=_cparams(("parallel", "parallel")),
    )(idx, w, m, v, own, other)


WEIGHTS = ['norm_w', 'w_in', 's5_a_re', 's5_a_im', 's5_b_re', 's5_b_im', 's5_c_re', 's5_c_im', 's5_d', 's5_log_step',
           's5_w_glu', 's5_b_glu', 'ssd_conv_w', 'ssd_conv_b', 'ssd_dt_bias', 'ssd_a_log', 'ssd_d', 'ssd_norm_w',
           'w_br_s5', 'w_br_ssd', 'w_out', 'ple_norm_w', 'w_ple_gate', 'w_ple_proj', 'final_norm_w']
SHARDED = {'w_in': ((IN_PROJ_DIM, 1024), 0), 's5_w_glu': ((512, 512), 0), 'ssd_conv_w': ((SSD_CONV, SSD_CONV_DIM), 1),
           'w_br_s5': ((512, 1024), 1), 'w_br_ssd': ((1536, 1024), 0), 'w_out': ((1024, 1024), 0),
           'w_ple_gate': ((1024, 1024), 0), 'w_ple_proj': ((256, 1024), 1)}
TRANSPOSED = ('w_in',)
SMALL = [n for n in WEIGHTS if n not in SHARDED]


def _shard_shape(name):
    (r, c), ax = SHARDED[name]
    return (r // N_CHIPS, c) if ax == 0 else (r, c // N_CHIPS)


def _half_axis(name):
    return 0 if (_shard_shape(name)[0] // 2) % 16 == 0 else 1


def _shard2d(name, a):
    r, c = _shard_shape(name)
    return a.reshape(c, r).T if name in TRANSPOSED else a.reshape(r, c)


def _unshard2d(name, a2, shape):
    return (a2.T if name in TRANSPOSED else a2).reshape(shape)


def _unslot(name, a4):
    (r, c), ax = SHARDED[name]
    if ax == 0:
        return a4.reshape(r, c)
    return jnp.transpose(a4, (1, 0, 2)).reshape(r, c)


def _slot(name, full):
    (r, c), ax = SHARDED[name]
    if ax == 0:
        return full.reshape(N_CHIPS, r // N_CHIPS, c)
    return jnp.transpose(full.reshape(r, N_CHIPS, c // N_CHIPS), (1, 0, 2))


GHP = ('s5_b_re', 's5_b_im')


def _view_shape(name):
    if name in ('s5_a_re', 's5_a_im'):
        return (S5_GROUPS, S5_STATE)
    if name in GHP + ('s5_c_re', 's5_c_im'):
        return (S5_GROUPS, S5_GROUP, S5_STATE)
    if name == 'ssd_conv_w':
        return (SSD_CONV, SSD_CONV_DIM // N_CHIPS)
    return (1, {'s5_log_step': S5_GROUPS, 'ssd_conv_b': SSD_CONV_DIM, 'ssd_norm_w': SSD_WIDTH, 's5_d': S5_WIDTH,
                's5_b_glu': S5_WIDTH, 'ssd_dt_bias': SSD_HEADS, 'ssd_a_log': SSD_HEADS, 'ssd_d': SSD_HEADS}.get(name, D_MODEL))


def _view(name, a):
    if name in GHP:
        return jnp.swapaxes(a.reshape(S5_GROUPS, S5_STATE, S5_GROUP), 1, 2)
    return a.reshape(_view_shape(name))


def _unview(name, a, shape):
    return (jnp.swapaxes(a, 1, 2) if name in GHP else a).reshape(shape)


def _adamw_small(ws, ms, vs, gs):
    n = len(ws)
    c1 = 1.0 - ADAM_B1 ** ADAM_STEP
    c2 = 1.0 - ADAM_B2 ** ADAM_STEP

    def body(*refs):
        w_r, m_r, v_r, g_r = (refs[k * n:(k + 1) * n] for k in range(4))
        d_o, m_o, v_o = (refs[k * n:(k + 1) * n] for k in range(4, 7))
        for i in range(n):
            g = g_r[i][...]
            m_new = ADAM_B1 * m_r[i][...] + (1.0 - ADAM_B1) * g
            v_new = ADAM_B2 * v_r[i][...] + (1.0 - ADAM_B2) * (g * g)
            m_o[i][...] = m_new
            v_o[i][...] = v_new
            d_o[i][...] = -ADAM_LR * ((m_new / c1) / (jnp.sqrt(v_new / c2) + ADAM_EPS) + ADAM_WD * w_r[i][...])

    return pl.pallas_call(
        body, name="adamw_small", out_shape=[jax.ShapeDtypeStruct(w.shape, F32) for w in ws] * 3,
        compiler_params=pltpu.CompilerParams(vmem_limit_bytes=VMEM_LIMIT),
    )(*ws, *ms, *vs, *gs)


def _pack_small(vals):
    flat = jnp.concatenate([v.reshape(-1).astype(F32) for v in vals])
    rows = -(-flat.shape[0] // (256 * 128)) * 256
    return jnp.pad(flat, (0, rows * 128 - flat.shape[0])).reshape(rows, 128)


def _unpack_small(packed, shapes):
    flat = packed.reshape(-1)
    out, off = [], 0
    for sh in shapes:
        n = math.prod(sh)
        out.append(flat[off:off + n].reshape(sh))
        off += n
    return out


def kernel(x, p, norm_w, w_in, s5_a_re, s5_a_im, s5_b_re, s5_b_im, s5_c_re, s5_c_im, s5_d, s5_log_step, s5_w_glu, s5_b_glu, ssd_conv_w, ssd_conv_b, ssd_dt_bias, ssd_a_log, ssd_d, ssd_norm_w, w_br_s5, w_br_ssd, w_out, ple_norm_w, w_ple_gate, w_ple_proj, final_norm_w, loss_target, m_norm_w, m_w_in, m_s5_a_re, m_s5_a_im, m_s5_b_re, m_s5_b_im, m_s5_c_re, m_s5_c_im, m_s5_d, m_s5_log_step, m_s5_w_glu, m_s5_b_glu, m_ssd_conv_w, m_ssd_conv_b, m_ssd_dt_bias, m_ssd_a_log, m_ssd_d, m_ssd_norm_w, m_w_br_s5, m_w_br_ssd, m_w_out, m_ple_norm_w, m_w_ple_gate, m_w_ple_proj, m_final_norm_w, v_norm_w, v_w_in, v_s5_a_re, v_s5_a_im, v_s5_b_re, v_s5_b_im, v_s5_c_re, v_s5_c_im, v_s5_d, v_s5_log_step, v_s5_w_glu, v_s5_b_glu, v_ssd_conv_w, v_ssd_conv_b, v_ssd_dt_bias, v_ssd_a_log, v_ssd_d, v_ssd_norm_w, v_w_br_s5, v_w_br_ssd, v_w_out, v_ple_norm_w, v_w_ple_gate, v_w_ple_proj, v_final_norm_w):
    args = locals()
    wl = {n: args[n] for n in WEIGHTS}
    ml = {n: args["m_" + n] for n in WEIGHTS}
    vl = {n: args["v_" + n] for n in WEIGHTS}
    big = [n for n in SHARDED if n != 'ssd_conv_w']
    chip = _chip_index(lax.axis_index("x"), lax.axis_index("y"))
    idx = jnp.stack([lax.axis_index("c"), chip]).astype(jnp.int32)

    axes = [_half_axis(n) for n in big]
    first = ['w_in']
    rest = [n for n in big if n not in first]
    bf_shards = {n: _shard2d(n, wl[n]).astype(BF16) for n in big}
    w_in_t, conv_w4 = _gather_chips([bf_shards[n] for n in first], [_half_axis(n) for n in first],
                                    [_shard2d('ssd_conv_w', wl['ssd_conv_w'])])
    full = {n: wl[n] for n in SMALL}
    full["w_in_t"] = w_in_t
    full['ssd_conv_w'] = _unslot('ssd_conv_w', conv_w4)
    send_sems, recv_sems, thru, lands, token = _gather_start([bf_shards[n] for n in rest], w_in_t)

    def fetch_rest(after):
        own, zones = _gather_wait(send_sems, recv_sems, thru, lands, after)
        return {n: _unslot(n, lax.dynamic_update_slice(z, o[None], (chip, 0, 0)))
                for n, o, z in zip(rest, own, zones, strict=True)}

    full["_late"] = (token, fetch_rest)

    in_flight = []

    def send_rest(g_now):
        in_flight.extend(_scatter_start("scatter_rest_start", [_slot(n, g_now[n]) for n in rest], [], g_now['s5_w_glu']))
        return in_flight[4]

    full["_early"] = send_rest

    packed = [n for n in SMALL if n != 'norm_w']
    first_axes = [_half_axis(n) for n in first]
    st = {}

    def send_first(g_now, loss_now):
        small_pack = _pack_small([loss_now] + [g_now[n + "_ghp" if n in GHP else n] for n in packed]
                                 + [g_now['ssd_conv_w']])
        swapped = _swap_halves([g_now["w_in_t"]], first_axes, small_pack)
        st["pair"] = [_pair_sum("pair_sum_w_in", g_now["w_in_t"], swapped[0], idx, first_axes[0])]
        small_chip = _sum_slots("sum_small_pair", swapped[-1])
        half_rows = small_chip.shape[0] // 2
        my_half = lax.dynamic_slice(small_chip, (lax.axis_index("c") * half_rows, 0), (half_rows, small_chip.shape[1]))
        st["flight"] = _scatter_start("scatter_first_start", [pb for pb, _ in st["pair"]], [my_half], small_chip)
        return st["flight"][4]

    full["_w_in_ready"] = send_first
    loss, grad_x, g = _local_step(x, p[0], loss_target, full)
    pair = st["pair"]
    s_sems, r_sems, thru, lands, tok = st["flight"]
    small_shapes = [(1, 1)] + [_view_shape(n) for n in packed] + [(SSD_CONV, SSD_CONV_DIM)]
    norm_all = _all_to_all("norm_w_all_to_all", g['norm_w'].reshape(8, 128))
    norm_g = _sum_slots("sum_norm_w", norm_all).reshape(1, D_MODEL)

    out_g, out_d, out_m, out_v = {}, {}, {}, {}
    own_slots, zones = _scatter_wait("scatter_rest_wait", len(rest), *in_flight[:4], norm_g)
    chip_sums = [_sum_chips("sum_chips_" + n, a, z, idx, tok) for n, a, z in zip(rest, own_slots, zones, strict=True)]
    sib_sums = _swap_sibling("swap_sibling_rest", chip_sums)
    for n, own, sib in zip(rest, chip_sums, sib_sums, strict=True):
        res = _adamw("adamw_" + n, _shard2d(n, wl[n]), _shard2d(n, ml[n]), _shard2d(n, vl[n]), [own, sib])
        out_g[n], out_d[n], out_m[n], out_v[n] = (_unshard2d(n, r, wl[n].shape) for r in res)

    sent, got = _scatter_wait("scatter_first_wait", len(first), s_sems, r_sems, thru, lands, out_d[rest[-1]])
    small_zone = lax.dynamic_update_slice(got[-1], sent[-1][None], (chip, 0, 0))
    small_half = _sum_slots("sum_small_chips", small_zone)
    halves = [_sum_parts("sum_chips_" + n, own, r) for n, (_, own), r in zip(first, pair, got[:-1], strict=True)]
    swapped2 = _swap_sibling("swap_sibling_first", halves + [small_half])
    other_halves = swapped2[:-1]
    c_is_0 = lax.axis_index("c") == 0
    small_sum = jnp.concatenate([jnp.where(c_is_0, small_half, swapped2[-1]),
                                 jnp.where(c_is_0, swapped2[-1], small_half)], axis=0)
    for n, own, oth, ax in zip(first, halves, other_halves, first_axes, strict=True):
        res = _adamw_halves("adamw_" + n, _shard2d(n, wl[n]), _shard2d(n, ml[n]), _shard2d(n, vl[n]), own, oth, idx, ax)
        out_g[n], out_d[n], out_m[n], out_v[n] = (_unshard2d(n, r, wl[n].shape) for r in res)
    sm = _unpack_small(small_sum, small_shapes)
    loss_total = sm[0].reshape(())
    conv_g = lax.dynamic_slice(sm[-1], (0, chip * (SSD_CONV_DIM // N_CHIPS)), (SSD_CONV, SSD_CONV_DIM // N_CHIPS))
    names = SMALL + ['ssd_conv_w']
    by_name = {**dict(zip(packed, sm[1:-1], strict=True)), 'norm_w': norm_g, 'ssd_conv_w': conv_g}
    grads = [by_name[n] for n in names]
    res = _adamw_small([_view(n, wl[n]) for n in names], [_view(n, ml[n]) for n in names],
                       [_view(n, vl[n]) for n in names], grads)
    for i, n in enumerate(names):
        out_g[n] = _unview(n, grads[i], wl[n].shape)
        out_d[n], out_m[n], out_v[n] = (_unview(n, res[k * len(names) + i], wl[n].shape) for k in range(3))

    return (loss_total, grad_x, *[out_g[n] for n in WEIGHTS], *[out_d[n] for n in WEIGHTS],
            *[out_m[n] for n in WEIGHTS], *[out_v[n] for n in WEIGHTS])
```

```python
import functools
import math

import jax
import jax.numpy as jnp
from jax import lax
from jax.experimental import pallas as pl
from jax.experimental.pallas import tpu as pltpu

F32 = jnp.float32
BF16 = jnp.bfloat16
MESH = pl.DeviceIdType.MESH

D_MODEL = 1024
PLE_DIM = 256
RMS_EPS = 1e-6
S5_WIDTH = 512
S5_GROUP = 16
S5_GROUPS = 32
S5_STATE = 64
S5_N = S5_GROUPS * S5_STATE
S5_LB = 512
S5_NJ = S5_N // S5_LB
S5_TB = 512
S5_LOG_TB = 8
SSD_WIDTH = 1536
SSD_HEADDIM = 64
SSD_HEADS = 24
SSD_GROUPS = 4
SSD_HPG = 6
SSD_STATE = 128
SSD_CONV = 4
SSD_CHUNK = 128
SSD_BC = 512
SSD_CONV_DIM = 2560
GROUP_W = SSD_WIDTH // SSD_GROUPS
N_CHIPS = 4
N_DEV = 8

OFF_XBC, OFF_U5, OFF_Z5, OFF_DT, OFF_G5, OFF_GS, OFF_ZS = 0, 2560, 3072, 3584, 4096, 5120, 6144
DT_W = 512
PROJ_W = 7680
IN_PROJ_DIM = 7192

ADAM_LR, ADAM_B1, ADAM_B2, ADAM_EPS, ADAM_WD, ADAM_STEP = 0.001, 0.9, 0.999, 1e-08, 0.01, 10

VMEM_LIMIT = 56 * 1024 * 1024


ROW_BLOCK_BYTES = 8 * 1024 * 1024


def _row_tile(r, bytes_per_row):
    for t in (r, 4096, 2048, 1024, 512, 256, 128, 64, 32, 16, 8):
        if t <= r and r % t == 0 and t * bytes_per_row <= ROW_BLOCK_BYTES:
            return t
    return r


def _cparams(sem):
    return pltpu.CompilerParams(dimension_semantics=sem, vmem_limit_bytes=VMEM_LIMIT)


def _dg(a, b, ca, cb):
    return lax.dot_general(a.astype(BF16), b.astype(BF16), (((ca,), (cb,)), ((), ())), preferred_element_type=F32)


@jax.custom_vjp
def dot_nn(a, b):
    return _dg(a, b, 1, 0)


@jax.custom_vjp
def dot_nt(a, b):
    return _dg(a, b, 1, 1)


@jax.custom_vjp
def dot_tn(a, b):
    return _dg(a, b, 0, 0)


dot_nn.defvjp(lambda a, b: (_dg(a, b, 1, 0), (a, b)), lambda r, g: (_dg(g, r[1], 1, 1), _dg(r[0], g, 0, 0)))
dot_nt.defvjp(lambda a, b: (_dg(a, b, 1, 1), (a, b)), lambda r, g: (_dg(g, r[1], 1, 0), _dg(g, r[0], 0, 0)))
dot_tn.defvjp(lambda a, b: (_dg(a, b, 0, 0), (a, b)), lambda r, g: (_dg(r[1], g, 1, 1), _dg(r[0], g, 1, 0)))


MM_VMEM_BUDGET = 40 * 1024 * 1024


def _mm_tiles(m, n, k, sa, sb, so, tn_only=None):
    best, best_key = None, None
    for tm in (1024, 512, 256, 128, 64, 32, 16, 8):
        if m % tm:
            continue
        for tn in (2048, 1536, 1280, 1024, 768, 640, 512, 384, 256, 128):
            if n % tn or (tn_only is not None and tn not in tn_only):
                continue
            for tk in (k, 2048, 1536, 1280, 1024, 768, 512, 256, 128):
                if k % tk or tk > max(k, 128):
                    continue
                need = 2 * (tm * tk * sa + tk * tn * sb + tm * tn * so) + (tm * tn * 4 if tk < k else 0)
                if need > MM_VMEM_BUDGET:
                    continue
                key = (tm * tn * tk, tk)
                if best_key is None or key > best_key:
                    best, best_key = (tm, tn, tk), key
    assert best is not None, (m, n, k)
    return best


def _matmul(name, a, b, *, ta=False, tb=False, a_win=None, out_dtype=F32, epilogue=None, epi_rows=(), epi_pars=(),
            epi_outs=(), full_rows=False, epi_accs=(), epi_into=None, dep=None):
    a_off, a_w = a_win if a_win is not None else (0, a.shape[1])
    if ta:
        kdim, m = a.shape[0], a_w
    else:
        m, kdim = a.shape[0], a_w
    n = b.shape[0] if tb else b.shape[1]
    assert (b.shape[1] if tb else b.shape[0]) == kdim, (name, a.shape, b.shape)
    out_dtypes = list(epi_outs) if epilogue is not None else [out_dtype]
    so = sum(jnp.dtype(d).itemsize for d in out_dtypes) + sum(r.dtype.itemsize for r, _ in epi_rows)
    tn_ok = [n] if full_rows else [t for t in (1024, 512, 256, 128) if all(off % t == 0 for _, off in epi_rows)]
    tm, tn, tk = _mm_tiles(m, n, kdim, a.dtype.itemsize, b.dtype.itemsize, so, tn_ok if epilogue is not None else None)
    nk = kdim // tk
    n_er, n_ep, n_out = len(epi_rows), len(epi_pars), len(out_dtypes)
    if ta:
        assert a_off % tm == 0
        a_spec = pl.BlockSpec((tk, tm), lambda i, j, k: (k, i + a_off // tm))
    else:
        assert a_off % tk == 0
        a_spec = pl.BlockSpec((tm, tk), lambda i, j, k: (i, k + a_off // tk))
    if tb:
        b_spec = pl.BlockSpec((tn, tk), lambda i, j, k: (j, k))
    else:
        b_spec = pl.BlockSpec((tk, tn), lambda i, j, k: (k, j))
    ca, cb = (0 if ta else 1), (1 if tb else 0)

    n_acc = len(epi_accs)
    n_x = (1 if epi_into is not None else 0) + (1 if dep is not None else 0)
    assert not (n_acc or epi_into is not None) or full_rows

    def body(a_ref, b_ref, *refs):
        er, ep = refs[:n_er], refs[n_er:n_er + n_ep]
        first_out = n_er + n_ep + n_x
        o_refs = refs[first_out:first_out + n_out]
        s_refs = refs[first_out + n_out:first_out + n_out + n_acc]
        acc = refs[first_out + n_out + n_acc:]

        def finish(c):
            outs = [c] if epilogue is None else epilogue(c, [r[...] for r in er], [p[...] for p in ep])
            if n_acc:
                outs, sums = outs

                @pl.when(pl.program_id(0) == 0)
                def _():
                    for s_ref in s_refs:
                        s_ref[...] = jnp.zeros_like(s_ref)

                for s_ref, s in zip(s_refs, sums, strict=True):
                    s_ref[...] += jnp.broadcast_to(s, s_ref.shape)
            for o_ref, o in zip(o_refs, outs, strict=True):
                o_ref[...] = o.astype(o_ref.dtype)

        if nk == 1:
            finish(_dg(a_ref[...], b_ref[...], ca, cb))
            return
        (acc_ref,) = acc
        k = pl.program_id(2)

        @pl.when(k == 0)
        def _():
            acc_ref[...] = jnp.zeros_like(acc_ref)

        acc_ref[...] += _dg(a_ref[...], b_ref[...], ca, cb)

        @pl.when(k == nk - 1)
        def _():
            finish(acc_ref[...])

    in_specs = [a_spec, b_spec]
    in_specs += [pl.BlockSpec((tm, tn), functools.partial(lambda i, j, k, c: (i, j + c), c=off // tn)) for _, off in epi_rows]
    in_specs += [pl.BlockSpec((p.shape[0], tn), lambda i, j, k: (0, j)) for p in epi_pars]
    out_specs = [pl.BlockSpec((tm, tn), lambda i, j, k: (i, j)) for _ in out_dtypes]
    out_shape = [jax.ShapeDtypeStruct((m, n), d) for d in out_dtypes]
    extra, aliases = [], {}
    if epi_into is not None:
        buf, off, width = epi_into
        assert off % width == 0 and buf.dtype == out_dtypes[0]
        in_specs.append(pl.BlockSpec(memory_space=pl.ANY))
        out_specs[0] = pl.BlockSpec((tm, width), functools.partial(lambda i, j, k, c: (i, c), c=off // width))
        out_shape[0] = jax.ShapeDtypeStruct(buf.shape, buf.dtype)
        extra, aliases = [buf], {2 + n_er + n_ep: 0}
    if dep is not None:
        in_specs.append(pl.BlockSpec(memory_space=pl.ANY))
        extra = extra + [dep]
    out_specs += [pl.BlockSpec((r, w), lambda i, j, k: (0, 0)) for r, w in epi_accs]
    out_shape += [jax.ShapeDtypeStruct((r, w), F32) for r, w in epi_accs]
    res = pl.pallas_call(
        body, name=name, grid=(m // tm, n // tn, nk),
        in_specs=in_specs, out_specs=out_specs, out_shape=out_shape, input_output_aliases=aliases,
        scratch_shapes=[pltpu.VMEM((tm, tn), F32)] if nk > 1 else [],
        compiler_params=_cparams(("arbitrary",) * 3 if n_acc else ("parallel", "parallel", "arbitrary")),
    )(a, b, *[r for r, _ in epi_rows], *epi_pars, *extra)
    return res if epilogue is not None else res[0]


BAND = 128


def _rowwise(name, fn, n_rows, tr, row_ins, par_ins, row_outs, acc_outs=(), into=None):
    nr, npar, no, na = len(row_ins), len(par_ins), len(row_outs), len(acc_outs)
    in_specs = []
    for arr, off, w in row_ins:
        assert off % w == 0 and arr.shape[0] == n_rows, (name, arr.shape, off, w)
        in_specs.append(pl.BlockSpec((tr, w), functools.partial(lambda i, c: (i, c), c=off // w)))
    for arr, off, w in par_ins:
        assert off % w == 0
        in_specs.append(pl.BlockSpec((arr.shape[0], w), functools.partial(lambda i, c: (0, c), c=off // w)))
    out_specs = [pl.BlockSpec((tr, w), lambda i: (i, 0)) for w, _ in row_outs]
    out_specs += [pl.BlockSpec((r, w), lambda i: (0, 0)) for r, w in acc_outs]
    out_shape = [jax.ShapeDtypeStruct((n_rows, w), dt) for w, dt in row_outs]
    out_shape += [jax.ShapeDtypeStruct((r, w), F32) for r, w in acc_outs]
    extra, aliases = [], {}
    if into is not None:
        buf, off = into
        w0 = row_outs[0][0]
        assert off % w0 == 0 and buf.dtype == row_outs[0][1]
        in_specs.append(pl.BlockSpec(memory_space=pl.ANY))
        out_specs[0] = pl.BlockSpec((tr, w0), functools.partial(lambda i, c: (i, c), c=off // w0))
        out_shape[0] = jax.ShapeDtypeStruct(buf.shape, buf.dtype)
        extra, aliases = [buf], {nr + npar: 0}
    nx = len(extra)

    def body(*refs):
        rows = [r[...] for r in refs[:nr]]
        pars = [r[...] for r in refs[nr:nr + npar]]
        o_refs = refs[nr + npar + nx:nr + npar + nx + no]
        a_refs = refs[nr + npar + nx + no:]
        outs, accs = fn(rows, pars)
        for o_ref, o in zip(o_refs, outs, strict=True):
            o_ref[...] = o.astype(o_ref.dtype)
        if na:
            @pl.when(pl.program_id(0) == 0)
            def _():
                for a_ref in a_refs:
                    a_ref[...] = jnp.zeros_like(a_ref)

            for a_ref, a in zip(a_refs, accs, strict=True):
                a_ref[...] += jnp.broadcast_to(a, a_ref.shape)

    res = pl.pallas_call(
        body, name=name, grid=(n_rows // tr,),
        in_specs=in_specs, out_specs=out_specs, out_shape=out_shape, input_output_aliases=aliases,
        compiler_params=_cparams(("arbitrary",) if na else ("parallel",)),
    )(*[a for a, _, _ in row_ins], *[a for a, _, _ in par_ins], *extra)
    return res


def _rms(x, w):
    return x * lax.rsqrt(jnp.mean(x * x, axis=-1, keepdims=True) + RMS_EPS) * w


def _s5_out(yc, u, z, t, d, bg):
    ge = jax.nn.gelu(yc + d * u)
    return ge * jax.nn.sigmoid(t + bg) * jax.nn.silu(z)


def _merge(g5, gs, m5, ms):
    return jax.nn.sigmoid(g5) * m5 + jax.nn.sigmoid(gs) * ms


def _head_loss(h1, pgl, pp, fw, tgt):
    h2 = h1 + jax.nn.sigmoid(pgl) * pp
    err = _rms(h2, fw) - tgt
    per_row = 0.5 * jnp.mean(err * err, axis=-1, keepdims=True)
    return jnp.sum(per_row, axis=0, keepdims=True)


def _s5_disc(a_re, a_im, log_step, b_re2, b_im2, expand):
    step = jnp.exp(log_step)
    mag = jnp.exp(a_re * step)
    lb_re = mag * jnp.cos(a_im * step)
    lb_im = mag * jnp.sin(a_im * step)
    den = a_re * a_re + a_im * a_im
    n_re = lb_re - 1.0
    f_re = (n_re * a_re + lb_im * a_im) / den
    f_im = (lb_im * a_re - n_re * a_im) / den
    hi = lax.Precision.HIGHEST
    fr = jnp.dot(expand, f_re, precision=hi, preferred_element_type=F32)
    fi = jnp.dot(expand, f_im, precision=hi, preferred_element_type=F32)
    return lb_re, lb_im, fr * b_re2 - fi * b_im2, fr * b_im2 + fi * b_re2


def _s5_params_fwd(a_re, a_im, log_step, b_re2, b_im2, expand):
    gp = a_re.shape

    def body(ar, ai, ls, br, bi, ex, pr_ref, pi_ref, bbr_ref, bbi_ref):
        lr, li, bbr, bbi = _s5_disc(ar[...], ai[...], ls[...], br[...], bi[...], ex[...])
        bbr_ref[...] = bbr
        bbi_ref[...] = bbi
        qr, qi = lr, li
        for k in range(S5_LOG_TB):
            pr_ref[k] = qr
            pi_ref[k] = qi
            qr, qi = qr * lr - qi * li, qr * li + qi * lr

    return pl.pallas_call(
        body, name="s5_params_fwd",
        out_shape=(jax.ShapeDtypeStruct((S5_LOG_TB,) + gp, F32), jax.ShapeDtypeStruct((S5_LOG_TB,) + gp, F32),
                   jax.ShapeDtypeStruct(b_re2.shape, F32), jax.ShapeDtypeStruct(b_re2.shape, F32)),
    )(a_re, a_im, log_step, b_re2, b_im2, expand)


def _s5_params_bwd(a_re, a_im, log_step, b_re2, b_im2, expand, d_lr, d_li, d_bbr, d_bbi):
    def body(ar, ai, ls, br, bi, ex, glr, gli, gbr, gbi, dar, dai, dls, dbr, dbi):
        _, vjp = jax.vjp(lambda *p: _s5_disc(*p, ex[...]), ar[...], ai[...], ls[...], br[...], bi[...])
        g = vjp((glr[...], gli[...], gbr[...], gbi[...]))
        for ref, val in zip((dar, dai, dls, dbr, dbi), g, strict=True):
            ref[...] = val

    return pl.pallas_call(
        body, name="s5_params_bwd",
        out_shape=tuple(jax.ShapeDtypeStruct(v.shape, F32) for v in (a_re, a_im, log_step, b_re2, b_im2)),
    )(a_re, a_im, log_step, b_re2, b_im2, expand, d_lr, d_li, d_bbr, d_bbi)


def _scan_block(x_ref, out_ref, lp, edge_pow, cr, ci, reverse, each=None):
    n_g = x_ref.shape[0] // 8
    sub = lax.broadcasted_iota(jnp.int32, (8, S5_LB), 0)
    steps = []
    for sh in (1, 2, 4):
        keep = (sub < 8 - sh) if reverse else (sub >= sh)
        steps.append((8 - sh if reverse else sh, jnp.where(keep, lp[sh - 1:sh, :S5_LB], 0.0),
                      jnp.where(keep, lp[sh - 1:sh, S5_LB:], 0.0)))
    e_r, e_i = edge_pow[:, :S5_LB], edge_pow[:, S5_LB:]
    for r in (range(n_g - 1, -1, -1) if reverse else range(n_g)):
        rows = slice(8 * r, 8 * r + 8)
        xr, xi = x_ref[rows, :S5_LB], x_ref[rows, S5_LB:]
        for by, a_r, a_i in steps:
            pr, pi = pltpu.roll(xr, by, 0), pltpu.roll(xi, by, 0)
            xr, xi = xr + a_r * pr - a_i * pi, xi + a_r * pi + a_i * pr
        xr = xr + e_r * cr - e_i * ci
        xi = xi + e_r * ci + e_i * cr
        out_ref[rows, :S5_LB] = xr
        out_ref[rows, S5_LB:] = xi
        if each is not None:
            each(r, xr, xi)
        cr, ci = (xr[0:1, :], xi[0:1, :]) if reverse else (xr[7:8, :], xi[7:8, :])
    return cr, ci


def _s5_fwd(proj, bb_band, c_band, lam_pow, s5_d, n_seq, seq_len):
    n_t = seq_len // S5_TB
    blk = 2 * S5_LB
    n_rows = n_seq * seq_len
    u_blk0 = OFF_U5 // BAND

    def body(u_ref, bb_ref, cb_ref, lp_ref, d_ref, s_ref, y_ref, ge_ref, cr, ci, buf):
        @pl.when(pl.program_id(2) == 0)
        def _():
            cr[...] = jnp.zeros_like(cr)
            ci[...] = jnp.zeros_like(ci)

        u = u_ref[...]
        buf[...] = _dg(u, bb_ref[...], 1, 0)
        lp = lp_ref[...]
        cr[...], ci[...] = _scan_block(buf, buf, lp, lp, cr[...], ci[...], False)
        s = buf[...].astype(s_ref.dtype)
        s_ref[...] = s
        y = _dg(s, cb_ref[...], 1, 1)
        y_ref[...] = y
        ge_ref[...] = jax.nn.gelu(y + d_ref[...] * u).astype(ge_ref.dtype)

    def rows(j, b, t):
        return (b * n_t + t, j)

    return pl.pallas_call(
        body, name="s5_fwd", grid=(S5_NJ, n_seq, n_t),
        in_specs=[pl.BlockSpec((S5_TB, BAND), lambda j, b, t: (b * n_t + t, u_blk0 + j)),
                  pl.BlockSpec((BAND, blk), lambda j, b, t: (j, 0)), pl.BlockSpec((BAND, blk), lambda j, b, t: (j, 0)),
                  pl.BlockSpec((S5_LOG_TB, blk), lambda j, b, t: (0, j)), pl.BlockSpec((1, BAND), lambda j, b, t: (0, j))],
        out_specs=[pl.BlockSpec((S5_TB, blk), rows), pl.BlockSpec((S5_TB, BAND), rows), pl.BlockSpec((S5_TB, BAND), rows)],
        out_shape=[jax.ShapeDtypeStruct((n_rows, S5_NJ * blk), BF16), jax.ShapeDtypeStruct((n_rows, S5_WIDTH), F32),
                   jax.ShapeDtypeStruct((n_rows, S5_WIDTH), BF16)],
        scratch_shapes=[pltpu.VMEM((1, S5_LB), F32), pltpu.VMEM((1, S5_LB), F32), pltpu.VMEM((S5_TB, blk), F32)],
        compiler_params=_cparams(("parallel", "parallel", "arbitrary")),
    )(proj, bb_band, c_band, lam_pow, s5_d)


def _s5_bwd(dyc, s, proj, du5_a, bb_band, c_band, lam_pow_conj, dproj, n_seq, seq_len):
    n_t = seq_len // S5_TB
    blk = 2 * S5_LB
    halo_rows = 16
    halo_per_blk = S5_TB // halo_rows
    u_blk0 = OFF_U5 // BAND

    def rows(j, b, t):
        return (b * n_t + (n_t - 1 - t), j)

    def u_rows(j, b, t):
        return (b * n_t + (n_t - 1 - t), u_blk0 + j)

    def halo(j, b, t):
        return (jnp.maximum((b * n_t + (n_t - 1 - t)) * halo_per_blk - 1, 0), j)

    def body(dy_ref, sq_ref, hq_ref, u_ref, dua_ref, bb_ref, cb_ref, lp_ref, _, du_ref, dl_ref, dbb_ref, dcb_ref,
             cr, ci, buf, s_ref):
        b, t = pl.program_id(1), pl.program_id(2)
        sq = sq_ref[...]
        s_ref[...] = sq.astype(F32)
        h_last = hq_ref[...].astype(F32)[halo_rows - 1:halo_rows, :]
        dy = dy_ref[...]
        buf[...] = _dg(dy, cb_ref[...], 1, 0)

        @pl.when(t == 0)
        def _():
            cr[...] = jnp.zeros_like(cr)
            ci[...] = jnp.zeros_like(ci)

        @pl.when((b == 0) & (t == 0))
        def _():
            dl_ref[...] = jnp.zeros_like(dl_ref)
            dbb_ref[...] = jnp.zeros_like(dbb_ref)
            dcb_ref[...] = jnp.zeros_like(dcb_ref)

        dcb_ref[...] += _dg(dy, sq, 0, 0)

        first_blk = t == n_t - 1
        sub = lax.broadcasted_iota(jnp.int32, (8, S5_LB), 0)
        acc = [jnp.zeros((8, S5_LB), F32), jnp.zeros((8, S5_LB), F32)]

        def each(r, gr, gi):
            rows = slice(8 * r, 8 * r + 8)
            if r == 0:
                before_r = jnp.where(first_blk, 0.0, h_last[:, :S5_LB])
                before_i = jnp.where(first_blk, 0.0, h_last[:, S5_LB:])
            else:
                before_r, before_i = s_ref[8 * r - 1:8 * r, :S5_LB], s_ref[8 * r - 1:8 * r, S5_LB:]
            sp_r = jnp.where(sub == 0, before_r, pltpu.roll(s_ref[rows, :S5_LB], 1, 0))
            sp_i = jnp.where(sub == 0, before_i, pltpu.roll(s_ref[rows, S5_LB:], 1, 0))
            acc[0] = acc[0] + gr * sp_r + gi * sp_i
            acc[1] = acc[1] + gi * sp_r - gr * sp_i

        lp = lp_ref[...]
        edge_pow = jnp.concatenate([lp[7 - i:8 - i, :] for i in range(8)], axis=0)
        cr[...], ci[...] = _scan_block(buf, buf, lp, edge_pow, cr[...], ci[...], True, each)
        g = buf[...].astype(BF16)
        du_ref[...] = (dua_ref[...] + _dg(g, bb_ref[...], 1, 1)).astype(du_ref.dtype)
        dbb_ref[...] += _dg(u_ref[...], g, 0, 0)
        dl_ref[:, :S5_LB] += jnp.sum(acc[0], axis=0, keepdims=True)
        dl_ref[:, S5_LB:] += jnp.sum(acc[1], axis=0, keepdims=True)

    band = pl.BlockSpec((BAND, blk), lambda j, b, t: (j, 0))
    return pl.pallas_call(
        body, name="s5_bwd", grid=(S5_NJ, n_seq, n_t),
        in_specs=[pl.BlockSpec((S5_TB, BAND), rows), pl.BlockSpec((S5_TB, blk), rows),
                  pl.BlockSpec((halo_rows, blk), halo), pl.BlockSpec((S5_TB, BAND), u_rows),
                  pl.BlockSpec((S5_TB, BAND), rows), band, band,
                  pl.BlockSpec((S5_LOG_TB, blk), lambda j, b, t: (0, j)), pl.BlockSpec(memory_space=pl.ANY)],
        out_specs=[pl.BlockSpec((S5_TB, BAND), u_rows), pl.BlockSpec((1, blk), lambda j, b, t: (0, j)), band, band],
        out_shape=[jax.ShapeDtypeStruct(dproj.shape, dproj.dtype), jax.ShapeDtypeStruct((1, S5_NJ * blk), F32),
                   jax.ShapeDtypeStruct(bb_band.shape, F32), jax.ShapeDtypeStruct(c_band.shape, F32)],
        input_output_aliases={8: 0},
        scratch_shapes=[pltpu.VMEM((1, S5_LB), F32), pltpu.VMEM((1, S5_LB), F32), pltpu.VMEM((S5_TB, blk), F32),
                        pltpu.VMEM((S5_TB, blk), F32)],
        compiler_params=_cparams(("parallel", "arbitrary", "arbitrary")),
    )(dyc, s, s, proj, du5_a, bb_band, c_band, lam_pow_conj, dproj)


CONV_TR = 512
CONV_CW = 512


def _shift_down(x, halo, k):
    if k == 0:
        return x
    row8 = lax.broadcasted_iota(jnp.int32, halo.shape, 0)
    rolled = pltpu.roll(x, k, 0)
    top = jnp.where(row8 < k, pltpu.roll(halo, k, 0), rolled[:8])
    if x.shape[0] == 8:
        return top
    return jnp.concatenate([top, rolled[8:]], axis=0)


def _shift_up(x, halo, k):
    if k == 0:
        return x
    n = x.shape[0]
    row8 = lax.broadcasted_iota(jnp.int32, halo.shape, 0)
    rolled = pltpu.roll(x, n - k, 0)
    bot = jnp.where(row8 >= 8 - k, pltpu.roll(halo, 8 - k, 0), rolled[n - 8:])
    if n == 8:
        return bot
    return jnp.concatenate([rolled[:n - 8], bot], axis=0)


def _conv_pre(x, halo, w, b):
    acc = b + w[SSD_CONV - 1:SSD_CONV, :] * x
    for k in range(SSD_CONV - 1):
        acc = acc + w[k:k + 1, :] * _shift_down(x, halo, SSD_CONV - 1 - k)
    return acc


def _conv_specs(seq_len, col_off):
    lt = seq_len // CONV_TR
    cb = col_off // CONV_CW
    cur = pl.BlockSpec((CONV_TR, CONV_CW), lambda j, i: (i, j + cb))
    prev = pl.BlockSpec((8, CONV_CW), lambda j, i: (jnp.maximum(i * (CONV_TR // 8) - 1, 0), j + cb))
    return lt, cur, prev


def _conv_fwd(proj, conv_w, conv_b, n_rows, seq_len):
    lt, cur, prev = _conv_specs(seq_len, OFF_XBC)

    def body(x_ref, h_ref, w_ref, b_ref, o_ref):
        halo = jnp.where(pl.program_id(1) % lt == 0, 0.0, h_ref[...])
        o_ref[...] = jax.nn.silu(_conv_pre(x_ref[...], halo, w_ref[...], b_ref[...]))

    return pl.pallas_call(
        body, name="ssd_conv_fwd", grid=(SSD_CONV_DIM // CONV_CW, n_rows // CONV_TR),
        in_specs=[cur, prev, pl.BlockSpec((SSD_CONV, CONV_CW), lambda j, i: (0, j)),
                  pl.BlockSpec((1, CONV_CW), lambda j, i: (0, j))],
        out_specs=pl.BlockSpec((CONV_TR, CONV_CW), lambda j, i: (i, j)),
        out_shape=jax.ShapeDtypeStruct((n_rows, SSD_CONV_DIM), F32),
        compiler_params=_cparams(("parallel", "parallel")),
    )(proj, proj, conv_w, conv_b)


def _conv_bwd(name, proj, d_act, conv_w, conv_b, n_rows, seq_len, col_off, dproj):
    width = d_act.shape[1]
    lt, cur, prev = _conv_specs(seq_len, OFF_XBC + col_off)
    n_blk = n_rows // CONV_TR
    cb = (OFF_XBC + col_off) // CONV_CW
    pb = col_off // CONV_CW
    nxt = pl.BlockSpec((8, CONV_CW), lambda j, i: (jnp.minimum((i + 1) * (CONV_TR // 8), n_rows // 8 - 1), j + cb))
    d_cur = pl.BlockSpec((CONV_TR, CONV_CW), lambda j, i: (i, j))
    d_nxt = pl.BlockSpec((8, CONV_CW), lambda j, i: (jnp.minimum((i + 1) * (CONV_TR // 8), n_rows // 8 - 1), j))

    def dsilu(pre):
        sg = jax.nn.sigmoid(pre)
        return sg * (1.0 + pre * (1.0 - sg))

    def body(x_ref, hp_ref, hn_ref, d_ref, dn_ref, w_ref, b_ref, _, dx_ref, dw_ref, db_ref):
        i = pl.program_id(1)
        x, w, b = x_ref[...], w_ref[...], b_ref[...]
        halo_p = jnp.where(i % lt == 0, 0.0, hp_ref[...])
        at_end = i % lt == lt - 1
        dpre = d_ref[...] * dsilu(_conv_pre(x, halo_p, w, b))
        pre_n = _conv_pre(hn_ref[...], x[CONV_TR - 8:, :], w, b)
        dpre_n = jnp.where(at_end, 0.0, dn_ref[...] * dsilu(pre_n))
        dx = w[SSD_CONV - 1:SSD_CONV, :] * dpre
        for k in range(SSD_CONV - 1):
            dx = dx + w[k:k + 1, :] * _shift_up(dpre, dpre_n, SSD_CONV - 1 - k)
        dx_ref[...] = dx.astype(dx_ref.dtype)

        @pl.when(i == 0)
        def _():
            dw_ref[...] = jnp.zeros_like(dw_ref)
            db_ref[...] = jnp.zeros_like(db_ref)

        for k in range(SSD_CONV):
            xs = _shift_down(x, halo_p, SSD_CONV - 1 - k)
            dw_ref[k:k + 1, :] += jnp.sum(dpre * xs, axis=0, keepdims=True)
        db_ref[...] += jnp.sum(dpre, axis=0, keepdims=True)

    return pl.pallas_call(
        body, name=name, grid=(width // CONV_CW, n_blk),
        in_specs=[cur, prev, nxt, d_cur, d_nxt,
                  pl.BlockSpec((SSD_CONV, CONV_CW), lambda j, i: (0, j + pb)),
                  pl.BlockSpec((1, CONV_CW), lambda j, i: (0, j + pb)), pl.BlockSpec(memory_space=pl.ANY)],
        out_specs=[pl.BlockSpec((CONV_TR, CONV_CW), lambda j, i: (i, j + cb)),
                   pl.BlockSpec((SSD_CONV, CONV_CW), lambda j, i: (0, j)),
                   pl.BlockSpec((1, CONV_CW), lambda j, i: (0, j))],
        out_shape=[jax.ShapeDtypeStruct(dproj.shape, dproj.dtype),
                   jax.ShapeDtypeStruct((SSD_CONV, width), F32), jax.ShapeDtypeStruct((1, width), F32)],
        input_output_aliases={7: 0},
        compiler_params=_cparams(("parallel", "arbitrary")),
    )(proj, proj, proj, d_act, d_act, conv_w, conv_b, dproj)


def _split3(x):
    hi = x.astype(BF16)
    r = x - hi.astype(F32)
    mid = r.astype(BF16)
    return hi, mid, (r - mid.astype(F32)).astype(BF16)


def _sel_dot(a, b, a_is_sel):
    dn = (((1,), (0,)), ((), ()))
    if a_is_sel:
        return sum(lax.dot_general(a, t, dn, preferred_element_type=F32) for t in _split3(b))
    return sum(lax.dot_general(t, b, dn, preferred_element_type=F32) for t in _split3(a))


@jax.custom_vjp
def sel_left(sel, sel_t, x):
    return _sel_dot(sel, x, True)


@jax.custom_vjp
def sel_right(x, sel, sel_t):
    return _sel_dot(x, sel, False)


sel_left.defvjp(lambda s, st, x: (_sel_dot(s, x, True), (s, st)),
                lambda r, g: (jnp.zeros_like(r[0]), jnp.zeros_like(r[1]), _sel_dot(r[1], g, True)))
sel_right.defvjp(lambda x, s, st: (_sel_dot(x, s, False), (s, st)),
                 lambda r, g: (_sel_dot(g, r[1], False), jnp.zeros_like(r[0]), jnp.zeros_like(r[1])))


def _ssd_chunk(xs, bm, cm, dtr, st, dtb, alog, dsk, k):
    dt = jax.nn.softplus(dtr + dtb)
    acum = sel_left(k["tri"], k["tri_t"], dt * (-jnp.exp(alog)))
    dt_e = sel_right(dt, k["spread"], k["spread_t"])
    ac_e = sel_right(acum, k["spread"], k["spread_t"])
    al_e = ac_e[SSD_CHUNK - 1:SSD_CHUNK, :]
    dsk_e = sel_right(jnp.broadcast_to(dsk, (8, 128)), k["spread"], k["spread_t"])[0:1, :]
    xdt = xs * dt_e
    acum_t = acum.T
    scores = dot_nt(cm, bm)
    y = dot_nn(cm, st) * jnp.exp(ac_e) + xs * dsk_e
    for j in range(SSD_HPG):
        lmat = jnp.exp(jnp.where(k["causal"], acum[:, j:j + 1] - acum_t[j:j + 1, :], -jnp.inf))
        y = y + dot_nn(scores * lmat, jnp.where(k["head"] == j, xdt, 0.0))
    new = st * jnp.exp(al_e) + dot_tn(bm, xdt * jnp.exp(al_e - ac_e))
    return y, new


def _ssd_consts():
    r = lax.broadcasted_iota(jnp.int32, (SSD_CHUNK, SSD_CHUNK), 0)
    c = lax.broadcasted_iota(jnp.int32, (SSD_CHUNK, SSD_CHUNK), 1)
    hd = jnp.int32(SSD_HEADDIM)
    sr = lax.broadcasted_iota(jnp.int32, (128, GROUP_W), 0)
    sc = lax.div(lax.broadcasted_iota(jnp.int32, (128, GROUP_W), 1), hd)
    tr = lax.div(lax.broadcasted_iota(jnp.int32, (GROUP_W, 128), 0), hd)
    tc = lax.broadcasted_iota(jnp.int32, (GROUP_W, 128), 1)
    return {"tri": (r >= c).astype(BF16), "tri_t": (c >= r).astype(BF16), "causal": r >= c,
            "spread": (sr == sc).astype(BF16), "spread_t": (tr == tc).astype(BF16),
            "head": lax.div(lax.broadcasted_iota(jnp.int32, (SSD_CHUNK, GROUP_W), 1), hd)}


def _ssd_specs(n_c, reverse):
    def cidx(c):
        return n_c - 1 - c if reverse else c

    xs = pl.BlockSpec((SSD_CHUNK, GROUP_W), lambda g, b, c: (b * n_c + cidx(c), g))
    bm = pl.BlockSpec((SSD_CHUNK, SSD_STATE), lambda g, b, c: (b * n_c + cidx(c), SSD_WIDTH // SSD_STATE + g))
    cm = pl.BlockSpec((SSD_CHUNK, SSD_STATE), lambda g, b, c: (b * n_c + cidx(c), (SSD_WIDTH + SSD_BC) // SSD_STATE + g))
    dt = pl.BlockSpec((SSD_CHUNK, 128), lambda g, b, c: (b * n_c + cidx(c), OFF_DT // 128 + g))
    par = pl.BlockSpec((1, 128), lambda g, b, c: (0, g))
    st = pl.BlockSpec((1, 1, 1, SSD_STATE, GROUP_W), lambda g, b, c: (b, cidx(c), g, 0, 0))
    return xs, bm, cm, dt, par, st


def _ssd_chunk_gated(xs, bm, cm, dtr, st, dtb, alog, dsk, z, nw, k):
    y, new = _ssd_chunk(xs, bm, cm, dtr, st, dtb, alog, dsk, k)
    yg = y * jax.nn.silu(z)
    return yg * lax.rsqrt(jnp.mean(yg * yg, axis=-1, keepdims=True) + RMS_EPS) * nw, new


def _ssd_gate_specs(n_c, reverse):
    def cidx(c):
        return n_c - 1 - c if reverse else c

    z = pl.BlockSpec((SSD_CHUNK, GROUP_W), lambda g, b, c: (b * n_c + cidx(c), OFF_ZS // GROUP_W + g))
    nw = pl.BlockSpec((1, GROUP_W), lambda g, b, c: (0, g))
    return z, nw


def _ssd_fwd(xbc_act, proj, dtb, alog, dsk, norm_w, n_seq, seq_len):
    n_c = seq_len // SSD_CHUNK
    xs_s, bm_s, cm_s, dt_s, par_s, st_s = _ssd_specs(n_c, False)
    z_s, nw_s = _ssd_gate_specs(n_c, False)

    def body(xs_ref, bm_ref, cm_ref, dt_ref, dtb_ref, al_ref, dk_ref, z_ref, nw_ref, y_ref, st_ref, state):
        @pl.when(pl.program_id(2) == 0)
        def _():
            state[...] = jnp.zeros_like(state)

        prev = state[...]
        st_ref[0, 0, 0] = prev
        y, new = _ssd_chunk_gated(xs_ref[...], bm_ref[...], cm_ref[...], dt_ref[...], prev,
                                  dtb_ref[...], al_ref[...], dk_ref[...], z_ref[...], nw_ref[...], _ssd_consts())
        y_ref[...] = y.astype(y_ref.dtype)
        state[...] = new

    return pl.pallas_call(
        body, name="ssd_fwd", grid=(SSD_GROUPS, n_seq, n_c),
        in_specs=[xs_s, bm_s, cm_s, dt_s, par_s, par_s, par_s, z_s, nw_s],
        out_specs=[pl.BlockSpec((SSD_CHUNK, GROUP_W), lambda g, b, c: (b * n_c + c, g)), st_s],
        out_shape=[jax.ShapeDtypeStruct((n_seq * seq_len, SSD_WIDTH), BF16),
                   jax.ShapeDtypeStruct((n_seq, n_c, SSD_GROUPS, SSD_STATE, GROUP_W), F32)],
        scratch_shapes=[pltpu.VMEM((SSD_STATE, GROUP_W), F32)],
        compiler_params=_cparams(("parallel", "parallel", "arbitrary")),
    )(xbc_act, xbc_act, xbc_act, proj, dtb, alog, dsk, proj, norm_w)


def _ssd_bwd(xbc_act, proj, states, dy, dtb, alog, dsk, norm_w, n_seq, seq_len, dproj):
    n_c = seq_len // SSD_CHUNK
    n_rows = n_seq * seq_len
    xs_s, bm_s, cm_s, dt_s, par_s, st_s = _ssd_specs(n_c, True)
    z_s, nw_s = _ssd_gate_specs(n_c, True)

    def rows(w, first=0):
        return pl.BlockSpec((SSD_CHUNK, w), lambda g, b, c: (b * n_c + (n_c - 1 - c), first + g))

    def body(xs_ref, bm_ref, cm_ref, dt_ref, st_ref, dy_ref, dtb_ref, al_ref, dk_ref, z_ref, nw_ref, _,
             dxs_ref, dbm_ref, dcm_ref, dz_ref, ddt_ref, ddtb_ref, dal_ref, ddk_ref, dnw_ref, dstate):
        b, c = pl.program_id(1), pl.program_id(2)

        @pl.when(c == 0)
        def _():
            dstate[...] = jnp.zeros_like(dstate)

        @pl.when((b == 0) & (c == 0))
        def _():
            for ref in (ddtb_ref, dal_ref, ddk_ref, dnw_ref):
                ref[...] = jnp.zeros_like(ref)

        consts = _ssd_consts()
        _, vjp = jax.vjp(
            lambda *a: _ssd_chunk_gated(*a, consts),
            xs_ref[...], bm_ref[...], cm_ref[...], dt_ref[...], st_ref[0, 0, 0], dtb_ref[...], al_ref[...], dk_ref[...],
            z_ref[...], nw_ref[...])
        dxs, dbm, dcm, ddtr, dprev, ddtb, dal, ddk, dz, dnw = vjp((dy_ref[...], dstate[...]))
        dxs_ref[...] = dxs
        dbm_ref[...] = dbm
        dcm_ref[...] = dcm
        dz_ref[...] = dz.astype(dz_ref.dtype)
        ddt_ref[...] = ddtr.astype(ddt_ref.dtype)
        ddtb_ref[...] += ddtb
        dal_ref[...] += dal
        ddk_ref[...] += ddk
        dnw_ref[...] += dnw
        dstate[...] = dprev

    acc = pl.BlockSpec((1, 128), lambda g, b, c: (0, g))
    return pl.pallas_call(
        body, name="ssd_bwd", grid=(SSD_GROUPS, n_seq, n_c),
        in_specs=[xs_s, bm_s, cm_s, dt_s, st_s, rows(GROUP_W), par_s, par_s, par_s, z_s, nw_s,
                  pl.BlockSpec(memory_space=pl.ANY)],
        out_specs=[rows(GROUP_W), rows(SSD_STATE), rows(SSD_STATE), rows(GROUP_W, OFF_ZS // GROUP_W), rows(128),
                   acc, acc, acc, pl.BlockSpec((1, GROUP_W), lambda g, b, c: (0, g))],
        out_shape=[jax.ShapeDtypeStruct((n_rows, SSD_WIDTH), F32), jax.ShapeDtypeStruct((n_rows, SSD_BC), F32),
                   jax.ShapeDtypeStruct((n_rows, SSD_BC), F32), jax.ShapeDtypeStruct(dproj.shape, dproj.dtype),
                   jax.ShapeDtypeStruct((n_rows, DT_W), BF16),
                   jax.ShapeDtypeStruct((1, 512), F32), jax.ShapeDtypeStruct((1, 512), F32),
                   jax.ShapeDtypeStruct((1, 512), F32), jax.ShapeDtypeStruct((1, SSD_WIDTH), F32)],
        input_output_aliases={11: 3},
        scratch_shapes=[pltpu.VMEM((SSD_STATE, GROUP_W), F32)],
        compiler_params=_cparams(("parallel", "arbitrary", "arbitrary")),
    )(xbc_act, xbc_act, xbc_act, proj, states, dy, dtb, alog, dsk, proj, norm_w, dproj)


def _pad_heads(v):
    return jnp.pad(v.reshape(SSD_GROUPS, SSD_HPG), ((0, 0), (0, 128 - SSD_HPG))).reshape(1, SSD_GROUPS * 128)


def _unpad_heads(v):
    return v.reshape(SSD_GROUPS, 128)[:, :SSD_HPG].reshape(1, SSD_HEADS)


def _state_cols(v):
    re, im = v
    lead = re.shape[:-1]
    re = re.reshape(lead + (S5_NJ, 1, S5_LB))
    im = im.reshape(lead + (S5_NJ, 1, S5_LB))
    return jnp.concatenate([re, im], axis=-2).reshape(lead + (2 * S5_N,))


def _state_uncols(v):
    lead = v.shape[:-1]
    v = v.reshape(lead + (S5_NJ, 2, S5_LB))
    return v[..., 0, :].reshape(lead + (S5_N,)), v[..., 1, :].reshape(lead + (S5_N,))


GROUPS_PER_BAND = BAND // S5_GROUP


def _band(w2_re, w2_im):
    gh = S5_GROUPS * S5_GROUP
    rg = ((jnp.arange(gh) // S5_GROUP) % GROUPS_PER_BAND)[:, None, None]
    cg = jnp.arange(GROUPS_PER_BAND)[None, :, None]
    parts = [jnp.where(rg == cg, v[:, None, :], 0.0).reshape(gh, S5_LB) for v in (w2_re, w2_im)]
    return jnp.concatenate(parts, axis=1)


def _band_take(wb):
    gh = S5_GROUPS * S5_GROUP
    w4 = wb.reshape(gh, 2, GROUPS_PER_BAND, S5_STATE)
    sel = w4[jnp.arange(gh), :, (jnp.arange(gh) // S5_GROUP) % GROUPS_PER_BAND, :]
    return sel[:, 0, :], sel[:, 1, :]


W_IN_SHARD = IN_PROJ_DIM // N_CHIPS
W_IN_SEGS = ((0, 512, OFF_U5), (512, 1024, OFF_Z5), (1024, 2560, OFF_ZS), (2560, 5120, OFF_XBC), (5144, 7192, OFF_G5))
DT_ROWS = (5120, 5144)


def _w_in_pieces():
    runs = []
    segs = list(W_IN_SEGS) + [(DT_ROWS[0] + SSD_HPG * g, DT_ROWS[0] + SSD_HPG * (g + 1), OFF_DT + 128 * g)
                              for g in range(SSD_GROUPS)]
    for lo, hi, off in segs:
        for j in range(N_CHIPS):
            s, e = max(lo, j * W_IN_SHARD), min(hi, (j + 1) * W_IN_SHARD)
            if s < e:
                runs.append((j, s - j * W_IN_SHARD, off + s - lo, e - s))
    return runs


RELAYOUT_LANES = 256


def _pad_w_in_t(a4):
    runs = _w_in_pieces()

    def body(a_ref, o_ref):
        o_ref[pl.ds(OFF_DT, DT_W), :] = jnp.zeros((DT_W, RELAYOUT_LANES), o_ref.dtype)
        for j, src, dst, n in runs:
            o_ref[pl.ds(dst, n), :] = a_ref[j, pl.ds(src, n), :]

    return pl.pallas_call(
        body, name="w_in_to_padded", grid=(D_MODEL // RELAYOUT_LANES,),
        in_specs=[pl.BlockSpec((N_CHIPS, W_IN_SHARD, RELAYOUT_LANES), lambda i: (0, 0, i))],
        out_specs=pl.BlockSpec((PROJ_W, RELAYOUT_LANES), lambda i: (0, i)),
        out_shape=jax.ShapeDtypeStruct((PROJ_W, D_MODEL), a4.dtype),
        compiler_params=_cparams(("parallel",)),
    )(a4)


def _unpad_w_in_t(wp):
    runs = _w_in_pieces()

    def body(p_ref, o_ref):
        for j, dst, src, n in runs:
            o_ref[j, pl.ds(dst, n), :] = p_ref[pl.ds(src, n), :]

    return pl.pallas_call(
        body, name="w_in_from_padded", grid=(D_MODEL // RELAYOUT_LANES,),
        in_specs=[pl.BlockSpec((PROJ_W, RELAYOUT_LANES), lambda i: (0, i))],
        out_specs=pl.BlockSpec((N_CHIPS, W_IN_SHARD, RELAYOUT_LANES), lambda i: (0, 0, i)),
        out_shape=jax.ShapeDtypeStruct((N_CHIPS, W_IN_SHARD, D_MODEL), wp.dtype),
        compiler_params=_cparams(("parallel",)),
    )(wp)


def _local_step(x, p, tgt, w):
    n_seq, seq_len, _ = x.shape
    n_rows = n_seq * seq_len
    tr = 512
    x2 = x.reshape(n_rows, D_MODEL)
    p2 = p.reshape(n_rows, PLE_DIM)
    t2 = tgt.reshape(n_rows, D_MODEL)
    row = functools.partial(_rowwise, n_rows=n_rows, tr=tr)

    w_pad_t = _pad_w_in_t(w["w_in_t"])
    norm_w = w["norm_w"].reshape(1, D_MODEL)
    ple_norm_w = w["ple_norm_w"].reshape(1, D_MODEL)
    final_w = w["final_norm_w"].reshape(1, D_MODEL)
    s5_d = w["s5_d"].reshape(1, S5_WIDTH)
    b_glu = w["s5_b_glu"].reshape(1, S5_WIDTH)
    conv_w = w["ssd_conv_w"].reshape(SSD_CONV, SSD_CONV_DIM)
    conv_b = w["ssd_conv_b"].reshape(1, SSD_CONV_DIM)
    ssd_norm_w = w["ssd_norm_w"].reshape(1, SSD_WIDTH)
    dtb, alog, dsk = (_pad_heads(w[k].reshape(1, SSD_HEADS)) for k in ("ssd_dt_bias", "ssd_a_log", "ssd_d"))

    gh = S5_GROUPS * S5_GROUP
    a_re = w["s5_a_re"].reshape(S5_GROUPS, S5_STATE)
    a_im = w["s5_a_im"].reshape(S5_GROUPS, S5_STATE)
    log_step = w["s5_log_step"].reshape(S5_GROUPS, 1)
    b_re2 = jnp.transpose(w["s5_b_re"].reshape(S5_GROUPS, S5_STATE, S5_GROUP), (0, 2, 1)).reshape(gh, S5_STATE)
    b_im2 = jnp.transpose(w["s5_b_im"].reshape(S5_GROUPS, S5_STATE, S5_GROUP), (0, 2, 1)).reshape(gh, S5_STATE)
    expand = (jnp.arange(gh)[:, None] // S5_GROUP == jnp.arange(S5_GROUPS)[None, :]).astype(F32)
    pow_re, pow_im, bb_re2, bb_im2 = _s5_params_fwd(a_re, a_im, log_step, b_re2, b_im2, expand)
    lam_pow = _state_cols((pow_re.reshape(S5_LOG_TB, S5_N), pow_im.reshape(S5_LOG_TB, S5_N)))
    lam_pow_conj = _state_cols((pow_re.reshape(S5_LOG_TB, S5_N), -pow_im.reshape(S5_LOG_TB, S5_N)))
    bb_band = _band(bb_re2, bb_im2).astype(BF16)
    c_band = _band(w["s5_c_re"].reshape(gh, S5_STATE), -w["s5_c_im"].reshape(gh, S5_STATE)).astype(BF16)

    late = w.get("_late")
    (hn,) = row("rms_in", lambda r, q: ([_rms(r[0], q[0])], []), row_ins=[(x2, 0, D_MODEL)],
                par_ins=[(norm_w, 0, D_MODEL)] + ([(late[0], 0, 128)] if late else []), row_outs=[(D_MODEL, BF16)])
    proj = _matmul("mm_proj", hn, w_pad_t, tb=True)
    s, yc, ge = _s5_fwd(proj, bb_band, c_band, lam_pow, s5_d, n_seq, seq_len)
    if late:
        w = {**w, **late[1](ge)}
    tg, y5 = _matmul("mm_s5_glu", ge, w["s5_w_glu"], epilogue=lambda c, r, q: [c, _s5_out(r[0], r[1], r[2], c, *q)],
                     epi_rows=[(yc, 0), (proj, OFF_U5), (proj, OFF_Z5)], epi_pars=[s5_d, b_glu], epi_outs=[F32, BF16])

    xbc_act = _conv_fwd(proj, conv_w, conv_b, n_rows, seq_len)
    yss, states = _ssd_fwd(xbc_act, proj, dtb, alog, dsk, ssd_norm_w, n_seq, seq_len)

    m5 = _matmul("mm_br_s5", y5, w["w_br_s5"])
    ms, merged = _matmul("mm_br_ssd", yss, w["w_br_ssd"], epilogue=lambda c, r, q: [c, _merge(r[0], r[1], r[2], c)],
                         epi_rows=[(proj, OFF_G5), (proj, OFF_GS), (m5, 0)], epi_outs=[F32, BF16])

    def resid_norm(c, r, q):
        h1_ = r[0] + c
        return [h1_, _rms(h1_, q[0])]

    h1, hp = _matmul("mm_out", merged, w["w_out"], epilogue=resid_norm, epi_rows=[(x2, 0)], epi_pars=[ple_norm_w],
                     epi_outs=[F32, BF16], full_rows=True)
    pp = _matmul("mm_ple_proj", p2, w["w_ple_proj"])

    def head_fn(pgl_, r, q):
        h1_, pp_, tgt_ = r
        loss, vjp = jax.vjp(lambda a, b, c, f: _head_loss(a, b, c, f, tgt_), h1_, pgl_, pp_, q[0])
        dh1_, dpgl_, dpp_, dfw_ = vjp(jnp.ones_like(loss))
        return [dh1_, dpgl_, dpp_], [loss, dfw_]

    dh2, dpgl, dpp, loss_acc, d_final_w = _matmul(
        "mm_ple_gate_head", hp, w["w_ple_gate"], epilogue=head_fn, epi_rows=[(h1, 0), (pp, 0), (t2, 0)],
        epi_pars=[final_w], epi_outs=[F32, BF16, BF16], epi_accs=[(1, 128), (1, D_MODEL)], full_rows=True)
    loss = loss_acc[0, 0]

    g = {}
    g["final_norm_w"] = d_final_w
    g["w_ple_gate"] = _matmul("mm_d_w_ple_gate", hp, dpgl, ta=True)
    g["w_ple_proj"] = _matmul("mm_d_w_ple_proj", p2, dpp, ta=True)

    def ple_norm_bwd(dhp_, r, q):
        h1_, dh2_ = r
        _, vjp = jax.vjp(_rms, h1_, q[0])
        dh, dw = vjp(dhp_)
        dh = dh + dh2_
        return [dh, dh], [dw]

    dh1, dh1_b, g["ple_norm_w"] = _matmul(
        "mm_d_hp", dpgl, w["w_ple_gate"], tb=True, epilogue=ple_norm_bwd, epi_rows=[(h1, 0), (dh2, 0)],
        epi_pars=[ple_norm_w], epi_outs=[F32, BF16], epi_accs=[(1, D_MODEL)], full_rows=True)
    g["w_out"] = _matmul("mm_d_w_out", merged, dh1_b, ta=True)

    dproj = lax.empty((n_rows, PROJ_W), BF16)

    def merge_bwd(dmerged, r, q):
        sg5, sgs = jax.nn.sigmoid(r[0]), jax.nn.sigmoid(r[1])
        d_gates = jnp.concatenate([dmerged * r[2] * sg5 * (1.0 - sg5), dmerged * r[3] * sgs * (1.0 - sgs)], axis=1)
        return [d_gates, dmerged * sg5, dmerged * sgs]

    dproj, dm5, dms = _matmul(
        "mm_d_merged", dh1_b, w["w_out"], tb=True, epilogue=merge_bwd,
        epi_rows=[(proj, OFF_G5), (proj, OFF_GS), (m5, 0), (ms, 0)], epi_outs=[BF16, BF16, BF16], full_rows=True,
        epi_into=(dproj, OFF_G5, 2 * D_MODEL))
    g["w_br_s5"] = _matmul("mm_d_w_br_s5", y5, dm5, ta=True)
    g["w_br_ssd"] = _matmul("mm_d_w_br_ssd", yss, dms, ta=True)
    dyss = _matmul("mm_d_yss", dms, w["w_br_ssd"], tb=True)

    def s5_out_bwd_a(dy_, r, q):
        yc_, u_, z_, t_ = r
        d_, bg_ = q
        ge_ = jax.nn.gelu(yc_ + d_ * u_)
        _, vjp = jax.vjp(lambda a, z, t, b: a * jax.nn.sigmoid(t + b) * jax.nn.silu(z), ge_, z_, t_, bg_)
        dge, dz, dt_, dbg = vjp(dy_)
        return [dz, dge, dt_], [dbg]

    dproj, dge_a, dtg, g["s5_b_glu"] = _matmul(
        "mm_d_y5", dm5, w["w_br_s5"], tb=True, epilogue=s5_out_bwd_a,
        epi_rows=[(yc, 0), (proj, OFF_U5), (proj, OFF_Z5), (tg, 0)], epi_pars=[s5_d, b_glu],
        epi_outs=[BF16, F32, BF16], epi_accs=[(1, S5_WIDTH)], full_rows=True, epi_into=(dproj, OFF_Z5, S5_WIDTH))
    g["s5_w_glu"] = _matmul("mm_d_w_glu", ge, dtg, ta=True)
    early = w["_early"](g) if "_early" in w else None

    def s5_out_bwd_b(dge_b, r, q):
        yc_, u_, da_ = r
        _, vjp = jax.vjp(lambda yc, u, d: jax.nn.gelu(yc + d * u), yc_, u_, q[0])
        dyc_, du_, dd_ = vjp(da_ + dge_b)
        return [dyc_, du_], [dd_]

    dyc, du5_a, g["s5_d"] = _matmul(
        "mm_d_ge", dtg, w["s5_w_glu"], tb=True, epilogue=s5_out_bwd_b,
        epi_rows=[(yc, 0), (proj, OFF_U5), (dge_a, 0)], epi_pars=[s5_d], epi_outs=[BF16, F32],
        epi_accs=[(1, S5_WIDTH)], full_rows=True, dep=early)
    dproj, d_lam, d_bb_band, d_c_band = _s5_bwd(dyc, s, proj, du5_a, bb_band, c_band, lam_pow_conj, dproj,
                                                n_seq, seq_len)

    d_lr, d_li = _state_uncols(d_lam)
    d_bbr, d_bbi = _band_take(d_bb_band)
    d_are, d_aim, d_ls, d_br2, d_bi2 = _s5_params_bwd(
        a_re, a_im, log_step, b_re2, b_im2, expand,
        d_lr.reshape(S5_GROUPS, S5_STATE), d_li.reshape(S5_GROUPS, S5_STATE), d_bbr, d_bbi)
    g["s5_a_re"], g["s5_a_im"], g["s5_log_step"] = d_are, d_aim, d_ls
    g["s5_b_re_ghp"], g["s5_b_im_ghp"] = d_br2, d_bi2
    d_cr, d_ci = _band_take(d_c_band)
    g["s5_c_re"], g["s5_c_im"] = d_cr, -d_ci

    dxs, dbm, dcm, dproj, ddt, d_dtb, d_alog, d_dsk, g["ssd_norm_w"] = _ssd_bwd(
        xbc_act, proj, states, dyss, dtb, alog, dsk, ssd_norm_w, n_seq, seq_len, dproj)
    (dproj,) = row("ssd_ddt", lambda r, q: ([r[0]], []), row_ins=[(ddt, 0, DT_W)], par_ins=[],
                   row_outs=[(DT_W, BF16)], into=(dproj, OFF_DT))
    g["ssd_dt_bias"], g["ssd_a_log"], g["ssd_d"] = _unpad_heads(d_dtb), _unpad_heads(d_alog), _unpad_heads(d_dsk)
    conv_dw, conv_db = [], []
    for nm, d_act, off in (("x", dxs, 0), ("b", dbm, SSD_WIDTH), ("c", dcm, SSD_WIDTH + SSD_BC)):
        dproj, dw_, db_ = _conv_bwd("ssd_conv_bwd_" + nm, proj, d_act, conv_w, conv_b, n_rows, seq_len, off, dproj)
        conv_dw.append(dw_)
        conv_db.append(db_)
    g["ssd_conv_w"] = jnp.concatenate(conv_dw, axis=1)
    g["ssd_conv_b"] = jnp.concatenate(conv_db, axis=1)

    g["w_in_t"] = _unpad_w_in_t(_matmul("mm_d_w_in", dproj, hn, ta=True))
    def norm_bwd(dhn_, r, q):
        x_, dh1_ = r
        _, vjp = jax.vjp(_rms, x_, q[0])
        dx_, dw_ = vjp(dhn_)
        return [dx_ + dh1_], [dw_]

    sent = w["_w_in_ready"](g, loss) if "_w_in_ready" in w else None
    dx, g["norm_w"] = _matmul("mm_d_hn", dproj, w_pad_t, epilogue=norm_bwd, epi_rows=[(x2, 0), (dh1, 0)],
                              epi_pars=[norm_w], epi_outs=[F32], epi_accs=[(1, D_MODEL)], full_rows=True, dep=sent)
    return loss, dx.reshape(x.shape), g


HBM = pl.BlockSpec(memory_space=pltpu.HBM)


def _chip_index(x, y):
    return 2 * x + y


def _half(shape2d, axis, which):
    h = shape2d[axis] // 2
    sl = pl.ds(pl.multiple_of(which * h, 128 if axis else 8), h)
    return (slice(None), sl) if axis else (sl, slice(None))


def _gather_chips(split, axes, whole):
    ns, nw = len(split), len(whole)
    n = ns + nw

    def body(*refs):
        ins, outs = refs[:n], refs[n:2 * n]
        ici_send, ici_recv, d2d_send, d2d_recv, local_sems = refs[2 * n:]
        x, y, c = lax.axis_index("x"), lax.axis_index("y"), lax.axis_index("c")
        me = _chip_index(x, y)
        sibling = (x, y, 1 - c)
        peers = [(1 - x, y), (x, 1 - y), (1 - x, 1 - y)]

        def half(t, which):
            return _half(split[t].shape, axes[t], which)

        copies = []
        for t in range(n):
            loc = pltpu.make_async_copy(ins[t], outs[t].at[me], local_sems.at[t])
            loc.start()
            copies.append(loc)

        def ici(t, k, slot):
            px, py = peers[k]
            if t < ns:
                src, dst = ins[t].at[half(t, c)], outs[t].at[(slot,) + half(t, c)]
            else:
                src, dst = ins[t], outs[t].at[slot]
            return pltpu.make_async_remote_copy(src_ref=src, dst_ref=dst, send_sem=ici_send.at[t, k],
                                                recv_sem=ici_recv.at[t, k], device_id=(px, py, c), device_id_type=MESH)

        def d2d(t, k, which):
            rows = outs[t].at[(_chip_index(*peers[k]),) + half(t, which)]
            return pltpu.make_async_remote_copy(src_ref=rows, dst_ref=rows, send_sem=d2d_send.at[t, k],
                                                recv_sem=d2d_recv.at[t, k], device_id=sibling, device_id_type=MESH)

        sends = []
        for t in range(n):
            for k in range(3):
                cp = ici(t, k, me)
                cp.start()
                sends.append(cp)
        for t in range(n):
            for k in range(3):
                ici(t, k, _chip_index(*peers[k])).wait_recv()
                if t < ns:
                    cp = d2d(t, k, c)
                    cp.start()
                    sends.append(cp)
        for t in range(ns):
            for k in range(3):
                d2d(t, k, 1 - c).wait_recv()
        for cp in sends:
            cp.wait_send()
        for cp in copies:
            cp.wait()

    arrays = list(split) + list(whole)
    return pl.pallas_call(
        body, name="gather_weights",
        in_specs=[HBM] * n, out_specs=[HBM] * n,
        out_shape=[jax.ShapeDtypeStruct((N_CHIPS,) + a.shape, a.dtype) for a in arrays],
        scratch_shapes=[pltpu.SemaphoreType.DMA((n, 3)), pltpu.SemaphoreType.DMA((n, 3)),
                        pltpu.SemaphoreType.DMA((ns, 3)), pltpu.SemaphoreType.DMA((ns, 3)),
                        pltpu.SemaphoreType.DMA((n,))],
    )(*arrays)


SEM = pl.BlockSpec(memory_space=pltpu.SEMAPHORE)
DATAFLOW = pltpu.SideEffectType.DATAFLOW_SIDE_EFFECTING


def _gather_start(shards, after):
    n = len(shards)

    def body(*refs):
        ins, lands = refs[:n], refs[n:2 * n]
        send_sems, recv_sems = refs[2 * n + 1], refs[2 * n + 2]
        token = refs[-1]
        x, y, c = lax.axis_index("x"), lax.axis_index("y"), lax.axis_index("c")
        me = _chip_index(x, y)
        for t in range(n):
            for k, (px, py) in enumerate([(1 - x, y), (x, 1 - y), (1 - x, 1 - y)]):
                pltpu.make_async_remote_copy(
                    src_ref=ins[t], dst_ref=lands[t].at[me], send_sem=send_sems.at[3 * t + k],
                    recv_sem=recv_sems.at[3 * t + k],
                    device_id=(px, py, c), device_id_type=MESH).start()
        token[...] = jnp.zeros_like(token)

    zones = [lax.empty((N_CHIPS,) + a.shape, a.dtype) for a in shards]
    res = pl.pallas_call(
        body, name="gather_rest_start",
        out_shape=(pltpu.SemaphoreType.DMA((3 * n,)), pltpu.SemaphoreType.DMA((3 * n,)),
                   *[pltpu.HBM(a.shape, a.dtype) for a in shards], *[pltpu.HBM(z.shape, z.dtype) for z in zones],
                   jax.ShapeDtypeStruct((8, 128), F32)),
        in_specs=[HBM] * (2 * n) + [pl.BlockSpec(memory_space=pl.ANY)],
        out_specs=(SEM, SEM, *[HBM] * (2 * n), pl.BlockSpec(memory_space=pltpu.VMEM)),
        input_output_aliases={t: 2 + t for t in range(2 * n)},
        compiler_params=pltpu.CompilerParams(has_side_effects=DATAFLOW),
    )(*[pltpu.with_memory_space_constraint(a, pltpu.HBM) for a in shards],
      *[pltpu.with_memory_space_constraint(z, pltpu.HBM) for z in zones], after)
    return res[0], res[1], list(res[2:2 + n]), list(res[2 + n:2 + 2 * n]), res[-1]


def _gather_wait(send_sems, recv_sems, thru, lands, after):
    n = len(thru)

    def body(*refs):
        ins, zones = refs[:n], refs[n:2 * n]
        s_sems, r_sems = refs[2 * n], refs[2 * n + 1]
        x, y, c = lax.axis_index("x"), lax.axis_index("y"), lax.axis_index("c")
        for t in range(n):
            for k, (px, py) in enumerate([(1 - x, y), (x, 1 - y), (1 - x, 1 - y)]):
                cp = pltpu.make_async_remote_copy(
                    src_ref=ins[t], dst_ref=zones[t].at[_chip_index(px, py)], send_sem=s_sems.at[3 * t + k],
                    recv_sem=r_sems.at[3 * t + k], device_id=(px, py, c), device_id_type=MESH)
                cp.wait_send()
                cp.wait_recv()

    res = pl.pallas_call(
        body, name="gather_rest_wait",
        out_shape=(*[pltpu.HBM(a.shape, a.dtype) for a in thru], *[pltpu.HBM(z.shape, z.dtype) for z in lands]),
        in_specs=[HBM] * (2 * n) + [SEM, SEM, pl.BlockSpec(memory_space=pl.ANY)], out_specs=[HBM] * (2 * n),
        input_output_aliases={t: t for t in range(2 * n)},
        compiler_params=pltpu.CompilerParams(has_side_effects=DATAFLOW),
    )(*thru, *lands, send_sems, recv_sems, after)
    return list(res[:n]), list(res[n:])


def _scatter_ends(t, k, n_slotted, ins, zones, x, y, me):
    px, py = [(1 - x, y), (x, 1 - y), (1 - x, 1 - y)][k]
    if t < n_slotted:
        return ins[t].at[_chip_index(px, py)], zones[t].at[k], (px, py)
    return ins[t], (zones[t].at[me], zones[t].at[_chip_index(px, py)]), (px, py)


def _scatter_start(name, slotted, whole, after):
    n_s = len(slotted)
    arrays = list(slotted) + list(whole)
    n = len(arrays)

    def body(*refs):
        ins, lands = refs[:n], refs[n:2 * n]
        send_sems, recv_sems = refs[2 * n + 1], refs[2 * n + 2]
        token = refs[-1]
        x, y, c = lax.axis_index("x"), lax.axis_index("y"), lax.axis_index("c")
        for t in range(n):
            for k in range(3):
                src, dst, (px, py) = _scatter_ends(t, k, n_s, ins, lands, x, y, _chip_index(x, y))
                pltpu.make_async_remote_copy(
                    src_ref=src, dst_ref=dst if t < n_s else dst[0], send_sem=send_sems.at[3 * t + k],
                    recv_sem=recv_sems.at[3 * t + k], device_id=(px, py, c), device_id_type=MESH).start()
        token[...] = jnp.zeros_like(token)

    zones = [lax.empty((3,) + a.shape[1:], a.dtype) for a in slotted]
    zones += [lax.empty((N_CHIPS,) + a.shape, a.dtype) for a in whole]
    slotted = arrays
    res = pl.pallas_call(
        body, name=name,
        out_shape=(pltpu.SemaphoreType.DMA((3 * n,)), pltpu.SemaphoreType.DMA((3 * n,)),
                   *[pltpu.HBM(a.shape, a.dtype) for a in slotted], *[pltpu.HBM(z.shape, z.dtype) for z in zones],
                   jax.ShapeDtypeStruct((8, 128), F32)),
        in_specs=[HBM] * (2 * n) + [pl.BlockSpec(memory_space=pl.ANY)],
        out_specs=(SEM, SEM, *[HBM] * (2 * n), pl.BlockSpec(memory_space=pltpu.VMEM)),
        input_output_aliases={t: 2 + t for t in range(2 * n)},
        compiler_params=pltpu.CompilerParams(has_side_effects=DATAFLOW),
    )(*[pltpu.with_memory_space_constraint(a, pltpu.HBM) for a in slotted],
      *[pltpu.with_memory_space_constraint(z, pltpu.HBM) for z in zones], after)
    return res[0], res[1], list(res[2:2 + n]), list(res[2 + n:2 + 2 * n]), res[-1]


def _scatter_wait(name, n_slotted, send_sems, recv_sems, thru, lands, after):
    n = len(thru)

    def body(*refs):
        ins, zones = refs[:n], refs[n:2 * n]
        s_sems, r_sems = refs[2 * n], refs[2 * n + 1]
        x, y, c = lax.axis_index("x"), lax.axis_index("y"), lax.axis_index("c")
        for t in range(n):
            for k in range(3):
                src, dst, (px, py) = _scatter_ends(t, k, n_slotted, ins, zones, x, y, _chip_index(x, y))
                cp = pltpu.make_async_remote_copy(
                    src_ref=src, dst_ref=dst if t < n_slotted else dst[1], send_sem=s_sems.at[3 * t + k],
                    recv_sem=r_sems.at[3 * t + k], device_id=(px, py, c), device_id_type=MESH)
                cp.wait_send()
                cp.wait_recv()

    res = pl.pallas_call(
        body, name=name,
        out_shape=(*[pltpu.HBM(a.shape, a.dtype) for a in thru], *[pltpu.HBM(z.shape, z.dtype) for z in lands]),
        in_specs=[HBM] * (2 * n) + [SEM, SEM, pl.BlockSpec(memory_space=pl.ANY)], out_specs=[HBM] * (2 * n),
        input_output_aliases={t: t for t in range(2 * n)},
        compiler_params=pltpu.CompilerParams(has_side_effects=DATAFLOW),
    )(*thru, *lands, send_sems, recv_sems, after)
    return list(res[:n]), list(res[n:])


def _sum_chips(name, slotted, recv, idx, dep):
    _, r, c = slotted.shape
    tr = _row_tile(r, 5 * c * 4)

    def body(idx_ref, own_ref, r_ref, _, o_ref):
        acc = own_ref[0]
        for k in range(3):
            acc = acc + r_ref[k]
        o_ref[...] = acc

    grid_spec = pltpu.PrefetchScalarGridSpec(
        num_scalar_prefetch=1, grid=(r // tr,),
        in_specs=[pl.BlockSpec((1, tr, c), lambda i, s: (s[1], i, 0)), pl.BlockSpec((3, tr, c), lambda i, s: (0, i, 0)),
                  pl.BlockSpec(memory_space=pl.ANY)],
        out_specs=pl.BlockSpec((tr, c), lambda i, s: (i, 0)))
    return pl.pallas_call(
        body, name=name, grid_spec=grid_spec, out_shape=jax.ShapeDtypeStruct((r, c), F32),
        compiler_params=_cparams(("parallel",)),
    )(idx, slotted, recv, dep)


def _half_shape(shape2d, axis):
    r, c = shape2d
    return (r, c // 2) if axis else (r // 2, c)


def _swap_halves(slotted, axes, small):
    n = len(slotted)

    def body(*refs):
        ins, sm_in = refs[:n], refs[n]
        outs, sm_out = refs[n + 1:2 * n + 1], refs[2 * n + 1]
        send_sems, recv_sems, local_sem = refs[2 * n + 2:]
        x, y, c = lax.axis_index("x"), lax.axis_index("y"), lax.axis_index("c")
        local = pltpu.make_async_copy(sm_in, sm_out.at[c], local_sem)
        local.start()
        sends = []
        for t in range(n):
            other = (slice(None),) + _half(slotted[t].shape[1:], axes[t], 1 - c)
            cp = pltpu.make_async_remote_copy(
                src_ref=ins[t].at[other], dst_ref=outs[t], send_sem=send_sems.at[t],
                recv_sem=recv_sems.at[t], device_id=(x, y, 1 - c), device_id_type=MESH)
            cp.start()
            sends.append(cp)
        cp = pltpu.make_async_remote_copy(
            src_ref=sm_in, dst_ref=sm_out.at[c], send_sem=send_sems.at[n], recv_sem=recv_sems.at[n],
            device_id=(x, y, 1 - c), device_id_type=MESH)
        cp.start()
        sends.append(cp)
        for cp in sends[:n]:
            cp.wait_recv()
        pltpu.make_async_remote_copy(
            src_ref=sm_in, dst_ref=sm_out.at[1 - c], send_sem=send_sems.at[n], recv_sem=recv_sems.at[n],
            device_id=(x, y, 1 - c), device_id_type=MESH).wait_recv()
        for cp in sends:
            cp.wait_send()
        local.wait()

    return pl.pallas_call(
        body, name="swap_halves",
        in_specs=[HBM] * (n + 1), out_specs=[HBM] * (n + 1),
        out_shape=[jax.ShapeDtypeStruct((a.shape[0],) + _half_shape(a.shape[1:], ax), a.dtype)
                   for a, ax in zip(slotted, axes, strict=True)]
        + [jax.ShapeDtypeStruct((2,) + small.shape, small.dtype)],
        scratch_shapes=[pltpu.SemaphoreType.DMA((n + 1,)), pltpu.SemaphoreType.DMA((n + 1,)), pltpu.SemaphoreType.DMA],
    )(*slotted, small)


def _tiling(r, c, f32_per_elem):
    if r % 8 == 0:
        tr = _row_tile(r, f32_per_elem * c * 4)
        return r // tr, (tr, c), lambda i: (i, 0)
    assert c % 128 == 0, (r, c)
    return c // 128, (r, 128), lambda i: (0, i)


def _pair_sum(name, slotted, other, idx, axis):
    _, r, c = slotted.shape
    hr, hc = _half_shape((r, c), axis)
    n, (tr, tc), at = _tiling(hr, hc, 13)
    if axis == 0:
        a = slotted.reshape(N_CHIPS, 2, hr, c)
        a_all = pl.BlockSpec((N_CHIPS, 1, tr, tc), lambda i, s: (0, s[0]) + at(i))
        a_own = pl.BlockSpec((1, 1, tr, tc), lambda i, s: (s[1], s[0]) + at(i))
    else:
        a, per_half = slotted, hc // tc
        a_all = pl.BlockSpec((N_CHIPS, tr, tc), lambda i, s: (0, at(i)[0], s[0] * per_half + at(i)[1]))
        a_own = pl.BlockSpec((1, tr, tc), lambda i, s: (s[1], at(i)[0], s[0] * per_half + at(i)[1]))

    def body(idx_ref, a_ref, b_ref, am_ref, bm_ref, p_ref, own_ref):
        mine, mine_own = (a_ref[:, 0], am_ref[0, 0]) if axis == 0 else (a_ref[...], am_ref[0])
        p_ref[...] = (mine + b_ref[...]).astype(p_ref.dtype)
        own_ref[...] = mine_own + bm_ref[0]

    grid_spec = pltpu.PrefetchScalarGridSpec(
        num_scalar_prefetch=1, grid=(n,),
        in_specs=[a_all, pl.BlockSpec((N_CHIPS, tr, tc), lambda i, s: (0,) + at(i)),
                  a_own, pl.BlockSpec((1, tr, tc), lambda i, s: (s[1],) + at(i))],
        out_specs=[pl.BlockSpec((N_CHIPS, tr, tc), lambda i, s: (0,) + at(i)),
                   pl.BlockSpec((tr, tc), lambda i, s: at(i))])
    return pl.pallas_call(
        body, name=name, grid_spec=grid_spec,
        out_shape=[jax.ShapeDtypeStruct((N_CHIPS, hr, hc), BF16), jax.ShapeDtypeStruct((hr, hc), F32)],
        compiler_params=_cparams(("parallel",)),
    )(idx, a, other, a, other)


def _sum_parts(name, own, recv):
    h, c = own.shape
    n, (tr, tc), at = _tiling(h, c, 4)

    def body(o_ref, r_ref, out_ref):
        acc = o_ref[...]
        for k in range(3):
            acc = acc + r_ref[k].astype(F32)
        out_ref[...] = acc

    return pl.pallas_call(
        body, name=name, grid=(n,),
        in_specs=[pl.BlockSpec((tr, tc), at), pl.BlockSpec((3, tr, tc), lambda i: (0,) + at(i))],
        out_specs=pl.BlockSpec((tr, tc), at),
        out_shape=jax.ShapeDtypeStruct((h, c), F32),
        compiler_params=_cparams(("parallel",)),
    )(own, recv)


def _all_to_all(name, small):
    def body(sm_in, sm_out, send_sems, recv_sems, local_sem):
        x, y, c = lax.axis_index("x"), lax.axis_index("y"), lax.axis_index("c")
        dev = 4 * x + 2 * y + c
        local = pltpu.make_async_copy(sm_in, sm_out.at[dev], local_sem)
        local.start()
        rel = [(fx, fy, fc) for fx in (0, 1) for fy in (0, 1) for fc in (0, 1)][1:]
        sends = []
        for k, (fx, fy, fc) in enumerate(rel):
            cp = pltpu.make_async_remote_copy(
                src_ref=sm_in, dst_ref=sm_out.at[dev], send_sem=send_sems.at[k], recv_sem=recv_sems.at[k],
                device_id=(x ^ fx, y ^ fy, c ^ fc), device_id_type=MESH)
            cp.start()
            sends.append(cp)
        for k, (fx, fy, fc) in enumerate(rel):
            src_dev = 4 * (x ^ fx) + 2 * (y ^ fy) + (c ^ fc)
            pltpu.make_async_remote_copy(
                src_ref=sm_in, dst_ref=sm_out.at[src_dev], send_sem=send_sems.at[k], recv_sem=recv_sems.at[k],
                device_id=(x ^ fx, y ^ fy, c ^ fc), device_id_type=MESH).wait_recv()
        for cp in sends:
            cp.wait_send()
        local.wait()

    return pl.pallas_call(
        body, name=name, in_specs=[HBM], out_specs=HBM,
        out_shape=jax.ShapeDtypeStruct((N_DEV,) + small.shape, small.dtype),
        scratch_shapes=[pltpu.SemaphoreType.DMA((7,)), pltpu.SemaphoreType.DMA((7,)), pltpu.SemaphoreType.DMA],
    )(small)


def _swap_sibling(name, parts):
    n = len(parts)

    def body(*refs):
        ins, outs = refs[:n], refs[n:2 * n]
        send_sems, recv_sems = refs[2 * n:]
        x, y, c = lax.axis_index("x"), lax.axis_index("y"), lax.axis_index("c")
        cps = []
        for t in range(n):
            cp = pltpu.make_async_remote_copy(
                src_ref=ins[t], dst_ref=outs[t], send_sem=send_sems.at[t], recv_sem=recv_sems.at[t],
                device_id=(x, y, 1 - c), device_id_type=MESH)
            cp.start()
            cps.append(cp)
        for cp in cps:
            cp.wait_recv()
        for cp in cps:
            cp.wait_send()

    return pl.pallas_call(
        body, name=name,
        in_specs=[HBM] * n, out_specs=[HBM] * n,
        out_shape=[jax.ShapeDtypeStruct(a.shape, a.dtype) for a in parts],
        scratch_shapes=[pltpu.SemaphoreType.DMA((n,)), pltpu.SemaphoreType.DMA((n,))],
    )(*parts)


def _sum_slots(name, a):
    k, r, c = a.shape
    tr = _row_tile(r, (k + 1) * c * 4)

    def body(a_ref, o_ref):
        acc = a_ref[0]
        for i in range(1, k):
            acc = acc + a_ref[i]
        o_ref[...] = acc

    return pl.pallas_call(
        body, name=name, grid=(r // tr,),
        in_specs=[pl.BlockSpec((k, tr, c), lambda i: (0, i, 0))],
        out_specs=pl.BlockSpec((tr, c), lambda i: (i, 0)),
        out_shape=jax.ShapeDtypeStruct((r, c), a.dtype),
        compiler_params=_cparams(("parallel",)),
    )(a)


def _adamw(name, w, m, v, g_parts):
    r, c = w.shape
    ng = len(g_parts)
    tr = _row_tile(r, (7 + ng) * c * 4)
    c1 = 1.0 - ADAM_B1 ** ADAM_STEP
    c2 = 1.0 - ADAM_B2 ** ADAM_STEP

    def body(*refs):
        w_ref, m_ref, v_ref = refs[:3]
        g_refs = refs[3:3 + ng]
        go_ref, d_ref, mo_ref, vo_ref = refs[3 + ng:]
        g = g_refs[0][...]
        for gr in g_refs[1:]:
            g = g + gr[...]
        m_new = ADAM_B1 * m_ref[...] + (1.0 - ADAM_B1) * g
        v_new = ADAM_B2 * v_ref[...] + (1.0 - ADAM_B2) * (g * g)
        go_ref[...] = g
        mo_ref[...] = m_new
        vo_ref[...] = v_new
        d_ref[...] = -ADAM_LR * ((m_new / c1) / (jnp.sqrt(v_new / c2) + ADAM_EPS) + ADAM_WD * w_ref[...])

    spec = pl.BlockSpec((tr, c), lambda i: (i, 0))
    return pl.pallas_call(
        body, name=name, grid=(r // tr,),
        in_specs=[spec] * (3 + ng), out_specs=[spec] * 4,
        out_shape=[jax.ShapeDtypeStruct((r, c), F32)] * 4,
        compiler_params=_cparams(("parallel",)),
    )(w, m, v, *g_parts)


def _adamw_halves(name, w, m, v, own, other, idx, axis):
    hr, hc = own.shape
    nb, (tr, tc), at = _tiling(hr, hc, 9)
    c1 = 1.0 - ADAM_B1 ** ADAM_STEP
    c2 = 1.0 - ADAM_B2 ** ADAM_STEP

    def body(idx_ref, w_ref, m_ref, v_ref, own_ref, oth_ref, go_ref, d_ref, mo_ref, vo_ref):
        g = jnp.where(pl.program_id(0) == idx_ref[0], own_ref[...], oth_ref[...])
        m_new = ADAM_B1 * m_ref[...] + (1.0 - ADAM_B1) * g
        v_new = ADAM_B2 * v_ref[...] + (1.0 - ADAM_B2) * (g * g)
        go_ref[...] = g
        mo_ref[...] = m_new
        vo_ref[...] = v_new
        d_ref[...] = -ADAM_LR * ((m_new / c1) / (jnp.sqrt(v_new / c2) + ADAM_EPS) + ADAM_WD * w_ref[...])

    per_half = (hc // tc) if axis else (hr // tr)
    if axis:
        full = pl.BlockSpec((tr, tc), lambda hh, i, s: (at(i)[0], hh * per_half + at(i)[1]))
    else:
        full = pl.BlockSpec((tr, tc), lambda hh, i, s: (hh * per_half + at(i)[0], at(i)[1]))
    part = pl.BlockSpec((tr, tc), lambda hh, i, s: at(i))
    grid_spec = pltpu.PrefetchScalarGridSpec(
        num_scalar_prefetch=1, grid=(2, nb), in_specs=[full, full, full, part, part], out_specs=[full] * 4)
    return pl.pallas_call(
        body, name=name, grid_spec=grid_spec, out_shape=[jax.ShapeDtypeStruct(w.shape, F32)] * 4,
        compiler_params=_cparams(("parallel", "parallel")),
    )(idx, w, m, v, own, other)


WEIGHTS = ['norm_w', 'w_in', 's5_a_re', 's5_a_im', 's5_b_re', 's5_b_im', 's5_c_re', 's5_c_im', 's5_d', 's5_log_step',
           's5_w_glu', 's5_b_glu', 'ssd_conv_w', 'ssd_conv_b', 'ssd_dt_bias', 'ssd_a_log', 'ssd_d', 'ssd_norm_w',
           'w_br_s5', 'w_br_ssd', 'w_out', 'ple_norm_w', 'w_ple_gate', 'w_ple_proj', 'final_norm_w']
SHARDED = {'w_in': ((IN_PROJ_DIM, 1024), 0), 's5_w_glu': ((512, 512), 0), 'ssd_conv_w': ((SSD_CONV, SSD_CONV_DIM), 1),
           'w_br_s5': ((512, 1024), 1), 'w_br_ssd': ((1536, 1024), 0), 'w_out': ((1024, 1024), 0),
           'w_ple_gate': ((1024, 1024), 0), 'w_ple_proj': ((256, 1024), 1)}
TRANSPOSED = ('w_in',)
SMALL = [n for n in WEIGHTS if n not in SHARDED]


def _shard_shape(name):
    (r, c), ax = SHARDED[name]
    return (r // N_CHIPS, c) if ax == 0 else (r, c // N_CHIPS)


def _half_axis(name):
    return 0 if (_shard_shape(name)[0] // 2) % 16 == 0 else 1


def _shard2d(name, a):
    r, c = _shard_shape(name)
    return a.reshape(c, r).T if name in TRANSPOSED else a.reshape(r, c)


def _unshard2d(name, a2, shape):
    return (a2.T if name in TRANSPOSED else a2).reshape(shape)


def _unslot(name, a4):
    (r, c), ax = SHARDED[name]
    if ax == 0:
        return a4.reshape(r, c)
    return jnp.transpose(a4, (1, 0, 2)).reshape(r, c)


def _slot(name, full):
    (r, c), ax = SHARDED[name]
    if ax == 0:
        return full.reshape(N_CHIPS, r // N_CHIPS, c)
    return jnp.transpose(full.reshape(r, N_CHIPS, c // N_CHIPS), (1, 0, 2))


GHP = ('s5_b_re', 's5_b_im')


def _view_shape(name):
    if name in ('s5_a_re', 's5_a_im'):
        return (S5_GROUPS, S5_STATE)
    if name in GHP + ('s5_c_re', 's5_c_im'):
        return (S5_GROUPS, S5_GROUP, S5_STATE)
    if name == 'ssd_conv_w':
        return (SSD_CONV, SSD_CONV_DIM // N_CHIPS)
    return (1, {'s5_log_step': S5_GROUPS, 'ssd_conv_b': SSD_CONV_DIM, 'ssd_norm_w': SSD_WIDTH, 's5_d': S5_WIDTH,
                's5_b_glu': S5_WIDTH, 'ssd_dt_bias': SSD_HEADS, 'ssd_a_log': SSD_HEADS, 'ssd_d': SSD_HEADS}.get(name, D_MODEL))


def _view(name, a):
    if name in GHP:
        return jnp.swapaxes(a.reshape(S5_GROUPS, S5_STATE, S5_GROUP), 1, 2)
    return a.reshape(_view_shape(name))


def _unview(name, a, shape):
    return (jnp.swapaxes(a, 1, 2) if name in GHP else a).reshape(shape)


def _adamw_small(ws, ms, vs, gs):
    n = len(ws)
    c1 = 1.0 - ADAM_B1 ** ADAM_STEP
    c2 = 1.0 - ADAM_B2 ** ADAM_STEP

    def body(*refs):
        w_r, m_r, v_r, g_r = (refs[k * n:(k + 1) * n] for k in range(4))
        d_o, m_o, v_o = (refs[k * n:(k + 1) * n] for k in range(4, 7))
        for i in range(n):
            g = g_r[i][...]
            m_new = ADAM_B1 * m_r[i][...] + (1.0 - ADAM_B1) * g
            v_new = ADAM_B2 * v_r[i][...] + (1.0 - ADAM_B2) * (g * g)
            m_o[i][...] = m_new
            v_o[i][...] = v_new
            d_o[i][...] = -ADAM_LR * ((m_new / c1) / (jnp.sqrt(v_new / c2) + ADAM_EPS) + ADAM_WD * w_r[i][...])

    return pl.pallas_call(
        body, name="adamw_small", out_shape=[jax.ShapeDtypeStruct(w.shape, F32) for w in ws] * 3,
        compiler_params=pltpu.CompilerParams(vmem_limit_bytes=VMEM_LIMIT),
    )(*ws, *ms, *vs, *gs)


def _pack_small(vals):
    flat = jnp.concatenate([v.reshape(-1).astype(F32) for v in vals])
    rows = -(-flat.shape[0] // (256 * 128)) * 256
    return jnp.pad(flat, (0, rows * 128 - flat.shape[0])).reshape(rows, 128)


def _unpack_small(packed, shapes):
    flat = packed.reshape(-1)
    out, off = [], 0
    for sh in shapes:
        n = math.prod(sh)
        out.append(flat[off:off + n].reshape(sh))
        off += n
    return out


def kernel(x, p, norm_w, w_in, s5_a_re, s5_a_im, s5_b_re, s5_b_im, s5_c_re, s5_c_im, s5_d, s5_log_step, s5_w_glu, s5_b_glu, ssd_conv_w, ssd_conv_b, ssd_dt_bias, ssd_a_log, ssd_d, ssd_norm_w, w_br_s5, w_br_ssd, w_out, ple_norm_w, w_ple_gate, w_ple_proj, final_norm_w, loss_target, m_norm_w, m_w_in, m_s5_a_re, m_s5_a_im, m_s5_b_re, m_s5_b_im, m_s5_c_re, m_s5_c_im, m_s5_d, m_s5_log_step, m_s5_w_glu, m_s5_b_glu, m_ssd_conv_w, m_ssd_conv_b, m_ssd_dt_bias, m_ssd_a_log, m_ssd_d, m_ssd_norm_w, m_w_br_s5, m_w_br_ssd, m_w_out, m_ple_norm_w, m_w_ple_gate, m_w_ple_proj, m_final_norm_w, v_norm_w, v_w_in, v_s5_a_re, v_s5_a_im, v_s5_b_re, v_s5_b_im, v_s5_c_re, v_s5_c_im, v_s5_d, v_s5_log_step, v_s5_w_glu, v_s5_b_glu, v_ssd_conv_w, v_ssd_conv_b, v_ssd_dt_bias, v_ssd_a_log, v_ssd_d, v_ssd_norm_w, v_w_br_s5, v_w_br_ssd, v_w_out, v_ple_norm_w, v_w_ple_gate, v_w_ple_proj, v_final_norm_w):
    args = locals()
    wl = {n: args[n] for n in WEIGHTS}
    ml = {n: args["m_" + n] for n in WEIGHTS}
    vl = {n: args["v_" + n] for n in WEIGHTS}
    big = [n for n in SHARDED if n != 'ssd_conv_w']
    chip = _chip_index(lax.axis_index("x"), lax.axis_index("y"))
    idx = jnp.stack([lax.axis_index("c"), chip]).astype(jnp.int32)

    axes = [_half_axis(n) for n in big]
    first = ['w_in']
    rest = [n for n in big if n not in first]
    bf_shards = {n: _shard2d(n, wl[n]).astype(BF16) for n in big}
    w_in_t, conv_w4 = _gather_chips([bf_shards[n] for n in first], [_half_axis(n) for n in first],
                                    [_shard2d('ssd_conv_w', wl['ssd_conv_w'])])
    full = {n: wl[n] for n in SMALL}
    full["w_in_t"] = w_in_t
    full['ssd_conv_w'] = _unslot('ssd_conv_w', conv_w4)
    send_sems, recv_sems, thru, lands, token = _gather_start([bf_shards[n] for n in rest], w_in_t)

    def fetch_rest(after):
        own, zones = _gather_wait(send_sems, recv_sems, thru, lands, after)
        return {n: _unslot(n, lax.dynamic_update_slice(z, o[None], (chip, 0, 0)))
                for n, o, z in zip(rest, own, zones, strict=True)}

    full["_late"] = (token, fetch_rest)

    in_flight = []

    def send_rest(g_now):
        in_flight.extend(_scatter_start("scatter_rest_start", [_slot(n, g_now[n]) for n in rest], [], g_now['s5_w_glu']))
        return in_flight[4]

    full["_early"] = send_rest

    packed = [n for n in SMALL if n != 'norm_w']
    first_axes = [_half_axis(n) for n in first]
    st = {}

    def send_first(g_now, loss_now):
        small_pack = _pack_small([loss_now] + [g_now[n + "_ghp" if n in GHP else n] for n in packed]
                                 + [g_now['ssd_conv_w']])
        swapped = _swap_halves([g_now["w_in_t"]], first_axes, small_pack)
        st["pair"] = [_pair_sum("pair_sum_w_in", g_now["w_in_t"], swapped[0], idx, first_axes[0])]
        small_chip = _sum_slots("sum_small_pair", swapped[-1])
        half_rows = small_chip.shape[0] // 2
        my_half = lax.dynamic_slice(small_chip, (lax.axis_index("c") * half_rows, 0), (half_rows, small_chip.shape[1]))
        st["flight"] = _scatter_start("scatter_first_start", [pb for pb, _ in st["pair"]], [my_half], small_chip)
        return st["flight"][4]

    full["_w_in_ready"] = send_first
    loss, grad_x, g = _local_step(x, p[0], loss_target, full)
    pair = st["pair"]
    s_sems, r_sems, thru, lands, tok = st["flight"]
    small_shapes = [(1, 1)] + [_view_shape(n) for n in packed] + [(SSD_CONV, SSD_CONV_DIM)]
    norm_all = _all_to_all("norm_w_all_to_all", g['norm_w'].reshape(8, 128))
    norm_g = _sum_slots("sum_norm_w", norm_all).reshape(1, D_MODEL)

    out_g, out_d, out_m, out_v = {}, {}, {}, {}
    own_slots, zones = _scatter_wait("scatter_rest_wait", len(rest), *in_flight[:4], norm_g)
    chip_sums = [_sum_chips("sum_chips_" + n, a, z, idx, tok) for n, a, z in zip(rest, own_slots, zones, strict=True)]
    sib_sums = _swap_sibling("swap_sibling_rest", chip_sums)
    for n, own, sib in zip(rest, chip_sums, sib_sums, strict=True):
        res = _adamw("adamw_" + n, _shard2d(n, wl[n]), _shard2d(n, ml[n]), _shard2d(n, vl[n]), [own, sib])
        out_g[n], out_d[n], out_m[n], out_v[n] = (_unshard2d(n, r, wl[n].shape) for r in res)

    sent, got = _scatter_wait("scatter_first_wait", len(first), s_sems, r_sems, thru, lands, out_d[rest[-1]])
    small_zone = lax.dynamic_update_slice(got[-1], sent[-1][None], (chip, 0, 0))
    small_half = _sum_slots("sum_small_chips", small_zone)
    halves = [_sum_parts("sum_chips_" + n, own, r) for n, (_, own), r in zip(first, pair, got[:-1], strict=True)]
    swapped2 = _swap_sibling("swap_sibling_first", halves + [small_half])
    other_halves = swapped2[:-1]
    c_is_0 = lax.axis_index("c") == 0
    small_sum = jnp.concatenate([jnp.where(c_is_0, small_half, swapped2[-1]),
                                 jnp.where(c_is_0, swapped2[-1], small_half)], axis=0)
    for n, own, oth, ax in zip(first, halves, other_halves, first_axes, strict=True):
        res = _adamw_halves("adamw_" + n, _shard2d(n, wl[n]), _shard2d(n, ml[n]), _shard2d(n, vl[n]), own, oth, idx, ax)
        out_g[n], out_d[n], out_m[n], out_v[n] = (_unshard2d(n, r, wl[n].shape) for r in res)
    sm = _unpack_small(small_sum, small_shapes)
    loss_total = sm[0].reshape(())
    conv_g = lax.dynamic_slice(sm[-1], (0, chip * (SSD_CONV_DIM // N_CHIPS)), (SSD_CONV, SSD_CONV_DIM // N_CHIPS))
    names = SMALL + ['ssd_conv_w']
    by_name = {**dict(zip(packed, sm[1:-1], strict=True)), 'norm_w': norm_g, 'ssd_conv_w': conv_g}
    grads = [by_name[n] for n in names]
    res = _adamw_small([_view(n, wl[n]) for n in names], [_view(n, ml[n]) for n in names],
                       [_view(n, vl[n]) for n in names], grads)
    for i, n in enumerate(names):
        out_g[n] = _unview(n, grads[i], wl[n].shape)
        out_d[n], out_m[n], out_v[n] = (_unview(n, res[k * len(names) + i], wl[n].shape) for k in range(3))

    return (loss_total, grad_x, *[out_g[n] for n in WEIGHTS], *[out_d[n] for n in WEIGHTS],
            *[out_m[n] for n in WEIGHTS], *[out_v[n] for n in WEIGHTS])
```

```python
import functools
import math

import jax
import jax.numpy as jnp
from jax import lax
from jax.experimental import pallas as pl
from jax.experimental.pallas import tpu as pltpu

F32 = jnp.float32
BF16 = jnp.bfloat16
MESH = pl.DeviceIdType.MESH

D_MODEL = 1024
PLE_DIM = 256
RMS_EPS = 1e-6
S5_WIDTH = 512
S5_GROUP = 16
S5_GROUPS = 32
S5_STATE = 64
S5_N = S5_GROUPS * S5_STATE
S5_LB = 512
S5_NJ = S5_N // S5_LB
S5_TB = 1024
S5_LOG_TB = 8
SSD_WIDTH = 1536
SSD_HEADDIM = 64
SSD_HEADS = 24
SSD_GROUPS = 4
SSD_HPG = 6
SSD_STATE = 128
SSD_CONV = 4
SSD_CHUNK = 128
SSD_BC = 512
SSD_CONV_DIM = 2560
GROUP_W = SSD_WIDTH // SSD_GROUPS
N_CHIPS = 4
N_DEV = 8

OFF_XBC, OFF_U5, OFF_Z5, OFF_DT, OFF_G5, OFF_GS, OFF_ZS = 0, 2560, 3072, 3584, 4096, 5120, 6144
DT_W = 512
PROJ_W = 7680
IN_PROJ_DIM = 7192

ADAM_LR, ADAM_B1, ADAM_B2, ADAM_EPS, ADAM_WD, ADAM_STEP = 0.001, 0.9, 0.999, 1e-08, 0.01, 10

VMEM_LIMIT = 56 * 1024 * 1024


ROW_BLOCK_BYTES = 8 * 1024 * 1024


def _row_tile(r, bytes_per_row):
    for t in (r, 4096, 2048, 1024, 512, 256, 128, 64, 32, 16, 8):
        if t <= r and r % t == 0 and t * bytes_per_row <= ROW_BLOCK_BYTES:
            return t
    return r


def _cparams(sem):
    return pltpu.CompilerParams(dimension_semantics=sem, vmem_limit_bytes=VMEM_LIMIT)


def _dg(a, b, ca, cb):
    return lax.dot_general(a.astype(BF16), b.astype(BF16), (((ca,), (cb,)), ((), ())), preferred_element_type=F32)


@jax.custom_vjp
def dot_nn(a, b):
    return _dg(a, b, 1, 0)


@jax.custom_vjp
def dot_nt(a, b):
    return _dg(a, b, 1, 1)


@jax.custom_vjp
def dot_tn(a, b):
    return _dg(a, b, 0, 0)


dot_nn.defvjp(lambda a, b: (_dg(a, b, 1, 0), (a, b)), lambda r, g: (_dg(g, r[1], 1, 1), _dg(r[0], g, 0, 0)))
dot_nt.defvjp(lambda a, b: (_dg(a, b, 1, 1), (a, b)), lambda r, g: (_dg(g, r[1], 1, 0), _dg(g, r[0], 0, 0)))
dot_tn.defvjp(lambda a, b: (_dg(a, b, 0, 0), (a, b)), lambda r, g: (_dg(r[1], g, 1, 1), _dg(r[0], g, 1, 0)))


MM_VMEM_BUDGET = 40 * 1024 * 1024


def _mm_tiles(m, n, k, sa, sb, so, tn_only=None):
    best, best_key = None, None
    for tm in (1024, 512, 256, 128, 64, 32, 16, 8):
        if m % tm:
            continue
        for tn in (2048, 1536, 1280, 1024, 768, 640, 512, 384, 256, 128):
            if n % tn or (tn_only is not None and tn not in tn_only):
                continue
            for tk in (k, 2048, 1536, 1280, 1024, 768, 512, 256, 128):
                if k % tk or tk > max(k, 128):
                    continue
                need = 2 * (tm * tk * sa + tk * tn * sb + tm * tn * so) + (tm * tn * 4 if tk < k else 0)
                if need > MM_VMEM_BUDGET:
                    continue
                key = (tm * tn * tk, tk)
                if best_key is None or key > best_key:
                    best, best_key = (tm, tn, tk), key
    assert best is not None, (m, n, k)
    return best


def _matmul(name, a, b, *, ta=False, tb=False, a_win=None, out_dtype=F32, epilogue=None, epi_rows=(), epi_pars=(),
            epi_outs=(), full_rows=False, epi_accs=(), epi_into=None, dep=None):
    a_off, a_w = a_win if a_win is not None else (0, a.shape[1])
    if ta:
        kdim, m = a.shape[0], a_w
    else:
        m, kdim = a.shape[0], a_w
    n = b.shape[0] if tb else b.shape[1]
    assert (b.shape[1] if tb else b.shape[0]) == kdim, (name, a.shape, b.shape)
    out_dtypes = list(epi_outs) if epilogue is not None else [out_dtype]
    so = sum(jnp.dtype(d).itemsize for d in out_dtypes) + sum(r.dtype.itemsize for r, _ in epi_rows)
    tn_ok = [n] if full_rows else [t for t in (1024, 512, 256, 128) if all(off % t == 0 for _, off in epi_rows)]
    tm, tn, tk = _mm_tiles(m, n, kdim, a.dtype.itemsize, b.dtype.itemsize, so, tn_ok if epilogue is not None else None)
    nk = kdim // tk
    n_er, n_ep, n_out = len(epi_rows), len(epi_pars), len(out_dtypes)
    if ta:
        assert a_off % tm == 0
        a_spec = pl.BlockSpec((tk, tm), lambda i, j, k: (k, i + a_off // tm))
    else:
        assert a_off % tk == 0
        a_spec = pl.BlockSpec((tm, tk), lambda i, j, k: (i, k + a_off // tk))
    if tb:
        b_spec = pl.BlockSpec((tn, tk), lambda i, j, k: (j, k))
    else:
        b_spec = pl.BlockSpec((tk, tn), lambda i, j, k: (k, j))
    ca, cb = (0 if ta else 1), (1 if tb else 0)

    n_acc = len(epi_accs)
    n_x = (1 if epi_into is not None else 0) + (1 if dep is not None else 0)
    assert not (n_acc or epi_into is not None) or full_rows

    def body(a_ref, b_ref, *refs):
        er, ep = refs[:n_er], refs[n_er:n_er + n_ep]
        first_out = n_er + n_ep + n_x
        o_refs = refs[first_out:first_out + n_out]
        s_refs = refs[first_out + n_out:first_out + n_out + n_acc]
        acc = refs[first_out + n_out + n_acc:]

        def finish(c):
            outs = [c] if epilogue is None else epilogue(c, [r[...] for r in er], [p[...] for p in ep])
            if n_acc:
                outs, sums = outs

                @pl.when(pl.program_id(0) == 0)
                def _():
                    for s_ref in s_refs:
                        s_ref[...] = jnp.zeros_like(s_ref)

                for s_ref, s in zip(s_refs, sums, strict=True):
                    s_ref[...] += jnp.broadcast_to(s, s_ref.shape)
            for o_ref, o in zip(o_refs, outs, strict=True):
                o_ref[...] = o.astype(o_ref.dtype)

        if nk == 1:
            finish(_dg(a_ref[...], b_ref[...], ca, cb))
            return
        (acc_ref,) = acc
        k = pl.program_id(2)

        @pl.when(k == 0)
        def _():
            acc_ref[...] = jnp.zeros_like(acc_ref)

        acc_ref[...] += _dg(a_ref[...], b_ref[...], ca, cb)

        @pl.when(k == nk - 1)
        def _():
            finish(acc_ref[...])

    in_specs = [a_spec, b_spec]
    in_specs += [pl.BlockSpec((tm, tn), functools.partial(lambda i, j, k, c: (i, j + c), c=off // tn)) for _, off in epi_rows]
    in_specs += [pl.BlockSpec((p.shape[0], tn), lambda i, j, k: (0, j)) for p in epi_pars]
    out_specs = [pl.BlockSpec((tm, tn), lambda i, j, k: (i, j)) for _ in out_dtypes]
    out_shape = [jax.ShapeDtypeStruct((m, n), d) for d in out_dtypes]
    extra, aliases = [], {}
    if epi_into is not None:
        buf, off, width = epi_into
        assert off % width == 0 and buf.dtype == out_dtypes[0]
        in_specs.append(pl.BlockSpec(memory_space=pl.ANY))
        out_specs[0] = pl.BlockSpec((tm, width), functools.partial(lambda i, j, k, c: (i, c), c=off // width))
        out_shape[0] = jax.ShapeDtypeStruct(buf.shape, buf.dtype)
        extra, aliases = [buf], {2 + n_er + n_ep: 0}
    if dep is not None:
        in_specs.append(pl.BlockSpec(memory_space=pl.ANY))
        extra = extra + [dep]
    out_specs += [pl.BlockSpec((r, w), lambda i, j, k: (0, 0)) for r, w in epi_accs]
    out_shape += [jax.ShapeDtypeStruct((r, w), F32) for r, w in epi_accs]
    res = pl.pallas_call(
        body, name=name, grid=(m // tm, n // tn, nk),
        in_specs=in_specs, out_specs=out_specs, out_shape=out_shape, input_output_aliases=aliases,
        scratch_shapes=[pltpu.VMEM((tm, tn), F32)] if nk > 1 else [],
        compiler_params=_cparams(("arbitrary",) * 3 if n_acc else ("parallel", "parallel", "arbitrary")),
    )(a, b, *[r for r, _ in epi_rows], *epi_pars, *extra)
    return res if epilogue is not None else res[0]


BAND = 128


def _rowwise(name, fn, n_rows, tr, row_ins, par_ins, row_outs, acc_outs=(), into=None):
    nr, npar, no, na = len(row_ins), len(par_ins), len(row_outs), len(acc_outs)
    in_specs = []
    for arr, off, w in row_ins:
        assert off % w == 0 and arr.shape[0] == n_rows, (name, arr.shape, off, w)
        in_specs.append(pl.BlockSpec((tr, w), functools.partial(lambda i, c: (i, c), c=off // w)))
    for arr, off, w in par_ins:
        assert off % w == 0
        in_specs.append(pl.BlockSpec((arr.shape[0], w), functools.partial(lambda i, c: (0, c), c=off // w)))
    out_specs = [pl.BlockSpec((tr, w), lambda i: (i, 0)) for w, _ in row_outs]
    out_specs += [pl.BlockSpec((r, w), lambda i: (0, 0)) for r, w in acc_outs]
    out_shape = [jax.ShapeDtypeStruct((n_rows, w), dt) for w, dt in row_outs]
    out_shape += [jax.ShapeDtypeStruct((r, w), F32) for r, w in acc_outs]
    extra, aliases = [], {}
    if into is not None:
        buf, off = into
        w0 = row_outs[0][0]
        assert off % w0 == 0 and buf.dtype == row_outs[0][1]
        in_specs.append(pl.BlockSpec(memory_space=pl.ANY))
        out_specs[0] = pl.BlockSpec((tr, w0), functools.partial(lambda i, c: (i, c), c=off // w0))
        out_shape[0] = jax.ShapeDtypeStruct(buf.shape, buf.dtype)
        extra, aliases = [buf], {nr + npar: 0}
    nx = len(extra)

    def body(*refs):
        rows = [r[...] for r in refs[:nr]]
        pars = [r[...] for r in refs[nr:nr + npar]]
        o_refs = refs[nr + npar + nx:nr + npar + nx + no]
        a_refs = refs[nr + npar + nx + no:]
        outs, accs = fn(rows, pars)
        for o_ref, o in zip(o_refs, outs, strict=True):
            o_ref[...] = o.astype(o_ref.dtype)
        if na:
            @pl.when(pl.program_id(0) == 0)
            def _():
                for a_ref in a_refs:
                    a_ref[...] = jnp.zeros_like(a_ref)

            for a_ref, a in zip(a_refs, accs, strict=True):
                a_ref[...] += jnp.broadcast_to(a, a_ref.shape)

    res = pl.pallas_call(
        body, name=name, grid=(n_rows // tr,),
        in_specs=in_specs, out_specs=out_specs, out_shape=out_shape, input_output_aliases=aliases,
        compiler_params=_cparams(("arbitrary",) if na else ("parallel",)),
    )(*[a for a, _, _ in row_ins], *[a for a, _, _ in par_ins], *extra)
    return res


def _rms(x, w):
    return x * lax.rsqrt(jnp.mean(x * x, axis=-1, keepdims=True) + RMS_EPS) * w


def _s5_out(yc, u, z, t, d, bg):
    ge = jax.nn.gelu(yc + d * u)
    return ge * jax.nn.sigmoid(t + bg) * jax.nn.silu(z)


def _merge(g5, gs, m5, ms):
    return jax.nn.sigmoid(g5) * m5 + jax.nn.sigmoid(gs) * ms


def _head_loss(h1, pgl, pp, fw, tgt):
    h2 = h1 + jax.nn.sigmoid(pgl) * pp
    err = _rms(h2, fw) - tgt
    per_row = 0.5 * jnp.mean(err * err, axis=-1, keepdims=True)
    return jnp.sum(per_row, axis=0, keepdims=True)


def _s5_disc(a_re, a_im, log_step, b_re2, b_im2, expand):
    step = jnp.exp(log_step)
    mag = jnp.exp(a_re * step)
    lb_re = mag * jnp.cos(a_im * step)
    lb_im = mag * jnp.sin(a_im * step)
    den = a_re * a_re + a_im * a_im
    n_re = lb_re - 1.0
    f_re = (n_re * a_re + lb_im * a_im) / den
    f_im = (lb_im * a_re - n_re * a_im) / den
    hi = lax.Precision.HIGHEST
    fr = jnp.dot(expand, f_re, precision=hi, preferred_element_type=F32)
    fi = jnp.dot(expand, f_im, precision=hi, preferred_element_type=F32)
    return lb_re, lb_im, fr * b_re2 - fi * b_im2, fr * b_im2 + fi * b_re2


def _s5_params_fwd(a_re, a_im, log_step, b_re2, b_im2, expand):
    gp = a_re.shape

    def body(ar, ai, ls, br, bi, ex, pr_ref, pi_ref, bbr_ref, bbi_ref):
        lr, li, bbr, bbi = _s5_disc(ar[...], ai[...], ls[...], br[...], bi[...], ex[...])
        bbr_ref[...] = bbr
        bbi_ref[...] = bbi
        qr, qi = lr, li
        for k in range(S5_LOG_TB):
            pr_ref[k] = qr
            pi_ref[k] = qi
            qr, qi = qr * lr - qi * li, qr * li + qi * lr

    return pl.pallas_call(
        body, name="s5_params_fwd",
        out_shape=(jax.ShapeDtypeStruct((S5_LOG_TB,) + gp, F32), jax.ShapeDtypeStruct((S5_LOG_TB,) + gp, F32),
                   jax.ShapeDtypeStruct(b_re2.shape, F32), jax.ShapeDtypeStruct(b_re2.shape, F32)),
    )(a_re, a_im, log_step, b_re2, b_im2, expand)


def _s5_params_bwd(a_re, a_im, log_step, b_re2, b_im2, expand, d_lr, d_li, d_bbr, d_bbi):
    def body(ar, ai, ls, br, bi, ex, glr, gli, gbr, gbi, dar, dai, dls, dbr, dbi):
        _, vjp = jax.vjp(lambda *p: _s5_disc(*p, ex[...]), ar[...], ai[...], ls[...], br[...], bi[...])
        g = vjp((glr[...], gli[...], gbr[...], gbi[...]))
        for ref, val in zip((dar, dai, dls, dbr, dbi), g, strict=True):
            ref[...] = val

    return pl.pallas_call(
        body, name="s5_params_bwd",
        out_shape=tuple(jax.ShapeDtypeStruct(v.shape, F32) for v in (a_re, a_im, log_step, b_re2, b_im2)),
    )(a_re, a_im, log_step, b_re2, b_im2, expand, d_lr, d_li, d_bbr, d_bbi)


def _scan_block(x_ref, out_ref, lp, edge_pow, cr, ci, reverse, each=None):
    n_g = x_ref.shape[0] // 8
    sub = lax.broadcasted_iota(jnp.int32, (8, S5_LB), 0)
    steps = []
    for sh in (1, 2, 4):
        keep = (sub < 8 - sh) if reverse else (sub >= sh)
        steps.append((8 - sh if reverse else sh, jnp.where(keep, lp[sh - 1:sh, :S5_LB], 0.0),
                      jnp.where(keep, lp[sh - 1:sh, S5_LB:], 0.0)))
    e_r, e_i = edge_pow[:, :S5_LB], edge_pow[:, S5_LB:]
    for r in (range(n_g - 1, -1, -1) if reverse else range(n_g)):
        rows = slice(8 * r, 8 * r + 8)
        xr, xi = x_ref[rows, :S5_LB], x_ref[rows, S5_LB:]
        for by, a_r, a_i in steps:
            pr, pi = pltpu.roll(xr, by, 0), pltpu.roll(xi, by, 0)
            xr, xi = xr + a_r * pr - a_i * pi, xi + a_r * pi + a_i * pr
        xr = xr + e_r * cr - e_i * ci
        xi = xi + e_r * ci + e_i * cr
        out_ref[rows, :S5_LB] = xr
        out_ref[rows, S5_LB:] = xi
        if each is not None:
            each(r, xr, xi)
        cr, ci = (xr[0:1, :], xi[0:1, :]) if reverse else (xr[7:8, :], xi[7:8, :])
    return cr, ci


def _s5_fwd(proj, bb_band, c_band, lam_pow, s5_d, n_seq, seq_len):
    n_t = seq_len // S5_TB
    blk = 2 * S5_LB
    n_rows = n_seq * seq_len
    u_blk0 = OFF_U5 // BAND

    def body(u_ref, bb_ref, cb_ref, lp_ref, d_ref, s_ref, y_ref, ge_ref, cr, ci, buf):
        @pl.when(pl.program_id(2) == 0)
        def _():
            cr[...] = jnp.zeros_like(cr)
            ci[...] = jnp.zeros_like(ci)

        u = u_ref[...]
        buf[...] = _dg(u, bb_ref[...], 1, 0)
        lp = lp_ref[...]
        cr[...], ci[...] = _scan_block(buf, buf, lp, lp, cr[...], ci[...], False)
        s = buf[...].astype(s_ref.dtype)
        s_ref[...] = s
        y = _dg(s, cb_ref[...], 1, 1)
        y_ref[...] = y
        ge_ref[...] = jax.nn.gelu(y + d_ref[...] * u).astype(ge_ref.dtype)

    def rows(j, b, t):
        return (b * n_t + t, j)

    return pl.pallas_call(
        body, name="s5_fwd", grid=(S5_NJ, n_seq, n_t),
        in_specs=[pl.BlockSpec((S5_TB, BAND), lambda j, b, t: (b * n_t + t, u_blk0 + j)),
                  pl.BlockSpec((BAND, blk), lambda j, b, t: (j, 0)), pl.BlockSpec((BAND, blk), lambda j, b, t: (j, 0)),
                  pl.BlockSpec((S5_LOG_TB, blk), lambda j, b, t: (0, j)), pl.BlockSpec((1, BAND), lambda j, b, t: (0, j))],
        out_specs=[pl.BlockSpec((S5_TB, blk), rows), pl.BlockSpec((S5_TB, BAND), rows), pl.BlockSpec((S5_TB, BAND), rows)],
        out_shape=[jax.ShapeDtypeStruct((n_rows, S5_NJ * blk), BF16), jax.ShapeDtypeStruct((n_rows, S5_WIDTH), F32),
                   jax.ShapeDtypeStruct((n_rows, S5_WIDTH), BF16)],
        scratch_shapes=[pltpu.VMEM((1, S5_LB), F32), pltpu.VMEM((1, S5_LB), F32), pltpu.VMEM((S5_TB, blk), F32)],
        compiler_params=_cparams(("parallel", "parallel", "arbitrary")),
    )(proj, bb_band, c_band, lam_pow, s5_d)


def _s5_bwd(dyc, s, proj, du5_a, bb_band, c_band, lam_pow_conj, dproj, n_seq, seq_len):
    n_t = seq_len // S5_TB
    blk = 2 * S5_LB
    halo_rows = 16
    halo_per_blk = S5_TB // halo_rows
    u_blk0 = OFF_U5 // BAND

    def rows(j, b, t):
        return (b * n_t + (n_t - 1 - t), j)

    def u_rows(j, b, t):
        return (b * n_t + (n_t - 1 - t), u_blk0 + j)

    def halo(j, b, t):
        return (jnp.maximum((b * n_t + (n_t - 1 - t)) * halo_per_blk - 1, 0), j)

    def body(dy_ref, sq_ref, hq_ref, u_ref, dua_ref, bb_ref, cb_ref, lp_ref, _, du_ref, dl_ref, dbb_ref, dcb_ref,
             cr, ci, buf, s_ref):
        b, t = pl.program_id(1), pl.program_id(2)
        sq = sq_ref[...]
        s_ref[...] = sq.astype(F32)
        h_last = hq_ref[...].astype(F32)[halo_rows - 1:halo_rows, :]
        dy = dy_ref[...]
        buf[...] = _dg(dy, cb_ref[...], 1, 0)

        @pl.when(t == 0)
        def _():
            cr[...] = jnp.zeros_like(cr)
            ci[...] = jnp.zeros_like(ci)

        @pl.when((b == 0) & (t == 0))
        def _():
            dl_ref[...] = jnp.zeros_like(dl_ref)
            dbb_ref[...] = jnp.zeros_like(dbb_ref)
            dcb_ref[...] = jnp.zeros_like(dcb_ref)

        dcb_ref[...] += _dg(dy, sq, 0, 0)

        first_blk = t == n_t - 1
        sub = lax.broadcasted_iota(jnp.int32, (8, S5_LB), 0)
        acc = [jnp.zeros((8, S5_LB), F32), jnp.zeros((8, S5_LB), F32)]

        def each(r, gr, gi):
            rows = slice(8 * r, 8 * r + 8)
            if r == 0:
                before_r = jnp.where(first_blk, 0.0, h_last[:, :S5_LB])
                before_i = jnp.where(first_blk, 0.0, h_last[:, S5_LB:])
            else:
                before_r, before_i = s_ref[8 * r - 1:8 * r, :S5_LB], s_ref[8 * r - 1:8 * r, S5_LB:]
            sp_r = jnp.where(sub == 0, before_r, pltpu.roll(s_ref[rows, :S5_LB], 1, 0))
            sp_i = jnp.where(sub == 0, before_i, pltpu.roll(s_ref[rows, S5_LB:], 1, 0))
            acc[0] = acc[0] + gr * sp_r + gi * sp_i
            acc[1] = acc[1] + gi * sp_r - gr * sp_i

        lp = lp_ref[...]
        edge_pow = jnp.concatenate([lp[7 - i:8 - i, :] for i in range(8)], axis=0)
        cr[...], ci[...] = _scan_block(buf, buf, lp, edge_pow, cr[...], ci[...], True, each)
        g = buf[...].astype(BF16)
        du_ref[...] = (dua_ref[...] + _dg(g, bb_ref[...], 1, 1)).astype(du_ref.dtype)
        dbb_ref[...] += _dg(u_ref[...], g, 0, 0)
        dl_ref[:, :S5_LB] += jnp.sum(acc[0], axis=0, keepdims=True)
        dl_ref[:, S5_LB:] += jnp.sum(acc[1], axis=0, keepdims=True)

    band = pl.BlockSpec((BAND, blk), lambda j, b, t: (j, 0))
    return pl.pallas_call(
        body, name="s5_bwd", grid=(S5_NJ, n_seq, n_t),
        in_specs=[pl.BlockSpec((S5_TB, BAND), rows), pl.BlockSpec((S5_TB, blk), rows),
                  pl.BlockSpec((halo_rows, blk), halo), pl.BlockSpec((S5_TB, BAND), u_rows),
                  pl.BlockSpec((S5_TB, BAND), rows), band, band,
                  pl.BlockSpec((S5_LOG_TB, blk), lambda j, b, t: (0, j)), pl.BlockSpec(memory_space=pl.ANY)],
        out_specs=[pl.BlockSpec((S5_TB, BAND), u_rows), pl.BlockSpec((1, blk), lambda j, b, t: (0, j)), band, band],
        out_shape=[jax.ShapeDtypeStruct(dproj.shape, dproj.dtype), jax.ShapeDtypeStruct((1, S5_NJ * blk), F32),
                   jax.ShapeDtypeStruct(bb_band.shape, F32), jax.ShapeDtypeStruct(c_band.shape, F32)],
        input_output_aliases={8: 0},
        scratch_shapes=[pltpu.VMEM((1, S5_LB), F32), pltpu.VMEM((1, S5_LB), F32), pltpu.VMEM((S5_TB, blk), F32),
                        pltpu.VMEM((S5_TB, blk), F32)],
        compiler_params=_cparams(("parallel", "arbitrary", "arbitrary")),
    )(dyc, s, s, proj, du5_a, bb_band, c_band, lam_pow_conj, dproj)


CONV_TR = 1024
CONV_CW = 512


def _shift_down(x, halo, k):
    if k == 0:
        return x
    row8 = lax.broadcasted_iota(jnp.int32, halo.shape, 0)
    rolled = pltpu.roll(x, k, 0)
    top = jnp.where(row8 < k, pltpu.roll(halo, k, 0), rolled[:8])
    if x.shape[0] == 8:
        return top
    return jnp.concatenate([top, rolled[8:]], axis=0)


def _shift_up(x, halo, k):
    if k == 0:
        return x
    n = x.shape[0]
    row8 = lax.broadcasted_iota(jnp.int32, halo.shape, 0)
    rolled = pltpu.roll(x, n - k, 0)
    bot = jnp.where(row8 >= 8 - k, pltpu.roll(halo, 8 - k, 0), rolled[n - 8:])
    if n == 8:
        return bot
    return jnp.concatenate([rolled[:n - 8], bot], axis=0)


def _conv_pre(x, halo, w, b):
    acc = b + w[SSD_CONV - 1:SSD_CONV, :] * x
    for k in range(SSD_CONV - 1):
        acc = acc + w[k:k + 1, :] * _shift_down(x, halo, SSD_CONV - 1 - k)
    return acc


def _conv_specs(seq_len, col_off):
    lt = seq_len // CONV_TR
    cb = col_off // CONV_CW
    cur = pl.BlockSpec((CONV_TR, CONV_CW), lambda j, i: (i, j + cb))
    prev = pl.BlockSpec((8, CONV_CW), lambda j, i: (jnp.maximum(i * (CONV_TR // 8) - 1, 0), j + cb))
    return lt, cur, prev


def _conv_fwd(proj, conv_w, conv_b, n_rows, seq_len):
    lt, cur, prev = _conv_specs(seq_len, OFF_XBC)

    def body(x_ref, h_ref, w_ref, b_ref, o_ref):
        halo = jnp.where(pl.program_id(1) % lt == 0, 0.0, h_ref[...])
        o_ref[...] = jax.nn.silu(_conv_pre(x_ref[...], halo, w_ref[...], b_ref[...]))

    return pl.pallas_call(
        body, name="ssd_conv_fwd", grid=(SSD_CONV_DIM // CONV_CW, n_rows // CONV_TR),
        in_specs=[cur, prev, pl.BlockSpec((SSD_CONV, CONV_CW), lambda j, i: (0, j)),
                  pl.BlockSpec((1, CONV_CW), lambda j, i: (0, j))],
        out_specs=pl.BlockSpec((CONV_TR, CONV_CW), lambda j, i: (i, j)),
        out_shape=jax.ShapeDtypeStruct((n_rows, SSD_CONV_DIM), F32),
        compiler_params=_cparams(("parallel", "parallel")),
    )(proj, proj, conv_w, conv_b)


def _conv_bwd(name, proj, d_act, conv_w, conv_b, n_rows, seq_len, col_off, dproj):
    width = d_act.shape[1]
    lt, cur, prev = _conv_specs(seq_len, OFF_XBC + col_off)
    n_blk = n_rows // CONV_TR
    cb = (OFF_XBC + col_off) // CONV_CW
    pb = col_off // CONV_CW
    nxt = pl.BlockSpec((8, CONV_CW), lambda j, i: (jnp.minimum((i + 1) * (CONV_TR // 8), n_rows // 8 - 1), j + cb))
    d_cur = pl.BlockSpec((CONV_TR, CONV_CW), lambda j, i: (i, j))
    d_nxt = pl.BlockSpec((8, CONV_CW), lambda j, i: (jnp.minimum((i + 1) * (CONV_TR // 8), n_rows // 8 - 1), j))

    def dsilu(pre):
        sg = jax.nn.sigmoid(pre)
        return sg * (1.0 + pre * (1.0 - sg))

    def body(x_ref, hp_ref, hn_ref, d_ref, dn_ref, w_ref, b_ref, _, dx_ref, dw_ref, db_ref):
        i = pl.program_id(1)
        x, w, b = x_ref[...], w_ref[...], b_ref[...]
        halo_p = jnp.where(i % lt == 0, 0.0, hp_ref[...])
        at_end = i % lt == lt - 1
        dpre = d_ref[...] * dsilu(_conv_pre(x, halo_p, w, b))
        pre_n = _conv_pre(hn_ref[...], x[CONV_TR - 8:, :], w, b)
        dpre_n = jnp.where(at_end, 0.0, dn_ref[...] * dsilu(pre_n))
        dx = w[SSD_CONV - 1:SSD_CONV, :] * dpre
        for k in range(SSD_CONV - 1):
            dx = dx + w[k:k + 1, :] * _shift_up(dpre, dpre_n, SSD_CONV - 1 - k)
        dx_ref[...] = dx.astype(dx_ref.dtype)

        @pl.when(i == 0)
        def _():
            dw_ref[...] = jnp.zeros_like(dw_ref)
            db_ref[...] = jnp.zeros_like(db_ref)

        for k in range(SSD_CONV):
            xs = _shift_down(x, halo_p, SSD_CONV - 1 - k)
            dw_ref[k:k + 1, :] += jnp.sum(dpre * xs, axis=0, keepdims=True)
        db_ref[...] += jnp.sum(dpre, axis=0, keepdims=True)

    return pl.pallas_call(
        body, name=name, grid=(width // CONV_CW, n_blk),
        in_specs=[cur, prev, nxt, d_cur, d_nxt,
                  pl.BlockSpec((SSD_CONV, CONV_CW), lambda j, i: (0, j + pb)),
                  pl.BlockSpec((1, CONV_CW), lambda j, i: (0, j + pb)), pl.BlockSpec(memory_space=pl.ANY)],
        out_specs=[pl.BlockSpec((CONV_TR, CONV_CW), lambda j, i: (i, j + cb)),
                   pl.BlockSpec((SSD_CONV, CONV_CW), lambda j, i: (0, j)),
                   pl.BlockSpec((1, CONV_CW), lambda j, i: (0, j))],
        out_shape=[jax.ShapeDtypeStruct(dproj.shape, dproj.dtype),
                   jax.ShapeDtypeStruct((SSD_CONV, width), F32), jax.ShapeDtypeStruct((1, width), F32)],
        input_output_aliases={7: 0},
        compiler_params=_cparams(("parallel", "arbitrary")),
    )(proj, proj, proj, d_act, d_act, conv_w, conv_b, dproj)


def _split3(x):
    hi = x.astype(BF16)
    r = x - hi.astype(F32)
    mid = r.astype(BF16)
    return hi, mid, (r - mid.astype(F32)).astype(BF16)


def _sel_dot(a, b, a_is_sel):
    dn = (((1,), (0,)), ((), ()))
    if a_is_sel:
        return sum(lax.dot_general(a, t, dn, preferred_element_type=F32) for t in _split3(b))
    return sum(lax.dot_general(t, b, dn, preferred_element_type=F32) for t in _split3(a))


@jax.custom_vjp
def sel_left(sel, sel_t, x):
    return _sel_dot(sel, x, True)


@jax.custom_vjp
def sel_right(x, sel, sel_t):
    return _sel_dot(x, sel, False)


sel_left.defvjp(lambda s, st, x: (_sel_dot(s, x, True), (s, st)),
                lambda r, g: (jnp.zeros_like(r[0]), jnp.zeros_like(r[1]), _sel_dot(r[1], g, True)))
sel_right.defvjp(lambda x, s, st: (_sel_dot(x, s, False), (s, st)),
                 lambda r, g: (_sel_dot(g, r[1], False), jnp.zeros_like(r[0]), jnp.zeros_like(r[1])))


def _ssd_chunk(xs, bm, cm, dtr, st, dtb, alog, dsk, k):
    dt = jax.nn.softplus(dtr + dtb)
    acum = sel_left(k["tri"], k["tri_t"], dt * (-jnp.exp(alog)))
    dt_e = sel_right(dt, k["spread"], k["spread_t"])
    ac_e = sel_right(acum, k["spread"], k["spread_t"])
    al_e = ac_e[SSD_CHUNK - 1:SSD_CHUNK, :]
    dsk_e = sel_right(jnp.broadcast_to(dsk, (8, 128)), k["spread"], k["spread_t"])[0:1, :]
    xdt = xs * dt_e
    acum_t = acum.T
    scores = dot_nt(cm, bm)
    y = dot_nn(cm, st) * jnp.exp(ac_e) + xs * dsk_e
    for j in range(SSD_HPG):
        lmat = jnp.exp(jnp.where(k["causal"], acum[:, j:j + 1] - acum_t[j:j + 1, :], -jnp.inf))
        y = y + dot_nn(scores * lmat, jnp.where(k["head"] == j, xdt, 0.0))
    new = st * jnp.exp(al_e) + dot_tn(bm, xdt * jnp.exp(al_e - ac_e))
    return y, new


def _ssd_consts():
    r = lax.broadcasted_iota(jnp.int32, (SSD_CHUNK, SSD_CHUNK), 0)
    c = lax.broadcasted_iota(jnp.int32, (SSD_CHUNK, SSD_CHUNK), 1)
    hd = jnp.int32(SSD_HEADDIM)
    sr = lax.broadcasted_iota(jnp.int32, (128, GROUP_W), 0)
    sc = lax.div(lax.broadcasted_iota(jnp.int32, (128, GROUP_W), 1), hd)
    tr = lax.div(lax.broadcasted_iota(jnp.int32, (GROUP_W, 128), 0), hd)
    tc = lax.broadcasted_iota(jnp.int32, (GROUP_W, 128), 1)
    return {"tri": (r >= c).astype(BF16), "tri_t": (c >= r).astype(BF16), "causal": r >= c,
            "spread": (sr == sc).astype(BF16), "spread_t": (tr == tc).astype(BF16),
            "head": lax.div(lax.broadcasted_iota(jnp.int32, (SSD_CHUNK, GROUP_W), 1), hd)}


def _ssd_specs(n_c, reverse):
    def cidx(c):
        return n_c - 1 - c if reverse else c

    xs = pl.BlockSpec((SSD_CHUNK, GROUP_W), lambda g, b, c: (b * n_c + cidx(c), g))
    bm = pl.BlockSpec((SSD_CHUNK, SSD_STATE), lambda g, b, c: (b * n_c + cidx(c), SSD_WIDTH // SSD_STATE + g))
    cm = pl.BlockSpec((SSD_CHUNK, SSD_STATE), lambda g, b, c: (b * n_c + cidx(c), (SSD_WIDTH + SSD_BC) // SSD_STATE + g))
    dt = pl.BlockSpec((SSD_CHUNK, 128), lambda g, b, c: (b * n_c + cidx(c), OFF_DT // 128 + g))
    par = pl.BlockSpec((1, 128), lambda g, b, c: (0, g))
    st = pl.BlockSpec((1, 1, 1, SSD_STATE, GROUP_W), lambda g, b, c: (b, cidx(c), g, 0, 0))
    return xs, bm, cm, dt, par, st


def _ssd_chunk_gated(xs, bm, cm, dtr, st, dtb, alog, dsk, z, nw, k):
    y, new = _ssd_chunk(xs, bm, cm, dtr, st, dtb, alog, dsk, k)
    yg = y * jax.nn.silu(z)
    return yg * lax.rsqrt(jnp.mean(yg * yg, axis=-1, keepdims=True) + RMS_EPS) * nw, new


def _ssd_gate_specs(n_c, reverse):
    def cidx(c):
        return n_c - 1 - c if reverse else c

    z = pl.BlockSpec((SSD_CHUNK, GROUP_W), lambda g, b, c: (b * n_c + cidx(c), OFF_ZS // GROUP_W + g))
    nw = pl.BlockSpec((1, GROUP_W), lambda g, b, c: (0, g))
    return z, nw


def _ssd_fwd(xbc_act, proj, dtb, alog, dsk, norm_w, n_seq, seq_len):
    n_c = seq_len // SSD_CHUNK
    xs_s, bm_s, cm_s, dt_s, par_s, st_s = _ssd_specs(n_c, False)
    z_s, nw_s = _ssd_gate_specs(n_c, False)

    def body(xs_ref, bm_ref, cm_ref, dt_ref, dtb_ref, al_ref, dk_ref, z_ref, nw_ref, y_ref, st_ref, state):
        @pl.when(pl.program_id(2) == 0)
        def _():
            state[...] = jnp.zeros_like(state)

        prev = state[...]
        st_ref[0, 0, 0] = prev
        y, new = _ssd_chunk_gated(xs_ref[...], bm_ref[...], cm_ref[...], dt_ref[...], prev,
                                  dtb_ref[...], al_ref[...], dk_ref[...], z_ref[...], nw_ref[...], _ssd_consts())
        y_ref[...] = y.astype(y_ref.dtype)
        state[...] = new

    return pl.pallas_call(
        body, name="ssd_fwd", grid=(SSD_GROUPS, n_seq, n_c),
        in_specs=[xs_s, bm_s, cm_s, dt_s, par_s, par_s, par_s, z_s, nw_s],
        out_specs=[pl.BlockSpec((SSD_CHUNK, GROUP_W), lambda g, b, c: (b * n_c + c, g)), st_s],
        out_shape=[jax.ShapeDtypeStruct((n_seq * seq_len, SSD_WIDTH), BF16),
                   jax.ShapeDtypeStruct((n_seq, n_c, SSD_GROUPS, SSD_STATE, GROUP_W), F32)],
        scratch_shapes=[pltpu.VMEM((SSD_STATE, GROUP_W), F32)],
        compiler_params=_cparams(("parallel", "parallel", "arbitrary")),
    )(xbc_act, xbc_act, xbc_act, proj, dtb, alog, dsk, proj, norm_w)


def _ssd_bwd(xbc_act, proj, states, dy, dtb, alog, dsk, norm_w, n_seq, seq_len, dproj):
    n_c = seq_len // SSD_CHUNK
    n_rows = n_seq * seq_len
    xs_s, bm_s, cm_s, dt_s, par_s, st_s = _ssd_specs(n_c, True)
    z_s, nw_s = _ssd_gate_specs(n_c, True)

    def rows(w, first=0):
        return pl.BlockSpec((SSD_CHUNK, w), lambda g, b, c: (b * n_c + (n_c - 1 - c), first + g))

    def body(xs_ref, bm_ref, cm_ref, dt_ref, st_ref, dy_ref, dtb_ref, al_ref, dk_ref, z_ref, nw_ref, _,
             dxs_ref, dbm_ref, dcm_ref, dz_ref, ddt_ref, ddtb_ref, dal_ref, ddk_ref, dnw_ref, dstate):
        b, c = pl.program_id(1), pl.program_id(2)

        @pl.when(c == 0)
        def _():
            dstate[...] = jnp.zeros_like(dstate)

        @pl.when((b == 0) & (c == 0))
        def _():
            for ref in (ddtb_ref, dal_ref, ddk_ref, dnw_ref):
                ref[...] = jnp.zeros_like(ref)

        consts = _ssd_consts()
        _, vjp = jax.vjp(
            lambda *a: _ssd_chunk_gated(*a, consts),
            xs_ref[...], bm_ref[...], cm_ref[...], dt_ref[...], st_ref[0, 0, 0], dtb_ref[...], al_ref[...], dk_ref[...],
            z_ref[...], nw_ref[...])
        dxs, dbm, dcm, ddtr, dprev, ddtb, dal, ddk, dz, dnw = vjp((dy_ref[...], dstate[...]))
        dxs_ref[...] = dxs
        dbm_ref[...] = dbm
        dcm_ref[...] = dcm
        dz_ref[...] = dz.astype(dz_ref.dtype)
        ddt_ref[...] = ddtr.astype(ddt_ref.dtype)
        ddtb_ref[...] += ddtb
        dal_ref[...] += dal
        ddk_ref[...] += ddk
        dnw_ref[...] += dnw
        dstate[...] = dprev

    acc = pl.BlockSpec((1, 128), lambda g, b, c: (0, g))
    return pl.pallas_call(
        body, name="ssd_bwd", grid=(SSD_GROUPS, n_seq, n_c),
        in_specs=[xs_s, bm_s, cm_s, dt_s, st_s, rows(GROUP_W), par_s, par_s, par_s, z_s, nw_s,
                  pl.BlockSpec(memory_space=pl.ANY)],
        out_specs=[rows(GROUP_W), rows(SSD_STATE), rows(SSD_STATE), rows(GROUP_W, OFF_ZS // GROUP_W), rows(128),
                   acc, acc, acc, pl.BlockSpec((1, GROUP_W), lambda g, b, c: (0, g))],
        out_shape=[jax.ShapeDtypeStruct((n_rows, SSD_WIDTH), F32), jax.ShapeDtypeStruct((n_rows, SSD_BC), F32),
                   jax.ShapeDtypeStruct((n_rows, SSD_BC), F32), jax.ShapeDtypeStruct(dproj.shape, dproj.dtype),
                   jax.ShapeDtypeStruct((n_rows, DT_W), BF16),
                   jax.ShapeDtypeStruct((1, 512), F32), jax.ShapeDtypeStruct((1, 512), F32),
                   jax.ShapeDtypeStruct((1, 512), F32), jax.ShapeDtypeStruct((1, SSD_WIDTH), F32)],
        input_output_aliases={11: 3},
        scratch_shapes=[pltpu.VMEM((SSD_STATE, GROUP_W), F32)],
        compiler_params=_cparams(("parallel", "arbitrary", "arbitrary")),
    )(xbc_act, xbc_act, xbc_act, proj, states, dy, dtb, alog, dsk, proj, norm_w, dproj)


def _pad_heads(v):
    return jnp.pad(v.reshape(SSD_GROUPS, SSD_HPG), ((0, 0), (0, 128 - SSD_HPG))).reshape(1, SSD_GROUPS * 128)


def _unpad_heads(v):
    return v.reshape(SSD_GROUPS, 128)[:, :SSD_HPG].reshape(1, SSD_HEADS)


def _state_cols(v):
    re, im = v
    lead = re.shape[:-1]
    re = re.reshape(lead + (S5_NJ, 1, S5_LB))
    im = im.reshape(lead + (S5_NJ, 1, S5_LB))
    return jnp.concatenate([re, im], axis=-2).reshape(lead + (2 * S5_N,))


def _state_uncols(v):
    lead = v.shape[:-1]
    v = v.reshape(lead + (S5_NJ, 2, S5_LB))
    return v[..., 0, :].reshape(lead + (S5_N,)), v[..., 1, :].reshape(lead + (S5_N,))


GROUPS_PER_BAND = BAND // S5_GROUP


def _band(w2_re, w2_im):
    gh = S5_GROUPS * S5_GROUP
    rg = ((jnp.arange(gh) // S5_GROUP) % GROUPS_PER_BAND)[:, None, None]
    cg = jnp.arange(GROUPS_PER_BAND)[None, :, None]
    parts = [jnp.where(rg == cg, v[:, None, :], 0.0).reshape(gh, S5_LB) for v in (w2_re, w2_im)]
    return jnp.concatenate(parts, axis=1)


def _band_take(wb):
    gh = S5_GROUPS * S5_GROUP
    w4 = wb.reshape(gh, 2, GROUPS_PER_BAND, S5_STATE)
    sel = w4[jnp.arange(gh), :, (jnp.arange(gh) // S5_GROUP) % GROUPS_PER_BAND, :]
    return sel[:, 0, :], sel[:, 1, :]


W_IN_SHARD = IN_PROJ_DIM // N_CHIPS
W_IN_SEGS = ((0, 512, OFF_U5), (512, 1024, OFF_Z5), (1024, 2560, OFF_ZS), (2560, 5120, OFF_XBC), (5144, 7192, OFF_G5))
DT_ROWS = (5120, 5144)


def _w_in_pieces():
    runs = []
    segs = list(W_IN_SEGS) + [(DT_ROWS[0] + SSD_HPG * g, DT_ROWS[0] + SSD_HPG * (g + 1), OFF_DT + 128 * g)
                              for g in range(SSD_GROUPS)]
    for lo, hi, off in segs:
        for j in range(N_CHIPS):
            s, e = max(lo, j * W_IN_SHARD), min(hi, (j + 1) * W_IN_SHARD)
            if s < e:
                runs.append((j, s - j * W_IN_SHARD, off + s - lo, e - s))
    return runs


RELAYOUT_LANES = 256


def _pad_w_in_t(a4):
    runs = _w_in_pieces()

    def body(a_ref, o_ref):
        o_ref[pl.ds(OFF_DT, DT_W), :] = jnp.zeros((DT_W, RELAYOUT_LANES), o_ref.dtype)
        for j, src, dst, n in runs:
            o_ref[pl.ds(dst, n), :] = a_ref[j, pl.ds(src, n), :]

    return pl.pallas_call(
        body, name="w_in_to_padded", grid=(D_MODEL // RELAYOUT_LANES,),
        in_specs=[pl.BlockSpec((N_CHIPS, W_IN_SHARD, RELAYOUT_LANES), lambda i: (0, 0, i))],
        out_specs=pl.BlockSpec((PROJ_W, RELAYOUT_LANES), lambda i: (0, i)),
        out_shape=jax.ShapeDtypeStruct((PROJ_W, D_MODEL), a4.dtype),
        compiler_params=_cparams(("parallel",)),
    )(a4)


def _unpad_w_in_t(wp):
    runs = _w_in_pieces()

    def body(p_ref, o_ref):
        for j, dst, src, n in runs:
            o_ref[j, pl.ds(dst, n), :] = p_ref[pl.ds(src, n), :]

    return pl.pallas_call(
        body, name="w_in_from_padded", grid=(D_MODEL // RELAYOUT_LANES,),
        in_specs=[pl.BlockSpec((PROJ_W, RELAYOUT_LANES), lambda i: (0, i))],
        out_specs=pl.BlockSpec((N_CHIPS, W_IN_SHARD, RELAYOUT_LANES), lambda i: (0, 0, i)),
        out_shape=jax.ShapeDtypeStruct((N_CHIPS, W_IN_SHARD, D_MODEL), wp.dtype),
        compiler_params=_cparams(("parallel",)),
    )(wp)


def _local_step(x, p, tgt, w):
    n_seq, seq_len, _ = x.shape
    n_rows = n_seq * seq_len
    tr = 512
    x2 = x.reshape(n_rows, D_MODEL)
    p2 = p.reshape(n_rows, PLE_DIM)
    t2 = tgt.reshape(n_rows, D_MODEL)
    row = functools.partial(_rowwise, n_rows=n_rows, tr=tr)

    w_pad_t = _pad_w_in_t(w["w_in_t"])
    norm_w = w["norm_w"].reshape(1, D_MODEL)
    ple_norm_w = w["ple_norm_w"].reshape(1, D_MODEL)
    final_w = w["final_norm_w"].reshape(1, D_MODEL)
    s5_d = w["s5_d"].reshape(1, S5_WIDTH)
    b_glu = w["s5_b_glu"].reshape(1, S5_WIDTH)
    conv_w = w["ssd_conv_w"].reshape(SSD_CONV, SSD_CONV_DIM)
    conv_b = w["ssd_conv_b"].reshape(1, SSD_CONV_DIM)
    ssd_norm_w = w["ssd_norm_w"].reshape(1, SSD_WIDTH)
    dtb, alog, dsk = (_pad_heads(w[k].reshape(1, SSD_HEADS)) for k in ("ssd_dt_bias", "ssd_a_log", "ssd_d"))

    gh = S5_GROUPS * S5_GROUP
    a_re = w["s5_a_re"].reshape(S5_GROUPS, S5_STATE)
    a_im = w["s5_a_im"].reshape(S5_GROUPS, S5_STATE)
    log_step = w["s5_log_step"].reshape(S5_GROUPS, 1)
    b_re2 = jnp.transpose(w["s5_b_re"].reshape(S5_GROUPS, S5_STATE, S5_GROUP), (0, 2, 1)).reshape(gh, S5_STATE)
    b_im2 = jnp.transpose(w["s5_b_im"].reshape(S5_GROUPS, S5_STATE, S5_GROUP), (0, 2, 1)).reshape(gh, S5_STATE)
    expand = (jnp.arange(gh)[:, None] // S5_GROUP == jnp.arange(S5_GROUPS)[None, :]).astype(F32)
    pow_re, pow_im, bb_re2, bb_im2 = _s5_params_fwd(a_re, a_im, log_step, b_re2, b_im2, expand)
    lam_pow = _state_cols((pow_re.reshape(S5_LOG_TB, S5_N), pow_im.reshape(S5_LOG_TB, S5_N)))
    lam_pow_conj = _state_cols((pow_re.reshape(S5_LOG_TB, S5_N), -pow_im.reshape(S5_LOG_TB, S5_N)))
    bb_band = _band(bb_re2, bb_im2).astype(BF16)
    c_band = _band(w["s5_c_re"].reshape(gh, S5_STATE), -w["s5_c_im"].reshape(gh, S5_STATE)).astype(BF16)

    late = w.get("_late")
    (hn,) = row("rms_in", lambda r, q: ([_rms(r[0], q[0])], []), row_ins=[(x2, 0, D_MODEL)],
                par_ins=[(norm_w, 0, D_MODEL)] + ([(late[0], 0, 128)] if late else []), row_outs=[(D_MODEL, BF16)])
    proj = _matmul("mm_proj", hn, w_pad_t, tb=True)
    s, yc, ge = _s5_fwd(proj, bb_band, c_band, lam_pow, s5_d, n_seq, seq_len)
    if late:
        w = {**w, **late[1](ge)}
    tg, y5 = _matmul("mm_s5_glu", ge, w["s5_w_glu"], epilogue=lambda c, r, q: [c, _s5_out(r[0], r[1], r[2], c, *q)],
                     epi_rows=[(yc, 0), (proj, OFF_U5), (proj, OFF_Z5)], epi_pars=[s5_d, b_glu], epi_outs=[F32, BF16])

    xbc_act = _conv_fwd(proj, conv_w, conv_b, n_rows, seq_len)
    yss, states = _ssd_fwd(xbc_act, proj, dtb, alog, dsk, ssd_norm_w, n_seq, seq_len)

    m5 = _matmul("mm_br_s5", y5, w["w_br_s5"])
    ms, merged = _matmul("mm_br_ssd", yss, w["w_br_ssd"], epilogue=lambda c, r, q: [c, _merge(r[0], r[1], r[2], c)],
                         epi_rows=[(proj, OFF_G5), (proj, OFF_GS), (m5, 0)], epi_outs=[F32, BF16])

    def resid_norm(c, r, q):
        h1_ = r[0] + c
        return [h1_, _rms(h1_, q[0])]

    h1, hp = _matmul("mm_out", merged, w["w_out"], epilogue=resid_norm, epi_rows=[(x2, 0)], epi_pars=[ple_norm_w],
                     epi_outs=[F32, BF16], full_rows=True)
    pp = _matmul("mm_ple_proj", p2, w["w_ple_proj"])

    def head_fn(pgl_, r, q):
        h1_, pp_, tgt_ = r
        loss, vjp = jax.vjp(lambda a, b, c, f: _head_loss(a, b, c, f, tgt_), h1_, pgl_, pp_, q[0])
        dh1_, dpgl_, dpp_, dfw_ = vjp(jnp.ones_like(loss))
        return [dh1_, dpgl_, dpp_], [loss, dfw_]

    dh2, dpgl, dpp, loss_acc, d_final_w = _matmul(
        "mm_ple_gate_head", hp, w["w_ple_gate"], epilogue=head_fn, epi_rows=[(h1, 0), (pp, 0), (t2, 0)],
        epi_pars=[final_w], epi_outs=[F32, BF16, BF16], epi_accs=[(1, 128), (1, D_MODEL)], full_rows=True)
    loss = loss_acc[0, 0]

    g = {}
    g["final_norm_w"] = d_final_w
    g["w_ple_gate"] = _matmul("mm_d_w_ple_gate", hp, dpgl, ta=True)
    g["w_ple_proj"] = _matmul("mm_d_w_ple_proj", p2, dpp, ta=True)

    def ple_norm_bwd(dhp_, r, q):
        h1_, dh2_ = r
        _, vjp = jax.vjp(_rms, h1_, q[0])
        dh, dw = vjp(dhp_)
        dh = dh + dh2_
        return [dh, dh], [dw]

    dh1, dh1_b, g["ple_norm_w"] = _matmul(
        "mm_d_hp", dpgl, w["w_ple_gate"], tb=True, epilogue=ple_norm_bwd, epi_rows=[(h1, 0), (dh2, 0)],
        epi_pars=[ple_norm_w], epi_outs=[F32, BF16], epi_accs=[(1, D_MODEL)], full_rows=True)
    g["w_out"] = _matmul("mm_d_w_out", merged, dh1_b, ta=True)

    dproj = lax.empty((n_rows, PROJ_W), BF16)

    def merge_bwd(dmerged, r, q):
        sg5, sgs = jax.nn.sigmoid(r[0]), jax.nn.sigmoid(r[1])
        d_gates = jnp.concatenate([dmerged * r[2] * sg5 * (1.0 - sg5), dmerged * r[3] * sgs * (1.0 - sgs)], axis=1)
        return [d_gates, dmerged * sg5, dmerged * sgs]

    dproj, dm5, dms = _matmul(
        "mm_d_merged", dh1_b, w["w_out"], tb=True, epilogue=merge_bwd,
        epi_rows=[(proj, OFF_G5), (proj, OFF_GS), (m5, 0), (ms, 0)], epi_outs=[BF16, BF16, BF16], full_rows=True,
        epi_into=(dproj, OFF_G5, 2 * D_MODEL))
    g["w_br_s5"] = _matmul("mm_d_w_br_s5", y5, dm5, ta=True)
    g["w_br_ssd"] = _matmul("mm_d_w_br_ssd", yss, dms, ta=True)
    dyss = _matmul("mm_d_yss", dms, w["w_br_ssd"], tb=True)

    def s5_out_bwd_a(dy_, r, q):
        yc_, u_, z_, t_ = r
        d_, bg_ = q
        ge_ = jax.nn.gelu(yc_ + d_ * u_)
        _, vjp = jax.vjp(lambda a, z, t, b: a * jax.nn.sigmoid(t + b) * jax.nn.silu(z), ge_, z_, t_, bg_)
        dge, dz, dt_, dbg = vjp(dy_)
        return [dz, dge, dt_], [dbg]

    dproj, dge_a, dtg, g["s5_b_glu"] = _matmul(
        "mm_d_y5", dm5, w["w_br_s5"], tb=True, epilogue=s5_out_bwd_a,
        epi_rows=[(yc, 0), (proj, OFF_U5), (proj, OFF_Z5), (tg, 0)], epi_pars=[s5_d, b_glu],
        epi_outs=[BF16, F32, BF16], epi_accs=[(1, S5_WIDTH)], full_rows=True, epi_into=(dproj, OFF_Z5, S5_WIDTH))
    g["s5_w_glu"] = _matmul("mm_d_w_glu", ge, dtg, ta=True)
    early = w["_early"](g) if "_early" in w else None

    def s5_out_bwd_b(dge_b, r, q):
        yc_, u_, da_ = r
        _, vjp = jax.vjp(lambda yc, u, d: jax.nn.gelu(yc + d * u), yc_, u_, q[0])
        dyc_, du_, dd_ = vjp(da_ + dge_b)
        return [dyc_, du_], [dd_]

    dyc, du5_a, g["s5_d"] = _matmul(
        "mm_d_ge", dtg, w["s5_w_glu"], tb=True, epilogue=s5_out_bwd_b,
        epi_rows=[(yc, 0), (proj, OFF_U5), (dge_a, 0)], epi_pars=[s5_d], epi_outs=[BF16, F32],
        epi_accs=[(1, S5_WIDTH)], full_rows=True, dep=early)
    dproj, d_lam, d_bb_band, d_c_band = _s5_bwd(dyc, s, proj, du5_a, bb_band, c_band, lam_pow_conj, dproj,
                                                n_seq, seq_len)

    d_lr, d_li = _state_uncols(d_lam)
    d_bbr, d_bbi = _band_take(d_bb_band)
    d_are, d_aim, d_ls, d_br2, d_bi2 = _s5_params_bwd(
        a_re, a_im, log_step, b_re2, b_im2, expand,
        d_lr.reshape(S5_GROUPS, S5_STATE), d_li.reshape(S5_GROUPS, S5_STATE), d_bbr, d_bbi)
    g["s5_a_re"], g["s5_a_im"], g["s5_log_step"] = d_are, d_aim, d_ls
    g["s5_b_re_ghp"], g["s5_b_im_ghp"] = d_br2, d_bi2
    d_cr, d_ci = _band_take(d_c_band)
    g["s5_c_re"], g["s5_c_im"] = d_cr, -d_ci

    dxs, dbm, dcm, dproj, ddt, d_dtb, d_alog, d_dsk, g["ssd_norm_w"] = _ssd_bwd(
        xbc_act, proj, states, dyss, dtb, alog, dsk, ssd_norm_w, n_seq, seq_len, dproj)
    (dproj,) = row("ssd_ddt", lambda r, q: ([r[0]], []), row_ins=[(ddt, 0, DT_W)], par_ins=[],
                   row_outs=[(DT_W, BF16)], into=(dproj, OFF_DT))
    g["ssd_dt_bias"], g["ssd_a_log"], g["ssd_d"] = _unpad_heads(d_dtb), _unpad_heads(d_alog), _unpad_heads(d_dsk)
    conv_dw, conv_db = [], []
    for nm, d_act, off in (("x", dxs, 0), ("b", dbm, SSD_WIDTH), ("c", dcm, SSD_WIDTH + SSD_BC)):
        dproj, dw_, db_ = _conv_bwd("ssd_conv_bwd_" + nm, proj, d_act, conv_w, conv_b, n_rows, seq_len, off, dproj)
        conv_dw.append(dw_)
        conv_db.append(db_)
    g["ssd_conv_w"] = jnp.concatenate(conv_dw, axis=1)
    g["ssd_conv_b"] = jnp.concatenate(conv_db, axis=1)

    g["w_in_t"] = _unpad_w_in_t(_matmul("mm_d_w_in", dproj, hn, ta=True))
    def norm_bwd(dhn_, r, q):
        x_, dh1_ = r
        _, vjp = jax.vjp(_rms, x_, q[0])
        dx_, dw_ = vjp(dhn_)
        return [dx_ + dh1_], [dw_]

    sent = w["_w_in_ready"](g, loss) if "_w_in_ready" in w else None
    dx, g["norm_w"] = _matmul("mm_d_hn", dproj, w_pad_t, epilogue=norm_bwd, epi_rows=[(x2, 0), (dh1, 0)],
                              epi_pars=[norm_w], epi_outs=[F32], epi_accs=[(1, D_MODEL)], full_rows=True, dep=sent)
    return loss, dx.reshape(x.shape), g


HBM = pl.BlockSpec(memory_space=pltpu.HBM)


def _chip_index(x, y):
    return 2 * x + y


def _half(shape2d, axis, which):
    h = shape2d[axis] // 2
    sl = pl.ds(pl.multiple_of(which * h, 128 if axis else 8), h)
    return (slice(None), sl) if axis else (sl, slice(None))


def _gather_chips(split, axes, whole):
    ns, nw = len(split), len(whole)
    n = ns + nw

    def body(*refs):
        ins, outs = refs[:n], refs[n:2 * n]
        ici_send, ici_recv, d2d_send, d2d_recv, local_sems = refs[2 * n:]
        x, y, c = lax.axis_index("x"), lax.axis_index("y"), lax.axis_index("c")
        me = _chip_index(x, y)
        sibling = (x, y, 1 - c)
        peers = [(1 - x, y), (x, 1 - y), (1 - x, 1 - y)]

        def half(t, which):
            return _half(split[t].shape, axes[t], which)

        copies = []
        for t in range(n):
            loc = pltpu.make_async_copy(ins[t], outs[t].at[me], local_sems.at[t])
            loc.start()
            copies.append(loc)

        def ici(t, k, slot):
            px, py = peers[k]
            if t < ns:
                src, dst = ins[t].at[half(t, c)], outs[t].at[(slot,) + half(t, c)]
            else:
                src, dst = ins[t], outs[t].at[slot]
            return pltpu.make_async_remote_copy(src_ref=src, dst_ref=dst, send_sem=ici_send.at[t, k],
                                                recv_sem=ici_recv.at[t, k], device_id=(px, py, c), device_id_type=MESH)

        def d2d(t, k, which):
            rows = outs[t].at[(_chip_index(*peers[k]),) + half(t, which)]
            return pltpu.make_async_remote_copy(src_ref=rows, dst_ref=rows, send_sem=d2d_send.at[t, k],
                                                recv_sem=d2d_recv.at[t, k], device_id=sibling, device_id_type=MESH)

        sends = []
        for t in range(n):
            for k in range(3):
                cp = ici(t, k, me)
                cp.start()
                sends.append(cp)
        for t in range(n):
            for k in range(3):
                ici(t, k, _chip_index(*peers[k])).wait_recv()
                if t < ns:
                    cp = d2d(t, k, c)
                    cp.start()
                    sends.append(cp)
        for t in range(ns):
            for k in range(3):
                d2d(t, k, 1 - c).wait_recv()
        for cp in sends:
            cp.wait_send()
        for cp in copies:
            cp.wait()

    arrays = list(split) + list(whole)
    return pl.pallas_call(
        body, name="gather_weights",
        in_specs=[HBM] * n, out_specs=[HBM] * n,
        out_shape=[jax.ShapeDtypeStruct((N_CHIPS,) + a.shape, a.dtype) for a in arrays],
        scratch_shapes=[pltpu.SemaphoreType.DMA((n, 3)), pltpu.SemaphoreType.DMA((n, 3)),
                        pltpu.SemaphoreType.DMA((ns, 3)), pltpu.SemaphoreType.DMA((ns, 3)),
                        pltpu.SemaphoreType.DMA((n,))],
    )(*arrays)


SEM = pl.BlockSpec(memory_space=pltpu.SEMAPHORE)
DATAFLOW = pltpu.SideEffectType.DATAFLOW_SIDE_EFFECTING


def _gather_start(shards, after):
    n = len(shards)

    def body(*refs):
        ins, lands = refs[:n], refs[n:2 * n]
        send_sems, recv_sems = refs[2 * n + 1], refs[2 * n + 2]
        token = refs[-1]
        x, y, c = lax.axis_index("x"), lax.axis_index("y"), lax.axis_index("c")
        me = _chip_index(x, y)
        for t in range(n):
            for k, (px, py) in enumerate([(1 - x, y), (x, 1 - y), (1 - x, 1 - y)]):
                pltpu.make_async_remote_copy(
                    src_ref=ins[t], dst_ref=lands[t].at[me], send_sem=send_sems.at[3 * t + k],
                    recv_sem=recv_sems.at[3 * t + k],
                    device_id=(px, py, c), device_id_type=MESH).start()
        token[...] = jnp.zeros_like(token)

    zones = [lax.empty((N_CHIPS,) + a.shape, a.dtype) for a in shards]
    res = pl.pallas_call(
        body, name="gather_rest_start",
        out_shape=(pltpu.SemaphoreType.DMA((3 * n,)), pltpu.SemaphoreType.DMA((3 * n,)),
                   *[pltpu.HBM(a.shape, a.dtype) for a in shards], *[pltpu.HBM(z.shape, z.dtype) for z in zones],
                   jax.ShapeDtypeStruct((8, 128), F32)),
        in_specs=[HBM] * (2 * n) + [pl.BlockSpec(memory_space=pl.ANY)],
        out_specs=(SEM, SEM, *[HBM] * (2 * n), pl.BlockSpec(memory_space=pltpu.VMEM)),
        input_output_aliases={t: 2 + t for t in range(2 * n)},
        compiler_params=pltpu.CompilerParams(has_side_effects=DATAFLOW),
    )(*[pltpu.with_memory_space_constraint(a, pltpu.HBM) for a in shards],
      *[pltpu.with_memory_space_constraint(z, pltpu.HBM) for z in zones], after)
    return res[0], res[1], list(res[2:2 + n]), list(res[2 + n:2 + 2 * n]), res[-1]


def _gather_wait(send_sems, recv_sems, thru, lands, after):
    n = len(thru)

    def body(*refs):
        ins, zones = refs[:n], refs[n:2 * n]
        s_sems, r_sems = refs[2 * n], refs[2 * n + 1]
        x, y, c = lax.axis_index("x"), lax.axis_index("y"), lax.axis_index("c")
        for t in range(n):
            for k, (px, py) in enumerate([(1 - x, y), (x, 1 - y), (1 - x, 1 - y)]):
                cp = pltpu.make_async_remote_copy(
                    src_ref=ins[t], dst_ref=zones[t].at[_chip_index(px, py)], send_sem=s_sems.at[3 * t + k],
                    recv_sem=r_sems.at[3 * t + k], device_id=(px, py, c), device_id_type=MESH)
                cp.wait_send()
                cp.wait_recv()

    res = pl.pallas_call(
        body, name="gather_rest_wait",
        out_shape=(*[pltpu.HBM(a.shape, a.dtype) for a in thru], *[pltpu.HBM(z.shape, z.dtype) for z in lands]),
        in_specs=[HBM] * (2 * n) + [SEM, SEM, pl.BlockSpec(memory_space=pl.ANY)], out_specs=[HBM] * (2 * n),
        input_output_aliases={t: t for t in range(2 * n)},
        compiler_params=pltpu.CompilerParams(has_side_effects=DATAFLOW),
    )(*thru, *lands, send_sems, recv_sems, after)
    return list(res[:n]), list(res[n:])


def _scatter_ends(t, k, n_slotted, ins, zones, x, y, me):
    px, py = [(1 - x, y), (x, 1 - y), (1 - x, 1 - y)][k]
    if t < n_slotted:
        return ins[t].at[_chip_index(px, py)], zones[t].at[k], (px, py)
    return ins[t], (zones[t].at[me], zones[t].at[_chip_index(px, py)]), (px, py)


def _scatter_start(name, slotted, whole, after):
    n_s = len(slotted)
    arrays = list(slotted) + list(whole)
    n = len(arrays)

    def body(*refs):
        ins, lands = refs[:n], refs[n:2 * n]
        send_sems, recv_sems = refs[2 * n + 1], refs[2 * n + 2]
        token = refs[-1]
        x, y, c = lax.axis_index("x"), lax.axis_index("y"), lax.axis_index("c")
        for t in range(n):
            for k in range(3):
                src, dst, (px, py) = _scatter_ends(t, k, n_s, ins, lands, x, y, _chip_index(x, y))
                pltpu.make_async_remote_copy(
                    src_ref=src, dst_ref=dst if t < n_s else dst[0], send_sem=send_sems.at[3 * t + k],
                    recv_sem=recv_sems.at[3 * t + k], device_id=(px, py, c), device_id_type=MESH).start()
        token[...] = jnp.zeros_like(token)

    zones = [lax.empty((3,) + a.shape[1:], a.dtype) for a in slotted]
    zones += [lax.empty((N_CHIPS,) + a.shape, a.dtype) for a in whole]
    slotted = arrays
    res = pl.pallas_call(
        body, name=name,
        out_shape=(pltpu.SemaphoreType.DMA((3 * n,)), pltpu.SemaphoreType.DMA((3 * n,)),
                   *[pltpu.HBM(a.shape, a.dtype) for a in slotted], *[pltpu.HBM(z.shape, z.dtype) for z in zones],
                   jax.ShapeDtypeStruct((8, 128), F32)),
        in_specs=[HBM] * (2 * n) + [pl.BlockSpec(memory_space=pl.ANY)],
        out_specs=(SEM, SEM, *[HBM] * (2 * n), pl.BlockSpec(memory_space=pltpu.VMEM)),
        input_output_aliases={t: 2 + t for t in range(2 * n)},
        compiler_params=pltpu.CompilerParams(has_side_effects=DATAFLOW),
    )(*[pltpu.with_memory_space_constraint(a, pltpu.HBM) for a in slotted],
      *[pltpu.with_memory_space_constraint(z, pltpu.HBM) for z in zones], after)
    return res[0], res[1], list(res[2:2 + n]), list(res[2 + n:2 + 2 * n]), res[-1]


def _scatter_wait(name, n_slotted, send_sems, recv_sems, thru, lands, after):
    n = len(thru)

    def body(*refs):
        ins, zones = refs[:n], refs[n:2 * n]
        s_sems, r_sems = refs[2 * n], refs[2 * n + 1]
        x, y, c = lax.axis_index("x"), lax.axis_index("y"), lax.axis_index("c")
        for t in range(n):
            for k in range(3):
                src, dst, (px, py) = _scatter_ends(t, k, n_slotted, ins, zones, x, y, _chip_index(x, y))
                cp = pltpu.make_async_remote_copy(
                    src_ref=src, dst_ref=dst if t < n_slotted else dst[1], send_sem=s_sems.at[3 * t + k],
                    recv_sem=r_sems.at[3 * t + k], device_id=(px, py, c), device_id_type=MESH)
                cp.wait_send()
                cp.wait_recv()

    res = pl.pallas_call(
        body, name=name,
        out_shape=(*[pltpu.HBM(a.shape, a.dtype) for a in thru], *[pltpu.HBM(z.shape, z.dtype) for z in lands]),
        in_specs=[HBM] * (2 * n) + [SEM, SEM, pl.BlockSpec(memory_space=pl.ANY)], out_specs=[HBM] * (2 * n),
        input_output_aliases={t: t for t in range(2 * n)},
        compiler_params=pltpu.CompilerParams(has_side_effects=DATAFLOW),
    )(*thru, *lands, send_sems, recv_sems, after)
    return list(res[:n]), list(res[n:])


def _sum_chips(name, slotted, recv, idx, dep):
    _, r, c = slotted.shape
    tr = _row_tile(r, 5 * c * 4)

    def body(idx_ref, own_ref, r_ref, _, o_ref):
        acc = own_ref[0]
        for k in range(3):
            acc = acc + r_ref[k]
        o_ref[...] = acc

    grid_spec = pltpu.PrefetchScalarGridSpec(
        num_scalar_prefetch=1, grid=(r // tr,),
        in_specs=[pl.BlockSpec((1, tr, c), lambda i, s: (s[1], i, 0)), pl.BlockSpec((3, tr, c), lambda i, s: (0, i, 0)),
                  pl.BlockSpec(memory_space=pl.ANY)],
        out_specs=pl.BlockSpec((tr, c), lambda i, s: (i, 0)))
    return pl.pallas_call(
        body, name=name, grid_spec=grid_spec, out_shape=jax.ShapeDtypeStruct((r, c), F32),
        compiler_params=_cparams(("parallel",)),
    )(idx, slotted, recv, dep)


def _half_shape(shape2d, axis):
    r, c = shape2d
    return (r, c // 2) if axis else (r // 2, c)


def _swap_halves(slotted, axes, small):
    n = len(slotted)

    def body(*refs):
        ins, sm_in = refs[:n], refs[n]
        outs, sm_out = refs[n + 1:2 * n + 1], refs[2 * n + 1]
        send_sems, recv_sems, local_sem = refs[2 * n + 2:]
        x, y, c = lax.axis_index("x"), lax.axis_index("y"), lax.axis_index("c")
        local = pltpu.make_async_copy(sm_in, sm_out.at[c], local_sem)
        local.start()
        sends = []
        for t in range(n):
            other = (slice(None),) + _half(slotted[t].shape[1:], axes[t], 1 - c)
            cp = pltpu.make_async_remote_copy(
                src_ref=ins[t].at[other], dst_ref=outs[t], send_sem=send_sems.at[t],
                recv_sem=recv_sems.at[t], device_id=(x, y, 1 - c), device_id_type=MESH)
            cp.start()
            sends.append(cp)
        cp = pltpu.make_async_remote_copy(
            src_ref=sm_in, dst_ref=sm_out.at[c], send_sem=send_sems.at[n], recv_sem=recv_sems.at[n],
            device_id=(x, y, 1 - c), device_id_type=MESH)
        cp.start()
        sends.append(cp)
        for cp in sends[:n]:
            cp.wait_recv()
        pltpu.make_async_remote_copy(
            src_ref=sm_in, dst_ref=sm_out.at[1 - c], send_sem=send_sems.at[n], recv_sem=recv_sems.at[n],
            device_id=(x, y, 1 - c), device_id_type=MESH).wait_recv()
        for cp in sends:
            cp.wait_send()
        local.wait()

    return pl.pallas_call(
        body, name="swap_halves",
        in_specs=[HBM] * (n + 1), out_specs=[HBM] * (n + 1),
        out_shape=[jax.ShapeDtypeStruct((a.shape[0],) + _half_shape(a.shape[1:], ax), a.dtype)
                   for a, ax in zip(slotted, axes, strict=True)]
        + [jax.ShapeDtypeStruct((2,) + small.shape, small.dtype)],
        scratch_shapes=[pltpu.SemaphoreType.DMA((n + 1,)), pltpu.SemaphoreType.DMA((n + 1,)), pltpu.SemaphoreType.DMA],
    )(*slotted, small)


def _tiling(r, c, f32_per_elem):
    if r % 8 == 0:
        tr = _row_tile(r, f32_per_elem * c * 4)
        return r // tr, (tr, c), lambda i: (i, 0)
    assert c % 128 == 0, (r, c)
    return c // 128, (r, 128), lambda i: (0, i)


def _pair_sum(name, slotted, other, idx, axis):
    _, r, c = slotted.shape
    hr, hc = _half_shape((r, c), axis)
    n, (tr, tc), at = _tiling(hr, hc, 13)
    if axis == 0:
        a = slotted.reshape(N_CHIPS, 2, hr, c)
        a_all = pl.BlockSpec((N_CHIPS, 1, tr, tc), lambda i, s: (0, s[0]) + at(i))
        a_own = pl.BlockSpec((1, 1, tr, tc), lambda i, s: (s[1], s[0]) + at(i))
    else:
        a, per_half = slotted, hc // tc
        a_all = pl.BlockSpec((N_CHIPS, tr, tc), lambda i, s: (0, at(i)[0], s[0] * per_half + at(i)[1]))
        a_own = pl.BlockSpec((1, tr, tc), lambda i, s: (s[1], at(i)[0], s[0] * per_half + at(i)[1]))

    def body(idx_ref, a_ref, b_ref, am_ref, bm_ref, p_ref, own_ref):
        mine, mine_own = (a_ref[:, 0], am_ref[0, 0]) if axis == 0 else (a_ref[...], am_ref[0])
        p_ref[...] = (mine + b_ref[...]).astype(p_ref.dtype)
        own_ref[...] = mine_own + bm_ref[0]

    grid_spec = pltpu.PrefetchScalarGridSpec(
        num_scalar_prefetch=1, grid=(n,),
        in_specs=[a_all, pl.BlockSpec((N_CHIPS, tr, tc), lambda i, s: (0,) + at(i)),
                  a_own, pl.BlockSpec((1, tr, tc), lambda i, s: (s[1],) + at(i))],
        out_specs=[pl.BlockSpec((N_CHIPS, tr, tc), lambda i, s: (0,) + at(i)),
                   pl.BlockSpec((tr, tc), lambda i, s: at(i))])
    return pl.pallas_call(
        body, name=name, grid_spec=grid_spec,
        out_shape=[jax.ShapeDtypeStruct((N_CHIPS, hr, hc), BF16), jax.ShapeDtypeStruct((hr, hc), F32)],
        compiler_params=_cparams(("parallel",)),
    )(idx, a, other, a, other)


def _sum_parts(name, own, recv):
    h, c = own.shape
    n, (tr, tc), at = _tiling(h, c, 4)

    def body(o_ref, r_ref, out_ref):
        acc = o_ref[...]
        for k in range(3):
            acc = acc + r_ref[k].astype(F32)
        out_ref[...] = acc

    return pl.pallas_call(
        body, name=name, grid=(n,),
        in_specs=[pl.BlockSpec((tr, tc), at), pl.BlockSpec((3, tr, tc), lambda i: (0,) + at(i))],
        out_specs=pl.BlockSpec((tr, tc), at),
        out_shape=jax.ShapeDtypeStruct((h, c), F32),
        compiler_params=_cparams(("parallel",)),
    )(own, recv)


def _all_to_all(name, small):
    def body(sm_in, sm_out, send_sems, recv_sems, local_sem):
        x, y, c = lax.axis_index("x"), lax.axis_index("y"), lax.axis_index("c")
        dev = 4 * x + 2 * y + c
        local = pltpu.make_async_copy(sm_in, sm_out.at[dev], local_sem)
        local.start()
        rel = [(fx, fy, fc) for fx in (0, 1) for fy in (0, 1) for fc in (0, 1)][1:]
        sends = []
        for k, (fx, fy, fc) in enumerate(rel):
            cp = pltpu.make_async_remote_copy(
                src_ref=sm_in, dst_ref=sm_out.at[dev], send_sem=send_sems.at[k], recv_sem=recv_sems.at[k],
                device_id=(x ^ fx, y ^ fy, c ^ fc), device_id_type=MESH)
            cp.start()
            sends.append(cp)
        for k, (fx, fy, fc) in enumerate(rel):
            src_dev = 4 * (x ^ fx) + 2 * (y ^ fy) + (c ^ fc)
            pltpu.make_async_remote_copy(
                src_ref=sm_in, dst_ref=sm_out.at[src_dev], send_sem=send_sems.at[k], recv_sem=recv_sems.at[k],
                device_id=(x ^ fx, y ^ fy, c ^ fc), device_id_type=MESH).wait_recv()
        for cp in sends:
            cp.wait_send()
        local.wait()

    return pl.pallas_call(
        body, name=name, in_specs=[HBM], out_specs=HBM,
        out_shape=jax.ShapeDtypeStruct((N_DEV,) + small.shape, small.dtype),
        scratch_shapes=[pltpu.SemaphoreType.DMA((7,)), pltpu.SemaphoreType.DMA((7,)), pltpu.SemaphoreType.DMA],
    )(small)


def _swap_sibling(name, parts):
    n = len(parts)

    def body(*refs):
        ins, outs = refs[:n], refs[n:2 * n]
        send_sems, recv_sems = refs[2 * n:]
        x, y, c = lax.axis_index("x"), lax.axis_index("y"), lax.axis_index("c")
        cps = []
        for t in range(n):
            cp = pltpu.make_async_remote_copy(
                src_ref=ins[t], dst_ref=outs[t], send_sem=send_sems.at[t], recv_sem=recv_sems.at[t],
                device_id=(x, y, 1 - c), device_id_type=MESH)
            cp.start()
            cps.append(cp)
        for cp in cps:
            cp.wait_recv()
        for cp in cps:
            cp.wait_send()

    return pl.pallas_call(
        body, name=name,
        in_specs=[HBM] * n, out_specs=[HBM] * n,
        out_shape=[jax.ShapeDtypeStruct(a.shape, a.dtype) for a in parts],
        scratch_shapes=[pltpu.SemaphoreType.DMA((n,)), pltpu.SemaphoreType.DMA((n,))],
    )(*parts)


def _sum_slots(name, a):
    k, r, c = a.shape
    tr = _row_tile(r, (k + 1) * c * 4)

    def body(a_ref, o_ref):
        acc = a_ref[0]
        for i in range(1, k):
            acc = acc + a_ref[i]
        o_ref[...] = acc

    return pl.pallas_call(
        body, name=name, grid=(r // tr,),
        in_specs=[pl.BlockSpec((k, tr, c), lambda i: (0, i, 0))],
        out_specs=pl.BlockSpec((tr, c), lambda i: (i, 0)),
        out_shape=jax.ShapeDtypeStruct((r, c), a.dtype),
        compiler_params=_cparams(("parallel",)),
    )(a)


def _adamw(name, w, m, v, g_parts):
    r, c = w.shape
    ng = len(g_parts)
    tr = _row_tile(r, (7 + ng) * c * 4)
    c1 = 1.0 - ADAM_B1 ** ADAM_STEP
    c2 = 1.0 - ADAM_B2 ** ADAM_STEP

    def body(*refs):
        w_ref, m_ref, v_ref = refs[:3]
        g_refs = refs[3:3 + ng]
        go_ref, d_ref, mo_ref, vo_ref = refs[3 + ng:]
        g = g_refs[0][...]
        for gr in g_refs[1:]:
            g = g + gr[...]
        m_new = ADAM_B1 * m_ref[...] + (1.0 - ADAM_B1) * g
        v_new = ADAM_B2 * v_ref[...] + (1.0 - ADAM_B2) * (g * g)
        go_ref[...] = g
        mo_ref[...] = m_new
        vo_ref[...] = v_new
        d_ref[...] = -ADAM_LR * ((m_new / c1) / (jnp.sqrt(v_new / c2) + ADAM_EPS) + ADAM_WD * w_ref[...])

    spec = pl.BlockSpec((tr, c), lambda i: (i, 0))
    return pl.pallas_call(
        body, name=name, grid=(r // tr,),
        in_specs=[spec] * (3 + ng), out_specs=[spec] * 4,
        out_shape=[jax.ShapeDtypeStruct((r, c), F32)] * 4,
        compiler_params=_cparams(("parallel",)),
    )(w, m, v, *g_parts)


def _adamw_halves(name, w, m, v, own, other, idx, axis):
    hr, hc = own.shape
    nb, (tr, tc), at = _tiling(hr, hc, 9)
    c1 = 1.0 - ADAM_B1 ** ADAM_STEP
    c2 = 1.0 - ADAM_B2 ** ADAM_STEP

    def body(idx_ref, w_ref, m_ref, v_ref, own_ref, oth_ref, go_ref, d_ref, mo_ref, vo_ref):
        g = jnp.where(pl.program_id(0) == idx_ref[0], own_ref[...], oth_ref[...])
        m_new = ADAM_B1 * m_ref[...] + (1.0 - ADAM_B1) * g
        v_new = ADAM_B2 * v_ref[...] + (1.0 - ADAM_B2) * (g * g)
        go_ref[...] = g
        mo_ref[...] = m_new
        vo_ref[...] = v_new
        d_ref[...] = -ADAM_LR * ((m_new / c1) / (jnp.sqrt(v_new / c2) + ADAM_EPS) + ADAM_WD * w_ref[...])

    per_half = (hc // tc) if axis else (hr // tr)
    if axis:
        full = pl.BlockSpec((tr, tc), lambda hh, i, s: (at(i)[0], hh * per_half + at(i)[1]))
    else:
        full = pl.BlockSpec((tr, tc), lambda hh, i, s: (hh * per_half + at(i)[0], at(i)[1]))
    part = pl.BlockSpec((tr, tc), lambda hh, i, s: at(i))
    grid_spec = pltpu.PrefetchScalarGridSpec(
        num_scalar_prefetch=1, grid=(2, nb), in_specs=[full, full, full, part, part], out_specs=[full] * 4)
    return pl.pallas_call(
        body, name=name, grid_spec=grid_spec, out_shape=[jax.ShapeDtypeStruct(w.shape, F32)] * 4,
        compiler_params=_cparams(("parallel", "parallel")),
    )(idx, w, m, v, own, other)


WEIGHTS = ['norm_w', 'w_in', 's5_a_re', 's5_a_im', 's5_b_re', 's5_b_im', 's5_c_re', 's5_c_im', 's5_d', 's5_log_step',
           's5_w_glu', 's5_b_glu', 'ssd_conv_w', 'ssd_conv_b', 'ssd_dt_bias', 'ssd_a_log', 'ssd_d', 'ssd_norm_w',
           'w_br_s5', 'w_br_ssd', 'w_out', 'ple_norm_w', 'w_ple_gate', 'w_ple_proj', 'final_norm_w']
SHARDED = {'w_in': ((IN_PROJ_DIM, 1024), 0), 's5_w_glu': ((512, 512), 0), 'ssd_conv_w': ((SSD_CONV, SSD_CONV_DIM), 1),
           'w_br_s5': ((512, 1024), 1), 'w_br_ssd': ((1536, 1024), 0), 'w_out': ((1024, 1024), 0),
           'w_ple_gate': ((1024, 1024), 0), 'w_ple_proj': ((256, 1024), 1)}
TRANSPOSED = ('w_in',)
SMALL = [n for n in WEIGHTS if n not in SHARDED]


def _shard_shape(name):
    (r, c), ax = SHARDED[name]
    return (r // N_CHIPS, c) if ax == 0 else (r, c // N_CHIPS)


def _half_axis(name):
    return 0 if (_shard_shape(name)[0] // 2) % 16 == 0 else 1


def _shard2d(name, a):
    r, c = _shard_shape(name)
    return a.reshape(c, r).T if name in TRANSPOSED else a.reshape(r, c)


def _unshard2d(name, a2, shape):
    return (a2.T if name in TRANSPOSED else a2).reshape(shape)


def _unslot(name, a4):
    (r, c), ax = SHARDED[name]
    if ax == 0:
        return a4.reshape(r, c)
    return jnp.transpose(a4, (1, 0, 2)).reshape(r, c)


def _slot(name, full):
    (r, c), ax = SHARDED[name]
    if ax == 0:
        return full.reshape(N_CHIPS, r // N_CHIPS, c)
    return jnp.transpose(full.reshape(r, N_CHIPS, c // N_CHIPS), (1, 0, 2))


GHP = ('s5_b_re', 's5_b_im')


def _view_shape(name):
    if name in ('s5_a_re', 's5_a_im'):
        return (S5_GROUPS, S5_STATE)
    if name in GHP + ('s5_c_re', 's5_c_im'):
        return (S5_GROUPS, S5_GROUP, S5_STATE)
    if name == 'ssd_conv_w':
        return (SSD_CONV, SSD_CONV_DIM // N_CHIPS)
    return (1, {'s5_log_step': S5_GROUPS, 'ssd_conv_b': SSD_CONV_DIM, 'ssd_norm_w': SSD_WIDTH, 's5_d': S5_WIDTH,
                's5_b_glu': S5_WIDTH, 'ssd_dt_bias': SSD_HEADS, 'ssd_a_log': SSD_HEADS, 'ssd_d': SSD_HEADS}.get(name, D_MODEL))


def _view(name, a):
    if name in GHP:
        return jnp.swapaxes(a.reshape(S5_GROUPS, S5_STATE, S5_GROUP), 1, 2)
    return a.reshape(_view_shape(name))


def _unview(name, a, shape):
    return (jnp.swapaxes(a, 1, 2) if name in GHP else a).reshape(shape)


def _adamw_small(ws, ms, vs, gs):
    n = len(ws)
    c1 = 1.0 - ADAM_B1 ** ADAM_STEP
    c2 = 1.0 - ADAM_B2 ** ADAM_STEP

    def body(*refs):
        w_r, m_r, v_r, g_r = (refs[k * n:(k + 1) * n] for k in range(4))
        d_o, m_o, v_o = (refs[k * n:(k + 1) * n] for k in range(4, 7))
        for i in range(n):
            g = g_r[i][...]
            m_new = ADAM_B1 * m_r[i][...] + (1.0 - ADAM_B1) * g
            v_new = ADAM_B2 * v_r[i][...] + (1.0 - ADAM_B2) * (g * g)
            m_o[i][...] = m_new
            v_o[i][...] = v_new
            d_o[i][...] = -ADAM_LR * ((m_new / c1) / (jnp.sqrt(v_new / c2) + ADAM_EPS) + ADAM_WD * w_r[i][...])

    return pl.pallas_call(
        body, name="adamw_small", out_shape=[jax.ShapeDtypeStruct(w.shape, F32) for w in ws] * 3,
        compiler_params=pltpu.CompilerParams(vmem_limit_bytes=VMEM_LIMIT),
    )(*ws, *ms, *vs, *gs)


def _pack_small(vals):
    flat = jnp.concatenate([v.reshape(-1).astype(F32) for v in vals])
    rows = -(-flat.shape[0] // (256 * 128)) * 256
    return jnp.pad(flat, (0, rows * 128 - flat.shape[0])).reshape(rows, 128)


def _unpack_small(packed, shapes):
    flat = packed.reshape(-1)
    out, off = [], 0
    for sh in shapes:
        n = math.prod(sh)
        out.append(flat[off:off + n].reshape(sh))
        off += n
    return out


def kernel(x, p, norm_w, w_in, s5_a_re, s5_a_im, s5_b_re, s5_b_im, s5_c_re, s5_c_im, s5_d, s5_log_step, s5_w_glu, s5_b_glu, ssd_conv_w, ssd_conv_b, ssd_dt_bias, ssd_a_log, ssd_d, ssd_norm_w, w_br_s5, w_br_ssd, w_out, ple_norm_w, w_ple_gate, w_ple_proj, final_norm_w, loss_target, m_norm_w, m_w_in, m_s5_a_re, m_s5_a_im, m_s5_b_re, m_s5_b_im, m_s5_c_re, m_s5_c_im, m_s5_d, m_s5_log_step, m_s5_w_glu, m_s5_b_glu, m_ssd_conv_w, m_ssd_conv_b, m_ssd_dt_bias, m_ssd_a_log, m_ssd_d, m_ssd_norm_w, m_w_br_s5, m_w_br_ssd, m_w_out, m_ple_norm_w, m_w_ple_gate, m_w_ple_proj, m_final_norm_w, v_norm_w, v_w_in, v_s5_a_re, v_s5_a_im, v_s5_b_re, v_s5_b_im, v_s5_c_re, v_s5_c_im, v_s5_d, v_s5_log_step, v_s5_w_glu, v_s5_b_glu, v_ssd_conv_w, v_ssd_conv_b, v_ssd_dt_bias, v_ssd_a_log, v_ssd_d, v_ssd_norm_w, v_w_br_s5, v_w_br_ssd, v_w_out, v_ple_norm_w, v_w_ple_gate, v_w_ple_proj, v_final_norm_w):
    args = locals()
    wl = {n: args[n] for n in WEIGHTS}
    ml = {n: args["m_" + n] for n in WEIGHTS}
    vl = {n: args["v_" + n] for n in WEIGHTS}
    big = [n for n in SHARDED if n != 'ssd_conv_w']
    chip = _chip_index(lax.axis_index("x"), lax.axis_index("y"))
    idx = jnp.stack([lax.axis_index("c"), chip]).astype(jnp.int32)

    axes = [_half_axis(n) for n in big]
    first = ['w_in']
    rest = [n for n in big if n not in first]
    bf_shards = {n: _shard2d(n, wl[n]).astype(BF16) for n in big}
    w_in_t, conv_w4 = _gather_chips([bf_shards[n] for n in first], [_half_axis(n) for n in first],
                                    [_shard2d('ssd_conv_w', wl['ssd_conv_w'])])
    full = {n: wl[n] for n in SMALL}
    full["w_in_t"] = w_in_t
    full['ssd_conv_w'] = _unslot('ssd_conv_w', conv_w4)
    send_sems, recv_sems, thru, lands, token = _gather_start([bf_shards[n] for n in rest], w_in_t)

    def fetch_rest(after):
        own, zones = _gather_wait(send_sems, recv_sems, thru, lands, after)
        return {n: _unslot(n, lax.dynamic_update_slice(z, o[None], (chip, 0, 0)))
                for n, o, z in zip(rest, own, zones, strict=True)}

    full["_late"] = (token, fetch_rest)

    in_flight = []

    def send_rest(g_now):
        in_flight.extend(_scatter_start("scatter_rest_start", [_slot(n, g_now[n]) for n in rest], [], g_now['s5_w_glu']))
        return in_flight[4]

    full["_early"] = send_rest

    packed = [n for n in SMALL if n != 'norm_w']
    first_axes = [_half_axis(n) for n in first]
    st = {}

    def send_first(g_now, loss_now):
        small_pack = _pack_small([loss_now] + [g_now[n + "_ghp" if n in GHP else n] for n in packed]
                                 + [g_now['ssd_conv_w']])
        swapped = _swap_halves([g_now["w_in_t"]], first_axes, small_pack)
        st["pair"] = [_pair_sum("pair_sum_w_in", g_now["w_in_t"], swapped[0], idx, first_axes[0])]
        small_chip = _sum_slots("sum_small_pair", swapped[-1])
        half_rows = small_chip.shape[0] // 2
        my_half = lax.dynamic_slice(small_chip, (lax.axis_index("c") * half_rows, 0), (half_rows, small_chip.shape[1]))
        st["flight"] = _scatter_start("scatter_first_start", [pb for pb, _ in st["pair"]], [my_half], small_chip)
        return st["flight"][4]

    full["_w_in_ready"] = send_first
    loss, grad_x, g = _local_step(x, p[0], loss_target, full)
    pair = st["pair"]
    s_sems, r_sems, thru, lands, tok = st["flight"]
    small_shapes = [(1, 1)] + [_view_shape(n) for n in packed] + [(SSD_CONV, SSD_CONV_DIM)]
    norm_all = _all_to_all("norm_w_all_to_all", g['norm_w'].reshape(8, 128))
    norm_g = _sum_slots("sum_norm_w", norm_all).reshape(1, D_MODEL)

    out_g, out_d, out_m, out_v = {}, {}, {}, {}
    own_slots, zones = _scatter_wait("scatter_rest_wait", len(rest), *in_flight[:4], norm_g)
    chip_sums = [_sum_chips("sum_chips_" + n, a, z, idx, tok) for n, a, z in zip(rest, own_slots, zones, strict=True)]
    sib_sums = _swap_sibling("swap_sibling_rest", chip_sums)
    for n, own, sib in zip(rest, chip_sums, sib_sums, strict=True):
        res = _adamw("adamw_" + n, _shard2d(n, wl[n]), _shard2d(n, ml[n]), _shard2d(n, vl[n]), [own, sib])
        out_g[n], out_d[n], out_m[n], out_v[n] = (_unshard2d(n, r, wl[n].shape) for r in res)

    sent, got = _scatter_wait("scatter_first_wait", len(first), s_sems, r_sems, thru, lands, out_d[rest[-1]])
    small_zone = lax.dynamic_update_slice(got[-1], sent[-1][None], (chip, 0, 0))
    small_half = _sum_slots("sum_small_chips", small_zone)
    halves = [_sum_parts("sum_chips_" + n, own, r) for n, (_, own), r in zip(first, pair, got[:-1], strict=True)]
    swapped2 = _swap_sibling("swap_sibling_first", halves + [small_half])
    other_halves = swapped2[:-1]
    c_is_0 = lax.axis_index("c") == 0
    small_sum = jnp.concatenate([jnp.where(c_is_0, small_half, swapped2[-1]),
                                 jnp.where(c_is_0, swapped2[-1], small_half)], axis=0)
    for n, own, oth, ax in zip(first, halves, other_halves, first_axes, strict=True):
        res = _adamw_halves("adamw_" + n, _shard2d(n, wl[n]), _shard2d(n, ml[n]), _shard2d(n, vl[n]), own, oth, idx, ax)
        out_g[n], out_d[n], out_m[n], out_v[n] = (_unshard2d(n, r, wl[n].shape) for r in res)
    sm = _unpack_small(small_sum, small_shapes)
    loss_total = sm[0].reshape(())
    conv_g = lax.dynamic_slice(sm[-1], (0, chip * (SSD_CONV_DIM // N_CHIPS)), (SSD_CONV, SSD_CONV_DIM // N_CHIPS))
    names = SMALL + ['ssd_conv_w']
    by_name = {**dict(zip(packed, sm[1:-1], strict=True)), 'norm_w': norm_g, 'ssd_conv_w': conv_g}
    grads = [by_name[n] for n in names]
    res = _adamw_small([_view(n, wl[n]) for n in names], [_view(n, ml[n]) for n in names],
                       [_view(n, vl[n]) for n in names], grads)
    for i, n in enumerate(names):
        out_g[n] = _unview(n, grads[i], wl[n].shape)
        out_d[n], out_m[n], out_v[n] = (_unview(n, res[k * len(names) + i], wl[n].shape) for k in range(3))

    return (loss_total, grad_x, *[out_g[n] for n in WEIGHTS], *[out_d[n] for n in WEIGHTS],
            *[out_m[n] for n in WEIGHTS], *[out_v[n] for n in WEIGHTS])
```

```python
import functools
import math

import jax
import jax.numpy as jnp
from jax import lax
from jax.experimental import pallas as pl
from jax.experimental.pallas import tpu as pltpu

F32 = jnp.float32
BF16 = jnp.bfloat16
MESH = pl.DeviceIdType.MESH

D_MODEL = 1024
PLE_DIM = 256
RMS_EPS = 1e-6
S5_WIDTH = 512
S5_GROUP = 16
S5_GROUPS = 32
S5_STATE = 64
S5_N = S5_GROUPS * S5_STATE
S5_LB = 512
S5_NJ = S5_N // S5_LB
S5_TB = 1024
S5_LOG_TB = 8
SSD_WIDTH = 1536
SSD_HEADDIM = 64
SSD_HEADS = 24
SSD_GROUPS = 4
SSD_HPG = 6
SSD_STATE = 128
SSD_CONV = 4
SSD_CHUNK = 128
SSD_BC = 512
SSD_CONV_DIM = 2560
GROUP_W = SSD_WIDTH // SSD_GROUPS
N_CHIPS = 4
N_DEV = 8

OFF_XBC, OFF_U5, OFF_Z5, OFF_DT, OFF_G5, OFF_GS, OFF_ZS = 0, 2560, 3072, 3584, 4096, 5120, 6144
DT_W = 512
PROJ_W = 7680
IN_PROJ_DIM = 7192

ADAM_LR, ADAM_B1, ADAM_B2, ADAM_EPS, ADAM_WD, ADAM_STEP = 0.001, 0.9, 0.999, 1e-08, 0.01, 10

VMEM_LIMIT = 56 * 1024 * 1024


ROW_BLOCK_BYTES = 8 * 1024 * 1024


def _row_tile(r, bytes_per_row):
    for t in (r, 4096, 2048, 1024, 512, 256, 128, 64, 32, 16, 8):
        if t <= r and r % t == 0 and t * bytes_per_row <= ROW_BLOCK_BYTES:
            return t
    return r


def _cparams(sem):
    return pltpu.CompilerParams(dimension_semantics=sem, vmem_limit_bytes=VMEM_LIMIT)


def _dg(a, b, ca, cb):
    return lax.dot_general(a.astype(BF16), b.astype(BF16), (((ca,), (cb,)), ((), ())), preferred_element_type=F32)


@jax.custom_vjp
def dot_nn(a, b):
    return _dg(a, b, 1, 0)


@jax.custom_vjp
def dot_nt(a, b):
    return _dg(a, b, 1, 1)


@jax.custom_vjp
def dot_tn(a, b):
    return _dg(a, b, 0, 0)


dot_nn.defvjp(lambda a, b: (_dg(a, b, 1, 0), (a, b)), lambda r, g: (_dg(g, r[1], 1, 1), _dg(r[0], g, 0, 0)))
dot_nt.defvjp(lambda a, b: (_dg(a, b, 1, 1), (a, b)), lambda r, g: (_dg(g, r[1], 1, 0), _dg(g, r[0], 0, 0)))
dot_tn.defvjp(lambda a, b: (_dg(a, b, 0, 0), (a, b)), lambda r, g: (_dg(r[1], g, 1, 1), _dg(r[0], g, 1, 0)))


MM_VMEM_BUDGET = 40 * 1024 * 1024


def _mm_tiles(m, n, k, sa, sb, so, tn_only=None):
    best, best_key = None, None
    for tm in (1024, 512, 256, 128, 64, 32, 16, 8):
        if m % tm:
            continue
        for tn in (2048, 1536, 1280, 1024, 768, 640, 512, 384, 256, 128):
            if n % tn or (tn_only is not None and tn not in tn_only):
                continue
            for tk in (k, 2048, 1536, 1280, 1024, 768, 512, 256, 128):
                if k % tk or tk > max(k, 128):
                    continue
                need = 2 * (tm * tk * sa + tk * tn * sb + tm * tn * so) + (tm * tn * 4 if tk < k else 0)
                if need > MM_VMEM_BUDGET:
                    continue
                key = (tm * tn * tk, tk)
                if best_key is None or key > best_key:
                    best, best_key = (tm, tn, tk), key
    assert best is not None, (m, n, k)
    return best


def _matmul(name, a, b, *, ta=False, tb=False, a_win=None, out_dtype=F32, epilogue=None, epi_rows=(), epi_pars=(),
            epi_outs=(), full_rows=False, epi_accs=(), epi_into=None, dep=None):
    a_off, a_w = a_win if a_win is not None else (0, a.shape[1])
    if ta:
        kdim, m = a.shape[0], a_w
    else:
        m, kdim = a.shape[0], a_w
    n = b.shape[0] if tb else b.shape[1]
    assert (b.shape[1] if tb else b.shape[0]) == kdim, (name, a.shape, b.shape)
    out_dtypes = list(epi_outs) if epilogue is not None else [out_dtype]
    so = sum(jnp.dtype(d).itemsize for d in out_dtypes) + sum(r.dtype.itemsize for r, _ in epi_rows)
    tn_ok = [n] if full_rows else [t for t in (1024, 512, 256, 128) if all(off % t == 0 for _, off in epi_rows)]
    tm, tn, tk = _mm_tiles(m, n, kdim, a.dtype.itemsize, b.dtype.itemsize, so, tn_ok if epilogue is not None else None)
    nk = kdim // tk
    n_er, n_ep, n_out = len(epi_rows), len(epi_pars), len(out_dtypes)
    if ta:
        assert a_off % tm == 0
        a_spec = pl.BlockSpec((tk, tm), lambda i, j, k: (k, i + a_off // tm))
    else:
        assert a_off % tk == 0
        a_spec = pl.BlockSpec((tm, tk), lambda i, j, k: (i, k + a_off // tk))
    if tb:
        b_spec = pl.BlockSpec((tn, tk), lambda i, j, k: (j, k))
    else:
        b_spec = pl.BlockSpec((tk, tn), lambda i, j, k: (k, j))
    ca, cb = (0 if ta else 1), (1 if tb else 0)

    n_acc = len(epi_accs)
    n_x = (1 if epi_into is not None else 0) + (1 if dep is not None else 0)
    assert not (n_acc or epi_into is not None) or full_rows

    def body(a_ref, b_ref, *refs):
        er, ep = refs[:n_er], refs[n_er:n_er + n_ep]
        first_out = n_er + n_ep + n_x
        o_refs = refs[first_out:first_out + n_out]
        s_refs = refs[first_out + n_out:first_out + n_out + n_acc]
        acc = refs[first_out + n_out + n_acc:]

        def finish(c):
            outs = [c] if epilogue is None else epilogue(c, [r[...] for r in er], [p[...] for p in ep])
            if n_acc:
                outs, sums = outs

                @pl.when(pl.program_id(0) == 0)
                def _():
                    for s_ref in s_refs:
                        s_ref[...] = jnp.zeros_like(s_ref)

                for s_ref, s in zip(s_refs, sums, strict=True):
                    s_ref[...] += jnp.broadcast_to(s, s_ref.shape)
            for o_ref, o in zip(o_refs, outs, strict=True):
                o_ref[...] = o.astype(o_ref.dtype)

        if nk == 1:
            finish(_dg(a_ref[...], b_ref[...], ca, cb))
            return
        (acc_ref,) = acc
        k = pl.program_id(2)

        @pl.when(k == 0)
        def _():
            acc_ref[...] = jnp.zeros_like(acc_ref)

        acc_ref[...] += _dg(a_ref[...], b_ref[...], ca, cb)

        @pl.when(k == nk - 1)
        def _():
            finish(acc_ref[...])

    in_specs = [a_spec, b_spec]
    in_specs += [pl.BlockSpec((tm, tn), functools.partial(lambda i, j, k, c: (i, j + c), c=off // tn)) for _, off in epi_rows]
    in_specs += [pl.BlockSpec((p.shape[0], tn), lambda i, j, k: (0, j)) for p in epi_pars]
    out_specs = [pl.BlockSpec((tm, tn), lambda i, j, k: (i, j)) for _ in out_dtypes]
    out_shape = [jax.ShapeDtypeStruct((m, n), d) for d in out_dtypes]
    extra, aliases = [], {}
    if epi_into is not None:
        buf, off, width = epi_into
        assert off % width == 0 and buf.dtype == out_dtypes[0]
        in_specs.append(pl.BlockSpec(memory_space=pl.ANY))
        out_specs[0] = pl.BlockSpec((tm, width), functools.partial(lambda i, j, k, c: (i, c), c=off // width))
        out_shape[0] = jax.ShapeDtypeStruct(buf.shape, buf.dtype)
        extra, aliases = [buf], {2 + n_er + n_ep: 0}
    if dep is not None:
        in_specs.append(pl.BlockSpec(memory_space=pl.ANY))
        extra = extra + [dep]
    out_specs += [pl.BlockSpec((r, w), lambda i, j, k: (0, 0)) for r, w in epi_accs]
    out_shape += [jax.ShapeDtypeStruct((r, w), F32) for r, w in epi_accs]
    res = pl.pallas_call(
        body, name=name, grid=(m // tm, n // tn, nk),
        in_specs=in_specs, out_specs=out_specs, out_shape=out_shape, input_output_aliases=aliases,
        scratch_shapes=[pltpu.VMEM((tm, tn), F32)] if nk > 1 else [],
        compiler_params=_cparams(("arbitrary",) * 3 if n_acc else ("parallel", "parallel", "arbitrary")),
    )(a, b, *[r for r, _ in epi_rows], *epi_pars, *extra)
    return res if epilogue is not None else res[0]


BAND = 128


def _rowwise(name, fn, n_rows, tr, row_ins, par_ins, row_outs, acc_outs=(), into=None):
    nr, npar, no, na = len(row_ins), len(par_ins), len(row_outs), len(acc_outs)
    in_specs = []
    for arr, off, w in row_ins:
        assert off % w == 0 and arr.shape[0] == n_rows, (name, arr.shape, off, w)
        in_specs.append(pl.BlockSpec((tr, w), functools.partial(lambda i, c: (i, c), c=off // w)))
    for arr, off, w in par_ins:
        assert off % w == 0
        in_specs.append(pl.BlockSpec((arr.shape[0], w), functools.partial(lambda i, c: (0, c), c=off // w)))
    out_specs = [pl.BlockSpec((tr, w), lambda i: (i, 0)) for w, _ in row_outs]
    out_specs += [pl.BlockSpec((r, w), lambda i: (0, 0)) for r, w in acc_outs]
    out_shape = [jax.ShapeDtypeStruct((n_rows, w), dt) for w, dt in row_outs]
    out_shape += [jax.ShapeDtypeStruct((r, w), F32) for r, w in acc_outs]
    extra, aliases = [], {}
    if into is not None:
        buf, off = into
        w0 = row_outs[0][0]
        assert off % w0 == 0 and buf.dtype == row_outs[0][1]
        in_specs.append(pl.BlockSpec(memory_space=pl.ANY))
        out_specs[0] = pl.BlockSpec((tr, w0), functools.partial(lambda i, c: (i, c), c=off // w0))
        out_shape[0] = jax.ShapeDtypeStruct(buf.shape, buf.dtype)
        extra, aliases = [buf], {nr + npar: 0}
    nx = len(extra)

    def body(*refs):
        rows = [r[...] for r in refs[:nr]]
        pars = [r[...] for r in refs[nr:nr + npar]]
        o_refs = refs[nr + npar + nx:nr + npar + nx + no]
        a_refs = refs[nr + npar + nx + no:]
        outs, accs = fn(rows, pars)
        for o_ref, o in zip(o_refs, outs, strict=True):
            o_ref[...] = o.astype(o_ref.dtype)
        if na:
            @pl.when(pl.program_id(0) == 0)
            def _():
                for a_ref in a_refs:
                    a_ref[...] = jnp.zeros_like(a_ref)

            for a_ref, a in zip(a_refs, accs, strict=True):
                a_ref[...] += jnp.broadcast_to(a, a_ref.shape)

    res = pl.pallas_call(
        body, name=name, grid=(n_rows // tr,),
        in_specs=in_specs, out_specs=out_specs, out_shape=out_shape, input_output_aliases=aliases,
        compiler_params=_cparams(("arbitrary",) if na else ("parallel",)),
    )(*[a for a, _, _ in row_ins], *[a for a, _, _ in par_ins], *extra)
    return res


def _rms(x, w):
    return x * lax.rsqrt(jnp.mean(x * x, axis=-1, keepdims=True) + RMS_EPS) * w


def _s5_out(yc, u, z, t, d, bg):
    ge = jax.nn.gelu(yc + d * u)
    return ge * jax.nn.sigmoid(t + bg) * jax.nn.silu(z)


def _merge(g5, gs, m5, ms):
    return jax.nn.sigmoid(g5) * m5 + jax.nn.sigmoid(gs) * ms


def _head_loss(h1, pgl, pp, fw, tgt):
    h2 = h1 + jax.nn.sigmoid(pgl) * pp
    err = _rms(h2, fw) - tgt
    per_row = 0.5 * jnp.mean(err * err, axis=-1, keepdims=True)
    return jnp.sum(per_row, axis=0, keepdims=True)


def _s5_disc(a_re, a_im, log_step, b_re2, b_im2, expand):
    step = jnp.exp(log_step)
    mag = jnp.exp(a_re * step)
    lb_re = mag * jnp.cos(a_im * step)
    lb_im = mag * jnp.sin(a_im * step)
    den = a_re * a_re + a_im * a_im
    n_re = lb_re - 1.0
    f_re = (n_re * a_re + lb_im * a_im) / den
    f_im = (lb_im * a_re - n_re * a_im) / den
    hi = lax.Precision.HIGHEST
    fr = jnp.dot(expand, f_re, precision=hi, preferred_element_type=F32)
    fi = jnp.dot(expand, f_im, precision=hi, preferred_element_type=F32)
    return lb_re, lb_im, fr * b_re2 - fi * b_im2, fr * b_im2 + fi * b_re2


def _s5_params_fwd(a_re, a_im, log_step, b_re2, b_im2, expand):
    gp = a_re.shape

    def body(ar, ai, ls, br, bi, ex, pr_ref, pi_ref, bbr_ref, bbi_ref):
        lr, li, bbr, bbi = _s5_disc(ar[...], ai[...], ls[...], br[...], bi[...], ex[...])
        bbr_ref[...] = bbr
        bbi_ref[...] = bbi
        qr, qi = lr, li
        for k in range(S5_LOG_TB):
            pr_ref[k] = qr
            pi_ref[k] = qi
            qr, qi = qr * lr - qi * li, qr * li + qi * lr

    return pl.pallas_call(
        body, name="s5_params_fwd",
        out_shape=(jax.ShapeDtypeStruct((S5_LOG_TB,) + gp, F32), jax.ShapeDtypeStruct((S5_LOG_TB,) + gp, F32),
                   jax.ShapeDtypeStruct(b_re2.shape, F32), jax.ShapeDtypeStruct(b_re2.shape, F32)),
    )(a_re, a_im, log_step, b_re2, b_im2, expand)


def _s5_params_bwd(a_re, a_im, log_step, b_re2, b_im2, expand, d_lr, d_li, d_bbr, d_bbi):
    def body(ar, ai, ls, br, bi, ex, glr, gli, gbr, gbi, dar, dai, dls, dbr, dbi):
        _, vjp = jax.vjp(lambda *p: _s5_disc(*p, ex[...]), ar[...], ai[...], ls[...], br[...], bi[...])
        g = vjp((glr[...], gli[...], gbr[...], gbi[...]))
        for ref, val in zip((dar, dai, dls, dbr, dbi), g, strict=True):
            ref[...] = val

    return pl.pallas_call(
        body, name="s5_params_bwd",
        out_shape=tuple(jax.ShapeDtypeStruct(v.shape, F32) for v in (a_re, a_im, log_step, b_re2, b_im2)),
    )(a_re, a_im, log_step, b_re2, b_im2, expand, d_lr, d_li, d_bbr, d_bbi)


def _scan_block(x_ref, out_ref, lp, edge_pow, cr, ci, reverse, each=None):
    n_g = x_ref.shape[0] // 8
    sub = lax.broadcasted_iota(jnp.int32, (8, S5_LB), 0)
    steps = []
    for sh in (1, 2, 4):
        keep = (sub < 8 - sh) if reverse else (sub >= sh)
        steps.append((8 - sh if reverse else sh, jnp.where(keep, lp[sh - 1:sh, :S5_LB], 0.0),
                      jnp.where(keep, lp[sh - 1:sh, S5_LB:], 0.0)))
    e_r, e_i = edge_pow[:, :S5_LB], edge_pow[:, S5_LB:]
    for r in (range(n_g - 1, -1, -1) if reverse else range(n_g)):
        rows = slice(8 * r, 8 * r + 8)
        xr, xi = x_ref[rows, :S5_LB], x_ref[rows, S5_LB:]
        for by, a_r, a_i in steps:
            pr, pi = pltpu.roll(xr, by, 0), pltpu.roll(xi, by, 0)
            xr, xi = xr + a_r * pr - a_i * pi, xi + a_r * pi + a_i * pr
        xr = xr + e_r * cr - e_i * ci
        xi = xi + e_r * ci + e_i * cr
        out_ref[rows, :S5_LB] = xr
        out_ref[rows, S5_LB:] = xi
        if each is not None:
            each(r, xr, xi)
        cr, ci = (xr[0:1, :], xi[0:1, :]) if reverse else (xr[7:8, :], xi[7:8, :])
    return cr, ci


def _s5_fwd(proj, bb_band, c_band, lam_pow, s5_d, n_seq, seq_len):
    n_t = seq_len // S5_TB
    blk = 2 * S5_LB
    n_rows = n_seq * seq_len
    u_blk0 = OFF_U5 // BAND

    def body(u_ref, bb_ref, cb_ref, lp_ref, d_ref, s_ref, y_ref, ge_ref, cr, ci, buf):
        @pl.when(pl.program_id(2) == 0)
        def _():
            cr[...] = jnp.zeros_like(cr)
            ci[...] = jnp.zeros_like(ci)

        u = u_ref[...]
        buf[...] = _dg(u, bb_ref[...], 1, 0)
        lp = lp_ref[...]
        cr[...], ci[...] = _scan_block(buf, buf, lp, lp, cr[...], ci[...], False)
        s = buf[...].astype(s_ref.dtype)
        s_ref[...] = s
        y = _dg(s, cb_ref[...], 1, 1)
        y_ref[...] = y
        ge_ref[...] = jax.nn.gelu(y + d_ref[...] * u).astype(ge_ref.dtype)

    def rows(j, b, t):
        return (b * n_t + t, j)

    return pl.pallas_call(
        body, name="s5_fwd", grid=(S5_NJ, n_seq, n_t),
        in_specs=[pl.BlockSpec((S5_TB, BAND), lambda j, b, t: (b * n_t + t, u_blk0 + j)),
                  pl.BlockSpec((BAND, blk), lambda j, b, t: (j, 0)), pl.BlockSpec((BAND, blk), lambda j, b, t: (j, 0)),
                  pl.BlockSpec((S5_LOG_TB, blk), lambda j, b, t: (0, j)), pl.BlockSpec((1, BAND), lambda j, b, t: (0, j))],
        out_specs=[pl.BlockSpec((S5_TB, blk), rows), pl.BlockSpec((S5_TB, BAND), rows), pl.BlockSpec((S5_TB, BAND), rows)],
        out_shape=[jax.ShapeDtypeStruct((n_rows, S5_NJ * blk), BF16), jax.ShapeDtypeStruct((n_rows, S5_WIDTH), F32),
                   jax.ShapeDtypeStruct((n_rows, S5_WIDTH), BF16)],
        scratch_shapes=[pltpu.VMEM((1, S5_LB), F32), pltpu.VMEM((1, S5_LB), F32), pltpu.VMEM((S5_TB, blk), F32)],
        compiler_params=_cparams(("parallel", "parallel", "arbitrary")),
    )(proj, bb_band, c_band, lam_pow, s5_d)


def _s5_bwd(dyc, s, proj, du5_a, bb_band, c_band, lam_pow_conj, dproj, n_seq, seq_len):
    n_t = seq_len // S5_TB
    blk = 2 * S5_LB
    halo_rows = 16
    halo_per_blk = S5_TB // halo_rows
    u_blk0 = OFF_U5 // BAND

    def rows(j, b, t):
        return (b * n_t + (n_t - 1 - t), j)

    def u_rows(j, b, t):
        return (b * n_t + (n_t - 1 - t), u_blk0 + j)

    def halo(j, b, t):
        return (jnp.maximum((b * n_t + (n_t - 1 - t)) * halo_per_blk - 1, 0), j)

    def body(dy_ref, sq_ref, hq_ref, u_ref, dua_ref, bb_ref, cb_ref, lp_ref, _, du_ref, dl_ref, dbb_ref, dcb_ref,
             cr, ci, buf, s_ref):
        b, t = pl.program_id(1), pl.program_id(2)
        sq = sq_ref[...]
        s_ref[...] = sq.astype(F32)
        h_last = hq_ref[...].astype(F32)[halo_rows - 1:halo_rows, :]
        dy = dy_ref[...]
        buf[...] = _dg(dy, cb_ref[...], 1, 0)

        @pl.when(t == 0)
        def _():
            cr[...] = jnp.zeros_like(cr)
            ci[...] = jnp.zeros_like(ci)

        @pl.when((b == 0) & (t == 0))
        def _():
            dl_ref[...] = jnp.zeros_like(dl_ref)
            dbb_ref[...] = jnp.zeros_like(dbb_ref)
            dcb_ref[...] = jnp.zeros_like(dcb_ref)

        dcb_ref[...] += _dg(dy, sq, 0, 0)

        first_blk = t == n_t - 1
        sub = lax.broadcasted_iota(jnp.int32, (8, S5_LB), 0)
        acc = [jnp.zeros((8, S5_LB), F32), jnp.zeros((8, S5_LB), F32)]

        def each(r, gr, gi):
            rows = slice(8 * r, 8 * r + 8)
            if r == 0:
                before_r = jnp.where(first_blk, 0.0, h_last[:, :S5_LB])
                before_i = jnp.where(first_blk, 0.0, h_last[:, S5_LB:])
            else:
                before_r, before_i = s_ref[8 * r - 1:8 * r, :S5_LB], s_ref[8 * r - 1:8 * r, S5_LB:]
            sp_r = jnp.where(sub == 0, before_r, pltpu.roll(s_ref[rows, :S5_LB], 1, 0))
            sp_i = jnp.where(sub == 0, before_i, pltpu.roll(s_ref[rows, S5_LB:], 1, 0))
            acc[0] = acc[0] + gr * sp_r + gi * sp_i
            acc[1] = acc[1] + gi * sp_r - gr * sp_i

        lp = lp_ref[...]
        edge_pow = jnp.concatenate([lp[7 - i:8 - i, :] for i in range(8)], axis=0)
        cr[...], ci[...] = _scan_block(buf, buf, lp, edge_pow, cr[...], ci[...], True, each)
        g = buf[...].astype(BF16)
        du_ref[...] = (dua_ref[...] + _dg(g, bb_ref[...], 1, 1)).astype(du_ref.dtype)
        dbb_ref[...] += _dg(u_ref[...], g, 0, 0)
        dl_ref[:, :S5_LB] += jnp.sum(acc[0], axis=0, keepdims=True)
        dl_ref[:, S5_LB:] += jnp.sum(acc[1], axis=0, keepdims=True)

    band = pl.BlockSpec((BAND, blk), lambda j, b, t: (j, 0))
    return pl.pallas_call(
        body, name="s5_bwd", grid=(S5_NJ, n_seq, n_t),
        in_specs=[pl.BlockSpec((S5_TB, BAND), rows), pl.BlockSpec((S5_TB, blk), rows),
                  pl.BlockSpec((halo_rows, blk), halo), pl.BlockSpec((S5_TB, BAND), u_rows),
                  pl.BlockSpec((S5_TB, BAND), rows), band, band,
                  pl.BlockSpec((S5_LOG_TB, blk), lambda j, b, t: (0, j)), pl.BlockSpec(memory_space=pl.ANY)],
        out_specs=[pl.BlockSpec((S5_TB, BAND), u_rows), pl.BlockSpec((1, blk), lambda j, b, t: (0, j)), band, band],
        out_shape=[jax.ShapeDtypeStruct(dproj.shape, dproj.dtype), jax.ShapeDtypeStruct((1, S5_NJ * blk), F32),
                   jax.ShapeDtypeStruct(bb_band.shape, F32), jax.ShapeDtypeStruct(c_band.shape, F32)],
        input_output_aliases={8: 0},
        scratch_shapes=[pltpu.VMEM((1, S5_LB), F32), pltpu.VMEM((1, S5_LB), F32), pltpu.VMEM((S5_TB, blk), F32),
                        pltpu.VMEM((S5_TB, blk), F32)],
        compiler_params=_cparams(("parallel", "arbitrary", "arbitrary")),
    )(dyc, s, s, proj, du5_a, bb_band, c_band, lam_pow_conj, dproj)


CONV_TR = 1024
CONV_CW = 512


def _shift_down(x, halo, k):
    if k == 0:
        return x
    row8 = lax.broadcasted_iota(jnp.int32, halo.shape, 0)
    rolled = pltpu.roll(x, k, 0)
    top = jnp.where(row8 < k, pltpu.roll(halo, k, 0), rolled[:8])
    if x.shape[0] == 8:
        return top
    return jnp.concatenate([top, rolled[8:]], axis=0)


def _shift_up(x, halo, k):
    if k == 0:
        return x
    n = x.shape[0]
    row8 = lax.broadcasted_iota(jnp.int32, halo.shape, 0)
    rolled = pltpu.roll(x, n - k, 0)
    bot = jnp.where(row8 >= 8 - k, pltpu.roll(halo, 8 - k, 0), rolled[n - 8:])
    if n == 8:
        return bot
    return jnp.concatenate([rolled[:n - 8], bot], axis=0)


def _conv_pre(x, halo, w, b):
    acc = b + w[SSD_CONV - 1:SSD_CONV, :] * x
    for k in range(SSD_CONV - 1):
        acc = acc + w[k:k + 1, :] * _shift_down(x, halo, SSD_CONV - 1 - k)
    return acc


def _conv_specs(seq_len, col_off):
    lt = seq_len // CONV_TR
    cb = col_off // CONV_CW
    cur = pl.BlockSpec((CONV_TR, CONV_CW), lambda j, i: (i, j + cb))
    prev = pl.BlockSpec((8, CONV_CW), lambda j, i: (jnp.maximum(i * (CONV_TR // 8) - 1, 0), j + cb))
    return lt, cur, prev


def _conv_fwd(proj, conv_w, conv_b, n_rows, seq_len):
    lt, cur, prev = _conv_specs(seq_len, OFF_XBC)

    def body(x_ref, h_ref, w_ref, b_ref, o_ref):
        halo = jnp.where(pl.program_id(1) % lt == 0, 0.0, h_ref[...])
        o_ref[...] = jax.nn.silu(_conv_pre(x_ref[...], halo, w_ref[...], b_ref[...]))

    return pl.pallas_call(
        body, name="ssd_conv_fwd", grid=(SSD_CONV_DIM // CONV_CW, n_rows // CONV_TR),
        in_specs=[cur, prev, pl.BlockSpec((SSD_CONV, CONV_CW), lambda j, i: (0, j)),
                  pl.BlockSpec((1, CONV_CW), lambda j, i: (0, j))],
        out_specs=pl.BlockSpec((CONV_TR, CONV_CW), lambda j, i: (i, j)),
        out_shape=jax.ShapeDtypeStruct((n_rows, SSD_CONV_DIM), F32),
        compiler_params=_cparams(("parallel", "parallel")),
    )(proj, proj, conv_w, conv_b)


def _conv_bwd(name, proj, d_act, conv_w, conv_b, n_rows, seq_len, col_off, dproj):
    width = d_act.shape[1]
    lt, cur, prev = _conv_specs(seq_len, OFF_XBC + col_off)
    n_blk = n_rows // CONV_TR
    cb = (OFF_XBC + col_off) // CONV_CW
    pb = col_off // CONV_CW
    nxt = pl.BlockSpec((8, CONV_CW), lambda j, i: (jnp.minimum((i + 1) * (CONV_TR // 8), n_rows // 8 - 1), j + cb))
    d_cur = pl.BlockSpec((CONV_TR, CONV_CW), lambda j, i: (i, j))
    d_nxt = pl.BlockSpec((8, CONV_CW), lambda j, i: (jnp.minimum((i + 1) * (CONV_TR // 8), n_rows // 8 - 1), j))

    def dsilu(pre):
        sg = jax.nn.sigmoid(pre)
        return sg * (1.0 + pre * (1.0 - sg))

    def body(x_ref, hp_ref, hn_ref, d_ref, dn_ref, w_ref, b_ref, _, dx_ref, dw_ref, db_ref):
        i = pl.program_id(1)
        x, w, b = x_ref[...], w_ref[...], b_ref[...]
        halo_p = jnp.where(i % lt == 0, 0.0, hp_ref[...])
        at_end = i % lt == lt - 1
        dpre = d_ref[...] * dsilu(_conv_pre(x, halo_p, w, b))
        pre_n = _conv_pre(hn_ref[...], x[CONV_TR - 8:, :], w, b)
        dpre_n = jnp.where(at_end, 0.0, dn_ref[...] * dsilu(pre_n))
        dx = w[SSD_CONV - 1:SSD_CONV, :] * dpre
        for k in range(SSD_CONV - 1):
            dx = dx + w[k:k + 1, :] * _shift_up(dpre, dpre_n, SSD_CONV - 1 - k)
        dx_ref[...] = dx.astype(dx_ref.dtype)

        @pl.when(i == 0)
        def _():
            dw_ref[...] = jnp.zeros_like(dw_ref)
            db_ref[...] = jnp.zeros_like(db_ref)

        for k in range(SSD_CONV):
            xs = _shift_down(x, halo_p, SSD_CONV - 1 - k)
            dw_ref[k:k + 1, :] += jnp.sum(dpre * xs, axis=0, keepdims=True)
        db_ref[...] += jnp.sum(dpre, axis=0, keepdims=True)

    return pl.pallas_call(
        body, name=name, grid=(width // CONV_CW, n_blk),
        in_specs=[cur, prev, nxt, d_cur, d_nxt,
                  pl.BlockSpec((SSD_CONV, CONV_CW), lambda j, i: (0, j + pb)),
                  pl.BlockSpec((1, CONV_CW), lambda j, i: (0, j + pb)), pl.BlockSpec(memory_space=pl.ANY)],
        out_specs=[pl.BlockSpec((CONV_TR, CONV_CW), lambda j, i: (i, j + cb)),
                   pl.BlockSpec((SSD_CONV, CONV_CW), lambda j, i: (0, j)),
                   pl.BlockSpec((1, CONV_CW), lambda j, i: (0, j))],
        out_shape=[jax.ShapeDtypeStruct(dproj.shape, dproj.dtype),
                   jax.ShapeDtypeStruct((SSD_CONV, width), F32), jax.ShapeDtypeStruct((1, width), F32)],
        input_output_aliases={7: 0},
        compiler_params=_cparams(("parallel", "arbitrary")),
    )(proj, proj, proj, d_act, d_act, conv_w, conv_b, dproj)


def _split3(x):
    hi = x.astype(BF16)
    r = x - hi.astype(F32)
    mid = r.astype(BF16)
    return hi, mid, (r - mid.astype(F32)).astype(BF16)


def _sel_dot(a, b, a_is_sel):
    dn = (((1,), (0,)), ((), ()))
    if a_is_sel:
        return sum(lax.dot_general(a, t, dn, preferred_element_type=F32) for t in _split3(b))
    return sum(lax.dot_general(t, b, dn, preferred_element_type=F32) for t in _split3(a))


@jax.custom_vjp
def sel_left(sel, sel_t, x):
    return _sel_dot(sel, x, True)


@jax.custom_vjp
def sel_right(x, sel, sel_t):
    return _sel_dot(x, sel, False)


sel_left.defvjp(lambda s, st, x: (_sel_dot(s, x, True), (s, st)),
                lambda r, g: (jnp.zeros_like(r[0]), jnp.zeros_like(r[1]), _sel_dot(r[1], g, True)))
sel_right.defvjp(lambda x, s, st: (_sel_dot(x, s, False), (s, st)),
                 lambda r, g: (_sel_dot(g, r[1], False), jnp.zeros_like(r[0]), jnp.zeros_like(r[1])))


def _ssd_chunk(xs, bm, cm, dtr, st, dtb, alog, dsk, k):
    dt = jax.nn.softplus(dtr + dtb)
    acum = sel_left(k["tri"], k["tri_t"], dt * (-jnp.exp(alog)))
    dt_e = sel_right(dt, k["spread"], k["spread_t"])
    ac_e = sel_right(acum, k["spread"], k["spread_t"])
    al_e = ac_e[SSD_CHUNK - 1:SSD_CHUNK, :]
    dsk_e = sel_right(jnp.broadcast_to(dsk, (8, 128)), k["spread"], k["spread_t"])[0:1, :]
    xdt = xs * dt_e
    acum_t = acum.T
    scores = dot_nt(cm, bm)
    y = dot_nn(cm, st) * jnp.exp(ac_e) + xs * dsk_e
    for j in range(SSD_HPG):
        lmat = jnp.exp(jnp.where(k["causal"], acum[:, j:j + 1] - acum_t[j:j + 1, :], -jnp.inf))
        y = y + dot_nn(scores * lmat, jnp.where(k["head"] == j, xdt, 0.0))
    new = st * jnp.exp(al_e) + dot_tn(bm, xdt * jnp.exp(al_e - ac_e))
    return y, new


def _ssd_consts():
    r = lax.broadcasted_iota(jnp.int32, (SSD_CHUNK, SSD_CHUNK), 0)
    c = lax.broadcasted_iota(jnp.int32, (SSD_CHUNK, SSD_CHUNK), 1)
    hd = jnp.int32(SSD_HEADDIM)
    sr = lax.broadcasted_iota(jnp.int32, (128, GROUP_W), 0)
    sc = lax.div(lax.broadcasted_iota(jnp.int32, (128, GROUP_W), 1), hd)
    tr = lax.div(lax.broadcasted_iota(jnp.int32, (GROUP_W, 128), 0), hd)
    tc = lax.broadcasted_iota(jnp.int32, (GROUP_W, 128), 1)
    return {"tri": (r >= c).astype(BF16), "tri_t": (c >= r).astype(BF16), "causal": r >= c,
            "spread": (sr == sc).astype(BF16), "spread_t": (tr == tc).astype(BF16),
            "head": lax.div(lax.broadcasted_iota(jnp.int32, (SSD_CHUNK, GROUP_W), 1), hd)}


def _ssd_specs(n_seq, n_c, reverse):
    def cidx(c):
        return n_c - 1 - c if reverse else c

    def rows(w, first=0):
        return pl.BlockSpec((n_seq, SSD_CHUNK, w), lambda g, c: (0, cidx(c), first + g))

    par = pl.BlockSpec((1, 128), lambda g, c: (0, g))
    st = pl.BlockSpec((n_seq, 1, 1, SSD_STATE, GROUP_W), lambda g, c: (0, cidx(c), g, 0, 0))
    return rows, par, st


def _ssd_chunk_gated(xs, bm, cm, dtr, st, dtb, alog, dsk, z, nw, k):
    y, new = _ssd_chunk(xs, bm, cm, dtr, st, dtb, alog, dsk, k)
    yg = y * jax.nn.silu(z)
    return yg * lax.rsqrt(jnp.mean(yg * yg, axis=-1, keepdims=True) + RMS_EPS) * nw, new


def _ssd_fwd(xbc_act, proj, dtb, alog, dsk, norm_w, n_seq, seq_len):
    n_c = seq_len // SSD_CHUNK
    rows, par_s, st_s = _ssd_specs(n_seq, n_c, False)
    xbc3 = xbc_act.reshape(n_seq, seq_len, SSD_CONV_DIM)
    proj3 = proj.reshape(n_seq, seq_len, PROJ_W)

    def body(xs_ref, bm_ref, cm_ref, dt_ref, dtb_ref, al_ref, dk_ref, z_ref, nw_ref, y_ref, st_ref, state):
        @pl.when(pl.program_id(1) == 0)
        def _():
            state[...] = jnp.zeros_like(state)

        consts = _ssd_consts()
        for b in range(n_seq):
            prev = state[b]
            st_ref[b, 0, 0] = prev
            y, new = _ssd_chunk_gated(xs_ref[b], bm_ref[b], cm_ref[b], dt_ref[b], prev,
                                      dtb_ref[...], al_ref[...], dk_ref[...], z_ref[b], nw_ref[...], consts)
            y_ref[b] = y.astype(y_ref.dtype)
            state[b] = new

    y3, states = pl.pallas_call(
        body, name="ssd_fwd", grid=(SSD_GROUPS, n_c),
        in_specs=[rows(GROUP_W), rows(SSD_STATE, SSD_WIDTH // SSD_STATE), rows(SSD_STATE, (SSD_WIDTH + SSD_BC) // SSD_STATE),
                  rows(128, OFF_DT // 128), par_s, par_s, par_s, rows(GROUP_W, OFF_ZS // GROUP_W),
                  pl.BlockSpec((1, GROUP_W), lambda g, c: (0, g))],
        out_specs=[rows(GROUP_W), st_s],
        out_shape=[jax.ShapeDtypeStruct((n_seq, seq_len, SSD_WIDTH), BF16),
                   jax.ShapeDtypeStruct((n_seq, n_c, SSD_GROUPS, SSD_STATE, GROUP_W), F32)],
        scratch_shapes=[pltpu.VMEM((n_seq, SSD_STATE, GROUP_W), F32)],
        compiler_params=_cparams(("parallel", "arbitrary")),
    )(xbc3, xbc3, xbc3, proj3, dtb, alog, dsk, proj3, norm_w)
    return y3.reshape(n_seq * seq_len, SSD_WIDTH), states


def _ssd_bwd(xbc_act, proj, states, dy, dtb, alog, dsk, norm_w, n_seq, seq_len, dproj):
    n_c = seq_len // SSD_CHUNK
    n_rows = n_seq * seq_len
    rows, par_s, st_s = _ssd_specs(n_seq, n_c, True)
    xbc3 = xbc_act.reshape(n_seq, seq_len, SSD_CONV_DIM)
    proj3 = proj.reshape(n_seq, seq_len, PROJ_W)

    def body(xs_ref, bm_ref, cm_ref, dt_ref, st_ref, dy_ref, dtb_ref, al_ref, dk_ref, z_ref, nw_ref, _,
             dxs_ref, dbm_ref, dcm_ref, dz_ref, ddt_ref, ddtb_ref, dal_ref, ddk_ref, dnw_ref, dstate):
        @pl.when(pl.program_id(1) == 0)
        def _():
            dstate[...] = jnp.zeros_like(dstate)
            for ref in (ddtb_ref, dal_ref, ddk_ref, dnw_ref):
                ref[...] = jnp.zeros_like(ref)

        consts = _ssd_consts()
        for b in range(n_seq):
            _, vjp = jax.vjp(
                lambda *a: _ssd_chunk_gated(*a, consts),
                xs_ref[b], bm_ref[b], cm_ref[b], dt_ref[b], st_ref[b, 0, 0], dtb_ref[...], al_ref[...], dk_ref[...],
                z_ref[b], nw_ref[...])
            dxs, dbm, dcm, ddtr, dprev, ddtb, dal, ddk, dz, dnw = vjp((dy_ref[b], dstate[b]))
            dxs_ref[b] = dxs
            dbm_ref[b] = dbm
            dcm_ref[b] = dcm
            dz_ref[b] = dz.astype(dz_ref.dtype)
            ddt_ref[b] = ddtr.astype(ddt_ref.dtype)
            ddtb_ref[...] += ddtb
            dal_ref[...] += dal
            ddk_ref[...] += ddk
            dnw_ref[...] += dnw
            dstate[b] = dprev

    acc = pl.BlockSpec((1, 128), lambda g, c: (0, g))
    res = pl.pallas_call(
        body, name="ssd_bwd", grid=(SSD_GROUPS, n_c),
        in_specs=[rows(GROUP_W), rows(SSD_STATE, SSD_WIDTH // SSD_STATE), rows(SSD_STATE, (SSD_WIDTH + SSD_BC) // SSD_STATE),
                  rows(128, OFF_DT // 128), st_s, rows(GROUP_W), par_s, par_s, par_s, rows(GROUP_W, OFF_ZS // GROUP_W),
                  pl.BlockSpec((1, GROUP_W), lambda g, c: (0, g)), pl.BlockSpec(memory_space=pl.ANY)],
        out_specs=[rows(GROUP_W), rows(SSD_STATE), rows(SSD_STATE), rows(GROUP_W, OFF_ZS // GROUP_W), rows(128),
                   acc, acc, acc, pl.BlockSpec((1, GROUP_W), lambda g, c: (0, g))],
        out_shape=[jax.ShapeDtypeStruct((n_seq, seq_len, SSD_WIDTH), F32), jax.ShapeDtypeStruct((n_seq, seq_len, SSD_BC), F32),
                   jax.ShapeDtypeStruct((n_seq, seq_len, SSD_BC), F32),
                   jax.ShapeDtypeStruct((n_seq, seq_len, PROJ_W), dproj.dtype),
                   jax.ShapeDtypeStruct((n_seq, seq_len, DT_W), BF16),
                   jax.ShapeDtypeStruct((1, 512), F32), jax.ShapeDtypeStruct((1, 512), F32),
                   jax.ShapeDtypeStruct((1, 512), F32), jax.ShapeDtypeStruct((1, SSD_WIDTH), F32)],
        input_output_aliases={11: 3},
        scratch_shapes=[pltpu.VMEM((n_seq, SSD_STATE, GROUP_W), F32)],
        compiler_params=_cparams(("parallel", "arbitrary")),
    )(xbc3, xbc3, xbc3, proj3, states, dy.reshape(n_seq, seq_len, SSD_WIDTH), dtb, alog, dsk, proj3, norm_w,
      dproj.reshape(n_seq, seq_len, PROJ_W))
    flat = [r.reshape(n_rows, r.shape[-1]) for r in res[:5]]
    return (*flat, *res[5:])


def _pad_heads(v):
    return jnp.pad(v.reshape(SSD_GROUPS, SSD_HPG), ((0, 0), (0, 128 - SSD_HPG))).reshape(1, SSD_GROUPS * 128)


def _unpad_heads(v):
    return v.reshape(SSD_GROUPS, 128)[:, :SSD_HPG].reshape(1, SSD_HEADS)


def _state_cols(v):
    re, im = v
    lead = re.shape[:-1]
    re = re.reshape(lead + (S5_NJ, 1, S5_LB))
    im = im.reshape(lead + (S5_NJ, 1, S5_LB))
    return jnp.concatenate([re, im], axis=-2).reshape(lead + (2 * S5_N,))


def _state_uncols(v):
    lead = v.shape[:-1]
    v = v.reshape(lead + (S5_NJ, 2, S5_LB))
    return v[..., 0, :].reshape(lead + (S5_N,)), v[..., 1, :].reshape(lead + (S5_N,))


GROUPS_PER_BAND = BAND // S5_GROUP


def _band(w2_re, w2_im):
    gh = S5_GROUPS * S5_GROUP
    rg = ((jnp.arange(gh) // S5_GROUP) % GROUPS_PER_BAND)[:, None, None]
    cg = jnp.arange(GROUPS_PER_BAND)[None, :, None]
    parts = [jnp.where(rg == cg, v[:, None, :], 0.0).reshape(gh, S5_LB) for v in (w2_re, w2_im)]
    return jnp.concatenate(parts, axis=1)


def _band_take(wb):
    gh = S5_GROUPS * S5_GROUP
    w4 = wb.reshape(gh, 2, GROUPS_PER_BAND, S5_STATE)
    sel = w4[jnp.arange(gh), :, (jnp.arange(gh) // S5_GROUP) % GROUPS_PER_BAND, :]
    return sel[:, 0, :], sel[:, 1, :]


W_IN_SHARD = IN_PROJ_DIM // N_CHIPS
W_IN_SEGS = ((0, 512, OFF_U5), (512, 1024, OFF_Z5), (1024, 2560, OFF_ZS), (2560, 5120, OFF_XBC), (5144, 7192, OFF_G5))
DT_ROWS = (5120, 5144)


def _w_in_pieces():
    runs = []
    segs = list(W_IN_SEGS) + [(DT_ROWS[0] + SSD_HPG * g, DT_ROWS[0] + SSD_HPG * (g + 1), OFF_DT + 128 * g)
                              for g in range(SSD_GROUPS)]
    for lo, hi, off in segs:
        for j in range(N_CHIPS):
            s, e = max(lo, j * W_IN_SHARD), min(hi, (j + 1) * W_IN_SHARD)
            if s < e:
                runs.append((j, s - j * W_IN_SHARD, off + s - lo, e - s))
    return runs


RELAYOUT_LANES = 256


def _pad_w_in_t(a4):
    runs = _w_in_pieces()

    def body(a_ref, o_ref):
        o_ref[pl.ds(OFF_DT, DT_W), :] = jnp.zeros((DT_W, RELAYOUT_LANES), o_ref.dtype)
        for j, src, dst, n in runs:
            o_ref[pl.ds(dst, n), :] = a_ref[j, pl.ds(src, n), :]

    return pl.pallas_call(
        body, name="w_in_to_padded", grid=(D_MODEL // RELAYOUT_LANES,),
        in_specs=[pl.BlockSpec((N_CHIPS, W_IN_SHARD, RELAYOUT_LANES), lambda i: (0, 0, i))],
        out_specs=pl.BlockSpec((PROJ_W, RELAYOUT_LANES), lambda i: (0, i)),
        out_shape=jax.ShapeDtypeStruct((PROJ_W, D_MODEL), a4.dtype),
        compiler_params=_cparams(("parallel",)),
    )(a4)


def _unpad_w_in_t(wp):
    runs = _w_in_pieces()

    def body(p_ref, o_ref):
        for j, dst, src, n in runs:
            o_ref[j, pl.ds(dst, n), :] = p_ref[pl.ds(src, n), :]

    return pl.pallas_call(
        body, name="w_in_from_padded", grid=(D_MODEL // RELAYOUT_LANES,),
        in_specs=[pl.BlockSpec((PROJ_W, RELAYOUT_LANES), lambda i: (0, i))],
        out_specs=pl.BlockSpec((N_CHIPS, W_IN_SHARD, RELAYOUT_LANES), lambda i: (0, 0, i)),
        out_shape=jax.ShapeDtypeStruct((N_CHIPS, W_IN_SHARD, D_MODEL), wp.dtype),
        compiler_params=_cparams(("parallel",)),
    )(wp)


def _local_step(x, p, tgt, w):
    n_seq, seq_len, _ = x.shape
    n_rows = n_seq * seq_len
    tr = 512
    x2 = x.reshape(n_rows, D_MODEL)
    p2 = p.reshape(n_rows, PLE_DIM)
    t2 = tgt.reshape(n_rows, D_MODEL)
    row = functools.partial(_rowwise, n_rows=n_rows, tr=tr)

    w_pad_t = _pad_w_in_t(w["w_in_t"])
    norm_w = w["norm_w"].reshape(1, D_MODEL)
    ple_norm_w = w["ple_norm_w"].reshape(1, D_MODEL)
    final_w = w["final_norm_w"].reshape(1, D_MODEL)
    s5_d = w["s5_d"].reshape(1, S5_WIDTH)
    b_glu = w["s5_b_glu"].reshape(1, S5_WIDTH)
    conv_w = w["ssd_conv_w"].reshape(SSD_CONV, SSD_CONV_DIM)
    conv_b = w["ssd_conv_b"].reshape(1, SSD_CONV_DIM)
    ssd_norm_w = w["ssd_norm_w"].reshape(1, SSD_WIDTH)
    dtb, alog, dsk = (_pad_heads(w[k].reshape(1, SSD_HEADS)) for k in ("ssd_dt_bias", "ssd_a_log", "ssd_d"))

    gh = S5_GROUPS * S5_GROUP
    a_re = w["s5_a_re"].reshape(S5_GROUPS, S5_STATE)
    a_im = w["s5_a_im"].reshape(S5_GROUPS, S5_STATE)
    log_step = w["s5_log_step"].reshape(S5_GROUPS, 1)
    b_re2 = jnp.transpose(w["s5_b_re"].reshape(S5_GROUPS, S5_STATE, S5_GROUP), (0, 2, 1)).reshape(gh, S5_STATE)
    b_im2 = jnp.transpose(w["s5_b_im"].reshape(S5_GROUPS, S5_STATE, S5_GROUP), (0, 2, 1)).reshape(gh, S5_STATE)
    expand = (jnp.arange(gh)[:, None] // S5_GROUP == jnp.arange(S5_GROUPS)[None, :]).astype(F32)
    pow_re, pow_im, bb_re2, bb_im2 = _s5_params_fwd(a_re, a_im, log_step, b_re2, b_im2, expand)
    lam_pow = _state_cols((pow_re.reshape(S5_LOG_TB, S5_N), pow_im.reshape(S5_LOG_TB, S5_N)))
    lam_pow_conj = _state_cols((pow_re.reshape(S5_LOG_TB, S5_N), -pow_im.reshape(S5_LOG_TB, S5_N)))
    bb_band = _band(bb_re2, bb_im2).astype(BF16)
    c_band = _band(w["s5_c_re"].reshape(gh, S5_STATE), -w["s5_c_im"].reshape(gh, S5_STATE)).astype(BF16)

    late = w.get("_late")
    (hn,) = row("rms_in", lambda r, q: ([_rms(r[0], q[0])], []), row_ins=[(x2, 0, D_MODEL)],
                par_ins=[(norm_w, 0, D_MODEL)] + ([(late[0], 0, 128)] if late else []), row_outs=[(D_MODEL, BF16)])
    proj = _matmul("mm_proj", hn, w_pad_t, tb=True)
    s, yc, ge = _s5_fwd(proj, bb_band, c_band, lam_pow, s5_d, n_seq, seq_len)
    if late:
        w = {**w, **late[1](ge)}
    tg, y5 = _matmul("mm_s5_glu", ge, w["s5_w_glu"], epilogue=lambda c, r, q: [c, _s5_out(r[0], r[1], r[2], c, *q)],
                     epi_rows=[(yc, 0), (proj, OFF_U5), (proj, OFF_Z5)], epi_pars=[s5_d, b_glu], epi_outs=[F32, BF16])

    xbc_act = _conv_fwd(proj, conv_w, conv_b, n_rows, seq_len)
    yss, states = _ssd_fwd(xbc_act, proj, dtb, alog, dsk, ssd_norm_w, n_seq, seq_len)

    m5 = _matmul("mm_br_s5", y5, w["w_br_s5"])
    ms, merged = _matmul("mm_br_ssd", yss, w["w_br_ssd"], epilogue=lambda c, r, q: [c, _merge(r[0], r[1], r[2], c)],
                         epi_rows=[(proj, OFF_G5), (proj, OFF_GS), (m5, 0)], epi_outs=[F32, BF16])

    def resid_norm(c, r, q):
        h1_ = r[0] + c
        return [h1_, _rms(h1_, q[0])]

    h1, hp = _matmul("mm_out", merged, w["w_out"], epilogue=resid_norm, epi_rows=[(x2, 0)], epi_pars=[ple_norm_w],
                     epi_outs=[F32, BF16], full_rows=True)
    pp = _matmul("mm_ple_proj", p2, w["w_ple_proj"])

    def head_fn(pgl_, r, q):
        h1_, pp_, tgt_ = r
        loss, vjp = jax.vjp(lambda a, b, c, f: _head_loss(a, b, c, f, tgt_), h1_, pgl_, pp_, q[0])
        dh1_, dpgl_, dpp_, dfw_ = vjp(jnp.ones_like(loss))
        return [dh1_, dpgl_, dpp_], [loss, dfw_]

    dh2, dpgl, dpp, loss_acc, d_final_w = _matmul(
        "mm_ple_gate_head", hp, w["w_ple_gate"], epilogue=head_fn, epi_rows=[(h1, 0), (pp, 0), (t2, 0)],
        epi_pars=[final_w], epi_outs=[F32, BF16, BF16], epi_accs=[(1, 128), (1, D_MODEL)], full_rows=True)
    loss = loss_acc[0, 0]

    g = {}
    g["final_norm_w"] = d_final_w
    g["w_ple_gate"] = _matmul("mm_d_w_ple_gate", hp, dpgl, ta=True)
    g["w_ple_proj"] = _matmul("mm_d_w_ple_proj", p2, dpp, ta=True)

    def ple_norm_bwd(dhp_, r, q):
        h1_, dh2_ = r
        _, vjp = jax.vjp(_rms, h1_, q[0])
        dh, dw = vjp(dhp_)
        dh = dh + dh2_
        return [dh, dh], [dw]

    dh1, dh1_b, g["ple_norm_w"] = _matmul(
        "mm_d_hp", dpgl, w["w_ple_gate"], tb=True, epilogue=ple_norm_bwd, epi_rows=[(h1, 0), (dh2, 0)],
        epi_pars=[ple_norm_w], epi_outs=[F32, BF16], epi_accs=[(1, D_MODEL)], full_rows=True)
    g["w_out"] = _matmul("mm_d_w_out", merged, dh1_b, ta=True)

    dproj = lax.empty((n_rows, PROJ_W), BF16)

    def merge_bwd(dmerged, r, q):
        sg5, sgs = jax.nn.sigmoid(r[0]), jax.nn.sigmoid(r[1])
        d_gates = jnp.concatenate([dmerged * r[2] * sg5 * (1.0 - sg5), dmerged * r[3] * sgs * (1.0 - sgs)], axis=1)
        return [d_gates, dmerged * sg5, dmerged * sgs]

    dproj, dm5, dms = _matmul(
        "mm_d_merged", dh1_b, w["w_out"], tb=True, epilogue=merge_bwd,
        epi_rows=[(proj, OFF_G5), (proj, OFF_GS), (m5, 0), (ms, 0)], epi_outs=[BF16, BF16, BF16], full_rows=True,
        epi_into=(dproj, OFF_G5, 2 * D_MODEL))
    g["w_br_s5"] = _matmul("mm_d_w_br_s5", y5, dm5, ta=True)
    g["w_br_ssd"] = _matmul("mm_d_w_br_ssd", yss, dms, ta=True)
    dyss = _matmul("mm_d_yss", dms, w["w_br_ssd"], tb=True)

    def s5_out_bwd_a(dy_, r, q):
        yc_, u_, z_, t_ = r
        d_, bg_ = q
        ge_ = jax.nn.gelu(yc_ + d_ * u_)
        _, vjp = jax.vjp(lambda a, z, t, b: a * jax.nn.sigmoid(t + b) * jax.nn.silu(z), ge_, z_, t_, bg_)
        dge, dz, dt_, dbg = vjp(dy_)
        return [dz, dge, dt_], [dbg]

    dproj, dge_a, dtg, g["s5_b_glu"] = _matmul(
        "mm_d_y5", dm5, w["w_br_s5"], tb=True, epilogue=s5_out_bwd_a,
        epi_rows=[(yc, 0), (proj, OFF_U5), (proj, OFF_Z5), (tg, 0)], epi_pars=[s5_d, b_glu],
        epi_outs=[BF16, F32, BF16], epi_accs=[(1, S5_WIDTH)], full_rows=True, epi_into=(dproj, OFF_Z5, S5_WIDTH))
    g["s5_w_glu"] = _matmul("mm_d_w_glu", ge, dtg, ta=True)
    early = w["_early"](g) if "_early" in w else None

    def s5_out_bwd_b(dge_b, r, q):
        yc_, u_, da_ = r
        _, vjp = jax.vjp(lambda yc, u, d: jax.nn.gelu(yc + d * u), yc_, u_, q[0])
        dyc_, du_, dd_ = vjp(da_ + dge_b)
        return [dyc_, du_], [dd_]

    dyc, du5_a, g["s5_d"] = _matmul(
        "mm_d_ge", dtg, w["s5_w_glu"], tb=True, epilogue=s5_out_bwd_b,
        epi_rows=[(yc, 0), (proj, OFF_U5), (dge_a, 0)], epi_pars=[s5_d], epi_outs=[BF16, F32],
        epi_accs=[(1, S5_WIDTH)], full_rows=True, dep=early)
    dproj, d_lam, d_bb_band, d_c_band = _s5_bwd(dyc, s, proj, du5_a, bb_band, c_band, lam_pow_conj, dproj,
                                                n_seq, seq_len)

    d_lr, d_li = _state_uncols(d_lam)
    d_bbr, d_bbi = _band_take(d_bb_band)
    d_are, d_aim, d_ls, d_br2, d_bi2 = _s5_params_bwd(
        a_re, a_im, log_step, b_re2, b_im2, expand,
        d_lr.reshape(S5_GROUPS, S5_STATE), d_li.reshape(S5_GROUPS, S5_STATE), d_bbr, d_bbi)
    g["s5_a_re"], g["s5_a_im"], g["s5_log_step"] = d_are, d_aim, d_ls
    g["s5_b_re_ghp"], g["s5_b_im_ghp"] = d_br2, d_bi2
    d_cr, d_ci = _band_take(d_c_band)
    g["s5_c_re"], g["s5_c_im"] = d_cr, -d_ci

    dxs, dbm, dcm, dproj, ddt, d_dtb, d_alog, d_dsk, g["ssd_norm_w"] = _ssd_bwd(
        xbc_act, proj, states, dyss, dtb, alog, dsk, ssd_norm_w, n_seq, seq_len, dproj)
    (dproj,) = row("ssd_ddt", lambda r, q: ([r[0]], []), row_ins=[(ddt, 0, DT_W)], par_ins=[],
                   row_outs=[(DT_W, BF16)], into=(dproj, OFF_DT))
    g["ssd_dt_bias"], g["ssd_a_log"], g["ssd_d"] = _unpad_heads(d_dtb), _unpad_heads(d_alog), _unpad_heads(d_dsk)
    conv_dw, conv_db = [], []
    for nm, d_act, off in (("x", dxs, 0), ("b", dbm, SSD_WIDTH), ("c", dcm, SSD_WIDTH + SSD_BC)):
        dproj, dw_, db_ = _conv_bwd("ssd_conv_bwd_" + nm, proj, d_act, conv_w, conv_b, n_rows, seq_len, off, dproj)
        conv_dw.append(dw_)
        conv_db.append(db_)
    g["ssd_conv_w"] = jnp.concatenate(conv_dw, axis=1)
    g["ssd_conv_b"] = jnp.concatenate(conv_db, axis=1)

    g["w_in_t"] = _unpad_w_in_t(_matmul("mm_d_w_in", dproj, hn, ta=True))
    def norm_bwd(dhn_, r, q):
        x_, dh1_ = r
        _, vjp = jax.vjp(_rms, x_, q[0])
        dx_, dw_ = vjp(dhn_)
        return [dx_ + dh1_], [dw_]

    sent = w["_w_in_ready"](g, loss) if "_w_in_ready" in w else None
    dx, g["norm_w"] = _matmul("mm_d_hn", dproj, w_pad_t, epilogue=norm_bwd, epi_rows=[(x2, 0), (dh1, 0)],
                              epi_pars=[norm_w], epi_outs=[F32], epi_accs=[(1, D_MODEL)], full_rows=True, dep=sent)
    return loss, dx.reshape(x.shape), g


HBM = pl.BlockSpec(memory_space=pltpu.HBM)


def _chip_index(x, y):
    return 2 * x + y


def _half(shape2d, axis, which):
    h = shape2d[axis] // 2
    sl = pl.ds(pl.multiple_of(which * h, 128 if axis else 8), h)
    return (slice(None), sl) if axis else (sl, slice(None))


def _gather_chips(split, axes, whole):
    ns, nw = len(split), len(whole)
    n = ns + nw

    def body(*refs):
        ins, outs = refs[:n], refs[n:2 * n]
        ici_send, ici_recv, d2d_send, d2d_recv, local_sems = refs[2 * n:]
        x, y, c = lax.axis_index("x"), lax.axis_index("y"), lax.axis_index("c")
        me = _chip_index(x, y)
        sibling = (x, y, 1 - c)
        peers = [(1 - x, y), (x, 1 - y), (1 - x, 1 - y)]

        def half(t, which):
            return _half(split[t].shape, axes[t], which)

        copies = []
        for t in range(n):
            loc = pltpu.make_async_copy(ins[t], outs[t].at[me], local_sems.at[t])
            loc.start()
            copies.append(loc)

        def ici(t, k, slot):
            px, py = peers[k]
            if t < ns:
                src, dst = ins[t].at[half(t, c)], outs[t].at[(slot,) + half(t, c)]
            else:
                src, dst = ins[t], outs[t].at[slot]
            return pltpu.make_async_remote_copy(src_ref=src, dst_ref=dst, send_sem=ici_send.at[t, k],
                                                recv_sem=ici_recv.at[t, k], device_id=(px, py, c), device_id_type=MESH)

        def d2d(t, k, which):
            rows = outs[t].at[(_chip_index(*peers[k]),) + half(t, which)]
            return pltpu.make_async_remote_copy(src_ref=rows, dst_ref=rows, send_sem=d2d_send.at[t, k],
                                                recv_sem=d2d_recv.at[t, k], device_id=sibling, device_id_type=MESH)

        sends = []
        for t in range(n):
            for k in range(3):
                cp = ici(t, k, me)
                cp.start()
                sends.append(cp)
        for t in range(n):
            for k in range(3):
                ici(t, k, _chip_index(*peers[k])).wait_recv()
                if t < ns:
                    cp = d2d(t, k, c)
                    cp.start()
                    sends.append(cp)
        for t in range(ns):
            for k in range(3):
                d2d(t, k, 1 - c).wait_recv()
        for cp in sends:
            cp.wait_send()
        for cp in copies:
            cp.wait()

    arrays = list(split) + list(whole)
    return pl.pallas_call(
        body, name="gather_weights",
        in_specs=[HBM] * n, out_specs=[HBM] * n,
        out_shape=[jax.ShapeDtypeStruct((N_CHIPS,) + a.shape, a.dtype) for a in arrays],
        scratch_shapes=[pltpu.SemaphoreType.DMA((n, 3)), pltpu.SemaphoreType.DMA((n, 3)),
                        pltpu.SemaphoreType.DMA((ns, 3)), pltpu.SemaphoreType.DMA((ns, 3)),
                        pltpu.SemaphoreType.DMA((n,))],
    )(*arrays)


SEM = pl.BlockSpec(memory_space=pltpu.SEMAPHORE)
DATAFLOW = pltpu.SideEffectType.DATAFLOW_SIDE_EFFECTING


def _gather_start(shards, after):
    n = len(shards)

    def body(*refs):
        ins, lands = refs[:n], refs[n:2 * n]
        send_sems, recv_sems = refs[2 * n + 1], refs[2 * n + 2]
        token = refs[-1]
        x, y, c = lax.axis_index("x"), lax.axis_index("y"), lax.axis_index("c")
        me = _chip_index(x, y)
        for t in range(n):
            for k, (px, py) in enumerate([(1 - x, y), (x, 1 - y), (1 - x, 1 - y)]):
                pltpu.make_async_remote_copy(
                    src_ref=ins[t], dst_ref=lands[t].at[me], send_sem=send_sems.at[3 * t + k],
                    recv_sem=recv_sems.at[3 * t + k],
                    device_id=(px, py, c), device_id_type=MESH).start()
        token[...] = jnp.zeros_like(token)

    zones = [lax.empty((N_CHIPS,) + a.shape, a.dtype) for a in shards]
    res = pl.pallas_call(
        body, name="gather_rest_start",
        out_shape=(pltpu.SemaphoreType.DMA((3 * n,)), pltpu.SemaphoreType.DMA((3 * n,)),
                   *[pltpu.HBM(a.shape, a.dtype) for a in shards], *[pltpu.HBM(z.shape, z.dtype) for z in zones],
                   jax.ShapeDtypeStruct((8, 128), F32)),
        in_specs=[HBM] * (2 * n) + [pl.BlockSpec(memory_space=pl.ANY)],
        out_specs=(SEM, SEM, *[HBM] * (2 * n), pl.BlockSpec(memory_space=pltpu.VMEM)),
        input_output_aliases={t: 2 + t for t in range(2 * n)},
        compiler_params=pltpu.CompilerParams(has_side_effects=DATAFLOW),
    )(*[pltpu.with_memory_space_constraint(a, pltpu.HBM) for a in shards],
      *[pltpu.with_memory_space_constraint(z, pltpu.HBM) for z in zones], after)
    return res[0], res[1], list(res[2:2 + n]), list(res[2 + n:2 + 2 * n]), res[-1]


def _gather_wait(send_sems, recv_sems, thru, lands, after):
    n = len(thru)

    def body(*refs):
        ins, zones = refs[:n], refs[n:2 * n]
        s_sems, r_sems = refs[2 * n], refs[2 * n + 1]
        x, y, c = lax.axis_index("x"), lax.axis_index("y"), lax.axis_index("c")
        for t in range(n):
            for k, (px, py) in enumerate([(1 - x, y), (x, 1 - y), (1 - x, 1 - y)]):
                cp = pltpu.make_async_remote_copy(
                    src_ref=ins[t], dst_ref=zones[t].at[_chip_index(px, py)], send_sem=s_sems.at[3 * t + k],
                    recv_sem=r_sems.at[3 * t + k], device_id=(px, py, c), device_id_type=MESH)
                cp.wait_send()
                cp.wait_recv()

    res = pl.pallas_call(
        body, name="gather_rest_wait",
        out_shape=(*[pltpu.HBM(a.shape, a.dtype) for a in thru], *[pltpu.HBM(z.shape, z.dtype) for z in lands]),
        in_specs=[HBM] * (2 * n) + [SEM, SEM, pl.BlockSpec(memory_space=pl.ANY)], out_specs=[HBM] * (2 * n),
        input_output_aliases={t: t for t in range(2 * n)},
        compiler_params=pltpu.CompilerParams(has_side_effects=DATAFLOW),
    )(*thru, *lands, send_sems, recv_sems, after)
    return list(res[:n]), list(res[n:])


def _scatter_ends(t, k, n_slotted, ins, zones, x, y, me):
    px, py = [(1 - x, y), (x, 1 - y), (1 - x, 1 - y)][k]
    if t < n_slotted:
        return ins[t].at[_chip_index(px, py)], zones[t].at[k], (px, py)
    return ins[t], (zones[t].at[me], zones[t].at[_chip_index(px, py)]), (px, py)


def _scatter_start(name, slotted, whole, after):
    n_s = len(slotted)
    arrays = list(slotted) + list(whole)
    n = len(arrays)

    def body(*refs):
        ins, lands = refs[:n], refs[n:2 * n]
        send_sems, recv_sems = refs[2 * n + 1], refs[2 * n + 2]
        token = refs[-1]
        x, y, c = lax.axis_index("x"), lax.axis_index("y"), lax.axis_index("c")
        for t in range(n):
            for k in range(3):
                src, dst, (px, py) = _scatter_ends(t, k, n_s, ins, lands, x, y, _chip_index(x, y))
                pltpu.make_async_remote_copy(
                    src_ref=src, dst_ref=dst if t < n_s else dst[0], send_sem=send_sems.at[3 * t + k],
                    recv_sem=recv_sems.at[3 * t + k], device_id=(px, py, c), device_id_type=MESH).start()
        token[...] = jnp.zeros_like(token)

    zones = [lax.empty((3,) + a.shape[1:], a.dtype) for a in slotted]
    zones += [lax.empty((N_CHIPS,) + a.shape, a.dtype) for a in whole]
    slotted = arrays
    res = pl.pallas_call(
        body, name=name,
        out_shape=(pltpu.SemaphoreType.DMA((3 * n,)), pltpu.SemaphoreType.DMA((3 * n,)),
                   *[pltpu.HBM(a.shape, a.dtype) for a in slotted], *[pltpu.HBM(z.shape, z.dtype) for z in zones],
                   jax.ShapeDtypeStruct((8, 128), F32)),
        in_specs=[HBM] * (2 * n) + [pl.BlockSpec(memory_space=pl.ANY)],
        out_specs=(SEM, SEM, *[HBM] * (2 * n), pl.BlockSpec(memory_space=pltpu.VMEM)),
        input_output_aliases={t: 2 + t for t in range(2 * n)},
        compiler_params=pltpu.CompilerParams(has_side_effects=DATAFLOW),
    )(*[pltpu.with_memory_space_constraint(a, pltpu.HBM) for a in slotted],
      *[pltpu.with_memory_space_constraint(z, pltpu.HBM) for z in zones], after)
    return res[0], res[1], list(res[2:2 + n]), list(res[2 + n:2 + 2 * n]), res[-1]


def _scatter_wait(name, n_slotted, send_sems, recv_sems, thru, lands, after):
    n = len(thru)

    def body(*refs):
        ins, zones = refs[:n], refs[n:2 * n]
        s_sems, r_sems = refs[2 * n], refs[2 * n + 1]
        x, y, c = lax.axis_index("x"), lax.axis_index("y"), lax.axis_index("c")
        for t in range(n):
            for k in range(3):
                src, dst, (px, py) = _scatter_ends(t, k, n_slotted, ins, zones, x, y, _chip_index(x, y))
                cp = pltpu.make_async_remote_copy(
                    src_ref=src, dst_ref=dst if t < n_slotted else dst[1], send_sem=s_sems.at[3 * t + k],
                    recv_sem=r_sems.at[3 * t + k], device_id=(px, py, c), device_id_type=MESH)
                cp.wait_send()
                cp.wait_recv()

    res = pl.pallas_call(
        body, name=name,
        out_shape=(*[pltpu.HBM(a.shape, a.dtype) for a in thru], *[pltpu.HBM(z.shape, z.dtype) for z in lands]),
        in_specs=[HBM] * (2 * n) + [SEM, SEM, pl.BlockSpec(memory_space=pl.ANY)], out_specs=[HBM] * (2 * n),
        input_output_aliases={t: t for t in range(2 * n)},
        compiler_params=pltpu.CompilerParams(has_side_effects=DATAFLOW),
    )(*thru, *lands, send_sems, recv_sems, after)
    return list(res[:n]), list(res[n:])


def _sum_chips(name, slotted, recv, idx, dep):
    _, r, c = slotted.shape
    tr = _row_tile(r, 5 * c * 4)

    def body(idx_ref, own_ref, r_ref, _, o_ref):
        acc = own_ref[0]
        for k in range(3):
            acc = acc + r_ref[k]
        o_ref[...] = acc

    grid_spec = pltpu.PrefetchScalarGridSpec(
        num_scalar_prefetch=1, grid=(r // tr,),
        in_specs=[pl.BlockSpec((1, tr, c), lambda i, s: (s[1], i, 0)), pl.BlockSpec((3, tr, c), lambda i, s: (0, i, 0)),
                  pl.BlockSpec(memory_space=pl.ANY)],
        out_specs=pl.BlockSpec((tr, c), lambda i, s: (i, 0)))
    return pl.pallas_call(
        body, name=name, grid_spec=grid_spec, out_shape=jax.ShapeDtypeStruct((r, c), F32),
        compiler_params=_cparams(("parallel",)),
    )(idx, slotted, recv, dep)


def _half_shape(shape2d, axis):
    r, c = shape2d
    return (r, c // 2) if axis else (r // 2, c)


def _swap_halves(slotted, axes, small):
    n = len(slotted)

    def body(*refs):
        ins, sm_in = refs[:n], refs[n]
        outs, sm_out = refs[n + 1:2 * n + 1], refs[2 * n + 1]
        send_sems, recv_sems, local_sem = refs[2 * n + 2:]
        x, y, c = lax.axis_index("x"), lax.axis_index("y"), lax.axis_index("c")
        local = pltpu.make_async_copy(sm_in, sm_out.at[c], local_sem)
        local.start()
        sends = []
        for t in range(n):
            other = (slice(None),) + _half(slotted[t].shape[1:], axes[t], 1 - c)
            cp = pltpu.make_async_remote_copy(
                src_ref=ins[t].at[other], dst_ref=outs[t], send_sem=send_sems.at[t],
                recv_sem=recv_sems.at[t], device_id=(x, y, 1 - c), device_id_type=MESH)
            cp.start()
            sends.append(cp)
        cp = pltpu.make_async_remote_copy(
            src_ref=sm_in, dst_ref=sm_out.at[c], send_sem=send_sems.at[n], recv_sem=recv_sems.at[n],
            device_id=(x, y, 1 - c), device_id_type=MESH)
        cp.start()
        sends.append(cp)
        for cp in sends[:n]:
            cp.wait_recv()
        pltpu.make_async_remote_copy(
            src_ref=sm_in, dst_ref=sm_out.at[1 - c], send_sem=send_sems.at[n], recv_sem=recv_sems.at[n],
            device_id=(x, y, 1 - c), device_id_type=MESH).wait_recv()
        for cp in sends:
            cp.wait_send()
        local.wait()

    return pl.pallas_call(
        body, name="swap_halves",
        in_specs=[HBM] * (n + 1), out_specs=[HBM] * (n + 1),
        out_shape=[jax.ShapeDtypeStruct((a.shape[0],) + _half_shape(a.shape[1:], ax), a.dtype)
                   for a, ax in zip(slotted, axes, strict=True)]
        + [jax.ShapeDtypeStruct((2,) + small.shape, small.dtype)],
        scratch_shapes=[pltpu.SemaphoreType.DMA((n + 1,)), pltpu.SemaphoreType.DMA((n + 1,)), pltpu.SemaphoreType.DMA],
    )(*slotted, small)


def _tiling(r, c, f32_per_elem):
    if r % 8 == 0:
        tr = _row_tile(r, f32_per_elem * c * 4)
        return r // tr, (tr, c), lambda i: (i, 0)
    assert c % 128 == 0, (r, c)
    return c // 128, (r, 128), lambda i: (0, i)


def _pair_sum(name, slotted, other, idx, axis):
    _, r, c = slotted.shape
    hr, hc = _half_shape((r, c), axis)
    n, (tr, tc), at = _tiling(hr, hc, 13)
    if axis == 0:
        a = slotted.reshape(N_CHIPS, 2, hr, c)
        a_all = pl.BlockSpec((N_CHIPS, 1, tr, tc), lambda i, s: (0, s[0]) + at(i))
        a_own = pl.BlockSpec((1, 1, tr, tc), lambda i, s: (s[1], s[0]) + at(i))
    else:
        a, per_half = slotted, hc // tc
        a_all = pl.BlockSpec((N_CHIPS, tr, tc), lambda i, s: (0, at(i)[0], s[0] * per_half + at(i)[1]))
        a_own = pl.BlockSpec((1, tr, tc), lambda i, s: (s[1], at(i)[0], s[0] * per_half + at(i)[1]))

    def body(idx_ref, a_ref, b_ref, am_ref, bm_ref, p_ref, own_ref):
        mine, mine_own = (a_ref[:, 0], am_ref[0, 0]) if axis == 0 else (a_ref[...], am_ref[0])
        p_ref[...] = (mine + b_ref[...]).astype(p_ref.dtype)
        own_ref[...] = mine_own + bm_ref[0]

    grid_spec = pltpu.PrefetchScalarGridSpec(
        num_scalar_prefetch=1, grid=(n,),
        in_specs=[a_all, pl.BlockSpec((N_CHIPS, tr, tc), lambda i, s: (0,) + at(i)),
                  a_own, pl.BlockSpec((1, tr, tc), lambda i, s: (s[1],) + at(i))],
        out_specs=[pl.BlockSpec((N_CHIPS, tr, tc), lambda i, s: (0,) + at(i)),
                   pl.BlockSpec((tr, tc), lambda i, s: at(i))])
    return pl.pallas_call(
        body, name=name, grid_spec=grid_spec,
        out_shape=[jax.ShapeDtypeStruct((N_CHIPS, hr, hc), BF16), jax.ShapeDtypeStruct((hr, hc), F32)],
        compiler_params=_cparams(("parallel",)),
    )(idx, a, other, a, other)


def _sum_parts(name, own, recv):
    h, c = own.shape
    n, (tr, tc), at = _tiling(h, c, 4)

    def body(o_ref, r_ref, out_ref):
        acc = o_ref[...]
        for k in range(3):
            acc = acc + r_ref[k].astype(F32)
        out_ref[...] = acc

    return pl.pallas_call(
        body, name=name, grid=(n,),
        in_specs=[pl.BlockSpec((tr, tc), at), pl.BlockSpec((3, tr, tc), lambda i: (0,) + at(i))],
        out_specs=pl.BlockSpec((tr, tc), at),
        out_shape=jax.ShapeDtypeStruct((h, c), F32),
        compiler_params=_cparams(("parallel",)),
    )(own, recv)


def _all_to_all(name, small):
    def body(sm_in, sm_out, send_sems, recv_sems, local_sem):
        x, y, c = lax.axis_index("x"), lax.axis_index("y"), lax.axis_index("c")
        dev = 4 * x + 2 * y + c
        local = pltpu.make_async_copy(sm_in, sm_out.at[dev], local_sem)
        local.start()
        rel = [(fx, fy, fc) for fx in (0, 1) for fy in (0, 1) for fc in (0, 1)][1:]
        sends = []
        for k, (fx, fy, fc) in enumerate(rel):
            cp = pltpu.make_async_remote_copy(
                src_ref=sm_in, dst_ref=sm_out.at[dev], send_sem=send_sems.at[k], recv_sem=recv_sems.at[k],
                device_id=(x ^ fx, y ^ fy, c ^ fc), device_id_type=MESH)
            cp.start()
            sends.append(cp)
        for k, (fx, fy, fc) in enumerate(rel):
            src_dev = 4 * (x ^ fx) + 2 * (y ^ fy) + (c ^ fc)
            pltpu.make_async_remote_copy(
                src_ref=sm_in, dst_ref=sm_out.at[src_dev], send_sem=send_sems.at[k], recv_sem=recv_sems.at[k],
                device_id=(x ^ fx, y ^ fy, c ^ fc), device_id_type=MESH).wait_recv()
        for cp in sends:
            cp.wait_send()
        local.wait()

    return pl.pallas_call(
        body, name=name, in_specs=[HBM], out_specs=HBM,
        out_shape=jax.ShapeDtypeStruct((N_DEV,) + small.shape, small.dtype),
        scratch_shapes=[pltpu.SemaphoreType.DMA((7,)), pltpu.SemaphoreType.DMA((7,)), pltpu.SemaphoreType.DMA],
    )(small)


def _swap_sibling(name, parts):
    n = len(parts)

    def body(*refs):
        ins, outs = refs[:n], refs[n:2 * n]
        send_sems, recv_sems = refs[2 * n:]
        x, y, c = lax.axis_index("x"), lax.axis_index("y"), lax.axis_index("c")
        cps = []
        for t in range(n):
            cp = pltpu.make_async_remote_copy(
                src_ref=ins[t], dst_ref=outs[t], send_sem=send_sems.at[t], recv_sem=recv_sems.at[t],
                device_id=(x, y, 1 - c), device_id_type=MESH)
            cp.start()
            cps.append(cp)
        for cp in cps:
            cp.wait_recv()
        for cp in cps:
            cp.wait_send()

    return pl.pallas_call(
        body, name=name,
        in_specs=[HBM] * n, out_specs=[HBM] * n,
        out_shape=[jax.ShapeDtypeStruct(a.shape, a.dtype) for a in parts],
        scratch_shapes=[pltpu.SemaphoreType.DMA((n,)), pltpu.SemaphoreType.DMA((n,))],
    )(*parts)


def _sum_slots(name, a):
    k, r, c = a.shape
    tr = _row_tile(r, (k + 1) * c * 4)

    def body(a_ref, o_ref):
        acc = a_ref[0]
        for i in range(1, k):
            acc = acc + a_ref[i]
        o_ref[...] = acc

    return pl.pallas_call(
        body, name=name, grid=(r // tr,),
        in_specs=[pl.BlockSpec((k, tr, c), lambda i: (0, i, 0))],
        out_specs=pl.BlockSpec((tr, c), lambda i: (i, 0)),
        out_shape=jax.ShapeDtypeStruct((r, c), a.dtype),
        compiler_params=_cparams(("parallel",)),
    )(a)


def _adamw(name, w, m, v, g_parts):
    r, c = w.shape
    ng = len(g_parts)
    tr = _row_tile(r, (7 + ng) * c * 4)
    c1 = 1.0 - ADAM_B1 ** ADAM_STEP
    c2 = 1.0 - ADAM_B2 ** ADAM_STEP

    def body(*refs):
        w_ref, m_ref, v_ref = refs[:3]
        g_refs = refs[3:3 + ng]
        go_ref, d_ref, mo_ref, vo_ref = refs[3 + ng:]
        g = g_refs[0][...]
        for gr in g_refs[1:]:
            g = g + gr[...]
        m_new = ADAM_B1 * m_ref[...] + (1.0 - ADAM_B1) * g
        v_new = ADAM_B2 * v_ref[...] + (1.0 - ADAM_B2) * (g * g)
        go_ref[...] = g
        mo_ref[...] = m_new
        vo_ref[...] = v_new
        d_ref[...] = -ADAM_LR * ((m_new / c1) / (jnp.sqrt(v_new / c2) + ADAM_EPS) + ADAM_WD * w_ref[...])

    spec = pl.BlockSpec((tr, c), lambda i: (i, 0))
    return pl.pallas_call(
        body, name=name, grid=(r // tr,),
        in_specs=[spec] * (3 + ng), out_specs=[spec] * 4,
        out_shape=[jax.ShapeDtypeStruct((r, c), F32)] * 4,
        compiler_params=_cparams(("parallel",)),
    )(w, m, v, *g_parts)


def _adamw_halves(name, w, m, v, own, other, idx, axis):
    hr, hc = own.shape
    nb, (tr, tc), at = _tiling(hr, hc, 9)
    c1 = 1.0 - ADAM_B1 ** ADAM_STEP
    c2 = 1.0 - ADAM_B2 ** ADAM_STEP

    def body(idx_ref, w_ref, m_ref, v_ref, own_ref, oth_ref, go_ref, d_ref, mo_ref, vo_ref):
        g = jnp.where(pl.program_id(0) == idx_ref[0], own_ref[...], oth_ref[...])
        m_new = ADAM_B1 * m_ref[...] + (1.0 - ADAM_B1) * g
        v_new = ADAM_B2 * v_ref[...] + (1.0 - ADAM_B2) * (g * g)
        go_ref[...] = g
        mo_ref[...] = m_new
        vo_ref[...] = v_new
        d_ref[...] = -ADAM_LR * ((m_new / c1) / (jnp.sqrt(v_new / c2) + ADAM_EPS) + ADAM_WD * w_ref[...])

    per_half = (hc // tc) if axis else (hr // tr)
    if axis:
        full = pl.BlockSpec((tr, tc), lambda hh, i, s: (at(i)[0], hh * per_half + at(i)[1]))
    else:
        full = pl.BlockSpec((tr, tc), lambda hh, i, s: (hh * per_half + at(i)[0], at(i)[1]))
    part = pl.BlockSpec((tr, tc), lambda hh, i, s: at(i))
    grid_spec = pltpu.PrefetchScalarGridSpec(
        num_scalar_prefetch=1, grid=(2, nb), in_specs=[full, full, full, part, part], out_specs=[full] * 4)
    return pl.pallas_call(
        body, name=name, grid_spec=grid_spec, out_shape=[jax.ShapeDtypeStruct(w.shape, F32)] * 4,
        compiler_params=_cparams(("parallel", "parallel")),
    )(idx, w, m, v, own, other)


WEIGHTS = ['norm_w', 'w_in', 's5_a_re', 's5_a_im', 's5_b_re', 's5_b_im', 's5_c_re', 's5_c_im', 's5_d', 's5_log_step',
           's5_w_glu', 's5_b_glu', 'ssd_conv_w', 'ssd_conv_b', 'ssd_dt_bias', 'ssd_a_log', 'ssd_d', 'ssd_norm_w',
           'w_br_s5', 'w_br_ssd', 'w_out', 'ple_norm_w', 'w_ple_gate', 'w_ple_proj', 'final_norm_w']
SHARDED = {'w_in': ((IN_PROJ_DIM, 1024), 0), 's5_w_glu': ((512, 512), 0), 'ssd_conv_w': ((SSD_CONV, SSD_CONV_DIM), 1),
           'w_br_s5': ((512, 1024), 1), 'w_br_ssd': ((1536, 1024), 0), 'w_out': ((1024, 1024), 0),
           'w_ple_gate': ((1024, 1024), 0), 'w_ple_proj': ((256, 1024), 1)}
TRANSPOSED = ('w_in',)
SMALL = [n for n in WEIGHTS if n not in SHARDED]


def _shard_shape(name):
    (r, c), ax = SHARDED[name]
    return (r // N_CHIPS, c) if ax == 0 else (r, c // N_CHIPS)


def _half_axis(name):
    return 0 if (_shard_shape(name)[0] // 2) % 16 == 0 else 1


def _shard2d(name, a):
    r, c = _shard_shape(name)
    return a.reshape(c, r).T if name in TRANSPOSED else a.reshape(r, c)


def _unshard2d(name, a2, shape):
    return (a2.T if name in TRANSPOSED else a2).reshape(shape)


def _unslot(name, a4):
    (r, c), ax = SHARDED[name]
    if ax == 0:
        return a4.reshape(r, c)
    return jnp.transpose(a4, (1, 0, 2)).reshape(r, c)


def _slot(name, full):
    (r, c), ax = SHARDED[name]
    if ax == 0:
        return full.reshape(N_CHIPS, r // N_CHIPS, c)
    return jnp.transpose(full.reshape(r, N_CHIPS, c // N_CHIPS), (1, 0, 2))


GHP = ('s5_b_re', 's5_b_im')


def _view_shape(name):
    if name in ('s5_a_re', 's5_a_im'):
        return (S5_GROUPS, S5_STATE)
    if name in GHP + ('s5_c_re', 's5_c_im'):
        return (S5_GROUPS, S5_GROUP, S5_STATE)
    if name == 'ssd_conv_w':
        return (SSD_CONV, SSD_CONV_DIM // N_CHIPS)
    return (1, {'s5_log_step': S5_GROUPS, 'ssd_conv_b': SSD_CONV_DIM, 'ssd_norm_w': SSD_WIDTH, 's5_d': S5_WIDTH,
                's5_b_glu': S5_WIDTH, 'ssd_dt_bias': SSD_HEADS, 'ssd_a_log': SSD_HEADS, 'ssd_d': SSD_HEADS}.get(name, D_MODEL))


def _view(name, a):
    if name in GHP:
        return jnp.swapaxes(a.reshape(S5_GROUPS, S5_STATE, S5_GROUP), 1, 2)
    return a.reshape(_view_shape(name))


def _unview(name, a, shape):
    return (jnp.swapaxes(a, 1, 2) if name in GHP else a).reshape(shape)


def _adamw_small(ws, ms, vs, gs):
    n = len(ws)
    c1 = 1.0 - ADAM_B1 ** ADAM_STEP
    c2 = 1.0 - ADAM_B2 ** ADAM_STEP

    def body(*refs):
        w_r, m_r, v_r, g_r = (refs[k * n:(k + 1) * n] for k in range(4))
        d_o, m_o, v_o = (refs[k * n:(k + 1) * n] for k in range(4, 7))
        for i in range(n):
            g = g_r[i][...]
            m_new = ADAM_B1 * m_r[i][...] + (1.0 - ADAM_B1) * g
            v_new = ADAM_B2 * v_r[i][...] + (1.0 - ADAM_B2) * (g * g)
            m_o[i][...] = m_new
            v_o[i][...] = v_new
            d_o[i][...] = -ADAM_LR * ((m_new / c1) / (jnp.sqrt(v_new / c2) + ADAM_EPS) + ADAM_WD * w_r[i][...])

    return pl.pallas_call(
        body, name="adamw_small", out_shape=[jax.ShapeDtypeStruct(w.shape, F32) for w in ws] * 3,
        compiler_params=pltpu.CompilerParams(vmem_limit_bytes=VMEM_LIMIT),
    )(*ws, *ms, *vs, *gs)


def _pack_small(vals):
    flat = jnp.concatenate([v.reshape(-1).astype(F32) for v in vals])
    rows = -(-flat.shape[0] // (256 * 128)) * 256
    return jnp.pad(flat, (0, rows * 128 - flat.shape[0])).reshape(rows, 128)


def _unpack_small(packed, shapes):
    flat = packed.reshape(-1)
    out, off = [], 0
    for sh in shapes:
        n = math.prod(sh)
        out.append(flat[off:off + n].reshape(sh))
        off += n
    return out


def kernel(x, p, norm_w, w_in, s5_a_re, s5_a_im, s5_b_re, s5_b_im, s5_c_re, s5_c_im, s5_d, s5_log_step, s5_w_glu, s5_b_glu, ssd_conv_w, ssd_conv_b, ssd_dt_bias, ssd_a_log, ssd_d, ssd_norm_w, w_br_s5, w_br_ssd, w_out, ple_norm_w, w_ple_gate, w_ple_proj, final_norm_w, loss_target, m_norm_w, m_w_in, m_s5_a_re, m_s5_a_im, m_s5_b_re, m_s5_b_im, m_s5_c_re, m_s5_c_im, m_s5_d, m_s5_log_step, m_s5_w_glu, m_s5_b_glu, m_ssd_conv_w, m_ssd_conv_b, m_ssd_dt_bias, m_ssd_a_log, m_ssd_d, m_ssd_norm_w, m_w_br_s5, m_w_br_ssd, m_w_out, m_ple_norm_w, m_w_ple_gate, m_w_ple_proj, m_final_norm_w, v_norm_w, v_w_in, v_s5_a_re, v_s5_a_im, v_s5_b_re, v_s5_b_im, v_s5_c_re, v_s5_c_im, v_s5_d, v_s5_log_step, v_s5_w_glu, v_s5_b_glu, v_ssd_conv_w, v_ssd_conv_b, v_ssd_dt_bias, v_ssd_a_log, v_ssd_d, v_ssd_norm_w, v_w_br_s5, v_w_br_ssd, v_w_out, v_ple_norm_w, v_w_ple_gate, v_w_ple_proj, v_final_norm_w):
    args = locals()
    wl = {n: args[n] for n in WEIGHTS}
    ml = {n: args["m_" + n] for n in WEIGHTS}
    vl = {n: args["v_" + n] for n in WEIGHTS}
    big = [n for n in SHARDED if n != 'ssd_conv_w']
    chip = _chip_index(lax.axis_index("x"), lax.axis_index("y"))
    idx = jnp.stack([lax.axis_index("c"), chip]).astype(jnp.int32)

    axes = [_half_axis(n) for n in big]
    first = ['w_in']
    rest = [n for n in big if n not in first]
    bf_shards = {n: _shard2d(n, wl[n]).astype(BF16) for n in big}
    w_in_t, conv_w4 = _gather_chips([bf_shards[n] for n in first], [_half_axis(n) for n in first],
                                    [_shard2d('ssd_conv_w', wl['ssd_conv_w'])])
    full = {n: wl[n] for n in SMALL}
    full["w_in_t"] = w_in_t
    full['ssd_conv_w'] = _unslot('ssd_conv_w', conv_w4)
    send_sems, recv_sems, thru, lands, token = _gather_start([bf_shards[n] for n in rest], w_in_t)

    def fetch_rest(after):
        own, zones = _gather_wait(send_sems, recv_sems, thru, lands, after)
        return {n: _unslot(n, lax.dynamic_update_slice(z, o[None], (chip, 0, 0)))
                for n, o, z in zip(rest, own, zones, strict=True)}

    full["_late"] = (token, fetch_rest)

    in_flight = []

    def send_rest(g_now):
        in_flight.extend(_scatter_start("scatter_rest_start", [_slot(n, g_now[n]) for n in rest], [], g_now['s5_w_glu']))
        return in_flight[4]

    full["_early"] = send_rest

    packed = [n for n in SMALL if n != 'norm_w']
    first_axes = [_half_axis(n) for n in first]
    st = {}

    def send_first(g_now, loss_now):
        small_pack = _pack_small([loss_now] + [g_now[n + "_ghp" if n in GHP else n] for n in packed]
                                 + [g_now['ssd_conv_w']])
        swapped = _swap_halves([g_now["w_in_t"]], first_axes, small_pack)
        st["pair"] = [_pair_sum("pair_sum_w_in", g_now["w_in_t"], swapped[0], idx, first_axes[0])]
        small_chip = _sum_slots("sum_small_pair", swapped[-1])
        half_rows = small_chip.shape[0] // 2
        my_half = lax.dynamic_slice(small_chip, (lax.axis_index("c") * half_rows, 0), (half_rows, small_chip.shape[1]))
        st["flight"] = _scatter_start("scatter_first_start", [pb for pb, _ in st["pair"]], [my_half], small_chip)
        return st["flight"][4]

    full["_w_in_ready"] = send_first
    loss, grad_x, g = _local_step(x, p[0], loss_target, full)
    pair = st["pair"]
    s_sems, r_sems, thru, lands, tok = st["flight"]
    small_shapes = [(1, 1)] + [_view_shape(n) for n in packed] + [(SSD_CONV, SSD_CONV_DIM)]
    norm_all = _all_to_all("norm_w_all_to_all", g['norm_w'].reshape(8, 128))
    norm_g = _sum_slots("sum_norm_w", norm_all).reshape(1, D_MODEL)

    out_g, out_d, out_m, out_v = {}, {}, {}, {}
    own_slots, zones = _scatter_wait("scatter_rest_wait", len(rest), *in_flight[:4], norm_g)
    chip_sums = [_sum_chips("sum_chips_" + n, a, z, idx, tok) for n, a, z in zip(rest, own_slots, zones, strict=True)]
    sib_sums = _swap_sibling("swap_sibling_rest", chip_sums)
    for n, own, sib in zip(rest, chip_sums, sib_sums, strict=True):
        res = _adamw("adamw_" + n, _shard2d(n, wl[n]), _shard2d(n, ml[n]), _shard2d(n, vl[n]), [own, sib])
        out_g[n], out_d[n], out_m[n], out_v[n] = (_unshard2d(n, r, wl[n].shape) for r in res)

    sent, got = _scatter_wait("scatter_first_wait", len(first), s_sems, r_sems, thru, lands, out_d[rest[-1]])
    small_zone = lax.dynamic_update_slice(got[-1], sent[-1][None], (chip, 0, 0))
    small_half = _sum_slots("sum_small_chips", small_zone)
    halves = [_sum_parts("sum_chips_" + n, own, r) for n, (_, own), r in zip(first, pair, got[:-1], strict=True)]
    swapped2 = _swap_sibling("swap_sibling_first", halves + [small_half])
    other_halves = swapped2[:-1]
    c_is_0 = lax.axis_index("c") == 0
    small_sum = jnp.concatenate([jnp.where(c_is_0, small_half, swapped2[-1]),
                                 jnp.where(c_is_0, swapped2[-1], small_half)], axis=0)
    for n, own, oth, ax in zip(first, halves, other_halves, first_axes, strict=True):
        res = _adamw_halves("adamw_" + n, _shard2d(n, wl[n]), _shard2d(n, ml[n]), _shard2d(n, vl[n]), own, oth, idx, ax)
        out_g[n], out_d[n], out_m[n], out_v[n] = (_unshard2d(n, r, wl[n].shape) for r in res)
    sm = _unpack_small(small_sum, small_shapes)
    loss_total = sm[0].reshape(())
    conv_g = lax.dynamic_slice(sm[-1], (0, chip * (SSD_CONV_DIM // N_CHIPS)), (SSD_CONV, SSD_CONV_DIM // N_CHIPS))
    names = SMALL + ['ssd_conv_w']
    by_name = {**dict(zip(packed, sm[1:-1], strict=True)), 'norm_w': norm_g, 'ssd_conv_w': conv_g}
    grads = [by_name[n] for n in names]
    res = _adamw_small([_view(n, wl[n]) for n in names], [_view(n, ml[n]) for n in names],
                       [_view(n, vl[n]) for n in names], grads)
    for i, n in enumerate(names):
        out_g[n] = _unview(n, grads[i], wl[n].shape)
        out_d[n], out_m[n], out_v[n] = (_unview(n, res[k * len(names) + i], wl[n].shape) for k in range(3))

    return (loss_total, grad_x, *[out_g[n] for n in WEIGHTS], *[out_d[n] for n in WEIGHTS],
            *[out_m[n] for n in WEIGHTS], *[out_v[n] for n in WEIGHTS])
```

```python
import functools
import math

import jax
import jax.numpy as jnp
from jax import lax
from jax.experimental import pallas as pl
from jax.experimental.pallas import tpu as pltpu

F32 = jnp.float32
BF16 = jnp.bfloat16
MESH = pl.DeviceIdType.MESH

D_MODEL = 1024
PLE_DIM = 256
RMS_EPS = 1e-6
S5_WIDTH = 512
S5_GROUP = 16
S5_GROUPS = 32
S5_STATE = 64
S5_N = S5_GROUPS * S5_STATE
S5_LB = 512
S5_NJ = S5_N // S5_LB
S5_TB = 2048
S5_LOG_TB = 8
SSD_WIDTH = 1536
SSD_HEADDIM = 64
SSD_HEADS = 24
SSD_GROUPS = 4
SSD_HPG = 6
SSD_STATE = 128
SSD_CONV = 4
SSD_CHUNK = 128
SSD_BC = 512
SSD_CONV_DIM = 2560
GROUP_W = SSD_WIDTH // SSD_GROUPS
N_CHIPS = 4
N_DEV = 8

OFF_XBC, OFF_U5, OFF_Z5, OFF_DT, OFF_G5, OFF_GS, OFF_ZS = 0, 2560, 3072, 3584, 4096, 5120, 6144
DT_W = 512
PROJ_W = 7680
IN_PROJ_DIM = 7192

ADAM_LR, ADAM_B1, ADAM_B2, ADAM_EPS, ADAM_WD, ADAM_STEP = 0.001, 0.9, 0.999, 1e-08, 0.01, 10

VMEM_LIMIT = 56 * 1024 * 1024


ROW_BLOCK_BYTES = 8 * 1024 * 1024


def _row_tile(r, bytes_per_row):
    for t in (r, 4096, 2048, 1024, 512, 256, 128, 64, 32, 16, 8):
        if t <= r and r % t == 0 and t * bytes_per_row <= ROW_BLOCK_BYTES:
            return t
    return r


def _cparams(sem):
    return pltpu.CompilerParams(dimension_semantics=sem, vmem_limit_bytes=VMEM_LIMIT)


def _dg(a, b, ca, cb):
    return lax.dot_general(a.astype(BF16), b.astype(BF16), (((ca,), (cb,)), ((), ())), preferred_element_type=F32)


@jax.custom_vjp
def dot_nn(a, b):
    return _dg(a, b, 1, 0)


@jax.custom_vjp
def dot_nt(a, b):
    return _dg(a, b, 1, 1)


@jax.custom_vjp
def dot_tn(a, b):
    return _dg(a, b, 0, 0)


dot_nn.defvjp(lambda a, b: (_dg(a, b, 1, 0), (a, b)), lambda r, g: (_dg(g, r[1], 1, 1), _dg(r[0], g, 0, 0)))
dot_nt.defvjp(lambda a, b: (_dg(a, b, 1, 1), (a, b)), lambda r, g: (_dg(g, r[1], 1, 0), _dg(g, r[0], 0, 0)))
dot_tn.defvjp(lambda a, b: (_dg(a, b, 0, 0), (a, b)), lambda r, g: (_dg(r[1], g, 1, 1), _dg(r[0], g, 1, 0)))


MM_VMEM_BUDGET = 40 * 1024 * 1024


def _mm_tiles(m, n, k, sa, sb, so, tn_only=None):
    best, best_key = None, None
    for tm in (1024, 512, 256, 128, 64, 32, 16, 8):
        if m % tm:
            continue
        for tn in (2048, 1536, 1280, 1024, 768, 640, 512, 384, 256, 128):
            if n % tn or (tn_only is not None and tn not in tn_only):
                continue
            for tk in (k, 2048, 1536, 1280, 1024, 768, 512, 256, 128):
                if k % tk or tk > max(k, 128):
                    continue
                need = 2 * (tm * tk * sa + tk * tn * sb + tm * tn * so) + (tm * tn * 4 if tk < k else 0)
                if need > MM_VMEM_BUDGET:
                    continue
                key = (tm * tn * tk, tk)
                if best_key is None or key > best_key:
                    best, best_key = (tm, tn, tk), key
    assert best is not None, (m, n, k)
    return best


def _matmul(name, a, b, *, ta=False, tb=False, a_win=None, out_dtype=F32, epilogue=None, epi_rows=(), epi_pars=(),
            epi_outs=(), full_rows=False, epi_accs=(), epi_into=None, dep=None):
    a_off, a_w = a_win if a_win is not None else (0, a.shape[1])
    if ta:
        kdim, m = a.shape[0], a_w
    else:
        m, kdim = a.shape[0], a_w
    n = b.shape[0] if tb else b.shape[1]
    assert (b.shape[1] if tb else b.shape[0]) == kdim, (name, a.shape, b.shape)
    out_dtypes = list(epi_outs) if epilogue is not None else [out_dtype]
    so = sum(jnp.dtype(d).itemsize for d in out_dtypes) + sum(r.dtype.itemsize for r, _ in epi_rows)
    tn_ok = [n] if full_rows else [t for t in (1024, 512, 256, 128) if all(off % t == 0 for _, off in epi_rows)]
    tm, tn, tk = _mm_tiles(m, n, kdim, a.dtype.itemsize, b.dtype.itemsize, so, tn_ok if epilogue is not None else None)
    nk = kdim // tk
    n_er, n_ep, n_out = len(epi_rows), len(epi_pars), len(out_dtypes)
    if ta:
        assert a_off % tm == 0
        a_spec = pl.BlockSpec((tk, tm), lambda i, j, k: (k, i + a_off // tm))
    else:
        assert a_off % tk == 0
        a_spec = pl.BlockSpec((tm, tk), lambda i, j, k: (i, k + a_off // tk))
    if tb:
        b_spec = pl.BlockSpec((tn, tk), lambda i, j, k: (j, k))
    else:
        b_spec = pl.BlockSpec((tk, tn), lambda i, j, k: (k, j))
    ca, cb = (0 if ta else 1), (1 if tb else 0)

    n_acc = len(epi_accs)
    n_x = (1 if epi_into is not None else 0) + (1 if dep is not None else 0)
    assert not (n_acc or epi_into is not None) or full_rows

    def body(a_ref, b_ref, *refs):
        er, ep = refs[:n_er], refs[n_er:n_er + n_ep]
        first_out = n_er + n_ep + n_x
        o_refs = refs[first_out:first_out + n_out]
        s_refs = refs[first_out + n_out:first_out + n_out + n_acc]
        acc = refs[first_out + n_out + n_acc:]

        def finish(c):
            outs = [c] if epilogue is None else epilogue(c, [r[...] for r in er], [p[...] for p in ep])
            if n_acc:
                outs, sums = outs

                @pl.when(pl.program_id(0) == 0)
                def _():
                    for s_ref in s_refs:
                        s_ref[...] = jnp.zeros_like(s_ref)

                for s_ref, s in zip(s_refs, sums, strict=True):
                    s_ref[...] += jnp.broadcast_to(s, s_ref.shape)
            for o_ref, o in zip(o_refs, outs, strict=True):
                o_ref[...] = o.astype(o_ref.dtype)

        if nk == 1:
            finish(_dg(a_ref[...], b_ref[...], ca, cb))
            return
        (acc_ref,) = acc
        k = pl.program_id(2)

        @pl.when(k == 0)
        def _():
            acc_ref[...] = jnp.zeros_like(acc_ref)

        acc_ref[...] += _dg(a_ref[...], b_ref[...], ca, cb)

        @pl.when(k == nk - 1)
        def _():
            finish(acc_ref[...])

    in_specs = [a_spec, b_spec]
    in_specs += [pl.BlockSpec((tm, tn), functools.partial(lambda i, j, k, c: (i, j + c), c=off // tn)) for _, off in epi_rows]
    in_specs += [pl.BlockSpec((p.shape[0], tn), lambda i, j, k: (0, j)) for p in epi_pars]
    out_specs = [pl.BlockSpec((tm, tn), lambda i, j, k: (i, j)) for _ in out_dtypes]
    out_shape = [jax.ShapeDtypeStruct((m, n), d) for d in out_dtypes]
    extra, aliases = [], {}
    if epi_into is not None:
        buf, off, width = epi_into
        assert off % width == 0 and buf.dtype == out_dtypes[0]
        in_specs.append(pl.BlockSpec(memory_space=pl.ANY))
        out_specs[0] = pl.BlockSpec((tm, width), functools.partial(lambda i, j, k, c: (i, c), c=off // width))
        out_shape[0] = jax.ShapeDtypeStruct(buf.shape, buf.dtype)
        extra, aliases = [buf], {2 + n_er + n_ep: 0}
    if dep is not None:
        in_specs.append(pl.BlockSpec(memory_space=pl.ANY))
        extra = extra + [dep]
    out_specs += [pl.BlockSpec((r, w), lambda i, j, k: (0, 0)) for r, w in epi_accs]
    out_shape += [jax.ShapeDtypeStruct((r, w), F32) for r, w in epi_accs]
    res = pl.pallas_call(
        body, name=name, grid=(m // tm, n // tn, nk),
        in_specs=in_specs, out_specs=out_specs, out_shape=out_shape, input_output_aliases=aliases,
        scratch_shapes=[pltpu.VMEM((tm, tn), F32)] if nk > 1 else [],
        compiler_params=_cparams(("arbitrary",) * 3 if n_acc else ("parallel", "parallel", "arbitrary")),
    )(a, b, *[r for r, _ in epi_rows], *epi_pars, *extra)
    return res if epilogue is not None else res[0]


BAND = 128


def _rowwise(name, fn, n_rows, tr, row_ins, par_ins, row_outs, acc_outs=(), into=None):
    nr, npar, no, na = len(row_ins), len(par_ins), len(row_outs), len(acc_outs)
    in_specs = []
    for arr, off, w in row_ins:
        assert off % w == 0 and arr.shape[0] == n_rows, (name, arr.shape, off, w)
        in_specs.append(pl.BlockSpec((tr, w), functools.partial(lambda i, c: (i, c), c=off // w)))
    for arr, off, w in par_ins:
        assert off % w == 0
        in_specs.append(pl.BlockSpec((arr.shape[0], w), functools.partial(lambda i, c: (0, c), c=off // w)))
    out_specs = [pl.BlockSpec((tr, w), lambda i: (i, 0)) for w, _ in row_outs]
    out_specs += [pl.BlockSpec((r, w), lambda i: (0, 0)) for r, w in acc_outs]
    out_shape = [jax.ShapeDtypeStruct((n_rows, w), dt) for w, dt in row_outs]
    out_shape += [jax.ShapeDtypeStruct((r, w), F32) for r, w in acc_outs]
    extra, aliases = [], {}
    if into is not None:
        buf, off = into
        w0 = row_outs[0][0]
        assert off % w0 == 0 and buf.dtype == row_outs[0][1]
        in_specs.append(pl.BlockSpec(memory_space=pl.ANY))
        out_specs[0] = pl.BlockSpec((tr, w0), functools.partial(lambda i, c: (i, c), c=off // w0))
        out_shape[0] = jax.ShapeDtypeStruct(buf.shape, buf.dtype)
        extra, aliases = [buf], {nr + npar: 0}
    nx = len(extra)

    def body(*refs):
        rows = [r[...] for r in refs[:nr]]
        pars = [r[...] for r in refs[nr:nr + npar]]
        o_refs = refs[nr + npar + nx:nr + npar + nx + no]
        a_refs = refs[nr + npar + nx + no:]
        outs, accs = fn(rows, pars)
        for o_ref, o in zip(o_refs, outs, strict=True):
            o_ref[...] = o.astype(o_ref.dtype)
        if na:
            @pl.when(pl.program_id(0) == 0)
            def _():
                for a_ref in a_refs:
                    a_ref[...] = jnp.zeros_like(a_ref)

            for a_ref, a in zip(a_refs, accs, strict=True):
                a_ref[...] += jnp.broadcast_to(a, a_ref.shape)

    res = pl.pallas_call(
        body, name=name, grid=(n_rows // tr,),
        in_specs=in_specs, out_specs=out_specs, out_shape=out_shape, input_output_aliases=aliases,
        compiler_params=_cparams(("arbitrary",) if na else ("parallel",)),
    )(*[a for a, _, _ in row_ins], *[a for a, _, _ in par_ins], *extra)
    return res


def _rms(x, w):
    return x * lax.rsqrt(jnp.mean(x * x, axis=-1, keepdims=True) + RMS_EPS) * w


def _s5_out(yc, u, z, t, d, bg):
    ge = jax.nn.gelu(yc + d * u)
    return ge * jax.nn.sigmoid(t + bg) * jax.nn.silu(z)


def _merge(g5, gs, m5, ms):
    return jax.nn.sigmoid(g5) * m5 + jax.nn.sigmoid(gs) * ms


def _head_loss(h1, pgl, pp, fw, tgt):
    h2 = h1 + jax.nn.sigmoid(pgl) * pp
    err = _rms(h2, fw) - tgt
    per_row = 0.5 * jnp.mean(err * err, axis=-1, keepdims=True)
    return jnp.sum(per_row, axis=0, keepdims=True)


def _s5_disc(a_re, a_im, log_step, b_re2, b_im2, expand):
    step = jnp.exp(log_step)
    mag = jnp.exp(a_re * step)
    lb_re = mag * jnp.cos(a_im * step)
    lb_im = mag * jnp.sin(a_im * step)
    den = a_re * a_re + a_im * a_im
    n_re = lb_re - 1.0
    f_re = (n_re * a_re + lb_im * a_im) / den
    f_im = (lb_im * a_re - n_re * a_im) / den
    hi = lax.Precision.HIGHEST
    fr = jnp.dot(expand, f_re, precision=hi, preferred_element_type=F32)
    fi = jnp.dot(expand, f_im, precision=hi, preferred_element_type=F32)
    return lb_re, lb_im, fr * b_re2 - fi * b_im2, fr * b_im2 + fi * b_re2


def _s5_params_fwd(a_re, a_im, log_step, b_re2, b_im2, expand):
    gp = a_re.shape

    def body(ar, ai, ls, br, bi, ex, pr_ref, pi_ref, bbr_ref, bbi_ref):
        lr, li, bbr, bbi = _s5_disc(ar[...], ai[...], ls[...], br[...], bi[...], ex[...])
        bbr_ref[...] = bbr
        bbi_ref[...] = bbi
        qr, qi = lr, li
        for k in range(S5_LOG_TB):
            pr_ref[k] = qr
            pi_ref[k] = qi
            qr, qi = qr * lr - qi * li, qr * li + qi * lr

    return pl.pallas_call(
        body, name="s5_params_fwd",
        out_shape=(jax.ShapeDtypeStruct((S5_LOG_TB,) + gp, F32), jax.ShapeDtypeStruct((S5_LOG_TB,) + gp, F32),
                   jax.ShapeDtypeStruct(b_re2.shape, F32), jax.ShapeDtypeStruct(b_re2.shape, F32)),
    )(a_re, a_im, log_step, b_re2, b_im2, expand)


def _s5_params_bwd(a_re, a_im, log_step, b_re2, b_im2, expand, d_lr, d_li, d_bbr, d_bbi):
    def body(ar, ai, ls, br, bi, ex, glr, gli, gbr, gbi, dar, dai, dls, dbr, dbi):
        _, vjp = jax.vjp(lambda *p: _s5_disc(*p, ex[...]), ar[...], ai[...], ls[...], br[...], bi[...])
        g = vjp((glr[...], gli[...], gbr[...], gbi[...]))
        for ref, val in zip((dar, dai, dls, dbr, dbi), g, strict=True):
            ref[...] = val

    return pl.pallas_call(
        body, name="s5_params_bwd",
        out_shape=tuple(jax.ShapeDtypeStruct(v.shape, F32) for v in (a_re, a_im, log_step, b_re2, b_im2)),
    )(a_re, a_im, log_step, b_re2, b_im2, expand, d_lr, d_li, d_bbr, d_bbi)


def _scan_block(x_ref, out_ref, lp, edge_pow, cr, ci, reverse, each=None):
    n_g = x_ref.shape[0] // 8
    sub = lax.broadcasted_iota(jnp.int32, (8, S5_LB), 0)
    steps = []
    for sh in (1, 2, 4):
        keep = (sub < 8 - sh) if reverse else (sub >= sh)
        steps.append((8 - sh if reverse else sh, jnp.where(keep, lp[sh - 1:sh, :S5_LB], 0.0),
                      jnp.where(keep, lp[sh - 1:sh, S5_LB:], 0.0)))
    e_r, e_i = edge_pow[:, :S5_LB], edge_pow[:, S5_LB:]
    for r in (range(n_g - 1, -1, -1) if reverse else range(n_g)):
        rows = slice(8 * r, 8 * r + 8)
        xr, xi = x_ref[rows, :S5_LB], x_ref[rows, S5_LB:]
        for by, a_r, a_i in steps:
            pr, pi = pltpu.roll(xr, by, 0), pltpu.roll(xi, by, 0)
            xr, xi = xr + a_r * pr - a_i * pi, xi + a_r * pi + a_i * pr
        xr = xr + e_r * cr - e_i * ci
        xi = xi + e_r * ci + e_i * cr
        out_ref[rows, :S5_LB] = xr
        out_ref[rows, S5_LB:] = xi
        if each is not None:
            each(r, xr, xi)
        cr, ci = (xr[0:1, :], xi[0:1, :]) if reverse else (xr[7:8, :], xi[7:8, :])
    return cr, ci


def _s5_fwd(proj, bb_band, c_band, lam_pow, s5_d, n_seq, seq_len):
    n_t = seq_len // S5_TB
    blk = 2 * S5_LB
    n_rows = n_seq * seq_len
    u_blk0 = OFF_U5 // BAND

    def body(u_ref, bb_ref, cb_ref, lp_ref, d_ref, s_ref, y_ref, ge_ref, cr, ci, buf):
        @pl.when(pl.program_id(2) == 0)
        def _():
            cr[...] = jnp.zeros_like(cr)
            ci[...] = jnp.zeros_like(ci)

        u = u_ref[...]
        buf[...] = _dg(u, bb_ref[...], 1, 0)
        lp = lp_ref[...]
        cr[...], ci[...] = _scan_block(buf, buf, lp, lp, cr[...], ci[...], False)
        s = buf[...].astype(s_ref.dtype)
        s_ref[...] = s
        y = _dg(s, cb_ref[...], 1, 1)
        y_ref[...] = y
        ge_ref[...] = jax.nn.gelu(y + d_ref[...] * u).astype(ge_ref.dtype)

    def rows(j, b, t):
        return (b * n_t + t, j)

    return pl.pallas_call(
        body, name="s5_fwd", grid=(S5_NJ, n_seq, n_t),
        in_specs=[pl.BlockSpec((S5_TB, BAND), lambda j, b, t: (b * n_t + t, u_blk0 + j)),
                  pl.BlockSpec((BAND, blk), lambda j, b, t: (j, 0)), pl.BlockSpec((BAND, blk), lambda j, b, t: (j, 0)),
                  pl.BlockSpec((S5_LOG_TB, blk), lambda j, b, t: (0, j)), pl.BlockSpec((1, BAND), lambda j, b, t: (0, j))],
        out_specs=[pl.BlockSpec((S5_TB, blk), rows), pl.BlockSpec((S5_TB, BAND), rows), pl.BlockSpec((S5_TB, BAND), rows)],
        out_shape=[jax.ShapeDtypeStruct((n_rows, S5_NJ * blk), BF16), jax.ShapeDtypeStruct((n_rows, S5_WIDTH), F32),
                   jax.ShapeDtypeStruct((n_rows, S5_WIDTH), BF16)],
        scratch_shapes=[pltpu.VMEM((1, S5_LB), F32), pltpu.VMEM((1, S5_LB), F32), pltpu.VMEM((S5_TB, blk), F32)],
        compiler_params=_cparams(("parallel", "parallel", "arbitrary")),
    )(proj, bb_band, c_band, lam_pow, s5_d)


def _s5_bwd(dyc, s, proj, du5_a, bb_band, c_band, lam_pow_conj, dproj, n_seq, seq_len):
    n_t = seq_len // S5_TB
    blk = 2 * S5_LB
    halo_rows = 16
    halo_per_blk = S5_TB // halo_rows
    u_blk0 = OFF_U5 // BAND

    def rows(j, b, t):
        return (b * n_t + (n_t - 1 - t), j)

    def u_rows(j, b, t):
        return (b * n_t + (n_t - 1 - t), u_blk0 + j)

    def halo(j, b, t):
        return (jnp.maximum((b * n_t + (n_t - 1 - t)) * halo_per_blk - 1, 0), j)

    def body(dy_ref, sq_ref, hq_ref, u_ref, dua_ref, bb_ref, cb_ref, lp_ref, _, du_ref, dl_ref, dbb_ref, dcb_ref,
             cr, ci, buf, s_ref):
        b, t = pl.program_id(1), pl.program_id(2)
        sq = sq_ref[...]
        s_ref[...] = sq.astype(F32)
        h_last = hq_ref[...].astype(F32)[halo_rows - 1:halo_rows, :]
        dy = dy_ref[...]
        buf[...] = _dg(dy, cb_ref[...], 1, 0)

        @pl.when(t == 0)
        def _():
            cr[...] = jnp.zeros_like(cr)
            ci[...] = jnp.zeros_like(ci)

        @pl.when((b == 0) & (t == 0))
        def _():
            dl_ref[...] = jnp.zeros_like(dl_ref)
            dbb_ref[...] = jnp.zeros_like(dbb_ref)
            dcb_ref[...] = jnp.zeros_like(dcb_ref)

        dcb_ref[...] += _dg(dy, sq, 0, 0)

        first_blk = t == n_t - 1
        sub = lax.broadcasted_iota(jnp.int32, (8, S5_LB), 0)
        acc = [jnp.zeros((8, S5_LB), F32), jnp.zeros((8, S5_LB), F32)]

        def each(r, gr, gi):
            rows = slice(8 * r, 8 * r + 8)
            if r == 0:
                before_r = jnp.where(first_blk, 0.0, h_last[:, :S5_LB])
                before_i = jnp.where(first_blk, 0.0, h_last[:, S5_LB:])
            else:
                before_r, before_i = s_ref[8 * r - 1:8 * r, :S5_LB], s_ref[8 * r - 1:8 * r, S5_LB:]
            sp_r = jnp.where(sub == 0, before_r, pltpu.roll(s_ref[rows, :S5_LB], 1, 0))
            sp_i = jnp.where(sub == 0, before_i, pltpu.roll(s_ref[rows, S5_LB:], 1, 0))
            acc[0] = acc[0] + gr * sp_r + gi * sp_i
            acc[1] = acc[1] + gi * sp_r - gr * sp_i

        lp = lp_ref[...]
        edge_pow = jnp.concatenate([lp[7 - i:8 - i, :] for i in range(8)], axis=0)
        cr[...], ci[...] = _scan_block(buf, buf, lp, edge_pow, cr[...], ci[...], True, each)
        g = buf[...].astype(BF16)
        du_ref[...] = (dua_ref[...] + _dg(g, bb_ref[...], 1, 1)).astype(du_ref.dtype)
        dbb_ref[...] += _dg(u_ref[...], g, 0, 0)
        dl_ref[:, :S5_LB] += jnp.sum(acc[0], axis=0, keepdims=True)
        dl_ref[:, S5_LB:] += jnp.sum(acc[1], axis=0, keepdims=True)

    band = pl.BlockSpec((BAND, blk), lambda j, b, t: (j, 0))
    return pl.pallas_call(
        body, name="s5_bwd", grid=(S5_NJ, n_seq, n_t),
        in_specs=[pl.BlockSpec((S5_TB, BAND), rows), pl.BlockSpec((S5_TB, blk), rows),
                  pl.BlockSpec((halo_rows, blk), halo), pl.BlockSpec((S5_TB, BAND), u_rows),
                  pl.BlockSpec((S5_TB, BAND), rows), band, band,
                  pl.BlockSpec((S5_LOG_TB, blk), lambda j, b, t: (0, j)), pl.BlockSpec(memory_space=pl.ANY)],
        out_specs=[pl.BlockSpec((S5_TB, BAND), u_rows), pl.BlockSpec((1, blk), lambda j, b, t: (0, j)), band, band],
        out_shape=[jax.ShapeDtypeStruct(dproj.shape, dproj.dtype), jax.ShapeDtypeStruct((1, S5_NJ * blk), F32),
                   jax.ShapeDtypeStruct(bb_band.shape, F32), jax.ShapeDtypeStruct(c_band.shape, F32)],
        input_output_aliases={8: 0},
        scratch_shapes=[pltpu.VMEM((1, S5_LB), F32), pltpu.VMEM((1, S5_LB), F32), pltpu.VMEM((S5_TB, blk), F32),
                        pltpu.VMEM((S5_TB, blk), F32)],
        compiler_params=_cparams(("parallel", "arbitrary", "arbitrary")),
    )(dyc, s, s, proj, du5_a, bb_band, c_band, lam_pow_conj, dproj)


CONV_TR = 1024
CONV_CW = 512


def _shift_down(x, halo, k):
    if k == 0:
        return x
    row8 = lax.broadcasted_iota(jnp.int32, halo.shape, 0)
    rolled = pltpu.roll(x, k, 0)
    top = jnp.where(row8 < k, pltpu.roll(halo, k, 0), rolled[:8])
    if x.shape[0] == 8:
        return top
    return jnp.concatenate([top, rolled[8:]], axis=0)


def _shift_up(x, halo, k):
    if k == 0:
        return x
    n = x.shape[0]
    row8 = lax.broadcasted_iota(jnp.int32, halo.shape, 0)
    rolled = pltpu.roll(x, n - k, 0)
    bot = jnp.where(row8 >= 8 - k, pltpu.roll(halo, 8 - k, 0), rolled[n - 8:])
    if n == 8:
        return bot
    return jnp.concatenate([rolled[:n - 8], bot], axis=0)


def _conv_pre(x, halo, w, b):
    acc = b + w[SSD_CONV - 1:SSD_CONV, :] * x
    for k in range(SSD_CONV - 1):
        acc = acc + w[k:k + 1, :] * _shift_down(x, halo, SSD_CONV - 1 - k)
    return acc


def _conv_specs(seq_len, col_off):
    lt = seq_len // CONV_TR
    cb = col_off // CONV_CW
    cur = pl.BlockSpec((CONV_TR, CONV_CW), lambda j, i: (i, j + cb))
    prev = pl.BlockSpec((8, CONV_CW), lambda j, i: (jnp.maximum(i * (CONV_TR // 8) - 1, 0), j + cb))
    return lt, cur, prev


def _conv_fwd(proj, conv_w, conv_b, n_rows, seq_len):
    lt, cur, prev = _conv_specs(seq_len, OFF_XBC)

    def body(x_ref, h_ref, w_ref, b_ref, o_ref):
        halo = jnp.where(pl.program_id(1) % lt == 0, 0.0, h_ref[...])
        o_ref[...] = jax.nn.silu(_conv_pre(x_ref[...], halo, w_ref[...], b_ref[...]))

    return pl.pallas_call(
        body, name="ssd_conv_fwd", grid=(SSD_CONV_DIM // CONV_CW, n_rows // CONV_TR),
        in_specs=[cur, prev, pl.BlockSpec((SSD_CONV, CONV_CW), lambda j, i: (0, j)),
                  pl.BlockSpec((1, CONV_CW), lambda j, i: (0, j))],
        out_specs=pl.BlockSpec((CONV_TR, CONV_CW), lambda j, i: (i, j)),
        out_shape=jax.ShapeDtypeStruct((n_rows, SSD_CONV_DIM), F32),
        compiler_params=_cparams(("parallel", "parallel")),
    )(proj, proj, conv_w, conv_b)


def _conv_bwd(name, proj, d_act, conv_w, conv_b, n_rows, seq_len, col_off, dproj):
    width = d_act.shape[1]
    lt, cur, prev = _conv_specs(seq_len, OFF_XBC + col_off)
    n_blk = n_rows // CONV_TR
    cb = (OFF_XBC + col_off) // CONV_CW
    pb = col_off // CONV_CW
    nxt = pl.BlockSpec((8, CONV_CW), lambda j, i: (jnp.minimum((i + 1) * (CONV_TR // 8), n_rows // 8 - 1), j + cb))
    d_cur = pl.BlockSpec((CONV_TR, CONV_CW), lambda j, i: (i, j))
    d_nxt = pl.BlockSpec((8, CONV_CW), lambda j, i: (jnp.minimum((i + 1) * (CONV_TR // 8), n_rows // 8 - 1), j))

    def dsilu(pre):
        sg = jax.nn.sigmoid(pre)
        return sg * (1.0 + pre * (1.0 - sg))

    def body(x_ref, hp_ref, hn_ref, d_ref, dn_ref, w_ref, b_ref, _, dx_ref, dw_ref, db_ref):
        i = pl.program_id(1)
        x, w, b = x_ref[...], w_ref[...], b_ref[...]
        halo_p = jnp.where(i % lt == 0, 0.0, hp_ref[...])
        at_end = i % lt == lt - 1
        dpre = d_ref[...] * dsilu(_conv_pre(x, halo_p, w, b))
        pre_n = _conv_pre(hn_ref[...], x[CONV_TR - 8:, :], w, b)
        dpre_n = jnp.where(at_end, 0.0, dn_ref[...] * dsilu(pre_n))
        dx = w[SSD_CONV - 1:SSD_CONV, :] * dpre
        for k in range(SSD_CONV - 1):
            dx = dx + w[k:k + 1, :] * _shift_up(dpre, dpre_n, SSD_CONV - 1 - k)
        dx_ref[...] = dx.astype(dx_ref.dtype)

        @pl.when(i == 0)
        def _():
            dw_ref[...] = jnp.zeros_like(dw_ref)
            db_ref[...] = jnp.zeros_like(db_ref)

        for k in range(SSD_CONV):
            xs = _shift_down(x, halo_p, SSD_CONV - 1 - k)
            dw_ref[k:k + 1, :] += jnp.sum(dpre * xs, axis=0, keepdims=True)
        db_ref[...] += jnp.sum(dpre, axis=0, keepdims=True)

    return pl.pallas_call(
        body, name=name, grid=(width // CONV_CW, n_blk),
        in_specs=[cur, prev, nxt, d_cur, d_nxt,
                  pl.BlockSpec((SSD_CONV, CONV_CW), lambda j, i: (0, j + pb)),
                  pl.BlockSpec((1, CONV_CW), lambda j, i: (0, j + pb)), pl.BlockSpec(memory_space=pl.ANY)],
        out_specs=[pl.BlockSpec((CONV_TR, CONV_CW), lambda j, i: (i, j + cb)),
                   pl.BlockSpec((SSD_CONV, CONV_CW), lambda j, i: (0, j)),
                   pl.BlockSpec((1, CONV_CW), lambda j, i: (0, j))],
        out_shape=[jax.ShapeDtypeStruct(dproj.shape, dproj.dtype),
                   jax.ShapeDtypeStruct((SSD_CONV, width), F32), jax.ShapeDtypeStruct((1, width), F32)],
        input_output_aliases={7: 0},
        compiler_params=_cparams(("parallel", "arbitrary")),
    )(proj, proj, proj, d_act, d_act, conv_w, conv_b, dproj)


def _split3(x):
    hi = x.astype(BF16)
    r = x - hi.astype(F32)
    mid = r.astype(BF16)
    return hi, mid, (r - mid.astype(F32)).astype(BF16)


def _sel_dot(a, b, a_is_sel):
    dn = (((1,), (0,)), ((), ()))
    if a_is_sel:
        return sum(lax.dot_general(a, t, dn, preferred_element_type=F32) for t in _split3(b))
    return sum(lax.dot_general(t, b, dn, preferred_element_type=F32) for t in _split3(a))


@jax.custom_vjp
def sel_left(sel, sel_t, x):
    return _sel_dot(sel, x, True)


@jax.custom_vjp
def sel_right(x, sel, sel_t):
    return _sel_dot(x, sel, False)


sel_left.defvjp(lambda s, st, x: (_sel_dot(s, x, True), (s, st)),
                lambda r, g: (jnp.zeros_like(r[0]), jnp.zeros_like(r[1]), _sel_dot(r[1], g, True)))
sel_right.defvjp(lambda x, s, st: (_sel_dot(x, s, False), (s, st)),
                 lambda r, g: (_sel_dot(g, r[1], False), jnp.zeros_like(r[0]), jnp.zeros_like(r[1])))


def _ssd_chunk(xs, bm, cm, dtr, st, dtb, alog, dsk, k):
    dt = jax.nn.softplus(dtr + dtb)
    acum = sel_left(k["tri"], k["tri_t"], dt * (-jnp.exp(alog)))
    dt_e = sel_right(dt, k["spread"], k["spread_t"])
    ac_e = sel_right(acum, k["spread"], k["spread_t"])
    al_e = ac_e[SSD_CHUNK - 1:SSD_CHUNK, :]
    dsk_e = sel_right(jnp.broadcast_to(dsk, (8, 128)), k["spread"], k["spread_t"])[0:1, :]
    xdt = xs * dt_e
    acum_t = acum.T
    scores = dot_nt(cm, bm)
    y = dot_nn(cm, st) * jnp.exp(ac_e) + xs * dsk_e
    for j in range(SSD_HPG):
        lmat = jnp.exp(jnp.where(k["causal"], acum[:, j:j + 1] - acum_t[j:j + 1, :], -jnp.inf))
        y = y + dot_nn(scores * lmat, jnp.where(k["head"] == j, xdt, 0.0))
    new = st * jnp.exp(al_e) + dot_tn(bm, xdt * jnp.exp(al_e - ac_e))
    return y, new


def _ssd_consts():
    r = lax.broadcasted_iota(jnp.int32, (SSD_CHUNK, SSD_CHUNK), 0)
    c = lax.broadcasted_iota(jnp.int32, (SSD_CHUNK, SSD_CHUNK), 1)
    hd = jnp.int32(SSD_HEADDIM)
    sr = lax.broadcasted_iota(jnp.int32, (128, GROUP_W), 0)
    sc = lax.div(lax.broadcasted_iota(jnp.int32, (128, GROUP_W), 1), hd)
    tr = lax.div(lax.broadcasted_iota(jnp.int32, (GROUP_W, 128), 0), hd)
    tc = lax.broadcasted_iota(jnp.int32, (GROUP_W, 128), 1)
    return {"tri": (r >= c).astype(BF16), "tri_t": (c >= r).astype(BF16), "causal": r >= c,
            "spread": (sr == sc).astype(BF16), "spread_t": (tr == tc).astype(BF16),
            "head": lax.div(lax.broadcasted_iota(jnp.int32, (SSD_CHUNK, GROUP_W), 1), hd)}


def _ssd_specs(n_seq, n_c, reverse):
    def cidx(c):
        return n_c - 1 - c if reverse else c

    def rows(w, first=0):
        return pl.BlockSpec((n_seq, SSD_CHUNK, w), lambda g, c: (0, cidx(c), first + g))

    par = pl.BlockSpec((1, 128), lambda g, c: (0, g))
    st = pl.BlockSpec((n_seq, 1, 1, SSD_STATE, GROUP_W), lambda g, c: (0, cidx(c), g, 0, 0))
    return rows, par, st


def _ssd_chunk_gated(xs, bm, cm, dtr, st, dtb, alog, dsk, z, nw, k):
    y, new = _ssd_chunk(xs, bm, cm, dtr, st, dtb, alog, dsk, k)
    yg = y * jax.nn.silu(z)
    return yg * lax.rsqrt(jnp.mean(yg * yg, axis=-1, keepdims=True) + RMS_EPS) * nw, new


def _ssd_fwd(xbc_act, proj, dtb, alog, dsk, norm_w, n_seq, seq_len):
    n_c = seq_len // SSD_CHUNK
    rows, par_s, st_s = _ssd_specs(n_seq, n_c, False)
    xbc3 = xbc_act.reshape(n_seq, seq_len, SSD_CONV_DIM)
    proj3 = proj.reshape(n_seq, seq_len, PROJ_W)

    def body(xs_ref, bm_ref, cm_ref, dt_ref, dtb_ref, al_ref, dk_ref, z_ref, nw_ref, y_ref, st_ref, state):
        @pl.when(pl.program_id(1) == 0)
        def _():
            state[...] = jnp.zeros_like(state)

        consts = _ssd_consts()
        for b in range(n_seq):
            prev = state[b]
            st_ref[b, 0, 0] = prev
            y, new = _ssd_chunk_gated(xs_ref[b], bm_ref[b], cm_ref[b], dt_ref[b], prev,
                                      dtb_ref[...], al_ref[...], dk_ref[...], z_ref[b], nw_ref[...], consts)
            y_ref[b] = y.astype(y_ref.dtype)
            state[b] = new

    y3, states = pl.pallas_call(
        body, name="ssd_fwd", grid=(SSD_GROUPS, n_c),
        in_specs=[rows(GROUP_W), rows(SSD_STATE, SSD_WIDTH // SSD_STATE), rows(SSD_STATE, (SSD_WIDTH + SSD_BC) // SSD_STATE),
                  rows(128, OFF_DT // 128), par_s, par_s, par_s, rows(GROUP_W, OFF_ZS // GROUP_W),
                  pl.BlockSpec((1, GROUP_W), lambda g, c: (0, g))],
        out_specs=[rows(GROUP_W), st_s],
        out_shape=[jax.ShapeDtypeStruct((n_seq, seq_len, SSD_WIDTH), BF16),
                   jax.ShapeDtypeStruct((n_seq, n_c, SSD_GROUPS, SSD_STATE, GROUP_W), F32)],
        scratch_shapes=[pltpu.VMEM((n_seq, SSD_STATE, GROUP_W), F32)],
        compiler_params=_cparams(("parallel", "arbitrary")),
    )(xbc3, xbc3, xbc3, proj3, dtb, alog, dsk, proj3, norm_w)
    return y3.reshape(n_seq * seq_len, SSD_WIDTH), states


def _ssd_bwd(xbc_act, proj, states, dy, dtb, alog, dsk, norm_w, n_seq, seq_len, dproj):
    n_c = seq_len // SSD_CHUNK
    n_rows = n_seq * seq_len
    rows, par_s, st_s = _ssd_specs(n_seq, n_c, True)
    xbc3 = xbc_act.reshape(n_seq, seq_len, SSD_CONV_DIM)
    proj3 = proj.reshape(n_seq, seq_len, PROJ_W)

    def body(xs_ref, bm_ref, cm_ref, dt_ref, st_ref, dy_ref, dtb_ref, al_ref, dk_ref, z_ref, nw_ref, _,
             dxs_ref, dbm_ref, dcm_ref, dz_ref, ddt_ref, ddtb_ref, dal_ref, ddk_ref, dnw_ref, dstate):
        @pl.when(pl.program_id(1) == 0)
        def _():
            dstate[...] = jnp.zeros_like(dstate)
            for ref in (ddtb_ref, dal_ref, ddk_ref, dnw_ref):
                ref[...] = jnp.zeros_like(ref)

        consts = _ssd_consts()
        for b in range(n_seq):
            _, vjp = jax.vjp(
                lambda *a: _ssd_chunk_gated(*a, consts),
                xs_ref[b], bm_ref[b], cm_ref[b], dt_ref[b], st_ref[b, 0, 0], dtb_ref[...], al_ref[...], dk_ref[...],
                z_ref[b], nw_ref[...])
            dxs, dbm, dcm, ddtr, dprev, ddtb, dal, ddk, dz, dnw = vjp((dy_ref[b], dstate[b]))
            dxs_ref[b] = dxs
            dbm_ref[b] = dbm
            dcm_ref[b] = dcm
            dz_ref[b] = dz.astype(dz_ref.dtype)
            ddt_ref[b] = ddtr.astype(ddt_ref.dtype)
            ddtb_ref[...] += ddtb
            dal_ref[...] += dal
            ddk_ref[...] += ddk
            dnw_ref[...] += dnw
            dstate[b] = dprev

    acc = pl.BlockSpec((1, 128), lambda g, c: (0, g))
    res = pl.pallas_call(
        body, name="ssd_bwd", grid=(SSD_GROUPS, n_c),
        in_specs=[rows(GROUP_W), rows(SSD_STATE, SSD_WIDTH // SSD_STATE), rows(SSD_STATE, (SSD_WIDTH + SSD_BC) // SSD_STATE),
                  rows(128, OFF_DT // 128), st_s, rows(GROUP_W), par_s, par_s, par_s, rows(GROUP_W, OFF_ZS // GROUP_W),
                  pl.BlockSpec((1, GROUP_W), lambda g, c: (0, g)), pl.BlockSpec(memory_space=pl.ANY)],
        out_specs=[rows(GROUP_W), rows(SSD_STATE), rows(SSD_STATE), rows(GROUP_W, OFF_ZS // GROUP_W), rows(128),
                   acc, acc, acc, pl.BlockSpec((1, GROUP_W), lambda g, c: (0, g))],
        out_shape=[jax.ShapeDtypeStruct((n_seq, seq_len, SSD_WIDTH), F32), jax.ShapeDtypeStruct((n_seq, seq_len, SSD_BC), F32),
                   jax.ShapeDtypeStruct((n_seq, seq_len, SSD_BC), F32),
                   jax.ShapeDtypeStruct((n_seq, seq_len, PROJ_W), dproj.dtype),
                   jax.ShapeDtypeStruct((n_seq, seq_len, DT_W), BF16),
                   jax.ShapeDtypeStruct((1, 512), F32), jax.ShapeDtypeStruct((1, 512), F32),
                   jax.ShapeDtypeStruct((1, 512), F32), jax.ShapeDtypeStruct((1, SSD_WIDTH), F32)],
        input_output_aliases={11: 3},
        scratch_shapes=[pltpu.VMEM((n_seq, SSD_STATE, GROUP_W), F32)],
        compiler_params=_cparams(("parallel", "arbitrary")),
    )(xbc3, xbc3, xbc3, proj3, states, dy.reshape(n_seq, seq_len, SSD_WIDTH), dtb, alog, dsk, proj3, norm_w,
      dproj.reshape(n_seq, seq_len, PROJ_W))
    flat = [r.reshape(n_rows, r.shape[-1]) for r in res[:5]]
    return (*flat, *res[5:])


def _pad_heads(v):
    return jnp.pad(v.reshape(SSD_GROUPS, SSD_HPG), ((0, 0), (0, 128 - SSD_HPG))).reshape(1, SSD_GROUPS * 128)


def _unpad_heads(v):
    return v.reshape(SSD_GROUPS, 128)[:, :SSD_HPG].reshape(1, SSD_HEADS)


def _state_cols(v):
    re, im = v
    lead = re.shape[:-1]
    re = re.reshape(lead + (S5_NJ, 1, S5_LB))
    im = im.reshape(lead + (S5_NJ, 1, S5_LB))
    return jnp.concatenate([re, im], axis=-2).reshape(lead + (2 * S5_N,))


def _state_uncols(v):
    lead = v.shape[:-1]
    v = v.reshape(lead + (S5_NJ, 2, S5_LB))
    return v[..., 0, :].reshape(lead + (S5_N,)), v[..., 1, :].reshape(lead + (S5_N,))


GROUPS_PER_BAND = BAND // S5_GROUP


def _band(w2_re, w2_im):
    gh = S5_GROUPS * S5_GROUP
    rg = ((jnp.arange(gh) // S5_GROUP) % GROUPS_PER_BAND)[:, None, None]
    cg = jnp.arange(GROUPS_PER_BAND)[None, :, None]
    parts = [jnp.where(rg == cg, v[:, None, :], 0.0).reshape(gh, S5_LB) for v in (w2_re, w2_im)]
    return jnp.concatenate(parts, axis=1)


def _band_take(wb):
    gh = S5_GROUPS * S5_GROUP
    w4 = wb.reshape(gh, 2, GROUPS_PER_BAND, S5_STATE)
    sel = w4[jnp.arange(gh), :, (jnp.arange(gh) // S5_GROUP) % GROUPS_PER_BAND, :]
    return sel[:, 0, :], sel[:, 1, :]


W_IN_SHARD = IN_PROJ_DIM // N_CHIPS
W_IN_SEGS = ((0, 512, OFF_U5), (512, 1024, OFF_Z5), (1024, 2560, OFF_ZS), (2560, 5120, OFF_XBC), (5144, 7192, OFF_G5))
DT_ROWS = (5120, 5144)


def _w_in_pieces():
    runs = []
    segs = list(W_IN_SEGS) + [(DT_ROWS[0] + SSD_HPG * g, DT_ROWS[0] + SSD_HPG * (g + 1), OFF_DT + 128 * g)
                              for g in range(SSD_GROUPS)]
    for lo, hi, off in segs:
        for j in range(N_CHIPS):
            s, e = max(lo, j * W_IN_SHARD), min(hi, (j + 1) * W_IN_SHARD)
            if s < e:
                runs.append((j, s - j * W_IN_SHARD, off + s - lo, e - s))
    return runs


RELAYOUT_LANES = 256


def _pad_w_in_t(a4):
    runs = _w_in_pieces()

    def body(a_ref, o_ref):
        o_ref[pl.ds(OFF_DT, DT_W), :] = jnp.zeros((DT_W, RELAYOUT_LANES), o_ref.dtype)
        for j, src, dst, n in runs:
            o_ref[pl.ds(dst, n), :] = a_ref[j, pl.ds(src, n), :]

    return pl.pallas_call(
        body, name="w_in_to_padded", grid=(D_MODEL // RELAYOUT_LANES,),
        in_specs=[pl.BlockSpec((N_CHIPS, W_IN_SHARD, RELAYOUT_LANES), lambda i: (0, 0, i))],
        out_specs=pl.BlockSpec((PROJ_W, RELAYOUT_LANES), lambda i: (0, i)),
        out_shape=jax.ShapeDtypeStruct((PROJ_W, D_MODEL), a4.dtype),
        compiler_params=_cparams(("parallel",)),
    )(a4)


def _unpad_w_in_t(wp):
    runs = _w_in_pieces()

    def body(p_ref, o_ref):
        for j, dst, src, n in runs:
            o_ref[j, pl.ds(dst, n), :] = p_ref[pl.ds(src, n), :]

    return pl.pallas_call(
        body, name="w_in_from_padded", grid=(D_MODEL // RELAYOUT_LANES,),
        in_specs=[pl.BlockSpec((PROJ_W, RELAYOUT_LANES), lambda i: (0, i))],
        out_specs=pl.BlockSpec((N_CHIPS, W_IN_SHARD, RELAYOUT_LANES), lambda i: (0, 0, i)),
        out_shape=jax.ShapeDtypeStruct((N_CHIPS, W_IN_SHARD, D_MODEL), wp.dtype),
        compiler_params=_cparams(("parallel",)),
    )(wp)


def _local_step(x, p, tgt, w):
    n_seq, seq_len, _ = x.shape
    n_rows = n_seq * seq_len
    tr = 512
    x2 = x.reshape(n_rows, D_MODEL)
    p2 = p.reshape(n_rows, PLE_DIM)
    t2 = tgt.reshape(n_rows, D_MODEL)
    row = functools.partial(_rowwise, n_rows=n_rows, tr=tr)

    w_pad_t = _pad_w_in_t(w["w_in_t"])
    norm_w = w["norm_w"].reshape(1, D_MODEL)
    ple_norm_w = w["ple_norm_w"].reshape(1, D_MODEL)
    final_w = w["final_norm_w"].reshape(1, D_MODEL)
    s5_d = w["s5_d"].reshape(1, S5_WIDTH)
    b_glu = w["s5_b_glu"].reshape(1, S5_WIDTH)
    conv_w = w["ssd_conv_w"].reshape(SSD_CONV, SSD_CONV_DIM)
    conv_b = w["ssd_conv_b"].reshape(1, SSD_CONV_DIM)
    ssd_norm_w = w["ssd_norm_w"].reshape(1, SSD_WIDTH)
    dtb, alog, dsk = (_pad_heads(w[k].reshape(1, SSD_HEADS)) for k in ("ssd_dt_bias", "ssd_a_log", "ssd_d"))

    gh = S5_GROUPS * S5_GROUP
    a_re = w["s5_a_re"].reshape(S5_GROUPS, S5_STATE)
    a_im = w["s5_a_im"].reshape(S5_GROUPS, S5_STATE)
    log_step = w["s5_log_step"].reshape(S5_GROUPS, 1)
    b_re2 = jnp.transpose(w["s5_b_re"].reshape(S5_GROUPS, S5_STATE, S5_GROUP), (0, 2, 1)).reshape(gh, S5_STATE)
    b_im2 = jnp.transpose(w["s5_b_im"].reshape(S5_GROUPS, S5_STATE, S5_GROUP), (0, 2, 1)).reshape(gh, S5_STATE)
    expand = (jnp.arange(gh)[:, None] // S5_GROUP == jnp.arange(S5_GROUPS)[None, :]).astype(F32)
    pow_re, pow_im, bb_re2, bb_im2 = _s5_params_fwd(a_re, a_im, log_step, b_re2, b_im2, expand)
    lam_pow = _state_cols((pow_re.reshape(S5_LOG_TB, S5_N), pow_im.reshape(S5_LOG_TB, S5_N)))
    lam_pow_conj = _state_cols((pow_re.reshape(S5_LOG_TB, S5_N), -pow_im.reshape(S5_LOG_TB, S5_N)))
    bb_band = _band(bb_re2, bb_im2).astype(BF16)
    c_band = _band(w["s5_c_re"].reshape(gh, S5_STATE), -w["s5_c_im"].reshape(gh, S5_STATE)).astype(BF16)

    late = w.get("_late")
    (hn,) = row("rms_in", lambda r, q: ([_rms(r[0], q[0])], []), row_ins=[(x2, 0, D_MODEL)],
                par_ins=[(norm_w, 0, D_MODEL)] + ([(late[0], 0, 128)] if late else []), row_outs=[(D_MODEL, BF16)])
    proj = _matmul("mm_proj", hn, w_pad_t, tb=True)
    s, yc, ge = _s5_fwd(proj, bb_band, c_band, lam_pow, s5_d, n_seq, seq_len)
    if late:
        w = {**w, **late[1](ge)}
    tg, y5 = _matmul("mm_s5_glu", ge, w["s5_w_glu"], epilogue=lambda c, r, q: [c, _s5_out(r[0], r[1], r[2], c, *q)],
                     epi_rows=[(yc, 0), (proj, OFF_U5), (proj, OFF_Z5)], epi_pars=[s5_d, b_glu], epi_outs=[F32, BF16])

    xbc_act = _conv_fwd(proj, conv_w, conv_b, n_rows, seq_len)
    yss, states = _ssd_fwd(xbc_act, proj, dtb, alog, dsk, ssd_norm_w, n_seq, seq_len)

    m5 = _matmul("mm_br_s5", y5, w["w_br_s5"])
    ms, merged = _matmul("mm_br_ssd", yss, w["w_br_ssd"], epilogue=lambda c, r, q: [c, _merge(r[0], r[1], r[2], c)],
                         epi_rows=[(proj, OFF_G5), (proj, OFF_GS), (m5, 0)], epi_outs=[F32, BF16])

    def resid_norm(c, r, q):
        h1_ = r[0] + c
        return [h1_, _rms(h1_, q[0])]

    h1, hp = _matmul("mm_out", merged, w["w_out"], epilogue=resid_norm, epi_rows=[(x2, 0)], epi_pars=[ple_norm_w],
                     epi_outs=[F32, BF16], full_rows=True)
    pp = _matmul("mm_ple_proj", p2, w["w_ple_proj"])

    def head_fn(pgl_, r, q):
        h1_, pp_, tgt_ = r
        loss, vjp = jax.vjp(lambda a, b, c, f: _head_loss(a, b, c, f, tgt_), h1_, pgl_, pp_, q[0])
        dh1_, dpgl_, dpp_, dfw_ = vjp(jnp.ones_like(loss))
        return [dh1_, dpgl_, dpp_], [loss, dfw_]

    dh2, dpgl, dpp, loss_acc, d_final_w = _matmul(
        "mm_ple_gate_head", hp, w["w_ple_gate"], epilogue=head_fn, epi_rows=[(h1, 0), (pp, 0), (t2, 0)],
        epi_pars=[final_w], epi_outs=[F32, BF16, BF16], epi_accs=[(1, 128), (1, D_MODEL)], full_rows=True)
    loss = loss_acc[0, 0]

    g = {}
    g["final_norm_w"] = d_final_w
    g["w_ple_gate"] = _matmul("mm_d_w_ple_gate", hp, dpgl, ta=True)
    g["w_ple_proj"] = _matmul("mm_d_w_ple_proj", p2, dpp, ta=True)

    def ple_norm_bwd(dhp_, r, q):
        h1_, dh2_ = r
        _, vjp = jax.vjp(_rms, h1_, q[0])
        dh, dw = vjp(dhp_)
        dh = dh + dh2_
        return [dh, dh], [dw]

    dh1, dh1_b, g["ple_norm_w"] = _matmul(
        "mm_d_hp", dpgl, w["w_ple_gate"], tb=True, epilogue=ple_norm_bwd, epi_rows=[(h1, 0), (dh2, 0)],
        epi_pars=[ple_norm_w], epi_outs=[F32, BF16], epi_accs=[(1, D_MODEL)], full_rows=True)
    g["w_out"] = _matmul("mm_d_w_out", merged, dh1_b, ta=True)

    dproj = lax.empty((n_rows, PROJ_W), BF16)

    def merge_bwd(dmerged, r, q):
        sg5, sgs = jax.nn.sigmoid(r[0]), jax.nn.sigmoid(r[1])
        d_gates = jnp.concatenate([dmerged * r[2] * sg5 * (1.0 - sg5), dmerged * r[3] * sgs * (1.0 - sgs)], axis=1)
        return [d_gates, dmerged * sg5, dmerged * sgs]

    dproj, dm5, dms = _matmul(
        "mm_d_merged", dh1_b, w["w_out"], tb=True, epilogue=merge_bwd,
        epi_rows=[(proj, OFF_G5), (proj, OFF_GS), (m5, 0), (ms, 0)], epi_outs=[BF16, BF16, BF16], full_rows=True,
        epi_into=(dproj, OFF_G5, 2 * D_MODEL))
    g["w_br_s5"] = _matmul("mm_d_w_br_s5", y5, dm5, ta=True)
    g["w_br_ssd"] = _matmul("mm_d_w_br_ssd", yss, dms, ta=True)
    dyss = _matmul("mm_d_yss", dms, w["w_br_ssd"], tb=True)

    def s5_out_bwd_a(dy_, r, q):
        yc_, u_, z_, t_ = r
        d_, bg_ = q
        ge_ = jax.nn.gelu(yc_ + d_ * u_)
        _, vjp = jax.vjp(lambda a, z, t, b: a * jax.nn.sigmoid(t + b) * jax.nn.silu(z), ge_, z_, t_, bg_)
        dge, dz, dt_, dbg = vjp(dy_)
        return [dz, dge, dt_], [dbg]

    dproj, dge_a, dtg, g["s5_b_glu"] = _matmul(
        "mm_d_y5", dm5, w["w_br_s5"], tb=True, epilogue=s5_out_bwd_a,
        epi_rows=[(yc, 0), (proj, OFF_U5), (proj, OFF_Z5), (tg, 0)], epi_pars=[s5_d, b_glu],
        epi_outs=[BF16, F32, BF16], epi_accs=[(1, S5_WIDTH)], full_rows=True, epi_into=(dproj, OFF_Z5, S5_WIDTH))
    g["s5_w_glu"] = _matmul("mm_d_w_glu", ge, dtg, ta=True)
    early = w["_early"](g) if "_early" in w else None

    def s5_out_bwd_b(dge_b, r, q):
        yc_, u_, da_ = r
        _, vjp = jax.vjp(lambda yc, u, d: jax.nn.gelu(yc + d * u), yc_, u_, q[0])
        dyc_, du_, dd_ = vjp(da_ + dge_b)
        return [dyc_, du_], [dd_]

    dyc, du5_a, g["s5_d"] = _matmul(
        "mm_d_ge", dtg, w["s5_w_glu"], tb=True, epilogue=s5_out_bwd_b,
        epi_rows=[(yc, 0), (proj, OFF_U5), (dge_a, 0)], epi_pars=[s5_d], epi_outs=[BF16, F32],
        epi_accs=[(1, S5_WIDTH)], full_rows=True, dep=early)
    dproj, d_lam, d_bb_band, d_c_band = _s5_bwd(dyc, s, proj, du5_a, bb_band, c_band, lam_pow_conj, dproj,
                                                n_seq, seq_len)

    d_lr, d_li = _state_uncols(d_lam)
    d_bbr, d_bbi = _band_take(d_bb_band)
    d_are, d_aim, d_ls, d_br2, d_bi2 = _s5_params_bwd(
        a_re, a_im, log_step, b_re2, b_im2, expand,
        d_lr.reshape(S5_GROUPS, S5_STATE), d_li.reshape(S5_GROUPS, S5_STATE), d_bbr, d_bbi)
    g["s5_a_re"], g["s5_a_im"], g["s5_log_step"] = d_are, d_aim, d_ls
    g["s5_b_re_ghp"], g["s5_b_im_ghp"] = d_br2, d_bi2
    d_cr, d_ci = _band_take(d_c_band)
    g["s5_c_re"], g["s5_c_im"] = d_cr, -d_ci

    dxs, dbm, dcm, dproj, ddt, d_dtb, d_alog, d_dsk, g["ssd_norm_w"] = _ssd_bwd(
        xbc_act, proj, states, dyss, dtb, alog, dsk, ssd_norm_w, n_seq, seq_len, dproj)
    (dproj,) = row("ssd_ddt", lambda r, q: ([r[0]], []), row_ins=[(ddt, 0, DT_W)], par_ins=[],
                   row_outs=[(DT_W, BF16)], into=(dproj, OFF_DT))
    g["ssd_dt_bias"], g["ssd_a_log"], g["ssd_d"] = _unpad_heads(d_dtb), _unpad_heads(d_alog), _unpad_heads(d_dsk)
    conv_dw, conv_db = [], []
    for nm, d_act, off in (("x", dxs, 0), ("b", dbm, SSD_WIDTH), ("c", dcm, SSD_WIDTH + SSD_BC)):
        dproj, dw_, db_ = _conv_bwd("ssd_conv_bwd_" + nm, proj, d_act, conv_w, conv_b, n_rows, seq_len, off, dproj)
        conv_dw.append(dw_)
        conv_db.append(db_)
    g["ssd_conv_w"] = jnp.concatenate(conv_dw, axis=1)
    g["ssd_conv_b"] = jnp.concatenate(conv_db, axis=1)

    g["w_in_t"] = _unpad_w_in_t(_matmul("mm_d_w_in", dproj, hn, ta=True))
    def norm_bwd(dhn_, r, q):
        x_, dh1_ = r
        _, vjp = jax.vjp(_rms, x_, q[0])
        dx_, dw_ = vjp(dhn_)
        return [dx_ + dh1_], [dw_]

    sent = w["_w_in_ready"](g, loss) if "_w_in_ready" in w else None
    dx, g["norm_w"] = _matmul("mm_d_hn", dproj, w_pad_t, epilogue=norm_bwd, epi_rows=[(x2, 0), (dh1, 0)],
                              epi_pars=[norm_w], epi_outs=[F32], epi_accs=[(1, D_MODEL)], full_rows=True, dep=sent)
    return loss, dx.reshape(x.shape), g


HBM = pl.BlockSpec(memory_space=pltpu.HBM)


def _chip_index(x, y):
    return 2 * x + y


def _half(shape2d, axis, which):
    h = shape2d[axis] // 2
    sl = pl.ds(pl.multiple_of(which * h, 128 if axis else 8), h)
    return (slice(None), sl) if axis else (sl, slice(None))


def _gather_chips(split, axes, whole):
    ns, nw = len(split), len(whole)
    n = ns + nw

    def body(*refs):
        ins, outs = refs[:n], refs[n:2 * n]
        ici_send, ici_recv, d2d_send, d2d_recv, local_sems = refs[2 * n:]
        x, y, c = lax.axis_index("x"), lax.axis_index("y"), lax.axis_index("c")
        me = _chip_index(x, y)
        sibling = (x, y, 1 - c)
        peers = [(1 - x, y), (x, 1 - y), (1 - x, 1 - y)]

        def half(t, which):
            return _half(split[t].shape, axes[t], which)

        copies = []
        for t in range(n):
            loc = pltpu.make_async_copy(ins[t], outs[t].at[me], local_sems.at[t])
            loc.start()
            copies.append(loc)

        def ici(t, k, slot):
            px, py = peers[k]
            if t < ns:
                src, dst = ins[t].at[half(t, c)], outs[t].at[(slot,) + half(t, c)]
            else:
                src, dst = ins[t], outs[t].at[slot]
            return pltpu.make_async_remote_copy(src_ref=src, dst_ref=dst, send_sem=ici_send.at[t, k],
                                                recv_sem=ici_recv.at[t, k], device_id=(px, py, c), device_id_type=MESH)

        def d2d(t, k, which):
            rows = outs[t].at[(_chip_index(*peers[k]),) + half(t, which)]
            return pltpu.make_async_remote_copy(src_ref=rows, dst_ref=rows, send_sem=d2d_send.at[t, k],
                                                recv_sem=d2d_recv.at[t, k], device_id=sibling, device_id_type=MESH)

        sends = []
        for t in range(n):
            for k in range(3):
                cp = ici(t, k, me)
                cp.start()
                sends.append(cp)
        for t in range(n):
            for k in range(3):
                ici(t, k, _chip_index(*peers[k])).wait_recv()
                if t < ns:
                    cp = d2d(t, k, c)
                    cp.start()
                    sends.append(cp)
        for t in range(ns):
            for k in range(3):
                d2d(t, k, 1 - c).wait_recv()
        for cp in sends:
            cp.wait_send()
        for cp in copies:
            cp.wait()

    arrays = list(split) + list(whole)
    return pl.pallas_call(
        body, name="gather_weights",
        in_specs=[HBM] * n, out_specs=[HBM] * n,
        out_shape=[jax.ShapeDtypeStruct((N_CHIPS,) + a.shape, a.dtype) for a in arrays],
        scratch_shapes=[pltpu.SemaphoreType.DMA((n, 3)), pltpu.SemaphoreType.DMA((n, 3)),
                        pltpu.SemaphoreType.DMA((ns, 3)), pltpu.SemaphoreType.DMA((ns, 3)),
                        pltpu.SemaphoreType.DMA((n,))],
    )(*arrays)


SEM = pl.BlockSpec(memory_space=pltpu.SEMAPHORE)
DATAFLOW = pltpu.SideEffectType.DATAFLOW_SIDE_EFFECTING


def _gather_start(shards, after):
    n = len(shards)

    def body(*refs):
        ins, lands = refs[:n], refs[n:2 * n]
        send_sems, recv_sems = refs[2 * n + 1], refs[2 * n + 2]
        token = refs[-1]
        x, y, c = lax.axis_index("x"), lax.axis_index("y"), lax.axis_index("c")
        me = _chip_index(x, y)
        for t in range(n):
            for k, (px, py) in enumerate([(1 - x, y), (x, 1 - y), (1 - x, 1 - y)]):
                pltpu.make_async_remote_copy(
                    src_ref=ins[t], dst_ref=lands[t].at[me], send_sem=send_sems.at[3 * t + k],
                    recv_sem=recv_sems.at[3 * t + k],
                    device_id=(px, py, c), device_id_type=MESH).start()
        token[...] = jnp.zeros_like(token)

    zones = [lax.empty((N_CHIPS,) + a.shape, a.dtype) for a in shards]
    res = pl.pallas_call(
        body, name="gather_rest_start",
        out_shape=(pltpu.SemaphoreType.DMA((3 * n,)), pltpu.SemaphoreType.DMA((3 * n,)),
                   *[pltpu.HBM(a.shape, a.dtype) for a in shards], *[pltpu.HBM(z.shape, z.dtype) for z in zones],
                   jax.ShapeDtypeStruct((8, 128), F32)),
        in_specs=[HBM] * (2 * n) + [pl.BlockSpec(memory_space=pl.ANY)],
        out_specs=(SEM, SEM, *[HBM] * (2 * n), pl.BlockSpec(memory_space=pltpu.VMEM)),
        input_output_aliases={t: 2 + t for t in range(2 * n)},
        compiler_params=pltpu.CompilerParams(has_side_effects=DATAFLOW),
    )(*[pltpu.with_memory_space_constraint(a, pltpu.HBM) for a in shards],
      *[pltpu.with_memory_space_constraint(z, pltpu.HBM) for z in zones], after)
    return res[0], res[1], list(res[2:2 + n]), list(res[2 + n:2 + 2 * n]), res[-1]


def _gather_wait(send_sems, recv_sems, thru, lands, after):
    n = len(thru)

    def body(*refs):
        ins, zones = refs[:n], refs[n:2 * n]
        s_sems, r_sems = refs[2 * n], refs[2 * n + 1]
        x, y, c = lax.axis_index("x"), lax.axis_index("y"), lax.axis_index("c")
        for t in range(n):
            for k, (px, py) in enumerate([(1 - x, y), (x, 1 - y), (1 - x, 1 - y)]):
                cp = pltpu.make_async_remote_copy(
                    src_ref=ins[t], dst_ref=zones[t].at[_chip_index(px, py)], send_sem=s_sems.at[3 * t + k],
                    recv_sem=r_sems.at[3 * t + k], device_id=(px, py, c), device_id_type=MESH)
                cp.wait_send()
                cp.wait_recv()

    res = pl.pallas_call(
        body, name="gather_rest_wait",
        out_shape=(*[pltpu.HBM(a.shape, a.dtype) for a in thru], *[pltpu.HBM(z.shape, z.dtype) for z in lands]),
        in_specs=[HBM] * (2 * n) + [SEM, SEM, pl.BlockSpec(memory_space=pl.ANY)], out_specs=[HBM] * (2 * n),
        input_output_aliases={t: t for t in range(2 * n)},
        compiler_params=pltpu.CompilerParams(has_side_effects=DATAFLOW),
    )(*thru, *lands, send_sems, recv_sems, after)
    return list(res[:n]), list(res[n:])


def _scatter_ends(t, k, n_slotted, ins, zones, x, y, me):
    px, py = [(1 - x, y), (x, 1 - y), (1 - x, 1 - y)][k]
    if t < n_slotted:
        return ins[t].at[_chip_index(px, py)], zones[t].at[k], (px, py)
    return ins[t], (zones[t].at[me], zones[t].at[_chip_index(px, py)]), (px, py)


def _scatter_start(name, slotted, whole, after):
    n_s = len(slotted)
    arrays = list(slotted) + list(whole)
    n = len(arrays)

    def body(*refs):
        ins, lands = refs[:n], refs[n:2 * n]
        send_sems, recv_sems = refs[2 * n + 1], refs[2 * n + 2]
        token = refs[-1]
        x, y, c = lax.axis_index("x"), lax.axis_index("y"), lax.axis_index("c")
        for t in range(n):
            for k in range(3):
                src, dst, (px, py) = _scatter_ends(t, k, n_s, ins, lands, x, y, _chip_index(x, y))
                pltpu.make_async_remote_copy(
                    src_ref=src, dst_ref=dst if t < n_s else dst[0], send_sem=send_sems.at[3 * t + k],
                    recv_sem=recv_sems.at[3 * t + k], device_id=(px, py, c), device_id_type=MESH).start()
        token[...] = jnp.zeros_like(token)

    zones = [lax.empty((3,) + a.shape[1:], a.dtype) for a in slotted]
    zones += [lax.empty((N_CHIPS,) + a.shape, a.dtype) for a in whole]
    slotted = arrays
    res = pl.pallas_call(
        body, name=name,
        out_shape=(pltpu.SemaphoreType.DMA((3 * n,)), pltpu.SemaphoreType.DMA((3 * n,)),
                   *[pltpu.HBM(a.shape, a.dtype) for a in slotted], *[pltpu.HBM(z.shape, z.dtype) for z in zones],
                   jax.ShapeDtypeStruct((8, 128), F32)),
        in_specs=[HBM] * (2 * n) + [pl.BlockSpec(memory_space=pl.ANY)],
        out_specs=(SEM, SEM, *[HBM] * (2 * n), pl.BlockSpec(memory_space=pltpu.VMEM)),
        input_output_aliases={t: 2 + t for t in range(2 * n)},
        compiler_params=pltpu.CompilerParams(has_side_effects=DATAFLOW),
    )(*[pltpu.with_memory_space_constraint(a, pltpu.HBM) for a in slotted],
      *[pltpu.with_memory_space_constraint(z, pltpu.HBM) for z in zones], after)
    return res[0], res[1], list(res[2:2 + n]), list(res[2 + n:2 + 2 * n]), res[-1]


def _scatter_wait(name, n_slotted, send_sems, recv_sems, thru, lands, after):
    n = len(thru)

    def body(*refs):
        ins, zones = refs[:n], refs[n:2 * n]
        s_sems, r_sems = refs[2 * n], refs[2 * n + 1]
        x, y, c = lax.axis_index("x"), lax.axis_index("y"), lax.axis_index("c")
        for t in range(n):
            for k in range(3):
                src, dst, (px, py) = _scatter_ends(t, k, n_slotted, ins, zones, x, y, _chip_index(x, y))
                cp = pltpu.make_async_remote_copy(
                    src_ref=src, dst_ref=dst if t < n_slotted else dst[1], send_sem=s_sems.at[3 * t + k],
                    recv_sem=r_sems.at[3 * t + k], device_id=(px, py, c), device_id_type=MESH)
                cp.wait_send()
                cp.wait_recv()

    res = pl.pallas_call(
        body, name=name,
        out_shape=(*[pltpu.HBM(a.shape, a.dtype) for a in thru], *[pltpu.HBM(z.shape, z.dtype) for z in lands]),
        in_specs=[HBM] * (2 * n) + [SEM, SEM, pl.BlockSpec(memory_space=pl.ANY)], out_specs=[HBM] * (2 * n),
        input_output_aliases={t: t for t in range(2 * n)},
        compiler_params=pltpu.CompilerParams(has_side_effects=DATAFLOW),
    )(*thru, *lands, send_sems, recv_sems, after)
    return list(res[:n]), list(res[n:])


def _sum_chips(name, slotted, recv, idx, dep):
    _, r, c = slotted.shape
    tr = _row_tile(r, 5 * c * 4)

    def body(idx_ref, own_ref, r_ref, _, o_ref):
        acc = own_ref[0]
        for k in range(3):
            acc = acc + r_ref[k]
        o_ref[...] = acc

    grid_spec = pltpu.PrefetchScalarGridSpec(
        num_scalar_prefetch=1, grid=(r // tr,),
        in_specs=[pl.BlockSpec((1, tr, c), lambda i, s: (s[1], i, 0)), pl.BlockSpec((3, tr, c), lambda i, s: (0, i, 0)),
                  pl.BlockSpec(memory_space=pl.ANY)],
        out_specs=pl.BlockSpec((tr, c), lambda i, s: (i, 0)))
    return pl.pallas_call(
        body, name=name, grid_spec=grid_spec, out_shape=jax.ShapeDtypeStruct((r, c), F32),
        compiler_params=_cparams(("parallel",)),
    )(idx, slotted, recv, dep)


def _half_shape(shape2d, axis):
    r, c = shape2d
    return (r, c // 2) if axis else (r // 2, c)


def _swap_halves(slotted, axes, small):
    n = len(slotted)

    def body(*refs):
        ins, sm_in = refs[:n], refs[n]
        outs, sm_out = refs[n + 1:2 * n + 1], refs[2 * n + 1]
        send_sems, recv_sems, local_sem = refs[2 * n + 2:]
        x, y, c = lax.axis_index("x"), lax.axis_index("y"), lax.axis_index("c")
        local = pltpu.make_async_copy(sm_in, sm_out.at[c], local_sem)
        local.start()
        sends = []
        for t in range(n):
            other = (slice(None),) + _half(slotted[t].shape[1:], axes[t], 1 - c)
            cp = pltpu.make_async_remote_copy(
                src_ref=ins[t].at[other], dst_ref=outs[t], send_sem=send_sems.at[t],
                recv_sem=recv_sems.at[t], device_id=(x, y, 1 - c), device_id_type=MESH)
            cp.start()
            sends.append(cp)
        cp = pltpu.make_async_remote_copy(
            src_ref=sm_in, dst_ref=sm_out.at[c], send_sem=send_sems.at[n], recv_sem=recv_sems.at[n],
            device_id=(x, y, 1 - c), device_id_type=MESH)
        cp.start()
        sends.append(cp)
        for cp in sends[:n]:
            cp.wait_recv()
        pltpu.make_async_remote_copy(
            src_ref=sm_in, dst_ref=sm_out.at[1 - c], send_sem=send_sems.at[n], recv_sem=recv_sems.at[n],
            device_id=(x, y, 1 - c), device_id_type=MESH).wait_recv()
        for cp in sends:
            cp.wait_send()
        local.wait()

    return pl.pallas_call(
        body, name="swap_halves",
        in_specs=[HBM] * (n + 1), out_specs=[HBM] * (n + 1),
        out_shape=[jax.ShapeDtypeStruct((a.shape[0],) + _half_shape(a.shape[1:], ax), a.dtype)
                   for a, ax in zip(slotted, axes, strict=True)]
        + [jax.ShapeDtypeStruct((2,) + small.shape, small.dtype)],
        scratch_shapes=[pltpu.SemaphoreType.DMA((n + 1,)), pltpu.SemaphoreType.DMA((n + 1,)), pltpu.SemaphoreType.DMA],
    )(*slotted, small)


def _tiling(r, c, f32_per_elem):
    if r % 8 == 0:
        tr = _row_tile(r, f32_per_elem * c * 4)
        return r // tr, (tr, c), lambda i: (i, 0)
    assert c % 128 == 0, (r, c)
    return c // 128, (r, 128), lambda i: (0, i)


def _pair_sum(name, slotted, other, idx, axis):
    _, r, c = slotted.shape
    hr, hc = _half_shape((r, c), axis)
    n, (tr, tc), at = _tiling(hr, hc, 13)
    if axis == 0:
        a = slotted.reshape(N_CHIPS, 2, hr, c)
        a_all = pl.BlockSpec((N_CHIPS, 1, tr, tc), lambda i, s: (0, s[0]) + at(i))
        a_own = pl.BlockSpec((1, 1, tr, tc), lambda i, s: (s[1], s[0]) + at(i))
    else:
        a, per_half = slotted, hc // tc
        a_all = pl.BlockSpec((N_CHIPS, tr, tc), lambda i, s: (0, at(i)[0], s[0] * per_half + at(i)[1]))
        a_own = pl.BlockSpec((1, tr, tc), lambda i, s: (s[1], at(i)[0], s[0] * per_half + at(i)[1]))

    def body(idx_ref, a_ref, b_ref, am_ref, bm_ref, p_ref, own_ref):
        mine, mine_own = (a_ref[:, 0], am_ref[0, 0]) if axis == 0 else (a_ref[...], am_ref[0])
        p_ref[...] = (mine + b_ref[...]).astype(p_ref.dtype)
        own_ref[...] = mine_own + bm_ref[0]

    grid_spec = pltpu.PrefetchScalarGridSpec(
        num_scalar_prefetch=1, grid=(n,),
        in_specs=[a_all, pl.BlockSpec((N_CHIPS, tr, tc), lambda i, s: (0,) + at(i)),
                  a_own, pl.BlockSpec((1, tr, tc), lambda i, s: (s[1],) + at(i))],
        out_specs=[pl.BlockSpec((N_CHIPS, tr, tc), lambda i, s: (0,) + at(i)),
                   pl.BlockSpec((tr, tc), lambda i, s: at(i))])
    return pl.pallas_call(
        body, name=name, grid_spec=grid_spec,
        out_shape=[jax.ShapeDtypeStruct((N_CHIPS, hr, hc), BF16), jax.ShapeDtypeStruct((hr, hc), F32)],
        compiler_params=_cparams(("parallel",)),
    )(idx, a, other, a, other)


def _sum_parts(name, own, recv):
    h, c = own.shape
    n, (tr, tc), at = _tiling(h, c, 4)

    def body(o_ref, r_ref, out_ref):
        acc = o_ref[...]
        for k in range(3):
            acc = acc + r_ref[k].astype(F32)
        out_ref[...] = acc

    return pl.pallas_call(
        body, name=name, grid=(n,),
        in_specs=[pl.BlockSpec((tr, tc), at), pl.BlockSpec((3, tr, tc), lambda i: (0,) + at(i))],
        out_specs=pl.BlockSpec((tr, tc), at),
        out_shape=jax.ShapeDtypeStruct((h, c), F32),
        compiler_params=_cparams(("parallel",)),
    )(own, recv)


def _all_to_all(name, small):
    def body(sm_in, sm_out, send_sems, recv_sems, local_sem):
        x, y, c = lax.axis_index("x"), lax.axis_index("y"), lax.axis_index("c")
        dev = 4 * x + 2 * y + c
        local = pltpu.make_async_copy(sm_in, sm_out.at[dev], local_sem)
        local.start()
        rel = [(fx, fy, fc) for fx in (0, 1) for fy in (0, 1) for fc in (0, 1)][1:]
        sends = []
        for k, (fx, fy, fc) in enumerate(rel):
            cp = pltpu.make_async_remote_copy(
                src_ref=sm_in, dst_ref=sm_out.at[dev], send_sem=send_sems.at[k], recv_sem=recv_sems.at[k],
                device_id=(x ^ fx, y ^ fy, c ^ fc), device_id_type=MESH)
            cp.start()
            sends.append(cp)
        for k, (fx, fy, fc) in enumerate(rel):
            src_dev = 4 * (x ^ fx) + 2 * (y ^ fy) + (c ^ fc)
            pltpu.make_async_remote_copy(
                src_ref=sm_in, dst_ref=sm_out.at[src_dev], send_sem=send_sems.at[k], recv_sem=recv_sems.at[k],
                device_id=(x ^ fx, y ^ fy, c ^ fc), device_id_type=MESH).wait_recv()
        for cp in sends:
            cp.wait_send()
        local.wait()

    return pl.pallas_call(
        body, name=name, in_specs=[HBM], out_specs=HBM,
        out_shape=jax.ShapeDtypeStruct((N_DEV,) + small.shape, small.dtype),
        scratch_shapes=[pltpu.SemaphoreType.DMA((7,)), pltpu.SemaphoreType.DMA((7,)), pltpu.SemaphoreType.DMA],
    )(small)


def _swap_sibling(name, parts):
    n = len(parts)

    def body(*refs):
        ins, outs = refs[:n], refs[n:2 * n]
        send_sems, recv_sems = refs[2 * n:]
        x, y, c = lax.axis_index("x"), lax.axis_index("y"), lax.axis_index("c")
        cps = []
        for t in range(n):
            cp = pltpu.make_async_remote_copy(
                src_ref=ins[t], dst_ref=outs[t], send_sem=send_sems.at[t], recv_sem=recv_sems.at[t],
                device_id=(x, y, 1 - c), device_id_type=MESH)
            cp.start()
            cps.append(cp)
        for cp in cps:
            cp.wait_recv()
        for cp in cps:
            cp.wait_send()

    return pl.pallas_call(
        body, name=name,
        in_specs=[HBM] * n, out_specs=[HBM] * n,
        out_shape=[jax.ShapeDtypeStruct(a.shape, a.dtype) for a in parts],
        scratch_shapes=[pltpu.SemaphoreType.DMA((n,)), pltpu.SemaphoreType.DMA((n,))],
    )(*parts)


def _sum_slots(name, a):
    k, r, c = a.shape
    tr = _row_tile(r, (k + 1) * c * 4)

    def body(a_ref, o_ref):
        acc = a_ref[0]
        for i in range(1, k):
            acc = acc + a_ref[i]
        o_ref[...] = acc

    return pl.pallas_call(
        body, name=name, grid=(r // tr,),
        in_specs=[pl.BlockSpec((k, tr, c), lambda i: (0, i, 0))],
        out_specs=pl.BlockSpec((tr, c), lambda i: (i, 0)),
        out_shape=jax.ShapeDtypeStruct((r, c), a.dtype),
        compiler_params=_cparams(("parallel",)),
    )(a)


def _adamw(name, w, m, v, g_parts):
    r, c = w.shape
    ng = len(g_parts)
    tr = _row_tile(r, (7 + ng) * c * 4)
    c1 = 1.0 - ADAM_B1 ** ADAM_STEP
    c2 = 1.0 - ADAM_B2 ** ADAM_STEP

    def body(*refs):
        w_ref, m_ref, v_ref = refs[:3]
        g_refs = refs[3:3 + ng]
        go_ref, d_ref, mo_ref, vo_ref = refs[3 + ng:]
        g = g_refs[0][...]
        for gr in g_refs[1:]:
            g = g + gr[...]
        m_new = ADAM_B1 * m_ref[...] + (1.0 - ADAM_B1) * g
        v_new = ADAM_B2 * v_ref[...] + (1.0 - ADAM_B2) * (g * g)
        go_ref[...] = g
        mo_ref[...] = m_new
        vo_ref[...] = v_new
        d_ref[...] = -ADAM_LR * ((m_new / c1) / (jnp.sqrt(v_new / c2) + ADAM_EPS) + ADAM_WD * w_ref[...])

    spec = pl.BlockSpec((tr, c), lambda i: (i, 0))
    return pl.pallas_call(
        body, name=name, grid=(r // tr,),
        in_specs=[spec] * (3 + ng), out_specs=[spec] * 4,
        out_shape=[jax.ShapeDtypeStruct((r, c), F32)] * 4,
        compiler_params=_cparams(("parallel",)),
    )(w, m, v, *g_parts)


def _adamw_halves(name, w, m, v, own, other, idx, axis):
    hr, hc = own.shape
    nb, (tr, tc), at = _tiling(hr, hc, 9)
    c1 = 1.0 - ADAM_B1 ** ADAM_STEP
    c2 = 1.0 - ADAM_B2 ** ADAM_STEP

    def body(idx_ref, w_ref, m_ref, v_ref, own_ref, oth_ref, go_ref, d_ref, mo_ref, vo_ref):
        g = jnp.where(pl.program_id(0) == idx_ref[0], own_ref[...], oth_ref[...])
        m_new = ADAM_B1 * m_ref[...] + (1.0 - ADAM_B1) * g
        v_new = ADAM_B2 * v_ref[...] + (1.0 - ADAM_B2) * (g * g)
        go_ref[...] = g
        mo_ref[...] = m_new
        vo_ref[...] = v_new
        d_ref[...] = -ADAM_LR * ((m_new / c1) / (jnp.sqrt(v_new / c2) + ADAM_EPS) + ADAM_WD * w_ref[...])

    per_half = (hc // tc) if axis else (hr // tr)
    if axis:
        full = pl.BlockSpec((tr, tc), lambda hh, i, s: (at(i)[0], hh * per_half + at(i)[1]))
    else:
        full = pl.BlockSpec((tr, tc), lambda hh, i, s: (hh * per_half + at(i)[0], at(i)[1]))
    part = pl.BlockSpec((tr, tc), lambda hh, i, s: at(i))
    grid_spec = pltpu.PrefetchScalarGridSpec(
        num_scalar_prefetch=1, grid=(2, nb), in_specs=[full, full, full, part, part], out_specs=[full] * 4)
    return pl.pallas_call(
        body, name=name, grid_spec=grid_spec, out_shape=[jax.ShapeDtypeStruct(w.shape, F32)] * 4,
        compiler_params=_cparams(("parallel", "parallel")),
    )(idx, w, m, v, own, other)


WEIGHTS = ['norm_w', 'w_in', 's5_a_re', 's5_a_im', 's5_b_re', 's5_b_im', 's5_c_re', 's5_c_im', 's5_d', 's5_log_step',
           's5_w_glu', 's5_b_glu', 'ssd_conv_w', 'ssd_conv_b', 'ssd_dt_bias', 'ssd_a_log', 'ssd_d', 'ssd_norm_w',
           'w_br_s5', 'w_br_ssd', 'w_out', 'ple_norm_w', 'w_ple_gate', 'w_ple_proj', 'final_norm_w']
SHARDED = {'w_in': ((IN_PROJ_DIM, 1024), 0), 's5_w_glu': ((512, 512), 0), 'ssd_conv_w': ((SSD_CONV, SSD_CONV_DIM), 1),
           'w_br_s5': ((512, 1024), 1), 'w_br_ssd': ((1536, 1024), 0), 'w_out': ((1024, 1024), 0),
           'w_ple_gate': ((1024, 1024), 0), 'w_ple_proj': ((256, 1024), 1)}
TRANSPOSED = ('w_in',)
SMALL = [n for n in WEIGHTS if n not in SHARDED]


def _shard_shape(name):
    (r, c), ax = SHARDED[name]
    return (r // N_CHIPS, c) if ax == 0 else (r, c // N_CHIPS)


def _half_axis(name):
    return 0 if (_shard_shape(name)[0] // 2) % 16 == 0 else 1


def _shard2d(name, a):
    r, c = _shard_shape(name)
    return a.reshape(c, r).T if name in TRANSPOSED else a.reshape(r, c)


def _unshard2d(name, a2, shape):
    return (a2.T if name in TRANSPOSED else a2).reshape(shape)


def _unslot(name, a4):
    (r, c), ax = SHARDED[name]
    if ax == 0:
        return a4.reshape(r, c)
    return jnp.transpose(a4, (1, 0, 2)).reshape(r, c)


def _slot(name, full):
    (r, c), ax = SHARDED[name]
    if ax == 0:
        return full.reshape(N_CHIPS, r // N_CHIPS, c)
    return jnp.transpose(full.reshape(r, N_CHIPS, c // N_CHIPS), (1, 0, 2))


GHP = ('s5_b_re', 's5_b_im')


def _view_shape(name):
    if name in ('s5_a_re', 's5_a_im'):
        return (S5_GROUPS, S5_STATE)
    if name in GHP + ('s5_c_re', 's5_c_im'):
        return (S5_GROUPS, S5_GROUP, S5_STATE)
    if name == 'ssd_conv_w':
        return (SSD_CONV, SSD_CONV_DIM // N_CHIPS)
    return (1, {'s5_log_step': S5_GROUPS, 'ssd_conv_b': SSD_CONV_DIM, 'ssd_norm_w': SSD_WIDTH, 's5_d': S5_WIDTH,
                's5_b_glu': S5_WIDTH, 'ssd_dt_bias': SSD_HEADS, 'ssd_a_log': SSD_HEADS, 'ssd_d': SSD_HEADS}.get(name, D_MODEL))


def _view(name, a):
    if name in GHP:
        return jnp.swapaxes(a.reshape(S5_GROUPS, S5_STATE, S5_GROUP), 1, 2)
    return a.reshape(_view_shape(name))


def _unview(name, a, shape):
    return (jnp.swapaxes(a, 1, 2) if name in GHP else a).reshape(shape)


def _adamw_small(ws, ms, vs, gs):
    n = len(ws)
    c1 = 1.0 - ADAM_B1 ** ADAM_STEP
    c2 = 1.0 - ADAM_B2 ** ADAM_STEP

    def body(*refs):
        w_r, m_r, v_r, g_r = (refs[k * n:(k + 1) * n] for k in range(4))
        d_o, m_o, v_o = (refs[k * n:(k + 1) * n] for k in range(4, 7))
        for i in range(n):
            g = g_r[i][...]
            m_new = ADAM_B1 * m_r[i][...] + (1.0 - ADAM_B1) * g
            v_new = ADAM_B2 * v_r[i][...] + (1.0 - ADAM_B2) * (g * g)
            m_o[i][...] = m_new
            v_o[i][...] = v_new
            d_o[i][...] = -ADAM_LR * ((m_new / c1) / (jnp.sqrt(v_new / c2) + ADAM_EPS) + ADAM_WD * w_r[i][...])

    return pl.pallas_call(
        body, name="adamw_small", out_shape=[jax.ShapeDtypeStruct(w.shape, F32) for w in ws] * 3,
        compiler_params=pltpu.CompilerParams(vmem_limit_bytes=VMEM_LIMIT),
    )(*ws, *ms, *vs, *gs)


def _pack_small(vals):
    flat = jnp.concatenate([v.reshape(-1).astype(F32) for v in vals])
    rows = -(-flat.shape[0] // (256 * 128)) * 256
    return jnp.pad(flat, (0, rows * 128 - flat.shape[0])).reshape(rows, 128)


def _unpack_small(packed, shapes):
    flat = packed.reshape(-1)
    out, off = [], 0
    for sh in shapes:
        n = math.prod(sh)
        out.append(flat[off:off + n].reshape(sh))
        off += n
    return out


def kernel(x, p, norm_w, w_in, s5_a_re, s5_a_im, s5_b_re, s5_b_im, s5_c_re, s5_c_im, s5_d, s5_log_step, s5_w_glu, s5_b_glu, ssd_conv_w, ssd_conv_b, ssd_dt_bias, ssd_a_log, ssd_d, ssd_norm_w, w_br_s5, w_br_ssd, w_out, ple_norm_w, w_ple_gate, w_ple_proj, final_norm_w, loss_target, m_norm_w, m_w_in, m_s5_a_re, m_s5_a_im, m_s5_b_re, m_s5_b_im, m_s5_c_re, m_s5_c_im, m_s5_d, m_s5_log_step, m_s5_w_glu, m_s5_b_glu, m_ssd_conv_w, m_ssd_conv_b, m_ssd_dt_bias, m_ssd_a_log, m_ssd_d, m_ssd_norm_w, m_w_br_s5, m_w_br_ssd, m_w_out, m_ple_norm_w, m_w_ple_gate, m_w_ple_proj, m_final_norm_w, v_norm_w, v_w_in, v_s5_a_re, v_s5_a_im, v_s5_b_re, v_s5_b_im, v_s5_c_re, v_s5_c_im, v_s5_d, v_s5_log_step, v_s5_w_glu, v_s5_b_glu, v_ssd_conv_w, v_ssd_conv_b, v_ssd_dt_bias, v_ssd_a_log, v_ssd_d, v_ssd_norm_w, v_w_br_s5, v_w_br_ssd, v_w_out, v_ple_norm_w, v_w_ple_gate, v_w_ple_proj, v_final_norm_w):
    args = locals()
    wl = {n: args[n] for n in WEIGHTS}
    ml = {n: args["m_" + n] for n in WEIGHTS}
    vl = {n: args["v_" + n] for n in WEIGHTS}
    big = [n for n in SHARDED if n != 'ssd_conv_w']
    chip = _chip_index(lax.axis_index("x"), lax.axis_index("y"))
    idx = jnp.stack([lax.axis_index("c"), chip]).astype(jnp.int32)

    axes = [_half_axis(n) for n in big]
    first = ['w_in']
    rest = [n for n in big if n not in first]
    bf_shards = {n: _shard2d(n, wl[n]).astype(BF16) for n in big}
    w_in_t, conv_w4 = _gather_chips([bf_shards[n] for n in first], [_half_axis(n) for n in first],
                                    [_shard2d('ssd_conv_w', wl['ssd_conv_w'])])
    full = {n: wl[n] for n in SMALL}
    full["w_in_t"] = w_in_t
    full['ssd_conv_w'] = _unslot('ssd_conv_w', conv_w4)
    send_sems, recv_sems, thru, lands, token = _gather_start([bf_shards[n] for n in rest], w_in_t)

    def fetch_rest(after):
        own, zones = _gather_wait(send_sems, recv_sems, thru, lands, after)
        return {n: _unslot(n, lax.dynamic_update_slice(z, o[None], (chip, 0, 0)))
                for n, o, z in zip(rest, own, zones, strict=True)}

    full["_late"] = (token, fetch_rest)

    in_flight = []

    def send_rest(g_now):
        in_flight.extend(_scatter_start("scatter_rest_start", [_slot(n, g_now[n]) for n in rest], [], g_now['s5_w_glu']))
        return in_flight[4]

    full["_early"] = send_rest

    packed = [n for n in SMALL if n != 'norm_w']
    first_axes = [_half_axis(n) for n in first]
    st = {}

    def send_first(g_now, loss_now):
        small_pack = _pack_small([loss_now] + [g_now[n + "_ghp" if n in GHP else n] for n in packed]
                                 + [g_now['ssd_conv_w']])
        swapped = _swap_halves([g_now["w_in_t"]], first_axes, small_pack)
        st["pair"] = [_pair_sum("pair_sum_w_in", g_now["w_in_t"], swapped[0], idx, first_axes[0])]
        small_chip = _sum_slots("sum_small_pair", swapped[-1])
        half_rows = small_chip.shape[0] // 2
        my_half = lax.dynamic_slice(small_chip, (lax.axis_index("c") * half_rows, 0), (half_rows, small_chip.shape[1]))
        st["flight"] = _scatter_start("scatter_first_start", [pb for pb, _ in st["pair"]], [my_half], small_chip)
        return st["flight"][4]

    full["_w_in_ready"] = send_first
    loss, grad_x, g = _local_step(x, p[0], loss_target, full)
    pair = st["pair"]
    s_sems, r_sems, thru, lands, tok = st["flight"]
    small_shapes = [(1, 1)] + [_view_shape(n) for n in packed] + [(SSD_CONV, SSD_CONV_DIM)]
    norm_all = _all_to_all("norm_w_all_to_all", g['norm_w'].reshape(8, 128))
    norm_g = _sum_slots("sum_norm_w", norm_all).reshape(1, D_MODEL)

    out_g, out_d, out_m, out_v = {}, {}, {}, {}
    own_slots, zones = _scatter_wait("scatter_rest_wait", len(rest), *in_flight[:4], norm_g)
    chip_sums = [_sum_chips("sum_chips_" + n, a, z, idx, tok) for n, a, z in zip(rest, own_slots, zones, strict=True)]
    sib_sums = _swap_sibling("swap_sibling_rest", chip_sums)
    for n, own, sib in zip(rest, chip_sums, sib_sums, strict=True):
        res = _adamw("adamw_" + n, _shard2d(n, wl[n]), _shard2d(n, ml[n]), _shard2d(n, vl[n]), [own, sib])
        out_g[n], out_d[n], out_m[n], out_v[n] = (_unshard2d(n, r, wl[n].shape) for r in res)

    sent, got = _scatter_wait("scatter_first_wait", len(first), s_sems, r_sems, thru, lands, out_d[rest[-1]])
    small_zone = lax.dynamic_update_slice(got[-1], sent[-1][None], (chip, 0, 0))
    small_half = _sum_slots("sum_small_chips", small_zone)
    halves = [_sum_parts("sum_chips_" + n, own, r) for n, (_, own), r in zip(first, pair, got[:-1], strict=True)]
    swapped2 = _swap_sibling("swap_sibling_first", halves + [small_half])
    other_halves = swapped2[:-1]
    c_is_0 = lax.axis_index("c") == 0
    small_sum = jnp.concatenate([jnp.where(c_is_0, small_half, swapped2[-1]),
                                 jnp.where(c_is_0, swapped2[-1], small_half)], axis=0)
    for n, own, oth, ax in zip(first, halves, other_halves, first_axes, strict=True):
        res = _adamw_halves("adamw_" + n, _shard2d(n, wl[n]), _shard2d(n, ml[n]), _shard2d(n, vl[n]), own, oth, idx, ax)
        out_g[n], out_d[n], out_m[n], out_v[n] = (_unshard2d(n, r, wl[n].shape) for r in res)
    sm = _unpack_small(small_sum, small_shapes)
    loss_total = sm[0].reshape(())
    conv_g = lax.dynamic_slice(sm[-1], (0, chip * (SSD_CONV_DIM // N_CHIPS)), (SSD_CONV, SSD_CONV_DIM // N_CHIPS))
    names = SMALL + ['ssd_conv_w']
    by_name = {**dict(zip(packed, sm[1:-1], strict=True)), 'norm_w': norm_g, 'ssd_conv_w': conv_g}
    grads = [by_name[n] for n in names]
    res = _adamw_small([_view(n, wl[n]) for n in names], [_view(n, ml[n]) for n in names],
                       [_view(n, vl[n]) for n in names], grads)
    for i, n in enumerate(names):
        out_g[n] = _unview(n, grads[i], wl[n].shape)
        out_d[n], out_m[n], out_v[n] = (_unview(n, res[k * len(names) + i], wl[n].shape) for k in range(3))

    return (loss_total, grad_x, *[out_g[n] for n in WEIGHTS], *[out_d[n] for n in WEIGHTS],
            *[out_m[n] for n in WEIGHTS], *[out_v[n] for n in WEIGHTS])
```

```python
import functools
import math

import jax
import jax.numpy as jnp
from jax import lax
from jax.experimental import pallas as pl
from jax.experimental.pallas import tpu as pltpu

F32 = jnp.float32
BF16 = jnp.bfloat16
MESH = pl.DeviceIdType.MESH

D_MODEL = 1024
PLE_DIM = 256
RMS_EPS = 1e-6
S5_WIDTH = 512
S5_GROUP = 16
S5_GROUPS = 32
S5_STATE = 64
S5_N = S5_GROUPS * S5_STATE
S5_LB = 512
S5_NJ = S5_N // S5_LB
S5_TB = 2048
S5_LOG_TB = 8
SSD_WIDTH = 1536
SSD_HEADDIM = 64
SSD_HEADS = 24
SSD_GROUPS = 4
SSD_HPG = 6
SSD_STATE = 128
SSD_CONV = 4
SSD_CHUNK = 128
SSD_BC = 512
SSD_CONV_DIM = 2560
GROUP_W = SSD_WIDTH // SSD_GROUPS
N_CHIPS = 4
N_DEV = 8

OFF_XBC, OFF_U5, OFF_Z5, OFF_DT, OFF_G5, OFF_GS, OFF_ZS = 0, 2560, 3072, 3584, 4096, 5120, 6144
DT_W = 512
PROJ_W = 7680
IN_PROJ_DIM = 7192

ADAM_LR, ADAM_B1, ADAM_B2, ADAM_EPS, ADAM_WD, ADAM_STEP = 0.001, 0.9, 0.999, 1e-08, 0.01, 10

VMEM_LIMIT = 56 * 1024 * 1024


ROW_BLOCK_BYTES = 8 * 1024 * 1024


def _row_tile(r, bytes_per_row):
    for t in (r, 4096, 2048, 1024, 512, 256, 128, 64, 32, 16, 8):
        if t <= r and r % t == 0 and t * bytes_per_row <= ROW_BLOCK_BYTES:
            return t
    return r


def _cparams(sem):
    return pltpu.CompilerParams(dimension_semantics=sem, vmem_limit_bytes=VMEM_LIMIT)


def _dg(a, b, ca, cb):
    return lax.dot_general(a.astype(BF16), b.astype(BF16), (((ca,), (cb,)), ((), ())), preferred_element_type=F32)


@jax.custom_vjp
def dot_nn(a, b):
    return _dg(a, b, 1, 0)


@jax.custom_vjp
def dot_nt(a, b):
    return _dg(a, b, 1, 1)


@jax.custom_vjp
def dot_tn(a, b):
    return _dg(a, b, 0, 0)


dot_nn.defvjp(lambda a, b: (_dg(a, b, 1, 0), (a, b)), lambda r, g: (_dg(g, r[1], 1, 1), _dg(r[0], g, 0, 0)))
dot_nt.defvjp(lambda a, b: (_dg(a, b, 1, 1), (a, b)), lambda r, g: (_dg(g, r[1], 1, 0), _dg(g, r[0], 0, 0)))
dot_tn.defvjp(lambda a, b: (_dg(a, b, 0, 0), (a, b)), lambda r, g: (_dg(r[1], g, 1, 1), _dg(r[0], g, 1, 0)))


MM_VMEM_BUDGET = 40 * 1024 * 1024


def _mm_tiles(m, n, k, sa, sb, so, tn_only=None):
    best, best_key = None, None
    for tm in (1024, 512, 256, 128, 64, 32, 16, 8):
        if m % tm:
            continue
        for tn in (2048, 1536, 1280, 1024, 768, 640, 512, 384, 256, 128):
            if n % tn or (tn_only is not None and tn not in tn_only):
                continue
            for tk in (k, 2048, 1536, 1280, 1024, 768, 512, 256, 128):
                if k % tk or tk > max(k, 128):
                    continue
                need = 2 * (tm * tk * sa + tk * tn * sb + tm * tn * so) + (tm * tn * 4 if tk < k else 0)
                if need > MM_VMEM_BUDGET:
                    continue
                key = (tm * tn * tk, tk)
                if best_key is None or key > best_key:
                    best, best_key = (tm, tn, tk), key
    assert best is not None, (m, n, k)
    return best


def _matmul(name, a, b, *, ta=False, tb=False, a_win=None, out_dtype=F32, epilogue=None, epi_rows=(), epi_pars=(),
            epi_outs=(), full_rows=False, epi_accs=(), epi_into=None, dep=None):
    a_off, a_w = a_win if a_win is not None else (0, a.shape[1])
    if ta:
        kdim, m = a.shape[0], a_w
    else:
        m, kdim = a.shape[0], a_w
    n = b.shape[0] if tb else b.shape[1]
    assert (b.shape[1] if tb else b.shape[0]) == kdim, (name, a.shape, b.shape)
    out_dtypes = list(epi_outs) if epilogue is not None else [out_dtype]
    so = sum(jnp.dtype(d).itemsize for d in out_dtypes) + sum(r.dtype.itemsize for r, _ in epi_rows)
    tn_ok = [n] if full_rows else [t for t in (1024, 512, 256, 128) if all(off % t == 0 for _, off in epi_rows)]
    tm, tn, tk = _mm_tiles(m, n, kdim, a.dtype.itemsize, b.dtype.itemsize, so, tn_ok if epilogue is not None else None)
    nk = kdim // tk
    n_er, n_ep, n_out = len(epi_rows), len(epi_pars), len(out_dtypes)
    if ta:
        assert a_off % tm == 0
        a_spec = pl.BlockSpec((tk, tm), lambda i, j, k: (k, i + a_off // tm))
    else:
        assert a_off % tk == 0
        a_spec = pl.BlockSpec((tm, tk), lambda i, j, k: (i, k + a_off // tk))
    if tb:
        b_spec = pl.BlockSpec((tn, tk), lambda i, j, k: (j, k))
    else:
        b_spec = pl.BlockSpec((tk, tn), lambda i, j, k: (k, j))
    ca, cb = (0 if ta else 1), (1 if tb else 0)

    n_acc = len(epi_accs)
    n_x = (1 if epi_into is not None else 0) + (1 if dep is not None else 0)
    assert not (n_acc or epi_into is not None) or full_rows

    def body(a_ref, b_ref, *refs):
        er, ep = refs[:n_er], refs[n_er:n_er + n_ep]
        first_out = n_er + n_ep + n_x
        o_refs = refs[first_out:first_out + n_out]
        s_refs = refs[first_out + n_out:first_out + n_out + n_acc]
        acc = refs[first_out + n_out + n_acc:]

        def finish(c):
            outs = [c] if epilogue is None else epilogue(c, [r[...] for r in er], [p[...] for p in ep])
            if n_acc:
                outs, sums = outs

                @pl.when(pl.program_id(0) == 0)
                def _():
                    for s_ref in s_refs:
                        s_ref[...] = jnp.zeros_like(s_ref)

                for s_ref, s in zip(s_refs, sums, strict=True):
                    s_ref[...] += jnp.broadcast_to(s, s_ref.shape)
            for o_ref, o in zip(o_refs, outs, strict=True):
                o_ref[...] = o.astype(o_ref.dtype)

        if nk == 1:
            finish(_dg(a_ref[...], b_ref[...], ca, cb))
            return
        (acc_ref,) = acc
        k = pl.program_id(2)

        @pl.when(k == 0)
        def _():
            acc_ref[...] = jnp.zeros_like(acc_ref)

        acc_ref[...] += _dg(a_ref[...], b_ref[...], ca, cb)

        @pl.when(k == nk - 1)
        def _():
            finish(acc_ref[...])

    in_specs = [a_spec, b_spec]
    in_specs += [pl.BlockSpec((tm, tn), functools.partial(lambda i, j, k, c: (i, j + c), c=off // tn)) for _, off in epi_rows]
    in_specs += [pl.BlockSpec((p.shape[0], tn), lambda i, j, k: (0, j)) for p in epi_pars]
    out_specs = [pl.BlockSpec((tm, tn), lambda i, j, k: (i, j)) for _ in out_dtypes]
    out_shape = [jax.ShapeDtypeStruct((m, n), d) for d in out_dtypes]
    extra, aliases = [], {}
    if epi_into is not None:
        buf, off, width = epi_into
        assert off % width == 0 and buf.dtype == out_dtypes[0]
        in_specs.append(pl.BlockSpec(memory_space=pl.ANY))
        out_specs[0] = pl.BlockSpec((tm, width), functools.partial(lambda i, j, k, c: (i, c), c=off // width))
        out_shape[0] = jax.ShapeDtypeStruct(buf.shape, buf.dtype)
        extra, aliases = [buf], {2 + n_er + n_ep: 0}
    if dep is not None:
        in_specs.append(pl.BlockSpec(memory_space=pl.ANY))
        extra = extra + [dep]
    out_specs += [pl.BlockSpec((r, w), lambda i, j, k: (0, 0)) for r, w in epi_accs]
    out_shape += [jax.ShapeDtypeStruct((r, w), F32) for r, w in epi_accs]
    res = pl.pallas_call(
        body, name=name, grid=(m // tm, n // tn, nk),
        in_specs=in_specs, out_specs=out_specs, out_shape=out_shape, input_output_aliases=aliases,
        scratch_shapes=[pltpu.VMEM((tm, tn), F32)] if nk > 1 else [],
        compiler_params=_cparams(("arbitrary",) * 3 if n_acc else ("parallel", "parallel", "arbitrary")),
    )(a, b, *[r for r, _ in epi_rows], *epi_pars, *extra)
    return res if epilogue is not None else res[0]


BAND = 128


def _rowwise(name, fn, n_rows, tr, row_ins, par_ins, row_outs, acc_outs=(), into=None):
    nr, npar, no, na = len(row_ins), len(par_ins), len(row_outs), len(acc_outs)
    in_specs = []
    for arr, off, w in row_ins:
        assert off % w == 0 and arr.shape[0] == n_rows, (name, arr.shape, off, w)
        in_specs.append(pl.BlockSpec((tr, w), functools.partial(lambda i, c: (i, c), c=off // w)))
    for arr, off, w in par_ins:
        assert off % w == 0
        in_specs.append(pl.BlockSpec((arr.shape[0], w), functools.partial(lambda i, c: (0, c), c=off // w)))
    out_specs = [pl.BlockSpec((tr, w), lambda i: (i, 0)) for w, _ in row_outs]
    out_specs += [pl.BlockSpec((r, w), lambda i: (0, 0)) for r, w in acc_outs]
    out_shape = [jax.ShapeDtypeStruct((n_rows, w), dt) for w, dt in row_outs]
    out_shape += [jax.ShapeDtypeStruct((r, w), F32) for r, w in acc_outs]
    extra, aliases = [], {}
    if into is not None:
        buf, off = into
        w0 = row_outs[0][0]
        assert off % w0 == 0 and buf.dtype == row_outs[0][1]
        in_specs.append(pl.BlockSpec(memory_space=pl.ANY))
        out_specs[0] = pl.BlockSpec((tr, w0), functools.partial(lambda i, c: (i, c), c=off // w0))
        out_shape[0] = jax.ShapeDtypeStruct(buf.shape, buf.dtype)
        extra, aliases = [buf], {nr + npar: 0}
    nx = len(extra)

    def body(*refs):
        rows = [r[...] for r in refs[:nr]]
        pars = [r[...] for r in refs[nr:nr + npar]]
        o_refs = refs[nr + npar + nx:nr + npar + nx + no]
        a_refs = refs[nr + npar + nx + no:]
        outs, accs = fn(rows, pars)
        for o_ref, o in zip(o_refs, outs, strict=True):
            o_ref[...] = o.astype(o_ref.dtype)
        if na:
            @pl.when(pl.program_id(0) == 0)
            def _():
                for a_ref in a_refs:
                    a_ref[...] = jnp.zeros_like(a_ref)

            for a_ref, a in zip(a_refs, accs, strict=True):
                a_ref[...] += jnp.broadcast_to(a, a_ref.shape)

    res = pl.pallas_call(
        body, name=name, grid=(n_rows // tr,),
        in_specs=in_specs, out_specs=out_specs, out_shape=out_shape, input_output_aliases=aliases,
        compiler_params=_cparams(("arbitrary",) if na else ("parallel",)),
    )(*[a for a, _, _ in row_ins], *[a for a, _, _ in par_ins], *extra)
    return res


def _rms(x, w):
    return x * lax.rsqrt(jnp.mean(x * x, axis=-1, keepdims=True) + RMS_EPS) * w


def _s5_out(yc, u, z, t, d, bg):
    ge = jax.nn.gelu(yc + d * u)
    return ge * jax.nn.sigmoid(t + bg) * jax.nn.silu(z)


def _merge(g5, gs, m5, ms):
    return jax.nn.sigmoid(g5) * m5 + jax.nn.sigmoid(gs) * ms


def _head_loss(h1, pgl, pp, fw, tgt):
    h2 = h1 + jax.nn.sigmoid(pgl) * pp
    err = _rms(h2, fw) - tgt
    per_row = 0.5 * jnp.mean(err * err, axis=-1, keepdims=True)
    return jnp.sum(per_row, axis=0, keepdims=True)


def _s5_disc(a_re, a_im, log_step, b_re2, b_im2, expand):
    step = jnp.exp(log_step)
    mag = jnp.exp(a_re * step)
    lb_re = mag * jnp.cos(a_im * step)
    lb_im = mag * jnp.sin(a_im * step)
    den = a_re * a_re + a_im * a_im
    n_re = lb_re - 1.0
    f_re = (n_re * a_re + lb_im * a_im) / den
    f_im = (lb_im * a_re - n_re * a_im) / den
    hi = lax.Precision.HIGHEST
    fr = jnp.dot(expand, f_re, precision=hi, preferred_element_type=F32)
    fi = jnp.dot(expand, f_im, precision=hi, preferred_element_type=F32)
    return lb_re, lb_im, fr * b_re2 - fi * b_im2, fr * b_im2 + fi * b_re2


def _s5_params_fwd(a_re, a_im, log_step, b_re2, b_im2, expand):
    gp = a_re.shape

    def body(ar, ai, ls, br, bi, ex, pr_ref, pi_ref, bbr_ref, bbi_ref):
        lr, li, bbr, bbi = _s5_disc(ar[...], ai[...], ls[...], br[...], bi[...], ex[...])
        bbr_ref[...] = bbr
        bbi_ref[...] = bbi
        qr, qi = lr, li
        for k in range(S5_LOG_TB):
            pr_ref[k] = qr
            pi_ref[k] = qi
            qr, qi = qr * lr - qi * li, qr * li + qi * lr

    return pl.pallas_call(
        body, name="s5_params_fwd",
        out_shape=(jax.ShapeDtypeStruct((S5_LOG_TB,) + gp, F32), jax.ShapeDtypeStruct((S5_LOG_TB,) + gp, F32),
                   jax.ShapeDtypeStruct(b_re2.shape, F32), jax.ShapeDtypeStruct(b_re2.shape, F32)),
    )(a_re, a_im, log_step, b_re2, b_im2, expand)


def _s5_params_bwd(a_re, a_im, log_step, b_re2, b_im2, expand, d_lr, d_li, d_bbr, d_bbi):
    def body(ar, ai, ls, br, bi, ex, glr, gli, gbr, gbi, dar, dai, dls, dbr, dbi):
        _, vjp = jax.vjp(lambda *p: _s5_disc(*p, ex[...]), ar[...], ai[...], ls[...], br[...], bi[...])
        g = vjp((glr[...], gli[...], gbr[...], gbi[...]))
        for ref, val in zip((dar, dai, dls, dbr, dbi), g, strict=True):
            ref[...] = val

    return pl.pallas_call(
        body, name="s5_params_bwd",
        out_shape=tuple(jax.ShapeDtypeStruct(v.shape, F32) for v in (a_re, a_im, log_step, b_re2, b_im2)),
    )(a_re, a_im, log_step, b_re2, b_im2, expand, d_lr, d_li, d_bbr, d_bbi)


def _scan_block(x_ref, out_ref, lp, edge_pow, cr, ci, reverse, each=None):
    n_g = x_ref.shape[0] // 8
    sub = lax.broadcasted_iota(jnp.int32, (8, S5_LB), 0)
    steps = []
    for sh in (1, 2, 4):
        keep = (sub < 8 - sh) if reverse else (sub >= sh)
        steps.append((8 - sh if reverse else sh, jnp.where(keep, lp[sh - 1:sh, :S5_LB], 0.0),
                      jnp.where(keep, lp[sh - 1:sh, S5_LB:], 0.0)))
    e_r, e_i = edge_pow[:, :S5_LB], edge_pow[:, S5_LB:]
    for r in (range(n_g - 1, -1, -1) if reverse else range(n_g)):
        rows = slice(8 * r, 8 * r + 8)
        xr, xi = x_ref[rows, :S5_LB], x_ref[rows, S5_LB:]
        for by, a_r, a_i in steps:
            pr, pi = pltpu.roll(xr, by, 0), pltpu.roll(xi, by, 0)
            xr, xi = xr + a_r * pr - a_i * pi, xi + a_r * pi + a_i * pr
        xr = xr + e_r * cr - e_i * ci
        xi = xi + e_r * ci + e_i * cr
        out_ref[rows, :S5_LB] = xr
        out_ref[rows, S5_LB:] = xi
        if each is not None:
            each(r, xr, xi)
        cr, ci = (xr[0:1, :], xi[0:1, :]) if reverse else (xr[7:8, :], xi[7:8, :])
    return cr, ci


def _s5_fwd(proj, bb_band, c_band, lam_pow, s5_d, n_seq, seq_len):
    n_t = seq_len // S5_TB
    blk = 2 * S5_LB
    n_rows = n_seq * seq_len
    u_blk0 = OFF_U5 // BAND

    def body(u_ref, bb_ref, cb_ref, lp_ref, d_ref, s_ref, y_ref, ge_ref, cr, ci, buf):
        @pl.when(pl.program_id(2) == 0)
        def _():
            cr[...] = jnp.zeros_like(cr)
            ci[...] = jnp.zeros_like(ci)

        u = u_ref[...]
        buf[...] = _dg(u, bb_ref[...], 1, 0)
        lp = lp_ref[...]
        cr[...], ci[...] = _scan_block(buf, buf, lp, lp, cr[...], ci[...], False)
        s = buf[...].astype(s_ref.dtype)
        s_ref[...] = s
        y = _dg(s, cb_ref[...], 1, 1)
        y_ref[...] = y
        ge_ref[...] = jax.nn.gelu(y + d_ref[...] * u).astype(ge_ref.dtype)

    def rows(j, b, t):
        return (b * n_t + t, j)

    return pl.pallas_call(
        body, name="s5_fwd", grid=(S5_NJ, n_seq, n_t),
        in_specs=[pl.BlockSpec((S5_TB, BAND), lambda j, b, t: (b * n_t + t, u_blk0 + j)),
                  pl.BlockSpec((BAND, blk), lambda j, b, t: (j, 0)), pl.BlockSpec((BAND, blk), lambda j, b, t: (j, 0)),
                  pl.BlockSpec((S5_LOG_TB, blk), lambda j, b, t: (0, j)), pl.BlockSpec((1, BAND), lambda j, b, t: (0, j))],
        out_specs=[pl.BlockSpec((S5_TB, blk), rows), pl.BlockSpec((S5_TB, BAND), rows), pl.BlockSpec((S5_TB, BAND), rows)],
        out_shape=[jax.ShapeDtypeStruct((n_rows, S5_NJ * blk), BF16), jax.ShapeDtypeStruct((n_rows, S5_WIDTH), F32),
                   jax.ShapeDtypeStruct((n_rows, S5_WIDTH), BF16)],
        scratch_shapes=[pltpu.VMEM((1, S5_LB), F32), pltpu.VMEM((1, S5_LB), F32), pltpu.VMEM((S5_TB, blk), F32)],
        compiler_params=_cparams(("parallel", "parallel", "arbitrary")),
    )(proj, bb_band, c_band, lam_pow, s5_d)


def _s5_bwd(dyc, s, proj, du5_a, bb_band, c_band, lam_pow_conj, dproj, n_seq, seq_len):
    n_t = seq_len // S5_TB
    blk = 2 * S5_LB
    halo_rows = 16
    halo_per_blk = S5_TB // halo_rows
    u_blk0 = OFF_U5 // BAND

    def rows(j, b, t):
        return (b * n_t + (n_t - 1 - t), j)

    def u_rows(j, b, t):
        return (b * n_t + (n_t - 1 - t), u_blk0 + j)

    def halo(j, b, t):
        return (jnp.maximum((b * n_t + (n_t - 1 - t)) * halo_per_blk - 1, 0), j)

    def body(dy_ref, sq_ref, hq_ref, u_ref, dua_ref, bb_ref, cb_ref, lp_ref, _, du_ref, dl_ref, dbb_ref, dcb_ref,
             cr, ci, buf, s_ref):
        b, t = pl.program_id(1), pl.program_id(2)
        sq = sq_ref[...]
        s_ref[...] = sq.astype(F32)
        h_last = hq_ref[...].astype(F32)[halo_rows - 1:halo_rows, :]
        dy = dy_ref[...]
        buf[...] = _dg(dy, cb_ref[...], 1, 0)

        @pl.when(t == 0)
        def _():
            cr[...] = jnp.zeros_like(cr)
            ci[...] = jnp.zeros_like(ci)

        @pl.when((b == 0) & (t == 0))
        def _():
            dl_ref[...] = jnp.zeros_like(dl_ref)
            dbb_ref[...] = jnp.zeros_like(dbb_ref)
            dcb_ref[...] = jnp.zeros_like(dcb_ref)

        dcb_ref[...] += _dg(dy, sq, 0, 0)

        first_blk = t == n_t - 1
        sub = lax.broadcasted_iota(jnp.int32, (8, S5_LB), 0)
        acc = [jnp.zeros((8, S5_LB), F32), jnp.zeros((8, S5_LB), F32)]

        def each(r, gr, gi):
            rows = slice(8 * r, 8 * r + 8)
            if r == 0:
                before_r = jnp.where(first_blk, 0.0, h_last[:, :S5_LB])
                before_i = jnp.where(first_blk, 0.0, h_last[:, S5_LB:])
            else:
                before_r, before_i = s_ref[8 * r - 1:8 * r, :S5_LB], s_ref[8 * r - 1:8 * r, S5_LB:]
            sp_r = jnp.where(sub == 0, before_r, pltpu.roll(s_ref[rows, :S5_LB], 1, 0))
            sp_i = jnp.where(sub == 0, before_i, pltpu.roll(s_ref[rows, S5_LB:], 1, 0))
            acc[0] = acc[0] + gr * sp_r + gi * sp_i
            acc[1] = acc[1] + gi * sp_r - gr * sp_i

        lp = lp_ref[...]
        edge_pow = jnp.concatenate([lp[7 - i:8 - i, :] for i in range(8)], axis=0)
        cr[...], ci[...] = _scan_block(buf, buf, lp, edge_pow, cr[...], ci[...], True, each)
        g = buf[...].astype(BF16)
        du_ref[...] = (dua_ref[...] + _dg(g, bb_ref[...], 1, 1)).astype(du_ref.dtype)
        dbb_ref[...] += _dg(u_ref[...], g, 0, 0)
        dl_ref[:, :S5_LB] += jnp.sum(acc[0], axis=0, keepdims=True)
        dl_ref[:, S5_LB:] += jnp.sum(acc[1], axis=0, keepdims=True)

    band = pl.BlockSpec((BAND, blk), lambda j, b, t: (j, 0))
    return pl.pallas_call(
        body, name="s5_bwd", grid=(S5_NJ, n_seq, n_t),
        in_specs=[pl.BlockSpec((S5_TB, BAND), rows), pl.BlockSpec((S5_TB, blk), rows),
                  pl.BlockSpec((halo_rows, blk), halo), pl.BlockSpec((S5_TB, BAND), u_rows),
                  pl.BlockSpec((S5_TB, BAND), rows), band, band,
                  pl.BlockSpec((S5_LOG_TB, blk), lambda j, b, t: (0, j)), pl.BlockSpec(memory_space=pl.ANY)],
        out_specs=[pl.BlockSpec((S5_TB, BAND), u_rows), pl.BlockSpec((1, blk), lambda j, b, t: (0, j)), band, band],
        out_shape=[jax.ShapeDtypeStruct(dproj.shape, dproj.dtype), jax.ShapeDtypeStruct((1, S5_NJ * blk), F32),
                   jax.ShapeDtypeStruct(bb_band.shape, F32), jax.ShapeDtypeStruct(c_band.shape, F32)],
        input_output_aliases={8: 0},
        scratch_shapes=[pltpu.VMEM((1, S5_LB), F32), pltpu.VMEM((1, S5_LB), F32), pltpu.VMEM((S5_TB, blk), F32),
                        pltpu.VMEM((S5_TB, blk), F32)],
        compiler_params=_cparams(("parallel", "arbitrary", "arbitrary")),
    )(dyc, s, s, proj, du5_a, bb_band, c_band, lam_pow_conj, dproj)


CONV_TR = 1024
CONV_CW = 512


def _shift_down(x, halo, k):
    if k == 0:
        return x
    row8 = lax.broadcasted_iota(jnp.int32, halo.shape, 0)
    rolled = pltpu.roll(x, k, 0)
    top = jnp.where(row8 < k, pltpu.roll(halo, k, 0), rolled[:8])
    if x.shape[0] == 8:
        return top
    return jnp.concatenate([top, rolled[8:]], axis=0)


def _shift_up(x, halo, k):
    if k == 0:
        return x
    n = x.shape[0]
    row8 = lax.broadcasted_iota(jnp.int32, halo.shape, 0)
    rolled = pltpu.roll(x, n - k, 0)
    bot = jnp.where(row8 >= 8 - k, pltpu.roll(halo, 8 - k, 0), rolled[n - 8:])
    if n == 8:
        return bot
    return jnp.concatenate([rolled[:n - 8], bot], axis=0)


def _conv_pre(x, halo, w, b):
    acc = b + w[SSD_CONV - 1:SSD_CONV, :] * x
    for k in range(SSD_CONV - 1):
        acc = acc + w[k:k + 1, :] * _shift_down(x, halo, SSD_CONV - 1 - k)
    return acc


def _conv_specs(seq_len, col_off):
    lt = seq_len // CONV_TR
    cb = col_off // CONV_CW
    cur = pl.BlockSpec((CONV_TR, CONV_CW), lambda j, i: (i, j + cb))
    prev = pl.BlockSpec((8, CONV_CW), lambda j, i: (jnp.maximum(i * (CONV_TR // 8) - 1, 0), j + cb))
    return lt, cur, prev


def _conv_fwd(proj, conv_w, conv_b, n_rows, seq_len):
    lt, cur, prev = _conv_specs(seq_len, OFF_XBC)

    def body(x_ref, h_ref, w_ref, b_ref, o_ref):
        halo = jnp.where(pl.program_id(1) % lt == 0, 0.0, h_ref[...])
        o_ref[...] = jax.nn.silu(_conv_pre(x_ref[...], halo, w_ref[...], b_ref[...]))

    return pl.pallas_call(
        body, name="ssd_conv_fwd", grid=(SSD_CONV_DIM // CONV_CW, n_rows // CONV_TR),
        in_specs=[cur, prev, pl.BlockSpec((SSD_CONV, CONV_CW), lambda j, i: (0, j)),
                  pl.BlockSpec((1, CONV_CW), lambda j, i: (0, j))],
        out_specs=pl.BlockSpec((CONV_TR, CONV_CW), lambda j, i: (i, j)),
        out_shape=jax.ShapeDtypeStruct((n_rows, SSD_CONV_DIM), F32),
        compiler_params=_cparams(("parallel", "parallel")),
    )(proj, proj, conv_w, conv_b)


def _conv_bwd(name, proj, d_act, conv_w, conv_b, n_rows, seq_len, col_off, dproj):
    width = d_act.shape[1]
    lt, cur, prev = _conv_specs(seq_len, OFF_XBC + col_off)
    n_blk = n_rows // CONV_TR
    cb = (OFF_XBC + col_off) // CONV_CW
    pb = col_off // CONV_CW
    nxt = pl.BlockSpec((8, CONV_CW), lambda j, i: (jnp.minimum((i + 1) * (CONV_TR // 8), n_rows // 8 - 1), j + cb))
    d_cur = pl.BlockSpec((CONV_TR, CONV_CW), lambda j, i: (i, j))
    d_nxt = pl.BlockSpec((8, CONV_CW), lambda j, i: (jnp.minimum((i + 1) * (CONV_TR // 8), n_rows // 8 - 1), j))

    def dsilu(pre):
        sg = jax.nn.sigmoid(pre)
        return sg * (1.0 + pre * (1.0 - sg))

    def body(x_ref, hp_ref, hn_ref, d_ref, dn_ref, w_ref, b_ref, _, dx_ref, dw_ref, db_ref):
        i = pl.program_id(1)
        x, w, b = x_ref[...], w_ref[...], b_ref[...]
        halo_p = jnp.where(i % lt == 0, 0.0, hp_ref[...])
        at_end = i % lt == lt - 1
        dpre = d_ref[...] * dsilu(_conv_pre(x, halo_p, w, b))
        pre_n = _conv_pre(hn_ref[...], x[CONV_TR - 8:, :], w, b)
        dpre_n = jnp.where(at_end, 0.0, dn_ref[...] * dsilu(pre_n))
        dx = w[SSD_CONV - 1:SSD_CONV, :] * dpre
        for k in range(SSD_CONV - 1):
            dx = dx + w[k:k + 1, :] * _shift_up(dpre, dpre_n, SSD_CONV - 1 - k)
        dx_ref[...] = dx.astype(dx_ref.dtype)

        @pl.when(i == 0)
        def _():
            dw_ref[...] = jnp.zeros_like(dw_ref)
            db_ref[...] = jnp.zeros_like(db_ref)

        for k in range(SSD_CONV):
            xs = _shift_down(x, halo_p, SSD_CONV - 1 - k)
            dw_ref[k:k + 1, :] += jnp.sum(dpre * xs, axis=0, keepdims=True)
        db_ref[...] += jnp.sum(dpre, axis=0, keepdims=True)

    return pl.pallas_call(
        body, name=name, grid=(width // CONV_CW, n_blk),
        in_specs=[cur, prev, nxt, d_cur, d_nxt,
                  pl.BlockSpec((SSD_CONV, CONV_CW), lambda j, i: (0, j + pb)),
                  pl.BlockSpec((1, CONV_CW), lambda j, i: (0, j + pb)), pl.BlockSpec(memory_space=pl.ANY)],
        out_specs=[pl.BlockSpec((CONV_TR, CONV_CW), lambda j, i: (i, j + cb)),
                   pl.BlockSpec((SSD_CONV, CONV_CW), lambda j, i: (0, j)),
                   pl.BlockSpec((1, CONV_CW), lambda j, i: (0, j))],
        out_shape=[jax.ShapeDtypeStruct(dproj.shape, dproj.dtype),
                   jax.ShapeDtypeStruct((SSD_CONV, width), F32), jax.ShapeDtypeStruct((1, width), F32)],
        input_output_aliases={7: 0},
        compiler_params=_cparams(("parallel", "arbitrary")),
    )(proj, proj, proj, d_act, d_act, conv_w, conv_b, dproj)


def _split3(x):
    hi = x.astype(BF16)
    r = x - hi.astype(F32)
    mid = r.astype(BF16)
    return hi, mid, (r - mid.astype(F32)).astype(BF16)


def _sel_dot(a, b, a_is_sel):
    dn = (((1,), (0,)), ((), ()))
    if a_is_sel:
        return sum(lax.dot_general(a, t, dn, preferred_element_type=F32) for t in _split3(b))
    return sum(lax.dot_general(t, b, dn, preferred_element_type=F32) for t in _split3(a))


@jax.custom_vjp
def sel_left(sel, sel_t, x):
    return _sel_dot(sel, x, True)


@jax.custom_vjp
def sel_right(x, sel, sel_t):
    return _sel_dot(x, sel, False)


sel_left.defvjp(lambda s, st, x: (_sel_dot(s, x, True), (s, st)),
                lambda r, g: (jnp.zeros_like(r[0]), jnp.zeros_like(r[1]), _sel_dot(r[1], g, True)))
sel_right.defvjp(lambda x, s, st: (_sel_dot(x, s, False), (s, st)),
                 lambda r, g: (_sel_dot(g, r[1], False), jnp.zeros_like(r[0]), jnp.zeros_like(r[1])))


def _ssd_chunk(xs, bm, cm, dtr, st, dtb, alog, dsk, k):
    dt = jax.nn.softplus(dtr + dtb)
    acum = sel_left(k["tri"], k["tri_t"], dt * (-jnp.exp(alog)))
    dt_e = sel_right(dt, k["spread"], k["spread_t"])
    ac_e = sel_right(acum, k["spread"], k["spread_t"])
    al_e = ac_e[SSD_CHUNK - 1:SSD_CHUNK, :]
    dsk_e = sel_right(jnp.broadcast_to(dsk, (8, 128)), k["spread"], k["spread_t"])[0:1, :]
    xdt = xs * dt_e
    acum_t = acum.T
    scores = dot_nt(cm, bm)
    y = dot_nn(cm, st) * jnp.exp(ac_e) + xs * dsk_e
    for j in range(SSD_HPG):
        lmat = jnp.exp(jnp.where(k["causal"], acum[:, j:j + 1] - acum_t[j:j + 1, :], -jnp.inf))
        y = y + dot_nn(scores * lmat, jnp.where(k["head"] == j, xdt, 0.0))
    new = st * jnp.exp(al_e) + dot_tn(bm, xdt * jnp.exp(al_e - ac_e))
    return y, new


def _ssd_consts():
    r = lax.broadcasted_iota(jnp.int32, (SSD_CHUNK, SSD_CHUNK), 0)
    c = lax.broadcasted_iota(jnp.int32, (SSD_CHUNK, SSD_CHUNK), 1)
    hd = jnp.int32(SSD_HEADDIM)
    sr = lax.broadcasted_iota(jnp.int32, (128, GROUP_W), 0)
    sc = lax.div(lax.broadcasted_iota(jnp.int32, (128, GROUP_W), 1), hd)
    tr = lax.div(lax.broadcasted_iota(jnp.int32, (GROUP_W, 128), 0), hd)
    tc = lax.broadcasted_iota(jnp.int32, (GROUP_W, 128), 1)
    return {"tri": (r >= c).astype(BF16), "tri_t": (c >= r).astype(BF16), "causal": r >= c,
            "spread": (sr == sc).astype(BF16), "spread_t": (tr == tc).astype(BF16),
            "head": lax.div(lax.broadcasted_iota(jnp.int32, (SSD_CHUNK, GROUP_W), 1), hd)}


def _ssd_specs(n_seq, n_c, reverse):
    def cidx(c):
        return n_c - 1 - c if reverse else c

    def rows(w, first=0):
        return pl.BlockSpec((n_seq, SSD_CHUNK, w), lambda g, c: (0, cidx(c), first + g))

    par = pl.BlockSpec((1, 128), lambda g, c: (0, g))
    st = pl.BlockSpec((n_seq, 1, 1, SSD_STATE, GROUP_W), lambda g, c: (0, cidx(c), g, 0, 0))
    return rows, par, st


def _ssd_chunk_gated(xs, bm, cm, dtr, st, dtb, alog, dsk, z, nw, k):
    y, new = _ssd_chunk(xs, bm, cm, dtr, st, dtb, alog, dsk, k)
    yg = y * jax.nn.silu(z)
    return yg * lax.rsqrt(jnp.mean(yg * yg, axis=-1, keepdims=True) + RMS_EPS) * nw, new


def _ssd_fwd(xbc_act, proj, dtb, alog, dsk, norm_w, n_seq, seq_len):
    n_c = seq_len // SSD_CHUNK
    gps = 2
    xbc3 = xbc_act.reshape(n_seq, seq_len, SSD_CONV_DIM)
    proj3 = proj.reshape(n_seq, seq_len, PROJ_W)

    def rows(w, first=0):
        return pl.BlockSpec((n_seq, SSD_CHUNK, gps * w), lambda g, c: (0, c, first + g))

    def par(w):
        return pl.BlockSpec((1, gps * w), lambda g, c: (0, g))

    def body(xs_ref, bm_ref, cm_ref, dt_ref, dtb_ref, al_ref, dk_ref, z_ref, nw_ref, y_ref, st_ref, state):
        @pl.when(pl.program_id(1) == 0)
        def _():
            state[...] = jnp.zeros_like(state)

        consts = _ssd_consts()
        for i in range(gps):
            wide, narrow = slice(i * GROUP_W, (i + 1) * GROUP_W), slice(i * 128, (i + 1) * 128)
            for b in range(n_seq):
                prev = state[b, i]
                st_ref[b, 0, i] = prev
                y, new = _ssd_chunk_gated(xs_ref[b, :, wide], bm_ref[b, :, narrow], cm_ref[b, :, narrow],
                                          dt_ref[b, :, narrow], prev, dtb_ref[:, narrow], al_ref[:, narrow],
                                          dk_ref[:, narrow], z_ref[b, :, wide], nw_ref[:, wide], consts)
                y_ref[b, :, wide] = y.astype(y_ref.dtype)
                state[b, i] = new

    y3, states = pl.pallas_call(
        body, name="ssd_fwd", grid=(SSD_GROUPS // gps, n_c),
        in_specs=[rows(GROUP_W), rows(SSD_STATE, SSD_WIDTH // (gps * SSD_STATE)),
                  rows(SSD_STATE, (SSD_WIDTH + SSD_BC) // (gps * SSD_STATE)), rows(128, OFF_DT // (gps * 128)),
                  par(128), par(128), par(128), rows(GROUP_W, OFF_ZS // (gps * GROUP_W)), par(GROUP_W)],
        out_specs=[rows(GROUP_W),
                   pl.BlockSpec((n_seq, 1, gps, SSD_STATE, GROUP_W), lambda g, c: (0, c, g, 0, 0))],
        out_shape=[jax.ShapeDtypeStruct((n_seq, seq_len, SSD_WIDTH), BF16),
                   jax.ShapeDtypeStruct((n_seq, n_c, SSD_GROUPS, SSD_STATE, GROUP_W), F32)],
        scratch_shapes=[pltpu.VMEM((n_seq, gps, SSD_STATE, GROUP_W), F32)],
        compiler_params=_cparams(("parallel", "arbitrary")),
    )(xbc3, xbc3, xbc3, proj3, dtb, alog, dsk, proj3, norm_w)
    return y3.reshape(n_seq * seq_len, SSD_WIDTH), states


def _ssd_bwd(xbc_act, proj, states, dy, dtb, alog, dsk, norm_w, n_seq, seq_len, dproj):
    n_c = seq_len // SSD_CHUNK
    n_rows = n_seq * seq_len
    rows, par_s, st_s = _ssd_specs(n_seq, n_c, True)
    xbc3 = xbc_act.reshape(n_seq, seq_len, SSD_CONV_DIM)
    proj3 = proj.reshape(n_seq, seq_len, PROJ_W)

    def body(xs_ref, bm_ref, cm_ref, dt_ref, st_ref, dy_ref, dtb_ref, al_ref, dk_ref, z_ref, nw_ref, _,
             dxs_ref, dbm_ref, dcm_ref, dz_ref, ddt_ref, ddtb_ref, dal_ref, ddk_ref, dnw_ref, dstate):
        @pl.when(pl.program_id(1) == 0)
        def _():
            dstate[...] = jnp.zeros_like(dstate)
            for ref in (ddtb_ref, dal_ref, ddk_ref, dnw_ref):
                ref[...] = jnp.zeros_like(ref)

        consts = _ssd_consts()
        for b in range(n_seq):
            _, vjp = jax.vjp(
                lambda *a: _ssd_chunk_gated(*a, consts),
                xs_ref[b], bm_ref[b], cm_ref[b], dt_ref[b], st_ref[b, 0, 0], dtb_ref[...], al_ref[...], dk_ref[...],
                z_ref[b], nw_ref[...])
            dxs, dbm, dcm, ddtr, dprev, ddtb, dal, ddk, dz, dnw = vjp((dy_ref[b], dstate[b]))
            dxs_ref[b] = dxs
            dbm_ref[b] = dbm
            dcm_ref[b] = dcm
            dz_ref[b] = dz.astype(dz_ref.dtype)
            ddt_ref[b] = ddtr.astype(ddt_ref.dtype)
            ddtb_ref[...] += ddtb
            dal_ref[...] += dal
            ddk_ref[...] += ddk
            dnw_ref[...] += dnw
            dstate[b] = dprev

    acc = pl.BlockSpec((1, 128), lambda g, c: (0, g))
    res = pl.pallas_call(
        body, name="ssd_bwd", grid=(SSD_GROUPS, n_c),
        in_specs=[rows(GROUP_W), rows(SSD_STATE, SSD_WIDTH // SSD_STATE), rows(SSD_STATE, (SSD_WIDTH + SSD_BC) // SSD_STATE),
                  rows(128, OFF_DT // 128), st_s, rows(GROUP_W), par_s, par_s, par_s, rows(GROUP_W, OFF_ZS // GROUP_W),
                  pl.BlockSpec((1, GROUP_W), lambda g, c: (0, g)), pl.BlockSpec(memory_space=pl.ANY)],
        out_specs=[rows(GROUP_W), rows(SSD_STATE), rows(SSD_STATE), rows(GROUP_W, OFF_ZS // GROUP_W), rows(128),
                   acc, acc, acc, pl.BlockSpec((1, GROUP_W), lambda g, c: (0, g))],
        out_shape=[jax.ShapeDtypeStruct((n_seq, seq_len, SSD_WIDTH), F32), jax.ShapeDtypeStruct((n_seq, seq_len, SSD_BC), F32),
                   jax.ShapeDtypeStruct((n_seq, seq_len, SSD_BC), F32),
                   jax.ShapeDtypeStruct((n_seq, seq_len, PROJ_W), dproj.dtype),
                   jax.ShapeDtypeStruct((n_seq, seq_len, DT_W), BF16),
                   jax.ShapeDtypeStruct((1, 512), F32), jax.ShapeDtypeStruct((1, 512), F32),
                   jax.ShapeDtypeStruct((1, 512), F32), jax.ShapeDtypeStruct((1, SSD_WIDTH), F32)],
        input_output_aliases={11: 3},
        scratch_shapes=[pltpu.VMEM((n_seq, SSD_STATE, GROUP_W), F32)],
        compiler_params=_cparams(("parallel", "arbitrary")),
    )(xbc3, xbc3, xbc3, proj3, states, dy.reshape(n_seq, seq_len, SSD_WIDTH), dtb, alog, dsk, proj3, norm_w,
      dproj.reshape(n_seq, seq_len, PROJ_W))
    flat = [r.reshape(n_rows, r.shape[-1]) for r in res[:5]]
    return (*flat, *res[5:])


def _pad_heads(v):
    return jnp.pad(v.reshape(SSD_GROUPS, SSD_HPG), ((0, 0), (0, 128 - SSD_HPG))).reshape(1, SSD_GROUPS * 128)


def _unpad_heads(v):
    return v.reshape(SSD_GROUPS, 128)[:, :SSD_HPG].reshape(1, SSD_HEADS)


def _state_cols(v):
    re, im = v
    lead = re.shape[:-1]
    re = re.reshape(lead + (S5_NJ, 1, S5_LB))
    im = im.reshape(lead + (S5_NJ, 1, S5_LB))
    return jnp.concatenate([re, im], axis=-2).reshape(lead + (2 * S5_N,))


def _state_uncols(v):
    lead = v.shape[:-1]
    v = v.reshape(lead + (S5_NJ, 2, S5_LB))
    return v[..., 0, :].reshape(lead + (S5_N,)), v[..., 1, :].reshape(lead + (S5_N,))


GROUPS_PER_BAND = BAND // S5_GROUP


def _band(w2_re, w2_im):
    gh = S5_GROUPS * S5_GROUP
    rg = ((jnp.arange(gh) // S5_GROUP) % GROUPS_PER_BAND)[:, None, None]
    cg = jnp.arange(GROUPS_PER_BAND)[None, :, None]
    parts = [jnp.where(rg == cg, v[:, None, :], 0.0).reshape(gh, S5_LB) for v in (w2_re, w2_im)]
    return jnp.concatenate(parts, axis=1)


def _band_take(wb):
    gh = S5_GROUPS * S5_GROUP
    w4 = wb.reshape(gh, 2, GROUPS_PER_BAND, S5_STATE)
    sel = w4[jnp.arange(gh), :, (jnp.arange(gh) // S5_GROUP) % GROUPS_PER_BAND, :]
    return sel[:, 0, :], sel[:, 1, :]


W_IN_SHARD = IN_PROJ_DIM // N_CHIPS
W_IN_SEGS = ((0, 512, OFF_U5), (512, 1024, OFF_Z5), (1024, 2560, OFF_ZS), (2560, 5120, OFF_XBC), (5144, 7192, OFF_G5))
DT_ROWS = (5120, 5144)


def _w_in_pieces():
    runs = []
    segs = list(W_IN_SEGS) + [(DT_ROWS[0] + SSD_HPG * g, DT_ROWS[0] + SSD_HPG * (g + 1), OFF_DT + 128 * g)
                              for g in range(SSD_GROUPS)]
    for lo, hi, off in segs:
        for j in range(N_CHIPS):
            s, e = max(lo, j * W_IN_SHARD), min(hi, (j + 1) * W_IN_SHARD)
            if s < e:
                runs.append((j, s - j * W_IN_SHARD, off + s - lo, e - s))
    return runs


RELAYOUT_LANES = 256


def _pad_w_in_t(a4):
    runs = _w_in_pieces()

    def body(a_ref, o_ref):
        o_ref[pl.ds(OFF_DT, DT_W), :] = jnp.zeros((DT_W, RELAYOUT_LANES), o_ref.dtype)
        for j, src, dst, n in runs:
            o_ref[pl.ds(dst, n), :] = a_ref[j, pl.ds(src, n), :]

    return pl.pallas_call(
        body, name="w_in_to_padded", grid=(D_MODEL // RELAYOUT_LANES,),
        in_specs=[pl.BlockSpec((N_CHIPS, W_IN_SHARD, RELAYOUT_LANES), lambda i: (0, 0, i))],
        out_specs=pl.BlockSpec((PROJ_W, RELAYOUT_LANES), lambda i: (0, i)),
        out_shape=jax.ShapeDtypeStruct((PROJ_W, D_MODEL), a4.dtype),
        compiler_params=_cparams(("parallel",)),
    )(a4)


def _unpad_w_in_t(wp):
    runs = _w_in_pieces()

    def body(p_ref, o_ref):
        for j, dst, src, n in runs:
            o_ref[j, pl.ds(dst, n), :] = p_ref[pl.ds(src, n), :]

    return pl.pallas_call(
        body, name="w_in_from_padded", grid=(D_MODEL // RELAYOUT_LANES,),
        in_specs=[pl.BlockSpec((PROJ_W, RELAYOUT_LANES), lambda i: (0, i))],
        out_specs=pl.BlockSpec((N_CHIPS, W_IN_SHARD, RELAYOUT_LANES), lambda i: (0, 0, i)),
        out_shape=jax.ShapeDtypeStruct((N_CHIPS, W_IN_SHARD, D_MODEL), wp.dtype),
        compiler_params=_cparams(("parallel",)),
    )(wp)


def _local_step(x, p, tgt, w):
    n_seq, seq_len, _ = x.shape
    n_rows = n_seq * seq_len
    tr = 512
    x2 = x.reshape(n_rows, D_MODEL)
    p2 = p.reshape(n_rows, PLE_DIM)
    t2 = tgt.reshape(n_rows, D_MODEL)
    row = functools.partial(_rowwise, n_rows=n_rows, tr=tr)

    w_pad_t = _pad_w_in_t(w["w_in_t"])
    norm_w = w["norm_w"].reshape(1, D_MODEL)
    ple_norm_w = w["ple_norm_w"].reshape(1, D_MODEL)
    final_w = w["final_norm_w"].reshape(1, D_MODEL)
    s5_d = w["s5_d"].reshape(1, S5_WIDTH)
    b_glu = w["s5_b_glu"].reshape(1, S5_WIDTH)
    conv_w = w["ssd_conv_w"].reshape(SSD_CONV, SSD_CONV_DIM)
    conv_b = w["ssd_conv_b"].reshape(1, SSD_CONV_DIM)
    ssd_norm_w = w["ssd_norm_w"].reshape(1, SSD_WIDTH)
    dtb, alog, dsk = (_pad_heads(w[k].reshape(1, SSD_HEADS)) for k in ("ssd_dt_bias", "ssd_a_log", "ssd_d"))

    gh = S5_GROUPS * S5_GROUP
    a_re = w["s5_a_re"].reshape(S5_GROUPS, S5_STATE)
    a_im = w["s5_a_im"].reshape(S5_GROUPS, S5_STATE)
    log_step = w["s5_log_step"].reshape(S5_GROUPS, 1)
    b_re2 = jnp.transpose(w["s5_b_re"].reshape(S5_GROUPS, S5_STATE, S5_GROUP), (0, 2, 1)).reshape(gh, S5_STATE)
    b_im2 = jnp.transpose(w["s5_b_im"].reshape(S5_GROUPS, S5_STATE, S5_GROUP), (0, 2, 1)).reshape(gh, S5_STATE)
    expand = (jnp.arange(gh)[:, None] // S5_GROUP == jnp.arange(S5_GROUPS)[None, :]).astype(F32)
    pow_re, pow_im, bb_re2, bb_im2 = _s5_params_fwd(a_re, a_im, log_step, b_re2, b_im2, expand)
    lam_pow = _state_cols((pow_re.reshape(S5_LOG_TB, S5_N), pow_im.reshape(S5_LOG_TB, S5_N)))
    lam_pow_conj = _state_cols((pow_re.reshape(S5_LOG_TB, S5_N), -pow_im.reshape(S5_LOG_TB, S5_N)))
    bb_band = _band(bb_re2, bb_im2).astype(BF16)
    c_band = _band(w["s5_c_re"].reshape(gh, S5_STATE), -w["s5_c_im"].reshape(gh, S5_STATE)).astype(BF16)

    late = w.get("_late")
    (hn,) = row("rms_in", lambda r, q: ([_rms(r[0], q[0])], []), row_ins=[(x2, 0, D_MODEL)],
                par_ins=[(norm_w, 0, D_MODEL)] + ([(late[0], 0, 128)] if late else []), row_outs=[(D_MODEL, BF16)])
    proj = _matmul("mm_proj", hn, w_pad_t, tb=True)
    s, yc, ge = _s5_fwd(proj, bb_band, c_band, lam_pow, s5_d, n_seq, seq_len)
    if late:
        w = {**w, **late[1](ge)}
    tg, y5 = _matmul("mm_s5_glu", ge, w["s5_w_glu"], epilogue=lambda c, r, q: [c, _s5_out(r[0], r[1], r[2], c, *q)],
                     epi_rows=[(yc, 0), (proj, OFF_U5), (proj, OFF_Z5)], epi_pars=[s5_d, b_glu], epi_outs=[F32, BF16])

    xbc_act = _conv_fwd(proj, conv_w, conv_b, n_rows, seq_len)
    yss, states = _ssd_fwd(xbc_act, proj, dtb, alog, dsk, ssd_norm_w, n_seq, seq_len)

    m5 = _matmul("mm_br_s5", y5, w["w_br_s5"])
    ms, merged = _matmul("mm_br_ssd", yss, w["w_br_ssd"], epilogue=lambda c, r, q: [c, _merge(r[0], r[1], r[2], c)],
                         epi_rows=[(proj, OFF_G5), (proj, OFF_GS), (m5, 0)], epi_outs=[F32, BF16])

    def resid_norm(c, r, q):
        h1_ = r[0] + c
        return [h1_, _rms(h1_, q[0])]

    h1, hp = _matmul("mm_out", merged, w["w_out"], epilogue=resid_norm, epi_rows=[(x2, 0)], epi_pars=[ple_norm_w],
                     epi_outs=[F32, BF16], full_rows=True)
    pp = _matmul("mm_ple_proj", p2, w["w_ple_proj"])

    def head_fn(pgl_, r, q):
        h1_, pp_, tgt_ = r
        loss, vjp = jax.vjp(lambda a, b, c, f: _head_loss(a, b, c, f, tgt_), h1_, pgl_, pp_, q[0])
        dh1_, dpgl_, dpp_, dfw_ = vjp(jnp.ones_like(loss))
        return [dh1_, dpgl_, dpp_], [loss, dfw_]

    dh2, dpgl, dpp, loss_acc, d_final_w = _matmul(
        "mm_ple_gate_head", hp, w["w_ple_gate"], epilogue=head_fn, epi_rows=[(h1, 0), (pp, 0), (t2, 0)],
        epi_pars=[final_w], epi_outs=[F32, BF16, BF16], epi_accs=[(1, 128), (1, D_MODEL)], full_rows=True)
    loss = loss_acc[0, 0]

    g = {}
    g["final_norm_w"] = d_final_w
    g["w_ple_gate"] = _matmul("mm_d_w_ple_gate", hp, dpgl, ta=True)
    g["w_ple_proj"] = _matmul("mm_d_w_ple_proj", p2, dpp, ta=True)

    def ple_norm_bwd(dhp_, r, q):
        h1_, dh2_ = r
        _, vjp = jax.vjp(_rms, h1_, q[0])
        dh, dw = vjp(dhp_)
        dh = dh + dh2_
        return [dh, dh], [dw]

    dh1, dh1_b, g["ple_norm_w"] = _matmul(
        "mm_d_hp", dpgl, w["w_ple_gate"], tb=True, epilogue=ple_norm_bwd, epi_rows=[(h1, 0), (dh2, 0)],
        epi_pars=[ple_norm_w], epi_outs=[F32, BF16], epi_accs=[(1, D_MODEL)], full_rows=True)
    g["w_out"] = _matmul("mm_d_w_out", merged, dh1_b, ta=True)

    dproj = lax.empty((n_rows, PROJ_W), BF16)

    def merge_bwd(dmerged, r, q):
        sg5, sgs = jax.nn.sigmoid(r[0]), jax.nn.sigmoid(r[1])
        d_gates = jnp.concatenate([dmerged * r[2] * sg5 * (1.0 - sg5), dmerged * r[3] * sgs * (1.0 - sgs)], axis=1)
        return [d_gates, dmerged * sg5, dmerged * sgs]

    dproj, dm5, dms = _matmul(
        "mm_d_merged", dh1_b, w["w_out"], tb=True, epilogue=merge_bwd,
        epi_rows=[(proj, OFF_G5), (proj, OFF_GS), (m5, 0), (ms, 0)], epi_outs=[BF16, BF16, BF16], full_rows=True,
        epi_into=(dproj, OFF_G5, 2 * D_MODEL))
    g["w_br_s5"] = _matmul("mm_d_w_br_s5", y5, dm5, ta=True)
    g["w_br_ssd"] = _matmul("mm_d_w_br_ssd", yss, dms, ta=True)
    dyss = _matmul("mm_d_yss", dms, w["w_br_ssd"], tb=True)

    def s5_out_bwd_a(dy_, r, q):
        yc_, u_, z_, t_ = r
        d_, bg_ = q
        ge_ = jax.nn.gelu(yc_ + d_ * u_)
        _, vjp = jax.vjp(lambda a, z, t, b: a * jax.nn.sigmoid(t + b) * jax.nn.silu(z), ge_, z_, t_, bg_)
        dge, dz, dt_, dbg = vjp(dy_)
        return [dz, dge, dt_], [dbg]

    dproj, dge_a, dtg, g["s5_b_glu"] = _matmul(
        "mm_d_y5", dm5, w["w_br_s5"], tb=True, epilogue=s5_out_bwd_a,
        epi_rows=[(yc, 0), (proj, OFF_U5), (proj, OFF_Z5), (tg, 0)], epi_pars=[s5_d, b_glu],
        epi_outs=[BF16, F32, BF16], epi_accs=[(1, S5_WIDTH)], full_rows=True, epi_into=(dproj, OFF_Z5, S5_WIDTH))
    g["s5_w_glu"] = _matmul("mm_d_w_glu", ge, dtg, ta=True)
    early = w["_early"](g) if "_early" in w else None

    def s5_out_bwd_b(dge_b, r, q):
        yc_, u_, da_ = r
        _, vjp = jax.vjp(lambda yc, u, d: jax.nn.gelu(yc + d * u), yc_, u_, q[0])
        dyc_, du_, dd_ = vjp(da_ + dge_b)
        return [dyc_, du_], [dd_]

    dyc, du5_a, g["s5_d"] = _matmul(
        "mm_d_ge", dtg, w["s5_w_glu"], tb=True, epilogue=s5_out_bwd_b,
        epi_rows=[(yc, 0), (proj, OFF_U5), (dge_a, 0)], epi_pars=[s5_d], epi_outs=[BF16, F32],
        epi_accs=[(1, S5_WIDTH)], full_rows=True, dep=early)
    dproj, d_lam, d_bb_band, d_c_band = _s5_bwd(dyc, s, proj, du5_a, bb_band, c_band, lam_pow_conj, dproj,
                                                n_seq, seq_len)

    d_lr, d_li = _state_uncols(d_lam)
    d_bbr, d_bbi = _band_take(d_bb_band)
    d_are, d_aim, d_ls, d_br2, d_bi2 = _s5_params_bwd(
        a_re, a_im, log_step, b_re2, b_im2, expand,
        d_lr.reshape(S5_GROUPS, S5_STATE), d_li.reshape(S5_GROUPS, S5_STATE), d_bbr, d_bbi)
    g["s5_a_re"], g["s5_a_im"], g["s5_log_step"] = d_are, d_aim, d_ls
    g["s5_b_re_ghp"], g["s5_b_im_ghp"] = d_br2, d_bi2
    d_cr, d_ci = _band_take(d_c_band)
    g["s5_c_re"], g["s5_c_im"] = d_cr, -d_ci

    dxs, dbm, dcm, dproj, ddt, d_dtb, d_alog, d_dsk, g["ssd_norm_w"] = _ssd_bwd(
        xbc_act, proj, states, dyss, dtb, alog, dsk, ssd_norm_w, n_seq, seq_len, dproj)
    (dproj,) = row("ssd_ddt", lambda r, q: ([r[0]], []), row_ins=[(ddt, 0, DT_W)], par_ins=[],
                   row_outs=[(DT_W, BF16)], into=(dproj, OFF_DT))
    g["ssd_dt_bias"], g["ssd_a_log"], g["ssd_d"] = _unpad_heads(d_dtb), _unpad_heads(d_alog), _unpad_heads(d_dsk)
    conv_dw, conv_db = [], []
    for nm, d_act, off in (("x", dxs, 0), ("b", dbm, SSD_WIDTH), ("c", dcm, SSD_WIDTH + SSD_BC)):
        dproj, dw_, db_ = _conv_bwd("ssd_conv_bwd_" + nm, proj, d_act, conv_w, conv_b, n_rows, seq_len, off, dproj)
        conv_dw.append(dw_)
        conv_db.append(db_)
    g["ssd_conv_w"] = jnp.concatenate(conv_dw, axis=1)
    g["ssd_conv_b"] = jnp.concatenate(conv_db, axis=1)

    g["w_in_t"] = _unpad_w_in_t(_matmul("mm_d_w_in", dproj, hn, ta=True))
    def norm_bwd(dhn_, r, q):
        x_, dh1_ = r
        _, vjp = jax.vjp(_rms, x_, q[0])
        dx_, dw_ = vjp(dhn_)
        return [dx_ + dh1_], [dw_]

    sent = w["_w_in_ready"](g, loss) if "_w_in_ready" in w else None
    dx, g["norm_w"] = _matmul("mm_d_hn", dproj, w_pad_t, epilogue=norm_bwd, epi_rows=[(x2, 0), (dh1, 0)],
                              epi_pars=[norm_w], epi_outs=[F32], epi_accs=[(1, D_MODEL)], full_rows=True, dep=sent)
    return loss, dx.reshape(x.shape), g


HBM = pl.BlockSpec(memory_space=pltpu.HBM)


def _chip_index(x, y):
    return 2 * x + y


def _half(shape2d, axis, which):
    h = shape2d[axis] // 2
    sl = pl.ds(pl.multiple_of(which * h, 128 if axis else 8), h)
    return (slice(None), sl) if axis else (sl, slice(None))


def _gather_chips(split, axes, whole):
    ns, nw = len(split), len(whole)
    n = ns + nw

    def body(*refs):
        ins, outs = refs[:n], refs[n:2 * n]
        ici_send, ici_recv, d2d_send, d2d_recv, local_sems = refs[2 * n:]
        x, y, c = lax.axis_index("x"), lax.axis_index("y"), lax.axis_index("c")
        me = _chip_index(x, y)
        sibling = (x, y, 1 - c)
        peers = [(1 - x, y), (x, 1 - y), (1 - x, 1 - y)]

        def half(t, which):
            return _half(split[t].shape, axes[t], which)

        copies = []
        for t in range(n):
            loc = pltpu.make_async_copy(ins[t], outs[t].at[me], local_sems.at[t])
            loc.start()
            copies.append(loc)

        def ici(t, k, slot):
            px, py = peers[k]
            if t < ns:
                src, dst = ins[t].at[half(t, c)], outs[t].at[(slot,) + half(t, c)]
            else:
                src, dst = ins[t], outs[t].at[slot]
            return pltpu.make_async_remote_copy(src_ref=src, dst_ref=dst, send_sem=ici_send.at[t, k],
                                                recv_sem=ici_recv.at[t, k], device_id=(px, py, c), device_id_type=MESH)

        def d2d(t, k, which):
            rows = outs[t].at[(_chip_index(*peers[k]),) + half(t, which)]
            return pltpu.make_async_remote_copy(src_ref=rows, dst_ref=rows, send_sem=d2d_send.at[t, k],
                                                recv_sem=d2d_recv.at[t, k], device_id=sibling, device_id_type=MESH)

        sends = []
        for t in range(n):
            for k in range(3):
                cp = ici(t, k, me)
                cp.start()
                sends.append(cp)
        for t in range(n):
            for k in range(3):
                ici(t, k, _chip_index(*peers[k])).wait_recv()
                if t < ns:
                    cp = d2d(t, k, c)
                    cp.start()
                    sends.append(cp)
        for t in range(ns):
            for k in range(3):
                d2d(t, k, 1 - c).wait_recv()
        for cp in sends:
            cp.wait_send()
        for cp in copies:
            cp.wait()

    arrays = list(split) + list(whole)
    return pl.pallas_call(
        body, name="gather_weights",
        in_specs=[HBM] * n, out_specs=[HBM] * n,
        out_shape=[jax.ShapeDtypeStruct((N_CHIPS,) + a.shape, a.dtype) for a in arrays],
        scratch_shapes=[pltpu.SemaphoreType.DMA((n, 3)), pltpu.SemaphoreType.DMA((n, 3)),
                        pltpu.SemaphoreType.DMA((ns, 3)), pltpu.SemaphoreType.DMA((ns, 3)),
                        pltpu.SemaphoreType.DMA((n,))],
    )(*arrays)


SEM = pl.BlockSpec(memory_space=pltpu.SEMAPHORE)
DATAFLOW = pltpu.SideEffectType.DATAFLOW_SIDE_EFFECTING


def _gather_start(shards, after):
    n = len(shards)

    def body(*refs):
        ins, lands = refs[:n], refs[n:2 * n]
        send_sems, recv_sems = refs[2 * n + 1], refs[2 * n + 2]
        token = refs[-1]
        x, y, c = lax.axis_index("x"), lax.axis_index("y"), lax.axis_index("c")
        me = _chip_index(x, y)
        for t in range(n):
            for k, (px, py) in enumerate([(1 - x, y), (x, 1 - y), (1 - x, 1 - y)]):
                pltpu.make_async_remote_copy(
                    src_ref=ins[t], dst_ref=lands[t].at[me], send_sem=send_sems.at[3 * t + k],
                    recv_sem=recv_sems.at[3 * t + k],
                    device_id=(px, py, c), device_id_type=MESH).start()
        token[...] = jnp.zeros_like(token)

    zones = [lax.empty((N_CHIPS,) + a.shape, a.dtype) for a in shards]
    res = pl.pallas_call(
        body, name="gather_rest_start",
        out_shape=(pltpu.SemaphoreType.DMA((3 * n,)), pltpu.SemaphoreType.DMA((3 * n,)),
                   *[pltpu.HBM(a.shape, a.dtype) for a in shards], *[pltpu.HBM(z.shape, z.dtype) for z in zones],
                   jax.ShapeDtypeStruct((8, 128), F32)),
        in_specs=[HBM] * (2 * n) + [pl.BlockSpec(memory_space=pl.ANY)],
        out_specs=(SEM, SEM, *[HBM] * (2 * n), pl.BlockSpec(memory_space=pltpu.VMEM)),
        input_output_aliases={t: 2 + t for t in range(2 * n)},
        compiler_params=pltpu.CompilerParams(has_side_effects=DATAFLOW),
    )(*[pltpu.with_memory_space_constraint(a, pltpu.HBM) for a in shards],
      *[pltpu.with_memory_space_constraint(z, pltpu.HBM) for z in zones], after)
    return res[0], res[1], list(res[2:2 + n]), list(res[2 + n:2 + 2 * n]), res[-1]


def _gather_wait(send_sems, recv_sems, thru, lands, after):
    n = len(thru)

    def body(*refs):
        ins, zones = refs[:n], refs[n:2 * n]
        s_sems, r_sems = refs[2 * n], refs[2 * n + 1]
        x, y, c = lax.axis_index("x"), lax.axis_index("y"), lax.axis_index("c")
        for t in range(n):
            for k, (px, py) in enumerate([(1 - x, y), (x, 1 - y), (1 - x, 1 - y)]):
                cp = pltpu.make_async_remote_copy(
                    src_ref=ins[t], dst_ref=zones[t].at[_chip_index(px, py)], send_sem=s_sems.at[3 * t + k],
                    recv_sem=r_sems.at[3 * t + k], device_id=(px, py, c), device_id_type=MESH)
                cp.wait_send()
                cp.wait_recv()

    res = pl.pallas_call(
        body, name="gather_rest_wait",
        out_shape=(*[pltpu.HBM(a.shape, a.dtype) for a in thru], *[pltpu.HBM(z.shape, z.dtype) for z in lands]),
        in_specs=[HBM] * (2 * n) + [SEM, SEM, pl.BlockSpec(memory_space=pl.ANY)], out_specs=[HBM] * (2 * n),
        input_output_aliases={t: t for t in range(2 * n)},
        compiler_params=pltpu.CompilerParams(has_side_effects=DATAFLOW),
    )(*thru, *lands, send_sems, recv_sems, after)
    return list(res[:n]), list(res[n:])


def _scatter_ends(t, k, n_slotted, ins, zones, x, y, me):
    px, py = [(1 - x, y), (x, 1 - y), (1 - x, 1 - y)][k]
    if t < n_slotted:
        return ins[t].at[_chip_index(px, py)], zones[t].at[k], (px, py)
    return ins[t], (zones[t].at[me], zones[t].at[_chip_index(px, py)]), (px, py)


def _scatter_start(name, slotted, whole, after):
    n_s = len(slotted)
    arrays = list(slotted) + list(whole)
    n = len(arrays)

    def body(*refs):
        ins, lands = refs[:n], refs[n:2 * n]
        send_sems, recv_sems = refs[2 * n + 1], refs[2 * n + 2]
        token = refs[-1]
        x, y, c = lax.axis_index("x"), lax.axis_index("y"), lax.axis_index("c")
        for t in range(n):
            for k in range(3):
                src, dst, (px, py) = _scatter_ends(t, k, n_s, ins, lands, x, y, _chip_index(x, y))
                pltpu.make_async_remote_copy(
                    src_ref=src, dst_ref=dst if t < n_s else dst[0], send_sem=send_sems.at[3 * t + k],
                    recv_sem=recv_sems.at[3 * t + k], device_id=(px, py, c), device_id_type=MESH).start()
        token[...] = jnp.zeros_like(token)

    zones = [lax.empty((3,) + a.shape[1:], a.dtype) for a in slotted]
    zones += [lax.empty((N_CHIPS,) + a.shape, a.dtype) for a in whole]
    slotted = arrays
    res = pl.pallas_call(
        body, name=name,
        out_shape=(pltpu.SemaphoreType.DMA((3 * n,)), pltpu.SemaphoreType.DMA((3 * n,)),
                   *[pltpu.HBM(a.shape, a.dtype) for a in slotted], *[pltpu.HBM(z.shape, z.dtype) for z in zones],
                   jax.ShapeDtypeStruct((8, 128), F32)),
        in_specs=[HBM] * (2 * n) + [pl.BlockSpec(memory_space=pl.ANY)],
        out_specs=(SEM, SEM, *[HBM] * (2 * n), pl.BlockSpec(memory_space=pltpu.VMEM)),
        input_output_aliases={t: 2 + t for t in range(2 * n)},
        compiler_params=pltpu.CompilerParams(has_side_effects=DATAFLOW),
    )(*[pltpu.with_memory_space_constraint(a, pltpu.HBM) for a in slotted],
      *[pltpu.with_memory_space_constraint(z, pltpu.HBM) for z in zones], after)
    return res[0], res[1], list(res[2:2 + n]), list(res[2 + n:2 + 2 * n]), res[-1]


def _scatter_wait(name, n_slotted, send_sems, recv_sems, thru, lands, after):
    n = len(thru)

    def body(*refs):
        ins, zones = refs[:n], refs[n:2 * n]
        s_sems, r_sems = refs[2 * n], refs[2 * n + 1]
        x, y, c = lax.axis_index("x"), lax.axis_index("y"), lax.axis_index("c")
        for t in range(n):
            for k in range(3):
                src, dst, (px, py) = _scatter_ends(t, k, n_slotted, ins, zones, x, y, _chip_index(x, y))
                cp = pltpu.make_async_remote_copy(
                    src_ref=src, dst_ref=dst if t < n_slotted else dst[1], send_sem=s_sems.at[3 * t + k],
                    recv_sem=r_sems.at[3 * t + k], device_id=(px, py, c), device_id_type=MESH)
                cp.wait_send()
                cp.wait_recv()

    res = pl.pallas_call(
        body, name=name,
        out_shape=(*[pltpu.HBM(a.shape, a.dtype) for a in thru], *[pltpu.HBM(z.shape, z.dtype) for z in lands]),
        in_specs=[HBM] * (2 * n) + [SEM, SEM, pl.BlockSpec(memory_space=pl.ANY)], out_specs=[HBM] * (2 * n),
        input_output_aliases={t: t for t in range(2 * n)},
        compiler_params=pltpu.CompilerParams(has_side_effects=DATAFLOW),
    )(*thru, *lands, send_sems, recv_sems, after)
    return list(res[:n]), list(res[n:])


def _sum_chips(name, slotted, recv, idx, dep):
    _, r, c = slotted.shape
    tr = _row_tile(r, 5 * c * 4)

    def body(idx_ref, own_ref, r_ref, _, o_ref):
        acc = own_ref[0]
        for k in range(3):
            acc = acc + r_ref[k]
        o_ref[...] = acc

    grid_spec = pltpu.PrefetchScalarGridSpec(
        num_scalar_prefetch=1, grid=(r // tr,),
        in_specs=[pl.BlockSpec((1, tr, c), lambda i, s: (s[1], i, 0)), pl.BlockSpec((3, tr, c), lambda i, s: (0, i, 0)),
                  pl.BlockSpec(memory_space=pl.ANY)],
        out_specs=pl.BlockSpec((tr, c), lambda i, s: (i, 0)))
    return pl.pallas_call(
        body, name=name, grid_spec=grid_spec, out_shape=jax.ShapeDtypeStruct((r, c), F32),
        compiler_params=_cparams(("parallel",)),
    )(idx, slotted, recv, dep)


def _half_shape(shape2d, axis):
    r, c = shape2d
    return (r, c // 2) if axis else (r // 2, c)


def _swap_halves(slotted, axes, small):
    n = len(slotted)

    def body(*refs):
        ins, sm_in = refs[:n], refs[n]
        outs, sm_out = refs[n + 1:2 * n + 1], refs[2 * n + 1]
        send_sems, recv_sems, local_sem = refs[2 * n + 2:]
        x, y, c = lax.axis_index("x"), lax.axis_index("y"), lax.axis_index("c")
        local = pltpu.make_async_copy(sm_in, sm_out.at[c], local_sem)
        local.start()
        sends = []
        for t in range(n):
            other = (slice(None),) + _half(slotted[t].shape[1:], axes[t], 1 - c)
            cp = pltpu.make_async_remote_copy(
                src_ref=ins[t].at[other], dst_ref=outs[t], send_sem=send_sems.at[t],
                recv_sem=recv_sems.at[t], device_id=(x, y, 1 - c), device_id_type=MESH)
            cp.start()
            sends.append(cp)
        cp = pltpu.make_async_remote_copy(
            src_ref=sm_in, dst_ref=sm_out.at[c], send_sem=send_sems.at[n], recv_sem=recv_sems.at[n],
            device_id=(x, y, 1 - c), device_id_type=MESH)
        cp.start()
        sends.append(cp)
        for cp in sends[:n]:
            cp.wait_recv()
        pltpu.make_async_remote_copy(
            src_ref=sm_in, dst_ref=sm_out.at[1 - c], send_sem=send_sems.at[n], recv_sem=recv_sems.at[n],
            device_id=(x, y, 1 - c), device_id_type=MESH).wait_recv()
        for cp in sends:
            cp.wait_send()
        local.wait()

    return pl.pallas_call(
        body, name="swap_halves",
        in_specs=[HBM] * (n + 1), out_specs=[HBM] * (n + 1),
        out_shape=[jax.ShapeDtypeStruct((a.shape[0],) + _half_shape(a.shape[1:], ax), a.dtype)
                   for a, ax in zip(slotted, axes, strict=True)]
        + [jax.ShapeDtypeStruct((2,) + small.shape, small.dtype)],
        scratch_shapes=[pltpu.SemaphoreType.DMA((n + 1,)), pltpu.SemaphoreType.DMA((n + 1,)), pltpu.SemaphoreType.DMA],
    )(*slotted, small)


def _tiling(r, c, f32_per_elem):
    if r % 8 == 0:
        tr = _row_tile(r, f32_per_elem * c * 4)
        return r // tr, (tr, c), lambda i: (i, 0)
    assert c % 128 == 0, (r, c)
    return c // 128, (r, 128), lambda i: (0, i)


def _pair_sum(name, slotted, other, idx, axis):
    _, r, c = slotted.shape
    hr, hc = _half_shape((r, c), axis)
    n, (tr, tc), at = _tiling(hr, hc, 13)
    if axis == 0:
        a = slotted.reshape(N_CHIPS, 2, hr, c)
        a_all = pl.BlockSpec((N_CHIPS, 1, tr, tc), lambda i, s: (0, s[0]) + at(i))
        a_own = pl.BlockSpec((1, 1, tr, tc), lambda i, s: (s[1], s[0]) + at(i))
    else:
        a, per_half = slotted, hc // tc
        a_all = pl.BlockSpec((N_CHIPS, tr, tc), lambda i, s: (0, at(i)[0], s[0] * per_half + at(i)[1]))
        a_own = pl.BlockSpec((1, tr, tc), lambda i, s: (s[1], at(i)[0], s[0] * per_half + at(i)[1]))

    def body(idx_ref, a_ref, b_ref, am_ref, bm_ref, p_ref, own_ref):
        mine, mine_own = (a_ref[:, 0], am_ref[0, 0]) if axis == 0 else (a_ref[...], am_ref[0])
        p_ref[...] = (mine + b_ref[...]).astype(p_ref.dtype)
        own_ref[...] = mine_own + bm_ref[0]

    grid_spec = pltpu.PrefetchScalarGridSpec(
        num_scalar_prefetch=1, grid=(n,),
        in_specs=[a_all, pl.BlockSpec((N_CHIPS, tr, tc), lambda i, s: (0,) + at(i)),
                  a_own, pl.BlockSpec((1, tr, tc), lambda i, s: (s[1],) + at(i))],
        out_specs=[pl.BlockSpec((N_CHIPS, tr, tc), lambda i, s: (0,) + at(i)),
                   pl.BlockSpec((tr, tc), lambda i, s: at(i))])
    return pl.pallas_call(
        body, name=name, grid_spec=grid_spec,
        out_shape=[jax.ShapeDtypeStruct((N_CHIPS, hr, hc), BF16), jax.ShapeDtypeStruct((hr, hc), F32)],
        compiler_params=_cparams(("parallel",)),
    )(idx, a, other, a, other)


def _sum_parts(name, own, recv):
    h, c = own.shape
    n, (tr, tc), at = _tiling(h, c, 4)

    def body(o_ref, r_ref, out_ref):
        acc = o_ref[...]
        for k in range(3):
            acc = acc + r_ref[k].astype(F32)
        out_ref[...] = acc

    return pl.pallas_call(
        body, name=name, grid=(n,),
        in_specs=[pl.BlockSpec((tr, tc), at), pl.BlockSpec((3, tr, tc), lambda i: (0,) + at(i))],
        out_specs=pl.BlockSpec((tr, tc), at),
        out_shape=jax.ShapeDtypeStruct((h, c), F32),
        compiler_params=_cparams(("parallel",)),
    )(own, recv)


def _all_to_all(name, small):
    def body(sm_in, sm_out, send_sems, recv_sems, local_sem):
        x, y, c = lax.axis_index("x"), lax.axis_index("y"), lax.axis_index("c")
        dev = 4 * x + 2 * y + c
        local = pltpu.make_async_copy(sm_in, sm_out.at[dev], local_sem)
        local.start()
        rel = [(fx, fy, fc) for fx in (0, 1) for fy in (0, 1) for fc in (0, 1)][1:]
        sends = []
        for k, (fx, fy, fc) in enumerate(rel):
            cp = pltpu.make_async_remote_copy(
                src_ref=sm_in, dst_ref=sm_out.at[dev], send_sem=send_sems.at[k], recv_sem=recv_sems.at[k],
                device_id=(x ^ fx, y ^ fy, c ^ fc), device_id_type=MESH)
            cp.start()
            sends.append(cp)
        for k, (fx, fy, fc) in enumerate(rel):
            src_dev = 4 * (x ^ fx) + 2 * (y ^ fy) + (c ^ fc)
            pltpu.make_async_remote_copy(
                src_ref=sm_in, dst_ref=sm_out.at[src_dev], send_sem=send_sems.at[k], recv_sem=recv_sems.at[k],
                device_id=(x ^ fx, y ^ fy, c ^ fc), device_id_type=MESH).wait_recv()
        for cp in sends:
            cp.wait_send()
        local.wait()

    return pl.pallas_call(
        body, name=name, in_specs=[HBM], out_specs=HBM,
        out_shape=jax.ShapeDtypeStruct((N_DEV,) + small.shape, small.dtype),
        scratch_shapes=[pltpu.SemaphoreType.DMA((7,)), pltpu.SemaphoreType.DMA((7,)), pltpu.SemaphoreType.DMA],
    )(small)


def _swap_sibling(name, parts):
    n = len(parts)

    def body(*refs):
        ins, outs = refs[:n], refs[n:2 * n]
        send_sems, recv_sems = refs[2 * n:]
        x, y, c = lax.axis_index("x"), lax.axis_index("y"), lax.axis_index("c")
        cps = []
        for t in range(n):
            cp = pltpu.make_async_remote_copy(
                src_ref=ins[t], dst_ref=outs[t], send_sem=send_sems.at[t], recv_sem=recv_sems.at[t],
                device_id=(x, y, 1 - c), device_id_type=MESH)
            cp.start()
            cps.append(cp)
        for cp in cps:
            cp.wait_recv()
        for cp in cps:
            cp.wait_send()

    return pl.pallas_call(
        body, name=name,
        in_specs=[HBM] * n, out_specs=[HBM] * n,
        out_shape=[jax.ShapeDtypeStruct(a.shape, a.dtype) for a in parts],
        scratch_shapes=[pltpu.SemaphoreType.DMA((n,)), pltpu.SemaphoreType.DMA((n,))],
    )(*parts)


def _sum_slots(name, a):
    k, r, c = a.shape
    tr = _row_tile(r, (k + 1) * c * 4)

    def body(a_ref, o_ref):
        acc = a_ref[0]
        for i in range(1, k):
            acc = acc + a_ref[i]
        o_ref[...] = acc

    return pl.pallas_call(
        body, name=name, grid=(r // tr,),
        in_specs=[pl.BlockSpec((k, tr, c), lambda i: (0, i, 0))],
        out_specs=pl.BlockSpec((tr, c), lambda i: (i, 0)),
        out_shape=jax.ShapeDtypeStruct((r, c), a.dtype),
        compiler_params=_cparams(("parallel",)),
    )(a)


def _adamw(name, w, m, v, g_parts):
    r, c = w.shape
    ng = len(g_parts)
    tr = _row_tile(r, (7 + ng) * c * 4)
    c1 = 1.0 - ADAM_B1 ** ADAM_STEP
    c2 = 1.0 - ADAM_B2 ** ADAM_STEP

    def body(*refs):
        w_ref, m_ref, v_ref = refs[:3]
        g_refs = refs[3:3 + ng]
        go_ref, d_ref, mo_ref, vo_ref = refs[3 + ng:]
        g = g_refs[0][...]
        for gr in g_refs[1:]:
            g = g + gr[...]
        m_new = ADAM_B1 * m_ref[...] + (1.0 - ADAM_B1) * g
        v_new = ADAM_B2 * v_ref[...] + (1.0 - ADAM_B2) * (g * g)
        go_ref[...] = g
        mo_ref[...] = m_new
        vo_ref[...] = v_new
        d_ref[...] = -ADAM_LR * ((m_new / c1) / (jnp.sqrt(v_new / c2) + ADAM_EPS) + ADAM_WD * w_ref[...])

    spec = pl.BlockSpec((tr, c), lambda i: (i, 0))
    return pl.pallas_call(
        body, name=name, grid=(r // tr,),
        in_specs=[spec] * (3 + ng), out_specs=[spec] * 4,
        out_shape=[jax.ShapeDtypeStruct((r, c), F32)] * 4,
        compiler_params=_cparams(("parallel",)),
    )(w, m, v, *g_parts)


def _adamw_halves(name, w, m, v, own, other, idx, axis):
    hr, hc = own.shape
    nb, (tr, tc), at = _tiling(hr, hc, 9)
    c1 = 1.0 - ADAM_B1 ** ADAM_STEP
    c2 = 1.0 - ADAM_B2 ** ADAM_STEP

    def body(idx_ref, w_ref, m_ref, v_ref, own_ref, oth_ref, go_ref, d_ref, mo_ref, vo_ref):
        g = jnp.where(pl.program_id(0) == idx_ref[0], own_ref[...], oth_ref[...])
        m_new = ADAM_B1 * m_ref[...] + (1.0 - ADAM_B1) * g
        v_new = ADAM_B2 * v_ref[...] + (1.0 - ADAM_B2) * (g * g)
        go_ref[...] = g
        mo_ref[...] = m_new
        vo_ref[...] = v_new
        d_ref[...] = -ADAM_LR * ((m_new / c1) / (jnp.sqrt(v_new / c2) + ADAM_EPS) + ADAM_WD * w_ref[...])

    per_half = (hc // tc) if axis else (hr // tr)
    if axis:
        full = pl.BlockSpec((tr, tc), lambda hh, i, s: (at(i)[0], hh * per_half + at(i)[1]))
    else:
        full = pl.BlockSpec((tr, tc), lambda hh, i, s: (hh * per_half + at(i)[0], at(i)[1]))
    part = pl.BlockSpec((tr, tc), lambda hh, i, s: at(i))
    grid_spec = pltpu.PrefetchScalarGridSpec(
        num_scalar_prefetch=1, grid=(2, nb), in_specs=[full, full, full, part, part], out_specs=[full] * 4)
    return pl.pallas_call(
        body, name=name, grid_spec=grid_spec, out_shape=[jax.ShapeDtypeStruct(w.shape, F32)] * 4,
        compiler_params=_cparams(("parallel", "parallel")),
    )(idx, w, m, v, own, other)


WEIGHTS = ['norm_w', 'w_in', 's5_a_re', 's5_a_im', 's5_b_re', 's5_b_im', 's5_c_re', 's5_c_im', 's5_d', 's5_log_step',
           's5_w_glu', 's5_b_glu', 'ssd_conv_w', 'ssd_conv_b', 'ssd_dt_bias', 'ssd_a_log', 'ssd_d', 'ssd_norm_w',
           'w_br_s5', 'w_br_ssd', 'w_out', 'ple_norm_w', 'w_ple_gate', 'w_ple_proj', 'final_norm_w']
SHARDED = {'w_in': ((IN_PROJ_DIM, 1024), 0), 's5_w_glu': ((512, 512), 0), 'ssd_conv_w': ((SSD_CONV, SSD_CONV_DIM), 1),
           'w_br_s5': ((512, 1024), 1), 'w_br_ssd': ((1536, 1024), 0), 'w_out': ((1024, 1024), 0),
           'w_ple_gate': ((1024, 1024), 0), 'w_ple_proj': ((256, 1024), 1)}
TRANSPOSED = ('w_in',)
SMALL = [n for n in WEIGHTS if n not in SHARDED]


def _shard_shape(name):
    (r, c), ax = SHARDED[name]
    return (r // N_CHIPS, c) if ax == 0 else (r, c // N_CHIPS)


def _half_axis(name):
    return 0 if (_shard_shape(name)[0] // 2) % 16 == 0 else 1


def _shard2d(name, a):
    r, c = _shard_shape(name)
    return a.reshape(c, r).T if name in TRANSPOSED else a.reshape(r, c)


def _unshard2d(name, a2, shape):
    return (a2.T if name in TRANSPOSED else a2).reshape(shape)


def _unslot(name, a4):
    (r, c), ax = SHARDED[name]
    if ax == 0:
        return a4.reshape(r, c)
    return jnp.transpose(a4, (1, 0, 2)).reshape(r, c)


def _slot(name, full):
    (r, c), ax = SHARDED[name]
    if ax == 0:
        return full.reshape(N_CHIPS, r // N_CHIPS, c)
    return jnp.transpose(full.reshape(r, N_CHIPS, c // N_CHIPS), (1, 0, 2))


GHP = ('s5_b_re', 's5_b_im')


def _view_shape(name):
    if name in ('s5_a_re', 's5_a_im'):
        return (S5_GROUPS, S5_STATE)
    if name in GHP + ('s5_c_re', 's5_c_im'):
        return (S5_GROUPS, S5_GROUP, S5_STATE)
    if name == 'ssd_conv_w':
        return (SSD_CONV, SSD_CONV_DIM // N_CHIPS)
    return (1, {'s5_log_step': S5_GROUPS, 'ssd_conv_b': SSD_CONV_DIM, 'ssd_norm_w': SSD_WIDTH, 's5_d': S5_WIDTH,
                's5_b_glu': S5_WIDTH, 'ssd_dt_bias': SSD_HEADS, 'ssd_a_log': SSD_HEADS, 'ssd_d': SSD_HEADS}.get(name, D_MODEL))


def _view(name, a):
    if name in GHP:
        return jnp.swapaxes(a.reshape(S5_GROUPS, S5_STATE, S5_GROUP), 1, 2)
    return a.reshape(_view_shape(name))


def _unview(name, a, shape):
    return (jnp.swapaxes(a, 1, 2) if name in GHP else a).reshape(shape)


def _adamw_small(ws, ms, vs, gs):
    n = len(ws)
    c1 = 1.0 - ADAM_B1 ** ADAM_STEP
    c2 = 1.0 - ADAM_B2 ** ADAM_STEP

    def body(*refs):
        w_r, m_r, v_r, g_r = (refs[k * n:(k + 1) * n] for k in range(4))
        d_o, m_o, v_o = (refs[k * n:(k + 1) * n] for k in range(4, 7))
        for i in range(n):
            g = g_r[i][...]
            m_new = ADAM_B1 * m_r[i][...] + (1.0 - ADAM_B1) * g
            v_new = ADAM_B2 * v_r[i][...] + (1.0 - ADAM_B2) * (g * g)
            m_o[i][...] = m_new
            v_o[i][...] = v_new
            d_o[i][...] = -ADAM_LR * ((m_new / c1) / (jnp.sqrt(v_new / c2) + ADAM_EPS) + ADAM_WD * w_r[i][...])

    return pl.pallas_call(
        body, name="adamw_small", out_shape=[jax.ShapeDtypeStruct(w.shape, F32) for w in ws] * 3,
        compiler_params=pltpu.CompilerParams(vmem_limit_bytes=VMEM_LIMIT),
    )(*ws, *ms, *vs, *gs)


def _pack_small(vals):
    flat = jnp.concatenate([v.reshape(-1).astype(F32) for v in vals])
    rows = -(-flat.shape[0] // (256 * 128)) * 256
    return jnp.pad(flat, (0, rows * 128 - flat.shape[0])).reshape(rows, 128)


def _unpack_small(packed, shapes):
    flat = packed.reshape(-1)
    out, off = [], 0
    for sh in shapes:
        n = math.prod(sh)
        out.append(flat[off:off + n].reshape(sh))
        off += n
    return out


def kernel(x, p, norm_w, w_in, s5_a_re, s5_a_im, s5_b_re, s5_b_im, s5_c_re, s5_c_im, s5_d, s5_log_step, s5_w_glu, s5_b_glu, ssd_conv_w, ssd_conv_b, ssd_dt_bias, ssd_a_log, ssd_d, ssd_norm_w, w_br_s5, w_br_ssd, w_out, ple_norm_w, w_ple_gate, w_ple_proj, final_norm_w, loss_target, m_norm_w, m_w_in, m_s5_a_re, m_s5_a_im, m_s5_b_re, m_s5_b_im, m_s5_c_re, m_s5_c_im, m_s5_d, m_s5_log_step, m_s5_w_glu, m_s5_b_glu, m_ssd_conv_w, m_ssd_conv_b, m_ssd_dt_bias, m_ssd_a_log, m_ssd_d, m_ssd_norm_w, m_w_br_s5, m_w_br_ssd, m_w_out, m_ple_norm_w, m_w_ple_gate, m_w_ple_proj, m_final_norm_w, v_norm_w, v_w_in, v_s5_a_re, v_s5_a_im, v_s5_b_re, v_s5_b_im, v_s5_c_re, v_s5_c_im, v_s5_d, v_s5_log_step, v_s5_w_glu, v_s5_b_glu, v_ssd_conv_w, v_ssd_conv_b, v_ssd_dt_bias, v_ssd_a_log, v_ssd_d, v_ssd_norm_w, v_w_br_s5, v_w_br_ssd, v_w_out, v_ple_norm_w, v_w_ple_gate, v_w_ple_proj, v_final_norm_w):
    args = locals()
    wl = {n: args[n] for n in WEIGHTS}
    ml = {n: args["m_" + n] for n in WEIGHTS}
    vl = {n: args["v_" + n] for n in WEIGHTS}
    big = [n for n in SHARDED if n != 'ssd_conv_w']
    chip = _chip_index(lax.axis_index("x"), lax.axis_index("y"))
    idx = jnp.stack([lax.axis_index("c"), chip]).astype(jnp.int32)

    axes = [_half_axis(n) for n in big]
    first = ['w_in']
    rest = [n for n in big if n not in first]
    bf_shards = {n: _shard2d(n, wl[n]).astype(BF16) for n in big}
    w_in_t, conv_w4 = _gather_chips([bf_shards[n] for n in first], [_half_axis(n) for n in first],
                                    [_shard2d('ssd_conv_w', wl['ssd_conv_w'])])
    full = {n: wl[n] for n in SMALL}
    full["w_in_t"] = w_in_t
    full['ssd_conv_w'] = _unslot('ssd_conv_w', conv_w4)
    send_sems, recv_sems, thru, lands, token = _gather_start([bf_shards[n] for n in rest], w_in_t)

    def fetch_rest(after):
        own, zones = _gather_wait(send_sems, recv_sems, thru, lands, after)
        return {n: _unslot(n, lax.dynamic_update_slice(z, o[None], (chip, 0, 0)))
                for n, o, z in zip(rest, own, zones, strict=True)}

    full["_late"] = (token, fetch_rest)

    in_flight = []

    def send_rest(g_now):
        in_flight.extend(_scatter_start("scatter_rest_start", [_slot(n, g_now[n]) for n in rest], [], g_now['s5_w_glu']))
        return in_flight[4]

    full["_early"] = send_rest

    packed = [n for n in SMALL if n != 'norm_w']
    first_axes = [_half_axis(n) for n in first]
    st = {}

    def send_first(g_now, loss_now):
        small_pack = _pack_small([loss_now] + [g_now[n + "_ghp" if n in GHP else n] for n in packed]
                                 + [g_now['ssd_conv_w']])
        swapped = _swap_halves([g_now["w_in_t"]], first_axes, small_pack)
        st["pair"] = [_pair_sum("pair_sum_w_in", g_now["w_in_t"], swapped[0], idx, first_axes[0])]
        small_chip = _sum_slots("sum_small_pair", swapped[-1])
        half_rows = small_chip.shape[0] // 2
        my_half = lax.dynamic_slice(small_chip, (lax.axis_index("c") * half_rows, 0), (half_rows, small_chip.shape[1]))
        st["flight"] = _scatter_start("scatter_first_start", [pb for pb, _ in st["pair"]], [my_half], small_chip)
        return st["flight"][4]

    full["_w_in_ready"] = send_first
    loss, grad_x, g = _local_step(x, p[0], loss_target, full)
    pair = st["pair"]
    s_sems, r_sems, thru, lands, tok = st["flight"]
    small_shapes = [(1, 1)] + [_view_shape(n) for n in packed] + [(SSD_CONV, SSD_CONV_DIM)]
    norm_all = _all_to_all("norm_w_all_to_all", g['norm_w'].reshape(8, 128))
    norm_g = _sum_slots("sum_norm_w", norm_all).reshape(1, D_MODEL)

    out_g, out_d, out_m, out_v = {}, {}, {}, {}
    own_slots, zones = _scatter_wait("scatter_rest_wait", len(rest), *in_flight[:4], norm_g)
    chip_sums = [_sum_chips("sum_chips_" + n, a, z, idx, tok) for n, a, z in zip(rest, own_slots, zones, strict=True)]
    sib_sums = _swap_sibling("swap_sibling_rest", chip_sums)
    for n, own, sib in zip(rest, chip_sums, sib_sums, strict=True):
        res = _adamw("adamw_" + n, _shard2d(n, wl[n]), _shard2d(n, ml[n]), _shard2d(n, vl[n]), [own, sib])
        out_g[n], out_d[n], out_m[n], out_v[n] = (_unshard2d(n, r, wl[n].shape) for r in res)

    sent, got = _scatter_wait("scatter_first_wait", len(first), s_sems, r_sems, thru, lands, out_d[rest[-1]])
    small_zone = lax.dynamic_update_slice(got[-1], sent[-1][None], (chip, 0, 0))
    small_half = _sum_slots("sum_small_chips", small_zone)
    halves = [_sum_parts("sum_chips_" + n, own, r) for n, (_, own), r in zip(first, pair, got[:-1], strict=True)]
    swapped2 = _swap_sibling("swap_sibling_first", halves + [small_half])
    other_halves = swapped2[:-1]
    c_is_0 = lax.axis_index("c") == 0
    small_sum = jnp.concatenate([jnp.where(c_is_0, small_half, swapped2[-1]),
                                 jnp.where(c_is_0, swapped2[-1], small_half)], axis=0)
    for n, own, oth, ax in zip(first, halves, other_halves, first_axes, strict=True):
        res = _adamw_halves("adamw_" + n, _shard2d(n, wl[n]), _shard2d(n, ml[n]), _shard2d(n, vl[n]), own, oth, idx, ax)
        out_g[n], out_d[n], out_m[n], out_v[n] = (_unshard2d(n, r, wl[n].shape) for r in res)
    sm = _unpack_small(small_sum, small_shapes)
    loss_total = sm[0].reshape(())
    conv_g = lax.dynamic_slice(sm[-1], (0, chip * (SSD_CONV_DIM // N_CHIPS)), (SSD_CONV, SSD_CONV_DIM // N_CHIPS))
    names = SMALL + ['ssd_conv_w']
    by_name = {**dict(zip(packed, sm[1:-1], strict=True)), 'norm_w': norm_g, 'ssd_conv_w': conv_g}
    grads = [by_name[n] for n in names]
    res = _adamw_small([_view(n, wl[n]) for n in names], [_view(n, ml[n]) for n in names],
                       [_view(n, vl[n]) for n in names], grads)
    for i, n in enumerate(names):
        out_g[n] = _unview(n, grads[i], wl[n].shape)
        out_d[n], out_m[n], out_v[n] = (_unview(n, res[k * len(names) + i], wl[n].shape) for k in range(3))

    return (loss_total, grad_x, *[out_g[n] for n in WEIGHTS], *[out_d[n] for n in WEIGHTS],
            *[out_m[n] for n in WEIGHTS], *[out_v[n] for n in WEIGHTS])
```

```python
import functools
import math

import jax
import jax.numpy as jnp
from jax import lax
from jax.experimental import pallas as pl
from jax.experimental.pallas import tpu as pltpu

F32 = jnp.float32
BF16 = jnp.bfloat16
MESH = pl.DeviceIdType.MESH

D_MODEL = 1024
PLE_DIM = 256
RMS_EPS = 1e-6
S5_WIDTH = 512
S5_GROUP = 16
S5_GROUPS = 32
S5_STATE = 64
S5_N = S5_GROUPS * S5_STATE
S5_LB = 512
S5_NJ = S5_N // S5_LB
S5_TB = 2048
S5_LOG_TB = 8
SSD_WIDTH = 1536
SSD_HEADDIM = 64
SSD_HEADS = 24
SSD_GROUPS = 4
SSD_HPG = 6
SSD_STATE = 128
SSD_CONV = 4
SSD_CHUNK = 128
SSD_BC = 512
SSD_CONV_DIM = 2560
GROUP_W = SSD_WIDTH // SSD_GROUPS
N_CHIPS = 4
N_DEV = 8

OFF_XBC, OFF_U5, OFF_Z5, OFF_DT, OFF_G5, OFF_GS, OFF_ZS = 0, 2560, 3072, 3584, 4096, 5120, 6144
DT_W = 512
PROJ_W = 7680
IN_PROJ_DIM = 7192

ADAM_LR, ADAM_B1, ADAM_B2, ADAM_EPS, ADAM_WD, ADAM_STEP = 0.001, 0.9, 0.999, 1e-08, 0.01, 10

VMEM_LIMIT = 56 * 1024 * 1024


ROW_BLOCK_BYTES = 8 * 1024 * 1024


def _row_tile(r, bytes_per_row):
    for t in (r, 4096, 2048, 1024, 512, 256, 128, 64, 32, 16, 8):
        if t <= r and r % t == 0 and t * bytes_per_row <= ROW_BLOCK_BYTES:
            return t
    return r


def _cparams(sem):
    return pltpu.CompilerParams(dimension_semantics=sem, vmem_limit_bytes=VMEM_LIMIT)


def _dg(a, b, ca, cb):
    return lax.dot_general(a.astype(BF16), b.astype(BF16), (((ca,), (cb,)), ((), ())), preferred_element_type=F32)


@jax.custom_vjp
def dot_nn(a, b):
    return _dg(a, b, 1, 0)


@jax.custom_vjp
def dot_nt(a, b):
    return _dg(a, b, 1, 1)


@jax.custom_vjp
def dot_tn(a, b):
    return _dg(a, b, 0, 0)


dot_nn.defvjp(lambda a, b: (_dg(a, b, 1, 0), (a, b)), lambda r, g: (_dg(g, r[1], 1, 1), _dg(r[0], g, 0, 0)))
dot_nt.defvjp(lambda a, b: (_dg(a, b, 1, 1), (a, b)), lambda r, g: (_dg(g, r[1], 1, 0), _dg(g, r[0], 0, 0)))
dot_tn.defvjp(lambda a, b: (_dg(a, b, 0, 0), (a, b)), lambda r, g: (_dg(r[1], g, 1, 1), _dg(r[0], g, 1, 0)))


MM_VMEM_BUDGET = 40 * 1024 * 1024


def _mm_tiles(m, n, k, sa, sb, so, tn_only=None):
    best, best_key = None, None
    for tm in (1024, 512, 256, 128, 64, 32, 16, 8):
        if m % tm:
            continue
        for tn in (2048, 1536, 1280, 1024, 768, 640, 512, 384, 256, 128):
            if n % tn or (tn_only is not None and tn not in tn_only):
                continue
            for tk in (k, 2048, 1536, 1280, 1024, 768, 512, 256, 128):
                if k % tk or tk > max(k, 128):
                    continue
                need = 2 * (tm * tk * sa + tk * tn * sb + tm * tn * so) + (tm * tn * 4 if tk < k else 0)
                if need > MM_VMEM_BUDGET:
                    continue
                key = (tm * tn * tk, tk)
                if best_key is None or key > best_key:
                    best, best_key = (tm, tn, tk), key
    assert best is not None, (m, n, k)
    return best


def _matmul(name, a, b, *, ta=False, tb=False, a_win=None, out_dtype=F32, epilogue=None, epi_rows=(), epi_pars=(),
            epi_outs=(), full_rows=False, epi_accs=(), epi_into=None, dep=None):
    a_off, a_w = a_win if a_win is not None else (0, a.shape[1])
    if ta:
        kdim, m = a.shape[0], a_w
    else:
        m, kdim = a.shape[0], a_w
    n = b.shape[0] if tb else b.shape[1]
    assert (b.shape[1] if tb else b.shape[0]) == kdim, (name, a.shape, b.shape)
    out_dtypes = list(epi_outs) if epilogue is not None else [out_dtype]
    so = sum(jnp.dtype(d).itemsize for d in out_dtypes) + sum(r.dtype.itemsize for r, _ in epi_rows)
    tn_ok = [n] if full_rows else [t for t in (1024, 512, 256, 128) if all(off % t == 0 for _, off in epi_rows)]
    tm, tn, tk = _mm_tiles(m, n, kdim, a.dtype.itemsize, b.dtype.itemsize, so, tn_ok if epilogue is not None else None)
    nk = kdim // tk
    n_er, n_ep, n_out = len(epi_rows), len(epi_pars), len(out_dtypes)
    if ta:
        assert a_off % tm == 0
        a_spec = pl.BlockSpec((tk, tm), lambda i, j, k: (k, i + a_off // tm))
    else:
        assert a_off % tk == 0
        a_spec = pl.BlockSpec((tm, tk), lambda i, j, k: (i, k + a_off // tk))
    if tb:
        b_spec = pl.BlockSpec((tn, tk), lambda i, j, k: (j, k))
    else:
        b_spec = pl.BlockSpec((tk, tn), lambda i, j, k: (k, j))
    ca, cb = (0 if ta else 1), (1 if tb else 0)

    n_acc = len(epi_accs)
    n_x = (1 if epi_into is not None else 0) + (1 if dep is not None else 0)
    assert not (n_acc or epi_into is not None) or full_rows

    def body(a_ref, b_ref, *refs):
        er, ep = refs[:n_er], refs[n_er:n_er + n_ep]
        first_out = n_er + n_ep + n_x
        o_refs = refs[first_out:first_out + n_out]
        s_refs = refs[first_out + n_out:first_out + n_out + n_acc]
        acc = refs[first_out + n_out + n_acc:]

        def finish(c):
            outs = [c] if epilogue is None else epilogue(c, [r[...] for r in er], [p[...] for p in ep])
            if n_acc:
                outs, sums = outs

                @pl.when(pl.program_id(0) == 0)
                def _():
                    for s_ref in s_refs:
                        s_ref[...] = jnp.zeros_like(s_ref)

                for s_ref, s in zip(s_refs, sums, strict=True):
                    s_ref[...] += jnp.broadcast_to(s, s_ref.shape)
            for o_ref, o in zip(o_refs, outs, strict=True):
                o_ref[...] = o.astype(o_ref.dtype)

        if nk == 1:
            finish(_dg(a_ref[...], b_ref[...], ca, cb))
            return
        (acc_ref,) = acc
        k = pl.program_id(2)

        @pl.when(k == 0)
        def _():
            acc_ref[...] = jnp.zeros_like(acc_ref)

        acc_ref[...] += _dg(a_ref[...], b_ref[...], ca, cb)

        @pl.when(k == nk - 1)
        def _():
            finish(acc_ref[...])

    in_specs = [a_spec, b_spec]
    in_specs += [pl.BlockSpec((tm, tn), functools.partial(lambda i, j, k, c: (i, j + c), c=off // tn)) for _, off in epi_rows]
    in_specs += [pl.BlockSpec((p.shape[0], tn), lambda i, j, k: (0, j)) for p in epi_pars]
    out_specs = [pl.BlockSpec((tm, tn), lambda i, j, k: (i, j)) for _ in out_dtypes]
    out_shape = [jax.ShapeDtypeStruct((m, n), d) for d in out_dtypes]
    extra, aliases = [], {}
    if epi_into is not None:
        buf, off, width = epi_into
        assert off % width == 0 and buf.dtype == out_dtypes[0]
        in_specs.append(pl.BlockSpec(memory_space=pl.ANY))
        out_specs[0] = pl.BlockSpec((tm, width), functools.partial(lambda i, j, k, c: (i, c), c=off // width))
        out_shape[0] = jax.ShapeDtypeStruct(buf.shape, buf.dtype)
        extra, aliases = [buf], {2 + n_er + n_ep: 0}
    if dep is not None:
        in_specs.append(pl.BlockSpec(memory_space=pl.ANY))
        extra = extra + [dep]
    out_specs += [pl.BlockSpec((r, w), lambda i, j, k: (0, 0)) for r, w in epi_accs]
    out_shape += [jax.ShapeDtypeStruct((r, w), F32) for r, w in epi_accs]
    res = pl.pallas_call(
        body, name=name, grid=(m // tm, n // tn, nk),
        in_specs=in_specs, out_specs=out_specs, out_shape=out_shape, input_output_aliases=aliases,
        scratch_shapes=[pltpu.VMEM((tm, tn), F32)] if nk > 1 else [],
        compiler_params=_cparams(("arbitrary",) * 3 if n_acc else ("parallel", "parallel", "arbitrary")),
    )(a, b, *[r for r, _ in epi_rows], *epi_pars, *extra)
    return res if epilogue is not None else res[0]


BAND = 128


def _rowwise(name, fn, n_rows, tr, row_ins, par_ins, row_outs, acc_outs=(), into=None):
    nr, npar, no, na = len(row_ins), len(par_ins), len(row_outs), len(acc_outs)
    in_specs = []
    for arr, off, w in row_ins:
        assert off % w == 0 and arr.shape[0] == n_rows, (name, arr.shape, off, w)
        in_specs.append(pl.BlockSpec((tr, w), functools.partial(lambda i, c: (i, c), c=off // w)))
    for arr, off, w in par_ins:
        assert off % w == 0
        in_specs.append(pl.BlockSpec((arr.shape[0], w), functools.partial(lambda i, c: (0, c), c=off // w)))
    out_specs = [pl.BlockSpec((tr, w), lambda i: (i, 0)) for w, _ in row_outs]
    out_specs += [pl.BlockSpec((r, w), lambda i: (0, 0)) for r, w in acc_outs]
    out_shape = [jax.ShapeDtypeStruct((n_rows, w), dt) for w, dt in row_outs]
    out_shape += [jax.ShapeDtypeStruct((r, w), F32) for r, w in acc_outs]
    extra, aliases = [], {}
    if into is not None:
        buf, off = into
        w0 = row_outs[0][0]
        assert off % w0 == 0 and buf.dtype == row_outs[0][1]
        in_specs.append(pl.BlockSpec(memory_space=pl.ANY))
        out_specs[0] = pl.BlockSpec((tr, w0), functools.partial(lambda i, c: (i, c), c=off // w0))
        out_shape[0] = jax.ShapeDtypeStruct(buf.shape, buf.dtype)
        extra, aliases = [buf], {nr + npar: 0}
    nx = len(extra)

    def body(*refs):
        rows = [r[...] for r in refs[:nr]]
        pars = [r[...] for r in refs[nr:nr + npar]]
        o_refs = refs[nr + npar + nx:nr + npar + nx + no]
        a_refs = refs[nr + npar + nx + no:]
        outs, accs = fn(rows, pars)
        for o_ref, o in zip(o_refs, outs, strict=True):
            o_ref[...] = o.astype(o_ref.dtype)
        if na:
            @pl.when(pl.program_id(0) == 0)
            def _():
                for a_ref in a_refs:
                    a_ref[...] = jnp.zeros_like(a_ref)

            for a_ref, a in zip(a_refs, accs, strict=True):
                a_ref[...] += jnp.broadcast_to(a, a_ref.shape)

    res = pl.pallas_call(
        body, name=name, grid=(n_rows // tr,),
        in_specs=in_specs, out_specs=out_specs, out_shape=out_shape, input_output_aliases=aliases,
        compiler_params=_cparams(("arbitrary",) if na else ("parallel",)),
    )(*[a for a, _, _ in row_ins], *[a for a, _, _ in par_ins], *extra)
    return res


def _rms(x, w):
    return x * lax.rsqrt(jnp.mean(x * x, axis=-1, keepdims=True) + RMS_EPS) * w


def _s5_out(yc, u, z, t, d, bg):
    ge = jax.nn.gelu(yc + d * u)
    return ge * jax.nn.sigmoid(t + bg) * jax.nn.silu(z)


def _merge(g5, gs, m5, ms):
    return jax.nn.sigmoid(g5) * m5 + jax.nn.sigmoid(gs) * ms


def _head_loss(h1, pgl, pp, fw, tgt):
    h2 = h1 + jax.nn.sigmoid(pgl) * pp
    err = _rms(h2, fw) - tgt
    per_row = 0.5 * jnp.mean(err * err, axis=-1, keepdims=True)
    return jnp.sum(per_row, axis=0, keepdims=True)


def _s5_disc(a_re, a_im, log_step, b_re2, b_im2, expand):
    step = jnp.exp(log_step)
    mag = jnp.exp(a_re * step)
    lb_re = mag * jnp.cos(a_im * step)
    lb_im = mag * jnp.sin(a_im * step)
    den = a_re * a_re + a_im * a_im
    n_re = lb_re - 1.0
    f_re = (n_re * a_re + lb_im * a_im) / den
    f_im = (lb_im * a_re - n_re * a_im) / den
    hi = lax.Precision.HIGHEST
    fr = jnp.dot(expand, f_re, precision=hi, preferred_element_type=F32)
    fi = jnp.dot(expand, f_im, precision=hi, preferred_element_type=F32)
    return lb_re, lb_im, fr * b_re2 - fi * b_im2, fr * b_im2 + fi * b_re2


def _s5_params_fwd(a_re, a_im, log_step, b_re2, b_im2, expand):
    gp = a_re.shape

    def body(ar, ai, ls, br, bi, ex, pr_ref, pi_ref, bbr_ref, bbi_ref):
        lr, li, bbr, bbi = _s5_disc(ar[...], ai[...], ls[...], br[...], bi[...], ex[...])
        bbr_ref[...] = bbr
        bbi_ref[...] = bbi
        qr, qi = lr, li
        for k in range(S5_LOG_TB):
            pr_ref[k] = qr
            pi_ref[k] = qi
            qr, qi = qr * lr - qi * li, qr * li + qi * lr

    return pl.pallas_call(
        body, name="s5_params_fwd",
        out_shape=(jax.ShapeDtypeStruct((S5_LOG_TB,) + gp, F32), jax.ShapeDtypeStruct((S5_LOG_TB,) + gp, F32),
                   jax.ShapeDtypeStruct(b_re2.shape, F32), jax.ShapeDtypeStruct(b_re2.shape, F32)),
    )(a_re, a_im, log_step, b_re2, b_im2, expand)


def _s5_params_bwd(a_re, a_im, log_step, b_re2, b_im2, expand, d_lr, d_li, d_bbr, d_bbi):
    def body(ar, ai, ls, br, bi, ex, glr, gli, gbr, gbi, dar, dai, dls, dbr, dbi):
        _, vjp = jax.vjp(lambda *p: _s5_disc(*p, ex[...]), ar[...], ai[...], ls[...], br[...], bi[...])
        g = vjp((glr[...], gli[...], gbr[...], gbi[...]))
        for ref, val in zip((dar, dai, dls, dbr, dbi), g, strict=True):
            ref[...] = val

    return pl.pallas_call(
        body, name="s5_params_bwd",
        out_shape=tuple(jax.ShapeDtypeStruct(v.shape, F32) for v in (a_re, a_im, log_step, b_re2, b_im2)),
    )(a_re, a_im, log_step, b_re2, b_im2, expand, d_lr, d_li, d_bbr, d_bbi)


def _scan_block(x_ref, out_ref, lp, edge_pow, cr, ci, reverse, each=None):
    n_g = x_ref.shape[0] // 8
    sub = lax.broadcasted_iota(jnp.int32, (8, S5_LB), 0)
    steps = []
    for sh in (1, 2, 4):
        keep = (sub < 8 - sh) if reverse else (sub >= sh)
        steps.append((8 - sh if reverse else sh, jnp.where(keep, lp[sh - 1:sh, :S5_LB], 0.0),
                      jnp.where(keep, lp[sh - 1:sh, S5_LB:], 0.0)))
    e_r, e_i = edge_pow[:, :S5_LB], edge_pow[:, S5_LB:]
    for r in (range(n_g - 1, -1, -1) if reverse else range(n_g)):
        rows = slice(8 * r, 8 * r + 8)
        xr, xi = x_ref[rows, :S5_LB], x_ref[rows, S5_LB:]
        for by, a_r, a_i in steps:
            pr, pi = pltpu.roll(xr, by, 0), pltpu.roll(xi, by, 0)
            xr, xi = xr + a_r * pr - a_i * pi, xi + a_r * pi + a_i * pr
        xr = xr + e_r * cr - e_i * ci
        xi = xi + e_r * ci + e_i * cr
        out_ref[rows, :S5_LB] = xr
        out_ref[rows, S5_LB:] = xi
        if each is not None:
            each(r, xr, xi)
        cr, ci = (xr[0:1, :], xi[0:1, :]) if reverse else (xr[7:8, :], xi[7:8, :])
    return cr, ci


def _s5_fwd(proj, bb_band, c_band, lam_pow, s5_d, n_seq, seq_len):
    n_t = seq_len // S5_TB
    blk = 2 * S5_LB
    n_rows = n_seq * seq_len
    u_blk0 = OFF_U5 // BAND

    def body(u_ref, bb_ref, cb_ref, lp_ref, d_ref, s_ref, y_ref, ge_ref, cr, ci, buf):
        @pl.when(pl.program_id(2) == 0)
        def _():
            cr[...] = jnp.zeros_like(cr)
            ci[...] = jnp.zeros_like(ci)

        u = u_ref[...]
        buf[...] = _dg(u, bb_ref[...], 1, 0)
        lp = lp_ref[...]
        cr[...], ci[...] = _scan_block(buf, buf, lp, lp, cr[...], ci[...], False)
        s = buf[...].astype(s_ref.dtype)
        s_ref[...] = s
        y = _dg(s, cb_ref[...], 1, 1)
        y_ref[...] = y
        ge_ref[...] = jax.nn.gelu(y + d_ref[...] * u).astype(ge_ref.dtype)

    def rows(j, b, t):
        return (b * n_t + t, j)

    return pl.pallas_call(
        body, name="s5_fwd", grid=(S5_NJ, n_seq, n_t),
        in_specs=[pl.BlockSpec((S5_TB, BAND), lambda j, b, t: (b * n_t + t, u_blk0 + j)),
                  pl.BlockSpec((BAND, blk), lambda j, b, t: (j, 0)), pl.BlockSpec((BAND, blk), lambda j, b, t: (j, 0)),
                  pl.BlockSpec((S5_LOG_TB, blk), lambda j, b, t: (0, j)), pl.BlockSpec((1, BAND), lambda j, b, t: (0, j))],
        out_specs=[pl.BlockSpec((S5_TB, blk), rows), pl.BlockSpec((S5_TB, BAND), rows), pl.BlockSpec((S5_TB, BAND), rows)],
        out_shape=[jax.ShapeDtypeStruct((n_rows, S5_NJ * blk), BF16), jax.ShapeDtypeStruct((n_rows, S5_WIDTH), F32),
                   jax.ShapeDtypeStruct((n_rows, S5_WIDTH), BF16)],
        scratch_shapes=[pltpu.VMEM((1, S5_LB), F32), pltpu.VMEM((1, S5_LB), F32), pltpu.VMEM((S5_TB, blk), F32)],
        compiler_params=_cparams(("parallel", "parallel", "arbitrary")),
    )(proj, bb_band, c_band, lam_pow, s5_d)


def _s5_bwd(dyc, s, proj, du5_a, bb_band, c_band, lam_pow_conj, dproj, n_seq, seq_len):
    n_t = seq_len // S5_TB
    blk = 2 * S5_LB
    halo_rows = 16
    halo_per_blk = S5_TB // halo_rows
    u_blk0 = OFF_U5 // BAND

    def rows(j, b, t):
        return (b * n_t + (n_t - 1 - t), j)

    def u_rows(j, b, t):
        return (b * n_t + (n_t - 1 - t), u_blk0 + j)

    def halo(j, b, t):
        return (jnp.maximum((b * n_t + (n_t - 1 - t)) * halo_per_blk - 1, 0), j)

    def body(dy_ref, sq_ref, hq_ref, u_ref, dua_ref, bb_ref, cb_ref, lp_ref, _, du_ref, dl_ref, dbb_ref, dcb_ref,
             cr, ci, buf, s_ref):
        b, t = pl.program_id(1), pl.program_id(2)
        sq = sq_ref[...]
        s_ref[...] = sq.astype(F32)
        h_last = hq_ref[...].astype(F32)[halo_rows - 1:halo_rows, :]
        dy = dy_ref[...]
        buf[...] = _dg(dy, cb_ref[...], 1, 0)

        @pl.when(t == 0)
        def _():
            cr[...] = jnp.zeros_like(cr)
            ci[...] = jnp.zeros_like(ci)

        @pl.when((b == 0) & (t == 0))
        def _():
            dl_ref[...] = jnp.zeros_like(dl_ref)
            dbb_ref[...] = jnp.zeros_like(dbb_ref)
            dcb_ref[...] = jnp.zeros_like(dcb_ref)

        dcb_ref[...] += _dg(dy, sq, 0, 0)

        first_blk = t == n_t - 1
        sub = lax.broadcasted_iota(jnp.int32, (8, S5_LB), 0)
        acc = [jnp.zeros((8, S5_LB), F32), jnp.zeros((8, S5_LB), F32)]

        def each(r, gr, gi):
            rows = slice(8 * r, 8 * r + 8)
            if r == 0:
                before_r = jnp.where(first_blk, 0.0, h_last[:, :S5_LB])
                before_i = jnp.where(first_blk, 0.0, h_last[:, S5_LB:])
            else:
                before_r, before_i = s_ref[8 * r - 1:8 * r, :S5_LB], s_ref[8 * r - 1:8 * r, S5_LB:]
            sp_r = jnp.where(sub == 0, before_r, pltpu.roll(s_ref[rows, :S5_LB], 1, 0))
            sp_i = jnp.where(sub == 0, before_i, pltpu.roll(s_ref[rows, S5_LB:], 1, 0))
            acc[0] = acc[0] + gr * sp_r + gi * sp_i
            acc[1] = acc[1] + gi * sp_r - gr * sp_i

        lp = lp_ref[...]
        edge_pow = jnp.concatenate([lp[7 - i:8 - i, :] for i in range(8)], axis=0)
        cr[...], ci[...] = _scan_block(buf, buf, lp, edge_pow, cr[...], ci[...], True, each)
        g = buf[...].astype(BF16)
        du_ref[...] = (dua_ref[...] + _dg(g, bb_ref[...], 1, 1)).astype(du_ref.dtype)
        dbb_ref[...] += _dg(u_ref[...], g, 0, 0)
        dl_ref[:, :S5_LB] += jnp.sum(acc[0], axis=0, keepdims=True)
        dl_ref[:, S5_LB:] += jnp.sum(acc[1], axis=0, keepdims=True)

    band = pl.BlockSpec((BAND, blk), lambda j, b, t: (j, 0))
    return pl.pallas_call(
        body, name="s5_bwd", grid=(S5_NJ, n_seq, n_t),
        in_specs=[pl.BlockSpec((S5_TB, BAND), rows), pl.BlockSpec((S5_TB, blk), rows),
                  pl.BlockSpec((halo_rows, blk), halo), pl.BlockSpec((S5_TB, BAND), u_rows),
                  pl.BlockSpec((S5_TB, BAND), rows), band, band,
                  pl.BlockSpec((S5_LOG_TB, blk), lambda j, b, t: (0, j)), pl.BlockSpec(memory_space=pl.ANY)],
        out_specs=[pl.BlockSpec((S5_TB, BAND), u_rows), pl.BlockSpec((1, blk), lambda j, b, t: (0, j)), band, band],
        out_shape=[jax.ShapeDtypeStruct(dproj.shape, dproj.dtype), jax.ShapeDtypeStruct((1, S5_NJ * blk), F32),
                   jax.ShapeDtypeStruct(bb_band.shape, F32), jax.ShapeDtypeStruct(c_band.shape, F32)],
        input_output_aliases={8: 0},
        scratch_shapes=[pltpu.VMEM((1, S5_LB), F32), pltpu.VMEM((1, S5_LB), F32), pltpu.VMEM((S5_TB, blk), F32),
                        pltpu.VMEM((S5_TB, blk), F32)],
        compiler_params=_cparams(("parallel", "arbitrary", "arbitrary")),
    )(dyc, s, s, proj, du5_a, bb_band, c_band, lam_pow_conj, dproj)


CONV_TR = 1024
CONV_CW = 512


def _shift_down(x, halo, k):
    if k == 0:
        return x
    row8 = lax.broadcasted_iota(jnp.int32, halo.shape, 0)
    rolled = pltpu.roll(x, k, 0)
    top = jnp.where(row8 < k, pltpu.roll(halo, k, 0), rolled[:8])
    if x.shape[0] == 8:
        return top
    return jnp.concatenate([top, rolled[8:]], axis=0)


def _shift_up(x, halo, k):
    if k == 0:
        return x
    n = x.shape[0]
    row8 = lax.broadcasted_iota(jnp.int32, halo.shape, 0)
    rolled = pltpu.roll(x, n - k, 0)
    bot = jnp.where(row8 >= 8 - k, pltpu.roll(halo, 8 - k, 0), rolled[n - 8:])
    if n == 8:
        return bot
    return jnp.concatenate([rolled[:n - 8], bot], axis=0)


def _conv_pre(x, halo, w, b):
    acc = b + w[SSD_CONV - 1:SSD_CONV, :] * x
    for k in range(SSD_CONV - 1):
        acc = acc + w[k:k + 1, :] * _shift_down(x, halo, SSD_CONV - 1 - k)
    return acc


def _conv_specs(seq_len, col_off):
    lt = seq_len // CONV_TR
    cb = col_off // CONV_CW
    cur = pl.BlockSpec((CONV_TR, CONV_CW), lambda j, i: (i, j + cb))
    prev = pl.BlockSpec((8, CONV_CW), lambda j, i: (jnp.maximum(i * (CONV_TR // 8) - 1, 0), j + cb))
    return lt, cur, prev


def _conv_fwd(proj, conv_w, conv_b, n_rows, seq_len):
    lt, cur, prev = _conv_specs(seq_len, OFF_XBC)

    def body(x_ref, h_ref, w_ref, b_ref, o_ref):
        halo = jnp.where(pl.program_id(1) % lt == 0, 0.0, h_ref[...])
        o_ref[...] = jax.nn.silu(_conv_pre(x_ref[...], halo, w_ref[...], b_ref[...]))

    return pl.pallas_call(
        body, name="ssd_conv_fwd", grid=(SSD_CONV_DIM // CONV_CW, n_rows // CONV_TR),
        in_specs=[cur, prev, pl.BlockSpec((SSD_CONV, CONV_CW), lambda j, i: (0, j)),
                  pl.BlockSpec((1, CONV_CW), lambda j, i: (0, j))],
        out_specs=pl.BlockSpec((CONV_TR, CONV_CW), lambda j, i: (i, j)),
        out_shape=jax.ShapeDtypeStruct((n_rows, SSD_CONV_DIM), F32),
        compiler_params=_cparams(("parallel", "parallel")),
    )(proj, proj, conv_w, conv_b)


def _conv_bwd(name, proj, d_act, conv_w, conv_b, n_rows, seq_len, col_off, dproj):
    width = d_act.shape[1]
    lt, cur, prev = _conv_specs(seq_len, OFF_XBC + col_off)
    n_blk = n_rows // CONV_TR
    cb = (OFF_XBC + col_off) // CONV_CW
    pb = col_off // CONV_CW
    nxt = pl.BlockSpec((8, CONV_CW), lambda j, i: (jnp.minimum((i + 1) * (CONV_TR // 8), n_rows // 8 - 1), j + cb))
    d_cur = pl.BlockSpec((CONV_TR, CONV_CW), lambda j, i: (i, j))
    d_nxt = pl.BlockSpec((8, CONV_CW), lambda j, i: (jnp.minimum((i + 1) * (CONV_TR // 8), n_rows // 8 - 1), j))

    def dsilu(pre):
        sg = jax.nn.sigmoid(pre)
        return sg * (1.0 + pre * (1.0 - sg))

    def body(x_ref, hp_ref, hn_ref, d_ref, dn_ref, w_ref, b_ref, _, dx_ref, dw_ref, db_ref):
        i = pl.program_id(1)
        x, w, b = x_ref[...], w_ref[...], b_ref[...]
        halo_p = jnp.where(i % lt == 0, 0.0, hp_ref[...])
        at_end = i % lt == lt - 1
        dpre = d_ref[...] * dsilu(_conv_pre(x, halo_p, w, b))
        pre_n = _conv_pre(hn_ref[...], x[CONV_TR - 8:, :], w, b)
        dpre_n = jnp.where(at_end, 0.0, dn_ref[...] * dsilu(pre_n))
        dx = w[SSD_CONV - 1:SSD_CONV, :] * dpre
        for k in range(SSD_CONV - 1):
            dx = dx + w[k:k + 1, :] * _shift_up(dpre, dpre_n, SSD_CONV - 1 - k)
        dx_ref[...] = dx.astype(dx_ref.dtype)

        @pl.when(i == 0)
        def _():
            dw_ref[...] = jnp.zeros_like(dw_ref)
            db_ref[...] = jnp.zeros_like(db_ref)

        for k in range(SSD_CONV):
            xs = _shift_down(x, halo_p, SSD_CONV - 1 - k)
            dw_ref[k:k + 1, :] += jnp.sum(dpre * xs, axis=0, keepdims=True)
        db_ref[...] += jnp.sum(dpre, axis=0, keepdims=True)

    return pl.pallas_call(
        body, name=name, grid=(width // CONV_CW, n_blk),
        in_specs=[cur, prev, nxt, d_cur, d_nxt,
                  pl.BlockSpec((SSD_CONV, CONV_CW), lambda j, i: (0, j + pb)),
                  pl.BlockSpec((1, CONV_CW), lambda j, i: (0, j + pb)), pl.BlockSpec(memory_space=pl.ANY)],
        out_specs=[pl.BlockSpec((CONV_TR, CONV_CW), lambda j, i: (i, j + cb)),
                   pl.BlockSpec((SSD_CONV, CONV_CW), lambda j, i: (0, j)),
                   pl.BlockSpec((1, CONV_CW), lambda j, i: (0, j))],
        out_shape=[jax.ShapeDtypeStruct(dproj.shape, dproj.dtype),
                   jax.ShapeDtypeStruct((SSD_CONV, width), F32), jax.ShapeDtypeStruct((1, width), F32)],
        input_output_aliases={7: 0},
        compiler_params=_cparams(("parallel", "arbitrary")),
    )(proj, proj, proj, d_act, d_act, conv_w, conv_b, dproj)


def _split3(x):
    hi = x.astype(BF16)
    r = x - hi.astype(F32)
    mid = r.astype(BF16)
    return hi, mid, (r - mid.astype(F32)).astype(BF16)


def _sel_dot(a, b, a_is_sel):
    dn = (((1,), (0,)), ((), ()))
    if a_is_sel:
        return sum(lax.dot_general(a, t, dn, preferred_element_type=F32) for t in _split3(b))
    return sum(lax.dot_general(t, b, dn, preferred_element_type=F32) for t in _split3(a))


@jax.custom_vjp
def sel_left(sel, sel_t, x):
    return _sel_dot(sel, x, True)


@jax.custom_vjp
def sel_right(x, sel, sel_t):
    return _sel_dot(x, sel, False)


sel_left.defvjp(lambda s, st, x: (_sel_dot(s, x, True), (s, st)),
                lambda r, g: (jnp.zeros_like(r[0]), jnp.zeros_like(r[1]), _sel_dot(r[1], g, True)))
sel_right.defvjp(lambda x, s, st: (_sel_dot(x, s, False), (s, st)),
                 lambda r, g: (_sel_dot(g, r[1], False), jnp.zeros_like(r[0]), jnp.zeros_like(r[1])))


def _ssd_chunk(xs, bm, cm, dtr, st, dtb, alog, dsk, k):
    dt = jax.nn.softplus(dtr + dtb)
    acum = sel_left(k["tri"], k["tri_t"], dt * (-jnp.exp(alog)))
    dt_e = sel_right(dt, k["spread"], k["spread_t"])
    ac_e = sel_right(acum, k["spread"], k["spread_t"])
    al_e = ac_e[SSD_CHUNK - 1:SSD_CHUNK, :]
    dsk_e = sel_right(jnp.broadcast_to(dsk, (8, 128)), k["spread"], k["spread_t"])[0:1, :]
    xdt = xs * dt_e
    acum_t = acum.T
    scores = dot_nt(cm, bm)
    y = dot_nn(cm, st) * jnp.exp(ac_e) + xs * dsk_e
    for j in range(SSD_HPG):
        lmat = jnp.exp(jnp.where(k["causal"], acum[:, j:j + 1] - acum_t[j:j + 1, :], -jnp.inf))
        y = y + dot_nn(scores * lmat, jnp.where(k["head"] == j, xdt, 0.0))
    new = st * jnp.exp(al_e) + dot_tn(bm, xdt * jnp.exp(al_e - ac_e))
    return y, new


def _ssd_consts():
    r = lax.broadcasted_iota(jnp.int32, (SSD_CHUNK, SSD_CHUNK), 0)
    c = lax.broadcasted_iota(jnp.int32, (SSD_CHUNK, SSD_CHUNK), 1)
    hd = jnp.int32(SSD_HEADDIM)
    sr = lax.broadcasted_iota(jnp.int32, (128, GROUP_W), 0)
    sc = lax.div(lax.broadcasted_iota(jnp.int32, (128, GROUP_W), 1), hd)
    tr = lax.div(lax.broadcasted_iota(jnp.int32, (GROUP_W, 128), 0), hd)
    tc = lax.broadcasted_iota(jnp.int32, (GROUP_W, 128), 1)
    return {"tri": (r >= c).astype(BF16), "tri_t": (c >= r).astype(BF16), "causal": r >= c,
            "spread": (sr == sc).astype(BF16), "spread_t": (tr == tc).astype(BF16),
            "head": lax.div(lax.broadcasted_iota(jnp.int32, (SSD_CHUNK, GROUP_W), 1), hd)}


def _ssd_chunk_gated(xs, bm, cm, dtr, st, dtb, alog, dsk, z, nw, k):
    y, new = _ssd_chunk(xs, bm, cm, dtr, st, dtb, alog, dsk, k)
    yg = y * jax.nn.silu(z)
    return yg * lax.rsqrt(jnp.mean(yg * yg, axis=-1, keepdims=True) + RMS_EPS) * nw, new


def _ssd_fwd(xbc_act, proj, dtb, alog, dsk, norm_w, n_seq, seq_len):
    n_c = seq_len // SSD_CHUNK
    gps = 2
    xbc3 = xbc_act.reshape(n_seq, seq_len, SSD_CONV_DIM)
    proj3 = proj.reshape(n_seq, seq_len, PROJ_W)

    def rows(w, first=0):
        return pl.BlockSpec((n_seq, SSD_CHUNK, gps * w), lambda g, c: (0, c, first + g))

    def par(w):
        return pl.BlockSpec((1, gps * w), lambda g, c: (0, g))

    def body(xs_ref, bm_ref, cm_ref, dt_ref, dtb_ref, al_ref, dk_ref, z_ref, nw_ref, y_ref, st_ref, state):
        @pl.when(pl.program_id(1) == 0)
        def _():
            state[...] = jnp.zeros_like(state)

        consts = _ssd_consts()
        for i in range(gps):
            wide, narrow = slice(i * GROUP_W, (i + 1) * GROUP_W), slice(i * 128, (i + 1) * 128)
            for b in range(n_seq):
                prev = state[b, i]
                st_ref[b, 0, i] = prev
                y, new = _ssd_chunk_gated(xs_ref[b, :, wide], bm_ref[b, :, narrow], cm_ref[b, :, narrow],
                                          dt_ref[b, :, narrow], prev, dtb_ref[:, narrow], al_ref[:, narrow],
                                          dk_ref[:, narrow], z_ref[b, :, wide], nw_ref[:, wide], consts)
                y_ref[b, :, wide] = y.astype(y_ref.dtype)
                state[b, i] = new

    y3, states = pl.pallas_call(
        body, name="ssd_fwd", grid=(SSD_GROUPS // gps, n_c),
        in_specs=[rows(GROUP_W), rows(SSD_STATE, SSD_WIDTH // (gps * SSD_STATE)),
                  rows(SSD_STATE, (SSD_WIDTH + SSD_BC) // (gps * SSD_STATE)), rows(128, OFF_DT // (gps * 128)),
                  par(128), par(128), par(128), rows(GROUP_W, OFF_ZS // (gps * GROUP_W)), par(GROUP_W)],
        out_specs=[rows(GROUP_W),
                   pl.BlockSpec((n_seq, 1, gps, SSD_STATE, GROUP_W), lambda g, c: (0, c, g, 0, 0))],
        out_shape=[jax.ShapeDtypeStruct((n_seq, seq_len, SSD_WIDTH), BF16),
                   jax.ShapeDtypeStruct((n_seq, n_c, SSD_GROUPS, SSD_STATE, GROUP_W), F32)],
        scratch_shapes=[pltpu.VMEM((n_seq, gps, SSD_STATE, GROUP_W), F32)],
        compiler_params=_cparams(("parallel", "arbitrary")),
    )(xbc3, xbc3, xbc3, proj3, dtb, alog, dsk, proj3, norm_w)
    return y3.reshape(n_seq * seq_len, SSD_WIDTH), states


def _ssd_bwd(xbc_act, proj, states, dy, dtb, alog, dsk, norm_w, n_seq, seq_len, dproj):
    n_c = seq_len // SSD_CHUNK
    n_rows = n_seq * seq_len
    gps = 2
    xbc3 = xbc_act.reshape(n_seq, seq_len, SSD_CONV_DIM)
    proj3 = proj.reshape(n_seq, seq_len, PROJ_W)

    def rows(w, first=0):
        return pl.BlockSpec((n_seq, SSD_CHUNK, gps * w), lambda g, c: (0, n_c - 1 - c, first + g))

    def par(w):
        return pl.BlockSpec((1, gps * w), lambda g, c: (0, g))

    def body(xs_ref, bm_ref, cm_ref, dt_ref, st_ref, dy_ref, dtb_ref, al_ref, dk_ref, z_ref, nw_ref, _,
             dxs_ref, dbm_ref, dcm_ref, dz_ref, ddt_ref, ddtb_ref, dal_ref, ddk_ref, dnw_ref, dstate):
        @pl.when(pl.program_id(1) == 0)
        def _():
            dstate[...] = jnp.zeros_like(dstate)
            for ref in (ddtb_ref, dal_ref, ddk_ref, dnw_ref):
                ref[...] = jnp.zeros_like(ref)

        consts = _ssd_consts()
        for i in range(gps):
            wide, narrow = slice(i * GROUP_W, (i + 1) * GROUP_W), slice(i * 128, (i + 1) * 128)
            for b in range(n_seq):
                _, vjp = jax.vjp(
                    lambda *a: _ssd_chunk_gated(*a, consts),
                    xs_ref[b, :, wide], bm_ref[b, :, narrow], cm_ref[b, :, narrow], dt_ref[b, :, narrow],
                    st_ref[b, 0, i], dtb_ref[:, narrow], al_ref[:, narrow], dk_ref[:, narrow], z_ref[b, :, wide],
                    nw_ref[:, wide])
                dxs, dbm, dcm, ddtr, dprev, ddtb, dal, ddk, dz, dnw = vjp((dy_ref[b, :, wide], dstate[b, i]))
                dxs_ref[b, :, wide] = dxs
                dbm_ref[b, :, narrow] = dbm
                dcm_ref[b, :, narrow] = dcm
                dz_ref[b, :, wide] = dz.astype(dz_ref.dtype)
                ddt_ref[b, :, narrow] = ddtr.astype(ddt_ref.dtype)
                ddtb_ref[:, narrow] += ddtb
                dal_ref[:, narrow] += dal
                ddk_ref[:, narrow] += ddk
                dnw_ref[:, wide] += dnw
                dstate[b, i] = dprev

    res = pl.pallas_call(
        body, name="ssd_bwd", grid=(SSD_GROUPS // gps, n_c),
        in_specs=[rows(GROUP_W), rows(SSD_STATE, SSD_WIDTH // (gps * SSD_STATE)),
                  rows(SSD_STATE, (SSD_WIDTH + SSD_BC) // (gps * SSD_STATE)), rows(128, OFF_DT // (gps * 128)),
                  pl.BlockSpec((n_seq, 1, gps, SSD_STATE, GROUP_W), lambda g, c: (0, n_c - 1 - c, g, 0, 0)),
                  rows(GROUP_W), par(128), par(128), par(128), rows(GROUP_W, OFF_ZS // (gps * GROUP_W)), par(GROUP_W),
                  pl.BlockSpec(memory_space=pl.ANY)],
        out_specs=[rows(GROUP_W), rows(SSD_STATE), rows(SSD_STATE), rows(GROUP_W, OFF_ZS // (gps * GROUP_W)), rows(128),
                   par(128), par(128), par(128), par(GROUP_W)],
        out_shape=[jax.ShapeDtypeStruct((n_seq, seq_len, SSD_WIDTH), F32), jax.ShapeDtypeStruct((n_seq, seq_len, SSD_BC), F32),
                   jax.ShapeDtypeStruct((n_seq, seq_len, SSD_BC), F32),
                   jax.ShapeDtypeStruct((n_seq, seq_len, PROJ_W), dproj.dtype),
                   jax.ShapeDtypeStruct((n_seq, seq_len, DT_W), BF16),
                   jax.ShapeDtypeStruct((1, 512), F32), jax.ShapeDtypeStruct((1, 512), F32),
                   jax.ShapeDtypeStruct((1, 512), F32), jax.ShapeDtypeStruct((1, SSD_WIDTH), F32)],
        input_output_aliases={11: 3},
        scratch_shapes=[pltpu.VMEM((n_seq, gps, SSD_STATE, GROUP_W), F32)],
        compiler_params=_cparams(("parallel", "arbitrary")),
    )(xbc3, xbc3, xbc3, proj3, states, dy.reshape(n_seq, seq_len, SSD_WIDTH), dtb, alog, dsk, proj3, norm_w,
      dproj.reshape(n_seq, seq_len, PROJ_W))
    flat = [r.reshape(n_rows, r.shape[-1]) for r in res[:5]]
    return (*flat, *res[5:])


def _pad_heads(v):
    return jnp.pad(v.reshape(SSD_GROUPS, SSD_HPG), ((0, 0), (0, 128 - SSD_HPG))).reshape(1, SSD_GROUPS * 128)


def _unpad_heads(v):
    return v.reshape(SSD_GROUPS, 128)[:, :SSD_HPG].reshape(1, SSD_HEADS)


def _state_cols(v):
    re, im = v
    lead = re.shape[:-1]
    re = re.reshape(lead + (S5_NJ, 1, S5_LB))
    im = im.reshape(lead + (S5_NJ, 1, S5_LB))
    return jnp.concatenate([re, im], axis=-2).reshape(lead + (2 * S5_N,))


def _state_uncols(v):
    lead = v.shape[:-1]
    v = v.reshape(lead + (S5_NJ, 2, S5_LB))
    return v[..., 0, :].reshape(lead + (S5_N,)), v[..., 1, :].reshape(lead + (S5_N,))


GROUPS_PER_BAND = BAND // S5_GROUP


def _band(w2_re, w2_im):
    gh = S5_GROUPS * S5_GROUP
    rg = ((jnp.arange(gh) // S5_GROUP) % GROUPS_PER_BAND)[:, None, None]
    cg = jnp.arange(GROUPS_PER_BAND)[None, :, None]
    parts = [jnp.where(rg == cg, v[:, None, :], 0.0).reshape(gh, S5_LB) for v in (w2_re, w2_im)]
    return jnp.concatenate(parts, axis=1)


def _band_take(wb):
    gh = S5_GROUPS * S5_GROUP
    w4 = wb.reshape(gh, 2, GROUPS_PER_BAND, S5_STATE)
    sel = w4[jnp.arange(gh), :, (jnp.arange(gh) // S5_GROUP) % GROUPS_PER_BAND, :]
    return sel[:, 0, :], sel[:, 1, :]


W_IN_SHARD = IN_PROJ_DIM // N_CHIPS
W_IN_SEGS = ((0, 512, OFF_U5), (512, 1024, OFF_Z5), (1024, 2560, OFF_ZS), (2560, 5120, OFF_XBC), (5144, 7192, OFF_G5))
DT_ROWS = (5120, 5144)


def _w_in_pieces():
    runs = []
    segs = list(W_IN_SEGS) + [(DT_ROWS[0] + SSD_HPG * g, DT_ROWS[0] + SSD_HPG * (g + 1), OFF_DT + 128 * g)
                              for g in range(SSD_GROUPS)]
    for lo, hi, off in segs:
        for j in range(N_CHIPS):
            s, e = max(lo, j * W_IN_SHARD), min(hi, (j + 1) * W_IN_SHARD)
            if s < e:
                runs.append((j, s - j * W_IN_SHARD, off + s - lo, e - s))
    return runs


RELAYOUT_LANES = 256


def _pad_w_in_t(a4):
    runs = _w_in_pieces()

    def body(a_ref, o_ref):
        o_ref[pl.ds(OFF_DT, DT_W), :] = jnp.zeros((DT_W, RELAYOUT_LANES), o_ref.dtype)
        for j, src, dst, n in runs:
            o_ref[pl.ds(dst, n), :] = a_ref[j, pl.ds(src, n), :]

    return pl.pallas_call(
        body, name="w_in_to_padded", grid=(D_MODEL // RELAYOUT_LANES,),
        in_specs=[pl.BlockSpec((N_CHIPS, W_IN_SHARD, RELAYOUT_LANES), lambda i: (0, 0, i))],
        out_specs=pl.BlockSpec((PROJ_W, RELAYOUT_LANES), lambda i: (0, i)),
        out_shape=jax.ShapeDtypeStruct((PROJ_W, D_MODEL), a4.dtype),
        compiler_params=_cparams(("parallel",)),
    )(a4)


def _unpad_w_in_t(wp):
    runs = _w_in_pieces()

    def body(p_ref, o_ref):
        for j, dst, src, n in runs:
            o_ref[j, pl.ds(dst, n), :] = p_ref[pl.ds(src, n), :]

    return pl.pallas_call(
        body, name="w_in_from_padded", grid=(D_MODEL // RELAYOUT_LANES,),
        in_specs=[pl.BlockSpec((PROJ_W, RELAYOUT_LANES), lambda i: (0, i))],
        out_specs=pl.BlockSpec((N_CHIPS, W_IN_SHARD, RELAYOUT_LANES), lambda i: (0, 0, i)),
        out_shape=jax.ShapeDtypeStruct((N_CHIPS, W_IN_SHARD, D_MODEL), wp.dtype),
        compiler_params=_cparams(("parallel",)),
    )(wp)


def _local_step(x, p, tgt, w):
    n_seq, seq_len, _ = x.shape
    n_rows = n_seq * seq_len
    tr = 512
    x2 = x.reshape(n_rows, D_MODEL)
    p2 = p.reshape(n_rows, PLE_DIM)
    t2 = tgt.reshape(n_rows, D_MODEL)
    row = functools.partial(_rowwise, n_rows=n_rows, tr=tr)

    w_pad_t = _pad_w_in_t(w["w_in_t"])
    norm_w = w["norm_w"].reshape(1, D_MODEL)
    ple_norm_w = w["ple_norm_w"].reshape(1, D_MODEL)
    final_w = w["final_norm_w"].reshape(1, D_MODEL)
    s5_d = w["s5_d"].reshape(1, S5_WIDTH)
    b_glu = w["s5_b_glu"].reshape(1, S5_WIDTH)
    conv_w = w["ssd_conv_w"].reshape(SSD_CONV, SSD_CONV_DIM)
    conv_b = w["ssd_conv_b"].reshape(1, SSD_CONV_DIM)
    ssd_norm_w = w["ssd_norm_w"].reshape(1, SSD_WIDTH)
    dtb, alog, dsk = (_pad_heads(w[k].reshape(1, SSD_HEADS)) for k in ("ssd_dt_bias", "ssd_a_log", "ssd_d"))

    gh = S5_GROUPS * S5_GROUP
    a_re = w["s5_a_re"].reshape(S5_GROUPS, S5_STATE)
    a_im = w["s5_a_im"].reshape(S5_GROUPS, S5_STATE)
    log_step = w["s5_log_step"].reshape(S5_GROUPS, 1)
    b_re2 = jnp.transpose(w["s5_b_re"].reshape(S5_GROUPS, S5_STATE, S5_GROUP), (0, 2, 1)).reshape(gh, S5_STATE)
    b_im2 = jnp.transpose(w["s5_b_im"].reshape(S5_GROUPS, S5_STATE, S5_GROUP), (0, 2, 1)).reshape(gh, S5_STATE)
    expand = (jnp.arange(gh)[:, None] // S5_GROUP == jnp.arange(S5_GROUPS)[None, :]).astype(F32)
    pow_re, pow_im, bb_re2, bb_im2 = _s5_params_fwd(a_re, a_im, log_step, b_re2, b_im2, expand)
    lam_pow = _state_cols((pow_re.reshape(S5_LOG_TB, S5_N), pow_im.reshape(S5_LOG_TB, S5_N)))
    lam_pow_conj = _state_cols((pow_re.reshape(S5_LOG_TB, S5_N), -pow_im.reshape(S5_LOG_TB, S5_N)))
    bb_band = _band(bb_re2, bb_im2).astype(BF16)
    c_band = _band(w["s5_c_re"].reshape(gh, S5_STATE), -w["s5_c_im"].reshape(gh, S5_STATE)).astype(BF16)

    late = w.get("_late")
    (hn,) = row("rms_in", lambda r, q: ([_rms(r[0], q[0])], []), row_ins=[(x2, 0, D_MODEL)],
                par_ins=[(norm_w, 0, D_MODEL)] + ([(late[0], 0, 128)] if late else []), row_outs=[(D_MODEL, BF16)])
    proj = _matmul("mm_proj", hn, w_pad_t, tb=True)
    s, yc, ge = _s5_fwd(proj, bb_band, c_band, lam_pow, s5_d, n_seq, seq_len)
    if late:
        w = {**w, **late[1](ge)}
    tg, y5 = _matmul("mm_s5_glu", ge, w["s5_w_glu"], epilogue=lambda c, r, q: [c, _s5_out(r[0], r[1], r[2], c, *q)],
                     epi_rows=[(yc, 0), (proj, OFF_U5), (proj, OFF_Z5)], epi_pars=[s5_d, b_glu], epi_outs=[F32, BF16])

    xbc_act = _conv_fwd(proj, conv_w, conv_b, n_rows, seq_len)
    yss, states = _ssd_fwd(xbc_act, proj, dtb, alog, dsk, ssd_norm_w, n_seq, seq_len)

    m5 = _matmul("mm_br_s5", y5, w["w_br_s5"])
    ms, merged = _matmul("mm_br_ssd", yss, w["w_br_ssd"], epilogue=lambda c, r, q: [c, _merge(r[0], r[1], r[2], c)],
                         epi_rows=[(proj, OFF_G5), (proj, OFF_GS), (m5, 0)], epi_outs=[F32, BF16])

    def resid_norm(c, r, q):
        h1_ = r[0] + c
        return [h1_, _rms(h1_, q[0])]

    h1, hp = _matmul("mm_out", merged, w["w_out"], epilogue=resid_norm, epi_rows=[(x2, 0)], epi_pars=[ple_norm_w],
                     epi_outs=[F32, BF16], full_rows=True)
    pp = _matmul("mm_ple_proj", p2, w["w_ple_proj"])

    def head_fn(pgl_, r, q):
        h1_, pp_, tgt_ = r
        loss, vjp = jax.vjp(lambda a, b, c, f: _head_loss(a, b, c, f, tgt_), h1_, pgl_, pp_, q[0])
        dh1_, dpgl_, dpp_, dfw_ = vjp(jnp.ones_like(loss))
        return [dh1_, dpgl_, dpp_], [loss, dfw_]

    dh2, dpgl, dpp, loss_acc, d_final_w = _matmul(
        "mm_ple_gate_head", hp, w["w_ple_gate"], epilogue=head_fn, epi_rows=[(h1, 0), (pp, 0), (t2, 0)],
        epi_pars=[final_w], epi_outs=[F32, BF16, BF16], epi_accs=[(1, 128), (1, D_MODEL)], full_rows=True)
    loss = loss_acc[0, 0]

    g = {}
    g["final_norm_w"] = d_final_w
    g["w_ple_gate"] = _matmul("mm_d_w_ple_gate", hp, dpgl, ta=True)
    g["w_ple_proj"] = _matmul("mm_d_w_ple_proj", p2, dpp, ta=True)

    def ple_norm_bwd(dhp_, r, q):
        h1_, dh2_ = r
        _, vjp = jax.vjp(_rms, h1_, q[0])
        dh, dw = vjp(dhp_)
        dh = dh + dh2_
        return [dh, dh], [dw]

    dh1, dh1_b, g["ple_norm_w"] = _matmul(
        "mm_d_hp", dpgl, w["w_ple_gate"], tb=True, epilogue=ple_norm_bwd, epi_rows=[(h1, 0), (dh2, 0)],
        epi_pars=[ple_norm_w], epi_outs=[F32, BF16], epi_accs=[(1, D_MODEL)], full_rows=True)
    g["w_out"] = _matmul("mm_d_w_out", merged, dh1_b, ta=True)

    dproj = lax.empty((n_rows, PROJ_W), BF16)

    def merge_bwd(dmerged, r, q):
        sg5, sgs = jax.nn.sigmoid(r[0]), jax.nn.sigmoid(r[1])
        d_gates = jnp.concatenate([dmerged * r[2] * sg5 * (1.0 - sg5), dmerged * r[3] * sgs * (1.0 - sgs)], axis=1)
        return [d_gates, dmerged * sg5, dmerged * sgs]

    dproj, dm5, dms = _matmul(
        "mm_d_merged", dh1_b, w["w_out"], tb=True, epilogue=merge_bwd,
        epi_rows=[(proj, OFF_G5), (proj, OFF_GS), (m5, 0), (ms, 0)], epi_outs=[BF16, BF16, BF16], full_rows=True,
        epi_into=(dproj, OFF_G5, 2 * D_MODEL))
    g["w_br_s5"] = _matmul("mm_d_w_br_s5", y5, dm5, ta=True)
    g["w_br_ssd"] = _matmul("mm_d_w_br_ssd", yss, dms, ta=True)
    dyss = _matmul("mm_d_yss", dms, w["w_br_ssd"], tb=True)

    def s5_out_bwd_a(dy_, r, q):
        yc_, u_, z_, t_ = r
        d_, bg_ = q
        ge_ = jax.nn.gelu(yc_ + d_ * u_)
        _, vjp = jax.vjp(lambda a, z, t, b: a * jax.nn.sigmoid(t + b) * jax.nn.silu(z), ge_, z_, t_, bg_)
        dge, dz, dt_, dbg = vjp(dy_)
        return [dz, dge, dt_], [dbg]

    dproj, dge_a, dtg, g["s5_b_glu"] = _matmul(
        "mm_d_y5", dm5, w["w_br_s5"], tb=True, epilogue=s5_out_bwd_a,
        epi_rows=[(yc, 0), (proj, OFF_U5), (proj, OFF_Z5), (tg, 0)], epi_pars=[s5_d, b_glu],
        epi_outs=[BF16, F32, BF16], epi_accs=[(1, S5_WIDTH)], full_rows=True, epi_into=(dproj, OFF_Z5, S5_WIDTH))
    g["s5_w_glu"] = _matmul("mm_d_w_glu", ge, dtg, ta=True)
    early = w["_early"](g) if "_early" in w else None

    def s5_out_bwd_b(dge_b, r, q):
        yc_, u_, da_ = r
        _, vjp = jax.vjp(lambda yc, u, d: jax.nn.gelu(yc + d * u), yc_, u_, q[0])
        dyc_, du_, dd_ = vjp(da_ + dge_b)
        return [dyc_, du_], [dd_]

    dyc, du5_a, g["s5_d"] = _matmul(
        "mm_d_ge", dtg, w["s5_w_glu"], tb=True, epilogue=s5_out_bwd_b,
        epi_rows=[(yc, 0), (proj, OFF_U5), (dge_a, 0)], epi_pars=[s5_d], epi_outs=[BF16, F32],
        epi_accs=[(1, S5_WIDTH)], full_rows=True, dep=early)
    dproj, d_lam, d_bb_band, d_c_band = _s5_bwd(dyc, s, proj, du5_a, bb_band, c_band, lam_pow_conj, dproj,
                                                n_seq, seq_len)

    d_lr, d_li = _state_uncols(d_lam)
    d_bbr, d_bbi = _band_take(d_bb_band)
    d_are, d_aim, d_ls, d_br2, d_bi2 = _s5_params_bwd(
        a_re, a_im, log_step, b_re2, b_im2, expand,
        d_lr.reshape(S5_GROUPS, S5_STATE), d_li.reshape(S5_GROUPS, S5_STATE), d_bbr, d_bbi)
    g["s5_a_re"], g["s5_a_im"], g["s5_log_step"] = d_are, d_aim, d_ls
    g["s5_b_re_ghp"], g["s5_b_im_ghp"] = d_br2, d_bi2
    d_cr, d_ci = _band_take(d_c_band)
    g["s5_c_re"], g["s5_c_im"] = d_cr, -d_ci

    dxs, dbm, dcm, dproj, ddt, d_dtb, d_alog, d_dsk, g["ssd_norm_w"] = _ssd_bwd(
        xbc_act, proj, states, dyss, dtb, alog, dsk, ssd_norm_w, n_seq, seq_len, dproj)
    (dproj,) = row("ssd_ddt", lambda r, q: ([r[0]], []), row_ins=[(ddt, 0, DT_W)], par_ins=[],
                   row_outs=[(DT_W, BF16)], into=(dproj, OFF_DT))
    g["ssd_dt_bias"], g["ssd_a_log"], g["ssd_d"] = _unpad_heads(d_dtb), _unpad_heads(d_alog), _unpad_heads(d_dsk)
    conv_dw, conv_db = [], []
    for nm, d_act, off in (("x", dxs, 0), ("b", dbm, SSD_WIDTH), ("c", dcm, SSD_WIDTH + SSD_BC)):
        dproj, dw_, db_ = _conv_bwd("ssd_conv_bwd_" + nm, proj, d_act, conv_w, conv_b, n_rows, seq_len, off, dproj)
        conv_dw.append(dw_)
        conv_db.append(db_)
    g["ssd_conv_w"] = jnp.concatenate(conv_dw, axis=1)
    g["ssd_conv_b"] = jnp.concatenate(conv_db, axis=1)

    g["w_in_t"] = _unpad_w_in_t(_matmul("mm_d_w_in", dproj, hn, ta=True))
    def norm_bwd(dhn_, r, q):
        x_, dh1_ = r
        _, vjp = jax.vjp(_rms, x_, q[0])
        dx_, dw_ = vjp(dhn_)
        return [dx_ + dh1_], [dw_]

    sent = w["_w_in_ready"](g, loss) if "_w_in_ready" in w else None
    dx, g["norm_w"] = _matmul("mm_d_hn", dproj, w_pad_t, epilogue=norm_bwd, epi_rows=[(x2, 0), (dh1, 0)],
                              epi_pars=[norm_w], epi_outs=[F32], epi_accs=[(1, D_MODEL)], full_rows=True, dep=sent)
    return loss, dx.reshape(x.shape), g


HBM = pl.BlockSpec(memory_space=pltpu.HBM)


def _chip_index(x, y):
    return 2 * x + y


def _half(shape2d, axis, which):
    h = shape2d[axis] // 2
    sl = pl.ds(pl.multiple_of(which * h, 128 if axis else 8), h)
    return (slice(None), sl) if axis else (sl, slice(None))


def _gather_chips(split, axes, whole):
    ns, nw = len(split), len(whole)
    n = ns + nw

    def body(*refs):
        ins, outs = refs[:n], refs[n:2 * n]
        ici_send, ici_recv, d2d_send, d2d_recv, local_sems = refs[2 * n:]
        x, y, c = lax.axis_index("x"), lax.axis_index("y"), lax.axis_index("c")
        me = _chip_index(x, y)
        sibling = (x, y, 1 - c)
        peers = [(1 - x, y), (x, 1 - y), (1 - x, 1 - y)]

        def half(t, which):
            return _half(split[t].shape, axes[t], which)

        copies = []
        for t in range(n):
            loc = pltpu.make_async_copy(ins[t], outs[t].at[me], local_sems.at[t])
            loc.start()
            copies.append(loc)

        def ici(t, k, slot):
            px, py = peers[k]
            if t < ns:
                src, dst = ins[t].at[half(t, c)], outs[t].at[(slot,) + half(t, c)]
            else:
                src, dst = ins[t], outs[t].at[slot]
            return pltpu.make_async_remote_copy(src_ref=src, dst_ref=dst, send_sem=ici_send.at[t, k],
                                                recv_sem=ici_recv.at[t, k], device_id=(px, py, c), device_id_type=MESH)

        def d2d(t, k, which):
            rows = outs[t].at[(_chip_index(*peers[k]),) + half(t, which)]
            return pltpu.make_async_remote_copy(src_ref=rows, dst_ref=rows, send_sem=d2d_send.at[t, k],
                                                recv_sem=d2d_recv.at[t, k], device_id=sibling, device_id_type=MESH)

        sends = []
        for t in range(n):
            for k in range(3):
                cp = ici(t, k, me)
                cp.start()
                sends.append(cp)
        for t in range(n):
            for k in range(3):
                ici(t, k, _chip_index(*peers[k])).wait_recv()
                if t < ns:
                    cp = d2d(t, k, c)
                    cp.start()
                    sends.append(cp)
        for t in range(ns):
            for k in range(3):
                d2d(t, k, 1 - c).wait_recv()
        for cp in sends:
            cp.wait_send()
        for cp in copies:
            cp.wait()

    arrays = list(split) + list(whole)
    return pl.pallas_call(
        body, name="gather_weights",
        in_specs=[HBM] * n, out_specs=[HBM] * n,
        out_shape=[jax.ShapeDtypeStruct((N_CHIPS,) + a.shape, a.dtype) for a in arrays],
        scratch_shapes=[pltpu.SemaphoreType.DMA((n, 3)), pltpu.SemaphoreType.DMA((n, 3)),
                        pltpu.SemaphoreType.DMA((ns, 3)), pltpu.SemaphoreType.DMA((ns, 3)),
                        pltpu.SemaphoreType.DMA((n,))],
    )(*arrays)


SEM = pl.BlockSpec(memory_space=pltpu.SEMAPHORE)
DATAFLOW = pltpu.SideEffectType.DATAFLOW_SIDE_EFFECTING


def _gather_start(shards, after):
    n = len(shards)

    def body(*refs):
        ins, lands = refs[:n], refs[n:2 * n]
        send_sems, recv_sems = refs[2 * n + 1], refs[2 * n + 2]
        token = refs[-1]
        x, y, c = lax.axis_index("x"), lax.axis_index("y"), lax.axis_index("c")
        me = _chip_index(x, y)
        for t in range(n):
            for k, (px, py) in enumerate([(1 - x, y), (x, 1 - y), (1 - x, 1 - y)]):
                pltpu.make_async_remote_copy(
                    src_ref=ins[t], dst_ref=lands[t].at[me], send_sem=send_sems.at[3 * t + k],
                    recv_sem=recv_sems.at[3 * t + k],
                    device_id=(px, py, c), device_id_type=MESH).start()
        token[...] = jnp.zeros_like(token)

    zones = [lax.empty((N_CHIPS,) + a.shape, a.dtype) for a in shards]
    res = pl.pallas_call(
        body, name="gather_rest_start",
        out_shape=(pltpu.SemaphoreType.DMA((3 * n,)), pltpu.SemaphoreType.DMA((3 * n,)),
                   *[pltpu.HBM(a.shape, a.dtype) for a in shards], *[pltpu.HBM(z.shape, z.dtype) for z in zones],
                   jax.ShapeDtypeStruct((8, 128), F32)),
        in_specs=[HBM] * (2 * n) + [pl.BlockSpec(memory_space=pl.ANY)],
        out_specs=(SEM, SEM, *[HBM] * (2 * n), pl.BlockSpec(memory_space=pltpu.VMEM)),
        input_output_aliases={t: 2 + t for t in range(2 * n)},
        compiler_params=pltpu.CompilerParams(has_side_effects=DATAFLOW),
    )(*[pltpu.with_memory_space_constraint(a, pltpu.HBM) for a in shards],
      *[pltpu.with_memory_space_constraint(z, pltpu.HBM) for z in zones], after)
    return res[0], res[1], list(res[2:2 + n]), list(res[2 + n:2 + 2 * n]), res[-1]


def _gather_wait(send_sems, recv_sems, thru, lands, after):
    n = len(thru)

    def body(*refs):
        ins, zones = refs[:n], refs[n:2 * n]
        s_sems, r_sems = refs[2 * n], refs[2 * n + 1]
        x, y, c = lax.axis_index("x"), lax.axis_index("y"), lax.axis_index("c")
        for t in range(n):
            for k, (px, py) in enumerate([(1 - x, y), (x, 1 - y), (1 - x, 1 - y)]):
                cp = pltpu.make_async_remote_copy(
                    src_ref=ins[t], dst_ref=zones[t].at[_chip_index(px, py)], send_sem=s_sems.at[3 * t + k],
                    recv_sem=r_sems.at[3 * t + k], device_id=(px, py, c), device_id_type=MESH)
                cp.wait_send()
                cp.wait_recv()

    res = pl.pallas_call(
        body, name="gather_rest_wait",
        out_shape=(*[pltpu.HBM(a.shape, a.dtype) for a in thru], *[pltpu.HBM(z.shape, z.dtype) for z in lands]),
        in_specs=[HBM] * (2 * n) + [SEM, SEM, pl.BlockSpec(memory_space=pl.ANY)], out_specs=[HBM] * (2 * n),
        input_output_aliases={t: t for t in range(2 * n)},
        compiler_params=pltpu.CompilerParams(has_side_effects=DATAFLOW),
    )(*thru, *lands, send_sems, recv_sems, after)
    return list(res[:n]), list(res[n:])


def _scatter_ends(t, k, n_slotted, ins, zones, x, y, me):
    px, py = [(1 - x, y), (x, 1 - y), (1 - x, 1 - y)][k]
    if t < n_slotted:
        return ins[t].at[_chip_index(px, py)], zones[t].at[k], (px, py)
    return ins[t], (zones[t].at[me], zones[t].at[_chip_index(px, py)]), (px, py)


def _scatter_start(name, slotted, whole, after):
    n_s = len(slotted)
    arrays = list(slotted) + list(whole)
    n = len(arrays)

    def body(*refs):
        ins, lands = refs[:n], refs[n:2 * n]
        send_sems, recv_sems = refs[2 * n + 1], refs[2 * n + 2]
        token = refs[-1]
        x, y, c = lax.axis_index("x"), lax.axis_index("y"), lax.axis_index("c")
        for t in range(n):
            for k in range(3):
                src, dst, (px, py) = _scatter_ends(t, k, n_s, ins, lands, x, y, _chip_index(x, y))
                pltpu.make_async_remote_copy(
                    src_ref=src, dst_ref=dst if t < n_s else dst[0], send_sem=send_sems.at[3 * t + k],
                    recv_sem=recv_sems.at[3 * t + k], device_id=(px, py, c), device_id_type=MESH).start()
        token[...] = jnp.zeros_like(token)

    zones = [lax.empty((3,) + a.shape[1:], a.dtype) for a in slotted]
    zones += [lax.empty((N_CHIPS,) + a.shape, a.dtype) for a in whole]
    slotted = arrays
    res = pl.pallas_call(
        body, name=name,
        out_shape=(pltpu.SemaphoreType.DMA((3 * n,)), pltpu.SemaphoreType.DMA((3 * n,)),
                   *[pltpu.HBM(a.shape, a.dtype) for a in slotted], *[pltpu.HBM(z.shape, z.dtype) for z in zones],
                   jax.ShapeDtypeStruct((8, 128), F32)),
        in_specs=[HBM] * (2 * n) + [pl.BlockSpec(memory_space=pl.ANY)],
        out_specs=(SEM, SEM, *[HBM] * (2 * n), pl.BlockSpec(memory_space=pltpu.VMEM)),
        input_output_aliases={t: 2 + t for t in range(2 * n)},
        compiler_params=pltpu.CompilerParams(has_side_effects=DATAFLOW),
    )(*[pltpu.with_memory_space_constraint(a, pltpu.HBM) for a in slotted],
      *[pltpu.with_memory_space_constraint(z, pltpu.HBM) for z in zones], after)
    return res[0], res[1], list(res[2:2 + n]), list(res[2 + n:2 + 2 * n]), res[-1]


def _scatter_wait(name, n_slotted, send_sems, recv_sems, thru, lands, after):
    n = len(thru)

    def body(*refs):
        ins, zones = refs[:n], refs[n:2 * n]
        s_sems, r_sems = refs[2 * n], refs[2 * n + 1]
        x, y, c = lax.axis_index("x"), lax.axis_index("y"), lax.axis_index("c")
        for t in range(n):
            for k in range(3):
                src, dst, (px, py) = _scatter_ends(t, k, n_slotted, ins, zones, x, y, _chip_index(x, y))
                cp = pltpu.make_async_remote_copy(
                    src_ref=src, dst_ref=dst if t < n_slotted else dst[1], send_sem=s_sems.at[3 * t + k],
                    recv_sem=r_sems.at[3 * t + k], device_id=(px, py, c), device_id_type=MESH)
                cp.wait_send()
                cp.wait_recv()

    res = pl.pallas_call(
        body, name=name,
        out_shape=(*[pltpu.HBM(a.shape, a.dtype) for a in thru], *[pltpu.HBM(z.shape, z.dtype) for z in lands]),
        in_specs=[HBM] * (2 * n) + [SEM, SEM, pl.BlockSpec(memory_space=pl.ANY)], out_specs=[HBM] * (2 * n),
        input_output_aliases={t: t for t in range(2 * n)},
        compiler_params=pltpu.CompilerParams(has_side_effects=DATAFLOW),
    )(*thru, *lands, send_sems, recv_sems, after)
    return list(res[:n]), list(res[n:])


def _sum_chips(name, slotted, recv, idx, dep):
    _, r, c = slotted.shape
    tr = _row_tile(r, 5 * c * 4)

    def body(idx_ref, own_ref, r_ref, _, o_ref):
        acc = own_ref[0]
        for k in range(3):
            acc = acc + r_ref[k]
        o_ref[...] = acc

    grid_spec = pltpu.PrefetchScalarGridSpec(
        num_scalar_prefetch=1, grid=(r // tr,),
        in_specs=[pl.BlockSpec((1, tr, c), lambda i, s: (s[1], i, 0)), pl.BlockSpec((3, tr, c), lambda i, s: (0, i, 0)),
                  pl.BlockSpec(memory_space=pl.ANY)],
        out_specs=pl.BlockSpec((tr, c), lambda i, s: (i, 0)))
    return pl.pallas_call(
        body, name=name, grid_spec=grid_spec, out_shape=jax.ShapeDtypeStruct((r, c), F32),
        compiler_params=_cparams(("parallel",)),
    )(idx, slotted, recv, dep)


def _half_shape(shape2d, axis):
    r, c = shape2d
    return (r, c // 2) if axis else (r // 2, c)


def _swap_halves(slotted, axes, small):
    n = len(slotted)

    def body(*refs):
        ins, sm_in = refs[:n], refs[n]
        outs, sm_out = refs[n + 1:2 * n + 1], refs[2 * n + 1]
        send_sems, recv_sems, local_sem = refs[2 * n + 2:]
        x, y, c = lax.axis_index("x"), lax.axis_index("y"), lax.axis_index("c")
        local = pltpu.make_async_copy(sm_in, sm_out.at[c], local_sem)
        local.start()
        sends = []
        for t in range(n):
            other = (slice(None),) + _half(slotted[t].shape[1:], axes[t], 1 - c)
            cp = pltpu.make_async_remote_copy(
                src_ref=ins[t].at[other], dst_ref=outs[t], send_sem=send_sems.at[t],
                recv_sem=recv_sems.at[t], device_id=(x, y, 1 - c), device_id_type=MESH)
            cp.start()
            sends.append(cp)
        cp = pltpu.make_async_remote_copy(
            src_ref=sm_in, dst_ref=sm_out.at[c], send_sem=send_sems.at[n], recv_sem=recv_sems.at[n],
            device_id=(x, y, 1 - c), device_id_type=MESH)
        cp.start()
        sends.append(cp)
        for cp in sends[:n]:
            cp.wait_recv()
        pltpu.make_async_remote_copy(
            src_ref=sm_in, dst_ref=sm_out.at[1 - c], send_sem=send_sems.at[n], recv_sem=recv_sems.at[n],
            device_id=(x, y, 1 - c), device_id_type=MESH).wait_recv()
        for cp in sends:
            cp.wait_send()
        local.wait()

    return pl.pallas_call(
        body, name="swap_halves",
        in_specs=[HBM] * (n + 1), out_specs=[HBM] * (n + 1),
        out_shape=[jax.ShapeDtypeStruct((a.shape[0],) + _half_shape(a.shape[1:], ax), a.dtype)
                   for a, ax in zip(slotted, axes, strict=True)]
        + [jax.ShapeDtypeStruct((2,) + small.shape, small.dtype)],
        scratch_shapes=[pltpu.SemaphoreType.DMA((n + 1,)), pltpu.SemaphoreType.DMA((n + 1,)), pltpu.SemaphoreType.DMA],
    )(*slotted, small)


def _tiling(r, c, f32_per_elem):
    if r % 8 == 0:
        tr = _row_tile(r, f32_per_elem * c * 4)
        return r // tr, (tr, c), lambda i: (i, 0)
    assert c % 128 == 0, (r, c)
    return c // 128, (r, 128), lambda i: (0, i)


def _pair_sum(name, slotted, other, idx, axis):
    _, r, c = slotted.shape
    hr, hc = _half_shape((r, c), axis)
    n, (tr, tc), at = _tiling(hr, hc, 13)
    if axis == 0:
        a = slotted.reshape(N_CHIPS, 2, hr, c)
        a_all = pl.BlockSpec((N_CHIPS, 1, tr, tc), lambda i, s: (0, s[0]) + at(i))
        a_own = pl.BlockSpec((1, 1, tr, tc), lambda i, s: (s[1], s[0]) + at(i))
    else:
        a, per_half = slotted, hc // tc
        a_all = pl.BlockSpec((N_CHIPS, tr, tc), lambda i, s: (0, at(i)[0], s[0] * per_half + at(i)[1]))
        a_own = pl.BlockSpec((1, tr, tc), lambda i, s: (s[1], at(i)[0], s[0] * per_half + at(i)[1]))

    def body(idx_ref, a_ref, b_ref, am_ref, bm_ref, p_ref, own_ref):
        mine, mine_own = (a_ref[:, 0], am_ref[0, 0]) if axis == 0 else (a_ref[...], am_ref[0])
        p_ref[...] = (mine + b_ref[...]).astype(p_ref.dtype)
        own_ref[...] = mine_own + bm_ref[0]

    grid_spec = pltpu.PrefetchScalarGridSpec(
        num_scalar_prefetch=1, grid=(n,),
        in_specs=[a_all, pl.BlockSpec((N_CHIPS, tr, tc), lambda i, s: (0,) + at(i)),
                  a_own, pl.BlockSpec((1, tr, tc), lambda i, s: (s[1],) + at(i))],
        out_specs=[pl.BlockSpec((N_CHIPS, tr, tc), lambda i, s: (0,) + at(i)),
                   pl.BlockSpec((tr, tc), lambda i, s: at(i))])
    return pl.pallas_call(
        body, name=name, grid_spec=grid_spec,
        out_shape=[jax.ShapeDtypeStruct((N_CHIPS, hr, hc), BF16), jax.ShapeDtypeStruct((hr, hc), F32)],
        compiler_params=_cparams(("parallel",)),
    )(idx, a, other, a, other)


def _sum_parts(name, own, recv):
    h, c = own.shape
    n, (tr, tc), at = _tiling(h, c, 4)

    def body(o_ref, r_ref, out_ref):
        acc = o_ref[...]
        for k in range(3):
            acc = acc + r_ref[k].astype(F32)
        out_ref[...] = acc

    return pl.pallas_call(
        body, name=name, grid=(n,),
        in_specs=[pl.BlockSpec((tr, tc), at), pl.BlockSpec((3, tr, tc), lambda i: (0,) + at(i))],
        out_specs=pl.BlockSpec((tr, tc), at),
        out_shape=jax.ShapeDtypeStruct((h, c), F32),
        compiler_params=_cparams(("parallel",)),
    )(own, recv)


def _all_to_all(name, small):
    def body(sm_in, sm_out, send_sems, recv_sems, local_sem):
        x, y, c = lax.axis_index("x"), lax.axis_index("y"), lax.axis_index("c")
        dev = 4 * x + 2 * y + c
        local = pltpu.make_async_copy(sm_in, sm_out.at[dev], local_sem)
        local.start()
        rel = [(fx, fy, fc) for fx in (0, 1) for fy in (0, 1) for fc in (0, 1)][1:]
        sends = []
        for k, (fx, fy, fc) in enumerate(rel):
            cp = pltpu.make_async_remote_copy(
                src_ref=sm_in, dst_ref=sm_out.at[dev], send_sem=send_sems.at[k], recv_sem=recv_sems.at[k],
                device_id=(x ^ fx, y ^ fy, c ^ fc), device_id_type=MESH)
            cp.start()
            sends.append(cp)
        for k, (fx, fy, fc) in enumerate(rel):
            src_dev = 4 * (x ^ fx) + 2 * (y ^ fy) + (c ^ fc)
            pltpu.make_async_remote_copy(
                src_ref=sm_in, dst_ref=sm_out.at[src_dev], send_sem=send_sems.at[k], recv_sem=recv_sems.at[k],
                device_id=(x ^ fx, y ^ fy, c ^ fc), device_id_type=MESH).wait_recv()
        for cp in sends:
            cp.wait_send()
        local.wait()

    return pl.pallas_call(
        body, name=name, in_specs=[HBM], out_specs=HBM,
        out_shape=jax.ShapeDtypeStruct((N_DEV,) + small.shape, small.dtype),
        scratch_shapes=[pltpu.SemaphoreType.DMA((7,)), pltpu.SemaphoreType.DMA((7,)), pltpu.SemaphoreType.DMA],
    )(small)


def _swap_sibling(name, parts):
    n = len(parts)

    def body(*refs):
        ins, outs = refs[:n], refs[n:2 * n]
        send_sems, recv_sems = refs[2 * n:]
        x, y, c = lax.axis_index("x"), lax.axis_index("y"), lax.axis_index("c")
        cps = []
        for t in range(n):
            cp = pltpu.make_async_remote_copy(
                src_ref=ins[t], dst_ref=outs[t], send_sem=send_sems.at[t], recv_sem=recv_sems.at[t],
                device_id=(x, y, 1 - c), device_id_type=MESH)
            cp.start()
            cps.append(cp)
        for cp in cps:
            cp.wait_recv()
        for cp in cps:
            cp.wait_send()

    return pl.pallas_call(
        body, name=name,
        in_specs=[HBM] * n, out_specs=[HBM] * n,
        out_shape=[jax.ShapeDtypeStruct(a.shape, a.dtype) for a in parts],
        scratch_shapes=[pltpu.SemaphoreType.DMA((n,)), pltpu.SemaphoreType.DMA((n,))],
    )(*parts)


def _sum_slots(name, a):
    k, r, c = a.shape
    tr = _row_tile(r, (k + 1) * c * 4)

    def body(a_ref, o_ref):
        acc = a_ref[0]
        for i in range(1, k):
            acc = acc + a_ref[i]
        o_ref[...] = acc

    return pl.pallas_call(
        body, name=name, grid=(r // tr,),
        in_specs=[pl.BlockSpec((k, tr, c), lambda i: (0, i, 0))],
        out_specs=pl.BlockSpec((tr, c), lambda i: (i, 0)),
        out_shape=jax.ShapeDtypeStruct((r, c), a.dtype),
        compiler_params=_cparams(("parallel",)),
    )(a)


def _adamw(name, w, m, v, g_parts):
    r, c = w.shape
    ng = len(g_parts)
    tr = _row_tile(r, (7 + ng) * c * 4)
    c1 = 1.0 - ADAM_B1 ** ADAM_STEP
    c2 = 1.0 - ADAM_B2 ** ADAM_STEP

    def body(*refs):
        w_ref, m_ref, v_ref = refs[:3]
        g_refs = refs[3:3 + ng]
        go_ref, d_ref, mo_ref, vo_ref = refs[3 + ng:]
        g = g_refs[0][...]
        for gr in g_refs[1:]:
            g = g + gr[...]
        m_new = ADAM_B1 * m_ref[...] + (1.0 - ADAM_B1) * g
        v_new = ADAM_B2 * v_ref[...] + (1.0 - ADAM_B2) * (g * g)
        go_ref[...] = g
        mo_ref[...] = m_new
        vo_ref[...] = v_new
        d_ref[...] = -ADAM_LR * ((m_new / c1) / (jnp.sqrt(v_new / c2) + ADAM_EPS) + ADAM_WD * w_ref[...])

    spec = pl.BlockSpec((tr, c), lambda i: (i, 0))
    return pl.pallas_call(
        body, name=name, grid=(r // tr,),
        in_specs=[spec] * (3 + ng), out_specs=[spec] * 4,
        out_shape=[jax.ShapeDtypeStruct((r, c), F32)] * 4,
        compiler_params=_cparams(("parallel",)),
    )(w, m, v, *g_parts)


def _adamw_halves(name, w, m, v, own, other, idx, axis):
    hr, hc = own.shape
    nb, (tr, tc), at = _tiling(hr, hc, 9)
    c1 = 1.0 - ADAM_B1 ** ADAM_STEP
    c2 = 1.0 - ADAM_B2 ** ADAM_STEP

    def body(idx_ref, w_ref, m_ref, v_ref, own_ref, oth_ref, go_ref, d_ref, mo_ref, vo_ref):
        g = jnp.where(pl.program_id(0) == idx_ref[0], own_ref[...], oth_ref[...])
        m_new = ADAM_B1 * m_ref[...] + (1.0 - ADAM_B1) * g
        v_new = ADAM_B2 * v_ref[...] + (1.0 - ADAM_B2) * (g * g)
        go_ref[...] = g
        mo_ref[...] = m_new
        vo_ref[...] = v_new
        d_ref[...] = -ADAM_LR * ((m_new / c1) / (jnp.sqrt(v_new / c2) + ADAM_EPS) + ADAM_WD * w_ref[...])

    per_half = (hc // tc) if axis else (hr // tr)
    if axis:
        full = pl.BlockSpec((tr, tc), lambda hh, i, s: (at(i)[0], hh * per_half + at(i)[1]))
    else:
        full = pl.BlockSpec((tr, tc), lambda hh, i, s: (hh * per_half + at(i)[0], at(i)[1]))
    part = pl.BlockSpec((tr, tc), lambda hh, i, s: at(i))
    grid_spec = pltpu.PrefetchScalarGridSpec(
        num_scalar_prefetch=1, grid=(2, nb), in_specs=[full, full, full, part, part], out_specs=[full] * 4)
    return pl.pallas_call(
        body, name=name, grid_spec=grid_spec, out_shape=[jax.ShapeDtypeStruct(w.shape, F32)] * 4,
        compiler_params=_cparams(("parallel", "parallel")),
    )(idx, w, m, v, own, other)


WEIGHTS = ['norm_w', 'w_in', 's5_a_re', 's5_a_im', 's5_b_re', 's5_b_im', 's5_c_re', 's5_c_im', 's5_d', 's5_log_step',
           's5_w_glu', 's5_b_glu', 'ssd_conv_w', 'ssd_conv_b', 'ssd_dt_bias', 'ssd_a_log', 'ssd_d', 'ssd_norm_w',
           'w_br_s5', 'w_br_ssd', 'w_out', 'ple_norm_w', 'w_ple_gate', 'w_ple_proj', 'final_norm_w']
SHARDED = {'w_in': ((IN_PROJ_DIM, 1024), 0), 's5_w_glu': ((512, 512), 0), 'ssd_conv_w': ((SSD_CONV, SSD_CONV_DIM), 1),
           'w_br_s5': ((512, 1024), 1), 'w_br_ssd': ((1536, 1024), 0), 'w_out': ((1024, 1024), 0),
           'w_ple_gate': ((1024, 1024), 0), 'w_ple_proj': ((256, 1024), 1)}
TRANSPOSED = ('w_in',)
SMALL = [n for n in WEIGHTS if n not in SHARDED]


def _shard_shape(name):
    (r, c), ax = SHARDED[name]
    return (r // N_CHIPS, c) if ax == 0 else (r, c // N_CHIPS)


def _half_axis(name):
    return 0 if (_shard_shape(name)[0] // 2) % 16 == 0 else 1


def _shard2d(name, a):
    r, c = _shard_shape(name)
    return a.reshape(c, r).T if name in TRANSPOSED else a.reshape(r, c)


def _unshard2d(name, a2, shape):
    return (a2.T if name in TRANSPOSED else a2).reshape(shape)


def _unslot(name, a4):
    (r, c), ax = SHARDED[name]
    if ax == 0:
        return a4.reshape(r, c)
    return jnp.transpose(a4, (1, 0, 2)).reshape(r, c)


def _slot(name, full):
    (r, c), ax = SHARDED[name]
    if ax == 0:
        return full.reshape(N_CHIPS, r // N_CHIPS, c)
    return jnp.transpose(full.reshape(r, N_CHIPS, c // N_CHIPS), (1, 0, 2))


GHP = ('s5_b_re', 's5_b_im')


def _view_shape(name):
    if name in ('s5_a_re', 's5_a_im'):
        return (S5_GROUPS, S5_STATE)
    if name in GHP + ('s5_c_re', 's5_c_im'):
        return (S5_GROUPS, S5_GROUP, S5_STATE)
    if name == 'ssd_conv_w':
        return (SSD_CONV, SSD_CONV_DIM // N_CHIPS)
    return (1, {'s5_log_step': S5_GROUPS, 'ssd_conv_b': SSD_CONV_DIM, 'ssd_norm_w': SSD_WIDTH, 's5_d': S5_WIDTH,
                's5_b_glu': S5_WIDTH, 'ssd_dt_bias': SSD_HEADS, 'ssd_a_log': SSD_HEADS, 'ssd_d': SSD_HEADS}.get(name, D_MODEL))


def _view(name, a):
    if name in GHP:
        return jnp.swapaxes(a.reshape(S5_GROUPS, S5_STATE, S5_GROUP), 1, 2)
    return a.reshape(_view_shape(name))


def _unview(name, a, shape):
    return (jnp.swapaxes(a, 1, 2) if name in GHP else a).reshape(shape)


def _adamw_small(ws, ms, vs, gs):
    n = len(ws)
    c1 = 1.0 - ADAM_B1 ** ADAM_STEP
    c2 = 1.0 - ADAM_B2 ** ADAM_STEP

    def body(*refs):
        w_r, m_r, v_r, g_r = (refs[k * n:(k + 1) * n] for k in range(4))
        d_o, m_o, v_o = (refs[k * n:(k + 1) * n] for k in range(4, 7))
        for i in range(n):
            g = g_r[i][...]
            m_new = ADAM_B1 * m_r[i][...] + (1.0 - ADAM_B1) * g
            v_new = ADAM_B2 * v_r[i][...] + (1.0 - ADAM_B2) * (g * g)
            m_o[i][...] = m_new
            v_o[i][...] = v_new
            d_o[i][...] = -ADAM_LR * ((m_new / c1) / (jnp.sqrt(v_new / c2) + ADAM_EPS) + ADAM_WD * w_r[i][...])

    return pl.pallas_call(
        body, name="adamw_small", out_shape=[jax.ShapeDtypeStruct(w.shape, F32) for w in ws] * 3,
        compiler_params=pltpu.CompilerParams(vmem_limit_bytes=VMEM_LIMIT),
    )(*ws, *ms, *vs, *gs)


def _pack_small(vals):
    flat = jnp.concatenate([v.reshape(-1).astype(F32) for v in vals])
    rows = -(-flat.shape[0] // (256 * 128)) * 256
    return jnp.pad(flat, (0, rows * 128 - flat.shape[0])).reshape(rows, 128)


def _unpack_small(packed, shapes):
    flat = packed.reshape(-1)
    out, off = [], 0
    for sh in shapes:
        n = math.prod(sh)
        out.append(flat[off:off + n].reshape(sh))
        off += n
    return out


def kernel(x, p, norm_w, w_in, s5_a_re, s5_a_im, s5_b_re, s5_b_im, s5_c_re, s5_c_im, s5_d, s5_log_step, s5_w_glu, s5_b_glu, ssd_conv_w, ssd_conv_b, ssd_dt_bias, ssd_a_log, ssd_d, ssd_norm_w, w_br_s5, w_br_ssd, w_out, ple_norm_w, w_ple_gate, w_ple_proj, final_norm_w, loss_target, m_norm_w, m_w_in, m_s5_a_re, m_s5_a_im, m_s5_b_re, m_s5_b_im, m_s5_c_re, m_s5_c_im, m_s5_d, m_s5_log_step, m_s5_w_glu, m_s5_b_glu, m_ssd_conv_w, m_ssd_conv_b, m_ssd_dt_bias, m_ssd_a_log, m_ssd_d, m_ssd_norm_w, m_w_br_s5, m_w_br_ssd, m_w_out, m_ple_norm_w, m_w_ple_gate, m_w_ple_proj, m_final_norm_w, v_norm_w, v_w_in, v_s5_a_re, v_s5_a_im, v_s5_b_re, v_s5_b_im, v_s5_c_re, v_s5_c_im, v_s5_d, v_s5_log_step, v_s5_w_glu, v_s5_b_glu, v_ssd_conv_w, v_ssd_conv_b, v_ssd_dt_bias, v_ssd_a_log, v_ssd_d, v_ssd_norm_w, v_w_br_s5, v_w_br_ssd, v_w_out, v_ple_norm_w, v_w_ple_gate, v_w_ple_proj, v_final_norm_w):
    args = locals()
    wl = {n: args[n] for n in WEIGHTS}
    ml = {n: args["m_" + n] for n in WEIGHTS}
    vl = {n: args["v_" + n] for n in WEIGHTS}
    big = [n for n in SHARDED if n != 'ssd_conv_w']
    chip = _chip_index(lax.axis_index("x"), lax.axis_index("y"))
    idx = jnp.stack([lax.axis_index("c"), chip]).astype(jnp.int32)

    axes = [_half_axis(n) for n in big]
    first = ['w_in']
    rest = [n for n in big if n not in first]
    bf_shards = {n: _shard2d(n, wl[n]).astype(BF16) for n in big}
    w_in_t, conv_w4 = _gather_chips([bf_shards[n] for n in first], [_half_axis(n) for n in first],
                                    [_shard2d('ssd_conv_w', wl['ssd_conv_w'])])
    full = {n: wl[n] for n in SMALL}
    full["w_in_t"] = w_in_t
    full['ssd_conv_w'] = _unslot('ssd_conv_w', conv_w4)
    send_sems, recv_sems, thru, lands, token = _gather_start([bf_shards[n] for n in rest], w_in_t)

    def fetch_rest(after):
        own, zones = _gather_wait(send_sems, recv_sems, thru, lands, after)
        return {n: _unslot(n, lax.dynamic_update_slice(z, o[None], (chip, 0, 0)))
                for n, o, z in zip(rest, own, zones, strict=True)}

    full["_late"] = (token, fetch_rest)

    in_flight = []

    def send_rest(g_now):
        in_flight.extend(_scatter_start("scatter_rest_start", [_slot(n, g_now[n]) for n in rest], [], g_now['s5_w_glu']))
        return in_flight[4]

    full["_early"] = send_rest

    packed = [n for n in SMALL if n != 'norm_w']
    first_axes = [_half_axis(n) for n in first]
    st = {}

    def send_first(g_now, loss_now):
        small_pack = _pack_small([loss_now] + [g_now[n + "_ghp" if n in GHP else n] for n in packed]
                                 + [g_now['ssd_conv_w']])
        swapped = _swap_halves([g_now["w_in_t"]], first_axes, small_pack)
        st["pair"] = [_pair_sum("pair_sum_w_in", g_now["w_in_t"], swapped[0], idx, first_axes[0])]
        small_chip = _sum_slots("sum_small_pair", swapped[-1])
        half_rows = small_chip.shape[0] // 2
        my_half = lax.dynamic_slice(small_chip, (lax.axis_index("c") * half_rows, 0), (half_rows, small_chip.shape[1]))
        st["flight"] = _scatter_start("scatter_first_start", [pb for pb, _ in st["pair"]], [my_half], small_chip)
        return st["flight"][4]

    full["_w_in_ready"] = send_first
    loss, grad_x, g = _local_step(x, p[0], loss_target, full)
    pair = st["pair"]
    s_sems, r_sems, thru, lands, tok = st["flight"]
    small_shapes = [(1, 1)] + [_view_shape(n) for n in packed] + [(SSD_CONV, SSD_CONV_DIM)]
    norm_all = _all_to_all("norm_w_all_to_all", g['norm_w'].reshape(8, 128))
    norm_g = _sum_slots("sum_norm_w", norm_all).reshape(1, D_MODEL)

    out_g, out_d, out_m, out_v = {}, {}, {}, {}
    own_slots, zones = _scatter_wait("scatter_rest_wait", len(rest), *in_flight[:4], norm_g)
    chip_sums = [_sum_chips("sum_chips_" + n, a, z, idx, tok) for n, a, z in zip(rest, own_slots, zones, strict=True)]
    sib_sums = _swap_sibling("swap_sibling_rest", chip_sums)
    for n, own, sib in zip(rest, chip_sums, sib_sums, strict=True):
        res = _adamw("adamw_" + n, _shard2d(n, wl[n]), _shard2d(n, ml[n]), _shard2d(n, vl[n]), [own, sib])
        out_g[n], out_d[n], out_m[n], out_v[n] = (_unshard2d(n, r, wl[n].shape) for r in res)

    sent, got = _scatter_wait("scatter_first_wait", len(first), s_sems, r_sems, thru, lands, out_d[rest[-1]])
    small_zone = lax.dynamic_update_slice(got[-1], sent[-1][None], (chip, 0, 0))
    small_half = _sum_slots("sum_small_chips", small_zone)
    halves = [_sum_parts("sum_chips_" + n, own, r) for n, (_, own), r in zip(first, pair, got[:-1], strict=True)]
    swapped2 = _swap_sibling("swap_sibling_first", halves + [small_half])
    other_halves = swapped2[:-1]
    c_is_0 = lax.axis_index("c") == 0
    small_sum = jnp.concatenate([jnp.where(c_is_0, small_half, swapped2[-1]),
                                 jnp.where(c_is_0, swapped2[-1], small_half)], axis=0)
    for n, own, oth, ax in zip(first, halves, other_halves, first_axes, strict=True):
        res = _adamw_halves("adamw_" + n, _shard2d(n, wl[n]), _shard2d(n, ml[n]), _shard2d(n, vl[n]), own, oth, idx, ax)
        out_g[n], out_d[n], out_m[n], out_v[n] = (_unshard2d(n, r, wl[n].shape) for r in res)
    sm = _unpack_small(small_sum, small_shapes)
    loss_total = sm[0].reshape(())
    conv_g = lax.dynamic_slice(sm[-1], (0, chip * (SSD_CONV_DIM // N_CHIPS)), (SSD_CONV, SSD_CONV_DIM // N_CHIPS))
    names = SMALL + ['ssd_conv_w']
    by_name = {**dict(zip(packed, sm[1:-1], strict=True)), 'norm_w': norm_g, 'ssd_conv_w': conv_g}
    grads = [by_name[n] for n in names]
    res = _adamw_small([_view(n, wl[n]) for n in names], [_view(n, ml[n]) for n in names],
                       [_view(n, vl[n]) for n in names], grads)
    for i, n in enumerate(names):
        out_g[n] = _unview(n, grads[i], wl[n].shape)
        out_d[n], out_m[n], out_v[n] = (_unview(n, res[k * len(names) + i], wl[n].shape) for k in range(3))

    return (loss_total, grad_x, *[out_g[n] for n in WEIGHTS], *[out_d[n] for n in WEIGHTS],
            *[out_m[n] for n in WEIGHTS], *[out_v[n] for n in WEIGHTS])
```

```python
import functools
import math

import jax
import jax.numpy as jnp
from jax import lax
from jax.experimental import pallas as pl
from jax.experimental.pallas import tpu as pltpu

F32 = jnp.float32
BF16 = jnp.bfloat16
MESH = pl.DeviceIdType.MESH

D_MODEL = 1024
PLE_DIM = 256
RMS_EPS = 1e-6
S5_WIDTH = 512
S5_GROUP = 16
S5_GROUPS = 32
S5_STATE = 64
S5_N = S5_GROUPS * S5_STATE
S5_LB = 512
S5_NJ = S5_N // S5_LB
S5_TB = 2048
S5_LOG_TB = 8
SSD_WIDTH = 1536
SSD_HEADDIM = 64
SSD_HEADS = 24
SSD_GROUPS = 4
SSD_HPG = 6
SSD_STATE = 128
SSD_CONV = 4
SSD_CHUNK = 128
SSD_BC = 512
SSD_CONV_DIM = 2560
GROUP_W = SSD_WIDTH // SSD_GROUPS
N_CHIPS = 4
N_DEV = 8

OFF_XBC, OFF_U5, OFF_Z5, OFF_DT, OFF_G5, OFF_GS, OFF_ZS = 0, 2560, 3072, 3584, 4096, 5120, 6144
DT_W = 512
PROJ_W = 7680
IN_PROJ_DIM = 7192

ADAM_LR, ADAM_B1, ADAM_B2, ADAM_EPS, ADAM_WD, ADAM_STEP = 0.001, 0.9, 0.999, 1e-08, 0.01, 10

VMEM_LIMIT = 56 * 1024 * 1024


ROW_BLOCK_BYTES = 8 * 1024 * 1024


def _row_tile(r, bytes_per_row):
    for t in (r, 4096, 2048, 1024, 512, 256, 128, 64, 32, 16, 8):
        if t <= r and r % t == 0 and t * bytes_per_row <= ROW_BLOCK_BYTES:
            return t
    return r


def _cparams(sem):
    return pltpu.CompilerParams(dimension_semantics=sem, vmem_limit_bytes=VMEM_LIMIT)


def _dg(a, b, ca, cb):
    return lax.dot_general(a.astype(BF16), b.astype(BF16), (((ca,), (cb,)), ((), ())), preferred_element_type=F32)


@jax.custom_vjp
def dot_nn(a, b):
    return _dg(a, b, 1, 0)


@jax.custom_vjp
def dot_nt(a, b):
    return _dg(a, b, 1, 1)


@jax.custom_vjp
def dot_tn(a, b):
    return _dg(a, b, 0, 0)


dot_nn.defvjp(lambda a, b: (_dg(a, b, 1, 0), (a, b)), lambda r, g: (_dg(g, r[1], 1, 1), _dg(r[0], g, 0, 0)))
dot_nt.defvjp(lambda a, b: (_dg(a, b, 1, 1), (a, b)), lambda r, g: (_dg(g, r[1], 1, 0), _dg(g, r[0], 0, 0)))
dot_tn.defvjp(lambda a, b: (_dg(a, b, 0, 0), (a, b)), lambda r, g: (_dg(r[1], g, 1, 1), _dg(r[0], g, 1, 0)))


MM_VMEM_BUDGET = 40 * 1024 * 1024


def _mm_tiles(m, n, k, sa, sb, so, tn_only=None):
    best, best_key = None, None
    for tm in (1024, 512, 256, 128, 64, 32, 16, 8):
        if m % tm:
            continue
        for tn in (2048, 1536, 1280, 1024, 768, 640, 512, 384, 256, 128):
            if n % tn or (tn_only is not None and tn not in tn_only):
                continue
            for tk in (k, 2048, 1536, 1280, 1024, 768, 512, 256, 128):
                if k % tk or tk > max(k, 128):
                    continue
                need = 2 * (tm * tk * sa + tk * tn * sb + tm * tn * so) + (tm * tn * 4 if tk < k else 0)
                if need > MM_VMEM_BUDGET:
                    continue
                key = (tm * tn * tk, tk)
                if best_key is None or key > best_key:
                    best, best_key = (tm, tn, tk), key
    assert best is not None, (m, n, k)
    return best


def _matmul(name, a, b, *, ta=False, tb=False, a_win=None, out_dtype=F32, epilogue=None, epi_rows=(), epi_pars=(),
            epi_outs=(), full_rows=False, epi_accs=(), epi_into=None, dep=None):
    a_off, a_w = a_win if a_win is not None else (0, a.shape[1])
    if ta:
        kdim, m = a.shape[0], a_w
    else:
        m, kdim = a.shape[0], a_w
    n = b.shape[0] if tb else b.shape[1]
    assert (b.shape[1] if tb else b.shape[0]) == kdim, (name, a.shape, b.shape)
    out_dtypes = list(epi_outs) if epilogue is not None else [out_dtype]
    so = sum(jnp.dtype(d).itemsize for d in out_dtypes) + sum(r.dtype.itemsize for r, _ in epi_rows)
    tn_ok = [n] if full_rows else [t for t in (1024, 512, 256, 128) if all(off % t == 0 for _, off in epi_rows)]
    tm, tn, tk = _mm_tiles(m, n, kdim, a.dtype.itemsize, b.dtype.itemsize, so, tn_ok if epilogue is not None else None)
    nk = kdim // tk
    n_er, n_ep, n_out = len(epi_rows), len(epi_pars), len(out_dtypes)
    if ta:
        assert a_off % tm == 0
        a_spec = pl.BlockSpec((tk, tm), lambda i, j, k: (k, i + a_off // tm))
    else:
        assert a_off % tk == 0
        a_spec = pl.BlockSpec((tm, tk), lambda i, j, k: (i, k + a_off // tk))
    if tb:
        b_spec = pl.BlockSpec((tn, tk), lambda i, j, k: (j, k))
    else:
        b_spec = pl.BlockSpec((tk, tn), lambda i, j, k: (k, j))
    ca, cb = (0 if ta else 1), (1 if tb else 0)

    n_acc = len(epi_accs)
    n_x = (1 if epi_into is not None else 0) + (1 if dep is not None else 0)
    assert not (n_acc or epi_into is not None) or full_rows

    def body(a_ref, b_ref, *refs):
        er, ep = refs[:n_er], refs[n_er:n_er + n_ep]
        first_out = n_er + n_ep + n_x
        o_refs = refs[first_out:first_out + n_out]
        s_refs = refs[first_out + n_out:first_out + n_out + n_acc]
        acc = refs[first_out + n_out + n_acc:]

        def finish(c):
            outs = [c] if epilogue is None else epilogue(c, [r[...] for r in er], [p[...] for p in ep])
            if n_acc:
                outs, sums = outs

                @pl.when(pl.program_id(0) == 0)
                def _():
                    for s_ref in s_refs:
                        s_ref[...] = jnp.zeros_like(s_ref)

                for s_ref, s in zip(s_refs, sums, strict=True):
                    s_ref[...] += jnp.broadcast_to(s, s_ref.shape)
            for o_ref, o in zip(o_refs, outs, strict=True):
                o_ref[...] = o.astype(o_ref.dtype)

        if nk == 1:
            finish(_dg(a_ref[...], b_ref[...], ca, cb))
            return
        (acc_ref,) = acc
        k = pl.program_id(2)

        @pl.when(k == 0)
        def _():
            acc_ref[...] = jnp.zeros_like(acc_ref)

        acc_ref[...] += _dg(a_ref[...], b_ref[...], ca, cb)

        @pl.when(k == nk - 1)
        def _():
            finish(acc_ref[...])

    in_specs = [a_spec, b_spec]
    in_specs += [pl.BlockSpec((tm, tn), functools.partial(lambda i, j, k, c: (i, j + c), c=off // tn)) for _, off in epi_rows]
    in_specs += [pl.BlockSpec((p.shape[0], tn), lambda i, j, k: (0, j)) for p in epi_pars]
    out_specs = [pl.BlockSpec((tm, tn), lambda i, j, k: (i, j)) for _ in out_dtypes]
    out_shape = [jax.ShapeDtypeStruct((m, n), d) for d in out_dtypes]
    extra, aliases = [], {}
    if epi_into is not None:
        buf, off, width = epi_into
        assert off % width == 0 and buf.dtype == out_dtypes[0]
        in_specs.append(pl.BlockSpec(memory_space=pl.ANY))
        out_specs[0] = pl.BlockSpec((tm, width), functools.partial(lambda i, j, k, c: (i, c), c=off // width))
        out_shape[0] = jax.ShapeDtypeStruct(buf.shape, buf.dtype)
        extra, aliases = [buf], {2 + n_er + n_ep: 0}
    if dep is not None:
        in_specs.append(pl.BlockSpec(memory_space=pl.ANY))
        extra = extra + [dep]
    out_specs += [pl.BlockSpec((r, w), lambda i, j, k: (0, 0)) for r, w in epi_accs]
    out_shape += [jax.ShapeDtypeStruct((r, w), F32) for r, w in epi_accs]
    res = pl.pallas_call(
        body, name=name, grid=(m // tm, n // tn, nk),
        in_specs=in_specs, out_specs=out_specs, out_shape=out_shape, input_output_aliases=aliases,
        scratch_shapes=[pltpu.VMEM((tm, tn), F32)] if nk > 1 else [],
        compiler_params=_cparams(("arbitrary",) * 3 if n_acc else ("parallel", "parallel", "arbitrary")),
    )(a, b, *[r for r, _ in epi_rows], *epi_pars, *extra)
    return res if epilogue is not None else res[0]


BAND = 128


def _rowwise(name, fn, n_rows, tr, row_ins, par_ins, row_outs, acc_outs=(), into=None):
    nr, npar, no, na = len(row_ins), len(par_ins), len(row_outs), len(acc_outs)
    in_specs = []
    for arr, off, w in row_ins:
        assert off % w == 0 and arr.shape[0] == n_rows, (name, arr.shape, off, w)
        in_specs.append(pl.BlockSpec((tr, w), functools.partial(lambda i, c: (i, c), c=off // w)))
    for arr, off, w in par_ins:
        assert off % w == 0
        in_specs.append(pl.BlockSpec((arr.shape[0], w), functools.partial(lambda i, c: (0, c), c=off // w)))
    out_specs = [pl.BlockSpec((tr, w), lambda i: (i, 0)) for w, _ in row_outs]
    out_specs += [pl.BlockSpec((r, w), lambda i: (0, 0)) for r, w in acc_outs]
    out_shape = [jax.ShapeDtypeStruct((n_rows, w), dt) for w, dt in row_outs]
    out_shape += [jax.ShapeDtypeStruct((r, w), F32) for r, w in acc_outs]
    extra, aliases = [], {}
    if into is not None:
        buf, off = into
        w0 = row_outs[0][0]
        assert off % w0 == 0 and buf.dtype == row_outs[0][1]
        in_specs.append(pl.BlockSpec(memory_space=pl.ANY))
        out_specs[0] = pl.BlockSpec((tr, w0), functools.partial(lambda i, c: (i, c), c=off // w0))
        out_shape[0] = jax.ShapeDtypeStruct(buf.shape, buf.dtype)
        extra, aliases = [buf], {nr + npar: 0}
    nx = len(extra)

    def body(*refs):
        rows = [r[...] for r in refs[:nr]]
        pars = [r[...] for r in refs[nr:nr + npar]]
        o_refs = refs[nr + npar + nx:nr + npar + nx + no]
        a_refs = refs[nr + npar + nx + no:]
        outs, accs = fn(rows, pars)
        for o_ref, o in zip(o_refs, outs, strict=True):
            o_ref[...] = o.astype(o_ref.dtype)
        if na:
            @pl.when(pl.program_id(0) == 0)
            def _():
                for a_ref in a_refs:
                    a_ref[...] = jnp.zeros_like(a_ref)

            for a_ref, a in zip(a_refs, accs, strict=True):
                a_ref[...] += jnp.broadcast_to(a, a_ref.shape)

    res = pl.pallas_call(
        body, name=name, grid=(n_rows // tr,),
        in_specs=in_specs, out_specs=out_specs, out_shape=out_shape, input_output_aliases=aliases,
        compiler_params=_cparams(("arbitrary",) if na else ("parallel",)),
    )(*[a for a, _, _ in row_ins], *[a for a, _, _ in par_ins], *extra)
    return res


def _rms(x, w):
    return x * lax.rsqrt(jnp.mean(x * x, axis=-1, keepdims=True) + RMS_EPS) * w


def _s5_out(yc, u, z, t, d, bg):
    ge = jax.nn.gelu(yc + d * u)
    return ge * jax.nn.sigmoid(t + bg) * jax.nn.silu(z)


def _merge(g5, gs, m5, ms):
    return jax.nn.sigmoid(g5) * m5 + jax.nn.sigmoid(gs) * ms


def _head_loss(h1, pgl, pp, fw, tgt):
    h2 = h1 + jax.nn.sigmoid(pgl) * pp
    err = _rms(h2, fw) - tgt
    per_row = 0.5 * jnp.mean(err * err, axis=-1, keepdims=True)
    return jnp.sum(per_row, axis=0, keepdims=True)


def _s5_disc(a_re, a_im, log_step, b_re2, b_im2, expand):
    step = jnp.exp(log_step)
    mag = jnp.exp(a_re * step)
    lb_re = mag * jnp.cos(a_im * step)
    lb_im = mag * jnp.sin(a_im * step)
    den = a_re * a_re + a_im * a_im
    n_re = lb_re - 1.0
    f_re = (n_re * a_re + lb_im * a_im) / den
    f_im = (lb_im * a_re - n_re * a_im) / den
    hi = lax.Precision.HIGHEST
    fr = jnp.dot(expand, f_re, precision=hi, preferred_element_type=F32)
    fi = jnp.dot(expand, f_im, precision=hi, preferred_element_type=F32)
    return lb_re, lb_im, fr * b_re2 - fi * b_im2, fr * b_im2 + fi * b_re2


def _s5_params_fwd(a_re, a_im, log_step, b_re2, b_im2, expand):
    gp = a_re.shape

    def body(ar, ai, ls, br, bi, ex, pr_ref, pi_ref, bbr_ref, bbi_ref):
        lr, li, bbr, bbi = _s5_disc(ar[...], ai[...], ls[...], br[...], bi[...], ex[...])
        bbr_ref[...] = bbr
        bbi_ref[...] = bbi
        qr, qi = lr, li
        for k in range(S5_LOG_TB):
            pr_ref[k] = qr
            pi_ref[k] = qi
            qr, qi = qr * lr - qi * li, qr * li + qi * lr

    return pl.pallas_call(
        body, name="s5_params_fwd",
        out_shape=(jax.ShapeDtypeStruct((S5_LOG_TB,) + gp, F32), jax.ShapeDtypeStruct((S5_LOG_TB,) + gp, F32),
                   jax.ShapeDtypeStruct(b_re2.shape, F32), jax.ShapeDtypeStruct(b_re2.shape, F32)),
    )(a_re, a_im, log_step, b_re2, b_im2, expand)


def _s5_params_bwd(a_re, a_im, log_step, b_re2, b_im2, expand, d_lr, d_li, d_bbr, d_bbi):
    def body(ar, ai, ls, br, bi, ex, glr, gli, gbr, gbi, dar, dai, dls, dbr, dbi):
        _, vjp = jax.vjp(lambda *p: _s5_disc(*p, ex[...]), ar[...], ai[...], ls[...], br[...], bi[...])
        g = vjp((glr[...], gli[...], gbr[...], gbi[...]))
        for ref, val in zip((dar, dai, dls, dbr, dbi), g, strict=True):
            ref[...] = val

    return pl.pallas_call(
        body, name="s5_params_bwd",
        out_shape=tuple(jax.ShapeDtypeStruct(v.shape, F32) for v in (a_re, a_im, log_step, b_re2, b_im2)),
    )(a_re, a_im, log_step, b_re2, b_im2, expand, d_lr, d_li, d_bbr, d_bbi)


def _scan_block(x_ref, out_ref, lp, edge_pow, cr, ci, reverse, each=None):
    n_g = x_ref.shape[0] // 8
    sub = lax.broadcasted_iota(jnp.int32, (8, S5_LB), 0)
    steps = []
    for sh in (1, 2, 4):
        keep = (sub < 8 - sh) if reverse else (sub >= sh)
        steps.append((8 - sh if reverse else sh, jnp.where(keep, lp[sh - 1:sh, :S5_LB], 0.0),
                      jnp.where(keep, lp[sh - 1:sh, S5_LB:], 0.0)))
    e_r, e_i = edge_pow[:, :S5_LB], edge_pow[:, S5_LB:]
    for r in (range(n_g - 1, -1, -1) if reverse else range(n_g)):
        rows = slice(8 * r, 8 * r + 8)
        xr, xi = x_ref[rows, :S5_LB], x_ref[rows, S5_LB:]
        for by, a_r, a_i in steps:
            pr, pi = pltpu.roll(xr, by, 0), pltpu.roll(xi, by, 0)
            xr, xi = xr + a_r * pr - a_i * pi, xi + a_r * pi + a_i * pr
        xr = xr + e_r * cr - e_i * ci
        xi = xi + e_r * ci + e_i * cr
        out_ref[rows, :S5_LB] = xr
        out_ref[rows, S5_LB:] = xi
        if each is not None:
            each(r, xr, xi)
        cr, ci = (xr[0:1, :], xi[0:1, :]) if reverse else (xr[7:8, :], xi[7:8, :])
    return cr, ci


def _s5_fwd(proj, bb_band, c_band, lam_pow, s5_d, n_seq, seq_len):
    n_t = seq_len // S5_TB
    blk = 2 * S5_LB
    n_rows = n_seq * seq_len
    u_blk0 = OFF_U5 // BAND

    def body(u_ref, bb_ref, cb_ref, lp_ref, d_ref, s_ref, y_ref, ge_ref, cr, ci, buf):
        @pl.when(pl.program_id(2) == 0)
        def _():
            cr[...] = jnp.zeros_like(cr)
            ci[...] = jnp.zeros_like(ci)

        u = u_ref[...]
        buf[...] = _dg(u, bb_ref[...], 1, 0)
        lp = lp_ref[...]
        cr[...], ci[...] = _scan_block(buf, buf, lp, lp, cr[...], ci[...], False)
        s = buf[...].astype(s_ref.dtype)
        s_ref[...] = s
        y = _dg(s, cb_ref[...], 1, 1)
        y_ref[...] = y
        ge_ref[...] = jax.nn.gelu(y + d_ref[...] * u).astype(ge_ref.dtype)

    def rows(j, b, t):
        return (b * n_t + t, j)

    return pl.pallas_call(
        body, name="s5_fwd", grid=(S5_NJ, n_seq, n_t),
        in_specs=[pl.BlockSpec((S5_TB, BAND), lambda j, b, t: (b * n_t + t, u_blk0 + j)),
                  pl.BlockSpec((BAND, blk), lambda j, b, t: (j, 0)), pl.BlockSpec((BAND, blk), lambda j, b, t: (j, 0)),
                  pl.BlockSpec((S5_LOG_TB, blk), lambda j, b, t: (0, j)), pl.BlockSpec((1, BAND), lambda j, b, t: (0, j))],
        out_specs=[pl.BlockSpec((S5_TB, blk), rows), pl.BlockSpec((S5_TB, BAND), rows), pl.BlockSpec((S5_TB, BAND), rows)],
        out_shape=[jax.ShapeDtypeStruct((n_rows, S5_NJ * blk), BF16), jax.ShapeDtypeStruct((n_rows, S5_WIDTH), F32),
                   jax.ShapeDtypeStruct((n_rows, S5_WIDTH), BF16)],
        scratch_shapes=[pltpu.VMEM((1, S5_LB), F32), pltpu.VMEM((1, S5_LB), F32), pltpu.VMEM((S5_TB, blk), F32)],
        compiler_params=_cparams(("parallel", "parallel", "arbitrary")),
    )(proj, bb_band, c_band, lam_pow, s5_d)


def _s5_bwd(dyc, s, proj, du5_a, bb_band, c_band, lam_pow_conj, dproj, n_seq, seq_len):
    n_t = seq_len // S5_TB
    blk = 2 * S5_LB
    halo_rows = 16
    halo_per_blk = S5_TB // halo_rows
    u_blk0 = OFF_U5 // BAND

    def rows(j, b, t):
        return (b * n_t + (n_t - 1 - t), j)

    def u_rows(j, b, t):
        return (b * n_t + (n_t - 1 - t), u_blk0 + j)

    def halo(j, b, t):
        return (jnp.maximum((b * n_t + (n_t - 1 - t)) * halo_per_blk - 1, 0), j)

    def body(dy_ref, sq_ref, hq_ref, u_ref, dua_ref, bb_ref, cb_ref, lp_ref, _, du_ref, dl_ref, dbb_ref, dcb_ref,
             cr, ci, buf, s_ref):
        b, t = pl.program_id(1), pl.program_id(2)
        sq = sq_ref[...]
        s_ref[...] = sq.astype(F32)
        h_last = hq_ref[...].astype(F32)[halo_rows - 1:halo_rows, :]
        dy = dy_ref[...]
        buf[...] = _dg(dy, cb_ref[...], 1, 0)

        @pl.when(t == 0)
        def _():
            cr[...] = jnp.zeros_like(cr)
            ci[...] = jnp.zeros_like(ci)

        @pl.when((b == 0) & (t == 0))
        def _():
            dl_ref[...] = jnp.zeros_like(dl_ref)
            dbb_ref[...] = jnp.zeros_like(dbb_ref)
            dcb_ref[...] = jnp.zeros_like(dcb_ref)

        dcb_ref[...] += _dg(dy, sq, 0, 0)

        first_blk = t == n_t - 1
        sub = lax.broadcasted_iota(jnp.int32, (8, S5_LB), 0)
        acc = [jnp.zeros((8, S5_LB), F32), jnp.zeros((8, S5_LB), F32)]

        def each(r, gr, gi):
            rows = slice(8 * r, 8 * r + 8)
            if r == 0:
                before_r = jnp.where(first_blk, 0.0, h_last[:, :S5_LB])
                before_i = jnp.where(first_blk, 0.0, h_last[:, S5_LB:])
            else:
                before_r, before_i = s_ref[8 * r - 1:8 * r, :S5_LB], s_ref[8 * r - 1:8 * r, S5_LB:]
            sp_r = jnp.where(sub == 0, before_r, pltpu.roll(s_ref[rows, :S5_LB], 1, 0))
            sp_i = jnp.where(sub == 0, before_i, pltpu.roll(s_ref[rows, S5_LB:], 1, 0))
            acc[0] = acc[0] + gr * sp_r + gi * sp_i
            acc[1] = acc[1] + gi * sp_r - gr * sp_i

        lp = lp_ref[...]
        edge_pow = jnp.concatenate([lp[7 - i:8 - i, :] for i in range(8)], axis=0)
        cr[...], ci[...] = _scan_block(buf, buf, lp, edge_pow, cr[...], ci[...], True, each)
        g = buf[...].astype(BF16)
        du_ref[...] = (dua_ref[...] + _dg(g, bb_ref[...], 1, 1)).astype(du_ref.dtype)
        dbb_ref[...] += _dg(u_ref[...], g, 0, 0)
        dl_ref[:, :S5_LB] += jnp.sum(acc[0], axis=0, keepdims=True)
        dl_ref[:, S5_LB:] += jnp.sum(acc[1], axis=0, keepdims=True)

    band = pl.BlockSpec((BAND, blk), lambda j, b, t: (j, 0))
    return pl.pallas_call(
        body, name="s5_bwd", grid=(S5_NJ, n_seq, n_t),
        in_specs=[pl.BlockSpec((S5_TB, BAND), rows), pl.BlockSpec((S5_TB, blk), rows),
                  pl.BlockSpec((halo_rows, blk), halo), pl.BlockSpec((S5_TB, BAND), u_rows),
                  pl.BlockSpec((S5_TB, BAND), rows), band, band,
                  pl.BlockSpec((S5_LOG_TB, blk), lambda j, b, t: (0, j)), pl.BlockSpec(memory_space=pl.ANY)],
        out_specs=[pl.BlockSpec((S5_TB, BAND), u_rows), pl.BlockSpec((1, blk), lambda j, b, t: (0, j)), band, band],
        out_shape=[jax.ShapeDtypeStruct(dproj.shape, dproj.dtype), jax.ShapeDtypeStruct((1, S5_NJ * blk), F32),
                   jax.ShapeDtypeStruct(bb_band.shape, F32), jax.ShapeDtypeStruct(c_band.shape, F32)],
        input_output_aliases={8: 0},
        scratch_shapes=[pltpu.VMEM((1, S5_LB), F32), pltpu.VMEM((1, S5_LB), F32), pltpu.VMEM((S5_TB, blk), F32),
                        pltpu.VMEM((S5_TB, blk), F32)],
        compiler_params=_cparams(("parallel", "arbitrary", "arbitrary")),
    )(dyc, s, s, proj, du5_a, bb_band, c_band, lam_pow_conj, dproj)


CONV_TR = 1024
CONV_CW = 512


def _shift_down(x, halo, k):
    if k == 0:
        return x
    row8 = lax.broadcasted_iota(jnp.int32, halo.shape, 0)
    rolled = pltpu.roll(x, k, 0)
    top = jnp.where(row8 < k, pltpu.roll(halo, k, 0), rolled[:8])
    if x.shape[0] == 8:
        return top
    return jnp.concatenate([top, rolled[8:]], axis=0)


def _shift_up(x, halo, k):
    if k == 0:
        return x
    n = x.shape[0]
    row8 = lax.broadcasted_iota(jnp.int32, halo.shape, 0)
    rolled = pltpu.roll(x, n - k, 0)
    bot = jnp.where(row8 >= 8 - k, pltpu.roll(halo, 8 - k, 0), rolled[n - 8:])
    if n == 8:
        return bot
    return jnp.concatenate([rolled[:n - 8], bot], axis=0)


def _conv_pre(x, halo, w, b):
    acc = b + w[SSD_CONV - 1:SSD_CONV, :] * x
    for k in range(SSD_CONV - 1):
        acc = acc + w[k:k + 1, :] * _shift_down(x, halo, SSD_CONV - 1 - k)
    return acc


def _conv_specs(seq_len, col_off):
    lt = seq_len // CONV_TR
    cb = col_off // CONV_CW
    cur = pl.BlockSpec((CONV_TR, CONV_CW), lambda j, i: (i, j + cb))
    prev = pl.BlockSpec((8, CONV_CW), lambda j, i: (jnp.maximum(i * (CONV_TR // 8) - 1, 0), j + cb))
    return lt, cur, prev


def _conv_fwd(proj, conv_w, conv_b, n_rows, seq_len):
    lt, cur, prev = _conv_specs(seq_len, OFF_XBC)

    def body(x_ref, h_ref, w_ref, b_ref, o_ref):
        halo = jnp.where(pl.program_id(1) % lt == 0, 0.0, h_ref[...])
        o_ref[...] = jax.nn.silu(_conv_pre(x_ref[...], halo, w_ref[...], b_ref[...]))

    return pl.pallas_call(
        body, name="ssd_conv_fwd", grid=(SSD_CONV_DIM // CONV_CW, n_rows // CONV_TR),
        in_specs=[cur, prev, pl.BlockSpec((SSD_CONV, CONV_CW), lambda j, i: (0, j)),
                  pl.BlockSpec((1, CONV_CW), lambda j, i: (0, j))],
        out_specs=pl.BlockSpec((CONV_TR, CONV_CW), lambda j, i: (i, j)),
        out_shape=jax.ShapeDtypeStruct((n_rows, SSD_CONV_DIM), F32),
        compiler_params=_cparams(("parallel", "parallel")),
    )(proj, proj, conv_w, conv_b)


def _conv_bwd(name, proj, d_act, conv_w, conv_b, n_rows, seq_len, col_off, dproj):
    width = d_act.shape[1]
    lt, cur, prev = _conv_specs(seq_len, OFF_XBC + col_off)
    n_blk = n_rows // CONV_TR
    cb = (OFF_XBC + col_off) // CONV_CW
    pb = col_off // CONV_CW
    nxt = pl.BlockSpec((8, CONV_CW), lambda j, i: (jnp.minimum((i + 1) * (CONV_TR // 8), n_rows // 8 - 1), j + cb))
    d_cur = pl.BlockSpec((CONV_TR, CONV_CW), lambda j, i: (i, j))
    d_nxt = pl.BlockSpec((8, CONV_CW), lambda j, i: (jnp.minimum((i + 1) * (CONV_TR // 8), n_rows // 8 - 1), j))

    def dsilu(pre):
        sg = jax.nn.sigmoid(pre)
        return sg * (1.0 + pre * (1.0 - sg))

    def body(x_ref, hp_ref, hn_ref, d_ref, dn_ref, w_ref, b_ref, _, dx_ref, dw_ref, db_ref):
        i = pl.program_id(1)
        x, w, b = x_ref[...], w_ref[...], b_ref[...]
        halo_p = jnp.where(i % lt == 0, 0.0, hp_ref[...])
        at_end = i % lt == lt - 1
        dpre = d_ref[...] * dsilu(_conv_pre(x, halo_p, w, b))
        pre_n = _conv_pre(hn_ref[...], x[CONV_TR - 8:, :], w, b)
        dpre_n = jnp.where(at_end, 0.0, dn_ref[...] * dsilu(pre_n))
        dx = w[SSD_CONV - 1:SSD_CONV, :] * dpre
        for k in range(SSD_CONV - 1):
            dx = dx + w[k:k + 1, :] * _shift_up(dpre, dpre_n, SSD_CONV - 1 - k)
        dx_ref[...] = dx.astype(dx_ref.dtype)

        @pl.when(i == 0)
        def _():
            dw_ref[...] = jnp.zeros_like(dw_ref)
            db_ref[...] = jnp.zeros_like(db_ref)

        for k in range(SSD_CONV):
            xs = _shift_down(x, halo_p, SSD_CONV - 1 - k)
            dw_ref[k:k + 1, :] += jnp.sum(dpre * xs, axis=0, keepdims=True)
        db_ref[...] += jnp.sum(dpre, axis=0, keepdims=True)

    return pl.pallas_call(
        body, name=name, grid=(width // CONV_CW, n_blk),
        in_specs=[cur, prev, nxt, d_cur, d_nxt,
                  pl.BlockSpec((SSD_CONV, CONV_CW), lambda j, i: (0, j + pb)),
                  pl.BlockSpec((1, CONV_CW), lambda j, i: (0, j + pb)), pl.BlockSpec(memory_space=pl.ANY)],
        out_specs=[pl.BlockSpec((CONV_TR, CONV_CW), lambda j, i: (i, j + cb)),
                   pl.BlockSpec((SSD_CONV, CONV_CW), lambda j, i: (0, j)),
                   pl.BlockSpec((1, CONV_CW), lambda j, i: (0, j))],
        out_shape=[jax.ShapeDtypeStruct(dproj.shape, dproj.dtype),
                   jax.ShapeDtypeStruct((SSD_CONV, width), F32), jax.ShapeDtypeStruct((1, width), F32)],
        input_output_aliases={7: 0},
        compiler_params=_cparams(("parallel", "arbitrary")),
    )(proj, proj, proj, d_act, d_act, conv_w, conv_b, dproj)


def _split3(x):
    hi = x.astype(BF16)
    r = x - hi.astype(F32)
    mid = r.astype(BF16)
    return hi, mid, (r - mid.astype(F32)).astype(BF16)


def _sel_dot(a, b, a_is_sel):
    dn = (((1,), (0,)), ((), ()))
    if a_is_sel:
        return sum(lax.dot_general(a, t, dn, preferred_element_type=F32) for t in _split3(b))
    return sum(lax.dot_general(t, b, dn, preferred_element_type=F32) for t in _split3(a))


@jax.custom_vjp
def sel_left(sel, sel_t, x):
    return _sel_dot(sel, x, True)


@jax.custom_vjp
def sel_right(x, sel, sel_t):
    return _sel_dot(x, sel, False)


sel_left.defvjp(lambda s, st, x: (_sel_dot(s, x, True), (s, st)),
                lambda r, g: (jnp.zeros_like(r[0]), jnp.zeros_like(r[1]), _sel_dot(r[1], g, True)))
sel_right.defvjp(lambda x, s, st: (_sel_dot(x, s, False), (s, st)),
                 lambda r, g: (_sel_dot(g, r[1], False), jnp.zeros_like(r[0]), jnp.zeros_like(r[1])))


def _ssd_chunk(xs, bm, cm, dtr, st, dtb, alog, dsk, k):
    dt = jax.nn.softplus(dtr + dtb)
    acum = sel_left(k["tri"], k["tri_t"], dt * (-jnp.exp(alog)))
    dt_e = sel_right(dt, k["spread"], k["spread_t"])
    ac_e = sel_right(acum, k["spread"], k["spread_t"])
    al_e = ac_e[SSD_CHUNK - 1:SSD_CHUNK, :]
    dsk_e = sel_right(jnp.broadcast_to(dsk, (8, 128)), k["spread"], k["spread_t"])[0:1, :]
    xdt = xs * dt_e
    acum_t = acum.T
    scores = dot_nt(cm, bm)
    y = dot_nn(cm, st) * jnp.exp(ac_e) + xs * dsk_e
    for j in range(SSD_HPG):
        lmat = jnp.exp(jnp.where(k["causal"], acum[:, j:j + 1] - acum_t[j:j + 1, :], -jnp.inf))
        y = y + dot_nn(scores * lmat, jnp.where(k["head"] == j, xdt, 0.0))
    new = st * jnp.exp(al_e) + dot_tn(bm, xdt * jnp.exp(al_e - ac_e))
    return y, new


def _ssd_consts():
    r = lax.broadcasted_iota(jnp.int32, (SSD_CHUNK, SSD_CHUNK), 0)
    c = lax.broadcasted_iota(jnp.int32, (SSD_CHUNK, SSD_CHUNK), 1)
    hd = jnp.int32(SSD_HEADDIM)
    sr = lax.broadcasted_iota(jnp.int32, (128, GROUP_W), 0)
    sc = lax.div(lax.broadcasted_iota(jnp.int32, (128, GROUP_W), 1), hd)
    tr = lax.div(lax.broadcasted_iota(jnp.int32, (GROUP_W, 128), 0), hd)
    tc = lax.broadcasted_iota(jnp.int32, (GROUP_W, 128), 1)
    return {"tri": (r >= c).astype(BF16), "tri_t": (c >= r).astype(BF16), "causal": r >= c,
            "spread": (sr == sc).astype(BF16), "spread_t": (tr == tc).astype(BF16),
            "head": lax.div(lax.broadcasted_iota(jnp.int32, (SSD_CHUNK, GROUP_W), 1), hd)}


def _ssd_chunk_gated(xs, bm, cm, dtr, st, dtb, alog, dsk, z, nw, k):
    y, new = _ssd_chunk(xs, bm, cm, dtr, st, dtb, alog, dsk, k)
    yg = y * jax.nn.silu(z)
    return yg * lax.rsqrt(jnp.mean(yg * yg, axis=-1, keepdims=True) + RMS_EPS) * nw, new


def _ssd_fwd(xbc_act, proj, dtb, alog, dsk, norm_w, n_seq, seq_len):
    n_c = seq_len // SSD_CHUNK
    gps = 2
    xbc3 = xbc_act.reshape(n_seq, seq_len, SSD_CONV_DIM)
    proj3 = proj.reshape(n_seq, seq_len, PROJ_W)

    def rows(w, first=0):
        return pl.BlockSpec((n_seq, SSD_CHUNK, gps * w), lambda g, c: (0, c, first + g))

    def par(w):
        return pl.BlockSpec((1, gps * w), lambda g, c: (0, g))

    def body(xs_ref, bm_ref, cm_ref, dt_ref, dtb_ref, al_ref, dk_ref, z_ref, nw_ref, y_ref, st_ref, state):
        @pl.when(pl.program_id(1) == 0)
        def _():
            state[...] = jnp.zeros_like(state)

        consts = _ssd_consts()
        for i in range(gps):
            wide, narrow = slice(i * GROUP_W, (i + 1) * GROUP_W), slice(i * 128, (i + 1) * 128)
            for b in range(n_seq):
                prev = state[b, i]
                st_ref[b, 0, i] = prev
                y, new = _ssd_chunk_gated(xs_ref[b, :, wide], bm_ref[b, :, narrow], cm_ref[b, :, narrow],
                                          dt_ref[b, :, narrow], prev, dtb_ref[:, narrow], al_ref[:, narrow],
                                          dk_ref[:, narrow], z_ref[b, :, wide], nw_ref[:, wide], consts)
                y_ref[b, :, wide] = y.astype(y_ref.dtype)
                state[b, i] = new

    y3, states = pl.pallas_call(
        body, name="ssd_fwd", grid=(SSD_GROUPS // gps, n_c),
        in_specs=[rows(GROUP_W), rows(SSD_STATE, SSD_WIDTH // (gps * SSD_STATE)),
                  rows(SSD_STATE, (SSD_WIDTH + SSD_BC) // (gps * SSD_STATE)), rows(128, OFF_DT // (gps * 128)),
                  par(128), par(128), par(128), rows(GROUP_W, OFF_ZS // (gps * GROUP_W)), par(GROUP_W)],
        out_specs=[rows(GROUP_W),
                   pl.BlockSpec((n_seq, 1, gps, SSD_STATE, GROUP_W), lambda g, c: (0, c, g, 0, 0))],
        out_shape=[jax.ShapeDtypeStruct((n_seq, seq_len, SSD_WIDTH), BF16),
                   jax.ShapeDtypeStruct((n_seq, n_c, SSD_GROUPS, SSD_STATE, GROUP_W), F32)],
        scratch_shapes=[pltpu.VMEM((n_seq, gps, SSD_STATE, GROUP_W), F32)],
        compiler_params=_cparams(("parallel", "arbitrary")),
    )(xbc3, xbc3, xbc3, proj3, dtb, alog, dsk, proj3, norm_w)
    return y3.reshape(n_seq * seq_len, SSD_WIDTH), states


def _ssd_bwd(xbc_act, proj, states, dy, dtb, alog, dsk, norm_w, n_seq, seq_len, dproj):
    n_c = seq_len // SSD_CHUNK
    n_rows = n_seq * seq_len
    gps = 2
    xbc3 = xbc_act.reshape(n_seq, seq_len, SSD_CONV_DIM)
    proj3 = proj.reshape(n_seq, seq_len, PROJ_W)

    def rows(w, first=0):
        return pl.BlockSpec((n_seq, SSD_CHUNK, gps * w), lambda g, c: (0, n_c - 1 - c, first + g))

    def par(w):
        return pl.BlockSpec((1, gps * w), lambda g, c: (0, g))

    def body(xs_ref, bm_ref, cm_ref, dt_ref, st_ref, dy_ref, dtb_ref, al_ref, dk_ref, z_ref, nw_ref, _,
             dxs_ref, dbm_ref, dcm_ref, dz_ref, ddt_ref, ddtb_ref, dal_ref, ddk_ref, dnw_ref, dstate):
        @pl.when(pl.program_id(1) == 0)
        def _():
            dstate[...] = jnp.zeros_like(dstate)
            for ref in (ddtb_ref, dal_ref, ddk_ref, dnw_ref):
                ref[...] = jnp.zeros_like(ref)

        consts = _ssd_consts()
        chains = []
        for i in range(gps):
            wide, narrow = slice(i * GROUP_W, (i + 1) * GROUP_W), slice(i * 128, (i + 1) * 128)
            for b in range(n_seq):
                _, vjp = jax.vjp(
                    lambda *a: _ssd_chunk_gated(*a, consts),
                    xs_ref[b, :, wide], bm_ref[b, :, narrow], cm_ref[b, :, narrow], dt_ref[b, :, narrow],
                    st_ref[b, 0, i], dtb_ref[:, narrow], al_ref[:, narrow], dk_ref[:, narrow], z_ref[b, :, wide],
                    nw_ref[:, wide])
                chains.append((i, b, wide, narrow, vjp))
        for i, b, wide, narrow, vjp in chains:
            dxs, dbm, dcm, ddtr, dprev, ddtb, dal, ddk, dz, dnw = vjp((dy_ref[b, :, wide], dstate[b, i]))
            dxs_ref[b, :, wide] = dxs
            dbm_ref[b, :, narrow] = dbm
            dcm_ref[b, :, narrow] = dcm
            dz_ref[b, :, wide] = dz.astype(dz_ref.dtype)
            ddt_ref[b, :, narrow] = ddtr.astype(ddt_ref.dtype)
            ddtb_ref[:, narrow] += ddtb
            dal_ref[:, narrow] += dal
            ddk_ref[:, narrow] += ddk
            dnw_ref[:, wide] += dnw
            dstate[b, i] = dprev

    res = pl.pallas_call(
        body, name="ssd_bwd", grid=(SSD_GROUPS // gps, n_c),
        in_specs=[rows(GROUP_W), rows(SSD_STATE, SSD_WIDTH // (gps * SSD_STATE)),
                  rows(SSD_STATE, (SSD_WIDTH + SSD_BC) // (gps * SSD_STATE)), rows(128, OFF_DT // (gps * 128)),
                  pl.BlockSpec((n_seq, 1, gps, SSD_STATE, GROUP_W), lambda g, c: (0, n_c - 1 - c, g, 0, 0)),
                  rows(GROUP_W), par(128), par(128), par(128), rows(GROUP_W, OFF_ZS // (gps * GROUP_W)), par(GROUP_W),
                  pl.BlockSpec(memory_space=pl.ANY)],
        out_specs=[rows(GROUP_W), rows(SSD_STATE), rows(SSD_STATE), rows(GROUP_W, OFF_ZS // (gps * GROUP_W)), rows(128),
                   par(128), par(128), par(128), par(GROUP_W)],
        out_shape=[jax.ShapeDtypeStruct((n_seq, seq_len, SSD_WIDTH), F32), jax.ShapeDtypeStruct((n_seq, seq_len, SSD_BC), F32),
                   jax.ShapeDtypeStruct((n_seq, seq_len, SSD_BC), F32),
                   jax.ShapeDtypeStruct((n_seq, seq_len, PROJ_W), dproj.dtype),
                   jax.ShapeDtypeStruct((n_seq, seq_len, DT_W), BF16),
                   jax.ShapeDtypeStruct((1, 512), F32), jax.ShapeDtypeStruct((1, 512), F32),
                   jax.ShapeDtypeStruct((1, 512), F32), jax.ShapeDtypeStruct((1, SSD_WIDTH), F32)],
        input_output_aliases={11: 3},
        scratch_shapes=[pltpu.VMEM((n_seq, gps, SSD_STATE, GROUP_W), F32)],
        compiler_params=_cparams(("parallel", "arbitrary")),
    )(xbc3, xbc3, xbc3, proj3, states, dy.reshape(n_seq, seq_len, SSD_WIDTH), dtb, alog, dsk, proj3, norm_w,
      dproj.reshape(n_seq, seq_len, PROJ_W))
    flat = [r.reshape(n_rows, r.shape[-1]) for r in res[:5]]
    return (*flat, *res[5:])


def _pad_heads(v):
    return jnp.pad(v.reshape(SSD_GROUPS, SSD_HPG), ((0, 0), (0, 128 - SSD_HPG))).reshape(1, SSD_GROUPS * 128)


def _unpad_heads(v):
    return v.reshape(SSD_GROUPS, 128)[:, :SSD_HPG].reshape(1, SSD_HEADS)


def _state_cols(v):
    re, im = v
    lead = re.shape[:-1]
    re = re.reshape(lead + (S5_NJ, 1, S5_LB))
    im = im.reshape(lead + (S5_NJ, 1, S5_LB))
    return jnp.concatenate([re, im], axis=-2).reshape(lead + (2 * S5_N,))


def _state_uncols(v):
    lead = v.shape[:-1]
    v = v.reshape(lead + (S5_NJ, 2, S5_LB))
    return v[..., 0, :].reshape(lead + (S5_N,)), v[..., 1, :].reshape(lead + (S5_N,))


GROUPS_PER_BAND = BAND // S5_GROUP


def _band(w2_re, w2_im):
    gh = S5_GROUPS * S5_GROUP
    rg = ((jnp.arange(gh) // S5_GROUP) % GROUPS_PER_BAND)[:, None, None]
    cg = jnp.arange(GROUPS_PER_BAND)[None, :, None]
    parts = [jnp.where(rg == cg, v[:, None, :], 0.0).reshape(gh, S5_LB) for v in (w2_re, w2_im)]
    return jnp.concatenate(parts, axis=1)


def _band_take(wb):
    gh = S5_GROUPS * S5_GROUP
    w4 = wb.reshape(gh, 2, GROUPS_PER_BAND, S5_STATE)
    sel = w4[jnp.arange(gh), :, (jnp.arange(gh) // S5_GROUP) % GROUPS_PER_BAND, :]
    return sel[:, 0, :], sel[:, 1, :]


W_IN_SHARD = IN_PROJ_DIM // N_CHIPS
W_IN_SEGS = ((0, 512, OFF_U5), (512, 1024, OFF_Z5), (1024, 2560, OFF_ZS), (2560, 5120, OFF_XBC), (5144, 7192, OFF_G5))
DT_ROWS = (5120, 5144)


def _w_in_pieces():
    runs = []
    segs = list(W_IN_SEGS) + [(DT_ROWS[0] + SSD_HPG * g, DT_ROWS[0] + SSD_HPG * (g + 1), OFF_DT + 128 * g)
                              for g in range(SSD_GROUPS)]
    for lo, hi, off in segs:
        for j in range(N_CHIPS):
            s, e = max(lo, j * W_IN_SHARD), min(hi, (j + 1) * W_IN_SHARD)
            if s < e:
                runs.append((j, s - j * W_IN_SHARD, off + s - lo, e - s))
    return runs


RELAYOUT_LANES = 256


def _pad_w_in_t(a4):
    runs = _w_in_pieces()

    def body(a_ref, o_ref):
        o_ref[pl.ds(OFF_DT, DT_W), :] = jnp.zeros((DT_W, RELAYOUT_LANES), o_ref.dtype)
        for j, src, dst, n in runs:
            o_ref[pl.ds(dst, n), :] = a_ref[j, pl.ds(src, n), :]

    return pl.pallas_call(
        body, name="w_in_to_padded", grid=(D_MODEL // RELAYOUT_LANES,),
        in_specs=[pl.BlockSpec((N_CHIPS, W_IN_SHARD, RELAYOUT_LANES), lambda i: (0, 0, i))],
        out_specs=pl.BlockSpec((PROJ_W, RELAYOUT_LANES), lambda i: (0, i)),
        out_shape=jax.ShapeDtypeStruct((PROJ_W, D_MODEL), a4.dtype),
        compiler_params=_cparams(("parallel",)),
    )(a4)


def _unpad_w_in_t(wp):
    runs = _w_in_pieces()

    def body(p_ref, o_ref):
        for j, dst, src, n in runs:
            o_ref[j, pl.ds(dst, n), :] = p_ref[pl.ds(src, n), :]

    return pl.pallas_call(
        body, name="w_in_from_padded", grid=(D_MODEL // RELAYOUT_LANES,),
        in_specs=[pl.BlockSpec((PROJ_W, RELAYOUT_LANES), lambda i: (0, i))],
        out_specs=pl.BlockSpec((N_CHIPS, W_IN_SHARD, RELAYOUT_LANES), lambda i: (0, 0, i)),
        out_shape=jax.ShapeDtypeStruct((N_CHIPS, W_IN_SHARD, D_MODEL), wp.dtype),
        compiler_params=_cparams(("parallel",)),
    )(wp)


def _local_step(x, p, tgt, w):
    n_seq, seq_len, _ = x.shape
    n_rows = n_seq * seq_len
    tr = 512
    x2 = x.reshape(n_rows, D_MODEL)
    p2 = p.reshape(n_rows, PLE_DIM)
    t2 = tgt.reshape(n_rows, D_MODEL)
    row = functools.partial(_rowwise, n_rows=n_rows, tr=tr)

    w_pad_t = _pad_w_in_t(w["w_in_t"])
    norm_w = w["norm_w"].reshape(1, D_MODEL)
    ple_norm_w = w["ple_norm_w"].reshape(1, D_MODEL)
    final_w = w["final_norm_w"].reshape(1, D_MODEL)
    s5_d = w["s5_d"].reshape(1, S5_WIDTH)
    b_glu = w["s5_b_glu"].reshape(1, S5_WIDTH)
    conv_w = w["ssd_conv_w"].reshape(SSD_CONV, SSD_CONV_DIM)
    conv_b = w["ssd_conv_b"].reshape(1, SSD_CONV_DIM)
    ssd_norm_w = w["ssd_norm_w"].reshape(1, SSD_WIDTH)
    dtb, alog, dsk = (_pad_heads(w[k].reshape(1, SSD_HEADS)) for k in ("ssd_dt_bias", "ssd_a_log", "ssd_d"))

    gh = S5_GROUPS * S5_GROUP
    a_re = w["s5_a_re"].reshape(S5_GROUPS, S5_STATE)
    a_im = w["s5_a_im"].reshape(S5_GROUPS, S5_STATE)
    log_step = w["s5_log_step"].reshape(S5_GROUPS, 1)
    b_re2 = jnp.transpose(w["s5_b_re"].reshape(S5_GROUPS, S5_STATE, S5_GROUP), (0, 2, 1)).reshape(gh, S5_STATE)
    b_im2 = jnp.transpose(w["s5_b_im"].reshape(S5_GROUPS, S5_STATE, S5_GROUP), (0, 2, 1)).reshape(gh, S5_STATE)
    expand = (jnp.arange(gh)[:, None] // S5_GROUP == jnp.arange(S5_GROUPS)[None, :]).astype(F32)
    pow_re, pow_im, bb_re2, bb_im2 = _s5_params_fwd(a_re, a_im, log_step, b_re2, b_im2, expand)
    lam_pow = _state_cols((pow_re.reshape(S5_LOG_TB, S5_N), pow_im.reshape(S5_LOG_TB, S5_N)))
    lam_pow_conj = _state_cols((pow_re.reshape(S5_LOG_TB, S5_N), -pow_im.reshape(S5_LOG_TB, S5_N)))
    bb_band = _band(bb_re2, bb_im2).astype(BF16)
    c_band = _band(w["s5_c_re"].reshape(gh, S5_STATE), -w["s5_c_im"].reshape(gh, S5_STATE)).astype(BF16)

    late = w.get("_late")
    (hn,) = row("rms_in", lambda r, q: ([_rms(r[0], q[0])], []), row_ins=[(x2, 0, D_MODEL)],
                par_ins=[(norm_w, 0, D_MODEL)] + ([(late[0], 0, 128)] if late else []), row_outs=[(D_MODEL, BF16)])
    proj = _matmul("mm_proj", hn, w_pad_t, tb=True)
    s, yc, ge = _s5_fwd(proj, bb_band, c_band, lam_pow, s5_d, n_seq, seq_len)
    if late:
        w = {**w, **late[1](ge)}
    tg, y5 = _matmul("mm_s5_glu", ge, w["s5_w_glu"], epilogue=lambda c, r, q: [c, _s5_out(r[0], r[1], r[2], c, *q)],
                     epi_rows=[(yc, 0), (proj, OFF_U5), (proj, OFF_Z5)], epi_pars=[s5_d, b_glu], epi_outs=[F32, BF16])

    xbc_act = _conv_fwd(proj, conv_w, conv_b, n_rows, seq_len)
    yss, states = _ssd_fwd(xbc_act, proj, dtb, alog, dsk, ssd_norm_w, n_seq, seq_len)

    m5 = _matmul("mm_br_s5", y5, w["w_br_s5"])
    ms, merged = _matmul("mm_br_ssd", yss, w["w_br_ssd"], epilogue=lambda c, r, q: [c, _merge(r[0], r[1], r[2], c)],
                         epi_rows=[(proj, OFF_G5), (proj, OFF_GS), (m5, 0)], epi_outs=[F32, BF16])

    def resid_norm(c, r, q):
        h1_ = r[0] + c
        return [h1_, _rms(h1_, q[0])]

    h1, hp = _matmul("mm_out", merged, w["w_out"], epilogue=resid_norm, epi_rows=[(x2, 0)], epi_pars=[ple_norm_w],
                     epi_outs=[F32, BF16], full_rows=True)
    pp = _matmul("mm_ple_proj", p2, w["w_ple_proj"])

    def head_fn(pgl_, r, q):
        h1_, pp_, tgt_ = r
        loss, vjp = jax.vjp(lambda a, b, c, f: _head_loss(a, b, c, f, tgt_), h1_, pgl_, pp_, q[0])
        dh1_, dpgl_, dpp_, dfw_ = vjp(jnp.ones_like(loss))
        return [dh1_, dpgl_, dpp_], [loss, dfw_]

    dh2, dpgl, dpp, loss_acc, d_final_w = _matmul(
        "mm_ple_gate_head", hp, w["w_ple_gate"], epilogue=head_fn, epi_rows=[(h1, 0), (pp, 0), (t2, 0)],
        epi_pars=[final_w], epi_outs=[F32, BF16, BF16], epi_accs=[(1, 128), (1, D_MODEL)], full_rows=True)
    loss = loss_acc[0, 0]

    g = {}
    g["final_norm_w"] = d_final_w
    g["w_ple_gate"] = _matmul("mm_d_w_ple_gate", hp, dpgl, ta=True)
    g["w_ple_proj"] = _matmul("mm_d_w_ple_proj", p2, dpp, ta=True)

    def ple_norm_bwd(dhp_, r, q):
        h1_, dh2_ = r
        _, vjp = jax.vjp(_rms, h1_, q[0])
        dh, dw = vjp(dhp_)
        dh = dh + dh2_
        return [dh, dh], [dw]

    dh1, dh1_b, g["ple_norm_w"] = _matmul(
        "mm_d_hp", dpgl, w["w_ple_gate"], tb=True, epilogue=ple_norm_bwd, epi_rows=[(h1, 0), (dh2, 0)],
        epi_pars=[ple_norm_w], epi_outs=[F32, BF16], epi_accs=[(1, D_MODEL)], full_rows=True)
    g["w_out"] = _matmul("mm_d_w_out", merged, dh1_b, ta=True)

    dproj = lax.empty((n_rows, PROJ_W), BF16)

    def merge_bwd(dmerged, r, q):
        sg5, sgs = jax.nn.sigmoid(r[0]), jax.nn.sigmoid(r[1])
        d_gates = jnp.concatenate([dmerged * r[2] * sg5 * (1.0 - sg5), dmerged * r[3] * sgs * (1.0 - sgs)], axis=1)
        return [d_gates, dmerged * sg5, dmerged * sgs]

    dproj, dm5, dms = _matmul(
        "mm_d_merged", dh1_b, w["w_out"], tb=True, epilogue=merge_bwd,
        epi_rows=[(proj, OFF_G5), (proj, OFF_GS), (m5, 0), (ms, 0)], epi_outs=[BF16, BF16, BF16], full_rows=True,
        epi_into=(dproj, OFF_G5, 2 * D_MODEL))
    g["w_br_s5"] = _matmul("mm_d_w_br_s5", y5, dm5, ta=True)
    g["w_br_ssd"] = _matmul("mm_d_w_br_ssd", yss, dms, ta=True)
    dyss = _matmul("mm_d_yss", dms, w["w_br_ssd"], tb=True)

    def s5_out_bwd_a(dy_, r, q):
        yc_, u_, z_, t_ = r
        d_, bg_ = q
        ge_ = jax.nn.gelu(yc_ + d_ * u_)
        _, vjp = jax.vjp(lambda a, z, t, b: a * jax.nn.sigmoid(t + b) * jax.nn.silu(z), ge_, z_, t_, bg_)
        dge, dz, dt_, dbg = vjp(dy_)
        return [dz, dge, dt_], [dbg]

    dproj, dge_a, dtg, g["s5_b_glu"] = _matmul(
        "mm_d_y5", dm5, w["w_br_s5"], tb=True, epilogue=s5_out_bwd_a,
        epi_rows=[(yc, 0), (proj, OFF_U5), (proj, OFF_Z5), (tg, 0)], epi_pars=[s5_d, b_glu],
        epi_outs=[BF16, F32, BF16], epi_accs=[(1, S5_WIDTH)], full_rows=True, epi_into=(dproj, OFF_Z5, S5_WIDTH))
    g["s5_w_glu"] = _matmul("mm_d_w_glu", ge, dtg, ta=True)
    early = w["_early"](g) if "_early" in w else None

    def s5_out_bwd_b(dge_b, r, q):
        yc_, u_, da_ = r
        _, vjp = jax.vjp(lambda yc, u, d: jax.nn.gelu(yc + d * u), yc_, u_, q[0])
        dyc_, du_, dd_ = vjp(da_ + dge_b)
        return [dyc_, du_], [dd_]

    dyc, du5_a, g["s5_d"] = _matmul(
        "mm_d_ge", dtg, w["s5_w_glu"], tb=True, epilogue=s5_out_bwd_b,
        epi_rows=[(yc, 0), (proj, OFF_U5), (dge_a, 0)], epi_pars=[s5_d], epi_outs=[BF16, F32],
        epi_accs=[(1, S5_WIDTH)], full_rows=True, dep=early)
    dproj, d_lam, d_bb_band, d_c_band = _s5_bwd(dyc, s, proj, du5_a, bb_band, c_band, lam_pow_conj, dproj,
                                                n_seq, seq_len)

    d_lr, d_li = _state_uncols(d_lam)
    d_bbr, d_bbi = _band_take(d_bb_band)
    d_are, d_aim, d_ls, d_br2, d_bi2 = _s5_params_bwd(
        a_re, a_im, log_step, b_re2, b_im2, expand,
        d_lr.reshape(S5_GROUPS, S5_STATE), d_li.reshape(S5_GROUPS, S5_STATE), d_bbr, d_bbi)
    g["s5_a_re"], g["s5_a_im"], g["s5_log_step"] = d_are, d_aim, d_ls
    g["s5_b_re_ghp"], g["s5_b_im_ghp"] = d_br2, d_bi2
    d_cr, d_ci = _band_take(d_c_band)
    g["s5_c_re"], g["s5_c_im"] = d_cr, -d_ci

    dxs, dbm, dcm, dproj, ddt, d_dtb, d_alog, d_dsk, g["ssd_norm_w"] = _ssd_bwd(
        xbc_act, proj, states, dyss, dtb, alog, dsk, ssd_norm_w, n_seq, seq_len, dproj)
    (dproj,) = row("ssd_ddt", lambda r, q: ([r[0]], []), row_ins=[(ddt, 0, DT_W)], par_ins=[],
                   row_outs=[(DT_W, BF16)], into=(dproj, OFF_DT))
    g["ssd_dt_bias"], g["ssd_a_log"], g["ssd_d"] = _unpad_heads(d_dtb), _unpad_heads(d_alog), _unpad_heads(d_dsk)
    conv_dw, conv_db = [], []
    for nm, d_act, off in (("x", dxs, 0), ("b", dbm, SSD_WIDTH), ("c", dcm, SSD_WIDTH + SSD_BC)):
        dproj, dw_, db_ = _conv_bwd("ssd_conv_bwd_" + nm, proj, d_act, conv_w, conv_b, n_rows, seq_len, off, dproj)
        conv_dw.append(dw_)
        conv_db.append(db_)
    g["ssd_conv_w"] = jnp.concatenate(conv_dw, axis=1)
    g["ssd_conv_b"] = jnp.concatenate(conv_db, axis=1)

    g["w_in_t"] = _unpad_w_in_t(_matmul("mm_d_w_in", dproj, hn, ta=True))
    def norm_bwd(dhn_, r, q):
        x_, dh1_ = r
        _, vjp = jax.vjp(_rms, x_, q[0])
        dx_, dw_ = vjp(dhn_)
        return [dx_ + dh1_], [dw_]

    sent = w["_w_in_ready"](g, loss) if "_w_in_ready" in w else None
    dx, g["norm_w"] = _matmul("mm_d_hn", dproj, w_pad_t, epilogue=norm_bwd, epi_rows=[(x2, 0), (dh1, 0)],
                              epi_pars=[norm_w], epi_outs=[F32], epi_accs=[(1, D_MODEL)], full_rows=True, dep=sent)
    return loss, dx.reshape(x.shape), g


HBM = pl.BlockSpec(memory_space=pltpu.HBM)


def _chip_index(x, y):
    return 2 * x + y


def _half(shape2d, axis, which):
    h = shape2d[axis] // 2
    sl = pl.ds(pl.multiple_of(which * h, 128 if axis else 8), h)
    return (slice(None), sl) if axis else (sl, slice(None))


def _gather_chips(split, axes, whole):
    ns, nw = len(split), len(whole)
    n = ns + nw

    def body(*refs):
        ins, outs = refs[:n], refs[n:2 * n]
        ici_send, ici_recv, d2d_send, d2d_recv, local_sems = refs[2 * n:]
        x, y, c = lax.axis_index("x"), lax.axis_index("y"), lax.axis_index("c")
        me = _chip_index(x, y)
        sibling = (x, y, 1 - c)
        peers = [(1 - x, y), (x, 1 - y), (1 - x, 1 - y)]

        def half(t, which):
            return _half(split[t].shape, axes[t], which)

        copies = []
        for t in range(n):
            loc = pltpu.make_async_copy(ins[t], outs[t].at[me], local_sems.at[t])
            loc.start()
            copies.append(loc)

        def ici(t, k, slot):
            px, py = peers[k]
            if t < ns:
                src, dst = ins[t].at[half(t, c)], outs[t].at[(slot,) + half(t, c)]
            else:
                src, dst = ins[t], outs[t].at[slot]
            return pltpu.make_async_remote_copy(src_ref=src, dst_ref=dst, send_sem=ici_send.at[t, k],
                                                recv_sem=ici_recv.at[t, k], device_id=(px, py, c), device_id_type=MESH)

        def d2d(t, k, which):
            rows = outs[t].at[(_chip_index(*peers[k]),) + half(t, which)]
            return pltpu.make_async_remote_copy(src_ref=rows, dst_ref=rows, send_sem=d2d_send.at[t, k],
                                                recv_sem=d2d_recv.at[t, k], device_id=sibling, device_id_type=MESH)

        sends = []
        for t in range(n):
            for k in range(3):
                cp = ici(t, k, me)
                cp.start()
                sends.append(cp)
        for t in range(n):
            for k in range(3):
                ici(t, k, _chip_index(*peers[k])).wait_recv()
                if t < ns:
                    cp = d2d(t, k, c)
                    cp.start()
                    sends.append(cp)
        for t in range(ns):
            for k in range(3):
                d2d(t, k, 1 - c).wait_recv()
        for cp in sends:
            cp.wait_send()
        for cp in copies:
            cp.wait()

    arrays = list(split) + list(whole)
    return pl.pallas_call(
        body, name="gather_weights",
        in_specs=[HBM] * n, out_specs=[HBM] * n,
        out_shape=[jax.ShapeDtypeStruct((N_CHIPS,) + a.shape, a.dtype) for a in arrays],
        scratch_shapes=[pltpu.SemaphoreType.DMA((n, 3)), pltpu.SemaphoreType.DMA((n, 3)),
                        pltpu.SemaphoreType.DMA((ns, 3)), pltpu.SemaphoreType.DMA((ns, 3)),
                        pltpu.SemaphoreType.DMA((n,))],
    )(*arrays)


SEM = pl.BlockSpec(memory_space=pltpu.SEMAPHORE)
DATAFLOW = pltpu.SideEffectType.DATAFLOW_SIDE_EFFECTING


def _gather_start(shards, after):
    n = len(shards)

    def body(*refs):
        ins, lands = refs[:n], refs[n:2 * n]
        send_sems, recv_sems = refs[2 * n + 1], refs[2 * n + 2]
        token = refs[-1]
        x, y, c = lax.axis_index("x"), lax.axis_index("y"), lax.axis_index("c")
        me = _chip_index(x, y)
        for t in range(n):
            for k, (px, py) in enumerate([(1 - x, y), (x, 1 - y), (1 - x, 1 - y)]):
                pltpu.make_async_remote_copy(
                    src_ref=ins[t], dst_ref=lands[t].at[me], send_sem=send_sems.at[3 * t + k],
                    recv_sem=recv_sems.at[3 * t + k],
                    device_id=(px, py, c), device_id_type=MESH).start()
        token[...] = jnp.zeros_like(token)

    zones = [lax.empty((N_CHIPS,) + a.shape, a.dtype) for a in shards]
    res = pl.pallas_call(
        body, name="gather_rest_start",
        out_shape=(pltpu.SemaphoreType.DMA((3 * n,)), pltpu.SemaphoreType.DMA((3 * n,)),
                   *[pltpu.HBM(a.shape, a.dtype) for a in shards], *[pltpu.HBM(z.shape, z.dtype) for z in zones],
                   jax.ShapeDtypeStruct((8, 128), F32)),
        in_specs=[HBM] * (2 * n) + [pl.BlockSpec(memory_space=pl.ANY)],
        out_specs=(SEM, SEM, *[HBM] * (2 * n), pl.BlockSpec(memory_space=pltpu.VMEM)),
        input_output_aliases={t: 2 + t for t in range(2 * n)},
        compiler_params=pltpu.CompilerParams(has_side_effects=DATAFLOW),
    )(*[pltpu.with_memory_space_constraint(a, pltpu.HBM) for a in shards],
      *[pltpu.with_memory_space_constraint(z, pltpu.HBM) for z in zones], after)
    return res[0], res[1], list(res[2:2 + n]), list(res[2 + n:2 + 2 * n]), res[-1]


def _gather_wait(send_sems, recv_sems, thru, lands, after):
    n = len(thru)

    def body(*refs):
        ins, zones = refs[:n], refs[n:2 * n]
        s_sems, r_sems = refs[2 * n], refs[2 * n + 1]
        x, y, c = lax.axis_index("x"), lax.axis_index("y"), lax.axis_index("c")
        for t in range(n):
            for k, (px, py) in enumerate([(1 - x, y), (x, 1 - y), (1 - x, 1 - y)]):
                cp = pltpu.make_async_remote_copy(
                    src_ref=ins[t], dst_ref=zones[t].at[_chip_index(px, py)], send_sem=s_sems.at[3 * t + k],
                    recv_sem=r_sems.at[3 * t + k], device_id=(px, py, c), device_id_type=MESH)
                cp.wait_send()
                cp.wait_recv()

    res = pl.pallas_call(
        body, name="gather_rest_wait",
        out_shape=(*[pltpu.HBM(a.shape, a.dtype) for a in thru], *[pltpu.HBM(z.shape, z.dtype) for z in lands]),
        in_specs=[HBM] * (2 * n) + [SEM, SEM, pl.BlockSpec(memory_space=pl.ANY)], out_specs=[HBM] * (2 * n),
        input_output_aliases={t: t for t in range(2 * n)},
        compiler_params=pltpu.CompilerParams(has_side_effects=DATAFLOW),
    )(*thru, *lands, send_sems, recv_sems, after)
    return list(res[:n]), list(res[n:])


def _scatter_ends(t, k, n_slotted, ins, zones, x, y, me):
    px, py = [(1 - x, y), (x, 1 - y), (1 - x, 1 - y)][k]
    if t < n_slotted:
        return ins[t].at[_chip_index(px, py)], zones[t].at[k], (px, py)
    return ins[t], (zones[t].at[me], zones[t].at[_chip_index(px, py)]), (px, py)


def _scatter_start(name, slotted, whole, after):
    n_s = len(slotted)
    arrays = list(slotted) + list(whole)
    n = len(arrays)

    def body(*refs):
        ins, lands = refs[:n], refs[n:2 * n]
        send_sems, recv_sems = refs[2 * n + 1], refs[2 * n + 2]
        token = refs[-1]
        x, y, c = lax.axis_index("x"), lax.axis_index("y"), lax.axis_index("c")
        for t in range(n):
            for k in range(3):
                src, dst, (px, py) = _scatter_ends(t, k, n_s, ins, lands, x, y, _chip_index(x, y))
                pltpu.make_async_remote_copy(
                    src_ref=src, dst_ref=dst if t < n_s else dst[0], send_sem=send_sems.at[3 * t + k],
                    recv_sem=recv_sems.at[3 * t + k], device_id=(px, py, c), device_id_type=MESH).start()
        token[...] = jnp.zeros_like(token)

    zones = [lax.empty((3,) + a.shape[1:], a.dtype) for a in slotted]
    zones += [lax.empty((N_CHIPS,) + a.shape, a.dtype) for a in whole]
    slotted = arrays
    res = pl.pallas_call(
        body, name=name,
        out_shape=(pltpu.SemaphoreType.DMA((3 * n,)), pltpu.SemaphoreType.DMA((3 * n,)),
                   *[pltpu.HBM(a.shape, a.dtype) for a in slotted], *[pltpu.HBM(z.shape, z.dtype) for z in zones],
                   jax.ShapeDtypeStruct((8, 128), F32)),
        in_specs=[HBM] * (2 * n) + [pl.BlockSpec(memory_space=pl.ANY)],
        out_specs=(SEM, SEM, *[HBM] * (2 * n), pl.BlockSpec(memory_space=pltpu.VMEM)),
        input_output_aliases={t: 2 + t for t in range(2 * n)},
        compiler_params=pltpu.CompilerParams(has_side_effects=DATAFLOW),
    )(*[pltpu.with_memory_space_constraint(a, pltpu.HBM) for a in slotted],
      *[pltpu.with_memory_space_constraint(z, pltpu.HBM) for z in zones], after)
    return res[0], res[1], list(res[2:2 + n]), list(res[2 + n:2 + 2 * n]), res[-1]


def _scatter_wait(name, n_slotted, send_sems, recv_sems, thru, lands, after):
    n = len(thru)

    def body(*refs):
        ins, zones = refs[:n], refs[n:2 * n]
        s_sems, r_sems = refs[2 * n], refs[2 * n + 1]
        x, y, c = lax.axis_index("x"), lax.axis_index("y"), lax.axis_index("c")
        for t in range(n):
            for k in range(3):
                src, dst, (px, py) = _scatter_ends(t, k, n_slotted, ins, zones, x, y, _chip_index(x, y))
                cp = pltpu.make_async_remote_copy(
                    src_ref=src, dst_ref=dst if t < n_slotted else dst[1], send_sem=s_sems.at[3 * t + k],
                    recv_sem=r_sems.at[3 * t + k], device_id=(px, py, c), device_id_type=MESH)
                cp.wait_send()
                cp.wait_recv()

    res = pl.pallas_call(
        body, name=name,
        out_shape=(*[pltpu.HBM(a.shape, a.dtype) for a in thru], *[pltpu.HBM(z.shape, z.dtype) for z in lands]),
        in_specs=[HBM] * (2 * n) + [SEM, SEM, pl.BlockSpec(memory_space=pl.ANY)], out_specs=[HBM] * (2 * n),
        input_output_aliases={t: t for t in range(2 * n)},
        compiler_params=pltpu.CompilerParams(has_side_effects=DATAFLOW),
    )(*thru, *lands, send_sems, recv_sems, after)
    return list(res[:n]), list(res[n:])


def _sum_chips(name, slotted, recv, idx, dep):
    _, r, c = slotted.shape
    tr = _row_tile(r, 5 * c * 4)

    def body(idx_ref, own_ref, r_ref, _, o_ref):
        acc = own_ref[0]
        for k in range(3):
            acc = acc + r_ref[k]
        o_ref[...] = acc

    grid_spec = pltpu.PrefetchScalarGridSpec(
        num_scalar_prefetch=1, grid=(r // tr,),
        in_specs=[pl.BlockSpec((1, tr, c), lambda i, s: (s[1], i, 0)), pl.BlockSpec((3, tr, c), lambda i, s: (0, i, 0)),
                  pl.BlockSpec(memory_space=pl.ANY)],
        out_specs=pl.BlockSpec((tr, c), lambda i, s: (i, 0)))
    return pl.pallas_call(
        body, name=name, grid_spec=grid_spec, out_shape=jax.ShapeDtypeStruct((r, c), F32),
        compiler_params=_cparams(("parallel",)),
    )(idx, slotted, recv, dep)


def _half_shape(shape2d, axis):
    r, c = shape2d
    return (r, c // 2) if axis else (r // 2, c)


def _swap_halves(slotted, axes, small):
    n = len(slotted)

    def body(*refs):
        ins, sm_in = refs[:n], refs[n]
        outs, sm_out = refs[n + 1:2 * n + 1], refs[2 * n + 1]
        send_sems, recv_sems, local_sem = refs[2 * n + 2:]
        x, y, c = lax.axis_index("x"), lax.axis_index("y"), lax.axis_index("c")
        local = pltpu.make_async_copy(sm_in, sm_out.at[c], local_sem)
        local.start()
        sends = []
        for t in range(n):
            other = (slice(None),) + _half(slotted[t].shape[1:], axes[t], 1 - c)
            cp = pltpu.make_async_remote_copy(
                src_ref=ins[t].at[other], dst_ref=outs[t], send_sem=send_sems.at[t],
                recv_sem=recv_sems.at[t], device_id=(x, y, 1 - c), device_id_type=MESH)
            cp.start()
            sends.append(cp)
        cp = pltpu.make_async_remote_copy(
            src_ref=sm_in, dst_ref=sm_out.at[c], send_sem=send_sems.at[n], recv_sem=recv_sems.at[n],
            device_id=(x, y, 1 - c), device_id_type=MESH)
        cp.start()
        sends.append(cp)
        for cp in sends[:n]:
            cp.wait_recv()
        pltpu.make_async_remote_copy(
            src_ref=sm_in, dst_ref=sm_out.at[1 - c], send_sem=send_sems.at[n], recv_sem=recv_sems.at[n],
            device_id=(x, y, 1 - c), device_id_type=MESH).wait_recv()
        for cp in sends:
            cp.wait_send()
        local.wait()

    return pl.pallas_call(
        body, name="swap_halves",
        in_specs=[HBM] * (n + 1), out_specs=[HBM] * (n + 1),
        out_shape=[jax.ShapeDtypeStruct((a.shape[0],) + _half_shape(a.shape[1:], ax), a.dtype)
                   for a, ax in zip(slotted, axes, strict=True)]
        + [jax.ShapeDtypeStruct((2,) + small.shape, small.dtype)],
        scratch_shapes=[pltpu.SemaphoreType.DMA((n + 1,)), pltpu.SemaphoreType.DMA((n + 1,)), pltpu.SemaphoreType.DMA],
    )(*slotted, small)


def _tiling(r, c, f32_per_elem):
    if r % 8 == 0:
        tr = _row_tile(r, f32_per_elem * c * 4)
        return r // tr, (tr, c), lambda i: (i, 0)
    assert c % 128 == 0, (r, c)
    return c // 128, (r, 128), lambda i: (0, i)


def _pair_sum(name, slotted, other, idx, axis):
    _, r, c = slotted.shape
    hr, hc = _half_shape((r, c), axis)
    n, (tr, tc), at = _tiling(hr, hc, 13)
    if axis == 0:
        a = slotted.reshape(N_CHIPS, 2, hr, c)
        a_all = pl.BlockSpec((N_CHIPS, 1, tr, tc), lambda i, s: (0, s[0]) + at(i))
        a_own = pl.BlockSpec((1, 1, tr, tc), lambda i, s: (s[1], s[0]) + at(i))
    else:
        a, per_half = slotted, hc // tc
        a_all = pl.BlockSpec((N_CHIPS, tr, tc), lambda i, s: (0, at(i)[0], s[0] * per_half + at(i)[1]))
        a_own = pl.BlockSpec((1, tr, tc), lambda i, s: (s[1], at(i)[0], s[0] * per_half + at(i)[1]))

    def body(idx_ref, a_ref, b_ref, am_ref, bm_ref, p_ref, own_ref):
        mine, mine_own = (a_ref[:, 0], am_ref[0, 0]) if axis == 0 else (a_ref[...], am_ref[0])
        p_ref[...] = (mine + b_ref[...]).astype(p_ref.dtype)
        own_ref[...] = mine_own + bm_ref[0]

    grid_spec = pltpu.PrefetchScalarGridSpec(
        num_scalar_prefetch=1, grid=(n,),
        in_specs=[a_all, pl.BlockSpec((N_CHIPS, tr, tc), lambda i, s: (0,) + at(i)),
                  a_own, pl.BlockSpec((1, tr, tc), lambda i, s: (s[1],) + at(i))],
        out_specs=[pl.BlockSpec((N_CHIPS, tr, tc), lambda i, s: (0,) + at(i)),
                   pl.BlockSpec((tr, tc), lambda i, s: at(i))])
    return pl.pallas_call(
        body, name=name, grid_spec=grid_spec,
        out_shape=[jax.ShapeDtypeStruct((N_CHIPS, hr, hc), BF16), jax.ShapeDtypeStruct((hr, hc), F32)],
        compiler_params=_cparams(("parallel",)),
    )(idx, a, other, a, other)


def _sum_parts(name, own, recv):
    h, c = own.shape
    n, (tr, tc), at = _tiling(h, c, 4)

    def body(o_ref, r_ref, out_ref):
        acc = o_ref[...]
        for k in range(3):
            acc = acc + r_ref[k].astype(F32)
        out_ref[...] = acc

    return pl.pallas_call(
        body, name=name, grid=(n,),
        in_specs=[pl.BlockSpec((tr, tc), at), pl.BlockSpec((3, tr, tc), lambda i: (0,) + at(i))],
        out_specs=pl.BlockSpec((tr, tc), at),
        out_shape=jax.ShapeDtypeStruct((h, c), F32),
        compiler_params=_cparams(("parallel",)),
    )(own, recv)


def _all_to_all(name, small):
    def body(sm_in, sm_out, send_sems, recv_sems, local_sem):
        x, y, c = lax.axis_index("x"), lax.axis_index("y"), lax.axis_index("c")
        dev = 4 * x + 2 * y + c
        local = pltpu.make_async_copy(sm_in, sm_out.at[dev], local_sem)
        local.start()
        rel = [(fx, fy, fc) for fx in (0, 1) for fy in (0, 1) for fc in (0, 1)][1:]
        sends = []
        for k, (fx, fy, fc) in enumerate(rel):
            cp = pltpu.make_async_remote_copy(
                src_ref=sm_in, dst_ref=sm_out.at[dev], send_sem=send_sems.at[k], recv_sem=recv_sems.at[k],
                device_id=(x ^ fx, y ^ fy, c ^ fc), device_id_type=MESH)
            cp.start()
            sends.append(cp)
        for k, (fx, fy, fc) in enumerate(rel):
            src_dev = 4 * (x ^ fx) + 2 * (y ^ fy) + (c ^ fc)
            pltpu.make_async_remote_copy(
                src_ref=sm_in, dst_ref=sm_out.at[src_dev], send_sem=send_sems.at[k], recv_sem=recv_sems.at[k],
                device_id=(x ^ fx, y ^ fy, c ^ fc), device_id_type=MESH).wait_recv()
        for cp in sends:
            cp.wait_send()
        local.wait()

    return pl.pallas_call(
        body, name=name, in_specs=[HBM], out_specs=HBM,
        out_shape=jax.ShapeDtypeStruct((N_DEV,) + small.shape, small.dtype),
        scratch_shapes=[pltpu.SemaphoreType.DMA((7,)), pltpu.SemaphoreType.DMA((7,)), pltpu.SemaphoreType.DMA],
    )(small)


def _swap_sibling(name, parts):
    n = len(parts)

    def body(*refs):
        ins, outs = refs[:n], refs[n:2 * n]
        send_sems, recv_sems = refs[2 * n:]
        x, y, c = lax.axis_index("x"), lax.axis_index("y"), lax.axis_index("c")
        cps = []
        for t in range(n):
            cp = pltpu.make_async_remote_copy(
                src_ref=ins[t], dst_ref=outs[t], send_sem=send_sems.at[t], recv_sem=recv_sems.at[t],
                device_id=(x, y, 1 - c), device_id_type=MESH)
            cp.start()
            cps.append(cp)
        for cp in cps:
            cp.wait_recv()
        for cp in cps:
            cp.wait_send()

    return pl.pallas_call(
        body, name=name,
        in_specs=[HBM] * n, out_specs=[HBM] * n,
        out_shape=[jax.ShapeDtypeStruct(a.shape, a.dtype) for a in parts],
        scratch_shapes=[pltpu.SemaphoreType.DMA((n,)), pltpu.SemaphoreType.DMA((n,))],
    )(*parts)


def _sum_slots(name, a):
    k, r, c = a.shape
    tr = _row_tile(r, (k + 1) * c * 4)

    def body(a_ref, o_ref):
        acc = a_ref[0]
        for i in range(1, k):
            acc = acc + a_ref[i]
        o_ref[...] = acc

    return pl.pallas_call(
        body, name=name, grid=(r // tr,),
        in_specs=[pl.BlockSpec((k, tr, c), lambda i: (0, i, 0))],
        out_specs=pl.BlockSpec((tr, c), lambda i: (i, 0)),
        out_shape=jax.ShapeDtypeStruct((r, c), a.dtype),
        compiler_params=_cparams(("parallel",)),
    )(a)


def _adamw(name, w, m, v, g_parts):
    r, c = w.shape
    ng = len(g_parts)
    tr = _row_tile(r, (7 + ng) * c * 4)
    c1 = 1.0 - ADAM_B1 ** ADAM_STEP
    c2 = 1.0 - ADAM_B2 ** ADAM_STEP

    def body(*refs):
        w_ref, m_ref, v_ref = refs[:3]
        g_refs = refs[3:3 + ng]
        go_ref, d_ref, mo_ref, vo_ref = refs[3 + ng:]
        g = g_refs[0][...]
        for gr in g_refs[1:]:
            g = g + gr[...]
        m_new = ADAM_B1 * m_ref[...] + (1.0 - ADAM_B1) * g
        v_new = ADAM_B2 * v_ref[...] + (1.0 - ADAM_B2) * (g * g)
        go_ref[...] = g
        mo_ref[...] = m_new
        vo_ref[...] = v_new
        d_ref[...] = -ADAM_LR * ((m_new / c1) / (jnp.sqrt(v_new / c2) + ADAM_EPS) + ADAM_WD * w_ref[...])

    spec = pl.BlockSpec((tr, c), lambda i: (i, 0))
    return pl.pallas_call(
        body, name=name, grid=(r // tr,),
        in_specs=[spec] * (3 + ng), out_specs=[spec] * 4,
        out_shape=[jax.ShapeDtypeStruct((r, c), F32)] * 4,
        compiler_params=_cparams(("parallel",)),
    )(w, m, v, *g_parts)


def _adamw_halves(name, w, m, v, own, other, idx, axis):
    hr, hc = own.shape
    nb, (tr, tc), at = _tiling(hr, hc, 9)
    c1 = 1.0 - ADAM_B1 ** ADAM_STEP
    c2 = 1.0 - ADAM_B2 ** ADAM_STEP

    def body(idx_ref, w_ref, m_ref, v_ref, own_ref, oth_ref, go_ref, d_ref, mo_ref, vo_ref):
        g = jnp.where(pl.program_id(0) == idx_ref[0], own_ref[...], oth_ref[...])
        m_new = ADAM_B1 * m_ref[...] + (1.0 - ADAM_B1) * g
        v_new = ADAM_B2 * v_ref[...] + (1.0 - ADAM_B2) * (g * g)
        go_ref[...] = g
        mo_ref[...] = m_new
        vo_ref[...] = v_new
        d_ref[...] = -ADAM_LR * ((m_new / c1) / (jnp.sqrt(v_new / c2) + ADAM_EPS) + ADAM_WD * w_ref[...])

    per_half = (hc // tc) if axis else (hr // tr)
    if axis:
        full = pl.BlockSpec((tr, tc), lambda hh, i, s: (at(i)[0], hh * per_half + at(i)[1]))
    else:
        full = pl.BlockSpec((tr, tc), lambda hh, i, s: (hh * per_half + at(i)[0], at(i)[1]))
    part = pl.BlockSpec((tr, tc), lambda hh, i, s: at(i))
    grid_spec = pltpu.PrefetchScalarGridSpec(
        num_scalar_prefetch=1, grid=(2, nb), in_specs=[full, full, full, part, part], out_specs=[full] * 4)
    return pl.pallas_call(
        body, name=name, grid_spec=grid_spec, out_shape=[jax.ShapeDtypeStruct(w.shape, F32)] * 4,
        compiler_params=_cparams(("parallel", "parallel")),
    )(idx, w, m, v, own, other)


WEIGHTS = ['norm_w', 'w_in', 's5_a_re', 's5_a_im', 's5_b_re', 's5_b_im', 's5_c_re', 's5_c_im', 's5_d', 's5_log_step',
           's5_w_glu', 's5_b_glu', 'ssd_conv_w', 'ssd_conv_b', 'ssd_dt_bias', 'ssd_a_log', 'ssd_d', 'ssd_norm_w',
           'w_br_s5', 'w_br_ssd', 'w_out', 'ple_norm_w', 'w_ple_gate', 'w_ple_proj', 'final_norm_w']
SHARDED = {'w_in': ((IN_PROJ_DIM, 1024), 0), 's5_w_glu': ((512, 512), 0), 'ssd_conv_w': ((SSD_CONV, SSD_CONV_DIM), 1),
           'w_br_s5': ((512, 1024), 1), 'w_br_ssd': ((1536, 1024), 0), 'w_out': ((1024, 1024), 0),
           'w_ple_gate': ((1024, 1024), 0), 'w_ple_proj': ((256, 1024), 1)}
TRANSPOSED = ('w_in',)
SMALL = [n for n in WEIGHTS if n not in SHARDED]


def _shard_shape(name):
    (r, c), ax = SHARDED[name]
    return (r // N_CHIPS, c) if ax == 0 else (r, c // N_CHIPS)


def _half_axis(name):
    return 0 if (_shard_shape(name)[0] // 2) % 16 == 0 else 1


def _shard2d(name, a):
    r, c = _shard_shape(name)
    return a.reshape(c, r).T if name in TRANSPOSED else a.reshape(r, c)


def _unshard2d(name, a2, shape):
    return (a2.T if name in TRANSPOSED else a2).reshape(shape)


def _unslot(name, a4):
    (r, c), ax = SHARDED[name]
    if ax == 0:
        return a4.reshape(r, c)
    return jnp.transpose(a4, (1, 0, 2)).reshape(r, c)


def _slot(name, full):
    (r, c), ax = SHARDED[name]
    if ax == 0:
        return full.reshape(N_CHIPS, r // N_CHIPS, c)
    return jnp.transpose(full.reshape(r, N_CHIPS, c // N_CHIPS), (1, 0, 2))


GHP = ('s5_b_re', 's5_b_im')


def _view_shape(name):
    if name in ('s5_a_re', 's5_a_im'):
        return (S5_GROUPS, S5_STATE)
    if name in GHP + ('s5_c_re', 's5_c_im'):
        return (S5_GROUPS, S5_GROUP, S5_STATE)
    if name == 'ssd_conv_w':
        return (SSD_CONV, SSD_CONV_DIM // N_CHIPS)
    return (1, {'s5_log_step': S5_GROUPS, 'ssd_conv_b': SSD_CONV_DIM, 'ssd_norm_w': SSD_WIDTH, 's5_d': S5_WIDTH,
                's5_b_glu': S5_WIDTH, 'ssd_dt_bias': SSD_HEADS, 'ssd_a_log': SSD_HEADS, 'ssd_d': SSD_HEADS}.get(name, D_MODEL))


def _view(name, a):
    if name in GHP:
        return jnp.swapaxes(a.reshape(S5_GROUPS, S5_STATE, S5_GROUP), 1, 2)
    return a.reshape(_view_shape(name))


def _unview(name, a, shape):
    return (jnp.swapaxes(a, 1, 2) if name in GHP else a).reshape(shape)


def _adamw_small(ws, ms, vs, gs):
    n = len(ws)
    c1 = 1.0 - ADAM_B1 ** ADAM_STEP
    c2 = 1.0 - ADAM_B2 ** ADAM_STEP

    def body(*refs):
        w_r, m_r, v_r, g_r = (refs[k * n:(k + 1) * n] for k in range(4))
        d_o, m_o, v_o = (refs[k * n:(k + 1) * n] for k in range(4, 7))
        for i in range(n):
            g = g_r[i][...]
            m_new = ADAM_B1 * m_r[i][...] + (1.0 - ADAM_B1) * g
            v_new = ADAM_B2 * v_r[i][...] + (1.0 - ADAM_B2) * (g * g)
            m_o[i][...] = m_new
            v_o[i][...] = v_new
            d_o[i][...] = -ADAM_LR * ((m_new / c1) / (jnp.sqrt(v_new / c2) + ADAM_EPS) + ADAM_WD * w_r[i][...])

    return pl.pallas_call(
        body, name="adamw_small", out_shape=[jax.ShapeDtypeStruct(w.shape, F32) for w in ws] * 3,
        compiler_params=pltpu.CompilerParams(vmem_limit_bytes=VMEM_LIMIT),
    )(*ws, *ms, *vs, *gs)


def _pack_small(vals):
    flat = jnp.concatenate([v.reshape(-1).astype(F32) for v in vals])
    rows = -(-flat.shape[0] // (256 * 128)) * 256
    return jnp.pad(flat, (0, rows * 128 - flat.shape[0])).reshape(rows, 128)


def _unpack_small(packed, shapes):
    flat = packed.reshape(-1)
    out, off = [], 0
    for sh in shapes:
        n = math.prod(sh)
        out.append(flat[off:off + n].reshape(sh))
        off += n
    return out


def kernel(x, p, norm_w, w_in, s5_a_re, s5_a_im, s5_b_re, s5_b_im, s5_c_re, s5_c_im, s5_d, s5_log_step, s5_w_glu, s5_b_glu, ssd_conv_w, ssd_conv_b, ssd_dt_bias, ssd_a_log, ssd_d, ssd_norm_w, w_br_s5, w_br_ssd, w_out, ple_norm_w, w_ple_gate, w_ple_proj, final_norm_w, loss_target, m_norm_w, m_w_in, m_s5_a_re, m_s5_a_im, m_s5_b_re, m_s5_b_im, m_s5_c_re, m_s5_c_im, m_s5_d, m_s5_log_step, m_s5_w_glu, m_s5_b_glu, m_ssd_conv_w, m_ssd_conv_b, m_ssd_dt_bias, m_ssd_a_log, m_ssd_d, m_ssd_norm_w, m_w_br_s5, m_w_br_ssd, m_w_out, m_ple_norm_w, m_w_ple_gate, m_w_ple_proj, m_final_norm_w, v_norm_w, v_w_in, v_s5_a_re, v_s5_a_im, v_s5_b_re, v_s5_b_im, v_s5_c_re, v_s5_c_im, v_s5_d, v_s5_log_step, v_s5_w_glu, v_s5_b_glu, v_ssd_conv_w, v_ssd_conv_b, v_ssd_dt_bias, v_ssd_a_log, v_ssd_d, v_ssd_norm_w, v_w_br_s5, v_w_br_ssd, v_w_out, v_ple_norm_w, v_w_ple_gate, v_w_ple_proj, v_final_norm_w):
    args = locals()
    wl = {n: args[n] for n in WEIGHTS}
    ml = {n: args["m_" + n] for n in WEIGHTS}
    vl = {n: args["v_" + n] for n in WEIGHTS}
    big = [n for n in SHARDED if n != 'ssd_conv_w']
    chip = _chip_index(lax.axis_index("x"), lax.axis_index("y"))
    idx = jnp.stack([lax.axis_index("c"), chip]).astype(jnp.int32)

    axes = [_half_axis(n) for n in big]
    first = ['w_in']
    rest = [n for n in big if n not in first]
    bf_shards = {n: _shard2d(n, wl[n]).astype(BF16) for n in big}
    w_in_t, conv_w4 = _gather_chips([bf_shards[n] for n in first], [_half_axis(n) for n in first],
                                    [_shard2d('ssd_conv_w', wl['ssd_conv_w'])])
    full = {n: wl[n] for n in SMALL}
    full["w_in_t"] = w_in_t
    full['ssd_conv_w'] = _unslot('ssd_conv_w', conv_w4)
    send_sems, recv_sems, thru, lands, token = _gather_start([bf_shards[n] for n in rest], w_in_t)

    def fetch_rest(after):
        own, zones = _gather_wait(send_sems, recv_sems, thru, lands, after)
        return {n: _unslot(n, lax.dynamic_update_slice(z, o[None], (chip, 0, 0)))
                for n, o, z in zip(rest, own, zones, strict=True)}

    full["_late"] = (token, fetch_rest)

    in_flight = []

    def send_rest(g_now):
        in_flight.extend(_scatter_start("scatter_rest_start", [_slot(n, g_now[n]) for n in rest], [], g_now['s5_w_glu']))
        return in_flight[4]

    full["_early"] = send_rest

    packed = [n for n in SMALL if n != 'norm_w']
    first_axes = [_half_axis(n) for n in first]
    st = {}

    def send_first(g_now, loss_now):
        small_pack = _pack_small([loss_now] + [g_now[n + "_ghp" if n in GHP else n] for n in packed]
                                 + [g_now['ssd_conv_w']])
        swapped = _swap_halves([g_now["w_in_t"]], first_axes, small_pack)
        st["pair"] = [_pair_sum("pair_sum_w_in", g_now["w_in_t"], swapped[0], idx, first_axes[0])]
        small_chip = _sum_slots("sum_small_pair", swapped[-1])
        half_rows = small_chip.shape[0] // 2
        my_half = lax.dynamic_slice(small_chip, (lax.axis_index("c") * half_rows, 0), (half_rows, small_chip.shape[1]))
        st["flight"] = _scatter_start("scatter_first_start", [pb for pb, _ in st["pair"]], [my_half], small_chip)
        return st["flight"][4]

    full["_w_in_ready"] = send_first
    loss, grad_x, g = _local_step(x, p[0], loss_target, full)
    pair = st["pair"]
    s_sems, r_sems, thru, lands, tok = st["flight"]
    small_shapes = [(1, 1)] + [_view_shape(n) for n in packed] + [(SSD_CONV, SSD_CONV_DIM)]
    norm_all = _all_to_all("norm_w_all_to_all", g['norm_w'].reshape(8, 128))
    norm_g = _sum_slots("sum_norm_w", norm_all).reshape(1, D_MODEL)

    out_g, out_d, out_m, out_v = {}, {}, {}, {}
    own_slots, zones = _scatter_wait("scatter_rest_wait", len(rest), *in_flight[:4], norm_g)
    chip_sums = [_sum_chips("sum_chips_" + n, a, z, idx, tok) for n, a, z in zip(rest, own_slots, zones, strict=True)]
    sib_sums = _swap_sibling("swap_sibling_rest", chip_sums)
    for n, own, sib in zip(rest, chip_sums, sib_sums, strict=True):
        res = _adamw("adamw_" + n, _shard2d(n, wl[n]), _shard2d(n, ml[n]), _shard2d(n, vl[n]), [own, sib])
        out_g[n], out_d[n], out_m[n], out_v[n] = (_unshard2d(n, r, wl[n].shape) for r in res)

    sent, got = _scatter_wait("scatter_first_wait", len(first), s_sems, r_sems, thru, lands, out_d[rest[-1]])
    small_zone = lax.dynamic_update_slice(got[-1], sent[-1][None], (chip, 0, 0))
    small_half = _sum_slots("sum_small_chips", small_zone)
    halves = [_sum_parts("sum_chips_" + n, own, r) for n, (_, own), r in zip(first, pair, got[:-1], strict=True)]
    swapped2 = _swap_sibling("swap_sibling_first", halves + [small_half])
    other_halves = swapped2[:-1]
    c_is_0 = lax.axis_index("c") == 0
    small_sum = jnp.concatenate([jnp.where(c_is_0, small_half, swapped2[-1]),
                                 jnp.where(c_is_0, swapped2[-1], small_half)], axis=0)
    for n, own, oth, ax in zip(first, halves, other_halves, first_axes, strict=True):
        res = _adamw_halves("adamw_" + n, _shard2d(n, wl[n]), _shard2d(n, ml[n]), _shard2d(n, vl[n]), own, oth, idx, ax)
        out_g[n], out_d[n], out_m[n], out_v[n] = (_unshard2d(n, r, wl[n].shape) for r in res)
    sm = _unpack_small(small_sum, small_shapes)
    loss_total = sm[0].reshape(())
    conv_g = lax.dynamic_slice(sm[-1], (0, chip * (SSD_CONV_DIM // N_CHIPS)), (SSD_CONV, SSD_CONV_DIM // N_CHIPS))
    names = SMALL + ['ssd_conv_w']
    by_name = {**dict(zip(packed, sm[1:-1], strict=True)), 'norm_w': norm_g, 'ssd_conv_w': conv_g}
    grads = [by_name[n] for n in names]
    res = _adamw_small([_view(n, wl[n]) for n in names], [_view(n, ml[n]) for n in names],
                       [_view(n, vl[n]) for n in names], grads)
    for i, n in enumerate(names):
        out_g[n] = _unview(n, grads[i], wl[n].shape)
        out_d[n], out_m[n], out_v[n] = (_unview(n, res[k * len(names) + i], wl[n].shape) for k in range(3))

    return (loss_total, grad_x, *[out_g[n] for n in WEIGHTS], *[out_d[n] for n in WEIGHTS],
            *[out_m[n] for n in WEIGHTS], *[out_v[n] for n in WEIGHTS])
```

```python
import functools
import math

import jax
import jax.numpy as jnp
from jax import lax
from jax.experimental import pallas as pl
from jax.experimental.pallas import tpu as pltpu

F32 = jnp.float32
BF16 = jnp.bfloat16
MESH = pl.DeviceIdType.MESH

D_MODEL = 1024
PLE_DIM = 256
RMS_EPS = 1e-6
S5_WIDTH = 512
S5_GROUP = 16
S5_GROUPS = 32
S5_STATE = 64
S5_N = S5_GROUPS * S5_STATE
S5_LB = 512
S5_NJ = S5_N // S5_LB
S5_TB = 2048
S5_LOG_TB = 8
SSD_WIDTH = 1536
SSD_HEADDIM = 64
SSD_HEADS = 24
SSD_GROUPS = 4
SSD_HPG = 6
SSD_STATE = 128
SSD_CONV = 4
SSD_CHUNK = 128
SSD_BC = 512
SSD_CONV_DIM = 2560
GROUP_W = SSD_WIDTH // SSD_GROUPS
N_CHIPS = 4
N_DEV = 8

OFF_XBC, OFF_U5, OFF_Z5, OFF_DT, OFF_G5, OFF_GS, OFF_ZS = 0, 2560, 3072, 3584, 4096, 5120, 6144
DT_W = 512
PROJ_W = 7680
IN_PROJ_DIM = 7192

ADAM_LR, ADAM_B1, ADAM_B2, ADAM_EPS, ADAM_WD, ADAM_STEP = 0.001, 0.9, 0.999, 1e-08, 0.01, 10

VMEM_LIMIT = 56 * 1024 * 1024


ROW_BLOCK_BYTES = 8 * 1024 * 1024


def _row_tile(r, bytes_per_row):
    for t in (r, 4096, 2048, 1024, 512, 256, 128, 64, 32, 16, 8):
        if t <= r and r % t == 0 and t * bytes_per_row <= ROW_BLOCK_BYTES:
            return t
    return r


def _cparams(sem):
    return pltpu.CompilerParams(dimension_semantics=sem, vmem_limit_bytes=VMEM_LIMIT)


def _dg(a, b, ca, cb):
    return lax.dot_general(a.astype(BF16), b.astype(BF16), (((ca,), (cb,)), ((), ())), preferred_element_type=F32)


@jax.custom_vjp
def dot_nn(a, b):
    return _dg(a, b, 1, 0)


@jax.custom_vjp
def dot_nt(a, b):
    return _dg(a, b, 1, 1)


@jax.custom_vjp
def dot_tn(a, b):
    return _dg(a, b, 0, 0)


dot_nn.defvjp(lambda a, b: (_dg(a, b, 1, 0), (a, b)), lambda r, g: (_dg(g, r[1], 1, 1), _dg(r[0], g, 0, 0)))
dot_nt.defvjp(lambda a, b: (_dg(a, b, 1, 1), (a, b)), lambda r, g: (_dg(g, r[1], 1, 0), _dg(g, r[0], 0, 0)))
dot_tn.defvjp(lambda a, b: (_dg(a, b, 0, 0), (a, b)), lambda r, g: (_dg(r[1], g, 1, 1), _dg(r[0], g, 1, 0)))


MM_VMEM_BUDGET = 40 * 1024 * 1024


def _mm_tiles(m, n, k, sa, sb, so, tn_only=None):
    best, best_key = None, None
    for tm in (1024, 512, 256, 128, 64, 32, 16, 8):
        if m % tm:
            continue
        for tn in (2048, 1536, 1280, 1024, 768, 640, 512, 384, 256, 128):
            if n % tn or (tn_only is not None and tn not in tn_only):
                continue
            for tk in (k, 2048, 1536, 1280, 1024, 768, 512, 256, 128):
                if k % tk or tk > max(k, 128):
                    continue
                need = 2 * (tm * tk * sa + tk * tn * sb + tm * tn * so) + (tm * tn * 4 if tk < k else 0)
                if need > MM_VMEM_BUDGET:
                    continue
                key = (tm * tn * tk, tk)
                if best_key is None or key > best_key:
                    best, best_key = (tm, tn, tk), key
    assert best is not None, (m, n, k)
    return best


def _matmul(name, a, b, *, ta=False, tb=False, a_win=None, out_dtype=F32, epilogue=None, epi_rows=(), epi_pars=(),
            epi_outs=(), full_rows=False, epi_accs=(), epi_into=None, dep=None):
    a_off, a_w = a_win if a_win is not None else (0, a.shape[1])
    if ta:
        kdim, m = a.shape[0], a_w
    else:
        m, kdim = a.shape[0], a_w
    n = b.shape[0] if tb else b.shape[1]
    assert (b.shape[1] if tb else b.shape[0]) == kdim, (name, a.shape, b.shape)
    out_dtypes = list(epi_outs) if epilogue is not None else [out_dtype]
    so = sum(jnp.dtype(d).itemsize for d in out_dtypes) + sum(r.dtype.itemsize for r, _ in epi_rows)
    tn_ok = [n] if full_rows else [t for t in (1024, 512, 256, 128) if all(off % t == 0 for _, off in epi_rows)]
    tm, tn, tk = _mm_tiles(m, n, kdim, a.dtype.itemsize, b.dtype.itemsize, so, tn_ok if epilogue is not None else None)
    nk = kdim // tk
    n_er, n_ep, n_out = len(epi_rows), len(epi_pars), len(out_dtypes)
    if ta:
        assert a_off % tm == 0
        a_spec = pl.BlockSpec((tk, tm), lambda i, j, k: (k, i + a_off // tm))
    else:
        assert a_off % tk == 0
        a_spec = pl.BlockSpec((tm, tk), lambda i, j, k: (i, k + a_off // tk))
    if tb:
        b_spec = pl.BlockSpec((tn, tk), lambda i, j, k: (j, k))
    else:
        b_spec = pl.BlockSpec((tk, tn), lambda i, j, k: (k, j))
    ca, cb = (0 if ta else 1), (1 if tb else 0)

    n_acc = len(epi_accs)
    n_x = (1 if epi_into is not None else 0) + (1 if dep is not None else 0)
    assert not (n_acc or epi_into is not None) or full_rows

    def body(a_ref, b_ref, *refs):
        er, ep = refs[:n_er], refs[n_er:n_er + n_ep]
        first_out = n_er + n_ep + n_x
        o_refs = refs[first_out:first_out + n_out]
        s_refs = refs[first_out + n_out:first_out + n_out + n_acc]
        acc = refs[first_out + n_out + n_acc:]

        def finish(c):
            outs = [c] if epilogue is None else epilogue(c, [r[...] for r in er], [p[...] for p in ep])
            if n_acc:
                outs, sums = outs

                @pl.when(pl.program_id(0) == 0)
                def _():
                    for s_ref in s_refs:
                        s_ref[...] = jnp.zeros_like(s_ref)

                for s_ref, s in zip(s_refs, sums, strict=True):
                    s_ref[...] += jnp.broadcast_to(s, s_ref.shape)
            for o_ref, o in zip(o_refs, outs, strict=True):
                o_ref[...] = o.astype(o_ref.dtype)

        if nk == 1:
            finish(_dg(a_ref[...], b_ref[...], ca, cb))
            return
        (acc_ref,) = acc
        k = pl.program_id(2)

        @pl.when(k == 0)
        def _():
            acc_ref[...] = jnp.zeros_like(acc_ref)

        acc_ref[...] += _dg(a_ref[...], b_ref[...], ca, cb)

        @pl.when(k == nk - 1)
        def _():
            finish(acc_ref[...])

    in_specs = [a_spec, b_spec]
    in_specs += [pl.BlockSpec((tm, tn), functools.partial(lambda i, j, k, c: (i, j + c), c=off // tn)) for _, off in epi_rows]
    in_specs += [pl.BlockSpec((p.shape[0], tn), lambda i, j, k: (0, j)) for p in epi_pars]
    out_specs = [pl.BlockSpec((tm, tn), lambda i, j, k: (i, j)) for _ in out_dtypes]
    out_shape = [jax.ShapeDtypeStruct((m, n), d) for d in out_dtypes]
    extra, aliases = [], {}
    if epi_into is not None:
        buf, off, width = epi_into
        assert off % width == 0 and buf.dtype == out_dtypes[0]
        in_specs.append(pl.BlockSpec(memory_space=pl.ANY))
        out_specs[0] = pl.BlockSpec((tm, width), functools.partial(lambda i, j, k, c: (i, c), c=off // width))
        out_shape[0] = jax.ShapeDtypeStruct(buf.shape, buf.dtype)
        extra, aliases = [buf], {2 + n_er + n_ep: 0}
    if dep is not None:
        in_specs.append(pl.BlockSpec(memory_space=pl.ANY))
        extra = extra + [dep]
    out_specs += [pl.BlockSpec((r, w), lambda i, j, k: (0, 0)) for r, w in epi_accs]
    out_shape += [jax.ShapeDtypeStruct((r, w), F32) for r, w in epi_accs]
    res = pl.pallas_call(
        body, name=name, grid=(m // tm, n // tn, nk),
        in_specs=in_specs, out_specs=out_specs, out_shape=out_shape, input_output_aliases=aliases,
        scratch_shapes=[pltpu.VMEM((tm, tn), F32)] if nk > 1 else [],
        compiler_params=_cparams(("arbitrary",) * 3 if n_acc else ("parallel", "parallel", "arbitrary")),
    )(a, b, *[r for r, _ in epi_rows], *epi_pars, *extra)
    return res if epilogue is not None else res[0]


BAND = 128


def _rowwise(name, fn, n_rows, tr, row_ins, par_ins, row_outs, acc_outs=(), into=None):
    nr, npar, no, na = len(row_ins), len(par_ins), len(row_outs), len(acc_outs)
    in_specs = []
    for arr, off, w in row_ins:
        assert off % w == 0 and arr.shape[0] == n_rows, (name, arr.shape, off, w)
        in_specs.append(pl.BlockSpec((tr, w), functools.partial(lambda i, c: (i, c), c=off // w)))
    for arr, off, w in par_ins:
        assert off % w == 0
        in_specs.append(pl.BlockSpec((arr.shape[0], w), functools.partial(lambda i, c: (0, c), c=off // w)))
    out_specs = [pl.BlockSpec((tr, w), lambda i: (i, 0)) for w, _ in row_outs]
    out_specs += [pl.BlockSpec((r, w), lambda i: (0, 0)) for r, w in acc_outs]
    out_shape = [jax.ShapeDtypeStruct((n_rows, w), dt) for w, dt in row_outs]
    out_shape += [jax.ShapeDtypeStruct((r, w), F32) for r, w in acc_outs]
    extra, aliases = [], {}
    if into is not None:
        buf, off = into
        w0 = row_outs[0][0]
        assert off % w0 == 0 and buf.dtype == row_outs[0][1]
        in_specs.append(pl.BlockSpec(memory_space=pl.ANY))
        out_specs[0] = pl.BlockSpec((tr, w0), functools.partial(lambda i, c: (i, c), c=off // w0))
        out_shape[0] = jax.ShapeDtypeStruct(buf.shape, buf.dtype)
        extra, aliases = [buf], {nr + npar: 0}
    nx = len(extra)

    def body(*refs):
        rows = [r[...] for r in refs[:nr]]
        pars = [r[...] for r in refs[nr:nr + npar]]
        o_refs = refs[nr + npar + nx:nr + npar + nx + no]
        a_refs = refs[nr + npar + nx + no:]
        outs, accs = fn(rows, pars)
        for o_ref, o in zip(o_refs, outs, strict=True):
            o_ref[...] = o.astype(o_ref.dtype)
        if na:
            @pl.when(pl.program_id(0) == 0)
            def _():
                for a_ref in a_refs:
                    a_ref[...] = jnp.zeros_like(a_ref)

            for a_ref, a in zip(a_refs, accs, strict=True):
                a_ref[...] += jnp.broadcast_to(a, a_ref.shape)

    res = pl.pallas_call(
        body, name=name, grid=(n_rows // tr,),
        in_specs=in_specs, out_specs=out_specs, out_shape=out_shape, input_output_aliases=aliases,
        compiler_params=_cparams(("arbitrary",) if na else ("parallel",)),
    )(*[a for a, _, _ in row_ins], *[a for a, _, _ in par_ins], *extra)
    return res


def _rms(x, w):
    return x * lax.rsqrt(jnp.mean(x * x, axis=-1, keepdims=True) + RMS_EPS) * w


def _s5_out(yc, u, z, t, d, bg):
    ge = jax.nn.gelu(yc + d * u)
    return ge * jax.nn.sigmoid(t + bg) * jax.nn.silu(z)


def _merge(g5, gs, m5, ms):
    return jax.nn.sigmoid(g5) * m5 + jax.nn.sigmoid(gs) * ms


def _head_loss(h1, pgl, pp, fw, tgt):
    h2 = h1 + jax.nn.sigmoid(pgl) * pp
    err = _rms(h2, fw) - tgt
    per_row = 0.5 * jnp.mean(err * err, axis=-1, keepdims=True)
    return jnp.sum(per_row, axis=0, keepdims=True)


def _s5_disc(a_re, a_im, log_step, b_re2, b_im2, expand):
    step = jnp.exp(log_step)
    mag = jnp.exp(a_re * step)
    lb_re = mag * jnp.cos(a_im * step)
    lb_im = mag * jnp.sin(a_im * step)
    den = a_re * a_re + a_im * a_im
    n_re = lb_re - 1.0
    f_re = (n_re * a_re + lb_im * a_im) / den
    f_im = (lb_im * a_re - n_re * a_im) / den
    hi = lax.Precision.HIGHEST
    fr = jnp.dot(expand, f_re, precision=hi, preferred_element_type=F32)
    fi = jnp.dot(expand, f_im, precision=hi, preferred_element_type=F32)
    return lb_re, lb_im, fr * b_re2 - fi * b_im2, fr * b_im2 + fi * b_re2


def _s5_params_fwd(a_re, a_im, log_step, b_re2, b_im2, expand):
    gp = a_re.shape

    def body(ar, ai, ls, br, bi, ex, pr_ref, pi_ref, bbr_ref, bbi_ref):
        lr, li, bbr, bbi = _s5_disc(ar[...], ai[...], ls[...], br[...], bi[...], ex[...])
        bbr_ref[...] = bbr
        bbi_ref[...] = bbi
        qr, qi = lr, li
        for k in range(S5_LOG_TB):
            pr_ref[k] = qr
            pi_ref[k] = qi
            qr, qi = qr * lr - qi * li, qr * li + qi * lr

    return pl.pallas_call(
        body, name="s5_params_fwd",
        out_shape=(jax.ShapeDtypeStruct((S5_LOG_TB,) + gp, F32), jax.ShapeDtypeStruct((S5_LOG_TB,) + gp, F32),
                   jax.ShapeDtypeStruct(b_re2.shape, F32), jax.ShapeDtypeStruct(b_re2.shape, F32)),
    )(a_re, a_im, log_step, b_re2, b_im2, expand)


def _s5_params_bwd(a_re, a_im, log_step, b_re2, b_im2, expand, d_lr, d_li, d_bbr, d_bbi):
    def body(ar, ai, ls, br, bi, ex, glr, gli, gbr, gbi, dar, dai, dls, dbr, dbi):
        _, vjp = jax.vjp(lambda *p: _s5_disc(*p, ex[...]), ar[...], ai[...], ls[...], br[...], bi[...])
        g = vjp((glr[...], gli[...], gbr[...], gbi[...]))
        for ref, val in zip((dar, dai, dls, dbr, dbi), g, strict=True):
            ref[...] = val

    return pl.pallas_call(
        body, name="s5_params_bwd",
        out_shape=tuple(jax.ShapeDtypeStruct(v.shape, F32) for v in (a_re, a_im, log_step, b_re2, b_im2)),
    )(a_re, a_im, log_step, b_re2, b_im2, expand, d_lr, d_li, d_bbr, d_bbi)


def _scan_block(x_ref, out_ref, lp, edge_pow, cr, ci, reverse, each=None):
    n_g = x_ref.shape[0] // 8
    sub = lax.broadcasted_iota(jnp.int32, (8, S5_LB), 0)
    steps = []
    for sh in (1, 2, 4):
        keep = (sub < 8 - sh) if reverse else (sub >= sh)
        steps.append((8 - sh if reverse else sh, jnp.where(keep, lp[sh - 1:sh, :S5_LB], 0.0),
                      jnp.where(keep, lp[sh - 1:sh, S5_LB:], 0.0)))
    e_r, e_i = edge_pow[:, :S5_LB], edge_pow[:, S5_LB:]
    for r in (range(n_g - 1, -1, -1) if reverse else range(n_g)):
        rows = slice(8 * r, 8 * r + 8)
        xr, xi = x_ref[rows, :S5_LB], x_ref[rows, S5_LB:]
        for by, a_r, a_i in steps:
            pr, pi = pltpu.roll(xr, by, 0), pltpu.roll(xi, by, 0)
            xr, xi = xr + a_r * pr - a_i * pi, xi + a_r * pi + a_i * pr
        xr = xr + e_r * cr - e_i * ci
        xi = xi + e_r * ci + e_i * cr
        out_ref[rows, :S5_LB] = xr
        out_ref[rows, S5_LB:] = xi
        if each is not None:
            each(r, xr, xi)
        cr, ci = (xr[0:1, :], xi[0:1, :]) if reverse else (xr[7:8, :], xi[7:8, :])
    return cr, ci


def _s5_fwd(proj, bb_band, c_band, lam_pow, s5_d, n_seq, seq_len):
    n_t = seq_len // S5_TB
    blk = 2 * S5_LB
    n_rows = n_seq * seq_len
    u_blk0 = OFF_U5 // BAND

    def body(u_ref, bb_ref, cb_ref, lp_ref, d_ref, s_ref, y_ref, ge_ref, cr, ci, buf):
        @pl.when(pl.program_id(2) == 0)
        def _():
            cr[...] = jnp.zeros_like(cr)
            ci[...] = jnp.zeros_like(ci)

        u = u_ref[...]
        buf[...] = _dg(u, bb_ref[...], 1, 0)
        lp = lp_ref[...]
        cr[...], ci[...] = _scan_block(buf, buf, lp, lp, cr[...], ci[...], False)
        s = buf[...].astype(s_ref.dtype)
        s_ref[...] = s
        y = _dg(s, cb_ref[...], 1, 1)
        y_ref[...] = y
        ge_ref[...] = jax.nn.gelu(y + d_ref[...] * u).astype(ge_ref.dtype)

    def rows(j, b, t):
        return (b * n_t + t, j)

    return pl.pallas_call(
        body, name="s5_fwd", grid=(S5_NJ, n_seq, n_t),
        in_specs=[pl.BlockSpec((S5_TB, BAND), lambda j, b, t: (b * n_t + t, u_blk0 + j)),
                  pl.BlockSpec((BAND, blk), lambda j, b, t: (j, 0)), pl.BlockSpec((BAND, blk), lambda j, b, t: (j, 0)),
                  pl.BlockSpec((S5_LOG_TB, blk), lambda j, b, t: (0, j)), pl.BlockSpec((1, BAND), lambda j, b, t: (0, j))],
        out_specs=[pl.BlockSpec((S5_TB, blk), rows), pl.BlockSpec((S5_TB, BAND), rows), pl.BlockSpec((S5_TB, BAND), rows)],
        out_shape=[jax.ShapeDtypeStruct((n_rows, S5_NJ * blk), BF16), jax.ShapeDtypeStruct((n_rows, S5_WIDTH), F32),
                   jax.ShapeDtypeStruct((n_rows, S5_WIDTH), BF16)],
        scratch_shapes=[pltpu.VMEM((1, S5_LB), F32), pltpu.VMEM((1, S5_LB), F32), pltpu.VMEM((S5_TB, blk), F32)],
        compiler_params=_cparams(("parallel", "parallel", "arbitrary")),
    )(proj, bb_band, c_band, lam_pow, s5_d)


def _s5_bwd(dyc, s, proj, du5_a, bb_band, c_band, lam_pow_conj, dproj, n_seq, seq_len):
    n_t = seq_len // S5_TB
    blk = 2 * S5_LB
    halo_rows = 16
    halo_per_blk = S5_TB // halo_rows
    u_blk0 = OFF_U5 // BAND

    def rows(j, b, t):
        return (b * n_t + (n_t - 1 - t), j)

    def u_rows(j, b, t):
        return (b * n_t + (n_t - 1 - t), u_blk0 + j)

    def halo(j, b, t):
        return (jnp.maximum((b * n_t + (n_t - 1 - t)) * halo_per_blk - 1, 0), j)

    def body(dy_ref, sq_ref, hq_ref, u_ref, dua_ref, bb_ref, cb_ref, lp_ref, _, du_ref, dl_ref, dbb_ref, dcb_ref,
             cr, ci, buf, s_ref):
        b, t = pl.program_id(1), pl.program_id(2)
        sq = sq_ref[...]
        s_ref[...] = sq.astype(F32)
        h_last = hq_ref[...].astype(F32)[halo_rows - 1:halo_rows, :]
        dy = dy_ref[...]
        buf[...] = _dg(dy, cb_ref[...], 1, 0)

        @pl.when(t == 0)
        def _():
            cr[...] = jnp.zeros_like(cr)
            ci[...] = jnp.zeros_like(ci)

        @pl.when((b == 0) & (t == 0))
        def _():
            dl_ref[...] = jnp.zeros_like(dl_ref)
            dbb_ref[...] = jnp.zeros_like(dbb_ref)
            dcb_ref[...] = jnp.zeros_like(dcb_ref)

        dcb_ref[...] += _dg(dy, sq, 0, 0)

        first_blk = t == n_t - 1
        sub = lax.broadcasted_iota(jnp.int32, (8, S5_LB), 0)
        acc = [jnp.zeros((8, S5_LB), F32), jnp.zeros((8, S5_LB), F32)]

        def each(r, gr, gi):
            rows = slice(8 * r, 8 * r + 8)
            if r == 0:
                before_r = jnp.where(first_blk, 0.0, h_last[:, :S5_LB])
                before_i = jnp.where(first_blk, 0.0, h_last[:, S5_LB:])
            else:
                before_r, before_i = s_ref[8 * r - 1:8 * r, :S5_LB], s_ref[8 * r - 1:8 * r, S5_LB:]
            sp_r = jnp.where(sub == 0, before_r, pltpu.roll(s_ref[rows, :S5_LB], 1, 0))
            sp_i = jnp.where(sub == 0, before_i, pltpu.roll(s_ref[rows, S5_LB:], 1, 0))
            acc[0] = acc[0] + gr * sp_r + gi * sp_i
            acc[1] = acc[1] + gi * sp_r - gr * sp_i

        lp = lp_ref[...]
        edge_pow = jnp.concatenate([lp[7 - i:8 - i, :] for i in range(8)], axis=0)
        cr[...], ci[...] = _scan_block(buf, buf, lp, edge_pow, cr[...], ci[...], True, each)
        g = buf[...].astype(BF16)
        du_ref[...] = (dua_ref[...] + _dg(g, bb_ref[...], 1, 1)).astype(du_ref.dtype)
        dbb_ref[...] += _dg(u_ref[...], g, 0, 0)
        dl_ref[:, :S5_LB] += jnp.sum(acc[0], axis=0, keepdims=True)
        dl_ref[:, S5_LB:] += jnp.sum(acc[1], axis=0, keepdims=True)

    band = pl.BlockSpec((BAND, blk), lambda j, b, t: (j, 0))
    return pl.pallas_call(
        body, name="s5_bwd", grid=(S5_NJ, n_seq, n_t),
        in_specs=[pl.BlockSpec((S5_TB, BAND), rows), pl.BlockSpec((S5_TB, blk), rows),
                  pl.BlockSpec((halo_rows, blk), halo), pl.BlockSpec((S5_TB, BAND), u_rows),
                  pl.BlockSpec((S5_TB, BAND), rows), band, band,
                  pl.BlockSpec((S5_LOG_TB, blk), lambda j, b, t: (0, j)), pl.BlockSpec(memory_space=pl.ANY)],
        out_specs=[pl.BlockSpec((S5_TB, BAND), u_rows), pl.BlockSpec((1, blk), lambda j, b, t: (0, j)), band, band],
        out_shape=[jax.ShapeDtypeStruct(dproj.shape, dproj.dtype), jax.ShapeDtypeStruct((1, S5_NJ * blk), F32),
                   jax.ShapeDtypeStruct(bb_band.shape, F32), jax.ShapeDtypeStruct(c_band.shape, F32)],
        input_output_aliases={8: 0},
        scratch_shapes=[pltpu.VMEM((1, S5_LB), F32), pltpu.VMEM((1, S5_LB), F32), pltpu.VMEM((S5_TB, blk), F32),
                        pltpu.VMEM((S5_TB, blk), F32)],
        compiler_params=_cparams(("parallel", "arbitrary", "arbitrary")),
    )(dyc, s, s, proj, du5_a, bb_band, c_band, lam_pow_conj, dproj)


CONV_TR = 1024
CONV_CW = 512


def _shift_down(x, halo, k):
    if k == 0:
        return x
    row8 = lax.broadcasted_iota(jnp.int32, halo.shape, 0)
    rolled = pltpu.roll(x, k, 0)
    top = jnp.where(row8 < k, pltpu.roll(halo, k, 0), rolled[:8])
    if x.shape[0] == 8:
        return top
    return jnp.concatenate([top, rolled[8:]], axis=0)


def _shift_up(x, halo, k):
    if k == 0:
        return x
    n = x.shape[0]
    row8 = lax.broadcasted_iota(jnp.int32, halo.shape, 0)
    rolled = pltpu.roll(x, n - k, 0)
    bot = jnp.where(row8 >= 8 - k, pltpu.roll(halo, 8 - k, 0), rolled[n - 8:])
    if n == 8:
        return bot
    return jnp.concatenate([rolled[:n - 8], bot], axis=0)


def _conv_pre(x, halo, w, b):
    acc = b + w[SSD_CONV - 1:SSD_CONV, :] * x
    for k in range(SSD_CONV - 1):
        acc = acc + w[k:k + 1, :] * _shift_down(x, halo, SSD_CONV - 1 - k)
    return acc


def _conv_specs(seq_len, col_off):
    lt = seq_len // CONV_TR
    cb = col_off // CONV_CW
    cur = pl.BlockSpec((CONV_TR, CONV_CW), lambda j, i: (i, j + cb))
    prev = pl.BlockSpec((8, CONV_CW), lambda j, i: (jnp.maximum(i * (CONV_TR // 8) - 1, 0), j + cb))
    return lt, cur, prev


def _conv_fwd(proj, conv_w, conv_b, n_rows, seq_len):
    lt, cur, prev = _conv_specs(seq_len, OFF_XBC)

    def body(x_ref, h_ref, w_ref, b_ref, o_ref):
        halo = jnp.where(pl.program_id(1) % lt == 0, 0.0, h_ref[...])
        o_ref[...] = jax.nn.silu(_conv_pre(x_ref[...], halo, w_ref[...], b_ref[...]))

    return pl.pallas_call(
        body, name="ssd_conv_fwd", grid=(SSD_CONV_DIM // CONV_CW, n_rows // CONV_TR),
        in_specs=[cur, prev, pl.BlockSpec((SSD_CONV, CONV_CW), lambda j, i: (0, j)),
                  pl.BlockSpec((1, CONV_CW), lambda j, i: (0, j))],
        out_specs=pl.BlockSpec((CONV_TR, CONV_CW), lambda j, i: (i, j)),
        out_shape=jax.ShapeDtypeStruct((n_rows, SSD_CONV_DIM), F32),
        compiler_params=_cparams(("parallel", "parallel")),
    )(proj, proj, conv_w, conv_b)


def _conv_bwd(name, proj, d_act, conv_w, conv_b, n_rows, seq_len, col_off, dproj):
    width = d_act.shape[1]
    lt, cur, prev = _conv_specs(seq_len, OFF_XBC + col_off)
    n_blk = n_rows // CONV_TR
    cb = (OFF_XBC + col_off) // CONV_CW
    pb = col_off // CONV_CW
    nxt = pl.BlockSpec((8, CONV_CW), lambda j, i: (jnp.minimum((i + 1) * (CONV_TR // 8), n_rows // 8 - 1), j + cb))
    d_cur = pl.BlockSpec((CONV_TR, CONV_CW), lambda j, i: (i, j))
    d_nxt = pl.BlockSpec((8, CONV_CW), lambda j, i: (jnp.minimum((i + 1) * (CONV_TR // 8), n_rows // 8 - 1), j))

    def dsilu(pre):
        sg = jax.nn.sigmoid(pre)
        return sg * (1.0 + pre * (1.0 - sg))

    def body(x_ref, hp_ref, hn_ref, d_ref, dn_ref, w_ref, b_ref, _, dx_ref, dw_ref, db_ref):
        i = pl.program_id(1)
        x, w, b = x_ref[...], w_ref[...], b_ref[...]
        halo_p = jnp.where(i % lt == 0, 0.0, hp_ref[...])
        at_end = i % lt == lt - 1
        dpre = d_ref[...] * dsilu(_conv_pre(x, halo_p, w, b))
        pre_n = _conv_pre(hn_ref[...], x[CONV_TR - 8:, :], w, b)
        dpre_n = jnp.where(at_end, 0.0, dn_ref[...] * dsilu(pre_n))
        dx = w[SSD_CONV - 1:SSD_CONV, :] * dpre
        for k in range(SSD_CONV - 1):
            dx = dx + w[k:k + 1, :] * _shift_up(dpre, dpre_n, SSD_CONV - 1 - k)
        dx_ref[...] = dx.astype(dx_ref.dtype)

        @pl.when(i == 0)
        def _():
            dw_ref[...] = jnp.zeros_like(dw_ref)
            db_ref[...] = jnp.zeros_like(db_ref)

        for k in range(SSD_CONV):
            xs = _shift_down(x, halo_p, SSD_CONV - 1 - k)
            dw_ref[k:k + 1, :] += jnp.sum(dpre * xs, axis=0, keepdims=True)
        db_ref[...] += jnp.sum(dpre, axis=0, keepdims=True)

    return pl.pallas_call(
        body, name=name, grid=(width // CONV_CW, n_blk),
        in_specs=[cur, prev, nxt, d_cur, d_nxt,
                  pl.BlockSpec((SSD_CONV, CONV_CW), lambda j, i: (0, j + pb)),
                  pl.BlockSpec((1, CONV_CW), lambda j, i: (0, j + pb)), pl.BlockSpec(memory_space=pl.ANY)],
        out_specs=[pl.BlockSpec((CONV_TR, CONV_CW), lambda j, i: (i, j + cb)),
                   pl.BlockSpec((SSD_CONV, CONV_CW), lambda j, i: (0, j)),
                   pl.BlockSpec((1, CONV_CW), lambda j, i: (0, j))],
        out_shape=[jax.ShapeDtypeStruct(dproj.shape, dproj.dtype),
                   jax.ShapeDtypeStruct((SSD_CONV, width), F32), jax.ShapeDtypeStruct((1, width), F32)],
        input_output_aliases={7: 0},
        compiler_params=_cparams(("parallel", "arbitrary")),
    )(proj, proj, proj, d_act, d_act, conv_w, conv_b, dproj)


def _split3(x):
    hi = x.astype(BF16)
    r = x - hi.astype(F32)
    mid = r.astype(BF16)
    return hi, mid, (r - mid.astype(F32)).astype(BF16)


def _sel_dot(a, b, a_is_sel):
    dn = (((1,), (0,)), ((), ()))
    if a_is_sel:
        return sum(lax.dot_general(a, t, dn, preferred_element_type=F32) for t in _split3(b))
    return sum(lax.dot_general(t, b, dn, preferred_element_type=F32) for t in _split3(a))


@jax.custom_vjp
def sel_left(sel, sel_t, x):
    return _sel_dot(sel, x, True)


@jax.custom_vjp
def sel_right(x, sel, sel_t):
    return _sel_dot(x, sel, False)


sel_left.defvjp(lambda s, st, x: (_sel_dot(s, x, True), (s, st)),
                lambda r, g: (jnp.zeros_like(r[0]), jnp.zeros_like(r[1]), _sel_dot(r[1], g, True)))
sel_right.defvjp(lambda x, s, st: (_sel_dot(x, s, False), (s, st)),
                 lambda r, g: (_sel_dot(g, r[1], False), jnp.zeros_like(r[0]), jnp.zeros_like(r[1])))


def _ssd_chunk(xs, bm, cm, dtr, st, dtb, alog, dsk, k):
    dt = jax.nn.softplus(dtr + dtb)
    acum = sel_left(k["tri"], k["tri_t"], dt * (-jnp.exp(alog)))
    dt_e = sel_right(dt, k["spread"], k["spread_t"])
    ac_e = sel_right(acum, k["spread"], k["spread_t"])
    al_e = ac_e[SSD_CHUNK - 1:SSD_CHUNK, :]
    dsk_e = sel_right(jnp.broadcast_to(dsk, (8, 128)), k["spread"], k["spread_t"])[0:1, :]
    xdt = xs * dt_e
    acum_t = acum.T
    scores = dot_nt(cm, bm)
    y = dot_nn(cm, st) * jnp.exp(ac_e) + xs * dsk_e
    for j in range(SSD_HPG):
        lmat = jnp.exp(jnp.where(k["causal"], acum[:, j:j + 1] - acum_t[j:j + 1, :], -jnp.inf))
        y = y + dot_nn(scores * lmat, jnp.where(k["head"] == j, xdt, 0.0))
    new = st * jnp.exp(al_e) + dot_tn(bm, xdt * jnp.exp(al_e - ac_e))
    return y, new


def _ssd_consts():
    r = lax.broadcasted_iota(jnp.int32, (SSD_CHUNK, SSD_CHUNK), 0)
    c = lax.broadcasted_iota(jnp.int32, (SSD_CHUNK, SSD_CHUNK), 1)
    hd = jnp.int32(SSD_HEADDIM)
    sr = lax.broadcasted_iota(jnp.int32, (128, GROUP_W), 0)
    sc = lax.div(lax.broadcasted_iota(jnp.int32, (128, GROUP_W), 1), hd)
    tr = lax.div(lax.broadcasted_iota(jnp.int32, (GROUP_W, 128), 0), hd)
    tc = lax.broadcasted_iota(jnp.int32, (GROUP_W, 128), 1)
    return {"tri": (r >= c).astype(BF16), "tri_t": (c >= r).astype(BF16), "causal": r >= c,
            "spread": (sr == sc).astype(BF16), "spread_t": (tr == tc).astype(BF16),
            "head": lax.div(lax.broadcasted_iota(jnp.int32, (SSD_CHUNK, GROUP_W), 1), hd)}


def _ssd_chunk_gated(xs, bm, cm, dtr, st, dtb, alog, dsk, z, nw, k):
    y, new = _ssd_chunk(xs, bm, cm, dtr, st, dtb, alog, dsk, k)
    yg = y * jax.nn.silu(z)
    return yg * lax.rsqrt(jnp.mean(yg * yg, axis=-1, keepdims=True) + RMS_EPS) * nw, new


def _ssd_fwd(xbc_act, proj, dtb, alog, dsk, norm_w, n_seq, seq_len):
    n_c = seq_len // SSD_CHUNK
    gps = 4
    xbc3 = xbc_act.reshape(n_seq, seq_len, SSD_CONV_DIM)
    proj3 = proj.reshape(n_seq, seq_len, PROJ_W)

    def rows(w, first=0):
        return pl.BlockSpec((n_seq, SSD_CHUNK, gps * w), lambda g, c: (0, c, first + g))

    def par(w):
        return pl.BlockSpec((1, gps * w), lambda g, c: (0, g))

    def body(xs_ref, bm_ref, cm_ref, dt_ref, dtb_ref, al_ref, dk_ref, z_ref, nw_ref, y_ref, st_ref, state):
        @pl.when(pl.program_id(1) == 0)
        def _():
            state[...] = jnp.zeros_like(state)

        consts = _ssd_consts()
        for i in range(gps):
            wide, narrow = slice(i * GROUP_W, (i + 1) * GROUP_W), slice(i * 128, (i + 1) * 128)
            for b in range(n_seq):
                prev = state[b, i]
                st_ref[b, 0, i] = prev
                y, new = _ssd_chunk_gated(xs_ref[b, :, wide], bm_ref[b, :, narrow], cm_ref[b, :, narrow],
                                          dt_ref[b, :, narrow], prev, dtb_ref[:, narrow], al_ref[:, narrow],
                                          dk_ref[:, narrow], z_ref[b, :, wide], nw_ref[:, wide], consts)
                y_ref[b, :, wide] = y.astype(y_ref.dtype)
                state[b, i] = new

    y3, states = pl.pallas_call(
        body, name="ssd_fwd", grid=(SSD_GROUPS // gps, n_c),
        in_specs=[rows(GROUP_W), rows(SSD_STATE, SSD_WIDTH // (gps * SSD_STATE)),
                  rows(SSD_STATE, (SSD_WIDTH + SSD_BC) // (gps * SSD_STATE)), rows(128, OFF_DT // (gps * 128)),
                  par(128), par(128), par(128), rows(GROUP_W, OFF_ZS // (gps * GROUP_W)), par(GROUP_W)],
        out_specs=[rows(GROUP_W),
                   pl.BlockSpec((n_seq, 1, gps, SSD_STATE, GROUP_W), lambda g, c: (0, c, g, 0, 0))],
        out_shape=[jax.ShapeDtypeStruct((n_seq, seq_len, SSD_WIDTH), BF16),
                   jax.ShapeDtypeStruct((n_seq, n_c, SSD_GROUPS, SSD_STATE, GROUP_W), F32)],
        scratch_shapes=[pltpu.VMEM((n_seq, gps, SSD_STATE, GROUP_W), F32)],
        compiler_params=_cparams(("parallel", "arbitrary")),
    )(xbc3, xbc3, xbc3, proj3, dtb, alog, dsk, proj3, norm_w)
    return y3.reshape(n_seq * seq_len, SSD_WIDTH), states


def _ssd_bwd(xbc_act, proj, states, dy, dtb, alog, dsk, norm_w, n_seq, seq_len, dproj):
    n_c = seq_len // SSD_CHUNK
    n_rows = n_seq * seq_len
    gps = 4
    xbc3 = xbc_act.reshape(n_seq, seq_len, SSD_CONV_DIM)
    proj3 = proj.reshape(n_seq, seq_len, PROJ_W)

    def rows(w, first=0):
        return pl.BlockSpec((n_seq, SSD_CHUNK, gps * w), lambda g, c: (0, n_c - 1 - c, first + g))

    def par(w):
        return pl.BlockSpec((1, gps * w), lambda g, c: (0, g))

    def body(xs_ref, bm_ref, cm_ref, dt_ref, st_ref, dy_ref, dtb_ref, al_ref, dk_ref, z_ref, nw_ref, _,
             dxs_ref, dbm_ref, dcm_ref, dz_ref, ddt_ref, ddtb_ref, dal_ref, ddk_ref, dnw_ref, dstate):
        @pl.when(pl.program_id(1) == 0)
        def _():
            dstate[...] = jnp.zeros_like(dstate)
            for ref in (ddtb_ref, dal_ref, ddk_ref, dnw_ref):
                ref[...] = jnp.zeros_like(ref)

        consts = _ssd_consts()
        chains = []
        for i in range(gps):
            wide, narrow = slice(i * GROUP_W, (i + 1) * GROUP_W), slice(i * 128, (i + 1) * 128)
            for b in range(n_seq):
                _, vjp = jax.vjp(
                    lambda *a: _ssd_chunk_gated(*a, consts),
                    xs_ref[b, :, wide], bm_ref[b, :, narrow], cm_ref[b, :, narrow], dt_ref[b, :, narrow],
                    st_ref[b, 0, i], dtb_ref[:, narrow], al_ref[:, narrow], dk_ref[:, narrow], z_ref[b, :, wide],
                    nw_ref[:, wide])
                chains.append((i, b, wide, narrow, vjp))
        for i, b, wide, narrow, vjp in chains:
            dxs, dbm, dcm, ddtr, dprev, ddtb, dal, ddk, dz, dnw = vjp((dy_ref[b, :, wide], dstate[b, i]))
            dxs_ref[b, :, wide] = dxs
            dbm_ref[b, :, narrow] = dbm
            dcm_ref[b, :, narrow] = dcm
            dz_ref[b, :, wide] = dz.astype(dz_ref.dtype)
            ddt_ref[b, :, narrow] = ddtr.astype(ddt_ref.dtype)
            ddtb_ref[:, narrow] += ddtb
            dal_ref[:, narrow] += dal
            ddk_ref[:, narrow] += ddk
            dnw_ref[:, wide] += dnw
            dstate[b, i] = dprev

    res = pl.pallas_call(
        body, name="ssd_bwd", grid=(SSD_GROUPS // gps, n_c),
        in_specs=[rows(GROUP_W), rows(SSD_STATE, SSD_WIDTH // (gps * SSD_STATE)),
                  rows(SSD_STATE, (SSD_WIDTH + SSD_BC) // (gps * SSD_STATE)), rows(128, OFF_DT // (gps * 128)),
                  pl.BlockSpec((n_seq, 1, gps, SSD_STATE, GROUP_W), lambda g, c: (0, n_c - 1 - c, g, 0, 0)),
                  rows(GROUP_W), par(128), par(128), par(128), rows(GROUP_W, OFF_ZS // (gps * GROUP_W)), par(GROUP_W),
                  pl.BlockSpec(memory_space=pl.ANY)],
        out_specs=[rows(GROUP_W), rows(SSD_STATE), rows(SSD_STATE), rows(GROUP_W, OFF_ZS // (gps * GROUP_W)), rows(128),
                   par(128), par(128), par(128), par(GROUP_W)],
        out_shape=[jax.ShapeDtypeStruct((n_seq, seq_len, SSD_WIDTH), F32), jax.ShapeDtypeStruct((n_seq, seq_len, SSD_BC), F32),
                   jax.ShapeDtypeStruct((n_seq, seq_len, SSD_BC), F32),
                   jax.ShapeDtypeStruct((n_seq, seq_len, PROJ_W), dproj.dtype),
                   jax.ShapeDtypeStruct((n_seq, seq_len, DT_W), BF16),
                   jax.ShapeDtypeStruct((1, 512), F32), jax.ShapeDtypeStruct((1, 512), F32),
                   jax.ShapeDtypeStruct((1, 512), F32), jax.ShapeDtypeStruct((1, SSD_WIDTH), F32)],
        input_output_aliases={11: 3},
        scratch_shapes=[pltpu.VMEM((n_seq, gps, SSD_STATE, GROUP_W), F32)],
        compiler_params=_cparams(("parallel", "arbitrary")),
    )(xbc3, xbc3, xbc3, proj3, states, dy.reshape(n_seq, seq_len, SSD_WIDTH), dtb, alog, dsk, proj3, norm_w,
      dproj.reshape(n_seq, seq_len, PROJ_W))
    flat = [r.reshape(n_rows, r.shape[-1]) for r in res[:5]]
    return (*flat, *res[5:])


def _pad_heads(v):
    return jnp.pad(v.reshape(SSD_GROUPS, SSD_HPG), ((0, 0), (0, 128 - SSD_HPG))).reshape(1, SSD_GROUPS * 128)


def _unpad_heads(v):
    return v.reshape(SSD_GROUPS, 128)[:, :SSD_HPG].reshape(1, SSD_HEADS)


def _state_cols(v):
    re, im = v
    lead = re.shape[:-1]
    re = re.reshape(lead + (S5_NJ, 1, S5_LB))
    im = im.reshape(lead + (S5_NJ, 1, S5_LB))
    return jnp.concatenate([re, im], axis=-2).reshape(lead + (2 * S5_N,))


def _state_uncols(v):
    lead = v.shape[:-1]
    v = v.reshape(lead + (S5_NJ, 2, S5_LB))
    return v[..., 0, :].reshape(lead + (S5_N,)), v[..., 1, :].reshape(lead + (S5_N,))


GROUPS_PER_BAND = BAND // S5_GROUP


def _band(w2_re, w2_im):
    gh = S5_GROUPS * S5_GROUP
    rg = ((jnp.arange(gh) // S5_GROUP) % GROUPS_PER_BAND)[:, None, None]
    cg = jnp.arange(GROUPS_PER_BAND)[None, :, None]
    parts = [jnp.where(rg == cg, v[:, None, :], 0.0).reshape(gh, S5_LB) for v in (w2_re, w2_im)]
    return jnp.concatenate(parts, axis=1)


def _band_take(wb):
    gh = S5_GROUPS * S5_GROUP
    w4 = wb.reshape(gh, 2, GROUPS_PER_BAND, S5_STATE)
    sel = w4[jnp.arange(gh), :, (jnp.arange(gh) // S5_GROUP) % GROUPS_PER_BAND, :]
    return sel[:, 0, :], sel[:, 1, :]


W_IN_SHARD = IN_PROJ_DIM // N_CHIPS
W_IN_SEGS = ((0, 512, OFF_U5), (512, 1024, OFF_Z5), (1024, 2560, OFF_ZS), (2560, 5120, OFF_XBC), (5144, 7192, OFF_G5))
DT_ROWS = (5120, 5144)


def _w_in_pieces():
    runs = []
    segs = list(W_IN_SEGS) + [(DT_ROWS[0] + SSD_HPG * g, DT_ROWS[0] + SSD_HPG * (g + 1), OFF_DT + 128 * g)
                              for g in range(SSD_GROUPS)]
    for lo, hi, off in segs:
        for j in range(N_CHIPS):
            s, e = max(lo, j * W_IN_SHARD), min(hi, (j + 1) * W_IN_SHARD)
            if s < e:
                runs.append((j, s - j * W_IN_SHARD, off + s - lo, e - s))
    return runs


RELAYOUT_LANES = 256


def _pad_w_in_t(a4):
    runs = _w_in_pieces()

    def body(a_ref, o_ref):
        o_ref[pl.ds(OFF_DT, DT_W), :] = jnp.zeros((DT_W, RELAYOUT_LANES), o_ref.dtype)
        for j, src, dst, n in runs:
            o_ref[pl.ds(dst, n), :] = a_ref[j, pl.ds(src, n), :]

    return pl.pallas_call(
        body, name="w_in_to_padded", grid=(D_MODEL // RELAYOUT_LANES,),
        in_specs=[pl.BlockSpec((N_CHIPS, W_IN_SHARD, RELAYOUT_LANES), lambda i: (0, 0, i))],
        out_specs=pl.BlockSpec((PROJ_W, RELAYOUT_LANES), lambda i: (0, i)),
        out_shape=jax.ShapeDtypeStruct((PROJ_W, D_MODEL), a4.dtype),
        compiler_params=_cparams(("parallel",)),
    )(a4)


def _unpad_w_in_t(wp):
    runs = _w_in_pieces()

    def body(p_ref, o_ref):
        for j, dst, src, n in runs:
            o_ref[j, pl.ds(dst, n), :] = p_ref[pl.ds(src, n), :]

    return pl.pallas_call(
        body, name="w_in_from_padded", grid=(D_MODEL // RELAYOUT_LANES,),
        in_specs=[pl.BlockSpec((PROJ_W, RELAYOUT_LANES), lambda i: (0, i))],
        out_specs=pl.BlockSpec((N_CHIPS, W_IN_SHARD, RELAYOUT_LANES), lambda i: (0, 0, i)),
        out_shape=jax.ShapeDtypeStruct((N_CHIPS, W_IN_SHARD, D_MODEL), wp.dtype),
        compiler_params=_cparams(("parallel",)),
    )(wp)


def _local_step(x, p, tgt, w):
    n_seq, seq_len, _ = x.shape
    n_rows = n_seq * seq_len
    tr = 512
    x2 = x.reshape(n_rows, D_MODEL)
    p2 = p.reshape(n_rows, PLE_DIM)
    t2 = tgt.reshape(n_rows, D_MODEL)
    row = functools.partial(_rowwise, n_rows=n_rows, tr=tr)

    w_pad_t = _pad_w_in_t(w["w_in_t"])
    norm_w = w["norm_w"].reshape(1, D_MODEL)
    ple_norm_w = w["ple_norm_w"].reshape(1, D_MODEL)
    final_w = w["final_norm_w"].reshape(1, D_MODEL)
    s5_d = w["s5_d"].reshape(1, S5_WIDTH)
    b_glu = w["s5_b_glu"].reshape(1, S5_WIDTH)
    conv_w = w["ssd_conv_w"].reshape(SSD_CONV, SSD_CONV_DIM)
    conv_b = w["ssd_conv_b"].reshape(1, SSD_CONV_DIM)
    ssd_norm_w = w["ssd_norm_w"].reshape(1, SSD_WIDTH)
    dtb, alog, dsk = (_pad_heads(w[k].reshape(1, SSD_HEADS)) for k in ("ssd_dt_bias", "ssd_a_log", "ssd_d"))

    gh = S5_GROUPS * S5_GROUP
    a_re = w["s5_a_re"].reshape(S5_GROUPS, S5_STATE)
    a_im = w["s5_a_im"].reshape(S5_GROUPS, S5_STATE)
    log_step = w["s5_log_step"].reshape(S5_GROUPS, 1)
    b_re2 = jnp.transpose(w["s5_b_re"].reshape(S5_GROUPS, S5_STATE, S5_GROUP), (0, 2, 1)).reshape(gh, S5_STATE)
    b_im2 = jnp.transpose(w["s5_b_im"].reshape(S5_GROUPS, S5_STATE, S5_GROUP), (0, 2, 1)).reshape(gh, S5_STATE)
    expand = (jnp.arange(gh)[:, None] // S5_GROUP == jnp.arange(S5_GROUPS)[None, :]).astype(F32)
    pow_re, pow_im, bb_re2, bb_im2 = _s5_params_fwd(a_re, a_im, log_step, b_re2, b_im2, expand)
    lam_pow = _state_cols((pow_re.reshape(S5_LOG_TB, S5_N), pow_im.reshape(S5_LOG_TB, S5_N)))
    lam_pow_conj = _state_cols((pow_re.reshape(S5_LOG_TB, S5_N), -pow_im.reshape(S5_LOG_TB, S5_N)))
    bb_band = _band(bb_re2, bb_im2).astype(BF16)
    c_band = _band(w["s5_c_re"].reshape(gh, S5_STATE), -w["s5_c_im"].reshape(gh, S5_STATE)).astype(BF16)

    late = w.get("_late")
    (hn,) = row("rms_in", lambda r, q: ([_rms(r[0], q[0])], []), row_ins=[(x2, 0, D_MODEL)],
                par_ins=[(norm_w, 0, D_MODEL)] + ([(late[0], 0, 128)] if late else []), row_outs=[(D_MODEL, BF16)])
    proj = _matmul("mm_proj", hn, w_pad_t, tb=True)
    s, yc, ge = _s5_fwd(proj, bb_band, c_band, lam_pow, s5_d, n_seq, seq_len)
    if late:
        w = {**w, **late[1](ge)}
    tg, y5 = _matmul("mm_s5_glu", ge, w["s5_w_glu"], epilogue=lambda c, r, q: [c, _s5_out(r[0], r[1], r[2], c, *q)],
                     epi_rows=[(yc, 0), (proj, OFF_U5), (proj, OFF_Z5)], epi_pars=[s5_d, b_glu], epi_outs=[F32, BF16])

    xbc_act = _conv_fwd(proj, conv_w, conv_b, n_rows, seq_len)
    yss, states = _ssd_fwd(xbc_act, proj, dtb, alog, dsk, ssd_norm_w, n_seq, seq_len)

    m5 = _matmul("mm_br_s5", y5, w["w_br_s5"])
    ms, merged = _matmul("mm_br_ssd", yss, w["w_br_ssd"], epilogue=lambda c, r, q: [c, _merge(r[0], r[1], r[2], c)],
                         epi_rows=[(proj, OFF_G5), (proj, OFF_GS), (m5, 0)], epi_outs=[F32, BF16])

    def resid_norm(c, r, q):
        h1_ = r[0] + c
        return [h1_, _rms(h1_, q[0])]

    h1, hp = _matmul("mm_out", merged, w["w_out"], epilogue=resid_norm, epi_rows=[(x2, 0)], epi_pars=[ple_norm_w],
                     epi_outs=[F32, BF16], full_rows=True)
    pp = _matmul("mm_ple_proj", p2, w["w_ple_proj"])

    def head_fn(pgl_, r, q):
        h1_, pp_, tgt_ = r
        loss, vjp = jax.vjp(lambda a, b, c, f: _head_loss(a, b, c, f, tgt_), h1_, pgl_, pp_, q[0])
        dh1_, dpgl_, dpp_, dfw_ = vjp(jnp.ones_like(loss))
        return [dh1_, dpgl_, dpp_], [loss, dfw_]

    dh2, dpgl, dpp, loss_acc, d_final_w = _matmul(
        "mm_ple_gate_head", hp, w["w_ple_gate"], epilogue=head_fn, epi_rows=[(h1, 0), (pp, 0), (t2, 0)],
        epi_pars=[final_w], epi_outs=[F32, BF16, BF16], epi_accs=[(1, 128), (1, D_MODEL)], full_rows=True)
    loss = loss_acc[0, 0]

    g = {}
    g["final_norm_w"] = d_final_w
    g["w_ple_gate"] = _matmul("mm_d_w_ple_gate", hp, dpgl, ta=True)
    g["w_ple_proj"] = _matmul("mm_d_w_ple_proj", p2, dpp, ta=True)

    def ple_norm_bwd(dhp_, r, q):
        h1_, dh2_ = r
        _, vjp = jax.vjp(_rms, h1_, q[0])
        dh, dw = vjp(dhp_)
        dh = dh + dh2_
        return [dh, dh], [dw]

    dh1, dh1_b, g["ple_norm_w"] = _matmul(
        "mm_d_hp", dpgl, w["w_ple_gate"], tb=True, epilogue=ple_norm_bwd, epi_rows=[(h1, 0), (dh2, 0)],
        epi_pars=[ple_norm_w], epi_outs=[F32, BF16], epi_accs=[(1, D_MODEL)], full_rows=True)
    g["w_out"] = _matmul("mm_d_w_out", merged, dh1_b, ta=True)

    dproj = lax.empty((n_rows, PROJ_W), BF16)

    def merge_bwd(dmerged, r, q):
        sg5, sgs = jax.nn.sigmoid(r[0]), jax.nn.sigmoid(r[1])
        d_gates = jnp.concatenate([dmerged * r[2] * sg5 * (1.0 - sg5), dmerged * r[3] * sgs * (1.0 - sgs)], axis=1)
        return [d_gates, dmerged * sg5, dmerged * sgs]

    dproj, dm5, dms = _matmul(
        "mm_d_merged", dh1_b, w["w_out"], tb=True, epilogue=merge_bwd,
        epi_rows=[(proj, OFF_G5), (proj, OFF_GS), (m5, 0), (ms, 0)], epi_outs=[BF16, BF16, BF16], full_rows=True,
        epi_into=(dproj, OFF_G5, 2 * D_MODEL))
    g["w_br_s5"] = _matmul("mm_d_w_br_s5", y5, dm5, ta=True)
    g["w_br_ssd"] = _matmul("mm_d_w_br_ssd", yss, dms, ta=True)
    dyss = _matmul("mm_d_yss", dms, w["w_br_ssd"], tb=True)

    def s5_out_bwd_a(dy_, r, q):
        yc_, u_, z_, t_ = r
        d_, bg_ = q
        ge_ = jax.nn.gelu(yc_ + d_ * u_)
        _, vjp = jax.vjp(lambda a, z, t, b: a * jax.nn.sigmoid(t + b) * jax.nn.silu(z), ge_, z_, t_, bg_)
        dge, dz, dt_, dbg = vjp(dy_)
        return [dz, dge, dt_], [dbg]

    dproj, dge_a, dtg, g["s5_b_glu"] = _matmul(
        "mm_d_y5", dm5, w["w_br_s5"], tb=True, epilogue=s5_out_bwd_a,
        epi_rows=[(yc, 0), (proj, OFF_U5), (proj, OFF_Z5), (tg, 0)], epi_pars=[s5_d, b_glu],
        epi_outs=[BF16, F32, BF16], epi_accs=[(1, S5_WIDTH)], full_rows=True, epi_into=(dproj, OFF_Z5, S5_WIDTH))
    g["s5_w_glu"] = _matmul("mm_d_w_glu", ge, dtg, ta=True)
    early = w["_early"](g) if "_early" in w else None

    def s5_out_bwd_b(dge_b, r, q):
        yc_, u_, da_ = r
        _, vjp = jax.vjp(lambda yc, u, d: jax.nn.gelu(yc + d * u), yc_, u_, q[0])
        dyc_, du_, dd_ = vjp(da_ + dge_b)
        return [dyc_, du_], [dd_]

    dyc, du5_a, g["s5_d"] = _matmul(
        "mm_d_ge", dtg, w["s5_w_glu"], tb=True, epilogue=s5_out_bwd_b,
        epi_rows=[(yc, 0), (proj, OFF_U5), (dge_a, 0)], epi_pars=[s5_d], epi_outs=[BF16, F32],
        epi_accs=[(1, S5_WIDTH)], full_rows=True, dep=early)
    dproj, d_lam, d_bb_band, d_c_band = _s5_bwd(dyc, s, proj, du5_a, bb_band, c_band, lam_pow_conj, dproj,
                                                n_seq, seq_len)

    d_lr, d_li = _state_uncols(d_lam)
    d_bbr, d_bbi = _band_take(d_bb_band)
    d_are, d_aim, d_ls, d_br2, d_bi2 = _s5_params_bwd(
        a_re, a_im, log_step, b_re2, b_im2, expand,
        d_lr.reshape(S5_GROUPS, S5_STATE), d_li.reshape(S5_GROUPS, S5_STATE), d_bbr, d_bbi)
    g["s5_a_re"], g["s5_a_im"], g["s5_log_step"] = d_are, d_aim, d_ls
    g["s5_b_re_ghp"], g["s5_b_im_ghp"] = d_br2, d_bi2
    d_cr, d_ci = _band_take(d_c_band)
    g["s5_c_re"], g["s5_c_im"] = d_cr, -d_ci

    dxs, dbm, dcm, dproj, ddt, d_dtb, d_alog, d_dsk, g["ssd_norm_w"] = _ssd_bwd(
        xbc_act, proj, states, dyss, dtb, alog, dsk, ssd_norm_w, n_seq, seq_len, dproj)
    (dproj,) = row("ssd_ddt", lambda r, q: ([r[0]], []), row_ins=[(ddt, 0, DT_W)], par_ins=[],
                   row_outs=[(DT_W, BF16)], into=(dproj, OFF_DT))
    g["ssd_dt_bias"], g["ssd_a_log"], g["ssd_d"] = _unpad_heads(d_dtb), _unpad_heads(d_alog), _unpad_heads(d_dsk)
    conv_dw, conv_db = [], []
    for nm, d_act, off in (("x", dxs, 0), ("b", dbm, SSD_WIDTH), ("c", dcm, SSD_WIDTH + SSD_BC)):
        dproj, dw_, db_ = _conv_bwd("ssd_conv_bwd_" + nm, proj, d_act, conv_w, conv_b, n_rows, seq_len, off, dproj)
        conv_dw.append(dw_)
        conv_db.append(db_)
    g["ssd_conv_w"] = jnp.concatenate(conv_dw, axis=1)
    g["ssd_conv_b"] = jnp.concatenate(conv_db, axis=1)

    g["w_in_t"] = _unpad_w_in_t(_matmul("mm_d_w_in", dproj, hn, ta=True))
    def norm_bwd(dhn_, r, q):
        x_, dh1_ = r
        _, vjp = jax.vjp(_rms, x_, q[0])
        dx_, dw_ = vjp(dhn_)
        return [dx_ + dh1_], [dw_]

    sent = w["_w_in_ready"](g, loss) if "_w_in_ready" in w else None
    dx, g["norm_w"] = _matmul("mm_d_hn", dproj, w_pad_t, epilogue=norm_bwd, epi_rows=[(x2, 0), (dh1, 0)],
                              epi_pars=[norm_w], epi_outs=[F32], epi_accs=[(1, D_MODEL)], full_rows=True, dep=sent)
    return loss, dx.reshape(x.shape), g


HBM = pl.BlockSpec(memory_space=pltpu.HBM)


def _chip_index(x, y):
    return 2 * x + y


def _half(shape2d, axis, which):
    h = shape2d[axis] // 2
    sl = pl.ds(pl.multiple_of(which * h, 128 if axis else 8), h)
    return (slice(None), sl) if axis else (sl, slice(None))


def _gather_chips(split, axes, whole):
    ns, nw = len(split), len(whole)
    n = ns + nw

    def body(*refs):
        ins, outs = refs[:n], refs[n:2 * n]
        ici_send, ici_recv, d2d_send, d2d_recv, local_sems = refs[2 * n:]
        x, y, c = lax.axis_index("x"), lax.axis_index("y"), lax.axis_index("c")
        me = _chip_index(x, y)
        sibling = (x, y, 1 - c)
        peers = [(1 - x, y), (x, 1 - y), (1 - x, 1 - y)]

        def half(t, which):
            return _half(split[t].shape, axes[t], which)

        copies = []
        for t in range(n):
            loc = pltpu.make_async_copy(ins[t], outs[t].at[me], local_sems.at[t])
            loc.start()
            copies.append(loc)

        def ici(t, k, slot):
            px, py = peers[k]
            if t < ns:
                src, dst = ins[t].at[half(t, c)], outs[t].at[(slot,) + half(t, c)]
            else:
                src, dst = ins[t], outs[t].at[slot]
            return pltpu.make_async_remote_copy(src_ref=src, dst_ref=dst, send_sem=ici_send.at[t, k],
                                                recv_sem=ici_recv.at[t, k], device_id=(px, py, c), device_id_type=MESH)

        def d2d(t, k, which):
            rows = outs[t].at[(_chip_index(*peers[k]),) + half(t, which)]
            return pltpu.make_async_remote_copy(src_ref=rows, dst_ref=rows, send_sem=d2d_send.at[t, k],
                                                recv_sem=d2d_recv.at[t, k], device_id=sibling, device_id_type=MESH)

        sends = []
        for t in range(n):
            for k in range(3):
                cp = ici(t, k, me)
                cp.start()
                sends.append(cp)
        for t in range(n):
            for k in range(3):
                ici(t, k, _chip_index(*peers[k])).wait_recv()
                if t < ns:
                    cp = d2d(t, k, c)
                    cp.start()
                    sends.append(cp)
        for t in range(ns):
            for k in range(3):
                d2d(t, k, 1 - c).wait_recv()
        for cp in sends:
            cp.wait_send()
        for cp in copies:
            cp.wait()

    arrays = list(split) + list(whole)
    return pl.pallas_call(
        body, name="gather_weights",
        in_specs=[HBM] * n, out_specs=[HBM] * n,
        out_shape=[jax.ShapeDtypeStruct((N_CHIPS,) + a.shape, a.dtype) for a in arrays],
        scratch_shapes=[pltpu.SemaphoreType.DMA((n, 3)), pltpu.SemaphoreType.DMA((n, 3)),
                        pltpu.SemaphoreType.DMA((ns, 3)), pltpu.SemaphoreType.DMA((ns, 3)),
                        pltpu.SemaphoreType.DMA((n,))],
    )(*arrays)


SEM = pl.BlockSpec(memory_space=pltpu.SEMAPHORE)
DATAFLOW = pltpu.SideEffectType.DATAFLOW_SIDE_EFFECTING


def _gather_start(shards, after):
    n = len(shards)

    def body(*refs):
        ins, lands = refs[:n], refs[n:2 * n]
        send_sems, recv_sems = refs[2 * n + 1], refs[2 * n + 2]
        token = refs[-1]
        x, y, c = lax.axis_index("x"), lax.axis_index("y"), lax.axis_index("c")
        me = _chip_index(x, y)
        for t in range(n):
            for k, (px, py) in enumerate([(1 - x, y), (x, 1 - y), (1 - x, 1 - y)]):
                pltpu.make_async_remote_copy(
                    src_ref=ins[t], dst_ref=lands[t].at[me], send_sem=send_sems.at[3 * t + k],
                    recv_sem=recv_sems.at[3 * t + k],
                    device_id=(px, py, c), device_id_type=MESH).start()
        token[...] = jnp.zeros_like(token)

    zones = [lax.empty((N_CHIPS,) + a.shape, a.dtype) for a in shards]
    res = pl.pallas_call(
        body, name="gather_rest_start",
        out_shape=(pltpu.SemaphoreType.DMA((3 * n,)), pltpu.SemaphoreType.DMA((3 * n,)),
                   *[pltpu.HBM(a.shape, a.dtype) for a in shards], *[pltpu.HBM(z.shape, z.dtype) for z in zones],
                   jax.ShapeDtypeStruct((8, 128), F32)),
        in_specs=[HBM] * (2 * n) + [pl.BlockSpec(memory_space=pl.ANY)],
        out_specs=(SEM, SEM, *[HBM] * (2 * n), pl.BlockSpec(memory_space=pltpu.VMEM)),
        input_output_aliases={t: 2 + t for t in range(2 * n)},
        compiler_params=pltpu.CompilerParams(has_side_effects=DATAFLOW),
    )(*[pltpu.with_memory_space_constraint(a, pltpu.HBM) for a in shards],
      *[pltpu.with_memory_space_constraint(z, pltpu.HBM) for z in zones], after)
    return res[0], res[1], list(res[2:2 + n]), list(res[2 + n:2 + 2 * n]), res[-1]


def _gather_wait(send_sems, recv_sems, thru, lands, after):
    n = len(thru)

    def body(*refs):
        ins, zones = refs[:n], refs[n:2 * n]
        s_sems, r_sems = refs[2 * n], refs[2 * n + 1]
        x, y, c = lax.axis_index("x"), lax.axis_index("y"), lax.axis_index("c")
        for t in range(n):
            for k, (px, py) in enumerate([(1 - x, y), (x, 1 - y), (1 - x, 1 - y)]):
                cp = pltpu.make_async_remote_copy(
                    src_ref=ins[t], dst_ref=zones[t].at[_chip_index(px, py)], send_sem=s_sems.at[3 * t + k],
                    recv_sem=r_sems.at[3 * t + k], device_id=(px, py, c), device_id_type=MESH)
                cp.wait_send()
                cp.wait_recv()

    res = pl.pallas_call(
        body, name="gather_rest_wait",
        out_shape=(*[pltpu.HBM(a.shape, a.dtype) for a in thru], *[pltpu.HBM(z.shape, z.dtype) for z in lands]),
        in_specs=[HBM] * (2 * n) + [SEM, SEM, pl.BlockSpec(memory_space=pl.ANY)], out_specs=[HBM] * (2 * n),
        input_output_aliases={t: t for t in range(2 * n)},
        compiler_params=pltpu.CompilerParams(has_side_effects=DATAFLOW),
    )(*thru, *lands, send_sems, recv_sems, after)
    return list(res[:n]), list(res[n:])


def _scatter_ends(t, k, n_slotted, ins, zones, x, y, me):
    px, py = [(1 - x, y), (x, 1 - y), (1 - x, 1 - y)][k]
    if t < n_slotted:
        return ins[t].at[_chip_index(px, py)], zones[t].at[k], (px, py)
    return ins[t], (zones[t].at[me], zones[t].at[_chip_index(px, py)]), (px, py)


def _scatter_start(name, slotted, whole, after):
    n_s = len(slotted)
    arrays = list(slotted) + list(whole)
    n = len(arrays)

    def body(*refs):
        ins, lands = refs[:n], refs[n:2 * n]
        send_sems, recv_sems = refs[2 * n + 1], refs[2 * n + 2]
        token = refs[-1]
        x, y, c = lax.axis_index("x"), lax.axis_index("y"), lax.axis_index("c")
        for t in range(n):
            for k in range(3):
                src, dst, (px, py) = _scatter_ends(t, k, n_s, ins, lands, x, y, _chip_index(x, y))
                pltpu.make_async_remote_copy(
                    src_ref=src, dst_ref=dst if t < n_s else dst[0], send_sem=send_sems.at[3 * t + k],
                    recv_sem=recv_sems.at[3 * t + k], device_id=(px, py, c), device_id_type=MESH).start()
        token[...] = jnp.zeros_like(token)

    zones = [lax.empty((3,) + a.shape[1:], a.dtype) for a in slotted]
    zones += [lax.empty((N_CHIPS,) + a.shape, a.dtype) for a in whole]
    slotted = arrays
    res = pl.pallas_call(
        body, name=name,
        out_shape=(pltpu.SemaphoreType.DMA((3 * n,)), pltpu.SemaphoreType.DMA((3 * n,)),
                   *[pltpu.HBM(a.shape, a.dtype) for a in slotted], *[pltpu.HBM(z.shape, z.dtype) for z in zones],
                   jax.ShapeDtypeStruct((8, 128), F32)),
        in_specs=[HBM] * (2 * n) + [pl.BlockSpec(memory_space=pl.ANY)],
        out_specs=(SEM, SEM, *[HBM] * (2 * n), pl.BlockSpec(memory_space=pltpu.VMEM)),
        input_output_aliases={t: 2 + t for t in range(2 * n)},
        compiler_params=pltpu.CompilerParams(has_side_effects=DATAFLOW),
    )(*[pltpu.with_memory_space_constraint(a, pltpu.HBM) for a in slotted],
      *[pltpu.with_memory_space_constraint(z, pltpu.HBM) for z in zones], after)
    return res[0], res[1], list(res[2:2 + n]), list(res[2 + n:2 + 2 * n]), res[-1]


def _scatter_wait(name, n_slotted, send_sems, recv_sems, thru, lands, after):
    n = len(thru)

    def body(*refs):
        ins, zones = refs[:n], refs[n:2 * n]
        s_sems, r_sems = refs[2 * n], refs[2 * n + 1]
        x, y, c = lax.axis_index("x"), lax.axis_index("y"), lax.axis_index("c")
        for t in range(n):
            for k in range(3):
                src, dst, (px, py) = _scatter_ends(t, k, n_slotted, ins, zones, x, y, _chip_index(x, y))
                cp = pltpu.make_async_remote_copy(
                    src_ref=src, dst_ref=dst if t < n_slotted else dst[1], send_sem=s_sems.at[3 * t + k],
                    recv_sem=r_sems.at[3 * t + k], device_id=(px, py, c), device_id_type=MESH)
                cp.wait_send()
                cp.wait_recv()

    res = pl.pallas_call(
        body, name=name,
        out_shape=(*[pltpu.HBM(a.shape, a.dtype) for a in thru], *[pltpu.HBM(z.shape, z.dtype) for z in lands]),
        in_specs=[HBM] * (2 * n) + [SEM, SEM, pl.BlockSpec(memory_space=pl.ANY)], out_specs=[HBM] * (2 * n),
        input_output_aliases={t: t for t in range(2 * n)},
        compiler_params=pltpu.CompilerParams(has_side_effects=DATAFLOW),
    )(*thru, *lands, send_sems, recv_sems, after)
    return list(res[:n]), list(res[n:])


def _sum_chips(name, slotted, recv, idx, dep):
    _, r, c = slotted.shape
    tr = _row_tile(r, 5 * c * 4)

    def body(idx_ref, own_ref, r_ref, _, o_ref):
        acc = own_ref[0]
        for k in range(3):
            acc = acc + r_ref[k]
        o_ref[...] = acc

    grid_spec = pltpu.PrefetchScalarGridSpec(
        num_scalar_prefetch=1, grid=(r // tr,),
        in_specs=[pl.BlockSpec((1, tr, c), lambda i, s: (s[1], i, 0)), pl.BlockSpec((3, tr, c), lambda i, s: (0, i, 0)),
                  pl.BlockSpec(memory_space=pl.ANY)],
        out_specs=pl.BlockSpec((tr, c), lambda i, s: (i, 0)))
    return pl.pallas_call(
        body, name=name, grid_spec=grid_spec, out_shape=jax.ShapeDtypeStruct((r, c), F32),
        compiler_params=_cparams(("parallel",)),
    )(idx, slotted, recv, dep)


def _half_shape(shape2d, axis):
    r, c = shape2d
    return (r, c // 2) if axis else (r // 2, c)


def _swap_halves(slotted, axes, small):
    n = len(slotted)

    def body(*refs):
        ins, sm_in = refs[:n], refs[n]
        outs, sm_out = refs[n + 1:2 * n + 1], refs[2 * n + 1]
        send_sems, recv_sems, local_sem = refs[2 * n + 2:]
        x, y, c = lax.axis_index("x"), lax.axis_index("y"), lax.axis_index("c")
        local = pltpu.make_async_copy(sm_in, sm_out.at[c], local_sem)
        local.start()
        sends = []
        for t in range(n):
            other = (slice(None),) + _half(slotted[t].shape[1:], axes[t], 1 - c)
            cp = pltpu.make_async_remote_copy(
                src_ref=ins[t].at[other], dst_ref=outs[t], send_sem=send_sems.at[t],
                recv_sem=recv_sems.at[t], device_id=(x, y, 1 - c), device_id_type=MESH)
            cp.start()
            sends.append(cp)
        cp = pltpu.make_async_remote_copy(
            src_ref=sm_in, dst_ref=sm_out.at[c], send_sem=send_sems.at[n], recv_sem=recv_sems.at[n],
            device_id=(x, y, 1 - c), device_id_type=MESH)
        cp.start()
        sends.append(cp)
        for cp in sends[:n]:
            cp.wait_recv()
        pltpu.make_async_remote_copy(
            src_ref=sm_in, dst_ref=sm_out.at[1 - c], send_sem=send_sems.at[n], recv_sem=recv_sems.at[n],
            device_id=(x, y, 1 - c), device_id_type=MESH).wait_recv()
        for cp in sends:
            cp.wait_send()
        local.wait()

    return pl.pallas_call(
        body, name="swap_halves",
        in_specs=[HBM] * (n + 1), out_specs=[HBM] * (n + 1),
        out_shape=[jax.ShapeDtypeStruct((a.shape[0],) + _half_shape(a.shape[1:], ax), a.dtype)
                   for a, ax in zip(slotted, axes, strict=True)]
        + [jax.ShapeDtypeStruct((2,) + small.shape, small.dtype)],
        scratch_shapes=[pltpu.SemaphoreType.DMA((n + 1,)), pltpu.SemaphoreType.DMA((n + 1,)), pltpu.SemaphoreType.DMA],
    )(*slotted, small)


def _tiling(r, c, f32_per_elem):
    if r % 8 == 0:
        tr = _row_tile(r, f32_per_elem * c * 4)
        return r // tr, (tr, c), lambda i: (i, 0)
    assert c % 128 == 0, (r, c)
    return c // 128, (r, 128), lambda i: (0, i)


def _pair_sum(name, slotted, other, idx, axis):
    _, r, c = slotted.shape
    hr, hc = _half_shape((r, c), axis)
    n, (tr, tc), at = _tiling(hr, hc, 13)
    if axis == 0:
        a = slotted.reshape(N_CHIPS, 2, hr, c)
        a_all = pl.BlockSpec((N_CHIPS, 1, tr, tc), lambda i, s: (0, s[0]) + at(i))
        a_own = pl.BlockSpec((1, 1, tr, tc), lambda i, s: (s[1], s[0]) + at(i))
    else:
        a, per_half = slotted, hc // tc
        a_all = pl.BlockSpec((N_CHIPS, tr, tc), lambda i, s: (0, at(i)[0], s[0] * per_half + at(i)[1]))
        a_own = pl.BlockSpec((1, tr, tc), lambda i, s: (s[1], at(i)[0], s[0] * per_half + at(i)[1]))

    def body(idx_ref, a_ref, b_ref, am_ref, bm_ref, p_ref, own_ref):
        mine, mine_own = (a_ref[:, 0], am_ref[0, 0]) if axis == 0 else (a_ref[...], am_ref[0])
        p_ref[...] = (mine + b_ref[...]).astype(p_ref.dtype)
        own_ref[...] = mine_own + bm_ref[0]

    grid_spec = pltpu.PrefetchScalarGridSpec(
        num_scalar_prefetch=1, grid=(n,),
        in_specs=[a_all, pl.BlockSpec((N_CHIPS, tr, tc), lambda i, s: (0,) + at(i)),
                  a_own, pl.BlockSpec((1, tr, tc), lambda i, s: (s[1],) + at(i))],
        out_specs=[pl.BlockSpec((N_CHIPS, tr, tc), lambda i, s: (0,) + at(i)),
                   pl.BlockSpec((tr, tc), lambda i, s: at(i))])
    return pl.pallas_call(
        body, name=name, grid_spec=grid_spec,
        out_shape=[jax.ShapeDtypeStruct((N_CHIPS, hr, hc), BF16), jax.ShapeDtypeStruct((hr, hc), F32)],
        compiler_params=_cparams(("parallel",)),
    )(idx, a, other, a, other)


def _sum_parts(name, own, recv):
    h, c = own.shape
    n, (tr, tc), at = _tiling(h, c, 4)

    def body(o_ref, r_ref, out_ref):
        acc = o_ref[...]
        for k in range(3):
            acc = acc + r_ref[k].astype(F32)
        out_ref[...] = acc

    return pl.pallas_call(
        body, name=name, grid=(n,),
        in_specs=[pl.BlockSpec((tr, tc), at), pl.BlockSpec((3, tr, tc), lambda i: (0,) + at(i))],
        out_specs=pl.BlockSpec((tr, tc), at),
        out_shape=jax.ShapeDtypeStruct((h, c), F32),
        compiler_params=_cparams(("parallel",)),
    )(own, recv)


def _all_to_all(name, small):
    def body(sm_in, sm_out, send_sems, recv_sems, local_sem):
        x, y, c = lax.axis_index("x"), lax.axis_index("y"), lax.axis_index("c")
        dev = 4 * x + 2 * y + c
        local = pltpu.make_async_copy(sm_in, sm_out.at[dev], local_sem)
        local.start()
        rel = [(fx, fy, fc) for fx in (0, 1) for fy in (0, 1) for fc in (0, 1)][1:]
        sends = []
        for k, (fx, fy, fc) in enumerate(rel):
            cp = pltpu.make_async_remote_copy(
                src_ref=sm_in, dst_ref=sm_out.at[dev], send_sem=send_sems.at[k], recv_sem=recv_sems.at[k],
                device_id=(x ^ fx, y ^ fy, c ^ fc), device_id_type=MESH)
            cp.start()
            sends.append(cp)
        for k, (fx, fy, fc) in enumerate(rel):
            src_dev = 4 * (x ^ fx) + 2 * (y ^ fy) + (c ^ fc)
            pltpu.make_async_remote_copy(
                src_ref=sm_in, dst_ref=sm_out.at[src_dev], send_sem=send_sems.at[k], recv_sem=recv_sems.at[k],
                device_id=(x ^ fx, y ^ fy, c ^ fc), device_id_type=MESH).wait_recv()
        for cp in sends:
            cp.wait_send()
        local.wait()

    return pl.pallas_call(
        body, name=name, in_specs=[HBM], out_specs=HBM,
        out_shape=jax.ShapeDtypeStruct((N_DEV,) + small.shape, small.dtype),
        scratch_shapes=[pltpu.SemaphoreType.DMA((7,)), pltpu.SemaphoreType.DMA((7,)), pltpu.SemaphoreType.DMA],
    )(small)


def _swap_sibling(name, parts):
    n = len(parts)

    def body(*refs):
        ins, outs = refs[:n], refs[n:2 * n]
        send_sems, recv_sems = refs[2 * n:]
        x, y, c = lax.axis_index("x"), lax.axis_index("y"), lax.axis_index("c")
        cps = []
        for t in range(n):
            cp = pltpu.make_async_remote_copy(
                src_ref=ins[t], dst_ref=outs[t], send_sem=send_sems.at[t], recv_sem=recv_sems.at[t],
                device_id=(x, y, 1 - c), device_id_type=MESH)
            cp.start()
            cps.append(cp)
        for cp in cps:
            cp.wait_recv()
        for cp in cps:
            cp.wait_send()

    return pl.pallas_call(
        body, name=name,
        in_specs=[HBM] * n, out_specs=[HBM] * n,
        out_shape=[jax.ShapeDtypeStruct(a.shape, a.dtype) for a in parts],
        scratch_shapes=[pltpu.SemaphoreType.DMA((n,)), pltpu.SemaphoreType.DMA((n,))],
    )(*parts)


def _sum_slots(name, a):
    k, r, c = a.shape
    tr = _row_tile(r, (k + 1) * c * 4)

    def body(a_ref, o_ref):
        acc = a_ref[0]
        for i in range(1, k):
            acc = acc + a_ref[i]
        o_ref[...] = acc

    return pl.pallas_call(
        body, name=name, grid=(r // tr,),
        in_specs=[pl.BlockSpec((k, tr, c), lambda i: (0, i, 0))],
        out_specs=pl.BlockSpec((tr, c), lambda i: (i, 0)),
        out_shape=jax.ShapeDtypeStruct((r, c), a.dtype),
        compiler_params=_cparams(("parallel",)),
    )(a)


def _adamw(name, w, m, v, g_parts):
    r, c = w.shape
    ng = len(g_parts)
    tr = _row_tile(r, (7 + ng) * c * 4)
    c1 = 1.0 - ADAM_B1 ** ADAM_STEP
    c2 = 1.0 - ADAM_B2 ** ADAM_STEP

    def body(*refs):
        w_ref, m_ref, v_ref = refs[:3]
        g_refs = refs[3:3 + ng]
        go_ref, d_ref, mo_ref, vo_ref = refs[3 + ng:]
        g = g_refs[0][...]
        for gr in g_refs[1:]:
            g = g + gr[...]
        m_new = ADAM_B1 * m_ref[...] + (1.0 - ADAM_B1) * g
        v_new = ADAM_B2 * v_ref[...] + (1.0 - ADAM_B2) * (g * g)
        go_ref[...] = g
        mo_ref[...] = m_new
        vo_ref[...] = v_new
        d_ref[...] = -ADAM_LR * ((m_new / c1) / (jnp.sqrt(v_new / c2) + ADAM_EPS) + ADAM_WD * w_ref[...])

    spec = pl.BlockSpec((tr, c), lambda i: (i, 0))
    return pl.pallas_call(
        body, name=name, grid=(r // tr,),
        in_specs=[spec] * (3 + ng), out_specs=[spec] * 4,
        out_shape=[jax.ShapeDtypeStruct((r, c), F32)] * 4,
        compiler_params=_cparams(("parallel",)),
    )(w, m, v, *g_parts)


def _adamw_halves(name, w, m, v, own, other, idx, axis):
    hr, hc = own.shape
    nb, (tr, tc), at = _tiling(hr, hc, 9)
    c1 = 1.0 - ADAM_B1 ** ADAM_STEP
    c2 = 1.0 - ADAM_B2 ** ADAM_STEP

    def body(idx_ref, w_ref, m_ref, v_ref, own_ref, oth_ref, go_ref, d_ref, mo_ref, vo_ref):
        g = jnp.where(pl.program_id(0) == idx_ref[0], own_ref[...], oth_ref[...])
        m_new = ADAM_B1 * m_ref[...] + (1.0 - ADAM_B1) * g
        v_new = ADAM_B2 * v_ref[...] + (1.0 - ADAM_B2) * (g * g)
        go_ref[...] = g
        mo_ref[...] = m_new
        vo_ref[...] = v_new
        d_ref[...] = -ADAM_LR * ((m_new / c1) / (jnp.sqrt(v_new / c2) + ADAM_EPS) + ADAM_WD * w_ref[...])

    per_half = (hc // tc) if axis else (hr // tr)
    if axis:
        full = pl.BlockSpec((tr, tc), lambda hh, i, s: (at(i)[0], hh * per_half + at(i)[1]))
    else:
        full = pl.BlockSpec((tr, tc), lambda hh, i, s: (hh * per_half + at(i)[0], at(i)[1]))
    part = pl.BlockSpec((tr, tc), lambda hh, i, s: at(i))
    grid_spec = pltpu.PrefetchScalarGridSpec(
        num_scalar_prefetch=1, grid=(2, nb), in_specs=[full, full, full, part, part], out_specs=[full] * 4)
    return pl.pallas_call(
        body, name=name, grid_spec=grid_spec, out_shape=[jax.ShapeDtypeStruct(w.shape, F32)] * 4,
        compiler_params=_cparams(("parallel", "parallel")),
    )(idx, w, m, v, own, other)


WEIGHTS = ['norm_w', 'w_in', 's5_a_re', 's5_a_im', 's5_b_re', 's5_b_im', 's5_c_re', 's5_c_im', 's5_d', 's5_log_step',
           's5_w_glu', 's5_b_glu', 'ssd_conv_w', 'ssd_conv_b', 'ssd_dt_bias', 'ssd_a_log', 'ssd_d', 'ssd_norm_w',
           'w_br_s5', 'w_br_ssd', 'w_out', 'ple_norm_w', 'w_ple_gate', 'w_ple_proj', 'final_norm_w']
SHARDED = {'w_in': ((IN_PROJ_DIM, 1024), 0), 's5_w_glu': ((512, 512), 0), 'ssd_conv_w': ((SSD_CONV, SSD_CONV_DIM), 1),
           'w_br_s5': ((512, 1024), 1), 'w_br_ssd': ((1536, 1024), 0), 'w_out': ((1024, 1024), 0),
           'w_ple_gate': ((1024, 1024), 0), 'w_ple_proj': ((256, 1024), 1)}
TRANSPOSED = ('w_in',)
SMALL = [n for n in WEIGHTS if n not in SHARDED]


def _shard_shape(name):
    (r, c), ax = SHARDED[name]
    return (r // N_CHIPS, c) if ax == 0 else (r, c // N_CHIPS)


def _half_axis(name):
    return 0 if (_shard_shape(name)[0] // 2) % 16 == 0 else 1


def _shard2d(name, a):
    r, c = _shard_shape(name)
    return a.reshape(c, r).T if name in TRANSPOSED else a.reshape(r, c)


def _unshard2d(name, a2, shape):
    return (a2.T if name in TRANSPOSED else a2).reshape(shape)


def _unslot(name, a4):
    (r, c), ax = SHARDED[name]
    if ax == 0:
        return a4.reshape(r, c)
    return jnp.transpose(a4, (1, 0, 2)).reshape(r, c)


def _slot(name, full):
    (r, c), ax = SHARDED[name]
    if ax == 0:
        return full.reshape(N_CHIPS, r // N_CHIPS, c)
    return jnp.transpose(full.reshape(r, N_CHIPS, c // N_CHIPS), (1, 0, 2))


GHP = ('s5_b_re', 's5_b_im')


def _view_shape(name):
    if name in ('s5_a_re', 's5_a_im'):
        return (S5_GROUPS, S5_STATE)
    if name in GHP + ('s5_c_re', 's5_c_im'):
        return (S5_GROUPS, S5_GROUP, S5_STATE)
    if name == 'ssd_conv_w':
        return (SSD_CONV, SSD_CONV_DIM // N_CHIPS)
    return (1, {'s5_log_step': S5_GROUPS, 'ssd_conv_b': SSD_CONV_DIM, 'ssd_norm_w': SSD_WIDTH, 's5_d': S5_WIDTH,
                's5_b_glu': S5_WIDTH, 'ssd_dt_bias': SSD_HEADS, 'ssd_a_log': SSD_HEADS, 'ssd_d': SSD_HEADS}.get(name, D_MODEL))


def _view(name, a):
    if name in GHP:
        return jnp.swapaxes(a.reshape(S5_GROUPS, S5_STATE, S5_GROUP), 1, 2)
    return a.reshape(_view_shape(name))


def _unview(name, a, shape):
    return (jnp.swapaxes(a, 1, 2) if name in GHP else a).reshape(shape)


def _adamw_small(ws, ms, vs, gs):
    n = len(ws)
    c1 = 1.0 - ADAM_B1 ** ADAM_STEP
    c2 = 1.0 - ADAM_B2 ** ADAM_STEP

    def body(*refs):
        w_r, m_r, v_r, g_r = (refs[k * n:(k + 1) * n] for k in range(4))
        d_o, m_o, v_o = (refs[k * n:(k + 1) * n] for k in range(4, 7))
        for i in range(n):
            g = g_r[i][...]
            m_new = ADAM_B1 * m_r[i][...] + (1.0 - ADAM_B1) * g
            v_new = ADAM_B2 * v_r[i][...] + (1.0 - ADAM_B2) * (g * g)
            m_o[i][...] = m_new
            v_o[i][...] = v_new
            d_o[i][...] = -ADAM_LR * ((m_new / c1) / (jnp.sqrt(v_new / c2) + ADAM_EPS) + ADAM_WD * w_r[i][...])

    return pl.pallas_call(
        body, name="adamw_small", out_shape=[jax.ShapeDtypeStruct(w.shape, F32) for w in ws] * 3,
        compiler_params=pltpu.CompilerParams(vmem_limit_bytes=VMEM_LIMIT),
    )(*ws, *ms, *vs, *gs)


def _pack_small(vals):
    flat = jnp.concatenate([v.reshape(-1).astype(F32) for v in vals])
    rows = -(-flat.shape[0] // (256 * 128)) * 256
    return jnp.pad(flat, (0, rows * 128 - flat.shape[0])).reshape(rows, 128)


def _unpack_small(packed, shapes):
    flat = packed.reshape(-1)
    out, off = [], 0
    for sh in shapes:
        n = math.prod(sh)
        out.append(flat[off:off + n].reshape(sh))
        off += n
    return out


def kernel(x, p, norm_w, w_in, s5_a_re, s5_a_im, s5_b_re, s5_b_im, s5_c_re, s5_c_im, s5_d, s5_log_step, s5_w_glu, s5_b_glu, ssd_conv_w, ssd_conv_b, ssd_dt_bias, ssd_a_log, ssd_d, ssd_norm_w, w_br_s5, w_br_ssd, w_out, ple_norm_w, w_ple_gate, w_ple_proj, final_norm_w, loss_target, m_norm_w, m_w_in, m_s5_a_re, m_s5_a_im, m_s5_b_re, m_s5_b_im, m_s5_c_re, m_s5_c_im, m_s5_d, m_s5_log_step, m_s5_w_glu, m_s5_b_glu, m_ssd_conv_w, m_ssd_conv_b, m_ssd_dt_bias, m_ssd_a_log, m_ssd_d, m_ssd_norm_w, m_w_br_s5, m_w_br_ssd, m_w_out, m_ple_norm_w, m_w_ple_gate, m_w_ple_proj, m_final_norm_w, v_norm_w, v_w_in, v_s5_a_re, v_s5_a_im, v_s5_b_re, v_s5_b_im, v_s5_c_re, v_s5_c_im, v_s5_d, v_s5_log_step, v_s5_w_glu, v_s5_b_glu, v_ssd_conv_w, v_ssd_conv_b, v_ssd_dt_bias, v_ssd_a_log, v_ssd_d, v_ssd_norm_w, v_w_br_s5, v_w_br_ssd, v_w_out, v_ple_norm_w, v_w_ple_gate, v_w_ple_proj, v_final_norm_w):
    args = locals()
    wl = {n: args[n] for n in WEIGHTS}
    ml = {n: args["m_" + n] for n in WEIGHTS}
    vl = {n: args["v_" + n] for n in WEIGHTS}
    big = [n for n in SHARDED if n != 'ssd_conv_w']
    chip = _chip_index(lax.axis_index("x"), lax.axis_index("y"))
    idx = jnp.stack([lax.axis_index("c"), chip]).astype(jnp.int32)

    axes = [_half_axis(n) for n in big]
    first = ['w_in']
    rest = [n for n in big if n not in first]
    bf_shards = {n: _shard2d(n, wl[n]).astype(BF16) for n in big}
    w_in_t, conv_w4 = _gather_chips([bf_shards[n] for n in first], [_half_axis(n) for n in first],
                                    [_shard2d('ssd_conv_w', wl['ssd_conv_w'])])
    full = {n: wl[n] for n in SMALL}
    full["w_in_t"] = w_in_t
    full['ssd_conv_w'] = _unslot('ssd_conv_w', conv_w4)
    send_sems, recv_sems, thru, lands, token = _gather_start([bf_shards[n] for n in rest], w_in_t)

    def fetch_rest(after):
        own, zones = _gather_wait(send_sems, recv_sems, thru, lands, after)
        return {n: _unslot(n, lax.dynamic_update_slice(z, o[None], (chip, 0, 0)))
                for n, o, z in zip(rest, own, zones, strict=True)}

    full["_late"] = (token, fetch_rest)

    in_flight = []

    def send_rest(g_now):
        in_flight.extend(_scatter_start("scatter_rest_start", [_slot(n, g_now[n]) for n in rest], [], g_now['s5_w_glu']))
        return in_flight[4]

    full["_early"] = send_rest

    packed = [n for n in SMALL if n != 'norm_w']
    first_axes = [_half_axis(n) for n in first]
    st = {}

    def send_first(g_now, loss_now):
        small_pack = _pack_small([loss_now] + [g_now[n + "_ghp" if n in GHP else n] for n in packed]
                                 + [g_now['ssd_conv_w']])
        swapped = _swap_halves([g_now["w_in_t"]], first_axes, small_pack)
        st["pair"] = [_pair_sum("pair_sum_w_in", g_now["w_in_t"], swapped[0], idx, first_axes[0])]
        small_chip = _sum_slots("sum_small_pair", swapped[-1])
        half_rows = small_chip.shape[0] // 2
        my_half = lax.dynamic_slice(small_chip, (lax.axis_index("c") * half_rows, 0), (half_rows, small_chip.shape[1]))
        st["flight"] = _scatter_start("scatter_first_start", [pb for pb, _ in st["pair"]], [my_half], small_chip)
        return st["flight"][4]

    full["_w_in_ready"] = send_first
    loss, grad_x, g = _local_step(x, p[0], loss_target, full)
    pair = st["pair"]
    s_sems, r_sems, thru, lands, tok = st["flight"]
    small_shapes = [(1, 1)] + [_view_shape(n) for n in packed] + [(SSD_CONV, SSD_CONV_DIM)]
    norm_all = _all_to_all("norm_w_all_to_all", g['norm_w'].reshape(8, 128))
    norm_g = _sum_slots("sum_norm_w", norm_all).reshape(1, D_MODEL)

    out_g, out_d, out_m, out_v = {}, {}, {}, {}
    own_slots, zones = _scatter_wait("scatter_rest_wait", len(rest), *in_flight[:4], norm_g)
    chip_sums = [_sum_chips("sum_chips_" + n, a, z, idx, tok) for n, a, z in zip(rest, own_slots, zones, strict=True)]
    sib_sums = _swap_sibling("swap_sibling_rest", chip_sums)
    for n, own, sib in zip(rest, chip_sums, sib_sums, strict=True):
        res = _adamw("adamw_" + n, _shard2d(n, wl[n]), _shard2d(n, ml[n]), _shard2d(n, vl[n]), [own, sib])
        out_g[n], out_d[n], out_m[n], out_v[n] = (_unshard2d(n, r, wl[n].shape) for r in res)

    sent, got = _scatter_wait("scatter_first_wait", len(first), s_sems, r_sems, thru, lands, out_d[rest[-1]])
    small_zone = lax.dynamic_update_slice(got[-1], sent[-1][None], (chip, 0, 0))
    small_half = _sum_slots("sum_small_chips", small_zone)
    halves = [_sum_parts("sum_chips_" + n, own, r) for n, (_, own), r in zip(first, pair, got[:-1], strict=True)]
    swapped2 = _swap_sibling("swap_sibling_first", halves + [small_half])
    other_halves = swapped2[:-1]
    c_is_0 = lax.axis_index("c") == 0
    small_sum = jnp.concatenate([jnp.where(c_is_0, small_half, swapped2[-1]),
                                 jnp.where(c_is_0, swapped2[-1], small_half)], axis=0)
    for n, own, oth, ax in zip(first, halves, other_halves, first_axes, strict=True):
        res = _adamw_halves("adamw_" + n, _shard2d(n, wl[n]), _shard2d(n, ml[n]), _shard2d(n, vl[n]), own, oth, idx, ax)
        out_g[n], out_d[n], out_m[n], out_v[n] = (_unshard2d(n, r, wl[n].shape) for r in res)
    sm = _unpack_small(small_sum, small_shapes)
    loss_total = sm[0].reshape(())
    conv_g = lax.dynamic_slice(sm[-1], (0, chip * (SSD_CONV_DIM // N_CHIPS)), (SSD_CONV, SSD_CONV_DIM // N_CHIPS))
    names = SMALL + ['ssd_conv_w']
    by_name = {**dict(zip(packed, sm[1:-1], strict=True)), 'norm_w': norm_g, 'ssd_conv_w': conv_g}
    grads = [by_name[n] for n in names]
    res = _adamw_small([_view(n, wl[n]) for n in names], [_view(n, ml[n]) for n in names],
                       [_view(n, vl[n]) for n in names], grads)
    for i, n in enumerate(names):
        out_g[n] = _unview(n, grads[i], wl[n].shape)
        out_d[n], out_m[n], out_v[n] = (_unview(n, res[k * len(names) + i], wl[n].shape) for k in range(3))

    return (loss_total, grad_x, *[out_g[n] for n in WEIGHTS], *[out_d[n] for n in WEIGHTS],
            *[out_m[n] for n in WEIGHTS], *[out_v[n] for n in WEIGHTS])
```
